```python
import jax, jax.numpy as jnp
from jax import lax
import numpy as np

D_MODEL = 1024
BATCH = 8
SEQ = 2048
DEPTH = 2

CHUNK = 64
HEAD_DIM = 64
EPS = 1e-6
NEG_INF = -1e30

A_HEADS = 8
A_KV_HEADS = 2
A_WINDOW = 128
A_PREV_CHUNKS = A_WINDOW // CHUNK
A_BAND_CHUNKS = A_PREV_CHUNKS + 1

B_HEADS = 8
B_BLOCK = 128
FORGET_BIAS_INIT = 3.0

C_HEADS = 8
C_PREV_CHUNKS = 8
C_BAND_CHUNKS = C_PREV_CHUNKS + 1
REL_CLIP = 128
N_REL = 2 * REL_CLIP + 1

N_BRANCH = 3
BRANCH_WIDTH = 8 * HEAD_DIM
FFN_HIDDEN = ((-(-8 * D_MODEL // 3)) + 255) // 256 * 256

IN_SPLIT_SIZES = (
    A_HEADS * HEAD_DIM, A_KV_HEADS * HEAD_DIM, A_KV_HEADS * HEAD_DIM,
    B_HEADS * HEAD_DIM, B_HEADS * HEAD_DIM, B_HEADS * HEAD_DIM, B_HEADS,
    C_HEADS * HEAD_DIM, C_HEADS * HEAD_DIM, C_HEADS * HEAD_DIM,
    N_BRANCH * D_MODEL,
)
N_IN_COLS = sum(IN_SPLIT_SIZES)

kernel_name = "chunk_causal_hybrid_swa_fox_relpos_adaln"


def rms_norm(x, g):
    xf = x.astype(jnp.float32)
    y = xf * lax.rsqrt(jnp.mean(xf * xf, axis=-1, keepdims=True) + EPS)
    return (y * g.astype(jnp.float32)).astype(x.dtype)


def modulate(h, shift, scale):
    return h * (1.0 + scale[:, None, :]) + shift[:, None, :]


def alibi_slopes(n_heads):
    return jnp.exp2(-8.0 * jnp.arange(1, n_heads + 1, dtype=jnp.float32) / n_heads)


def sliding_window_sink_attention(q, k, v, sinks):
    b, s, _, dh = q.shape
    nc = s // CHUNK
    g = A_HEADS // A_KV_HEADS
    band_len = A_BAND_CHUNKS * CHUNK
    qc = q.reshape(b, nc, CHUNK, A_KV_HEADS, g, dh)

    def band(t):
        tp = jnp.pad(t, ((0, 0), (A_PREV_CHUNKS * CHUNK, 0), (0, 0), (0, 0)))
        tp = tp.reshape(b, nc + A_PREV_CHUNKS, CHUNK, A_KV_HEADS, dh)
        return jnp.concatenate([tp[:, j:j + nc] for j in range(A_BAND_CHUNKS)], axis=2)

    kb, vb = band(k), band(v)
    scores = jnp.einsum('bnqkgd,bnskd->bnkgqs', qc, kb).astype(jnp.float32) * (dh ** -0.5)
    qi = jnp.arange(CHUNK)
    si = jnp.arange(band_len)
    dist = A_PREV_CHUNKS * CHUNK + qi[:, None] - si[None, :]
    alibi = -alibi_slopes(A_HEADS).reshape(A_KV_HEADS, g, 1, 1) * jnp.abs(dist).astype(jnp.float32)
    key_chunk = jnp.arange(nc)[:, None] - A_PREV_CHUNKS + si[None, :] // CHUNK
    valid = (key_chunk >= 0)[None, :, None, None, None, :]
    scores = jnp.where(valid, scores + alibi, NEG_INF)
    sink_col = jnp.broadcast_to(sinks.astype(jnp.float32).reshape(1, 1, A_KV_HEADS, g, 1, 1),
                                scores.shape[:-1] + (1,))
    probs = jax.nn.softmax(jnp.concatenate([scores, sink_col], axis=-1), axis=-1)[..., :-1]
    out = jnp.einsum('bnkgqs,bnskd->bnqkgd', probs.astype(v.dtype), vb)
    return out.reshape(b, s, A_HEADS * dh)


def forgetting_attention(q, k, v, f_logit):
    b, s, h, dh = q.shape
    log_f = jax.nn.log_sigmoid(f_logit.astype(jnp.float32))
    cum = lax.cumsum(log_f, axis=1).transpose(0, 2, 1)
    kpos = jnp.arange(s)
    scale = dh ** -0.5

    def block(i):
        start = i * B_BLOCK
        qb = lax.dynamic_slice_in_dim(q, start, B_BLOCK, axis=1)
        cq = lax.dynamic_slice_in_dim(cum, start, B_BLOCK, axis=2)
        sc = jnp.einsum('bqhd,bshd->bhqs', qb, k).astype(jnp.float32) * scale
        sc = sc + cq[..., :, None] - cum[:, :, None, :]
        qpos = start + jnp.arange(B_BLOCK)
        sc = jnp.where(kpos[None, :] <= qpos[:, None], sc, NEG_INF)
        p = jax.nn.softmax(sc, axis=-1)
        return jnp.einsum('bhqs,bshd->bqhd', p.astype(v.dtype), v)

    out = lax.map(block, jnp.arange(s // B_BLOCK))
    return out.transpose(1, 0, 2, 3, 4).reshape(b, s, h * dh)


def chunked_relpos_attention(q, k, v, rel_table):
    b, s, h, dh = q.shape
    nc = s // CHUNK
    band_len = C_BAND_CHUNKS * CHUNK
    pad = C_PREV_CHUNKS * CHUNK
    kp = jnp.pad(k, ((0, 0), (pad, 0), (0, 0), (0, 0)))
    vp = jnp.pad(v, ((0, 0), (pad, 0), (0, 0), (0, 0)))
    qi = jnp.arange(CHUNK)
    si = jnp.arange(band_len)
    dist = pad + qi[:, None] - si[None, :]
    rel_idx = jnp.clip(dist, -REL_CLIP, REL_CLIP) + REL_CLIP
    bias = rel_table[:, rel_idx].astype(jnp.float32)
    scale = dh ** -0.5

    def chunk(n):
        qc = lax.dynamic_slice_in_dim(q, n * CHUNK, CHUNK, axis=1)
        kc = lax.dynamic_slice_in_dim(kp, n * CHUNK, band_len, axis=1)
        vc = lax.dynamic_slice_in_dim(vp, n * CHUNK, band_len, axis=1)
        sc = jnp.einsum('bqhd,bshd->bhqs', qc, kc).astype(jnp.float32) * scale + bias
        valid = (n * CHUNK - pad + si) >= 0
        sc = jnp.where(valid[None, None, None, :], sc, NEG_INF)
        p = jax.nn.softmax(sc, axis=-1)
        return jnp.einsum('bhqs,bshd->bqhd', p.astype(vc.dtype), vc)

    out = lax.map(chunk, jnp.arange(nc))
    return out.transpose(1, 0, 2, 3, 4).reshape(b, s, h * dh)


def hybrid_mixer(h, w_in, b_forget, sinks, rel_table, w_branch, w_out):
    b, s, _ = h.shape
    proj = jnp.einsum('bsd,de->bse', h, w_in)
    split_points = [int(p) for p in np.cumsum(IN_SPLIT_SIZES)[:-1]]
    qa, ka, va, qb, kb, vb, fb, qc, kc, vc, gates = jnp.split(proj, split_points, axis=-1)
    heads = lambda t, n: t.reshape(b, s, n, HEAD_DIM)
    o_a = sliding_window_sink_attention(heads(qa, A_HEADS), heads(ka, A_KV_HEADS),
                                        heads(va, A_KV_HEADS), sinks)
    o_b = forgetting_attention(heads(qb, B_HEADS), heads(kb, B_HEADS), heads(vb, B_HEADS),
                               fb + b_forget)
    o_c = chunked_relpos_attention(heads(qc, C_HEADS), heads(kc, C_HEADS), heads(vc, C_HEADS),
                                   rel_table)
    branches = jnp.stack([o_a, o_b, o_c], axis=2)
    y = jnp.einsum('bskw,kwd->bskd', branches, w_branch)
    g = jax.nn.sigmoid(gates.reshape(b, s, N_BRANCH, D_MODEL))
    merged = jnp.sum(g * y, axis=2)
    return jnp.einsum('bsd,de->bse', merged, w_out)


def swiglu(h, w_ffn_in, w_ffn_out):
    u = jnp.einsum('bsd,df->bsf', h, w_ffn_in)
    gate, up = jnp.split(u, 2, axis=-1)
    return jnp.einsum('bsf,fd->bsd', jax.nn.silu(gate) * up, w_ffn_out)


def _fwd_setup_inputs(seed: int = 0) -> dict:
    key = jax.random.key(seed)
    ks = jax.random.split(key, 16)
    f32 = jnp.float32
    nrm = lambda k, shape, sd: jax.random.normal(k, shape, f32) * sd
    return {
        "x": nrm(ks[0], (BATCH, SEQ, D_MODEL), 1.0),
        "c": nrm(ks[1], (BATCH, D_MODEL), 1.0),
        "norm_mix_g": 1.0 + nrm(ks[2], (DEPTH, D_MODEL), 0.02),
        "norm_ffn_g": 1.0 + nrm(ks[3], (DEPTH, D_MODEL), 0.02),
        "w_ada": nrm(ks[4], (DEPTH, D_MODEL, 6 * D_MODEL), 0.5 * D_MODEL ** -0.5),
        "b_ada": nrm(ks[5], (DEPTH, 6 * D_MODEL), 0.02),
        "w_in": nrm(ks[6], (DEPTH, D_MODEL, N_IN_COLS), D_MODEL ** -0.5),
        "b_forget": FORGET_BIAS_INIT + nrm(ks[7], (DEPTH, B_HEADS), 0.5),
        "sinks": nrm(ks[8], (DEPTH, A_HEADS), 0.5),
        "rel_bias": nrm(ks[9], (DEPTH, C_HEADS, N_REL), 0.1),
        "w_branch": nrm(ks[10], (DEPTH, N_BRANCH, BRANCH_WIDTH, D_MODEL), BRANCH_WIDTH ** -0.5),
        "w_out": nrm(ks[11], (DEPTH, D_MODEL, D_MODEL), D_MODEL ** -0.5),
        "w_ffn_in": nrm(ks[12], (DEPTH, D_MODEL, 2 * FFN_HIDDEN), D_MODEL ** -0.5),
        "w_ffn_out": nrm(ks[13], (DEPTH, FFN_HIDDEN, D_MODEL), FFN_HIDDEN ** -0.5),
        "final_norm_g": 1.0 + nrm(ks[14], (D_MODEL,), 0.02),
    }


def _fwd_reference(x, c, norm_mix_g, norm_ffn_g, w_ada, b_ada, w_in, b_forget, sinks, rel_bias,
              w_branch, w_out, w_ffn_in, w_ffn_out, final_norm_g):
    cond = jax.nn.silu(c)
    for l in range(DEPTH):
        mod = jnp.einsum('bd,de->be', cond, w_ada[l]) + b_ada[l]
        sh_m, sc_m, g_m, sh_f, sc_f, g_f = jnp.split(mod, 6, axis=-1)
        h = modulate(rms_norm(x, norm_mix_g[l]), sh_m, sc_m)
        x = x + g_m[:, None, :] * hybrid_mixer(h, w_in[l], b_forget[l], sinks[l], rel_bias[l],
                                               w_branch[l], w_out[l])
        h = modulate(rms_norm(x, norm_ffn_g[l]), sh_f, sc_f)
        x = x + g_f[:, None, :] * swiglu(h, w_ffn_in[l], w_ffn_out[l])
    return rms_norm(x, final_norm_g)


import jax as _jax
import jax.numpy as _jnp

TWIN_FORMAT = 'train_step'
FWD_PARAMS = ['x', 'c', 'norm_mix_g', 'norm_ffn_g', 'w_ada', 'b_ada', 'w_in', 'b_forget', 'sinks', 'rel_bias', 'w_branch', 'w_out', 'w_ffn_in', 'w_ffn_out', 'final_norm_g']
TWIN_WEIGHTS = ['norm_mix_g', 'norm_ffn_g', 'w_ada', 'b_ada', 'w_in', 'b_forget', 'sinks', 'rel_bias', 'w_branch', 'w_out', 'w_ffn_in', 'w_ffn_out', 'final_norm_g']
TWIN_DIFF_INPUT = 'x'
TWIN_INPUTS = ['x', 'c', 'norm_mix_g', 'norm_ffn_g', 'w_ada', 'b_ada', 'w_in', 'b_forget', 'sinks', 'rel_bias', 'w_branch', 'w_out', 'w_ffn_in', 'w_ffn_out', 'final_norm_g', 'loss_target', 'm_norm_mix_g', 'm_norm_ffn_g', 'm_w_ada', 'm_b_ada', 'm_w_in', 'm_b_forget', 'm_sinks', 'm_rel_bias', 'm_w_branch', 'm_w_out', 'm_w_ffn_in', 'm_w_ffn_out', 'm_final_norm_g', 'v_norm_mix_g', 'v_norm_ffn_g', 'v_w_ada', 'v_b_ada', 'v_w_in', 'v_b_forget', 'v_sinks', 'v_rel_bias', 'v_w_branch', 'v_w_out', 'v_w_ffn_in', 'v_w_ffn_out', 'v_final_norm_g']
TWIN_OUTPUTS = ['loss', 'grad_x', 'grad_norm_mix_g', 'grad_norm_ffn_g', 'grad_w_ada', 'grad_b_ada', 'grad_w_in', 'grad_b_forget', 'grad_sinks', 'grad_rel_bias', 'grad_w_branch', 'grad_w_out', 'grad_w_ffn_in', 'grad_w_ffn_out', 'grad_final_norm_g', 'delta_norm_mix_g', 'delta_norm_ffn_g', 'delta_w_ada', 'delta_b_ada', 'delta_w_in', 'delta_b_forget', 'delta_sinks', 'delta_rel_bias', 'delta_w_branch', 'delta_w_out', 'delta_w_ffn_in', 'delta_w_ffn_out', 'delta_final_norm_g', 'new_m_norm_mix_g', 'new_m_norm_ffn_g', 'new_m_w_ada', 'new_m_b_ada', 'new_m_w_in', 'new_m_b_forget', 'new_m_sinks', 'new_m_rel_bias', 'new_m_w_branch', 'new_m_w_out', 'new_m_w_ffn_in', 'new_m_w_ffn_out', 'new_m_final_norm_g', 'new_v_norm_mix_g', 'new_v_norm_ffn_g', 'new_v_w_ada', 'new_v_b_ada', 'new_v_w_in', 'new_v_b_forget', 'new_v_sinks', 'new_v_rel_bias', 'new_v_w_branch', 'new_v_w_out', 'new_v_w_ffn_in', 'new_v_w_ffn_out', 'new_v_final_norm_g']
TWIN_LEAF_KINDS = {'loss': 'loss', 'grad_x': 'grad_x', 'grad_norm_mix_g': 'grad_w', 'grad_norm_ffn_g': 'grad_w', 'grad_w_ada': 'grad_w', 'grad_b_ada': 'grad_w', 'grad_w_in': 'grad_w', 'grad_b_forget': 'grad_w', 'grad_sinks': 'grad_w', 'grad_rel_bias': 'grad_w', 'grad_w_branch': 'grad_w', 'grad_w_out': 'grad_w', 'grad_w_ffn_in': 'grad_w', 'grad_w_ffn_out': 'grad_w', 'grad_final_norm_g': 'grad_w', 'delta_norm_mix_g': 'delta_w', 'delta_norm_ffn_g': 'delta_w', 'delta_w_ada': 'delta_w', 'delta_b_ada': 'delta_w', 'delta_w_in': 'delta_w', 'delta_b_forget': 'delta_w', 'delta_sinks': 'delta_w', 'delta_rel_bias': 'delta_w', 'delta_w_branch': 'delta_w', 'delta_w_out': 'delta_w', 'delta_w_ffn_in': 'delta_w', 'delta_w_ffn_out': 'delta_w', 'delta_final_norm_g': 'delta_w', 'new_m_norm_mix_g': 'new_m', 'new_m_norm_ffn_g': 'new_m', 'new_m_w_ada': 'new_m', 'new_m_b_ada': 'new_m', 'new_m_w_in': 'new_m', 'new_m_b_forget': 'new_m', 'new_m_sinks': 'new_m', 'new_m_rel_bias': 'new_m', 'new_m_w_branch': 'new_m', 'new_m_w_out': 'new_m', 'new_m_w_ffn_in': 'new_m', 'new_m_w_ffn_out': 'new_m', 'new_m_final_norm_g': 'new_m', 'new_v_norm_mix_g': 'new_v', 'new_v_norm_ffn_g': 'new_v', 'new_v_w_ada': 'new_v', 'new_v_b_ada': 'new_v', 'new_v_w_in': 'new_v', 'new_v_b_forget': 'new_v', 'new_v_sinks': 'new_v', 'new_v_rel_bias': 'new_v', 'new_v_w_branch': 'new_v', 'new_v_w_out': 'new_v', 'new_v_w_ffn_in': 'new_v', 'new_v_w_ffn_out': 'new_v', 'new_v_final_norm_g': 'new_v'}


def _forward(args):
    return _fwd_reference(*[args[k] for k in FWD_PARAMS])


def _output_shape():
    out = _jax.eval_shape(lambda: _forward(_fwd_setup_inputs(0)))
    return out.shape, out.dtype

N_MICROBATCH = 1
ADAM_LR = 0.001
ADAM_B1 = 0.9
ADAM_B2 = 0.999
ADAM_EPS = 1e-08
ADAM_WD = 0.01
ADAM_STEP = 10
PER_EXAMPLE_BATCH_AXIS = {'x': 0, 'c': 0, 'loss_target': 0}
SHARED_INPUTS = []
_WEIGHT_DTYPES = {'norm_mix_g': _jnp.float32, 'norm_ffn_g': _jnp.float32, 'w_ada': _jnp.float32, 'b_ada': _jnp.float32, 'w_in': _jnp.float32, 'b_forget': _jnp.float32, 'sinks': _jnp.float32, 'rel_bias': _jnp.float32, 'w_branch': _jnp.float32, 'w_out': _jnp.float32, 'w_ffn_in': _jnp.float32, 'w_ffn_out': _jnp.float32, 'final_norm_g': _jnp.float32}
MOMENT_SCALE = {'norm_mix_g': 1.890725e-02, 'norm_ffn_g': 3.813381e-02, 'w_ada': 3.143873e-02, 'b_ada': 5.140445e-02, 'w_in': 8.628820e-03, 'b_forget': 5.975715e-02, 'sinks': 6.373468e-03, 'rel_bias': 2.364863e-03, 'w_branch': 9.266292e-03, 'w_out': 1.590017e-02, 'w_ffn_in': 1.670360e-02, 'w_ffn_out': 2.727315e-02, 'final_norm_g': 1.601454e+01}


def _to_microbatches(a, axis):
    t = _jnp.moveaxis(a, axis, 0)
    t = t.reshape((N_MICROBATCH, t.shape[0] // N_MICROBATCH) + t.shape[1:])
    return _jnp.moveaxis(t, 1, axis + 1)


def setup_inputs(seed: int = 0) -> dict:
    inp = _fwd_setup_inputs(seed)
    key = _jax.random.fold_in(_jax.random.key(seed), 7919)
    shape, _ = _output_shape()
    out = dict(inp)
    out["loss_target"] = _jax.random.normal(_jax.random.fold_in(key, 0), shape, _jnp.float32)
    for i, name in enumerate(TWIN_WEIGHTS):
        w = inp[name].astype(_jnp.float32)
        if MOMENT_SCALE is None:
            s = _jnp.sqrt(_jnp.mean(_jnp.square(w)) + 1e-30)
        else:
            s = MOMENT_SCALE[name]
        km, kv = _jax.random.split(_jax.random.fold_in(key, i + 1))
        out[name] = w
        out["m_" + name] = s * _jax.random.normal(km, w.shape, _jnp.float32)
        out["v_" + name] = (s * s) * _jax.random.uniform(kv, w.shape, _jnp.float32, 0.5, 1.5)
    if N_MICROBATCH > 1:
        for name, axis in PER_EXAMPLE_BATCH_AXIS.items():
            out[name] = _to_microbatches(out[name], axis)
    return {'x': out['x'], 'c': out['c'], 'norm_mix_g': out['norm_mix_g'], 'norm_ffn_g': out['norm_ffn_g'], 'w_ada': out['w_ada'], 'b_ada': out['b_ada'], 'w_in': out['w_in'], 'b_forget': out['b_forget'], 'sinks': out['sinks'], 'rel_bias': out['rel_bias'], 'w_branch': out['w_branch'], 'w_out': out['w_out'], 'w_ffn_in': out['w_ffn_in'], 'w_ffn_out': out['w_ffn_out'], 'final_norm_g': out['final_norm_g'], 'loss_target': out['loss_target'], 'm_norm_mix_g': out['m_norm_mix_g'], 'm_norm_ffn_g': out['m_norm_ffn_g'], 'm_w_ada': out['m_w_ada'], 'm_b_ada': out['m_b_ada'], 'm_w_in': out['m_w_in'], 'm_b_forget': out['m_b_forget'], 'm_sinks': out['m_sinks'], 'm_rel_bias': out['m_rel_bias'], 'm_w_branch': out['m_w_branch'], 'm_w_out': out['m_w_out'], 'm_w_ffn_in': out['m_w_ffn_in'], 'm_w_ffn_out': out['m_w_ffn_out'], 'm_final_norm_g': out['m_final_norm_g'], 'v_norm_mix_g': out['v_norm_mix_g'], 'v_norm_ffn_g': out['v_norm_ffn_g'], 'v_w_ada': out['v_w_ada'], 'v_b_ada': out['v_b_ada'], 'v_w_in': out['v_w_in'], 'v_b_forget': out['v_b_forget'], 'v_sinks': out['v_sinks'], 'v_rel_bias': out['v_rel_bias'], 'v_w_branch': out['v_w_branch'], 'v_w_out': out['v_w_out'], 'v_w_ffn_in': out['v_w_ffn_in'], 'v_w_ffn_out': out['v_w_ffn_out'], 'v_final_norm_g': out['v_final_norm_g']}


def _loss(weights, diff, rest, loss_target):
    with _jax.named_scope("forward"):
        args = {**rest, TWIN_DIFF_INPUT: diff, **{k: w.astype(_WEIGHT_DTYPES[k]) for k, w in weights.items()}}
        y = _forward(args)
    with _jax.named_scope("loss_head"):
        err = _jnp.square(y.astype(_jnp.float32) - loss_target)
        return 0.5 * _jnp.sum(_jnp.mean(err, axis=-1)) if err.ndim else 0.5 * err


def _adamw(w, g, m, v):
    m = ADAM_B1 * m + (1.0 - ADAM_B1) * g
    v = ADAM_B2 * v + (1.0 - ADAM_B2) * _jnp.square(g)
    m_hat = m / (1.0 - ADAM_B1 ** ADAM_STEP)
    v_hat = v / (1.0 - ADAM_B2 ** ADAM_STEP)
    delta = -ADAM_LR * (m_hat / (_jnp.sqrt(v_hat) + ADAM_EPS) + ADAM_WD * w)
    return delta, m, v


def reference(x, c, norm_mix_g, norm_ffn_g, w_ada, b_ada, w_in, b_forget, sinks, rel_bias, w_branch, w_out, w_ffn_in, w_ffn_out, final_norm_g, loss_target, m_norm_mix_g, m_norm_ffn_g, m_w_ada, m_b_ada, m_w_in, m_b_forget, m_sinks, m_rel_bias, m_w_branch, m_w_out, m_w_ffn_in, m_w_ffn_out, m_final_norm_g, v_norm_mix_g, v_norm_ffn_g, v_w_ada, v_b_ada, v_w_in, v_b_forget, v_sinks, v_rel_bias, v_w_branch, v_w_out, v_w_ffn_in, v_w_ffn_out, v_final_norm_g):
    given = dict(x=x, c=c, norm_mix_g=norm_mix_g, norm_ffn_g=norm_ffn_g, w_ada=w_ada, b_ada=b_ada, w_in=w_in, b_forget=b_forget, sinks=sinks, rel_bias=rel_bias, w_branch=w_branch, w_out=w_out, w_ffn_in=w_ffn_in, w_ffn_out=w_ffn_out, final_norm_g=final_norm_g, loss_target=loss_target, m_norm_mix_g=m_norm_mix_g, m_norm_ffn_g=m_norm_ffn_g, m_w_ada=m_w_ada, m_b_ada=m_b_ada, m_w_in=m_w_in, m_b_forget=m_b_forget, m_sinks=m_sinks, m_rel_bias=m_rel_bias, m_w_branch=m_w_branch, m_w_out=m_w_out, m_w_ffn_in=m_w_ffn_in, m_w_ffn_out=m_w_ffn_out, m_final_norm_g=m_final_norm_g, v_norm_mix_g=v_norm_mix_g, v_norm_ffn_g=v_norm_ffn_g, v_w_ada=v_w_ada, v_b_ada=v_b_ada, v_w_in=v_w_in, v_b_forget=v_b_forget, v_sinks=v_sinks, v_rel_bias=v_rel_bias, v_w_branch=v_w_branch, v_w_out=v_w_out, v_w_ffn_in=v_w_ffn_in, v_w_ffn_out=v_w_ffn_out, v_final_norm_g=v_final_norm_g)
    weights = {n: given[n] for n in TWIN_WEIGHTS}
    shared = {n: given[n] for n in SHARED_INPUTS}
    per_example = {n: given[n] for n in ['x', 'c']}
    grad_fn = _jax.value_and_grad(_loss, argnums=(0, 1))

    def one_microbatch(ex, loss_target):
        ex = dict(ex)
        diff = ex.pop(TWIN_DIFF_INPUT)
        return grad_fn(weights, diff, {**shared, **ex}, loss_target)

    if N_MICROBATCH == 1:
        loss, (grad_w, grad_x) = one_microbatch(per_example, given["loss_target"])
    else:
        def body(carry, xs):
            loss_sum, grad_sum = carry
            l_k, (gw_k, gx_k) = one_microbatch(xs[0], xs[1])
            with _jax.named_scope("update"):
                return (loss_sum + l_k, _jax.tree.map(_jnp.add, grad_sum, gw_k)), gx_k

        init = (_jnp.zeros((), _jnp.float32), _jax.tree.map(_jnp.zeros_like, weights))
        (loss, grad_w), grad_x = _jax.lax.scan(body, init, (per_example, given["loss_target"]))
    with _jax.named_scope("update"):
        delta_w, new_m, new_v = {}, {}, {}
        for n in TWIN_WEIGHTS:
            delta_w[n], new_m[n], new_v[n] = _adamw(weights[n], grad_w[n], given["m_" + n], given["v_" + n])
    return (loss, grad_x, *[grad_w[n] for n in TWIN_WEIGHTS], *[delta_w[n] for n in TWIN_WEIGHTS],
            *[new_m[n] for n in TWIN_WEIGHTS], *[new_v[n] for n in TWIN_WEIGHTS])
```

```python
import functools

import jax
import jax.numpy as jnp
from jax import lax
from jax.experimental import pallas as pl
from jax.experimental.pallas import tpu as pltpu

F32 = jnp.float32
BF16 = jnp.bfloat16
NEG_INF = -1e30
EPS = 1e-6
N_DEV = 8
BLK = 128
VMEM_LIMIT_BYTES = 56 * 1024 * 1024

D_MODEL = 1024
N_QKV = 3840
N_GATES = 3072
N_MAIN = N_QKV + N_GATES
N_FORGET = 8
N_IN = N_MAIN + N_FORGET
N_INR = N_MAIN + BLK
F_COL = 2304
FFN_HIDDEN = 2816

ADAM_LR, ADAM_B1, ADAM_B2, ADAM_EPS, ADAM_WD, ADAM_STEP = 0.001, 0.9, 0.999, 1e-08, 0.01, 10

NN = (((1,), (0,)), ((), ()))
NT = (((1,), (1,)), ((), ()))
TN = (((0,), (0,)), ((), ()))
HIGHEST = lax.Precision.HIGHEST

ATTN_COLS = {"a": (0, 4, 5), "b": (6, 10, 14), "c": (18, 22, 26)}


def _params():
    return pltpu.CompilerParams(vmem_limit_bytes=VMEM_LIMIT_BYTES)


def _tile(n, target):
    best = None
    t = 128
    while t <= min(n, target):
        if n % t == 0:
            best = t
        t += 128
    return best if best is not None else n


def _row_tile(n, target):
    t = min(n, target)
    while n % t:
        t -= 8
    return t


def _matmul(a, b, mode, out_dtype, name, *, n=None, a_off=0, b_off=0, m=None, tm=512, tn=768, tk=1408):
    if mode == "nn":
        M, K = a.shape if m is None else (m, a.shape[1])
        N = b.shape[1] if n is None else n
    elif mode == "nt":
        M, K = a.shape
        N = b.shape[0] if n is None else n
    else:
        K = a.shape[0]
        M = a.shape[1] if m is None else m
        N = b.shape[1] if n is None else n
    tm = _tile(M, tm) if M % 128 == 0 else M
    tn = _tile(N, tn)
    tk = _tile(K, tk)
    nk = K // tk
    dims = {"nn": NN, "nt": NT, "tn": TN}[mode]
    if mode == "nn":
        a_spec = pl.BlockSpec((tm, tk), lambda i, j, k: (i + a_off, k))
        b_spec = pl.BlockSpec((tk, tn), lambda i, j, k: (k, j + b_off))
    elif mode == "nt":
        a_spec = pl.BlockSpec((tm, tk), lambda i, j, k: (i + a_off, k))
        b_spec = pl.BlockSpec((tn, tk), lambda i, j, k: (j + b_off, k))
    else:
        a_spec = pl.BlockSpec((tk, tm), lambda i, j, k: (k, i + a_off))
        b_spec = pl.BlockSpec((tk, tn), lambda i, j, k: (k, j + b_off))

    def body(a_ref, b_ref, o_ref, acc_ref):
        k = pl.program_id(2)
        part = lax.dot_general(a_ref[...], b_ref[...], dims, preferred_element_type=F32)
        if nk == 1:
            o_ref[...] = part.astype(o_ref.dtype)
        else:
            @pl.when(k == 0)
            def _():
                acc_ref[...] = part

            @pl.when(k > 0)
            def _():
                acc_ref[...] += part

            @pl.when(k == nk - 1)
            def _():
                o_ref[...] = acc_ref[...].astype(o_ref.dtype)

    return pl.pallas_call(
        body, name=name,
        out_shape=jax.ShapeDtypeStruct((M, N), out_dtype),
        grid=(M // tm, N // tn, nk),
        in_specs=[a_spec, b_spec],
        out_specs=pl.BlockSpec((tm, tn), lambda i, j, k: (i, j)),
        scratch_shapes=[pltpu.VMEM((tm, tn) if nk > 1 else (8, 128), F32)],
        compiler_params=_params(),
    )(a, b)


def _matmul_resid(a, b, resid, gate, name, *, tm=512, tn=512, tk=1408):
    M, K = a.shape
    N = b.shape[1]
    tm, tn, tk = _tile(M, tm), _tile(N, tn), _tile(K, tk)
    nk = K // tk

    def body(a_ref, b_ref, r_ref, g_ref, o_ref, s_ref, acc_ref):
        k = pl.program_id(2)
        part = jnp.dot(a_ref[...], b_ref[...], preferred_element_type=F32)

        def finish(acc):
            o_ref[...] = r_ref[...] + g_ref[...] * acc
            s_ref[...] = acc.astype(BF16)

        if nk == 1:
            finish(part)
        else:
            @pl.when(k == 0)
            def _():
                acc_ref[...] = part

            @pl.when(k > 0)
            def _():
                acc_ref[...] += part

            @pl.when(k == nk - 1)
            def _():
                finish(acc_ref[...])

    return pl.pallas_call(
        body, name=name,
        out_shape=(jax.ShapeDtypeStruct((M, N), F32), jax.ShapeDtypeStruct((M, N), BF16)),
        grid=(M // tm, N // tn, nk),
        in_specs=[pl.BlockSpec((tm, tk), lambda i, j, k: (i, k)),
                  pl.BlockSpec((tk, tn), lambda i, j, k: (k, j)),
                  pl.BlockSpec((tm, tn), lambda i, j, k: (i, j)),
                  pl.BlockSpec((1, tn), lambda i, j, k: (0, j))],
        out_specs=(pl.BlockSpec((tm, tn), lambda i, j, k: (i, j)),
                   pl.BlockSpec((tm, tn), lambda i, j, k: (i, j))),
        scratch_shapes=[pltpu.VMEM((tm, tn) if nk > 1 else (8, 128), F32)],
        compiler_params=_params(),
    )(a, b, resid, gate)


def _norm_mod_fwd(x, g, shift, scale, name):
    S, D = x.shape
    ts = _row_tile(S, 256)

    def body(x_ref, g_ref, sh_ref, sc_ref, h_ref):
        xv = x_ref[...]
        rstd = lax.rsqrt(jnp.mean(xv * xv, axis=-1, keepdims=True) + EPS)
        y = xv * rstd * g_ref[...]
        h_ref[...] = (y * (1.0 + sc_ref[...]) + sh_ref[...]).astype(BF16)

    row = pl.BlockSpec((1, D), lambda i: (0, 0))
    return pl.pallas_call(
        body, name=name, out_shape=jax.ShapeDtypeStruct((S, D), BF16), grid=(S // ts,),
        in_specs=[pl.BlockSpec((ts, D), lambda i: (i, 0)), row, row, row],
        out_specs=pl.BlockSpec((ts, D), lambda i: (i, 0)),
        compiler_params=_params(),
    )(x, g, shift, scale)


def _norm_mod_bwd(x, dh, dres, g, scale, name):
    S, D = x.shape
    ts = _row_tile(S, 256)

    def body(x_ref, dh_ref, dr_ref, g_ref, sc_ref, dx_ref, dsh_ref, dsc_ref, dg_ref):
        i = pl.program_id(0)
        xv, dhv, gv = x_ref[...], dh_ref[...], g_ref[...]
        rstd = lax.rsqrt(jnp.mean(xv * xv, axis=-1, keepdims=True) + EPS)
        xhat = xv * rstd
        dn = dhv * (1.0 + sc_ref[...])
        dxhat = dn * gv
        proj = jnp.mean(dxhat * xhat, axis=-1, keepdims=True)
        dx_ref[...] = dr_ref[...] + rstd * (dxhat - xhat * proj)
        dsh = jnp.sum(dhv, axis=0, keepdims=True)
        dsc = jnp.sum(dhv * (xhat * gv), axis=0, keepdims=True)
        dg = jnp.sum(dn * xhat, axis=0, keepdims=True)

        @pl.when(i == 0)
        def _():
            dsh_ref[...] = dsh
            dsc_ref[...] = dsc
            dg_ref[...] = dg

        @pl.when(i > 0)
        def _():
            dsh_ref[...] += dsh
            dsc_ref[...] += dsc
            dg_ref[...] += dg

    tile = pl.BlockSpec((ts, D), lambda i: (i, 0))
    row = pl.BlockSpec((1, D), lambda i: (0, 0))
    vec = jax.ShapeDtypeStruct((1, D), F32)
    return pl.pallas_call(
        body, name=name, out_shape=(jax.ShapeDtypeStruct((S, D), F32), vec, vec, vec), grid=(S // ts,),
        in_specs=[tile, tile, tile, row, row], out_specs=(tile, row, row, row),
        compiler_params=_params(),
    )(x, dh, dres, g, scale)


def _gate_bwd(dx, f, gate, name):
    S, D = dx.shape
    ts = _row_tile(S, 256)

    def body(dx_ref, f_ref, g_ref, df_ref, dg_ref):
        i = pl.program_id(0)
        dxv = dx_ref[...]
        df_ref[...] = (dxv * g_ref[...]).astype(BF16)
        dg = jnp.sum(dxv * f_ref[...].astype(F32), axis=0, keepdims=True)

        @pl.when(i == 0)
        def _():
            dg_ref[...] = dg

        @pl.when(i > 0)
        def _():
            dg_ref[...] += dg

    tile = pl.BlockSpec((ts, D), lambda i: (i, 0))
    row = pl.BlockSpec((1, D), lambda i: (0, 0))
    return pl.pallas_call(
        body, name=name,
        out_shape=(jax.ShapeDtypeStruct((S, D), BF16), jax.ShapeDtypeStruct((1, D), F32)), grid=(S // ts,),
        in_specs=[tile, tile, row], out_specs=(tile, row),
        compiler_params=_params(),
    )(dx, f, gate)


def _ffn_in_fwd(h, w, name, *, tm=512, tn=256):
    S, D = h.shape
    F = w.shape[1] // 2
    tm, tn = _tile(S, tm), _tile(F, tn)
    nj = F // tn

    def body(h_ref, wg_ref, wu_ref, o_ref):
        hv = h_ref[...]
        ug = jnp.dot(hv, wg_ref[...], preferred_element_type=F32)
        uu = jnp.dot(hv, wu_ref[...], preferred_element_type=F32)
        o_ref[...] = (ug * jax.nn.sigmoid(ug) * uu).astype(BF16)

    return pl.pallas_call(
        body, name=name, out_shape=jax.ShapeDtypeStruct((S, F), BF16), grid=(S // tm, nj),
        in_specs=[pl.BlockSpec((tm, D), lambda i, j: (i, 0)),
                  pl.BlockSpec((D, tn), lambda i, j: (0, j)),
                  pl.BlockSpec((D, tn), lambda i, j: (0, j + nj))],
        out_specs=pl.BlockSpec((tm, tn), lambda i, j: (i, j)),
        compiler_params=_params(),
    )(h, w, w)


def _ffn_mid_bwd(h, df, w_in, w_out, name, *, tm=512, tn=256):
    S, D = h.shape
    F = w_in.shape[1] // 2
    tm, tn = _tile(S, tm), _tile(F, tn)
    nj = F // tn

    def body(h_ref, df_ref, wg_ref, wu_ref, wo_ref, dg_ref, du_ref):
        hv = h_ref[...]
        ug = jnp.dot(hv, wg_ref[...], preferred_element_type=F32)
        uu = jnp.dot(hv, wu_ref[...], preferred_element_type=F32)
        dact = lax.dot_general(df_ref[...], wo_ref[...], NT, preferred_element_type=F32)
        sig = jax.nn.sigmoid(ug)
        dg_ref[...] = (dact * uu * (sig * (1.0 + ug * (1.0 - sig)))).astype(BF16)
        du_ref[...] = (dact * (ug * sig)).astype(BF16)

    out = jax.ShapeDtypeStruct((S, F), BF16)
    return pl.pallas_call(
        body, name=name, out_shape=(out, out), grid=(S // tm, nj),
        in_specs=[pl.BlockSpec((tm, D), lambda i, j: (i, 0)),
                  pl.BlockSpec((tm, D), lambda i, j: (i, 0)),
                  pl.BlockSpec((D, tn), lambda i, j: (0, j)),
                  pl.BlockSpec((D, tn), lambda i, j: (0, j + nj)),
                  pl.BlockSpec((tn, D), lambda i, j: (j, 0))],
        out_specs=(pl.BlockSpec((tm, tn), lambda i, j: (i, j)), pl.BlockSpec((tm, tn), lambda i, j: (i, j))),
        compiler_params=_params(),
    )(h, df, w_in, w_in, w_out)


def _merge_fwd(o_a, o_b, o_c, gates, w_branch, name, *, tm=256):
    S, W = o_a.shape
    D = w_branch.shape[2]
    tm = _row_tile(S, tm)

    def body(oa_ref, ob_ref, oc_ref, g_ref, w_ref, m_ref):
        acc = None
        for k, o_ref in enumerate((oa_ref, ob_ref, oc_ref)):
            y = jnp.dot(o_ref[...], w_ref[k], preferred_element_type=F32)
            t = jax.nn.sigmoid(g_ref[:, k * D:(k + 1) * D]) * y
            acc = t if acc is None else acc + t
        m_ref[...] = acc.astype(BF16)

    o_spec = pl.BlockSpec((tm, W), lambda i: (i, 0))
    return pl.pallas_call(
        body, name=name, out_shape=jax.ShapeDtypeStruct((S, D), BF16), grid=(S // tm,),
        in_specs=[o_spec, o_spec, o_spec, pl.BlockSpec((tm, 3 * D), lambda i: (i, 0)),
                  pl.BlockSpec((3, W, D), lambda i: (0, 0, 0))],
        out_specs=pl.BlockSpec((tm, D), lambda i: (i, 0)),
        compiler_params=_params(),
    )(o_a, o_b, o_c, gates, w_branch)


def _merge_bwd(dmerged, o_a, o_b, o_c, gates, w_branch, name, *, tm=256):
    S, W = o_a.shape
    D = w_branch.shape[2]
    tm = _row_tile(S, tm)

    def body(dm_ref, oa_ref, ob_ref, oc_ref, g_ref, w_ref, dg_ref, dy_ref, doa_ref, dob_ref, doc_ref):
        dm = dm_ref[...]
        for k, (o_ref, do_ref) in enumerate(((oa_ref, doa_ref), (ob_ref, dob_ref), (oc_ref, doc_ref))):
            wk = w_ref[k]
            y = jnp.dot(o_ref[...], wk, preferred_element_type=F32)
            g = jax.nn.sigmoid(g_ref[:, k * D:(k + 1) * D])
            dy = (dm * g).astype(BF16)
            dy_ref[:, k * D:(k + 1) * D] = dy
            dg_ref[:, k * D:(k + 1) * D] = (dm * y * (g * (1.0 - g))).astype(BF16)
            do_ref[...] = lax.dot_general(dy, wk, NT, preferred_element_type=F32).astype(BF16)

    o_spec = pl.BlockSpec((tm, W), lambda i: (i, 0))
    wide = pl.BlockSpec((tm, 3 * D), lambda i: (i, 0))
    o_out = jax.ShapeDtypeStruct((S, W), BF16)
    wide_out = jax.ShapeDtypeStruct((S, 3 * D), BF16)
    return pl.pallas_call(
        body, name=name, out_shape=(wide_out, wide_out, o_out, o_out, o_out), grid=(S // tm,),
        in_specs=[pl.BlockSpec((tm, D), lambda i: (i, 0)), o_spec, o_spec, o_spec, wide,
                  pl.BlockSpec((3, W, D), lambda i: (0, 0, 0))],
        out_specs=(wide, wide, o_spec, o_spec, o_spec),
        compiler_params=_params(),
    )(dmerged, o_a, o_b, o_c, gates, w_branch)


def _attn_back(variant, nb):
    return {"a": 1, "b": nb, "c": 4}[variant]


def _attn_logits(variant, s, t_abs, s_abs, half, off, d_blk, aux):
    if variant == "a":
        slope = aux["slope"]
        s = s + (-slope) * jnp.abs(t_abs - s_abs).astype(F32)
        qc, kc = t_abs >> 6, s_abs >> 6
        valid = (kc <= qc) & (kc >= qc - 2)
    elif variant == "b":
        s = s + aux["cq"] - aux["ck_ref"][half:half + 1, pl.ds(off, BLK)]
        valid = s_abs <= t_abs
    else:
        s = s + aux["bias_ref"][half, d_blk]
        qc, kc = t_abs >> 6, s_abs >> 6
        valid = (kc <= qc) & (kc >= qc - 8)
    return jnp.where(valid, s, NEG_INF), valid


def _half_select(x, hmask, swap):
    xh = jnp.where(hmask, x, 0.0)
    if swap is not None:
        xh = jnp.where(swap, pltpu.roll(xh, 64, 1), xh)
    return xh


def _attn_fwd(variant, qkv, name, *, sinks=None, slopes=None, cq=None, ck=None, bias=None):
    S = qkv.shape[0]
    nb = S // BLK
    qb, kb, vb = ATTN_COLS[variant]
    shared_kv = variant == "a"
    back = _attn_back(variant, nb)

    def body(*refs):
        if variant == "a":
            q_ref, k_ref, v_ref, sink_ref, slope_ref, o_ref, lse_ref = refs
        elif variant == "b":
            q_ref, k_ref, v_ref, cq_ref, ck_ref, o_ref, lse_ref = refs
        else:
            q_ref, k_ref, v_ref, bias_ref, o_ref, lse_ref = refs
        p, i = pl.program_id(0), pl.program_id(1)
        lane = lax.broadcasted_iota(jnp.int32, (BLK, BLK), 1)
        t_abs = i * BLK + lax.broadcasted_iota(jnp.int32, (BLK, BLK), 0)
        q2 = q_ref[...].astype(F32) * 0.125
        lo = jnp.maximum(i - back, 0)
        outs = []
        for half in (0, 1):
            hmask = (lane >= 64) if half else (lane < 64)
            swap = ((p // 2) != half) if shared_kv else None
            qh = _half_select(q2, hmask, swap).astype(BF16)
            aux = {}
            if variant == "a":
                head = 2 * p + half
                aux["slope"] = slope_ref[head]
                m0 = jnp.full((BLK, 1), sink_ref[head], F32)
                l0 = jnp.ones((BLK, 1), F32)
            else:
                m0 = jnp.full((BLK, 1), NEG_INF, F32)
                l0 = jnp.zeros((BLK, 1), F32)
                if variant == "b":
                    aux["cq"] = cq_ref[:, half:half + 1]
                    aux["ck_ref"] = ck_ref
                else:
                    aux["bias_ref"] = bias_ref

            def step(j, carry, half=half, qh=qh, aux=aux):
                m, l, acc = carry
                off = pl.multiple_of(j * BLK, BLK)
                k2 = k_ref[pl.ds(off, BLK), :]
                v2 = v_ref[pl.ds(off, BLK), :]
                s = lax.dot_general(qh, k2, NT, preferred_element_type=F32)
                s_abs = j * BLK + lane
                s, _ = _attn_logits(variant, s, t_abs, s_abs, half, off, i - j, aux)
                m_new = jnp.maximum(m, jnp.max(s, axis=1, keepdims=True))
                alpha = jnp.exp(m - m_new)
                pe = jnp.exp(s - m_new)
                l = alpha * l + jnp.sum(pe, axis=1, keepdims=True)
                acc = alpha * acc + jnp.dot(pe.astype(BF16), v2, preferred_element_type=F32)
                return m_new, l, acc

            m, l, acc = lax.fori_loop(lo, i + 1, step, (m0, l0, jnp.zeros((BLK, BLK), F32)))
            out = acc / l
            if shared_kv:
                out = jnp.where(swap, pltpu.roll(out, 64, 1), out)
            outs.append(out)
            lse_ref[:, half:half + 1] = m + jnp.log(l)
        o_ref[...] = jnp.where(lane < 64, outs[0], outs[1]).astype(BF16)

    kv_col = (lambda p, i: (0, kb)) if shared_kv else (lambda p, i: (0, kb + p))
    vv_col = (lambda p, i: (0, vb)) if shared_kv else (lambda p, i: (0, vb + p))
    in_specs = [pl.BlockSpec((BLK, BLK), lambda p, i: (i, qb + p)),
                pl.BlockSpec((S, BLK), kv_col), pl.BlockSpec((S, BLK), vv_col)]
    args = [qkv, qkv, qkv]
    if variant == "a":
        in_specs += [pl.BlockSpec(memory_space=pltpu.SMEM), pl.BlockSpec(memory_space=pltpu.SMEM)]
        args += [sinks, slopes]
    elif variant == "b":
        in_specs += [pl.BlockSpec((None, BLK, 2), lambda p, i: (p, i, 0)),
                     pl.BlockSpec((None, 2, S), lambda p, i: (p, 0, 0))]
        args += [cq, ck]
    else:
        in_specs += [pl.BlockSpec((2, 5, BLK, BLK), lambda p, i: (p, 0, 0, 0))]
        args += [bias]
    return pl.pallas_call(
        body, name=name,
        out_shape=(jax.ShapeDtypeStruct((S, 512), BF16), jax.ShapeDtypeStruct((4, S, 2), F32)),
        grid=(4, nb), in_specs=in_specs,
        out_specs=(pl.BlockSpec((BLK, BLK), lambda p, i: (i, p)),
                   pl.BlockSpec((None, BLK, 2), lambda p, i: (p, i, 0))),
        compiler_params=_params(),
    )(*args)


def _attn_bwd(variant, qkv, o, do, lse, name, *, sinks=None, slopes=None, cq=None, ck=None, bias=None):
    S = qkv.shape[0]
    nb = S // BLK
    qb, kb, vb = ATTN_COLS[variant]
    shared_kv = variant == "a"
    back = _attn_back(variant, nb)

    def body(*refs):
        if variant == "a":
            (q_ref, k_ref, v_ref, o_ref, do_ref, lse_ref, sink_ref, slope_ref,
             dq_ref, dk_ref, dv_ref, ex_ref) = refs
        elif variant == "b":
            (q_ref, k_ref, v_ref, o_ref, do_ref, lse_ref, cq_ref, ck_ref,
             dq_ref, dk_ref, dv_ref, ex_ref, dcq_ref) = refs
        else:
            (q_ref, k_ref, v_ref, o_ref, do_ref, lse_ref, bias_ref,
             dq_ref, dk_ref, dv_ref, ex_ref) = refs
        p, j = pl.program_id(0), pl.program_id(1)
        lane = lax.broadcasted_iota(jnp.int32, (BLK, BLK), 1)
        row = lax.broadcasted_iota(jnp.int32, (BLK, BLK), 0)
        s_abs = j * BLK + lane
        off_k = pl.multiple_of(j * BLK, BLK)
        k2 = k_ref[...]
        v2 = v_ref[...]
        k2f = k2.astype(F32) * 0.125
        hmasks = [(lane < 64), (lane >= 64)]
        swaps = [((p // 2) != half) if shared_kv else None for half in (0, 1)]
        if shared_kv:
            kv_lane = (lane >> 6) == (p // 2)
            k_for_dq = [jnp.where(kv_lane, k2f, 0.0).astype(BF16)] * 2
        else:
            k_for_dq = [jnp.where(hmasks[half], k2f, 0.0).astype(BF16) for half in (0, 1)]

        @pl.when(j == 0)
        def _():
            dq_ref[...] = jnp.zeros_like(dq_ref)
            if variant == "b":
                dcq_ref[...] = jnp.zeros_like(dcq_ref)
            else:
                ex_ref[...] = jnp.zeros_like(ex_ref)

        def step(i, carry):
            dk_acc, dv_acc, dck0, dck1 = carry
            off = pl.multiple_of(i * BLK, BLK)
            t_abs = i * BLK + row
            q2 = q_ref[pl.ds(off, BLK), :].astype(F32) * 0.125
            do2 = do_ref[pl.ds(off, BLK), :].astype(F32)
            o2 = o_ref[pl.ds(off, BLK), :].astype(F32)
            dq_new = None
            dcks = [dck0, dck1]
            for half in (0, 1):
                hmask, swap = hmasks[half], swaps[half]
                qh = _half_select(q2, hmask, swap).astype(BF16)
                doh = _half_select(do2, hmask, swap).astype(BF16)
                delta = jnp.sum(jnp.where(hmask, do2 * o2, 0.0), axis=1, keepdims=True)
                aux = {}
                if variant == "a":
                    aux["slope"] = slope_ref[2 * p + half]
                elif variant == "b":
                    aux["cq"] = cq_ref[pl.ds(off, BLK), half:half + 1]
                    aux["ck_ref"] = ck_ref
                else:
                    aux["bias_ref"] = bias_ref
                s = lax.dot_general(qh, k2, NT, preferred_element_type=F32)
                s, valid = _attn_logits(variant, s, t_abs, s_abs, half, off_k, i - j, aux)
                pr = jnp.where(valid, jnp.exp(s - lse_ref[pl.ds(off, BLK), half:half + 1]), 0.0)
                dp = lax.dot_general(doh, v2, NT, preferred_element_type=F32)
                ds = pr * (dp - delta)
                ds16 = ds.astype(BF16)
                dv_acc = dv_acc + lax.dot_general(pr.astype(BF16), doh, TN, preferred_element_type=F32)
                dk_acc = dk_acc + lax.dot_general(ds16, qh, TN, preferred_element_type=F32)
                dqh = jnp.dot(ds16, k_for_dq[half], preferred_element_type=F32)
                if shared_kv:
                    dqh = jnp.where(swap, pltpu.roll(dqh, 64, 1), dqh)
                dq_new = dqh if dq_new is None else dq_new + dqh
                if variant == "b":
                    dcks[half] = dcks[half] - jnp.sum(ds, axis=0, keepdims=True)
                    dcq_ref[pl.ds(off, BLK), half:half + 1] += jnp.sum(ds, axis=1, keepdims=True)
                elif variant == "c":
                    ex_ref[half, i - j] += ds
            dq_ref[pl.ds(off, BLK), :] += dq_new
            return dk_acc, dv_acc, dcks[0], dcks[1]

        hi = jnp.minimum(j + back, nb - 1)
        zero_t = jnp.zeros((BLK, BLK), F32)
        zero_r = jnp.zeros((1, BLK), F32)
        dk_acc, dv_acc, dck0, dck1 = lax.fori_loop(j, hi + 1, step, (zero_t, zero_t, zero_r, zero_r))

        if shared_kv:
            @pl.when(p == 0)
            def _():
                dk_ref[pl.ds(off_k, BLK), :] = dk_acc
                dv_ref[pl.ds(off_k, BLK), :] = dv_acc

            @pl.when(p > 0)
            def _():
                dk_ref[pl.ds(off_k, BLK), :] += dk_acc
                dv_ref[pl.ds(off_k, BLK), :] += dv_acc

            do2 = do_ref[pl.ds(off_k, BLK), :].astype(F32)
            o2 = o_ref[pl.ds(off_k, BLK), :].astype(F32)
            for half in (0, 1):
                delta = jnp.sum(jnp.where(hmasks[half], do2 * o2, 0.0), axis=1, keepdims=True)
                p_sink = jnp.exp(sink_ref[2 * p + half] - lse_ref[pl.ds(off_k, BLK), half:half + 1])
                ex_ref[half:half + 1, :] += -jnp.sum(p_sink * delta, axis=0, keepdims=True)
        else:
            dk_ref[pl.ds(off_k, BLK), :] = dk_acc
            dv_ref[pl.ds(off_k, BLK), :] = dv_acc
            if variant == "b":
                ex_ref[0:1, :] = dck0
                ex_ref[1:2, :] = dck1

    col = lambda c0: (lambda p, j: (0, c0 + p))
    kv_blk = (lambda c0: (lambda p, j: (j, c0))) if shared_kv else (lambda c0: (lambda p, j: (j, c0 + p)))
    pair = lambda p, j: (0, p)
    in_specs = [pl.BlockSpec((S, BLK), col(qb)),
                pl.BlockSpec((BLK, BLK), kv_blk(kb)), pl.BlockSpec((BLK, BLK), kv_blk(vb)),
                pl.BlockSpec((S, BLK), pair), pl.BlockSpec((S, BLK), pair),
                pl.BlockSpec((None, S, 2), lambda p, j: (p, 0, 0))]
    args = [qkv, qkv, qkv, o, do, lse]
    kv_width = BLK if shared_kv else 512
    kv_out = pl.BlockSpec((S, BLK), (lambda p, j: (0, 0)) if shared_kv else pair)
    out_shape = [jax.ShapeDtypeStruct((S, 512), F32), jax.ShapeDtypeStruct((S, kv_width), F32),
                 jax.ShapeDtypeStruct((S, kv_width), F32)]
    out_specs = [pl.BlockSpec((S, BLK), pair), kv_out, kv_out]
    if variant == "a":
        in_specs += [pl.BlockSpec(memory_space=pltpu.SMEM), pl.BlockSpec(memory_space=pltpu.SMEM)]
        args += [sinks, slopes]
        out_shape.append(jax.ShapeDtypeStruct((4, 8, BLK), F32))
        out_specs.append(pl.BlockSpec((None, 8, BLK), lambda p, j: (p, 0, 0)))
    elif variant == "b":
        in_specs += [pl.BlockSpec((None, S, 2), lambda p, j: (p, 0, 0)),
                     pl.BlockSpec((None, 2, S), lambda p, j: (p, 0, 0))]
        args += [cq, ck]
        out_shape += [jax.ShapeDtypeStruct((4, 2, S), F32), jax.ShapeDtypeStruct((4, S, 2), F32)]
        out_specs += [pl.BlockSpec((None, 2, BLK), lambda p, j: (p, 0, j)),
                      pl.BlockSpec((None, S, 2), lambda p, j: (p, 0, 0))]
    else:
        in_specs += [pl.BlockSpec((2, 5, BLK, BLK), lambda p, j: (p, 0, 0, 0))]
        args += [bias]
        out_shape.append(jax.ShapeDtypeStruct((8, 5, BLK, BLK), F32))
        out_specs.append(pl.BlockSpec((2, 5, BLK, BLK), lambda p, j: (p, 0, 0, 0)))
    return pl.pallas_call(
        body, name=name, out_shape=tuple(out_shape), grid=(4, nb),
        in_specs=in_specs, out_specs=tuple(out_specs),
        compiler_params=_params(),
    )(*args)


def _log_sigmoid(x):
    return jnp.minimum(x, 0.0) - jnp.log(1.0 + jnp.exp(-jnp.abs(x)))


def _forget_fwd(fb, b_forget, name):
    S = fb.shape[0]
    nb = S // BLK

    def body(fb_ref, b_ref, cum_ref, carry_ref):
        i = pl.program_id(0)
        logf = _log_sigmoid(fb_ref[...] + b_ref[...])
        r = lax.broadcasted_iota(jnp.int32, (BLK, BLK), 0)
        c = lax.broadcasted_iota(jnp.int32, (BLK, BLK), 1)
        tri = (c <= r).astype(F32)

        @pl.when(i == 0)
        def _():
            carry_ref[...] = jnp.zeros_like(carry_ref)

        cum = jnp.dot(tri, logf, preferred_element_type=F32, precision=HIGHEST) + carry_ref[0:1, :]
        cum_ref[...] = cum
        carry_ref[...] = jnp.broadcast_to(cum[BLK - 1:BLK, :], carry_ref.shape)

    return pl.pallas_call(
        body, name=name, out_shape=jax.ShapeDtypeStruct((S, BLK), F32), grid=(nb,),
        in_specs=[pl.BlockSpec((BLK, BLK), lambda i: (i, 0)), pl.BlockSpec((1, BLK), lambda i: (0, 0))],
        out_specs=pl.BlockSpec((BLK, BLK), lambda i: (i, 0)),
        scratch_shapes=[pltpu.VMEM((8, BLK), F32)],
        compiler_params=_params(),
    )(fb, b_forget)


def _forget_bwd(dcum_q, dcum_k, fb, b_forget, name):
    S = fb.shape[0]
    nb = S // BLK

    def body(dq_ref, dk_ref, fb_ref, b_ref, dfb_ref, db_ref, carry_ref):
        g = pl.program_id(0)
        r = lax.broadcasted_iota(jnp.int32, (BLK, BLK), 0)
        c = lax.broadcasted_iota(jnp.int32, (BLK, BLK), 1)
        tri = (c >= r).astype(F32)

        @pl.when(g == 0)
        def _():
            carry_ref[...] = jnp.zeros_like(carry_ref)

        dcum = dq_ref[...] + dk_ref[...]
        dlogf = jnp.dot(tri, dcum, preferred_element_type=F32, precision=HIGHEST) + carry_ref[0:1, :]
        carry_ref[...] = jnp.broadcast_to(dlogf[0:1, :], carry_ref.shape)
        x = fb_ref[...] + b_ref[...]
        dfb = jnp.where(c < N_FORGET, dlogf * jax.nn.sigmoid(-x), 0.0)
        dfb_ref[...] = dfb
        db = jnp.sum(dfb, axis=0, keepdims=True)

        @pl.when(g == 0)
        def _():
            db_ref[...] = db

        @pl.when(g > 0)
        def _():
            db_ref[...] += db

    rev = pl.BlockSpec((BLK, BLK), lambda g: (nb - 1 - g, 0))
    row = pl.BlockSpec((1, BLK), lambda g: (0, 0))
    return pl.pallas_call(
        body, name=name,
        out_shape=(jax.ShapeDtypeStruct((S, BLK), F32), jax.ShapeDtypeStruct((1, BLK), F32)), grid=(nb,),
        in_specs=[rev, rev, rev, row], out_specs=(rev, row),
        scratch_shapes=[pltpu.VMEM((8, BLK), F32)],
        compiler_params=_params(),
    )(dcum_q, dcum_k, fb, b_forget)


REL_PAD = 384
REL_TILE = 2048


def _rel_onehot(d_blk, t):
    m = t * REL_TILE + lax.broadcasted_iota(jnp.int32, (REL_PAD, REL_TILE), 1)
    kk = lax.broadcasted_iota(jnp.int32, (REL_PAD, REL_TILE), 0)
    dist = d_blk * BLK + (m >> 7) - (m & 127)
    return (kk == jnp.clip(dist, -128, 128) + 128).astype(F32)


def _rel_expand(rel_pad, name):
    def body(rel_ref, o_ref):
        onehot = _rel_onehot(pl.program_id(0), pl.program_id(1))
        o_ref[...] = jnp.dot(rel_ref[...], onehot, preferred_element_type=F32, precision=HIGHEST)

    return pl.pallas_call(
        body, name=name, out_shape=jax.ShapeDtypeStruct((5, 8, BLK * BLK), F32),
        grid=(5, BLK * BLK // REL_TILE),
        in_specs=[pl.BlockSpec((8, REL_PAD), lambda d, t: (0, 0))],
        out_specs=pl.BlockSpec((None, 8, REL_TILE), lambda d, t: (d, 0, t)),
        compiler_params=_params(),
    )(rel_pad)


def _rel_reduce(dtiles, name):
    def body(dt_ref, o_ref):
        d, t = pl.program_id(0), pl.program_id(1)
        onehot = _rel_onehot(d, t)
        part = lax.dot_general(dt_ref[...], onehot, NT, preferred_element_type=F32, precision=HIGHEST)

        @pl.when((d == 0) & (t == 0))
        def _():
            o_ref[...] = part

        @pl.when((d > 0) | (t > 0))
        def _():
            o_ref[...] += part

    return pl.pallas_call(
        body, name=name, out_shape=jax.ShapeDtypeStruct((8, REL_PAD), F32),
        grid=(5, BLK * BLK // REL_TILE),
        in_specs=[pl.BlockSpec((None, 8, REL_TILE), lambda d, t: (d, 0, t))],
        out_specs=pl.BlockSpec((8, REL_PAD), lambda d, t: (0, 0)),
        compiler_params=_params(),
    )(dtiles)


def _final_loss(x, target, g, name):
    S, D = x.shape
    ts = _row_tile(S, 256)

    def body(x_ref, t_ref, g_ref, dx_ref, loss_ref, dg_ref):
        i = pl.program_id(0)
        xv, gv = x_ref[...], g_ref[...]
        rstd = lax.rsqrt(jnp.mean(xv * xv, axis=-1, keepdims=True) + EPS)
        xhat = xv * rstd
        err = xhat * gv - t_ref[...]
        part = 0.5 * jnp.sum(jnp.mean(err * err, axis=-1, keepdims=True), axis=0, keepdims=True)
        dy = err / D
        dg = jnp.sum(dy * xhat, axis=0, keepdims=True)
        dxhat = dy * gv
        proj = jnp.mean(dxhat * xhat, axis=-1, keepdims=True)
        dx_ref[...] = rstd * (dxhat - xhat * proj)

        @pl.when(i == 0)
        def _():
            loss_ref[...] = jnp.broadcast_to(part, loss_ref.shape)
            dg_ref[...] = dg

        @pl.when(i > 0)
        def _():
            loss_ref[...] += jnp.broadcast_to(part, loss_ref.shape)
            dg_ref[...] += dg

    tile = pl.BlockSpec((ts, D), lambda i: (i, 0))
    row = pl.BlockSpec((1, D), lambda i: (0, 0))
    return pl.pallas_call(
        body, name=name,
        out_shape=(jax.ShapeDtypeStruct((S, D), F32), jax.ShapeDtypeStruct((8, 128), F32),
                   jax.ShapeDtypeStruct((1, D), F32)),
        grid=(S // ts,), in_specs=[tile, tile, row],
        out_specs=(tile, pl.BlockSpec((8, 128), lambda i: (0, 0)), row),
        compiler_params=_params(),
    )(x, target, g)


def _ada_fwd(c_all, w_ada, name):
    L, D, E = w_ada.shape

    def body(c_ref, w_ref, o_ref):
        cv = c_ref[...]
        cond = cv * jax.nn.sigmoid(cv)
        o_ref[...] = jnp.dot(cond, w_ref[...], preferred_element_type=F32, precision=HIGHEST)

    return pl.pallas_call(
        body, name=name, out_shape=jax.ShapeDtypeStruct((L, N_DEV, E), F32), grid=(L,),
        in_specs=[pl.BlockSpec((N_DEV, D), lambda l: (0, 0)), pl.BlockSpec((None, D, E), lambda l: (l, 0, 0))],
        out_specs=pl.BlockSpec((None, N_DEV, E), lambda l: (l, 0, 0)),
        compiler_params=_params(),
    )(c_all, w_ada)


def _ada_bwd(c_all_t, dmod, name):
    D = c_all_t.shape[0]
    L, _, E = dmod.shape

    def body(c_ref, d_ref, o_ref):
        cv = c_ref[...]
        cond = cv * jax.nn.sigmoid(cv)
        acc = None
        for b in range(N_DEV):
            t = cond[:, b:b + 1] * d_ref[b:b + 1, :]
            acc = t if acc is None else acc + t
        o_ref[...] = acc

    return pl.pallas_call(
        body, name=name, out_shape=jax.ShapeDtypeStruct((L, D, E), F32), grid=(L,),
        in_specs=[pl.BlockSpec((D, N_DEV), lambda l: (0, 0)), pl.BlockSpec((None, N_DEV, E), lambda l: (l, 0, 0))],
        out_specs=pl.BlockSpec((None, D, E), lambda l: (l, 0, 0)),
        compiler_params=_params(),
    )(c_all_t, dmod)


def _adamw(w, m, v, g_parts, name):
    R, C = w.shape
    P = g_parts.shape[0]
    tr = _row_tile(R, max(8, (256 * 1024 // max(C, 128)) // 8 * 8))
    c1 = 1.0 - ADAM_B1 ** ADAM_STEP
    c2 = 1.0 - ADAM_B2 ** ADAM_STEP

    def body(w_ref, m_ref, v_ref, g_ref, go_ref, d_ref, mo_ref, vo_ref):
        g = g_ref[0].astype(F32)
        for k in range(1, P):
            g = g + g_ref[k].astype(F32)
        mn = ADAM_B1 * m_ref[...] + (1.0 - ADAM_B1) * g
        vn = ADAM_B2 * v_ref[...] + (1.0 - ADAM_B2) * (g * g)
        m_hat = mn / c1
        v_hat = vn / c2
        go_ref[...] = g
        d_ref[...] = -ADAM_LR * (m_hat / (jnp.sqrt(v_hat) + ADAM_EPS) + ADAM_WD * w_ref[...])
        mo_ref[...] = mn
        vo_ref[...] = vn

    tile = pl.BlockSpec((tr, C), lambda i: (i, 0))
    out = jax.ShapeDtypeStruct((R, C), F32)
    return pl.pallas_call(
        body, name=name, out_shape=(out, out, out, out), grid=(R // tr,),
        in_specs=[tile, tile, tile, pl.BlockSpec((P, tr, C), lambda i: (0, i, 0))],
        out_specs=(tile, tile, tile, tile),
        compiler_params=_params(),
    )(w, m, v, g_parts)


def _pair_add(pieces, recv, core, name):
    _, _, R, C = pieces.shape
    tr = _row_tile(R, max(8, (512 * 1024 // max(C, 128)) // 8 * 8))

    def body(core_ref, a_ref, b_ref, o_ref):
        o_ref[...] = (a_ref[...].astype(F32) + b_ref[...].astype(F32)).astype(BF16)

    return pl.pallas_call(
        body, name=name, out_shape=jax.ShapeDtypeStruct((4, R, C), BF16),
        grid_spec=pltpu.PrefetchScalarGridSpec(
            num_scalar_prefetch=1, grid=(4, R // tr),
            in_specs=[pl.BlockSpec((None, None, tr, C), lambda k, i, core_ref: (core_ref[0], k, i, 0)),
                      pl.BlockSpec((None, tr, C), lambda k, i, core_ref: (k, i, 0))],
            out_specs=pl.BlockSpec((None, tr, C), lambda k, i, core_ref: (k, i, 0))),
        compiler_params=_params(),
    )(core, pieces, recv)


MESH = pl.DeviceIdType.MESH
ANY = pl.BlockSpec(memory_space=pl.ANY)


def _position():
    return lax.axis_index("x"), lax.axis_index("y"), lax.axis_index("c")


def _small_all_gather(v, name):
    m_per, n = v.shape

    def body(x_ref, out_ref, send_sems, recv_sems, local_sem):
        x, y, c = _position()
        me, sibling = (x, y, c), (x, y, 1 - c)
        chips = [(1 - x, y), (x, 1 - y), (1 - x, 1 - y)]

        def rows(px, py, pc):
            return out_ref.at[pl.ds((4 * px + 2 * py + pc) * m_per, m_per), :]

        def copy(k, block, to, src=None):
            return pltpu.make_async_remote_copy(
                src_ref=rows(*block) if src is None else src, dst_ref=rows(*block),
                send_sem=send_sems.at[k], recv_sem=recv_sems.at[k], device_id=to, device_id_type=MESH)

        mine = pltpu.make_async_copy(x_ref, rows(*me), local_sem)
        mine.start()
        first = [copy(0, me, sibling, src=x_ref)]
        first += [copy(1 + j, me, (*chip, c), src=x_ref) for j, chip in enumerate(chips)]
        for cp in first:
            cp.start()
        passed = [copy(4 + j, (*chip, c), sibling) for j, chip in enumerate(chips)]
        for j, chip in enumerate(chips):
            copy(1 + j, (*chip, c), me).wait_recv()
            passed[j].start()
        copy(0, sibling, me).wait_recv()
        for j, chip in enumerate(chips):
            copy(4 + j, (*chip, 1 - c), me).wait_recv()
        for cp in first + passed:
            cp.wait_send()
        mine.wait()

    return pl.pallas_call(
        body, name=name, out_shape=jax.ShapeDtypeStruct((N_DEV * m_per, n), v.dtype),
        in_specs=[pl.BlockSpec(memory_space=pltpu.VMEM)], out_specs=pl.BlockSpec(memory_space=pltpu.VMEM),
        scratch_shapes=[pltpu.SemaphoreType.DMA((7,)), pltpu.SemaphoreType.DMA((7,)), pltpu.SemaphoreType.DMA],
    )(v)


def _big_all_gather(shards, name):
    n_arr = len(shards)

    def body(*refs):
        x_refs, out_refs = refs[:n_arr], refs[n_arr:2 * n_arr]
        send_sems, recv_sems, local_sems = refs[2 * n_arr:]
        x, y, c = _position()
        me, sibling = (x, y, c), (x, y, 1 - c)
        chips = [(1 - x, y), (x, 1 - y), (1 - x, 1 - y)]

        def slot(a, px, py, pc):
            return out_refs[a].at[4 * px + 2 * py + pc]

        def copy(a, k, block, to, src=None):
            return pltpu.make_async_remote_copy(
                src_ref=slot(a, *block) if src is None else src, dst_ref=slot(a, *block),
                send_sem=send_sems.at[a, k], recv_sem=recv_sems.at[a, k], device_id=to, device_id_type=MESH)

        mine = [pltpu.make_async_copy(x_refs[a], slot(a, *me), local_sems.at[a]) for a in range(n_arr)]
        for cp in mine:
            cp.start()
        first = []
        for j, chip in enumerate(chips):
            first += [copy(a, 1 + j, me, (*chip, c), src=x_refs[a]) for a in range(n_arr)]
        first += [copy(a, 0, me, sibling, src=x_refs[a]) for a in range(n_arr)]
        for cp in first:
            cp.start()
        passed = []
        for j, chip in enumerate(chips):
            for a in range(n_arr):
                copy(a, 1 + j, (*chip, c), me).wait_recv()
                fwd = copy(a, 4 + j, (*chip, c), sibling)
                fwd.start()
                passed.append(fwd)
        for a in range(n_arr):
            copy(a, 0, sibling, me).wait_recv()
        for j, chip in enumerate(chips):
            for a in range(n_arr):
                copy(a, 4 + j, (*chip, 1 - c), me).wait_recv()
        for cp in first + passed:
            cp.wait_send()
        for cp in mine:
            cp.wait()

    return pl.pallas_call(
        body, name=name,
        out_shape=tuple(jax.ShapeDtypeStruct((N_DEV,) + s.shape, s.dtype) for s in shards),
        in_specs=[ANY] * n_arr, out_specs=tuple([ANY] * n_arr),
        scratch_shapes=[pltpu.SemaphoreType.DMA((n_arr, 7)), pltpu.SemaphoreType.DMA((n_arr, 7)),
                        pltpu.SemaphoreType.DMA((n_arr,))],
    )(*shards)


def _sibling_exchange(pieces, name):
    n_arr = len(pieces)

    def body(*refs):
        p_refs, out_refs = refs[:n_arr], refs[n_arr:2 * n_arr]
        send_sems, recv_sems = refs[2 * n_arr:]
        x, y, c = _position()
        copies = [pltpu.make_async_remote_copy(
            src_ref=p_refs[a].at[1 - c], dst_ref=out_refs[a], send_sem=send_sems.at[a], recv_sem=recv_sems.at[a],
            device_id=(x, y, 1 - c), device_id_type=MESH) for a in range(n_arr)]
        for cp in copies:
            cp.start()
        for cp in copies:
            cp.wait()

    return pl.pallas_call(
        body, name=name,
        out_shape=tuple(jax.ShapeDtypeStruct(p.shape[1:], p.dtype) for p in pieces),
        in_specs=[ANY] * n_arr, out_specs=tuple([ANY] * n_arr),
        scratch_shapes=[pltpu.SemaphoreType.DMA((n_arr,)), pltpu.SemaphoreType.DMA((n_arr,))],
    )(*pieces)


def _chip_exchange(sums, name):
    n_arr = len(sums)

    def body(*refs):
        s_refs, out_refs = refs[:n_arr], refs[n_arr:2 * n_arr]
        send_sems, recv_sems, local_sems = refs[2 * n_arr:]
        x, y, c = _position()
        my_chip = 2 * x + y
        chips = [(1 - x, y), (x, 1 - y), (1 - x, 1 - y)]
        mine = [pltpu.make_async_copy(s_refs[a].at[my_chip], out_refs[a].at[my_chip], local_sems.at[a])
                for a in range(n_arr)]
        for cp in mine:
            cp.start()
        copies = []
        for j, (px, py) in enumerate(chips):
            copies += [pltpu.make_async_remote_copy(
                src_ref=s_refs[a].at[2 * px + py], dst_ref=out_refs[a].at[my_chip],
                send_sem=send_sems.at[a, j], recv_sem=recv_sems.at[a, j],
                device_id=(px, py, c), device_id_type=MESH) for a in range(n_arr)]
        for cp in copies:
            cp.start()
        for j, (px, py) in enumerate(chips):
            for a in range(n_arr):
                pltpu.make_async_remote_copy(
                    src_ref=s_refs[a].at[my_chip], dst_ref=out_refs[a].at[2 * px + py],
                    send_sem=send_sems.at[a, j], recv_sem=recv_sems.at[a, j],
                    device_id=(px, py, c), device_id_type=MESH).wait_recv()
        for cp in copies:
            cp.wait_send()
        for cp in mine:
            cp.wait()

    return pl.pallas_call(
        body, name=name,
        out_shape=tuple(jax.ShapeDtypeStruct(s.shape, s.dtype) for s in sums),
        in_specs=[ANY] * n_arr, out_specs=tuple([ANY] * n_arr),
        scratch_shapes=[pltpu.SemaphoreType.DMA((n_arr, 3)), pltpu.SemaphoreType.DMA((n_arr, 3)),
                        pltpu.SemaphoreType.DMA((n_arr,))],
    )(*sums)


def _cols_from_shards(g):
    return jnp.transpose(g, (1, 0, 2)).reshape(g.shape[1], N_DEV * g.shape[2])


def _w_in_rearranged(g):
    w = _cols_from_shards(g)
    return jnp.concatenate([w[:, :F_COL], w[:, F_COL + N_FORGET:], w[:, F_COL:F_COL + N_FORGET],
                            jnp.zeros((w.shape[0], BLK - N_FORGET), w.dtype)], axis=1)


def _w_in_original(dw_r):
    return jnp.concatenate([dw_r[:, :F_COL], dw_r[:, N_MAIN:N_MAIN + N_FORGET], dw_r[:, F_COL:N_MAIN]], axis=1)


def _col_pieces(dw):
    r = dw.shape[0]
    return jnp.transpose(dw.reshape(r, 4, 2, dw.shape[1] // N_DEV), (2, 1, 0, 3))


def _row_pieces(dw):
    return jnp.transpose(dw.reshape(4, 2, dw.shape[0] // N_DEV, dw.shape[1]), (1, 0, 2, 3))


def _pairs_col(a):
    return jnp.transpose(a.reshape(a.shape[0], 4, 2), (1, 0, 2))


def _pairs_row(a):
    return jnp.transpose(a.reshape(a.shape[0], 4, 2), (1, 2, 0))


def _pad_lanes(a, n):
    return jnp.pad(a, [(0, 0)] * (a.ndim - 1) + [(0, n - a.shape[-1])])


SMALL_SEGMENTS = (("dmod", 2 * 6 * D_MODEL), ("norm_mix_g", 2 * D_MODEL), ("norm_ffn_g", 2 * D_MODEL),
                  ("final_norm_g", D_MODEL), ("b_forget", 128), ("sinks", 128), ("rel_bias", 4224))
SMALL_ROWS = 176


def _pack_small(parts):
    flat = [_pad_lanes(parts[name].reshape(1, -1), size) for name, size in SMALL_SEGMENTS]
    total = sum(size for _, size in SMALL_SEGMENTS)
    flat.append(jnp.zeros((1, SMALL_ROWS * 128 - total), F32))
    return jnp.concatenate(flat, axis=1).reshape(SMALL_ROWS, 128)


def _unpack_small(packed, shapes):
    flat = packed.reshape(-1)
    out, pos = {}, 0
    for name, size in SMALL_SEGMENTS:
        shape = shapes[name]
        count = 1
        for d in shape:
            count *= d
        out[name] = flat[pos:pos + count].reshape(shape)
        pos += size
    return out


def kernel(x, c, norm_mix_g, norm_ffn_g, w_ada, b_ada, w_in, b_forget, sinks, rel_bias, w_branch, w_out, w_ffn_in, w_ffn_out, final_norm_g, loss_target, m_norm_mix_g, m_norm_ffn_g, m_w_ada, m_b_ada, m_w_in, m_b_forget, m_sinks, m_rel_bias, m_w_branch, m_w_out, m_w_ffn_in, m_w_ffn_out, m_final_norm_g, v_norm_mix_g, v_norm_ffn_g, v_w_ada, v_b_ada, v_w_in, v_b_forget, v_sinks, v_rel_bias, v_w_branch, v_w_out, v_w_ffn_in, v_w_ffn_out, v_final_norm_g):
    depth = w_in.shape[0]
    S, D = x.shape[1], x.shape[2]
    px, py, pc = _position()
    me = 4 * px + 2 * py + pc
    x0 = x[0]

    g_in, g_branch, g_out, g_fin, g_fout = _big_all_gather(
        [w_in.astype(BF16), w_branch.astype(BF16), w_out.astype(BF16), w_ffn_in.astype(BF16),
         w_ffn_out.astype(BF16)], "comm_gather_weights")
    W_in = [_w_in_rearranged(g_in[:, l]) for l in range(depth)]
    W_branch = [jnp.transpose(g_branch[:, l], (1, 2, 0, 3)).reshape(3, 512, D) for l in range(depth)]
    W_out = [g_out[:, l].reshape(D, D) for l in range(depth)]
    W_fin = [_cols_from_shards(g_fin[:, l]) for l in range(depth)]
    W_fout = [g_fout[:, l].reshape(FFN_HIDDEN, D) for l in range(depth)]

    c_all = _small_all_gather(c.reshape(8, 128), "comm_gather_c").reshape(N_DEV, D)
    mod_cols = _ada_fwd(c_all, w_ada, "ada_fwd")
    mod_all = _small_all_gather(mod_cols.reshape(-1, 128), "comm_gather_mod")
    mod_all = mod_all.reshape(N_DEV, depth, N_DEV, w_ada.shape[2])
    mod_mine = lax.dynamic_index_in_dim(mod_all, me, axis=2, keepdims=False)
    mod = jnp.transpose(mod_mine, (1, 0, 2)).reshape(depth, 6 * D) + b_ada
    mods = [[mod[l:l + 1, k * D:(k + 1) * D] for k in range(6)] for l in range(depth)]

    slopes = jnp.exp2(-jnp.arange(1, 9, dtype=F32))
    saved = []
    xs = x0
    for l in range(depth):
        sh_m, sc_m, g_m, sh_f, sc_f, g_f = mods[l]
        gm, gf = norm_mix_g[l:l + 1], norm_ffn_g[l:l + 1]
        bfor = _pad_lanes(b_forget[l:l + 1], BLK)
        h = _norm_mod_fwd(xs, gm, sh_m, sc_m, f"norm_mix_fwd{l}")
        qkv = _matmul(h, W_in[l], "nn", BF16, f"proj_qkv{l}", n=N_QKV, tn=768)
        gates = _matmul(h, W_in[l], "nn", F32, f"proj_gates{l}", n=N_GATES, tn=768, b_off=N_QKV // 768)
        fb = _matmul(h, W_in[l], "nn", F32, f"proj_forget{l}", n=BLK, tn=BLK, b_off=N_MAIN // BLK)
        cum = _forget_fwd(fb, bfor, f"forget_fwd{l}")[:, :N_FORGET]
        cq, ck = _pairs_col(cum), _pairs_row(cum)
        rel_pad = _pad_lanes(rel_bias[l], REL_PAD)
        tiles = _rel_expand(rel_pad, f"rel_expand{l}").reshape(5, 8, BLK, BLK)
        tiles = jnp.transpose(tiles, (1, 0, 2, 3))
        o_a, lse_a = _attn_fwd("a", qkv, f"attn_a_fwd{l}", sinks=sinks[l], slopes=slopes)
        o_b, lse_b = _attn_fwd("b", qkv, f"attn_b_fwd{l}", cq=cq, ck=ck)
        o_c, lse_c = _attn_fwd("c", qkv, f"attn_c_fwd{l}", bias=tiles)
        merged = _merge_fwd(o_a, o_b, o_c, gates, W_branch[l], f"merge_fwd{l}")
        x1, mix = _matmul_resid(merged, W_out[l], xs, g_m, f"out_proj{l}")
        h2 = _norm_mod_fwd(x1, gf, sh_f, sc_f, f"norm_ffn_fwd{l}")
        act = _ffn_in_fwd(h2, W_fin[l], f"ffn_in_fwd{l}")
        x2, ffn = _matmul_resid(act, W_fout[l], x1, g_f, f"ffn_out{l}")
        saved.append(dict(x=xs, h=h, qkv=qkv, gates=gates, fb=fb, bfor=bfor, cq=cq, ck=ck, tiles=tiles,
                          o=(o_a, o_b, o_c), lse=(lse_a, lse_b, lse_c), merged=merged, mix=mix, x1=x1,
                          h2=h2, act=act, ffn=ffn))
        xs = x2

    dx, loss_tile, d_final_g = _final_loss(xs, loss_target[0], final_norm_g.reshape(1, D), "final_loss")
    loss = lax.psum(loss_tile[0, 0], ("x", "y", "c"))

    grads = {k: [None] * depth for k in ("w_in", "w_branch", "w_out", "w_ffn_in", "w_ffn_out", "norm_mix_g",
                                          "norm_ffn_g", "b_forget", "sinks", "rel_bias", "dmod")}
    for l in reversed(range(depth)):
        sv = saved[l]
        sh_m, sc_m, g_m, sh_f, sc_f, g_f = mods[l]
        gm, gf = norm_mix_g[l:l + 1], norm_ffn_g[l:l + 1]
        df, d_g_f = _gate_bwd(dx, sv["ffn"], g_f, f"ffn_gate_bwd{l}")
        du_g, du_u = _ffn_mid_bwd(sv["h2"], df, W_fin[l], W_fout[l], f"ffn_mid_bwd{l}")
        du = jnp.concatenate([du_g, du_u], axis=1)
        grads["w_ffn_out"][l] = _matmul(sv["act"], df, "tn", BF16, f"wgrad_ffn_out{l}", tm=1408, tn=512, tk=512)
        grads["w_ffn_in"][l] = _matmul(sv["h2"], du, "tn", BF16, f"wgrad_ffn_in{l}", tm=512, tn=1408, tk=512)
        dh2 = _matmul(du, W_fin[l], "nt", F32, f"dgrad_ffn_in{l}", tn=512)
        dx1, d_sh_f, d_sc_f, d_gf = _norm_mod_bwd(sv["x1"], dh2, dx, gf, sc_f, f"norm_ffn_bwd{l}")
        dmix, d_g_m = _gate_bwd(dx1, sv["mix"], g_m, f"mix_gate_bwd{l}")
        grads["w_out"][l] = _matmul(sv["merged"], dmix, "tn", BF16, f"wgrad_out{l}", tm=512, tn=1024, tk=512)
        dmerged = _matmul(dmix, W_out[l], "nt", F32, f"dgrad_out{l}", tn=512)
        o_a, o_b, o_c = sv["o"]
        dgates, dy, do_a, do_b, do_c = _merge_bwd(dmerged, o_a, o_b, o_c, sv["gates"], W_branch[l], f"merge_bwd{l}")
        dwb = [_matmul(o_k, dy, "tn", BF16, f"wgrad_branch{l}_{k}", n=D, b_off=k * (D // 512), tm=512, tn=512, tk=512)
               for k, o_k in enumerate((o_a, o_b, o_c))]
        grads["w_branch"][l] = jnp.stack(dwb)
        dq_a, dk_a, dv_a, dsink = _attn_bwd("a", sv["qkv"], o_a, do_a, sv["lse"][0], f"attn_a_bwd{l}",
                                            sinks=sinks[l], slopes=slopes)
        dq_b, dk_b, dv_b, dck, dcq = _attn_bwd("b", sv["qkv"], o_b, do_b, sv["lse"][1], f"attn_b_bwd{l}",
                                          cq=sv["cq"], ck=sv["ck"])
        dq_c, dk_c, dv_c, dtiles = _attn_bwd("c", sv["qkv"], o_c, do_c, sv["lse"][2], f"attn_c_bwd{l}",
                                             bias=sv["tiles"])
        grads["sinks"][l] = dsink[:, :2, 0].reshape(8)
        dtiles = jnp.transpose(dtiles, (1, 0, 2, 3)).reshape(5, 8, BLK * BLK)
        grads["rel_bias"][l] = _rel_reduce(dtiles, f"rel_reduce{l}")[:, :rel_bias.shape[2]]
        dcum_k = _pad_lanes(jnp.transpose(dck, (2, 0, 1)).reshape(S, 8), BLK)
        dcum_q = _pad_lanes(jnp.transpose(dcq, (1, 0, 2)).reshape(S, 8), BLK)
        dfb, d_bfor = _forget_bwd(dcum_q, dcum_k, sv["fb"], sv["bfor"], f"forget_bwd{l}")
        grads["b_forget"][l] = d_bfor[0, :N_FORGET]
        dproj = jnp.concatenate([t.astype(BF16) for t in (dq_a, dk_a, dv_a, dq_b, dk_b, dv_b, dq_c, dk_c, dv_c)]
                                + [dgates, dfb.astype(BF16)], axis=1)
        grads["w_in"][l] = _w_in_original(
            _matmul(sv["h"], dproj, "tn", BF16, f"wgrad_in{l}", tm=512, tn=1408, tk=512))
        dh = _matmul(dproj, W_in[l], "nt", F32, f"dgrad_in{l}", tn=512)
        dx, d_sh_m, d_sc_m, d_gm = _norm_mod_bwd(sv["x"], dh, dx1, gm, sc_m, f"norm_mix_bwd{l}")
        grads["norm_mix_g"][l] = d_gm[0]
        grads["norm_ffn_g"][l] = d_gf[0]
        grads["dmod"][l] = jnp.concatenate([d_sh_m, d_sc_m, d_g_m, d_sh_f, d_sc_f, d_g_f], axis=1)[0]

    grad_x = dx.reshape(x.shape)

    small_shapes = dict(dmod=b_ada.shape, norm_mix_g=norm_mix_g.shape, norm_ffn_g=norm_ffn_g.shape,
                        final_norm_g=final_norm_g.shape, b_forget=b_forget.shape, sinks=sinks.shape,
                        rel_bias=rel_bias.shape)
    mine_small = _pack_small(dict(
        dmod=jnp.stack(grads["dmod"]), norm_mix_g=jnp.stack(grads["norm_mix_g"]),
        norm_ffn_g=jnp.stack(grads["norm_ffn_g"]), final_norm_g=d_final_g[0],
        b_forget=_pad_lanes(jnp.stack(grads["b_forget"]).reshape(1, -1), 128),
        sinks=_pad_lanes(jnp.stack(grads["sinks"]).reshape(1, -1), 128),
        rel_bias=_pad_lanes(jnp.stack(grads["rel_bias"]).reshape(1, -1), 4224)))
    all_small = _small_all_gather(mine_small, "comm_gather_small").reshape(N_DEV, SMALL_ROWS, 128)

    def pack_params(b_ada_, nm, nf, fn, bf, sk, rb):
        return _pack_small(dict(dmod=b_ada_, norm_mix_g=nm, norm_ffn_g=nf, final_norm_g=fn,
                                b_forget=_pad_lanes(bf.reshape(1, -1), 128), sinks=_pad_lanes(sk.reshape(1, -1), 128),
                                rel_bias=_pad_lanes(rb.reshape(1, -1), 4224)))

    small_out = _adamw(
        pack_params(b_ada, norm_mix_g, norm_ffn_g, final_norm_g, b_forget, sinks, rel_bias),
        pack_params(m_b_ada, m_norm_mix_g, m_norm_ffn_g, m_final_norm_g, m_b_forget, m_sinks, m_rel_bias),
        pack_params(v_b_ada, v_norm_mix_g, v_norm_ffn_g, v_final_norm_g, v_b_forget, v_sinks, v_rel_bias),
        all_small, "adamw_small")
    small_out = [_unpack_small(t, small_shapes) for t in small_out]

    dmod_all = all_small[:, :96].reshape(N_DEV, depth, 6 * D)
    dmod_cols = lax.dynamic_slice_in_dim(dmod_all, me * w_ada.shape[2], w_ada.shape[2], axis=2)
    d_w_ada = _ada_bwd(jnp.transpose(c_all), jnp.transpose(dmod_cols, (1, 0, 2)), "ada_bwd")

    pieces = [
        jnp.concatenate([_col_pieces(g) for g in grads["w_in"]], axis=2),
        jnp.concatenate([jnp.transpose(g.reshape(3, 512, 4, 2, D // N_DEV), (3, 2, 0, 1, 4)).reshape(
            2, 4, 3 * 512, D // N_DEV) for g in grads["w_branch"]], axis=2),
        jnp.concatenate([_row_pieces(g) for g in grads["w_out"]], axis=2),
        jnp.concatenate([_col_pieces(g) for g in grads["w_ffn_in"]], axis=2),
        jnp.concatenate([_row_pieces(g) for g in grads["w_ffn_out"]], axis=2),
    ]
    from_sibling = _sibling_exchange(pieces, "comm_reduce_sibling")
    core = pc.astype(jnp.int32).reshape(1)
    pair_sums = [_pair_add(p, r, core, f"pair_add{a}") for a, (p, r) in enumerate(zip(pieces, from_sibling))]
    chip_parts = _chip_exchange(pair_sums, "comm_reduce_chips")

    big = {}
    for name, w, m, v, parts in (
            ("w_in", w_in, m_w_in, v_w_in, chip_parts[0]), ("w_branch", w_branch, m_w_branch, v_w_branch, chip_parts[1]),
            ("w_out", w_out, m_w_out, v_w_out, chip_parts[2]), ("w_ffn_in", w_ffn_in, m_w_ffn_in, v_w_ffn_in, chip_parts[3]),
            ("w_ffn_out", w_ffn_out, m_w_ffn_out, v_w_ffn_out, chip_parts[4])):
        flat = lambda t: t.reshape(-1, t.shape[-1])
        outs = _adamw(flat(w), flat(m), flat(v), parts, f"adamw_{name}")
        big[name] = [t.reshape(w.shape) for t in outs]
    flat = lambda t: t.reshape(-1, t.shape[-1])
    big["w_ada"] = [t.reshape(w_ada.shape) for t in
                    _adamw(flat(w_ada), flat(m_w_ada), flat(v_w_ada), d_w_ada.reshape(1, -1, w_ada.shape[2]), "adamw_w_ada")]

    def leaf(kind, name):
        if name in big:
            return big[name][kind]
        return small_out[kind]["dmod" if name == "b_ada" else name]

    order = ["norm_mix_g", "norm_ffn_g", "w_ada", "b_ada", "w_in", "b_forget", "sinks", "rel_bias", "w_branch",
             "w_out", "w_ffn_in", "w_ffn_out", "final_norm_g"]
    return (loss, grad_x, *[leaf(0, n) for n in order], *[leaf(1, n) for n in order],
            *[leaf(2, n) for n in order], *[leaf(3, n) for n in order])
```

```python
import functools

import jax
import jax.numpy as jnp
from jax import lax
from jax.experimental import pallas as pl
from jax.experimental.pallas import tpu as pltpu

F32 = jnp.float32
BF16 = jnp.bfloat16
NEG_INF = -1e30
EPS = 1e-6
N_DEV = 8
BLK = 128
GROUP = 4 * BLK
VMEM_LIMIT_BYTES = 56 * 1024 * 1024

D_MODEL = 1024
N_QKV = 3840
N_GATES = 3072
N_MAIN = N_QKV + N_GATES
N_FORGET = 8
N_IN = N_MAIN + N_FORGET
N_INR = N_MAIN + BLK
F_COL = 2304
FFN_HIDDEN = 2816
N_REL = 257

ADAM_LR, ADAM_B1, ADAM_B2, ADAM_EPS, ADAM_WD, ADAM_STEP = 0.001, 0.9, 0.999, 1e-08, 0.01, 10

NN = (((1,), (0,)), ((), ()))
NT = (((1,), (1,)), ((), ()))
TN = (((0,), (0,)), ((), ()))
HIGHEST = lax.Precision.HIGHEST

ATTN_COLS = {"a": (0, 4, 5), "b": (6, 10, 14), "c": (18, 22, 26)}
ATTN_WINDOW = {"a": 2, "c": 5}


def _params():
    return pltpu.CompilerParams(vmem_limit_bytes=VMEM_LIMIT_BYTES)


def _tile(n, target):
    best = None
    t = 128
    while t <= min(n, target):
        if n % t == 0:
            best = t
        t += 128
    return best if best is not None else n


def _row_tile(n, target):
    t = min(n, target)
    while n % t:
        t -= 8
    return t


def _matmul(a, b, mode, out_dtype, name, *, n=None, a_off=0, b_off=0, m=None, tm=512, tn=768, tk=1408):
    if mode == "nn":
        M, K = a.shape if m is None else (m, a.shape[1])
        N = b.shape[1] if n is None else n
    elif mode == "nt":
        M, K = a.shape
        N = b.shape[0] if n is None else n
    else:
        K = a.shape[0]
        M = a.shape[1] if m is None else m
        N = b.shape[1] if n is None else n
    tm = _tile(M, tm) if M % 128 == 0 else M
    tn = _tile(N, tn)
    tk = _tile(K, tk)
    nk = K // tk
    dims = {"nn": NN, "nt": NT, "tn": TN}[mode]
    if mode == "nn":
        a_spec = pl.BlockSpec((tm, tk), lambda i, j, k: (i + a_off, k))
        b_spec = pl.BlockSpec((tk, tn), lambda i, j, k: (k, j + b_off))
    elif mode == "nt":
        a_spec = pl.BlockSpec((tm, tk), lambda i, j, k: (i + a_off, k))
        b_spec = pl.BlockSpec((tn, tk), lambda i, j, k: (j + b_off, k))
    else:
        a_spec = pl.BlockSpec((tk, tm), lambda i, j, k: (k, i + a_off))
        b_spec = pl.BlockSpec((tk, tn), lambda i, j, k: (k, j + b_off))

    def body(a_ref, b_ref, o_ref, acc_ref):
        k = pl.program_id(2)
        part = lax.dot_general(a_ref[...], b_ref[...], dims, preferred_element_type=F32)
        if nk == 1:
            o_ref[...] = part.astype(o_ref.dtype)
        else:
            @pl.when(k == 0)
            def _():
                acc_ref[...] = part

            @pl.when(k > 0)
            def _():
                acc_ref[...] += part

            @pl.when(k == nk - 1)
            def _():
                o_ref[...] = acc_ref[...].astype(o_ref.dtype)

    return pl.pallas_call(
        body, name=name,
        out_shape=jax.ShapeDtypeStruct((M, N), out_dtype),
        grid=(M // tm, N // tn, nk),
        in_specs=[a_spec, b_spec],
        out_specs=pl.BlockSpec((tm, tn), lambda i, j, k: (i, j)),
        scratch_shapes=[pltpu.VMEM((tm, tn) if nk > 1 else (8, 128), F32)],
        compiler_params=_params(),
    )(a, b)


def _matmul_resid(a, b, resid, gate, name, *, tm=512, tn=512, tk=1408):
    M, K = a.shape
    N = b.shape[1]
    tm, tn, tk = _tile(M, tm), _tile(N, tn), _tile(K, tk)
    nk = K // tk

    def body(a_ref, b_ref, r_ref, g_ref, o_ref, s_ref, acc_ref):
        k = pl.program_id(2)
        part = jnp.dot(a_ref[...], b_ref[...], preferred_element_type=F32)

        def finish(acc):
            o_ref[...] = r_ref[...] + g_ref[...] * acc
            s_ref[...] = acc.astype(BF16)

        if nk == 1:
            finish(part)
        else:
            @pl.when(k == 0)
            def _():
                acc_ref[...] = part

            @pl.when(k > 0)
            def _():
                acc_ref[...] += part

            @pl.when(k == nk - 1)
            def _():
                finish(acc_ref[...])

    return pl.pallas_call(
        body, name=name,
        out_shape=(jax.ShapeDtypeStruct((M, N), F32), jax.ShapeDtypeStruct((M, N), BF16)),
        grid=(M // tm, N // tn, nk),
        in_specs=[pl.BlockSpec((tm, tk), lambda i, j, k: (i, k)),
                  pl.BlockSpec((tk, tn), lambda i, j, k: (k, j)),
                  pl.BlockSpec((tm, tn), lambda i, j, k: (i, j)),
                  pl.BlockSpec((1, tn), lambda i, j, k: (0, j))],
        out_specs=(pl.BlockSpec((tm, tn), lambda i, j, k: (i, j)),
                   pl.BlockSpec((tm, tn), lambda i, j, k: (i, j))),
        scratch_shapes=[pltpu.VMEM((tm, tn) if nk > 1 else (8, 128), F32)],
        compiler_params=_params(),
    )(a, b, resid, gate)


def _norm_mod_fwd(x, g, shift, scale, name):
    S, D = x.shape
    ts = _row_tile(S, 256)

    def body(x_ref, g_ref, sh_ref, sc_ref, h_ref):
        xv = x_ref[...]
        rstd = lax.rsqrt(jnp.mean(xv * xv, axis=-1, keepdims=True) + EPS)
        y = xv * rstd * g_ref[...]
        h_ref[...] = (y * (1.0 + sc_ref[...]) + sh_ref[...]).astype(BF16)

    row = pl.BlockSpec((1, D), lambda i: (0, 0))
    return pl.pallas_call(
        body, name=name, out_shape=jax.ShapeDtypeStruct((S, D), BF16), grid=(S // ts,),
        in_specs=[pl.BlockSpec((ts, D), lambda i: (i, 0)), row, row, row],
        out_specs=pl.BlockSpec((ts, D), lambda i: (i, 0)),
        compiler_params=_params(),
    )(x, g, shift, scale)


def _norm_mod_bwd(x, dh, dres, g, scale, name):
    S, D = x.shape
    ts = _row_tile(S, 256)

    def body(x_ref, dh_ref, dr_ref, g_ref, sc_ref, dx_ref, dsh_ref, dsc_ref, dg_ref):
        i = pl.program_id(0)
        xv, dhv, gv = x_ref[...], dh_ref[...], g_ref[...]
        rstd = lax.rsqrt(jnp.mean(xv * xv, axis=-1, keepdims=True) + EPS)
        xhat = xv * rstd
        dn = dhv * (1.0 + sc_ref[...])
        dxhat = dn * gv
        proj = jnp.mean(dxhat * xhat, axis=-1, keepdims=True)
        dx_ref[...] = dr_ref[...] + rstd * (dxhat - xhat * proj)
        dsh = jnp.sum(dhv, axis=0, keepdims=True)
        dsc = jnp.sum(dhv * (xhat * gv), axis=0, keepdims=True)
        dg = jnp.sum(dn * xhat, axis=0, keepdims=True)

        @pl.when(i == 0)
        def _():
            dsh_ref[...] = dsh
            dsc_ref[...] = dsc
            dg_ref[...] = dg

        @pl.when(i > 0)
        def _():
            dsh_ref[...] += dsh
            dsc_ref[...] += dsc
            dg_ref[...] += dg

    tile = pl.BlockSpec((ts, D), lambda i: (i, 0))
    row = pl.BlockSpec((1, D), lambda i: (0, 0))
    vec = jax.ShapeDtypeStruct((1, D), F32)
    return pl.pallas_call(
        body, name=name, out_shape=(jax.ShapeDtypeStruct((S, D), F32), vec, vec, vec), grid=(S // ts,),
        in_specs=[tile, tile, tile, row, row], out_specs=(tile, row, row, row),
        compiler_params=_params(),
    )(x, dh, dres, g, scale)


def _gate_bwd(dx, f, gate, name):
    S, D = dx.shape
    ts = _row_tile(S, 256)

    def body(dx_ref, f_ref, g_ref, df_ref, dg_ref):
        i = pl.program_id(0)
        dxv = dx_ref[...]
        df_ref[...] = (dxv * g_ref[...]).astype(BF16)
        dg = jnp.sum(dxv * f_ref[...].astype(F32), axis=0, keepdims=True)

        @pl.when(i == 0)
        def _():
            dg_ref[...] = dg

        @pl.when(i > 0)
        def _():
            dg_ref[...] += dg

    tile = pl.BlockSpec((ts, D), lambda i: (i, 0))
    row = pl.BlockSpec((1, D), lambda i: (0, 0))
    return pl.pallas_call(
        body, name=name,
        out_shape=(jax.ShapeDtypeStruct((S, D), BF16), jax.ShapeDtypeStruct((1, D), F32)), grid=(S // ts,),
        in_specs=[tile, tile, row], out_specs=(tile, row),
        compiler_params=_params(),
    )(dx, f, gate)


def _ffn_in_fwd(h, w, name, *, tm=512, tn=256):
    S, D = h.shape
    F = w.shape[1] // 2
    tm, tn = _tile(S, tm), _tile(F, tn)
    nj = F // tn

    def body(h_ref, wg_ref, wu_ref, o_ref):
        hv = h_ref[...]
        ug = jnp.dot(hv, wg_ref[...], preferred_element_type=F32)
        uu = jnp.dot(hv, wu_ref[...], preferred_element_type=F32)
        o_ref[...] = (ug * jax.nn.sigmoid(ug) * uu).astype(BF16)

    return pl.pallas_call(
        body, name=name, out_shape=jax.ShapeDtypeStruct((S, F), BF16), grid=(S // tm, nj),
        in_specs=[pl.BlockSpec((tm, D), lambda i, j: (i, 0)),
                  pl.BlockSpec((D, tn), lambda i, j: (0, j)),
                  pl.BlockSpec((D, tn), lambda i, j: (0, j + nj))],
        out_specs=pl.BlockSpec((tm, tn), lambda i, j: (i, j)),
        compiler_params=_params(),
    )(h, w, w)


def _ffn_mid_bwd(h, df, w_in, w_out, name, *, tm=512, tn=256):
    S, D = h.shape
    F = w_in.shape[1] // 2
    tm, tn = _tile(S, tm), _tile(F, tn)
    nj = F // tn

    def body(h_ref, df_ref, wg_ref, wu_ref, wo_ref, dg_ref, du_ref):
        hv = h_ref[...]
        ug = jnp.dot(hv, wg_ref[...], preferred_element_type=F32)
        uu = jnp.dot(hv, wu_ref[...], preferred_element_type=F32)
        dact = lax.dot_general(df_ref[...], wo_ref[...], NT, preferred_element_type=F32)
        sig = jax.nn.sigmoid(ug)
        dg_ref[...] = (dact * uu * (sig * (1.0 + ug * (1.0 - sig)))).astype(BF16)
        du_ref[...] = (dact * (ug * sig)).astype(BF16)

    out = jax.ShapeDtypeStruct((S, F), BF16)
    return pl.pallas_call(
        body, name=name, out_shape=(out, out), grid=(S // tm, nj),
        in_specs=[pl.BlockSpec((tm, D), lambda i, j: (i, 0)),
                  pl.BlockSpec((tm, D), lambda i, j: (i, 0)),
                  pl.BlockSpec((D, tn), lambda i, j: (0, j)),
                  pl.BlockSpec((D, tn), lambda i, j: (0, j + nj)),
                  pl.BlockSpec((tn, D), lambda i, j: (j, 0))],
        out_specs=(pl.BlockSpec((tm, tn), lambda i, j: (i, j)), pl.BlockSpec((tm, tn), lambda i, j: (i, j))),
        compiler_params=_params(),
    )(h, df, w_in, w_in, w_out)


def _merge_fwd(o_a, o_b, o_c, gates, w_branch, name, *, tm=256):
    S, W = o_a.shape
    D = w_branch.shape[2]
    tm = _row_tile(S, tm)

    def body(oa_ref, ob_ref, oc_ref, g_ref, w_ref, m_ref):
        acc = None
        for k, o_ref in enumerate((oa_ref, ob_ref, oc_ref)):
            y = jnp.dot(o_ref[...], w_ref[k], preferred_element_type=F32)
            t = jax.nn.sigmoid(g_ref[:, k * D:(k + 1) * D]) * y
            acc = t if acc is None else acc + t
        m_ref[...] = acc.astype(BF16)

    o_spec = pl.BlockSpec((tm, W), lambda i: (i, 0))
    return pl.pallas_call(
        body, name=name, out_shape=jax.ShapeDtypeStruct((S, D), BF16), grid=(S // tm,),
        in_specs=[o_spec, o_spec, o_spec, pl.BlockSpec((tm, 3 * D), lambda i: (i, 0)),
                  pl.BlockSpec((3, W, D), lambda i: (0, 0, 0))],
        out_specs=pl.BlockSpec((tm, D), lambda i: (i, 0)),
        compiler_params=_params(),
    )(o_a, o_b, o_c, gates, w_branch)


def _merge_bwd(dmerged, o_a, o_b, o_c, gates, w_branch, name, *, tm=256):
    S, W = o_a.shape
    D = w_branch.shape[2]
    tm = _row_tile(S, tm)
    n_heads = W // 64

    def body(dm_ref, oa_ref, ob_ref, oc_ref, g_ref, w_ref, dg_ref, dy_ref,
             doa_ref, dob_ref, doc_ref, dla_ref, dlb_ref, dlc_ref):
        dm = dm_ref[...]
        branches = ((oa_ref, doa_ref, dla_ref), (ob_ref, dob_ref, dlb_ref), (oc_ref, doc_ref, dlc_ref))
        for k, (o_ref, do_ref, dl_ref) in enumerate(branches):
            wk = w_ref[k]
            ov = o_ref[...]
            y = jnp.dot(ov, wk, preferred_element_type=F32)
            g = jax.nn.sigmoid(g_ref[:, k * D:(k + 1) * D])
            dy = (dm * g).astype(BF16)
            dy_ref[:, k * D:(k + 1) * D] = dy
            dg_ref[:, k * D:(k + 1) * D] = (dm * y * (g * (1.0 - g))).astype(BF16)
            do16 = lax.dot_general(dy, wk, NT, preferred_element_type=F32).astype(BF16)
            do_ref[...] = do16
            prod = do16.astype(F32) * ov.astype(F32)
            for h in range(n_heads):
                dl_ref[:, h:h + 1] = jnp.sum(prod[:, 64 * h:64 * (h + 1)], axis=1, keepdims=True)

    o_spec = pl.BlockSpec((tm, W), lambda i: (i, 0))
    wide = pl.BlockSpec((tm, 3 * D), lambda i: (i, 0))
    dl_spec = pl.BlockSpec((tm, n_heads), lambda i: (i, 0))
    o_out = jax.ShapeDtypeStruct((S, W), BF16)
    wide_out = jax.ShapeDtypeStruct((S, 3 * D), BF16)
    dl_out = jax.ShapeDtypeStruct((S, n_heads), F32)
    return pl.pallas_call(
        body, name=name, out_shape=(wide_out, wide_out, o_out, o_out, o_out, dl_out, dl_out, dl_out),
        grid=(S // tm,),
        in_specs=[pl.BlockSpec((tm, D), lambda i: (i, 0)), o_spec, o_spec, o_spec, wide,
                  pl.BlockSpec((3, W, D), lambda i: (0, 0, 0))],
        out_specs=(wide, wide, o_spec, o_spec, o_spec, dl_spec, dl_spec, dl_spec),
        compiler_params=_params(),
    )(dmerged, o_a, o_b, o_c, gates, w_branch)


def _band_mask(variant, t_abs, s_abs):
    if variant == "b":
        return s_abs <= t_abs
    qc, kc = t_abs >> 6, s_abs >> 6
    return (kc <= qc) & (kc >= qc - (2 if variant == "a" else 8))


def _attn_fwd(variant, qkv, name, *, sinks=None, slopes=None, cq_col=None, ck_row=None, bias=None):
    S = qkv.shape[0]
    nb = S // BLK
    qb, kb, vb = ATTN_COLS[variant]
    shared_kv = variant == "a"
    win = ATTN_WINDOW.get(variant)

    def body(*refs):
        if variant == "a":
            q_ref, k_ref, v_ref, sink_ref, slope_ref, o_ref, lse_ref = refs
        elif variant == "b":
            q_ref, k_ref, v_ref, cq_ref, ck_ref, o_ref, lse_ref = refs
        else:
            q_ref, k_ref, v_ref, bias_ref, o_ref, lse_ref = refs
        p, i = pl.program_id(0), pl.program_id(1)
        lane = lax.broadcasted_iota(jnp.int32, (BLK, BLK), 1)
        t_abs = i * BLK + lax.broadcasted_iota(jnp.int32, (BLK, 1), 0)
        q2 = q_ref[...].astype(F32) * 0.125

        def compute(start, n_keys):
            k_w = k_ref[pl.ds(start, n_keys), :]
            v_w = v_ref[pl.ds(start, n_keys), :]
            s_abs = start + lax.broadcasted_iota(jnp.int32, (1, n_keys), 1)
            valid = _band_mask(variant, t_abs, s_abs)
            outs = []
            for half in (0, 1):
                hmask = (lane >= 64) if half else (lane < 64)
                qh = jnp.where(hmask, q2, 0.0)
                if shared_kv:
                    swap = (p // 2) != half
                    qh = jnp.where(swap, pltpu.roll(qh, 64, 1), qh)
                s = lax.dot_general(qh.astype(BF16), k_w, NT, preferred_element_type=F32)
                if variant == "a":
                    head = 2 * p + half
                    s = s + (-slope_ref[head]) * jnp.abs(t_abs - s_abs).astype(F32)
                elif variant == "b":
                    s = s + cq_ref[:, half:half + 1] - ck_ref[half:half + 1, pl.ds(start, n_keys)]
                else:
                    j0 = start // BLK
                    s = s + jnp.concatenate(
                        [bias_ref[half, jnp.clip(i - j0 - b, 0, 4)] for b in range(win)], axis=1)
                s = jnp.where(valid, s, NEG_INF)
                m = jnp.max(s, axis=1, keepdims=True)
                if variant == "a":
                    m = jnp.maximum(m, sink_ref[head])
                pe = jnp.exp(s - m)
                l = jnp.sum(pe, axis=1, keepdims=True)
                if variant == "a":
                    l = l + jnp.exp(sink_ref[head] - m)
                out = jnp.dot(pe.astype(BF16), v_w, preferred_element_type=F32) / l
                if shared_kv:
                    out = jnp.where(swap, pltpu.roll(out, 64, 1), out)
                outs.append(out)
                lse_ref[:, half:half + 1] = m + jnp.log(l)
            o_ref[...] = jnp.where(lane < 64, outs[0], outs[1]).astype(BF16)

        if variant == "b":
            for g in range(S // GROUP):
                pl.when(i // 4 == g)(functools.partial(compute, 0, (g + 1) * GROUP))
        else:
            start = jnp.clip(i - (win - 1), 0, nb - win) * BLK
            compute(pl.multiple_of(start, BLK), win * BLK)

    kv_col = (lambda p, i: (0, kb)) if shared_kv else (lambda p, i: (0, kb + p))
    vv_col = (lambda p, i: (0, vb)) if shared_kv else (lambda p, i: (0, vb + p))
    in_specs = [pl.BlockSpec((BLK, BLK), lambda p, i: (i, qb + p)),
                pl.BlockSpec((S, BLK), kv_col), pl.BlockSpec((S, BLK), vv_col)]
    args = [qkv, qkv, qkv]
    if variant == "a":
        in_specs += [pl.BlockSpec(memory_space=pltpu.SMEM), pl.BlockSpec(memory_space=pltpu.SMEM)]
        args += [sinks, slopes]
    elif variant == "b":
        in_specs += [pl.BlockSpec((None, BLK, 2), lambda p, i: (p, i, 0)),
                     pl.BlockSpec((None, 2, S), lambda p, i: (p, 0, 0))]
        args += [cq_col, ck_row]
    else:
        in_specs += [pl.BlockSpec((2, 5, BLK, BLK), lambda p, i: (p, 0, 0, 0))]
        args += [bias]
    return pl.pallas_call(
        body, name=name,
        out_shape=(jax.ShapeDtypeStruct((S, 512), BF16), jax.ShapeDtypeStruct((4, S, 2), F32)),
        grid=(4, nb), in_specs=in_specs,
        out_specs=(pl.BlockSpec((BLK, BLK), lambda p, i: (i, p)),
                   pl.BlockSpec((None, BLK, 2), lambda p, i: (p, i, 0))),
        compiler_params=_params(),
    )(*args)


def _attn_bwd(variant, qkv, do, lse_row, delta_row, name, *, sinks=None, slopes=None, cq_row=None,
              ck_col=None, bias_t=None):
    S = qkv.shape[0]
    nb = S // BLK
    qb, kb, vb = ATTN_COLS[variant]
    shared_kv = variant == "a"
    win = ATTN_WINDOW.get(variant)

    def body(*refs):
        if variant == "a":
            (q_ref, k_ref, v_ref, do_ref, lse_ref, dl_ref, sink_ref, slope_ref,
             dq_ref, dk_ref, dv_ref, ex_ref) = refs
        elif variant == "b":
            (q_ref, k_ref, v_ref, do_ref, lse_ref, dl_ref, cq_ref, ck_ref,
             dq_ref, dk_ref, dv_ref, ex_ref, dcq_ref) = refs
        else:
            (q_ref, k_ref, v_ref, do_ref, lse_ref, dl_ref, bias_ref,
             dq_ref, dk_ref, dv_ref, ex_ref) = refs
        p, j = pl.program_id(0), pl.program_id(1)
        lane = lax.broadcasted_iota(jnp.int32, (BLK, BLK), 1)
        s_abs = j * BLK + lax.broadcasted_iota(jnp.int32, (BLK, 1), 0)
        off_k = pl.multiple_of(j * BLK, BLK)
        k2 = k_ref[...].astype(F32)
        v2 = v_ref[...].astype(F32)
        hmasks = [(lane < 64), (lane >= 64)]
        if shared_kv:
            kv_lane = (lane >> 6) == (p // 2)
            swaps = [(p // 2) != half for half in (0, 1)]
            k_src, v_src = jnp.where(kv_lane, k2, 0.0), jnp.where(kv_lane, v2, 0.0)
            k_al = [jnp.where(swaps[h], pltpu.roll(k_src, 64, 1), k_src) for h in (0, 1)]
            v_al = [jnp.where(swaps[h], pltpu.roll(v_src, 64, 1), v_src) for h in (0, 1)]
        else:
            k_al = [jnp.where(hmasks[h], k2, 0.0) for h in (0, 1)]
            v_al = [jnp.where(hmasks[h], v2, 0.0) for h in (0, 1)]
        k_al = [(t * 0.125).astype(BF16) for t in k_al]
        v_al = [t.astype(BF16) for t in v_al]

        @pl.when(j == 0)
        def _():
            dq_ref[...] = jnp.zeros_like(dq_ref)
            if variant == "b":
                dcq_ref[...] = jnp.zeros_like(dcq_ref)
            else:
                ex_ref[...] = jnp.zeros_like(ex_ref)

        def to_kv_lanes(x, h):
            x = jnp.where(hmasks[h], x, 0.0)
            if shared_kv:
                x = jnp.where(swaps[h], pltpu.roll(x, 64, 1), x)
            return x

        def compute(start, n_q):
            q_w = q_ref[pl.ds(start, n_q), :]
            do_w = do_ref[pl.ds(start, n_q), :]
            t_abs = start + lax.broadcasted_iota(jnp.int32, (1, n_q), 1)
            valid = _band_mask(variant, t_abs, s_abs)
            dk_acc = dv_acc = None
            ds_both = []
            for half in (0, 1):
                s = lax.dot_general(k_al[half], q_w, NT, preferred_element_type=F32)
                if variant == "a":
                    s = s + (-slope_ref[2 * p + half]) * jnp.abs(t_abs - s_abs).astype(F32)
                elif variant == "b":
                    s = s + cq_ref[half:half + 1, pl.ds(start, n_q)] - ck_ref[:, half:half + 1]
                else:
                    i0 = start // BLK
                    s = s + jnp.concatenate(
                        [bias_ref[half, jnp.clip(i0 + b - j, 0, 4)] for b in range(win)], axis=1)
                pr = jnp.where(valid, jnp.exp(s - lse_ref[half:half + 1, pl.ds(start, n_q)]), 0.0)
                dp = lax.dot_general(v_al[half], do_w, NT, preferred_element_type=F32)
                ds = pr * (dp - dl_ref[half:half + 1, pl.ds(start, n_q)])
                ds16 = ds.astype(BF16)
                dv_h = to_kv_lanes(jnp.dot(pr.astype(BF16), do_w, preferred_element_type=F32), half)
                dk_h = to_kv_lanes(jnp.dot(ds16, q_w, preferred_element_type=F32) * 0.125, half)
                dv_acc = dv_h if dv_acc is None else dv_acc + dv_h
                dk_acc = dk_h if dk_acc is None else dk_acc + dk_h
                ds_both.append(ds16)
                if variant == "b":
                    ex_ref[:, half:half + 1] = -jnp.sum(ds, axis=1, keepdims=True)
                    dcq_ref[half:half + 1, pl.ds(start, n_q)] += jnp.sum(ds, axis=0, keepdims=True)
                elif variant == "c":
                    for b in range(win):
                        ex_ref[half, jnp.clip(i0 + b - j, 0, 4)] += ds[:, b * BLK:(b + 1) * BLK]
            dq_t = lax.dot_general(jnp.concatenate(k_al, axis=0), jnp.concatenate(ds_both, axis=0), TN,
                                   preferred_element_type=F32)
            dq_ref[:, pl.ds(start, n_q)] += dq_t
            if shared_kv:
                @pl.when(p == 0)
                def _():
                    dk_ref[pl.ds(off_k, BLK), :] = dk_acc
                    dv_ref[pl.ds(off_k, BLK), :] = dv_acc

                @pl.when(p > 0)
                def _():
                    dk_ref[pl.ds(off_k, BLK), :] += dk_acc
                    dv_ref[pl.ds(off_k, BLK), :] += dv_acc
            else:
                dk_ref[pl.ds(off_k, BLK), :] = dk_acc
                dv_ref[pl.ds(off_k, BLK), :] = dv_acc

        if variant == "b":
            for g in range(S // GROUP):
                pl.when(j // 4 == g)(functools.partial(compute, g * GROUP, S - g * GROUP))
        else:
            start = jnp.clip(j, 0, nb - win) * BLK
            compute(pl.multiple_of(start, BLK), win * BLK)

        if variant == "a":
            for half in (0, 1):
                p_sink = jnp.exp(sink_ref[2 * p + half] - lse_ref[half:half + 1, pl.ds(off_k, BLK)])
                term = p_sink * dl_ref[half:half + 1, pl.ds(off_k, BLK)]
                ex_ref[half:half + 1, :] += -jnp.sum(term, axis=1, keepdims=True)

    col = lambda c0: (lambda p, j: (0, c0 + p))
    kv_blk = (lambda c0: (lambda p, j: (j, c0))) if shared_kv else (lambda c0: (lambda p, j: (j, c0 + p)))
    pair = lambda p, j: (0, p)
    row_stat = pl.BlockSpec((None, 2, S), lambda p, j: (p, 0, 0))
    in_specs = [pl.BlockSpec((S, BLK), col(qb)),
                pl.BlockSpec((BLK, BLK), kv_blk(kb)), pl.BlockSpec((BLK, BLK), kv_blk(vb)),
                pl.BlockSpec((S, BLK), pair), row_stat, row_stat]
    args = [qkv, qkv, qkv, do, lse_row, delta_row]
    kv_width = BLK if shared_kv else 512
    kv_out = pl.BlockSpec((S, BLK), (lambda p, j: (0, 0)) if shared_kv else pair)
    out_shape = [jax.ShapeDtypeStruct((512, S), F32), jax.ShapeDtypeStruct((S, kv_width), F32),
                 jax.ShapeDtypeStruct((S, kv_width), F32)]
    out_specs = [pl.BlockSpec((BLK, S), lambda p, j: (p, 0)), kv_out, kv_out]
    if variant == "a":
        in_specs += [pl.BlockSpec(memory_space=pltpu.SMEM), pl.BlockSpec(memory_space=pltpu.SMEM)]
        args += [sinks, slopes]
        out_shape.append(jax.ShapeDtypeStruct((4, 8, BLK), F32))
        out_specs.append(pl.BlockSpec((None, 8, BLK), lambda p, j: (p, 0, 0)))
    elif variant == "b":
        in_specs += [row_stat, pl.BlockSpec((None, BLK, 2), lambda p, j: (p, j, 0))]
        args += [cq_row, ck_col]
        out_shape += [jax.ShapeDtypeStruct((4, S, 2), F32), jax.ShapeDtypeStruct((4, 2, S), F32)]
        out_specs += [pl.BlockSpec((None, BLK, 2), lambda p, j: (p, j, 0)), row_stat]
    else:
        in_specs += [pl.BlockSpec((2, 5, BLK, BLK), lambda p, j: (p, 0, 0, 0))]
        args += [bias_t]
        out_shape.append(jax.ShapeDtypeStruct((8, 5, BLK, BLK), F32))
        out_specs.append(pl.BlockSpec((2, 5, BLK, BLK), lambda p, j: (p, 0, 0, 0)))
    return pl.pallas_call(
        body, name=name, out_shape=tuple(out_shape), grid=(4, nb),
        in_specs=in_specs, out_specs=tuple(out_specs),
        compiler_params=_params(),
    )(*args)


def _log_sigmoid(x):
    return jnp.minimum(x, 0.0) - jnp.log(1.0 + jnp.exp(-jnp.abs(x)))


def _forget_fwd(fb, b_forget, name):
    S = fb.shape[0]
    nb = S // BLK

    def body(fb_ref, b_ref, cum_ref, carry_ref):
        i = pl.program_id(0)
        logf = _log_sigmoid(fb_ref[...] + b_ref[...])
        r = lax.broadcasted_iota(jnp.int32, (BLK, BLK), 0)
        c = lax.broadcasted_iota(jnp.int32, (BLK, BLK), 1)
        tri = (c <= r).astype(F32)

        @pl.when(i == 0)
        def _():
            carry_ref[...] = jnp.zeros_like(carry_ref)

        cum = jnp.dot(tri, logf, preferred_element_type=F32, precision=HIGHEST) + carry_ref[0:1, :]
        cum_ref[...] = cum
        carry_ref[...] = jnp.broadcast_to(cum[BLK - 1:BLK, :], carry_ref.shape)

    return pl.pallas_call(
        body, name=name, out_shape=jax.ShapeDtypeStruct((S, BLK), F32), grid=(nb,),
        in_specs=[pl.BlockSpec((BLK, BLK), lambda i: (i, 0)), pl.BlockSpec((1, BLK), lambda i: (0, 0))],
        out_specs=pl.BlockSpec((BLK, BLK), lambda i: (i, 0)),
        scratch_shapes=[pltpu.VMEM((8, BLK), F32)],
        compiler_params=_params(),
    )(fb, b_forget)


def _forget_bwd(dcum_q, dcum_k, fb, b_forget, name):
    S = fb.shape[0]
    nb = S // BLK

    def body(dq_ref, dk_ref, fb_ref, b_ref, dfb_ref, db_ref, carry_ref):
        g = pl.program_id(0)
        r = lax.broadcasted_iota(jnp.int32, (BLK, BLK), 0)
        c = lax.broadcasted_iota(jnp.int32, (BLK, BLK), 1)
        tri = (c >= r).astype(F32)

        @pl.when(g == 0)
        def _():
            carry_ref[...] = jnp.zeros_like(carry_ref)

        dcum = dq_ref[...] + dk_ref[...]
        dlogf = jnp.dot(tri, dcum, preferred_element_type=F32, precision=HIGHEST) + carry_ref[0:1, :]
        carry_ref[...] = jnp.broadcast_to(dlogf[0:1, :], carry_ref.shape)
        x = fb_ref[...] + b_ref[...]
        dfb = jnp.where(c < N_FORGET, dlogf * jax.nn.sigmoid(-x), 0.0)
        dfb_ref[...] = dfb
        db = jnp.sum(dfb, axis=0, keepdims=True)

        @pl.when(g == 0)
        def _():
            db_ref[...] = db

        @pl.when(g > 0)
        def _():
            db_ref[...] += db

    rev = pl.BlockSpec((BLK, BLK), lambda g: (nb - 1 - g, 0))
    row = pl.BlockSpec((1, BLK), lambda g: (0, 0))
    return pl.pallas_call(
        body, name=name,
        out_shape=(jax.ShapeDtypeStruct((S, BLK), F32), jax.ShapeDtypeStruct((1, BLK), F32)), grid=(nb,),
        in_specs=[rev, rev, rev, row], out_specs=(rev, row),
        scratch_shapes=[pltpu.VMEM((8, BLK), F32)],
        compiler_params=_params(),
    )(dcum_q, dcum_k, fb, b_forget)


def _skew(x, sign):
    row = lax.broadcasted_iota(jnp.int32, x.shape, 0)
    for b in range(7):
        amount = (1 << b) if sign > 0 else 256 - (1 << b)
        x = jnp.where(((row >> b) & 1) == 1, pltpu.roll(x, amount, 1), x)
    return x


def _rel_bases(rel):
    far = rel[:, 256:257]
    far127 = jnp.broadcast_to(far, (rel.shape[0], 127))
    base0 = jnp.concatenate([rel[:, 128:0:-1], far, rel[:, 255:128:-1]], axis=1)
    base1 = jnp.concatenate([rel[:, 256:128:-1], far, far127], axis=1)
    base0_t = jnp.concatenate([rel[:, 128:256], far, rel[:, 1:128]], axis=1)
    base1_t = jnp.concatenate([jnp.broadcast_to(far, (rel.shape[0], 128)), far, rel[:, 129:256]], axis=1)
    return jnp.stack([base0, base1, base0_t, base1_t], axis=1)


def _rel_expand(bases, name):
    def body(b_ref, t_ref, tt_ref):
        far = jnp.broadcast_to(b_ref[1:2, 0:1], (BLK, BLK))
        for k, out_ref in ((0, t_ref), (2, tt_ref)):
            for d in (0, 1):
                x = jnp.broadcast_to(b_ref[k + d:k + d + 1, :], (BLK, 2 * BLK))
                out_ref[d] = _skew(x, 1)[:, :BLK]
            for d in (2, 3, 4):
                out_ref[d] = far

    out = jax.ShapeDtypeStruct((8, 5, BLK, BLK), F32)
    spec = pl.BlockSpec((None, 5, BLK, BLK), lambda h: (h, 0, 0, 0))
    return pl.pallas_call(
        body, name=name, out_shape=(out, out), grid=(8,),
        in_specs=[pl.BlockSpec((None, 4, 2 * BLK), lambda h: (h, 0, 0))], out_specs=(spec, spec),
        compiler_params=_params(),
    )(bases)


def _rel_reduce(dtiles_t, name):
    def body(dt_ref, o_ref):
        zeros = jnp.zeros((BLK, BLK), F32)
        sums = []
        for d in (0, 1):
            x = _skew(jnp.concatenate([dt_ref[d], zeros], axis=1), -1)
            sums.append(jnp.broadcast_to(jnp.sum(x, axis=0, keepdims=True), (8, 2 * BLK)))
        lane = lax.broadcasted_iota(jnp.int32, (8, 2 * BLK), 1)
        main = pltpu.roll(sums[0], BLK, 1) + jnp.where(lane > BLK, sums[1], 0.0)
        far = jnp.sum(jnp.where(lane < BLK, sums[1], 0.0)[0:1], axis=1, keepdims=True)
        far = far + jnp.sum(jnp.sum(dt_ref[2] + dt_ref[3] + dt_ref[4], axis=0, keepdims=True), axis=1, keepdims=True)
        o_ref[...] = jnp.concatenate([main[0:1], jnp.broadcast_to(far, (1, BLK))], axis=1)

    return pl.pallas_call(
        body, name=name, out_shape=jax.ShapeDtypeStruct((8, 1, 3 * BLK), F32), grid=(8,),
        in_specs=[pl.BlockSpec((None, 5, BLK, BLK), lambda h: (h, 0, 0, 0))],
        out_specs=pl.BlockSpec((None, 1, 3 * BLK), lambda h: (h, 0, 0)),
        compiler_params=_params(),
    )(dtiles_t)


def _final_loss(x, target, g, name):
    S, D = x.shape
    ts = _row_tile(S, 256)

    def body(x_ref, t_ref, g_ref, dx_ref, loss_ref, dg_ref):
        i = pl.program_id(0)
        xv, gv = x_ref[...], g_ref[...]
        rstd = lax.rsqrt(jnp.mean(xv * xv, axis=-1, keepdims=True) + EPS)
        xhat = xv * rstd
        err = xhat * gv - t_ref[...]
        part = 0.5 * jnp.sum(jnp.mean(err * err, axis=-1, keepdims=True), axis=0, keepdims=True)
        dy = err / D
        dg = jnp.sum(dy * xhat, axis=0, keepdims=True)
        dxhat = dy * gv
        proj = jnp.mean(dxhat * xhat, axis=-1, keepdims=True)
        dx_ref[...] = rstd * (dxhat - xhat * proj)

        @pl.when(i == 0)
        def _():
            loss_ref[...] = jnp.broadcast_to(part, loss_ref.shape)
            dg_ref[...] = dg

        @pl.when(i > 0)
        def _():
            loss_ref[...] += jnp.broadcast_to(part, loss_ref.shape)
            dg_ref[...] += dg

    tile = pl.BlockSpec((ts, D), lambda i: (i, 0))
    row = pl.BlockSpec((1, D), lambda i: (0, 0))
    return pl.pallas_call(
        body, name=name,
        out_shape=(jax.ShapeDtypeStruct((S, D), F32), jax.ShapeDtypeStruct((8, 128), F32),
                   jax.ShapeDtypeStruct((1, D), F32)),
        grid=(S // ts,), in_specs=[tile, tile, row],
        out_specs=(tile, pl.BlockSpec((8, 128), lambda i: (0, 0)), row),
        compiler_params=_params(),
    )(x, target, g)


def _ada_fwd(c_all, w_ada, name):
    L, D, E = w_ada.shape

    def body(c_ref, w_ref, o_ref):
        cv = c_ref[...]
        cond = cv * jax.nn.sigmoid(cv)
        o_ref[...] = jnp.dot(cond, w_ref[...], preferred_element_type=F32, precision=HIGHEST)

    return pl.pallas_call(
        body, name=name, out_shape=jax.ShapeDtypeStruct((L, N_DEV, E), F32), grid=(L,),
        in_specs=[pl.BlockSpec((N_DEV, D), lambda l: (0, 0)), pl.BlockSpec((None, D, E), lambda l: (l, 0, 0))],
        out_specs=pl.BlockSpec((None, N_DEV, E), lambda l: (l, 0, 0)),
        compiler_params=_params(),
    )(c_all, w_ada)


def _ada_bwd(c_all_t, dmod, name):
    D = c_all_t.shape[0]
    L, _, E = dmod.shape

    def body(c_ref, d_ref, o_ref):
        cv = c_ref[...]
        cond = cv * jax.nn.sigmoid(cv)
        acc = None
        for b in range(N_DEV):
            t = cond[:, b:b + 1] * d_ref[b:b + 1, :]
            acc = t if acc is None else acc + t
        o_ref[...] = acc

    return pl.pallas_call(
        body, name=name, out_shape=jax.ShapeDtypeStruct((L, D, E), F32), grid=(L,),
        in_specs=[pl.BlockSpec((D, N_DEV), lambda l: (0, 0)), pl.BlockSpec((None, N_DEV, E), lambda l: (l, 0, 0))],
        out_specs=pl.BlockSpec((None, D, E), lambda l: (l, 0, 0)),
        compiler_params=_params(),
    )(c_all_t, dmod)


def _adamw(w, m, v, g_parts, name):
    R, C = w.shape
    P = g_parts.shape[0]
    tr = _row_tile(R, max(8, (256 * 1024 // max(C, 128)) // 8 * 8))
    c1 = 1.0 - ADAM_B1 ** ADAM_STEP
    c2 = 1.0 - ADAM_B2 ** ADAM_STEP

    def body(w_ref, m_ref, v_ref, g_ref, go_ref, d_ref, mo_ref, vo_ref):
        g = g_ref[0].astype(F32)
        for k in range(1, P):
            g = g + g_ref[k].astype(F32)
        mn = ADAM_B1 * m_ref[...] + (1.0 - ADAM_B1) * g
        vn = ADAM_B2 * v_ref[...] + (1.0 - ADAM_B2) * (g * g)
        m_hat = mn / c1
        v_hat = vn / c2
        go_ref[...] = g
        d_ref[...] = -ADAM_LR * (m_hat / (jnp.sqrt(v_hat) + ADAM_EPS) + ADAM_WD * w_ref[...])
        mo_ref[...] = mn
        vo_ref[...] = vn

    tile = pl.BlockSpec((tr, C), lambda i: (i, 0))
    out = jax.ShapeDtypeStruct((R, C), F32)
    return pl.pallas_call(
        body, name=name, out_shape=(out, out, out, out), grid=(R // tr,),
        in_specs=[tile, tile, tile, pl.BlockSpec((P, tr, C), lambda i: (0, i, 0))],
        out_specs=(tile, tile, tile, tile),
        compiler_params=_params(),
    )(w, m, v, g_parts)


def _pair_add(pieces, recv, core, name):
    _, _, R, C = pieces.shape
    tr = _row_tile(R, max(8, (512 * 1024 // max(C, 128)) // 8 * 8))

    def body(core_ref, a_ref, b_ref, o_ref):
        o_ref[...] = (a_ref[...].astype(F32) + b_ref[...].astype(F32)).astype(BF16)

    return pl.pallas_call(
        body, name=name, out_shape=jax.ShapeDtypeStruct((4, R, C), BF16),
        grid_spec=pltpu.PrefetchScalarGridSpec(
            num_scalar_prefetch=1, grid=(4, R // tr),
            in_specs=[pl.BlockSpec((None, None, tr, C), lambda k, i, core_ref: (core_ref[0], k, i, 0)),
                      pl.BlockSpec((None, tr, C), lambda k, i, core_ref: (k, i, 0))],
            out_specs=pl.BlockSpec((None, tr, C), lambda k, i, core_ref: (k, i, 0))),
        compiler_params=_params(),
    )(core, pieces, recv)


MESH = pl.DeviceIdType.MESH
ANY = pl.BlockSpec(memory_space=pl.ANY)


def _position():
    return lax.axis_index("x"), lax.axis_index("y"), lax.axis_index("c")


def _small_all_gather(v, name):
    m_per, n = v.shape

    def body(x_ref, out_ref, send_sems, recv_sems, local_sem):
        x, y, c = _position()
        me, sibling = (x, y, c), (x, y, 1 - c)
        chips = [(1 - x, y), (x, 1 - y), (1 - x, 1 - y)]

        def rows(px, py, pc):
            return out_ref.at[pl.ds((4 * px + 2 * py + pc) * m_per, m_per), :]

        def copy(k, block, to, src=None):
            return pltpu.make_async_remote_copy(
                src_ref=rows(*block) if src is None else src, dst_ref=rows(*block),
                send_sem=send_sems.at[k], recv_sem=recv_sems.at[k], device_id=to, device_id_type=MESH)

        mine = pltpu.make_async_copy(x_ref, rows(*me), local_sem)
        mine.start()
        first = [copy(0, me, sibling, src=x_ref)]
        first += [copy(1 + j, me, (*chip, c), src=x_ref) for j, chip in enumerate(chips)]
        for cp in first:
            cp.start()
        passed = [copy(4 + j, (*chip, c), sibling) for j, chip in enumerate(chips)]
        for j, chip in enumerate(chips):
            copy(1 + j, (*chip, c), me).wait_recv()
            passed[j].start()
        copy(0, sibling, me).wait_recv()
        for j, chip in enumerate(chips):
            copy(4 + j, (*chip, 1 - c), me).wait_recv()
        for cp in first + passed:
            cp.wait_send()
        mine.wait()

    return pl.pallas_call(
        body, name=name, out_shape=jax.ShapeDtypeStruct((N_DEV * m_per, n), v.dtype),
        in_specs=[pl.BlockSpec(memory_space=pltpu.VMEM)], out_specs=pl.BlockSpec(memory_space=pltpu.VMEM),
        scratch_shapes=[pltpu.SemaphoreType.DMA((7,)), pltpu.SemaphoreType.DMA((7,)), pltpu.SemaphoreType.DMA],
    )(v)


def _big_all_gather(shards, name):
    n_arr = len(shards)

    def body(*refs):
        x_refs, out_refs = refs[:n_arr], refs[n_arr:2 * n_arr]
        send_sems, recv_sems, local_sems = refs[2 * n_arr:]
        x, y, c = _position()
        me, sibling = (x, y, c), (x, y, 1 - c)
        chips = [(1 - x, y), (x, 1 - y), (1 - x, 1 - y)]

        def slot(a, px, py, pc):
            return out_refs[a].at[4 * px + 2 * py + pc]

        def copy(a, k, block, to, src=None):
            return pltpu.make_async_remote_copy(
                src_ref=slot(a, *block) if src is None else src, dst_ref=slot(a, *block),
                send_sem=send_sems.at[a, k], recv_sem=recv_sems.at[a, k], device_id=to, device_id_type=MESH)

        mine = [pltpu.make_async_copy(x_refs[a], slot(a, *me), local_sems.at[a]) for a in range(n_arr)]
        for cp in mine:
            cp.start()
        first = []
        for j, chip in enumerate(chips):
            first += [copy(a, 1 + j, me, (*chip, c), src=x_refs[a]) for a in range(n_arr)]
        first += [copy(a, 0, me, sibling, src=x_refs[a]) for a in range(n_arr)]
        for cp in first:
            cp.start()
        passed = []
        for j, chip in enumerate(chips):
            for a in range(n_arr):
                copy(a, 1 + j, (*chip, c), me).wait_recv()
                fwd = copy(a, 4 + j, (*chip, c), sibling)
                fwd.start()
                passed.append(fwd)
        for a in range(n_arr):
            copy(a, 0, sibling, me).wait_recv()
        for j, chip in enumerate(chips):
            for a in range(n_arr):
                copy(a, 4 + j, (*chip, 1 - c), me).wait_recv()
        for cp in first + passed:
            cp.wait_send()
        for cp in mine:
            cp.wait()

    return pl.pallas_call(
        body, name=name,
        out_shape=tuple(jax.ShapeDtypeStruct((N_DEV,) + s.shape, s.dtype) for s in shards),
        in_specs=[ANY] * n_arr, out_specs=tuple([ANY] * n_arr),
        scratch_shapes=[pltpu.SemaphoreType.DMA((n_arr, 7)), pltpu.SemaphoreType.DMA((n_arr, 7)),
                        pltpu.SemaphoreType.DMA((n_arr,))],
    )(*shards)


def _sibling_exchange(pieces, name):
    n_arr = len(pieces)

    def body(*refs):
        p_refs, out_refs = refs[:n_arr], refs[n_arr:2 * n_arr]
        send_sems, recv_sems = refs[2 * n_arr:]
        x, y, c = _position()
        copies = [pltpu.make_async_remote_copy(
            src_ref=p_refs[a].at[1 - c], dst_ref=out_refs[a], send_sem=send_sems.at[a], recv_sem=recv_sems.at[a],
            device_id=(x, y, 1 - c), device_id_type=MESH) for a in range(n_arr)]
        for cp in copies:
            cp.start()
        for cp in copies:
            cp.wait()

    return pl.pallas_call(
        body, name=name,
        out_shape=tuple(jax.ShapeDtypeStruct(p.shape[1:], p.dtype) for p in pieces),
        in_specs=[ANY] * n_arr, out_specs=tuple([ANY] * n_arr),
        scratch_shapes=[pltpu.SemaphoreType.DMA((n_arr,)), pltpu.SemaphoreType.DMA((n_arr,))],
    )(*pieces)


def _chip_exchange(sums, name):
    n_arr = len(sums)

    def body(*refs):
        s_refs, out_refs = refs[:n_arr], refs[n_arr:2 * n_arr]
        send_sems, recv_sems, local_sems = refs[2 * n_arr:]
        x, y, c = _position()
        my_chip = 2 * x + y
        chips = [(1 - x, y), (x, 1 - y), (1 - x, 1 - y)]
        mine = [pltpu.make_async_copy(s_refs[a].at[my_chip], out_refs[a].at[my_chip], local_sems.at[a])
                for a in range(n_arr)]
        for cp in mine:
            cp.start()
        copies = []
        for j, (px, py) in enumerate(chips):
            copies += [pltpu.make_async_remote_copy(
                src_ref=s_refs[a].at[2 * px + py], dst_ref=out_refs[a].at[my_chip],
                send_sem=send_sems.at[a, j], recv_sem=recv_sems.at[a, j],
                device_id=(px, py, c), device_id_type=MESH) for a in range(n_arr)]
        for cp in copies:
            cp.start()
        for j, (px, py) in enumerate(chips):
            for a in range(n_arr):
                pltpu.make_async_remote_copy(
                    src_ref=s_refs[a].at[my_chip], dst_ref=out_refs[a].at[2 * px + py],
                    send_sem=send_sems.at[a, j], recv_sem=recv_sems.at[a, j],
                    device_id=(px, py, c), device_id_type=MESH).wait_recv()
        for cp in copies:
            cp.wait_send()
        for cp in mine:
            cp.wait()

    return pl.pallas_call(
        body, name=name,
        out_shape=tuple(jax.ShapeDtypeStruct(s.shape, s.dtype) for s in sums),
        in_specs=[ANY] * n_arr, out_specs=tuple([ANY] * n_arr),
        scratch_shapes=[pltpu.SemaphoreType.DMA((n_arr, 3)), pltpu.SemaphoreType.DMA((n_arr, 3)),
                        pltpu.SemaphoreType.DMA((n_arr,))],
    )(*sums)


def _cols_from_shards(g):
    return jnp.transpose(g, (1, 0, 2)).reshape(g.shape[1], N_DEV * g.shape[2])


def _w_in_rearranged(g):
    w = _cols_from_shards(g)
    return jnp.concatenate([w[:, :F_COL], w[:, F_COL + N_FORGET:], w[:, F_COL:F_COL + N_FORGET],
                            jnp.zeros((w.shape[0], BLK - N_FORGET), w.dtype)], axis=1)


def _w_in_original(dw_r):
    return jnp.concatenate([dw_r[:, :F_COL], dw_r[:, N_MAIN:N_MAIN + N_FORGET], dw_r[:, F_COL:N_MAIN]], axis=1)


def _col_pieces(dw):
    r = dw.shape[0]
    return jnp.transpose(dw.reshape(r, 4, 2, dw.shape[1] // N_DEV), (2, 1, 0, 3))


def _row_pieces(dw):
    return jnp.transpose(dw.reshape(4, 2, dw.shape[0] // N_DEV, dw.shape[1]), (1, 0, 2, 3))


def _pairs_col(a):
    return jnp.transpose(a.reshape(a.shape[0], 4, 2), (1, 0, 2))


def _pairs_row(a):
    return jnp.transpose(a.reshape(a.shape[0], 4, 2), (1, 2, 0))


def _heads_from_col(a):
    return jnp.transpose(a, (1, 0, 2)).reshape(a.shape[1], 8)


def _heads_from_row(a):
    return jnp.transpose(a, (2, 0, 1)).reshape(a.shape[2], 8)


def _pad_lanes(a, n):
    return jnp.pad(a, [(0, 0)] * (a.ndim - 1) + [(0, n - a.shape[-1])])


SMALL_SEGMENTS = (("dmod", 2 * 6 * D_MODEL), ("norm_mix_g", 2 * D_MODEL), ("norm_ffn_g", 2 * D_MODEL),
                  ("final_norm_g", D_MODEL), ("b_forget", 128), ("sinks", 128), ("rel_bias", 4224))
SMALL_ROWS = 176


def _pack_small(parts):
    flat = [_pad_lanes(parts[name].reshape(1, -1), size) for name, size in SMALL_SEGMENTS]
    total = sum(size for _, size in SMALL_SEGMENTS)
    flat.append(jnp.zeros((1, SMALL_ROWS * 128 - total), F32))
    return jnp.concatenate(flat, axis=1).reshape(SMALL_ROWS, 128)


def _unpack_small(packed, shapes):
    flat = packed.reshape(-1)
    out, pos = {}, 0
    for name, size in SMALL_SEGMENTS:
        shape = shapes[name]
        count = 1
        for d in shape:
            count *= d
        out[name] = flat[pos:pos + count].reshape(shape)
        pos += size
    return out


def kernel(x, c, norm_mix_g, norm_ffn_g, w_ada, b_ada, w_in, b_forget, sinks, rel_bias, w_branch, w_out, w_ffn_in, w_ffn_out, final_norm_g, loss_target, m_norm_mix_g, m_norm_ffn_g, m_w_ada, m_b_ada, m_w_in, m_b_forget, m_sinks, m_rel_bias, m_w_branch, m_w_out, m_w_ffn_in, m_w_ffn_out, m_final_norm_g, v_norm_mix_g, v_norm_ffn_g, v_w_ada, v_b_ada, v_w_in, v_b_forget, v_sinks, v_rel_bias, v_w_branch, v_w_out, v_w_ffn_in, v_w_ffn_out, v_final_norm_g):
    depth = w_in.shape[0]
    S, D = x.shape[1], x.shape[2]
    assert S % GROUP == 0 and S >= ATTN_WINDOW["c"] * BLK
    px, py, pc = _position()
    me = 4 * px + 2 * py + pc
    x0 = x[0]

    g_in, g_branch, g_out, g_fin, g_fout = _big_all_gather(
        [w_in.astype(BF16), w_branch.astype(BF16), w_out.astype(BF16), w_ffn_in.astype(BF16),
         w_ffn_out.astype(BF16)], "comm_gather_weights")
    W_in = [_w_in_rearranged(g_in[:, l]) for l in range(depth)]
    W_branch = [jnp.transpose(g_branch[:, l], (1, 2, 0, 3)).reshape(3, 512, D) for l in range(depth)]
    W_out = [g_out[:, l].reshape(D, D) for l in range(depth)]
    W_fin = [_cols_from_shards(g_fin[:, l]) for l in range(depth)]
    W_fout = [g_fout[:, l].reshape(FFN_HIDDEN, D) for l in range(depth)]

    c_all = _small_all_gather(c.reshape(8, 128), "comm_gather_c").reshape(N_DEV, D)
    mod_cols = _ada_fwd(c_all, w_ada, "ada_fwd")
    mod_all = _small_all_gather(mod_cols.reshape(-1, 128), "comm_gather_mod")
    mod_all = mod_all.reshape(N_DEV, depth, N_DEV, w_ada.shape[2])
    mod_mine = lax.dynamic_index_in_dim(mod_all, me, axis=2, keepdims=False)
    mod = jnp.transpose(mod_mine, (1, 0, 2)).reshape(depth, 6 * D) + b_ada
    mods = [[mod[l:l + 1, k * D:(k + 1) * D] for k in range(6)] for l in range(depth)]

    slopes = jnp.exp2(-jnp.arange(1, 9, dtype=F32))
    saved = []
    xs = x0
    for l in range(depth):
        sh_m, sc_m, g_m, sh_f, sc_f, g_f = mods[l]
        gm, gf = norm_mix_g[l:l + 1], norm_ffn_g[l:l + 1]
        bfor = _pad_lanes(b_forget[l:l + 1], BLK)
        h = _norm_mod_fwd(xs, gm, sh_m, sc_m, f"norm_mix_fwd{l}")
        qkv = _matmul(h, W_in[l], "nn", BF16, f"proj_qkv{l}", n=N_QKV, tn=768)
        gates = _matmul(h, W_in[l], "nn", F32, f"proj_gates{l}", n=N_GATES, tn=768, b_off=N_QKV // 768)
        fb = _matmul(h, W_in[l], "nn", F32, f"proj_forget{l}", n=BLK, tn=BLK, b_off=N_MAIN // BLK)
        cum = _forget_fwd(fb, bfor, f"forget_fwd{l}")[:, :N_FORGET]
        cum_col, cum_row = _pairs_col(cum), _pairs_row(cum)
        tiles, tiles_t = _rel_expand(_rel_bases(rel_bias[l]), f"rel_expand{l}")
        o_a, lse_a = _attn_fwd("a", qkv, f"attn_a_fwd{l}", sinks=sinks[l], slopes=slopes)
        o_b, lse_b = _attn_fwd("b", qkv, f"attn_b_fwd{l}", cq_col=cum_col, ck_row=cum_row)
        o_c, lse_c = _attn_fwd("c", qkv, f"attn_c_fwd{l}", bias=tiles)
        merged = _merge_fwd(o_a, o_b, o_c, gates, W_branch[l], f"merge_fwd{l}")
        x1, mix = _matmul_resid(merged, W_out[l], xs, g_m, f"out_proj{l}")
        h2 = _norm_mod_fwd(x1, gf, sh_f, sc_f, f"norm_ffn_fwd{l}")
        act = _ffn_in_fwd(h2, W_fin[l], f"ffn_in_fwd{l}")
        x2, ffn = _matmul_resid(act, W_fout[l], x1, g_f, f"ffn_out{l}")
        saved.append(dict(x=xs, h=h, qkv=qkv, gates=gates, fb=fb, bfor=bfor, cum_col=cum_col, cum_row=cum_row,
                          tiles_t=tiles_t, o=(o_a, o_b, o_c), lse=(lse_a, lse_b, lse_c), merged=merged, mix=mix,
                          x1=x1, h2=h2, act=act, ffn=ffn))
        xs = x2

    dx, loss_tile, d_final_g = _final_loss(xs, loss_target[0], final_norm_g.reshape(1, D), "final_loss")
    loss = lax.psum(loss_tile[0, 0], ("x", "y", "c"))

    grads = {k: [None] * depth for k in ("w_in", "w_branch", "w_out", "w_ffn_in", "w_ffn_out", "norm_mix_g",
                                          "norm_ffn_g", "b_forget", "sinks", "rel_bias", "dmod")}
    for l in reversed(range(depth)):
        sv = saved[l]
        sh_m, sc_m, g_m, sh_f, sc_f, g_f = mods[l]
        gm, gf = norm_mix_g[l:l + 1], norm_ffn_g[l:l + 1]
        df, d_g_f = _gate_bwd(dx, sv["ffn"], g_f, f"ffn_gate_bwd{l}")
        du_g, du_u = _ffn_mid_bwd(sv["h2"], df, W_fin[l], W_fout[l], f"ffn_mid_bwd{l}")
        du = jnp.concatenate([du_g, du_u], axis=1)
        grads["w_ffn_out"][l] = _matmul(sv["act"], df, "tn", BF16, f"wgrad_ffn_out{l}", tm=1408, tn=512, tk=512)
        grads["w_ffn_in"][l] = _matmul(sv["h2"], du, "tn", BF16, f"wgrad_ffn_in{l}", tm=512, tn=1408, tk=512)
        dh2 = _matmul(du, W_fin[l], "nt", F32, f"dgrad_ffn_in{l}", tn=512)
        dx1, d_sh_f, d_sc_f, d_gf = _norm_mod_bwd(sv["x1"], dh2, dx, gf, sc_f, f"norm_ffn_bwd{l}")
        dmix, d_g_m = _gate_bwd(dx1, sv["mix"], g_m, f"mix_gate_bwd{l}")
        grads["w_out"][l] = _matmul(sv["merged"], dmix, "tn", BF16, f"wgrad_out{l}", tm=512, tn=1024, tk=512)
        dmerged = _matmul(dmix, W_out[l], "nt", F32, f"dgrad_out{l}", tn=512)
        o_a, o_b, o_c = sv["o"]
        dgates, dy, do_a, do_b, do_c, dl_a, dl_b, dl_c = _merge_bwd(
            dmerged, o_a, o_b, o_c, sv["gates"], W_branch[l], f"merge_bwd{l}")
        dwb = [_matmul(o_k, dy, "tn", BF16, f"wgrad_branch{l}_{k}", n=D, b_off=k * (D // 512), tm=512, tn=512, tk=512)
               for k, o_k in enumerate((o_a, o_b, o_c))]
        grads["w_branch"][l] = jnp.stack(dwb)
        lse_rows = [_pairs_row(_heads_from_col(t)) for t in sv["lse"]]
        dqt_a, dk_a, dv_a, dsink = _attn_bwd("a", sv["qkv"], do_a, lse_rows[0], _pairs_row(dl_a), f"attn_a_bwd{l}",
                                             sinks=sinks[l], slopes=slopes)
        dqt_b, dk_b, dv_b, dck, dcq = _attn_bwd("b", sv["qkv"], do_b, lse_rows[1], _pairs_row(dl_b),
                                                f"attn_b_bwd{l}", cq_row=sv["cum_row"], ck_col=sv["cum_col"])
        dqt_c, dk_c, dv_c, dtiles_t = _attn_bwd("c", sv["qkv"], do_c, lse_rows[2], _pairs_row(dl_c),
                                                f"attn_c_bwd{l}", bias_t=sv["tiles_t"])
        grads["sinks"][l] = dsink[:, :2, 0].reshape(8)
        grads["rel_bias"][l] = _rel_reduce(dtiles_t, f"rel_reduce{l}")[:, 0, :N_REL]
        dcum_k = _pad_lanes(_heads_from_col(dck), BLK)
        dcum_q = _pad_lanes(_heads_from_row(dcq), BLK)
        dfb, d_bfor = _forget_bwd(dcum_q, dcum_k, sv["fb"], sv["bfor"], f"forget_bwd{l}")
        grads["b_forget"][l] = d_bfor[0, :N_FORGET]
        dproj = jnp.concatenate(
            [t.astype(BF16) for t in (dqt_a.T, dk_a, dv_a, dqt_b.T, dk_b, dv_b, dqt_c.T, dk_c, dv_c)]
            + [dgates, dfb.astype(BF16)], axis=1)
        grads["w_in"][l] = _w_in_original(
            _matmul(sv["h"], dproj, "tn", BF16, f"wgrad_in{l}", tm=512, tn=1408, tk=512))
        dh = _matmul(dproj, W_in[l], "nt", F32, f"dgrad_in{l}", tn=512)
        dx, d_sh_m, d_sc_m, d_gm = _norm_mod_bwd(sv["x"], dh, dx1, gm, sc_m, f"norm_mix_bwd{l}")
        grads["norm_mix_g"][l] = d_gm[0]
        grads["norm_ffn_g"][l] = d_gf[0]
        grads["dmod"][l] = jnp.concatenate([d_sh_m, d_sc_m, d_g_m, d_sh_f, d_sc_f, d_g_f], axis=1)[0]

    grad_x = dx.reshape(x.shape)

    small_shapes = dict(dmod=b_ada.shape, norm_mix_g=norm_mix_g.shape, norm_ffn_g=norm_ffn_g.shape,
                        final_norm_g=final_norm_g.shape, b_forget=b_forget.shape, sinks=sinks.shape,
                        rel_bias=rel_bias.shape)
    mine_small = _pack_small(dict(
        dmod=jnp.stack(grads["dmod"]), norm_mix_g=jnp.stack(grads["norm_mix_g"]),
        norm_ffn_g=jnp.stack(grads["norm_ffn_g"]), final_norm_g=d_final_g[0],
        b_forget=_pad_lanes(jnp.stack(grads["b_forget"]).reshape(1, -1), 128),
        sinks=_pad_lanes(jnp.stack(grads["sinks"]).reshape(1, -1), 128),
        rel_bias=_pad_lanes(jnp.stack(grads["rel_bias"]).reshape(1, -1), 4224)))
    all_small = _small_all_gather(mine_small, "comm_gather_small").reshape(N_DEV, SMALL_ROWS, 128)

    def pack_params(b_ada_, nm, nf, fn, bf, sk, rb):
        return _pack_small(dict(dmod=b_ada_, norm_mix_g=nm, norm_ffn_g=nf, final_norm_g=fn,
                                b_forget=_pad_lanes(bf.reshape(1, -1), 128), sinks=_pad_lanes(sk.reshape(1, -1), 128),
                                rel_bias=_pad_lanes(rb.reshape(1, -1), 4224)))

    small_out = _adamw(
        pack_params(b_ada, norm_mix_g, norm_ffn_g, final_norm_g, b_forget, sinks, rel_bias),
        pack_params(m_b_ada, m_norm_mix_g, m_norm_ffn_g, m_final_norm_g, m_b_forget, m_sinks, m_rel_bias),
        pack_params(v_b_ada, v_norm_mix_g, v_norm_ffn_g, v_final_norm_g, v_b_forget, v_sinks, v_rel_bias),
        all_small, "adamw_small")
    small_out = [_unpack_small(t, small_shapes) for t in small_out]

    dmod_all = all_small[:, :96].reshape(N_DEV, depth, 6 * D)
    dmod_cols = lax.dynamic_slice_in_dim(dmod_all, me * w_ada.shape[2], w_ada.shape[2], axis=2)
    d_w_ada = _ada_bwd(jnp.transpose(c_all), jnp.transpose(dmod_cols, (1, 0, 2)), "ada_bwd")

    pieces = [
        jnp.concatenate([_col_pieces(g) for g in grads["w_in"]], axis=2),
        jnp.concatenate([jnp.transpose(g.reshape(3, 512, 4, 2, D // N_DEV), (3, 2, 0, 1, 4)).reshape(
            2, 4, 3 * 512, D // N_DEV) for g in grads["w_branch"]], axis=2),
        jnp.concatenate([_row_pieces(g) for g in grads["w_out"]], axis=2),
        jnp.concatenate([_col_pieces(g) for g in grads["w_ffn_in"]], axis=2),
        jnp.concatenate([_row_pieces(g) for g in grads["w_ffn_out"]], axis=2),
    ]
    from_sibling = _sibling_exchange(pieces, "comm_reduce_sibling")
    core = pc.astype(jnp.int32).reshape(1)
    pair_sums = [_pair_add(p, r, core, f"pair_add{a}") for a, (p, r) in enumerate(zip(pieces, from_sibling))]
    chip_parts = _chip_exchange(pair_sums, "comm_reduce_chips")

    big = {}
    for name, w, m, v, parts in (
            ("w_in", w_in, m_w_in, v_w_in, chip_parts[0]), ("w_branch", w_branch, m_w_branch, v_w_branch, chip_parts[1]),
            ("w_out", w_out, m_w_out, v_w_out, chip_parts[2]), ("w_ffn_in", w_ffn_in, m_w_ffn_in, v_w_ffn_in, chip_parts[3]),
            ("w_ffn_out", w_ffn_out, m_w_ffn_out, v_w_ffn_out, chip_parts[4])):
        flat = lambda t: t.reshape(-1, t.shape[-1])
        outs = _adamw(flat(w), flat(m), flat(v), parts, f"adamw_{name}")
        big[name] = [t.reshape(w.shape) for t in outs]
    flat = lambda t: t.reshape(-1, t.shape[-1])
    big["w_ada"] = [t.reshape(w_ada.shape) for t in
                    _adamw(flat(w_ada), flat(m_w_ada), flat(v_w_ada), d_w_ada.reshape(1, -1, w_ada.shape[2]), "adamw_w_ada")]

    def leaf(kind, name):
        if name in big:
            return big[name][kind]
        return small_out[kind]["dmod" if name == "b_ada" else name]

    order = ["norm_mix_g", "norm_ffn_g", "w_ada", "b_ada", "w_in", "b_forget", "sinks", "rel_bias", "w_branch",
             "w_out", "w_ffn_in", "w_ffn_out", "final_norm_g"]
    return (loss, grad_x, *[leaf(0, n) for n in order], *[leaf(1, n) for n in order],
            *[leaf(2, n) for n in order], *[leaf(3, n) for n in order])
```

```python
import functools

import jax
import jax.numpy as jnp
from jax import lax
from jax.experimental import pallas as pl
from jax.experimental.pallas import tpu as pltpu

F32 = jnp.float32
BF16 = jnp.bfloat16
NEG_INF = -1e30
EPS = 1e-6
N_DEV = 8
BLK = 128
GROUP = 4 * BLK
VMEM_LIMIT_BYTES = 56 * 1024 * 1024

D_MODEL = 1024
N_QKV = 3840
N_GATES = 3072
N_MAIN = N_QKV + N_GATES
N_FORGET = 8
N_IN = N_MAIN + N_FORGET
N_INR = N_MAIN + BLK
F_COL = 2304
FFN_HIDDEN = 2816
N_REL = 257

ADAM_LR, ADAM_B1, ADAM_B2, ADAM_EPS, ADAM_WD, ADAM_STEP = 0.001, 0.9, 0.999, 1e-08, 0.01, 10

NN = (((1,), (0,)), ((), ()))
NT = (((1,), (1,)), ((), ()))
TN = (((0,), (0,)), ((), ()))
HIGHEST = lax.Precision.HIGHEST

ATTN_COLS = {"a": (0, 4, 5), "b": (6, 10, 14), "c": (18, 22, 26)}
ATTN_WINDOW = {"a": 2, "c": 5}


def _params():
    return pltpu.CompilerParams(vmem_limit_bytes=VMEM_LIMIT_BYTES)


def _tile(n, target):
    best = None
    t = 128
    while t <= min(n, target):
        if n % t == 0:
            best = t
        t += 128
    return best if best is not None else n


def _row_tile(n, target):
    t = min(n, target)
    while n % t:
        t -= 8
    return t


def _matmul(a, b, mode, out_dtype, name, *, n=None, a_off=0, b_off=0, m=None, tm=512, tn=768, tk=1408):
    if mode == "nn":
        M, K = a.shape if m is None else (m, a.shape[1])
        N = b.shape[1] if n is None else n
    elif mode == "nt":
        M, K = a.shape
        N = b.shape[0] if n is None else n
    else:
        K = a.shape[0]
        M = a.shape[1] if m is None else m
        N = b.shape[1] if n is None else n
    tm = _tile(M, tm) if M % 128 == 0 else M
    tn = _tile(N, tn)
    tk = _tile(K, tk)
    nk = K // tk
    dims = {"nn": NN, "nt": NT, "tn": TN}[mode]
    if mode == "nn":
        a_spec = pl.BlockSpec((tm, tk), lambda i, j, k: (i + a_off, k))
        b_spec = pl.BlockSpec((tk, tn), lambda i, j, k: (k, j + b_off))
    elif mode == "nt":
        a_spec = pl.BlockSpec((tm, tk), lambda i, j, k: (i + a_off, k))
        b_spec = pl.BlockSpec((tn, tk), lambda i, j, k: (j + b_off, k))
    else:
        a_spec = pl.BlockSpec((tk, tm), lambda i, j, k: (k, i + a_off))
        b_spec = pl.BlockSpec((tk, tn), lambda i, j, k: (k, j + b_off))

    def body(a_ref, b_ref, o_ref, acc_ref):
        k = pl.program_id(2)
        part = lax.dot_general(a_ref[...], b_ref[...], dims, preferred_element_type=F32)
        if nk == 1:
            o_ref[...] = part.astype(o_ref.dtype)
        else:
            @pl.when(k == 0)
            def _():
                acc_ref[...] = part

            @pl.when(k > 0)
            def _():
                acc_ref[...] += part

            @pl.when(k == nk - 1)
            def _():
                o_ref[...] = acc_ref[...].astype(o_ref.dtype)

    return pl.pallas_call(
        body, name=name,
        out_shape=jax.ShapeDtypeStruct((M, N), out_dtype),
        grid=(M // tm, N // tn, nk),
        in_specs=[a_spec, b_spec],
        out_specs=pl.BlockSpec((tm, tn), lambda i, j, k: (i, j)),
        scratch_shapes=[pltpu.VMEM((tm, tn) if nk > 1 else (8, 128), F32)],
        compiler_params=_params(),
    )(a, b)


def _matmul_resid(a, b, resid, gate, name, *, tm=512, tn=512, tk=1408):
    M, K = a.shape
    N = b.shape[1]
    tm, tn, tk = _tile(M, tm), _tile(N, tn), _tile(K, tk)
    nk = K // tk

    def body(a_ref, b_ref, r_ref, g_ref, o_ref, s_ref, acc_ref):
        k = pl.program_id(2)
        part = jnp.dot(a_ref[...], b_ref[...], preferred_element_type=F32)

        def finish(acc):
            o_ref[...] = r_ref[...] + g_ref[...] * acc
            s_ref[...] = acc.astype(BF16)

        if nk == 1:
            finish(part)
        else:
            @pl.when(k == 0)
            def _():
                acc_ref[...] = part

            @pl.when(k > 0)
            def _():
                acc_ref[...] += part

            @pl.when(k == nk - 1)
            def _():
                finish(acc_ref[...])

    return pl.pallas_call(
        body, name=name,
        out_shape=(jax.ShapeDtypeStruct((M, N), F32), jax.ShapeDtypeStruct((M, N), BF16)),
        grid=(M // tm, N // tn, nk),
        in_specs=[pl.BlockSpec((tm, tk), lambda i, j, k: (i, k)),
                  pl.BlockSpec((tk, tn), lambda i, j, k: (k, j)),
                  pl.BlockSpec((tm, tn), lambda i, j, k: (i, j)),
                  pl.BlockSpec((1, tn), lambda i, j, k: (0, j))],
        out_specs=(pl.BlockSpec((tm, tn), lambda i, j, k: (i, j)),
                   pl.BlockSpec((tm, tn), lambda i, j, k: (i, j))),
        scratch_shapes=[pltpu.VMEM((tm, tn) if nk > 1 else (8, 128), F32)],
        compiler_params=_params(),
    )(a, b, resid, gate)


def _norm_mod_fwd(x, g, shift, scale, name):
    S, D = x.shape
    ts = _row_tile(S, 256)

    def body(x_ref, g_ref, sh_ref, sc_ref, h_ref):
        xv = x_ref[...]
        rstd = lax.rsqrt(jnp.mean(xv * xv, axis=-1, keepdims=True) + EPS)
        y = xv * rstd * g_ref[...]
        h_ref[...] = (y * (1.0 + sc_ref[...]) + sh_ref[...]).astype(BF16)

    row = pl.BlockSpec((1, D), lambda i: (0, 0))
    return pl.pallas_call(
        body, name=name, out_shape=jax.ShapeDtypeStruct((S, D), BF16), grid=(S // ts,),
        in_specs=[pl.BlockSpec((ts, D), lambda i: (i, 0)), row, row, row],
        out_specs=pl.BlockSpec((ts, D), lambda i: (i, 0)),
        compiler_params=_params(),
    )(x, g, shift, scale)


def _norm_mod_bwd(x, dh, dres, g, scale, name):
    S, D = x.shape
    ts = _row_tile(S, 256)

    def body(x_ref, dh_ref, dr_ref, g_ref, sc_ref, dx_ref, dsh_ref, dsc_ref, dg_ref):
        i = pl.program_id(0)
        xv, dhv, gv = x_ref[...], dh_ref[...], g_ref[...]
        rstd = lax.rsqrt(jnp.mean(xv * xv, axis=-1, keepdims=True) + EPS)
        xhat = xv * rstd
        dn = dhv * (1.0 + sc_ref[...])
        dxhat = dn * gv
        proj = jnp.mean(dxhat * xhat, axis=-1, keepdims=True)
        dx_ref[...] = dr_ref[...] + rstd * (dxhat - xhat * proj)
        dsh = jnp.sum(dhv, axis=0, keepdims=True)
        dsc = jnp.sum(dhv * (xhat * gv), axis=0, keepdims=True)
        dg = jnp.sum(dn * xhat, axis=0, keepdims=True)

        @pl.when(i == 0)
        def _():
            dsh_ref[...] = dsh
            dsc_ref[...] = dsc
            dg_ref[...] = dg

        @pl.when(i > 0)
        def _():
            dsh_ref[...] += dsh
            dsc_ref[...] += dsc
            dg_ref[...] += dg

    tile = pl.BlockSpec((ts, D), lambda i: (i, 0))
    row = pl.BlockSpec((1, D), lambda i: (0, 0))
    vec = jax.ShapeDtypeStruct((1, D), F32)
    return pl.pallas_call(
        body, name=name, out_shape=(jax.ShapeDtypeStruct((S, D), F32), vec, vec, vec), grid=(S // ts,),
        in_specs=[tile, tile, tile, row, row], out_specs=(tile, row, row, row),
        compiler_params=_params(),
    )(x, dh, dres, g, scale)


def _gate_bwd(dx, f, gate, name):
    S, D = dx.shape
    ts = _row_tile(S, 256)

    def body(dx_ref, f_ref, g_ref, df_ref, dg_ref):
        i = pl.program_id(0)
        dxv = dx_ref[...]
        df_ref[...] = (dxv * g_ref[...]).astype(BF16)
        dg = jnp.sum(dxv * f_ref[...].astype(F32), axis=0, keepdims=True)

        @pl.when(i == 0)
        def _():
            dg_ref[...] = dg

        @pl.when(i > 0)
        def _():
            dg_ref[...] += dg

    tile = pl.BlockSpec((ts, D), lambda i: (i, 0))
    row = pl.BlockSpec((1, D), lambda i: (0, 0))
    return pl.pallas_call(
        body, name=name,
        out_shape=(jax.ShapeDtypeStruct((S, D), BF16), jax.ShapeDtypeStruct((1, D), F32)), grid=(S // ts,),
        in_specs=[tile, tile, row], out_specs=(tile, row),
        compiler_params=_params(),
    )(dx, f, gate)


def _ffn_in_fwd(h, w, name, *, tm=512, tn=256):
    S, D = h.shape
    F = w.shape[1] // 2
    tm, tn = _tile(S, tm), _tile(F, tn)
    nj = F // tn

    def body(h_ref, wg_ref, wu_ref, o_ref):
        hv = h_ref[...]
        ug = jnp.dot(hv, wg_ref[...], preferred_element_type=F32)
        uu = jnp.dot(hv, wu_ref[...], preferred_element_type=F32)
        o_ref[...] = (ug * jax.nn.sigmoid(ug) * uu).astype(BF16)

    return pl.pallas_call(
        body, name=name, out_shape=jax.ShapeDtypeStruct((S, F), BF16), grid=(S // tm, nj),
        in_specs=[pl.BlockSpec((tm, D), lambda i, j: (i, 0)),
                  pl.BlockSpec((D, tn), lambda i, j: (0, j)),
                  pl.BlockSpec((D, tn), lambda i, j: (0, j + nj))],
        out_specs=pl.BlockSpec((tm, tn), lambda i, j: (i, j)),
        compiler_params=_params(),
    )(h, w, w)


def _ffn_mid_bwd(h, df, w_in, w_out, name, *, tm=512, tn=256):
    S, D = h.shape
    F = w_in.shape[1] // 2
    tm, tn = _tile(S, tm), _tile(F, tn)
    nj = F // tn

    def body(h_ref, df_ref, wg_ref, wu_ref, wo_ref, dg_ref, du_ref):
        hv = h_ref[...]
        ug = jnp.dot(hv, wg_ref[...], preferred_element_type=F32)
        uu = jnp.dot(hv, wu_ref[...], preferred_element_type=F32)
        dact = lax.dot_general(df_ref[...], wo_ref[...], NT, preferred_element_type=F32)
        sig = jax.nn.sigmoid(ug)
        dg_ref[...] = (dact * uu * (sig * (1.0 + ug * (1.0 - sig)))).astype(BF16)
        du_ref[...] = (dact * (ug * sig)).astype(BF16)

    out = jax.ShapeDtypeStruct((S, F), BF16)
    return pl.pallas_call(
        body, name=name, out_shape=(out, out), grid=(S // tm, nj),
        in_specs=[pl.BlockSpec((tm, D), lambda i, j: (i, 0)),
                  pl.BlockSpec((tm, D), lambda i, j: (i, 0)),
                  pl.BlockSpec((D, tn), lambda i, j: (0, j)),
                  pl.BlockSpec((D, tn), lambda i, j: (0, j + nj)),
                  pl.BlockSpec((tn, D), lambda i, j: (j, 0))],
        out_specs=(pl.BlockSpec((tm, tn), lambda i, j: (i, j)), pl.BlockSpec((tm, tn), lambda i, j: (i, j))),
        compiler_params=_params(),
    )(h, df, w_in, w_in, w_out)


def _merge_fwd(o_a, o_b, o_c, gates, w_branch, name, *, tm=256):
    S, W = o_a.shape
    D = w_branch.shape[2]
    tm = _row_tile(S, tm)

    def body(oa_ref, ob_ref, oc_ref, g_ref, w_ref, m_ref):
        acc = None
        for k, o_ref in enumerate((oa_ref, ob_ref, oc_ref)):
            y = jnp.dot(o_ref[...], w_ref[k], preferred_element_type=F32)
            t = jax.nn.sigmoid(g_ref[:, k * D:(k + 1) * D]) * y
            acc = t if acc is None else acc + t
        m_ref[...] = acc.astype(BF16)

    o_spec = pl.BlockSpec((tm, W), lambda i: (i, 0))
    return pl.pallas_call(
        body, name=name, out_shape=jax.ShapeDtypeStruct((S, D), BF16), grid=(S // tm,),
        in_specs=[o_spec, o_spec, o_spec, pl.BlockSpec((tm, 3 * D), lambda i: (i, 0)),
                  pl.BlockSpec((3, W, D), lambda i: (0, 0, 0))],
        out_specs=pl.BlockSpec((tm, D), lambda i: (i, 0)),
        compiler_params=_params(),
    )(o_a, o_b, o_c, gates, w_branch)


def _merge_bwd(dmerged, o_a, o_b, o_c, gates, w_branch, name, *, tm=256):
    S, W = o_a.shape
    D = w_branch.shape[2]
    tm = _row_tile(S, tm)
    n_heads = W // 64

    def body(dm_ref, oa_ref, ob_ref, oc_ref, g_ref, w_ref, dg_ref, dy_ref,
             doa_ref, dob_ref, doc_ref, dla_ref, dlb_ref, dlc_ref):
        dm = dm_ref[...]
        branches = ((oa_ref, doa_ref, dla_ref), (ob_ref, dob_ref, dlb_ref), (oc_ref, doc_ref, dlc_ref))
        for k, (o_ref, do_ref, dl_ref) in enumerate(branches):
            wk = w_ref[k]
            ov = o_ref[...]
            y = jnp.dot(ov, wk, preferred_element_type=F32)
            g = jax.nn.sigmoid(g_ref[:, k * D:(k + 1) * D])
            dy = (dm * g).astype(BF16)
            dy_ref[:, k * D:(k + 1) * D] = dy
            dg_ref[:, k * D:(k + 1) * D] = (dm * y * (g * (1.0 - g))).astype(BF16)
            do16 = lax.dot_general(dy, wk, NT, preferred_element_type=F32).astype(BF16)
            do_ref[...] = do16
            prod = do16.astype(F32) * ov.astype(F32)
            for h in range(n_heads):
                dl_ref[:, h:h + 1] = jnp.sum(prod[:, 64 * h:64 * (h + 1)], axis=1, keepdims=True)

    o_spec = pl.BlockSpec((tm, W), lambda i: (i, 0))
    wide = pl.BlockSpec((tm, 3 * D), lambda i: (i, 0))
    dl_spec = pl.BlockSpec((tm, n_heads), lambda i: (i, 0))
    o_out = jax.ShapeDtypeStruct((S, W), BF16)
    wide_out = jax.ShapeDtypeStruct((S, 3 * D), BF16)
    dl_out = jax.ShapeDtypeStruct((S, n_heads), F32)
    return pl.pallas_call(
        body, name=name, out_shape=(wide_out, wide_out, o_out, o_out, o_out, dl_out, dl_out, dl_out),
        grid=(S // tm,),
        in_specs=[pl.BlockSpec((tm, D), lambda i: (i, 0)), o_spec, o_spec, o_spec, wide,
                  pl.BlockSpec((3, W, D), lambda i: (0, 0, 0))],
        out_specs=(wide, wide, o_spec, o_spec, o_spec, dl_spec, dl_spec, dl_spec),
        compiler_params=_params(),
    )(dmerged, o_a, o_b, o_c, gates, w_branch)


def _band_mask(variant, t_abs, s_abs):
    if variant == "b":
        return s_abs <= t_abs
    qc, kc = t_abs >> 6, s_abs >> 6
    return (kc <= qc) & (kc >= qc - (2 if variant == "a" else 8))


def _attn_fwd(variant, qkv, name, *, sinks=None, slopes=None, cq_col=None, ck_row=None, bias=None):
    S = qkv.shape[0]
    nb = S // BLK
    qb, kb, vb = ATTN_COLS[variant]
    shared_kv = variant == "a"
    win = ATTN_WINDOW.get(variant)

    def body(*refs):
        if variant == "a":
            q_ref, k_ref, v_ref, sink_ref, slope_ref, o_ref, lse_ref = refs
        elif variant == "b":
            q_ref, k_ref, v_ref, cq_ref, ck_ref, o_ref, lse_ref = refs
        else:
            q_ref, k_ref, v_ref, bias_ref, o_ref, lse_ref = refs
        p, i = pl.program_id(0), pl.program_id(1)
        lane = lax.broadcasted_iota(jnp.int32, (BLK, BLK), 1)
        t_abs = i * BLK + lax.broadcasted_iota(jnp.int32, (BLK, 1), 0)
        q2 = q_ref[...].astype(F32) * 0.125

        def compute(start, n_keys):
            k_w = k_ref[pl.ds(start, n_keys), :]
            v_w = v_ref[pl.ds(start, n_keys), :]
            s_abs = start + lax.broadcasted_iota(jnp.int32, (1, n_keys), 1)
            valid = _band_mask(variant, t_abs, s_abs)
            outs = []
            for half in (0, 1):
                hmask = (lane >= 64) if half else (lane < 64)
                qh = jnp.where(hmask, q2, 0.0)
                if shared_kv:
                    swap = (p // 2) != half
                    qh = jnp.where(swap, pltpu.roll(qh, 64, 1), qh)
                s = lax.dot_general(qh.astype(BF16), k_w, NT, preferred_element_type=F32)
                if variant == "a":
                    head = 2 * p + half
                    s = s + (-slope_ref[head]) * jnp.abs(t_abs - s_abs).astype(F32)
                elif variant == "b":
                    s = s + cq_ref[:, half:half + 1] - ck_ref[half:half + 1, pl.ds(start, n_keys)]
                else:
                    j0 = start // BLK
                    s = s + jnp.concatenate(
                        [bias_ref[half, jnp.clip(i - j0 - b, 0, 4)] for b in range(win)], axis=1)
                s = jnp.where(valid, s, NEG_INF)
                m = jnp.max(s, axis=1, keepdims=True)
                if variant == "a":
                    m = jnp.maximum(m, sink_ref[head])
                pe = jnp.exp(s - m)
                l = jnp.sum(pe, axis=1, keepdims=True)
                if variant == "a":
                    l = l + jnp.exp(sink_ref[head] - m)
                out = jnp.dot(pe.astype(BF16), v_w, preferred_element_type=F32) / l
                if shared_kv:
                    out = jnp.where(swap, pltpu.roll(out, 64, 1), out)
                outs.append(out)
                lse_ref[:, half:half + 1] = m + jnp.log(l)
            o_ref[...] = jnp.where(lane < 64, outs[0], outs[1]).astype(BF16)

        if variant == "b":
            for g in range(S // GROUP):
                pl.when(i // 4 == g)(functools.partial(compute, 0, (g + 1) * GROUP))
        else:
            start = jnp.clip(i - (win - 1), 0, nb - win) * BLK
            compute(pl.multiple_of(start, BLK), win * BLK)

    kv_col = (lambda p, i: (0, kb)) if shared_kv else (lambda p, i: (0, kb + p))
    vv_col = (lambda p, i: (0, vb)) if shared_kv else (lambda p, i: (0, vb + p))
    in_specs = [pl.BlockSpec((BLK, BLK), lambda p, i: (i, qb + p)),
                pl.BlockSpec((S, BLK), kv_col), pl.BlockSpec((S, BLK), vv_col)]
    args = [qkv, qkv, qkv]
    if variant == "a":
        in_specs += [pl.BlockSpec(memory_space=pltpu.SMEM), pl.BlockSpec(memory_space=pltpu.SMEM)]
        args += [sinks, slopes]
    elif variant == "b":
        in_specs += [pl.BlockSpec((None, BLK, 2), lambda p, i: (p, i, 0)),
                     pl.BlockSpec((None, 2, S), lambda p, i: (p, 0, 0))]
        args += [cq_col, ck_row]
    else:
        in_specs += [pl.BlockSpec((2, 5, BLK, BLK), lambda p, i: (p, 0, 0, 0))]
        args += [bias]
    return pl.pallas_call(
        body, name=name,
        out_shape=(jax.ShapeDtypeStruct((S, 512), BF16), jax.ShapeDtypeStruct((4, S, 2), F32)),
        grid=(4, nb), in_specs=in_specs,
        out_specs=(pl.BlockSpec((BLK, BLK), lambda p, i: (i, p)),
                   pl.BlockSpec((None, BLK, 2), lambda p, i: (p, i, 0))),
        compiler_params=_params(),
    )(*args)


def _attn_bwd(variant, qkv, do, lse_row, delta_row, name, *, sinks=None, slopes=None, cq_row=None,
              ck_col=None, bias_t=None):
    S = qkv.shape[0]
    nb = S // BLK
    qb, kb, vb = ATTN_COLS[variant]
    shared_kv = variant == "a"
    win = ATTN_WINDOW.get(variant)

    def body(*refs):
        if variant == "a":
            (q_ref, k_ref, v_ref, do_ref, lse_ref, dl_ref, sink_ref, slope_ref,
             dq_ref, dk_ref, dv_ref, ex_ref) = refs
        elif variant == "b":
            (q_ref, k_ref, v_ref, do_ref, lse_ref, dl_ref, cq_ref, ck_ref,
             dq_ref, dk_ref, dv_ref, ex_ref, dcq_ref) = refs
        else:
            (q_ref, k_ref, v_ref, do_ref, lse_ref, dl_ref, bias_ref,
             dq_ref, dk_ref, dv_ref, ex_ref) = refs
        p, j = pl.program_id(0), pl.program_id(1)
        lane = lax.broadcasted_iota(jnp.int32, (BLK, BLK), 1)
        s_abs = j * BLK + lax.broadcasted_iota(jnp.int32, (BLK, 1), 0)
        off_k = pl.multiple_of(j * BLK, BLK)
        k2 = k_ref[...].astype(F32)
        v2 = v_ref[...].astype(F32)
        hmasks = [(lane < 64), (lane >= 64)]
        if shared_kv:
            kv_lane = (lane >> 6) == (p // 2)
            swaps = [(p // 2) != half for half in (0, 1)]
            k_src, v_src = jnp.where(kv_lane, k2, 0.0), jnp.where(kv_lane, v2, 0.0)
            k_al = [jnp.where(swaps[h], pltpu.roll(k_src, 64, 1), k_src) for h in (0, 1)]
            v_al = [jnp.where(swaps[h], pltpu.roll(v_src, 64, 1), v_src) for h in (0, 1)]
        else:
            k_al = [jnp.where(hmasks[h], k2, 0.0) for h in (0, 1)]
            v_al = [jnp.where(hmasks[h], v2, 0.0) for h in (0, 1)]
        k_al = [(t * 0.125).astype(BF16) for t in k_al]
        v_al = [t.astype(BF16) for t in v_al]

        @pl.when(j == 0)
        def _():
            dq_ref[...] = jnp.zeros_like(dq_ref)
            if variant == "b":
                dcq_ref[...] = jnp.zeros_like(dcq_ref)
            else:
                ex_ref[...] = jnp.zeros_like(ex_ref)

        def to_kv_lanes(x, h):
            x = jnp.where(hmasks[h], x, 0.0)
            if shared_kv:
                x = jnp.where(swaps[h], pltpu.roll(x, 64, 1), x)
            return x

        def compute(start, n_q):
            q_w = q_ref[pl.ds(start, n_q), :]
            do_w = do_ref[pl.ds(start, n_q), :]
            t_abs = start + lax.broadcasted_iota(jnp.int32, (1, n_q), 1)
            valid = _band_mask(variant, t_abs, s_abs)
            dk_acc = dv_acc = None
            ds_both = []
            for half in (0, 1):
                s = lax.dot_general(k_al[half], q_w, NT, preferred_element_type=F32)
                if variant == "a":
                    s = s + (-slope_ref[2 * p + half]) * jnp.abs(t_abs - s_abs).astype(F32)
                elif variant == "b":
                    s = s + cq_ref[half:half + 1, pl.ds(start, n_q)] - ck_ref[:, half:half + 1]
                else:
                    i0 = start // BLK
                    s = s + jnp.concatenate(
                        [bias_ref[half, jnp.clip(i0 + b - j, 0, 4)] for b in range(win)], axis=1)
                pr = jnp.where(valid, jnp.exp(s - lse_ref[half:half + 1, pl.ds(start, n_q)]), 0.0)
                dp = lax.dot_general(v_al[half], do_w, NT, preferred_element_type=F32)
                ds = pr * (dp - dl_ref[half:half + 1, pl.ds(start, n_q)])
                ds16 = ds.astype(BF16)
                dv_h = to_kv_lanes(jnp.dot(pr.astype(BF16), do_w, preferred_element_type=F32), half)
                dk_h = to_kv_lanes(jnp.dot(ds16, q_w, preferred_element_type=F32) * 0.125, half)
                dv_acc = dv_h if dv_acc is None else dv_acc + dv_h
                dk_acc = dk_h if dk_acc is None else dk_acc + dk_h
                ds_both.append(ds16)
                if variant == "b":
                    ex_ref[:, half:half + 1] = -jnp.sum(ds, axis=1, keepdims=True)
                    dcq_ref[half:half + 1, pl.ds(start, n_q)] += jnp.sum(ds, axis=0, keepdims=True)
                elif variant == "c":
                    for b in range(win):
                        ex_ref[half, jnp.clip(i0 + b - j, 0, 4)] += ds[:, b * BLK:(b + 1) * BLK]
            dq_t = lax.dot_general(jnp.concatenate(k_al, axis=0), jnp.concatenate(ds_both, axis=0), TN,
                                   preferred_element_type=F32)
            dq_ref[:, pl.ds(start, n_q)] += dq_t
            if shared_kv:
                @pl.when(p == 0)
                def _():
                    dk_ref[pl.ds(off_k, BLK), :] = dk_acc
                    dv_ref[pl.ds(off_k, BLK), :] = dv_acc

                @pl.when(p > 0)
                def _():
                    dk_ref[pl.ds(off_k, BLK), :] += dk_acc
                    dv_ref[pl.ds(off_k, BLK), :] += dv_acc
            else:
                dk_ref[pl.ds(off_k, BLK), :] = dk_acc
                dv_ref[pl.ds(off_k, BLK), :] = dv_acc

        if variant == "b":
            for g in range(S // GROUP):
                pl.when(j // 4 == g)(functools.partial(compute, g * GROUP, S - g * GROUP))
        else:
            start = jnp.clip(j, 0, nb - win) * BLK
            compute(pl.multiple_of(start, BLK), win * BLK)

        if variant == "a":
            for half in (0, 1):
                p_sink = jnp.exp(sink_ref[2 * p + half] - lse_ref[half:half + 1, pl.ds(off_k, BLK)])
                term = p_sink * dl_ref[half:half + 1, pl.ds(off_k, BLK)]
                ex_ref[half:half + 1, :] += -jnp.sum(term, axis=1, keepdims=True)

    col = lambda c0: (lambda p, j: (0, c0 + p))
    kv_blk = (lambda c0: (lambda p, j: (j, c0))) if shared_kv else (lambda c0: (lambda p, j: (j, c0 + p)))
    pair = lambda p, j: (0, p)
    row_stat = pl.BlockSpec((None, 2, S), lambda p, j: (p, 0, 0))
    in_specs = [pl.BlockSpec((S, BLK), col(qb)),
                pl.BlockSpec((BLK, BLK), kv_blk(kb)), pl.BlockSpec((BLK, BLK), kv_blk(vb)),
                pl.BlockSpec((S, BLK), pair), row_stat, row_stat]
    args = [qkv, qkv, qkv, do, lse_row, delta_row]
    kv_width = BLK if shared_kv else 512
    kv_out = pl.BlockSpec((S, BLK), (lambda p, j: (0, 0)) if shared_kv else pair)
    out_shape = [jax.ShapeDtypeStruct((512, S), F32), jax.ShapeDtypeStruct((S, kv_width), F32),
                 jax.ShapeDtypeStruct((S, kv_width), F32)]
    out_specs = [pl.BlockSpec((BLK, S), lambda p, j: (p, 0)), kv_out, kv_out]
    if variant == "a":
        in_specs += [pl.BlockSpec(memory_space=pltpu.SMEM), pl.BlockSpec(memory_space=pltpu.SMEM)]
        args += [sinks, slopes]
        out_shape.append(jax.ShapeDtypeStruct((4, 8, BLK), F32))
        out_specs.append(pl.BlockSpec((None, 8, BLK), lambda p, j: (p, 0, 0)))
    elif variant == "b":
        in_specs += [row_stat, pl.BlockSpec((None, BLK, 2), lambda p, j: (p, j, 0))]
        args += [cq_row, ck_col]
        out_shape += [jax.ShapeDtypeStruct((4, S, 2), F32), jax.ShapeDtypeStruct((4, 2, S), F32)]
        out_specs += [pl.BlockSpec((None, BLK, 2), lambda p, j: (p, j, 0)), row_stat]
    else:
        in_specs += [pl.BlockSpec((2, 5, BLK, BLK), lambda p, j: (p, 0, 0, 0))]
        args += [bias_t]
        out_shape.append(jax.ShapeDtypeStruct((8, 5, BLK, BLK), F32))
        out_specs.append(pl.BlockSpec((2, 5, BLK, BLK), lambda p, j: (p, 0, 0, 0)))
    return pl.pallas_call(
        body, name=name, out_shape=tuple(out_shape), grid=(4, nb),
        in_specs=in_specs, out_specs=tuple(out_specs),
        compiler_params=_params(),
    )(*args)


def _log_sigmoid(x):
    return jnp.minimum(x, 0.0) - jnp.log(1.0 + jnp.exp(-jnp.abs(x)))


def _forget_fwd(fb, b_forget, name):
    S = fb.shape[0]
    nb = S // BLK

    def body(fb_ref, b_ref, cum_ref, carry_ref):
        i = pl.program_id(0)
        logf = _log_sigmoid(fb_ref[...] + b_ref[...])
        r = lax.broadcasted_iota(jnp.int32, (BLK, BLK), 0)
        c = lax.broadcasted_iota(jnp.int32, (BLK, BLK), 1)
        tri = (c <= r).astype(F32)

        @pl.when(i == 0)
        def _():
            carry_ref[...] = jnp.zeros_like(carry_ref)

        cum = jnp.dot(tri, logf, preferred_element_type=F32, precision=HIGHEST) + carry_ref[0:1, :]
        cum_ref[...] = cum
        carry_ref[...] = jnp.broadcast_to(cum[BLK - 1:BLK, :], carry_ref.shape)

    return pl.pallas_call(
        body, name=name, out_shape=jax.ShapeDtypeStruct((S, BLK), F32), grid=(nb,),
        in_specs=[pl.BlockSpec((BLK, BLK), lambda i: (i, 0)), pl.BlockSpec((1, BLK), lambda i: (0, 0))],
        out_specs=pl.BlockSpec((BLK, BLK), lambda i: (i, 0)),
        scratch_shapes=[pltpu.VMEM((8, BLK), F32)],
        compiler_params=_params(),
    )(fb, b_forget)


def _forget_bwd(dcum_q, dcum_k, fb, b_forget, name):
    S = fb.shape[0]
    nb = S // BLK

    def body(dq_ref, dk_ref, fb_ref, b_ref, dfb_ref, db_ref, carry_ref):
        g = pl.program_id(0)
        r = lax.broadcasted_iota(jnp.int32, (BLK, BLK), 0)
        c = lax.broadcasted_iota(jnp.int32, (BLK, BLK), 1)
        tri = (c >= r).astype(F32)

        @pl.when(g == 0)
        def _():
            carry_ref[...] = jnp.zeros_like(carry_ref)

        dcum = dq_ref[...] + dk_ref[...]
        dlogf = jnp.dot(tri, dcum, preferred_element_type=F32, precision=HIGHEST) + carry_ref[0:1, :]
        carry_ref[...] = jnp.broadcast_to(dlogf[0:1, :], carry_ref.shape)
        x = fb_ref[...] + b_ref[...]
        dfb = jnp.where(c < N_FORGET, dlogf * jax.nn.sigmoid(-x), 0.0)
        dfb_ref[...] = dfb
        db = jnp.sum(dfb, axis=0, keepdims=True)

        @pl.when(g == 0)
        def _():
            db_ref[...] = db

        @pl.when(g > 0)
        def _():
            db_ref[...] += db

    rev = pl.BlockSpec((BLK, BLK), lambda g: (nb - 1 - g, 0))
    row = pl.BlockSpec((1, BLK), lambda g: (0, 0))
    return pl.pallas_call(
        body, name=name,
        out_shape=(jax.ShapeDtypeStruct((S, BLK), F32), jax.ShapeDtypeStruct((1, BLK), F32)), grid=(nb,),
        in_specs=[rev, rev, rev, row], out_specs=(rev, row),
        scratch_shapes=[pltpu.VMEM((8, BLK), F32)],
        compiler_params=_params(),
    )(dcum_q, dcum_k, fb, b_forget)


def _skew(x, sign):
    row = lax.broadcasted_iota(jnp.int32, x.shape, 0)
    for b in range(7):
        amount = (1 << b) if sign > 0 else 256 - (1 << b)
        x = jnp.where(((row >> b) & 1) == 1, pltpu.roll(x, amount, 1), x)
    return x


def _rel_bases(rel):
    far = rel[:, 256:257]
    far127 = jnp.broadcast_to(far, (rel.shape[0], 127))
    base0 = jnp.concatenate([rel[:, 128:0:-1], far, rel[:, 255:128:-1]], axis=1)
    base1 = jnp.concatenate([rel[:, 256:128:-1], far, far127], axis=1)
    base0_t = jnp.concatenate([rel[:, 128:256], far, rel[:, 1:128]], axis=1)
    base1_t = jnp.concatenate([jnp.broadcast_to(far, (rel.shape[0], 128)), far, rel[:, 129:256]], axis=1)
    return jnp.stack([base0, base1, base0_t, base1_t], axis=1)


def _rel_expand(bases, name):
    def body(b_ref, t_ref, tt_ref):
        far = jnp.broadcast_to(b_ref[1:2, 0:1], (BLK, BLK))
        for k, out_ref in ((0, t_ref), (2, tt_ref)):
            for d in (0, 1):
                x = jnp.broadcast_to(b_ref[k + d:k + d + 1, :], (BLK, 2 * BLK))
                out_ref[d] = _skew(x, 1)[:, :BLK]
            for d in (2, 3, 4):
                out_ref[d] = far

    out = jax.ShapeDtypeStruct((8, 5, BLK, BLK), F32)
    spec = pl.BlockSpec((None, 5, BLK, BLK), lambda h: (h, 0, 0, 0))
    return pl.pallas_call(
        body, name=name, out_shape=(out, out), grid=(8,),
        in_specs=[pl.BlockSpec((None, 4, 2 * BLK), lambda h: (h, 0, 0))], out_specs=(spec, spec),
        compiler_params=_params(),
    )(bases)


def _rel_reduce(dtiles_t, name):
    def body(dt_ref, o_ref):
        zeros = jnp.zeros((BLK, BLK), F32)
        sums = []
        for d in (0, 1):
            x = _skew(jnp.concatenate([dt_ref[d], zeros], axis=1), -1)
            sums.append(jnp.broadcast_to(jnp.sum(x, axis=0, keepdims=True), (8, 2 * BLK)))
        lane = lax.broadcasted_iota(jnp.int32, (8, 2 * BLK), 1)
        main = pltpu.roll(sums[0], BLK, 1) + jnp.where(lane > BLK, sums[1], 0.0)
        far = jnp.sum(jnp.where(lane < BLK, sums[1], 0.0)[0:1], axis=1, keepdims=True)
        far = far + jnp.sum(jnp.sum(dt_ref[2] + dt_ref[3] + dt_ref[4], axis=0, keepdims=True), axis=1, keepdims=True)
        o_ref[...] = jnp.concatenate([main[0:1], jnp.broadcast_to(far, (1, BLK))], axis=1)

    return pl.pallas_call(
        body, name=name, out_shape=jax.ShapeDtypeStruct((8, 1, 3 * BLK), F32), grid=(8,),
        in_specs=[pl.BlockSpec((None, 5, BLK, BLK), lambda h: (h, 0, 0, 0))],
        out_specs=pl.BlockSpec((None, 1, 3 * BLK), lambda h: (h, 0, 0)),
        compiler_params=_params(),
    )(dtiles_t)


def _final_loss(x, target, g, name):
    S, D = x.shape
    ts = _row_tile(S, 256)

    def body(x_ref, t_ref, g_ref, dx_ref, loss_ref, dg_ref):
        i = pl.program_id(0)
        xv, gv = x_ref[...], g_ref[...]
        rstd = lax.rsqrt(jnp.mean(xv * xv, axis=-1, keepdims=True) + EPS)
        xhat = xv * rstd
        err = xhat * gv - t_ref[...]
        part = 0.5 * jnp.sum(jnp.mean(err * err, axis=-1, keepdims=True), axis=0, keepdims=True)
        dy = err / D
        dg = jnp.sum(dy * xhat, axis=0, keepdims=True)
        dxhat = dy * gv
        proj = jnp.mean(dxhat * xhat, axis=-1, keepdims=True)
        dx_ref[...] = rstd * (dxhat - xhat * proj)

        @pl.when(i == 0)
        def _():
            loss_ref[...] = jnp.broadcast_to(part, loss_ref.shape)
            dg_ref[...] = dg

        @pl.when(i > 0)
        def _():
            loss_ref[...] += jnp.broadcast_to(part, loss_ref.shape)
            dg_ref[...] += dg

    tile = pl.BlockSpec((ts, D), lambda i: (i, 0))
    row = pl.BlockSpec((1, D), lambda i: (0, 0))
    return pl.pallas_call(
        body, name=name,
        out_shape=(jax.ShapeDtypeStruct((S, D), F32), jax.ShapeDtypeStruct((8, 128), F32),
                   jax.ShapeDtypeStruct((1, D), F32)),
        grid=(S // ts,), in_specs=[tile, tile, row],
        out_specs=(tile, pl.BlockSpec((8, 128), lambda i: (0, 0)), row),
        compiler_params=_params(),
    )(x, target, g)


def _ada_fwd(c_all, w_ada, name):
    L, D, E = w_ada.shape

    def body(c_ref, w_ref, o_ref):
        cv = c_ref[...]
        cond = cv * jax.nn.sigmoid(cv)
        o_ref[...] = jnp.dot(cond, w_ref[...], preferred_element_type=F32, precision=HIGHEST)

    return pl.pallas_call(
        body, name=name, out_shape=jax.ShapeDtypeStruct((L, N_DEV, E), F32), grid=(L,),
        in_specs=[pl.BlockSpec((N_DEV, D), lambda l: (0, 0)), pl.BlockSpec((None, D, E), lambda l: (l, 0, 0))],
        out_specs=pl.BlockSpec((None, N_DEV, E), lambda l: (l, 0, 0)),
        compiler_params=_params(),
    )(c_all, w_ada)


def _ada_bwd(c_all_t, dmod, name):
    D = c_all_t.shape[0]
    L, _, E = dmod.shape

    def body(c_ref, d_ref, o_ref):
        cv = c_ref[...]
        cond = cv * jax.nn.sigmoid(cv)
        acc = None
        for b in range(N_DEV):
            t = cond[:, b:b + 1] * d_ref[b:b + 1, :]
            acc = t if acc is None else acc + t
        o_ref[...] = acc

    return pl.pallas_call(
        body, name=name, out_shape=jax.ShapeDtypeStruct((L, D, E), F32), grid=(L,),
        in_specs=[pl.BlockSpec((D, N_DEV), lambda l: (0, 0)), pl.BlockSpec((None, N_DEV, E), lambda l: (l, 0, 0))],
        out_specs=pl.BlockSpec((None, D, E), lambda l: (l, 0, 0)),
        compiler_params=_params(),
    )(c_all_t, dmod)


def _adamw(w, m, v, g_parts, name):
    L, R, C = w.shape
    tr = _row_tile(R, max(8, (256 * 1024 // max(C, 128)) // 8 * 8))
    nr = R // tr
    c1 = 1.0 - ADAM_B1 ** ADAM_STEP
    c2 = 1.0 - ADAM_B2 ** ADAM_STEP

    def body(w_ref, m_ref, v_ref, *rest):
        g_refs, (go_ref, d_ref, mo_ref, vo_ref) = rest[:L], rest[L:]
        layer = pl.program_id(0)
        g = None
        for l, g_ref in enumerate(g_refs):
            gl = g_ref[0].astype(F32)
            for k in range(1, g_ref.shape[0]):
                gl = gl + g_ref[k].astype(F32)
            g = gl if g is None else jnp.where(layer == l, gl, g)
        mn = ADAM_B1 * m_ref[...] + (1.0 - ADAM_B1) * g
        vn = ADAM_B2 * v_ref[...] + (1.0 - ADAM_B2) * (g * g)
        m_hat = mn / c1
        v_hat = vn / c2
        go_ref[...] = g
        d_ref[...] = -ADAM_LR * (m_hat / (jnp.sqrt(v_hat) + ADAM_EPS) + ADAM_WD * w_ref[...])
        mo_ref[...] = mn
        vo_ref[...] = vn

    def parts_spec(l, n_parts):
        return pl.BlockSpec((n_parts, tr, C), lambda layer, i: (0, jnp.where(layer == l, i, 0 if l > 0 else nr - 1), 0))

    tile = pl.BlockSpec((None, tr, C), lambda layer, i: (layer, i, 0))
    out = jax.ShapeDtypeStruct((L, R, C), F32)
    return pl.pallas_call(
        body, name=name, out_shape=(out, out, out, out), grid=(L, nr),
        in_specs=[tile, tile, tile] + [parts_spec(l, p.shape[0]) for l, p in enumerate(g_parts)],
        out_specs=(tile, tile, tile, tile),
        compiler_params=_params(),
    )(w, m, v, *g_parts)


def _pair_add(pieces, recv, core, name):
    _, _, R, C = pieces.shape
    tr = _row_tile(R, max(8, (512 * 1024 // max(C, 128)) // 8 * 8))

    def body(core_ref, a_ref, b_ref, o_ref):
        o_ref[...] = (a_ref[...].astype(F32) + b_ref[...].astype(F32)).astype(BF16)

    return pl.pallas_call(
        body, name=name, out_shape=jax.ShapeDtypeStruct((4, R, C), BF16),
        grid_spec=pltpu.PrefetchScalarGridSpec(
            num_scalar_prefetch=1, grid=(4, R // tr),
            in_specs=[pl.BlockSpec((None, None, tr, C), lambda k, i, core_ref: (core_ref[0], k, i, 0)),
                      pl.BlockSpec((None, tr, C), lambda k, i, core_ref: (k, i, 0))],
            out_specs=pl.BlockSpec((None, tr, C), lambda k, i, core_ref: (k, i, 0))),
        compiler_params=_params(),
    )(core, pieces, recv)


MESH = pl.DeviceIdType.MESH
ANY = pl.BlockSpec(memory_space=pl.ANY)


def _position():
    return lax.axis_index("x"), lax.axis_index("y"), lax.axis_index("c")


def _small_all_gather(v, name):
    m_per, n = v.shape

    def body(x_ref, out_ref, send_sems, recv_sems, local_sem):
        x, y, c = _position()
        me, sibling = (x, y, c), (x, y, 1 - c)
        chips = [(1 - x, y), (x, 1 - y), (1 - x, 1 - y)]

        def rows(px, py, pc):
            return out_ref.at[pl.ds((4 * px + 2 * py + pc) * m_per, m_per), :]

        def copy(k, block, to, src=None):
            return pltpu.make_async_remote_copy(
                src_ref=rows(*block) if src is None else src, dst_ref=rows(*block),
                send_sem=send_sems.at[k], recv_sem=recv_sems.at[k], device_id=to, device_id_type=MESH)

        mine = pltpu.make_async_copy(x_ref, rows(*me), local_sem)
        mine.start()
        first = [copy(0, me, sibling, src=x_ref)]
        first += [copy(1 + j, me, (*chip, c), src=x_ref) for j, chip in enumerate(chips)]
        for cp in first:
            cp.start()
        passed = [copy(4 + j, (*chip, c), sibling) for j, chip in enumerate(chips)]
        for j, chip in enumerate(chips):
            copy(1 + j, (*chip, c), me).wait_recv()
            passed[j].start()
        copy(0, sibling, me).wait_recv()
        for j, chip in enumerate(chips):
            copy(4 + j, (*chip, 1 - c), me).wait_recv()
        for cp in first + passed:
            cp.wait_send()
        mine.wait()

    return pl.pallas_call(
        body, name=name, out_shape=jax.ShapeDtypeStruct((N_DEV * m_per, n), v.dtype),
        in_specs=[pl.BlockSpec(memory_space=pltpu.VMEM)], out_specs=pl.BlockSpec(memory_space=pltpu.VMEM),
        scratch_shapes=[pltpu.SemaphoreType.DMA((7,)), pltpu.SemaphoreType.DMA((7,)), pltpu.SemaphoreType.DMA],
    )(v)


def _big_all_gather(shards, name):
    n_arr = len(shards)

    def body(*refs):
        x_refs, out_refs = refs[:n_arr], refs[n_arr:2 * n_arr]
        send_sems, recv_sems, local_sems = refs[2 * n_arr:]
        x, y, c = _position()
        me, sibling = (x, y, c), (x, y, 1 - c)
        chips = [(1 - x, y), (x, 1 - y), (1 - x, 1 - y)]

        def slot(a, px, py, pc):
            return out_refs[a].at[4 * px + 2 * py + pc]

        def copy(a, k, block, to, src=None):
            return pltpu.make_async_remote_copy(
                src_ref=slot(a, *block) if src is None else src, dst_ref=slot(a, *block),
                send_sem=send_sems.at[a, k], recv_sem=recv_sems.at[a, k], device_id=to, device_id_type=MESH)

        mine = [pltpu.make_async_copy(x_refs[a], slot(a, *me), local_sems.at[a]) for a in range(n_arr)]
        for cp in mine:
            cp.start()
        first = []
        for j, chip in enumerate(chips):
            first += [copy(a, 1 + j, me, (*chip, c), src=x_refs[a]) for a in range(n_arr)]
        first += [copy(a, 0, me, sibling, src=x_refs[a]) for a in range(n_arr)]
        for cp in first:
            cp.start()
        passed = []
        for j, chip in enumerate(chips):
            for a in range(n_arr):
                copy(a, 1 + j, (*chip, c), me).wait_recv()
                fwd = copy(a, 4 + j, (*chip, c), sibling)
                fwd.start()
                passed.append(fwd)
        for a in range(n_arr):
            copy(a, 0, sibling, me).wait_recv()
        for j, chip in enumerate(chips):
            for a in range(n_arr):
                copy(a, 4 + j, (*chip, 1 - c), me).wait_recv()
        for cp in first + passed:
            cp.wait_send()
        for cp in mine:
            cp.wait()

    return pl.pallas_call(
        body, name=name,
        out_shape=tuple(jax.ShapeDtypeStruct((N_DEV,) + s.shape, s.dtype) for s in shards),
        in_specs=[ANY] * n_arr, out_specs=tuple([ANY] * n_arr),
        scratch_shapes=[pltpu.SemaphoreType.DMA((n_arr, 7)), pltpu.SemaphoreType.DMA((n_arr, 7)),
                        pltpu.SemaphoreType.DMA((n_arr,))],
    )(*shards)


def _sibling_exchange(pieces, name):
    n_arr = len(pieces)

    def body(*refs):
        p_refs, out_refs = refs[:n_arr], refs[n_arr:2 * n_arr]
        send_sems, recv_sems = refs[2 * n_arr:]
        x, y, c = _position()
        copies = [pltpu.make_async_remote_copy(
            src_ref=p_refs[a].at[1 - c], dst_ref=out_refs[a], send_sem=send_sems.at[a], recv_sem=recv_sems.at[a],
            device_id=(x, y, 1 - c), device_id_type=MESH) for a in range(n_arr)]
        for cp in copies:
            cp.start()
        for cp in copies:
            cp.wait()

    return pl.pallas_call(
        body, name=name,
        out_shape=tuple(jax.ShapeDtypeStruct(p.shape[1:], p.dtype) for p in pieces),
        in_specs=[ANY] * n_arr, out_specs=tuple([ANY] * n_arr),
        scratch_shapes=[pltpu.SemaphoreType.DMA((n_arr,)), pltpu.SemaphoreType.DMA((n_arr,))],
    )(*pieces)


def _chip_exchange(sums, name):
    n_arr = len(sums)

    def body(*refs):
        s_refs, out_refs = refs[:n_arr], refs[n_arr:2 * n_arr]
        send_sems, recv_sems, local_sems = refs[2 * n_arr:]
        x, y, c = _position()
        my_chip = 2 * x + y
        chips = [(1 - x, y), (x, 1 - y), (1 - x, 1 - y)]
        mine = [pltpu.make_async_copy(s_refs[a].at[my_chip], out_refs[a].at[my_chip], local_sems.at[a])
                for a in range(n_arr)]
        for cp in mine:
            cp.start()
        copies = []
        for j, (px, py) in enumerate(chips):
            copies += [pltpu.make_async_remote_copy(
                src_ref=s_refs[a].at[2 * px + py], dst_ref=out_refs[a].at[my_chip],
                send_sem=send_sems.at[a, j], recv_sem=recv_sems.at[a, j],
                device_id=(px, py, c), device_id_type=MESH) for a in range(n_arr)]
        for cp in copies:
            cp.start()
        for j, (px, py) in enumerate(chips):
            for a in range(n_arr):
                pltpu.make_async_remote_copy(
                    src_ref=s_refs[a].at[my_chip], dst_ref=out_refs[a].at[2 * px + py],
                    send_sem=send_sems.at[a, j], recv_sem=recv_sems.at[a, j],
                    device_id=(px, py, c), device_id_type=MESH).wait_recv()
        for cp in copies:
            cp.wait_send()
        for cp in mine:
            cp.wait()

    return pl.pallas_call(
        body, name=name,
        out_shape=tuple(jax.ShapeDtypeStruct(s.shape, s.dtype) for s in sums),
        in_specs=[ANY] * n_arr, out_specs=tuple([ANY] * n_arr),
        scratch_shapes=[pltpu.SemaphoreType.DMA((n_arr, 3)), pltpu.SemaphoreType.DMA((n_arr, 3)),
                        pltpu.SemaphoreType.DMA((n_arr,))],
    )(*sums)


HBM = pl.BlockSpec(memory_space=pltpu.HBM)
SEM = pl.BlockSpec(memory_space=pltpu.SEMAPHORE)
EFFECT = pltpu.SideEffectType.DATAFLOW_SIDE_EFFECTING
RELATIONS = [(rx, ry, rc) for rx in (0, 1) for ry in (0, 1) for rc in (0, 1)][1:]


def _exchange_copies(src_refs, land_refs, send_sems, recv_sems, scatter, receive_side):
    x, y, c = _position()
    me = 4 * x + 2 * y + c
    copies = []
    for k, (rx, ry, rc) in enumerate(RELATIONS):
        peer = ((1 - x) if rx else x, (1 - y) if ry else y, (1 - c) if rc else c)
        peer_index = 4 * peer[0] + 2 * peer[1] + peer[2]
        for a, (src, land) in enumerate(zip(src_refs, land_refs)):
            copies.append(pltpu.make_async_remote_copy(
                src_ref=src.at[peer_index] if scatter else src,
                dst_ref=land.at[peer_index if receive_side else me],
                send_sem=send_sems.at[a * len(RELATIONS) + k], recv_sem=recv_sems.at[a * len(RELATIONS) + k],
                device_id=peer, device_id_type=MESH))
    return copies


def _exchange_start(srcs, scatter, after, name):
    n = len(srcs)
    land_shapes = [(s.shape if scatter else (N_DEV,) + s.shape) for s in srcs]

    def body(*refs):
        src_refs, land_refs = refs[:n], refs[n:2 * n]
        send_sems, recv_sems = refs[2 * n + 1], refs[2 * n + 2]
        token = refs[-1]
        for cp in _exchange_copies(src_refs, land_refs, send_sems, recv_sems, scatter, False):
            cp.start()
        token[...] = jnp.zeros_like(token)

    sems = pltpu.SemaphoreType.DMA((n * len(RELATIONS),))
    outs = pl.pallas_call(
        body, name=name,
        out_shape=(sems, sems, *[pltpu.HBM(s.shape, s.dtype) for s in srcs],
                   *[pltpu.HBM(shape, s.dtype) for shape, s in zip(land_shapes, srcs)],
                   jax.ShapeDtypeStruct((8, 128), F32)),
        in_specs=[HBM] * (2 * n) + [ANY],
        out_specs=(SEM, SEM, *[HBM] * (2 * n), pl.BlockSpec(memory_space=pltpu.VMEM)),
        input_output_aliases={a: 2 + a for a in range(2 * n)},
        compiler_params=pltpu.CompilerParams(has_side_effects=EFFECT),
    )(*[pltpu.with_memory_space_constraint(s, pltpu.HBM) for s in srcs],
      *[pltpu.with_memory_space_constraint(lax.empty(shape, s.dtype), pltpu.HBM)
        for shape, s in zip(land_shapes, srcs)], after)
    return outs[0], outs[1], outs[2:2 + n], outs[2 + n:2 + 2 * n], outs[-1]


def _exchange_wait(started, scatter, after, name):
    send_sems, recv_sems, srcs, lands, _ = started
    n = len(srcs)

    def body(*refs):
        src_refs, land_refs = refs[:n], refs[n:2 * n]
        send_sems, recv_sems = refs[2 * n], refs[2 * n + 1]
        copies = _exchange_copies(src_refs, land_refs, send_sems, recv_sems, scatter, True)
        for cp in copies:
            cp.wait_send()
        for cp in copies:
            cp.wait_recv()

    outs = pl.pallas_call(
        body, name=name,
        out_shape=(*[pltpu.HBM(s.shape, s.dtype) for s in srcs], *[pltpu.HBM(t.shape, t.dtype) for t in lands]),
        in_specs=[HBM] * (2 * n) + [SEM, SEM, ANY], out_specs=tuple([HBM] * (2 * n)),
        input_output_aliases={a: a for a in range(2 * n)},
        compiler_params=pltpu.CompilerParams(has_side_effects=EFFECT),
    )(*srcs, *lands, send_sems, recv_sems, after)
    return outs[:n], outs[n:]


W_IN_SHARD = N_IN // N_DEV
F_SHARD = F_COL // W_IN_SHARD
F_LO = F_COL - F_SHARD * W_IN_SHARD


def _cols_from_shards(g):
    return jnp.concatenate([g[d] for d in range(N_DEV)], axis=1)


def _w_in_rearranged(g):
    parts = [g[d] for d in range(N_DEV)]
    parts[F_SHARD:F_SHARD + 1] = [g[F_SHARD][:, :F_LO], g[F_SHARD][:, F_LO + N_FORGET:]]
    parts += [g[F_SHARD][:, F_LO:F_LO + N_FORGET], jnp.zeros((g.shape[1], BLK - N_FORGET), g.dtype)]
    return jnp.concatenate(parts, axis=1)


def _w_in_pieces(dw_r):
    def original(lo, hi):
        shift = 0 if hi <= F_COL else N_FORGET
        return dw_r[:, lo - shift:hi - shift]

    pieces = []
    for d in range(N_DEV):
        lo, hi = d * W_IN_SHARD, (d + 1) * W_IN_SHARD
        if d == F_SHARD:
            pieces.append(jnp.concatenate([original(lo, F_COL), dw_r[:, N_MAIN:N_MAIN + N_FORGET],
                                           original(F_COL + N_FORGET, hi)], axis=1))
        else:
            pieces.append(original(lo, hi))
    return jnp.stack(pieces)


def _col_pieces(dw):
    width = dw.shape[1] // N_DEV
    return jnp.stack([dw[:, d * width:(d + 1) * width] for d in range(N_DEV)])


def _row_pieces(dw):
    return dw.reshape(N_DEV, dw.shape[0] // N_DEV, dw.shape[1])


def _branch_pieces(dw):
    k, w, d = dw.shape
    return jnp.transpose(dw.reshape(k, w, N_DEV, d // N_DEV), (2, 0, 1, 3)).reshape(N_DEV, k * w, d // N_DEV)


def _pair_major(p8):
    return jnp.stack([p8[0::2], p8[1::2]])


def _pairs_col(a):
    return jnp.transpose(a.reshape(a.shape[0], 4, 2), (1, 0, 2))


def _pairs_row(a):
    return jnp.transpose(a.reshape(a.shape[0], 4, 2), (1, 2, 0))


def _heads_from_col(a):
    return jnp.transpose(a, (1, 0, 2)).reshape(a.shape[1], 8)


def _heads_from_row(a):
    return jnp.transpose(a, (2, 0, 1)).reshape(a.shape[2], 8)


def _pad_lanes(a, n):
    return jnp.pad(a, [(0, 0)] * (a.ndim - 1) + [(0, n - a.shape[-1])])


SMALL_SEGMENTS = (("dmod", 2 * 6 * D_MODEL), ("norm_mix_g", 2 * D_MODEL), ("norm_ffn_g", 2 * D_MODEL),
                  ("final_norm_g", D_MODEL), ("b_forget", 128), ("sinks", 128), ("rel_bias", 4224))
SMALL_ROWS = 176


def _pack_small(parts):
    flat = [_pad_lanes(parts[name].reshape(1, -1), size) for name, size in SMALL_SEGMENTS]
    total = sum(size for _, size in SMALL_SEGMENTS)
    flat.append(jnp.zeros((1, SMALL_ROWS * 128 - total), F32))
    return jnp.concatenate(flat, axis=1).reshape(SMALL_ROWS, 128)


def _unpack_small(packed, shapes):
    flat = packed.reshape(-1)
    out, pos = {}, 0
    for name, size in SMALL_SEGMENTS:
        shape = shapes[name]
        count = 1
        for d in shape:
            count *= d
        out[name] = flat[pos:pos + count].reshape(shape)
        pos += size
    return out


def kernel(x, c, norm_mix_g, norm_ffn_g, w_ada, b_ada, w_in, b_forget, sinks, rel_bias, w_branch, w_out, w_ffn_in, w_ffn_out, final_norm_g, loss_target, m_norm_mix_g, m_norm_ffn_g, m_w_ada, m_b_ada, m_w_in, m_b_forget, m_sinks, m_rel_bias, m_w_branch, m_w_out, m_w_ffn_in, m_w_ffn_out, m_final_norm_g, v_norm_mix_g, v_norm_ffn_g, v_w_ada, v_b_ada, v_w_in, v_b_forget, v_sinks, v_rel_bias, v_w_branch, v_w_out, v_w_ffn_in, v_w_ffn_out, v_final_norm_g):
    depth = w_in.shape[0]
    S, D = x.shape[1], x.shape[2]
    assert S % GROUP == 0 and S >= ATTN_WINDOW["c"] * BLK
    px, py, pc = _position()
    me = 4 * px + 2 * py + pc
    x0 = x[0]

    assert depth == 2
    big_weights = (w_in, w_branch, w_out, w_ffn_in, w_ffn_out)

    def full_matrices(g_in, g_branch, g_out, g_fin, g_fout):
        return dict(w_in=_w_in_rearranged(g_in),
                    w_branch=jnp.transpose(g_branch, (1, 2, 0, 3)).reshape(3, 512, D),
                    w_out=g_out.reshape(D, D), w_fin=_cols_from_shards(g_fin),
                    w_fout=g_fout.reshape(FFN_HIDDEN, D))

    shards = [[w[l].astype(BF16) for w in big_weights] for l in range(depth)]
    gathered0 = _big_all_gather(shards[0], "comm_gather_weights0")
    gather1 = _exchange_start(shards[1], False, gathered0[0], "comm_gather_weights1_start")
    weights = [full_matrices(*gathered0), None]
    W_in, W_branch, W_out, W_fin, W_fout = ([weights[0][k], None] for k in ("w_in", "w_branch", "w_out", "w_fin", "w_fout"))

    c_all = _small_all_gather(c.reshape(8, 128), "comm_gather_c").reshape(N_DEV, D)
    mod_cols = _ada_fwd(c_all, w_ada, "ada_fwd")
    mod_all = _small_all_gather(mod_cols.reshape(-1, 128), "comm_gather_mod")
    mod_all = mod_all.reshape(N_DEV, depth, N_DEV, w_ada.shape[2])
    mod_mine = lax.dynamic_index_in_dim(mod_all, me, axis=2, keepdims=False)
    mod = jnp.transpose(mod_mine, (1, 0, 2)).reshape(depth, 6 * D) + b_ada + gather1[4][0:1, 0:1]
    mods = [[mod[l:l + 1, k * D:(k + 1) * D] for k in range(6)] for l in range(depth)]

    slopes = jnp.exp2(-jnp.arange(1, 9, dtype=F32))
    saved = []
    xs = x0
    for l in range(depth):
        if l == 1:
            mine, landed = _exchange_wait(gather1, False, xs, "comm_gather_weights1_wait")
            weights[1] = full_matrices(*[lax.dynamic_update_index_in_dim(t, s, me, 0) for t, s in zip(landed, mine)])
            for held, k in ((W_in, "w_in"), (W_branch, "w_branch"), (W_out, "w_out"), (W_fin, "w_fin"),
                            (W_fout, "w_fout")):
                held[1] = weights[1][k]
        sh_m, sc_m, g_m, sh_f, sc_f, g_f = mods[l]
        gm, gf = norm_mix_g[l:l + 1], norm_ffn_g[l:l + 1]
        bfor = _pad_lanes(b_forget[l:l + 1], BLK)
        h = _norm_mod_fwd(xs, gm, sh_m, sc_m, f"norm_mix_fwd{l}")
        qkv = _matmul(h, W_in[l], "nn", BF16, f"proj_qkv{l}", n=N_QKV, tn=768)
        gates = _matmul(h, W_in[l], "nn", F32, f"proj_gates{l}", n=N_GATES, tn=768, b_off=N_QKV // 768)
        fb = _matmul(h, W_in[l], "nn", F32, f"proj_forget{l}", n=BLK, tn=BLK, b_off=N_MAIN // BLK)
        cum = _forget_fwd(fb, bfor, f"forget_fwd{l}")[:, :N_FORGET]
        cum_col, cum_row = _pairs_col(cum), _pairs_row(cum)
        tiles, tiles_t = _rel_expand(_rel_bases(rel_bias[l]), f"rel_expand{l}")
        o_a, lse_a = _attn_fwd("a", qkv, f"attn_a_fwd{l}", sinks=sinks[l], slopes=slopes)
        o_b, lse_b = _attn_fwd("b", qkv, f"attn_b_fwd{l}", cq_col=cum_col, ck_row=cum_row)
        o_c, lse_c = _attn_fwd("c", qkv, f"attn_c_fwd{l}", bias=tiles)
        merged = _merge_fwd(o_a, o_b, o_c, gates, W_branch[l], f"merge_fwd{l}")
        x1, mix = _matmul_resid(merged, W_out[l], xs, g_m, f"out_proj{l}")
        h2 = _norm_mod_fwd(x1, gf, sh_f, sc_f, f"norm_ffn_fwd{l}")
        act = _ffn_in_fwd(h2, W_fin[l], f"ffn_in_fwd{l}")
        x2, ffn = _matmul_resid(act, W_fout[l], x1, g_f, f"ffn_out{l}")
        saved.append(dict(x=xs, h=h, qkv=qkv, gates=gates, fb=fb, bfor=bfor, cum_col=cum_col, cum_row=cum_row,
                          tiles_t=tiles_t, o=(o_a, o_b, o_c), lse=(lse_a, lse_b, lse_c), merged=merged, mix=mix,
                          x1=x1, h2=h2, act=act, ffn=ffn))
        xs = x2

    dx, loss_tile, d_final_g = _final_loss(xs, loss_target[0], final_norm_g.reshape(1, D), "final_loss")
    loss = lax.psum(loss_tile[0, 0], ("x", "y", "c"))

    grads = {k: [None] * depth for k in ("w_in", "w_branch", "w_out", "w_ffn_in", "w_ffn_out", "norm_mix_g",
                                          "norm_ffn_g", "b_forget", "sinks", "rel_bias", "dmod")}
    def layer_pieces(l):
        return [_w_in_pieces(grads["w_in"][l]), _branch_pieces(grads["w_branch"][l]), _row_pieces(grads["w_out"][l]),
                _col_pieces(grads["w_ffn_in"][l]), _row_pieces(grads["w_ffn_out"][l])]

    reduce1 = None
    for l in reversed(range(depth)):
        sv = saved[l]
        sh_m, sc_m, g_m, sh_f, sc_f, g_f = mods[l]
        if l == 0:
            g_f = g_f + reduce1[4][0:1, 0:1]
        gm, gf = norm_mix_g[l:l + 1], norm_ffn_g[l:l + 1]
        df, d_g_f = _gate_bwd(dx, sv["ffn"], g_f, f"ffn_gate_bwd{l}")
        du_g, du_u = _ffn_mid_bwd(sv["h2"], df, W_fin[l], W_fout[l], f"ffn_mid_bwd{l}")
        du = jnp.concatenate([du_g, du_u], axis=1)
        grads["w_ffn_out"][l] = _matmul(sv["act"], df, "tn", BF16, f"wgrad_ffn_out{l}", tm=1408, tn=512, tk=512)
        grads["w_ffn_in"][l] = _matmul(sv["h2"], du, "tn", BF16, f"wgrad_ffn_in{l}", tm=512, tn=1408, tk=512)
        dh2 = _matmul(du, W_fin[l], "nt", F32, f"dgrad_ffn_in{l}", tn=512)
        dx1, d_sh_f, d_sc_f, d_gf = _norm_mod_bwd(sv["x1"], dh2, dx, gf, sc_f, f"norm_ffn_bwd{l}")
        dmix, d_g_m = _gate_bwd(dx1, sv["mix"], g_m, f"mix_gate_bwd{l}")
        grads["w_out"][l] = _matmul(sv["merged"], dmix, "tn", BF16, f"wgrad_out{l}", tm=512, tn=1024, tk=512)
        dmerged = _matmul(dmix, W_out[l], "nt", F32, f"dgrad_out{l}", tn=512)
        o_a, o_b, o_c = sv["o"]
        dgates, dy, do_a, do_b, do_c, dl_a, dl_b, dl_c = _merge_bwd(
            dmerged, o_a, o_b, o_c, sv["gates"], W_branch[l], f"merge_bwd{l}")
        dwb = [_matmul(o_k, dy, "tn", BF16, f"wgrad_branch{l}_{k}", n=D, b_off=k * (D // 512), tm=512, tn=512, tk=512)
               for k, o_k in enumerate((o_a, o_b, o_c))]
        grads["w_branch"][l] = jnp.stack(dwb)
        lse_rows = [_pairs_row(_heads_from_col(t)) for t in sv["lse"]]
        dqt_a, dk_a, dv_a, dsink = _attn_bwd("a", sv["qkv"], do_a, lse_rows[0], _pairs_row(dl_a), f"attn_a_bwd{l}",
                                             sinks=sinks[l], slopes=slopes)
        dqt_b, dk_b, dv_b, dck, dcq = _attn_bwd("b", sv["qkv"], do_b, lse_rows[1], _pairs_row(dl_b),
                                                f"attn_b_bwd{l}", cq_row=sv["cum_row"], ck_col=sv["cum_col"])
        dqt_c, dk_c, dv_c, dtiles_t = _attn_bwd("c", sv["qkv"], do_c, lse_rows[2], _pairs_row(dl_c),
                                                f"attn_c_bwd{l}", bias_t=sv["tiles_t"])
        grads["sinks"][l] = dsink[:, :2, 0].reshape(8)
        grads["rel_bias"][l] = _rel_reduce(dtiles_t, f"rel_reduce{l}")[:, 0, :N_REL]
        dcum_k = _pad_lanes(_heads_from_col(dck), BLK)
        dcum_q = _pad_lanes(_heads_from_row(dcq), BLK)
        dfb, d_bfor = _forget_bwd(dcum_q, dcum_k, sv["fb"], sv["bfor"], f"forget_bwd{l}")
        grads["b_forget"][l] = d_bfor[0, :N_FORGET]
        dproj = jnp.concatenate(
            [t.astype(BF16) for t in (dqt_a.T, dk_a, dv_a, dqt_b.T, dk_b, dv_b, dqt_c.T, dk_c, dv_c)]
            + [dgates, dfb.astype(BF16)], axis=1)
        grads["w_in"][l] = _matmul(sv["h"], dproj, "tn", BF16, f"wgrad_in{l}", tm=512, tn=1408, tk=512)
        dh = _matmul(dproj, W_in[l], "nt", F32, f"dgrad_in{l}", tn=512)
        dx, d_sh_m, d_sc_m, d_gm = _norm_mod_bwd(sv["x"], dh, dx1, gm, sc_m, f"norm_mix_bwd{l}")
        grads["norm_mix_g"][l] = d_gm[0]
        grads["norm_ffn_g"][l] = d_gf[0]
        grads["dmod"][l] = jnp.concatenate([d_sh_m, d_sc_m, d_g_m, d_sh_f, d_sc_f, d_g_f], axis=1)[0]
        if l == 1:
            reduce1 = _exchange_start(layer_pieces(1), True, dx, "comm_reduce1_start")

    grad_x = dx.reshape(x.shape)

    small_shapes = dict(dmod=b_ada.shape, norm_mix_g=norm_mix_g.shape, norm_ffn_g=norm_ffn_g.shape,
                        final_norm_g=final_norm_g.shape, b_forget=b_forget.shape, sinks=sinks.shape,
                        rel_bias=rel_bias.shape)
    mine_small = _pack_small(dict(
        dmod=jnp.stack(grads["dmod"]), norm_mix_g=jnp.stack(grads["norm_mix_g"]),
        norm_ffn_g=jnp.stack(grads["norm_ffn_g"]), final_norm_g=d_final_g[0],
        b_forget=_pad_lanes(jnp.stack(grads["b_forget"]).reshape(1, -1), 128),
        sinks=_pad_lanes(jnp.stack(grads["sinks"]).reshape(1, -1), 128),
        rel_bias=_pad_lanes(jnp.stack(grads["rel_bias"]).reshape(1, -1), 4224)))
    all_small = _small_all_gather(mine_small, "comm_gather_small").reshape(N_DEV, SMALL_ROWS, 128)

    def pack_params(b_ada_, nm, nf, fn, bf, sk, rb):
        return _pack_small(dict(dmod=b_ada_, norm_mix_g=nm, norm_ffn_g=nf, final_norm_g=fn,
                                b_forget=_pad_lanes(bf.reshape(1, -1), 128), sinks=_pad_lanes(sk.reshape(1, -1), 128),
                                rel_bias=_pad_lanes(rb.reshape(1, -1), 4224)))

    small_out = _adamw(
        pack_params(b_ada, norm_mix_g, norm_ffn_g, final_norm_g, b_forget, sinks, rel_bias)[None],
        pack_params(m_b_ada, m_norm_mix_g, m_norm_ffn_g, m_final_norm_g, m_b_forget, m_sinks, m_rel_bias)[None],
        pack_params(v_b_ada, v_norm_mix_g, v_norm_ffn_g, v_final_norm_g, v_b_forget, v_sinks, v_rel_bias)[None],
        [all_small], "adamw_small")
    small_out = [_unpack_small(t[0], small_shapes) for t in small_out]

    dmod_all = all_small[:, :96].reshape(N_DEV, depth, 6 * D)
    dmod_cols = lax.dynamic_slice_in_dim(dmod_all, me * w_ada.shape[2], w_ada.shape[2], axis=2)
    d_w_ada = _ada_bwd(jnp.transpose(c_all), jnp.transpose(dmod_cols, (1, 0, 2)), "ada_bwd")

    pieces0 = [_pair_major(p) for p in layer_pieces(0)]
    from_sibling = _sibling_exchange(pieces0, "comm_reduce0_sibling")
    core = pc.astype(jnp.int32).reshape(1)
    pair_sums = [_pair_add(p, r, core, f"pair_add{a}") for a, (p, r) in enumerate(zip(pieces0, from_sibling))]
    parts0 = _chip_exchange(pair_sums, "comm_reduce0_chips")
    pieces1, landed1 = _exchange_wait(reduce1, True, parts0[0], "comm_reduce1_wait")
    parts1 = [lax.dynamic_update_index_in_dim(t, lax.dynamic_index_in_dim(p, me, 0, keepdims=False), me, 0)
              for t, p in zip(landed1, pieces1)]

    big = {}
    for a, (name, w, m, v) in enumerate((
            ("w_in", w_in, m_w_in, v_w_in), ("w_branch", w_branch, m_w_branch, v_w_branch),
            ("w_out", w_out, m_w_out, v_w_out), ("w_ffn_in", w_ffn_in, m_w_ffn_in, v_w_ffn_in),
            ("w_ffn_out", w_ffn_out, m_w_ffn_out, v_w_ffn_out))):
        per_layer = lambda t: t.reshape(depth, -1, t.shape[-1])
        outs = _adamw(per_layer(w), per_layer(m), per_layer(v), [parts0[a], parts1[a]], f"adamw_{name}")
        big[name] = [t.reshape(w.shape) for t in outs]
    big["w_ada"] = _adamw(w_ada, m_w_ada, v_w_ada, [d_w_ada[l:l + 1] for l in range(depth)], "adamw_w_ada")

    def leaf(kind, name):
        if name in big:
            return big[name][kind]
        return small_out[kind]["dmod" if name == "b_ada" else name]

    order = ["norm_mix_g", "norm_ffn_g", "w_ada", "b_ada", "w_in", "b_forget", "sinks", "rel_bias", "w_branch",
             "w_out", "w_ffn_in", "w_ffn_out", "final_norm_g"]
    return (loss, grad_x, *[leaf(0, n) for n in order], *[leaf(1, n) for n in order],
            *[leaf(2, n) for n in order], *[leaf(3, n) for n in order])
```

```python
import functools

import jax
import jax.numpy as jnp
from jax import lax
from jax.experimental import pallas as pl
from jax.experimental.pallas import tpu as pltpu

F32 = jnp.float32
BF16 = jnp.bfloat16
NEG_INF = -1e30
EPS = 1e-6
N_DEV = 8
BLK = 128
GROUP = 4 * BLK
VMEM_LIMIT_BYTES = 56 * 1024 * 1024

D_MODEL = 1024
N_QKV = 3840
N_GATES = 3072
N_MAIN = N_QKV + N_GATES
N_FORGET = 8
N_IN = N_MAIN + N_FORGET
N_INR = N_MAIN + BLK
F_COL = 2304
FFN_HIDDEN = 2816
N_REL = 257

ADAM_LR, ADAM_B1, ADAM_B2, ADAM_EPS, ADAM_WD, ADAM_STEP = 0.001, 0.9, 0.999, 1e-08, 0.01, 10

NN = (((1,), (0,)), ((), ()))
NT = (((1,), (1,)), ((), ()))
TN = (((0,), (0,)), ((), ()))
HIGHEST = lax.Precision.HIGHEST

ATTN_COLS = {"a": (0, 4, 5), "b": (6, 10, 14), "c": (18, 22, 26)}
ATTN_WINDOW = {"a": 2, "c": 5}


def _params():
    return pltpu.CompilerParams(vmem_limit_bytes=VMEM_LIMIT_BYTES)


def _tile(n, target):
    best = None
    t = 128
    while t <= min(n, target):
        if n % t == 0:
            best = t
        t += 128
    return best if best is not None else n


def _row_tile(n, target):
    t = min(n, target)
    while n % t:
        t -= 8
    return t


TILES = {
    "proj_qkv": (1024, 1280, 1024), "proj_gates": (1024, 768, 1024), "proj_forget": (1024, 128, 1024),
    "out_proj": (1024, 512, 1024), "ffn_out": (1024, 512, 1408), "ffn_fused": (512, 1408),
    "wgrad_ffn_out": (1408, 1024, 1024), "wgrad_ffn_in": (1024, 1408, 1024), "dgrad_ffn_in": (1024, 1024, 1408),
    "wgrad_out": (1024, 1024, 1024), "dgrad_out": (1024, 1024, 1024), "wgrad_branch": (512, 1024, 1024),
    "wgrad_in": (1024, 1408, 1024), "dgrad_in": (1024, 1024, 1408),
}


def _matmul(a, b, mode, out_dtype, name, tiles, *, n=None, a_off=0, b_off=0, m=None):
    tm, tn, tk = tiles
    if mode == "nn":
        M, K = a.shape if m is None else (m, a.shape[1])
        N = b.shape[1] if n is None else n
    elif mode == "nt":
        M, K = a.shape
        N = b.shape[0] if n is None else n
    else:
        K = a.shape[0]
        M = a.shape[1] if m is None else m
        N = b.shape[1] if n is None else n
    tm = _tile(M, tm) if M % 128 == 0 else M
    tn = _tile(N, tn)
    tk = _tile(K, tk)
    nk = K // tk
    dims = {"nn": NN, "nt": NT, "tn": TN}[mode]
    if mode == "nn":
        a_spec = pl.BlockSpec((tm, tk), lambda i, j, k: (i + a_off, k))
        b_spec = pl.BlockSpec((tk, tn), lambda i, j, k: (k, j + b_off))
    elif mode == "nt":
        a_spec = pl.BlockSpec((tm, tk), lambda i, j, k: (i + a_off, k))
        b_spec = pl.BlockSpec((tn, tk), lambda i, j, k: (j + b_off, k))
    else:
        a_spec = pl.BlockSpec((tk, tm), lambda i, j, k: (k, i + a_off))
        b_spec = pl.BlockSpec((tk, tn), lambda i, j, k: (k, j + b_off))

    def body(a_ref, b_ref, o_ref, acc_ref):
        k = pl.program_id(2)
        part = lax.dot_general(a_ref[...], b_ref[...], dims, preferred_element_type=F32)
        if nk == 1:
            o_ref[...] = part.astype(o_ref.dtype)
        else:
            @pl.when(k == 0)
            def _():
                acc_ref[...] = part

            @pl.when(k > 0)
            def _():
                acc_ref[...] += part

            @pl.when(k == nk - 1)
            def _():
                o_ref[...] = acc_ref[...].astype(o_ref.dtype)

    return pl.pallas_call(
        body, name=name,
        out_shape=jax.ShapeDtypeStruct((M, N), out_dtype),
        grid=(M // tm, N // tn, nk),
        in_specs=[a_spec, b_spec],
        out_specs=pl.BlockSpec((tm, tn), lambda i, j, k: (i, j)),
        scratch_shapes=[pltpu.VMEM((tm, tn) if nk > 1 else (8, 128), F32)],
        compiler_params=_params(),
    )(a, b)


def _matmul_resid(a, b, resid, gate, name, tiles):
    M, K = a.shape
    N = b.shape[1]
    tm, tn, tk = (_tile(d, t) for d, t in zip((M, N, K), tiles))
    nk = K // tk

    def body(a_ref, b_ref, r_ref, g_ref, o_ref, s_ref, acc_ref):
        k = pl.program_id(2)
        part = jnp.dot(a_ref[...], b_ref[...], preferred_element_type=F32)

        def finish(acc):
            o_ref[...] = r_ref[...] + g_ref[...] * acc
            s_ref[...] = acc.astype(BF16)

        if nk == 1:
            finish(part)
        else:
            @pl.when(k == 0)
            def _():
                acc_ref[...] = part

            @pl.when(k > 0)
            def _():
                acc_ref[...] += part

            @pl.when(k == nk - 1)
            def _():
                finish(acc_ref[...])

    return pl.pallas_call(
        body, name=name,
        out_shape=(jax.ShapeDtypeStruct((M, N), F32), jax.ShapeDtypeStruct((M, N), BF16)),
        grid=(M // tm, N // tn, nk),
        in_specs=[pl.BlockSpec((tm, tk), lambda i, j, k: (i, k)),
                  pl.BlockSpec((tk, tn), lambda i, j, k: (k, j)),
                  pl.BlockSpec((tm, tn), lambda i, j, k: (i, j)),
                  pl.BlockSpec((1, tn), lambda i, j, k: (0, j))],
        out_specs=(pl.BlockSpec((tm, tn), lambda i, j, k: (i, j)),
                   pl.BlockSpec((tm, tn), lambda i, j, k: (i, j))),
        scratch_shapes=[pltpu.VMEM((tm, tn) if nk > 1 else (8, 128), F32)],
        compiler_params=_params(),
    )(a, b, resid, gate)


def _norm_mod_fwd(x, g, shift, scale, name):
    S, D = x.shape
    ts = _row_tile(S, 256)

    def body(x_ref, g_ref, sh_ref, sc_ref, h_ref):
        xv = x_ref[...]
        rstd = lax.rsqrt(jnp.mean(xv * xv, axis=-1, keepdims=True) + EPS)
        y = xv * rstd * g_ref[...]
        h_ref[...] = (y * (1.0 + sc_ref[...]) + sh_ref[...]).astype(BF16)

    row = pl.BlockSpec((1, D), lambda i: (0, 0))
    return pl.pallas_call(
        body, name=name, out_shape=jax.ShapeDtypeStruct((S, D), BF16), grid=(S // ts,),
        in_specs=[pl.BlockSpec((ts, D), lambda i: (i, 0)), row, row, row],
        out_specs=pl.BlockSpec((ts, D), lambda i: (i, 0)),
        compiler_params=_params(),
    )(x, g, shift, scale)


def _norm_mod_bwd(x, dh, dres, g, scale, name):
    S, D = x.shape
    ts = _row_tile(S, 256)

    def body(x_ref, dh_ref, dr_ref, g_ref, sc_ref, dx_ref, dsh_ref, dsc_ref, dg_ref):
        i = pl.program_id(0)
        xv, dhv, gv = x_ref[...], dh_ref[...], g_ref[...]
        rstd = lax.rsqrt(jnp.mean(xv * xv, axis=-1, keepdims=True) + EPS)
        xhat = xv * rstd
        dn = dhv * (1.0 + sc_ref[...])
        dxhat = dn * gv
        proj = jnp.mean(dxhat * xhat, axis=-1, keepdims=True)
        dx_ref[...] = dr_ref[...] + rstd * (dxhat - xhat * proj)
        dsh = jnp.sum(dhv, axis=0, keepdims=True)
        dsc = jnp.sum(dhv * (xhat * gv), axis=0, keepdims=True)
        dg = jnp.sum(dn * xhat, axis=0, keepdims=True)

        @pl.when(i == 0)
        def _():
            dsh_ref[...] = dsh
            dsc_ref[...] = dsc
            dg_ref[...] = dg

        @pl.when(i > 0)
        def _():
            dsh_ref[...] += dsh
            dsc_ref[...] += dsc
            dg_ref[...] += dg

    tile = pl.BlockSpec((ts, D), lambda i: (i, 0))
    row = pl.BlockSpec((1, D), lambda i: (0, 0))
    vec = jax.ShapeDtypeStruct((1, D), F32)
    return pl.pallas_call(
        body, name=name, out_shape=(jax.ShapeDtypeStruct((S, D), F32), vec, vec, vec), grid=(S // ts,),
        in_specs=[tile, tile, tile, row, row], out_specs=(tile, row, row, row),
        compiler_params=_params(),
    )(x, dh, dres, g, scale)


def _gate_bwd(dx, f, gate, name):
    S, D = dx.shape
    ts = _row_tile(S, 256)

    def body(dx_ref, f_ref, g_ref, df_ref, dg_ref):
        i = pl.program_id(0)
        dxv = dx_ref[...]
        df_ref[...] = (dxv * g_ref[...]).astype(BF16)
        dg = jnp.sum(dxv * f_ref[...].astype(F32), axis=0, keepdims=True)

        @pl.when(i == 0)
        def _():
            dg_ref[...] = dg

        @pl.when(i > 0)
        def _():
            dg_ref[...] += dg

    tile = pl.BlockSpec((ts, D), lambda i: (i, 0))
    row = pl.BlockSpec((1, D), lambda i: (0, 0))
    return pl.pallas_call(
        body, name=name,
        out_shape=(jax.ShapeDtypeStruct((S, D), BF16), jax.ShapeDtypeStruct((1, D), F32)), grid=(S // ts,),
        in_specs=[tile, tile, row], out_specs=(tile, row),
        compiler_params=_params(),
    )(dx, f, gate)


def _ffn_in_fwd(h, w, name):
    S, D = h.shape
    F = w.shape[1] // 2
    tm, tn = _tile(S, TILES["ffn_fused"][0]), _tile(F, TILES["ffn_fused"][1])
    nj = F // tn

    def body(h_ref, wg_ref, wu_ref, o_ref):
        hv = h_ref[...]
        ug = jnp.dot(hv, wg_ref[...], preferred_element_type=F32)
        uu = jnp.dot(hv, wu_ref[...], preferred_element_type=F32)
        o_ref[...] = (ug * jax.nn.sigmoid(ug) * uu).astype(BF16)

    return pl.pallas_call(
        body, name=name, out_shape=jax.ShapeDtypeStruct((S, F), BF16), grid=(nj, S // tm),
        in_specs=[pl.BlockSpec((tm, D), lambda j, i: (i, 0)),
                  pl.BlockSpec((D, tn), lambda j, i: (0, j)),
                  pl.BlockSpec((D, tn), lambda j, i: (0, j + nj))],
        out_specs=pl.BlockSpec((tm, tn), lambda j, i: (i, j)),
        compiler_params=_params(),
    )(h, w, w)


def _ffn_mid_bwd(h, df, w_in, w_out, name):
    S, D = h.shape
    F = w_in.shape[1] // 2
    tm, tn = _tile(S, TILES["ffn_fused"][0]), _tile(F, TILES["ffn_fused"][1])
    nj = F // tn

    def body(h_ref, df_ref, wg_ref, wu_ref, wo_ref, dg_ref, du_ref):
        hv = h_ref[...]
        ug = jnp.dot(hv, wg_ref[...], preferred_element_type=F32)
        uu = jnp.dot(hv, wu_ref[...], preferred_element_type=F32)
        dact = lax.dot_general(df_ref[...], wo_ref[...], NT, preferred_element_type=F32)
        sig = jax.nn.sigmoid(ug)
        dg_ref[...] = (dact * uu * (sig * (1.0 + ug * (1.0 - sig)))).astype(BF16)
        du_ref[...] = (dact * (ug * sig)).astype(BF16)

    out = jax.ShapeDtypeStruct((S, F), BF16)
    return pl.pallas_call(
        body, name=name, out_shape=(out, out), grid=(nj, S // tm),
        in_specs=[pl.BlockSpec((tm, D), lambda j, i: (i, 0)),
                  pl.BlockSpec((tm, D), lambda j, i: (i, 0)),
                  pl.BlockSpec((D, tn), lambda j, i: (0, j)),
                  pl.BlockSpec((D, tn), lambda j, i: (0, j + nj)),
                  pl.BlockSpec((tn, D), lambda j, i: (j, 0))],
        out_specs=(pl.BlockSpec((tm, tn), lambda j, i: (i, j)), pl.BlockSpec((tm, tn), lambda j, i: (i, j))),
        compiler_params=_params(),
    )(h, df, w_in, w_in, w_out)


def _merge_fwd(o_a, o_b, o_c, gates, w_branch, name, *, tm=256):
    S, W = o_a.shape
    D = w_branch.shape[2]
    tm = _row_tile(S, tm)

    def body(oa_ref, ob_ref, oc_ref, g_ref, w_ref, m_ref):
        acc = None
        for k, o_ref in enumerate((oa_ref, ob_ref, oc_ref)):
            y = jnp.dot(o_ref[...], w_ref[k], preferred_element_type=F32)
            t = jax.nn.sigmoid(g_ref[:, k * D:(k + 1) * D]) * y
            acc = t if acc is None else acc + t
        m_ref[...] = acc.astype(BF16)

    o_spec = pl.BlockSpec((tm, W), lambda i: (i, 0))
    return pl.pallas_call(
        body, name=name, out_shape=jax.ShapeDtypeStruct((S, D), BF16), grid=(S // tm,),
        in_specs=[o_spec, o_spec, o_spec, pl.BlockSpec((tm, 3 * D), lambda i: (i, 0)),
                  pl.BlockSpec((3, W, D), lambda i: (0, 0, 0))],
        out_specs=pl.BlockSpec((tm, D), lambda i: (i, 0)),
        compiler_params=_params(),
    )(o_a, o_b, o_c, gates, w_branch)


def _merge_bwd(dmerged, o_a, o_b, o_c, gates, w_branch, name, *, tm=256):
    S, W = o_a.shape
    D = w_branch.shape[2]
    tm = _row_tile(S, tm)
    n_heads = W // 64

    def body(dm_ref, oa_ref, ob_ref, oc_ref, g_ref, w_ref, dg_ref, dy_ref,
             doa_ref, dob_ref, doc_ref, dla_ref, dlb_ref, dlc_ref):
        dm = dm_ref[...]
        branches = ((oa_ref, doa_ref, dla_ref), (ob_ref, dob_ref, dlb_ref), (oc_ref, doc_ref, dlc_ref))
        for k, (o_ref, do_ref, dl_ref) in enumerate(branches):
            wk = w_ref[k]
            ov = o_ref[...]
            y = jnp.dot(ov, wk, preferred_element_type=F32)
            g = jax.nn.sigmoid(g_ref[:, k * D:(k + 1) * D])
            dy = (dm * g).astype(BF16)
            dy_ref[:, k * D:(k + 1) * D] = dy
            dg_ref[:, k * D:(k + 1) * D] = (dm * y * (g * (1.0 - g))).astype(BF16)
            do16 = lax.dot_general(dy, wk, NT, preferred_element_type=F32).astype(BF16)
            do_ref[...] = do16
            prod = do16.astype(F32) * ov.astype(F32)
            for h in range(n_heads):
                dl_ref[:, h:h + 1] = jnp.sum(prod[:, 64 * h:64 * (h + 1)], axis=1, keepdims=True)

    o_spec = pl.BlockSpec((tm, W), lambda i: (i, 0))
    wide = pl.BlockSpec((tm, 3 * D), lambda i: (i, 0))
    dl_spec = pl.BlockSpec((tm, n_heads), lambda i: (i, 0))
    o_out = jax.ShapeDtypeStruct((S, W), BF16)
    wide_out = jax.ShapeDtypeStruct((S, 3 * D), BF16)
    dl_out = jax.ShapeDtypeStruct((S, n_heads), F32)
    return pl.pallas_call(
        body, name=name, out_shape=(wide_out, wide_out, o_out, o_out, o_out, dl_out, dl_out, dl_out),
        grid=(S // tm,),
        in_specs=[pl.BlockSpec((tm, D), lambda i: (i, 0)), o_spec, o_spec, o_spec, wide,
                  pl.BlockSpec((3, W, D), lambda i: (0, 0, 0))],
        out_specs=(wide, wide, o_spec, o_spec, o_spec, dl_spec, dl_spec, dl_spec),
        compiler_params=_params(),
    )(dmerged, o_a, o_b, o_c, gates, w_branch)


def _band_mask(variant, t_abs, s_abs):
    if variant == "b":
        return s_abs <= t_abs
    qc, kc = t_abs >> 6, s_abs >> 6
    return (kc <= qc) & (kc >= qc - (2 if variant == "a" else 8))


def _attn_fwd(variant, qkv, name, *, sinks=None, slopes=None, cq_col=None, ck_row=None, bias=None):
    S = qkv.shape[0]
    nb = S // BLK
    qb, kb, vb = ATTN_COLS[variant]
    shared_kv = variant == "a"
    win = ATTN_WINDOW.get(variant)

    def body(*refs):
        if variant == "a":
            q_ref, k_ref, v_ref, sink_ref, slope_ref, o_ref, lse_ref = refs
        elif variant == "b":
            q_ref, k_ref, v_ref, cq_ref, ck_ref, o_ref, lse_ref = refs
        else:
            q_ref, k_ref, v_ref, bias_ref, o_ref, lse_ref = refs
        p, i = pl.program_id(0), pl.program_id(1)
        lane = lax.broadcasted_iota(jnp.int32, (BLK, BLK), 1)
        t_abs = i * BLK + lax.broadcasted_iota(jnp.int32, (BLK, 1), 0)
        q2 = q_ref[...].astype(F32) * 0.125

        def compute(start, n_keys):
            k_w = k_ref[pl.ds(start, n_keys), :]
            v_w = v_ref[pl.ds(start, n_keys), :]
            s_abs = start + lax.broadcasted_iota(jnp.int32, (1, n_keys), 1)
            valid = _band_mask(variant, t_abs, s_abs)
            outs = []
            for half in (0, 1):
                hmask = (lane >= 64) if half else (lane < 64)
                qh = jnp.where(hmask, q2, 0.0)
                if shared_kv:
                    swap = (p // 2) != half
                    qh = jnp.where(swap, pltpu.roll(qh, 64, 1), qh)
                s = lax.dot_general(qh.astype(BF16), k_w, NT, preferred_element_type=F32)
                if variant == "a":
                    head = 2 * p + half
                    s = s + (-slope_ref[head]) * jnp.abs(t_abs - s_abs).astype(F32)
                elif variant == "b":
                    s = s + cq_ref[:, half:half + 1] - ck_ref[half:half + 1, pl.ds(start, n_keys)]
                else:
                    j0 = start // BLK
                    s = s + jnp.concatenate(
                        [bias_ref[half, jnp.clip(i - j0 - b, 0, 4)] for b in range(win)], axis=1)
                s = jnp.where(valid, s, NEG_INF)
                m = jnp.max(s, axis=1, keepdims=True)
                if variant == "a":
                    m = jnp.maximum(m, sink_ref[head])
                pe = jnp.exp(s - m)
                l = jnp.sum(pe, axis=1, keepdims=True)
                if variant == "a":
                    l = l + jnp.exp(sink_ref[head] - m)
                out = jnp.dot(pe.astype(BF16), v_w, preferred_element_type=F32) / l
                if shared_kv:
                    out = jnp.where(swap, pltpu.roll(out, 64, 1), out)
                outs.append(out)
                lse_ref[:, half:half + 1] = m + jnp.log(l)
            o_ref[...] = jnp.where(lane < 64, outs[0], outs[1]).astype(BF16)

        if variant == "b":
            for g in range(S // GROUP):
                pl.when(i // 4 == g)(functools.partial(compute, 0, (g + 1) * GROUP))
        else:
            start = jnp.clip(i - (win - 1), 0, nb - win) * BLK
            compute(pl.multiple_of(start, BLK), win * BLK)

    kv_col = (lambda p, i: (0, kb)) if shared_kv else (lambda p, i: (0, kb + p))
    vv_col = (lambda p, i: (0, vb)) if shared_kv else (lambda p, i: (0, vb + p))
    in_specs = [pl.BlockSpec((BLK, BLK), lambda p, i: (i, qb + p)),
                pl.BlockSpec((S, BLK), kv_col), pl.BlockSpec((S, BLK), vv_col)]
    args = [qkv, qkv, qkv]
    if variant == "a":
        in_specs += [pl.BlockSpec(memory_space=pltpu.SMEM), pl.BlockSpec(memory_space=pltpu.SMEM)]
        args += [sinks, slopes]
    elif variant == "b":
        in_specs += [pl.BlockSpec((None, BLK, 2), lambda p, i: (p, i, 0)),
                     pl.BlockSpec((None, 2, S), lambda p, i: (p, 0, 0))]
        args += [cq_col, ck_row]
    else:
        in_specs += [pl.BlockSpec((2, 5, BLK, BLK), lambda p, i: (p, 0, 0, 0))]
        args += [bias]
    return pl.pallas_call(
        body, name=name,
        out_shape=(jax.ShapeDtypeStruct((S, 512), BF16), jax.ShapeDtypeStruct((4, S, 2), F32)),
        grid=(4, nb), in_specs=in_specs,
        out_specs=(pl.BlockSpec((BLK, BLK), lambda p, i: (i, p)),
                   pl.BlockSpec((None, BLK, 2), lambda p, i: (p, i, 0))),
        compiler_params=_params(),
    )(*args)


def _attn_bwd(variant, qkv, do, lse_row, delta_row, name, *, sinks=None, slopes=None, cq_row=None,
              ck_col=None, bias_t=None):
    S = qkv.shape[0]
    nb = S // BLK
    qb, kb, vb = ATTN_COLS[variant]
    shared_kv = variant == "a"
    win = ATTN_WINDOW.get(variant)

    def body(*refs):
        if variant == "a":
            (q_ref, k_ref, v_ref, do_ref, lse_ref, dl_ref, sink_ref, slope_ref,
             dq_ref, dk_ref, dv_ref, ex_ref) = refs
        elif variant == "b":
            (q_ref, k_ref, v_ref, do_ref, lse_ref, dl_ref, cq_ref, ck_ref,
             dq_ref, dk_ref, dv_ref, ex_ref, dcq_ref) = refs
        else:
            (q_ref, k_ref, v_ref, do_ref, lse_ref, dl_ref, bias_ref,
             dq_ref, dk_ref, dv_ref, ex_ref) = refs
        p, j = pl.program_id(0), pl.program_id(1)
        lane = lax.broadcasted_iota(jnp.int32, (BLK, BLK), 1)
        s_abs = j * BLK + lax.broadcasted_iota(jnp.int32, (BLK, 1), 0)
        off_k = pl.multiple_of(j * BLK, BLK)
        k2 = k_ref[...].astype(F32)
        v2 = v_ref[...].astype(F32)
        hmasks = [(lane < 64), (lane >= 64)]
        if shared_kv:
            kv_lane = (lane >> 6) == (p // 2)
            swaps = [(p // 2) != half for half in (0, 1)]
            k_src, v_src = jnp.where(kv_lane, k2, 0.0), jnp.where(kv_lane, v2, 0.0)
            k_al = [jnp.where(swaps[h], pltpu.roll(k_src, 64, 1), k_src) for h in (0, 1)]
            v_al = [jnp.where(swaps[h], pltpu.roll(v_src, 64, 1), v_src) for h in (0, 1)]
        else:
            k_al = [jnp.where(hmasks[h], k2, 0.0) for h in (0, 1)]
            v_al = [jnp.where(hmasks[h], v2, 0.0) for h in (0, 1)]
        k_al = [(t * 0.125).astype(BF16) for t in k_al]
        v_al = [t.astype(BF16) for t in v_al]

        @pl.when(j == 0)
        def _():
            dq_ref[...] = jnp.zeros_like(dq_ref)
            if variant == "b":
                dcq_ref[...] = jnp.zeros_like(dcq_ref)
            else:
                ex_ref[...] = jnp.zeros_like(ex_ref)

        def to_kv_lanes(x, h):
            x = jnp.where(hmasks[h], x, 0.0)
            if shared_kv:
                x = jnp.where(swaps[h], pltpu.roll(x, 64, 1), x)
            return x

        def compute(start, n_q):
            q_w = q_ref[pl.ds(start, n_q), :]
            do_w = do_ref[pl.ds(start, n_q), :]
            t_abs = start + lax.broadcasted_iota(jnp.int32, (1, n_q), 1)
            valid = _band_mask(variant, t_abs, s_abs)
            dk_acc = dv_acc = None
            ds_both = []
            for half in (0, 1):
                s = lax.dot_general(k_al[half], q_w, NT, preferred_element_type=F32)
                if variant == "a":
                    s = s + (-slope_ref[2 * p + half]) * jnp.abs(t_abs - s_abs).astype(F32)
                elif variant == "b":
                    s = s + cq_ref[half:half + 1, pl.ds(start, n_q)] - ck_ref[:, half:half + 1]
                else:
                    i0 = start // BLK
                    s = s + jnp.concatenate(
                        [bias_ref[half, jnp.clip(i0 + b - j, 0, 4)] for b in range(win)], axis=1)
                pr = jnp.where(valid, jnp.exp(s - lse_ref[half:half + 1, pl.ds(start, n_q)]), 0.0)
                dp = lax.dot_general(v_al[half], do_w, NT, preferred_element_type=F32)
                ds = pr * (dp - dl_ref[half:half + 1, pl.ds(start, n_q)])
                ds16 = ds.astype(BF16)
                dv_h = to_kv_lanes(jnp.dot(pr.astype(BF16), do_w, preferred_element_type=F32), half)
                dk_h = to_kv_lanes(jnp.dot(ds16, q_w, preferred_element_type=F32) * 0.125, half)
                dv_acc = dv_h if dv_acc is None else dv_acc + dv_h
                dk_acc = dk_h if dk_acc is None else dk_acc + dk_h
                ds_both.append(ds16)
                if variant == "b":
                    ex_ref[:, half:half + 1] = -jnp.sum(ds, axis=1, keepdims=True)
                    dcq_ref[half:half + 1, pl.ds(start, n_q)] += jnp.sum(ds, axis=0, keepdims=True)
                elif variant == "c":
                    for b in range(win):
                        ex_ref[half, jnp.clip(i0 + b - j, 0, 4)] += ds[:, b * BLK:(b + 1) * BLK]
            dq_t = lax.dot_general(jnp.concatenate(k_al, axis=0), jnp.concatenate(ds_both, axis=0), TN,
                                   preferred_element_type=F32)
            dq_ref[:, pl.ds(start, n_q)] += dq_t
            if shared_kv:
                @pl.when(p == 0)
                def _():
                    dk_ref[pl.ds(off_k, BLK), :] = dk_acc
                    dv_ref[pl.ds(off_k, BLK), :] = dv_acc

                @pl.when(p > 0)
                def _():
                    dk_ref[pl.ds(off_k, BLK), :] += dk_acc
                    dv_ref[pl.ds(off_k, BLK), :] += dv_acc
            else:
                dk_ref[pl.ds(off_k, BLK), :] = dk_acc
                dv_ref[pl.ds(off_k, BLK), :] = dv_acc

        if variant == "b":
            for g in range(S // GROUP):
                pl.when(j // 4 == g)(functools.partial(compute, g * GROUP, S - g * GROUP))
        else:
            start = jnp.clip(j, 0, nb - win) * BLK
            compute(pl.multiple_of(start, BLK), win * BLK)

        if variant == "a":
            for half in (0, 1):
                p_sink = jnp.exp(sink_ref[2 * p + half] - lse_ref[half:half + 1, pl.ds(off_k, BLK)])
                term = p_sink * dl_ref[half:half + 1, pl.ds(off_k, BLK)]
                ex_ref[half:half + 1, :] += -jnp.sum(term, axis=1, keepdims=True)

    col = lambda c0: (lambda p, j: (0, c0 + p))
    kv_blk = (lambda c0: (lambda p, j: (j, c0))) if shared_kv else (lambda c0: (lambda p, j: (j, c0 + p)))
    pair = lambda p, j: (0, p)
    row_stat = pl.BlockSpec((None, 2, S), lambda p, j: (p, 0, 0))
    in_specs = [pl.BlockSpec((S, BLK), col(qb)),
                pl.BlockSpec((BLK, BLK), kv_blk(kb)), pl.BlockSpec((BLK, BLK), kv_blk(vb)),
                pl.BlockSpec((S, BLK), pair), row_stat, row_stat]
    args = [qkv, qkv, qkv, do, lse_row, delta_row]
    kv_width = BLK if shared_kv else 512
    kv_out = pl.BlockSpec((S, BLK), (lambda p, j: (0, 0)) if shared_kv else pair)
    out_shape = [jax.ShapeDtypeStruct((512, S), F32), jax.ShapeDtypeStruct((S, kv_width), F32),
                 jax.ShapeDtypeStruct((S, kv_width), F32)]
    out_specs = [pl.BlockSpec((BLK, S), lambda p, j: (p, 0)), kv_out, kv_out]
    if variant == "a":
        in_specs += [pl.BlockSpec(memory_space=pltpu.SMEM), pl.BlockSpec(memory_space=pltpu.SMEM)]
        args += [sinks, slopes]
        out_shape.append(jax.ShapeDtypeStruct((4, 8, BLK), F32))
        out_specs.append(pl.BlockSpec((None, 8, BLK), lambda p, j: (p, 0, 0)))
    elif variant == "b":
        in_specs += [row_stat, pl.BlockSpec((None, BLK, 2), lambda p, j: (p, j, 0))]
        args += [cq_row, ck_col]
        out_shape += [jax.ShapeDtypeStruct((4, S, 2), F32), jax.ShapeDtypeStruct((4, 2, S), F32)]
        out_specs += [pl.BlockSpec((None, BLK, 2), lambda p, j: (p, j, 0)), row_stat]
    else:
        in_specs += [pl.BlockSpec((2, 5, BLK, BLK), lambda p, j: (p, 0, 0, 0))]
        args += [bias_t]
        out_shape.append(jax.ShapeDtypeStruct((8, 5, BLK, BLK), F32))
        out_specs.append(pl.BlockSpec((2, 5, BLK, BLK), lambda p, j: (p, 0, 0, 0)))
    return pl.pallas_call(
        body, name=name, out_shape=tuple(out_shape), grid=(4, nb),
        in_specs=in_specs, out_specs=tuple(out_specs),
        compiler_params=_params(),
    )(*args)


def _log_sigmoid(x):
    return jnp.minimum(x, 0.0) - jnp.log(1.0 + jnp.exp(-jnp.abs(x)))


def _forget_fwd(fb, b_forget, name):
    S = fb.shape[0]
    nb = S // BLK

    def body(fb_ref, b_ref, cum_ref, carry_ref):
        i = pl.program_id(0)
        logf = _log_sigmoid(fb_ref[...] + b_ref[...])
        r = lax.broadcasted_iota(jnp.int32, (BLK, BLK), 0)
        c = lax.broadcasted_iota(jnp.int32, (BLK, BLK), 1)
        tri = (c <= r).astype(F32)

        @pl.when(i == 0)
        def _():
            carry_ref[...] = jnp.zeros_like(carry_ref)

        cum = jnp.dot(tri, logf, preferred_element_type=F32, precision=HIGHEST) + carry_ref[0:1, :]
        cum_ref[...] = cum
        carry_ref[...] = jnp.broadcast_to(cum[BLK - 1:BLK, :], carry_ref.shape)

    return pl.pallas_call(
        body, name=name, out_shape=jax.ShapeDtypeStruct((S, BLK), F32), grid=(nb,),
        in_specs=[pl.BlockSpec((BLK, BLK), lambda i: (i, 0)), pl.BlockSpec((1, BLK), lambda i: (0, 0))],
        out_specs=pl.BlockSpec((BLK, BLK), lambda i: (i, 0)),
        scratch_shapes=[pltpu.VMEM((8, BLK), F32)],
        compiler_params=_params(),
    )(fb, b_forget)


def _forget_bwd(dcum_q, dcum_k, fb, b_forget, name):
    S = fb.shape[0]
    nb = S // BLK

    def body(dq_ref, dk_ref, fb_ref, b_ref, dfb_ref, db_ref, carry_ref):
        g = pl.program_id(0)
        r = lax.broadcasted_iota(jnp.int32, (BLK, BLK), 0)
        c = lax.broadcasted_iota(jnp.int32, (BLK, BLK), 1)
        tri = (c >= r).astype(F32)

        @pl.when(g == 0)
        def _():
            carry_ref[...] = jnp.zeros_like(carry_ref)

        dcum = dq_ref[...] + dk_ref[...]
        dlogf = jnp.dot(tri, dcum, preferred_element_type=F32, precision=HIGHEST) + carry_ref[0:1, :]
        carry_ref[...] = jnp.broadcast_to(dlogf[0:1, :], carry_ref.shape)
        x = fb_ref[...] + b_ref[...]
        dfb = jnp.where(c < N_FORGET, dlogf * jax.nn.sigmoid(-x), 0.0)
        dfb_ref[...] = dfb
        db = jnp.sum(dfb, axis=0, keepdims=True)

        @pl.when(g == 0)
        def _():
            db_ref[...] = db

        @pl.when(g > 0)
        def _():
            db_ref[...] += db

    rev = pl.BlockSpec((BLK, BLK), lambda g: (nb - 1 - g, 0))
    row = pl.BlockSpec((1, BLK), lambda g: (0, 0))
    return pl.pallas_call(
        body, name=name,
        out_shape=(jax.ShapeDtypeStruct((S, BLK), F32), jax.ShapeDtypeStruct((1, BLK), F32)), grid=(nb,),
        in_specs=[rev, rev, rev, row], out_specs=(rev, row),
        scratch_shapes=[pltpu.VMEM((8, BLK), F32)],
        compiler_params=_params(),
    )(dcum_q, dcum_k, fb, b_forget)


def _skew(x, sign):
    row = lax.broadcasted_iota(jnp.int32, x.shape, 0)
    for b in range(7):
        amount = (1 << b) if sign > 0 else 256 - (1 << b)
        x = jnp.where(((row >> b) & 1) == 1, pltpu.roll(x, amount, 1), x)
    return x


def _rel_bases(rel):
    far = rel[:, 256:257]
    far127 = jnp.broadcast_to(far, (rel.shape[0], 127))
    base0 = jnp.concatenate([rel[:, 128:0:-1], far, rel[:, 255:128:-1]], axis=1)
    base1 = jnp.concatenate([rel[:, 256:128:-1], far, far127], axis=1)
    base0_t = jnp.concatenate([rel[:, 128:256], far, rel[:, 1:128]], axis=1)
    base1_t = jnp.concatenate([jnp.broadcast_to(far, (rel.shape[0], 128)), far, rel[:, 129:256]], axis=1)
    return jnp.stack([base0, base1, base0_t, base1_t], axis=1)


def _rel_expand(bases, name):
    def body(b_ref, t_ref, tt_ref):
        far = jnp.broadcast_to(b_ref[1:2, 0:1], (BLK, BLK))
        for k, out_ref in ((0, t_ref), (2, tt_ref)):
            for d in (0, 1):
                x = jnp.broadcast_to(b_ref[k + d:k + d + 1, :], (BLK, 2 * BLK))
                out_ref[d] = _skew(x, 1)[:, :BLK]
            for d in (2, 3, 4):
                out_ref[d] = far

    out = jax.ShapeDtypeStruct((8, 5, BLK, BLK), F32)
    spec = pl.BlockSpec((None, 5, BLK, BLK), lambda h: (h, 0, 0, 0))
    return pl.pallas_call(
        body, name=name, out_shape=(out, out), grid=(8,),
        in_specs=[pl.BlockSpec((None, 4, 2 * BLK), lambda h: (h, 0, 0))], out_specs=(spec, spec),
        compiler_params=_params(),
    )(bases)


def _rel_reduce(dtiles_t, name):
    def body(dt_ref, o_ref):
        zeros = jnp.zeros((BLK, BLK), F32)
        sums = []
        for d in (0, 1):
            x = _skew(jnp.concatenate([dt_ref[d], zeros], axis=1), -1)
            sums.append(jnp.broadcast_to(jnp.sum(x, axis=0, keepdims=True), (8, 2 * BLK)))
        lane = lax.broadcasted_iota(jnp.int32, (8, 2 * BLK), 1)
        main = pltpu.roll(sums[0], BLK, 1) + jnp.where(lane > BLK, sums[1], 0.0)
        far = jnp.sum(jnp.where(lane < BLK, sums[1], 0.0)[0:1], axis=1, keepdims=True)
        far = far + jnp.sum(jnp.sum(dt_ref[2] + dt_ref[3] + dt_ref[4], axis=0, keepdims=True), axis=1, keepdims=True)
        o_ref[...] = jnp.concatenate([main[0:1], jnp.broadcast_to(far, (1, BLK))], axis=1)

    return pl.pallas_call(
        body, name=name, out_shape=jax.ShapeDtypeStruct((8, 1, 3 * BLK), F32), grid=(8,),
        in_specs=[pl.BlockSpec((None, 5, BLK, BLK), lambda h: (h, 0, 0, 0))],
        out_specs=pl.BlockSpec((None, 1, 3 * BLK), lambda h: (h, 0, 0)),
        compiler_params=_params(),
    )(dtiles_t)


def _final_loss(x, target, g, name):
    S, D = x.shape
    ts = _row_tile(S, 256)

    def body(x_ref, t_ref, g_ref, dx_ref, loss_ref, dg_ref):
        i = pl.program_id(0)
        xv, gv = x_ref[...], g_ref[...]
        rstd = lax.rsqrt(jnp.mean(xv * xv, axis=-1, keepdims=True) + EPS)
        xhat = xv * rstd
        err = xhat * gv - t_ref[...]
        part = 0.5 * jnp.sum(jnp.mean(err * err, axis=-1, keepdims=True), axis=0, keepdims=True)
        dy = err / D
        dg = jnp.sum(dy * xhat, axis=0, keepdims=True)
        dxhat = dy * gv
        proj = jnp.mean(dxhat * xhat, axis=-1, keepdims=True)
        dx_ref[...] = rstd * (dxhat - xhat * proj)

        @pl.when(i == 0)
        def _():
            loss_ref[...] = jnp.broadcast_to(part, loss_ref.shape)
            dg_ref[...] = dg

        @pl.when(i > 0)
        def _():
            loss_ref[...] += jnp.broadcast_to(part, loss_ref.shape)
            dg_ref[...] += dg

    tile = pl.BlockSpec((ts, D), lambda i: (i, 0))
    row = pl.BlockSpec((1, D), lambda i: (0, 0))
    return pl.pallas_call(
        body, name=name,
        out_shape=(jax.ShapeDtypeStruct((S, D), F32), jax.ShapeDtypeStruct((8, 128), F32),
                   jax.ShapeDtypeStruct((1, D), F32)),
        grid=(S // ts,), in_specs=[tile, tile, row],
        out_specs=(tile, pl.BlockSpec((8, 128), lambda i: (0, 0)), row),
        compiler_params=_params(),
    )(x, target, g)


def _ada_fwd(c_all, w_ada, name):
    L, D, E = w_ada.shape

    def body(c_ref, w_ref, o_ref):
        cv = c_ref[...]
        cond = cv * jax.nn.sigmoid(cv)
        o_ref[...] = jnp.dot(cond, w_ref[...], preferred_element_type=F32, precision=HIGHEST)

    return pl.pallas_call(
        body, name=name, out_shape=jax.ShapeDtypeStruct((L, N_DEV, E), F32), grid=(L,),
        in_specs=[pl.BlockSpec((N_DEV, D), lambda l: (0, 0)), pl.BlockSpec((None, D, E), lambda l: (l, 0, 0))],
        out_specs=pl.BlockSpec((None, N_DEV, E), lambda l: (l, 0, 0)),
        compiler_params=_params(),
    )(c_all, w_ada)


def _ada_bwd(c_all_t, dmod, name):
    D = c_all_t.shape[0]
    L, _, E = dmod.shape

    def body(c_ref, d_ref, o_ref):
        cv = c_ref[...]
        cond = cv * jax.nn.sigmoid(cv)
        acc = None
        for b in range(N_DEV):
            t = cond[:, b:b + 1] * d_ref[b:b + 1, :]
            acc = t if acc is None else acc + t
        o_ref[...] = acc

    return pl.pallas_call(
        body, name=name, out_shape=jax.ShapeDtypeStruct((L, D, E), F32), grid=(L,),
        in_specs=[pl.BlockSpec((D, N_DEV), lambda l: (0, 0)), pl.BlockSpec((None, N_DEV, E), lambda l: (l, 0, 0))],
        out_specs=pl.BlockSpec((None, D, E), lambda l: (l, 0, 0)),
        compiler_params=_params(),
    )(c_all_t, dmod)


def _adamw(w, m, v, g_parts, name):
    L, R, C = w.shape
    tr = _row_tile(R, max(8, (256 * 1024 // max(C, 128)) // 8 * 8))
    nr = R // tr
    c1 = 1.0 - ADAM_B1 ** ADAM_STEP
    c2 = 1.0 - ADAM_B2 ** ADAM_STEP

    def body(w_ref, m_ref, v_ref, *rest):
        g_refs, (go_ref, d_ref, mo_ref, vo_ref) = rest[:L], rest[L:]
        layer = pl.program_id(0)
        g = None
        for l, g_ref in enumerate(g_refs):
            gl = g_ref[0].astype(F32)
            for k in range(1, g_ref.shape[0]):
                gl = gl + g_ref[k].astype(F32)
            g = gl if g is None else jnp.where(layer == l, gl, g)
        mn = ADAM_B1 * m_ref[...] + (1.0 - ADAM_B1) * g
        vn = ADAM_B2 * v_ref[...] + (1.0 - ADAM_B2) * (g * g)
        m_hat = mn / c1
        v_hat = vn / c2
        go_ref[...] = g
        d_ref[...] = -ADAM_LR * (m_hat / (jnp.sqrt(v_hat) + ADAM_EPS) + ADAM_WD * w_ref[...])
        mo_ref[...] = mn
        vo_ref[...] = vn

    def parts_spec(l, n_parts):
        return pl.BlockSpec((n_parts, tr, C), lambda layer, i: (0, jnp.where(layer == l, i, 0 if l > 0 else nr - 1), 0))

    tile = pl.BlockSpec((None, tr, C), lambda layer, i: (layer, i, 0))
    out = jax.ShapeDtypeStruct((L, R, C), F32)
    return pl.pallas_call(
        body, name=name, out_shape=(out, out, out, out), grid=(L, nr),
        in_specs=[tile, tile, tile] + [parts_spec(l, p.shape[0]) for l, p in enumerate(g_parts)],
        out_specs=(tile, tile, tile, tile),
        compiler_params=_params(),
    )(w, m, v, *g_parts)


def _pair_add(pieces, recv, core, name):
    _, _, R, C = pieces.shape
    tr = _row_tile(R, max(8, (512 * 1024 // max(C, 128)) // 8 * 8))

    def body(core_ref, a_ref, b_ref, o_ref):
        o_ref[...] = (a_ref[...].astype(F32) + b_ref[...].astype(F32)).astype(BF16)

    return pl.pallas_call(
        body, name=name, out_shape=jax.ShapeDtypeStruct((4, R, C), BF16),
        grid_spec=pltpu.PrefetchScalarGridSpec(
            num_scalar_prefetch=1, grid=(4, R // tr),
            in_specs=[pl.BlockSpec((None, None, tr, C), lambda k, i, core_ref: (core_ref[0], k, i, 0)),
                      pl.BlockSpec((None, tr, C), lambda k, i, core_ref: (k, i, 0))],
            out_specs=pl.BlockSpec((None, tr, C), lambda k, i, core_ref: (k, i, 0))),
        compiler_params=_params(),
    )(core, pieces, recv)


MESH = pl.DeviceIdType.MESH
ANY = pl.BlockSpec(memory_space=pl.ANY)


def _position():
    return lax.axis_index("x"), lax.axis_index("y"), lax.axis_index("c")


def _small_all_gather(v, name):
    m_per, n = v.shape

    def body(x_ref, out_ref, send_sems, recv_sems, local_sem):
        x, y, c = _position()
        me, sibling = (x, y, c), (x, y, 1 - c)
        chips = [(1 - x, y), (x, 1 - y), (1 - x, 1 - y)]

        def rows(px, py, pc):
            return out_ref.at[pl.ds((4 * px + 2 * py + pc) * m_per, m_per), :]

        def copy(k, block, to, src=None):
            return pltpu.make_async_remote_copy(
                src_ref=rows(*block) if src is None else src, dst_ref=rows(*block),
                send_sem=send_sems.at[k], recv_sem=recv_sems.at[k], device_id=to, device_id_type=MESH)

        mine = pltpu.make_async_copy(x_ref, rows(*me), local_sem)
        mine.start()
        first = [copy(0, me, sibling, src=x_ref)]
        first += [copy(1 + j, me, (*chip, c), src=x_ref) for j, chip in enumerate(chips)]
        for cp in first:
            cp.start()
        passed = [copy(4 + j, (*chip, c), sibling) for j, chip in enumerate(chips)]
        for j, chip in enumerate(chips):
            copy(1 + j, (*chip, c), me).wait_recv()
            passed[j].start()
        copy(0, sibling, me).wait_recv()
        for j, chip in enumerate(chips):
            copy(4 + j, (*chip, 1 - c), me).wait_recv()
        for cp in first + passed:
            cp.wait_send()
        mine.wait()

    return pl.pallas_call(
        body, name=name, out_shape=jax.ShapeDtypeStruct((N_DEV * m_per, n), v.dtype),
        in_specs=[pl.BlockSpec(memory_space=pltpu.VMEM)], out_specs=pl.BlockSpec(memory_space=pltpu.VMEM),
        scratch_shapes=[pltpu.SemaphoreType.DMA((7,)), pltpu.SemaphoreType.DMA((7,)), pltpu.SemaphoreType.DMA],
    )(v)


def _big_all_gather(shards, name):
    n_arr = len(shards)

    def body(*refs):
        x_refs, out_refs = refs[:n_arr], refs[n_arr:2 * n_arr]
        send_sems, recv_sems, local_sems = refs[2 * n_arr:]
        x, y, c = _position()
        me, sibling = (x, y, c), (x, y, 1 - c)
        chips = [(1 - x, y), (x, 1 - y), (1 - x, 1 - y)]

        def slot(a, px, py, pc):
            return out_refs[a].at[4 * px + 2 * py + pc]

        def copy(a, k, block, to, src=None):
            return pltpu.make_async_remote_copy(
                src_ref=slot(a, *block) if src is None else src, dst_ref=slot(a, *block),
                send_sem=send_sems.at[a, k], recv_sem=recv_sems.at[a, k], device_id=to, device_id_type=MESH)

        mine = [pltpu.make_async_copy(x_refs[a], slot(a, *me), local_sems.at[a]) for a in range(n_arr)]
        for cp in mine:
            cp.start()
        first = []
        for j, chip in enumerate(chips):
            first += [copy(a, 1 + j, me, (*chip, c), src=x_refs[a]) for a in range(n_arr)]
        first += [copy(a, 0, me, sibling, src=x_refs[a]) for a in range(n_arr)]
        for cp in first:
            cp.start()
        passed = []
        for j, chip in enumerate(chips):
            for a in range(n_arr):
                copy(a, 1 + j, (*chip, c), me).wait_recv()
                fwd = copy(a, 4 + j, (*chip, c), sibling)
                fwd.start()
                passed.append(fwd)
        for a in range(n_arr):
            copy(a, 0, sibling, me).wait_recv()
        for j, chip in enumerate(chips):
            for a in range(n_arr):
                copy(a, 4 + j, (*chip, 1 - c), me).wait_recv()
        for cp in first + passed:
            cp.wait_send()
        for cp in mine:
            cp.wait()

    return pl.pallas_call(
        body, name=name,
        out_shape=tuple(jax.ShapeDtypeStruct((N_DEV,) + s.shape, s.dtype) for s in shards),
        in_specs=[ANY] * n_arr, out_specs=tuple([ANY] * n_arr),
        scratch_shapes=[pltpu.SemaphoreType.DMA((n_arr, 7)), pltpu.SemaphoreType.DMA((n_arr, 7)),
                        pltpu.SemaphoreType.DMA((n_arr,))],
    )(*shards)


def _sibling_exchange(pieces, name):
    n_arr = len(pieces)

    def body(*refs):
        p_refs, out_refs = refs[:n_arr], refs[n_arr:2 * n_arr]
        send_sems, recv_sems = refs[2 * n_arr:]
        x, y, c = _position()
        copies = [pltpu.make_async_remote_copy(
            src_ref=p_refs[a].at[1 - c], dst_ref=out_refs[a], send_sem=send_sems.at[a], recv_sem=recv_sems.at[a],
            device_id=(x, y, 1 - c), device_id_type=MESH) for a in range(n_arr)]
        for cp in copies:
            cp.start()
        for cp in copies:
            cp.wait()

    return pl.pallas_call(
        body, name=name,
        out_shape=tuple(jax.ShapeDtypeStruct(p.shape[1:], p.dtype) for p in pieces),
        in_specs=[ANY] * n_arr, out_specs=tuple([ANY] * n_arr),
        scratch_shapes=[pltpu.SemaphoreType.DMA((n_arr,)), pltpu.SemaphoreType.DMA((n_arr,))],
    )(*pieces)


def _chip_exchange(sums, name):
    n_arr = len(sums)

    def body(*refs):
        s_refs, out_refs = refs[:n_arr], refs[n_arr:2 * n_arr]
        send_sems, recv_sems, local_sems = refs[2 * n_arr:]
        x, y, c = _position()
        my_chip = 2 * x + y
        chips = [(1 - x, y), (x, 1 - y), (1 - x, 1 - y)]
        mine = [pltpu.make_async_copy(s_refs[a].at[my_chip], out_refs[a].at[my_chip], local_sems.at[a])
                for a in range(n_arr)]
        for cp in mine:
            cp.start()
        copies = []
        for j, (px, py) in enumerate(chips):
            copies += [pltpu.make_async_remote_copy(
                src_ref=s_refs[a].at[2 * px + py], dst_ref=out_refs[a].at[my_chip],
                send_sem=send_sems.at[a, j], recv_sem=recv_sems.at[a, j],
                device_id=(px, py, c), device_id_type=MESH) for a in range(n_arr)]
        for cp in copies:
            cp.start()
        for j, (px, py) in enumerate(chips):
            for a in range(n_arr):
                pltpu.make_async_remote_copy(
                    src_ref=s_refs[a].at[my_chip], dst_ref=out_refs[a].at[2 * px + py],
                    send_sem=send_sems.at[a, j], recv_sem=recv_sems.at[a, j],
                    device_id=(px, py, c), device_id_type=MESH).wait_recv()
        for cp in copies:
            cp.wait_send()
        for cp in mine:
            cp.wait()

    return pl.pallas_call(
        body, name=name,
        out_shape=tuple(jax.ShapeDtypeStruct(s.shape, s.dtype) for s in sums),
        in_specs=[ANY] * n_arr, out_specs=tuple([ANY] * n_arr),
        scratch_shapes=[pltpu.SemaphoreType.DMA((n_arr, 3)), pltpu.SemaphoreType.DMA((n_arr, 3)),
                        pltpu.SemaphoreType.DMA((n_arr,))],
    )(*sums)


HBM = pl.BlockSpec(memory_space=pltpu.HBM)
SEM = pl.BlockSpec(memory_space=pltpu.SEMAPHORE)
EFFECT = pltpu.SideEffectType.DATAFLOW_SIDE_EFFECTING
RELATIONS = [(rx, ry, rc) for rx in (0, 1) for ry in (0, 1) for rc in (0, 1)][1:]


def _exchange_copies(src_refs, land_refs, send_sems, recv_sems, scatter, receive_side):
    x, y, c = _position()
    me = 4 * x + 2 * y + c
    copies = []
    for k, (rx, ry, rc) in enumerate(RELATIONS):
        peer = ((1 - x) if rx else x, (1 - y) if ry else y, (1 - c) if rc else c)
        peer_index = 4 * peer[0] + 2 * peer[1] + peer[2]
        for a, (src, land) in enumerate(zip(src_refs, land_refs)):
            copies.append(pltpu.make_async_remote_copy(
                src_ref=src.at[peer_index] if scatter else src,
                dst_ref=land.at[peer_index if receive_side else me],
                send_sem=send_sems.at[a * len(RELATIONS) + k], recv_sem=recv_sems.at[a * len(RELATIONS) + k],
                device_id=peer, device_id_type=MESH))
    return copies


def _exchange_start(srcs, scatter, after, name):
    n = len(srcs)
    land_shapes = [(s.shape if scatter else (N_DEV,) + s.shape) for s in srcs]

    def body(*refs):
        src_refs, land_refs = refs[:n], refs[n:2 * n]
        send_sems, recv_sems = refs[2 * n + 1], refs[2 * n + 2]
        token = refs[-1]
        for cp in _exchange_copies(src_refs, land_refs, send_sems, recv_sems, scatter, False):
            cp.start()
        token[...] = jnp.zeros_like(token)

    sems = pltpu.SemaphoreType.DMA((n * len(RELATIONS),))
    outs = pl.pallas_call(
        body, name=name,
        out_shape=(sems, sems, *[pltpu.HBM(s.shape, s.dtype) for s in srcs],
                   *[pltpu.HBM(shape, s.dtype) for shape, s in zip(land_shapes, srcs)],
                   jax.ShapeDtypeStruct((8, 128), F32)),
        in_specs=[HBM] * (2 * n) + [ANY],
        out_specs=(SEM, SEM, *[HBM] * (2 * n), pl.BlockSpec(memory_space=pltpu.VMEM)),
        input_output_aliases={a: 2 + a for a in range(2 * n)},
        compiler_params=pltpu.CompilerParams(has_side_effects=EFFECT),
    )(*[pltpu.with_memory_space_constraint(s, pltpu.HBM) for s in srcs],
      *[pltpu.with_memory_space_constraint(lax.empty(shape, s.dtype), pltpu.HBM)
        for shape, s in zip(land_shapes, srcs)], after)
    return outs[0], outs[1], outs[2:2 + n], outs[2 + n:2 + 2 * n], outs[-1]


def _exchange_wait(started, scatter, after, name):
    send_sems, recv_sems, srcs, lands, _ = started
    n = len(srcs)

    def body(*refs):
        src_refs, land_refs = refs[:n], refs[n:2 * n]
        send_sems, recv_sems = refs[2 * n], refs[2 * n + 1]
        copies = _exchange_copies(src_refs, land_refs, send_sems, recv_sems, scatter, True)
        for cp in copies:
            cp.wait_send()
        for cp in copies:
            cp.wait_recv()

    outs = pl.pallas_call(
        body, name=name,
        out_shape=(*[pltpu.HBM(s.shape, s.dtype) for s in srcs], *[pltpu.HBM(t.shape, t.dtype) for t in lands]),
        in_specs=[HBM] * (2 * n) + [SEM, SEM, ANY], out_specs=tuple([HBM] * (2 * n)),
        input_output_aliases={a: a for a in range(2 * n)},
        compiler_params=pltpu.CompilerParams(has_side_effects=EFFECT),
    )(*srcs, *lands, send_sems, recv_sems, after)
    return outs[:n], outs[n:]


W_IN_SHARD = N_IN // N_DEV
F_SHARD = F_COL // W_IN_SHARD
F_LO = F_COL - F_SHARD * W_IN_SHARD


def _cols_from_shards(g):
    return jnp.concatenate([g[d] for d in range(N_DEV)], axis=1)


def _w_in_rearranged(g):
    parts = [g[d] for d in range(N_DEV)]
    parts[F_SHARD:F_SHARD + 1] = [g[F_SHARD][:, :F_LO], g[F_SHARD][:, F_LO + N_FORGET:]]
    parts += [g[F_SHARD][:, F_LO:F_LO + N_FORGET], jnp.zeros((g.shape[1], BLK - N_FORGET), g.dtype)]
    return jnp.concatenate(parts, axis=1)


def _w_in_pieces(dw_r):
    def original(lo, hi):
        shift = 0 if hi <= F_COL else N_FORGET
        return dw_r[:, lo - shift:hi - shift]

    pieces = []
    for d in range(N_DEV):
        lo, hi = d * W_IN_SHARD, (d + 1) * W_IN_SHARD
        if d == F_SHARD:
            pieces.append(jnp.concatenate([original(lo, F_COL), dw_r[:, N_MAIN:N_MAIN + N_FORGET],
                                           original(F_COL + N_FORGET, hi)], axis=1))
        else:
            pieces.append(original(lo, hi))
    return jnp.stack(pieces)


def _col_pieces(dw):
    width = dw.shape[1] // N_DEV
    return jnp.stack([dw[:, d * width:(d + 1) * width] for d in range(N_DEV)])


def _row_pieces(dw):
    return dw.reshape(N_DEV, dw.shape[0] // N_DEV, dw.shape[1])


def _branch_pieces(dw):
    k, w, d = dw.shape
    return jnp.transpose(dw.reshape(k, w, N_DEV, d // N_DEV), (2, 0, 1, 3)).reshape(N_DEV, k * w, d // N_DEV)


def _pair_major(p8):
    return jnp.stack([p8[0::2], p8[1::2]])


def _pairs_col(a):
    return jnp.transpose(a.reshape(a.shape[0], 4, 2), (1, 0, 2))


def _pairs_row(a):
    return jnp.transpose(a.reshape(a.shape[0], 4, 2), (1, 2, 0))


def _heads_from_col(a):
    return jnp.transpose(a, (1, 0, 2)).reshape(a.shape[1], 8)


def _heads_from_row(a):
    return jnp.transpose(a, (2, 0, 1)).reshape(a.shape[2], 8)


def _pad_lanes(a, n):
    return jnp.pad(a, [(0, 0)] * (a.ndim - 1) + [(0, n - a.shape[-1])])


SMALL_SEGMENTS = (("dmod", 2 * 6 * D_MODEL), ("norm_mix_g", 2 * D_MODEL), ("norm_ffn_g", 2 * D_MODEL),
                  ("final_norm_g", D_MODEL), ("b_forget", 128), ("sinks", 128), ("rel_bias", 4224))
SMALL_ROWS = 176


def _pack_small(parts):
    flat = [_pad_lanes(parts[name].reshape(1, -1), size) for name, size in SMALL_SEGMENTS]
    total = sum(size for _, size in SMALL_SEGMENTS)
    flat.append(jnp.zeros((1, SMALL_ROWS * 128 - total), F32))
    return jnp.concatenate(flat, axis=1).reshape(SMALL_ROWS, 128)


def _unpack_small(packed, shapes):
    flat = packed.reshape(-1)
    out, pos = {}, 0
    for name, size in SMALL_SEGMENTS:
        shape = shapes[name]
        count = 1
        for d in shape:
            count *= d
        out[name] = flat[pos:pos + count].reshape(shape)
        pos += size
    return out


def kernel(x, c, norm_mix_g, norm_ffn_g, w_ada, b_ada, w_in, b_forget, sinks, rel_bias, w_branch, w_out, w_ffn_in, w_ffn_out, final_norm_g, loss_target, m_norm_mix_g, m_norm_ffn_g, m_w_ada, m_b_ada, m_w_in, m_b_forget, m_sinks, m_rel_bias, m_w_branch, m_w_out, m_w_ffn_in, m_w_ffn_out, m_final_norm_g, v_norm_mix_g, v_norm_ffn_g, v_w_ada, v_b_ada, v_w_in, v_b_forget, v_sinks, v_rel_bias, v_w_branch, v_w_out, v_w_ffn_in, v_w_ffn_out, v_final_norm_g):
    depth = w_in.shape[0]
    S, D = x.shape[1], x.shape[2]
    assert S % GROUP == 0 and S >= ATTN_WINDOW["c"] * BLK
    px, py, pc = _position()
    me = 4 * px + 2 * py + pc
    x0 = x[0]

    assert depth == 2
    big_weights = (w_in, w_branch, w_out, w_ffn_in, w_ffn_out)

    def full_matrices(g_in, g_branch, g_out, g_fin, g_fout):
        return dict(w_in=_w_in_rearranged(g_in),
                    w_branch=jnp.transpose(g_branch, (1, 2, 0, 3)).reshape(3, 512, D),
                    w_out=g_out.reshape(D, D), w_fin=_cols_from_shards(g_fin),
                    w_fout=g_fout.reshape(FFN_HIDDEN, D))

    shards = [[w[l].astype(BF16) for w in big_weights] for l in range(depth)]
    gathered0 = _big_all_gather(shards[0], "comm_gather_weights0")
    gather1 = _exchange_start(shards[1], False, gathered0[0], "comm_gather_weights1_start")
    weights = [full_matrices(*gathered0), None]
    W_in, W_branch, W_out, W_fin, W_fout = ([weights[0][k], None] for k in ("w_in", "w_branch", "w_out", "w_fin", "w_fout"))

    c_all = _small_all_gather(c.reshape(8, 128), "comm_gather_c").reshape(N_DEV, D)
    mod_cols = _ada_fwd(c_all, w_ada, "ada_fwd")
    mod_all = _small_all_gather(mod_cols.reshape(-1, 128), "comm_gather_mod")
    mod_all = mod_all.reshape(N_DEV, depth, N_DEV, w_ada.shape[2])
    mod_mine = lax.dynamic_index_in_dim(mod_all, me, axis=2, keepdims=False)
    mod = jnp.transpose(mod_mine, (1, 0, 2)).reshape(depth, 6 * D) + b_ada + gather1[4][0:1, 0:1]
    mods = [[mod[l:l + 1, k * D:(k + 1) * D] for k in range(6)] for l in range(depth)]

    slopes = jnp.exp2(-jnp.arange(1, 9, dtype=F32))
    saved = []
    xs = x0
    for l in range(depth):
        if l == 1:
            mine, landed = _exchange_wait(gather1, False, xs, "comm_gather_weights1_wait")
            weights[1] = full_matrices(*[lax.dynamic_update_index_in_dim(t, s, me, 0) for t, s in zip(landed, mine)])
            for held, k in ((W_in, "w_in"), (W_branch, "w_branch"), (W_out, "w_out"), (W_fin, "w_fin"),
                            (W_fout, "w_fout")):
                held[1] = weights[1][k]
        sh_m, sc_m, g_m, sh_f, sc_f, g_f = mods[l]
        gm, gf = norm_mix_g[l:l + 1], norm_ffn_g[l:l + 1]
        bfor = _pad_lanes(b_forget[l:l + 1], BLK)
        h = _norm_mod_fwd(xs, gm, sh_m, sc_m, f"norm_mix_fwd{l}")
        qkv = _matmul(h, W_in[l], "nn", BF16, f"proj_qkv{l}", TILES["proj_qkv"], n=N_QKV)
        gates = _matmul(h, W_in[l], "nn", F32, f"proj_gates{l}", TILES["proj_gates"], n=N_GATES,
                        b_off=N_QKV // TILES["proj_gates"][1])
        fb = _matmul(h, W_in[l], "nn", F32, f"proj_forget{l}", TILES["proj_forget"], n=BLK, b_off=N_MAIN // BLK)
        cum = _forget_fwd(fb, bfor, f"forget_fwd{l}")[:, :N_FORGET]
        cum_col, cum_row = _pairs_col(cum), _pairs_row(cum)
        tiles, tiles_t = _rel_expand(_rel_bases(rel_bias[l]), f"rel_expand{l}")
        o_a, lse_a = _attn_fwd("a", qkv, f"attn_a_fwd{l}", sinks=sinks[l], slopes=slopes)
        o_b, lse_b = _attn_fwd("b", qkv, f"attn_b_fwd{l}", cq_col=cum_col, ck_row=cum_row)
        o_c, lse_c = _attn_fwd("c", qkv, f"attn_c_fwd{l}", bias=tiles)
        merged = _merge_fwd(o_a, o_b, o_c, gates, W_branch[l], f"merge_fwd{l}")
        x1, mix = _matmul_resid(merged, W_out[l], xs, g_m, f"out_proj{l}", TILES["out_proj"])
        h2 = _norm_mod_fwd(x1, gf, sh_f, sc_f, f"norm_ffn_fwd{l}")
        act = _ffn_in_fwd(h2, W_fin[l], f"ffn_in_fwd{l}")
        x2, ffn = _matmul_resid(act, W_fout[l], x1, g_f, f"ffn_out{l}", TILES["ffn_out"])
        saved.append(dict(x=xs, h=h, qkv=qkv, gates=gates, fb=fb, bfor=bfor, cum_col=cum_col, cum_row=cum_row,
                          tiles_t=tiles_t, o=(o_a, o_b, o_c), lse=(lse_a, lse_b, lse_c), merged=merged, mix=mix,
                          x1=x1, h2=h2, act=act, ffn=ffn))
        xs = x2

    dx, loss_tile, d_final_g = _final_loss(xs, loss_target[0], final_norm_g.reshape(1, D), "final_loss")
    loss = lax.psum(loss_tile[0, 0], ("x", "y", "c"))

    grads = {k: [None] * depth for k in ("w_in", "w_branch", "w_out", "w_ffn_in", "w_ffn_out", "norm_mix_g",
                                          "norm_ffn_g", "b_forget", "sinks", "rel_bias", "dmod")}
    def layer_pieces(l):
        return [_w_in_pieces(grads["w_in"][l]), _branch_pieces(grads["w_branch"][l]), _row_pieces(grads["w_out"][l]),
                _col_pieces(grads["w_ffn_in"][l]), _row_pieces(grads["w_ffn_out"][l])]

    reduce1 = None
    for l in reversed(range(depth)):
        sv = saved[l]
        sh_m, sc_m, g_m, sh_f, sc_f, g_f = mods[l]
        if l == 0:
            g_f = g_f + reduce1[4][0:1, 0:1]
        gm, gf = norm_mix_g[l:l + 1], norm_ffn_g[l:l + 1]
        df, d_g_f = _gate_bwd(dx, sv["ffn"], g_f, f"ffn_gate_bwd{l}")
        du_g, du_u = _ffn_mid_bwd(sv["h2"], df, W_fin[l], W_fout[l], f"ffn_mid_bwd{l}")
        du = jnp.concatenate([du_g, du_u], axis=1)
        grads["w_ffn_out"][l] = _matmul(sv["act"], df, "tn", BF16, f"wgrad_ffn_out{l}", TILES["wgrad_ffn_out"])
        grads["w_ffn_in"][l] = _matmul(sv["h2"], du, "tn", BF16, f"wgrad_ffn_in{l}", TILES["wgrad_ffn_in"])
        dh2 = _matmul(du, W_fin[l], "nt", F32, f"dgrad_ffn_in{l}", TILES["dgrad_ffn_in"])
        dx1, d_sh_f, d_sc_f, d_gf = _norm_mod_bwd(sv["x1"], dh2, dx, gf, sc_f, f"norm_ffn_bwd{l}")
        dmix, d_g_m = _gate_bwd(dx1, sv["mix"], g_m, f"mix_gate_bwd{l}")
        grads["w_out"][l] = _matmul(sv["merged"], dmix, "tn", BF16, f"wgrad_out{l}", TILES["wgrad_out"])
        dmerged = _matmul(dmix, W_out[l], "nt", F32, f"dgrad_out{l}", TILES["dgrad_out"])
        o_a, o_b, o_c = sv["o"]
        dgates, dy, do_a, do_b, do_c, dl_a, dl_b, dl_c = _merge_bwd(
            dmerged, o_a, o_b, o_c, sv["gates"], W_branch[l], f"merge_bwd{l}")
        dwb = [_matmul(o_k, dy, "tn", BF16, f"wgrad_branch{l}_{k}", TILES["wgrad_branch"], n=D,
                       b_off=k * (D // TILES["wgrad_branch"][1])) for k, o_k in enumerate((o_a, o_b, o_c))]
        grads["w_branch"][l] = jnp.stack(dwb)
        lse_rows = [_pairs_row(_heads_from_col(t)) for t in sv["lse"]]
        dqt_a, dk_a, dv_a, dsink = _attn_bwd("a", sv["qkv"], do_a, lse_rows[0], _pairs_row(dl_a), f"attn_a_bwd{l}",
                                             sinks=sinks[l], slopes=slopes)
        dqt_b, dk_b, dv_b, dck, dcq = _attn_bwd("b", sv["qkv"], do_b, lse_rows[1], _pairs_row(dl_b),
                                                f"attn_b_bwd{l}", cq_row=sv["cum_row"], ck_col=sv["cum_col"])
        dqt_c, dk_c, dv_c, dtiles_t = _attn_bwd("c", sv["qkv"], do_c, lse_rows[2], _pairs_row(dl_c),
                                                f"attn_c_bwd{l}", bias_t=sv["tiles_t"])
        grads["sinks"][l] = dsink[:, :2, 0].reshape(8)
        grads["rel_bias"][l] = _rel_reduce(dtiles_t, f"rel_reduce{l}")[:, 0, :N_REL]
        dcum_k = _pad_lanes(_heads_from_col(dck), BLK)
        dcum_q = _pad_lanes(_heads_from_row(dcq), BLK)
        dfb, d_bfor = _forget_bwd(dcum_q, dcum_k, sv["fb"], sv["bfor"], f"forget_bwd{l}")
        grads["b_forget"][l] = d_bfor[0, :N_FORGET]
        dproj = jnp.concatenate(
            [t.astype(BF16) for t in (dqt_a.T, dk_a, dv_a, dqt_b.T, dk_b, dv_b, dqt_c.T, dk_c, dv_c)]
            + [dgates, dfb.astype(BF16)], axis=1)
        grads["w_in"][l] = _matmul(sv["h"], dproj, "tn", BF16, f"wgrad_in{l}", TILES["wgrad_in"])
        dh = _matmul(dproj, W_in[l], "nt", F32, f"dgrad_in{l}", TILES["dgrad_in"])
        dx, d_sh_m, d_sc_m, d_gm = _norm_mod_bwd(sv["x"], dh, dx1, gm, sc_m, f"norm_mix_bwd{l}")
        grads["norm_mix_g"][l] = d_gm[0]
        grads["norm_ffn_g"][l] = d_gf[0]
        grads["dmod"][l] = jnp.concatenate([d_sh_m, d_sc_m, d_g_m, d_sh_f, d_sc_f, d_g_f], axis=1)[0]
        if l == 1:
            reduce1 = _exchange_start(layer_pieces(1), True, dx, "comm_reduce1_start")

    grad_x = dx.reshape(x.shape)

    small_shapes = dict(dmod=b_ada.shape, norm_mix_g=norm_mix_g.shape, norm_ffn_g=norm_ffn_g.shape,
                        final_norm_g=final_norm_g.shape, b_forget=b_forget.shape, sinks=sinks.shape,
                        rel_bias=rel_bias.shape)
    mine_small = _pack_small(dict(
        dmod=jnp.stack(grads["dmod"]), norm_mix_g=jnp.stack(grads["norm_mix_g"]),
        norm_ffn_g=jnp.stack(grads["norm_ffn_g"]), final_norm_g=d_final_g[0],
        b_forget=_pad_lanes(jnp.stack(grads["b_forget"]).reshape(1, -1), 128),
        sinks=_pad_lanes(jnp.stack(grads["sinks"]).reshape(1, -1), 128),
        rel_bias=_pad_lanes(jnp.stack(grads["rel_bias"]).reshape(1, -1), 4224)))
    all_small = _small_all_gather(mine_small, "comm_gather_small").reshape(N_DEV, SMALL_ROWS, 128)

    def pack_params(b_ada_, nm, nf, fn, bf, sk, rb):
        return _pack_small(dict(dmod=b_ada_, norm_mix_g=nm, norm_ffn_g=nf, final_norm_g=fn,
                                b_forget=_pad_lanes(bf.reshape(1, -1), 128), sinks=_pad_lanes(sk.reshape(1, -1), 128),
                                rel_bias=_pad_lanes(rb.reshape(1, -1), 4224)))

    small_out = _adamw(
        pack_params(b_ada, norm_mix_g, norm_ffn_g, final_norm_g, b_forget, sinks, rel_bias)[None],
        pack_params(m_b_ada, m_norm_mix_g, m_norm_ffn_g, m_final_norm_g, m_b_forget, m_sinks, m_rel_bias)[None],
        pack_params(v_b_ada, v_norm_mix_g, v_norm_ffn_g, v_final_norm_g, v_b_forget, v_sinks, v_rel_bias)[None],
        [all_small], "adamw_small")
    small_out = [_unpack_small(t[0], small_shapes) for t in small_out]

    dmod_all = all_small[:, :96].reshape(N_DEV, depth, 6 * D)
    dmod_cols = lax.dynamic_slice_in_dim(dmod_all, me * w_ada.shape[2], w_ada.shape[2], axis=2)
    d_w_ada = _ada_bwd(jnp.transpose(c_all), jnp.transpose(dmod_cols, (1, 0, 2)), "ada_bwd")

    pieces0 = [_pair_major(p) for p in layer_pieces(0)]
    from_sibling = _sibling_exchange(pieces0, "comm_reduce0_sibling")
    core = pc.astype(jnp.int32).reshape(1)
    pair_sums = [_pair_add(p, r, core, f"pair_add{a}") for a, (p, r) in enumerate(zip(pieces0, from_sibling))]
    parts0 = _chip_exchange(pair_sums, "comm_reduce0_chips")
    pieces1, landed1 = _exchange_wait(reduce1, True, parts0[0], "comm_reduce1_wait")
    parts1 = [lax.dynamic_update_index_in_dim(t, lax.dynamic_index_in_dim(p, me, 0, keepdims=False), me, 0)
              for t, p in zip(landed1, pieces1)]

    big = {}
    for a, (name, w, m, v) in enumerate((
            ("w_in", w_in, m_w_in, v_w_in), ("w_branch", w_branch, m_w_branch, v_w_branch),
            ("w_out", w_out, m_w_out, v_w_out), ("w_ffn_in", w_ffn_in, m_w_ffn_in, v_w_ffn_in),
            ("w_ffn_out", w_ffn_out, m_w_ffn_out, v_w_ffn_out))):
        per_layer = lambda t: t.reshape(depth, -1, t.shape[-1])
        outs = _adamw(per_layer(w), per_layer(m), per_layer(v), [parts0[a], parts1[a]], f"adamw_{name}")
        big[name] = [t.reshape(w.shape) for t in outs]
    big["w_ada"] = _adamw(w_ada, m_w_ada, v_w_ada, [d_w_ada[l:l + 1] for l in range(depth)], "adamw_w_ada")

    def leaf(kind, name):
        if name in big:
            return big[name][kind]
        return small_out[kind]["dmod" if name == "b_ada" else name]

    order = ["norm_mix_g", "norm_ffn_g", "w_ada", "b_ada", "w_in", "b_forget", "sinks", "rel_bias", "w_branch",
             "w_out", "w_ffn_in", "w_ffn_out", "final_norm_g"]
    return (loss, grad_x, *[leaf(0, n) for n in order], *[leaf(1, n) for n in order],
            *[leaf(2, n) for n in order], *[leaf(3, n) for n in order])
```

```python
import functools

import jax
import jax.numpy as jnp
from jax import lax
from jax.experimental import pallas as pl
from jax.experimental.pallas import tpu as pltpu

F32 = jnp.float32
BF16 = jnp.bfloat16
NEG_INF = -1e30
EPS = 1e-6
N_DEV = 8
BLK = 128
GROUP = 4 * BLK
VMEM_LIMIT_BYTES = 56 * 1024 * 1024

D_MODEL = 1024
N_QKV = 3840
N_GATES = 3072
N_MAIN = N_QKV + N_GATES
N_FORGET = 8
N_IN = N_MAIN + N_FORGET
N_INR = N_MAIN + BLK
F_COL = 2304
FFN_HIDDEN = 2816
N_REL = 257

ADAM_LR, ADAM_B1, ADAM_B2, ADAM_EPS, ADAM_WD, ADAM_STEP = 0.001, 0.9, 0.999, 1e-08, 0.01, 10

NN = (((1,), (0,)), ((), ()))
NT = (((1,), (1,)), ((), ()))
TN = (((0,), (0,)), ((), ()))
HIGHEST = lax.Precision.HIGHEST

ATTN_COLS = {"a": (0, 4, 5), "b": (6, 10, 14), "c": (18, 22, 26)}
ATTN_WINDOW = {"a": 2, "c": 5}


def _params():
    return pltpu.CompilerParams(vmem_limit_bytes=VMEM_LIMIT_BYTES)


def _tile(n, target):
    best = None
    t = 128
    while t <= min(n, target):
        if n % t == 0:
            best = t
        t += 128
    return best if best is not None else n


def _row_tile(n, target):
    t = min(n, target)
    while n % t:
        t -= 8
    return t


TILES = {
    "proj_qkv": (1024, 1280, 1024), "proj_gates": (1024, 768, 1024), "proj_forget": (1024, 128, 1024),
    "out_proj": (1024, 512, 1024), "ffn_out": (1024, 512, 1408), "ffn_fused": (512, 1408),
    "wgrad_ffn_out": (1408, 1024, 1024), "wgrad_ffn_in": (1024, 1408, 1024), "dgrad_ffn_in": (1024, 1024, 1408),
    "wgrad_out": (1024, 1024, 1024), "dgrad_out": (1024, 1024, 1024), "wgrad_branch": (512, 1024, 1024),
    "wgrad_in": (1024, 1408, 1024), "dgrad_in": (1024, 1024, 1408),
}


def _matmul(a, b, mode, out_dtype, name, tiles, *, n=None, a_off=0, b_off=0, m=None, after=None):
    tm, tn, tk = tiles
    if mode == "nn":
        M, K = a.shape if m is None else (m, a.shape[1])
        N = b.shape[1] if n is None else n
    elif mode == "nt":
        M, K = a.shape
        N = b.shape[0] if n is None else n
    else:
        K = a.shape[0]
        M = a.shape[1] if m is None else m
        N = b.shape[1] if n is None else n
    tm = _tile(M, tm) if M % 128 == 0 else M
    tn = _tile(N, tn)
    tk = _tile(K, tk)
    nk = K // tk
    dims = {"nn": NN, "nt": NT, "tn": TN}[mode]
    if mode == "nn":
        a_spec = pl.BlockSpec((tm, tk), lambda i, j, k: (i + a_off, k))
        b_spec = pl.BlockSpec((tk, tn), lambda i, j, k: (k, j + b_off))
    elif mode == "nt":
        a_spec = pl.BlockSpec((tm, tk), lambda i, j, k: (i + a_off, k))
        b_spec = pl.BlockSpec((tn, tk), lambda i, j, k: (j + b_off, k))
    else:
        a_spec = pl.BlockSpec((tk, tm), lambda i, j, k: (k, i + a_off))
        b_spec = pl.BlockSpec((tk, tn), lambda i, j, k: (k, j + b_off))

    def body(a_ref, b_ref, *rest):
        o_ref, acc_ref = rest[-2:]
        k = pl.program_id(2)
        part = lax.dot_general(a_ref[...], b_ref[...], dims, preferred_element_type=F32)
        if nk == 1:
            o_ref[...] = part.astype(o_ref.dtype)
        else:
            @pl.when(k == 0)
            def _():
                acc_ref[...] = part

            @pl.when(k > 0)
            def _():
                acc_ref[...] += part

            @pl.when(k == nk - 1)
            def _():
                o_ref[...] = acc_ref[...].astype(o_ref.dtype)

    return pl.pallas_call(
        body, name=name,
        out_shape=jax.ShapeDtypeStruct((M, N), out_dtype),
        grid=(M // tm, N // tn, nk),
        in_specs=[a_spec, b_spec] + ([ANY] if after is not None else []),
        out_specs=pl.BlockSpec((tm, tn), lambda i, j, k: (i, j)),
        scratch_shapes=[pltpu.VMEM((tm, tn) if nk > 1 else (8, 128), F32)],
        compiler_params=_params(),
    )(a, b, *([after] if after is not None else []))


def _matmul_resid(a, b, resid, gate, name, tiles):
    M, K = a.shape
    N = b.shape[1]
    tm, tn, tk = (_tile(d, t) for d, t in zip((M, N, K), tiles))
    nk = K // tk

    def body(a_ref, b_ref, r_ref, g_ref, o_ref, s_ref, acc_ref):
        k = pl.program_id(2)
        part = jnp.dot(a_ref[...], b_ref[...], preferred_element_type=F32)

        def finish(acc):
            o_ref[...] = r_ref[...] + g_ref[...] * acc
            s_ref[...] = acc.astype(BF16)

        if nk == 1:
            finish(part)
        else:
            @pl.when(k == 0)
            def _():
                acc_ref[...] = part

            @pl.when(k > 0)
            def _():
                acc_ref[...] += part

            @pl.when(k == nk - 1)
            def _():
                finish(acc_ref[...])

    return pl.pallas_call(
        body, name=name,
        out_shape=(jax.ShapeDtypeStruct((M, N), F32), jax.ShapeDtypeStruct((M, N), BF16)),
        grid=(M // tm, N // tn, nk),
        in_specs=[pl.BlockSpec((tm, tk), lambda i, j, k: (i, k)),
                  pl.BlockSpec((tk, tn), lambda i, j, k: (k, j)),
                  pl.BlockSpec((tm, tn), lambda i, j, k: (i, j)),
                  pl.BlockSpec((1, tn), lambda i, j, k: (0, j))],
        out_specs=(pl.BlockSpec((tm, tn), lambda i, j, k: (i, j)),
                   pl.BlockSpec((tm, tn), lambda i, j, k: (i, j))),
        scratch_shapes=[pltpu.VMEM((tm, tn) if nk > 1 else (8, 128), F32)],
        compiler_params=_params(),
    )(a, b, resid, gate)


def _norm_mod_fwd(x, g, shift, scale, name):
    S, D = x.shape
    ts = _row_tile(S, 256)

    def body(x_ref, g_ref, sh_ref, sc_ref, h_ref):
        xv = x_ref[...]
        rstd = lax.rsqrt(jnp.mean(xv * xv, axis=-1, keepdims=True) + EPS)
        y = xv * rstd * g_ref[...]
        h_ref[...] = (y * (1.0 + sc_ref[...]) + sh_ref[...]).astype(BF16)

    row = pl.BlockSpec((1, D), lambda i: (0, 0))
    return pl.pallas_call(
        body, name=name, out_shape=jax.ShapeDtypeStruct((S, D), BF16), grid=(S // ts,),
        in_specs=[pl.BlockSpec((ts, D), lambda i: (i, 0)), row, row, row],
        out_specs=pl.BlockSpec((ts, D), lambda i: (i, 0)),
        compiler_params=_params(),
    )(x, g, shift, scale)


def _norm_mod_bwd(x, dh, dres, g, scale, name):
    S, D = x.shape
    ts = _row_tile(S, 256)

    def body(x_ref, dh_ref, dr_ref, g_ref, sc_ref, dx_ref, dsh_ref, dsc_ref, dg_ref):
        i = pl.program_id(0)
        xv, dhv, gv = x_ref[...], dh_ref[...], g_ref[...]
        rstd = lax.rsqrt(jnp.mean(xv * xv, axis=-1, keepdims=True) + EPS)
        xhat = xv * rstd
        dn = dhv * (1.0 + sc_ref[...])
        dxhat = dn * gv
        proj = jnp.mean(dxhat * xhat, axis=-1, keepdims=True)
        dx_ref[...] = dr_ref[...] + rstd * (dxhat - xhat * proj)
        dsh = jnp.sum(dhv, axis=0, keepdims=True)
        dsc = jnp.sum(dhv * (xhat * gv), axis=0, keepdims=True)
        dg = jnp.sum(dn * xhat, axis=0, keepdims=True)

        @pl.when(i == 0)
        def _():
            dsh_ref[...] = dsh
            dsc_ref[...] = dsc
            dg_ref[...] = dg

        @pl.when(i > 0)
        def _():
            dsh_ref[...] += dsh
            dsc_ref[...] += dsc
            dg_ref[...] += dg

    tile = pl.BlockSpec((ts, D), lambda i: (i, 0))
    row = pl.BlockSpec((1, D), lambda i: (0, 0))
    vec = jax.ShapeDtypeStruct((1, D), F32)
    return pl.pallas_call(
        body, name=name, out_shape=(jax.ShapeDtypeStruct((S, D), F32), vec, vec, vec), grid=(S // ts,),
        in_specs=[tile, tile, tile, row, row], out_specs=(tile, row, row, row),
        compiler_params=_params(),
    )(x, dh, dres, g, scale)


def _gate_bwd(dx, f, gate, name):
    S, D = dx.shape
    ts = _row_tile(S, 256)

    def body(dx_ref, f_ref, g_ref, df_ref, dg_ref):
        i = pl.program_id(0)
        dxv = dx_ref[...]
        df_ref[...] = (dxv * g_ref[...]).astype(BF16)
        dg = jnp.sum(dxv * f_ref[...].astype(F32), axis=0, keepdims=True)

        @pl.when(i == 0)
        def _():
            dg_ref[...] = dg

        @pl.when(i > 0)
        def _():
            dg_ref[...] += dg

    tile = pl.BlockSpec((ts, D), lambda i: (i, 0))
    row = pl.BlockSpec((1, D), lambda i: (0, 0))
    return pl.pallas_call(
        body, name=name,
        out_shape=(jax.ShapeDtypeStruct((S, D), BF16), jax.ShapeDtypeStruct((1, D), F32)), grid=(S // ts,),
        in_specs=[tile, tile, row], out_specs=(tile, row),
        compiler_params=_params(),
    )(dx, f, gate)


def _ffn_in_fwd(h, w, name):
    S, D = h.shape
    F = w.shape[1] // 2
    tm, tn = _tile(S, TILES["ffn_fused"][0]), _tile(F, TILES["ffn_fused"][1])
    nj = F // tn

    def body(h_ref, wg_ref, wu_ref, o_ref):
        hv = h_ref[...]
        ug = jnp.dot(hv, wg_ref[...], preferred_element_type=F32)
        uu = jnp.dot(hv, wu_ref[...], preferred_element_type=F32)
        o_ref[...] = (ug * jax.nn.sigmoid(ug) * uu).astype(BF16)

    return pl.pallas_call(
        body, name=name, out_shape=jax.ShapeDtypeStruct((S, F), BF16), grid=(nj, S // tm),
        in_specs=[pl.BlockSpec((tm, D), lambda j, i: (i, 0)),
                  pl.BlockSpec((D, tn), lambda j, i: (0, j)),
                  pl.BlockSpec((D, tn), lambda j, i: (0, j + nj))],
        out_specs=pl.BlockSpec((tm, tn), lambda j, i: (i, j)),
        compiler_params=_params(),
    )(h, w, w)


def _ffn_mid_bwd(h, df, w_in, w_out, name):
    S, D = h.shape
    F = w_in.shape[1] // 2
    tm, tn = _tile(S, TILES["ffn_fused"][0]), _tile(F, TILES["ffn_fused"][1])
    nj = F // tn

    def body(h_ref, df_ref, wg_ref, wu_ref, wo_ref, dg_ref, du_ref):
        hv = h_ref[...]
        ug = jnp.dot(hv, wg_ref[...], preferred_element_type=F32)
        uu = jnp.dot(hv, wu_ref[...], preferred_element_type=F32)
        dact = lax.dot_general(df_ref[...], wo_ref[...], NT, preferred_element_type=F32)
        sig = jax.nn.sigmoid(ug)
        dg_ref[...] = (dact * uu * (sig * (1.0 + ug * (1.0 - sig)))).astype(BF16)
        du_ref[...] = (dact * (ug * sig)).astype(BF16)

    out = jax.ShapeDtypeStruct((S, F), BF16)
    return pl.pallas_call(
        body, name=name, out_shape=(out, out), grid=(nj, S // tm),
        in_specs=[pl.BlockSpec((tm, D), lambda j, i: (i, 0)),
                  pl.BlockSpec((tm, D), lambda j, i: (i, 0)),
                  pl.BlockSpec((D, tn), lambda j, i: (0, j)),
                  pl.BlockSpec((D, tn), lambda j, i: (0, j + nj)),
                  pl.BlockSpec((tn, D), lambda j, i: (j, 0))],
        out_specs=(pl.BlockSpec((tm, tn), lambda j, i: (i, j)), pl.BlockSpec((tm, tn), lambda j, i: (i, j))),
        compiler_params=_params(),
    )(h, df, w_in, w_in, w_out)


def _merge_fwd(o_a, o_b, o_c, gates, w_branch, name, *, tm=256):
    S, W = o_a.shape
    D = w_branch.shape[2]
    tm = _row_tile(S, tm)

    def body(oa_ref, ob_ref, oc_ref, g_ref, w_ref, m_ref):
        acc = None
        for k, o_ref in enumerate((oa_ref, ob_ref, oc_ref)):
            y = jnp.dot(o_ref[...], w_ref[k], preferred_element_type=F32)
            t = jax.nn.sigmoid(g_ref[:, k * D:(k + 1) * D]) * y
            acc = t if acc is None else acc + t
        m_ref[...] = acc.astype(BF16)

    o_spec = pl.BlockSpec((tm, W), lambda i: (i, 0))
    return pl.pallas_call(
        body, name=name, out_shape=jax.ShapeDtypeStruct((S, D), BF16), grid=(S // tm,),
        in_specs=[o_spec, o_spec, o_spec, pl.BlockSpec((tm, 3 * D), lambda i: (i, 0)),
                  pl.BlockSpec((3, W, D), lambda i: (0, 0, 0))],
        out_specs=pl.BlockSpec((tm, D), lambda i: (i, 0)),
        compiler_params=_params(),
    )(o_a, o_b, o_c, gates, w_branch)


def _merge_bwd(dmerged, o_a, o_b, o_c, gates, w_branch, name, *, tm=256):
    S, W = o_a.shape
    D = w_branch.shape[2]
    tm = _row_tile(S, tm)
    n_heads = W // 64

    def body(dm_ref, oa_ref, ob_ref, oc_ref, g_ref, w_ref, dg_ref, dy_ref,
             doa_ref, dob_ref, doc_ref, dla_ref, dlb_ref, dlc_ref):
        dm = dm_ref[...]
        branches = ((oa_ref, doa_ref, dla_ref), (ob_ref, dob_ref, dlb_ref), (oc_ref, doc_ref, dlc_ref))
        for k, (o_ref, do_ref, dl_ref) in enumerate(branches):
            wk = w_ref[k]
            ov = o_ref[...]
            y = jnp.dot(ov, wk, preferred_element_type=F32)
            g = jax.nn.sigmoid(g_ref[:, k * D:(k + 1) * D])
            dy = (dm * g).astype(BF16)
            dy_ref[:, k * D:(k + 1) * D] = dy
            dg_ref[:, k * D:(k + 1) * D] = (dm * y * (g * (1.0 - g))).astype(BF16)
            do16 = lax.dot_general(dy, wk, NT, preferred_element_type=F32).astype(BF16)
            do_ref[...] = do16
            prod = do16.astype(F32) * ov.astype(F32)
            for h in range(n_heads):
                dl_ref[:, h:h + 1] = jnp.sum(prod[:, 64 * h:64 * (h + 1)], axis=1, keepdims=True)

    o_spec = pl.BlockSpec((tm, W), lambda i: (i, 0))
    wide = pl.BlockSpec((tm, 3 * D), lambda i: (i, 0))
    dl_spec = pl.BlockSpec((tm, n_heads), lambda i: (i, 0))
    o_out = jax.ShapeDtypeStruct((S, W), BF16)
    wide_out = jax.ShapeDtypeStruct((S, 3 * D), BF16)
    dl_out = jax.ShapeDtypeStruct((S, n_heads), F32)
    return pl.pallas_call(
        body, name=name, out_shape=(wide_out, wide_out, o_out, o_out, o_out, dl_out, dl_out, dl_out),
        grid=(S // tm,),
        in_specs=[pl.BlockSpec((tm, D), lambda i: (i, 0)), o_spec, o_spec, o_spec, wide,
                  pl.BlockSpec((3, W, D), lambda i: (0, 0, 0))],
        out_specs=(wide, wide, o_spec, o_spec, o_spec, dl_spec, dl_spec, dl_spec),
        compiler_params=_params(),
    )(dmerged, o_a, o_b, o_c, gates, w_branch)


def _band_mask(variant, t_abs, s_abs):
    if variant == "b":
        return s_abs <= t_abs
    qc, kc = t_abs >> 6, s_abs >> 6
    return (kc <= qc) & (kc >= qc - (2 if variant == "a" else 8))


def _attn_fwd(variant, qkv, name, *, sinks=None, slopes=None, cq_col=None, ck_row=None, bias=None):
    S = qkv.shape[0]
    nb = S // BLK
    qb, kb, vb = ATTN_COLS[variant]
    shared_kv = variant == "a"
    win = ATTN_WINDOW.get(variant)

    def body(*refs):
        if variant == "a":
            q_ref, k_ref, v_ref, sink_ref, slope_ref, o_ref, lse_ref = refs
        elif variant == "b":
            q_ref, k_ref, v_ref, cq_ref, ck_ref, o_ref, lse_ref = refs
        else:
            q_ref, k_ref, v_ref, bias_ref, o_ref, lse_ref = refs
        p, i = pl.program_id(0), pl.program_id(1)
        lane = lax.broadcasted_iota(jnp.int32, (BLK, BLK), 1)
        t_abs = i * BLK + lax.broadcasted_iota(jnp.int32, (BLK, 1), 0)
        q2 = q_ref[...].astype(F32) * 0.125

        def compute(start, n_keys):
            k_w = k_ref[pl.ds(start, n_keys), :]
            v_w = v_ref[pl.ds(start, n_keys), :]
            s_abs = start + lax.broadcasted_iota(jnp.int32, (1, n_keys), 1)
            valid = _band_mask(variant, t_abs, s_abs)
            outs = []
            for half in (0, 1):
                hmask = (lane >= 64) if half else (lane < 64)
                qh = jnp.where(hmask, q2, 0.0)
                if shared_kv:
                    swap = (p // 2) != half
                    qh = jnp.where(swap, pltpu.roll(qh, 64, 1), qh)
                s = lax.dot_general(qh.astype(BF16), k_w, NT, preferred_element_type=F32)
                if variant == "a":
                    head = 2 * p + half
                    s = s + (-slope_ref[head]) * jnp.abs(t_abs - s_abs).astype(F32)
                elif variant == "b":
                    s = s + cq_ref[:, half:half + 1] - ck_ref[half:half + 1, pl.ds(start, n_keys)]
                else:
                    j0 = start // BLK
                    s = s + jnp.concatenate(
                        [bias_ref[half, jnp.clip(i - j0 - b, 0, 4)] for b in range(win)], axis=1)
                s = jnp.where(valid, s, NEG_INF)
                m = jnp.max(s, axis=1, keepdims=True)
                if variant == "a":
                    m = jnp.maximum(m, sink_ref[head])
                pe = jnp.exp(s - m)
                l = jnp.sum(pe, axis=1, keepdims=True)
                if variant == "a":
                    l = l + jnp.exp(sink_ref[head] - m)
                out = jnp.dot(pe.astype(BF16), v_w, preferred_element_type=F32) / l
                if shared_kv:
                    out = jnp.where(swap, pltpu.roll(out, 64, 1), out)
                outs.append(out)
                lse_ref[:, half:half + 1] = m + jnp.log(l)
            o_ref[...] = jnp.where(lane < 64, outs[0], outs[1]).astype(BF16)

        if variant == "b":
            for g in range(S // GROUP):
                pl.when(i // 4 == g)(functools.partial(compute, 0, (g + 1) * GROUP))
        else:
            start = jnp.clip(i - (win - 1), 0, nb - win) * BLK
            compute(pl.multiple_of(start, BLK), win * BLK)

    kv_col = (lambda p, i: (0, kb)) if shared_kv else (lambda p, i: (0, kb + p))
    vv_col = (lambda p, i: (0, vb)) if shared_kv else (lambda p, i: (0, vb + p))
    in_specs = [pl.BlockSpec((BLK, BLK), lambda p, i: (i, qb + p)),
                pl.BlockSpec((S, BLK), kv_col), pl.BlockSpec((S, BLK), vv_col)]
    args = [qkv, qkv, qkv]
    if variant == "a":
        in_specs += [pl.BlockSpec(memory_space=pltpu.SMEM), pl.BlockSpec(memory_space=pltpu.SMEM)]
        args += [sinks, slopes]
    elif variant == "b":
        in_specs += [pl.BlockSpec((None, BLK, 2), lambda p, i: (p, i, 0)),
                     pl.BlockSpec((None, 2, S), lambda p, i: (p, 0, 0))]
        args += [cq_col, ck_row]
    else:
        in_specs += [pl.BlockSpec((2, 5, BLK, BLK), lambda p, i: (p, 0, 0, 0))]
        args += [bias]
    return pl.pallas_call(
        body, name=name,
        out_shape=(jax.ShapeDtypeStruct((S, 512), BF16), jax.ShapeDtypeStruct((4, S, 2), F32)),
        grid=(4, nb), in_specs=in_specs,
        out_specs=(pl.BlockSpec((BLK, BLK), lambda p, i: (i, p)),
                   pl.BlockSpec((None, BLK, 2), lambda p, i: (p, i, 0))),
        compiler_params=_params(),
    )(*args)


def _attn_bwd(variant, qkv, do, lse_row, delta_row, name, *, sinks=None, slopes=None, cq_row=None,
              ck_col=None, bias_t=None):
    S = qkv.shape[0]
    nb = S // BLK
    qb, kb, vb = ATTN_COLS[variant]
    shared_kv = variant == "a"
    win = ATTN_WINDOW.get(variant)

    def body(*refs):
        if variant == "a":
            (q_ref, k_ref, v_ref, do_ref, lse_ref, dl_ref, sink_ref, slope_ref,
             dq_ref, dk_ref, dv_ref, ex_ref) = refs
        elif variant == "b":
            (q_ref, k_ref, v_ref, do_ref, lse_ref, dl_ref, cq_ref, ck_ref,
             dq_ref, dk_ref, dv_ref, ex_ref, dcq_ref) = refs
        else:
            (q_ref, k_ref, v_ref, do_ref, lse_ref, dl_ref, bias_ref,
             dq_ref, dk_ref, dv_ref, ex_ref) = refs
        p, j = pl.program_id(0), pl.program_id(1)
        lane = lax.broadcasted_iota(jnp.int32, (BLK, BLK), 1)
        s_abs = j * BLK + lax.broadcasted_iota(jnp.int32, (BLK, 1), 0)
        off_k = pl.multiple_of(j * BLK, BLK)
        k2 = k_ref[...].astype(F32)
        v2 = v_ref[...].astype(F32)
        hmasks = [(lane < 64), (lane >= 64)]
        if shared_kv:
            kv_lane = (lane >> 6) == (p // 2)
            swaps = [(p // 2) != half for half in (0, 1)]
            k_src, v_src = jnp.where(kv_lane, k2, 0.0), jnp.where(kv_lane, v2, 0.0)
            k_al = [jnp.where(swaps[h], pltpu.roll(k_src, 64, 1), k_src) for h in (0, 1)]
            v_al = [jnp.where(swaps[h], pltpu.roll(v_src, 64, 1), v_src) for h in (0, 1)]
        else:
            k_al = [jnp.where(hmasks[h], k2, 0.0) for h in (0, 1)]
            v_al = [jnp.where(hmasks[h], v2, 0.0) for h in (0, 1)]
        k_al = [(t * 0.125).astype(BF16) for t in k_al]
        v_al = [t.astype(BF16) for t in v_al]

        @pl.when(j == 0)
        def _():
            dq_ref[...] = jnp.zeros_like(dq_ref)
            if variant == "b":
                dcq_ref[...] = jnp.zeros_like(dcq_ref)
            else:
                ex_ref[...] = jnp.zeros_like(ex_ref)

        def to_kv_lanes(x, h):
            x = jnp.where(hmasks[h], x, 0.0)
            if shared_kv:
                x = jnp.where(swaps[h], pltpu.roll(x, 64, 1), x)
            return x

        def compute(start, n_q):
            q_w = q_ref[pl.ds(start, n_q), :]
            do_w = do_ref[pl.ds(start, n_q), :]
            t_abs = start + lax.broadcasted_iota(jnp.int32, (1, n_q), 1)
            valid = _band_mask(variant, t_abs, s_abs)
            dk_acc = dv_acc = None
            ds_both = []
            for half in (0, 1):
                s = lax.dot_general(k_al[half], q_w, NT, preferred_element_type=F32)
                if variant == "a":
                    s = s + (-slope_ref[2 * p + half]) * jnp.abs(t_abs - s_abs).astype(F32)
                elif variant == "b":
                    s = s + cq_ref[half:half + 1, pl.ds(start, n_q)] - ck_ref[:, half:half + 1]
                else:
                    i0 = start // BLK
                    s = s + jnp.concatenate(
                        [bias_ref[half, jnp.clip(i0 + b - j, 0, 4)] for b in range(win)], axis=1)
                pr = jnp.where(valid, jnp.exp(s - lse_ref[half:half + 1, pl.ds(start, n_q)]), 0.0)
                dp = lax.dot_general(v_al[half], do_w, NT, preferred_element_type=F32)
                ds = pr * (dp - dl_ref[half:half + 1, pl.ds(start, n_q)])
                ds16 = ds.astype(BF16)
                dv_h = to_kv_lanes(jnp.dot(pr.astype(BF16), do_w, preferred_element_type=F32), half)
                dk_h = to_kv_lanes(jnp.dot(ds16, q_w, preferred_element_type=F32) * 0.125, half)
                dv_acc = dv_h if dv_acc is None else dv_acc + dv_h
                dk_acc = dk_h if dk_acc is None else dk_acc + dk_h
                ds_both.append(ds16)
                if variant == "b":
                    ex_ref[:, half:half + 1] = -jnp.sum(ds, axis=1, keepdims=True)
                    dcq_ref[half:half + 1, pl.ds(start, n_q)] += jnp.sum(ds, axis=0, keepdims=True)
                elif variant == "c":
                    for b in range(win):
                        ex_ref[half, jnp.clip(i0 + b - j, 0, 4)] += ds[:, b * BLK:(b + 1) * BLK]
            dq_t = lax.dot_general(jnp.concatenate(k_al, axis=0), jnp.concatenate(ds_both, axis=0), TN,
                                   preferred_element_type=F32)
            dq_ref[:, pl.ds(start, n_q)] += dq_t
            if shared_kv:
                @pl.when(p == 0)
                def _():
                    dk_ref[pl.ds(off_k, BLK), :] = dk_acc
                    dv_ref[pl.ds(off_k, BLK), :] = dv_acc

                @pl.when(p > 0)
                def _():
                    dk_ref[pl.ds(off_k, BLK), :] += dk_acc
                    dv_ref[pl.ds(off_k, BLK), :] += dv_acc
            else:
                dk_ref[pl.ds(off_k, BLK), :] = dk_acc
                dv_ref[pl.ds(off_k, BLK), :] = dv_acc

        if variant == "b":
            for g in range(S // GROUP):
                pl.when(j // 4 == g)(functools.partial(compute, g * GROUP, S - g * GROUP))
        else:
            start = jnp.clip(j, 0, nb - win) * BLK
            compute(pl.multiple_of(start, BLK), win * BLK)

        if variant == "a":
            for half in (0, 1):
                p_sink = jnp.exp(sink_ref[2 * p + half] - lse_ref[half:half + 1, pl.ds(off_k, BLK)])
                term = p_sink * dl_ref[half:half + 1, pl.ds(off_k, BLK)]
                ex_ref[half:half + 1, :] += -jnp.sum(term, axis=1, keepdims=True)

    col = lambda c0: (lambda p, j: (0, c0 + p))
    kv_blk = (lambda c0: (lambda p, j: (j, c0))) if shared_kv else (lambda c0: (lambda p, j: (j, c0 + p)))
    pair = lambda p, j: (0, p)
    row_stat = pl.BlockSpec((None, 2, S), lambda p, j: (p, 0, 0))
    in_specs = [pl.BlockSpec((S, BLK), col(qb)),
                pl.BlockSpec((BLK, BLK), kv_blk(kb)), pl.BlockSpec((BLK, BLK), kv_blk(vb)),
                pl.BlockSpec((S, BLK), pair), row_stat, row_stat]
    args = [qkv, qkv, qkv, do, lse_row, delta_row]
    kv_width = BLK if shared_kv else 512
    kv_out = pl.BlockSpec((S, BLK), (lambda p, j: (0, 0)) if shared_kv else pair)
    out_shape = [jax.ShapeDtypeStruct((512, S), F32), jax.ShapeDtypeStruct((S, kv_width), F32),
                 jax.ShapeDtypeStruct((S, kv_width), F32)]
    out_specs = [pl.BlockSpec((BLK, S), lambda p, j: (p, 0)), kv_out, kv_out]
    if variant == "a":
        in_specs += [pl.BlockSpec(memory_space=pltpu.SMEM), pl.BlockSpec(memory_space=pltpu.SMEM)]
        args += [sinks, slopes]
        out_shape.append(jax.ShapeDtypeStruct((4, 8, BLK), F32))
        out_specs.append(pl.BlockSpec((None, 8, BLK), lambda p, j: (p, 0, 0)))
    elif variant == "b":
        in_specs += [row_stat, pl.BlockSpec((None, BLK, 2), lambda p, j: (p, j, 0))]
        args += [cq_row, ck_col]
        out_shape += [jax.ShapeDtypeStruct((4, S, 2), F32), jax.ShapeDtypeStruct((4, 2, S), F32)]
        out_specs += [pl.BlockSpec((None, BLK, 2), lambda p, j: (p, j, 0)), row_stat]
    else:
        in_specs += [pl.BlockSpec((2, 5, BLK, BLK), lambda p, j: (p, 0, 0, 0))]
        args += [bias_t]
        out_shape.append(jax.ShapeDtypeStruct((8, 5, BLK, BLK), F32))
        out_specs.append(pl.BlockSpec((2, 5, BLK, BLK), lambda p, j: (p, 0, 0, 0)))
    return pl.pallas_call(
        body, name=name, out_shape=tuple(out_shape), grid=(4, nb),
        in_specs=in_specs, out_specs=tuple(out_specs),
        compiler_params=_params(),
    )(*args)


def _log_sigmoid(x):
    return jnp.minimum(x, 0.0) - jnp.log(1.0 + jnp.exp(-jnp.abs(x)))


def _forget_fwd(fb, b_forget, name):
    S = fb.shape[0]
    nb = S // BLK

    def body(fb_ref, b_ref, cum_ref, carry_ref):
        i = pl.program_id(0)
        logf = _log_sigmoid(fb_ref[...] + b_ref[...])
        r = lax.broadcasted_iota(jnp.int32, (BLK, BLK), 0)
        c = lax.broadcasted_iota(jnp.int32, (BLK, BLK), 1)
        tri = (c <= r).astype(F32)

        @pl.when(i == 0)
        def _():
            carry_ref[...] = jnp.zeros_like(carry_ref)

        cum = jnp.dot(tri, logf, preferred_element_type=F32, precision=HIGHEST) + carry_ref[0:1, :]
        cum_ref[...] = cum
        carry_ref[...] = jnp.broadcast_to(cum[BLK - 1:BLK, :], carry_ref.shape)

    return pl.pallas_call(
        body, name=name, out_shape=jax.ShapeDtypeStruct((S, BLK), F32), grid=(nb,),
        in_specs=[pl.BlockSpec((BLK, BLK), lambda i: (i, 0)), pl.BlockSpec((1, BLK), lambda i: (0, 0))],
        out_specs=pl.BlockSpec((BLK, BLK), lambda i: (i, 0)),
        scratch_shapes=[pltpu.VMEM((8, BLK), F32)],
        compiler_params=_params(),
    )(fb, b_forget)


def _forget_bwd(dcum_q, dcum_k, fb, b_forget, name):
    S = fb.shape[0]
    nb = S // BLK

    def body(dq_ref, dk_ref, fb_ref, b_ref, dfb_ref, db_ref, carry_ref):
        g = pl.program_id(0)
        r = lax.broadcasted_iota(jnp.int32, (BLK, BLK), 0)
        c = lax.broadcasted_iota(jnp.int32, (BLK, BLK), 1)
        tri = (c >= r).astype(F32)

        @pl.when(g == 0)
        def _():
            carry_ref[...] = jnp.zeros_like(carry_ref)

        dcum = dq_ref[...] + dk_ref[...]
        dlogf = jnp.dot(tri, dcum, preferred_element_type=F32, precision=HIGHEST) + carry_ref[0:1, :]
        carry_ref[...] = jnp.broadcast_to(dlogf[0:1, :], carry_ref.shape)
        x = fb_ref[...] + b_ref[...]
        dfb = jnp.where(c < N_FORGET, dlogf * jax.nn.sigmoid(-x), 0.0)
        dfb_ref[...] = dfb
        db = jnp.sum(dfb, axis=0, keepdims=True)

        @pl.when(g == 0)
        def _():
            db_ref[...] = db

        @pl.when(g > 0)
        def _():
            db_ref[...] += db

    rev = pl.BlockSpec((BLK, BLK), lambda g: (nb - 1 - g, 0))
    row = pl.BlockSpec((1, BLK), lambda g: (0, 0))
    return pl.pallas_call(
        body, name=name,
        out_shape=(jax.ShapeDtypeStruct((S, BLK), F32), jax.ShapeDtypeStruct((1, BLK), F32)), grid=(nb,),
        in_specs=[rev, rev, rev, row], out_specs=(rev, row),
        scratch_shapes=[pltpu.VMEM((8, BLK), F32)],
        compiler_params=_params(),
    )(dcum_q, dcum_k, fb, b_forget)


def _skew(x, sign):
    row = lax.broadcasted_iota(jnp.int32, x.shape, 0)
    for b in range(7):
        amount = (1 << b) if sign > 0 else 256 - (1 << b)
        x = jnp.where(((row >> b) & 1) == 1, pltpu.roll(x, amount, 1), x)
    return x


def _rel_bases(rel):
    far = rel[:, 256:257]
    far127 = jnp.broadcast_to(far, (rel.shape[0], 127))
    base0 = jnp.concatenate([rel[:, 128:0:-1], far, rel[:, 255:128:-1]], axis=1)
    base1 = jnp.concatenate([rel[:, 256:128:-1], far, far127], axis=1)
    base0_t = jnp.concatenate([rel[:, 128:256], far, rel[:, 1:128]], axis=1)
    base1_t = jnp.concatenate([jnp.broadcast_to(far, (rel.shape[0], 128)), far, rel[:, 129:256]], axis=1)
    return jnp.stack([base0, base1, base0_t, base1_t], axis=1)


def _rel_expand(bases, name):
    def body(b_ref, t_ref, tt_ref):
        far = jnp.broadcast_to(b_ref[1:2, 0:1], (BLK, BLK))
        for k, out_ref in ((0, t_ref), (2, tt_ref)):
            for d in (0, 1):
                x = jnp.broadcast_to(b_ref[k + d:k + d + 1, :], (BLK, 2 * BLK))
                out_ref[d] = _skew(x, 1)[:, :BLK]
            for d in (2, 3, 4):
                out_ref[d] = far

    out = jax.ShapeDtypeStruct((8, 5, BLK, BLK), F32)
    spec = pl.BlockSpec((None, 5, BLK, BLK), lambda h: (h, 0, 0, 0))
    return pl.pallas_call(
        body, name=name, out_shape=(out, out), grid=(8,),
        in_specs=[pl.BlockSpec((None, 4, 2 * BLK), lambda h: (h, 0, 0))], out_specs=(spec, spec),
        compiler_params=_params(),
    )(bases)


def _rel_reduce(dtiles_t, name):
    def body(dt_ref, o_ref):
        zeros = jnp.zeros((BLK, BLK), F32)
        sums = []
        for d in (0, 1):
            x = _skew(jnp.concatenate([dt_ref[d], zeros], axis=1), -1)
            sums.append(jnp.broadcast_to(jnp.sum(x, axis=0, keepdims=True), (8, 2 * BLK)))
        lane = lax.broadcasted_iota(jnp.int32, (8, 2 * BLK), 1)
        main = pltpu.roll(sums[0], BLK, 1) + jnp.where(lane > BLK, sums[1], 0.0)
        far = jnp.sum(jnp.where(lane < BLK, sums[1], 0.0)[0:1], axis=1, keepdims=True)
        far = far + jnp.sum(jnp.sum(dt_ref[2] + dt_ref[3] + dt_ref[4], axis=0, keepdims=True), axis=1, keepdims=True)
        o_ref[...] = jnp.concatenate([main[0:1], jnp.broadcast_to(far, (1, BLK))], axis=1)

    return pl.pallas_call(
        body, name=name, out_shape=jax.ShapeDtypeStruct((8, 1, 3 * BLK), F32), grid=(8,),
        in_specs=[pl.BlockSpec((None, 5, BLK, BLK), lambda h: (h, 0, 0, 0))],
        out_specs=pl.BlockSpec((None, 1, 3 * BLK), lambda h: (h, 0, 0)),
        compiler_params=_params(),
    )(dtiles_t)


def _final_loss(x, target, g, name):
    S, D = x.shape
    ts = _row_tile(S, 256)

    def body(x_ref, t_ref, g_ref, dx_ref, loss_ref, dg_ref):
        i = pl.program_id(0)
        xv, gv = x_ref[...], g_ref[...]
        rstd = lax.rsqrt(jnp.mean(xv * xv, axis=-1, keepdims=True) + EPS)
        xhat = xv * rstd
        err = xhat * gv - t_ref[...]
        part = 0.5 * jnp.sum(jnp.mean(err * err, axis=-1, keepdims=True), axis=0, keepdims=True)
        dy = err / D
        dg = jnp.sum(dy * xhat, axis=0, keepdims=True)
        dxhat = dy * gv
        proj = jnp.mean(dxhat * xhat, axis=-1, keepdims=True)
        dx_ref[...] = rstd * (dxhat - xhat * proj)

        @pl.when(i == 0)
        def _():
            loss_ref[...] = jnp.broadcast_to(part, loss_ref.shape)
            dg_ref[...] = dg

        @pl.when(i > 0)
        def _():
            loss_ref[...] += jnp.broadcast_to(part, loss_ref.shape)
            dg_ref[...] += dg

    tile = pl.BlockSpec((ts, D), lambda i: (i, 0))
    row = pl.BlockSpec((1, D), lambda i: (0, 0))
    return pl.pallas_call(
        body, name=name,
        out_shape=(jax.ShapeDtypeStruct((S, D), F32), jax.ShapeDtypeStruct((8, 128), F32),
                   jax.ShapeDtypeStruct((1, D), F32)),
        grid=(S // ts,), in_specs=[tile, tile, row],
        out_specs=(tile, pl.BlockSpec((8, 128), lambda i: (0, 0)), row),
        compiler_params=_params(),
    )(x, target, g)


def _ada_fwd(c_all, w_ada, name):
    L, D, E = w_ada.shape

    def body(c_ref, w_ref, o_ref):
        cv = c_ref[...]
        cond = cv * jax.nn.sigmoid(cv)
        o_ref[...] = jnp.dot(cond, w_ref[...], preferred_element_type=F32, precision=HIGHEST)

    return pl.pallas_call(
        body, name=name, out_shape=jax.ShapeDtypeStruct((L, N_DEV, E), F32), grid=(L,),
        in_specs=[pl.BlockSpec((N_DEV, D), lambda l: (0, 0)), pl.BlockSpec((None, D, E), lambda l: (l, 0, 0))],
        out_specs=pl.BlockSpec((None, N_DEV, E), lambda l: (l, 0, 0)),
        compiler_params=_params(),
    )(c_all, w_ada)


def _ada_bwd(c_all_t, dmod, name):
    D = c_all_t.shape[0]
    L, _, E = dmod.shape

    def body(c_ref, d_ref, o_ref):
        cv = c_ref[...]
        cond = cv * jax.nn.sigmoid(cv)
        acc = None
        for b in range(N_DEV):
            t = cond[:, b:b + 1] * d_ref[b:b + 1, :]
            acc = t if acc is None else acc + t
        o_ref[...] = acc

    return pl.pallas_call(
        body, name=name, out_shape=jax.ShapeDtypeStruct((L, D, E), F32), grid=(L,),
        in_specs=[pl.BlockSpec((D, N_DEV), lambda l: (0, 0)), pl.BlockSpec((None, N_DEV, E), lambda l: (l, 0, 0))],
        out_specs=pl.BlockSpec((None, D, E), lambda l: (l, 0, 0)),
        compiler_params=_params(),
    )(c_all_t, dmod)


def _adamw(w, m, v, g_parts, name, me):
    L, R, C = w.shape
    tr = _row_tile(R, max(8, (256 * 1024 // max(C, 128)) // 8 * 8))
    nr = R // tr
    c1 = 1.0 - ADAM_B1 ** ADAM_STEP
    c2 = 1.0 - ADAM_B2 ** ADAM_STEP
    direct = [isinstance(p, tuple) for p in g_parts]
    n_in = sum(2 if d else 1 for d in direct)

    def body(me_ref, w_ref, m_ref, v_ref, *rest):
        g_refs, (go_ref, d_ref, mo_ref, vo_ref) = list(rest[:n_in]), rest[n_in:]
        layer = pl.program_id(0)
        g = None
        for l in range(L):
            land_ref = g_refs.pop(0)
            own = g_refs.pop(0)[...].astype(F32) if direct[l] else None
            gl = None
            for k in range(land_ref.shape[0]):
                part = land_ref[k].astype(F32)
                if direct[l]:
                    part = jnp.where(me_ref[0] == k, own, part)
                gl = part if gl is None else gl + part
            g = gl if g is None else jnp.where(layer == l, gl, g)
        mn = ADAM_B1 * m_ref[...] + (1.0 - ADAM_B1) * g
        vn = ADAM_B2 * v_ref[...] + (1.0 - ADAM_B2) * (g * g)
        m_hat = mn / c1
        v_hat = vn / c2
        go_ref[...] = g
        d_ref[...] = -ADAM_LR * (m_hat / (jnp.sqrt(v_hat) + ADAM_EPS) + ADAM_WD * w_ref[...])
        mo_ref[...] = mn
        vo_ref[...] = vn

    def rows(l, layer, i):
        return jnp.where(layer == l, i, 0 if l > 0 else nr - 1)

    in_specs, operands = [], []
    for l, p in enumerate(g_parts):
        land, sent = p if direct[l] else (p, None)
        in_specs.append(pl.BlockSpec((land.shape[0], tr, C), lambda layer, i, me_ref, l=l: (0, rows(l, layer, i), 0)))
        operands.append(land)
        if direct[l]:
            in_specs.append(pl.BlockSpec((None, tr, C), lambda layer, i, me_ref, l=l: (me_ref[0], rows(l, layer, i), 0)))
            operands.append(sent)
    tile = pl.BlockSpec((None, tr, C), lambda layer, i, me_ref: (layer, i, 0))
    out = jax.ShapeDtypeStruct((L, R, C), F32)
    return pl.pallas_call(
        body, name=name, out_shape=(out, out, out, out),
        grid_spec=pltpu.PrefetchScalarGridSpec(
            num_scalar_prefetch=1, grid=(L, nr), in_specs=[tile, tile, tile] + in_specs,
            out_specs=(tile, tile, tile, tile)),
        compiler_params=_params(),
    )(me, w, m, v, *operands)


def _pair_add(pieces, recv, core, name):
    _, _, R, C = pieces.shape
    tr = _row_tile(R, max(8, (512 * 1024 // max(C, 128)) // 8 * 8))

    def body(core_ref, a_ref, b_ref, o_ref):
        o_ref[...] = (a_ref[...].astype(F32) + b_ref[...].astype(F32)).astype(BF16)

    return pl.pallas_call(
        body, name=name, out_shape=jax.ShapeDtypeStruct((4, R, C), BF16),
        grid_spec=pltpu.PrefetchScalarGridSpec(
            num_scalar_prefetch=1, grid=(4, R // tr),
            in_specs=[pl.BlockSpec((None, None, tr, C), lambda k, i, core_ref: (core_ref[0], k, i, 0)),
                      pl.BlockSpec((None, tr, C), lambda k, i, core_ref: (k, i, 0))],
            out_specs=pl.BlockSpec((None, tr, C), lambda k, i, core_ref: (k, i, 0))),
        compiler_params=_params(),
    )(core, pieces, recv)


MESH = pl.DeviceIdType.MESH
ANY = pl.BlockSpec(memory_space=pl.ANY)


def _position():
    return lax.axis_index("x"), lax.axis_index("y"), lax.axis_index("c")


def _small_all_gather(v, name):
    m_per, n = v.shape

    def body(x_ref, out_ref, send_sems, recv_sems, local_sem):
        x, y, c = _position()
        me, sibling = (x, y, c), (x, y, 1 - c)
        chips = [(1 - x, y), (x, 1 - y), (1 - x, 1 - y)]

        def rows(px, py, pc):
            return out_ref.at[pl.ds((4 * px + 2 * py + pc) * m_per, m_per), :]

        def copy(k, block, to, src=None):
            return pltpu.make_async_remote_copy(
                src_ref=rows(*block) if src is None else src, dst_ref=rows(*block),
                send_sem=send_sems.at[k], recv_sem=recv_sems.at[k], device_id=to, device_id_type=MESH)

        mine = pltpu.make_async_copy(x_ref, rows(*me), local_sem)
        mine.start()
        first = [copy(0, me, sibling, src=x_ref)]
        first += [copy(1 + j, me, (*chip, c), src=x_ref) for j, chip in enumerate(chips)]
        for cp in first:
            cp.start()
        passed = [copy(4 + j, (*chip, c), sibling) for j, chip in enumerate(chips)]
        for j, chip in enumerate(chips):
            copy(1 + j, (*chip, c), me).wait_recv()
            passed[j].start()
        copy(0, sibling, me).wait_recv()
        for j, chip in enumerate(chips):
            copy(4 + j, (*chip, 1 - c), me).wait_recv()
        for cp in first + passed:
            cp.wait_send()
        mine.wait()

    return pl.pallas_call(
        body, name=name, out_shape=jax.ShapeDtypeStruct((N_DEV * m_per, n), v.dtype),
        in_specs=[pl.BlockSpec(memory_space=pltpu.VMEM)], out_specs=pl.BlockSpec(memory_space=pltpu.VMEM),
        scratch_shapes=[pltpu.SemaphoreType.DMA((7,)), pltpu.SemaphoreType.DMA((7,)), pltpu.SemaphoreType.DMA],
    )(v)


def _big_all_gather(shards, name):
    n_arr = len(shards)

    def body(*refs):
        x_refs, out_refs = refs[:n_arr], refs[n_arr:2 * n_arr]
        send_sems, recv_sems, local_sems = refs[2 * n_arr:]
        x, y, c = _position()
        me, sibling = (x, y, c), (x, y, 1 - c)
        chips = [(1 - x, y), (x, 1 - y), (1 - x, 1 - y)]

        def slot(a, px, py, pc):
            return out_refs[a].at[4 * px + 2 * py + pc]

        def copy(a, k, block, to, src=None):
            return pltpu.make_async_remote_copy(
                src_ref=slot(a, *block) if src is None else src, dst_ref=slot(a, *block),
                send_sem=send_sems.at[a, k], recv_sem=recv_sems.at[a, k], device_id=to, device_id_type=MESH)

        mine = [pltpu.make_async_copy(x_refs[a], slot(a, *me), local_sems.at[a]) for a in range(n_arr)]
        for cp in mine:
            cp.start()
        first = []
        for j, chip in enumerate(chips):
            first += [copy(a, 1 + j, me, (*chip, c), src=x_refs[a]) for a in range(n_arr)]
        first += [copy(a, 0, me, sibling, src=x_refs[a]) for a in range(n_arr)]
        for cp in first:
            cp.start()
        passed = []
        for j, chip in enumerate(chips):
            for a in range(n_arr):
                copy(a, 1 + j, (*chip, c), me).wait_recv()
                fwd = copy(a, 4 + j, (*chip, c), sibling)
                fwd.start()
                passed.append(fwd)
        for a in range(n_arr):
            copy(a, 0, sibling, me).wait_recv()
        for j, chip in enumerate(chips):
            for a in range(n_arr):
                copy(a, 4 + j, (*chip, 1 - c), me).wait_recv()
        for cp in first + passed:
            cp.wait_send()
        for cp in mine:
            cp.wait()

    return pl.pallas_call(
        body, name=name,
        out_shape=tuple(jax.ShapeDtypeStruct((N_DEV,) + s.shape, s.dtype) for s in shards),
        in_specs=[ANY] * n_arr, out_specs=tuple([ANY] * n_arr),
        scratch_shapes=[pltpu.SemaphoreType.DMA((n_arr, 7)), pltpu.SemaphoreType.DMA((n_arr, 7)),
                        pltpu.SemaphoreType.DMA((n_arr,))],
    )(*shards)


def _sibling_exchange(pieces, name):
    n_arr = len(pieces)

    def body(*refs):
        p_refs, out_refs = refs[:n_arr], refs[n_arr:2 * n_arr]
        send_sems, recv_sems = refs[2 * n_arr:]
        x, y, c = _position()
        copies = [pltpu.make_async_remote_copy(
            src_ref=p_refs[a].at[1 - c], dst_ref=out_refs[a], send_sem=send_sems.at[a], recv_sem=recv_sems.at[a],
            device_id=(x, y, 1 - c), device_id_type=MESH) for a in range(n_arr)]
        for cp in copies:
            cp.start()
        for cp in copies:
            cp.wait()

    return pl.pallas_call(
        body, name=name,
        out_shape=tuple(jax.ShapeDtypeStruct(p.shape[1:], p.dtype) for p in pieces),
        in_specs=[ANY] * n_arr, out_specs=tuple([ANY] * n_arr),
        scratch_shapes=[pltpu.SemaphoreType.DMA((n_arr,)), pltpu.SemaphoreType.DMA((n_arr,))],
    )(*pieces)


def _chip_exchange(sums, name):
    n_arr = len(sums)

    def body(*refs):
        s_refs, out_refs = refs[:n_arr], refs[n_arr:2 * n_arr]
        send_sems, recv_sems, local_sems = refs[2 * n_arr:]
        x, y, c = _position()
        my_chip = 2 * x + y
        chips = [(1 - x, y), (x, 1 - y), (1 - x, 1 - y)]
        mine = [pltpu.make_async_copy(s_refs[a].at[my_chip], out_refs[a].at[my_chip], local_sems.at[a])
                for a in range(n_arr)]
        for cp in mine:
            cp.start()
        copies = []
        for j, (px, py) in enumerate(chips):
            copies += [pltpu.make_async_remote_copy(
                src_ref=s_refs[a].at[2 * px + py], dst_ref=out_refs[a].at[my_chip],
                send_sem=send_sems.at[a, j], recv_sem=recv_sems.at[a, j],
                device_id=(px, py, c), device_id_type=MESH) for a in range(n_arr)]
        for cp in copies:
            cp.start()
        for j, (px, py) in enumerate(chips):
            for a in range(n_arr):
                pltpu.make_async_remote_copy(
                    src_ref=s_refs[a].at[my_chip], dst_ref=out_refs[a].at[2 * px + py],
                    send_sem=send_sems.at[a, j], recv_sem=recv_sems.at[a, j],
                    device_id=(px, py, c), device_id_type=MESH).wait_recv()
        for cp in copies:
            cp.wait_send()
        for cp in mine:
            cp.wait()

    return pl.pallas_call(
        body, name=name,
        out_shape=tuple(jax.ShapeDtypeStruct(s.shape, s.dtype) for s in sums),
        in_specs=[ANY] * n_arr, out_specs=tuple([ANY] * n_arr),
        scratch_shapes=[pltpu.SemaphoreType.DMA((n_arr, 3)), pltpu.SemaphoreType.DMA((n_arr, 3)),
                        pltpu.SemaphoreType.DMA((n_arr,))],
    )(*sums)


HBM = pl.BlockSpec(memory_space=pltpu.HBM)
SEM = pl.BlockSpec(memory_space=pltpu.SEMAPHORE)
EFFECT = pltpu.SideEffectType.DATAFLOW_SIDE_EFFECTING
RELATIONS = [(rx, ry, rc) for rx in (0, 1) for ry in (0, 1) for rc in (0, 1)][1:]


SAME_CORE = [r for r in RELATIONS if r == (0, 0, 1) or r[2] == 0]


def _exchange_copies(src_refs, land_refs, send_sems, recv_sems, scatter, receive_side, relations):
    x, y, c = _position()
    me = 4 * x + 2 * y + c
    copies = []
    for k, (rx, ry, rc) in enumerate(relations):
        peer = ((1 - x) if rx else x, (1 - y) if ry else y, (1 - c) if rc else c)
        peer_index = 4 * peer[0] + 2 * peer[1] + peer[2]
        for a, (src, land) in enumerate(zip(src_refs, land_refs)):
            copies.append(pltpu.make_async_remote_copy(
                src_ref=src.at[peer_index] if scatter else src,
                dst_ref=land.at[peer_index if receive_side else me],
                send_sem=send_sems.at[a * len(relations) + k], recv_sem=recv_sems.at[a * len(relations) + k],
                device_id=peer, device_id_type=MESH))
    return copies


def _exchange_start(srcs, scatter, after, name, relations=RELATIONS):
    n = len(srcs)
    land_shapes = [(s.shape if scatter else (N_DEV,) + s.shape) for s in srcs]

    def body(*refs):
        src_refs, land_refs = refs[:n], refs[n:2 * n]
        send_sems, recv_sems = refs[2 * n + 1], refs[2 * n + 2]
        token = refs[-1]
        for cp in _exchange_copies(src_refs, land_refs, send_sems, recv_sems, scatter, False, relations):
            cp.start()
        token[...] = jnp.zeros_like(token)

    sems = pltpu.SemaphoreType.DMA((n * len(relations),))
    outs = pl.pallas_call(
        body, name=name,
        out_shape=(sems, sems, *[pltpu.HBM(s.shape, s.dtype) for s in srcs],
                   *[pltpu.HBM(shape, s.dtype) for shape, s in zip(land_shapes, srcs)],
                   jax.ShapeDtypeStruct((8, 128), F32)),
        in_specs=[HBM] * (2 * n) + [ANY],
        out_specs=(SEM, SEM, *[HBM] * (2 * n), pl.BlockSpec(memory_space=pltpu.VMEM)),
        input_output_aliases={a: 2 + a for a in range(2 * n)},
        compiler_params=pltpu.CompilerParams(has_side_effects=EFFECT),
    )(*[pltpu.with_memory_space_constraint(s, pltpu.HBM) for s in srcs],
      *[pltpu.with_memory_space_constraint(lax.empty(shape, s.dtype), pltpu.HBM)
        for shape, s in zip(land_shapes, srcs)], after)
    return outs[0], outs[1], outs[2:2 + n], outs[2 + n:2 + 2 * n], outs[-1]


def _exchange_wait(started, scatter, after, name, relations=RELATIONS):
    send_sems, recv_sems, srcs, lands, _ = started
    n = len(srcs)

    def body(*refs):
        src_refs, land_refs = refs[:n], refs[n:2 * n]
        send_sems, recv_sems = refs[2 * n], refs[2 * n + 1]
        copies = _exchange_copies(src_refs, land_refs, send_sems, recv_sems, scatter, True, relations)
        for cp in copies:
            cp.wait_send()
        for cp in copies:
            cp.wait_recv()

    outs = pl.pallas_call(
        body, name=name,
        out_shape=(*[pltpu.HBM(s.shape, s.dtype) for s in srcs], *[pltpu.HBM(t.shape, t.dtype) for t in lands]),
        in_specs=[HBM] * (2 * n) + [SEM, SEM, ANY], out_specs=tuple([HBM] * (2 * n)),
        input_output_aliases={a: a for a in range(2 * n)},
        compiler_params=pltpu.CompilerParams(has_side_effects=EFFECT),
    )(*srcs, *lands, send_sems, recv_sems, after)
    return outs[:n], outs[n:]


def _sibling_forward(lands, name):
    n = len(lands)

    def body(*refs):
        in_refs, out_refs = refs[:n], refs[n:2 * n]
        send_sems, recv_sems = refs[2 * n:]
        x, y, c = _position()
        copies, arrivals = [], []
        for j, (px, py) in enumerate([(1 - x, y), (x, 1 - y), (1 - x, 1 - y)]):
            held, coming = 4 * px + 2 * py + c, 4 * px + 2 * py + (1 - c)
            for a in range(n):
                sems = dict(send_sem=send_sems.at[a, j], recv_sem=recv_sems.at[a, j], device_id=(x, y, 1 - c),
                            device_id_type=MESH)
                copies.append(pltpu.make_async_remote_copy(
                    src_ref=in_refs[a].at[held], dst_ref=out_refs[a].at[held], **sems))
                arrivals.append(pltpu.make_async_remote_copy(
                    src_ref=in_refs[a].at[held], dst_ref=out_refs[a].at[coming], **sems))
        for cp in copies:
            cp.start()
        for cp in copies:
            cp.wait_send()
        for cp in arrivals:
            cp.wait_recv()

    return pl.pallas_call(
        body, name=name, out_shape=tuple(jax.ShapeDtypeStruct(t.shape, t.dtype) for t in lands),
        in_specs=[ANY] * n, out_specs=tuple([ANY] * n), input_output_aliases={a: a for a in range(n)},
        scratch_shapes=[pltpu.SemaphoreType.DMA((n, 3)), pltpu.SemaphoreType.DMA((n, 3))],
    )(*lands)


W_IN_SHARD = N_IN // N_DEV
F_SHARD = F_COL // W_IN_SHARD
F_LO = F_COL - F_SHARD * W_IN_SHARD


def _cols_from_shards(g):
    return jnp.concatenate([g[d] for d in range(N_DEV)], axis=1)


def _w_in_rearranged(g):
    parts = [g[d] for d in range(N_DEV)]
    with_f = parts[F_SHARD]
    parts[F_SHARD:F_SHARD + 1] = [with_f[:, :F_LO], with_f[:, F_LO + N_FORGET:]]
    parts += [with_f[:, F_LO:F_LO + N_FORGET], jnp.zeros((with_f.shape[0], BLK - N_FORGET), with_f.dtype)]
    return jnp.concatenate(parts, axis=1)


def _w_in_pieces(dw_r):
    def original(lo, hi):
        shift = 0 if hi <= F_COL else N_FORGET
        return dw_r[:, lo - shift:hi - shift]

    pieces = []
    for d in range(N_DEV):
        lo, hi = d * W_IN_SHARD, (d + 1) * W_IN_SHARD
        if d == F_SHARD:
            pieces.append(jnp.concatenate([original(lo, F_COL), dw_r[:, N_MAIN:N_MAIN + N_FORGET],
                                           original(F_COL + N_FORGET, hi)], axis=1))
        else:
            pieces.append(original(lo, hi))
    return jnp.stack(pieces)


def _col_pieces(dw):
    width = dw.shape[1] // N_DEV
    return jnp.stack([dw[:, d * width:(d + 1) * width] for d in range(N_DEV)])


def _row_pieces(dw):
    return dw.reshape(N_DEV, dw.shape[0] // N_DEV, dw.shape[1])


def _branch_pieces(dw):
    k, w, d = dw.shape
    return jnp.transpose(dw.reshape(k, w, N_DEV, d // N_DEV), (2, 0, 1, 3)).reshape(N_DEV, k * w, d // N_DEV)


def _pair_major(p8):
    return jnp.stack([p8[0::2], p8[1::2]])


def _pairs_col(a):
    return jnp.transpose(a.reshape(a.shape[0], 4, 2), (1, 0, 2))


def _pairs_row(a):
    return jnp.transpose(a.reshape(a.shape[0], 4, 2), (1, 2, 0))


def _heads_from_col(a):
    return jnp.transpose(a, (1, 0, 2)).reshape(a.shape[1], 8)


def _heads_from_row(a):
    return jnp.transpose(a, (2, 0, 1)).reshape(a.shape[2], 8)


def _pad_lanes(a, n):
    return jnp.pad(a, [(0, 0)] * (a.ndim - 1) + [(0, n - a.shape[-1])])


SMALL_SEGMENTS = (("dmod", 2 * 6 * D_MODEL), ("norm_mix_g", 2 * D_MODEL), ("norm_ffn_g", 2 * D_MODEL),
                  ("final_norm_g", D_MODEL), ("b_forget", 128), ("sinks", 128), ("rel_bias", 4224))
SMALL_ROWS = 176


def _pack_small(parts):
    flat = [_pad_lanes(parts[name].reshape(1, -1), size) for name, size in SMALL_SEGMENTS]
    total = sum(size for _, size in SMALL_SEGMENTS)
    flat.append(jnp.zeros((1, SMALL_ROWS * 128 - total), F32))
    return jnp.concatenate(flat, axis=1).reshape(SMALL_ROWS, 128)


def _unpack_small(packed, shapes):
    flat = packed.reshape(-1)
    out, pos = {}, 0
    for name, size in SMALL_SEGMENTS:
        shape = shapes[name]
        count = 1
        for d in shape:
            count *= d
        out[name] = flat[pos:pos + count].reshape(shape)
        pos += size
    return out


def kernel(x, c, norm_mix_g, norm_ffn_g, w_ada, b_ada, w_in, b_forget, sinks, rel_bias, w_branch, w_out, w_ffn_in, w_ffn_out, final_norm_g, loss_target, m_norm_mix_g, m_norm_ffn_g, m_w_ada, m_b_ada, m_w_in, m_b_forget, m_sinks, m_rel_bias, m_w_branch, m_w_out, m_w_ffn_in, m_w_ffn_out, m_final_norm_g, v_norm_mix_g, v_norm_ffn_g, v_w_ada, v_b_ada, v_w_in, v_b_forget, v_sinks, v_rel_bias, v_w_branch, v_w_out, v_w_ffn_in, v_w_ffn_out, v_final_norm_g):
    depth = w_in.shape[0]
    S, D = x.shape[1], x.shape[2]
    assert S % GROUP == 0 and S >= ATTN_WINDOW["c"] * BLK
    px, py, pc = _position()
    me = 4 * px + 2 * py + pc
    x0 = x[0]

    assert depth == 2
    big_weights = (w_in, w_branch, w_out, w_ffn_in, w_ffn_out)
    me_arr = me.astype(jnp.int32).reshape(1)

    def slabs(landed, mine):
        return [jnp.where(me == d, mine, landed[d]) for d in range(N_DEV)]

    def rest_matrices(g_branch, g_out, g_fin, g_fout):
        return (jnp.transpose(jnp.stack(g_branch), (1, 2, 0, 3)).reshape(3, 512, D),
                jnp.concatenate(g_out, axis=0), _cols_from_shards(g_fin), jnp.concatenate(g_fout, axis=0))

    def finish_gather(started, after, name):
        mine, landed = _exchange_wait(started, False, after, f"{name}_wait", SAME_CORE)
        landed = _sibling_forward(landed, f"{name}_forward")
        return [slabs(t, s) for t, s in zip(landed, mine)]

    shards = [[w[l].astype(BF16) for w in big_weights] for l in range(depth)]
    gathered_in0 = _big_all_gather(shards[0][:1], "comm_gather_w_in0")[0]
    gather_rest0 = _exchange_start(shards[0][1:], False, gathered_in0, "comm_gather_rest0_start", SAME_CORE)
    gather1 = _exchange_start(shards[1], False, gather_rest0[4], "comm_gather_weights1_start", SAME_CORE)
    W_in, W_branch, W_out, W_fin, W_fout = ([None, None] for _ in range(5))
    W_in[0] = _w_in_rearranged(gathered_in0)

    c_all = _small_all_gather(c.reshape(8, 128), "comm_gather_c").reshape(N_DEV, D)
    mod_cols = _ada_fwd(c_all, w_ada, "ada_fwd")
    mod_all = _small_all_gather(mod_cols.reshape(-1, 128), "comm_gather_mod")
    mod_all = mod_all.reshape(N_DEV, depth, N_DEV, w_ada.shape[2])
    mod_mine = lax.dynamic_index_in_dim(mod_all, me, axis=2, keepdims=False)
    mod = jnp.transpose(mod_mine, (1, 0, 2)).reshape(depth, 6 * D) + b_ada + gather1[4][0:1, 0:1]
    mods = [[mod[l:l + 1, k * D:(k + 1) * D] for k in range(6)] for l in range(depth)]

    slopes = jnp.exp2(-jnp.arange(1, 9, dtype=F32))
    saved = []
    xs = x0
    for l in range(depth):
        if l == 1:
            g_in1, *g_rest1 = finish_gather(gather1, xs, "comm_gather_weights1")
            W_in[1] = _w_in_rearranged(g_in1)
            W_branch[1], W_out[1], W_fin[1], W_fout[1] = rest_matrices(*g_rest1)
        sh_m, sc_m, g_m, sh_f, sc_f, g_f = mods[l]
        gm, gf = norm_mix_g[l:l + 1], norm_ffn_g[l:l + 1]
        bfor = _pad_lanes(b_forget[l:l + 1], BLK)
        h = _norm_mod_fwd(xs, gm, sh_m, sc_m, f"norm_mix_fwd{l}")
        qkv = _matmul(h, W_in[l], "nn", BF16, f"proj_qkv{l}", TILES["proj_qkv"], n=N_QKV)
        gates = _matmul(h, W_in[l], "nn", F32, f"proj_gates{l}", TILES["proj_gates"], n=N_GATES,
                        b_off=N_QKV // TILES["proj_gates"][1])
        fb = _matmul(h, W_in[l], "nn", F32, f"proj_forget{l}", TILES["proj_forget"], n=BLK, b_off=N_MAIN // BLK)
        cum = _forget_fwd(fb, bfor, f"forget_fwd{l}")[:, :N_FORGET]
        cum_col, cum_row = _pairs_col(cum), _pairs_row(cum)
        tiles, tiles_t = _rel_expand(_rel_bases(rel_bias[l]), f"rel_expand{l}")
        o_a, lse_a = _attn_fwd("a", qkv, f"attn_a_fwd{l}", sinks=sinks[l], slopes=slopes)
        o_b, lse_b = _attn_fwd("b", qkv, f"attn_b_fwd{l}", cq_col=cum_col, ck_row=cum_row)
        o_c, lse_c = _attn_fwd("c", qkv, f"attn_c_fwd{l}", bias=tiles)
        if l == 0:
            W_branch[0], W_out[0], W_fin[0], W_fout[0] = rest_matrices(*finish_gather(gather_rest0, o_c, "comm_gather_rest0"))
        merged = _merge_fwd(o_a, o_b, o_c, gates, W_branch[l], f"merge_fwd{l}")
        x1, mix = _matmul_resid(merged, W_out[l], xs, g_m, f"out_proj{l}", TILES["out_proj"])
        h2 = _norm_mod_fwd(x1, gf, sh_f, sc_f, f"norm_ffn_fwd{l}")
        act = _ffn_in_fwd(h2, W_fin[l], f"ffn_in_fwd{l}")
        x2, ffn = _matmul_resid(act, W_fout[l], x1, g_f, f"ffn_out{l}", TILES["ffn_out"])
        saved.append(dict(x=xs, h=h, qkv=qkv, gates=gates, fb=fb, bfor=bfor, cum_col=cum_col, cum_row=cum_row,
                          tiles_t=tiles_t, o=(o_a, o_b, o_c), lse=(lse_a, lse_b, lse_c), merged=merged, mix=mix,
                          x1=x1, h2=h2, act=act, ffn=ffn))
        xs = x2

    dx, loss_tile, d_final_g = _final_loss(xs, loss_target[0], final_norm_g.reshape(1, D), "final_loss")
    loss = lax.psum(loss_tile[0, 0], ("x", "y", "c"))

    grads = {k: [None] * depth for k in ("w_in", "w_branch", "w_out", "w_ffn_in", "w_ffn_out", "norm_mix_g",
                                          "norm_ffn_g", "b_forget", "sinks", "rel_bias", "dmod")}
    def rest_pieces(l):
        return [_branch_pieces(grads["w_branch"][l]), _row_pieces(grads["w_out"][l]),
                _col_pieces(grads["w_ffn_in"][l]), _row_pieces(grads["w_ffn_out"][l])]

    reduce1 = reduce_rest0 = reduce_in0 = None
    for l in reversed(range(depth)):
        sv = saved[l]
        sh_m, sc_m, g_m, sh_f, sc_f, g_f = mods[l]
        if l == 0:
            g_f = g_f + reduce1[4][0:1, 0:1]
        gm, gf = norm_mix_g[l:l + 1], norm_ffn_g[l:l + 1]
        df, d_g_f = _gate_bwd(dx, sv["ffn"], g_f, f"ffn_gate_bwd{l}")
        du_g, du_u = _ffn_mid_bwd(sv["h2"], df, W_fin[l], W_fout[l], f"ffn_mid_bwd{l}")
        du = jnp.concatenate([du_g, du_u], axis=1)
        grads["w_ffn_out"][l] = _matmul(sv["act"], df, "tn", BF16, f"wgrad_ffn_out{l}", TILES["wgrad_ffn_out"])
        grads["w_ffn_in"][l] = _matmul(sv["h2"], du, "tn", BF16, f"wgrad_ffn_in{l}", TILES["wgrad_ffn_in"])
        dh2 = _matmul(du, W_fin[l], "nt", F32, f"dgrad_ffn_in{l}", TILES["dgrad_ffn_in"])
        dx1, d_sh_f, d_sc_f, d_gf = _norm_mod_bwd(sv["x1"], dh2, dx, gf, sc_f, f"norm_ffn_bwd{l}")
        dmix, d_g_m = _gate_bwd(dx1, sv["mix"], g_m, f"mix_gate_bwd{l}")
        grads["w_out"][l] = _matmul(sv["merged"], dmix, "tn", BF16, f"wgrad_out{l}", TILES["wgrad_out"])
        dmerged = _matmul(dmix, W_out[l], "nt", F32, f"dgrad_out{l}", TILES["dgrad_out"])
        o_a, o_b, o_c = sv["o"]
        dgates, dy, do_a, do_b, do_c, dl_a, dl_b, dl_c = _merge_bwd(
            dmerged, o_a, o_b, o_c, sv["gates"], W_branch[l], f"merge_bwd{l}")
        dwb = [_matmul(o_k, dy, "tn", BF16, f"wgrad_branch{l}_{k}", TILES["wgrad_branch"], n=D,
                       b_off=k * (D // TILES["wgrad_branch"][1])) for k, o_k in enumerate((o_a, o_b, o_c))]
        grads["w_branch"][l] = jnp.stack(dwb)
        lse_rows = [_pairs_row(_heads_from_col(t)) for t in sv["lse"]]
        if l == 0:
            reduce_rest0 = _exchange_start(rest_pieces(0), True, dy, "comm_reduce_rest0_start")
            lse_rows = [t + reduce_rest0[4][0:1, 0:1] for t in lse_rows]
        dqt_a, dk_a, dv_a, dsink = _attn_bwd("a", sv["qkv"], do_a, lse_rows[0], _pairs_row(dl_a), f"attn_a_bwd{l}",
                                             sinks=sinks[l], slopes=slopes)
        dqt_b, dk_b, dv_b, dck, dcq = _attn_bwd("b", sv["qkv"], do_b, lse_rows[1], _pairs_row(dl_b),
                                                f"attn_b_bwd{l}", cq_row=sv["cum_row"], ck_col=sv["cum_col"])
        dqt_c, dk_c, dv_c, dtiles_t = _attn_bwd("c", sv["qkv"], do_c, lse_rows[2], _pairs_row(dl_c),
                                                f"attn_c_bwd{l}", bias_t=sv["tiles_t"])
        grads["sinks"][l] = dsink[:, :2, 0].reshape(8)
        grads["rel_bias"][l] = _rel_reduce(dtiles_t, f"rel_reduce{l}")[:, 0, :N_REL]
        dcum_k = _pad_lanes(_heads_from_col(dck), BLK)
        dcum_q = _pad_lanes(_heads_from_row(dcq), BLK)
        dfb, d_bfor = _forget_bwd(dcum_q, dcum_k, sv["fb"], sv["bfor"], f"forget_bwd{l}")
        grads["b_forget"][l] = d_bfor[0, :N_FORGET]
        dproj = jnp.concatenate(
            [t.astype(BF16) for t in (dqt_a.T, dk_a, dv_a, dqt_b.T, dk_b, dv_b, dqt_c.T, dk_c, dv_c)]
            + [dgates, dfb.astype(BF16)], axis=1)
        grads["w_in"][l] = _matmul(sv["h"], dproj, "tn", BF16, f"wgrad_in{l}", TILES["wgrad_in"])
        if l == 0:
            reduce_in0 = _exchange_start([_w_in_pieces(grads["w_in"][0])], True, dproj, "comm_reduce_in0_start")
        dh = _matmul(dproj, W_in[l], "nt", F32, f"dgrad_in{l}", TILES["dgrad_in"],
                     after=reduce_in0[4] if l == 0 else None)
        dx, d_sh_m, d_sc_m, d_gm = _norm_mod_bwd(sv["x"], dh, dx1, gm, sc_m, f"norm_mix_bwd{l}")
        grads["norm_mix_g"][l] = d_gm[0]
        grads["norm_ffn_g"][l] = d_gf[0]
        grads["dmod"][l] = jnp.concatenate([d_sh_m, d_sc_m, d_g_m, d_sh_f, d_sc_f, d_g_f], axis=1)[0]
        if l == 1:
            reduce1 = _exchange_start([_w_in_pieces(grads["w_in"][1])] + rest_pieces(1), True, dx, "comm_reduce1_start")

    grad_x = dx.reshape(x.shape)

    small_shapes = dict(dmod=b_ada.shape, norm_mix_g=norm_mix_g.shape, norm_ffn_g=norm_ffn_g.shape,
                        final_norm_g=final_norm_g.shape, b_forget=b_forget.shape, sinks=sinks.shape,
                        rel_bias=rel_bias.shape)
    mine_small = _pack_small(dict(
        dmod=jnp.stack(grads["dmod"]), norm_mix_g=jnp.stack(grads["norm_mix_g"]),
        norm_ffn_g=jnp.stack(grads["norm_ffn_g"]), final_norm_g=d_final_g[0],
        b_forget=_pad_lanes(jnp.stack(grads["b_forget"]).reshape(1, -1), 128),
        sinks=_pad_lanes(jnp.stack(grads["sinks"]).reshape(1, -1), 128),
        rel_bias=_pad_lanes(jnp.stack(grads["rel_bias"]).reshape(1, -1), 4224)))
    all_small = _small_all_gather(mine_small, "comm_gather_small").reshape(N_DEV, SMALL_ROWS, 128)

    def pack_params(b_ada_, nm, nf, fn, bf, sk, rb):
        return _pack_small(dict(dmod=b_ada_, norm_mix_g=nm, norm_ffn_g=nf, final_norm_g=fn,
                                b_forget=_pad_lanes(bf.reshape(1, -1), 128), sinks=_pad_lanes(sk.reshape(1, -1), 128),
                                rel_bias=_pad_lanes(rb.reshape(1, -1), 4224)))

    small_out = _adamw(
        pack_params(b_ada, norm_mix_g, norm_ffn_g, final_norm_g, b_forget, sinks, rel_bias)[None],
        pack_params(m_b_ada, m_norm_mix_g, m_norm_ffn_g, m_final_norm_g, m_b_forget, m_sinks, m_rel_bias)[None],
        pack_params(v_b_ada, v_norm_mix_g, v_norm_ffn_g, v_final_norm_g, v_b_forget, v_sinks, v_rel_bias)[None],
        [all_small], "adamw_small", me_arr)
    small_out = [_unpack_small(t[0], small_shapes) for t in small_out]

    dmod_all = all_small[:, :96].reshape(N_DEV, depth, 6 * D)
    dmod_cols = lax.dynamic_slice_in_dim(dmod_all, me * w_ada.shape[2], w_ada.shape[2], axis=2)
    d_w_ada = _ada_bwd(jnp.transpose(c_all), jnp.transpose(dmod_cols, (1, 0, 2)), "ada_bwd")

    big = {"w_ada": _adamw(w_ada, m_w_ada, v_w_ada, [d_w_ada[l:l + 1] for l in range(depth)], "adamw_w_ada", me_arr)}
    sent1, landed1 = _exchange_wait(reduce1, True, big["w_ada"][0], "comm_reduce1_wait")
    sent_rest0, landed_rest0 = _exchange_wait(reduce_rest0, True, landed1[0], "comm_reduce_rest0_wait")
    parts = {"w_in": [None, (landed1[0], sent1[0])]}
    for a, name in enumerate(("w_branch", "w_out", "w_ffn_in", "w_ffn_out")):
        parts[name] = [(landed_rest0[a], sent_rest0[a]), (landed1[1 + a], sent1[1 + a])]

    def update(name, w, m, v):
        per_layer = lambda t: t.reshape(depth, -1, t.shape[-1])
        outs = _adamw(per_layer(w), per_layer(m), per_layer(v), parts[name], f"adamw_{name}", me_arr)
        big[name] = [t.reshape(w.shape) for t in outs]

    update("w_ffn_in", w_ffn_in, m_w_ffn_in, v_w_ffn_in)
    update("w_ffn_out", w_ffn_out, m_w_ffn_out, v_w_ffn_out)
    update("w_branch", w_branch, m_w_branch, v_w_branch)
    update("w_out", w_out, m_w_out, v_w_out)
    sent_in0, landed_in0 = _exchange_wait(reduce_in0, True, big["w_out"][0], "comm_reduce_in0_wait")
    parts["w_in"][0] = (landed_in0[0], sent_in0[0])
    update("w_in", w_in, m_w_in, v_w_in)

    def leaf(kind, name):
        if name in big:
            return big[name][kind]
        return small_out[kind]["dmod" if name == "b_ada" else name]

    order = ["norm_mix_g", "norm_ffn_g", "w_ada", "b_ada", "w_in", "b_forget", "sinks", "rel_bias", "w_branch",
             "w_out", "w_ffn_in", "w_ffn_out", "final_norm_g"]
    return (loss, grad_x, *[leaf(0, n) for n in order], *[leaf(1, n) for n in order],
            *[leaf(2, n) for n in order], *[leaf(3, n) for n in order])
```

```python
import functools

import jax
import jax.numpy as jnp
from jax import lax
from jax.experimental import pallas as pl
from jax.experimental.pallas import tpu as pltpu

F32 = jnp.float32
BF16 = jnp.bfloat16
NEG_INF = -1e30
EPS = 1e-6
N_DEV = 8
BLK = 128
GROUP = 4 * BLK
VMEM_LIMIT_BYTES = 56 * 1024 * 1024

D_MODEL = 1024
N_QKV = 3840
N_GATES = 3072
N_MAIN = N_QKV + N_GATES
N_FORGET = 8
N_IN = N_MAIN + N_FORGET
N_INR = N_MAIN + BLK
F_COL = 2304
FFN_HIDDEN = 2816
N_REL = 257

ADAM_LR, ADAM_B1, ADAM_B2, ADAM_EPS, ADAM_WD, ADAM_STEP = 0.001, 0.9, 0.999, 1e-08, 0.01, 10

NN = (((1,), (0,)), ((), ()))
NT = (((1,), (1,)), ((), ()))
TN = (((0,), (0,)), ((), ()))
HIGHEST = lax.Precision.HIGHEST

ATTN_COLS = {"a": (0, 4, 5), "b": (6, 10, 14), "c": (18, 22, 26)}
ATTN_WINDOW = {"a": 2, "c": 5}


def _params():
    return pltpu.CompilerParams(vmem_limit_bytes=VMEM_LIMIT_BYTES)


def _tile(n, target):
    best = None
    t = 128
    while t <= min(n, target):
        if n % t == 0:
            best = t
        t += 128
    return best if best is not None else n


def _row_tile(n, target):
    t = min(n, target)
    while n % t:
        t -= 8
    return t


TILES = {
    "proj_qkv": (1024, 1280, 1024), "proj_gates": (1024, 768, 1024), "proj_forget": (1024, 128, 1024),
    "out_proj": (1024, 512, 1024), "ffn_out": (1024, 512, 1408), "ffn_fused": (512, 1408),
    "wgrad_ffn_out": (1408, 1024, 1024), "wgrad_ffn_in": (1408, 1024, 1024), "dgrad_ffn_in": (1024, 1024, 1408),
    "wgrad_out": (1024, 1024, 1024), "dgrad_out": (1024, 1024, 1024), "wgrad_branch": (512, 1024, 1024),
    "wgrad_in": (1024, 1408, 1024), "dgrad_in": (1024, 1024, 1408),
}


def _matmul(a, b, mode, out_dtype, name, tiles, *, n=None, a_off=0, b_off=0, m=None, after=None):
    tm, tn, tk = tiles
    if mode == "nn":
        M, K = a.shape if m is None else (m, a.shape[1])
        N = b.shape[1] if n is None else n
    elif mode == "nt":
        M, K = a.shape
        N = b.shape[0] if n is None else n
    else:
        K = a.shape[0]
        M = a.shape[1] if m is None else m
        N = b.shape[1] if n is None else n
    tm = _tile(M, tm) if M % 128 == 0 else M
    tn = _tile(N, tn)
    tk = _tile(K, tk)
    nk = K // tk
    dims = {"nn": NN, "nt": NT, "tn": TN}[mode]
    if mode == "nn":
        a_spec = pl.BlockSpec((tm, tk), lambda i, j, k: (i + a_off, k))
        b_spec = pl.BlockSpec((tk, tn), lambda i, j, k: (k, j + b_off))
    elif mode == "nt":
        a_spec = pl.BlockSpec((tm, tk), lambda i, j, k: (i + a_off, k))
        b_spec = pl.BlockSpec((tn, tk), lambda i, j, k: (j + b_off, k))
    else:
        a_spec = pl.BlockSpec((tk, tm), lambda i, j, k: (k, i + a_off))
        b_spec = pl.BlockSpec((tk, tn), lambda i, j, k: (k, j + b_off))

    def body(a_ref, b_ref, *rest):
        o_ref, acc_ref = rest[-2:]
        k = pl.program_id(2)
        part = lax.dot_general(a_ref[...], b_ref[...], dims, preferred_element_type=F32)
        if nk == 1:
            o_ref[...] = part.astype(o_ref.dtype)
        else:
            @pl.when(k == 0)
            def _():
                acc_ref[...] = part

            @pl.when(k > 0)
            def _():
                acc_ref[...] += part

            @pl.when(k == nk - 1)
            def _():
                o_ref[...] = acc_ref[...].astype(o_ref.dtype)

    return pl.pallas_call(
        body, name=name,
        out_shape=jax.ShapeDtypeStruct((M, N), out_dtype),
        grid=(M // tm, N // tn, nk),
        in_specs=[a_spec, b_spec] + ([ANY] if after is not None else []),
        out_specs=pl.BlockSpec((tm, tn), lambda i, j, k: (i, j)),
        scratch_shapes=[pltpu.VMEM((tm, tn) if nk > 1 else (8, 128), F32)],
        compiler_params=_params(),
    )(a, b, *([after] if after is not None else []))


def _matmul_resid(a, b, resid, gate, name, tiles):
    M, K = a.shape
    N = b.shape[1]
    tm, tn, tk = (_tile(d, t) for d, t in zip((M, N, K), tiles))
    nk = K // tk

    def body(a_ref, b_ref, r_ref, g_ref, o_ref, s_ref, acc_ref):
        k = pl.program_id(2)
        part = jnp.dot(a_ref[...], b_ref[...], preferred_element_type=F32)

        def finish(acc):
            o_ref[...] = r_ref[...] + g_ref[...] * acc
            s_ref[...] = acc.astype(BF16)

        if nk == 1:
            finish(part)
        else:
            @pl.when(k == 0)
            def _():
                acc_ref[...] = part

            @pl.when(k > 0)
            def _():
                acc_ref[...] += part

            @pl.when(k == nk - 1)
            def _():
                finish(acc_ref[...])

    return pl.pallas_call(
        body, name=name,
        out_shape=(jax.ShapeDtypeStruct((M, N), F32), jax.ShapeDtypeStruct((M, N), BF16)),
        grid=(M // tm, N // tn, nk),
        in_specs=[pl.BlockSpec((tm, tk), lambda i, j, k: (i, k)),
                  pl.BlockSpec((tk, tn), lambda i, j, k: (k, j)),
                  pl.BlockSpec((tm, tn), lambda i, j, k: (i, j)),
                  pl.BlockSpec((1, tn), lambda i, j, k: (0, j))],
        out_specs=(pl.BlockSpec((tm, tn), lambda i, j, k: (i, j)),
                   pl.BlockSpec((tm, tn), lambda i, j, k: (i, j))),
        scratch_shapes=[pltpu.VMEM((tm, tn) if nk > 1 else (8, 128), F32)],
        compiler_params=_params(),
    )(a, b, resid, gate)


def _norm_mod_fwd(x, g, shift, scale, name):
    S, D = x.shape
    ts = _row_tile(S, 256)

    def body(x_ref, g_ref, sh_ref, sc_ref, h_ref):
        xv = x_ref[...]
        rstd = lax.rsqrt(jnp.mean(xv * xv, axis=-1, keepdims=True) + EPS)
        y = xv * rstd * g_ref[...]
        h_ref[...] = (y * (1.0 + sc_ref[...]) + sh_ref[...]).astype(BF16)

    row = pl.BlockSpec((1, D), lambda i: (0, 0))
    return pl.pallas_call(
        body, name=name, out_shape=jax.ShapeDtypeStruct((S, D), BF16), grid=(S // ts,),
        in_specs=[pl.BlockSpec((ts, D), lambda i: (i, 0)), row, row, row],
        out_specs=pl.BlockSpec((ts, D), lambda i: (i, 0)),
        compiler_params=_params(),
    )(x, g, shift, scale)


def _norm_mod_bwd(x, dh, dres, g, scale, name):
    S, D = x.shape
    ts = _row_tile(S, 256)

    def body(x_ref, dh_ref, dr_ref, g_ref, sc_ref, dx_ref, dsh_ref, dsc_ref, dg_ref):
        i = pl.program_id(0)
        xv, dhv, gv = x_ref[...], dh_ref[...], g_ref[...]
        rstd = lax.rsqrt(jnp.mean(xv * xv, axis=-1, keepdims=True) + EPS)
        xhat = xv * rstd
        dn = dhv * (1.0 + sc_ref[...])
        dxhat = dn * gv
        proj = jnp.mean(dxhat * xhat, axis=-1, keepdims=True)
        dx_ref[...] = dr_ref[...] + rstd * (dxhat - xhat * proj)
        dsh = jnp.sum(dhv, axis=0, keepdims=True)
        dsc = jnp.sum(dhv * (xhat * gv), axis=0, keepdims=True)
        dg = jnp.sum(dn * xhat, axis=0, keepdims=True)

        @pl.when(i == 0)
        def _():
            dsh_ref[...] = dsh
            dsc_ref[...] = dsc
            dg_ref[...] = dg

        @pl.when(i > 0)
        def _():
            dsh_ref[...] += dsh
            dsc_ref[...] += dsc
            dg_ref[...] += dg

    tile = pl.BlockSpec((ts, D), lambda i: (i, 0))
    row = pl.BlockSpec((1, D), lambda i: (0, 0))
    vec = jax.ShapeDtypeStruct((1, D), F32)
    return pl.pallas_call(
        body, name=name, out_shape=(jax.ShapeDtypeStruct((S, D), F32), vec, vec, vec), grid=(S // ts,),
        in_specs=[tile, tile, tile, row, row], out_specs=(tile, row, row, row),
        compiler_params=_params(),
    )(x, dh, dres, g, scale)


def _gate_bwd(dx, f, gate, name):
    S, D = dx.shape
    ts = _row_tile(S, 256)

    def body(dx_ref, f_ref, g_ref, df_ref, dg_ref):
        i = pl.program_id(0)
        dxv = dx_ref[...]
        df_ref[...] = (dxv * g_ref[...]).astype(BF16)
        dg = jnp.sum(dxv * f_ref[...].astype(F32), axis=0, keepdims=True)

        @pl.when(i == 0)
        def _():
            dg_ref[...] = dg

        @pl.when(i > 0)
        def _():
            dg_ref[...] += dg

    tile = pl.BlockSpec((ts, D), lambda i: (i, 0))
    row = pl.BlockSpec((1, D), lambda i: (0, 0))
    return pl.pallas_call(
        body, name=name,
        out_shape=(jax.ShapeDtypeStruct((S, D), BF16), jax.ShapeDtypeStruct((1, D), F32)), grid=(S // ts,),
        in_specs=[tile, tile, row], out_specs=(tile, row),
        compiler_params=_params(),
    )(dx, f, gate)


def _ffn_in_fwd(h, w_t, name):
    S, D = h.shape
    F = w_t.shape[0] // 2
    tm, tn = _tile(S, TILES["ffn_fused"][0]), _tile(F, TILES["ffn_fused"][1])
    nj = F // tn

    def body(h_ref, wg_ref, wu_ref, o_ref):
        hv = h_ref[...]
        ug = lax.dot_general(hv, wg_ref[...], NT, preferred_element_type=F32)
        uu = lax.dot_general(hv, wu_ref[...], NT, preferred_element_type=F32)
        o_ref[...] = (ug * jax.nn.sigmoid(ug) * uu).astype(BF16)

    return pl.pallas_call(
        body, name=name, out_shape=jax.ShapeDtypeStruct((S, F), BF16), grid=(nj, S // tm),
        in_specs=[pl.BlockSpec((tm, D), lambda j, i: (i, 0)),
                  pl.BlockSpec((tn, D), lambda j, i: (j, 0)),
                  pl.BlockSpec((tn, D), lambda j, i: (j + nj, 0))],
        out_specs=pl.BlockSpec((tm, tn), lambda j, i: (i, j)),
        compiler_params=_params(),
    )(h, w_t, w_t)


def _ffn_mid_bwd(h, df, w_in_t, w_out, name):
    S, D = h.shape
    F = w_in_t.shape[0] // 2
    tm, tn = _tile(S, TILES["ffn_fused"][0]), _tile(F, TILES["ffn_fused"][1])
    nj = F // tn

    def body(h_ref, df_ref, wg_ref, wu_ref, wo_ref, dg_ref, du_ref):
        hv = h_ref[...]
        ug = lax.dot_general(hv, wg_ref[...], NT, preferred_element_type=F32)
        uu = lax.dot_general(hv, wu_ref[...], NT, preferred_element_type=F32)
        dact = lax.dot_general(df_ref[...], wo_ref[...], NT, preferred_element_type=F32)
        sig = jax.nn.sigmoid(ug)
        dg_ref[...] = (dact * uu * (sig * (1.0 + ug * (1.0 - sig)))).astype(BF16)
        du_ref[...] = (dact * (ug * sig)).astype(BF16)

    out = jax.ShapeDtypeStruct((S, F), BF16)
    return pl.pallas_call(
        body, name=name, out_shape=(out, out), grid=(nj, S // tm),
        in_specs=[pl.BlockSpec((tm, D), lambda j, i: (i, 0)),
                  pl.BlockSpec((tm, D), lambda j, i: (i, 0)),
                  pl.BlockSpec((tn, D), lambda j, i: (j, 0)),
                  pl.BlockSpec((tn, D), lambda j, i: (j + nj, 0)),
                  pl.BlockSpec((tn, D), lambda j, i: (j, 0))],
        out_specs=(pl.BlockSpec((tm, tn), lambda j, i: (i, j)), pl.BlockSpec((tm, tn), lambda j, i: (i, j))),
        compiler_params=_params(),
    )(h, df, w_in_t, w_in_t, w_out)


def _merge_fwd(o_a, o_b, o_c, gates, w_branch, name, *, tm=256):
    S, W = o_a.shape
    D = w_branch.shape[2]
    tm = _row_tile(S, tm)

    def body(oa_ref, ob_ref, oc_ref, g_ref, w_ref, m_ref):
        acc = None
        for k, o_ref in enumerate((oa_ref, ob_ref, oc_ref)):
            y = jnp.dot(o_ref[...], w_ref[k], preferred_element_type=F32)
            t = jax.nn.sigmoid(g_ref[:, k * D:(k + 1) * D]) * y
            acc = t if acc is None else acc + t
        m_ref[...] = acc.astype(BF16)

    o_spec = pl.BlockSpec((tm, W), lambda i: (i, 0))
    return pl.pallas_call(
        body, name=name, out_shape=jax.ShapeDtypeStruct((S, D), BF16), grid=(S // tm,),
        in_specs=[o_spec, o_spec, o_spec, pl.BlockSpec((tm, 3 * D), lambda i: (i, 0)),
                  pl.BlockSpec((3, W, D), lambda i: (0, 0, 0))],
        out_specs=pl.BlockSpec((tm, D), lambda i: (i, 0)),
        compiler_params=_params(),
    )(o_a, o_b, o_c, gates, w_branch)


def _merge_bwd(dmerged, o_a, o_b, o_c, gates, w_branch, name, *, tm=256):
    S, W = o_a.shape
    D = w_branch.shape[2]
    tm = _row_tile(S, tm)
    n_heads = W // 64

    def body(dm_ref, oa_ref, ob_ref, oc_ref, g_ref, w_ref, dg_ref, dy_ref,
             doa_ref, dob_ref, doc_ref, dla_ref, dlb_ref, dlc_ref):
        dm = dm_ref[...]
        branches = ((oa_ref, doa_ref, dla_ref), (ob_ref, dob_ref, dlb_ref), (oc_ref, doc_ref, dlc_ref))
        for k, (o_ref, do_ref, dl_ref) in enumerate(branches):
            wk = w_ref[k]
            ov = o_ref[...]
            y = jnp.dot(ov, wk, preferred_element_type=F32)
            g = jax.nn.sigmoid(g_ref[:, k * D:(k + 1) * D])
            dy = (dm * g).astype(BF16)
            dy_ref[:, k * D:(k + 1) * D] = dy
            dg_ref[:, k * D:(k + 1) * D] = (dm * y * (g * (1.0 - g))).astype(BF16)
            do16 = lax.dot_general(dy, wk, NT, preferred_element_type=F32).astype(BF16)
            do_ref[...] = do16
            prod = do16.astype(F32) * ov.astype(F32)
            for h in range(n_heads):
                dl_ref[:, h:h + 1] = jnp.sum(prod[:, 64 * h:64 * (h + 1)], axis=1, keepdims=True)

    o_spec = pl.BlockSpec((tm, W), lambda i: (i, 0))
    wide = pl.BlockSpec((tm, 3 * D), lambda i: (i, 0))
    dl_spec = pl.BlockSpec((tm, n_heads), lambda i: (i, 0))
    o_out = jax.ShapeDtypeStruct((S, W), BF16)
    wide_out = jax.ShapeDtypeStruct((S, 3 * D), BF16)
    dl_out = jax.ShapeDtypeStruct((S, n_heads), F32)
    return pl.pallas_call(
        body, name=name, out_shape=(wide_out, wide_out, o_out, o_out, o_out, dl_out, dl_out, dl_out),
        grid=(S // tm,),
        in_specs=[pl.BlockSpec((tm, D), lambda i: (i, 0)), o_spec, o_spec, o_spec, wide,
                  pl.BlockSpec((3, W, D), lambda i: (0, 0, 0))],
        out_specs=(wide, wide, o_spec, o_spec, o_spec, dl_spec, dl_spec, dl_spec),
        compiler_params=_params(),
    )(dmerged, o_a, o_b, o_c, gates, w_branch)


def _band_mask(variant, t_abs, s_abs):
    if variant == "b":
        return s_abs <= t_abs
    qc, kc = t_abs >> 6, s_abs >> 6
    return (kc <= qc) & (kc >= qc - (2 if variant == "a" else 8))


def _attn_fwd(variant, qkv, name, *, sinks=None, slopes=None, cq_col=None, ck_row=None, bias=None):
    S = qkv.shape[0]
    nb = S // BLK
    qb, kb, vb = ATTN_COLS[variant]
    shared_kv = variant == "a"
    win = ATTN_WINDOW.get(variant)

    def body(*refs):
        if variant == "a":
            q_ref, k_ref, v_ref, sink_ref, slope_ref, o_ref, lse_ref = refs
        elif variant == "b":
            q_ref, k_ref, v_ref, cq_ref, ck_ref, o_ref, lse_ref = refs
        else:
            q_ref, k_ref, v_ref, bias_ref, o_ref, lse_ref = refs
        p, i = pl.program_id(0), pl.program_id(1)
        lane = lax.broadcasted_iota(jnp.int32, (BLK, BLK), 1)
        t_abs = i * BLK + lax.broadcasted_iota(jnp.int32, (BLK, 1), 0)
        q2 = q_ref[...].astype(F32) * 0.125

        def compute(start, n_keys):
            k_w = k_ref[pl.ds(start, n_keys), :]
            v_w = v_ref[pl.ds(start, n_keys), :]
            s_abs = start + lax.broadcasted_iota(jnp.int32, (1, n_keys), 1)
            valid = _band_mask(variant, t_abs, s_abs)
            outs = []
            for half in (0, 1):
                hmask = (lane >= 64) if half else (lane < 64)
                qh = jnp.where(hmask, q2, 0.0)
                if shared_kv:
                    swap = (p // 2) != half
                    qh = jnp.where(swap, pltpu.roll(qh, 64, 1), qh)
                s = lax.dot_general(qh.astype(BF16), k_w, NT, preferred_element_type=F32)
                if variant == "a":
                    head = 2 * p + half
                    s = s + (-slope_ref[head]) * jnp.abs(t_abs - s_abs).astype(F32)
                elif variant == "b":
                    s = s + cq_ref[:, half:half + 1] - ck_ref[half:half + 1, pl.ds(start, n_keys)]
                else:
                    j0 = start // BLK
                    s = s + jnp.concatenate(
                        [bias_ref[half, jnp.clip(i - j0 - b, 0, 4)] for b in range(win)], axis=1)
                s = jnp.where(valid, s, NEG_INF)
                m = jnp.max(s, axis=1, keepdims=True)
                if variant == "a":
                    m = jnp.maximum(m, sink_ref[head])
                pe = jnp.exp(s - m)
                l = jnp.sum(pe, axis=1, keepdims=True)
                if variant == "a":
                    l = l + jnp.exp(sink_ref[head] - m)
                out = jnp.dot(pe.astype(BF16), v_w, preferred_element_type=F32) / l
                if shared_kv:
                    out = jnp.where(swap, pltpu.roll(out, 64, 1), out)
                outs.append(out)
                lse_ref[:, half:half + 1] = m + jnp.log(l)
            o_ref[...] = jnp.where(lane < 64, outs[0], outs[1]).astype(BF16)

        if variant == "b":
            for g in range(S // GROUP):
                pl.when(i // 4 == g)(functools.partial(compute, 0, (g + 1) * GROUP))
        else:
            start = jnp.clip(i - (win - 1), 0, nb - win) * BLK
            compute(pl.multiple_of(start, BLK), win * BLK)

    kv_col = (lambda p, i: (0, kb)) if shared_kv else (lambda p, i: (0, kb + p))
    vv_col = (lambda p, i: (0, vb)) if shared_kv else (lambda p, i: (0, vb + p))
    in_specs = [pl.BlockSpec((BLK, BLK), lambda p, i: (i, qb + p)),
                pl.BlockSpec((S, BLK), kv_col), pl.BlockSpec((S, BLK), vv_col)]
    args = [qkv, qkv, qkv]
    if variant == "a":
        in_specs += [pl.BlockSpec(memory_space=pltpu.SMEM), pl.BlockSpec(memory_space=pltpu.SMEM)]
        args += [sinks, slopes]
    elif variant == "b":
        in_specs += [pl.BlockSpec((None, BLK, 2), lambda p, i: (p, i, 0)),
                     pl.BlockSpec((None, 2, S), lambda p, i: (p, 0, 0))]
        args += [cq_col, ck_row]
    else:
        in_specs += [pl.BlockSpec((2, 5, BLK, BLK), lambda p, i: (p, 0, 0, 0))]
        args += [bias]
    return pl.pallas_call(
        body, name=name,
        out_shape=(jax.ShapeDtypeStruct((S, 512), BF16), jax.ShapeDtypeStruct((4, S, 2), F32)),
        grid=(4, nb), in_specs=in_specs,
        out_specs=(pl.BlockSpec((BLK, BLK), lambda p, i: (i, p)),
                   pl.BlockSpec((None, BLK, 2), lambda p, i: (p, i, 0))),
        compiler_params=_params(),
    )(*args)


def _attn_bwd(variant, qkv, do, lse_row, delta_row, name, *, sinks=None, slopes=None, cq_row=None,
              ck_col=None, bias_t=None):
    S = qkv.shape[0]
    nb = S // BLK
    qb, kb, vb = ATTN_COLS[variant]
    shared_kv = variant == "a"
    win = ATTN_WINDOW.get(variant)

    def body(*refs):
        if variant == "a":
            (q_ref, k_ref, v_ref, do_ref, lse_ref, dl_ref, sink_ref, slope_ref,
             dq_ref, dk_ref, dv_ref, ex_ref) = refs
        elif variant == "b":
            (q_ref, k_ref, v_ref, do_ref, lse_ref, dl_ref, cq_ref, ck_ref,
             dq_ref, dk_ref, dv_ref, ex_ref, dcq_ref) = refs
        else:
            (q_ref, k_ref, v_ref, do_ref, lse_ref, dl_ref, bias_ref,
             dq_ref, dk_ref, dv_ref, ex_ref) = refs
        p, j = pl.program_id(0), pl.program_id(1)
        lane = lax.broadcasted_iota(jnp.int32, (BLK, BLK), 1)
        s_abs = j * BLK + lax.broadcasted_iota(jnp.int32, (BLK, 1), 0)
        off_k = pl.multiple_of(j * BLK, BLK)
        k2 = k_ref[...].astype(F32)
        v2 = v_ref[...].astype(F32)
        hmasks = [(lane < 64), (lane >= 64)]
        if shared_kv:
            kv_lane = (lane >> 6) == (p // 2)
            swaps = [(p // 2) != half for half in (0, 1)]
            k_src, v_src = jnp.where(kv_lane, k2, 0.0), jnp.where(kv_lane, v2, 0.0)
            k_al = [jnp.where(swaps[h], pltpu.roll(k_src, 64, 1), k_src) for h in (0, 1)]
            v_al = [jnp.where(swaps[h], pltpu.roll(v_src, 64, 1), v_src) for h in (0, 1)]
        else:
            k_al = [jnp.where(hmasks[h], k2, 0.0) for h in (0, 1)]
            v_al = [jnp.where(hmasks[h], v2, 0.0) for h in (0, 1)]
        k_al = [(t * 0.125).astype(BF16) for t in k_al]
        v_al = [t.astype(BF16) for t in v_al]

        @pl.when(j == 0)
        def _():
            dq_ref[...] = jnp.zeros_like(dq_ref)
            if variant == "b":
                dcq_ref[...] = jnp.zeros_like(dcq_ref)
            else:
                ex_ref[...] = jnp.zeros_like(ex_ref)

        def to_kv_lanes(x, h):
            x = jnp.where(hmasks[h], x, 0.0)
            if shared_kv:
                x = jnp.where(swaps[h], pltpu.roll(x, 64, 1), x)
            return x

        def compute(start, n_q):
            q_w = q_ref[pl.ds(start, n_q), :]
            do_w = do_ref[pl.ds(start, n_q), :]
            t_abs = start + lax.broadcasted_iota(jnp.int32, (1, n_q), 1)
            valid = _band_mask(variant, t_abs, s_abs)
            dk_acc = dv_acc = None
            ds_both = []
            for half in (0, 1):
                s = lax.dot_general(k_al[half], q_w, NT, preferred_element_type=F32)
                if variant == "a":
                    s = s + (-slope_ref[2 * p + half]) * jnp.abs(t_abs - s_abs).astype(F32)
                elif variant == "b":
                    s = s + cq_ref[half:half + 1, pl.ds(start, n_q)] - ck_ref[:, half:half + 1]
                else:
                    i0 = start // BLK
                    s = s + jnp.concatenate(
                        [bias_ref[half, jnp.clip(i0 + b - j, 0, 4)] for b in range(win)], axis=1)
                pr = jnp.where(valid, jnp.exp(s - lse_ref[half:half + 1, pl.ds(start, n_q)]), 0.0)
                dp = lax.dot_general(v_al[half], do_w, NT, preferred_element_type=F32)
                ds = pr * (dp - dl_ref[half:half + 1, pl.ds(start, n_q)])
                ds16 = ds.astype(BF16)
                dv_h = to_kv_lanes(jnp.dot(pr.astype(BF16), do_w, preferred_element_type=F32), half)
                dk_h = to_kv_lanes(jnp.dot(ds16, q_w, preferred_element_type=F32) * 0.125, half)
                dv_acc = dv_h if dv_acc is None else dv_acc + dv_h
                dk_acc = dk_h if dk_acc is None else dk_acc + dk_h
                ds_both.append(ds16)
                if variant == "b":
                    ex_ref[:, half:half + 1] = -jnp.sum(ds, axis=1, keepdims=True)
                    dcq_ref[half:half + 1, pl.ds(start, n_q)] += jnp.sum(ds, axis=0, keepdims=True)
                elif variant == "c":
                    for b in range(win):
                        ex_ref[half, jnp.clip(i0 + b - j, 0, 4)] += ds[:, b * BLK:(b + 1) * BLK]
            dq_t = lax.dot_general(jnp.concatenate(k_al, axis=0), jnp.concatenate(ds_both, axis=0), TN,
                                   preferred_element_type=F32)
            dq_ref[:, pl.ds(start, n_q)] += dq_t
            if shared_kv:
                @pl.when(p == 0)
                def _():
                    dk_ref[pl.ds(off_k, BLK), :] = dk_acc
                    dv_ref[pl.ds(off_k, BLK), :] = dv_acc

                @pl.when(p > 0)
                def _():
                    dk_ref[pl.ds(off_k, BLK), :] += dk_acc
                    dv_ref[pl.ds(off_k, BLK), :] += dv_acc
            else:
                dk_ref[pl.ds(off_k, BLK), :] = dk_acc
                dv_ref[pl.ds(off_k, BLK), :] = dv_acc

        if variant == "b":
            for g in range(S // GROUP):
                pl.when(j // 4 == g)(functools.partial(compute, g * GROUP, S - g * GROUP))
        else:
            start = jnp.clip(j, 0, nb - win) * BLK
            compute(pl.multiple_of(start, BLK), win * BLK)

        if variant == "a":
            for half in (0, 1):
                p_sink = jnp.exp(sink_ref[2 * p + half] - lse_ref[half:half + 1, pl.ds(off_k, BLK)])
                term = p_sink * dl_ref[half:half + 1, pl.ds(off_k, BLK)]
                ex_ref[half:half + 1, :] += -jnp.sum(term, axis=1, keepdims=True)

    col = lambda c0: (lambda p, j: (0, c0 + p))
    kv_blk = (lambda c0: (lambda p, j: (j, c0))) if shared_kv else (lambda c0: (lambda p, j: (j, c0 + p)))
    pair = lambda p, j: (0, p)
    row_stat = pl.BlockSpec((None, 2, S), lambda p, j: (p, 0, 0))
    in_specs = [pl.BlockSpec((S, BLK), col(qb)),
                pl.BlockSpec((BLK, BLK), kv_blk(kb)), pl.BlockSpec((BLK, BLK), kv_blk(vb)),
                pl.BlockSpec((S, BLK), pair), row_stat, row_stat]
    args = [qkv, qkv, qkv, do, lse_row, delta_row]
    kv_width = BLK if shared_kv else 512
    kv_out = pl.BlockSpec((S, BLK), (lambda p, j: (0, 0)) if shared_kv else pair)
    out_shape = [jax.ShapeDtypeStruct((512, S), F32), jax.ShapeDtypeStruct((S, kv_width), F32),
                 jax.ShapeDtypeStruct((S, kv_width), F32)]
    out_specs = [pl.BlockSpec((BLK, S), lambda p, j: (p, 0)), kv_out, kv_out]
    if variant == "a":
        in_specs += [pl.BlockSpec(memory_space=pltpu.SMEM), pl.BlockSpec(memory_space=pltpu.SMEM)]
        args += [sinks, slopes]
        out_shape.append(jax.ShapeDtypeStruct((4, 8, BLK), F32))
        out_specs.append(pl.BlockSpec((None, 8, BLK), lambda p, j: (p, 0, 0)))
    elif variant == "b":
        in_specs += [row_stat, pl.BlockSpec((None, BLK, 2), lambda p, j: (p, j, 0))]
        args += [cq_row, ck_col]
        out_shape += [jax.ShapeDtypeStruct((4, S, 2), F32), jax.ShapeDtypeStruct((4, 2, S), F32)]
        out_specs += [pl.BlockSpec((None, BLK, 2), lambda p, j: (p, j, 0)), row_stat]
    else:
        in_specs += [pl.BlockSpec((2, 5, BLK, BLK), lambda p, j: (p, 0, 0, 0))]
        args += [bias_t]
        out_shape.append(jax.ShapeDtypeStruct((8, 5, BLK, BLK), F32))
        out_specs.append(pl.BlockSpec((2, 5, BLK, BLK), lambda p, j: (p, 0, 0, 0)))
    return pl.pallas_call(
        body, name=name, out_shape=tuple(out_shape), grid=(4, nb),
        in_specs=in_specs, out_specs=tuple(out_specs),
        compiler_params=_params(),
    )(*args)


def _log_sigmoid(x):
    return jnp.minimum(x, 0.0) - jnp.log(1.0 + jnp.exp(-jnp.abs(x)))


def _forget_fwd(fb, b_forget, name):
    S = fb.shape[0]
    nb = S // BLK

    def body(fb_ref, b_ref, cum_ref, carry_ref):
        i = pl.program_id(0)
        logf = _log_sigmoid(fb_ref[...] + b_ref[...])
        r = lax.broadcasted_iota(jnp.int32, (BLK, BLK), 0)
        c = lax.broadcasted_iota(jnp.int32, (BLK, BLK), 1)
        tri = (c <= r).astype(F32)

        @pl.when(i == 0)
        def _():
            carry_ref[...] = jnp.zeros_like(carry_ref)

        cum = jnp.dot(tri, logf, preferred_element_type=F32, precision=HIGHEST) + carry_ref[0:1, :]
        cum_ref[...] = cum
        carry_ref[...] = jnp.broadcast_to(cum[BLK - 1:BLK, :], carry_ref.shape)

    return pl.pallas_call(
        body, name=name, out_shape=jax.ShapeDtypeStruct((S, BLK), F32), grid=(nb,),
        in_specs=[pl.BlockSpec((BLK, BLK), lambda i: (i, 0)), pl.BlockSpec((1, BLK), lambda i: (0, 0))],
        out_specs=pl.BlockSpec((BLK, BLK), lambda i: (i, 0)),
        scratch_shapes=[pltpu.VMEM((8, BLK), F32)],
        compiler_params=_params(),
    )(fb, b_forget)


def _forget_bwd(dcum_q, dcum_k, fb, b_forget, name):
    S = fb.shape[0]
    nb = S // BLK

    def body(dq_ref, dk_ref, fb_ref, b_ref, dfb_ref, db_ref, carry_ref):
        g = pl.program_id(0)
        r = lax.broadcasted_iota(jnp.int32, (BLK, BLK), 0)
        c = lax.broadcasted_iota(jnp.int32, (BLK, BLK), 1)
        tri = (c >= r).astype(F32)

        @pl.when(g == 0)
        def _():
            carry_ref[...] = jnp.zeros_like(carry_ref)

        dcum = dq_ref[...] + dk_ref[...]
        dlogf = jnp.dot(tri, dcum, preferred_element_type=F32, precision=HIGHEST) + carry_ref[0:1, :]
        carry_ref[...] = jnp.broadcast_to(dlogf[0:1, :], carry_ref.shape)
        x = fb_ref[...] + b_ref[...]
        dfb = jnp.where(c < N_FORGET, dlogf * jax.nn.sigmoid(-x), 0.0)
        dfb_ref[...] = dfb
        db = jnp.sum(dfb, axis=0, keepdims=True)

        @pl.when(g == 0)
        def _():
            db_ref[...] = db

        @pl.when(g > 0)
        def _():
            db_ref[...] += db

    rev = pl.BlockSpec((BLK, BLK), lambda g: (nb - 1 - g, 0))
    row = pl.BlockSpec((1, BLK), lambda g: (0, 0))
    return pl.pallas_call(
        body, name=name,
        out_shape=(jax.ShapeDtypeStruct((S, BLK), F32), jax.ShapeDtypeStruct((1, BLK), F32)), grid=(nb,),
        in_specs=[rev, rev, rev, row], out_specs=(rev, row),
        scratch_shapes=[pltpu.VMEM((8, BLK), F32)],
        compiler_params=_params(),
    )(dcum_q, dcum_k, fb, b_forget)


def _skew(x, sign):
    row = lax.broadcasted_iota(jnp.int32, x.shape, 0)
    for b in range(7):
        amount = (1 << b) if sign > 0 else 256 - (1 << b)
        x = jnp.where(((row >> b) & 1) == 1, pltpu.roll(x, amount, 1), x)
    return x


def _rel_bases(rel):
    far = rel[:, 256:257]
    far127 = jnp.broadcast_to(far, (rel.shape[0], 127))
    base0 = jnp.concatenate([rel[:, 128:0:-1], far, rel[:, 255:128:-1]], axis=1)
    base1 = jnp.concatenate([rel[:, 256:128:-1], far, far127], axis=1)
    base0_t = jnp.concatenate([rel[:, 128:256], far, rel[:, 1:128]], axis=1)
    base1_t = jnp.concatenate([jnp.broadcast_to(far, (rel.shape[0], 128)), far, rel[:, 129:256]], axis=1)
    return jnp.stack([base0, base1, base0_t, base1_t], axis=1)


def _rel_expand(bases, name):
    def body(b_ref, t_ref, tt_ref):
        far = jnp.broadcast_to(b_ref[1:2, 0:1], (BLK, BLK))
        for k, out_ref in ((0, t_ref), (2, tt_ref)):
            for d in (0, 1):
                x = jnp.broadcast_to(b_ref[k + d:k + d + 1, :], (BLK, 2 * BLK))
                out_ref[d] = _skew(x, 1)[:, :BLK]
            for d in (2, 3, 4):
                out_ref[d] = far

    out = jax.ShapeDtypeStruct((8, 5, BLK, BLK), F32)
    spec = pl.BlockSpec((None, 5, BLK, BLK), lambda h: (h, 0, 0, 0))
    return pl.pallas_call(
        body, name=name, out_shape=(out, out), grid=(8,),
        in_specs=[pl.BlockSpec((None, 4, 2 * BLK), lambda h: (h, 0, 0))], out_specs=(spec, spec),
        compiler_params=_params(),
    )(bases)


def _rel_reduce(dtiles_t, name):
    def body(dt_ref, o_ref):
        zeros = jnp.zeros((BLK, BLK), F32)
        sums = []
        for d in (0, 1):
            x = _skew(jnp.concatenate([dt_ref[d], zeros], axis=1), -1)
            sums.append(jnp.broadcast_to(jnp.sum(x, axis=0, keepdims=True), (8, 2 * BLK)))
        lane = lax.broadcasted_iota(jnp.int32, (8, 2 * BLK), 1)
        main = pltpu.roll(sums[0], BLK, 1) + jnp.where(lane > BLK, sums[1], 0.0)
        far = jnp.sum(jnp.where(lane < BLK, sums[1], 0.0)[0:1], axis=1, keepdims=True)
        far = far + jnp.sum(jnp.sum(dt_ref[2] + dt_ref[3] + dt_ref[4], axis=0, keepdims=True), axis=1, keepdims=True)
        o_ref[...] = jnp.concatenate([main[0:1], jnp.broadcast_to(far, (1, BLK))], axis=1)

    return pl.pallas_call(
        body, name=name, out_shape=jax.ShapeDtypeStruct((8, 1, 3 * BLK), F32), grid=(8,),
        in_specs=[pl.BlockSpec((None, 5, BLK, BLK), lambda h: (h, 0, 0, 0))],
        out_specs=pl.BlockSpec((None, 1, 3 * BLK), lambda h: (h, 0, 0)),
        compiler_params=_params(),
    )(dtiles_t)


def _final_loss(x, target, g, name):
    S, D = x.shape
    ts = _row_tile(S, 256)

    def body(x_ref, t_ref, g_ref, dx_ref, loss_ref, dg_ref):
        i = pl.program_id(0)
        xv, gv = x_ref[...], g_ref[...]
        rstd = lax.rsqrt(jnp.mean(xv * xv, axis=-1, keepdims=True) + EPS)
        xhat = xv * rstd
        err = xhat * gv - t_ref[...]
        part = 0.5 * jnp.sum(jnp.mean(err * err, axis=-1, keepdims=True), axis=0, keepdims=True)
        dy = err / D
        dg = jnp.sum(dy * xhat, axis=0, keepdims=True)
        dxhat = dy * gv
        proj = jnp.mean(dxhat * xhat, axis=-1, keepdims=True)
        dx_ref[...] = rstd * (dxhat - xhat * proj)

        @pl.when(i == 0)
        def _():
            loss_ref[...] = jnp.broadcast_to(part, loss_ref.shape)
            dg_ref[...] = dg

        @pl.when(i > 0)
        def _():
            loss_ref[...] += jnp.broadcast_to(part, loss_ref.shape)
            dg_ref[...] += dg

    tile = pl.BlockSpec((ts, D), lambda i: (i, 0))
    row = pl.BlockSpec((1, D), lambda i: (0, 0))
    return pl.pallas_call(
        body, name=name,
        out_shape=(jax.ShapeDtypeStruct((S, D), F32), jax.ShapeDtypeStruct((8, 128), F32),
                   jax.ShapeDtypeStruct((1, D), F32)),
        grid=(S // ts,), in_specs=[tile, tile, row],
        out_specs=(tile, pl.BlockSpec((8, 128), lambda i: (0, 0)), row),
        compiler_params=_params(),
    )(x, target, g)


def _ada_fwd(c_all, w_ada, name):
    L, D, E = w_ada.shape

    def body(c_ref, w_ref, o_ref):
        cv = c_ref[...]
        cond = cv * jax.nn.sigmoid(cv)
        o_ref[...] = jnp.dot(cond, w_ref[...], preferred_element_type=F32, precision=HIGHEST)

    return pl.pallas_call(
        body, name=name, out_shape=jax.ShapeDtypeStruct((L, N_DEV, E), F32), grid=(L,),
        in_specs=[pl.BlockSpec((N_DEV, D), lambda l: (0, 0)), pl.BlockSpec((None, D, E), lambda l: (l, 0, 0))],
        out_specs=pl.BlockSpec((None, N_DEV, E), lambda l: (l, 0, 0)),
        compiler_params=_params(),
    )(c_all, w_ada)


def _ada_bwd(c_all_t, dmod, name):
    D = c_all_t.shape[0]
    L, _, E = dmod.shape

    def body(c_ref, d_ref, o_ref):
        cv = c_ref[...]
        cond = cv * jax.nn.sigmoid(cv)
        acc = None
        for b in range(N_DEV):
            t = cond[:, b:b + 1] * d_ref[b:b + 1, :]
            acc = t if acc is None else acc + t
        o_ref[...] = acc

    return pl.pallas_call(
        body, name=name, out_shape=jax.ShapeDtypeStruct((L, D, E), F32), grid=(L,),
        in_specs=[pl.BlockSpec((D, N_DEV), lambda l: (0, 0)), pl.BlockSpec((None, N_DEV, E), lambda l: (l, 0, 0))],
        out_specs=pl.BlockSpec((None, D, E), lambda l: (l, 0, 0)),
        compiler_params=_params(),
    )(c_all_t, dmod)


def _adamw(w, m, v, g_parts, name, me):
    L, R, C = w.shape
    tr = _row_tile(R, max(8, (256 * 1024 // max(C, 128)) // 8 * 8))
    nr = R // tr
    c1 = 1.0 - ADAM_B1 ** ADAM_STEP
    c2 = 1.0 - ADAM_B2 ** ADAM_STEP
    direct = [isinstance(p, tuple) for p in g_parts]
    n_in = sum(2 if d else 1 for d in direct)

    def body(me_ref, w_ref, m_ref, v_ref, *rest):
        g_refs, (go_ref, d_ref, mo_ref, vo_ref) = list(rest[:n_in]), rest[n_in:]
        layer = pl.program_id(0)
        g = None
        for l in range(L):
            land_ref = g_refs.pop(0)
            own = g_refs.pop(0)[...].astype(F32) if direct[l] else None
            gl = None
            for k in range(land_ref.shape[0]):
                part = land_ref[k].astype(F32)
                if direct[l]:
                    part = jnp.where(me_ref[0] == k, own, part)
                gl = part if gl is None else gl + part
            g = gl if g is None else jnp.where(layer == l, gl, g)
        mn = ADAM_B1 * m_ref[...] + (1.0 - ADAM_B1) * g
        vn = ADAM_B2 * v_ref[...] + (1.0 - ADAM_B2) * (g * g)
        m_hat = mn / c1
        v_hat = vn / c2
        go_ref[...] = g
        d_ref[...] = -ADAM_LR * (m_hat / (jnp.sqrt(v_hat) + ADAM_EPS) + ADAM_WD * w_ref[...])
        mo_ref[...] = mn
        vo_ref[...] = vn

    def rows(l, layer, i):
        return jnp.where(layer == l, i, 0 if l > 0 else nr - 1)

    in_specs, operands = [], []
    for l, p in enumerate(g_parts):
        land, sent = p if direct[l] else (p, None)
        in_specs.append(pl.BlockSpec((land.shape[0], tr, C), lambda layer, i, me_ref, l=l: (0, rows(l, layer, i), 0)))
        operands.append(land)
        if direct[l]:
            in_specs.append(pl.BlockSpec((None, tr, C), lambda layer, i, me_ref, l=l: (me_ref[0], rows(l, layer, i), 0)))
            operands.append(sent)
    tile = pl.BlockSpec((None, tr, C), lambda layer, i, me_ref: (layer, i, 0))
    out = jax.ShapeDtypeStruct((L, R, C), F32)
    return pl.pallas_call(
        body, name=name, out_shape=(out, out, out, out),
        grid_spec=pltpu.PrefetchScalarGridSpec(
            num_scalar_prefetch=1, grid=(L, nr), in_specs=[tile, tile, tile] + in_specs,
            out_specs=(tile, tile, tile, tile)),
        compiler_params=_params(),
    )(me, w, m, v, *operands)


def _pair_add(pieces, recv, core, name):
    _, _, R, C = pieces.shape
    tr = _row_tile(R, max(8, (512 * 1024 // max(C, 128)) // 8 * 8))

    def body(core_ref, a_ref, b_ref, o_ref):
        o_ref[...] = (a_ref[...].astype(F32) + b_ref[...].astype(F32)).astype(BF16)

    return pl.pallas_call(
        body, name=name, out_shape=jax.ShapeDtypeStruct((4, R, C), BF16),
        grid_spec=pltpu.PrefetchScalarGridSpec(
            num_scalar_prefetch=1, grid=(4, R // tr),
            in_specs=[pl.BlockSpec((None, None, tr, C), lambda k, i, core_ref: (core_ref[0], k, i, 0)),
                      pl.BlockSpec((None, tr, C), lambda k, i, core_ref: (k, i, 0))],
            out_specs=pl.BlockSpec((None, tr, C), lambda k, i, core_ref: (k, i, 0))),
        compiler_params=_params(),
    )(core, pieces, recv)


MESH = pl.DeviceIdType.MESH
ANY = pl.BlockSpec(memory_space=pl.ANY)


def _position():
    return lax.axis_index("x"), lax.axis_index("y"), lax.axis_index("c")


def _small_all_gather(v, name):
    m_per, n = v.shape

    def body(x_ref, out_ref, send_sems, recv_sems, local_sem):
        x, y, c = _position()
        me, sibling = (x, y, c), (x, y, 1 - c)
        chips = [(1 - x, y), (x, 1 - y), (1 - x, 1 - y)]

        def rows(px, py, pc):
            return out_ref.at[pl.ds((4 * px + 2 * py + pc) * m_per, m_per), :]

        def copy(k, block, to, src=None):
            return pltpu.make_async_remote_copy(
                src_ref=rows(*block) if src is None else src, dst_ref=rows(*block),
                send_sem=send_sems.at[k], recv_sem=recv_sems.at[k], device_id=to, device_id_type=MESH)

        mine = pltpu.make_async_copy(x_ref, rows(*me), local_sem)
        mine.start()
        first = [copy(0, me, sibling, src=x_ref)]
        first += [copy(1 + j, me, (*chip, c), src=x_ref) for j, chip in enumerate(chips)]
        for cp in first:
            cp.start()
        passed = [copy(4 + j, (*chip, c), sibling) for j, chip in enumerate(chips)]
        for j, chip in enumerate(chips):
            copy(1 + j, (*chip, c), me).wait_recv()
            passed[j].start()
        copy(0, sibling, me).wait_recv()
        for j, chip in enumerate(chips):
            copy(4 + j, (*chip, 1 - c), me).wait_recv()
        for cp in first + passed:
            cp.wait_send()
        mine.wait()

    return pl.pallas_call(
        body, name=name, out_shape=jax.ShapeDtypeStruct((N_DEV * m_per, n), v.dtype),
        in_specs=[pl.BlockSpec(memory_space=pltpu.VMEM)], out_specs=pl.BlockSpec(memory_space=pltpu.VMEM),
        scratch_shapes=[pltpu.SemaphoreType.DMA((7,)), pltpu.SemaphoreType.DMA((7,)), pltpu.SemaphoreType.DMA],
    )(v)


def _big_all_gather(shards, name):
    n_arr = len(shards)

    def body(*refs):
        x_refs, out_refs = refs[:n_arr], refs[n_arr:2 * n_arr]
        send_sems, recv_sems, local_sems = refs[2 * n_arr:]
        x, y, c = _position()
        me, sibling = (x, y, c), (x, y, 1 - c)
        chips = [(1 - x, y), (x, 1 - y), (1 - x, 1 - y)]

        def slot(a, px, py, pc):
            return out_refs[a].at[4 * px + 2 * py + pc]

        def copy(a, k, block, to, src=None):
            return pltpu.make_async_remote_copy(
                src_ref=slot(a, *block) if src is None else src, dst_ref=slot(a, *block),
                send_sem=send_sems.at[a, k], recv_sem=recv_sems.at[a, k], device_id=to, device_id_type=MESH)

        mine = [pltpu.make_async_copy(x_refs[a], slot(a, *me), local_sems.at[a]) for a in range(n_arr)]
        for cp in mine:
            cp.start()
        first = []
        for j, chip in enumerate(chips):
            first += [copy(a, 1 + j, me, (*chip, c), src=x_refs[a]) for a in range(n_arr)]
        first += [copy(a, 0, me, sibling, src=x_refs[a]) for a in range(n_arr)]
        for cp in first:
            cp.start()
        passed = []
        for j, chip in enumerate(chips):
            for a in range(n_arr):
                copy(a, 1 + j, (*chip, c), me).wait_recv()
                fwd = copy(a, 4 + j, (*chip, c), sibling)
                fwd.start()
                passed.append(fwd)
        for a in range(n_arr):
            copy(a, 0, sibling, me).wait_recv()
        for j, chip in enumerate(chips):
            for a in range(n_arr):
                copy(a, 4 + j, (*chip, 1 - c), me).wait_recv()
        for cp in first + passed:
            cp.wait_send()
        for cp in mine:
            cp.wait()

    return pl.pallas_call(
        body, name=name,
        out_shape=tuple(jax.ShapeDtypeStruct((N_DEV,) + s.shape, s.dtype) for s in shards),
        in_specs=[ANY] * n_arr, out_specs=tuple([ANY] * n_arr),
        scratch_shapes=[pltpu.SemaphoreType.DMA((n_arr, 7)), pltpu.SemaphoreType.DMA((n_arr, 7)),
                        pltpu.SemaphoreType.DMA((n_arr,))],
    )(*shards)


def _sibling_exchange(pieces, name):
    n_arr = len(pieces)

    def body(*refs):
        p_refs, out_refs = refs[:n_arr], refs[n_arr:2 * n_arr]
        send_sems, recv_sems = refs[2 * n_arr:]
        x, y, c = _position()
        copies = [pltpu.make_async_remote_copy(
            src_ref=p_refs[a].at[1 - c], dst_ref=out_refs[a], send_sem=send_sems.at[a], recv_sem=recv_sems.at[a],
            device_id=(x, y, 1 - c), device_id_type=MESH) for a in range(n_arr)]
        for cp in copies:
            cp.start()
        for cp in copies:
            cp.wait()

    return pl.pallas_call(
        body, name=name,
        out_shape=tuple(jax.ShapeDtypeStruct(p.shape[1:], p.dtype) for p in pieces),
        in_specs=[ANY] * n_arr, out_specs=tuple([ANY] * n_arr),
        scratch_shapes=[pltpu.SemaphoreType.DMA((n_arr,)), pltpu.SemaphoreType.DMA((n_arr,))],
    )(*pieces)


def _chip_exchange(sums, name):
    n_arr = len(sums)

    def body(*refs):
        s_refs, out_refs = refs[:n_arr], refs[n_arr:2 * n_arr]
        send_sems, recv_sems, local_sems = refs[2 * n_arr:]
        x, y, c = _position()
        my_chip = 2 * x + y
        chips = [(1 - x, y), (x, 1 - y), (1 - x, 1 - y)]
        mine = [pltpu.make_async_copy(s_refs[a].at[my_chip], out_refs[a].at[my_chip], local_sems.at[a])
                for a in range(n_arr)]
        for cp in mine:
            cp.start()
        copies = []
        for j, (px, py) in enumerate(chips):
            copies += [pltpu.make_async_remote_copy(
                src_ref=s_refs[a].at[2 * px + py], dst_ref=out_refs[a].at[my_chip],
                send_sem=send_sems.at[a, j], recv_sem=recv_sems.at[a, j],
                device_id=(px, py, c), device_id_type=MESH) for a in range(n_arr)]
        for cp in copies:
            cp.start()
        for j, (px, py) in enumerate(chips):
            for a in range(n_arr):
                pltpu.make_async_remote_copy(
                    src_ref=s_refs[a].at[my_chip], dst_ref=out_refs[a].at[2 * px + py],
                    send_sem=send_sems.at[a, j], recv_sem=recv_sems.at[a, j],
                    device_id=(px, py, c), device_id_type=MESH).wait_recv()
        for cp in copies:
            cp.wait_send()
        for cp in mine:
            cp.wait()

    return pl.pallas_call(
        body, name=name,
        out_shape=tuple(jax.ShapeDtypeStruct(s.shape, s.dtype) for s in sums),
        in_specs=[ANY] * n_arr, out_specs=tuple([ANY] * n_arr),
        scratch_shapes=[pltpu.SemaphoreType.DMA((n_arr, 3)), pltpu.SemaphoreType.DMA((n_arr, 3)),
                        pltpu.SemaphoreType.DMA((n_arr,))],
    )(*sums)


HBM = pl.BlockSpec(memory_space=pltpu.HBM)
SEM = pl.BlockSpec(memory_space=pltpu.SEMAPHORE)
EFFECT = pltpu.SideEffectType.DATAFLOW_SIDE_EFFECTING
RELATIONS = [(rx, ry, rc) for rx in (0, 1) for ry in (0, 1) for rc in (0, 1)][1:]


SAME_CORE = [r for r in RELATIONS if r == (0, 0, 1) or r[2] == 0]


def _exchange_copies(src_refs, land_refs, send_sems, recv_sems, scatter, receive_side, relations):
    x, y, c = _position()
    me = 4 * x + 2 * y + c
    copies = []
    for k, (rx, ry, rc) in enumerate(relations):
        peer = ((1 - x) if rx else x, (1 - y) if ry else y, (1 - c) if rc else c)
        peer_index = 4 * peer[0] + 2 * peer[1] + peer[2]
        for a, (src, land) in enumerate(zip(src_refs, land_refs)):
            copies.append(pltpu.make_async_remote_copy(
                src_ref=src.at[peer_index] if scatter else src,
                dst_ref=land.at[peer_index if receive_side else me],
                send_sem=send_sems.at[a * len(relations) + k], recv_sem=recv_sems.at[a * len(relations) + k],
                device_id=peer, device_id_type=MESH))
    return copies


def _exchange_start(srcs, scatter, after, name, relations=RELATIONS):
    n = len(srcs)
    land_shapes = [(s.shape if scatter else (N_DEV,) + s.shape) for s in srcs]

    def body(*refs):
        src_refs, land_refs = refs[:n], refs[n:2 * n]
        send_sems, recv_sems = refs[2 * n + 1], refs[2 * n + 2]
        token = refs[-1]
        for cp in _exchange_copies(src_refs, land_refs, send_sems, recv_sems, scatter, False, relations):
            cp.start()
        token[...] = jnp.zeros_like(token)

    sems = pltpu.SemaphoreType.DMA((n * len(relations),))
    outs = pl.pallas_call(
        body, name=name,
        out_shape=(sems, sems, *[pltpu.HBM(s.shape, s.dtype) for s in srcs],
                   *[pltpu.HBM(shape, s.dtype) for shape, s in zip(land_shapes, srcs)],
                   jax.ShapeDtypeStruct((8, 128), F32)),
        in_specs=[HBM] * (2 * n) + [ANY],
        out_specs=(SEM, SEM, *[HBM] * (2 * n), pl.BlockSpec(memory_space=pltpu.VMEM)),
        input_output_aliases={a: 2 + a for a in range(2 * n)},
        compiler_params=pltpu.CompilerParams(has_side_effects=EFFECT),
    )(*[pltpu.with_memory_space_constraint(s, pltpu.HBM) for s in srcs],
      *[pltpu.with_memory_space_constraint(lax.empty(shape, s.dtype), pltpu.HBM)
        for shape, s in zip(land_shapes, srcs)], after)
    return outs[0], outs[1], outs[2:2 + n], outs[2 + n:2 + 2 * n], outs[-1]


def _exchange_wait(started, scatter, after, name, relations=RELATIONS):
    send_sems, recv_sems, srcs, lands, _ = started
    n = len(srcs)

    def body(*refs):
        src_refs, land_refs = refs[:n], refs[n:2 * n]
        send_sems, recv_sems = refs[2 * n], refs[2 * n + 1]
        copies = _exchange_copies(src_refs, land_refs, send_sems, recv_sems, scatter, True, relations)
        for cp in copies:
            cp.wait_send()
        for cp in copies:
            cp.wait_recv()

    outs = pl.pallas_call(
        body, name=name,
        out_shape=(*[pltpu.HBM(s.shape, s.dtype) for s in srcs], *[pltpu.HBM(t.shape, t.dtype) for t in lands]),
        in_specs=[HBM] * (2 * n) + [SEM, SEM, ANY], out_specs=tuple([HBM] * (2 * n)),
        input_output_aliases={a: a for a in range(2 * n)},
        compiler_params=pltpu.CompilerParams(has_side_effects=EFFECT),
    )(*srcs, *lands, send_sems, recv_sems, after)
    return outs[:n], outs[n:]


def _sibling_forward(lands, name):
    n = len(lands)

    def body(*refs):
        in_refs, out_refs = refs[:n], refs[n:2 * n]
        send_sems, recv_sems = refs[2 * n:]
        x, y, c = _position()
        copies, arrivals = [], []
        for j, (px, py) in enumerate([(1 - x, y), (x, 1 - y), (1 - x, 1 - y)]):
            held, coming = 4 * px + 2 * py + c, 4 * px + 2 * py + (1 - c)
            for a in range(n):
                sems = dict(send_sem=send_sems.at[a, j], recv_sem=recv_sems.at[a, j], device_id=(x, y, 1 - c),
                            device_id_type=MESH)
                copies.append(pltpu.make_async_remote_copy(
                    src_ref=in_refs[a].at[held], dst_ref=out_refs[a].at[held], **sems))
                arrivals.append(pltpu.make_async_remote_copy(
                    src_ref=in_refs[a].at[held], dst_ref=out_refs[a].at[coming], **sems))
        for cp in copies:
            cp.start()
        for cp in copies:
            cp.wait_send()
        for cp in arrivals:
            cp.wait_recv()

    return pl.pallas_call(
        body, name=name, out_shape=tuple(jax.ShapeDtypeStruct(t.shape, t.dtype) for t in lands),
        in_specs=[ANY] * n, out_specs=tuple([ANY] * n), input_output_aliases={a: a for a in range(n)},
        scratch_shapes=[pltpu.SemaphoreType.DMA((n, 3)), pltpu.SemaphoreType.DMA((n, 3))],
    )(*lands)


W_IN_SHARD = N_IN // N_DEV
F_SHARD = F_COL // W_IN_SHARD
F_LO = F_COL - F_SHARD * W_IN_SHARD


def _w_ffn_in_view(w):
    return jnp.transpose(w, (0, 2, 1))


def _w_in_rearranged(g):
    parts = [g[d] for d in range(N_DEV)]
    with_f = parts[F_SHARD]
    parts[F_SHARD:F_SHARD + 1] = [with_f[:, :F_LO], with_f[:, F_LO + N_FORGET:]]
    parts += [with_f[:, F_LO:F_LO + N_FORGET], jnp.zeros((with_f.shape[0], BLK - N_FORGET), with_f.dtype)]
    return jnp.concatenate(parts, axis=1)


def _w_in_pieces(dw_r):
    def original(lo, hi):
        shift = 0 if hi <= F_COL else N_FORGET
        return dw_r[:, lo - shift:hi - shift]

    pieces = []
    for d in range(N_DEV):
        lo, hi = d * W_IN_SHARD, (d + 1) * W_IN_SHARD
        if d == F_SHARD:
            pieces.append(jnp.concatenate([original(lo, F_COL), dw_r[:, N_MAIN:N_MAIN + N_FORGET],
                                           original(F_COL + N_FORGET, hi)], axis=1))
        else:
            pieces.append(original(lo, hi))
    return jnp.stack(pieces)


def _row_pieces(dw):
    return dw.reshape(N_DEV, dw.shape[0] // N_DEV, dw.shape[1])


def _branch_pieces(dw):
    k, w, d = dw.shape
    return jnp.transpose(dw.reshape(k, w, N_DEV, d // N_DEV), (2, 0, 1, 3)).reshape(N_DEV, k * w, d // N_DEV)


def _pair_major(p8):
    return jnp.stack([p8[0::2], p8[1::2]])


def _pairs_col(a):
    return jnp.transpose(a.reshape(a.shape[0], 4, 2), (1, 0, 2))


def _pairs_row(a):
    return jnp.transpose(a.reshape(a.shape[0], 4, 2), (1, 2, 0))


def _heads_from_col(a):
    return jnp.transpose(a, (1, 0, 2)).reshape(a.shape[1], 8)


def _heads_from_row(a):
    return jnp.transpose(a, (2, 0, 1)).reshape(a.shape[2], 8)


def _pad_lanes(a, n):
    return jnp.pad(a, [(0, 0)] * (a.ndim - 1) + [(0, n - a.shape[-1])])


SMALL_SEGMENTS = (("dmod", 2 * 6 * D_MODEL), ("norm_mix_g", 2 * D_MODEL), ("norm_ffn_g", 2 * D_MODEL),
                  ("final_norm_g", D_MODEL), ("b_forget", 128), ("sinks", 128), ("rel_bias", 4224))
SMALL_ROWS = 176


def _pack_small(parts):
    flat = [_pad_lanes(parts[name].reshape(1, -1), size) for name, size in SMALL_SEGMENTS]
    total = sum(size for _, size in SMALL_SEGMENTS)
    flat.append(jnp.zeros((1, SMALL_ROWS * 128 - total), F32))
    return jnp.concatenate(flat, axis=1).reshape(SMALL_ROWS, 128)


def _unpack_small(packed, shapes):
    flat = packed.reshape(-1)
    out, pos = {}, 0
    for name, size in SMALL_SEGMENTS:
        shape = shapes[name]
        count = 1
        for d in shape:
            count *= d
        out[name] = flat[pos:pos + count].reshape(shape)
        pos += size
    return out


def kernel(x, c, norm_mix_g, norm_ffn_g, w_ada, b_ada, w_in, b_forget, sinks, rel_bias, w_branch, w_out, w_ffn_in, w_ffn_out, final_norm_g, loss_target, m_norm_mix_g, m_norm_ffn_g, m_w_ada, m_b_ada, m_w_in, m_b_forget, m_sinks, m_rel_bias, m_w_branch, m_w_out, m_w_ffn_in, m_w_ffn_out, m_final_norm_g, v_norm_mix_g, v_norm_ffn_g, v_w_ada, v_b_ada, v_w_in, v_b_forget, v_sinks, v_rel_bias, v_w_branch, v_w_out, v_w_ffn_in, v_w_ffn_out, v_final_norm_g):
    depth = w_in.shape[0]
    S, D = x.shape[1], x.shape[2]
    assert S % GROUP == 0 and S >= ATTN_WINDOW["c"] * BLK
    px, py, pc = _position()
    me = 4 * px + 2 * py + pc
    x0 = x[0]

    assert depth == 2
    big_weights = (w_in, w_branch, w_out, w_ffn_in, w_ffn_out)
    me_arr = me.astype(jnp.int32).reshape(1)

    def slabs(landed, mine):
        return [jnp.where(me == d, mine, landed[d]) for d in range(N_DEV)]

    def rest_matrices(g_branch, g_out, g_fin, g_fout):
        return (jnp.transpose(jnp.stack(g_branch), (1, 2, 0, 3)).reshape(3, 512, D),
                jnp.concatenate(g_out, axis=0), jnp.concatenate(g_fin, axis=0), jnp.concatenate(g_fout, axis=0))

    def finish_gather(started, after, name):
        mine, landed = _exchange_wait(started, False, after, f"{name}_wait", SAME_CORE)
        landed = _sibling_forward(landed, f"{name}_forward")
        return [slabs(t, s) for t, s in zip(landed, mine)]

    w_fin_t = _w_ffn_in_view(w_ffn_in)
    shards = [[t.astype(BF16) for t in (w_in[l], w_branch[l], w_out[l], w_fin_t[l], w_ffn_out[l])]
              for l in range(depth)]
    gathered_in0 = _big_all_gather(shards[0][:1], "comm_gather_w_in0")[0]
    gather_rest0 = _exchange_start(shards[0][1:], False, gathered_in0, "comm_gather_rest0_start", SAME_CORE)
    gather1 = _exchange_start(shards[1], False, gather_rest0[4], "comm_gather_weights1_start", SAME_CORE)
    W_in, W_branch, W_out, W_fin, W_fout = ([None, None] for _ in range(5))
    W_in[0] = _w_in_rearranged(gathered_in0)

    c_all = _small_all_gather(c.reshape(8, 128), "comm_gather_c").reshape(N_DEV, D)
    mod_cols = _ada_fwd(c_all, w_ada, "ada_fwd")
    mod_all = _small_all_gather(mod_cols.reshape(-1, 128), "comm_gather_mod")
    mod_all = mod_all.reshape(N_DEV, depth, N_DEV, w_ada.shape[2])
    mod_mine = lax.dynamic_index_in_dim(mod_all, me, axis=2, keepdims=False)
    mod = jnp.transpose(mod_mine, (1, 0, 2)).reshape(depth, 6 * D) + b_ada + gather1[4][0:1, 0:1]
    mods = [[mod[l:l + 1, k * D:(k + 1) * D] for k in range(6)] for l in range(depth)]

    slopes = jnp.exp2(-jnp.arange(1, 9, dtype=F32))
    saved = []
    xs = x0
    for l in range(depth):
        if l == 1:
            g_in1, *g_rest1 = finish_gather(gather1, xs, "comm_gather_weights1")
            W_in[1] = _w_in_rearranged(g_in1)
            W_branch[1], W_out[1], W_fin[1], W_fout[1] = rest_matrices(*g_rest1)
        sh_m, sc_m, g_m, sh_f, sc_f, g_f = mods[l]
        gm, gf = norm_mix_g[l:l + 1], norm_ffn_g[l:l + 1]
        bfor = _pad_lanes(b_forget[l:l + 1], BLK)
        h = _norm_mod_fwd(xs, gm, sh_m, sc_m, f"norm_mix_fwd{l}")
        qkv = _matmul(h, W_in[l], "nn", BF16, f"proj_qkv{l}", TILES["proj_qkv"], n=N_QKV)
        gates = _matmul(h, W_in[l], "nn", F32, f"proj_gates{l}", TILES["proj_gates"], n=N_GATES,
                        b_off=N_QKV // TILES["proj_gates"][1])
        fb = _matmul(h, W_in[l], "nn", F32, f"proj_forget{l}", TILES["proj_forget"], n=BLK, b_off=N_MAIN // BLK)
        cum = _forget_fwd(fb, bfor, f"forget_fwd{l}")[:, :N_FORGET]
        cum_col, cum_row = _pairs_col(cum), _pairs_row(cum)
        tiles, tiles_t = _rel_expand(_rel_bases(rel_bias[l]), f"rel_expand{l}")
        o_a, lse_a = _attn_fwd("a", qkv, f"attn_a_fwd{l}", sinks=sinks[l], slopes=slopes)
        o_b, lse_b = _attn_fwd("b", qkv, f"attn_b_fwd{l}", cq_col=cum_col, ck_row=cum_row)
        o_c, lse_c = _attn_fwd("c", qkv, f"attn_c_fwd{l}", bias=tiles)
        if l == 0:
            W_branch[0], W_out[0], W_fin[0], W_fout[0] = rest_matrices(*finish_gather(gather_rest0, o_c, "comm_gather_rest0"))
        merged = _merge_fwd(o_a, o_b, o_c, gates, W_branch[l], f"merge_fwd{l}")
        x1, mix = _matmul_resid(merged, W_out[l], xs, g_m, f"out_proj{l}", TILES["out_proj"])
        h2 = _norm_mod_fwd(x1, gf, sh_f, sc_f, f"norm_ffn_fwd{l}")
        act = _ffn_in_fwd(h2, W_fin[l], f"ffn_in_fwd{l}")
        x2, ffn = _matmul_resid(act, W_fout[l], x1, g_f, f"ffn_out{l}", TILES["ffn_out"])
        saved.append(dict(x=xs, h=h, qkv=qkv, gates=gates, fb=fb, bfor=bfor, cum_col=cum_col, cum_row=cum_row,
                          tiles_t=tiles_t, o=(o_a, o_b, o_c), lse=(lse_a, lse_b, lse_c), merged=merged, mix=mix,
                          x1=x1, h2=h2, act=act, ffn=ffn))
        xs = x2

    dx, loss_tile, d_final_g = _final_loss(xs, loss_target[0], final_norm_g.reshape(1, D), "final_loss")
    loss = lax.psum(loss_tile[0, 0], ("x", "y", "c"))

    grads = {k: [None] * depth for k in ("w_in", "w_branch", "w_out", "w_ffn_in", "w_ffn_out", "norm_mix_g",
                                          "norm_ffn_g", "b_forget", "sinks", "rel_bias", "dmod")}
    def rest_pieces(l):
        return [_branch_pieces(grads["w_branch"][l]), _row_pieces(grads["w_out"][l]),
                _row_pieces(grads["w_ffn_in"][l]), _row_pieces(grads["w_ffn_out"][l])]

    reduce1 = reduce_rest0 = reduce_in0 = None
    for l in reversed(range(depth)):
        sv = saved[l]
        sh_m, sc_m, g_m, sh_f, sc_f, g_f = mods[l]
        if l == 0:
            g_f = g_f + reduce1[4][0:1, 0:1]
        gm, gf = norm_mix_g[l:l + 1], norm_ffn_g[l:l + 1]
        df, d_g_f = _gate_bwd(dx, sv["ffn"], g_f, f"ffn_gate_bwd{l}")
        du_g, du_u = _ffn_mid_bwd(sv["h2"], df, W_fin[l], W_fout[l], f"ffn_mid_bwd{l}")
        du = jnp.concatenate([du_g, du_u], axis=1)
        grads["w_ffn_out"][l] = _matmul(sv["act"], df, "tn", BF16, f"wgrad_ffn_out{l}", TILES["wgrad_ffn_out"])
        grads["w_ffn_in"][l] = _matmul(du, sv["h2"], "tn", BF16, f"wgrad_ffn_in{l}", TILES["wgrad_ffn_in"])
        dh2 = _matmul(du, W_fin[l], "nn", F32, f"dgrad_ffn_in{l}", TILES["dgrad_ffn_in"])
        dx1, d_sh_f, d_sc_f, d_gf = _norm_mod_bwd(sv["x1"], dh2, dx, gf, sc_f, f"norm_ffn_bwd{l}")
        dmix, d_g_m = _gate_bwd(dx1, sv["mix"], g_m, f"mix_gate_bwd{l}")
        grads["w_out"][l] = _matmul(sv["merged"], dmix, "tn", BF16, f"wgrad_out{l}", TILES["wgrad_out"])
        dmerged = _matmul(dmix, W_out[l], "nt", F32, f"dgrad_out{l}", TILES["dgrad_out"])
        o_a, o_b, o_c = sv["o"]
        dgates, dy, do_a, do_b, do_c, dl_a, dl_b, dl_c = _merge_bwd(
            dmerged, o_a, o_b, o_c, sv["gates"], W_branch[l], f"merge_bwd{l}")
        dwb = [_matmul(o_k, dy, "tn", BF16, f"wgrad_branch{l}_{k}", TILES["wgrad_branch"], n=D,
                       b_off=k * (D // TILES["wgrad_branch"][1])) for k, o_k in enumerate((o_a, o_b, o_c))]
        grads["w_branch"][l] = jnp.stack(dwb)
        lse_rows = [_pairs_row(_heads_from_col(t)) for t in sv["lse"]]
        if l == 0:
            reduce_rest0 = _exchange_start(rest_pieces(0), True, dy, "comm_reduce_rest0_start")
            lse_rows = [t + reduce_rest0[4][0:1, 0:1] for t in lse_rows]
        dqt_a, dk_a, dv_a, dsink = _attn_bwd("a", sv["qkv"], do_a, lse_rows[0], _pairs_row(dl_a), f"attn_a_bwd{l}",
                                             sinks=sinks[l], slopes=slopes)
        dqt_b, dk_b, dv_b, dck, dcq = _attn_bwd("b", sv["qkv"], do_b, lse_rows[1], _pairs_row(dl_b),
                                                f"attn_b_bwd{l}", cq_row=sv["cum_row"], ck_col=sv["cum_col"])
        dqt_c, dk_c, dv_c, dtiles_t = _attn_bwd("c", sv["qkv"], do_c, lse_rows[2], _pairs_row(dl_c),
                                                f"attn_c_bwd{l}", bias_t=sv["tiles_t"])
        grads["sinks"][l] = dsink[:, :2, 0].reshape(8)
        grads["rel_bias"][l] = _rel_reduce(dtiles_t, f"rel_reduce{l}")[:, 0, :N_REL]
        dcum_k = _pad_lanes(_heads_from_col(dck), BLK)
        dcum_q = _pad_lanes(_heads_from_row(dcq), BLK)
        dfb, d_bfor = _forget_bwd(dcum_q, dcum_k, sv["fb"], sv["bfor"], f"forget_bwd{l}")
        grads["b_forget"][l] = d_bfor[0, :N_FORGET]
        dproj = jnp.concatenate(
            [t.astype(BF16) for t in (dqt_a.T, dk_a, dv_a, dqt_b.T, dk_b, dv_b, dqt_c.T, dk_c, dv_c)]
            + [dgates, dfb.astype(BF16)], axis=1)
        grads["w_in"][l] = _matmul(sv["h"], dproj, "tn", BF16, f"wgrad_in{l}", TILES["wgrad_in"])
        dh = _matmul(dproj, W_in[l], "nt", F32, f"dgrad_in{l}", TILES["dgrad_in"])
        dx, d_sh_m, d_sc_m, d_gm = _norm_mod_bwd(sv["x"], dh, dx1, gm, sc_m, f"norm_mix_bwd{l}")
        grads["norm_mix_g"][l] = d_gm[0]
        grads["norm_ffn_g"][l] = d_gf[0]
        grads["dmod"][l] = jnp.concatenate([d_sh_m, d_sc_m, d_g_m, d_sh_f, d_sc_f, d_g_f], axis=1)[0]
        if l == 1:
            reduce1 = _exchange_start([_w_in_pieces(grads["w_in"][1])] + rest_pieces(1), True, dx, "comm_reduce1_start")

    grad_x = dx.reshape(x.shape)

    small_shapes = dict(dmod=b_ada.shape, norm_mix_g=norm_mix_g.shape, norm_ffn_g=norm_ffn_g.shape,
                        final_norm_g=final_norm_g.shape, b_forget=b_forget.shape, sinks=sinks.shape,
                        rel_bias=rel_bias.shape)
    mine_small = _pack_small(dict(
        dmod=jnp.stack(grads["dmod"]), norm_mix_g=jnp.stack(grads["norm_mix_g"]),
        norm_ffn_g=jnp.stack(grads["norm_ffn_g"]), final_norm_g=d_final_g[0],
        b_forget=_pad_lanes(jnp.stack(grads["b_forget"]).reshape(1, -1), 128),
        sinks=_pad_lanes(jnp.stack(grads["sinks"]).reshape(1, -1), 128),
        rel_bias=_pad_lanes(jnp.stack(grads["rel_bias"]).reshape(1, -1), 4224)))
    all_small = _small_all_gather(mine_small, "comm_gather_small").reshape(N_DEV, SMALL_ROWS, 128)
    reduce_in0 = _exchange_start([_w_in_pieces(grads["w_in"][0])], True, all_small, "comm_reduce_in0_start")
    me_arr = me_arr + reduce_in0[4][0, 0:1].astype(jnp.int32)

    def pack_params(b_ada_, nm, nf, fn, bf, sk, rb):
        return _pack_small(dict(dmod=b_ada_, norm_mix_g=nm, norm_ffn_g=nf, final_norm_g=fn,
                                b_forget=_pad_lanes(bf.reshape(1, -1), 128), sinks=_pad_lanes(sk.reshape(1, -1), 128),
                                rel_bias=_pad_lanes(rb.reshape(1, -1), 4224)))

    small_out = _adamw(
        pack_params(b_ada, norm_mix_g, norm_ffn_g, final_norm_g, b_forget, sinks, rel_bias)[None],
        pack_params(m_b_ada, m_norm_mix_g, m_norm_ffn_g, m_final_norm_g, m_b_forget, m_sinks, m_rel_bias)[None],
        pack_params(v_b_ada, v_norm_mix_g, v_norm_ffn_g, v_final_norm_g, v_b_forget, v_sinks, v_rel_bias)[None],
        [all_small], "adamw_small", me_arr)
    small_out = [_unpack_small(t[0], small_shapes) for t in small_out]

    dmod_all = all_small[:, :96].reshape(N_DEV, depth, 6 * D)
    dmod_cols = lax.dynamic_slice_in_dim(dmod_all, me * w_ada.shape[2], w_ada.shape[2], axis=2)
    d_w_ada = _ada_bwd(jnp.transpose(c_all), jnp.transpose(dmod_cols, (1, 0, 2)), "ada_bwd")

    big = {"w_ada": _adamw(w_ada, m_w_ada, v_w_ada, [d_w_ada[l:l + 1] for l in range(depth)], "adamw_w_ada", me_arr)}
    sent1, landed1 = _exchange_wait(reduce1, True, big["w_ada"][0], "comm_reduce1_wait")
    sent_rest0, landed_rest0 = _exchange_wait(reduce_rest0, True, landed1[0], "comm_reduce_rest0_wait")
    parts = {"w_in": [None, (landed1[0], sent1[0])]}
    for a, name in enumerate(("w_branch", "w_out", "w_ffn_in", "w_ffn_out")):
        parts[name] = [(landed_rest0[a], sent_rest0[a]), (landed1[1 + a], sent1[1 + a])]

    def update(name, w, m, v, view=lambda t: t):
        per_layer = lambda t: t.reshape(depth, -1, t.shape[-1])
        outs = _adamw(*[per_layer(view(t)) for t in (w, m, v)], parts[name], f"adamw_{name}", me_arr)
        big[name] = [view(t).reshape(w.shape) for t in outs]

    update("w_ffn_in", w_ffn_in, m_w_ffn_in, v_w_ffn_in, _w_ffn_in_view)
    update("w_ffn_out", w_ffn_out, m_w_ffn_out, v_w_ffn_out)
    update("w_branch", w_branch, m_w_branch, v_w_branch)
    update("w_out", w_out, m_w_out, v_w_out)
    sent_in0, landed_in0 = _exchange_wait(reduce_in0, True, big["w_out"][0], "comm_reduce_in0_wait")
    parts["w_in"][0] = (landed_in0[0], sent_in0[0])
    update("w_in", w_in, m_w_in, v_w_in)

    def leaf(kind, name):
        if name in big:
            return big[name][kind]
        return small_out[kind]["dmod" if name == "b_ada" else name]

    order = ["norm_mix_g", "norm_ffn_g", "w_ada", "b_ada", "w_in", "b_forget", "sinks", "rel_bias", "w_branch",
             "w_out", "w_ffn_in", "w_ffn_out", "final_norm_g"]
    return (loss, grad_x, *[leaf(0, n) for n in order], *[leaf(1, n) for n in order],
            *[leaf(2, n) for n in order], *[leaf(3, n) for n in order])
```

```python
import functools

import jax
import jax.numpy as jnp
from jax import lax
from jax.experimental import pallas as pl
from jax.experimental.pallas import tpu as pltpu

F32 = jnp.float32
BF16 = jnp.bfloat16
NEG_INF = -1e30
EPS = 1e-6
N_DEV = 8
BLK = 128
GROUP = 4 * BLK
VMEM_LIMIT_BYTES = 56 * 1024 * 1024

D_MODEL = 1024
N_QKV = 3840
N_GATES = 3072
N_MAIN = N_QKV + N_GATES
N_FORGET = 8
N_IN = N_MAIN + N_FORGET
N_INR = N_MAIN + BLK
F_COL = 2304
FFN_HIDDEN = 2816
N_REL = 257

ADAM_LR, ADAM_B1, ADAM_B2, ADAM_EPS, ADAM_WD, ADAM_STEP = 0.001, 0.9, 0.999, 1e-08, 0.01, 10

NN = (((1,), (0,)), ((), ()))
NT = (((1,), (1,)), ((), ()))
TN = (((0,), (0,)), ((), ()))
HIGHEST = lax.Precision.HIGHEST

ATTN_COLS = {"a": (0, 4, 5), "b": (6, 10, 14), "c": (18, 22, 26)}
ATTN_WINDOW = {"a": 2, "c": 5}
ATTN_BLOCKS_PER_STEP = {"a": 4, "b": 1, "c": 2}


def _params():
    return pltpu.CompilerParams(vmem_limit_bytes=VMEM_LIMIT_BYTES)


def _tile(n, target):
    best = None
    t = 128
    while t <= min(n, target):
        if n % t == 0:
            best = t
        t += 128
    return best if best is not None else n


def _row_tile(n, target):
    t = min(n, target)
    while n % t:
        t -= 8
    return t


TILES = {
    "proj_qkv": (1024, 1280, 1024), "proj_gates": (1024, 768, 1024), "proj_forget": (1024, 128, 1024),
    "out_proj": (1024, 512, 1024), "ffn_out": (1024, 512, 1408), "ffn_fused": (512, 1408),
    "wgrad_ffn_out": (1408, 1024, 1024), "wgrad_ffn_in": (1408, 1024, 1024), "dgrad_ffn_in": (1024, 1024, 1408),
    "wgrad_out": (1024, 1024, 1024), "dgrad_out": (1024, 1024, 1024), "wgrad_branch": (512, 1024, 1024),
    "wgrad_in": (1024, 1408, 1024), "dgrad_in": (1024, 1024, 1408),
}


def _matmul(a, b, mode, out_dtype, name, tiles, *, n=None, a_off=0, b_off=0, m=None, after=None):
    tm, tn, tk = tiles
    if mode == "nn":
        M, K = a.shape if m is None else (m, a.shape[1])
        N = b.shape[1] if n is None else n
    elif mode == "nt":
        M, K = a.shape
        N = b.shape[0] if n is None else n
    else:
        K = a.shape[0]
        M = a.shape[1] if m is None else m
        N = b.shape[1] if n is None else n
    tm = _tile(M, tm) if M % 128 == 0 else M
    tn = _tile(N, tn)
    tk = _tile(K, tk)
    nk = K // tk
    dims = {"nn": NN, "nt": NT, "tn": TN}[mode]
    if mode == "nn":
        a_spec = pl.BlockSpec((tm, tk), lambda i, j, k: (i + a_off, k))
        b_spec = pl.BlockSpec((tk, tn), lambda i, j, k: (k, j + b_off))
    elif mode == "nt":
        a_spec = pl.BlockSpec((tm, tk), lambda i, j, k: (i + a_off, k))
        b_spec = pl.BlockSpec((tn, tk), lambda i, j, k: (j + b_off, k))
    else:
        a_spec = pl.BlockSpec((tk, tm), lambda i, j, k: (k, i + a_off))
        b_spec = pl.BlockSpec((tk, tn), lambda i, j, k: (k, j + b_off))

    def body(a_ref, b_ref, *rest):
        o_ref, acc_ref = rest[-2:]
        k = pl.program_id(2)
        part = lax.dot_general(a_ref[...], b_ref[...], dims, preferred_element_type=F32)
        if nk == 1:
            o_ref[...] = part.astype(o_ref.dtype)
        else:
            @pl.when(k == 0)
            def _():
                acc_ref[...] = part

            @pl.when(k > 0)
            def _():
                acc_ref[...] += part

            @pl.when(k == nk - 1)
            def _():
                o_ref[...] = acc_ref[...].astype(o_ref.dtype)

    return pl.pallas_call(
        body, name=name,
        out_shape=jax.ShapeDtypeStruct((M, N), out_dtype),
        grid=(M // tm, N // tn, nk),
        in_specs=[a_spec, b_spec] + ([ANY] if after is not None else []),
        out_specs=pl.BlockSpec((tm, tn), lambda i, j, k: (i, j)),
        scratch_shapes=[pltpu.VMEM((tm, tn) if nk > 1 else (8, 128), F32)],
        compiler_params=_params(),
    )(a, b, *([after] if after is not None else []))


def _matmul_resid(a, b, resid, gate, name, tiles):
    M, K = a.shape
    N = b.shape[1]
    tm, tn, tk = (_tile(d, t) for d, t in zip((M, N, K), tiles))
    nk = K // tk

    def body(a_ref, b_ref, r_ref, g_ref, o_ref, s_ref, acc_ref):
        k = pl.program_id(2)
        part = jnp.dot(a_ref[...], b_ref[...], preferred_element_type=F32)

        def finish(acc):
            o_ref[...] = r_ref[...] + g_ref[...] * acc
            s_ref[...] = acc.astype(BF16)

        if nk == 1:
            finish(part)
        else:
            @pl.when(k == 0)
            def _():
                acc_ref[...] = part

            @pl.when(k > 0)
            def _():
                acc_ref[...] += part

            @pl.when(k == nk - 1)
            def _():
                finish(acc_ref[...])

    return pl.pallas_call(
        body, name=name,
        out_shape=(jax.ShapeDtypeStruct((M, N), F32), jax.ShapeDtypeStruct((M, N), BF16)),
        grid=(M // tm, N // tn, nk),
        in_specs=[pl.BlockSpec((tm, tk), lambda i, j, k: (i, k)),
                  pl.BlockSpec((tk, tn), lambda i, j, k: (k, j)),
                  pl.BlockSpec((tm, tn), lambda i, j, k: (i, j)),
                  pl.BlockSpec((1, tn), lambda i, j, k: (0, j))],
        out_specs=(pl.BlockSpec((tm, tn), lambda i, j, k: (i, j)),
                   pl.BlockSpec((tm, tn), lambda i, j, k: (i, j))),
        scratch_shapes=[pltpu.VMEM((tm, tn) if nk > 1 else (8, 128), F32)],
        compiler_params=_params(),
    )(a, b, resid, gate)


def _norm_mod_fwd(x, g, shift, scale, name):
    S, D = x.shape
    ts = _row_tile(S, 256)

    def body(x_ref, g_ref, sh_ref, sc_ref, h_ref):
        xv = x_ref[...]
        rstd = lax.rsqrt(jnp.mean(xv * xv, axis=-1, keepdims=True) + EPS)
        y = xv * rstd * g_ref[...]
        h_ref[...] = (y * (1.0 + sc_ref[...]) + sh_ref[...]).astype(BF16)

    row = pl.BlockSpec((1, D), lambda i: (0, 0))
    return pl.pallas_call(
        body, name=name, out_shape=jax.ShapeDtypeStruct((S, D), BF16), grid=(S // ts,),
        in_specs=[pl.BlockSpec((ts, D), lambda i: (i, 0)), row, row, row],
        out_specs=pl.BlockSpec((ts, D), lambda i: (i, 0)),
        compiler_params=_params(),
    )(x, g, shift, scale)


def _norm_mod_bwd(x, dh, dres, g, scale, name):
    S, D = x.shape
    ts = _row_tile(S, 256)

    def body(x_ref, dh_ref, dr_ref, g_ref, sc_ref, dx_ref, dsh_ref, dsc_ref, dg_ref):
        i = pl.program_id(0)
        xv, dhv, gv = x_ref[...], dh_ref[...], g_ref[...]
        rstd = lax.rsqrt(jnp.mean(xv * xv, axis=-1, keepdims=True) + EPS)
        xhat = xv * rstd
        dn = dhv * (1.0 + sc_ref[...])
        dxhat = dn * gv
        proj = jnp.mean(dxhat * xhat, axis=-1, keepdims=True)
        dx_ref[...] = dr_ref[...] + rstd * (dxhat - xhat * proj)
        dsh = jnp.sum(dhv, axis=0, keepdims=True)
        dsc = jnp.sum(dhv * (xhat * gv), axis=0, keepdims=True)
        dg = jnp.sum(dn * xhat, axis=0, keepdims=True)

        @pl.when(i == 0)
        def _():
            dsh_ref[...] = dsh
            dsc_ref[...] = dsc
            dg_ref[...] = dg

        @pl.when(i > 0)
        def _():
            dsh_ref[...] += dsh
            dsc_ref[...] += dsc
            dg_ref[...] += dg

    tile = pl.BlockSpec((ts, D), lambda i: (i, 0))
    row = pl.BlockSpec((1, D), lambda i: (0, 0))
    vec = jax.ShapeDtypeStruct((1, D), F32)
    return pl.pallas_call(
        body, name=name, out_shape=(jax.ShapeDtypeStruct((S, D), F32), vec, vec, vec), grid=(S // ts,),
        in_specs=[tile, tile, tile, row, row], out_specs=(tile, row, row, row),
        compiler_params=_params(),
    )(x, dh, dres, g, scale)


def _gate_bwd(dx, f, gate, name):
    S, D = dx.shape
    ts = _row_tile(S, 256)

    def body(dx_ref, f_ref, g_ref, df_ref, dg_ref):
        i = pl.program_id(0)
        dxv = dx_ref[...]
        df_ref[...] = (dxv * g_ref[...]).astype(BF16)
        dg = jnp.sum(dxv * f_ref[...].astype(F32), axis=0, keepdims=True)

        @pl.when(i == 0)
        def _():
            dg_ref[...] = dg

        @pl.when(i > 0)
        def _():
            dg_ref[...] += dg

    tile = pl.BlockSpec((ts, D), lambda i: (i, 0))
    row = pl.BlockSpec((1, D), lambda i: (0, 0))
    return pl.pallas_call(
        body, name=name,
        out_shape=(jax.ShapeDtypeStruct((S, D), BF16), jax.ShapeDtypeStruct((1, D), F32)), grid=(S // ts,),
        in_specs=[tile, tile, row], out_specs=(tile, row),
        compiler_params=_params(),
    )(dx, f, gate)


def _ffn_in_fwd(h, w_t, name):
    S, D = h.shape
    F = w_t.shape[0] // 2
    tm, tn = _tile(S, TILES["ffn_fused"][0]), _tile(F, TILES["ffn_fused"][1])
    nj = F // tn

    def body(h_ref, wg_ref, wu_ref, o_ref):
        hv = h_ref[...]
        ug = lax.dot_general(hv, wg_ref[...], NT, preferred_element_type=F32)
        uu = lax.dot_general(hv, wu_ref[...], NT, preferred_element_type=F32)
        o_ref[...] = (ug * jax.nn.sigmoid(ug) * uu).astype(BF16)

    return pl.pallas_call(
        body, name=name, out_shape=jax.ShapeDtypeStruct((S, F), BF16), grid=(nj, S // tm),
        in_specs=[pl.BlockSpec((tm, D), lambda j, i: (i, 0)),
                  pl.BlockSpec((tn, D), lambda j, i: (j, 0)),
                  pl.BlockSpec((tn, D), lambda j, i: (j + nj, 0))],
        out_specs=pl.BlockSpec((tm, tn), lambda j, i: (i, j)),
        compiler_params=_params(),
    )(h, w_t, w_t)


def _ffn_mid_bwd(h, df, w_in_t, w_out, name):
    S, D = h.shape
    F = w_in_t.shape[0] // 2
    tm, tn = _tile(S, TILES["ffn_fused"][0]), _tile(F, TILES["ffn_fused"][1])
    nj = F // tn

    def body(h_ref, df_ref, wg_ref, wu_ref, wo_ref, dg_ref, du_ref):
        hv = h_ref[...]
        ug = lax.dot_general(hv, wg_ref[...], NT, preferred_element_type=F32)
        uu = lax.dot_general(hv, wu_ref[...], NT, preferred_element_type=F32)
        dact = lax.dot_general(df_ref[...], wo_ref[...], NT, preferred_element_type=F32)
        sig = jax.nn.sigmoid(ug)
        dg_ref[...] = (dact * uu * (sig * (1.0 + ug * (1.0 - sig)))).astype(BF16)
        du_ref[...] = (dact * (ug * sig)).astype(BF16)

    out = jax.ShapeDtypeStruct((S, F), BF16)
    return pl.pallas_call(
        body, name=name, out_shape=(out, out), grid=(nj, S // tm),
        in_specs=[pl.BlockSpec((tm, D), lambda j, i: (i, 0)),
                  pl.BlockSpec((tm, D), lambda j, i: (i, 0)),
                  pl.BlockSpec((tn, D), lambda j, i: (j, 0)),
                  pl.BlockSpec((tn, D), lambda j, i: (j + nj, 0)),
                  pl.BlockSpec((tn, D), lambda j, i: (j, 0))],
        out_specs=(pl.BlockSpec((tm, tn), lambda j, i: (i, j)), pl.BlockSpec((tm, tn), lambda j, i: (i, j))),
        compiler_params=_params(),
    )(h, df, w_in_t, w_in_t, w_out)


def _merge_fwd(o_a, o_b, o_c, gates, w_branch, name, *, tm=256):
    S, W = o_a.shape
    D = w_branch.shape[2]
    tm = _row_tile(S, tm)

    def body(oa_ref, ob_ref, oc_ref, g_ref, w_ref, m_ref):
        acc = None
        for k, o_ref in enumerate((oa_ref, ob_ref, oc_ref)):
            y = jnp.dot(o_ref[...], w_ref[k], preferred_element_type=F32)
            t = jax.nn.sigmoid(g_ref[:, k * D:(k + 1) * D]) * y
            acc = t if acc is None else acc + t
        m_ref[...] = acc.astype(BF16)

    o_spec = pl.BlockSpec((tm, W), lambda i: (i, 0))
    return pl.pallas_call(
        body, name=name, out_shape=jax.ShapeDtypeStruct((S, D), BF16), grid=(S // tm,),
        in_specs=[o_spec, o_spec, o_spec, pl.BlockSpec((tm, 3 * D), lambda i: (i, 0)),
                  pl.BlockSpec((3, W, D), lambda i: (0, 0, 0))],
        out_specs=pl.BlockSpec((tm, D), lambda i: (i, 0)),
        compiler_params=_params(),
    )(o_a, o_b, o_c, gates, w_branch)


def _merge_bwd(dmerged, o_a, o_b, o_c, gates, w_branch, name, *, tm=256):
    S, W = o_a.shape
    D = w_branch.shape[2]
    tm = _row_tile(S, tm)
    n_heads = W // 64

    def body(dm_ref, oa_ref, ob_ref, oc_ref, g_ref, w_ref, dg_ref, dy_ref,
             doa_ref, dob_ref, doc_ref, dla_ref, dlb_ref, dlc_ref):
        dm = dm_ref[...]
        branches = ((oa_ref, doa_ref, dla_ref), (ob_ref, dob_ref, dlb_ref), (oc_ref, doc_ref, dlc_ref))
        for k, (o_ref, do_ref, dl_ref) in enumerate(branches):
            wk = w_ref[k]
            ov = o_ref[...]
            y = jnp.dot(ov, wk, preferred_element_type=F32)
            g = jax.nn.sigmoid(g_ref[:, k * D:(k + 1) * D])
            dy = (dm * g).astype(BF16)
            dy_ref[:, k * D:(k + 1) * D] = dy
            dg_ref[:, k * D:(k + 1) * D] = (dm * y * (g * (1.0 - g))).astype(BF16)
            do16 = lax.dot_general(dy, wk, NT, preferred_element_type=F32).astype(BF16)
            do_ref[...] = do16
            prod = do16.astype(F32) * ov.astype(F32)
            for h in range(n_heads):
                dl_ref[:, h:h + 1] = jnp.sum(prod[:, 64 * h:64 * (h + 1)], axis=1, keepdims=True)

    o_spec = pl.BlockSpec((tm, W), lambda i: (i, 0))
    wide = pl.BlockSpec((tm, 3 * D), lambda i: (i, 0))
    dl_spec = pl.BlockSpec((tm, n_heads), lambda i: (i, 0))
    o_out = jax.ShapeDtypeStruct((S, W), BF16)
    wide_out = jax.ShapeDtypeStruct((S, 3 * D), BF16)
    dl_out = jax.ShapeDtypeStruct((S, n_heads), F32)
    return pl.pallas_call(
        body, name=name, out_shape=(wide_out, wide_out, o_out, o_out, o_out, dl_out, dl_out, dl_out),
        grid=(S // tm,),
        in_specs=[pl.BlockSpec((tm, D), lambda i: (i, 0)), o_spec, o_spec, o_spec, wide,
                  pl.BlockSpec((3, W, D), lambda i: (0, 0, 0))],
        out_specs=(wide, wide, o_spec, o_spec, o_spec, dl_spec, dl_spec, dl_spec),
        compiler_params=_params(),
    )(dmerged, o_a, o_b, o_c, gates, w_branch)


def _band_mask(variant, t_abs, s_abs):
    if variant == "b":
        return s_abs <= t_abs
    qc, kc = t_abs >> 6, s_abs >> 6
    return (kc <= qc) & (kc >= qc - (2 if variant == "a" else 8))


def _attn_fwd(variant, qkv, name, *, sinks=None, slopes=None, cq_col=None, ck_row=None, bias=None):
    S = qkv.shape[0]
    nb = S // BLK
    qb, kb, vb = ATTN_COLS[variant]
    shared_kv = variant == "a"
    win = ATTN_WINDOW.get(variant)
    per_step = ATTN_BLOCKS_PER_STEP[variant]

    def body(*refs):
        if variant == "a":
            q_ref, k_ref, v_ref, sink_ref, slope_ref, o_ref, lse_ref = refs
        elif variant == "b":
            q_ref, k_ref, v_ref, cq_ref, ck_ref, o_ref, lse_ref = refs
        else:
            q_ref, k_ref, v_ref, bias_ref, o_ref, lse_ref = refs
        p = pl.program_id(0)
        lane = lax.broadcasted_iota(jnp.int32, (BLK, BLK), 1)

        def compute(i, rows, start, n_keys):
            t_abs = i * BLK + lax.broadcasted_iota(jnp.int32, (BLK, 1), 0)
            q2 = q_ref[rows, :].astype(F32) * 0.125
            k_w = k_ref[pl.ds(start, n_keys), :]
            v_w = v_ref[pl.ds(start, n_keys), :]
            s_abs = start + lax.broadcasted_iota(jnp.int32, (1, n_keys), 1)
            valid = _band_mask(variant, t_abs, s_abs)
            outs = []
            for half in (0, 1):
                hmask = (lane >= 64) if half else (lane < 64)
                qh = jnp.where(hmask, q2, 0.0)
                if shared_kv:
                    swap = (p // 2) != half
                    qh = jnp.where(swap, pltpu.roll(qh, 64, 1), qh)
                s = lax.dot_general(qh.astype(BF16), k_w, NT, preferred_element_type=F32)
                if variant == "a":
                    head = 2 * p + half
                    s = s + (-slope_ref[head]) * jnp.abs(t_abs - s_abs).astype(F32)
                elif variant == "b":
                    s = s + cq_ref[rows, half:half + 1] - ck_ref[half:half + 1, pl.ds(start, n_keys)]
                else:
                    j0 = start // BLK
                    s = s + jnp.concatenate(
                        [bias_ref[half, jnp.clip(i - j0 - b, 0, 4)] for b in range(win)], axis=1)
                s = jnp.where(valid, s, NEG_INF)
                m = jnp.max(s, axis=1, keepdims=True)
                if variant == "a":
                    m = jnp.maximum(m, sink_ref[head])
                pe = jnp.exp(s - m)
                l = jnp.sum(pe, axis=1, keepdims=True)
                if variant == "a":
                    l = l + jnp.exp(sink_ref[head] - m)
                out = jnp.dot(pe.astype(BF16), v_w, preferred_element_type=F32) / l
                if shared_kv:
                    out = jnp.where(swap, pltpu.roll(out, 64, 1), out)
                outs.append(out)
                lse_ref[rows, half:half + 1] = m + jnp.log(l)
            o_ref[rows, :] = jnp.where(lane < 64, outs[0], outs[1]).astype(BF16)

        for sub in range(per_step):
            i = pl.program_id(1) * per_step + sub
            rows = slice(sub * BLK, (sub + 1) * BLK)
            if variant == "b":
                for g in range(S // GROUP):
                    pl.when(i // 4 == g)(functools.partial(compute, i, rows, 0, (g + 1) * GROUP))
            else:
                start = jnp.clip(i - (win - 1), 0, nb - win) * BLK
                compute(i, rows, pl.multiple_of(start, BLK), win * BLK)

    tq = per_step * BLK
    kv_col = (lambda p, i: (0, kb)) if shared_kv else (lambda p, i: (0, kb + p))
    vv_col = (lambda p, i: (0, vb)) if shared_kv else (lambda p, i: (0, vb + p))
    in_specs = [pl.BlockSpec((tq, BLK), lambda p, i: (i, qb + p)),
                pl.BlockSpec((S, BLK), kv_col), pl.BlockSpec((S, BLK), vv_col)]
    args = [qkv, qkv, qkv]
    if variant == "a":
        in_specs += [pl.BlockSpec(memory_space=pltpu.SMEM), pl.BlockSpec(memory_space=pltpu.SMEM)]
        args += [sinks, slopes]
    elif variant == "b":
        in_specs += [pl.BlockSpec((None, tq, 2), lambda p, i: (p, i, 0)),
                     pl.BlockSpec((None, 2, S), lambda p, i: (p, 0, 0))]
        args += [cq_col, ck_row]
    else:
        in_specs += [pl.BlockSpec((2, 5, BLK, BLK), lambda p, i: (p, 0, 0, 0))]
        args += [bias]
    return pl.pallas_call(
        body, name=name,
        out_shape=(jax.ShapeDtypeStruct((S, 512), BF16), jax.ShapeDtypeStruct((4, S, 2), F32)),
        grid=(4, nb // per_step), in_specs=in_specs,
        out_specs=(pl.BlockSpec((tq, BLK), lambda p, i: (i, p)),
                   pl.BlockSpec((None, tq, 2), lambda p, i: (p, i, 0))),
        compiler_params=_params(),
    )(*args)


def _attn_bwd(variant, qkv, do, lse_row, delta_row, name, *, sinks=None, slopes=None, cq_row=None,
              ck_col=None, bias_t=None):
    S = qkv.shape[0]
    nb = S // BLK
    qb, kb, vb = ATTN_COLS[variant]
    shared_kv = variant == "a"
    win = ATTN_WINDOW.get(variant)
    per_step = ATTN_BLOCKS_PER_STEP[variant]

    def body(*refs):
        if variant == "a":
            (q_ref, k_ref, v_ref, do_ref, lse_ref, dl_ref, sink_ref, slope_ref,
             dq_ref, dk_ref, dv_ref, ex_ref) = refs
        elif variant == "b":
            (q_ref, k_ref, v_ref, do_ref, lse_ref, dl_ref, cq_ref, ck_ref,
             dq_ref, dk_ref, dv_ref, ex_ref, dcq_ref) = refs
        else:
            (q_ref, k_ref, v_ref, do_ref, lse_ref, dl_ref, bias_ref,
             dq_ref, dk_ref, dv_ref, ex_ref) = refs
        p = pl.program_id(0)
        lane = lax.broadcasted_iota(jnp.int32, (BLK, BLK), 1)
        hmasks = [(lane < 64), (lane >= 64)]
        swaps = [(p // 2) != half for half in (0, 1)] if shared_kv else None

        @pl.when(pl.program_id(1) == 0)
        def _():
            dq_ref[...] = jnp.zeros_like(dq_ref)
            if variant == "b":
                dcq_ref[...] = jnp.zeros_like(dcq_ref)
            else:
                ex_ref[...] = jnp.zeros_like(ex_ref)

        def to_kv_lanes(x, h):
            x = jnp.where(hmasks[h], x, 0.0)
            if shared_kv:
                x = jnp.where(swaps[h], pltpu.roll(x, 64, 1), x)
            return x

        def compute(j, rows, start, n_q):
            s_abs = j * BLK + lax.broadcasted_iota(jnp.int32, (BLK, 1), 0)
            off_k = pl.multiple_of(j * BLK, BLK)
            k2 = k_ref[rows, :].astype(F32)
            v2 = v_ref[rows, :].astype(F32)
            if shared_kv:
                kv_lane = (lane >> 6) == (p // 2)
                k_src, v_src = jnp.where(kv_lane, k2, 0.0), jnp.where(kv_lane, v2, 0.0)
                k_al = [jnp.where(swaps[h], pltpu.roll(k_src, 64, 1), k_src) for h in (0, 1)]
                v_al = [jnp.where(swaps[h], pltpu.roll(v_src, 64, 1), v_src) for h in (0, 1)]
            else:
                k_al = [jnp.where(hmasks[h], k2, 0.0) for h in (0, 1)]
                v_al = [jnp.where(hmasks[h], v2, 0.0) for h in (0, 1)]
            k_al = [(t * 0.125).astype(BF16) for t in k_al]
            v_al = [t.astype(BF16) for t in v_al]
            q_w = q_ref[pl.ds(start, n_q), :]
            do_w = do_ref[pl.ds(start, n_q), :]
            t_abs = start + lax.broadcasted_iota(jnp.int32, (1, n_q), 1)
            valid = _band_mask(variant, t_abs, s_abs)
            dk_acc = dv_acc = None
            ds_both = []
            for half in (0, 1):
                s = lax.dot_general(k_al[half], q_w, NT, preferred_element_type=F32)
                if variant == "a":
                    s = s + (-slope_ref[2 * p + half]) * jnp.abs(t_abs - s_abs).astype(F32)
                elif variant == "b":
                    s = s + cq_ref[half:half + 1, pl.ds(start, n_q)] - ck_ref[rows, half:half + 1]
                else:
                    i0 = start // BLK
                    s = s + jnp.concatenate(
                        [bias_ref[half, jnp.clip(i0 + b - j, 0, 4)] for b in range(win)], axis=1)
                pr = jnp.where(valid, jnp.exp(s - lse_ref[half:half + 1, pl.ds(start, n_q)]), 0.0)
                dp = lax.dot_general(v_al[half], do_w, NT, preferred_element_type=F32)
                ds = pr * (dp - dl_ref[half:half + 1, pl.ds(start, n_q)])
                ds16 = ds.astype(BF16)
                dv_h = to_kv_lanes(jnp.dot(pr.astype(BF16), do_w, preferred_element_type=F32), half)
                dk_h = to_kv_lanes(jnp.dot(ds16, q_w, preferred_element_type=F32) * 0.125, half)
                dv_acc = dv_h if dv_acc is None else dv_acc + dv_h
                dk_acc = dk_h if dk_acc is None else dk_acc + dk_h
                ds_both.append(ds16)
                if variant == "b":
                    ex_ref[rows, half:half + 1] = -jnp.sum(ds, axis=1, keepdims=True)
                    dcq_ref[half:half + 1, pl.ds(start, n_q)] += jnp.sum(ds, axis=0, keepdims=True)
                elif variant == "c":
                    for b in range(win):
                        ex_ref[half, jnp.clip(i0 + b - j, 0, 4)] += ds[:, b * BLK:(b + 1) * BLK]
            dq_t = lax.dot_general(jnp.concatenate(k_al, axis=0), jnp.concatenate(ds_both, axis=0), TN,
                                   preferred_element_type=F32)
            dq_ref[:, pl.ds(start, n_q)] += dq_t
            if shared_kv:
                @pl.when(p == 0)
                def _():
                    dk_ref[pl.ds(off_k, BLK), :] = dk_acc
                    dv_ref[pl.ds(off_k, BLK), :] = dv_acc

                @pl.when(p > 0)
                def _():
                    dk_ref[pl.ds(off_k, BLK), :] += dk_acc
                    dv_ref[pl.ds(off_k, BLK), :] += dv_acc
            else:
                dk_ref[pl.ds(off_k, BLK), :] = dk_acc
                dv_ref[pl.ds(off_k, BLK), :] = dv_acc
            if variant == "a":
                for half in (0, 1):
                    p_sink = jnp.exp(sink_ref[2 * p + half] - lse_ref[half:half + 1, pl.ds(off_k, BLK)])
                    term = p_sink * dl_ref[half:half + 1, pl.ds(off_k, BLK)]
                    ex_ref[half:half + 1, :] += -jnp.sum(term, axis=1, keepdims=True)

        for sub in range(per_step):
            j = pl.program_id(1) * per_step + sub
            rows = slice(sub * BLK, (sub + 1) * BLK)
            if variant == "b":
                for g in range(S // GROUP):
                    pl.when(j // 4 == g)(functools.partial(compute, j, rows, g * GROUP, S - g * GROUP))
            else:
                start = jnp.clip(j, 0, nb - win) * BLK
                compute(j, rows, pl.multiple_of(start, BLK), win * BLK)

    tk = per_step * BLK
    col = lambda c0: (lambda p, j: (0, c0 + p))
    kv_blk = (lambda c0: (lambda p, j: (j, c0))) if shared_kv else (lambda c0: (lambda p, j: (j, c0 + p)))
    pair = lambda p, j: (0, p)
    row_stat = pl.BlockSpec((None, 2, S), lambda p, j: (p, 0, 0))
    in_specs = [pl.BlockSpec((S, BLK), col(qb)),
                pl.BlockSpec((tk, BLK), kv_blk(kb)), pl.BlockSpec((tk, BLK), kv_blk(vb)),
                pl.BlockSpec((S, BLK), pair), row_stat, row_stat]
    args = [qkv, qkv, qkv, do, lse_row, delta_row]
    kv_width = BLK if shared_kv else 512
    kv_out = pl.BlockSpec((S, BLK), (lambda p, j: (0, 0)) if shared_kv else pair)
    out_shape = [jax.ShapeDtypeStruct((512, S), F32), jax.ShapeDtypeStruct((S, kv_width), F32),
                 jax.ShapeDtypeStruct((S, kv_width), F32)]
    out_specs = [pl.BlockSpec((BLK, S), lambda p, j: (p, 0)), kv_out, kv_out]
    if variant == "a":
        in_specs += [pl.BlockSpec(memory_space=pltpu.SMEM), pl.BlockSpec(memory_space=pltpu.SMEM)]
        args += [sinks, slopes]
        out_shape.append(jax.ShapeDtypeStruct((4, 8, BLK), F32))
        out_specs.append(pl.BlockSpec((None, 8, BLK), lambda p, j: (p, 0, 0)))
    elif variant == "b":
        in_specs += [row_stat, pl.BlockSpec((None, tk, 2), lambda p, j: (p, j, 0))]
        args += [cq_row, ck_col]
        out_shape += [jax.ShapeDtypeStruct((4, S, 2), F32), jax.ShapeDtypeStruct((4, 2, S), F32)]
        out_specs += [pl.BlockSpec((None, tk, 2), lambda p, j: (p, j, 0)), row_stat]
    else:
        in_specs += [pl.BlockSpec((2, 5, BLK, BLK), lambda p, j: (p, 0, 0, 0))]
        args += [bias_t]
        out_shape.append(jax.ShapeDtypeStruct((8, 5, BLK, BLK), F32))
        out_specs.append(pl.BlockSpec((2, 5, BLK, BLK), lambda p, j: (p, 0, 0, 0)))
    return pl.pallas_call(
        body, name=name, out_shape=tuple(out_shape), grid=(4, nb // per_step),
        in_specs=in_specs, out_specs=tuple(out_specs),
        compiler_params=_params(),
    )(*args)


def _log_sigmoid(x):
    return jnp.minimum(x, 0.0) - jnp.log(1.0 + jnp.exp(-jnp.abs(x)))


def _forget_fwd(fb, b_forget, name):
    S = fb.shape[0]
    nb = S // BLK

    def body(fb_ref, b_ref, cum_ref, carry_ref):
        i = pl.program_id(0)
        logf = _log_sigmoid(fb_ref[...] + b_ref[...])
        r = lax.broadcasted_iota(jnp.int32, (BLK, BLK), 0)
        c = lax.broadcasted_iota(jnp.int32, (BLK, BLK), 1)
        tri = (c <= r).astype(F32)

        @pl.when(i == 0)
        def _():
            carry_ref[...] = jnp.zeros_like(carry_ref)

        cum = jnp.dot(tri, logf, preferred_element_type=F32, precision=HIGHEST) + carry_ref[0:1, :]
        cum_ref[...] = cum
        carry_ref[...] = jnp.broadcast_to(cum[BLK - 1:BLK, :], carry_ref.shape)

    return pl.pallas_call(
        body, name=name, out_shape=jax.ShapeDtypeStruct((S, BLK), F32), grid=(nb,),
        in_specs=[pl.BlockSpec((BLK, BLK), lambda i: (i, 0)), pl.BlockSpec((1, BLK), lambda i: (0, 0))],
        out_specs=pl.BlockSpec((BLK, BLK), lambda i: (i, 0)),
        scratch_shapes=[pltpu.VMEM((8, BLK), F32)],
        compiler_params=_params(),
    )(fb, b_forget)


def _forget_bwd(dcum_q, dcum_k, fb, b_forget, name):
    S = fb.shape[0]
    nb = S // BLK

    def body(dq_ref, dk_ref, fb_ref, b_ref, dfb_ref, db_ref, carry_ref):
        g = pl.program_id(0)
        r = lax.broadcasted_iota(jnp.int32, (BLK, BLK), 0)
        c = lax.broadcasted_iota(jnp.int32, (BLK, BLK), 1)
        tri = (c >= r).astype(F32)

        @pl.when(g == 0)
        def _():
            carry_ref[...] = jnp.zeros_like(carry_ref)

        dcum = dq_ref[...] + dk_ref[...]
        dlogf = jnp.dot(tri, dcum, preferred_element_type=F32, precision=HIGHEST) + carry_ref[0:1, :]
        carry_ref[...] = jnp.broadcast_to(dlogf[0:1, :], carry_ref.shape)
        x = fb_ref[...] + b_ref[...]
        dfb = jnp.where(c < N_FORGET, dlogf * jax.nn.sigmoid(-x), 0.0)
        dfb_ref[...] = dfb
        db = jnp.sum(dfb, axis=0, keepdims=True)

        @pl.when(g == 0)
        def _():
            db_ref[...] = db

        @pl.when(g > 0)
        def _():
            db_ref[...] += db

    rev = pl.BlockSpec((BLK, BLK), lambda g: (nb - 1 - g, 0))
    row = pl.BlockSpec((1, BLK), lambda g: (0, 0))
    return pl.pallas_call(
        body, name=name,
        out_shape=(jax.ShapeDtypeStruct((S, BLK), F32), jax.ShapeDtypeStruct((1, BLK), F32)), grid=(nb,),
        in_specs=[rev, rev, rev, row], out_specs=(rev, row),
        scratch_shapes=[pltpu.VMEM((8, BLK), F32)],
        compiler_params=_params(),
    )(dcum_q, dcum_k, fb, b_forget)


def _skew(x, sign):
    row = lax.broadcasted_iota(jnp.int32, x.shape, 0)
    for b in range(7):
        amount = (1 << b) if sign > 0 else 256 - (1 << b)
        x = jnp.where(((row >> b) & 1) == 1, pltpu.roll(x, amount, 1), x)
    return x


def _rel_bases(rel):
    far = rel[:, 256:257]
    far127 = jnp.broadcast_to(far, (rel.shape[0], 127))
    base0 = jnp.concatenate([rel[:, 128:0:-1], far, rel[:, 255:128:-1]], axis=1)
    base1 = jnp.concatenate([rel[:, 256:128:-1], far, far127], axis=1)
    base0_t = jnp.concatenate([rel[:, 128:256], far, rel[:, 1:128]], axis=1)
    base1_t = jnp.concatenate([jnp.broadcast_to(far, (rel.shape[0], 128)), far, rel[:, 129:256]], axis=1)
    return jnp.stack([base0, base1, base0_t, base1_t], axis=1)


def _rel_expand(bases, name):
    def body(b_ref, t_ref, tt_ref):
        far = jnp.broadcast_to(b_ref[1:2, 0:1], (BLK, BLK))
        for k, out_ref in ((0, t_ref), (2, tt_ref)):
            for d in (0, 1):
                x = jnp.broadcast_to(b_ref[k + d:k + d + 1, :], (BLK, 2 * BLK))
                out_ref[d] = _skew(x, 1)[:, :BLK]
            for d in (2, 3, 4):
                out_ref[d] = far

    out = jax.ShapeDtypeStruct((8, 5, BLK, BLK), F32)
    spec = pl.BlockSpec((None, 5, BLK, BLK), lambda h: (h, 0, 0, 0))
    return pl.pallas_call(
        body, name=name, out_shape=(out, out), grid=(8,),
        in_specs=[pl.BlockSpec((None, 4, 2 * BLK), lambda h: (h, 0, 0))], out_specs=(spec, spec),
        compiler_params=_params(),
    )(bases)


def _rel_reduce(dtiles_t, name):
    def body(dt_ref, o_ref):
        zeros = jnp.zeros((BLK, BLK), F32)
        sums = []
        for d in (0, 1):
            x = _skew(jnp.concatenate([dt_ref[d], zeros], axis=1), -1)
            sums.append(jnp.broadcast_to(jnp.sum(x, axis=0, keepdims=True), (8, 2 * BLK)))
        lane = lax.broadcasted_iota(jnp.int32, (8, 2 * BLK), 1)
        main = pltpu.roll(sums[0], BLK, 1) + jnp.where(lane > BLK, sums[1], 0.0)
        far = jnp.sum(jnp.where(lane < BLK, sums[1], 0.0)[0:1], axis=1, keepdims=True)
        far = far + jnp.sum(jnp.sum(dt_ref[2] + dt_ref[3] + dt_ref[4], axis=0, keepdims=True), axis=1, keepdims=True)
        o_ref[...] = jnp.concatenate([main[0:1], jnp.broadcast_to(far, (1, BLK))], axis=1)

    return pl.pallas_call(
        body, name=name, out_shape=jax.ShapeDtypeStruct((8, 1, 3 * BLK), F32), grid=(8,),
        in_specs=[pl.BlockSpec((None, 5, BLK, BLK), lambda h: (h, 0, 0, 0))],
        out_specs=pl.BlockSpec((None, 1, 3 * BLK), lambda h: (h, 0, 0)),
        compiler_params=_params(),
    )(dtiles_t)


def _final_loss(x, target, g, name):
    S, D = x.shape
    ts = _row_tile(S, 256)

    def body(x_ref, t_ref, g_ref, dx_ref, loss_ref, dg_ref):
        i = pl.program_id(0)
        xv, gv = x_ref[...], g_ref[...]
        rstd = lax.rsqrt(jnp.mean(xv * xv, axis=-1, keepdims=True) + EPS)
        xhat = xv * rstd
        err = xhat * gv - t_ref[...]
        part = 0.5 * jnp.sum(jnp.mean(err * err, axis=-1, keepdims=True), axis=0, keepdims=True)
        dy = err / D
        dg = jnp.sum(dy * xhat, axis=0, keepdims=True)
        dxhat = dy * gv
        proj = jnp.mean(dxhat * xhat, axis=-1, keepdims=True)
        dx_ref[...] = rstd * (dxhat - xhat * proj)

        @pl.when(i == 0)
        def _():
            loss_ref[...] = jnp.broadcast_to(part, loss_ref.shape)
            dg_ref[...] = dg

        @pl.when(i > 0)
        def _():
            loss_ref[...] += jnp.broadcast_to(part, loss_ref.shape)
            dg_ref[...] += dg

    tile = pl.BlockSpec((ts, D), lambda i: (i, 0))
    row = pl.BlockSpec((1, D), lambda i: (0, 0))
    return pl.pallas_call(
        body, name=name,
        out_shape=(jax.ShapeDtypeStruct((S, D), F32), jax.ShapeDtypeStruct((8, 128), F32),
                   jax.ShapeDtypeStruct((1, D), F32)),
        grid=(S // ts,), in_specs=[tile, tile, row],
        out_specs=(tile, pl.BlockSpec((8, 128), lambda i: (0, 0)), row),
        compiler_params=_params(),
    )(x, target, g)


def _ada_fwd(c_all, w_ada, name):
    L, D, E = w_ada.shape

    def body(c_ref, w_ref, o_ref):
        cv = c_ref[...]
        cond = cv * jax.nn.sigmoid(cv)
        o_ref[...] = jnp.dot(cond, w_ref[...], preferred_element_type=F32, precision=HIGHEST)

    return pl.pallas_call(
        body, name=name, out_shape=jax.ShapeDtypeStruct((L, N_DEV, E), F32), grid=(L,),
        in_specs=[pl.BlockSpec((N_DEV, D), lambda l: (0, 0)), pl.BlockSpec((None, D, E), lambda l: (l, 0, 0))],
        out_specs=pl.BlockSpec((None, N_DEV, E), lambda l: (l, 0, 0)),
        compiler_params=_params(),
    )(c_all, w_ada)


def _ada_bwd(c_all_t, dmod, name):
    D = c_all_t.shape[0]
    L, _, E = dmod.shape

    def body(c_ref, d_ref, o_ref):
        cv = c_ref[...]
        cond = cv * jax.nn.sigmoid(cv)
        acc = None
        for b in range(N_DEV):
            t = cond[:, b:b + 1] * d_ref[b:b + 1, :]
            acc = t if acc is None else acc + t
        o_ref[...] = acc

    return pl.pallas_call(
        body, name=name, out_shape=jax.ShapeDtypeStruct((L, D, E), F32), grid=(L,),
        in_specs=[pl.BlockSpec((D, N_DEV), lambda l: (0, 0)), pl.BlockSpec((None, N_DEV, E), lambda l: (l, 0, 0))],
        out_specs=pl.BlockSpec((None, D, E), lambda l: (l, 0, 0)),
        compiler_params=_params(),
    )(c_all_t, dmod)


def _adamw(w, m, v, g_parts, name, me, after=None):
    L, R, C = w.shape
    tr = _row_tile(R, max(8, (256 * 1024 // max(C, 128)) // 8 * 8))
    nr = R // tr
    c1 = 1.0 - ADAM_B1 ** ADAM_STEP
    c2 = 1.0 - ADAM_B2 ** ADAM_STEP
    direct = [isinstance(p, tuple) for p in g_parts]
    n_in = sum(2 if d else 1 for d in direct)

    def body(me_ref, w_ref, m_ref, v_ref, *rest):
        g_refs, (go_ref, d_ref, mo_ref, vo_ref) = list(rest[:n_in]), rest[-4:]
        layer = pl.program_id(0)
        g = None
        for l in range(L):
            land_ref = g_refs.pop(0)
            own = g_refs.pop(0)[...].astype(F32) if direct[l] else None
            gl = None
            for k in range(land_ref.shape[0]):
                part = land_ref[k].astype(F32)
                if direct[l]:
                    part = jnp.where(me_ref[0] == k, own, part)
                gl = part if gl is None else gl + part
            g = gl if g is None else jnp.where(layer == l, gl, g)
        mn = ADAM_B1 * m_ref[...] + (1.0 - ADAM_B1) * g
        vn = ADAM_B2 * v_ref[...] + (1.0 - ADAM_B2) * (g * g)
        m_hat = mn / c1
        v_hat = vn / c2
        go_ref[...] = g
        d_ref[...] = -ADAM_LR * (m_hat / (jnp.sqrt(v_hat) + ADAM_EPS) + ADAM_WD * w_ref[...])
        mo_ref[...] = mn
        vo_ref[...] = vn

    def rows(l, layer, i):
        return jnp.where(layer == l, i, 0 if l > 0 else nr - 1)

    in_specs, operands = [], []
    for l, p in enumerate(g_parts):
        land, sent = p if direct[l] else (p, None)
        in_specs.append(pl.BlockSpec((land.shape[0], tr, C), lambda layer, i, me_ref, l=l: (0, rows(l, layer, i), 0)))
        operands.append(land)
        if direct[l]:
            in_specs.append(pl.BlockSpec((None, tr, C), lambda layer, i, me_ref, l=l: (me_ref[0], rows(l, layer, i), 0)))
            operands.append(sent)
    if after is not None:
        in_specs.append(ANY)
        operands.append(after)
    tile = pl.BlockSpec((None, tr, C), lambda layer, i, me_ref: (layer, i, 0))
    out = jax.ShapeDtypeStruct((L, R, C), F32)
    return pl.pallas_call(
        body, name=name, out_shape=(out, out, out, out),
        grid_spec=pltpu.PrefetchScalarGridSpec(
            num_scalar_prefetch=1, grid=(L, nr), in_specs=[tile, tile, tile] + in_specs,
            out_specs=(tile, tile, tile, tile)),
        compiler_params=_params(),
    )(me, w, m, v, *operands)


def _pair_add(pieces, recv, core, name):
    _, _, R, C = pieces.shape
    tr = _row_tile(R, max(8, (512 * 1024 // max(C, 128)) // 8 * 8))

    def body(core_ref, a_ref, b_ref, o_ref):
        o_ref[...] = (a_ref[...].astype(F32) + b_ref[...].astype(F32)).astype(BF16)

    return pl.pallas_call(
        body, name=name, out_shape=jax.ShapeDtypeStruct((4, R, C), BF16),
        grid_spec=pltpu.PrefetchScalarGridSpec(
            num_scalar_prefetch=1, grid=(4, R // tr),
            in_specs=[pl.BlockSpec((None, None, tr, C), lambda k, i, core_ref: (core_ref[0], k, i, 0)),
                      pl.BlockSpec((None, tr, C), lambda k, i, core_ref: (k, i, 0))],
            out_specs=pl.BlockSpec((None, tr, C), lambda k, i, core_ref: (k, i, 0))),
        compiler_params=_params(),
    )(core, pieces, recv)


MESH = pl.DeviceIdType.MESH
ANY = pl.BlockSpec(memory_space=pl.ANY)


def _position():
    return lax.axis_index("x"), lax.axis_index("y"), lax.axis_index("c")


def _small_all_gather(v, name):
    m_per, n = v.shape

    def body(x_ref, out_ref, send_sems, recv_sems, local_sem):
        x, y, c = _position()
        me, sibling = (x, y, c), (x, y, 1 - c)
        chips = [(1 - x, y), (x, 1 - y), (1 - x, 1 - y)]

        def rows(px, py, pc):
            return out_ref.at[pl.ds((4 * px + 2 * py + pc) * m_per, m_per), :]

        def copy(k, block, to, src=None):
            return pltpu.make_async_remote_copy(
                src_ref=rows(*block) if src is None else src, dst_ref=rows(*block),
                send_sem=send_sems.at[k], recv_sem=recv_sems.at[k], device_id=to, device_id_type=MESH)

        mine = pltpu.make_async_copy(x_ref, rows(*me), local_sem)
        mine.start()
        first = [copy(0, me, sibling, src=x_ref)]
        first += [copy(1 + j, me, (*chip, c), src=x_ref) for j, chip in enumerate(chips)]
        for cp in first:
            cp.start()
        passed = [copy(4 + j, (*chip, c), sibling) for j, chip in enumerate(chips)]
        for j, chip in enumerate(chips):
            copy(1 + j, (*chip, c), me).wait_recv()
            passed[j].start()
        copy(0, sibling, me).wait_recv()
        for j, chip in enumerate(chips):
            copy(4 + j, (*chip, 1 - c), me).wait_recv()
        for cp in first + passed:
            cp.wait_send()
        mine.wait()

    return pl.pallas_call(
        body, name=name, out_shape=jax.ShapeDtypeStruct((N_DEV * m_per, n), v.dtype),
        in_specs=[pl.BlockSpec(memory_space=pltpu.VMEM)], out_specs=pl.BlockSpec(memory_space=pltpu.VMEM),
        scratch_shapes=[pltpu.SemaphoreType.DMA((7,)), pltpu.SemaphoreType.DMA((7,)), pltpu.SemaphoreType.DMA],
    )(v)


def _big_all_gather(shards, name):
    n_arr = len(shards)

    def body(*refs):
        x_refs, out_refs = refs[:n_arr], refs[n_arr:2 * n_arr]
        send_sems, recv_sems, local_sems = refs[2 * n_arr:]
        x, y, c = _position()
        me, sibling = (x, y, c), (x, y, 1 - c)
        chips = [(1 - x, y), (x, 1 - y), (1 - x, 1 - y)]

        def slot(a, px, py, pc):
            return out_refs[a].at[4 * px + 2 * py + pc]

        def copy(a, k, block, to, src=None):
            return pltpu.make_async_remote_copy(
                src_ref=slot(a, *block) if src is None else src, dst_ref=slot(a, *block),
                send_sem=send_sems.at[a, k], recv_sem=recv_sems.at[a, k], device_id=to, device_id_type=MESH)

        mine = [pltpu.make_async_copy(x_refs[a], slot(a, *me), local_sems.at[a]) for a in range(n_arr)]
        for cp in mine:
            cp.start()
        first = []
        for j, chip in enumerate(chips):
            first += [copy(a, 1 + j, me, (*chip, c), src=x_refs[a]) for a in range(n_arr)]
        first += [copy(a, 0, me, sibling, src=x_refs[a]) for a in range(n_arr)]
        for cp in first:
            cp.start()
        passed = []
        for j, chip in enumerate(chips):
            for a in range(n_arr):
                copy(a, 1 + j, (*chip, c), me).wait_recv()
                fwd = copy(a, 4 + j, (*chip, c), sibling)
                fwd.start()
                passed.append(fwd)
        for a in range(n_arr):
            copy(a, 0, sibling, me).wait_recv()
        for j, chip in enumerate(chips):
            for a in range(n_arr):
                copy(a, 4 + j, (*chip, 1 - c), me).wait_recv()
        for cp in first + passed:
            cp.wait_send()
        for cp in mine:
            cp.wait()

    return pl.pallas_call(
        body, name=name,
        out_shape=tuple(jax.ShapeDtypeStruct((N_DEV,) + s.shape, s.dtype) for s in shards),
        in_specs=[ANY] * n_arr, out_specs=tuple([ANY] * n_arr),
        scratch_shapes=[pltpu.SemaphoreType.DMA((n_arr, 7)), pltpu.SemaphoreType.DMA((n_arr, 7)),
                        pltpu.SemaphoreType.DMA((n_arr,))],
    )(*shards)


def _sibling_exchange(pieces, name):
    n_arr = len(pieces)

    def body(*refs):
        p_refs, out_refs = refs[:n_arr], refs[n_arr:2 * n_arr]
        send_sems, recv_sems = refs[2 * n_arr:]
        x, y, c = _position()
        copies = [pltpu.make_async_remote_copy(
            src_ref=p_refs[a].at[1 - c], dst_ref=out_refs[a], send_sem=send_sems.at[a], recv_sem=recv_sems.at[a],
            device_id=(x, y, 1 - c), device_id_type=MESH) for a in range(n_arr)]
        for cp in copies:
            cp.start()
        for cp in copies:
            cp.wait()

    return pl.pallas_call(
        body, name=name,
        out_shape=tuple(jax.ShapeDtypeStruct(p.shape[1:], p.dtype) for p in pieces),
        in_specs=[ANY] * n_arr, out_specs=tuple([ANY] * n_arr),
        scratch_shapes=[pltpu.SemaphoreType.DMA((n_arr,)), pltpu.SemaphoreType.DMA((n_arr,))],
    )(*pieces)


def _chip_exchange(sums, name):
    n_arr = len(sums)

    def body(*refs):
        s_refs, out_refs = refs[:n_arr], refs[n_arr:2 * n_arr]
        send_sems, recv_sems, local_sems = refs[2 * n_arr:]
        x, y, c = _position()
        my_chip = 2 * x + y
        chips = [(1 - x, y), (x, 1 - y), (1 - x, 1 - y)]
        mine = [pltpu.make_async_copy(s_refs[a].at[my_chip], out_refs[a].at[my_chip], local_sems.at[a])
                for a in range(n_arr)]
        for cp in mine:
            cp.start()
        copies = []
        for j, (px, py) in enumerate(chips):
            copies += [pltpu.make_async_remote_copy(
                src_ref=s_refs[a].at[2 * px + py], dst_ref=out_refs[a].at[my_chip],
                send_sem=send_sems.at[a, j], recv_sem=recv_sems.at[a, j],
                device_id=(px, py, c), device_id_type=MESH) for a in range(n_arr)]
        for cp in copies:
            cp.start()
        for j, (px, py) in enumerate(chips):
            for a in range(n_arr):
                pltpu.make_async_remote_copy(
                    src_ref=s_refs[a].at[my_chip], dst_ref=out_refs[a].at[2 * px + py],
                    send_sem=send_sems.at[a, j], recv_sem=recv_sems.at[a, j],
                    device_id=(px, py, c), device_id_type=MESH).wait_recv()
        for cp in copies:
            cp.wait_send()
        for cp in mine:
            cp.wait()

    return pl.pallas_call(
        body, name=name,
        out_shape=tuple(jax.ShapeDtypeStruct(s.shape, s.dtype) for s in sums),
        in_specs=[ANY] * n_arr, out_specs=tuple([ANY] * n_arr),
        scratch_shapes=[pltpu.SemaphoreType.DMA((n_arr, 3)), pltpu.SemaphoreType.DMA((n_arr, 3)),
                        pltpu.SemaphoreType.DMA((n_arr,))],
    )(*sums)


HBM = pl.BlockSpec(memory_space=pltpu.HBM)
SEM = pl.BlockSpec(memory_space=pltpu.SEMAPHORE)
EFFECT = pltpu.SideEffectType.DATAFLOW_SIDE_EFFECTING
RELATIONS = [(rx, ry, rc) for rx in (0, 1) for ry in (0, 1) for rc in (0, 1)][1:]


SAME_CORE = [r for r in RELATIONS if r == (0, 0, 1) or r[2] == 0]


def _exchange_copies(src_refs, land_refs, send_sems, recv_sems, scatter, receive_side, relations):
    x, y, c = _position()
    me = 4 * x + 2 * y + c
    copies = []
    for k, (rx, ry, rc) in enumerate(relations):
        peer = ((1 - x) if rx else x, (1 - y) if ry else y, (1 - c) if rc else c)
        peer_index = 4 * peer[0] + 2 * peer[1] + peer[2]
        for a, (src, land) in enumerate(zip(src_refs, land_refs)):
            copies.append(pltpu.make_async_remote_copy(
                src_ref=src.at[peer_index] if scatter else src,
                dst_ref=land.at[peer_index if receive_side else me],
                send_sem=send_sems.at[a * len(relations) + k], recv_sem=recv_sems.at[a * len(relations) + k],
                device_id=peer, device_id_type=MESH))
    return copies


def _exchange_start(srcs, scatter, after, name, relations=RELATIONS):
    n = len(srcs)
    land_shapes = [(s.shape if scatter else (N_DEV,) + s.shape) for s in srcs]

    def body(*refs):
        src_refs, land_refs = refs[:n], refs[n:2 * n]
        send_sems, recv_sems = refs[2 * n + 1], refs[2 * n + 2]
        token = refs[-1]
        for cp in _exchange_copies(src_refs, land_refs, send_sems, recv_sems, scatter, False, relations):
            cp.start()
        token[...] = jnp.zeros_like(token)

    sems = pltpu.SemaphoreType.DMA((n * len(relations),))
    outs = pl.pallas_call(
        body, name=name,
        out_shape=(sems, sems, *[pltpu.HBM(s.shape, s.dtype) for s in srcs],
                   *[pltpu.HBM(shape, s.dtype) for shape, s in zip(land_shapes, srcs)],
                   jax.ShapeDtypeStruct((8, 128), F32)),
        in_specs=[HBM] * (2 * n) + [ANY],
        out_specs=(SEM, SEM, *[HBM] * (2 * n), pl.BlockSpec(memory_space=pltpu.VMEM)),
        input_output_aliases={a: 2 + a for a in range(2 * n)},
        compiler_params=pltpu.CompilerParams(has_side_effects=EFFECT),
    )(*[pltpu.with_memory_space_constraint(s, pltpu.HBM) for s in srcs],
      *[pltpu.with_memory_space_constraint(lax.empty(shape, s.dtype), pltpu.HBM)
        for shape, s in zip(land_shapes, srcs)], after)
    return outs[0], outs[1], outs[2:2 + n], outs[2 + n:2 + 2 * n], outs[-1]


def _exchange_wait(started, scatter, after, name, relations=RELATIONS):
    send_sems, recv_sems, srcs, lands, _ = started
    n = len(srcs)

    def body(*refs):
        src_refs, land_refs = refs[:n], refs[n:2 * n]
        send_sems, recv_sems = refs[2 * n], refs[2 * n + 1]
        copies = _exchange_copies(src_refs, land_refs, send_sems, recv_sems, scatter, True, relations)
        for cp in copies:
            cp.wait_send()
        for cp in copies:
            cp.wait_recv()

    outs = pl.pallas_call(
        body, name=name,
        out_shape=(*[pltpu.HBM(s.shape, s.dtype) for s in srcs], *[pltpu.HBM(t.shape, t.dtype) for t in lands]),
        in_specs=[HBM] * (2 * n) + [SEM, SEM, ANY], out_specs=tuple([HBM] * (2 * n)),
        input_output_aliases={a: a for a in range(2 * n)},
        compiler_params=pltpu.CompilerParams(has_side_effects=EFFECT),
    )(*srcs, *lands, send_sems, recv_sems, after)
    return outs[:n], outs[n:]


def _sibling_forward(lands, name):
    n = len(lands)

    def body(*refs):
        in_refs, out_refs = refs[:n], refs[n:2 * n]
        send_sems, recv_sems = refs[2 * n:]
        x, y, c = _position()
        copies, arrivals = [], []
        for j, (px, py) in enumerate([(1 - x, y), (x, 1 - y), (1 - x, 1 - y)]):
            held, coming = 4 * px + 2 * py + c, 4 * px + 2 * py + (1 - c)
            for a in range(n):
                sems = dict(send_sem=send_sems.at[a, j], recv_sem=recv_sems.at[a, j], device_id=(x, y, 1 - c),
                            device_id_type=MESH)
                copies.append(pltpu.make_async_remote_copy(
                    src_ref=in_refs[a].at[held], dst_ref=out_refs[a].at[held], **sems))
                arrivals.append(pltpu.make_async_remote_copy(
                    src_ref=in_refs[a].at[held], dst_ref=out_refs[a].at[coming], **sems))
        for cp in copies:
            cp.start()
        for cp in copies:
            cp.wait_send()
        for cp in arrivals:
            cp.wait_recv()

    return pl.pallas_call(
        body, name=name, out_shape=tuple(jax.ShapeDtypeStruct(t.shape, t.dtype) for t in lands),
        in_specs=[ANY] * n, out_specs=tuple([ANY] * n), input_output_aliases={a: a for a in range(n)},
        scratch_shapes=[pltpu.SemaphoreType.DMA((n, 3)), pltpu.SemaphoreType.DMA((n, 3))],
    )(*lands)


W_IN_SHARD = N_IN // N_DEV
F_SHARD = F_COL // W_IN_SHARD
F_LO = F_COL - F_SHARD * W_IN_SHARD


def _w_ffn_in_view(w):
    return jnp.transpose(w, (0, 2, 1))


def _w_in_rearranged(g):
    parts = [g[d] for d in range(N_DEV)]
    with_f = parts[F_SHARD]
    parts[F_SHARD:F_SHARD + 1] = [with_f[:, :F_LO], with_f[:, F_LO + N_FORGET:]]
    parts += [with_f[:, F_LO:F_LO + N_FORGET], jnp.zeros((with_f.shape[0], BLK - N_FORGET), with_f.dtype)]
    return jnp.concatenate(parts, axis=1)


def _w_in_pieces(dw_r):
    def original(lo, hi):
        shift = 0 if hi <= F_COL else N_FORGET
        return dw_r[:, lo - shift:hi - shift]

    pieces = []
    for d in range(N_DEV):
        lo, hi = d * W_IN_SHARD, (d + 1) * W_IN_SHARD
        if d == F_SHARD:
            pieces.append(jnp.concatenate([original(lo, F_COL), dw_r[:, N_MAIN:N_MAIN + N_FORGET],
                                           original(F_COL + N_FORGET, hi)], axis=1))
        else:
            pieces.append(original(lo, hi))
    return jnp.stack(pieces)


def _row_pieces(dw):
    return dw.reshape(N_DEV, dw.shape[0] // N_DEV, dw.shape[1])


def _branch_pieces(dw):
    k, w, d = dw.shape
    return jnp.transpose(dw.reshape(k, w, N_DEV, d // N_DEV), (2, 0, 1, 3)).reshape(N_DEV, k * w, d // N_DEV)


def _pair_major(p8):
    return jnp.stack([p8[0::2], p8[1::2]])


def _pairs_col(a):
    return jnp.transpose(a.reshape(a.shape[0], 4, 2), (1, 0, 2))


def _pairs_row(a):
    return jnp.transpose(a.reshape(a.shape[0], 4, 2), (1, 2, 0))


def _heads_from_col(a):
    return jnp.transpose(a, (1, 0, 2)).reshape(a.shape[1], 8)


def _heads_from_row(a):
    return jnp.transpose(a, (2, 0, 1)).reshape(a.shape[2], 8)


def _pad_lanes(a, n):
    return jnp.pad(a, [(0, 0)] * (a.ndim - 1) + [(0, n - a.shape[-1])])


SMALL_SEGMENTS = (("dmod", 2 * 6 * D_MODEL), ("norm_mix_g", 2 * D_MODEL), ("norm_ffn_g", 2 * D_MODEL),
                  ("final_norm_g", D_MODEL), ("b_forget", 128), ("sinks", 128), ("rel_bias", 4224))
SMALL_ROWS = 176


def _pack_small(parts):
    flat = [_pad_lanes(parts[name].reshape(1, -1), size) for name, size in SMALL_SEGMENTS]
    total = sum(size for _, size in SMALL_SEGMENTS)
    flat.append(jnp.zeros((1, SMALL_ROWS * 128 - total), F32))
    return jnp.concatenate(flat, axis=1).reshape(SMALL_ROWS, 128)


def _unpack_small(packed, shapes):
    flat = packed.reshape(-1)
    out, pos = {}, 0
    for name, size in SMALL_SEGMENTS:
        shape = shapes[name]
        count = 1
        for d in shape:
            count *= d
        out[name] = flat[pos:pos + count].reshape(shape)
        pos += size
    return out


def kernel(x, c, norm_mix_g, norm_ffn_g, w_ada, b_ada, w_in, b_forget, sinks, rel_bias, w_branch, w_out, w_ffn_in, w_ffn_out, final_norm_g, loss_target, m_norm_mix_g, m_norm_ffn_g, m_w_ada, m_b_ada, m_w_in, m_b_forget, m_sinks, m_rel_bias, m_w_branch, m_w_out, m_w_ffn_in, m_w_ffn_out, m_final_norm_g, v_norm_mix_g, v_norm_ffn_g, v_w_ada, v_b_ada, v_w_in, v_b_forget, v_sinks, v_rel_bias, v_w_branch, v_w_out, v_w_ffn_in, v_w_ffn_out, v_final_norm_g):
    depth = w_in.shape[0]
    S, D = x.shape[1], x.shape[2]
    assert S % GROUP == 0 and S >= ATTN_WINDOW["c"] * BLK
    px, py, pc = _position()
    me = 4 * px + 2 * py + pc
    x0 = x[0]

    assert depth == 2
    big_weights = (w_in, w_branch, w_out, w_ffn_in, w_ffn_out)
    me_arr = me.astype(jnp.int32).reshape(1)

    def slabs(landed, mine):
        return [jnp.where(me == d, mine, landed[d]) for d in range(N_DEV)]

    def rest_matrices(g_branch, g_out, g_fin, g_fout):
        return (jnp.transpose(jnp.stack(g_branch), (1, 2, 0, 3)).reshape(3, 512, D),
                jnp.concatenate(g_out, axis=0), jnp.concatenate(g_fin, axis=0), jnp.concatenate(g_fout, axis=0))

    def finish_gather(started, after, name):
        mine, landed = _exchange_wait(started, False, after, f"{name}_wait", SAME_CORE)
        landed = _sibling_forward(landed, f"{name}_forward")
        return [slabs(t, s) for t, s in zip(landed, mine)]

    w_fin_t = _w_ffn_in_view(w_ffn_in)
    shards = [[t.astype(BF16) for t in (w_in[l], w_branch[l], w_out[l], w_fin_t[l], w_ffn_out[l])]
              for l in range(depth)]
    gathered_in0 = _big_all_gather(shards[0][:1], "comm_gather_w_in0")[0]
    gather_rest0 = _exchange_start(shards[0][1:], False, gathered_in0, "comm_gather_rest0_start", SAME_CORE)
    gather1 = _exchange_start(shards[1], False, gather_rest0[4], "comm_gather_weights1_start", SAME_CORE)
    W_in, W_branch, W_out, W_fin, W_fout = ([None, None] for _ in range(5))
    W_in[0] = _w_in_rearranged(gathered_in0)

    c_all = _small_all_gather(c.reshape(8, 128), "comm_gather_c").reshape(N_DEV, D)
    mod_cols = _ada_fwd(c_all, w_ada, "ada_fwd")
    mod_all = _small_all_gather(mod_cols.reshape(-1, 128), "comm_gather_mod")
    mod_all = mod_all.reshape(N_DEV, depth, N_DEV, w_ada.shape[2])
    mod_mine = lax.dynamic_index_in_dim(mod_all, me, axis=2, keepdims=False)
    mod = jnp.transpose(mod_mine, (1, 0, 2)).reshape(depth, 6 * D) + b_ada + gather1[4][0:1, 0:1]
    mods = [[mod[l:l + 1, k * D:(k + 1) * D] for k in range(6)] for l in range(depth)]

    slopes = jnp.exp2(-jnp.arange(1, 9, dtype=F32))
    saved = []
    xs = x0
    for l in range(depth):
        if l == 1:
            g_in1, *g_rest1 = finish_gather(gather1, xs, "comm_gather_weights1")
            W_in[1] = _w_in_rearranged(g_in1)
            W_branch[1], W_out[1], W_fin[1], W_fout[1] = rest_matrices(*g_rest1)
        sh_m, sc_m, g_m, sh_f, sc_f, g_f = mods[l]
        gm, gf = norm_mix_g[l:l + 1], norm_ffn_g[l:l + 1]
        bfor = _pad_lanes(b_forget[l:l + 1], BLK)
        h = _norm_mod_fwd(xs, gm, sh_m, sc_m, f"norm_mix_fwd{l}")
        qkv = _matmul(h, W_in[l], "nn", BF16, f"proj_qkv{l}", TILES["proj_qkv"], n=N_QKV)
        gates = _matmul(h, W_in[l], "nn", F32, f"proj_gates{l}", TILES["proj_gates"], n=N_GATES,
                        b_off=N_QKV // TILES["proj_gates"][1])
        fb = _matmul(h, W_in[l], "nn", F32, f"proj_forget{l}", TILES["proj_forget"], n=BLK, b_off=N_MAIN // BLK)
        cum = _forget_fwd(fb, bfor, f"forget_fwd{l}")[:, :N_FORGET]
        cum_col, cum_row = _pairs_col(cum), _pairs_row(cum)
        tiles, tiles_t = _rel_expand(_rel_bases(rel_bias[l]), f"rel_expand{l}")
        o_a, lse_a = _attn_fwd("a", qkv, f"attn_a_fwd{l}", sinks=sinks[l], slopes=slopes)
        o_b, lse_b = _attn_fwd("b", qkv, f"attn_b_fwd{l}", cq_col=cum_col, ck_row=cum_row)
        o_c, lse_c = _attn_fwd("c", qkv, f"attn_c_fwd{l}", bias=tiles)
        if l == 0:
            W_branch[0], W_out[0], W_fin[0], W_fout[0] = rest_matrices(*finish_gather(gather_rest0, o_c, "comm_gather_rest0"))
        merged = _merge_fwd(o_a, o_b, o_c, gates, W_branch[l], f"merge_fwd{l}")
        x1, mix = _matmul_resid(merged, W_out[l], xs, g_m, f"out_proj{l}", TILES["out_proj"])
        h2 = _norm_mod_fwd(x1, gf, sh_f, sc_f, f"norm_ffn_fwd{l}")
        act = _ffn_in_fwd(h2, W_fin[l], f"ffn_in_fwd{l}")
        x2, ffn = _matmul_resid(act, W_fout[l], x1, g_f, f"ffn_out{l}", TILES["ffn_out"])
        saved.append(dict(x=xs, h=h, qkv=qkv, gates=gates, fb=fb, bfor=bfor, cum_col=cum_col, cum_row=cum_row,
                          tiles_t=tiles_t, o=(o_a, o_b, o_c), lse=(lse_a, lse_b, lse_c), merged=merged, mix=mix,
                          x1=x1, h2=h2, act=act, ffn=ffn))
        xs = x2

    dx, loss_tile, d_final_g = _final_loss(xs, loss_target[0], final_norm_g.reshape(1, D), "final_loss")
    loss = lax.psum(loss_tile[0, 0], ("x", "y", "c"))

    grads = {k: [None] * depth for k in ("w_in", "w_branch", "w_out", "w_ffn_in", "w_ffn_out", "norm_mix_g",
                                          "norm_ffn_g", "b_forget", "sinks", "rel_bias", "dmod")}
    def rest_pieces(l):
        return [_branch_pieces(grads["w_branch"][l]), _row_pieces(grads["w_out"][l]),
                _row_pieces(grads["w_ffn_in"][l]), _row_pieces(grads["w_ffn_out"][l])]

    reduce1 = reduce_rest0 = reduce_in0 = None
    for l in reversed(range(depth)):
        sv = saved[l]
        sh_m, sc_m, g_m, sh_f, sc_f, g_f = mods[l]
        if l == 0:
            g_f = g_f + reduce1[4][0:1, 0:1]
        gm, gf = norm_mix_g[l:l + 1], norm_ffn_g[l:l + 1]
        df, d_g_f = _gate_bwd(dx, sv["ffn"], g_f, f"ffn_gate_bwd{l}")
        du_g, du_u = _ffn_mid_bwd(sv["h2"], df, W_fin[l], W_fout[l], f"ffn_mid_bwd{l}")
        du = jnp.concatenate([du_g, du_u], axis=1)
        grads["w_ffn_out"][l] = _matmul(sv["act"], df, "tn", BF16, f"wgrad_ffn_out{l}", TILES["wgrad_ffn_out"])
        grads["w_ffn_in"][l] = _matmul(du, sv["h2"], "tn", BF16, f"wgrad_ffn_in{l}", TILES["wgrad_ffn_in"])
        dh2 = _matmul(du, W_fin[l], "nn", F32, f"dgrad_ffn_in{l}", TILES["dgrad_ffn_in"])
        dx1, d_sh_f, d_sc_f, d_gf = _norm_mod_bwd(sv["x1"], dh2, dx, gf, sc_f, f"norm_ffn_bwd{l}")
        dmix, d_g_m = _gate_bwd(dx1, sv["mix"], g_m, f"mix_gate_bwd{l}")
        grads["w_out"][l] = _matmul(sv["merged"], dmix, "tn", BF16, f"wgrad_out{l}", TILES["wgrad_out"])
        dmerged = _matmul(dmix, W_out[l], "nt", F32, f"dgrad_out{l}", TILES["dgrad_out"])
        o_a, o_b, o_c = sv["o"]
        dgates, dy, do_a, do_b, do_c, dl_a, dl_b, dl_c = _merge_bwd(
            dmerged, o_a, o_b, o_c, sv["gates"], W_branch[l], f"merge_bwd{l}")
        dwb = [_matmul(o_k, dy, "tn", BF16, f"wgrad_branch{l}_{k}", TILES["wgrad_branch"], n=D,
                       b_off=k * (D // TILES["wgrad_branch"][1])) for k, o_k in enumerate((o_a, o_b, o_c))]
        grads["w_branch"][l] = jnp.stack(dwb)
        lse_rows = [_pairs_row(_heads_from_col(t)) for t in sv["lse"]]
        if l == 0:
            reduce_rest0 = _exchange_start(rest_pieces(0), True, dy, "comm_reduce_rest0_start")
            lse_rows = [t + reduce_rest0[4][0:1, 0:1] for t in lse_rows]
        dqt_a, dk_a, dv_a, dsink = _attn_bwd("a", sv["qkv"], do_a, lse_rows[0], _pairs_row(dl_a), f"attn_a_bwd{l}",
                                             sinks=sinks[l], slopes=slopes)
        dqt_b, dk_b, dv_b, dck, dcq = _attn_bwd("b", sv["qkv"], do_b, lse_rows[1], _pairs_row(dl_b),
                                                f"attn_b_bwd{l}", cq_row=sv["cum_row"], ck_col=sv["cum_col"])
        dqt_c, dk_c, dv_c, dtiles_t = _attn_bwd("c", sv["qkv"], do_c, lse_rows[2], _pairs_row(dl_c),
                                                f"attn_c_bwd{l}", bias_t=sv["tiles_t"])
        grads["sinks"][l] = dsink[:, :2, 0].reshape(8)
        grads["rel_bias"][l] = _rel_reduce(dtiles_t, f"rel_reduce{l}")[:, 0, :N_REL]
        dcum_k = _pad_lanes(_heads_from_col(dck), BLK)
        dcum_q = _pad_lanes(_heads_from_row(dcq), BLK)
        dfb, d_bfor = _forget_bwd(dcum_q, dcum_k, sv["fb"], sv["bfor"], f"forget_bwd{l}")
        grads["b_forget"][l] = d_bfor[0, :N_FORGET]
        dproj = jnp.concatenate(
            [t.astype(BF16) for t in (dqt_a.T, dk_a, dv_a, dqt_b.T, dk_b, dv_b, dqt_c.T, dk_c, dv_c)]
            + [dgates, dfb.astype(BF16)], axis=1)
        grads["w_in"][l] = _matmul(sv["h"], dproj, "tn", BF16, f"wgrad_in{l}", TILES["wgrad_in"])
        dh = _matmul(dproj, W_in[l], "nt", F32, f"dgrad_in{l}", TILES["dgrad_in"])
        dx, d_sh_m, d_sc_m, d_gm = _norm_mod_bwd(sv["x"], dh, dx1, gm, sc_m, f"norm_mix_bwd{l}")
        grads["norm_mix_g"][l] = d_gm[0]
        grads["norm_ffn_g"][l] = d_gf[0]
        grads["dmod"][l] = jnp.concatenate([d_sh_m, d_sc_m, d_g_m, d_sh_f, d_sc_f, d_g_f], axis=1)[0]
        if l == 1:
            reduce1 = _exchange_start([_w_in_pieces(grads["w_in"][1])] + rest_pieces(1), True, dx, "comm_reduce1_start")

    grad_x = dx.reshape(x.shape)

    small_shapes = dict(dmod=b_ada.shape, norm_mix_g=norm_mix_g.shape, norm_ffn_g=norm_ffn_g.shape,
                        final_norm_g=final_norm_g.shape, b_forget=b_forget.shape, sinks=sinks.shape,
                        rel_bias=rel_bias.shape)
    mine_small = _pack_small(dict(
        dmod=jnp.stack(grads["dmod"]), norm_mix_g=jnp.stack(grads["norm_mix_g"]),
        norm_ffn_g=jnp.stack(grads["norm_ffn_g"]), final_norm_g=d_final_g[0],
        b_forget=_pad_lanes(jnp.stack(grads["b_forget"]).reshape(1, -1), 128),
        sinks=_pad_lanes(jnp.stack(grads["sinks"]).reshape(1, -1), 128),
        rel_bias=_pad_lanes(jnp.stack(grads["rel_bias"]).reshape(1, -1), 4224)))
    all_small = _small_all_gather(mine_small, "comm_gather_small").reshape(N_DEV, SMALL_ROWS, 128)
    reduce_in0 = _exchange_start([_w_in_pieces(grads["w_in"][0])], True, all_small, "comm_reduce_in0_start")
    in0_started = reduce_in0[4]

    def pack_params(b_ada_, nm, nf, fn, bf, sk, rb):
        return _pack_small(dict(dmod=b_ada_, norm_mix_g=nm, norm_ffn_g=nf, final_norm_g=fn,
                                b_forget=_pad_lanes(bf.reshape(1, -1), 128), sinks=_pad_lanes(sk.reshape(1, -1), 128),
                                rel_bias=_pad_lanes(rb.reshape(1, -1), 4224)))

    small_out = _adamw(
        pack_params(b_ada, norm_mix_g, norm_ffn_g, final_norm_g, b_forget, sinks, rel_bias)[None],
        pack_params(m_b_ada, m_norm_mix_g, m_norm_ffn_g, m_final_norm_g, m_b_forget, m_sinks, m_rel_bias)[None],
        pack_params(v_b_ada, v_norm_mix_g, v_norm_ffn_g, v_final_norm_g, v_b_forget, v_sinks, v_rel_bias)[None],
        [all_small], "adamw_small", me_arr, after=in0_started)
    small_out = [_unpack_small(t[0], small_shapes) for t in small_out]

    dmod_all = all_small[:, :96].reshape(N_DEV, depth, 6 * D)
    dmod_cols = lax.dynamic_slice_in_dim(dmod_all, me * w_ada.shape[2], w_ada.shape[2], axis=2)
    d_w_ada = _ada_bwd(jnp.transpose(c_all), jnp.transpose(dmod_cols, (1, 0, 2)), "ada_bwd")

    big = {"w_ada": _adamw(w_ada, m_w_ada, v_w_ada, [d_w_ada[l:l + 1] for l in range(depth)], "adamw_w_ada", me_arr,
                           after=in0_started)}
    sent1, landed1 = _exchange_wait(reduce1, True, big["w_ada"][0], "comm_reduce1_wait")
    sent_rest0, landed_rest0 = _exchange_wait(reduce_rest0, True, landed1[0], "comm_reduce_rest0_wait")
    parts = {"w_in": [None, (landed1[0], sent1[0])]}
    for a, name in enumerate(("w_branch", "w_out", "w_ffn_in", "w_ffn_out")):
        parts[name] = [(landed_rest0[a], sent_rest0[a]), (landed1[1 + a], sent1[1 + a])]

    def update(name, w, m, v, view=lambda t: t):
        per_layer = lambda t: t.reshape(depth, -1, t.shape[-1])
        outs = _adamw(*[per_layer(view(t)) for t in (w, m, v)], parts[name], f"adamw_{name}", me_arr)
        big[name] = [view(t).reshape(w.shape) for t in outs]

    update("w_ffn_in", w_ffn_in, m_w_ffn_in, v_w_ffn_in, _w_ffn_in_view)
    update("w_ffn_out", w_ffn_out, m_w_ffn_out, v_w_ffn_out)
    update("w_branch", w_branch, m_w_branch, v_w_branch)
    update("w_out", w_out, m_w_out, v_w_out)
    sent_in0, landed_in0 = _exchange_wait(reduce_in0, True, big["w_out"][0], "comm_reduce_in0_wait")
    parts["w_in"][0] = (landed_in0[0], sent_in0[0])
    update("w_in", w_in, m_w_in, v_w_in)

    def leaf(kind, name):
        if name in big:
            return big[name][kind]
        return small_out[kind]["dmod" if name == "b_ada" else name]

    order = ["norm_mix_g", "norm_ffn_g", "w_ada", "b_ada", "w_in", "b_forget", "sinks", "rel_bias", "w_branch",
             "w_out", "w_ffn_in", "w_ffn_out", "final_norm_g"]
    return (loss, grad_x, *[leaf(0, n) for n in order], *[leaf(1, n) for n in order],
            *[leaf(2, n) for n in order], *[leaf(3, n) for n in order])
```

```python
import functools

import jax
import jax.numpy as jnp
from jax import lax
from jax.experimental import pallas as pl
from jax.experimental.pallas import tpu as pltpu

F32 = jnp.float32
BF16 = jnp.bfloat16
NEG_INF = -1e30
EPS = 1e-6
N_DEV = 8
BLK = 128
GROUP = 4 * BLK
VMEM_LIMIT_BYTES = 56 * 1024 * 1024

D_MODEL = 1024
N_QKV = 3840
N_GATES = 3072
N_MAIN = N_QKV + N_GATES
N_FORGET = 8
N_IN = N_MAIN + N_FORGET
N_INR = N_MAIN + BLK
F_COL = 2304
FFN_HIDDEN = 2816
N_REL = 257

ADAM_LR, ADAM_B1, ADAM_B2, ADAM_EPS, ADAM_WD, ADAM_STEP = 0.001, 0.9, 0.999, 1e-08, 0.01, 10

NN = (((1,), (0,)), ((), ()))
NT = (((1,), (1,)), ((), ()))
TN = (((0,), (0,)), ((), ()))
HIGHEST = lax.Precision.HIGHEST

ATTN_COLS = {"a": (0, 4, 5), "b": (6, 10, 14), "c": (18, 22, 26)}
ATTN_WINDOW = {"a": 2, "c": 5}
ATTN_BLOCKS_PER_STEP = {"a": 4, "b": GROUP // BLK, "c": 2}


def _params():
    return pltpu.CompilerParams(vmem_limit_bytes=VMEM_LIMIT_BYTES)


def _tile(n, target):
    best = None
    t = 128
    while t <= min(n, target):
        if n % t == 0:
            best = t
        t += 128
    return best if best is not None else n


def _row_tile(n, target):
    t = min(n, target)
    while n % t:
        t -= 8
    return t


TILES = {
    "proj_qkv": (1024, 1280, 1024), "proj_gates": (1024, 768, 1024), "proj_forget": (1024, 128, 1024),
    "out_proj": (1024, 512, 1024), "ffn_out": (1024, 512, 1408), "ffn_fused": (512, 1408),
    "wgrad_ffn_out": (1408, 1024, 1024), "wgrad_ffn_in": (1408, 1024, 1024), "dgrad_ffn_in": (1024, 1024, 1408),
    "wgrad_out": (1024, 1024, 1024), "dgrad_out": (1024, 1024, 1024), "wgrad_branch": (512, 1024, 1024),
    "wgrad_in": (1024, 1408, 1024), "dgrad_in": (1024, 1024, 1408),
}


def _matmul(a, b, mode, out_dtype, name, tiles, *, n=None, a_off=0, b_off=0, m=None, after=None):
    tm, tn, tk = tiles
    if mode == "nn":
        M, K = a.shape if m is None else (m, a.shape[1])
        N = b.shape[1] if n is None else n
    elif mode == "nt":
        M, K = a.shape
        N = b.shape[0] if n is None else n
    else:
        K = a.shape[0]
        M = a.shape[1] if m is None else m
        N = b.shape[1] if n is None else n
    tm = _tile(M, tm) if M % 128 == 0 else M
    tn = _tile(N, tn)
    tk = _tile(K, tk)
    nk = K // tk
    dims = {"nn": NN, "nt": NT, "tn": TN}[mode]
    if mode == "nn":
        a_spec = pl.BlockSpec((tm, tk), lambda i, j, k: (i + a_off, k))
        b_spec = pl.BlockSpec((tk, tn), lambda i, j, k: (k, j + b_off))
    elif mode == "nt":
        a_spec = pl.BlockSpec((tm, tk), lambda i, j, k: (i + a_off, k))
        b_spec = pl.BlockSpec((tn, tk), lambda i, j, k: (j + b_off, k))
    else:
        a_spec = pl.BlockSpec((tk, tm), lambda i, j, k: (k, i + a_off))
        b_spec = pl.BlockSpec((tk, tn), lambda i, j, k: (k, j + b_off))

    def body(a_ref, b_ref, *rest):
        o_ref, acc_ref = rest[-2:]
        k = pl.program_id(2)
        part = lax.dot_general(a_ref[...], b_ref[...], dims, preferred_element_type=F32)
        if nk == 1:
            o_ref[...] = part.astype(o_ref.dtype)
        else:
            @pl.when(k == 0)
            def _():
                acc_ref[...] = part

            @pl.when(k > 0)
            def _():
                acc_ref[...] += part

            @pl.when(k == nk - 1)
            def _():
                o_ref[...] = acc_ref[...].astype(o_ref.dtype)

    return pl.pallas_call(
        body, name=name,
        out_shape=jax.ShapeDtypeStruct((M, N), out_dtype),
        grid=(M // tm, N // tn, nk),
        in_specs=[a_spec, b_spec] + ([ANY] if after is not None else []),
        out_specs=pl.BlockSpec((tm, tn), lambda i, j, k: (i, j)),
        scratch_shapes=[pltpu.VMEM((tm, tn) if nk > 1 else (8, 128), F32)],
        compiler_params=_params(),
    )(a, b, *([after] if after is not None else []))


def _matmul_resid(a, b, resid, gate, name, tiles):
    M, K = a.shape
    N = b.shape[1]
    tm, tn, tk = (_tile(d, t) for d, t in zip((M, N, K), tiles))
    nk = K // tk

    def body(a_ref, b_ref, r_ref, g_ref, o_ref, s_ref, acc_ref):
        k = pl.program_id(2)
        part = jnp.dot(a_ref[...], b_ref[...], preferred_element_type=F32)

        def finish(acc):
            o_ref[...] = r_ref[...] + g_ref[...] * acc
            s_ref[...] = acc.astype(BF16)

        if nk == 1:
            finish(part)
        else:
            @pl.when(k == 0)
            def _():
                acc_ref[...] = part

            @pl.when(k > 0)
            def _():
                acc_ref[...] += part

            @pl.when(k == nk - 1)
            def _():
                finish(acc_ref[...])

    return pl.pallas_call(
        body, name=name,
        out_shape=(jax.ShapeDtypeStruct((M, N), F32), jax.ShapeDtypeStruct((M, N), BF16)),
        grid=(M // tm, N // tn, nk),
        in_specs=[pl.BlockSpec((tm, tk), lambda i, j, k: (i, k)),
                  pl.BlockSpec((tk, tn), lambda i, j, k: (k, j)),
                  pl.BlockSpec((tm, tn), lambda i, j, k: (i, j)),
                  pl.BlockSpec((1, tn), lambda i, j, k: (0, j))],
        out_specs=(pl.BlockSpec((tm, tn), lambda i, j, k: (i, j)),
                   pl.BlockSpec((tm, tn), lambda i, j, k: (i, j))),
        scratch_shapes=[pltpu.VMEM((tm, tn) if nk > 1 else (8, 128), F32)],
        compiler_params=_params(),
    )(a, b, resid, gate)


def _norm_mod_fwd(x, g, shift, scale, name):
    S, D = x.shape
    ts = _row_tile(S, 256)

    def body(x_ref, g_ref, sh_ref, sc_ref, h_ref):
        xv = x_ref[...]
        rstd = lax.rsqrt(jnp.mean(xv * xv, axis=-1, keepdims=True) + EPS)
        y = xv * rstd * g_ref[...]
        h_ref[...] = (y * (1.0 + sc_ref[...]) + sh_ref[...]).astype(BF16)

    row = pl.BlockSpec((1, D), lambda i: (0, 0))
    return pl.pallas_call(
        body, name=name, out_shape=jax.ShapeDtypeStruct((S, D), BF16), grid=(S // ts,),
        in_specs=[pl.BlockSpec((ts, D), lambda i: (i, 0)), row, row, row],
        out_specs=pl.BlockSpec((ts, D), lambda i: (i, 0)),
        compiler_params=_params(),
    )(x, g, shift, scale)


def _norm_mod_bwd(x, dh, dres, g, scale, name):
    S, D = x.shape
    ts = _row_tile(S, 256)

    def body(x_ref, dh_ref, dr_ref, g_ref, sc_ref, dx_ref, dsh_ref, dsc_ref, dg_ref):
        i = pl.program_id(0)
        xv, dhv, gv = x_ref[...], dh_ref[...], g_ref[...]
        rstd = lax.rsqrt(jnp.mean(xv * xv, axis=-1, keepdims=True) + EPS)
        xhat = xv * rstd
        dn = dhv * (1.0 + sc_ref[...])
        dxhat = dn * gv
        proj = jnp.mean(dxhat * xhat, axis=-1, keepdims=True)
        dx_ref[...] = dr_ref[...] + rstd * (dxhat - xhat * proj)
        dsh = jnp.sum(dhv, axis=0, keepdims=True)
        dsc = jnp.sum(dhv * (xhat * gv), axis=0, keepdims=True)
        dg = jnp.sum(dn * xhat, axis=0, keepdims=True)

        @pl.when(i == 0)
        def _():
            dsh_ref[...] = dsh
            dsc_ref[...] = dsc
            dg_ref[...] = dg

        @pl.when(i > 0)
        def _():
            dsh_ref[...] += dsh
            dsc_ref[...] += dsc
            dg_ref[...] += dg

    tile = pl.BlockSpec((ts, D), lambda i: (i, 0))
    row = pl.BlockSpec((1, D), lambda i: (0, 0))
    vec = jax.ShapeDtypeStruct((1, D), F32)
    return pl.pallas_call(
        body, name=name, out_shape=(jax.ShapeDtypeStruct((S, D), F32), vec, vec, vec), grid=(S // ts,),
        in_specs=[tile, tile, tile, row, row], out_specs=(tile, row, row, row),
        compiler_params=_params(),
    )(x, dh, dres, g, scale)


def _gate_bwd(dx, f, gate, name):
    S, D = dx.shape
    ts = _row_tile(S, 256)

    def body(dx_ref, f_ref, g_ref, df_ref, dg_ref):
        i = pl.program_id(0)
        dxv = dx_ref[...]
        df_ref[...] = (dxv * g_ref[...]).astype(BF16)
        dg = jnp.sum(dxv * f_ref[...].astype(F32), axis=0, keepdims=True)

        @pl.when(i == 0)
        def _():
            dg_ref[...] = dg

        @pl.when(i > 0)
        def _():
            dg_ref[...] += dg

    tile = pl.BlockSpec((ts, D), lambda i: (i, 0))
    row = pl.BlockSpec((1, D), lambda i: (0, 0))
    return pl.pallas_call(
        body, name=name,
        out_shape=(jax.ShapeDtypeStruct((S, D), BF16), jax.ShapeDtypeStruct((1, D), F32)), grid=(S // ts,),
        in_specs=[tile, tile, row], out_specs=(tile, row),
        compiler_params=_params(),
    )(dx, f, gate)


def _ffn_in_fwd(h, w_t, name):
    S, D = h.shape
    F = w_t.shape[0] // 2
    tm, tn = _tile(S, TILES["ffn_fused"][0]), _tile(F, TILES["ffn_fused"][1])
    nj = F // tn

    def body(h_ref, wg_ref, wu_ref, o_ref):
        hv = h_ref[...]
        ug = lax.dot_general(hv, wg_ref[...], NT, preferred_element_type=F32)
        uu = lax.dot_general(hv, wu_ref[...], NT, preferred_element_type=F32)
        o_ref[...] = (ug * jax.nn.sigmoid(ug) * uu).astype(BF16)

    return pl.pallas_call(
        body, name=name, out_shape=jax.ShapeDtypeStruct((S, F), BF16), grid=(nj, S // tm),
        in_specs=[pl.BlockSpec((tm, D), lambda j, i: (i, 0)),
                  pl.BlockSpec((tn, D), lambda j, i: (j, 0)),
                  pl.BlockSpec((tn, D), lambda j, i: (j + nj, 0))],
        out_specs=pl.BlockSpec((tm, tn), lambda j, i: (i, j)),
        compiler_params=_params(),
    )(h, w_t, w_t)


def _ffn_mid_bwd(h, df, w_in_t, w_out, name):
    S, D = h.shape
    F = w_in_t.shape[0] // 2
    tm, tn = _tile(S, TILES["ffn_fused"][0]), _tile(F, TILES["ffn_fused"][1])
    nj = F // tn

    def body(h_ref, df_ref, wg_ref, wu_ref, wo_ref, dg_ref, du_ref):
        hv = h_ref[...]
        ug = lax.dot_general(hv, wg_ref[...], NT, preferred_element_type=F32)
        uu = lax.dot_general(hv, wu_ref[...], NT, preferred_element_type=F32)
        dact = lax.dot_general(df_ref[...], wo_ref[...], NT, preferred_element_type=F32)
        sig = jax.nn.sigmoid(ug)
        dg_ref[...] = (dact * uu * (sig * (1.0 + ug * (1.0 - sig)))).astype(BF16)
        du_ref[...] = (dact * (ug * sig)).astype(BF16)

    out = jax.ShapeDtypeStruct((S, F), BF16)
    return pl.pallas_call(
        body, name=name, out_shape=(out, out), grid=(nj, S // tm),
        in_specs=[pl.BlockSpec((tm, D), lambda j, i: (i, 0)),
                  pl.BlockSpec((tm, D), lambda j, i: (i, 0)),
                  pl.BlockSpec((tn, D), lambda j, i: (j, 0)),
                  pl.BlockSpec((tn, D), lambda j, i: (j + nj, 0)),
                  pl.BlockSpec((tn, D), lambda j, i: (j, 0))],
        out_specs=(pl.BlockSpec((tm, tn), lambda j, i: (i, j)), pl.BlockSpec((tm, tn), lambda j, i: (i, j))),
        compiler_params=_params(),
    )(h, df, w_in_t, w_in_t, w_out)


def _merge_fwd(o_a, o_b, o_c, gates, w_branch, name, *, tm=512):
    S, W = o_a.shape
    D = w_branch.shape[2]
    tm = _row_tile(S, tm)

    def body(oa_ref, ob_ref, oc_ref, g_ref, w_ref, m_ref):
        acc = None
        for k, o_ref in enumerate((oa_ref, ob_ref, oc_ref)):
            y = jnp.dot(o_ref[...], w_ref[k], preferred_element_type=F32)
            t = jax.nn.sigmoid(g_ref[:, k * D:(k + 1) * D]) * y
            acc = t if acc is None else acc + t
        m_ref[...] = acc.astype(BF16)

    o_spec = pl.BlockSpec((tm, W), lambda i: (i, 0))
    return pl.pallas_call(
        body, name=name, out_shape=jax.ShapeDtypeStruct((S, D), BF16), grid=(S // tm,),
        in_specs=[o_spec, o_spec, o_spec, pl.BlockSpec((tm, 3 * D), lambda i: (i, 0)),
                  pl.BlockSpec((3, W, D), lambda i: (0, 0, 0))],
        out_specs=pl.BlockSpec((tm, D), lambda i: (i, 0)),
        compiler_params=_params(),
    )(o_a, o_b, o_c, gates, w_branch)


def _merge_bwd(dmerged, o_a, o_b, o_c, gates, w_branch, name, *, tm=512):
    S, W = o_a.shape
    D = w_branch.shape[2]
    tm = _row_tile(S, tm)
    n_heads = W // 64

    def body(dm_ref, oa_ref, ob_ref, oc_ref, g_ref, w_ref, dg_ref, dy_ref,
             doa_ref, dob_ref, doc_ref, dla_ref, dlb_ref, dlc_ref):
        dm = dm_ref[...]
        branches = ((oa_ref, doa_ref, dla_ref), (ob_ref, dob_ref, dlb_ref), (oc_ref, doc_ref, dlc_ref))
        for k, (o_ref, do_ref, dl_ref) in enumerate(branches):
            wk = w_ref[k]
            ov = o_ref[...]
            y = jnp.dot(ov, wk, preferred_element_type=F32)
            g = jax.nn.sigmoid(g_ref[:, k * D:(k + 1) * D])
            dy = (dm * g).astype(BF16)
            dy_ref[:, k * D:(k + 1) * D] = dy
            dg_ref[:, k * D:(k + 1) * D] = (dm * y * (g * (1.0 - g))).astype(BF16)
            do16 = lax.dot_general(dy, wk, NT, preferred_element_type=F32).astype(BF16)
            do_ref[...] = do16
            prod = do16.astype(F32) * ov.astype(F32)
            for h in range(n_heads):
                dl_ref[:, h:h + 1] = jnp.sum(prod[:, 64 * h:64 * (h + 1)], axis=1, keepdims=True)

    o_spec = pl.BlockSpec((tm, W), lambda i: (i, 0))
    wide = pl.BlockSpec((tm, 3 * D), lambda i: (i, 0))
    dl_spec = pl.BlockSpec((tm, n_heads), lambda i: (i, 0))
    o_out = jax.ShapeDtypeStruct((S, W), BF16)
    wide_out = jax.ShapeDtypeStruct((S, 3 * D), BF16)
    dl_out = jax.ShapeDtypeStruct((S, n_heads), F32)
    return pl.pallas_call(
        body, name=name, out_shape=(wide_out, wide_out, o_out, o_out, o_out, dl_out, dl_out, dl_out),
        grid=(S // tm,),
        in_specs=[pl.BlockSpec((tm, D), lambda i: (i, 0)), o_spec, o_spec, o_spec, wide,
                  pl.BlockSpec((3, W, D), lambda i: (0, 0, 0))],
        out_specs=(wide, wide, o_spec, o_spec, o_spec, dl_spec, dl_spec, dl_spec),
        compiler_params=_params(),
    )(dmerged, o_a, o_b, o_c, gates, w_branch)


def _band_mask(variant, t_abs, s_abs):
    if variant == "b":
        return s_abs <= t_abs
    qc, kc = t_abs >> 6, s_abs >> 6
    return (kc <= qc) & (kc >= qc - (2 if variant == "a" else 8))


def _attn_fwd(variant, qkv, name, *, sinks=None, slopes=None, cq_col=None, ck_row=None, bias=None):
    S = qkv.shape[0]
    nb = S // BLK
    qb, kb, vb = ATTN_COLS[variant]
    shared_kv = variant == "a"
    win = ATTN_WINDOW.get(variant)
    per_step = ATTN_BLOCKS_PER_STEP[variant]

    def body(*refs):
        if variant == "a":
            q_ref, k_ref, v_ref, sink_ref, slope_ref, o_ref, lse_ref = refs
        elif variant == "b":
            q_ref, k_ref, v_ref, cq_ref, ck_ref, o_ref, lse_ref = refs
        else:
            q_ref, k_ref, v_ref, bias_ref, o_ref, lse_ref = refs
        p = pl.program_id(0)
        lane = lax.broadcasted_iota(jnp.int32, (1, BLK), 1)

        def compute(i, rows, start, n_keys):
            n_rows = rows.stop - rows.start
            t_abs = i * BLK + lax.broadcasted_iota(jnp.int32, (n_rows, 1), 0)
            q2 = q_ref[rows, :].astype(F32) * 0.125
            k_w = k_ref[pl.ds(start, n_keys), :]
            v_w = v_ref[pl.ds(start, n_keys), :]
            s_abs = start + lax.broadcasted_iota(jnp.int32, (1, n_keys), 1)
            valid = _band_mask(variant, t_abs, s_abs)
            outs = []
            for half in (0, 1):
                hmask = (lane >= 64) if half else (lane < 64)
                qh = jnp.where(hmask, q2, 0.0)
                if shared_kv:
                    swap = (p // 2) != half
                    qh = jnp.where(swap, pltpu.roll(qh, 64, 1), qh)
                s = lax.dot_general(qh.astype(BF16), k_w, NT, preferred_element_type=F32)
                if variant == "a":
                    head = 2 * p + half
                    s = s + (-slope_ref[head]) * jnp.abs(t_abs - s_abs).astype(F32)
                elif variant == "b":
                    s = s + cq_ref[rows, half:half + 1] - ck_ref[half:half + 1, pl.ds(start, n_keys)]
                else:
                    j0 = start // BLK
                    s = s + jnp.concatenate(
                        [bias_ref[half, jnp.clip(i - j0 - b, 0, 4)] for b in range(win)], axis=1)
                s = jnp.where(valid, s, NEG_INF)
                m = jnp.max(s, axis=1, keepdims=True)
                if variant == "a":
                    m = jnp.maximum(m, sink_ref[head])
                pe = jnp.exp(s - m)
                l = jnp.sum(pe, axis=1, keepdims=True)
                if variant == "a":
                    l = l + jnp.exp(sink_ref[head] - m)
                out = jnp.dot(pe.astype(BF16), v_w, preferred_element_type=F32) / l
                if shared_kv:
                    out = jnp.where(swap, pltpu.roll(out, 64, 1), out)
                outs.append(out)
                lse_ref[rows, half:half + 1] = m + jnp.log(l)
            o_ref[rows, :] = jnp.where(lane < 64, outs[0], outs[1]).astype(BF16)

        step = pl.program_id(1)
        if variant == "b":
            for g in range(S // GROUP):
                pl.when(step == g)(functools.partial(compute, step * per_step, slice(0, GROUP), 0, (g + 1) * GROUP))
        else:
            for sub in range(per_step):
                i = step * per_step + sub
                start = jnp.clip(i - (win - 1), 0, nb - win) * BLK
                compute(i, slice(sub * BLK, (sub + 1) * BLK), pl.multiple_of(start, BLK), win * BLK)

    tq = per_step * BLK
    kv_col = (lambda p, i: (0, kb)) if shared_kv else (lambda p, i: (0, kb + p))
    vv_col = (lambda p, i: (0, vb)) if shared_kv else (lambda p, i: (0, vb + p))
    in_specs = [pl.BlockSpec((tq, BLK), lambda p, i: (i, qb + p)),
                pl.BlockSpec((S, BLK), kv_col), pl.BlockSpec((S, BLK), vv_col)]
    args = [qkv, qkv, qkv]
    if variant == "a":
        in_specs += [pl.BlockSpec(memory_space=pltpu.SMEM), pl.BlockSpec(memory_space=pltpu.SMEM)]
        args += [sinks, slopes]
    elif variant == "b":
        in_specs += [pl.BlockSpec((None, tq, 2), lambda p, i: (p, i, 0)),
                     pl.BlockSpec((None, 2, S), lambda p, i: (p, 0, 0))]
        args += [cq_col, ck_row]
    else:
        in_specs += [pl.BlockSpec((2, 5, BLK, BLK), lambda p, i: (p, 0, 0, 0))]
        args += [bias]
    return pl.pallas_call(
        body, name=name,
        out_shape=(jax.ShapeDtypeStruct((S, 512), BF16), jax.ShapeDtypeStruct((4, S, 2), F32)),
        grid=(4, nb // per_step), in_specs=in_specs,
        out_specs=(pl.BlockSpec((tq, BLK), lambda p, i: (i, p)),
                   pl.BlockSpec((None, tq, 2), lambda p, i: (p, i, 0))),
        compiler_params=_params(),
    )(*args)


def _attn_bwd(variant, qkv, do, lse_row, delta_row, name, *, sinks=None, slopes=None, cq_row=None,
              ck_col=None, bias_t=None):
    S = qkv.shape[0]
    nb = S // BLK
    qb, kb, vb = ATTN_COLS[variant]
    shared_kv = variant == "a"
    win = ATTN_WINDOW.get(variant)
    per_step = ATTN_BLOCKS_PER_STEP[variant]

    def body(*refs):
        if variant == "a":
            (q_ref, k_ref, v_ref, do_ref, lse_ref, dl_ref, sink_ref, slope_ref,
             dq_ref, dk_ref, dv_ref, ex_ref) = refs
        elif variant == "b":
            (q_ref, k_ref, v_ref, do_ref, lse_ref, dl_ref, cq_ref, ck_ref,
             dq_ref, dk_ref, dv_ref, ex_ref, dcq_ref) = refs
        else:
            (q_ref, k_ref, v_ref, do_ref, lse_ref, dl_ref, bias_ref,
             dq_ref, dk_ref, dv_ref, ex_ref) = refs
        p = pl.program_id(0)
        lane = lax.broadcasted_iota(jnp.int32, (1, BLK), 1)
        hmasks = [(lane < 64), (lane >= 64)]
        swaps = [(p // 2) != half for half in (0, 1)] if shared_kv else None

        @pl.when(pl.program_id(1) == 0)
        def _():
            dq_ref[...] = jnp.zeros_like(dq_ref)
            if variant == "b":
                dcq_ref[...] = jnp.zeros_like(dcq_ref)
            else:
                ex_ref[...] = jnp.zeros_like(ex_ref)

        def to_kv_lanes(x, h):
            x = jnp.where(hmasks[h], x, 0.0)
            if shared_kv:
                x = jnp.where(swaps[h], pltpu.roll(x, 64, 1), x)
            return x

        def compute(j, rows, start, n_q):
            n_rows = rows.stop - rows.start
            s_abs = j * BLK + lax.broadcasted_iota(jnp.int32, (n_rows, 1), 0)
            off_k = pl.multiple_of(j * BLK, BLK)
            k2 = k_ref[rows, :].astype(F32)
            v2 = v_ref[rows, :].astype(F32)
            if shared_kv:
                kv_lane = (lane >> 6) == (p // 2)
                k_src, v_src = jnp.where(kv_lane, k2, 0.0), jnp.where(kv_lane, v2, 0.0)
                k_al = [jnp.where(swaps[h], pltpu.roll(k_src, 64, 1), k_src) for h in (0, 1)]
                v_al = [jnp.where(swaps[h], pltpu.roll(v_src, 64, 1), v_src) for h in (0, 1)]
            else:
                k_al = [jnp.where(hmasks[h], k2, 0.0) for h in (0, 1)]
                v_al = [jnp.where(hmasks[h], v2, 0.0) for h in (0, 1)]
            k_al = [(t * 0.125).astype(BF16) for t in k_al]
            v_al = [t.astype(BF16) for t in v_al]
            q_w = q_ref[pl.ds(start, n_q), :]
            do_w = do_ref[pl.ds(start, n_q), :]
            t_abs = start + lax.broadcasted_iota(jnp.int32, (1, n_q), 1)
            valid = _band_mask(variant, t_abs, s_abs)
            dk_acc = dv_acc = None
            ds_both = []
            for half in (0, 1):
                s = lax.dot_general(k_al[half], q_w, NT, preferred_element_type=F32)
                if variant == "a":
                    s = s + (-slope_ref[2 * p + half]) * jnp.abs(t_abs - s_abs).astype(F32)
                elif variant == "b":
                    s = s + cq_ref[half:half + 1, pl.ds(start, n_q)] - ck_ref[rows, half:half + 1]
                else:
                    i0 = start // BLK
                    s = s + jnp.concatenate(
                        [bias_ref[half, jnp.clip(i0 + b - j, 0, 4)] for b in range(win)], axis=1)
                pr = jnp.where(valid, jnp.exp(s - lse_ref[half:half + 1, pl.ds(start, n_q)]), 0.0)
                dp = lax.dot_general(v_al[half], do_w, NT, preferred_element_type=F32)
                ds = pr * (dp - dl_ref[half:half + 1, pl.ds(start, n_q)])
                ds16 = ds.astype(BF16)
                dv_h = to_kv_lanes(jnp.dot(pr.astype(BF16), do_w, preferred_element_type=F32), half)
                dk_h = to_kv_lanes(jnp.dot(ds16, q_w, preferred_element_type=F32) * 0.125, half)
                dv_acc = dv_h if dv_acc is None else dv_acc + dv_h
                dk_acc = dk_h if dk_acc is None else dk_acc + dk_h
                ds_both.append(ds16)
                if variant == "b":
                    ex_ref[rows, half:half + 1] = -jnp.sum(ds, axis=1, keepdims=True)
                    dcq_ref[half:half + 1, pl.ds(start, n_q)] += jnp.sum(ds, axis=0, keepdims=True)
                elif variant == "c":
                    for b in range(win):
                        ex_ref[half, jnp.clip(i0 + b - j, 0, 4)] += ds[:, b * BLK:(b + 1) * BLK]
            dq_t = lax.dot_general(jnp.concatenate(k_al, axis=0), jnp.concatenate(ds_both, axis=0), TN,
                                   preferred_element_type=F32)
            dq_ref[:, pl.ds(start, n_q)] += dq_t
            if shared_kv:
                @pl.when(p == 0)
                def _():
                    dk_ref[pl.ds(off_k, n_rows), :] = dk_acc
                    dv_ref[pl.ds(off_k, n_rows), :] = dv_acc

                @pl.when(p > 0)
                def _():
                    dk_ref[pl.ds(off_k, n_rows), :] += dk_acc
                    dv_ref[pl.ds(off_k, n_rows), :] += dv_acc
            else:
                dk_ref[pl.ds(off_k, n_rows), :] = dk_acc
                dv_ref[pl.ds(off_k, n_rows), :] = dv_acc
            if variant == "a":
                for half in (0, 1):
                    p_sink = jnp.exp(sink_ref[2 * p + half] - lse_ref[half:half + 1, pl.ds(off_k, n_rows)])
                    term = p_sink * dl_ref[half:half + 1, pl.ds(off_k, n_rows)]
                    ex_ref[half:half + 1, :] += -jnp.sum(term, axis=1, keepdims=True)

        step = pl.program_id(1)
        if variant == "b":
            for g in range(S // GROUP):
                pl.when(step == g)(functools.partial(compute, step * per_step, slice(0, GROUP), g * GROUP, S - g * GROUP))
        else:
            for sub in range(per_step):
                j = step * per_step + sub
                start = jnp.clip(j, 0, nb - win) * BLK
                compute(j, slice(sub * BLK, (sub + 1) * BLK), pl.multiple_of(start, BLK), win * BLK)

    tk = per_step * BLK
    col = lambda c0: (lambda p, j: (0, c0 + p))
    kv_blk = (lambda c0: (lambda p, j: (j, c0))) if shared_kv else (lambda c0: (lambda p, j: (j, c0 + p)))
    pair = lambda p, j: (0, p)
    row_stat = pl.BlockSpec((None, 2, S), lambda p, j: (p, 0, 0))
    in_specs = [pl.BlockSpec((S, BLK), col(qb)),
                pl.BlockSpec((tk, BLK), kv_blk(kb)), pl.BlockSpec((tk, BLK), kv_blk(vb)),
                pl.BlockSpec((S, BLK), pair), row_stat, row_stat]
    args = [qkv, qkv, qkv, do, lse_row, delta_row]
    kv_width = BLK if shared_kv else 512
    kv_out = pl.BlockSpec((S, BLK), (lambda p, j: (0, 0)) if shared_kv else pair)
    out_shape = [jax.ShapeDtypeStruct((512, S), F32), jax.ShapeDtypeStruct((S, kv_width), F32),
                 jax.ShapeDtypeStruct((S, kv_width), F32)]
    out_specs = [pl.BlockSpec((BLK, S), lambda p, j: (p, 0)), kv_out, kv_out]
    if variant == "a":
        in_specs += [pl.BlockSpec(memory_space=pltpu.SMEM), pl.BlockSpec(memory_space=pltpu.SMEM)]
        args += [sinks, slopes]
        out_shape.append(jax.ShapeDtypeStruct((4, 8, BLK), F32))
        out_specs.append(pl.BlockSpec((None, 8, BLK), lambda p, j: (p, 0, 0)))
    elif variant == "b":
        in_specs += [row_stat, pl.BlockSpec((None, tk, 2), lambda p, j: (p, j, 0))]
        args += [cq_row, ck_col]
        out_shape += [jax.ShapeDtypeStruct((4, S, 2), F32), jax.ShapeDtypeStruct((4, 2, S), F32)]
        out_specs += [pl.BlockSpec((None, tk, 2), lambda p, j: (p, j, 0)), row_stat]
    else:
        in_specs += [pl.BlockSpec((2, 5, BLK, BLK), lambda p, j: (p, 0, 0, 0))]
        args += [bias_t]
        out_shape.append(jax.ShapeDtypeStruct((8, 5, BLK, BLK), F32))
        out_specs.append(pl.BlockSpec((2, 5, BLK, BLK), lambda p, j: (p, 0, 0, 0)))
    return pl.pallas_call(
        body, name=name, out_shape=tuple(out_shape), grid=(4, nb // per_step),
        in_specs=in_specs, out_specs=tuple(out_specs),
        compiler_params=_params(),
    )(*args)


def _log_sigmoid(x):
    return jnp.minimum(x, 0.0) - jnp.log(1.0 + jnp.exp(-jnp.abs(x)))


def _forget_fwd(fb, b_forget, name):
    S = fb.shape[0]
    nb = S // BLK

    def body(fb_ref, b_ref, cum_ref, carry_ref):
        i = pl.program_id(0)
        logf = _log_sigmoid(fb_ref[...] + b_ref[...])
        r = lax.broadcasted_iota(jnp.int32, (BLK, BLK), 0)
        c = lax.broadcasted_iota(jnp.int32, (BLK, BLK), 1)
        tri = (c <= r).astype(F32)

        @pl.when(i == 0)
        def _():
            carry_ref[...] = jnp.zeros_like(carry_ref)

        cum = jnp.dot(tri, logf, preferred_element_type=F32, precision=HIGHEST) + carry_ref[0:1, :]
        cum_ref[...] = cum
        carry_ref[...] = jnp.broadcast_to(cum[BLK - 1:BLK, :], carry_ref.shape)

    return pl.pallas_call(
        body, name=name, out_shape=jax.ShapeDtypeStruct((S, BLK), F32), grid=(nb,),
        in_specs=[pl.BlockSpec((BLK, BLK), lambda i: (i, 0)), pl.BlockSpec((1, BLK), lambda i: (0, 0))],
        out_specs=pl.BlockSpec((BLK, BLK), lambda i: (i, 0)),
        scratch_shapes=[pltpu.VMEM((8, BLK), F32)],
        compiler_params=_params(),
    )(fb, b_forget)


def _forget_bwd(dcum_q, dcum_k, fb, b_forget, name):
    S = fb.shape[0]
    nb = S // BLK

    def body(dq_ref, dk_ref, fb_ref, b_ref, dfb_ref, db_ref, carry_ref):
        g = pl.program_id(0)
        r = lax.broadcasted_iota(jnp.int32, (BLK, BLK), 0)
        c = lax.broadcasted_iota(jnp.int32, (BLK, BLK), 1)
        tri = (c >= r).astype(F32)

        @pl.when(g == 0)
        def _():
            carry_ref[...] = jnp.zeros_like(carry_ref)

        dcum = dq_ref[...] + dk_ref[...]
        dlogf = jnp.dot(tri, dcum, preferred_element_type=F32, precision=HIGHEST) + carry_ref[0:1, :]
        carry_ref[...] = jnp.broadcast_to(dlogf[0:1, :], carry_ref.shape)
        x = fb_ref[...] + b_ref[...]
        dfb = jnp.where(c < N_FORGET, dlogf * jax.nn.sigmoid(-x), 0.0)
        dfb_ref[...] = dfb
        db = jnp.sum(dfb, axis=0, keepdims=True)

        @pl.when(g == 0)
        def _():
            db_ref[...] = db

        @pl.when(g > 0)
        def _():
            db_ref[...] += db

    rev = pl.BlockSpec((BLK, BLK), lambda g: (nb - 1 - g, 0))
    row = pl.BlockSpec((1, BLK), lambda g: (0, 0))
    return pl.pallas_call(
        body, name=name,
        out_shape=(jax.ShapeDtypeStruct((S, BLK), F32), jax.ShapeDtypeStruct((1, BLK), F32)), grid=(nb,),
        in_specs=[rev, rev, rev, row], out_specs=(rev, row),
        scratch_shapes=[pltpu.VMEM((8, BLK), F32)],
        compiler_params=_params(),
    )(dcum_q, dcum_k, fb, b_forget)


def _skew(x, sign):
    row = lax.broadcasted_iota(jnp.int32, x.shape, 0)
    for b in range(7):
        amount = (1 << b) if sign > 0 else 256 - (1 << b)
        x = jnp.where(((row >> b) & 1) == 1, pltpu.roll(x, amount, 1), x)
    return x


def _rel_bases(rel):
    far = rel[:, 256:257]
    far127 = jnp.broadcast_to(far, (rel.shape[0], 127))
    base0 = jnp.concatenate([rel[:, 128:0:-1], far, rel[:, 255:128:-1]], axis=1)
    base1 = jnp.concatenate([rel[:, 256:128:-1], far, far127], axis=1)
    base0_t = jnp.concatenate([rel[:, 128:256], far, rel[:, 1:128]], axis=1)
    base1_t = jnp.concatenate([jnp.broadcast_to(far, (rel.shape[0], 128)), far, rel[:, 129:256]], axis=1)
    return jnp.stack([base0, base1, base0_t, base1_t], axis=1)


def _rel_expand(bases, name):
    def body(b_ref, t_ref, tt_ref):
        far = jnp.broadcast_to(b_ref[1:2, 0:1], (BLK, BLK))
        for k, out_ref in ((0, t_ref), (2, tt_ref)):
            for d in (0, 1):
                x = jnp.broadcast_to(b_ref[k + d:k + d + 1, :], (BLK, 2 * BLK))
                out_ref[d] = _skew(x, 1)[:, :BLK]
            for d in (2, 3, 4):
                out_ref[d] = far

    out = jax.ShapeDtypeStruct((8, 5, BLK, BLK), F32)
    spec = pl.BlockSpec((None, 5, BLK, BLK), lambda h: (h, 0, 0, 0))
    return pl.pallas_call(
        body, name=name, out_shape=(out, out), grid=(8,),
        in_specs=[pl.BlockSpec((None, 4, 2 * BLK), lambda h: (h, 0, 0))], out_specs=(spec, spec),
        compiler_params=_params(),
    )(bases)


def _rel_reduce(dtiles_t, name):
    def body(dt_ref, o_ref):
        zeros = jnp.zeros((BLK, BLK), F32)
        sums = []
        for d in (0, 1):
            x = _skew(jnp.concatenate([dt_ref[d], zeros], axis=1), -1)
            sums.append(jnp.broadcast_to(jnp.sum(x, axis=0, keepdims=True), (8, 2 * BLK)))
        lane = lax.broadcasted_iota(jnp.int32, (8, 2 * BLK), 1)
        main = pltpu.roll(sums[0], BLK, 1) + jnp.where(lane > BLK, sums[1], 0.0)
        far = jnp.sum(jnp.where(lane < BLK, sums[1], 0.0)[0:1], axis=1, keepdims=True)
        far = far + jnp.sum(jnp.sum(dt_ref[2] + dt_ref[3] + dt_ref[4], axis=0, keepdims=True), axis=1, keepdims=True)
        o_ref[...] = jnp.concatenate([main[0:1], jnp.broadcast_to(far, (1, BLK))], axis=1)

    return pl.pallas_call(
        body, name=name, out_shape=jax.ShapeDtypeStruct((8, 1, 3 * BLK), F32), grid=(8,),
        in_specs=[pl.BlockSpec((None, 5, BLK, BLK), lambda h: (h, 0, 0, 0))],
        out_specs=pl.BlockSpec((None, 1, 3 * BLK), lambda h: (h, 0, 0)),
        compiler_params=_params(),
    )(dtiles_t)


def _final_loss(x, target, g, name):
    S, D = x.shape
    ts = _row_tile(S, 256)

    def body(x_ref, t_ref, g_ref, dx_ref, loss_ref, dg_ref):
        i = pl.program_id(0)
        xv, gv = x_ref[...], g_ref[...]
        rstd = lax.rsqrt(jnp.mean(xv * xv, axis=-1, keepdims=True) + EPS)
        xhat = xv * rstd
        err = xhat * gv - t_ref[...]
        part = 0.5 * jnp.sum(jnp.mean(err * err, axis=-1, keepdims=True), axis=0, keepdims=True)
        dy = err / D
        dg = jnp.sum(dy * xhat, axis=0, keepdims=True)
        dxhat = dy * gv
        proj = jnp.mean(dxhat * xhat, axis=-1, keepdims=True)
        dx_ref[...] = rstd * (dxhat - xhat * proj)

        @pl.when(i == 0)
        def _():
            loss_ref[...] = jnp.broadcast_to(part, loss_ref.shape)
            dg_ref[...] = dg

        @pl.when(i > 0)
        def _():
            loss_ref[...] += jnp.broadcast_to(part, loss_ref.shape)
            dg_ref[...] += dg

    tile = pl.BlockSpec((ts, D), lambda i: (i, 0))
    row = pl.BlockSpec((1, D), lambda i: (0, 0))
    return pl.pallas_call(
        body, name=name,
        out_shape=(jax.ShapeDtypeStruct((S, D), F32), jax.ShapeDtypeStruct((8, 128), F32),
                   jax.ShapeDtypeStruct((1, D), F32)),
        grid=(S // ts,), in_specs=[tile, tile, row],
        out_specs=(tile, pl.BlockSpec((8, 128), lambda i: (0, 0)), row),
        compiler_params=_params(),
    )(x, target, g)


def _ada_fwd(c_all, w_ada, name):
    L, D, E = w_ada.shape

    def body(c_ref, w_ref, o_ref):
        cv = c_ref[...]
        cond = cv * jax.nn.sigmoid(cv)
        o_ref[...] = jnp.dot(cond, w_ref[...], preferred_element_type=F32, precision=HIGHEST)

    return pl.pallas_call(
        body, name=name, out_shape=jax.ShapeDtypeStruct((L, N_DEV, E), F32), grid=(L,),
        in_specs=[pl.BlockSpec((N_DEV, D), lambda l: (0, 0)), pl.BlockSpec((None, D, E), lambda l: (l, 0, 0))],
        out_specs=pl.BlockSpec((None, N_DEV, E), lambda l: (l, 0, 0)),
        compiler_params=_params(),
    )(c_all, w_ada)


def _ada_bwd(c_all_t, dmod, name):
    D = c_all_t.shape[0]
    L, _, E = dmod.shape

    def body(c_ref, d_ref, o_ref):
        cv = c_ref[...]
        cond = cv * jax.nn.sigmoid(cv)
        acc = None
        for b in range(N_DEV):
            t = cond[:, b:b + 1] * d_ref[b:b + 1, :]
            acc = t if acc is None else acc + t
        o_ref[...] = acc

    return pl.pallas_call(
        body, name=name, out_shape=jax.ShapeDtypeStruct((L, D, E), F32), grid=(L,),
        in_specs=[pl.BlockSpec((D, N_DEV), lambda l: (0, 0)), pl.BlockSpec((None, N_DEV, E), lambda l: (l, 0, 0))],
        out_specs=pl.BlockSpec((None, D, E), lambda l: (l, 0, 0)),
        compiler_params=_params(),
    )(c_all_t, dmod)


def _adamw(w, m, v, g_parts, name, me, after=None):
    L, R, C = w.shape
    tr = _row_tile(R, max(8, (256 * 1024 // max(C, 128)) // 8 * 8))
    nr = R // tr
    c1 = 1.0 - ADAM_B1 ** ADAM_STEP
    c2 = 1.0 - ADAM_B2 ** ADAM_STEP
    direct = [isinstance(p, tuple) for p in g_parts]
    n_in = sum(2 if d else 1 for d in direct)

    def body(me_ref, w_ref, m_ref, v_ref, *rest):
        g_refs, (go_ref, d_ref, mo_ref, vo_ref) = list(rest[:n_in]), rest[-4:]
        layer = pl.program_id(0)
        g = None
        for l in range(L):
            land_ref = g_refs.pop(0)
            own = g_refs.pop(0)[...].astype(F32) if direct[l] else None
            gl = None
            for k in range(land_ref.shape[0]):
                part = land_ref[k].astype(F32)
                if direct[l]:
                    part = jnp.where(me_ref[0] == k, own, part)
                gl = part if gl is None else gl + part
            g = gl if g is None else jnp.where(layer == l, gl, g)
        mn = ADAM_B1 * m_ref[...] + (1.0 - ADAM_B1) * g
        vn = ADAM_B2 * v_ref[...] + (1.0 - ADAM_B2) * (g * g)
        m_hat = mn / c1
        v_hat = vn / c2
        go_ref[...] = g
        d_ref[...] = -ADAM_LR * (m_hat / (jnp.sqrt(v_hat) + ADAM_EPS) + ADAM_WD * w_ref[...])
        mo_ref[...] = mn
        vo_ref[...] = vn

    def rows(l, layer, i):
        return jnp.where(layer == l, i, 0 if l > 0 else nr - 1)

    in_specs, operands = [], []
    for l, p in enumerate(g_parts):
        land, sent = p if direct[l] else (p, None)
        in_specs.append(pl.BlockSpec((land.shape[0], tr, C), lambda layer, i, me_ref, l=l: (0, rows(l, layer, i), 0)))
        operands.append(land)
        if direct[l]:
            in_specs.append(pl.BlockSpec((None, tr, C), lambda layer, i, me_ref, l=l: (me_ref[0], rows(l, layer, i), 0)))
            operands.append(sent)
    if after is not None:
        in_specs.append(ANY)
        operands.append(after)
    tile = pl.BlockSpec((None, tr, C), lambda layer, i, me_ref: (layer, i, 0))
    out = jax.ShapeDtypeStruct((L, R, C), F32)
    return pl.pallas_call(
        body, name=name, out_shape=(out, out, out, out),
        grid_spec=pltpu.PrefetchScalarGridSpec(
            num_scalar_prefetch=1, grid=(L, nr), in_specs=[tile, tile, tile] + in_specs,
            out_specs=(tile, tile, tile, tile)),
        compiler_params=_params(),
    )(me, w, m, v, *operands)


def _pair_add(pieces, recv, core, name):
    _, _, R, C = pieces.shape
    tr = _row_tile(R, max(8, (512 * 1024 // max(C, 128)) // 8 * 8))

    def body(core_ref, a_ref, b_ref, o_ref):
        o_ref[...] = (a_ref[...].astype(F32) + b_ref[...].astype(F32)).astype(BF16)

    return pl.pallas_call(
        body, name=name, out_shape=jax.ShapeDtypeStruct((4, R, C), BF16),
        grid_spec=pltpu.PrefetchScalarGridSpec(
            num_scalar_prefetch=1, grid=(4, R // tr),
            in_specs=[pl.BlockSpec((None, None, tr, C), lambda k, i, core_ref: (core_ref[0], k, i, 0)),
                      pl.BlockSpec((None, tr, C), lambda k, i, core_ref: (k, i, 0))],
            out_specs=pl.BlockSpec((None, tr, C), lambda k, i, core_ref: (k, i, 0))),
        compiler_params=_params(),
    )(core, pieces, recv)


MESH = pl.DeviceIdType.MESH
ANY = pl.BlockSpec(memory_space=pl.ANY)


def _position():
    return lax.axis_index("x"), lax.axis_index("y"), lax.axis_index("c")


def _small_all_gather(v, name):
    m_per, n = v.shape

    def body(x_ref, out_ref, send_sems, recv_sems, local_sem):
        x, y, c = _position()
        me, sibling = (x, y, c), (x, y, 1 - c)
        chips = [(1 - x, y), (x, 1 - y), (1 - x, 1 - y)]

        def rows(px, py, pc):
            return out_ref.at[pl.ds((4 * px + 2 * py + pc) * m_per, m_per), :]

        def copy(k, block, to, src=None):
            return pltpu.make_async_remote_copy(
                src_ref=rows(*block) if src is None else src, dst_ref=rows(*block),
                send_sem=send_sems.at[k], recv_sem=recv_sems.at[k], device_id=to, device_id_type=MESH)

        mine = pltpu.make_async_copy(x_ref, rows(*me), local_sem)
        mine.start()
        first = [copy(0, me, sibling, src=x_ref)]
        first += [copy(1 + j, me, (*chip, c), src=x_ref) for j, chip in enumerate(chips)]
        for cp in first:
            cp.start()
        passed = [copy(4 + j, (*chip, c), sibling) for j, chip in enumerate(chips)]
        for j, chip in enumerate(chips):
            copy(1 + j, (*chip, c), me).wait_recv()
            passed[j].start()
        copy(0, sibling, me).wait_recv()
        for j, chip in enumerate(chips):
            copy(4 + j, (*chip, 1 - c), me).wait_recv()
        for cp in first + passed:
            cp.wait_send()
        mine.wait()

    return pl.pallas_call(
        body, name=name, out_shape=jax.ShapeDtypeStruct((N_DEV * m_per, n), v.dtype),
        in_specs=[pl.BlockSpec(memory_space=pltpu.VMEM)], out_specs=pl.BlockSpec(memory_space=pltpu.VMEM),
        scratch_shapes=[pltpu.SemaphoreType.DMA((7,)), pltpu.SemaphoreType.DMA((7,)), pltpu.SemaphoreType.DMA],
    )(v)


def _big_all_gather(shards, name):
    n_arr = len(shards)

    def body(*refs):
        x_refs, out_refs = refs[:n_arr], refs[n_arr:2 * n_arr]
        send_sems, recv_sems, local_sems = refs[2 * n_arr:]
        x, y, c = _position()
        me, sibling = (x, y, c), (x, y, 1 - c)
        chips = [(1 - x, y), (x, 1 - y), (1 - x, 1 - y)]

        def slot(a, px, py, pc):
            return out_refs[a].at[4 * px + 2 * py + pc]

        def copy(a, k, block, to, src=None):
            return pltpu.make_async_remote_copy(
                src_ref=slot(a, *block) if src is None else src, dst_ref=slot(a, *block),
                send_sem=send_sems.at[a, k], recv_sem=recv_sems.at[a, k], device_id=to, device_id_type=MESH)

        mine = [pltpu.make_async_copy(x_refs[a], slot(a, *me), local_sems.at[a]) for a in range(n_arr)]
        for cp in mine:
            cp.start()
        first = []
        for j, chip in enumerate(chips):
            first += [copy(a, 1 + j, me, (*chip, c), src=x_refs[a]) for a in range(n_arr)]
        first += [copy(a, 0, me, sibling, src=x_refs[a]) for a in range(n_arr)]
        for cp in first:
            cp.start()
        passed = []
        for j, chip in enumerate(chips):
            for a in range(n_arr):
                copy(a, 1 + j, (*chip, c), me).wait_recv()
                fwd = copy(a, 4 + j, (*chip, c), sibling)
                fwd.start()
                passed.append(fwd)
        for a in range(n_arr):
            copy(a, 0, sibling, me).wait_recv()
        for j, chip in enumerate(chips):
            for a in range(n_arr):
                copy(a, 4 + j, (*chip, 1 - c), me).wait_recv()
        for cp in first + passed:
            cp.wait_send()
        for cp in mine:
            cp.wait()

    return pl.pallas_call(
        body, name=name,
        out_shape=tuple(jax.ShapeDtypeStruct((N_DEV,) + s.shape, s.dtype) for s in shards),
        in_specs=[ANY] * n_arr, out_specs=tuple([ANY] * n_arr),
        scratch_shapes=[pltpu.SemaphoreType.DMA((n_arr, 7)), pltpu.SemaphoreType.DMA((n_arr, 7)),
                        pltpu.SemaphoreType.DMA((n_arr,))],
    )(*shards)


def _sibling_exchange(pieces, name):
    n_arr = len(pieces)

    def body(*refs):
        p_refs, out_refs = refs[:n_arr], refs[n_arr:2 * n_arr]
        send_sems, recv_sems = refs[2 * n_arr:]
        x, y, c = _position()
        copies = [pltpu.make_async_remote_copy(
            src_ref=p_refs[a].at[1 - c], dst_ref=out_refs[a], send_sem=send_sems.at[a], recv_sem=recv_sems.at[a],
            device_id=(x, y, 1 - c), device_id_type=MESH) for a in range(n_arr)]
        for cp in copies:
            cp.start()
        for cp in copies:
            cp.wait()

    return pl.pallas_call(
        body, name=name,
        out_shape=tuple(jax.ShapeDtypeStruct(p.shape[1:], p.dtype) for p in pieces),
        in_specs=[ANY] * n_arr, out_specs=tuple([ANY] * n_arr),
        scratch_shapes=[pltpu.SemaphoreType.DMA((n_arr,)), pltpu.SemaphoreType.DMA((n_arr,))],
    )(*pieces)


def _chip_exchange(sums, name):
    n_arr = len(sums)

    def body(*refs):
        s_refs, out_refs = refs[:n_arr], refs[n_arr:2 * n_arr]
        send_sems, recv_sems, local_sems = refs[2 * n_arr:]
        x, y, c = _position()
        my_chip = 2 * x + y
        chips = [(1 - x, y), (x, 1 - y), (1 - x, 1 - y)]
        mine = [pltpu.make_async_copy(s_refs[a].at[my_chip], out_refs[a].at[my_chip], local_sems.at[a])
                for a in range(n_arr)]
        for cp in mine:
            cp.start()
        copies = []
        for j, (px, py) in enumerate(chips):
            copies += [pltpu.make_async_remote_copy(
                src_ref=s_refs[a].at[2 * px + py], dst_ref=out_refs[a].at[my_chip],
                send_sem=send_sems.at[a, j], recv_sem=recv_sems.at[a, j],
                device_id=(px, py, c), device_id_type=MESH) for a in range(n_arr)]
        for cp in copies:
            cp.start()
        for j, (px, py) in enumerate(chips):
            for a in range(n_arr):
                pltpu.make_async_remote_copy(
                    src_ref=s_refs[a].at[my_chip], dst_ref=out_refs[a].at[2 * px + py],
                    send_sem=send_sems.at[a, j], recv_sem=recv_sems.at[a, j],
                    device_id=(px, py, c), device_id_type=MESH).wait_recv()
        for cp in copies:
            cp.wait_send()
        for cp in mine:
            cp.wait()

    return pl.pallas_call(
        body, name=name,
        out_shape=tuple(jax.ShapeDtypeStruct(s.shape, s.dtype) for s in sums),
        in_specs=[ANY] * n_arr, out_specs=tuple([ANY] * n_arr),
        scratch_shapes=[pltpu.SemaphoreType.DMA((n_arr, 3)), pltpu.SemaphoreType.DMA((n_arr, 3)),
                        pltpu.SemaphoreType.DMA((n_arr,))],
    )(*sums)


HBM = pl.BlockSpec(memory_space=pltpu.HBM)
SEM = pl.BlockSpec(memory_space=pltpu.SEMAPHORE)
EFFECT = pltpu.SideEffectType.DATAFLOW_SIDE_EFFECTING
RELATIONS = [(rx, ry, rc) for rx in (0, 1) for ry in (0, 1) for rc in (0, 1)][1:]


SAME_CORE = [r for r in RELATIONS if r == (0, 0, 1) or r[2] == 0]


def _exchange_copies(src_refs, land_refs, send_sems, recv_sems, scatter, receive_side, relations):
    x, y, c = _position()
    me = 4 * x + 2 * y + c
    copies = []
    for k, (rx, ry, rc) in enumerate(relations):
        peer = ((1 - x) if rx else x, (1 - y) if ry else y, (1 - c) if rc else c)
        peer_index = 4 * peer[0] + 2 * peer[1] + peer[2]
        for a, (src, land) in enumerate(zip(src_refs, land_refs)):
            copies.append(pltpu.make_async_remote_copy(
                src_ref=src.at[peer_index] if scatter else src,
                dst_ref=land.at[peer_index if receive_side else me],
                send_sem=send_sems.at[a * len(relations) + k], recv_sem=recv_sems.at[a * len(relations) + k],
                device_id=peer, device_id_type=MESH))
    return copies


def _exchange_start(srcs, scatter, after, name, relations=RELATIONS):
    n = len(srcs)
    land_shapes = [(s.shape if scatter else (N_DEV,) + s.shape) for s in srcs]

    def body(*refs):
        src_refs, land_refs = refs[:n], refs[n:2 * n]
        send_sems, recv_sems = refs[2 * n + 1], refs[2 * n + 2]
        token = refs[-1]
        for cp in _exchange_copies(src_refs, land_refs, send_sems, recv_sems, scatter, False, relations):
            cp.start()
        token[...] = jnp.zeros_like(token)

    sems = pltpu.SemaphoreType.DMA((n * len(relations),))
    outs = pl.pallas_call(
        body, name=name,
        out_shape=(sems, sems, *[pltpu.HBM(s.shape, s.dtype) for s in srcs],
                   *[pltpu.HBM(shape, s.dtype) for shape, s in zip(land_shapes, srcs)],
                   jax.ShapeDtypeStruct((8, 128), F32)),
        in_specs=[HBM] * (2 * n) + [ANY],
        out_specs=(SEM, SEM, *[HBM] * (2 * n), pl.BlockSpec(memory_space=pltpu.VMEM)),
        input_output_aliases={a: 2 + a for a in range(2 * n)},
        compiler_params=pltpu.CompilerParams(has_side_effects=EFFECT),
    )(*[pltpu.with_memory_space_constraint(s, pltpu.HBM) for s in srcs],
      *[pltpu.with_memory_space_constraint(lax.empty(shape, s.dtype), pltpu.HBM)
        for shape, s in zip(land_shapes, srcs)], after)
    return outs[0], outs[1], outs[2:2 + n], outs[2 + n:2 + 2 * n], outs[-1]


def _exchange_wait(started, scatter, after, name, relations=RELATIONS):
    send_sems, recv_sems, srcs, lands, _ = started
    n = len(srcs)

    def body(*refs):
        src_refs, land_refs = refs[:n], refs[n:2 * n]
        send_sems, recv_sems = refs[2 * n], refs[2 * n + 1]
        copies = _exchange_copies(src_refs, land_refs, send_sems, recv_sems, scatter, True, relations)
        for cp in copies:
            cp.wait_send()
        for cp in copies:
            cp.wait_recv()

    outs = pl.pallas_call(
        body, name=name,
        out_shape=(*[pltpu.HBM(s.shape, s.dtype) for s in srcs], *[pltpu.HBM(t.shape, t.dtype) for t in lands]),
        in_specs=[HBM] * (2 * n) + [SEM, SEM, ANY], out_specs=tuple([HBM] * (2 * n)),
        input_output_aliases={a: a for a in range(2 * n)},
        compiler_params=pltpu.CompilerParams(has_side_effects=EFFECT),
    )(*srcs, *lands, send_sems, recv_sems, after)
    return outs[:n], outs[n:]


def _sibling_forward(lands, name):
    n = len(lands)

    def body(*refs):
        in_refs, out_refs = refs[:n], refs[n:2 * n]
        send_sems, recv_sems = refs[2 * n:]
        x, y, c = _position()
        copies, arrivals = [], []
        for j, (px, py) in enumerate([(1 - x, y), (x, 1 - y), (1 - x, 1 - y)]):
            held, coming = 4 * px + 2 * py + c, 4 * px + 2 * py + (1 - c)
            for a in range(n):
                sems = dict(send_sem=send_sems.at[a, j], recv_sem=recv_sems.at[a, j], device_id=(x, y, 1 - c),
                            device_id_type=MESH)
                copies.append(pltpu.make_async_remote_copy(
                    src_ref=in_refs[a].at[held], dst_ref=out_refs[a].at[held], **sems))
                arrivals.append(pltpu.make_async_remote_copy(
                    src_ref=in_refs[a].at[held], dst_ref=out_refs[a].at[coming], **sems))
        for cp in copies:
            cp.start()
        for cp in copies:
            cp.wait_send()
        for cp in arrivals:
            cp.wait_recv()

    return pl.pallas_call(
        body, name=name, out_shape=tuple(jax.ShapeDtypeStruct(t.shape, t.dtype) for t in lands),
        in_specs=[ANY] * n, out_specs=tuple([ANY] * n), input_output_aliases={a: a for a in range(n)},
        scratch_shapes=[pltpu.SemaphoreType.DMA((n, 3)), pltpu.SemaphoreType.DMA((n, 3))],
    )(*lands)


W_IN_SHARD = N_IN // N_DEV
F_SHARD = F_COL // W_IN_SHARD
F_LO = F_COL - F_SHARD * W_IN_SHARD


def _w_ffn_in_view(w):
    return jnp.transpose(w, (0, 2, 1))


def _w_in_rearranged(g):
    parts = [g[d] for d in range(N_DEV)]
    with_f = parts[F_SHARD]
    parts[F_SHARD:F_SHARD + 1] = [with_f[:, :F_LO], with_f[:, F_LO + N_FORGET:]]
    parts += [with_f[:, F_LO:F_LO + N_FORGET], jnp.zeros((with_f.shape[0], BLK - N_FORGET), with_f.dtype)]
    return jnp.concatenate(parts, axis=1)


def _w_in_pieces(dw_r):
    def original(lo, hi):
        shift = 0 if hi <= F_COL else N_FORGET
        return dw_r[:, lo - shift:hi - shift]

    pieces = []
    for d in range(N_DEV):
        lo, hi = d * W_IN_SHARD, (d + 1) * W_IN_SHARD
        if d == F_SHARD:
            pieces.append(jnp.concatenate([original(lo, F_COL), dw_r[:, N_MAIN:N_MAIN + N_FORGET],
                                           original(F_COL + N_FORGET, hi)], axis=1))
        else:
            pieces.append(original(lo, hi))
    return jnp.stack(pieces)


def _row_pieces(dw):
    return dw.reshape(N_DEV, dw.shape[0] // N_DEV, dw.shape[1])


def _branch_pieces(dw):
    k, w, d = dw.shape
    return jnp.transpose(dw.reshape(k, w, N_DEV, d // N_DEV), (2, 0, 1, 3)).reshape(N_DEV, k * w, d // N_DEV)


def _pair_major(p8):
    return jnp.stack([p8[0::2], p8[1::2]])


def _pairs_col(a):
    return jnp.transpose(a.reshape(a.shape[0], 4, 2), (1, 0, 2))


def _pairs_row(a):
    return jnp.transpose(a.reshape(a.shape[0], 4, 2), (1, 2, 0))


def _heads_from_col(a):
    return jnp.transpose(a, (1, 0, 2)).reshape(a.shape[1], 8)


def _heads_from_row(a):
    return jnp.transpose(a, (2, 0, 1)).reshape(a.shape[2], 8)


def _pad_lanes(a, n):
    return jnp.pad(a, [(0, 0)] * (a.ndim - 1) + [(0, n - a.shape[-1])])


SMALL_SEGMENTS = (("dmod", 2 * 6 * D_MODEL), ("norm_mix_g", 2 * D_MODEL), ("norm_ffn_g", 2 * D_MODEL),
                  ("final_norm_g", D_MODEL), ("b_forget", 128), ("sinks", 128), ("rel_bias", 4224))
SMALL_ROWS = 176


def _pack_small(parts):
    flat = [_pad_lanes(parts[name].reshape(1, -1), size) for name, size in SMALL_SEGMENTS]
    total = sum(size for _, size in SMALL_SEGMENTS)
    flat.append(jnp.zeros((1, SMALL_ROWS * 128 - total), F32))
    return jnp.concatenate(flat, axis=1).reshape(SMALL_ROWS, 128)


def _unpack_small(packed, shapes):
    flat = packed.reshape(-1)
    out, pos = {}, 0
    for name, size in SMALL_SEGMENTS:
        shape = shapes[name]
        count = 1
        for d in shape:
            count *= d
        out[name] = flat[pos:pos + count].reshape(shape)
        pos += size
    return out


def kernel(x, c, norm_mix_g, norm_ffn_g, w_ada, b_ada, w_in, b_forget, sinks, rel_bias, w_branch, w_out, w_ffn_in, w_ffn_out, final_norm_g, loss_target, m_norm_mix_g, m_norm_ffn_g, m_w_ada, m_b_ada, m_w_in, m_b_forget, m_sinks, m_rel_bias, m_w_branch, m_w_out, m_w_ffn_in, m_w_ffn_out, m_final_norm_g, v_norm_mix_g, v_norm_ffn_g, v_w_ada, v_b_ada, v_w_in, v_b_forget, v_sinks, v_rel_bias, v_w_branch, v_w_out, v_w_ffn_in, v_w_ffn_out, v_final_norm_g):
    depth = w_in.shape[0]
    S, D = x.shape[1], x.shape[2]
    assert S % GROUP == 0 and S >= ATTN_WINDOW["c"] * BLK
    px, py, pc = _position()
    me = 4 * px + 2 * py + pc
    x0 = x[0]

    assert depth == 2
    big_weights = (w_in, w_branch, w_out, w_ffn_in, w_ffn_out)
    me_arr = me.astype(jnp.int32).reshape(1)

    def slabs(landed, mine):
        return [jnp.where(me == d, mine, landed[d]) for d in range(N_DEV)]

    def rest_matrices(g_branch, g_out, g_fin, g_fout):
        return (jnp.transpose(jnp.stack(g_branch), (1, 2, 0, 3)).reshape(3, 512, D),
                jnp.concatenate(g_out, axis=0), jnp.concatenate(g_fin, axis=0), jnp.concatenate(g_fout, axis=0))

    def finish_gather(started, after, name):
        mine, landed = _exchange_wait(started, False, after, f"{name}_wait", SAME_CORE)
        landed = _sibling_forward(landed, f"{name}_forward")
        return [slabs(t, s) for t, s in zip(landed, mine)]

    w_fin_t = _w_ffn_in_view(w_ffn_in)
    shards = [[t.astype(BF16) for t in (w_in[l], w_branch[l], w_out[l], w_fin_t[l], w_ffn_out[l])]
              for l in range(depth)]
    gathered_in0 = _big_all_gather(shards[0][:1], "comm_gather_w_in0")[0]
    gather_rest0 = _exchange_start(shards[0][1:], False, gathered_in0, "comm_gather_rest0_start", SAME_CORE)
    gather1 = _exchange_start(shards[1], False, gather_rest0[4], "comm_gather_weights1_start", SAME_CORE)
    W_in, W_branch, W_out, W_fin, W_fout = ([None, None] for _ in range(5))
    W_in[0] = _w_in_rearranged(gathered_in0)

    c_all = _small_all_gather(c.reshape(8, 128), "comm_gather_c").reshape(N_DEV, D)
    mod_cols = _ada_fwd(c_all, w_ada, "ada_fwd")
    mod_all = _small_all_gather(mod_cols.reshape(-1, 128), "comm_gather_mod")
    mod_all = mod_all.reshape(N_DEV, depth, N_DEV, w_ada.shape[2])
    mod_mine = lax.dynamic_index_in_dim(mod_all, me, axis=2, keepdims=False)
    mod = jnp.transpose(mod_mine, (1, 0, 2)).reshape(depth, 6 * D) + b_ada + gather1[4][0:1, 0:1]
    mods = [[mod[l:l + 1, k * D:(k + 1) * D] for k in range(6)] for l in range(depth)]

    slopes = jnp.exp2(-jnp.arange(1, 9, dtype=F32))
    saved = []
    xs = x0
    for l in range(depth):
        if l == 1:
            g_in1, *g_rest1 = finish_gather(gather1, xs, "comm_gather_weights1")
            W_in[1] = _w_in_rearranged(g_in1)
            W_branch[1], W_out[1], W_fin[1], W_fout[1] = rest_matrices(*g_rest1)
        sh_m, sc_m, g_m, sh_f, sc_f, g_f = mods[l]
        gm, gf = norm_mix_g[l:l + 1], norm_ffn_g[l:l + 1]
        bfor = _pad_lanes(b_forget[l:l + 1], BLK)
        h = _norm_mod_fwd(xs, gm, sh_m, sc_m, f"norm_mix_fwd{l}")
        qkv = _matmul(h, W_in[l], "nn", BF16, f"proj_qkv{l}", TILES["proj_qkv"], n=N_QKV)
        gates = _matmul(h, W_in[l], "nn", F32, f"proj_gates{l}", TILES["proj_gates"], n=N_GATES,
                        b_off=N_QKV // TILES["proj_gates"][1])
        fb = _matmul(h, W_in[l], "nn", F32, f"proj_forget{l}", TILES["proj_forget"], n=BLK, b_off=N_MAIN // BLK)
        cum = _forget_fwd(fb, bfor, f"forget_fwd{l}")[:, :N_FORGET]
        cum_col, cum_row = _pairs_col(cum), _pairs_row(cum)
        tiles, tiles_t = _rel_expand(_rel_bases(rel_bias[l]), f"rel_expand{l}")
        o_a, lse_a = _attn_fwd("a", qkv, f"attn_a_fwd{l}", sinks=sinks[l], slopes=slopes)
        o_b, lse_b = _attn_fwd("b", qkv, f"attn_b_fwd{l}", cq_col=cum_col, ck_row=cum_row)
        o_c, lse_c = _attn_fwd("c", qkv, f"attn_c_fwd{l}", bias=tiles)
        if l == 0:
            W_branch[0], W_out[0], W_fin[0], W_fout[0] = rest_matrices(*finish_gather(gather_rest0, o_c, "comm_gather_rest0"))
        merged = _merge_fwd(o_a, o_b, o_c, gates, W_branch[l], f"merge_fwd{l}")
        x1, mix = _matmul_resid(merged, W_out[l], xs, g_m, f"out_proj{l}", TILES["out_proj"])
        h2 = _norm_mod_fwd(x1, gf, sh_f, sc_f, f"norm_ffn_fwd{l}")
        act = _ffn_in_fwd(h2, W_fin[l], f"ffn_in_fwd{l}")
        x2, ffn = _matmul_resid(act, W_fout[l], x1, g_f, f"ffn_out{l}", TILES["ffn_out"])
        saved.append(dict(x=xs, h=h, qkv=qkv, gates=gates, fb=fb, bfor=bfor, cum_col=cum_col, cum_row=cum_row,
                          tiles_t=tiles_t, o=(o_a, o_b, o_c), lse=(lse_a, lse_b, lse_c), merged=merged, mix=mix,
                          x1=x1, h2=h2, act=act, ffn=ffn))
        xs = x2

    dx, loss_tile, d_final_g = _final_loss(xs, loss_target[0], final_norm_g.reshape(1, D), "final_loss")
    loss = lax.psum(loss_tile[0, 0], ("x", "y", "c"))

    grads = {k: [None] * depth for k in ("w_in", "w_branch", "w_out", "w_ffn_in", "w_ffn_out", "norm_mix_g",
                                          "norm_ffn_g", "b_forget", "sinks", "rel_bias", "dmod")}
    def rest_pieces(l):
        return [_branch_pieces(grads["w_branch"][l]), _row_pieces(grads["w_out"][l]),
                _row_pieces(grads["w_ffn_in"][l]), _row_pieces(grads["w_ffn_out"][l])]

    reduce1 = reduce_rest0 = reduce_in0 = None
    for l in reversed(range(depth)):
        sv = saved[l]
        sh_m, sc_m, g_m, sh_f, sc_f, g_f = mods[l]
        if l == 0:
            g_f = g_f + reduce1[4][0:1, 0:1]
        gm, gf = norm_mix_g[l:l + 1], norm_ffn_g[l:l + 1]
        df, d_g_f = _gate_bwd(dx, sv["ffn"], g_f, f"ffn_gate_bwd{l}")
        du_g, du_u = _ffn_mid_bwd(sv["h2"], df, W_fin[l], W_fout[l], f"ffn_mid_bwd{l}")
        du = jnp.concatenate([du_g, du_u], axis=1)
        grads["w_ffn_out"][l] = _matmul(sv["act"], df, "tn", BF16, f"wgrad_ffn_out{l}", TILES["wgrad_ffn_out"])
        grads["w_ffn_in"][l] = _matmul(du, sv["h2"], "tn", BF16, f"wgrad_ffn_in{l}", TILES["wgrad_ffn_in"])
        dh2 = _matmul(du, W_fin[l], "nn", F32, f"dgrad_ffn_in{l}", TILES["dgrad_ffn_in"])
        dx1, d_sh_f, d_sc_f, d_gf = _norm_mod_bwd(sv["x1"], dh2, dx, gf, sc_f, f"norm_ffn_bwd{l}")
        dmix, d_g_m = _gate_bwd(dx1, sv["mix"], g_m, f"mix_gate_bwd{l}")
        grads["w_out"][l] = _matmul(sv["merged"], dmix, "tn", BF16, f"wgrad_out{l}", TILES["wgrad_out"])
        dmerged = _matmul(dmix, W_out[l], "nt", F32, f"dgrad_out{l}", TILES["dgrad_out"])
        o_a, o_b, o_c = sv["o"]
        dgates, dy, do_a, do_b, do_c, dl_a, dl_b, dl_c = _merge_bwd(
            dmerged, o_a, o_b, o_c, sv["gates"], W_branch[l], f"merge_bwd{l}")
        dwb = [_matmul(o_k, dy, "tn", BF16, f"wgrad_branch{l}_{k}", TILES["wgrad_branch"], n=D,
                       b_off=k * (D // TILES["wgrad_branch"][1])) for k, o_k in enumerate((o_a, o_b, o_c))]
        grads["w_branch"][l] = jnp.stack(dwb)
        lse_rows = [_pairs_row(_heads_from_col(t)) for t in sv["lse"]]
        if l == 0:
            reduce_rest0 = _exchange_start(rest_pieces(0), True, dy, "comm_reduce_rest0_start")
            lse_rows = [t + reduce_rest0[4][0:1, 0:1] for t in lse_rows]
        dqt_a, dk_a, dv_a, dsink = _attn_bwd("a", sv["qkv"], do_a, lse_rows[0], _pairs_row(dl_a), f"attn_a_bwd{l}",
                                             sinks=sinks[l], slopes=slopes)
        dqt_b, dk_b, dv_b, dck, dcq = _attn_bwd("b", sv["qkv"], do_b, lse_rows[1], _pairs_row(dl_b),
                                                f"attn_b_bwd{l}", cq_row=sv["cum_row"], ck_col=sv["cum_col"])
        dqt_c, dk_c, dv_c, dtiles_t = _attn_bwd("c", sv["qkv"], do_c, lse_rows[2], _pairs_row(dl_c),
                                                f"attn_c_bwd{l}", bias_t=sv["tiles_t"])
        grads["sinks"][l] = dsink[:, :2, 0].reshape(8)
        grads["rel_bias"][l] = _rel_reduce(dtiles_t, f"rel_reduce{l}")[:, 0, :N_REL]
        dcum_k = _pad_lanes(_heads_from_col(dck), BLK)
        dcum_q = _pad_lanes(_heads_from_row(dcq), BLK)
        dfb, d_bfor = _forget_bwd(dcum_q, dcum_k, sv["fb"], sv["bfor"], f"forget_bwd{l}")
        grads["b_forget"][l] = d_bfor[0, :N_FORGET]
        dproj = jnp.concatenate(
            [t.astype(BF16) for t in (dqt_a.T, dk_a, dv_a, dqt_b.T, dk_b, dv_b, dqt_c.T, dk_c, dv_c)]
            + [dgates, dfb.astype(BF16)], axis=1)
        grads["w_in"][l] = _matmul(sv["h"], dproj, "tn", BF16, f"wgrad_in{l}", TILES["wgrad_in"])
        dh = _matmul(dproj, W_in[l], "nt", F32, f"dgrad_in{l}", TILES["dgrad_in"])
        dx, d_sh_m, d_sc_m, d_gm = _norm_mod_bwd(sv["x"], dh, dx1, gm, sc_m, f"norm_mix_bwd{l}")
        grads["norm_mix_g"][l] = d_gm[0]
        grads["norm_ffn_g"][l] = d_gf[0]
        grads["dmod"][l] = jnp.concatenate([d_sh_m, d_sc_m, d_g_m, d_sh_f, d_sc_f, d_g_f], axis=1)[0]
        if l == 1:
            reduce1 = _exchange_start([_w_in_pieces(grads["w_in"][1])] + rest_pieces(1), True, dx, "comm_reduce1_start")

    grad_x = dx.reshape(x.shape)

    small_shapes = dict(dmod=b_ada.shape, norm_mix_g=norm_mix_g.shape, norm_ffn_g=norm_ffn_g.shape,
                        final_norm_g=final_norm_g.shape, b_forget=b_forget.shape, sinks=sinks.shape,
                        rel_bias=rel_bias.shape)
    mine_small = _pack_small(dict(
        dmod=jnp.stack(grads["dmod"]), norm_mix_g=jnp.stack(grads["norm_mix_g"]),
        norm_ffn_g=jnp.stack(grads["norm_ffn_g"]), final_norm_g=d_final_g[0],
        b_forget=_pad_lanes(jnp.stack(grads["b_forget"]).reshape(1, -1), 128),
        sinks=_pad_lanes(jnp.stack(grads["sinks"]).reshape(1, -1), 128),
        rel_bias=_pad_lanes(jnp.stack(grads["rel_bias"]).reshape(1, -1), 4224)))
    all_small = _small_all_gather(mine_small, "comm_gather_small").reshape(N_DEV, SMALL_ROWS, 128)
    reduce_in0 = _exchange_start([_w_in_pieces(grads["w_in"][0])], True, all_small, "comm_reduce_in0_start")
    in0_started = reduce_in0[4]

    def pack_params(b_ada_, nm, nf, fn, bf, sk, rb):
        return _pack_small(dict(dmod=b_ada_, norm_mix_g=nm, norm_ffn_g=nf, final_norm_g=fn,
                                b_forget=_pad_lanes(bf.reshape(1, -1), 128), sinks=_pad_lanes(sk.reshape(1, -1), 128),
                                rel_bias=_pad_lanes(rb.reshape(1, -1), 4224)))

    small_out = _adamw(
        pack_params(b_ada, norm_mix_g, norm_ffn_g, final_norm_g, b_forget, sinks, rel_bias)[None],
        pack_params(m_b_ada, m_norm_mix_g, m_norm_ffn_g, m_final_norm_g, m_b_forget, m_sinks, m_rel_bias)[None],
        pack_params(v_b_ada, v_norm_mix_g, v_norm_ffn_g, v_final_norm_g, v_b_forget, v_sinks, v_rel_bias)[None],
        [all_small], "adamw_small", me_arr, after=in0_started)
    small_out = [_unpack_small(t[0], small_shapes) for t in small_out]

    dmod_all = all_small[:, :96].reshape(N_DEV, depth, 6 * D)
    dmod_cols = lax.dynamic_slice_in_dim(dmod_all, me * w_ada.shape[2], w_ada.shape[2], axis=2)
    d_w_ada = _ada_bwd(jnp.transpose(c_all), jnp.transpose(dmod_cols, (1, 0, 2)), "ada_bwd")

    big = {"w_ada": _adamw(w_ada, m_w_ada, v_w_ada, [d_w_ada[l:l + 1] for l in range(depth)], "adamw_w_ada", me_arr,
                           after=in0_started)}
    sent1, landed1 = _exchange_wait(reduce1, True, big["w_ada"][0], "comm_reduce1_wait")
    sent_rest0, landed_rest0 = _exchange_wait(reduce_rest0, True, landed1[0], "comm_reduce_rest0_wait")
    parts = {"w_in": [None, (landed1[0], sent1[0])]}
    for a, name in enumerate(("w_branch", "w_out", "w_ffn_in", "w_ffn_out")):
        parts[name] = [(landed_rest0[a], sent_rest0[a]), (landed1[1 + a], sent1[1 + a])]

    def update(name, w, m, v, view=lambda t: t):
        per_layer = lambda t: t.reshape(depth, -1, t.shape[-1])
        outs = _adamw(*[per_layer(view(t)) for t in (w, m, v)], parts[name], f"adamw_{name}", me_arr)
        big[name] = [view(t).reshape(w.shape) for t in outs]

    update("w_ffn_in", w_ffn_in, m_w_ffn_in, v_w_ffn_in, _w_ffn_in_view)
    update("w_ffn_out", w_ffn_out, m_w_ffn_out, v_w_ffn_out)
    update("w_branch", w_branch, m_w_branch, v_w_branch)
    update("w_out", w_out, m_w_out, v_w_out)
    sent_in0, landed_in0 = _exchange_wait(reduce_in0, True, big["w_out"][0], "comm_reduce_in0_wait")
    parts["w_in"][0] = (landed_in0[0], sent_in0[0])
    update("w_in", w_in, m_w_in, v_w_in)

    def leaf(kind, name):
        if name in big:
            return big[name][kind]
        return small_out[kind]["dmod" if name == "b_ada" else name]

    order = ["norm_mix_g", "norm_ffn_g", "w_ada", "b_ada", "w_in", "b_forget", "sinks", "rel_bias", "w_branch",
             "w_out", "w_ffn_in", "w_ffn_out", "final_norm_g"]
    return (loss, grad_x, *[leaf(0, n) for n in order], *[leaf(1, n) for n in order],
            *[leaf(2, n) for n in order], *[leaf(3, n) for n in order])
```

```python
import functools

import jax
import jax.numpy as jnp
from jax import lax
from jax.experimental import pallas as pl
from jax.experimental.pallas import tpu as pltpu

F32 = jnp.float32
BF16 = jnp.bfloat16
NEG_INF = -1e30
EPS = 1e-6
N_DEV = 8
BLK = 128
GROUP = 4 * BLK
VMEM_LIMIT_BYTES = 56 * 1024 * 1024

D_MODEL = 1024
N_QKV = 3840
N_GATES = 3072
N_MAIN = N_QKV + N_GATES
N_FORGET = 8
N_IN = N_MAIN + N_FORGET
N_INR = N_MAIN + BLK
F_COL = 2304
FFN_HIDDEN = 2816
N_REL = 257

ADAM_LR, ADAM_B1, ADAM_B2, ADAM_EPS, ADAM_WD, ADAM_STEP = 0.001, 0.9, 0.999, 1e-08, 0.01, 10

NN = (((1,), (0,)), ((), ()))
NT = (((1,), (1,)), ((), ()))
TN = (((0,), (0,)), ((), ()))
HIGHEST = lax.Precision.HIGHEST

ATTN_COLS = {"a": (0, 4, 5), "b": (6, 10, 14), "c": (18, 22, 26)}
ATTN_WINDOW = {"a": 2, "c": 5}
ATTN_BLOCKS_PER_STEP = {"a": 4, "b": GROUP // BLK, "c": 2}


def _params():
    return pltpu.CompilerParams(vmem_limit_bytes=VMEM_LIMIT_BYTES)


def _tile(n, target):
    best = None
    t = 128
    while t <= min(n, target):
        if n % t == 0:
            best = t
        t += 128
    return best if best is not None else n


def _row_tile(n, target):
    t = min(n, target)
    while n % t:
        t -= 8
    return t


TILES = {
    "proj_qkv": (1024, 1280, 1024), "proj_gates": (1024, 768, 1024), "proj_forget": (1024, 128, 1024),
    "out_proj": (1024, 512, 1024), "ffn_out": (1024, 512, 2816), "ffn_fused": (512, 1408),
    "wgrad_ffn_out": (1408, 1024, 1024), "wgrad_ffn_in": (1408, 1024, 1024), "dgrad_ffn_in": (1024, 1024, 1408),
    "wgrad_out": (1024, 1024, 1024), "dgrad_out": (1024, 1024, 1024), "wgrad_branch": (512, 1024, 1024),
    "wgrad_in": (1024, 1408, 1024), "dgrad_in": (1024, 1024, 1408),
}


def _matmul(a, b, mode, out_dtype, name, tiles, *, n=None, a_off=0, b_off=0, m=None, after=None):
    tm, tn, tk = tiles
    if mode == "nn":
        M, K = a.shape if m is None else (m, a.shape[1])
        N = b.shape[1] if n is None else n
    elif mode == "nt":
        M, K = a.shape
        N = b.shape[0] if n is None else n
    else:
        K = a.shape[0]
        M = a.shape[1] if m is None else m
        N = b.shape[1] if n is None else n
    tm = _tile(M, tm) if M % 128 == 0 else M
    tn = _tile(N, tn)
    tk = _tile(K, tk)
    nk = K // tk
    dims = {"nn": NN, "nt": NT, "tn": TN}[mode]
    if mode == "nn":
        a_spec = pl.BlockSpec((tm, tk), lambda i, j, k: (i + a_off, k))
        b_spec = pl.BlockSpec((tk, tn), lambda i, j, k: (k, j + b_off))
    elif mode == "nt":
        a_spec = pl.BlockSpec((tm, tk), lambda i, j, k: (i + a_off, k))
        b_spec = pl.BlockSpec((tn, tk), lambda i, j, k: (j + b_off, k))
    else:
        a_spec = pl.BlockSpec((tk, tm), lambda i, j, k: (k, i + a_off))
        b_spec = pl.BlockSpec((tk, tn), lambda i, j, k: (k, j + b_off))

    def body(a_ref, b_ref, *rest):
        o_ref, acc_ref = rest[-2:]
        k = pl.program_id(2)
        part = lax.dot_general(a_ref[...], b_ref[...], dims, preferred_element_type=F32)
        if nk == 1:
            o_ref[...] = part.astype(o_ref.dtype)
        else:
            @pl.when(k == 0)
            def _():
                acc_ref[...] = part

            @pl.when(k > 0)
            def _():
                acc_ref[...] += part

            @pl.when(k == nk - 1)
            def _():
                o_ref[...] = acc_ref[...].astype(o_ref.dtype)

    return pl.pallas_call(
        body, name=name,
        out_shape=jax.ShapeDtypeStruct((M, N), out_dtype),
        grid=(M // tm, N // tn, nk),
        in_specs=[a_spec, b_spec] + ([ANY] if after is not None else []),
        out_specs=pl.BlockSpec((tm, tn), lambda i, j, k: (i, j)),
        scratch_shapes=[pltpu.VMEM((tm, tn) if nk > 1 else (8, 128), F32)],
        compiler_params=_params(),
    )(a, b, *([after] if after is not None else []))


def _matmul_resid(a, b, resid, gate, name, tiles):
    M, K = a.shape
    N = b.shape[1]
    tm, tn, tk = (_tile(d, t) for d, t in zip((M, N, K), tiles))
    nk = K // tk

    def body(a_ref, b_ref, r_ref, g_ref, o_ref, s_ref, acc_ref):
        k = pl.program_id(2)
        part = jnp.dot(a_ref[...], b_ref[...], preferred_element_type=F32)

        def finish(acc):
            o_ref[...] = r_ref[...] + g_ref[...] * acc
            s_ref[...] = acc.astype(BF16)

        if nk == 1:
            finish(part)
        else:
            @pl.when(k == 0)
            def _():
                acc_ref[...] = part

            @pl.when(k > 0)
            def _():
                acc_ref[...] += part

            @pl.when(k == nk - 1)
            def _():
                finish(acc_ref[...])

    return pl.pallas_call(
        body, name=name,
        out_shape=(jax.ShapeDtypeStruct((M, N), F32), jax.ShapeDtypeStruct((M, N), BF16)),
        grid=(M // tm, N // tn, nk),
        in_specs=[pl.BlockSpec((tm, tk), lambda i, j, k: (i, k)),
                  pl.BlockSpec((tk, tn), lambda i, j, k: (k, j)),
                  pl.BlockSpec((tm, tn), lambda i, j, k: (i, j)),
                  pl.BlockSpec((1, tn), lambda i, j, k: (0, j))],
        out_specs=(pl.BlockSpec((tm, tn), lambda i, j, k: (i, j)),
                   pl.BlockSpec((tm, tn), lambda i, j, k: (i, j))),
        scratch_shapes=[pltpu.VMEM((tm, tn) if nk > 1 else (8, 128), F32)],
        compiler_params=_params(),
    )(a, b, resid, gate)


def _norm_mod_fwd(x, g, shift, scale, name):
    S, D = x.shape
    ts = _row_tile(S, 256)

    def body(x_ref, g_ref, sh_ref, sc_ref, h_ref):
        xv = x_ref[...]
        rstd = lax.rsqrt(jnp.mean(xv * xv, axis=-1, keepdims=True) + EPS)
        y = xv * rstd * g_ref[...]
        h_ref[...] = (y * (1.0 + sc_ref[...]) + sh_ref[...]).astype(BF16)

    row = pl.BlockSpec((1, D), lambda i: (0, 0))
    return pl.pallas_call(
        body, name=name, out_shape=jax.ShapeDtypeStruct((S, D), BF16), grid=(S // ts,),
        in_specs=[pl.BlockSpec((ts, D), lambda i: (i, 0)), row, row, row],
        out_specs=pl.BlockSpec((ts, D), lambda i: (i, 0)),
        compiler_params=_params(),
    )(x, g, shift, scale)


def _norm_mod_bwd(x, dh, dres, g, scale, name):
    S, D = x.shape
    ts = _row_tile(S, 256)

    def body(x_ref, dh_ref, dr_ref, g_ref, sc_ref, dx_ref, dsh_ref, dsc_ref, dg_ref):
        i = pl.program_id(0)
        xv, dhv, gv = x_ref[...], dh_ref[...], g_ref[...]
        rstd = lax.rsqrt(jnp.mean(xv * xv, axis=-1, keepdims=True) + EPS)
        xhat = xv * rstd
        dn = dhv * (1.0 + sc_ref[...])
        dxhat = dn * gv
        proj = jnp.mean(dxhat * xhat, axis=-1, keepdims=True)
        dx_ref[...] = dr_ref[...] + rstd * (dxhat - xhat * proj)
        dsh = jnp.sum(dhv, axis=0, keepdims=True)
        dsc = jnp.sum(dhv * (xhat * gv), axis=0, keepdims=True)
        dg = jnp.sum(dn * xhat, axis=0, keepdims=True)

        @pl.when(i == 0)
        def _():
            dsh_ref[...] = dsh
            dsc_ref[...] = dsc
            dg_ref[...] = dg

        @pl.when(i > 0)
        def _():
            dsh_ref[...] += dsh
            dsc_ref[...] += dsc
            dg_ref[...] += dg

    tile = pl.BlockSpec((ts, D), lambda i: (i, 0))
    row = pl.BlockSpec((1, D), lambda i: (0, 0))
    vec = jax.ShapeDtypeStruct((1, D), F32)
    return pl.pallas_call(
        body, name=name, out_shape=(jax.ShapeDtypeStruct((S, D), F32), vec, vec, vec), grid=(S // ts,),
        in_specs=[tile, tile, tile, row, row], out_specs=(tile, row, row, row),
        compiler_params=_params(),
    )(x, dh, dres, g, scale)


def _gate_bwd(dx, f, gate, name):
    S, D = dx.shape
    ts = _row_tile(S, 256)

    def body(dx_ref, f_ref, g_ref, df_ref, dg_ref):
        i = pl.program_id(0)
        dxv = dx_ref[...]
        df_ref[...] = (dxv * g_ref[...]).astype(BF16)
        dg = jnp.sum(dxv * f_ref[...].astype(F32), axis=0, keepdims=True)

        @pl.when(i == 0)
        def _():
            dg_ref[...] = dg

        @pl.when(i > 0)
        def _():
            dg_ref[...] += dg

    tile = pl.BlockSpec((ts, D), lambda i: (i, 0))
    row = pl.BlockSpec((1, D), lambda i: (0, 0))
    return pl.pallas_call(
        body, name=name,
        out_shape=(jax.ShapeDtypeStruct((S, D), BF16), jax.ShapeDtypeStruct((1, D), F32)), grid=(S // ts,),
        in_specs=[tile, tile, row], out_specs=(tile, row),
        compiler_params=_params(),
    )(dx, f, gate)


def _ffn_in_fwd(h, w_t, name):
    S, D = h.shape
    F = w_t.shape[0] // 2
    tm, tn = _tile(S, TILES["ffn_fused"][0]), _tile(F, TILES["ffn_fused"][1])
    nj = F // tn

    def body(h_ref, wg_ref, wu_ref, o_ref):
        hv = h_ref[...]
        ug = lax.dot_general(hv, wg_ref[...], NT, preferred_element_type=F32)
        uu = lax.dot_general(hv, wu_ref[...], NT, preferred_element_type=F32)
        o_ref[...] = (ug * jax.nn.sigmoid(ug) * uu).astype(BF16)

    return pl.pallas_call(
        body, name=name, out_shape=jax.ShapeDtypeStruct((S, F), BF16), grid=(nj, S // tm),
        in_specs=[pl.BlockSpec((tm, D), lambda j, i: (i, 0)),
                  pl.BlockSpec((tn, D), lambda j, i: (j, 0)),
                  pl.BlockSpec((tn, D), lambda j, i: (j + nj, 0))],
        out_specs=pl.BlockSpec((tm, tn), lambda j, i: (i, j)),
        compiler_params=_params(),
    )(h, w_t, w_t)


def _ffn_mid_bwd(h, df, w_in_t, w_out, name):
    S, D = h.shape
    F = w_in_t.shape[0] // 2
    tm, tn = _tile(S, TILES["ffn_fused"][0]), _tile(F, TILES["ffn_fused"][1])
    nj = F // tn

    def body(h_ref, df_ref, wg_ref, wu_ref, wo_ref, dg_ref, du_ref):
        hv = h_ref[...]
        ug = lax.dot_general(hv, wg_ref[...], NT, preferred_element_type=F32)
        uu = lax.dot_general(hv, wu_ref[...], NT, preferred_element_type=F32)
        dact = lax.dot_general(df_ref[...], wo_ref[...], NT, preferred_element_type=F32)
        sig = jax.nn.sigmoid(ug)
        dg_ref[...] = (dact * uu * (sig * (1.0 + ug * (1.0 - sig)))).astype(BF16)
        du_ref[...] = (dact * (ug * sig)).astype(BF16)

    out = jax.ShapeDtypeStruct((S, F), BF16)
    return pl.pallas_call(
        body, name=name, out_shape=(out, out), grid=(nj, S // tm),
        in_specs=[pl.BlockSpec((tm, D), lambda j, i: (i, 0)),
                  pl.BlockSpec((tm, D), lambda j, i: (i, 0)),
                  pl.BlockSpec((tn, D), lambda j, i: (j, 0)),
                  pl.BlockSpec((tn, D), lambda j, i: (j + nj, 0)),
                  pl.BlockSpec((tn, D), lambda j, i: (j, 0))],
        out_specs=(pl.BlockSpec((tm, tn), lambda j, i: (i, j)), pl.BlockSpec((tm, tn), lambda j, i: (i, j))),
        compiler_params=_params(),
    )(h, df, w_in_t, w_in_t, w_out)


def _merge_fwd(o_a, o_b, o_c, gates, w_branch, name, *, tm=512):
    S, W = o_a.shape
    D = w_branch.shape[2]
    tm = _row_tile(S, tm)

    def body(oa_ref, ob_ref, oc_ref, g_ref, w_ref, m_ref):
        acc = None
        for k, o_ref in enumerate((oa_ref, ob_ref, oc_ref)):
            y = jnp.dot(o_ref[...], w_ref[k], preferred_element_type=F32)
            t = jax.nn.sigmoid(g_ref[:, k * D:(k + 1) * D]) * y
            acc = t if acc is None else acc + t
        m_ref[...] = acc.astype(BF16)

    o_spec = pl.BlockSpec((tm, W), lambda i: (i, 0))
    return pl.pallas_call(
        body, name=name, out_shape=jax.ShapeDtypeStruct((S, D), BF16), grid=(S // tm,),
        in_specs=[o_spec, o_spec, o_spec, pl.BlockSpec((tm, 3 * D), lambda i: (i, 0)),
                  pl.BlockSpec((3, W, D), lambda i: (0, 0, 0))],
        out_specs=pl.BlockSpec((tm, D), lambda i: (i, 0)),
        compiler_params=_params(),
    )(o_a, o_b, o_c, gates, w_branch)


def _merge_bwd(dmerged, o_a, o_b, o_c, gates, w_branch, name, *, tm=512):
    S, W = o_a.shape
    D = w_branch.shape[2]
    tm = _row_tile(S, tm)
    n_heads = W // 64

    def body(dm_ref, oa_ref, ob_ref, oc_ref, g_ref, w_ref, dg_ref, dy_ref,
             doa_ref, dob_ref, doc_ref, dla_ref, dlb_ref, dlc_ref):
        dm = dm_ref[...]
        branches = ((oa_ref, doa_ref, dla_ref), (ob_ref, dob_ref, dlb_ref), (oc_ref, doc_ref, dlc_ref))
        for k, (o_ref, do_ref, dl_ref) in enumerate(branches):
            wk = w_ref[k]
            ov = o_ref[...]
            y = jnp.dot(ov, wk, preferred_element_type=F32)
            g = jax.nn.sigmoid(g_ref[:, k * D:(k + 1) * D])
            dy = (dm * g).astype(BF16)
            dy_ref[:, k * D:(k + 1) * D] = dy
            dg_ref[:, k * D:(k + 1) * D] = (dm * y * (g * (1.0 - g))).astype(BF16)
            do16 = lax.dot_general(dy, wk, NT, preferred_element_type=F32).astype(BF16)
            do_ref[...] = do16
            prod = do16.astype(F32) * ov.astype(F32)
            for h in range(n_heads):
                dl_ref[:, h:h + 1] = jnp.sum(prod[:, 64 * h:64 * (h + 1)], axis=1, keepdims=True)

    o_spec = pl.BlockSpec((tm, W), lambda i: (i, 0))
    wide = pl.BlockSpec((tm, 3 * D), lambda i: (i, 0))
    dl_spec = pl.BlockSpec((tm, n_heads), lambda i: (i, 0))
    o_out = jax.ShapeDtypeStruct((S, W), BF16)
    wide_out = jax.ShapeDtypeStruct((S, 3 * D), BF16)
    dl_out = jax.ShapeDtypeStruct((S, n_heads), F32)
    return pl.pallas_call(
        body, name=name, out_shape=(wide_out, wide_out, o_out, o_out, o_out, dl_out, dl_out, dl_out),
        grid=(S // tm,),
        in_specs=[pl.BlockSpec((tm, D), lambda i: (i, 0)), o_spec, o_spec, o_spec, wide,
                  pl.BlockSpec((3, W, D), lambda i: (0, 0, 0))],
        out_specs=(wide, wide, o_spec, o_spec, o_spec, dl_spec, dl_spec, dl_spec),
        compiler_params=_params(),
    )(dmerged, o_a, o_b, o_c, gates, w_branch)


def _band_mask(variant, t_abs, s_abs):
    if variant == "b":
        return s_abs <= t_abs
    qc, kc = t_abs >> 6, s_abs >> 6
    return (kc <= qc) & (kc >= qc - (2 if variant == "a" else 8))


def _attn_fwd(variant, qkv, name, *, sinks=None, slopes=None, cq_col=None, ck_row=None, bias=None):
    S = qkv.shape[0]
    nb = S // BLK
    qb, kb, vb = ATTN_COLS[variant]
    shared_kv = variant == "a"
    win = ATTN_WINDOW.get(variant)
    per_step = ATTN_BLOCKS_PER_STEP[variant]

    def body(*refs):
        if variant == "a":
            q_ref, k_ref, v_ref, sink_ref, slope_ref, o_ref, lse_ref = refs
        elif variant == "b":
            q_ref, k_ref, v_ref, cq_ref, ck_ref, o_ref, lse_ref = refs
        else:
            q_ref, k_ref, v_ref, bias_ref, o_ref, lse_ref = refs
        p = pl.program_id(0)
        lane = lax.broadcasted_iota(jnp.int32, (1, BLK), 1)

        def compute(i, rows, start, n_keys):
            n_rows = rows.stop - rows.start
            t_abs = i * BLK + lax.broadcasted_iota(jnp.int32, (n_rows, 1), 0)
            q2 = q_ref[rows, :].astype(F32) * 0.125
            k_w = k_ref[pl.ds(start, n_keys), :]
            v_w = v_ref[pl.ds(start, n_keys), :]
            s_abs = start + lax.broadcasted_iota(jnp.int32, (1, n_keys), 1)
            valid = _band_mask(variant, t_abs, s_abs)
            outs = []
            for half in (0, 1):
                hmask = (lane >= 64) if half else (lane < 64)
                qh = jnp.where(hmask, q2, 0.0)
                if shared_kv:
                    swap = (p // 2) != half
                    qh = jnp.where(swap, pltpu.roll(qh, 64, 1), qh)
                s = lax.dot_general(qh.astype(BF16), k_w, NT, preferred_element_type=F32)
                if variant == "a":
                    head = 2 * p + half
                    s = s + (-slope_ref[head]) * jnp.abs(t_abs - s_abs).astype(F32)
                elif variant == "b":
                    s = s + cq_ref[rows, half:half + 1] - ck_ref[half:half + 1, pl.ds(start, n_keys)]
                else:
                    j0 = start // BLK
                    s = s + jnp.concatenate(
                        [bias_ref[half, jnp.clip(i - j0 - b, 0, 4)] for b in range(win)], axis=1)
                s = jnp.where(valid, s, NEG_INF)
                m = jnp.max(s, axis=1, keepdims=True)
                if variant == "a":
                    m = jnp.maximum(m, sink_ref[head])
                pe = jnp.exp(s - m)
                l = jnp.sum(pe, axis=1, keepdims=True)
                if variant == "a":
                    l = l + jnp.exp(sink_ref[head] - m)
                out = jnp.dot(pe.astype(BF16), v_w, preferred_element_type=F32) / l
                if shared_kv:
                    out = jnp.where(swap, pltpu.roll(out, 64, 1), out)
                outs.append(out)
                lse_ref[rows, half:half + 1] = m + jnp.log(l)
            o_ref[rows, :] = jnp.where(lane < 64, outs[0], outs[1]).astype(BF16)

        step = pl.program_id(1)
        if variant == "b":
            for g in range(S // GROUP):
                pl.when(step == g)(functools.partial(compute, step * per_step, slice(0, GROUP), 0, (g + 1) * GROUP))
        else:
            for sub in range(per_step):
                i = step * per_step + sub
                start = jnp.clip(i - (win - 1), 0, nb - win) * BLK
                compute(i, slice(sub * BLK, (sub + 1) * BLK), pl.multiple_of(start, BLK), win * BLK)

    tq = per_step * BLK
    kv_col = (lambda p, i: (0, kb)) if shared_kv else (lambda p, i: (0, kb + p))
    vv_col = (lambda p, i: (0, vb)) if shared_kv else (lambda p, i: (0, vb + p))
    in_specs = [pl.BlockSpec((tq, BLK), lambda p, i: (i, qb + p)),
                pl.BlockSpec((S, BLK), kv_col), pl.BlockSpec((S, BLK), vv_col)]
    args = [qkv, qkv, qkv]
    if variant == "a":
        in_specs += [pl.BlockSpec(memory_space=pltpu.SMEM), pl.BlockSpec(memory_space=pltpu.SMEM)]
        args += [sinks, slopes]
    elif variant == "b":
        in_specs += [pl.BlockSpec((None, tq, 2), lambda p, i: (p, i, 0)),
                     pl.BlockSpec((None, 2, S), lambda p, i: (p, 0, 0))]
        args += [cq_col, ck_row]
    else:
        in_specs += [pl.BlockSpec((2, 5, BLK, BLK), lambda p, i: (p, 0, 0, 0))]
        args += [bias]
    return pl.pallas_call(
        body, name=name,
        out_shape=(jax.ShapeDtypeStruct((S, 512), BF16), jax.ShapeDtypeStruct((4, S, 2), F32)),
        grid=(4, nb // per_step), in_specs=in_specs,
        out_specs=(pl.BlockSpec((tq, BLK), lambda p, i: (i, p)),
                   pl.BlockSpec((None, tq, 2), lambda p, i: (p, i, 0))),
        compiler_params=_params(),
    )(*args)


def _attn_bwd(variant, qkv, do, lse_row, delta_row, name, *, sinks=None, slopes=None, cq_row=None,
              ck_col=None, bias_t=None):
    S = qkv.shape[0]
    nb = S // BLK
    qb, kb, vb = ATTN_COLS[variant]
    shared_kv = variant == "a"
    win = ATTN_WINDOW.get(variant)
    per_step = ATTN_BLOCKS_PER_STEP[variant]

    def body(*refs):
        if variant == "a":
            (q_ref, k_ref, v_ref, do_ref, lse_ref, dl_ref, sink_ref, slope_ref,
             dq_ref, dk_ref, dv_ref, ex_ref) = refs
        elif variant == "b":
            (q_ref, k_ref, v_ref, do_ref, lse_ref, dl_ref, cq_ref, ck_ref,
             dq_ref, dk_ref, dv_ref, ex_ref, dcq_ref) = refs
        else:
            (q_ref, k_ref, v_ref, do_ref, lse_ref, dl_ref, bias_ref,
             dq_ref, dk_ref, dv_ref, ex_ref) = refs
        p = pl.program_id(0)
        lane = lax.broadcasted_iota(jnp.int32, (1, BLK), 1)
        hmasks = [(lane < 64), (lane >= 64)]
        swaps = [(p // 2) != half for half in (0, 1)] if shared_kv else None

        @pl.when(pl.program_id(1) == 0)
        def _():
            dq_ref[...] = jnp.zeros_like(dq_ref)
            if variant == "b":
                dcq_ref[...] = jnp.zeros_like(dcq_ref)
            else:
                ex_ref[...] = jnp.zeros_like(ex_ref)

        def to_kv_lanes(x, h):
            x = jnp.where(hmasks[h], x, 0.0)
            if shared_kv:
                x = jnp.where(swaps[h], pltpu.roll(x, 64, 1), x)
            return x

        def compute(j, rows, start, n_q):
            n_rows = rows.stop - rows.start
            s_abs = j * BLK + lax.broadcasted_iota(jnp.int32, (n_rows, 1), 0)
            off_k = pl.multiple_of(j * BLK, BLK)
            k2 = k_ref[rows, :].astype(F32)
            v2 = v_ref[rows, :].astype(F32)
            if shared_kv:
                kv_lane = (lane >> 6) == (p // 2)
                k_src, v_src = jnp.where(kv_lane, k2, 0.0), jnp.where(kv_lane, v2, 0.0)
                k_al = [jnp.where(swaps[h], pltpu.roll(k_src, 64, 1), k_src) for h in (0, 1)]
                v_al = [jnp.where(swaps[h], pltpu.roll(v_src, 64, 1), v_src) for h in (0, 1)]
            else:
                k_al = [jnp.where(hmasks[h], k2, 0.0) for h in (0, 1)]
                v_al = [jnp.where(hmasks[h], v2, 0.0) for h in (0, 1)]
            k_al = [(t * 0.125).astype(BF16) for t in k_al]
            v_al = [t.astype(BF16) for t in v_al]
            q_w = q_ref[pl.ds(start, n_q), :]
            do_w = do_ref[pl.ds(start, n_q), :]
            t_abs = start + lax.broadcasted_iota(jnp.int32, (1, n_q), 1)
            valid = _band_mask(variant, t_abs, s_abs)
            dk_acc = dv_acc = None
            ds_both = []
            for half in (0, 1):
                s = lax.dot_general(k_al[half], q_w, NT, preferred_element_type=F32)
                if variant == "a":
                    s = s + (-slope_ref[2 * p + half]) * jnp.abs(t_abs - s_abs).astype(F32)
                elif variant == "b":
                    s = s + cq_ref[half:half + 1, pl.ds(start, n_q)] - ck_ref[rows, half:half + 1]
                else:
                    i0 = start // BLK
                    s = s + jnp.concatenate(
                        [bias_ref[half, jnp.clip(i0 + b - j, 0, 4)] for b in range(win)], axis=1)
                pr = jnp.where(valid, jnp.exp(s - lse_ref[half:half + 1, pl.ds(start, n_q)]), 0.0)
                dp = lax.dot_general(v_al[half], do_w, NT, preferred_element_type=F32)
                ds = pr * (dp - dl_ref[half:half + 1, pl.ds(start, n_q)])
                ds16 = ds.astype(BF16)
                dv_h = to_kv_lanes(jnp.dot(pr.astype(BF16), do_w, preferred_element_type=F32), half)
                dk_h = to_kv_lanes(jnp.dot(ds16, q_w, preferred_element_type=F32) * 0.125, half)
                dv_acc = dv_h if dv_acc is None else dv_acc + dv_h
                dk_acc = dk_h if dk_acc is None else dk_acc + dk_h
                ds_both.append(ds16)
                if variant == "b":
                    ex_ref[rows, half:half + 1] = -jnp.sum(ds, axis=1, keepdims=True)
                    dcq_ref[half:half + 1, pl.ds(start, n_q)] += jnp.sum(ds, axis=0, keepdims=True)
                elif variant == "c":
                    for b in range(win):
                        ex_ref[half, jnp.clip(i0 + b - j, 0, 4)] += ds[:, b * BLK:(b + 1) * BLK]
            dq_t = lax.dot_general(jnp.concatenate(k_al, axis=0), jnp.concatenate(ds_both, axis=0), TN,
                                   preferred_element_type=F32)
            dq_ref[:, pl.ds(start, n_q)] += dq_t
            if shared_kv:
                @pl.when(p == 0)
                def _():
                    dk_ref[pl.ds(off_k, n_rows), :] = dk_acc
                    dv_ref[pl.ds(off_k, n_rows), :] = dv_acc

                @pl.when(p > 0)
                def _():
                    dk_ref[pl.ds(off_k, n_rows), :] += dk_acc
                    dv_ref[pl.ds(off_k, n_rows), :] += dv_acc
            else:
                dk_ref[pl.ds(off_k, n_rows), :] = dk_acc
                dv_ref[pl.ds(off_k, n_rows), :] = dv_acc
            if variant == "a":
                for half in (0, 1):
                    p_sink = jnp.exp(sink_ref[2 * p + half] - lse_ref[half:half + 1, pl.ds(off_k, n_rows)])
                    term = p_sink * dl_ref[half:half + 1, pl.ds(off_k, n_rows)]
                    ex_ref[half:half + 1, :] += -jnp.sum(term, axis=1, keepdims=True)

        step = pl.program_id(1)
        if variant == "b":
            for g in range(S // GROUP):
                pl.when(step == g)(functools.partial(compute, step * per_step, slice(0, GROUP), g * GROUP, S - g * GROUP))
        else:
            for sub in range(per_step):
                j = step * per_step + sub
                start = jnp.clip(j, 0, nb - win) * BLK
                compute(j, slice(sub * BLK, (sub + 1) * BLK), pl.multiple_of(start, BLK), win * BLK)

    tk = per_step * BLK
    col = lambda c0: (lambda p, j: (0, c0 + p))
    kv_blk = (lambda c0: (lambda p, j: (j, c0))) if shared_kv else (lambda c0: (lambda p, j: (j, c0 + p)))
    pair = lambda p, j: (0, p)
    row_stat = pl.BlockSpec((None, 2, S), lambda p, j: (p, 0, 0))
    in_specs = [pl.BlockSpec((S, BLK), col(qb)),
                pl.BlockSpec((tk, BLK), kv_blk(kb)), pl.BlockSpec((tk, BLK), kv_blk(vb)),
                pl.BlockSpec((S, BLK), pair), row_stat, row_stat]
    args = [qkv, qkv, qkv, do, lse_row, delta_row]
    kv_width = BLK if shared_kv else 512
    kv_out = pl.BlockSpec((S, BLK), (lambda p, j: (0, 0)) if shared_kv else pair)
    out_shape = [jax.ShapeDtypeStruct((512, S), F32), jax.ShapeDtypeStruct((S, kv_width), F32),
                 jax.ShapeDtypeStruct((S, kv_width), F32)]
    out_specs = [pl.BlockSpec((BLK, S), lambda p, j: (p, 0)), kv_out, kv_out]
    if variant == "a":
        in_specs += [pl.BlockSpec(memory_space=pltpu.SMEM), pl.BlockSpec(memory_space=pltpu.SMEM)]
        args += [sinks, slopes]
        out_shape.append(jax.ShapeDtypeStruct((4, 8, BLK), F32))
        out_specs.append(pl.BlockSpec((None, 8, BLK), lambda p, j: (p, 0, 0)))
    elif variant == "b":
        in_specs += [row_stat, pl.BlockSpec((None, tk, 2), lambda p, j: (p, j, 0))]
        args += [cq_row, ck_col]
        out_shape += [jax.ShapeDtypeStruct((4, S, 2), F32), jax.ShapeDtypeStruct((4, 2, S), F32)]
        out_specs += [pl.BlockSpec((None, tk, 2), lambda p, j: (p, j, 0)), row_stat]
    else:
        in_specs += [pl.BlockSpec((2, 5, BLK, BLK), lambda p, j: (p, 0, 0, 0))]
        args += [bias_t]
        out_shape.append(jax.ShapeDtypeStruct((8, 5, BLK, BLK), F32))
        out_specs.append(pl.BlockSpec((2, 5, BLK, BLK), lambda p, j: (p, 0, 0, 0)))
    return pl.pallas_call(
        body, name=name, out_shape=tuple(out_shape), grid=(4, nb // per_step),
        in_specs=in_specs, out_specs=tuple(out_specs),
        compiler_params=_params(),
    )(*args)


def _log_sigmoid(x):
    return jnp.minimum(x, 0.0) - jnp.log(1.0 + jnp.exp(-jnp.abs(x)))


def _forget_fwd(fb, b_forget, name):
    S = fb.shape[0]
    nb = S // BLK

    def body(fb_ref, b_ref, cum_ref, carry_ref):
        i = pl.program_id(0)
        logf = _log_sigmoid(fb_ref[...] + b_ref[...])
        r = lax.broadcasted_iota(jnp.int32, (BLK, BLK), 0)
        c = lax.broadcasted_iota(jnp.int32, (BLK, BLK), 1)
        tri = (c <= r).astype(F32)

        @pl.when(i == 0)
        def _():
            carry_ref[...] = jnp.zeros_like(carry_ref)

        cum = jnp.dot(tri, logf, preferred_element_type=F32, precision=HIGHEST) + carry_ref[0:1, :]
        cum_ref[...] = cum
        carry_ref[...] = jnp.broadcast_to(cum[BLK - 1:BLK, :], carry_ref.shape)

    return pl.pallas_call(
        body, name=name, out_shape=jax.ShapeDtypeStruct((S, BLK), F32), grid=(nb,),
        in_specs=[pl.BlockSpec((BLK, BLK), lambda i: (i, 0)), pl.BlockSpec((1, BLK), lambda i: (0, 0))],
        out_specs=pl.BlockSpec((BLK, BLK), lambda i: (i, 0)),
        scratch_shapes=[pltpu.VMEM((8, BLK), F32)],
        compiler_params=_params(),
    )(fb, b_forget)


def _forget_bwd(dcum_q, dcum_k, fb, b_forget, name):
    S = fb.shape[0]
    nb = S // BLK

    def body(dq_ref, dk_ref, fb_ref, b_ref, dfb_ref, db_ref, carry_ref):
        g = pl.program_id(0)
        r = lax.broadcasted_iota(jnp.int32, (BLK, BLK), 0)
        c = lax.broadcasted_iota(jnp.int32, (BLK, BLK), 1)
        tri = (c >= r).astype(F32)

        @pl.when(g == 0)
        def _():
            carry_ref[...] = jnp.zeros_like(carry_ref)

        dcum = dq_ref[...] + dk_ref[...]
        dlogf = jnp.dot(tri, dcum, preferred_element_type=F32, precision=HIGHEST) + carry_ref[0:1, :]
        carry_ref[...] = jnp.broadcast_to(dlogf[0:1, :], carry_ref.shape)
        x = fb_ref[...] + b_ref[...]
        dfb = jnp.where(c < N_FORGET, dlogf * jax.nn.sigmoid(-x), 0.0)
        dfb_ref[...] = dfb
        db = jnp.sum(dfb, axis=0, keepdims=True)

        @pl.when(g == 0)
        def _():
            db_ref[...] = db

        @pl.when(g > 0)
        def _():
            db_ref[...] += db

    rev = pl.BlockSpec((BLK, BLK), lambda g: (nb - 1 - g, 0))
    row = pl.BlockSpec((1, BLK), lambda g: (0, 0))
    return pl.pallas_call(
        body, name=name,
        out_shape=(jax.ShapeDtypeStruct((S, BLK), F32), jax.ShapeDtypeStruct((1, BLK), F32)), grid=(nb,),
        in_specs=[rev, rev, rev, row], out_specs=(rev, row),
        scratch_shapes=[pltpu.VMEM((8, BLK), F32)],
        compiler_params=_params(),
    )(dcum_q, dcum_k, fb, b_forget)


def _skew(x, sign):
    row = lax.broadcasted_iota(jnp.int32, x.shape, 0)
    for b in range(7):
        amount = (1 << b) if sign > 0 else 256 - (1 << b)
        x = jnp.where(((row >> b) & 1) == 1, pltpu.roll(x, amount, 1), x)
    return x


def _rel_bases(rel):
    far = rel[:, 256:257]
    far127 = jnp.broadcast_to(far, (rel.shape[0], 127))
    base0 = jnp.concatenate([rel[:, 128:0:-1], far, rel[:, 255:128:-1]], axis=1)
    base1 = jnp.concatenate([rel[:, 256:128:-1], far, far127], axis=1)
    base0_t = jnp.concatenate([rel[:, 128:256], far, rel[:, 1:128]], axis=1)
    base1_t = jnp.concatenate([jnp.broadcast_to(far, (rel.shape[0], 128)), far, rel[:, 129:256]], axis=1)
    return jnp.stack([base0, base1, base0_t, base1_t], axis=1)


def _rel_expand(bases, name):
    def body(b_ref, t_ref, tt_ref):
        far = jnp.broadcast_to(b_ref[1:2, 0:1], (BLK, BLK))
        for k, out_ref in ((0, t_ref), (2, tt_ref)):
            for d in (0, 1):
                x = jnp.broadcast_to(b_ref[k + d:k + d + 1, :], (BLK, 2 * BLK))
                out_ref[d] = _skew(x, 1)[:, :BLK]
            for d in (2, 3, 4):
                out_ref[d] = far

    out = jax.ShapeDtypeStruct((8, 5, BLK, BLK), F32)
    spec = pl.BlockSpec((None, 5, BLK, BLK), lambda h: (h, 0, 0, 0))
    return pl.pallas_call(
        body, name=name, out_shape=(out, out), grid=(8,),
        in_specs=[pl.BlockSpec((None, 4, 2 * BLK), lambda h: (h, 0, 0))], out_specs=(spec, spec),
        compiler_params=_params(),
    )(bases)


def _rel_reduce(dtiles_t, name):
    def body(dt_ref, o_ref):
        zeros = jnp.zeros((BLK, BLK), F32)
        sums = []
        for d in (0, 1):
            x = _skew(jnp.concatenate([dt_ref[d], zeros], axis=1), -1)
            sums.append(jnp.broadcast_to(jnp.sum(x, axis=0, keepdims=True), (8, 2 * BLK)))
        lane = lax.broadcasted_iota(jnp.int32, (8, 2 * BLK), 1)
        main = pltpu.roll(sums[0], BLK, 1) + jnp.where(lane > BLK, sums[1], 0.0)
        far = jnp.sum(jnp.where(lane < BLK, sums[1], 0.0)[0:1], axis=1, keepdims=True)
        far = far + jnp.sum(jnp.sum(dt_ref[2] + dt_ref[3] + dt_ref[4], axis=0, keepdims=True), axis=1, keepdims=True)
        o_ref[...] = jnp.concatenate([main[0:1], jnp.broadcast_to(far, (1, BLK))], axis=1)

    return pl.pallas_call(
        body, name=name, out_shape=jax.ShapeDtypeStruct((8, 1, 3 * BLK), F32), grid=(8,),
        in_specs=[pl.BlockSpec((None, 5, BLK, BLK), lambda h: (h, 0, 0, 0))],
        out_specs=pl.BlockSpec((None, 1, 3 * BLK), lambda h: (h, 0, 0)),
        compiler_params=_params(),
    )(dtiles_t)


def _final_loss(x, target, g, name):
    S, D = x.shape
    ts = _row_tile(S, 256)

    def body(x_ref, t_ref, g_ref, dx_ref, loss_ref, dg_ref):
        i = pl.program_id(0)
        xv, gv = x_ref[...], g_ref[...]
        rstd = lax.rsqrt(jnp.mean(xv * xv, axis=-1, keepdims=True) + EPS)
        xhat = xv * rstd
        err = xhat * gv - t_ref[...]
        part = 0.5 * jnp.sum(jnp.mean(err * err, axis=-1, keepdims=True), axis=0, keepdims=True)
        dy = err / D
        dg = jnp.sum(dy * xhat, axis=0, keepdims=True)
        dxhat = dy * gv
        proj = jnp.mean(dxhat * xhat, axis=-1, keepdims=True)
        dx_ref[...] = rstd * (dxhat - xhat * proj)

        @pl.when(i == 0)
        def _():
            loss_ref[...] = jnp.broadcast_to(part, loss_ref.shape)
            dg_ref[...] = dg

        @pl.when(i > 0)
        def _():
            loss_ref[...] += jnp.broadcast_to(part, loss_ref.shape)
            dg_ref[...] += dg

    tile = pl.BlockSpec((ts, D), lambda i: (i, 0))
    row = pl.BlockSpec((1, D), lambda i: (0, 0))
    return pl.pallas_call(
        body, name=name,
        out_shape=(jax.ShapeDtypeStruct((S, D), F32), jax.ShapeDtypeStruct((8, 128), F32),
                   jax.ShapeDtypeStruct((1, D), F32)),
        grid=(S // ts,), in_specs=[tile, tile, row],
        out_specs=(tile, pl.BlockSpec((8, 128), lambda i: (0, 0)), row),
        compiler_params=_params(),
    )(x, target, g)


def _ada_fwd(c_all, w_ada, name):
    L, D, E = w_ada.shape

    def body(c_ref, w_ref, o_ref):
        cv = c_ref[...]
        cond = cv * jax.nn.sigmoid(cv)
        o_ref[...] = jnp.dot(cond, w_ref[...], preferred_element_type=F32, precision=HIGHEST)

    return pl.pallas_call(
        body, name=name, out_shape=jax.ShapeDtypeStruct((L, N_DEV, E), F32), grid=(L,),
        in_specs=[pl.BlockSpec((N_DEV, D), lambda l: (0, 0)), pl.BlockSpec((None, D, E), lambda l: (l, 0, 0))],
        out_specs=pl.BlockSpec((None, N_DEV, E), lambda l: (l, 0, 0)),
        compiler_params=_params(),
    )(c_all, w_ada)


def _ada_bwd(c_all_t, dmod, name):
    D = c_all_t.shape[0]
    L, _, E = dmod.shape

    def body(c_ref, d_ref, o_ref):
        cv = c_ref[...]
        cond = cv * jax.nn.sigmoid(cv)
        acc = None
        for b in range(N_DEV):
            t = cond[:, b:b + 1] * d_ref[b:b + 1, :]
            acc = t if acc is None else acc + t
        o_ref[...] = acc

    return pl.pallas_call(
        body, name=name, out_shape=jax.ShapeDtypeStruct((L, D, E), F32), grid=(L,),
        in_specs=[pl.BlockSpec((D, N_DEV), lambda l: (0, 0)), pl.BlockSpec((None, N_DEV, E), lambda l: (l, 0, 0))],
        out_specs=pl.BlockSpec((None, D, E), lambda l: (l, 0, 0)),
        compiler_params=_params(),
    )(c_all_t, dmod)


def _adamw(w, m, v, g_parts, name, me, after=None):
    L, R, C = w.shape
    tr = _row_tile(R, max(8, (256 * 1024 // max(C, 128)) // 8 * 8))
    nr = R // tr
    c1 = 1.0 - ADAM_B1 ** ADAM_STEP
    c2 = 1.0 - ADAM_B2 ** ADAM_STEP
    direct = [isinstance(p, tuple) for p in g_parts]
    n_in = sum(2 if d else 1 for d in direct)

    def body(me_ref, w_ref, m_ref, v_ref, *rest):
        g_refs, (go_ref, d_ref, mo_ref, vo_ref) = list(rest[:n_in]), rest[-4:]
        layer = pl.program_id(0)
        g = None
        for l in range(L):
            land_ref = g_refs.pop(0)
            own = g_refs.pop(0)[...].astype(F32) if direct[l] else None
            gl = None
            for k in range(land_ref.shape[0]):
                part = land_ref[k].astype(F32)
                if direct[l]:
                    part = jnp.where(me_ref[l] == k, own, part)
                gl = part if gl is None else gl + part
            g = gl if g is None else jnp.where(layer == l, gl, g)
        mn = ADAM_B1 * m_ref[...] + (1.0 - ADAM_B1) * g
        vn = ADAM_B2 * v_ref[...] + (1.0 - ADAM_B2) * (g * g)
        m_hat = mn / c1
        v_hat = vn / c2
        go_ref[...] = g
        d_ref[...] = -ADAM_LR * (m_hat / (jnp.sqrt(v_hat) + ADAM_EPS) + ADAM_WD * w_ref[...])
        mo_ref[...] = mn
        vo_ref[...] = vn

    def rows(l, layer, i):
        return jnp.where(layer == l, i, 0 if l > 0 else nr - 1)

    in_specs, operands = [], []
    for l, p in enumerate(g_parts):
        land, sent = p if direct[l] else (p, None)
        in_specs.append(pl.BlockSpec((land.shape[0], tr, C), lambda layer, i, me_ref, l=l: (0, rows(l, layer, i), 0)))
        operands.append(land)
        if direct[l]:
            in_specs.append(pl.BlockSpec((None, tr, C), lambda layer, i, me_ref, l=l: (me_ref[l], rows(l, layer, i), 0)))
            operands.append(sent)
    if after is not None:
        in_specs.append(ANY)
        operands.append(after)
    tile = pl.BlockSpec((None, tr, C), lambda layer, i, me_ref: (layer, i, 0))
    out = jax.ShapeDtypeStruct((L, R, C), F32)
    return pl.pallas_call(
        body, name=name, out_shape=(out, out, out, out),
        grid_spec=pltpu.PrefetchScalarGridSpec(
            num_scalar_prefetch=1, grid=(L, nr), in_specs=[tile, tile, tile] + in_specs,
            out_specs=(tile, tile, tile, tile)),
        compiler_params=_params(),
    )(me, w, m, v, *operands)


def _pair_add(pieces, recv, core, name):
    _, _, R, C = pieces.shape
    tr = _row_tile(R, max(8, (512 * 1024 // max(C, 128)) // 8 * 8))

    def body(core_ref, a_ref, b_ref, o_ref):
        o_ref[...] = (a_ref[...].astype(F32) + b_ref[...].astype(F32)).astype(BF16)

    return pl.pallas_call(
        body, name=name, out_shape=jax.ShapeDtypeStruct((4, R, C), BF16),
        grid_spec=pltpu.PrefetchScalarGridSpec(
            num_scalar_prefetch=1, grid=(4, R // tr),
            in_specs=[pl.BlockSpec((None, None, tr, C), lambda k, i, core_ref: (core_ref[0], k, i, 0)),
                      pl.BlockSpec((None, tr, C), lambda k, i, core_ref: (k, i, 0))],
            out_specs=pl.BlockSpec((None, tr, C), lambda k, i, core_ref: (k, i, 0))),
        compiler_params=_params(),
    )(core, pieces, recv)


MESH = pl.DeviceIdType.MESH
ANY = pl.BlockSpec(memory_space=pl.ANY)


def _position():
    return lax.axis_index("x"), lax.axis_index("y"), lax.axis_index("c")


def _small_all_gather(v, name):
    m_per, n = v.shape

    def body(x_ref, out_ref, send_sems, recv_sems, local_sem):
        x, y, c = _position()
        me, sibling = (x, y, c), (x, y, 1 - c)
        chips = [(1 - x, y), (x, 1 - y), (1 - x, 1 - y)]

        def rows(px, py, pc):
            return out_ref.at[pl.ds((4 * px + 2 * py + pc) * m_per, m_per), :]

        def copy(k, block, to, src=None):
            return pltpu.make_async_remote_copy(
                src_ref=rows(*block) if src is None else src, dst_ref=rows(*block),
                send_sem=send_sems.at[k], recv_sem=recv_sems.at[k], device_id=to, device_id_type=MESH)

        mine = pltpu.make_async_copy(x_ref, rows(*me), local_sem)
        mine.start()
        first = [copy(0, me, sibling, src=x_ref)]
        first += [copy(1 + j, me, (*chip, c), src=x_ref) for j, chip in enumerate(chips)]
        for cp in first:
            cp.start()
        passed = [copy(4 + j, (*chip, c), sibling) for j, chip in enumerate(chips)]
        for j, chip in enumerate(chips):
            copy(1 + j, (*chip, c), me).wait_recv()
            passed[j].start()
        copy(0, sibling, me).wait_recv()
        for j, chip in enumerate(chips):
            copy(4 + j, (*chip, 1 - c), me).wait_recv()
        for cp in first + passed:
            cp.wait_send()
        mine.wait()

    return pl.pallas_call(
        body, name=name, out_shape=jax.ShapeDtypeStruct((N_DEV * m_per, n), v.dtype),
        in_specs=[pl.BlockSpec(memory_space=pltpu.VMEM)], out_specs=pl.BlockSpec(memory_space=pltpu.VMEM),
        scratch_shapes=[pltpu.SemaphoreType.DMA((7,)), pltpu.SemaphoreType.DMA((7,)), pltpu.SemaphoreType.DMA],
    )(v)


def _big_all_gather(shards, name):
    n_arr = len(shards)

    def body(*refs):
        x_refs, out_refs = refs[:n_arr], refs[n_arr:2 * n_arr]
        send_sems, recv_sems, local_sems = refs[2 * n_arr:]
        x, y, c = _position()
        me, sibling = (x, y, c), (x, y, 1 - c)
        chips = [(1 - x, y), (x, 1 - y), (1 - x, 1 - y)]

        def slot(a, px, py, pc):
            return out_refs[a].at[4 * px + 2 * py + pc]

        def copy(a, k, block, to, src=None):
            return pltpu.make_async_remote_copy(
                src_ref=slot(a, *block) if src is None else src, dst_ref=slot(a, *block),
                send_sem=send_sems.at[a, k], recv_sem=recv_sems.at[a, k], device_id=to, device_id_type=MESH)

        mine = [pltpu.make_async_copy(x_refs[a], slot(a, *me), local_sems.at[a]) for a in range(n_arr)]
        for cp in mine:
            cp.start()
        first = []
        for j, chip in enumerate(chips):
            first += [copy(a, 1 + j, me, (*chip, c), src=x_refs[a]) for a in range(n_arr)]
        first += [copy(a, 0, me, sibling, src=x_refs[a]) for a in range(n_arr)]
        for cp in first:
            cp.start()
        passed = []
        for j, chip in enumerate(chips):
            for a in range(n_arr):
                copy(a, 1 + j, (*chip, c), me).wait_recv()
                fwd = copy(a, 4 + j, (*chip, c), sibling)
                fwd.start()
                passed.append(fwd)
        for a in range(n_arr):
            copy(a, 0, sibling, me).wait_recv()
        for j, chip in enumerate(chips):
            for a in range(n_arr):
                copy(a, 4 + j, (*chip, 1 - c), me).wait_recv()
        for cp in first + passed:
            cp.wait_send()
        for cp in mine:
            cp.wait()

    return pl.pallas_call(
        body, name=name,
        out_shape=tuple(jax.ShapeDtypeStruct((N_DEV,) + s.shape, s.dtype) for s in shards),
        in_specs=[ANY] * n_arr, out_specs=tuple([ANY] * n_arr),
        scratch_shapes=[pltpu.SemaphoreType.DMA((n_arr, 7)), pltpu.SemaphoreType.DMA((n_arr, 7)),
                        pltpu.SemaphoreType.DMA((n_arr,))],
    )(*shards)


def _sibling_exchange(pieces, name):
    n_arr = len(pieces)

    def body(*refs):
        p_refs, out_refs = refs[:n_arr], refs[n_arr:2 * n_arr]
        send_sems, recv_sems = refs[2 * n_arr:]
        x, y, c = _position()
        copies = [pltpu.make_async_remote_copy(
            src_ref=p_refs[a].at[1 - c], dst_ref=out_refs[a], send_sem=send_sems.at[a], recv_sem=recv_sems.at[a],
            device_id=(x, y, 1 - c), device_id_type=MESH) for a in range(n_arr)]
        for cp in copies:
            cp.start()
        for cp in copies:
            cp.wait()

    return pl.pallas_call(
        body, name=name,
        out_shape=tuple(jax.ShapeDtypeStruct(p.shape[1:], p.dtype) for p in pieces),
        in_specs=[ANY] * n_arr, out_specs=tuple([ANY] * n_arr),
        scratch_shapes=[pltpu.SemaphoreType.DMA((n_arr,)), pltpu.SemaphoreType.DMA((n_arr,))],
    )(*pieces)


def _chip_exchange(sums, name):
    n_arr = len(sums)

    def body(*refs):
        s_refs, out_refs = refs[:n_arr], refs[n_arr:2 * n_arr]
        send_sems, recv_sems, local_sems = refs[2 * n_arr:]
        x, y, c = _position()
        my_chip = 2 * x + y
        chips = [(1 - x, y), (x, 1 - y), (1 - x, 1 - y)]
        mine = [pltpu.make_async_copy(s_refs[a].at[my_chip], out_refs[a].at[my_chip], local_sems.at[a])
                for a in range(n_arr)]
        for cp in mine:
            cp.start()
        copies = []
        for j, (px, py) in enumerate(chips):
            copies += [pltpu.make_async_remote_copy(
                src_ref=s_refs[a].at[2 * px + py], dst_ref=out_refs[a].at[my_chip],
                send_sem=send_sems.at[a, j], recv_sem=recv_sems.at[a, j],
                device_id=(px, py, c), device_id_type=MESH) for a in range(n_arr)]
        for cp in copies:
            cp.start()
        for j, (px, py) in enumerate(chips):
            for a in range(n_arr):
                pltpu.make_async_remote_copy(
                    src_ref=s_refs[a].at[my_chip], dst_ref=out_refs[a].at[2 * px + py],
                    send_sem=send_sems.at[a, j], recv_sem=recv_sems.at[a, j],
                    device_id=(px, py, c), device_id_type=MESH).wait_recv()
        for cp in copies:
            cp.wait_send()
        for cp in mine:
            cp.wait()

    return pl.pallas_call(
        body, name=name,
        out_shape=tuple(jax.ShapeDtypeStruct(s.shape, s.dtype) for s in sums),
        in_specs=[ANY] * n_arr, out_specs=tuple([ANY] * n_arr),
        scratch_shapes=[pltpu.SemaphoreType.DMA((n_arr, 3)), pltpu.SemaphoreType.DMA((n_arr, 3)),
                        pltpu.SemaphoreType.DMA((n_arr,))],
    )(*sums)


HBM = pl.BlockSpec(memory_space=pltpu.HBM)
SEM = pl.BlockSpec(memory_space=pltpu.SEMAPHORE)
EFFECT = pltpu.SideEffectType.DATAFLOW_SIDE_EFFECTING
RELATIONS = [(rx, ry, rc) for rx in (0, 1) for ry in (0, 1) for rc in (0, 1)][1:]


SAME_CORE = [r for r in RELATIONS if r == (0, 0, 1) or r[2] == 0]


CHIPS = [r for r in RELATIONS if r[2] == 0]


def _exchange_copies(src_refs, land_refs, send_sems, recv_sems, scatter, receive_side, relations):
    x, y, c = _position()
    index = (lambda px, py, pc: 2 * px + py) if relations == CHIPS else (lambda px, py, pc: 4 * px + 2 * py + pc)
    me = index(x, y, c)
    copies = []
    for k, (rx, ry, rc) in enumerate(relations):
        peer = ((1 - x) if rx else x, (1 - y) if ry else y, (1 - c) if rc else c)
        peer_index = index(*peer)
        for a, (src, land) in enumerate(zip(src_refs, land_refs)):
            copies.append(pltpu.make_async_remote_copy(
                src_ref=src.at[peer_index] if scatter else src,
                dst_ref=land.at[peer_index if receive_side else me],
                send_sem=send_sems.at[a * len(relations) + k], recv_sem=recv_sems.at[a * len(relations) + k],
                device_id=peer, device_id_type=MESH))
    return copies


def _exchange_start(srcs, scatter, after, name, relations=RELATIONS):
    n = len(srcs)
    land_shapes = [(s.shape if scatter else (N_DEV,) + s.shape) for s in srcs]

    def body(*refs):
        src_refs, land_refs = refs[:n], refs[n:2 * n]
        send_sems, recv_sems = refs[2 * n + 1], refs[2 * n + 2]
        token = refs[-1]
        for cp in _exchange_copies(src_refs, land_refs, send_sems, recv_sems, scatter, False, relations):
            cp.start()
        token[...] = jnp.zeros_like(token)

    sems = pltpu.SemaphoreType.DMA((n * len(relations),))
    outs = pl.pallas_call(
        body, name=name,
        out_shape=(sems, sems, *[pltpu.HBM(s.shape, s.dtype) for s in srcs],
                   *[pltpu.HBM(shape, s.dtype) for shape, s in zip(land_shapes, srcs)],
                   jax.ShapeDtypeStruct((8, 128), F32)),
        in_specs=[HBM] * (2 * n) + [ANY],
        out_specs=(SEM, SEM, *[HBM] * (2 * n), pl.BlockSpec(memory_space=pltpu.VMEM)),
        input_output_aliases={a: 2 + a for a in range(2 * n)},
        compiler_params=pltpu.CompilerParams(has_side_effects=EFFECT),
    )(*[pltpu.with_memory_space_constraint(s, pltpu.HBM) for s in srcs],
      *[pltpu.with_memory_space_constraint(lax.empty(shape, s.dtype), pltpu.HBM)
        for shape, s in zip(land_shapes, srcs)], after)
    return outs[0], outs[1], outs[2:2 + n], outs[2 + n:2 + 2 * n], outs[-1]


def _exchange_wait(started, scatter, after, name, relations=RELATIONS):
    send_sems, recv_sems, srcs, lands, _ = started
    n = len(srcs)

    def body(*refs):
        src_refs, land_refs = refs[:n], refs[n:2 * n]
        send_sems, recv_sems = refs[2 * n], refs[2 * n + 1]
        copies = _exchange_copies(src_refs, land_refs, send_sems, recv_sems, scatter, True, relations)
        for cp in copies:
            cp.wait_send()
        for cp in copies:
            cp.wait_recv()

    outs = pl.pallas_call(
        body, name=name,
        out_shape=(*[pltpu.HBM(s.shape, s.dtype) for s in srcs], *[pltpu.HBM(t.shape, t.dtype) for t in lands]),
        in_specs=[HBM] * (2 * n) + [SEM, SEM, ANY], out_specs=tuple([HBM] * (2 * n)),
        input_output_aliases={a: a for a in range(2 * n)},
        compiler_params=pltpu.CompilerParams(has_side_effects=EFFECT),
    )(*srcs, *lands, send_sems, recv_sems, after)
    return outs[:n], outs[n:]


def _sibling_forward(lands, name):
    n = len(lands)

    def body(*refs):
        in_refs, out_refs = refs[:n], refs[n:2 * n]
        send_sems, recv_sems = refs[2 * n:]
        x, y, c = _position()
        copies, arrivals = [], []
        for j, (px, py) in enumerate([(1 - x, y), (x, 1 - y), (1 - x, 1 - y)]):
            held, coming = 4 * px + 2 * py + c, 4 * px + 2 * py + (1 - c)
            for a in range(n):
                sems = dict(send_sem=send_sems.at[a, j], recv_sem=recv_sems.at[a, j], device_id=(x, y, 1 - c),
                            device_id_type=MESH)
                copies.append(pltpu.make_async_remote_copy(
                    src_ref=in_refs[a].at[held], dst_ref=out_refs[a].at[held], **sems))
                arrivals.append(pltpu.make_async_remote_copy(
                    src_ref=in_refs[a].at[held], dst_ref=out_refs[a].at[coming], **sems))
        for cp in copies:
            cp.start()
        for cp in copies:
            cp.wait_send()
        for cp in arrivals:
            cp.wait_recv()

    return pl.pallas_call(
        body, name=name, out_shape=tuple(jax.ShapeDtypeStruct(t.shape, t.dtype) for t in lands),
        in_specs=[ANY] * n, out_specs=tuple([ANY] * n), input_output_aliases={a: a for a in range(n)},
        scratch_shapes=[pltpu.SemaphoreType.DMA((n, 3)), pltpu.SemaphoreType.DMA((n, 3))],
    )(*lands)


W_IN_SHARD = N_IN // N_DEV
F_SHARD = F_COL // W_IN_SHARD
F_LO = F_COL - F_SHARD * W_IN_SHARD


def _w_ffn_in_view(w):
    return jnp.transpose(w, (0, 2, 1))


def _w_in_rearranged(g):
    parts = [g[d] for d in range(N_DEV)]
    with_f = parts[F_SHARD]
    parts[F_SHARD:F_SHARD + 1] = [with_f[:, :F_LO], with_f[:, F_LO + N_FORGET:]]
    parts += [with_f[:, F_LO:F_LO + N_FORGET], jnp.zeros((with_f.shape[0], BLK - N_FORGET), with_f.dtype)]
    return jnp.concatenate(parts, axis=1)


def _w_in_pieces(dw_r):
    def original(lo, hi):
        shift = 0 if hi <= F_COL else N_FORGET
        return dw_r[:, lo - shift:hi - shift]

    pieces = []
    for d in range(N_DEV):
        lo, hi = d * W_IN_SHARD, (d + 1) * W_IN_SHARD
        if d == F_SHARD:
            pieces.append(jnp.concatenate([original(lo, F_COL), dw_r[:, N_MAIN:N_MAIN + N_FORGET],
                                           original(F_COL + N_FORGET, hi)], axis=1))
        else:
            pieces.append(original(lo, hi))
    return jnp.stack(pieces)


def _row_pieces(dw):
    return dw.reshape(N_DEV, dw.shape[0] // N_DEV, dw.shape[1])


def _branch_pieces(dw):
    k, w, d = dw.shape
    return jnp.transpose(dw.reshape(k, w, N_DEV, d // N_DEV), (2, 0, 1, 3)).reshape(N_DEV, k * w, d // N_DEV)


def _pair_major(p8):
    return jnp.stack([p8[0::2], p8[1::2]])


def _pairs_col(a):
    return jnp.transpose(a.reshape(a.shape[0], 4, 2), (1, 0, 2))


def _pairs_row(a):
    return jnp.transpose(a.reshape(a.shape[0], 4, 2), (1, 2, 0))


def _heads_from_col(a):
    return jnp.transpose(a, (1, 0, 2)).reshape(a.shape[1], 8)


def _heads_from_row(a):
    return jnp.transpose(a, (2, 0, 1)).reshape(a.shape[2], 8)


def _pad_lanes(a, n):
    return jnp.pad(a, [(0, 0)] * (a.ndim - 1) + [(0, n - a.shape[-1])])


SMALL_SEGMENTS = (("dmod", 2 * 6 * D_MODEL), ("norm_mix_g", 2 * D_MODEL), ("norm_ffn_g", 2 * D_MODEL),
                  ("final_norm_g", D_MODEL), ("b_forget", 128), ("sinks", 128), ("rel_bias", 4224))
SMALL_ROWS = 176


def _pack_small(parts):
    flat = [_pad_lanes(parts[name].reshape(1, -1), size) for name, size in SMALL_SEGMENTS]
    total = sum(size for _, size in SMALL_SEGMENTS)
    flat.append(jnp.zeros((1, SMALL_ROWS * 128 - total), F32))
    return jnp.concatenate(flat, axis=1).reshape(SMALL_ROWS, 128)


def _unpack_small(packed, shapes):
    flat = packed.reshape(-1)
    out, pos = {}, 0
    for name, size in SMALL_SEGMENTS:
        shape = shapes[name]
        count = 1
        for d in shape:
            count *= d
        out[name] = flat[pos:pos + count].reshape(shape)
        pos += size
    return out


def kernel(x, c, norm_mix_g, norm_ffn_g, w_ada, b_ada, w_in, b_forget, sinks, rel_bias, w_branch, w_out, w_ffn_in, w_ffn_out, final_norm_g, loss_target, m_norm_mix_g, m_norm_ffn_g, m_w_ada, m_b_ada, m_w_in, m_b_forget, m_sinks, m_rel_bias, m_w_branch, m_w_out, m_w_ffn_in, m_w_ffn_out, m_final_norm_g, v_norm_mix_g, v_norm_ffn_g, v_w_ada, v_b_ada, v_w_in, v_b_forget, v_sinks, v_rel_bias, v_w_branch, v_w_out, v_w_ffn_in, v_w_ffn_out, v_final_norm_g):
    depth = w_in.shape[0]
    S, D = x.shape[1], x.shape[2]
    assert S % GROUP == 0 and S >= ATTN_WINDOW["c"] * BLK
    px, py, pc = _position()
    me = 4 * px + 2 * py + pc
    x0 = x[0]

    assert depth == 2
    big_weights = (w_in, w_branch, w_out, w_ffn_in, w_ffn_out)
    me_arr = jnp.stack([me, me]).astype(jnp.int32)
    me_in_arr = jnp.stack([2 * px + py, me]).astype(jnp.int32)

    def slabs(landed, mine):
        return [jnp.where(me == d, mine, landed[d]) for d in range(N_DEV)]

    def rest_matrices(g_branch, g_out, g_fin, g_fout):
        return (jnp.transpose(jnp.stack(g_branch), (1, 2, 0, 3)).reshape(3, 512, D),
                jnp.concatenate(g_out, axis=0), jnp.concatenate(g_fin, axis=0), jnp.concatenate(g_fout, axis=0))

    def finish_gather(started, after, name):
        mine, landed = _exchange_wait(started, False, after, f"{name}_wait", SAME_CORE)
        landed = _sibling_forward(landed, f"{name}_forward")
        return [slabs(t, s) for t, s in zip(landed, mine)]

    w_fin_t = _w_ffn_in_view(w_ffn_in)
    shards = [[t.astype(BF16) for t in (w_in[l], w_branch[l], w_out[l], w_fin_t[l], w_ffn_out[l])]
              for l in range(depth)]
    gathered_in0 = _big_all_gather(shards[0][:1], "comm_gather_w_in0")[0]
    gather_rest0 = _exchange_start(shards[0][1:], False, gathered_in0, "comm_gather_rest0_start", SAME_CORE)
    gather1 = _exchange_start(shards[1], False, gather_rest0[4], "comm_gather_weights1_start", SAME_CORE)
    W_in, W_branch, W_out, W_fin, W_fout = ([None, None] for _ in range(5))
    W_in[0] = _w_in_rearranged(gathered_in0)

    c_all = _small_all_gather(c.reshape(8, 128), "comm_gather_c").reshape(N_DEV, D)
    mod_cols = _ada_fwd(c_all, w_ada, "ada_fwd")
    mod_all = _small_all_gather(mod_cols.reshape(-1, 128), "comm_gather_mod")
    mod_all = mod_all.reshape(N_DEV, depth, N_DEV, w_ada.shape[2])
    mod_mine = lax.dynamic_index_in_dim(mod_all, me, axis=2, keepdims=False)
    mod = jnp.transpose(mod_mine, (1, 0, 2)).reshape(depth, 6 * D) + b_ada + gather1[4][0:1, 0:1]
    mods = [[mod[l:l + 1, k * D:(k + 1) * D] for k in range(6)] for l in range(depth)]

    slopes = jnp.exp2(-jnp.arange(1, 9, dtype=F32))
    saved = []
    xs = x0
    for l in range(depth):
        if l == 1:
            g_in1, *g_rest1 = finish_gather(gather1, xs, "comm_gather_weights1")
            W_in[1] = _w_in_rearranged(g_in1)
            W_branch[1], W_out[1], W_fin[1], W_fout[1] = rest_matrices(*g_rest1)
        sh_m, sc_m, g_m, sh_f, sc_f, g_f = mods[l]
        gm, gf = norm_mix_g[l:l + 1], norm_ffn_g[l:l + 1]
        bfor = _pad_lanes(b_forget[l:l + 1], BLK)
        h = _norm_mod_fwd(xs, gm, sh_m, sc_m, f"norm_mix_fwd{l}")
        qkv = _matmul(h, W_in[l], "nn", BF16, f"proj_qkv{l}", TILES["proj_qkv"], n=N_QKV)
        gates = _matmul(h, W_in[l], "nn", F32, f"proj_gates{l}", TILES["proj_gates"], n=N_GATES,
                        b_off=N_QKV // TILES["proj_gates"][1])
        fb = _matmul(h, W_in[l], "nn", F32, f"proj_forget{l}", TILES["proj_forget"], n=BLK, b_off=N_MAIN // BLK)
        cum = _forget_fwd(fb, bfor, f"forget_fwd{l}")[:, :N_FORGET]
        cum_col, cum_row = _pairs_col(cum), _pairs_row(cum)
        tiles, tiles_t = _rel_expand(_rel_bases(rel_bias[l]), f"rel_expand{l}")
        o_a, lse_a = _attn_fwd("a", qkv, f"attn_a_fwd{l}", sinks=sinks[l], slopes=slopes)
        o_b, lse_b = _attn_fwd("b", qkv, f"attn_b_fwd{l}", cq_col=cum_col, ck_row=cum_row)
        o_c, lse_c = _attn_fwd("c", qkv, f"attn_c_fwd{l}", bias=tiles)
        if l == 0:
            W_branch[0], W_out[0], W_fin[0], W_fout[0] = rest_matrices(*finish_gather(gather_rest0, o_c, "comm_gather_rest0"))
        merged = _merge_fwd(o_a, o_b, o_c, gates, W_branch[l], f"merge_fwd{l}")
        x1, mix = _matmul_resid(merged, W_out[l], xs, g_m, f"out_proj{l}", TILES["out_proj"])
        h2 = _norm_mod_fwd(x1, gf, sh_f, sc_f, f"norm_ffn_fwd{l}")
        act = _ffn_in_fwd(h2, W_fin[l], f"ffn_in_fwd{l}")
        x2, ffn = _matmul_resid(act, W_fout[l], x1, g_f, f"ffn_out{l}", TILES["ffn_out"])
        saved.append(dict(x=xs, h=h, qkv=qkv, gates=gates, fb=fb, bfor=bfor, cum_col=cum_col, cum_row=cum_row,
                          tiles_t=tiles_t, o=(o_a, o_b, o_c), lse=(lse_a, lse_b, lse_c), merged=merged, mix=mix,
                          x1=x1, h2=h2, act=act, ffn=ffn))
        xs = x2

    dx, loss_tile, d_final_g = _final_loss(xs, loss_target[0], final_norm_g.reshape(1, D), "final_loss")
    loss = lax.psum(loss_tile[0, 0], ("x", "y", "c"))

    grads = {k: [None] * depth for k in ("w_in", "w_branch", "w_out", "w_ffn_in", "w_ffn_out", "norm_mix_g",
                                          "norm_ffn_g", "b_forget", "sinks", "rel_bias", "dmod")}
    def rest_pieces(l):
        return [_branch_pieces(grads["w_branch"][l]), _row_pieces(grads["w_out"][l]),
                _row_pieces(grads["w_ffn_in"][l]), _row_pieces(grads["w_ffn_out"][l])]

    reduce1 = reduce_rest0 = reduce_in0 = None
    for l in reversed(range(depth)):
        sv = saved[l]
        sh_m, sc_m, g_m, sh_f, sc_f, g_f = mods[l]
        if l == 0:
            g_f = g_f + reduce1[4][0:1, 0:1]
        gm, gf = norm_mix_g[l:l + 1], norm_ffn_g[l:l + 1]
        df, d_g_f = _gate_bwd(dx, sv["ffn"], g_f, f"ffn_gate_bwd{l}")
        du_g, du_u = _ffn_mid_bwd(sv["h2"], df, W_fin[l], W_fout[l], f"ffn_mid_bwd{l}")
        du = jnp.concatenate([du_g, du_u], axis=1)
        grads["w_ffn_out"][l] = _matmul(sv["act"], df, "tn", BF16, f"wgrad_ffn_out{l}", TILES["wgrad_ffn_out"])
        grads["w_ffn_in"][l] = _matmul(du, sv["h2"], "tn", BF16, f"wgrad_ffn_in{l}", TILES["wgrad_ffn_in"])
        dh2 = _matmul(du, W_fin[l], "nn", F32, f"dgrad_ffn_in{l}", TILES["dgrad_ffn_in"])
        dx1, d_sh_f, d_sc_f, d_gf = _norm_mod_bwd(sv["x1"], dh2, dx, gf, sc_f, f"norm_ffn_bwd{l}")
        dmix, d_g_m = _gate_bwd(dx1, sv["mix"], g_m, f"mix_gate_bwd{l}")
        grads["w_out"][l] = _matmul(sv["merged"], dmix, "tn", BF16, f"wgrad_out{l}", TILES["wgrad_out"])
        dmerged = _matmul(dmix, W_out[l], "nt", F32, f"dgrad_out{l}", TILES["dgrad_out"])
        o_a, o_b, o_c = sv["o"]
        dgates, dy, do_a, do_b, do_c, dl_a, dl_b, dl_c = _merge_bwd(
            dmerged, o_a, o_b, o_c, sv["gates"], W_branch[l], f"merge_bwd{l}")
        dwb = [_matmul(o_k, dy, "tn", BF16, f"wgrad_branch{l}_{k}", TILES["wgrad_branch"], n=D,
                       b_off=k * (D // TILES["wgrad_branch"][1])) for k, o_k in enumerate((o_a, o_b, o_c))]
        grads["w_branch"][l] = jnp.stack(dwb)
        lse_rows = [_pairs_row(_heads_from_col(t)) for t in sv["lse"]]
        if l == 0:
            reduce_rest0 = _exchange_start(rest_pieces(0), True, dy, "comm_reduce_rest0_start")
            lse_rows = [t + reduce_rest0[4][0:1, 0:1] for t in lse_rows]
        dqt_a, dk_a, dv_a, dsink = _attn_bwd("a", sv["qkv"], do_a, lse_rows[0], _pairs_row(dl_a), f"attn_a_bwd{l}",
                                             sinks=sinks[l], slopes=slopes)
        dqt_b, dk_b, dv_b, dck, dcq = _attn_bwd("b", sv["qkv"], do_b, lse_rows[1], _pairs_row(dl_b),
                                                f"attn_b_bwd{l}", cq_row=sv["cum_row"], ck_col=sv["cum_col"])
        dqt_c, dk_c, dv_c, dtiles_t = _attn_bwd("c", sv["qkv"], do_c, lse_rows[2], _pairs_row(dl_c),
                                                f"attn_c_bwd{l}", bias_t=sv["tiles_t"])
        grads["sinks"][l] = dsink[:, :2, 0].reshape(8)
        grads["rel_bias"][l] = _rel_reduce(dtiles_t, f"rel_reduce{l}")[:, 0, :N_REL]
        dcum_k = _pad_lanes(_heads_from_col(dck), BLK)
        dcum_q = _pad_lanes(_heads_from_row(dcq), BLK)
        dfb, d_bfor = _forget_bwd(dcum_q, dcum_k, sv["fb"], sv["bfor"], f"forget_bwd{l}")
        grads["b_forget"][l] = d_bfor[0, :N_FORGET]
        dproj = jnp.concatenate(
            [t.astype(BF16) for t in (dqt_a.T, dk_a, dv_a, dqt_b.T, dk_b, dv_b, dqt_c.T, dk_c, dv_c)]
            + [dgates, dfb.astype(BF16)], axis=1)
        grads["w_in"][l] = _matmul(sv["h"], dproj, "tn", BF16, f"wgrad_in{l}", TILES["wgrad_in"])
        dh = _matmul(dproj, W_in[l], "nt", F32, f"dgrad_in{l}", TILES["dgrad_in"])
        dx, d_sh_m, d_sc_m, d_gm = _norm_mod_bwd(sv["x"], dh, dx1, gm, sc_m, f"norm_mix_bwd{l}")
        grads["norm_mix_g"][l] = d_gm[0]
        grads["norm_ffn_g"][l] = d_gf[0]
        grads["dmod"][l] = jnp.concatenate([d_sh_m, d_sc_m, d_g_m, d_sh_f, d_sc_f, d_g_f], axis=1)[0]
        if l == 1:
            reduce1 = _exchange_start([_w_in_pieces(grads["w_in"][1])] + rest_pieces(1), True, dx, "comm_reduce1_start")

    grad_x = dx.reshape(x.shape)

    small_shapes = dict(dmod=b_ada.shape, norm_mix_g=norm_mix_g.shape, norm_ffn_g=norm_ffn_g.shape,
                        final_norm_g=final_norm_g.shape, b_forget=b_forget.shape, sinks=sinks.shape,
                        rel_bias=rel_bias.shape)
    mine_small = _pack_small(dict(
        dmod=jnp.stack(grads["dmod"]), norm_mix_g=jnp.stack(grads["norm_mix_g"]),
        norm_ffn_g=jnp.stack(grads["norm_ffn_g"]), final_norm_g=d_final_g[0],
        b_forget=_pad_lanes(jnp.stack(grads["b_forget"]).reshape(1, -1), 128),
        sinks=_pad_lanes(jnp.stack(grads["sinks"]).reshape(1, -1), 128),
        rel_bias=_pad_lanes(jnp.stack(grads["rel_bias"]).reshape(1, -1), 4224)))
    all_small = _small_all_gather(mine_small, "comm_gather_small").reshape(N_DEV, SMALL_ROWS, 128)
    pieces_in0 = _pair_major(_w_in_pieces(grads["w_in"][0]))
    from_sibling = _sibling_exchange([pieces_in0], "comm_reduce_in0_sibling")[0]
    pair_sum_in0 = _pair_add(pieces_in0, from_sibling, pc.astype(jnp.int32).reshape(1), "pair_add_in0")
    reduce_in0 = _exchange_start([pair_sum_in0], True, all_small, "comm_reduce_in0_start", CHIPS)
    in0_started = reduce_in0[4]

    def pack_params(b_ada_, nm, nf, fn, bf, sk, rb):
        return _pack_small(dict(dmod=b_ada_, norm_mix_g=nm, norm_ffn_g=nf, final_norm_g=fn,
                                b_forget=_pad_lanes(bf.reshape(1, -1), 128), sinks=_pad_lanes(sk.reshape(1, -1), 128),
                                rel_bias=_pad_lanes(rb.reshape(1, -1), 4224)))

    small_out = _adamw(
        pack_params(b_ada, norm_mix_g, norm_ffn_g, final_norm_g, b_forget, sinks, rel_bias)[None],
        pack_params(m_b_ada, m_norm_mix_g, m_norm_ffn_g, m_final_norm_g, m_b_forget, m_sinks, m_rel_bias)[None],
        pack_params(v_b_ada, v_norm_mix_g, v_norm_ffn_g, v_final_norm_g, v_b_forget, v_sinks, v_rel_bias)[None],
        [all_small], "adamw_small", me_arr, after=in0_started)
    small_out = [_unpack_small(t[0], small_shapes) for t in small_out]

    dmod_all = all_small[:, :96].reshape(N_DEV, depth, 6 * D)
    dmod_cols = lax.dynamic_slice_in_dim(dmod_all, me * w_ada.shape[2], w_ada.shape[2], axis=2)
    d_w_ada = _ada_bwd(jnp.transpose(c_all), jnp.transpose(dmod_cols, (1, 0, 2)), "ada_bwd")

    big = {"w_ada": _adamw(w_ada, m_w_ada, v_w_ada, [d_w_ada[l:l + 1] for l in range(depth)], "adamw_w_ada", me_arr,
                           after=in0_started)}
    sent1, landed1 = _exchange_wait(reduce1, True, big["w_ada"][0], "comm_reduce1_wait")
    sent_rest0, landed_rest0 = _exchange_wait(reduce_rest0, True, landed1[0], "comm_reduce_rest0_wait")
    parts = {"w_in": [None, (landed1[0], sent1[0])]}
    for a, name in enumerate(("w_branch", "w_out", "w_ffn_in", "w_ffn_out")):
        parts[name] = [(landed_rest0[a], sent_rest0[a]), (landed1[1 + a], sent1[1 + a])]

    def update(name, w, m, v, view=lambda t: t):
        per_layer = lambda t: t.reshape(depth, -1, t.shape[-1])
        outs = _adamw(*[per_layer(view(t)) for t in (w, m, v)], parts[name], f"adamw_{name}",
                      me_in_arr if name == "w_in" else me_arr)
        big[name] = [view(t).reshape(w.shape) for t in outs]

    update("w_ffn_in", w_ffn_in, m_w_ffn_in, v_w_ffn_in, _w_ffn_in_view)
    update("w_ffn_out", w_ffn_out, m_w_ffn_out, v_w_ffn_out)
    update("w_branch", w_branch, m_w_branch, v_w_branch)
    update("w_out", w_out, m_w_out, v_w_out)
    sent_in0, landed_in0 = _exchange_wait(reduce_in0, True, big["w_out"][0], "comm_reduce_in0_wait", CHIPS)
    parts["w_in"][0] = (landed_in0[0], sent_in0[0])
    update("w_in", w_in, m_w_in, v_w_in)

    def leaf(kind, name):
        if name in big:
            return big[name][kind]
        return small_out[kind]["dmod" if name == "b_ada" else name]

    order = ["norm_mix_g", "norm_ffn_g", "w_ada", "b_ada", "w_in", "b_forget", "sinks", "rel_bias", "w_branch",
             "w_out", "w_ffn_in", "w_ffn_out", "final_norm_g"]
    return (loss, grad_x, *[leaf(0, n) for n in order], *[leaf(1, n) for n in order],
            *[leaf(2, n) for n in order], *[leaf(3, n) for n in order])
```

```python
import functools

import jax
import jax.numpy as jnp
from jax import lax
from jax.experimental import pallas as pl
from jax.experimental.pallas import tpu as pltpu

F32 = jnp.float32
BF16 = jnp.bfloat16
NEG_INF = -1e30
EPS = 1e-6
N_DEV = 8
BLK = 128
GROUP = 4 * BLK
VMEM_LIMIT_BYTES = 56 * 1024 * 1024

D_MODEL = 1024
N_QKV = 3840
N_GATES = 3072
N_MAIN = N_QKV + N_GATES
N_FORGET = 8
N_IN = N_MAIN + N_FORGET
N_INR = N_MAIN + BLK
F_COL = 2304
FFN_HIDDEN = 2816
N_REL = 257

ADAM_LR, ADAM_B1, ADAM_B2, ADAM_EPS, ADAM_WD, ADAM_STEP = 0.001, 0.9, 0.999, 1e-08, 0.01, 10

NN = (((1,), (0,)), ((), ()))
NT = (((1,), (1,)), ((), ()))
TN = (((0,), (0,)), ((), ()))
HIGHEST = lax.Precision.HIGHEST

ATTN_COLS = {"a": (0, 4, 5), "b": (6, 10, 14), "c": (18, 22, 26)}
ATTN_WINDOW = {"a": 2, "c": 5}
ATTN_BLOCKS_PER_STEP = {"a": 4, "b": GROUP // BLK, "c": 2}


def _params():
    return pltpu.CompilerParams(vmem_limit_bytes=VMEM_LIMIT_BYTES)


def _tile(n, target):
    best = None
    t = 128
    while t <= min(n, target):
        if n % t == 0:
            best = t
        t += 128
    return best if best is not None else n


def _row_tile(n, target):
    t = min(n, target)
    while n % t:
        t -= 8
    return t


TILES = {
    "proj_qkv": (1024, 1280, 1024), "proj_gates": (1024, 768, 1024), "proj_forget": (1024, 128, 1024),
    "ffn_out": (1024, 512, 2816), "ffn_fused": (512, 1408),
    "wgrad_ffn_out": (1408, 1024, 1024), "wgrad_ffn_in": (1408, 1024, 1024), "dgrad_ffn_in": (1024, 1024, 1408),
    "wgrad_out": (1024, 1024, 1024), "wgrad_branch": (512, 1024, 1024),
    "wgrad_in": (1024, 1408, 1024), "dgrad_in": (1024, 1024, 1408),
}


def _matmul(a, b, mode, out_dtype, name, tiles, *, n=None, a_off=0, b_off=0, m=None, after=None):
    tm, tn, tk = tiles
    if mode == "nn":
        M, K = a.shape if m is None else (m, a.shape[1])
        N = b.shape[1] if n is None else n
    elif mode == "nt":
        M, K = a.shape
        N = b.shape[0] if n is None else n
    else:
        K = a.shape[0]
        M = a.shape[1] if m is None else m
        N = b.shape[1] if n is None else n
    tm = _tile(M, tm) if M % 128 == 0 else M
    tn = _tile(N, tn)
    tk = _tile(K, tk)
    nk = K // tk
    dims = {"nn": NN, "nt": NT, "tn": TN}[mode]
    if mode == "nn":
        a_spec = pl.BlockSpec((tm, tk), lambda i, j, k: (i + a_off, k))
        b_spec = pl.BlockSpec((tk, tn), lambda i, j, k: (k, j + b_off))
    elif mode == "nt":
        a_spec = pl.BlockSpec((tm, tk), lambda i, j, k: (i + a_off, k))
        b_spec = pl.BlockSpec((tn, tk), lambda i, j, k: (j + b_off, k))
    else:
        a_spec = pl.BlockSpec((tk, tm), lambda i, j, k: (k, i + a_off))
        b_spec = pl.BlockSpec((tk, tn), lambda i, j, k: (k, j + b_off))

    def body(a_ref, b_ref, *rest):
        o_ref, acc_ref = rest[-2:]
        k = pl.program_id(2)
        part = lax.dot_general(a_ref[...], b_ref[...], dims, preferred_element_type=F32)
        if nk == 1:
            o_ref[...] = part.astype(o_ref.dtype)
        else:
            @pl.when(k == 0)
            def _():
                acc_ref[...] = part

            @pl.when(k > 0)
            def _():
                acc_ref[...] += part

            @pl.when(k == nk - 1)
            def _():
                o_ref[...] = acc_ref[...].astype(o_ref.dtype)

    return pl.pallas_call(
        body, name=name,
        out_shape=jax.ShapeDtypeStruct((M, N), out_dtype),
        grid=(M // tm, N // tn, nk),
        in_specs=[a_spec, b_spec] + ([ANY] if after is not None else []),
        out_specs=pl.BlockSpec((tm, tn), lambda i, j, k: (i, j)),
        scratch_shapes=[pltpu.VMEM((tm, tn) if nk > 1 else (8, 128), F32)],
        compiler_params=_params(),
    )(a, b, *([after] if after is not None else []))


def _matmul_resid(a, b, resid, gate, name, tiles, after=None):
    M, K = a.shape
    N = b.shape[1]
    tm, tn, tk = (_tile(d, t) for d, t in zip((M, N, K), tiles))
    nk = K // tk

    def body(a_ref, b_ref, r_ref, g_ref, *rest):
        o_ref, s_ref, acc_ref = rest[-3:]
        k = pl.program_id(2)
        part = jnp.dot(a_ref[...], b_ref[...], preferred_element_type=F32)

        def finish(acc):
            o_ref[...] = r_ref[...] + g_ref[...] * acc
            s_ref[...] = acc.astype(BF16)

        if nk == 1:
            finish(part)
        else:
            @pl.when(k == 0)
            def _():
                acc_ref[...] = part

            @pl.when(k > 0)
            def _():
                acc_ref[...] += part

            @pl.when(k == nk - 1)
            def _():
                finish(acc_ref[...])

    return pl.pallas_call(
        body, name=name,
        out_shape=(jax.ShapeDtypeStruct((M, N), F32), jax.ShapeDtypeStruct((M, N), BF16)),
        grid=(M // tm, N // tn, nk),
        in_specs=[pl.BlockSpec((tm, tk), lambda i, j, k: (i, k)),
                  pl.BlockSpec((tk, tn), lambda i, j, k: (k, j)),
                  pl.BlockSpec((tm, tn), lambda i, j, k: (i, j)),
                  pl.BlockSpec((1, tn), lambda i, j, k: (0, j))] + ([ANY] if after is not None else []),
        out_specs=(pl.BlockSpec((tm, tn), lambda i, j, k: (i, j)),
                   pl.BlockSpec((tm, tn), lambda i, j, k: (i, j))),
        scratch_shapes=[pltpu.VMEM((tm, tn) if nk > 1 else (8, 128), F32)],
        compiler_params=_params(),
    )(a, b, resid, gate, *([after] if after is not None else []))


def _norm_mod_fwd(x, g, shift, scale, name):
    S, D = x.shape
    ts = _row_tile(S, 256)

    def body(x_ref, g_ref, sh_ref, sc_ref, h_ref):
        xv = x_ref[...]
        rstd = lax.rsqrt(jnp.mean(xv * xv, axis=-1, keepdims=True) + EPS)
        y = xv * rstd * g_ref[...]
        h_ref[...] = (y * (1.0 + sc_ref[...]) + sh_ref[...]).astype(BF16)

    row = pl.BlockSpec((1, D), lambda i: (0, 0))
    return pl.pallas_call(
        body, name=name, out_shape=jax.ShapeDtypeStruct((S, D), BF16), grid=(S // ts,),
        in_specs=[pl.BlockSpec((ts, D), lambda i: (i, 0)), row, row, row],
        out_specs=pl.BlockSpec((ts, D), lambda i: (i, 0)),
        compiler_params=_params(),
    )(x, g, shift, scale)


def _norm_mod_bwd(x, dh, dres, g, scale, name):
    S, D = x.shape
    ts = _row_tile(S, 256)

    def body(x_ref, dh_ref, dr_ref, g_ref, sc_ref, dx_ref, dsh_ref, dsc_ref, dg_ref):
        i = pl.program_id(0)
        xv, dhv, gv = x_ref[...], dh_ref[...], g_ref[...]
        rstd = lax.rsqrt(jnp.mean(xv * xv, axis=-1, keepdims=True) + EPS)
        xhat = xv * rstd
        dn = dhv * (1.0 + sc_ref[...])
        dxhat = dn * gv
        proj = jnp.mean(dxhat * xhat, axis=-1, keepdims=True)
        dx_ref[...] = dr_ref[...] + rstd * (dxhat - xhat * proj)
        dsh = jnp.sum(dhv, axis=0, keepdims=True)
        dsc = jnp.sum(dhv * (xhat * gv), axis=0, keepdims=True)
        dg = jnp.sum(dn * xhat, axis=0, keepdims=True)

        @pl.when(i == 0)
        def _():
            dsh_ref[...] = dsh
            dsc_ref[...] = dsc
            dg_ref[...] = dg

        @pl.when(i > 0)
        def _():
            dsh_ref[...] += dsh
            dsc_ref[...] += dsc
            dg_ref[...] += dg

    tile = pl.BlockSpec((ts, D), lambda i: (i, 0))
    row = pl.BlockSpec((1, D), lambda i: (0, 0))
    vec = jax.ShapeDtypeStruct((1, D), F32)
    return pl.pallas_call(
        body, name=name, out_shape=(jax.ShapeDtypeStruct((S, D), F32), vec, vec, vec), grid=(S // ts,),
        in_specs=[tile, tile, tile, row, row], out_specs=(tile, row, row, row),
        compiler_params=_params(),
    )(x, dh, dres, g, scale)


def _gate_bwd(dx, f, gate, name):
    S, D = dx.shape
    ts = _row_tile(S, 256)

    def body(dx_ref, f_ref, g_ref, df_ref, dg_ref):
        i = pl.program_id(0)
        dxv = dx_ref[...]
        df_ref[...] = (dxv * g_ref[...]).astype(BF16)
        dg = jnp.sum(dxv * f_ref[...].astype(F32), axis=0, keepdims=True)

        @pl.when(i == 0)
        def _():
            dg_ref[...] = dg

        @pl.when(i > 0)
        def _():
            dg_ref[...] += dg

    tile = pl.BlockSpec((ts, D), lambda i: (i, 0))
    row = pl.BlockSpec((1, D), lambda i: (0, 0))
    return pl.pallas_call(
        body, name=name,
        out_shape=(jax.ShapeDtypeStruct((S, D), BF16), jax.ShapeDtypeStruct((1, D), F32)), grid=(S // ts,),
        in_specs=[tile, tile, row], out_specs=(tile, row),
        compiler_params=_params(),
    )(dx, f, gate)


def _ffn_in_fwd(h, w_t, name):
    S, D = h.shape
    F = w_t.shape[0] // 2
    tm, tn = _tile(S, TILES["ffn_fused"][0]), _tile(F, TILES["ffn_fused"][1])
    nj = F // tn

    def body(h_ref, wg_ref, wu_ref, o_ref):
        hv = h_ref[...]
        ug = lax.dot_general(hv, wg_ref[...], NT, preferred_element_type=F32)
        uu = lax.dot_general(hv, wu_ref[...], NT, preferred_element_type=F32)
        o_ref[...] = (ug * jax.nn.sigmoid(ug) * uu).astype(BF16)

    return pl.pallas_call(
        body, name=name, out_shape=jax.ShapeDtypeStruct((S, F), BF16), grid=(nj, S // tm),
        in_specs=[pl.BlockSpec((tm, D), lambda j, i: (i, 0)),
                  pl.BlockSpec((tn, D), lambda j, i: (j, 0)),
                  pl.BlockSpec((tn, D), lambda j, i: (j + nj, 0))],
        out_specs=pl.BlockSpec((tm, tn), lambda j, i: (i, j)),
        compiler_params=_params(),
    )(h, w_t, w_t)


def _ffn_mid_bwd(h, df, w_in_t, w_out, name):
    S, D = h.shape
    F = w_in_t.shape[0] // 2
    tm, tn = _tile(S, TILES["ffn_fused"][0]), _tile(F, TILES["ffn_fused"][1])
    nj = F // tn

    def body(h_ref, df_ref, wg_ref, wu_ref, wo_ref, dg_ref, du_ref):
        hv = h_ref[...]
        ug = lax.dot_general(hv, wg_ref[...], NT, preferred_element_type=F32)
        uu = lax.dot_general(hv, wu_ref[...], NT, preferred_element_type=F32)
        dact = lax.dot_general(df_ref[...], wo_ref[...], NT, preferred_element_type=F32)
        sig = jax.nn.sigmoid(ug)
        dg_ref[...] = (dact * uu * (sig * (1.0 + ug * (1.0 - sig)))).astype(BF16)
        du_ref[...] = (dact * (ug * sig)).astype(BF16)

    out = jax.ShapeDtypeStruct((S, F), BF16)
    return pl.pallas_call(
        body, name=name, out_shape=(out, out), grid=(nj, S // tm),
        in_specs=[pl.BlockSpec((tm, D), lambda j, i: (i, 0)),
                  pl.BlockSpec((tm, D), lambda j, i: (i, 0)),
                  pl.BlockSpec((tn, D), lambda j, i: (j, 0)),
                  pl.BlockSpec((tn, D), lambda j, i: (j + nj, 0)),
                  pl.BlockSpec((tn, D), lambda j, i: (j, 0))],
        out_specs=(pl.BlockSpec((tm, tn), lambda j, i: (i, j)), pl.BlockSpec((tm, tn), lambda j, i: (i, j))),
        compiler_params=_params(),
    )(h, df, w_in_t, w_in_t, w_out)


def _merge_fwd(o_a, o_b, o_c, gates, w_branch, w_out, resid, gate, name, *, tm=512):
    S, W = o_a.shape
    D = w_branch.shape[2]
    tm = _row_tile(S, tm)

    def body(oa_ref, ob_ref, oc_ref, g_ref, w_ref, wo_ref, r_ref, gm_ref, x_ref, m_ref, mix_ref):
        acc = None
        for k, o_ref in enumerate((oa_ref, ob_ref, oc_ref)):
            y = jnp.dot(o_ref[...], w_ref[k], preferred_element_type=F32)
            t = jax.nn.sigmoid(g_ref[:, k * D:(k + 1) * D]) * y
            acc = t if acc is None else acc + t
        merged = acc.astype(BF16)
        m_ref[...] = merged
        mix = jnp.dot(merged, wo_ref[...], preferred_element_type=F32)
        x_ref[...] = r_ref[...] + gm_ref[...] * mix
        mix_ref[...] = mix.astype(BF16)

    o_spec = pl.BlockSpec((tm, W), lambda i: (i, 0))
    tile = pl.BlockSpec((tm, D), lambda i: (i, 0))
    return pl.pallas_call(
        body, name=name,
        out_shape=(jax.ShapeDtypeStruct((S, D), F32), jax.ShapeDtypeStruct((S, D), BF16), jax.ShapeDtypeStruct((S, D), BF16)),
        grid=(S // tm,),
        in_specs=[o_spec, o_spec, o_spec, pl.BlockSpec((tm, 3 * D), lambda i: (i, 0)),
                  pl.BlockSpec((3, W, D), lambda i: (0, 0, 0)), pl.BlockSpec((D, D), lambda i: (0, 0)),
                  tile, pl.BlockSpec((1, D), lambda i: (0, 0))],
        out_specs=(tile, tile, tile),
        compiler_params=_params(),
    )(o_a, o_b, o_c, gates, w_branch, w_out, resid, gate)


def _merge_bwd(dmix, o_a, o_b, o_c, gates, w_branch, w_out, name, *, tm=512):
    S, W = o_a.shape
    D = w_branch.shape[2]
    tm = _row_tile(S, tm)
    n_heads = W // 64

    def body(dm_ref, oa_ref, ob_ref, oc_ref, g_ref, w_ref, wo_ref, dg_ref, dy_ref,
             doa_ref, dob_ref, doc_ref, dla_ref, dlb_ref, dlc_ref):
        dm = lax.dot_general(dm_ref[...], wo_ref[...], NT, preferred_element_type=F32)
        branches = ((oa_ref, doa_ref, dla_ref), (ob_ref, dob_ref, dlb_ref), (oc_ref, doc_ref, dlc_ref))
        for k, (o_ref, do_ref, dl_ref) in enumerate(branches):
            wk = w_ref[k]
            ov = o_ref[...]
            y = jnp.dot(ov, wk, preferred_element_type=F32)
            g = jax.nn.sigmoid(g_ref[:, k * D:(k + 1) * D])
            dy = (dm * g).astype(BF16)
            dy_ref[:, k * D:(k + 1) * D] = dy
            dg_ref[:, k * D:(k + 1) * D] = (dm * y * (g * (1.0 - g))).astype(BF16)
            do16 = lax.dot_general(dy, wk, NT, preferred_element_type=F32).astype(BF16)
            do_ref[...] = do16
            prod = do16.astype(F32) * ov.astype(F32)
            for h in range(n_heads):
                dl_ref[:, h:h + 1] = jnp.sum(prod[:, 64 * h:64 * (h + 1)], axis=1, keepdims=True)

    o_spec = pl.BlockSpec((tm, W), lambda i: (i, 0))
    wide = pl.BlockSpec((tm, 3 * D), lambda i: (i, 0))
    dl_spec = pl.BlockSpec((tm, n_heads), lambda i: (i, 0))
    o_out = jax.ShapeDtypeStruct((S, W), BF16)
    wide_out = jax.ShapeDtypeStruct((S, 3 * D), BF16)
    dl_out = jax.ShapeDtypeStruct((S, n_heads), F32)
    return pl.pallas_call(
        body, name=name, out_shape=(wide_out, wide_out, o_out, o_out, o_out, dl_out, dl_out, dl_out),
        grid=(S // tm,),
        in_specs=[pl.BlockSpec((tm, D), lambda i: (i, 0)), o_spec, o_spec, o_spec, wide,
                  pl.BlockSpec((3, W, D), lambda i: (0, 0, 0)), pl.BlockSpec((D, D), lambda i: (0, 0))],
        out_specs=(wide, wide, o_spec, o_spec, o_spec, dl_spec, dl_spec, dl_spec),
        compiler_params=_params(),
    )(dmix, o_a, o_b, o_c, gates, w_branch, w_out)


def _band_mask(variant, t_abs, s_abs):
    if variant == "b":
        return s_abs <= t_abs
    qc, kc = t_abs >> 6, s_abs >> 6
    return (kc <= qc) & (kc >= qc - (2 if variant == "a" else 8))


def _attn_fwd(variant, qkv, name, *, sinks=None, slopes=None, cq_col=None, ck_row=None, bias=None, after=None):
    S = qkv.shape[0]
    nb = S // BLK
    qb, kb, vb = ATTN_COLS[variant]
    shared_kv = variant == "a"
    win = ATTN_WINDOW.get(variant)
    per_step = ATTN_BLOCKS_PER_STEP[variant]

    def body(*refs):
        if after is not None:
            refs = refs[:-3] + refs[-2:]
        if variant == "a":
            q_ref, k_ref, v_ref, sink_ref, slope_ref, o_ref, lse_ref = refs
        elif variant == "b":
            q_ref, k_ref, v_ref, cq_ref, ck_ref, o_ref, lse_ref = refs
        else:
            q_ref, k_ref, v_ref, bias_ref, o_ref, lse_ref = refs
        p = pl.program_id(0)
        lane = lax.broadcasted_iota(jnp.int32, (1, BLK), 1)

        def compute(i, rows, start, n_keys):
            n_rows = rows.stop - rows.start
            t_abs = i * BLK + lax.broadcasted_iota(jnp.int32, (n_rows, 1), 0)
            q2 = q_ref[rows, :].astype(F32) * 0.125
            k_w = k_ref[pl.ds(start, n_keys), :]
            v_w = v_ref[pl.ds(start, n_keys), :]
            s_abs = start + lax.broadcasted_iota(jnp.int32, (1, n_keys), 1)
            valid = _band_mask(variant, t_abs, s_abs)
            outs = []
            for half in (0, 1):
                hmask = (lane >= 64) if half else (lane < 64)
                qh = jnp.where(hmask, q2, 0.0)
                if shared_kv:
                    swap = (p // 2) != half
                    qh = jnp.where(swap, pltpu.roll(qh, 64, 1), qh)
                s = lax.dot_general(qh.astype(BF16), k_w, NT, preferred_element_type=F32)
                if variant == "a":
                    head = 2 * p + half
                    s = s + (-slope_ref[head]) * jnp.abs(t_abs - s_abs).astype(F32)
                elif variant == "b":
                    s = s + cq_ref[rows, half:half + 1] - ck_ref[half:half + 1, pl.ds(start, n_keys)]
                else:
                    j0 = start // BLK
                    s = s + jnp.concatenate(
                        [bias_ref[half, jnp.clip(i - j0 - b, 0, 4)] for b in range(win)], axis=1)
                s = jnp.where(valid, s, NEG_INF)
                m = jnp.max(s, axis=1, keepdims=True)
                if variant == "a":
                    m = jnp.maximum(m, sink_ref[head])
                pe = jnp.exp(s - m)
                l = jnp.sum(pe, axis=1, keepdims=True)
                if variant == "a":
                    l = l + jnp.exp(sink_ref[head] - m)
                out = jnp.dot(pe.astype(BF16), v_w, preferred_element_type=F32) / l
                if shared_kv:
                    out = jnp.where(swap, pltpu.roll(out, 64, 1), out)
                outs.append(out)
                lse_ref[rows, half:half + 1] = m + jnp.log(l)
            o_ref[rows, :] = jnp.where(lane < 64, outs[0], outs[1]).astype(BF16)

        step = pl.program_id(1)
        if variant == "b":
            for g in range(S // GROUP):
                pl.when(step == g)(functools.partial(compute, step * per_step, slice(0, GROUP), 0, (g + 1) * GROUP))
        else:
            for sub in range(per_step):
                i = step * per_step + sub
                start = jnp.clip(i - (win - 1), 0, nb - win) * BLK
                compute(i, slice(sub * BLK, (sub + 1) * BLK), pl.multiple_of(start, BLK), win * BLK)

    tq = per_step * BLK
    kv_col = (lambda p, i: (0, kb)) if shared_kv else (lambda p, i: (0, kb + p))
    vv_col = (lambda p, i: (0, vb)) if shared_kv else (lambda p, i: (0, vb + p))
    in_specs = [pl.BlockSpec((tq, BLK), lambda p, i: (i, qb + p)),
                pl.BlockSpec((S, BLK), kv_col), pl.BlockSpec((S, BLK), vv_col)]
    args = [qkv, qkv, qkv]
    if variant == "a":
        in_specs += [pl.BlockSpec(memory_space=pltpu.SMEM), pl.BlockSpec(memory_space=pltpu.SMEM)]
        args += [sinks, slopes]
    elif variant == "b":
        in_specs += [pl.BlockSpec((None, tq, 2), lambda p, i: (p, i, 0)),
                     pl.BlockSpec((None, 2, S), lambda p, i: (p, 0, 0))]
        args += [cq_col, ck_row]
    else:
        in_specs += [pl.BlockSpec((2, 5, BLK, BLK), lambda p, i: (p, 0, 0, 0))]
        args += [bias]
    if after is not None:
        in_specs.append(ANY)
        args.append(after)
    return pl.pallas_call(
        body, name=name,
        out_shape=(jax.ShapeDtypeStruct((S, 512), BF16), jax.ShapeDtypeStruct((4, S, 2), F32)),
        grid=(4, nb // per_step), in_specs=in_specs,
        out_specs=(pl.BlockSpec((tq, BLK), lambda p, i: (i, p)),
                   pl.BlockSpec((None, tq, 2), lambda p, i: (p, i, 0))),
        compiler_params=_params(),
    )(*args)


def _attn_bwd(variant, qkv, do, lse_row, delta_row, name, *, sinks=None, slopes=None, cq_row=None,
              ck_col=None, bias_t=None):
    S = qkv.shape[0]
    nb = S // BLK
    qb, kb, vb = ATTN_COLS[variant]
    shared_kv = variant == "a"
    win = ATTN_WINDOW.get(variant)
    per_step = ATTN_BLOCKS_PER_STEP[variant]

    def body(*refs):
        if variant == "a":
            (q_ref, k_ref, v_ref, do_ref, lse_ref, dl_ref, sink_ref, slope_ref,
             dq_ref, dk_ref, dv_ref, ex_ref) = refs
        elif variant == "b":
            (q_ref, k_ref, v_ref, do_ref, lse_ref, dl_ref, cq_ref, ck_ref,
             dq_ref, dk_ref, dv_ref, ex_ref, dcq_ref) = refs
        else:
            (q_ref, k_ref, v_ref, do_ref, lse_ref, dl_ref, bias_ref,
             dq_ref, dk_ref, dv_ref, ex_ref) = refs
        p = pl.program_id(0)
        lane = lax.broadcasted_iota(jnp.int32, (1, BLK), 1)
        hmasks = [(lane < 64), (lane >= 64)]
        swaps = [(p // 2) != half for half in (0, 1)] if shared_kv else None

        @pl.when(pl.program_id(1) == 0)
        def _():
            dq_ref[...] = jnp.zeros_like(dq_ref)
            if variant == "b":
                dcq_ref[...] = jnp.zeros_like(dcq_ref)
            else:
                ex_ref[...] = jnp.zeros_like(ex_ref)

        def to_kv_lanes(x, h):
            x = jnp.where(hmasks[h], x, 0.0)
            if shared_kv:
                x = jnp.where(swaps[h], pltpu.roll(x, 64, 1), x)
            return x

        def compute(j, rows, start, n_q):
            n_rows = rows.stop - rows.start
            s_abs = j * BLK + lax.broadcasted_iota(jnp.int32, (n_rows, 1), 0)
            off_k = pl.multiple_of(j * BLK, BLK)
            k2 = k_ref[rows, :].astype(F32)
            v2 = v_ref[rows, :].astype(F32)
            if shared_kv:
                kv_lane = (lane >> 6) == (p // 2)
                k_src, v_src = jnp.where(kv_lane, k2, 0.0), jnp.where(kv_lane, v2, 0.0)
                k_al = [jnp.where(swaps[h], pltpu.roll(k_src, 64, 1), k_src) for h in (0, 1)]
                v_al = [jnp.where(swaps[h], pltpu.roll(v_src, 64, 1), v_src) for h in (0, 1)]
            else:
                k_al = [jnp.where(hmasks[h], k2, 0.0) for h in (0, 1)]
                v_al = [jnp.where(hmasks[h], v2, 0.0) for h in (0, 1)]
            k_al = [(t * 0.125).astype(BF16) for t in k_al]
            v_al = [t.astype(BF16) for t in v_al]
            q_w = q_ref[pl.ds(start, n_q), :]
            do_w = do_ref[pl.ds(start, n_q), :]
            t_abs = start + lax.broadcasted_iota(jnp.int32, (1, n_q), 1)
            valid = _band_mask(variant, t_abs, s_abs)
            dk_acc = dv_acc = None
            ds_both = []
            for half in (0, 1):
                s = lax.dot_general(k_al[half], q_w, NT, preferred_element_type=F32)
                if variant == "a":
                    s = s + (-slope_ref[2 * p + half]) * jnp.abs(t_abs - s_abs).astype(F32)
                elif variant == "b":
                    s = s + cq_ref[half:half + 1, pl.ds(start, n_q)] - ck_ref[rows, half:half + 1]
                else:
                    i0 = start // BLK
                    s = s + jnp.concatenate(
                        [bias_ref[half, jnp.clip(i0 + b - j, 0, 4)] for b in range(win)], axis=1)
                pr = jnp.where(valid, jnp.exp(s - lse_ref[half:half + 1, pl.ds(start, n_q)]), 0.0)
                dp = lax.dot_general(v_al[half], do_w, NT, preferred_element_type=F32)
                ds = pr * (dp - dl_ref[half:half + 1, pl.ds(start, n_q)])
                ds16 = ds.astype(BF16)
                dv_h = to_kv_lanes(jnp.dot(pr.astype(BF16), do_w, preferred_element_type=F32), half)
                dk_h = to_kv_lanes(jnp.dot(ds16, q_w, preferred_element_type=F32) * 0.125, half)
                dv_acc = dv_h if dv_acc is None else dv_acc + dv_h
                dk_acc = dk_h if dk_acc is None else dk_acc + dk_h
                ds_both.append(ds16)
                if variant == "b":
                    ex_ref[rows, half:half + 1] = -jnp.sum(ds, axis=1, keepdims=True)
                    dcq_ref[half:half + 1, pl.ds(start, n_q)] += jnp.sum(ds, axis=0, keepdims=True)
                elif variant == "c":
                    for b in range(win):
                        ex_ref[half, jnp.clip(i0 + b - j, 0, 4)] += ds[:, b * BLK:(b + 1) * BLK]
            dq_t = lax.dot_general(jnp.concatenate(k_al, axis=0), jnp.concatenate(ds_both, axis=0), TN,
                                   preferred_element_type=F32)
            dq_ref[:, pl.ds(start, n_q)] += dq_t
            if shared_kv:
                @pl.when(p == 0)
                def _():
                    dk_ref[pl.ds(off_k, n_rows), :] = dk_acc
                    dv_ref[pl.ds(off_k, n_rows), :] = dv_acc

                @pl.when(p > 0)
                def _():
                    dk_ref[pl.ds(off_k, n_rows), :] += dk_acc
                    dv_ref[pl.ds(off_k, n_rows), :] += dv_acc
            else:
                dk_ref[pl.ds(off_k, n_rows), :] = dk_acc
                dv_ref[pl.ds(off_k, n_rows), :] = dv_acc
            if variant == "a":
                for half in (0, 1):
                    p_sink = jnp.exp(sink_ref[2 * p + half] - lse_ref[half:half + 1, pl.ds(off_k, n_rows)])
                    term = p_sink * dl_ref[half:half + 1, pl.ds(off_k, n_rows)]
                    ex_ref[half:half + 1, :] += -jnp.sum(term, axis=1, keepdims=True)

        step = pl.program_id(1)
        if variant == "b":
            for g in range(S // GROUP):
                pl.when(step == g)(functools.partial(compute, step * per_step, slice(0, GROUP), g * GROUP, S - g * GROUP))
        else:
            for sub in range(per_step):
                j = step * per_step + sub
                start = jnp.clip(j, 0, nb - win) * BLK
                compute(j, slice(sub * BLK, (sub + 1) * BLK), pl.multiple_of(start, BLK), win * BLK)

    tk = per_step * BLK
    col = lambda c0: (lambda p, j: (0, c0 + p))
    kv_blk = (lambda c0: (lambda p, j: (j, c0))) if shared_kv else (lambda c0: (lambda p, j: (j, c0 + p)))
    pair = lambda p, j: (0, p)
    row_stat = pl.BlockSpec((None, 2, S), lambda p, j: (p, 0, 0))
    in_specs = [pl.BlockSpec((S, BLK), col(qb)),
                pl.BlockSpec((tk, BLK), kv_blk(kb)), pl.BlockSpec((tk, BLK), kv_blk(vb)),
                pl.BlockSpec((S, BLK), pair), row_stat, row_stat]
    args = [qkv, qkv, qkv, do, lse_row, delta_row]
    kv_width = BLK if shared_kv else 512
    kv_out = pl.BlockSpec((S, BLK), (lambda p, j: (0, 0)) if shared_kv else pair)
    out_shape = [jax.ShapeDtypeStruct((512, S), F32), jax.ShapeDtypeStruct((S, kv_width), F32),
                 jax.ShapeDtypeStruct((S, kv_width), F32)]
    out_specs = [pl.BlockSpec((BLK, S), lambda p, j: (p, 0)), kv_out, kv_out]
    if variant == "a":
        in_specs += [pl.BlockSpec(memory_space=pltpu.SMEM), pl.BlockSpec(memory_space=pltpu.SMEM)]
        args += [sinks, slopes]
        out_shape.append(jax.ShapeDtypeStruct((4, 8, BLK), F32))
        out_specs.append(pl.BlockSpec((None, 8, BLK), lambda p, j: (p, 0, 0)))
    elif variant == "b":
        in_specs += [row_stat, pl.BlockSpec((None, tk, 2), lambda p, j: (p, j, 0))]
        args += [cq_row, ck_col]
        out_shape += [jax.ShapeDtypeStruct((4, S, 2), F32), jax.ShapeDtypeStruct((4, 2, S), F32)]
        out_specs += [pl.BlockSpec((None, tk, 2), lambda p, j: (p, j, 0)), row_stat]
    else:
        in_specs += [pl.BlockSpec((2, 5, BLK, BLK), lambda p, j: (p, 0, 0, 0))]
        args += [bias_t]
        out_shape.append(jax.ShapeDtypeStruct((8, 5, BLK, BLK), F32))
        out_specs.append(pl.BlockSpec((2, 5, BLK, BLK), lambda p, j: (p, 0, 0, 0)))
    return pl.pallas_call(
        body, name=name, out_shape=tuple(out_shape), grid=(4, nb // per_step),
        in_specs=in_specs, out_specs=tuple(out_specs),
        compiler_params=_params(),
    )(*args)


def _log_sigmoid(x):
    return jnp.minimum(x, 0.0) - jnp.log(1.0 + jnp.exp(-jnp.abs(x)))


def _forget_fwd(fb, b_forget, name):
    S = fb.shape[0]
    nb = S // BLK

    def body(fb_ref, b_ref, cum_ref, carry_ref):
        i = pl.program_id(0)
        logf = _log_sigmoid(fb_ref[...] + b_ref[...])
        r = lax.broadcasted_iota(jnp.int32, (BLK, BLK), 0)
        c = lax.broadcasted_iota(jnp.int32, (BLK, BLK), 1)
        tri = (c <= r).astype(F32)

        @pl.when(i == 0)
        def _():
            carry_ref[...] = jnp.zeros_like(carry_ref)

        cum = jnp.dot(tri, logf, preferred_element_type=F32, precision=HIGHEST) + carry_ref[0:1, :]
        cum_ref[...] = cum
        carry_ref[...] = jnp.broadcast_to(cum[BLK - 1:BLK, :], carry_ref.shape)

    return pl.pallas_call(
        body, name=name, out_shape=jax.ShapeDtypeStruct((S, BLK), F32), grid=(nb,),
        in_specs=[pl.BlockSpec((BLK, BLK), lambda i: (i, 0)), pl.BlockSpec((1, BLK), lambda i: (0, 0))],
        out_specs=pl.BlockSpec((BLK, BLK), lambda i: (i, 0)),
        scratch_shapes=[pltpu.VMEM((8, BLK), F32)],
        compiler_params=_params(),
    )(fb, b_forget)


def _forget_bwd(dcum_q, dcum_k, fb, b_forget, name):
    S = fb.shape[0]
    nb = S // BLK

    def body(dq_ref, dk_ref, fb_ref, b_ref, dfb_ref, db_ref, carry_ref):
        g = pl.program_id(0)
        r = lax.broadcasted_iota(jnp.int32, (BLK, BLK), 0)
        c = lax.broadcasted_iota(jnp.int32, (BLK, BLK), 1)
        tri = (c >= r).astype(F32)

        @pl.when(g == 0)
        def _():
            carry_ref[...] = jnp.zeros_like(carry_ref)

        dcum = dq_ref[...] + dk_ref[...]
        dlogf = jnp.dot(tri, dcum, preferred_element_type=F32, precision=HIGHEST) + carry_ref[0:1, :]
        carry_ref[...] = jnp.broadcast_to(dlogf[0:1, :], carry_ref.shape)
        x = fb_ref[...] + b_ref[...]
        dfb = jnp.where(c < N_FORGET, dlogf * jax.nn.sigmoid(-x), 0.0)
        dfb_ref[...] = dfb
        db = jnp.sum(dfb, axis=0, keepdims=True)

        @pl.when(g == 0)
        def _():
            db_ref[...] = db

        @pl.when(g > 0)
        def _():
            db_ref[...] += db

    rev = pl.BlockSpec((BLK, BLK), lambda g: (nb - 1 - g, 0))
    row = pl.BlockSpec((1, BLK), lambda g: (0, 0))
    return pl.pallas_call(
        body, name=name,
        out_shape=(jax.ShapeDtypeStruct((S, BLK), F32), jax.ShapeDtypeStruct((1, BLK), F32)), grid=(nb,),
        in_specs=[rev, rev, rev, row], out_specs=(rev, row),
        scratch_shapes=[pltpu.VMEM((8, BLK), F32)],
        compiler_params=_params(),
    )(dcum_q, dcum_k, fb, b_forget)


def _skew(x, sign):
    row = lax.broadcasted_iota(jnp.int32, x.shape, 0)
    for b in range(7):
        amount = (1 << b) if sign > 0 else 256 - (1 << b)
        x = jnp.where(((row >> b) & 1) == 1, pltpu.roll(x, amount, 1), x)
    return x


def _rel_bases(rel):
    far = rel[:, 256:257]
    far127 = jnp.broadcast_to(far, (rel.shape[0], 127))
    base0 = jnp.concatenate([rel[:, 128:0:-1], far, rel[:, 255:128:-1]], axis=1)
    base1 = jnp.concatenate([rel[:, 256:128:-1], far, far127], axis=1)
    base0_t = jnp.concatenate([rel[:, 128:256], far, rel[:, 1:128]], axis=1)
    base1_t = jnp.concatenate([jnp.broadcast_to(far, (rel.shape[0], 128)), far, rel[:, 129:256]], axis=1)
    return jnp.stack([base0, base1, base0_t, base1_t], axis=1)


def _rel_expand(bases, name):
    def body(b_ref, t_ref, tt_ref):
        far = jnp.broadcast_to(b_ref[1:2, 0:1], (BLK, BLK))
        for k, out_ref in ((0, t_ref), (2, tt_ref)):
            for d in (0, 1):
                x = jnp.broadcast_to(b_ref[k + d:k + d + 1, :], (BLK, 2 * BLK))
                out_ref[d] = _skew(x, 1)[:, :BLK]
            for d in (2, 3, 4):
                out_ref[d] = far

    out = jax.ShapeDtypeStruct((8, 5, BLK, BLK), F32)
    spec = pl.BlockSpec((None, 5, BLK, BLK), lambda h: (h, 0, 0, 0))
    return pl.pallas_call(
        body, name=name, out_shape=(out, out), grid=(8,),
        in_specs=[pl.BlockSpec((None, 4, 2 * BLK), lambda h: (h, 0, 0))], out_specs=(spec, spec),
        compiler_params=_params(),
    )(bases)


def _rel_reduce(dtiles_t, name):
    def body(dt_ref, o_ref):
        zeros = jnp.zeros((BLK, BLK), F32)
        sums = []
        for d in (0, 1):
            x = _skew(jnp.concatenate([dt_ref[d], zeros], axis=1), -1)
            sums.append(jnp.broadcast_to(jnp.sum(x, axis=0, keepdims=True), (8, 2 * BLK)))
        lane = lax.broadcasted_iota(jnp.int32, (8, 2 * BLK), 1)
        main = pltpu.roll(sums[0], BLK, 1) + jnp.where(lane > BLK, sums[1], 0.0)
        far = jnp.sum(jnp.where(lane < BLK, sums[1], 0.0)[0:1], axis=1, keepdims=True)
        far = far + jnp.sum(jnp.sum(dt_ref[2] + dt_ref[3] + dt_ref[4], axis=0, keepdims=True), axis=1, keepdims=True)
        o_ref[...] = jnp.concatenate([main[0:1], jnp.broadcast_to(far, (1, BLK))], axis=1)

    return pl.pallas_call(
        body, name=name, out_shape=jax.ShapeDtypeStruct((8, 1, 3 * BLK), F32), grid=(8,),
        in_specs=[pl.BlockSpec((None, 5, BLK, BLK), lambda h: (h, 0, 0, 0))],
        out_specs=pl.BlockSpec((None, 1, 3 * BLK), lambda h: (h, 0, 0)),
        compiler_params=_params(),
    )(dtiles_t)


def _final_loss(x, target, g, name):
    S, D = x.shape
    ts = _row_tile(S, 256)

    def body(x_ref, t_ref, g_ref, dx_ref, loss_ref, dg_ref):
        i = pl.program_id(0)
        xv, gv = x_ref[...], g_ref[...]
        rstd = lax.rsqrt(jnp.mean(xv * xv, axis=-1, keepdims=True) + EPS)
        xhat = xv * rstd
        err = xhat * gv - t_ref[...]
        part = 0.5 * jnp.sum(jnp.mean(err * err, axis=-1, keepdims=True), axis=0, keepdims=True)
        dy = err / D
        dg = jnp.sum(dy * xhat, axis=0, keepdims=True)
        dxhat = dy * gv
        proj = jnp.mean(dxhat * xhat, axis=-1, keepdims=True)
        dx_ref[...] = rstd * (dxhat - xhat * proj)

        @pl.when(i == 0)
        def _():
            loss_ref[...] = jnp.broadcast_to(part, loss_ref.shape)
            dg_ref[...] = dg

        @pl.when(i > 0)
        def _():
            loss_ref[...] += jnp.broadcast_to(part, loss_ref.shape)
            dg_ref[...] += dg

    tile = pl.BlockSpec((ts, D), lambda i: (i, 0))
    row = pl.BlockSpec((1, D), lambda i: (0, 0))
    return pl.pallas_call(
        body, name=name,
        out_shape=(jax.ShapeDtypeStruct((S, D), F32), jax.ShapeDtypeStruct((8, 128), F32),
                   jax.ShapeDtypeStruct((1, D), F32)),
        grid=(S // ts,), in_specs=[tile, tile, row],
        out_specs=(tile, pl.BlockSpec((8, 128), lambda i: (0, 0)), row),
        compiler_params=_params(),
    )(x, target, g)


def _ada_fwd(c_all, w_ada, name):
    L, D, E = w_ada.shape

    def body(c_ref, w_ref, o_ref):
        cv = c_ref[...]
        cond = cv * jax.nn.sigmoid(cv)
        o_ref[...] = jnp.dot(cond, w_ref[...], preferred_element_type=F32, precision=HIGHEST)

    return pl.pallas_call(
        body, name=name, out_shape=jax.ShapeDtypeStruct((L, N_DEV, E), F32), grid=(L,),
        in_specs=[pl.BlockSpec((N_DEV, D), lambda l: (0, 0)), pl.BlockSpec((None, D, E), lambda l: (l, 0, 0))],
        out_specs=pl.BlockSpec((None, N_DEV, E), lambda l: (l, 0, 0)),
        compiler_params=_params(),
    )(c_all, w_ada)


def _ada_bwd(c_all_t, dmod, name):
    D = c_all_t.shape[0]
    L, _, E = dmod.shape

    def body(c_ref, d_ref, o_ref):
        cv = c_ref[...]
        cond = cv * jax.nn.sigmoid(cv)
        acc = None
        for b in range(N_DEV):
            t = cond[:, b:b + 1] * d_ref[b:b + 1, :]
            acc = t if acc is None else acc + t
        o_ref[...] = acc

    return pl.pallas_call(
        body, name=name, out_shape=jax.ShapeDtypeStruct((L, D, E), F32), grid=(L,),
        in_specs=[pl.BlockSpec((D, N_DEV), lambda l: (0, 0)), pl.BlockSpec((None, N_DEV, E), lambda l: (l, 0, 0))],
        out_specs=pl.BlockSpec((None, D, E), lambda l: (l, 0, 0)),
        compiler_params=_params(),
    )(c_all_t, dmod)


def _adamw(w, m, v, g_parts, name, me, after=None):
    L, R, C = w.shape
    tr = _row_tile(R, max(8, (256 * 1024 // max(C, 128)) // 8 * 8))
    nr = R // tr
    c1 = 1.0 - ADAM_B1 ** ADAM_STEP
    c2 = 1.0 - ADAM_B2 ** ADAM_STEP
    direct = [isinstance(p, tuple) for p in g_parts]
    n_in = sum(2 if d else 1 for d in direct)

    def body(me_ref, w_ref, m_ref, v_ref, *rest):
        g_refs, (go_ref, d_ref, mo_ref, vo_ref) = list(rest[:n_in]), rest[-4:]
        layer = pl.program_id(0)
        g = None
        for l in range(L):
            land_ref = g_refs.pop(0)
            own = g_refs.pop(0)[...].astype(F32) if direct[l] else None
            gl = None
            for k in range(land_ref.shape[0]):
                part = land_ref[k].astype(F32)
                if direct[l]:
                    part = jnp.where(me_ref[l] == k, own, part)
                gl = part if gl is None else gl + part
            g = gl if g is None else jnp.where(layer == l, gl, g)
        mn = ADAM_B1 * m_ref[...] + (1.0 - ADAM_B1) * g
        vn = ADAM_B2 * v_ref[...] + (1.0 - ADAM_B2) * (g * g)
        m_hat = mn / c1
        v_hat = vn / c2
        go_ref[...] = g
        d_ref[...] = -ADAM_LR * (m_hat / (jnp.sqrt(v_hat) + ADAM_EPS) + ADAM_WD * w_ref[...])
        mo_ref[...] = mn
        vo_ref[...] = vn

    def rows(l, layer, i):
        return jnp.where(layer == l, i, 0 if l > 0 else nr - 1)

    in_specs, operands = [], []
    for l, p in enumerate(g_parts):
        land, sent = p if direct[l] else (p, None)
        in_specs.append(pl.BlockSpec((land.shape[0], tr, C), lambda layer, i, me_ref, l=l: (0, rows(l, layer, i), 0)))
        operands.append(land)
        if direct[l]:
            in_specs.append(pl.BlockSpec((None, tr, C), lambda layer, i, me_ref, l=l: (me_ref[l], rows(l, layer, i), 0)))
            operands.append(sent)
    if after is not None:
        in_specs.append(ANY)
        operands.append(after)
    tile = pl.BlockSpec((None, tr, C), lambda layer, i, me_ref: (layer, i, 0))
    out = jax.ShapeDtypeStruct((L, R, C), F32)
    return pl.pallas_call(
        body, name=name, out_shape=(out, out, out, out),
        grid_spec=pltpu.PrefetchScalarGridSpec(
            num_scalar_prefetch=1, grid=(L, nr), in_specs=[tile, tile, tile] + in_specs,
            out_specs=(tile, tile, tile, tile)),
        compiler_params=_params(),
    )(me, w, m, v, *operands)


def _pair_add(pieces, recv, core, name):
    _, _, R, C = pieces.shape
    tr = _row_tile(R, max(8, (512 * 1024 // max(C, 128)) // 8 * 8))

    def body(core_ref, a_ref, b_ref, o_ref):
        o_ref[...] = (a_ref[...].astype(F32) + b_ref[...].astype(F32)).astype(BF16)

    return pl.pallas_call(
        body, name=name, out_shape=jax.ShapeDtypeStruct((4, R, C), BF16),
        grid_spec=pltpu.PrefetchScalarGridSpec(
            num_scalar_prefetch=1, grid=(4, R // tr),
            in_specs=[pl.BlockSpec((None, None, tr, C), lambda k, i, core_ref: (core_ref[0], k, i, 0)),
                      pl.BlockSpec((None, tr, C), lambda k, i, core_ref: (k, i, 0))],
            out_specs=pl.BlockSpec((None, tr, C), lambda k, i, core_ref: (k, i, 0))),
        compiler_params=_params(),
    )(core, pieces, recv)


MESH = pl.DeviceIdType.MESH
ANY = pl.BlockSpec(memory_space=pl.ANY)


def _position():
    return lax.axis_index("x"), lax.axis_index("y"), lax.axis_index("c")


def _small_all_gather(v, name):
    m_per, n = v.shape

    def body(x_ref, out_ref, send_sems, recv_sems, local_sem):
        x, y, c = _position()
        me, sibling = (x, y, c), (x, y, 1 - c)
        chips = [(1 - x, y), (x, 1 - y), (1 - x, 1 - y)]

        def rows(px, py, pc):
            return out_ref.at[pl.ds((4 * px + 2 * py + pc) * m_per, m_per), :]

        def copy(k, block, to, src=None):
            return pltpu.make_async_remote_copy(
                src_ref=rows(*block) if src is None else src, dst_ref=rows(*block),
                send_sem=send_sems.at[k], recv_sem=recv_sems.at[k], device_id=to, device_id_type=MESH)

        mine = pltpu.make_async_copy(x_ref, rows(*me), local_sem)
        mine.start()
        first = [copy(0, me, sibling, src=x_ref)]
        first += [copy(1 + j, me, (*chip, c), src=x_ref) for j, chip in enumerate(chips)]
        for cp in first:
            cp.start()
        passed = [copy(4 + j, (*chip, c), sibling) for j, chip in enumerate(chips)]
        for j, chip in enumerate(chips):
            copy(1 + j, (*chip, c), me).wait_recv()
            passed[j].start()
        copy(0, sibling, me).wait_recv()
        for j, chip in enumerate(chips):
            copy(4 + j, (*chip, 1 - c), me).wait_recv()
        for cp in first + passed:
            cp.wait_send()
        mine.wait()

    return pl.pallas_call(
        body, name=name, out_shape=jax.ShapeDtypeStruct((N_DEV * m_per, n), v.dtype),
        in_specs=[pl.BlockSpec(memory_space=pltpu.VMEM)], out_specs=pl.BlockSpec(memory_space=pltpu.VMEM),
        scratch_shapes=[pltpu.SemaphoreType.DMA((7,)), pltpu.SemaphoreType.DMA((7,)), pltpu.SemaphoreType.DMA],
    )(v)


def _big_all_gather(shards, name):
    n_arr = len(shards)

    def body(*refs):
        x_refs, out_refs = refs[:n_arr], refs[n_arr:2 * n_arr]
        send_sems, recv_sems, local_sems = refs[2 * n_arr:]
        x, y, c = _position()
        me, sibling = (x, y, c), (x, y, 1 - c)
        chips = [(1 - x, y), (x, 1 - y), (1 - x, 1 - y)]

        def slot(a, px, py, pc):
            return out_refs[a].at[4 * px + 2 * py + pc]

        def copy(a, k, block, to, src=None):
            return pltpu.make_async_remote_copy(
                src_ref=slot(a, *block) if src is None else src, dst_ref=slot(a, *block),
                send_sem=send_sems.at[a, k], recv_sem=recv_sems.at[a, k], device_id=to, device_id_type=MESH)

        mine = [pltpu.make_async_copy(x_refs[a], slot(a, *me), local_sems.at[a]) for a in range(n_arr)]
        for cp in mine:
            cp.start()
        first = []
        for j, chip in enumerate(chips):
            first += [copy(a, 1 + j, me, (*chip, c), src=x_refs[a]) for a in range(n_arr)]
        first += [copy(a, 0, me, sibling, src=x_refs[a]) for a in range(n_arr)]
        for cp in first:
            cp.start()
        passed = []
        for j, chip in enumerate(chips):
            for a in range(n_arr):
                copy(a, 1 + j, (*chip, c), me).wait_recv()
                fwd = copy(a, 4 + j, (*chip, c), sibling)
                fwd.start()
                passed.append(fwd)
        for a in range(n_arr):
            copy(a, 0, sibling, me).wait_recv()
        for j, chip in enumerate(chips):
            for a in range(n_arr):
                copy(a, 4 + j, (*chip, 1 - c), me).wait_recv()
        for cp in first + passed:
            cp.wait_send()
        for cp in mine:
            cp.wait()

    return pl.pallas_call(
        body, name=name,
        out_shape=tuple(jax.ShapeDtypeStruct((N_DEV,) + s.shape, s.dtype) for s in shards),
        in_specs=[ANY] * n_arr, out_specs=tuple([ANY] * n_arr),
        scratch_shapes=[pltpu.SemaphoreType.DMA((n_arr, 7)), pltpu.SemaphoreType.DMA((n_arr, 7)),
                        pltpu.SemaphoreType.DMA((n_arr,))],
    )(*shards)


def _sibling_exchange(pieces, name):
    n_arr = len(pieces)

    def body(*refs):
        p_refs, out_refs = refs[:n_arr], refs[n_arr:2 * n_arr]
        send_sems, recv_sems = refs[2 * n_arr:]
        x, y, c = _position()
        copies = [pltpu.make_async_remote_copy(
            src_ref=p_refs[a].at[1 - c], dst_ref=out_refs[a], send_sem=send_sems.at[a], recv_sem=recv_sems.at[a],
            device_id=(x, y, 1 - c), device_id_type=MESH) for a in range(n_arr)]
        for cp in copies:
            cp.start()
        for cp in copies:
            cp.wait()

    return pl.pallas_call(
        body, name=name,
        out_shape=tuple(jax.ShapeDtypeStruct(p.shape[1:], p.dtype) for p in pieces),
        in_specs=[ANY] * n_arr, out_specs=tuple([ANY] * n_arr),
        scratch_shapes=[pltpu.SemaphoreType.DMA((n_arr,)), pltpu.SemaphoreType.DMA((n_arr,))],
    )(*pieces)


HBM = pl.BlockSpec(memory_space=pltpu.HBM)
SEM = pl.BlockSpec(memory_space=pltpu.SEMAPHORE)
EFFECT = pltpu.SideEffectType.DATAFLOW_SIDE_EFFECTING
RELATIONS = [(rx, ry, rc) for rx in (0, 1) for ry in (0, 1) for rc in (0, 1)][1:]


SAME_CORE = [r for r in RELATIONS if r == (0, 0, 1) or r[2] == 0]


CHIPS = [r for r in RELATIONS if r[2] == 0]


def _exchange_copies(src_refs, land_refs, send_sems, recv_sems, scatter, receive_side, relations):
    x, y, c = _position()
    index = (lambda px, py, pc: 2 * px + py) if relations == CHIPS else (lambda px, py, pc: 4 * px + 2 * py + pc)
    me = index(x, y, c)
    copies = []
    for k, (rx, ry, rc) in enumerate(relations):
        peer = ((1 - x) if rx else x, (1 - y) if ry else y, (1 - c) if rc else c)
        peer_index = index(*peer)
        for a, (src, land) in enumerate(zip(src_refs, land_refs)):
            copies.append(pltpu.make_async_remote_copy(
                src_ref=src.at[peer_index] if scatter else src,
                dst_ref=land.at[peer_index if receive_side else me],
                send_sem=send_sems.at[a * len(relations) + k], recv_sem=recv_sems.at[a * len(relations) + k],
                device_id=peer, device_id_type=MESH))
    return copies


def _exchange_start(srcs, scatter, after, name, relations=RELATIONS):
    n = len(srcs)
    land_shapes = [(s.shape if scatter else (N_DEV,) + s.shape) for s in srcs]

    def body(*refs):
        src_refs, land_refs = refs[:n], refs[n:2 * n]
        send_sems, recv_sems = refs[2 * n + 1], refs[2 * n + 2]
        token = refs[-1]
        for cp in _exchange_copies(src_refs, land_refs, send_sems, recv_sems, scatter, False, relations):
            cp.start()
        token[...] = jnp.zeros_like(token)

    sems = pltpu.SemaphoreType.DMA((n * len(relations),))
    outs = pl.pallas_call(
        body, name=name,
        out_shape=(sems, sems, *[pltpu.HBM(s.shape, s.dtype) for s in srcs],
                   *[pltpu.HBM(shape, s.dtype) for shape, s in zip(land_shapes, srcs)],
                   jax.ShapeDtypeStruct((8, 128), F32)),
        in_specs=[HBM] * (2 * n) + [ANY],
        out_specs=(SEM, SEM, *[HBM] * (2 * n), pl.BlockSpec(memory_space=pltpu.VMEM)),
        input_output_aliases={a: 2 + a for a in range(2 * n)},
        compiler_params=pltpu.CompilerParams(has_side_effects=EFFECT),
    )(*[pltpu.with_memory_space_constraint(s, pltpu.HBM) for s in srcs],
      *[pltpu.with_memory_space_constraint(lax.empty(shape, s.dtype), pltpu.HBM)
        for shape, s in zip(land_shapes, srcs)], after)
    return outs[0], outs[1], outs[2:2 + n], outs[2 + n:2 + 2 * n], outs[-1]


def _exchange_wait(started, scatter, after, name, relations=RELATIONS):
    send_sems, recv_sems, srcs, lands, _ = started
    n = len(srcs)

    def body(*refs):
        src_refs, land_refs = refs[:n], refs[n:2 * n]
        send_sems, recv_sems = refs[2 * n], refs[2 * n + 1]
        copies = _exchange_copies(src_refs, land_refs, send_sems, recv_sems, scatter, True, relations)
        for cp in copies:
            cp.wait_send()
        for cp in copies:
            cp.wait_recv()

    outs = pl.pallas_call(
        body, name=name,
        out_shape=(*[pltpu.HBM(s.shape, s.dtype) for s in srcs], *[pltpu.HBM(t.shape, t.dtype) for t in lands]),
        in_specs=[HBM] * (2 * n) + [SEM, SEM, ANY], out_specs=tuple([HBM] * (2 * n)),
        input_output_aliases={a: a for a in range(2 * n)},
        compiler_params=pltpu.CompilerParams(has_side_effects=EFFECT),
    )(*srcs, *lands, send_sems, recv_sems, after)
    return outs[:n], outs[n:]


def _forward_copies(land_refs, send_sems, recv_sems, receive_side):
    x, y, c = _position()
    copies = []
    for j, (px, py) in enumerate([(1 - x, y), (x, 1 - y), (1 - x, 1 - y)]):
        held, coming = 4 * px + 2 * py + c, 4 * px + 2 * py + (1 - c)
        for a, land in enumerate(land_refs):
            copies.append(pltpu.make_async_remote_copy(
                src_ref=land.at[held], dst_ref=land.at[coming if receive_side else held],
                send_sem=send_sems.at[3 * a + j], recv_sem=recv_sems.at[3 * a + j],
                device_id=(x, y, 1 - c), device_id_type=MESH))
    return copies


def _forward_start(lands, after, name):
    n = len(lands)

    def body(*refs):
        send_sems, recv_sems, token = refs[n + 1], refs[n + 2], refs[-1]
        for cp in _forward_copies(refs[:n], send_sems, recv_sems, False):
            cp.start()
        token[...] = jnp.zeros_like(token)

    sems = pltpu.SemaphoreType.DMA((3 * n,))
    outs = pl.pallas_call(
        body, name=name,
        out_shape=(sems, sems, *[pltpu.HBM(t.shape, t.dtype) for t in lands], jax.ShapeDtypeStruct((8, 128), F32)),
        in_specs=[HBM] * n + [ANY], out_specs=(SEM, SEM, *[HBM] * n, pl.BlockSpec(memory_space=pltpu.VMEM)),
        input_output_aliases={a: 2 + a for a in range(n)},
        compiler_params=pltpu.CompilerParams(has_side_effects=EFFECT),
    )(*lands, after)
    return outs[0], outs[1], outs[2:2 + n], outs[-1]


def _forward_wait(started, after, name):
    send_sems, recv_sems, lands, _ = started
    n = len(lands)

    def body(*refs):
        copies = _forward_copies(refs[:n], refs[n], refs[n + 1], True)
        for cp in copies:
            cp.wait_send()
        for cp in copies:
            cp.wait_recv()

    return pl.pallas_call(
        body, name=name, out_shape=tuple(pltpu.HBM(t.shape, t.dtype) for t in lands),
        in_specs=[HBM] * n + [SEM, SEM, ANY], out_specs=tuple([HBM] * n),
        input_output_aliases={a: a for a in range(n)},
        compiler_params=pltpu.CompilerParams(has_side_effects=EFFECT),
    )(*lands, send_sems, recv_sems, after)


W_IN_SHARD = N_IN // N_DEV
F_SHARD = F_COL // W_IN_SHARD
F_LO = F_COL - F_SHARD * W_IN_SHARD


def _w_ffn_in_view(w):
    return jnp.transpose(w, (0, 2, 1))


def _w_in_rearranged(g):
    parts = [g[d] for d in range(N_DEV)]
    with_f = parts[F_SHARD]
    parts[F_SHARD:F_SHARD + 1] = [with_f[:, :F_LO], with_f[:, F_LO + N_FORGET:]]
    parts += [with_f[:, F_LO:F_LO + N_FORGET], jnp.zeros((with_f.shape[0], BLK - N_FORGET), with_f.dtype)]
    return jnp.concatenate(parts, axis=1)


def _w_in_pieces(dw_r):
    def original(lo, hi):
        shift = 0 if hi <= F_COL else N_FORGET
        return dw_r[:, lo - shift:hi - shift]

    pieces = []
    for d in range(N_DEV):
        lo, hi = d * W_IN_SHARD, (d + 1) * W_IN_SHARD
        if d == F_SHARD:
            pieces.append(jnp.concatenate([original(lo, F_COL), dw_r[:, N_MAIN:N_MAIN + N_FORGET],
                                           original(F_COL + N_FORGET, hi)], axis=1))
        else:
            pieces.append(original(lo, hi))
    return jnp.stack(pieces)


def _row_pieces(dw):
    return dw.reshape(N_DEV, dw.shape[0] // N_DEV, dw.shape[1])


def _branch_pieces(dw):
    k, w, d = dw.shape
    return jnp.transpose(dw.reshape(k, w, N_DEV, d // N_DEV), (2, 0, 1, 3)).reshape(N_DEV, k * w, d // N_DEV)


def _pair_major(p8):
    return jnp.stack([p8[0::2], p8[1::2]])


def _pairs_col(a):
    return jnp.transpose(a.reshape(a.shape[0], 4, 2), (1, 0, 2))


def _pairs_row(a):
    return jnp.transpose(a.reshape(a.shape[0], 4, 2), (1, 2, 0))


def _heads_from_col(a):
    return jnp.transpose(a, (1, 0, 2)).reshape(a.shape[1], 8)


def _heads_from_row(a):
    return jnp.transpose(a, (2, 0, 1)).reshape(a.shape[2], 8)


def _pad_lanes(a, n):
    return jnp.pad(a, [(0, 0)] * (a.ndim - 1) + [(0, n - a.shape[-1])])


SMALL_SEGMENTS = (("dmod", 2 * 6 * D_MODEL), ("norm_mix_g", 2 * D_MODEL), ("norm_ffn_g", 2 * D_MODEL),
                  ("final_norm_g", D_MODEL), ("b_forget", 128), ("sinks", 128), ("rel_bias", 4224))
SMALL_ROWS = 176


def _pack_small(parts):
    flat = [_pad_lanes(parts[name].reshape(1, -1), size) for name, size in SMALL_SEGMENTS]
    total = sum(size for _, size in SMALL_SEGMENTS)
    flat.append(jnp.zeros((1, SMALL_ROWS * 128 - total), F32))
    return jnp.concatenate(flat, axis=1).reshape(SMALL_ROWS, 128)


def _unpack_small(packed, shapes):
    flat = packed.reshape(-1)
    out, pos = {}, 0
    for name, size in SMALL_SEGMENTS:
        shape = shapes[name]
        count = 1
        for d in shape:
            count *= d
        out[name] = flat[pos:pos + count].reshape(shape)
        pos += size
    return out


def kernel(x, c, norm_mix_g, norm_ffn_g, w_ada, b_ada, w_in, b_forget, sinks, rel_bias, w_branch, w_out, w_ffn_in, w_ffn_out, final_norm_g, loss_target, m_norm_mix_g, m_norm_ffn_g, m_w_ada, m_b_ada, m_w_in, m_b_forget, m_sinks, m_rel_bias, m_w_branch, m_w_out, m_w_ffn_in, m_w_ffn_out, m_final_norm_g, v_norm_mix_g, v_norm_ffn_g, v_w_ada, v_b_ada, v_w_in, v_b_forget, v_sinks, v_rel_bias, v_w_branch, v_w_out, v_w_ffn_in, v_w_ffn_out, v_final_norm_g):
    depth = w_in.shape[0]
    S, D = x.shape[1], x.shape[2]
    assert S % GROUP == 0 and S >= ATTN_WINDOW["c"] * BLK
    px, py, pc = _position()
    me = 4 * px + 2 * py + pc
    x0 = x[0]

    assert depth == 2
    big_weights = (w_in, w_branch, w_out, w_ffn_in, w_ffn_out)
    me_arr = jnp.stack([me, me]).astype(jnp.int32)
    me_in_arr = jnp.stack([2 * px + py, me]).astype(jnp.int32)

    def slabs(landed, mine):
        return [jnp.where(me == d, mine, landed[d]) for d in range(N_DEV)]

    def rest_matrices(g_branch, g_out, g_fin, g_fout):
        return (jnp.transpose(jnp.stack(g_branch), (1, 2, 0, 3)).reshape(3, 512, D),
                jnp.concatenate(g_out, axis=0), jnp.concatenate(g_fin, axis=0), jnp.concatenate(g_fout, axis=0))

    def arrive(started, after, name):
        mine, landed = _exchange_wait(started, False, after, f"{name}_wait", SAME_CORE)
        return mine, _forward_start(landed, mine[0], f"{name}_forward_start")

    def finish_gather(arrived, after, name):
        mine, forward = arrived
        landed = _forward_wait(forward, after, f"{name}_forward_wait")
        return [slabs(t, s) for t, s in zip(landed, mine)]

    w_fin_t = _w_ffn_in_view(w_ffn_in)
    shards = [[t.astype(BF16) for t in (w_in[l], w_branch[l], w_out[l], w_fin_t[l], w_ffn_out[l])]
              for l in range(depth)]
    gathered_in0 = _big_all_gather(shards[0][:1], "comm_gather_w_in0")[0]
    gather_rest0 = _exchange_start(shards[0][1:], False, gathered_in0, "comm_gather_rest0_start", SAME_CORE)
    gather1 = _exchange_start(shards[1], False, gather_rest0[4], "comm_gather_weights1_start", SAME_CORE)
    W_in, W_branch, W_out, W_fin, W_fout = ([None, None] for _ in range(5))
    W_in[0] = _w_in_rearranged(gathered_in0)

    c_all = _small_all_gather(c.reshape(8, 128), "comm_gather_c").reshape(N_DEV, D)
    mod_cols = _ada_fwd(c_all, w_ada, "ada_fwd")
    mod_all = _small_all_gather(mod_cols.reshape(-1, 128), "comm_gather_mod")
    mod_all = mod_all.reshape(N_DEV, depth, N_DEV, w_ada.shape[2])
    mod_mine = lax.dynamic_index_in_dim(mod_all, me, axis=2, keepdims=False)
    mod = jnp.transpose(mod_mine, (1, 0, 2)).reshape(depth, 6 * D) + b_ada + gather1[4][0:1, 0:1]
    mods = [[mod[l:l + 1, k * D:(k + 1) * D] for k in range(6)] for l in range(depth)]

    slopes = jnp.exp2(-jnp.arange(1, 9, dtype=F32))
    saved = []
    xs = x0
    for l in range(depth):
        if l == 1:
            g_in1, *g_rest1 = finish_gather(arrived1, xs, "comm_gather_weights1")
            W_in[1] = _w_in_rearranged(g_in1)
            W_branch[1], W_out[1], W_fin[1], W_fout[1] = rest_matrices(*g_rest1)
        sh_m, sc_m, g_m, sh_f, sc_f, g_f = mods[l]
        gm, gf = norm_mix_g[l:l + 1], norm_ffn_g[l:l + 1]
        bfor = _pad_lanes(b_forget[l:l + 1], BLK)
        h = _norm_mod_fwd(xs, gm, sh_m, sc_m, f"norm_mix_fwd{l}")
        qkv = _matmul(h, W_in[l], "nn", BF16, f"proj_qkv{l}", TILES["proj_qkv"], n=N_QKV)
        gates = _matmul(h, W_in[l], "nn", F32, f"proj_gates{l}", TILES["proj_gates"], n=N_GATES,
                        b_off=N_QKV // TILES["proj_gates"][1])
        fb = _matmul(h, W_in[l], "nn", F32, f"proj_forget{l}", TILES["proj_forget"], n=BLK, b_off=N_MAIN // BLK)
        cum = _forget_fwd(fb, bfor, f"forget_fwd{l}")[:, :N_FORGET]
        cum_col, cum_row = _pairs_col(cum), _pairs_row(cum)
        tiles, tiles_t = _rel_expand(_rel_bases(rel_bias[l]), f"rel_expand{l}")
        o_a, lse_a = _attn_fwd("a", qkv, f"attn_a_fwd{l}", sinks=sinks[l], slopes=slopes)
        o_b, lse_b = _attn_fwd("b", qkv, f"attn_b_fwd{l}", cq_col=cum_col, ck_row=cum_row)
        arrived_rest0 = arrive(gather_rest0, o_b, "comm_gather_rest0") if l == 0 else None
        o_c, lse_c = _attn_fwd("c", qkv, f"attn_c_fwd{l}", bias=tiles, after=arrived_rest0[1][3] if l == 0 else None)
        if l == 0:
            W_branch[0], W_out[0], W_fin[0], W_fout[0] = rest_matrices(
                *finish_gather(arrived_rest0, o_c, "comm_gather_rest0"))
        x1, merged, mix = _merge_fwd(o_a, o_b, o_c, gates, W_branch[l], W_out[l], xs, g_m, f"merge_fwd{l}")
        h2 = _norm_mod_fwd(x1, gf, sh_f, sc_f, f"norm_ffn_fwd{l}")
        act = _ffn_in_fwd(h2, W_fin[l], f"ffn_in_fwd{l}")
        if l == 0:
            arrived1 = arrive(gather1, act, "comm_gather_weights1")
        x2, ffn = _matmul_resid(act, W_fout[l], x1, g_f, f"ffn_out{l}", TILES["ffn_out"],
                                after=arrived1[1][3] if l == 0 else None)
        saved.append(dict(x=xs, h=h, qkv=qkv, gates=gates, fb=fb, bfor=bfor, cum_col=cum_col, cum_row=cum_row,
                          tiles_t=tiles_t, o=(o_a, o_b, o_c), lse=(lse_a, lse_b, lse_c), merged=merged, mix=mix,
                          x1=x1, h2=h2, act=act, ffn=ffn))
        xs = x2

    dx, loss_tile, d_final_g = _final_loss(xs, loss_target[0], final_norm_g.reshape(1, D), "final_loss")
    loss = lax.psum(loss_tile[0, 0], ("x", "y", "c"))

    grads = {k: [None] * depth for k in ("w_in", "w_branch", "w_out", "w_ffn_in", "w_ffn_out", "norm_mix_g",
                                          "norm_ffn_g", "b_forget", "sinks", "rel_bias", "dmod")}
    def rest_pieces(l):
        return [_branch_pieces(grads["w_branch"][l]), _row_pieces(grads["w_out"][l]),
                _row_pieces(grads["w_ffn_in"][l]), _row_pieces(grads["w_ffn_out"][l])]

    reduce1 = reduce_rest0 = reduce_in0 = None
    for l in reversed(range(depth)):
        sv = saved[l]
        sh_m, sc_m, g_m, sh_f, sc_f, g_f = mods[l]
        if l == 0:
            g_f = g_f + reduce1[4][0:1, 0:1]
        gm, gf = norm_mix_g[l:l + 1], norm_ffn_g[l:l + 1]
        df, d_g_f = _gate_bwd(dx, sv["ffn"], g_f, f"ffn_gate_bwd{l}")
        du_g, du_u = _ffn_mid_bwd(sv["h2"], df, W_fin[l], W_fout[l], f"ffn_mid_bwd{l}")
        du = jnp.concatenate([du_g, du_u], axis=1)
        grads["w_ffn_out"][l] = _matmul(sv["act"], df, "tn", BF16, f"wgrad_ffn_out{l}", TILES["wgrad_ffn_out"])
        grads["w_ffn_in"][l] = _matmul(du, sv["h2"], "tn", BF16, f"wgrad_ffn_in{l}", TILES["wgrad_ffn_in"])
        dh2 = _matmul(du, W_fin[l], "nn", F32, f"dgrad_ffn_in{l}", TILES["dgrad_ffn_in"])
        dx1, d_sh_f, d_sc_f, d_gf = _norm_mod_bwd(sv["x1"], dh2, dx, gf, sc_f, f"norm_ffn_bwd{l}")
        dmix, d_g_m = _gate_bwd(dx1, sv["mix"], g_m, f"mix_gate_bwd{l}")
        grads["w_out"][l] = _matmul(sv["merged"], dmix, "tn", BF16, f"wgrad_out{l}", TILES["wgrad_out"])
        o_a, o_b, o_c = sv["o"]
        dgates, dy, do_a, do_b, do_c, dl_a, dl_b, dl_c = _merge_bwd(
            dmix, o_a, o_b, o_c, sv["gates"], W_branch[l], W_out[l], f"merge_bwd{l}")
        dwb = [_matmul(o_k, dy, "tn", BF16, f"wgrad_branch{l}_{k}", TILES["wgrad_branch"], n=D,
                       b_off=k * (D // TILES["wgrad_branch"][1])) for k, o_k in enumerate((o_a, o_b, o_c))]
        grads["w_branch"][l] = jnp.stack(dwb)
        lse_rows = [_pairs_row(_heads_from_col(t)) for t in sv["lse"]]
        if l == 0:
            reduce_rest0 = _exchange_start(rest_pieces(0), True, dy, "comm_reduce_rest0_start")
            lse_rows = [t + reduce_rest0[4][0:1, 0:1] for t in lse_rows]
        dqt_a, dk_a, dv_a, dsink = _attn_bwd("a", sv["qkv"], do_a, lse_rows[0], _pairs_row(dl_a), f"attn_a_bwd{l}",
                                             sinks=sinks[l], slopes=slopes)
        dqt_b, dk_b, dv_b, dck, dcq = _attn_bwd("b", sv["qkv"], do_b, lse_rows[1], _pairs_row(dl_b),
                                                f"attn_b_bwd{l}", cq_row=sv["cum_row"], ck_col=sv["cum_col"])
        dqt_c, dk_c, dv_c, dtiles_t = _attn_bwd("c", sv["qkv"], do_c, lse_rows[2], _pairs_row(dl_c),
                                                f"attn_c_bwd{l}", bias_t=sv["tiles_t"])
        grads["sinks"][l] = dsink[:, :2, 0].reshape(8)
        grads["rel_bias"][l] = _rel_reduce(dtiles_t, f"rel_reduce{l}")[:, 0, :N_REL]
        dcum_k = _pad_lanes(_heads_from_col(dck), BLK)
        dcum_q = _pad_lanes(_heads_from_row(dcq), BLK)
        dfb, d_bfor = _forget_bwd(dcum_q, dcum_k, sv["fb"], sv["bfor"], f"forget_bwd{l}")
        grads["b_forget"][l] = d_bfor[0, :N_FORGET]
        dproj = jnp.concatenate(
            [t.astype(BF16) for t in (dqt_a.T, dk_a, dv_a, dqt_b.T, dk_b, dv_b, dqt_c.T, dk_c, dv_c)]
            + [dgates, dfb.astype(BF16)], axis=1)
        grads["w_in"][l] = _matmul(sv["h"], dproj, "tn", BF16, f"wgrad_in{l}", TILES["wgrad_in"])
        dh = _matmul(dproj, W_in[l], "nt", F32, f"dgrad_in{l}", TILES["dgrad_in"])
        dx, d_sh_m, d_sc_m, d_gm = _norm_mod_bwd(sv["x"], dh, dx1, gm, sc_m, f"norm_mix_bwd{l}")
        grads["norm_mix_g"][l] = d_gm[0]
        grads["norm_ffn_g"][l] = d_gf[0]
        grads["dmod"][l] = jnp.concatenate([d_sh_m, d_sc_m, d_g_m, d_sh_f, d_sc_f, d_g_f], axis=1)[0]
        if l == 1:
            reduce1 = _exchange_start([_w_in_pieces(grads["w_in"][1])] + rest_pieces(1), True, dx, "comm_reduce1_start")

    grad_x = dx.reshape(x.shape)

    small_shapes = dict(dmod=b_ada.shape, norm_mix_g=norm_mix_g.shape, norm_ffn_g=norm_ffn_g.shape,
                        final_norm_g=final_norm_g.shape, b_forget=b_forget.shape, sinks=sinks.shape,
                        rel_bias=rel_bias.shape)
    mine_small = _pack_small(dict(
        dmod=jnp.stack(grads["dmod"]), norm_mix_g=jnp.stack(grads["norm_mix_g"]),
        norm_ffn_g=jnp.stack(grads["norm_ffn_g"]), final_norm_g=d_final_g[0],
        b_forget=_pad_lanes(jnp.stack(grads["b_forget"]).reshape(1, -1), 128),
        sinks=_pad_lanes(jnp.stack(grads["sinks"]).reshape(1, -1), 128),
        rel_bias=_pad_lanes(jnp.stack(grads["rel_bias"]).reshape(1, -1), 4224)))
    all_small = _small_all_gather(mine_small, "comm_gather_small").reshape(N_DEV, SMALL_ROWS, 128)
    pieces_in0 = _pair_major(_w_in_pieces(grads["w_in"][0]))
    from_sibling = _sibling_exchange([pieces_in0], "comm_reduce_in0_sibling")[0]
    pair_sum_in0 = _pair_add(pieces_in0, from_sibling, pc.astype(jnp.int32).reshape(1), "pair_add_in0")
    reduce_in0 = _exchange_start([pair_sum_in0], True, all_small, "comm_reduce_in0_start", CHIPS)
    in0_started = reduce_in0[4]

    def pack_params(b_ada_, nm, nf, fn, bf, sk, rb):
        return _pack_small(dict(dmod=b_ada_, norm_mix_g=nm, norm_ffn_g=nf, final_norm_g=fn,
                                b_forget=_pad_lanes(bf.reshape(1, -1), 128), sinks=_pad_lanes(sk.reshape(1, -1), 128),
                                rel_bias=_pad_lanes(rb.reshape(1, -1), 4224)))

    small_out = _adamw(
        pack_params(b_ada, norm_mix_g, norm_ffn_g, final_norm_g, b_forget, sinks, rel_bias)[None],
        pack_params(m_b_ada, m_norm_mix_g, m_norm_ffn_g, m_final_norm_g, m_b_forget, m_sinks, m_rel_bias)[None],
        pack_params(v_b_ada, v_norm_mix_g, v_norm_ffn_g, v_final_norm_g, v_b_forget, v_sinks, v_rel_bias)[None],
        [all_small], "adamw_small", me_arr, after=in0_started)
    small_out = [_unpack_small(t[0], small_shapes) for t in small_out]

    dmod_all = all_small[:, :96].reshape(N_DEV, depth, 6 * D)
    dmod_cols = lax.dynamic_slice_in_dim(dmod_all, me * w_ada.shape[2], w_ada.shape[2], axis=2)
    d_w_ada = _ada_bwd(jnp.transpose(c_all), jnp.transpose(dmod_cols, (1, 0, 2)), "ada_bwd")

    big = {"w_ada": _adamw(w_ada, m_w_ada, v_w_ada, [d_w_ada[l:l + 1] for l in range(depth)], "adamw_w_ada", me_arr,
                           after=in0_started)}
    sent1, landed1 = _exchange_wait(reduce1, True, big["w_ada"][0], "comm_reduce1_wait")
    sent_rest0, landed_rest0 = _exchange_wait(reduce_rest0, True, landed1[0], "comm_reduce_rest0_wait")
    parts = {"w_in": [None, (landed1[0], sent1[0])]}
    for a, name in enumerate(("w_branch", "w_out", "w_ffn_in", "w_ffn_out")):
        parts[name] = [(landed_rest0[a], sent_rest0[a]), (landed1[1 + a], sent1[1 + a])]

    def update(name, w, m, v, view=lambda t: t):
        per_layer = lambda t: t.reshape(depth, -1, t.shape[-1])
        outs = _adamw(*[per_layer(view(t)) for t in (w, m, v)], parts[name], f"adamw_{name}",
                      me_in_arr if name == "w_in" else me_arr)
        big[name] = [view(t).reshape(w.shape) for t in outs]

    update("w_ffn_in", w_ffn_in, m_w_ffn_in, v_w_ffn_in, _w_ffn_in_view)
    update("w_ffn_out", w_ffn_out, m_w_ffn_out, v_w_ffn_out)
    update("w_branch", w_branch, m_w_branch, v_w_branch)
    update("w_out", w_out, m_w_out, v_w_out)
    sent_in0, landed_in0 = _exchange_wait(reduce_in0, True, big["w_out"][0], "comm_reduce_in0_wait", CHIPS)
    parts["w_in"][0] = (landed_in0[0], sent_in0[0])
    update("w_in", w_in, m_w_in, v_w_in)

    def leaf(kind, name):
        if name in big:
            return big[name][kind]
        return small_out[kind]["dmod" if name == "b_ada" else name]

    order = ["norm_mix_g", "norm_ffn_g", "w_ada", "b_ada", "w_in", "b_forget", "sinks", "rel_bias", "w_branch",
             "w_out", "w_ffn_in", "w_ffn_out", "final_norm_g"]
    return (loss, grad_x, *[leaf(0, n) for n in order], *[leaf(1, n) for n in order],
            *[leaf(2, n) for n in order], *[leaf(3, n) for n in order])
```

```python
import functools

import jax
import jax.numpy as jnp
from jax import lax
from jax.experimental import pallas as pl
from jax.experimental.pallas import tpu as pltpu

F32 = jnp.float32
BF16 = jnp.bfloat16
NEG_INF = -1e30
EPS = 1e-6
N_DEV = 8
BLK = 128
GROUP = 4 * BLK
VMEM_LIMIT_BYTES = 56 * 1024 * 1024

D_MODEL = 1024
N_QKV = 3840
N_GATES = 3072
N_MAIN = N_QKV + N_GATES
N_FORGET = 8
N_IN = N_MAIN + N_FORGET
F_COL = 2304
FFN_HIDDEN = 2816
N_REL = 257

ADAM_LR, ADAM_B1, ADAM_B2, ADAM_EPS, ADAM_WD, ADAM_STEP = 0.001, 0.9, 0.999, 1e-08, 0.01, 10

NN = (((1,), (0,)), ((), ()))
NT = (((1,), (1,)), ((), ()))
TN = (((0,), (0,)), ((), ()))
HIGHEST = lax.Precision.HIGHEST

ATTN_COLS = {"a": (0, 4, 5), "b": (6, 10, 14), "c": (18, 22, 26)}
ATTN_WINDOW = {"a": 2, "c": 5}
ATTN_BLOCKS_PER_STEP = {"a": 4, "b": GROUP // BLK, "c": 2}


def _params():
    return pltpu.CompilerParams(vmem_limit_bytes=VMEM_LIMIT_BYTES)


def _tile(n, target):
    best = None
    t = 128
    while t <= min(n, target):
        if n % t == 0:
            best = t
        t += 128
    return best if best is not None else n


def _row_tile(n, target):
    t = min(n, target)
    while n % t:
        t -= 8
    return t


TILES = {
    "proj_qkv": (1024, 1280, 1024), "proj_gates": (1024, 768, 1024), "proj_forget": (1024, 128, 1024),
    "ffn_out": (1024, 512, 2816), "ffn_fused": (512, 1408),
    "wgrad_ffn_out": (1408, 1024, 1024), "wgrad_ffn_in": (1408, 1024, 1024), "dgrad_ffn_in": (1024, 1024, 1408),
    "wgrad_out": (1024, 1024, 1024),
    "wgrad_in": (1024, 1408, 1024), "dgrad_in": (1024, 1024, 1408),
}


def _matmul(a, b, mode, out_dtype, name, tiles, *, n=None, a_off=0, b_off=0, m=None, after=None):
    tm, tn, tk = tiles
    if mode == "nn":
        M, K = a.shape if m is None else (m, a.shape[1])
        N = b.shape[1] if n is None else n
    elif mode == "nt":
        M, K = a.shape
        N = b.shape[0] if n is None else n
    else:
        K = a.shape[0]
        M = a.shape[1] if m is None else m
        N = b.shape[1] if n is None else n
    tm = _tile(M, tm) if M % 128 == 0 else M
    tn = _tile(N, tn)
    tk = _tile(K, tk)
    nk = K // tk
    dims = {"nn": NN, "nt": NT, "tn": TN}[mode]
    if mode == "nn":
        a_spec = pl.BlockSpec((tm, tk), lambda i, j, k: (i + a_off, k))
        b_spec = pl.BlockSpec((tk, tn), lambda i, j, k: (k, j + b_off))
    elif mode == "nt":
        a_spec = pl.BlockSpec((tm, tk), lambda i, j, k: (i + a_off, k))
        b_spec = pl.BlockSpec((tn, tk), lambda i, j, k: (j + b_off, k))
    else:
        a_spec = pl.BlockSpec((tk, tm), lambda i, j, k: (k, i + a_off))
        b_spec = pl.BlockSpec((tk, tn), lambda i, j, k: (k, j + b_off))

    def body(a_ref, b_ref, *rest):
        o_ref, acc_ref = rest[-2:]
        k = pl.program_id(2)
        part = lax.dot_general(a_ref[...], b_ref[...], dims, preferred_element_type=F32)
        if nk == 1:
            o_ref[...] = part.astype(o_ref.dtype)
        else:
            @pl.when(k == 0)
            def _():
                acc_ref[...] = part

            @pl.when(k > 0)
            def _():
                acc_ref[...] += part

            @pl.when(k == nk - 1)
            def _():
                o_ref[...] = acc_ref[...].astype(o_ref.dtype)

    return pl.pallas_call(
        body, name=name,
        out_shape=jax.ShapeDtypeStruct((M, N), out_dtype),
        grid=(M // tm, N // tn, nk),
        in_specs=[a_spec, b_spec] + ([ANY] if after is not None else []),
        out_specs=pl.BlockSpec((tm, tn), lambda i, j, k: (i, j)),
        scratch_shapes=[pltpu.VMEM((tm, tn) if nk > 1 else (8, 128), F32)],
        compiler_params=_params(),
    )(a, b, *([after] if after is not None else []))


def _matmul_resid(a, b, resid, gate, name, tiles, after=None):
    M, K = a.shape
    N = b.shape[1]
    tm, tn, tk = (_tile(d, t) for d, t in zip((M, N, K), tiles))
    nk = K // tk

    def body(a_ref, b_ref, r_ref, g_ref, *rest):
        o_ref, s_ref, acc_ref = rest[-3:]
        k = pl.program_id(2)
        part = jnp.dot(a_ref[...], b_ref[...], preferred_element_type=F32)

        def finish(acc):
            o_ref[...] = r_ref[...] + g_ref[...] * acc
            s_ref[...] = acc.astype(BF16)

        if nk == 1:
            finish(part)
        else:
            @pl.when(k == 0)
            def _():
                acc_ref[...] = part

            @pl.when(k > 0)
            def _():
                acc_ref[...] += part

            @pl.when(k == nk - 1)
            def _():
                finish(acc_ref[...])

    return pl.pallas_call(
        body, name=name,
        out_shape=(jax.ShapeDtypeStruct((M, N), F32), jax.ShapeDtypeStruct((M, N), BF16)),
        grid=(M // tm, N // tn, nk),
        in_specs=[pl.BlockSpec((tm, tk), lambda i, j, k: (i, k)),
                  pl.BlockSpec((tk, tn), lambda i, j, k: (k, j)),
                  pl.BlockSpec((tm, tn), lambda i, j, k: (i, j)),
                  pl.BlockSpec((1, tn), lambda i, j, k: (0, j))] + ([ANY] if after is not None else []),
        out_specs=(pl.BlockSpec((tm, tn), lambda i, j, k: (i, j)),
                   pl.BlockSpec((tm, tn), lambda i, j, k: (i, j))),
        scratch_shapes=[pltpu.VMEM((tm, tn) if nk > 1 else (8, 128), F32)],
        compiler_params=_params(),
    )(a, b, resid, gate, *([after] if after is not None else []))


def _norm_mod_fwd(x, g, shift, scale, name):
    S, D = x.shape
    ts = _row_tile(S, 256)

    def body(x_ref, g_ref, sh_ref, sc_ref, h_ref):
        xv = x_ref[...]
        rstd = lax.rsqrt(jnp.mean(xv * xv, axis=-1, keepdims=True) + EPS)
        y = xv * rstd * g_ref[...]
        h_ref[...] = (y * (1.0 + sc_ref[...]) + sh_ref[...]).astype(BF16)

    row = pl.BlockSpec((1, D), lambda i: (0, 0))
    return pl.pallas_call(
        body, name=name, out_shape=jax.ShapeDtypeStruct((S, D), BF16), grid=(S // ts,),
        in_specs=[pl.BlockSpec((ts, D), lambda i: (i, 0)), row, row, row],
        out_specs=pl.BlockSpec((ts, D), lambda i: (i, 0)),
        compiler_params=_params(),
    )(x, g, shift, scale)


def _norm_mod_bwd(x, dh, dres, g, scale, name):
    S, D = x.shape
    ts = _row_tile(S, 256)

    def body(x_ref, dh_ref, dr_ref, g_ref, sc_ref, dx_ref, dsh_ref, dsc_ref, dg_ref):
        i = pl.program_id(0)
        xv, dhv, gv = x_ref[...], dh_ref[...], g_ref[...]
        rstd = lax.rsqrt(jnp.mean(xv * xv, axis=-1, keepdims=True) + EPS)
        xhat = xv * rstd
        dn = dhv * (1.0 + sc_ref[...])
        dxhat = dn * gv
        proj = jnp.mean(dxhat * xhat, axis=-1, keepdims=True)
        dx_ref[...] = dr_ref[...] + rstd * (dxhat - xhat * proj)
        dsh = jnp.sum(dhv, axis=0, keepdims=True)
        dsc = jnp.sum(dhv * (xhat * gv), axis=0, keepdims=True)
        dg = jnp.sum(dn * xhat, axis=0, keepdims=True)

        @pl.when(i == 0)
        def _():
            dsh_ref[...] = dsh
            dsc_ref[...] = dsc
            dg_ref[...] = dg

        @pl.when(i > 0)
        def _():
            dsh_ref[...] += dsh
            dsc_ref[...] += dsc
            dg_ref[...] += dg

    tile = pl.BlockSpec((ts, D), lambda i: (i, 0))
    row = pl.BlockSpec((1, D), lambda i: (0, 0))
    vec = jax.ShapeDtypeStruct((1, D), F32)
    return pl.pallas_call(
        body, name=name, out_shape=(jax.ShapeDtypeStruct((S, D), F32), vec, vec, vec), grid=(S // ts,),
        in_specs=[tile, tile, tile, row, row], out_specs=(tile, row, row, row),
        compiler_params=_params(),
    )(x, dh, dres, g, scale)


def _gate_bwd(dx, f, gate, name):
    S, D = dx.shape
    ts = _row_tile(S, 256)

    def body(dx_ref, f_ref, g_ref, df_ref, dg_ref):
        i = pl.program_id(0)
        dxv = dx_ref[...]
        df_ref[...] = (dxv * g_ref[...]).astype(BF16)
        dg = jnp.sum(dxv * f_ref[...].astype(F32), axis=0, keepdims=True)

        @pl.when(i == 0)
        def _():
            dg_ref[...] = dg

        @pl.when(i > 0)
        def _():
            dg_ref[...] += dg

    tile = pl.BlockSpec((ts, D), lambda i: (i, 0))
    row = pl.BlockSpec((1, D), lambda i: (0, 0))
    return pl.pallas_call(
        body, name=name,
        out_shape=(jax.ShapeDtypeStruct((S, D), BF16), jax.ShapeDtypeStruct((1, D), F32)), grid=(S // ts,),
        in_specs=[tile, tile, row], out_specs=(tile, row),
        compiler_params=_params(),
    )(dx, f, gate)


def _ffn_in_fwd(h, w_t, name):
    S, D = h.shape
    F = w_t.shape[0] // 2
    tm, tn = _tile(S, TILES["ffn_fused"][0]), _tile(F, TILES["ffn_fused"][1])
    nj = F // tn

    def body(h_ref, wg_ref, wu_ref, o_ref):
        hv = h_ref[...]
        ug = lax.dot_general(hv, wg_ref[...], NT, preferred_element_type=F32)
        uu = lax.dot_general(hv, wu_ref[...], NT, preferred_element_type=F32)
        o_ref[...] = (ug * jax.nn.sigmoid(ug) * uu).astype(BF16)

    return pl.pallas_call(
        body, name=name, out_shape=jax.ShapeDtypeStruct((S, F), BF16), grid=(nj, S // tm),
        in_specs=[pl.BlockSpec((tm, D), lambda j, i: (i, 0)),
                  pl.BlockSpec((tn, D), lambda j, i: (j, 0)),
                  pl.BlockSpec((tn, D), lambda j, i: (j + nj, 0))],
        out_specs=pl.BlockSpec((tm, tn), lambda j, i: (i, j)),
        compiler_params=_params(),
    )(h, w_t, w_t)


def _ffn_mid_bwd(h, df, w_in_t, w_out, name):
    S, D = h.shape
    F = w_in_t.shape[0] // 2
    tm, tn = _tile(S, TILES["ffn_fused"][0]), _tile(F, TILES["ffn_fused"][1])
    nj = F // tn

    def body(h_ref, df_ref, wg_ref, wu_ref, wo_ref, dg_ref, du_ref):
        hv = h_ref[...]
        ug = lax.dot_general(hv, wg_ref[...], NT, preferred_element_type=F32)
        uu = lax.dot_general(hv, wu_ref[...], NT, preferred_element_type=F32)
        dact = lax.dot_general(df_ref[...], wo_ref[...], NT, preferred_element_type=F32)
        sig = jax.nn.sigmoid(ug)
        dg_ref[...] = (dact * uu * (sig * (1.0 + ug * (1.0 - sig)))).astype(BF16)
        du_ref[...] = (dact * (ug * sig)).astype(BF16)

    out = jax.ShapeDtypeStruct((S, F), BF16)
    return pl.pallas_call(
        body, name=name, out_shape=(out, out), grid=(nj, S // tm),
        in_specs=[pl.BlockSpec((tm, D), lambda j, i: (i, 0)),
                  pl.BlockSpec((tm, D), lambda j, i: (i, 0)),
                  pl.BlockSpec((tn, D), lambda j, i: (j, 0)),
                  pl.BlockSpec((tn, D), lambda j, i: (j + nj, 0)),
                  pl.BlockSpec((tn, D), lambda j, i: (j, 0))],
        out_specs=(pl.BlockSpec((tm, tn), lambda j, i: (i, j)), pl.BlockSpec((tm, tn), lambda j, i: (i, j))),
        compiler_params=_params(),
    )(h, df, w_in_t, w_in_t, w_out)


def _merge_fwd(o_a, o_b, o_c, gates, w_branch, w_out, resid, gate, name, *, tm=512):
    S, W = o_a.shape
    D = w_branch.shape[2]
    tm = _row_tile(S, tm)

    def body(oa_ref, ob_ref, oc_ref, g_ref, w_ref, wo_ref, r_ref, gm_ref, x_ref, m_ref, mix_ref):
        acc = None
        for k, o_ref in enumerate((oa_ref, ob_ref, oc_ref)):
            y = jnp.dot(o_ref[...], w_ref[k], preferred_element_type=F32)
            t = jax.nn.sigmoid(g_ref[:, k * D:(k + 1) * D]) * y
            acc = t if acc is None else acc + t
        merged = acc.astype(BF16)
        m_ref[...] = merged
        mix = jnp.dot(merged, wo_ref[...], preferred_element_type=F32)
        x_ref[...] = r_ref[...] + gm_ref[...] * mix
        mix_ref[...] = mix.astype(BF16)

    o_spec = pl.BlockSpec((tm, W), lambda i: (i, 0))
    tile = pl.BlockSpec((tm, D), lambda i: (i, 0))
    return pl.pallas_call(
        body, name=name,
        out_shape=(jax.ShapeDtypeStruct((S, D), F32), jax.ShapeDtypeStruct((S, D), BF16), jax.ShapeDtypeStruct((S, D), BF16)),
        grid=(S // tm,),
        in_specs=[o_spec, o_spec, o_spec, pl.BlockSpec((tm, 3 * D), lambda i: (i, 0)),
                  pl.BlockSpec((3, W, D), lambda i: (0, 0, 0)), pl.BlockSpec((D, D), lambda i: (0, 0)),
                  tile, pl.BlockSpec((1, D), lambda i: (0, 0))],
        out_specs=(tile, tile, tile),
        compiler_params=_params(),
    )(o_a, o_b, o_c, gates, w_branch, w_out, resid, gate)


def _merge_bwd(dmix, o_a, o_b, o_c, gates, w_branch, w_out, name, *, tm=256):
    S, W = o_a.shape
    D = w_branch.shape[2]
    tm = _row_tile(S, tm)
    n_heads = W // 64

    def body(dm_ref, oa_ref, ob_ref, oc_ref, g_ref, w_ref, wo_ref, dg_ref, dw_ref,
             doa_ref, dob_ref, doc_ref, dla_ref, dlb_ref, dlc_ref):
        first = pl.program_id(0) == 0
        dm = lax.dot_general(dm_ref[...], wo_ref[...], NT, preferred_element_type=F32)
        branches = ((oa_ref, doa_ref, dla_ref), (ob_ref, dob_ref, dlb_ref), (oc_ref, doc_ref, dlc_ref))
        for k, (o_ref, do_ref, dl_ref) in enumerate(branches):
            wk = w_ref[k]
            ov = o_ref[...]
            y = jnp.dot(ov, wk, preferred_element_type=F32)
            g = jax.nn.sigmoid(g_ref[:, k * D:(k + 1) * D])
            dy = (dm * g).astype(BF16)
            dwk = lax.dot_general(ov, dy, TN, preferred_element_type=F32)

            @pl.when(first)
            def _(k=k, dwk=dwk):
                dw_ref[k] = dwk

            @pl.when(jnp.logical_not(first))
            def _(k=k, dwk=dwk):
                dw_ref[k] += dwk
            dg_ref[:, k * D:(k + 1) * D] = (dm * y * (g * (1.0 - g))).astype(BF16)
            do16 = lax.dot_general(dy, wk, NT, preferred_element_type=F32).astype(BF16)
            do_ref[...] = do16
            prod = do16.astype(F32) * ov.astype(F32)
            for h in range(n_heads):
                dl_ref[:, h:h + 1] = jnp.sum(prod[:, 64 * h:64 * (h + 1)], axis=1, keepdims=True)

    o_spec = pl.BlockSpec((tm, W), lambda i: (i, 0))
    wide = pl.BlockSpec((tm, 3 * D), lambda i: (i, 0))
    dl_spec = pl.BlockSpec((tm, n_heads), lambda i: (i, 0))
    o_out = jax.ShapeDtypeStruct((S, W), BF16)
    wide_out = jax.ShapeDtypeStruct((S, 3 * D), BF16)
    dl_out = jax.ShapeDtypeStruct((S, n_heads), F32)
    whole = pl.BlockSpec((3, W, D), lambda i: (0, 0, 0))
    return pl.pallas_call(
        body, name=name,
        out_shape=(wide_out, jax.ShapeDtypeStruct((3, W, D), F32), o_out, o_out, o_out, dl_out, dl_out, dl_out),
        grid=(S // tm,),
        in_specs=[pl.BlockSpec((tm, D), lambda i: (i, 0)), o_spec, o_spec, o_spec, wide, whole,
                  pl.BlockSpec((D, D), lambda i: (0, 0))],
        out_specs=(wide, whole, o_spec, o_spec, o_spec, dl_spec, dl_spec, dl_spec),
        compiler_params=_params(),
    )(dmix, o_a, o_b, o_c, gates, w_branch, w_out)


def _band_mask(variant, t_abs, s_abs):
    if variant == "b":
        return s_abs <= t_abs
    qc, kc = t_abs >> 6, s_abs >> 6
    return (kc <= qc) & (kc >= qc - (2 if variant == "a" else 8))


def _attn_fwd(variant, qkv, name, *, sinks=None, slopes=None, cq_col=None, ck_row=None, bias=None, after=None):
    S = qkv.shape[0]
    nb = S // BLK
    qb, kb, vb = ATTN_COLS[variant]
    shared_kv = variant == "a"
    win = ATTN_WINDOW.get(variant)
    per_step = ATTN_BLOCKS_PER_STEP[variant]

    def body(*refs):
        if after is not None:
            refs = refs[:-3] + refs[-2:]
        if variant == "a":
            q_ref, k_ref, v_ref, sink_ref, slope_ref, o_ref, lse_ref = refs
        elif variant == "b":
            q_ref, k_ref, v_ref, cq_ref, ck_ref, o_ref, lse_ref = refs
        else:
            q_ref, k_ref, v_ref, bias_ref, o_ref, lse_ref = refs
        p = pl.program_id(0)
        lane = lax.broadcasted_iota(jnp.int32, (1, BLK), 1)

        def compute(i, rows, start, n_keys):
            n_rows = rows.stop - rows.start
            t_abs = i * BLK + lax.broadcasted_iota(jnp.int32, (n_rows, 1), 0)
            q2 = q_ref[rows, :].astype(F32) * 0.125
            k_w = k_ref[pl.ds(start, n_keys), :]
            v_w = v_ref[pl.ds(start, n_keys), :]
            s_abs = start + lax.broadcasted_iota(jnp.int32, (1, n_keys), 1)
            valid = _band_mask(variant, t_abs, s_abs)
            outs = []
            for half in (0, 1):
                hmask = (lane >= 64) if half else (lane < 64)
                qh = jnp.where(hmask, q2, 0.0)
                if shared_kv:
                    swap = (p // 2) != half
                    qh = jnp.where(swap, pltpu.roll(qh, 64, 1), qh)
                s = lax.dot_general(qh.astype(BF16), k_w, NT, preferred_element_type=F32)
                if variant == "a":
                    head = 2 * p + half
                    s = s + (-slope_ref[head]) * jnp.abs(t_abs - s_abs).astype(F32)
                elif variant == "b":
                    s = s + cq_ref[rows, half:half + 1] - ck_ref[half:half + 1, pl.ds(start, n_keys)]
                else:
                    j0 = start // BLK
                    s = s + jnp.concatenate(
                        [bias_ref[half, jnp.clip(i - j0 - b, 0, 4)] for b in range(win)], axis=1)
                s = jnp.where(valid, s, NEG_INF)
                m = jnp.max(s, axis=1, keepdims=True)
                if variant == "a":
                    m = jnp.maximum(m, sink_ref[head])
                pe = jnp.exp(s - m)
                l = jnp.sum(pe, axis=1, keepdims=True)
                if variant == "a":
                    l = l + jnp.exp(sink_ref[head] - m)
                out = jnp.dot(pe.astype(BF16), v_w, preferred_element_type=F32) / l
                if shared_kv:
                    out = jnp.where(swap, pltpu.roll(out, 64, 1), out)
                outs.append(out)
                lse_ref[rows, half:half + 1] = m + jnp.log(l)
            o_ref[rows, :] = jnp.where(lane < 64, outs[0], outs[1]).astype(BF16)

        step = pl.program_id(1)
        if variant == "b":
            for g in range(S // GROUP):
                pl.when(step == g)(functools.partial(compute, step * per_step, slice(0, GROUP), 0, (g + 1) * GROUP))
        else:
            for sub in range(per_step):
                i = step * per_step + sub
                start = jnp.clip(i - (win - 1), 0, nb - win) * BLK
                compute(i, slice(sub * BLK, (sub + 1) * BLK), pl.multiple_of(start, BLK), win * BLK)

    tq = per_step * BLK
    kv_col = (lambda p, i: (0, kb)) if shared_kv else (lambda p, i: (0, kb + p))
    vv_col = (lambda p, i: (0, vb)) if shared_kv else (lambda p, i: (0, vb + p))
    in_specs = [pl.BlockSpec((tq, BLK), lambda p, i: (i, qb + p)),
                pl.BlockSpec((S, BLK), kv_col), pl.BlockSpec((S, BLK), vv_col)]
    args = [qkv, qkv, qkv]
    if variant == "a":
        in_specs += [pl.BlockSpec(memory_space=pltpu.SMEM), pl.BlockSpec(memory_space=pltpu.SMEM)]
        args += [sinks, slopes]
    elif variant == "b":
        in_specs += [pl.BlockSpec((None, tq, 2), lambda p, i: (p, i, 0)),
                     pl.BlockSpec((None, 2, S), lambda p, i: (p, 0, 0))]
        args += [cq_col, ck_row]
    else:
        in_specs += [pl.BlockSpec((2, 5, BLK, BLK), lambda p, i: (p, 0, 0, 0))]
        args += [bias]
    if after is not None:
        in_specs.append(ANY)
        args.append(after)
    return pl.pallas_call(
        body, name=name,
        out_shape=(jax.ShapeDtypeStruct((S, 512), BF16), jax.ShapeDtypeStruct((4, S, 2), F32)),
        grid=(4, nb // per_step), in_specs=in_specs,
        out_specs=(pl.BlockSpec((tq, BLK), lambda p, i: (i, p)),
                   pl.BlockSpec((None, tq, 2), lambda p, i: (p, i, 0))),
        compiler_params=_params(),
    )(*args)


def _attn_bwd(variant, qkv, do, lse_row, delta_row, name, *, sinks=None, slopes=None, cq_row=None,
              ck_col=None, bias_t=None):
    S = qkv.shape[0]
    nb = S // BLK
    qb, kb, vb = ATTN_COLS[variant]
    shared_kv = variant == "a"
    win = ATTN_WINDOW.get(variant)
    per_step = ATTN_BLOCKS_PER_STEP[variant]

    def body(*refs):
        if variant == "a":
            (q_ref, k_ref, v_ref, do_ref, lse_ref, dl_ref, sink_ref, slope_ref,
             dq_ref, dk_ref, dv_ref, ex_ref) = refs
        elif variant == "b":
            (q_ref, k_ref, v_ref, do_ref, lse_ref, dl_ref, cq_ref, ck_ref,
             dq_ref, dk_ref, dv_ref, ex_ref, dcq_ref) = refs
        else:
            (q_ref, k_ref, v_ref, do_ref, lse_ref, dl_ref, bias_ref,
             dq_ref, dk_ref, dv_ref, ex_ref) = refs
        p = pl.program_id(0)
        lane = lax.broadcasted_iota(jnp.int32, (1, BLK), 1)
        hmasks = [(lane < 64), (lane >= 64)]
        swaps = [(p // 2) != half for half in (0, 1)] if shared_kv else None

        @pl.when(pl.program_id(1) == 0)
        def _():
            dq_ref[...] = jnp.zeros_like(dq_ref)
            if variant == "b":
                dcq_ref[...] = jnp.zeros_like(dcq_ref)
            else:
                ex_ref[...] = jnp.zeros_like(ex_ref)

        def to_kv_lanes(x, h):
            x = jnp.where(hmasks[h], x, 0.0)
            if shared_kv:
                x = jnp.where(swaps[h], pltpu.roll(x, 64, 1), x)
            return x

        def compute(j, rows, start, n_q):
            n_rows = rows.stop - rows.start
            s_abs = j * BLK + lax.broadcasted_iota(jnp.int32, (n_rows, 1), 0)
            off_k = pl.multiple_of(j * BLK, BLK)
            k2 = k_ref[rows, :].astype(F32)
            v2 = v_ref[rows, :].astype(F32)
            if shared_kv:
                kv_lane = (lane >> 6) == (p // 2)
                k_src, v_src = jnp.where(kv_lane, k2, 0.0), jnp.where(kv_lane, v2, 0.0)
                k_al = [jnp.where(swaps[h], pltpu.roll(k_src, 64, 1), k_src) for h in (0, 1)]
                v_al = [jnp.where(swaps[h], pltpu.roll(v_src, 64, 1), v_src) for h in (0, 1)]
            else:
                k_al = [jnp.where(hmasks[h], k2, 0.0) for h in (0, 1)]
                v_al = [jnp.where(hmasks[h], v2, 0.0) for h in (0, 1)]
            k_al = [(t * 0.125).astype(BF16) for t in k_al]
            v_al = [t.astype(BF16) for t in v_al]
            q_w = q_ref[pl.ds(start, n_q), :]
            do_w = do_ref[pl.ds(start, n_q), :]
            t_abs = start + lax.broadcasted_iota(jnp.int32, (1, n_q), 1)
            valid = _band_mask(variant, t_abs, s_abs)
            dk_acc = dv_acc = None
            ds_both = []
            for half in (0, 1):
                s = lax.dot_general(k_al[half], q_w, NT, preferred_element_type=F32)
                if variant == "a":
                    s = s + (-slope_ref[2 * p + half]) * jnp.abs(t_abs - s_abs).astype(F32)
                elif variant == "b":
                    s = s + cq_ref[half:half + 1, pl.ds(start, n_q)] - ck_ref[rows, half:half + 1]
                else:
                    i0 = start // BLK
                    s = s + jnp.concatenate(
                        [bias_ref[half, jnp.clip(i0 + b - j, 0, 4)] for b in range(win)], axis=1)
                pr = jnp.where(valid, jnp.exp(s - lse_ref[half:half + 1, pl.ds(start, n_q)]), 0.0)
                dp = lax.dot_general(v_al[half], do_w, NT, preferred_element_type=F32)
                ds = pr * (dp - dl_ref[half:half + 1, pl.ds(start, n_q)])
                ds16 = ds.astype(BF16)
                dv_h = to_kv_lanes(jnp.dot(pr.astype(BF16), do_w, preferred_element_type=F32), half)
                dk_h = to_kv_lanes(jnp.dot(ds16, q_w, preferred_element_type=F32) * 0.125, half)
                dv_acc = dv_h if dv_acc is None else dv_acc + dv_h
                dk_acc = dk_h if dk_acc is None else dk_acc + dk_h
                ds_both.append(ds16)
                if variant == "b":
                    ex_ref[rows, half:half + 1] = -jnp.sum(ds, axis=1, keepdims=True)
                    dcq_ref[half:half + 1, pl.ds(start, n_q)] += jnp.sum(ds, axis=0, keepdims=True)
                elif variant == "c":
                    for b in range(win):
                        ex_ref[half, jnp.clip(i0 + b - j, 0, 4)] += ds[:, b * BLK:(b + 1) * BLK]
            dq_t = lax.dot_general(jnp.concatenate(k_al, axis=0), jnp.concatenate(ds_both, axis=0), TN,
                                   preferred_element_type=F32)
            dq_ref[:, pl.ds(start, n_q)] += dq_t
            if shared_kv:
                @pl.when(p == 0)
                def _():
                    dk_ref[pl.ds(off_k, n_rows), :] = dk_acc
                    dv_ref[pl.ds(off_k, n_rows), :] = dv_acc

                @pl.when(p > 0)
                def _():
                    dk_ref[pl.ds(off_k, n_rows), :] += dk_acc
                    dv_ref[pl.ds(off_k, n_rows), :] += dv_acc
            else:
                dk_ref[pl.ds(off_k, n_rows), :] = dk_acc
                dv_ref[pl.ds(off_k, n_rows), :] = dv_acc
            if variant == "a":
                for half in (0, 1):
                    p_sink = jnp.exp(sink_ref[2 * p + half] - lse_ref[half:half + 1, pl.ds(off_k, n_rows)])
                    term = p_sink * dl_ref[half:half + 1, pl.ds(off_k, n_rows)]
                    ex_ref[half:half + 1, :] += -jnp.sum(term, axis=1, keepdims=True)

        step = pl.program_id(1)
        if variant == "b":
            for g in range(S // GROUP):
                pl.when(step == g)(functools.partial(compute, step * per_step, slice(0, GROUP), g * GROUP, S - g * GROUP))
        else:
            for sub in range(per_step):
                j = step * per_step + sub
                start = jnp.clip(j, 0, nb - win) * BLK
                compute(j, slice(sub * BLK, (sub + 1) * BLK), pl.multiple_of(start, BLK), win * BLK)

    tk = per_step * BLK
    col = lambda c0: (lambda p, j: (0, c0 + p))
    kv_blk = (lambda c0: (lambda p, j: (j, c0))) if shared_kv else (lambda c0: (lambda p, j: (j, c0 + p)))
    pair = lambda p, j: (0, p)
    row_stat = pl.BlockSpec((None, 2, S), lambda p, j: (p, 0, 0))
    in_specs = [pl.BlockSpec((S, BLK), col(qb)),
                pl.BlockSpec((tk, BLK), kv_blk(kb)), pl.BlockSpec((tk, BLK), kv_blk(vb)),
                pl.BlockSpec((S, BLK), pair), row_stat, row_stat]
    args = [qkv, qkv, qkv, do, lse_row, delta_row]
    kv_width = BLK if shared_kv else 512
    kv_out = pl.BlockSpec((S, BLK), (lambda p, j: (0, 0)) if shared_kv else pair)
    out_shape = [jax.ShapeDtypeStruct((512, S), F32), jax.ShapeDtypeStruct((S, kv_width), F32),
                 jax.ShapeDtypeStruct((S, kv_width), F32)]
    out_specs = [pl.BlockSpec((BLK, S), lambda p, j: (p, 0)), kv_out, kv_out]
    if variant == "a":
        in_specs += [pl.BlockSpec(memory_space=pltpu.SMEM), pl.BlockSpec(memory_space=pltpu.SMEM)]
        args += [sinks, slopes]
        out_shape.append(jax.ShapeDtypeStruct((4, 8, BLK), F32))
        out_specs.append(pl.BlockSpec((None, 8, BLK), lambda p, j: (p, 0, 0)))
    elif variant == "b":
        in_specs += [row_stat, pl.BlockSpec((None, tk, 2), lambda p, j: (p, j, 0))]
        args += [cq_row, ck_col]
        out_shape += [jax.ShapeDtypeStruct((4, S, 2), F32), jax.ShapeDtypeStruct((4, 2, S), F32)]
        out_specs += [pl.BlockSpec((None, tk, 2), lambda p, j: (p, j, 0)), row_stat]
    else:
        in_specs += [pl.BlockSpec((2, 5, BLK, BLK), lambda p, j: (p, 0, 0, 0))]
        args += [bias_t]
        out_shape.append(jax.ShapeDtypeStruct((8, 5, BLK, BLK), F32))
        out_specs.append(pl.BlockSpec((2, 5, BLK, BLK), lambda p, j: (p, 0, 0, 0)))
    return pl.pallas_call(
        body, name=name, out_shape=tuple(out_shape), grid=(4, nb // per_step),
        in_specs=in_specs, out_specs=tuple(out_specs),
        compiler_params=_params(),
    )(*args)


def _log_sigmoid(x):
    return jnp.minimum(x, 0.0) - jnp.log(1.0 + jnp.exp(-jnp.abs(x)))


def _forget_fwd(fb, b_forget, name):
    S = fb.shape[0]
    nb = S // BLK

    def body(fb_ref, b_ref, cum_ref, carry_ref):
        i = pl.program_id(0)
        logf = _log_sigmoid(fb_ref[...] + b_ref[...])
        r = lax.broadcasted_iota(jnp.int32, (BLK, BLK), 0)
        c = lax.broadcasted_iota(jnp.int32, (BLK, BLK), 1)
        tri = (c <= r).astype(F32)

        @pl.when(i == 0)
        def _():
            carry_ref[...] = jnp.zeros_like(carry_ref)

        cum = jnp.dot(tri, logf, preferred_element_type=F32, precision=HIGHEST) + carry_ref[0:1, :]
        cum_ref[...] = cum
        carry_ref[...] = jnp.broadcast_to(cum[BLK - 1:BLK, :], carry_ref.shape)

    return pl.pallas_call(
        body, name=name, out_shape=jax.ShapeDtypeStruct((S, BLK), F32), grid=(nb,),
        in_specs=[pl.BlockSpec((BLK, BLK), lambda i: (i, 0)), pl.BlockSpec((1, BLK), lambda i: (0, 0))],
        out_specs=pl.BlockSpec((BLK, BLK), lambda i: (i, 0)),
        scratch_shapes=[pltpu.VMEM((8, BLK), F32)],
        compiler_params=_params(),
    )(fb, b_forget)


def _forget_bwd(dcum_q, dcum_k, fb, b_forget, name):
    S = fb.shape[0]
    nb = S // BLK

    def body(dq_ref, dk_ref, fb_ref, b_ref, dfb_ref, db_ref, carry_ref):
        g = pl.program_id(0)
        r = lax.broadcasted_iota(jnp.int32, (BLK, BLK), 0)
        c = lax.broadcasted_iota(jnp.int32, (BLK, BLK), 1)
        tri = (c >= r).astype(F32)

        @pl.when(g == 0)
        def _():
            carry_ref[...] = jnp.zeros_like(carry_ref)

        dcum = dq_ref[...] + dk_ref[...]
        dlogf = jnp.dot(tri, dcum, preferred_element_type=F32, precision=HIGHEST) + carry_ref[0:1, :]
        carry_ref[...] = jnp.broadcast_to(dlogf[0:1, :], carry_ref.shape)
        x = fb_ref[...] + b_ref[...]
        dfb = jnp.where(c < N_FORGET, dlogf * jax.nn.sigmoid(-x), 0.0)
        dfb_ref[...] = dfb
        db = jnp.sum(dfb, axis=0, keepdims=True)

        @pl.when(g == 0)
        def _():
            db_ref[...] = db

        @pl.when(g > 0)
        def _():
            db_ref[...] += db

    rev = pl.BlockSpec((BLK, BLK), lambda g: (nb - 1 - g, 0))
    row = pl.BlockSpec((1, BLK), lambda g: (0, 0))
    return pl.pallas_call(
        body, name=name,
        out_shape=(jax.ShapeDtypeStruct((S, BLK), F32), jax.ShapeDtypeStruct((1, BLK), F32)), grid=(nb,),
        in_specs=[rev, rev, rev, row], out_specs=(rev, row),
        scratch_shapes=[pltpu.VMEM((8, BLK), F32)],
        compiler_params=_params(),
    )(dcum_q, dcum_k, fb, b_forget)


def _skew(x, sign):
    row = lax.broadcasted_iota(jnp.int32, x.shape, 0)
    for b in range(7):
        amount = (1 << b) if sign > 0 else 256 - (1 << b)
        x = jnp.where(((row >> b) & 1) == 1, pltpu.roll(x, amount, 1), x)
    return x


def _rel_bases(rel):
    far = rel[:, 256:257]
    far127 = jnp.broadcast_to(far, (rel.shape[0], 127))
    base0 = jnp.concatenate([rel[:, 128:0:-1], far, rel[:, 255:128:-1]], axis=1)
    base1 = jnp.concatenate([rel[:, 256:128:-1], far, far127], axis=1)
    base0_t = jnp.concatenate([rel[:, 128:256], far, rel[:, 1:128]], axis=1)
    base1_t = jnp.concatenate([jnp.broadcast_to(far, (rel.shape[0], 128)), far, rel[:, 129:256]], axis=1)
    return jnp.stack([base0, base1, base0_t, base1_t], axis=1)


def _rel_expand(bases, name):
    def body(b_ref, t_ref, tt_ref):
        far = jnp.broadcast_to(b_ref[1:2, 0:1], (BLK, BLK))
        for k, out_ref in ((0, t_ref), (2, tt_ref)):
            for d in (0, 1):
                x = jnp.broadcast_to(b_ref[k + d:k + d + 1, :], (BLK, 2 * BLK))
                out_ref[d] = _skew(x, 1)[:, :BLK]
            for d in (2, 3, 4):
                out_ref[d] = far

    out = jax.ShapeDtypeStruct((8, 5, BLK, BLK), F32)
    spec = pl.BlockSpec((None, 5, BLK, BLK), lambda h: (h, 0, 0, 0))
    return pl.pallas_call(
        body, name=name, out_shape=(out, out), grid=(8,),
        in_specs=[pl.BlockSpec((None, 4, 2 * BLK), lambda h: (h, 0, 0))], out_specs=(spec, spec),
        compiler_params=_params(),
    )(bases)


def _rel_reduce(dtiles_t, name):
    def body(dt_ref, o_ref):
        zeros = jnp.zeros((BLK, BLK), F32)
        sums = []
        for d in (0, 1):
            x = _skew(jnp.concatenate([dt_ref[d], zeros], axis=1), -1)
            sums.append(jnp.broadcast_to(jnp.sum(x, axis=0, keepdims=True), (8, 2 * BLK)))
        lane = lax.broadcasted_iota(jnp.int32, (8, 2 * BLK), 1)
        main = pltpu.roll(sums[0], BLK, 1) + jnp.where(lane > BLK, sums[1], 0.0)
        far = jnp.sum(jnp.where(lane < BLK, sums[1], 0.0)[0:1], axis=1, keepdims=True)
        far = far + jnp.sum(jnp.sum(dt_ref[2] + dt_ref[3] + dt_ref[4], axis=0, keepdims=True), axis=1, keepdims=True)
        o_ref[...] = jnp.concatenate([main[0:1], jnp.broadcast_to(far, (1, BLK))], axis=1)

    return pl.pallas_call(
        body, name=name, out_shape=jax.ShapeDtypeStruct((8, 1, 3 * BLK), F32), grid=(8,),
        in_specs=[pl.BlockSpec((None, 5, BLK, BLK), lambda h: (h, 0, 0, 0))],
        out_specs=pl.BlockSpec((None, 1, 3 * BLK), lambda h: (h, 0, 0)),
        compiler_params=_params(),
    )(dtiles_t)


def _final_loss(x, target, g, name):
    S, D = x.shape
    ts = _row_tile(S, 256)

    def body(x_ref, t_ref, g_ref, dx_ref, loss_ref, dg_ref):
        i = pl.program_id(0)
        xv, gv = x_ref[...], g_ref[...]
        rstd = lax.rsqrt(jnp.mean(xv * xv, axis=-1, keepdims=True) + EPS)
        xhat = xv * rstd
        err = xhat * gv - t_ref[...]
        part = 0.5 * jnp.sum(jnp.mean(err * err, axis=-1, keepdims=True), axis=0, keepdims=True)
        dy = err / D
        dg = jnp.sum(dy * xhat, axis=0, keepdims=True)
        dxhat = dy * gv
        proj = jnp.mean(dxhat * xhat, axis=-1, keepdims=True)
        dx_ref[...] = rstd * (dxhat - xhat * proj)

        @pl.when(i == 0)
        def _():
            loss_ref[...] = jnp.broadcast_to(part, loss_ref.shape)
            dg_ref[...] = dg

        @pl.when(i > 0)
        def _():
            loss_ref[...] += jnp.broadcast_to(part, loss_ref.shape)
            dg_ref[...] += dg

    tile = pl.BlockSpec((ts, D), lambda i: (i, 0))
    row = pl.BlockSpec((1, D), lambda i: (0, 0))
    return pl.pallas_call(
        body, name=name,
        out_shape=(jax.ShapeDtypeStruct((S, D), F32), jax.ShapeDtypeStruct((8, 128), F32),
                   jax.ShapeDtypeStruct((1, D), F32)),
        grid=(S // ts,), in_specs=[tile, tile, row],
        out_specs=(tile, pl.BlockSpec((8, 128), lambda i: (0, 0)), row),
        compiler_params=_params(),
    )(x, target, g)


def _ada_fwd(c_all, w_ada, name):
    L, D, E = w_ada.shape

    def body(c_ref, w_ref, o_ref):
        cv = c_ref[...]
        cond = cv * jax.nn.sigmoid(cv)
        o_ref[...] = jnp.dot(cond, w_ref[...], preferred_element_type=F32, precision=HIGHEST)

    return pl.pallas_call(
        body, name=name, out_shape=jax.ShapeDtypeStruct((L, N_DEV, E), F32), grid=(L,),
        in_specs=[pl.BlockSpec((N_DEV, D), lambda l: (0, 0)), pl.BlockSpec((None, D, E), lambda l: (l, 0, 0))],
        out_specs=pl.BlockSpec((None, N_DEV, E), lambda l: (l, 0, 0)),
        compiler_params=_params(),
    )(c_all, w_ada)


def _ada_bwd(c_all_t, dmod, name):
    D = c_all_t.shape[0]
    L, _, E = dmod.shape

    def body(c_ref, d_ref, o_ref):
        cv = c_ref[...]
        cond = cv * jax.nn.sigmoid(cv)
        acc = None
        for b in range(N_DEV):
            t = cond[:, b:b + 1] * d_ref[b:b + 1, :]
            acc = t if acc is None else acc + t
        o_ref[...] = acc

    return pl.pallas_call(
        body, name=name, out_shape=jax.ShapeDtypeStruct((L, D, E), F32), grid=(L,),
        in_specs=[pl.BlockSpec((D, N_DEV), lambda l: (0, 0)), pl.BlockSpec((None, N_DEV, E), lambda l: (l, 0, 0))],
        out_specs=pl.BlockSpec((None, D, E), lambda l: (l, 0, 0)),
        compiler_params=_params(),
    )(c_all_t, dmod)


def _adamw(w, m, v, g_parts, name, me, after=None):
    L, R, C = w.shape
    tr = _row_tile(R, max(8, (256 * 1024 // max(C, 128)) // 8 * 8))
    nr = R // tr
    c1 = 1.0 - ADAM_B1 ** ADAM_STEP
    c2 = 1.0 - ADAM_B2 ** ADAM_STEP
    direct = [isinstance(p, tuple) for p in g_parts]
    n_in = sum(2 if d else 1 for d in direct)

    def body(me_ref, w_ref, m_ref, v_ref, *rest):
        g_refs, (go_ref, d_ref, mo_ref, vo_ref) = list(rest[:n_in]), rest[-4:]
        layer = pl.program_id(0)
        g = None
        for l in range(L):
            land_ref = g_refs.pop(0)
            own = g_refs.pop(0)[...].astype(F32) if direct[l] else None
            gl = None
            for k in range(land_ref.shape[0]):
                part = land_ref[k].astype(F32)
                if direct[l]:
                    part = jnp.where(me_ref[l] == k, own, part)
                gl = part if gl is None else gl + part
            g = gl if g is None else jnp.where(layer == l, gl, g)
        mn = ADAM_B1 * m_ref[...] + (1.0 - ADAM_B1) * g
        vn = ADAM_B2 * v_ref[...] + (1.0 - ADAM_B2) * (g * g)
        m_hat = mn / c1
        v_hat = vn / c2
        go_ref[...] = g
        d_ref[...] = -ADAM_LR * (m_hat / (jnp.sqrt(v_hat) + ADAM_EPS) + ADAM_WD * w_ref[...])
        mo_ref[...] = mn
        vo_ref[...] = vn

    def rows(l, layer, i):
        return jnp.where(layer == l, i, 0 if l > 0 else nr - 1)

    in_specs, operands = [], []
    for l, p in enumerate(g_parts):
        land, sent = p if direct[l] else (p, None)
        in_specs.append(pl.BlockSpec((land.shape[0], tr, C), lambda layer, i, me_ref, l=l: (0, rows(l, layer, i), 0)))
        operands.append(land)
        if direct[l]:
            in_specs.append(pl.BlockSpec((None, tr, C), lambda layer, i, me_ref, l=l: (me_ref[l], rows(l, layer, i), 0)))
            operands.append(sent)
    if after is not None:
        in_specs.append(ANY)
        operands.append(after)
    tile = pl.BlockSpec((None, tr, C), lambda layer, i, me_ref: (layer, i, 0))
    out = jax.ShapeDtypeStruct((L, R, C), F32)
    return pl.pallas_call(
        body, name=name, out_shape=(out, out, out, out),
        grid_spec=pltpu.PrefetchScalarGridSpec(
            num_scalar_prefetch=1, grid=(L, nr), in_specs=[tile, tile, tile] + in_specs,
            out_specs=(tile, tile, tile, tile)),
        compiler_params=_params(),
    )(me, w, m, v, *operands)


def _pair_add(pieces, recv, core, name):
    _, _, R, C = pieces.shape
    tr = _row_tile(R, max(8, (512 * 1024 // max(C, 128)) // 8 * 8))

    def body(core_ref, a_ref, b_ref, o_ref):
        o_ref[...] = (a_ref[...].astype(F32) + b_ref[...].astype(F32)).astype(BF16)

    return pl.pallas_call(
        body, name=name, out_shape=jax.ShapeDtypeStruct((4, R, C), BF16),
        grid_spec=pltpu.PrefetchScalarGridSpec(
            num_scalar_prefetch=1, grid=(4, R // tr),
            in_specs=[pl.BlockSpec((None, None, tr, C), lambda k, i, core_ref: (core_ref[0], k, i, 0)),
                      pl.BlockSpec((None, tr, C), lambda k, i, core_ref: (k, i, 0))],
            out_specs=pl.BlockSpec((None, tr, C), lambda k, i, core_ref: (k, i, 0))),
        compiler_params=_params(),
    )(core, pieces, recv)


MESH = pl.DeviceIdType.MESH
ANY = pl.BlockSpec(memory_space=pl.ANY)


def _position():
    return lax.axis_index("x"), lax.axis_index("y"), lax.axis_index("c")


def _small_all_gather(v, name):
    m_per, n = v.shape

    def body(x_ref, out_ref, send_sems, recv_sems, local_sem):
        x, y, c = _position()
        me, sibling = (x, y, c), (x, y, 1 - c)
        chips = [(1 - x, y), (x, 1 - y), (1 - x, 1 - y)]

        def rows(px, py, pc):
            return out_ref.at[pl.ds((4 * px + 2 * py + pc) * m_per, m_per), :]

        def copy(k, block, to, src=None):
            return pltpu.make_async_remote_copy(
                src_ref=rows(*block) if src is None else src, dst_ref=rows(*block),
                send_sem=send_sems.at[k], recv_sem=recv_sems.at[k], device_id=to, device_id_type=MESH)

        mine = pltpu.make_async_copy(x_ref, rows(*me), local_sem)
        mine.start()
        first = [copy(0, me, sibling, src=x_ref)]
        first += [copy(1 + j, me, (*chip, c), src=x_ref) for j, chip in enumerate(chips)]
        for cp in first:
            cp.start()
        passed = [copy(4 + j, (*chip, c), sibling) for j, chip in enumerate(chips)]
        for j, chip in enumerate(chips):
            copy(1 + j, (*chip, c), me).wait_recv()
            passed[j].start()
        copy(0, sibling, me).wait_recv()
        for j, chip in enumerate(chips):
            copy(4 + j, (*chip, 1 - c), me).wait_recv()
        for cp in first + passed:
            cp.wait_send()
        mine.wait()

    return pl.pallas_call(
        body, name=name, out_shape=jax.ShapeDtypeStruct((N_DEV * m_per, n), v.dtype),
        in_specs=[pl.BlockSpec(memory_space=pltpu.VMEM)], out_specs=pl.BlockSpec(memory_space=pltpu.VMEM),
        scratch_shapes=[pltpu.SemaphoreType.DMA((7,)), pltpu.SemaphoreType.DMA((7,)), pltpu.SemaphoreType.DMA],
    )(v)


def _big_all_gather(shards, name):
    n_arr = len(shards)

    def body(*refs):
        x_refs, out_refs = refs[:n_arr], refs[n_arr:2 * n_arr]
        send_sems, recv_sems, local_sems = refs[2 * n_arr:]
        x, y, c = _position()
        me, sibling = (x, y, c), (x, y, 1 - c)
        chips = [(1 - x, y), (x, 1 - y), (1 - x, 1 - y)]

        def slot(a, px, py, pc):
            return out_refs[a].at[4 * px + 2 * py + pc]

        def copy(a, k, block, to, src=None):
            return pltpu.make_async_remote_copy(
                src_ref=slot(a, *block) if src is None else src, dst_ref=slot(a, *block),
                send_sem=send_sems.at[a, k], recv_sem=recv_sems.at[a, k], device_id=to, device_id_type=MESH)

        mine = [pltpu.make_async_copy(x_refs[a], slot(a, *me), local_sems.at[a]) for a in range(n_arr)]
        for cp in mine:
            cp.start()
        first = []
        for j, chip in enumerate(chips):
            first += [copy(a, 1 + j, me, (*chip, c), src=x_refs[a]) for a in range(n_arr)]
        first += [copy(a, 0, me, sibling, src=x_refs[a]) for a in range(n_arr)]
        for cp in first:
            cp.start()
        passed = []
        for j, chip in enumerate(chips):
            for a in range(n_arr):
                copy(a, 1 + j, (*chip, c), me).wait_recv()
                fwd = copy(a, 4 + j, (*chip, c), sibling)
                fwd.start()
                passed.append(fwd)
        for a in range(n_arr):
            copy(a, 0, sibling, me).wait_recv()
        for j, chip in enumerate(chips):
            for a in range(n_arr):
                copy(a, 4 + j, (*chip, 1 - c), me).wait_recv()
        for cp in first + passed:
            cp.wait_send()
        for cp in mine:
            cp.wait()

    return pl.pallas_call(
        body, name=name,
        out_shape=tuple(jax.ShapeDtypeStruct((N_DEV,) + s.shape, s.dtype) for s in shards),
        in_specs=[ANY] * n_arr, out_specs=tuple([ANY] * n_arr),
        scratch_shapes=[pltpu.SemaphoreType.DMA((n_arr, 7)), pltpu.SemaphoreType.DMA((n_arr, 7)),
                        pltpu.SemaphoreType.DMA((n_arr,))],
    )(*shards)


def _sibling_exchange(pieces, name):
    n_arr = len(pieces)

    def body(*refs):
        p_refs, out_refs = refs[:n_arr], refs[n_arr:2 * n_arr]
        send_sems, recv_sems = refs[2 * n_arr:]
        x, y, c = _position()
        copies = [pltpu.make_async_remote_copy(
            src_ref=p_refs[a].at[1 - c], dst_ref=out_refs[a], send_sem=send_sems.at[a], recv_sem=recv_sems.at[a],
            device_id=(x, y, 1 - c), device_id_type=MESH) for a in range(n_arr)]
        for cp in copies:
            cp.start()
        for cp in copies:
            cp.wait()

    return pl.pallas_call(
        body, name=name,
        out_shape=tuple(jax.ShapeDtypeStruct(p.shape[1:], p.dtype) for p in pieces),
        in_specs=[ANY] * n_arr, out_specs=tuple([ANY] * n_arr),
        scratch_shapes=[pltpu.SemaphoreType.DMA((n_arr,)), pltpu.SemaphoreType.DMA((n_arr,))],
    )(*pieces)


HBM = pl.BlockSpec(memory_space=pltpu.HBM)
SEM = pl.BlockSpec(memory_space=pltpu.SEMAPHORE)
EFFECT = pltpu.SideEffectType.DATAFLOW_SIDE_EFFECTING
RELATIONS = [(rx, ry, rc) for rx in (0, 1) for ry in (0, 1) for rc in (0, 1)][1:]


SAME_CORE = [r for r in RELATIONS if r == (0, 0, 1) or r[2] == 0]


CHIPS = [r for r in RELATIONS if r[2] == 0]


def _exchange_copies(src_refs, land_refs, send_sems, recv_sems, scatter, receive_side, relations):
    x, y, c = _position()
    index = (lambda px, py, pc: 2 * px + py) if relations == CHIPS else (lambda px, py, pc: 4 * px + 2 * py + pc)
    me = index(x, y, c)
    copies = []
    for k, (rx, ry, rc) in enumerate(relations):
        peer = ((1 - x) if rx else x, (1 - y) if ry else y, (1 - c) if rc else c)
        peer_index = index(*peer)
        for a, (src, land) in enumerate(zip(src_refs, land_refs)):
            copies.append(pltpu.make_async_remote_copy(
                src_ref=src.at[peer_index] if scatter else src,
                dst_ref=land.at[peer_index if receive_side else me],
                send_sem=send_sems.at[a * len(relations) + k], recv_sem=recv_sems.at[a * len(relations) + k],
                device_id=peer, device_id_type=MESH))
    return copies


def _exchange_start(srcs, scatter, after, name, relations=RELATIONS):
    n = len(srcs)
    land_shapes = [(s.shape if scatter else (N_DEV,) + s.shape) for s in srcs]

    def body(*refs):
        src_refs, land_refs = refs[:n], refs[n:2 * n]
        send_sems, recv_sems = refs[2 * n + 1], refs[2 * n + 2]
        token = refs[-1]
        for cp in _exchange_copies(src_refs, land_refs, send_sems, recv_sems, scatter, False, relations):
            cp.start()
        token[...] = jnp.zeros_like(token)

    sems = pltpu.SemaphoreType.DMA((n * len(relations),))
    outs = pl.pallas_call(
        body, name=name,
        out_shape=(sems, sems, *[pltpu.HBM(s.shape, s.dtype) for s in srcs],
                   *[pltpu.HBM(shape, s.dtype) for shape, s in zip(land_shapes, srcs)],
                   jax.ShapeDtypeStruct((8, 128), F32)),
        in_specs=[HBM] * (2 * n) + [ANY],
        out_specs=(SEM, SEM, *[HBM] * (2 * n), pl.BlockSpec(memory_space=pltpu.VMEM)),
        input_output_aliases={a: 2 + a for a in range(2 * n)},
        compiler_params=pltpu.CompilerParams(has_side_effects=EFFECT),
    )(*[pltpu.with_memory_space_constraint(s, pltpu.HBM) for s in srcs],
      *[pltpu.with_memory_space_constraint(lax.empty(shape, s.dtype), pltpu.HBM)
        for shape, s in zip(land_shapes, srcs)], after)
    return outs[0], outs[1], outs[2:2 + n], outs[2 + n:2 + 2 * n], outs[-1]


def _exchange_wait(started, scatter, after, name, relations=RELATIONS):
    send_sems, recv_sems, srcs, lands, _ = started
    n = len(srcs)

    def body(*refs):
        src_refs, land_refs = refs[:n], refs[n:2 * n]
        send_sems, recv_sems = refs[2 * n], refs[2 * n + 1]
        copies = _exchange_copies(src_refs, land_refs, send_sems, recv_sems, scatter, True, relations)
        for cp in copies:
            cp.wait_send()
        for cp in copies:
            cp.wait_recv()

    outs = pl.pallas_call(
        body, name=name,
        out_shape=(*[pltpu.HBM(s.shape, s.dtype) for s in srcs], *[pltpu.HBM(t.shape, t.dtype) for t in lands]),
        in_specs=[HBM] * (2 * n) + [SEM, SEM, ANY], out_specs=tuple([HBM] * (2 * n)),
        input_output_aliases={a: a for a in range(2 * n)},
        compiler_params=pltpu.CompilerParams(has_side_effects=EFFECT),
    )(*srcs, *lands, send_sems, recv_sems, after)
    return outs[:n], outs[n:]


def _forward_copies(land_refs, send_sems, recv_sems, receive_side):
    x, y, c = _position()
    copies = []
    for j, (px, py) in enumerate([(1 - x, y), (x, 1 - y), (1 - x, 1 - y)]):
        held, coming = 4 * px + 2 * py + c, 4 * px + 2 * py + (1 - c)
        for a, land in enumerate(land_refs):
            copies.append(pltpu.make_async_remote_copy(
                src_ref=land.at[held], dst_ref=land.at[coming if receive_side else held],
                send_sem=send_sems.at[3 * a + j], recv_sem=recv_sems.at[3 * a + j],
                device_id=(x, y, 1 - c), device_id_type=MESH))
    return copies


def _forward_start(lands, after, name):
    n = len(lands)

    def body(*refs):
        send_sems, recv_sems, token = refs[n + 1], refs[n + 2], refs[-1]
        for cp in _forward_copies(refs[:n], send_sems, recv_sems, False):
            cp.start()
        token[...] = jnp.zeros_like(token)

    sems = pltpu.SemaphoreType.DMA((3 * n,))
    outs = pl.pallas_call(
        body, name=name,
        out_shape=(sems, sems, *[pltpu.HBM(t.shape, t.dtype) for t in lands], jax.ShapeDtypeStruct((8, 128), F32)),
        in_specs=[HBM] * n + [ANY], out_specs=(SEM, SEM, *[HBM] * n, pl.BlockSpec(memory_space=pltpu.VMEM)),
        input_output_aliases={a: 2 + a for a in range(n)},
        compiler_params=pltpu.CompilerParams(has_side_effects=EFFECT),
    )(*lands, after)
    return outs[0], outs[1], outs[2:2 + n], outs[-1]


def _forward_wait(started, after, name):
    send_sems, recv_sems, lands, _ = started
    n = len(lands)

    def body(*refs):
        copies = _forward_copies(refs[:n], refs[n], refs[n + 1], True)
        for cp in copies:
            cp.wait_send()
        for cp in copies:
            cp.wait_recv()

    return pl.pallas_call(
        body, name=name, out_shape=tuple(pltpu.HBM(t.shape, t.dtype) for t in lands),
        in_specs=[HBM] * n + [SEM, SEM, ANY], out_specs=tuple([HBM] * n),
        input_output_aliases={a: a for a in range(n)},
        compiler_params=pltpu.CompilerParams(has_side_effects=EFFECT),
    )(*lands, send_sems, recv_sems, after)


def _place_own(lands, mine, name):
    n = len(lands)

    def body(*refs):
        mine_refs, out_refs, sems = refs[:n], refs[2 * n:3 * n], refs[3 * n]
        x, y, c = _position()
        copies = [pltpu.make_async_copy(mine_refs[a], out_refs[a].at[4 * x + 2 * y + c], sems.at[a]) for a in range(n)]
        for cp in copies:
            cp.start()
        for cp in copies:
            cp.wait()

    return pl.pallas_call(
        body, name=name, out_shape=tuple(jax.ShapeDtypeStruct(t.shape, t.dtype) for t in lands),
        in_specs=[ANY] * (2 * n), out_specs=tuple([ANY] * n), input_output_aliases={n + a: a for a in range(n)},
        scratch_shapes=[pltpu.SemaphoreType.DMA((n,))],
    )(*mine, *lands)


W_IN_SHARD = N_IN // N_DEV
F_SHARD = F_COL // W_IN_SHARD
F_LO = F_COL - F_SHARD * W_IN_SHARD


def _w_ffn_in_view(w):
    return jnp.transpose(w, (0, 2, 1))


def _w_in_rearranged(g):
    parts = [g[d] for d in range(N_DEV)]
    with_f = parts[F_SHARD]
    parts[F_SHARD:F_SHARD + 1] = [with_f[:, :F_LO], with_f[:, F_LO + N_FORGET:]]
    parts += [with_f[:, F_LO:F_LO + N_FORGET], jnp.zeros((with_f.shape[0], BLK - N_FORGET), with_f.dtype)]
    return jnp.concatenate(parts, axis=1)


def _w_in_pieces(dw_r):
    def original(lo, hi):
        shift = 0 if hi <= F_COL else N_FORGET
        return dw_r[:, lo - shift:hi - shift]

    pieces = []
    for d in range(N_DEV):
        lo, hi = d * W_IN_SHARD, (d + 1) * W_IN_SHARD
        if d == F_SHARD:
            pieces.append(jnp.concatenate([original(lo, F_COL), dw_r[:, N_MAIN:N_MAIN + N_FORGET],
                                           original(F_COL + N_FORGET, hi)], axis=1))
        else:
            pieces.append(original(lo, hi))
    return jnp.stack(pieces)


def _row_pieces(dw):
    return dw.reshape(N_DEV, dw.shape[0] // N_DEV, dw.shape[1])


def _branch_pieces(dw):
    k, w, d = dw.shape
    return jnp.transpose(dw.reshape(k, w, N_DEV, d // N_DEV), (2, 0, 1, 3)).reshape(N_DEV, k * w, d // N_DEV)


def _pair_major(p8):
    return jnp.stack([p8[0::2], p8[1::2]])


def _pairs_col(a):
    return jnp.transpose(a.reshape(a.shape[0], 4, 2), (1, 0, 2))


def _pairs_row(a):
    return jnp.transpose(a.reshape(a.shape[0], 4, 2), (1, 2, 0))


def _heads_from_col(a):
    return jnp.transpose(a, (1, 0, 2)).reshape(a.shape[1], 8)


def _heads_from_row(a):
    return jnp.transpose(a, (2, 0, 1)).reshape(a.shape[2], 8)


def _pad_lanes(a, n):
    return jnp.pad(a, [(0, 0)] * (a.ndim - 1) + [(0, n - a.shape[-1])])


SMALL_SEGMENTS = (("dmod", 2 * 6 * D_MODEL), ("norm_mix_g", 2 * D_MODEL), ("norm_ffn_g", 2 * D_MODEL),
                  ("final_norm_g", D_MODEL), ("b_forget", 128), ("sinks", 128), ("rel_bias", 4224))
SMALL_ROWS = 176


def _pack_small(parts):
    flat = [_pad_lanes(parts[name].reshape(1, -1), size) for name, size in SMALL_SEGMENTS]
    total = sum(size for _, size in SMALL_SEGMENTS)
    flat.append(jnp.zeros((1, SMALL_ROWS * 128 - total), F32))
    return jnp.concatenate(flat, axis=1).reshape(SMALL_ROWS, 128)


def _unpack_small(packed, shapes):
    flat = packed.reshape(-1)
    out, pos = {}, 0
    for name, size in SMALL_SEGMENTS:
        shape = shapes[name]
        count = 1
        for d in shape:
            count *= d
        out[name] = flat[pos:pos + count].reshape(shape)
        pos += size
    return out


def kernel(x, c, norm_mix_g, norm_ffn_g, w_ada, b_ada, w_in, b_forget, sinks, rel_bias, w_branch, w_out, w_ffn_in, w_ffn_out, final_norm_g, loss_target, m_norm_mix_g, m_norm_ffn_g, m_w_ada, m_b_ada, m_w_in, m_b_forget, m_sinks, m_rel_bias, m_w_branch, m_w_out, m_w_ffn_in, m_w_ffn_out, m_final_norm_g, v_norm_mix_g, v_norm_ffn_g, v_w_ada, v_b_ada, v_w_in, v_b_forget, v_sinks, v_rel_bias, v_w_branch, v_w_out, v_w_ffn_in, v_w_ffn_out, v_final_norm_g):
    depth = w_in.shape[0]
    S, D = x.shape[1], x.shape[2]
    assert S % GROUP == 0 and S >= ATTN_WINDOW["c"] * BLK
    px, py, pc = _position()
    me = 4 * px + 2 * py + pc
    x0 = x[0]

    assert depth == 2
    big_weights = (w_in, w_branch, w_out, w_ffn_in, w_ffn_out)
    me_arr = jnp.stack([me, me]).astype(jnp.int32)
    me_in_arr = jnp.stack([2 * px + py, me]).astype(jnp.int32)

    def rest_matrices(g_branch, g_out, g_fin, g_fout):
        return (jnp.transpose(g_branch, (1, 2, 0, 3)).reshape(3, 512, D), g_out.reshape(D, D),
                g_fin.reshape(2 * FFN_HIDDEN, D), g_fout.reshape(FFN_HIDDEN, D))

    def arrive(started, after, name):
        mine, landed = _exchange_wait(started, False, after, f"{name}_wait", SAME_CORE)
        return mine, _forward_start(landed, mine[0], f"{name}_forward_start")

    def finish_gather(arrived, after, name):
        mine, forward = arrived
        landed = _forward_wait(forward, after, f"{name}_forward_wait")
        return _place_own(landed, mine, f"{name}_own")

    w_fin_t = _w_ffn_in_view(w_ffn_in)
    shards = [[t.astype(BF16) for t in (w_in[l], w_branch[l], w_out[l], w_fin_t[l], w_ffn_out[l])]
              for l in range(depth)]
    gathered_in0 = _big_all_gather(shards[0][:1], "comm_gather_w_in0")[0]
    gather_rest0 = _exchange_start(shards[0][1:], False, gathered_in0, "comm_gather_rest0_start", SAME_CORE)
    gather1 = _exchange_start(shards[1], False, gather_rest0[4], "comm_gather_weights1_start", SAME_CORE)
    W_in, W_branch, W_out, W_fin, W_fout = ([None, None] for _ in range(5))
    W_in[0] = _w_in_rearranged(gathered_in0)

    c_all = _small_all_gather(c.reshape(8, 128), "comm_gather_c").reshape(N_DEV, D)
    mod_cols = _ada_fwd(c_all, w_ada, "ada_fwd")
    mod_all = _small_all_gather(mod_cols.reshape(-1, 128), "comm_gather_mod")
    mod_all = mod_all.reshape(N_DEV, depth, N_DEV, w_ada.shape[2])
    mod_mine = lax.dynamic_index_in_dim(mod_all, me, axis=2, keepdims=False)
    mod = jnp.transpose(mod_mine, (1, 0, 2)).reshape(depth, 6 * D) + b_ada + gather1[4][0:1, 0:1]
    mods = [[mod[l:l + 1, k * D:(k + 1) * D] for k in range(6)] for l in range(depth)]

    slopes = jnp.exp2(-jnp.arange(1, 9, dtype=F32))
    saved = []
    xs = x0
    for l in range(depth):
        if l == 1:
            g_in1, *g_rest1 = finish_gather(arrived1, xs, "comm_gather_weights1")
            W_in[1] = _w_in_rearranged(g_in1)
            W_branch[1], W_out[1], W_fin[1], W_fout[1] = rest_matrices(*g_rest1)
        sh_m, sc_m, g_m, sh_f, sc_f, g_f = mods[l]
        gm, gf = norm_mix_g[l:l + 1], norm_ffn_g[l:l + 1]
        bfor = _pad_lanes(b_forget[l:l + 1], BLK)
        h = _norm_mod_fwd(xs, gm, sh_m, sc_m, f"norm_mix_fwd{l}")
        qkv = _matmul(h, W_in[l], "nn", BF16, f"proj_qkv{l}", TILES["proj_qkv"], n=N_QKV)
        gates = _matmul(h, W_in[l], "nn", F32, f"proj_gates{l}", TILES["proj_gates"], n=N_GATES,
                        b_off=N_QKV // TILES["proj_gates"][1])
        fb = _matmul(h, W_in[l], "nn", F32, f"proj_forget{l}", TILES["proj_forget"], n=BLK, b_off=N_MAIN // BLK)
        cum = _forget_fwd(fb, bfor, f"forget_fwd{l}")[:, :N_FORGET]
        cum_col, cum_row = _pairs_col(cum), _pairs_row(cum)
        tiles, tiles_t = _rel_expand(_rel_bases(rel_bias[l]), f"rel_expand{l}")
        o_a, lse_a = _attn_fwd("a", qkv, f"attn_a_fwd{l}", sinks=sinks[l], slopes=slopes)
        o_b, lse_b = _attn_fwd("b", qkv, f"attn_b_fwd{l}", cq_col=cum_col, ck_row=cum_row)
        arrived_rest0 = arrive(gather_rest0, o_b, "comm_gather_rest0") if l == 0 else None
        o_c, lse_c = _attn_fwd("c", qkv, f"attn_c_fwd{l}", bias=tiles, after=arrived_rest0[1][3] if l == 0 else None)
        if l == 0:
            W_branch[0], W_out[0], W_fin[0], W_fout[0] = rest_matrices(
                *finish_gather(arrived_rest0, o_c, "comm_gather_rest0"))
        x1, merged, mix = _merge_fwd(o_a, o_b, o_c, gates, W_branch[l], W_out[l], xs, g_m, f"merge_fwd{l}")
        h2 = _norm_mod_fwd(x1, gf, sh_f, sc_f, f"norm_ffn_fwd{l}")
        act = _ffn_in_fwd(h2, W_fin[l], f"ffn_in_fwd{l}")
        if l == 0:
            arrived1 = arrive(gather1, act, "comm_gather_weights1")
        x2, ffn = _matmul_resid(act, W_fout[l], x1, g_f, f"ffn_out{l}", TILES["ffn_out"],
                                after=arrived1[1][3] if l == 0 else None)
        saved.append(dict(x=xs, h=h, qkv=qkv, gates=gates, fb=fb, bfor=bfor, cum_col=cum_col, cum_row=cum_row,
                          tiles_t=tiles_t, o=(o_a, o_b, o_c), lse=(lse_a, lse_b, lse_c), merged=merged, mix=mix,
                          x1=x1, h2=h2, act=act, ffn=ffn))
        xs = x2

    dx, loss_tile, d_final_g = _final_loss(xs, loss_target[0], final_norm_g.reshape(1, D), "final_loss")
    loss = lax.psum(loss_tile[0, 0], ("x", "y", "c"))

    grads = {k: [None] * depth for k in ("w_in", "w_branch", "w_out", "w_ffn_in", "w_ffn_out", "norm_mix_g",
                                          "norm_ffn_g", "b_forget", "sinks", "rel_bias", "dmod")}
    def rest_pieces(l):
        return [_branch_pieces(grads["w_branch"][l]), _row_pieces(grads["w_out"][l]),
                _row_pieces(grads["w_ffn_in"][l]), _row_pieces(grads["w_ffn_out"][l])]

    reduce1 = reduce_rest0 = reduce_in0 = None
    for l in reversed(range(depth)):
        sv = saved[l]
        sh_m, sc_m, g_m, sh_f, sc_f, g_f = mods[l]
        if l == 0:
            g_f = g_f + reduce1[4][0:1, 0:1]
        gm, gf = norm_mix_g[l:l + 1], norm_ffn_g[l:l + 1]
        df, d_g_f = _gate_bwd(dx, sv["ffn"], g_f, f"ffn_gate_bwd{l}")
        du_g, du_u = _ffn_mid_bwd(sv["h2"], df, W_fin[l], W_fout[l], f"ffn_mid_bwd{l}")
        du = jnp.concatenate([du_g, du_u], axis=1)
        grads["w_ffn_out"][l] = _matmul(sv["act"], df, "tn", BF16, f"wgrad_ffn_out{l}", TILES["wgrad_ffn_out"])
        grads["w_ffn_in"][l] = _matmul(du, sv["h2"], "tn", BF16, f"wgrad_ffn_in{l}", TILES["wgrad_ffn_in"])
        dh2 = _matmul(du, W_fin[l], "nn", F32, f"dgrad_ffn_in{l}", TILES["dgrad_ffn_in"])
        dx1, d_sh_f, d_sc_f, d_gf = _norm_mod_bwd(sv["x1"], dh2, dx, gf, sc_f, f"norm_ffn_bwd{l}")
        dmix, d_g_m = _gate_bwd(dx1, sv["mix"], g_m, f"mix_gate_bwd{l}")
        grads["w_out"][l] = _matmul(sv["merged"], dmix, "tn", BF16, f"wgrad_out{l}", TILES["wgrad_out"])
        o_a, o_b, o_c = sv["o"]
        dgates, d_w_branch, do_a, do_b, do_c, dl_a, dl_b, dl_c = _merge_bwd(
            dmix, o_a, o_b, o_c, sv["gates"], W_branch[l], W_out[l], f"merge_bwd{l}")
        grads["w_branch"][l] = d_w_branch.astype(BF16)
        lse_rows = [_pairs_row(_heads_from_col(t)) for t in sv["lse"]]
        if l == 0:
            reduce_rest0 = _exchange_start(rest_pieces(0), True, dgates, "comm_reduce_rest0_start")
            lse_rows = [t + reduce_rest0[4][0:1, 0:1] for t in lse_rows]
        dqt_a, dk_a, dv_a, dsink = _attn_bwd("a", sv["qkv"], do_a, lse_rows[0], _pairs_row(dl_a), f"attn_a_bwd{l}",
                                             sinks=sinks[l], slopes=slopes)
        dqt_b, dk_b, dv_b, dck, dcq = _attn_bwd("b", sv["qkv"], do_b, lse_rows[1], _pairs_row(dl_b),
                                                f"attn_b_bwd{l}", cq_row=sv["cum_row"], ck_col=sv["cum_col"])
        dqt_c, dk_c, dv_c, dtiles_t = _attn_bwd("c", sv["qkv"], do_c, lse_rows[2], _pairs_row(dl_c),
                                                f"attn_c_bwd{l}", bias_t=sv["tiles_t"])
        grads["sinks"][l] = dsink[:, :2, 0].reshape(8)
        grads["rel_bias"][l] = _rel_reduce(dtiles_t, f"rel_reduce{l}")[:, 0, :N_REL]
        dcum_k = _pad_lanes(_heads_from_col(dck), BLK)
        dcum_q = _pad_lanes(_heads_from_row(dcq), BLK)
        dfb, d_bfor = _forget_bwd(dcum_q, dcum_k, sv["fb"], sv["bfor"], f"forget_bwd{l}")
        grads["b_forget"][l] = d_bfor[0, :N_FORGET]
        dproj = jnp.concatenate(
            [t.astype(BF16) for t in (dqt_a.T, dk_a, dv_a, dqt_b.T, dk_b, dv_b, dqt_c.T, dk_c, dv_c)]
            + [dgates, dfb.astype(BF16)], axis=1)
        grads["w_in"][l] = _matmul(sv["h"], dproj, "tn", BF16, f"wgrad_in{l}", TILES["wgrad_in"])
        dh = _matmul(dproj, W_in[l], "nt", F32, f"dgrad_in{l}", TILES["dgrad_in"])
        dx, d_sh_m, d_sc_m, d_gm = _norm_mod_bwd(sv["x"], dh, dx1, gm, sc_m, f"norm_mix_bwd{l}")
        grads["norm_mix_g"][l] = d_gm[0]
        grads["norm_ffn_g"][l] = d_gf[0]
        grads["dmod"][l] = jnp.concatenate([d_sh_m, d_sc_m, d_g_m, d_sh_f, d_sc_f, d_g_f], axis=1)[0]
        if l == 1:
            reduce1 = _exchange_start([_w_in_pieces(grads["w_in"][1])] + rest_pieces(1), True, dx, "comm_reduce1_start")

    grad_x = dx.reshape(x.shape)

    small_shapes = dict(dmod=b_ada.shape, norm_mix_g=norm_mix_g.shape, norm_ffn_g=norm_ffn_g.shape,
                        final_norm_g=final_norm_g.shape, b_forget=b_forget.shape, sinks=sinks.shape,
                        rel_bias=rel_bias.shape)
    mine_small = _pack_small(dict(
        dmod=jnp.stack(grads["dmod"]), norm_mix_g=jnp.stack(grads["norm_mix_g"]),
        norm_ffn_g=jnp.stack(grads["norm_ffn_g"]), final_norm_g=d_final_g[0],
        b_forget=_pad_lanes(jnp.stack(grads["b_forget"]).reshape(1, -1), 128),
        sinks=_pad_lanes(jnp.stack(grads["sinks"]).reshape(1, -1), 128),
        rel_bias=_pad_lanes(jnp.stack(grads["rel_bias"]).reshape(1, -1), 4224)))
    all_small = _small_all_gather(mine_small, "comm_gather_small").reshape(N_DEV, SMALL_ROWS, 128)
    pieces_in0 = _pair_major(_w_in_pieces(grads["w_in"][0]))
    from_sibling = _sibling_exchange([pieces_in0], "comm_reduce_in0_sibling")[0]
    pair_sum_in0 = _pair_add(pieces_in0, from_sibling, pc.astype(jnp.int32).reshape(1), "pair_add_in0")
    reduce_in0 = _exchange_start([pair_sum_in0], True, all_small, "comm_reduce_in0_start", CHIPS)
    in0_started = reduce_in0[4]

    def pack_params(b_ada_, nm, nf, fn, bf, sk, rb):
        return _pack_small(dict(dmod=b_ada_, norm_mix_g=nm, norm_ffn_g=nf, final_norm_g=fn,
                                b_forget=_pad_lanes(bf.reshape(1, -1), 128), sinks=_pad_lanes(sk.reshape(1, -1), 128),
                                rel_bias=_pad_lanes(rb.reshape(1, -1), 4224)))

    small_out = _adamw(
        pack_params(b_ada, norm_mix_g, norm_ffn_g, final_norm_g, b_forget, sinks, rel_bias)[None],
        pack_params(m_b_ada, m_norm_mix_g, m_norm_ffn_g, m_final_norm_g, m_b_forget, m_sinks, m_rel_bias)[None],
        pack_params(v_b_ada, v_norm_mix_g, v_norm_ffn_g, v_final_norm_g, v_b_forget, v_sinks, v_rel_bias)[None],
        [all_small], "adamw_small", me_arr, after=in0_started)
    small_out = [_unpack_small(t[0], small_shapes) for t in small_out]

    dmod_all = all_small[:, :96].reshape(N_DEV, depth, 6 * D)
    dmod_cols = lax.dynamic_slice_in_dim(dmod_all, me * w_ada.shape[2], w_ada.shape[2], axis=2)
    d_w_ada = _ada_bwd(jnp.transpose(c_all), jnp.transpose(dmod_cols, (1, 0, 2)), "ada_bwd")

    big = {"w_ada": _adamw(w_ada, m_w_ada, v_w_ada, [d_w_ada[l:l + 1] for l in range(depth)], "adamw_w_ada", me_arr,
                           after=in0_started)}
    sent1, landed1 = _exchange_wait(reduce1, True, big["w_ada"][0], "comm_reduce1_wait")
    sent_rest0, landed_rest0 = _exchange_wait(reduce_rest0, True, landed1[0], "comm_reduce_rest0_wait")
    parts = {"w_in": [None, (landed1[0], sent1[0])]}
    for a, name in enumerate(("w_branch", "w_out", "w_ffn_in", "w_ffn_out")):
        parts[name] = [(landed_rest0[a], sent_rest0[a]), (landed1[1 + a], sent1[1 + a])]

    def update(name, w, m, v, view=lambda t: t):
        per_layer = lambda t: t.reshape(depth, -1, t.shape[-1])
        outs = _adamw(*[per_layer(view(t)) for t in (w, m, v)], parts[name], f"adamw_{name}",
                      me_in_arr if name == "w_in" else me_arr)
        big[name] = [view(t).reshape(w.shape) for t in outs]

    update("w_ffn_in", w_ffn_in, m_w_ffn_in, v_w_ffn_in, _w_ffn_in_view)
    update("w_ffn_out", w_ffn_out, m_w_ffn_out, v_w_ffn_out)
    update("w_branch", w_branch, m_w_branch, v_w_branch)
    update("w_out", w_out, m_w_out, v_w_out)
    sent_in0, landed_in0 = _exchange_wait(reduce_in0, True, big["w_out"][0], "comm_reduce_in0_wait", CHIPS)
    parts["w_in"][0] = (landed_in0[0], sent_in0[0])
    update("w_in", w_in, m_w_in, v_w_in)

    def leaf(kind, name):
        if name in big:
            return big[name][kind]
        return small_out[kind]["dmod" if name == "b_ada" else name]

    order = ["norm_mix_g", "norm_ffn_g", "w_ada", "b_ada", "w_in", "b_forget", "sinks", "rel_bias", "w_branch",
             "w_out", "w_ffn_in", "w_ffn_out", "final_norm_g"]
    return (loss, grad_x, *[leaf(0, n) for n in order], *[leaf(1, n) for n in order],
            *[leaf(2, n) for n in order], *[leaf(3, n) for n in order])
```

```python
import functools

import jax
import jax.numpy as jnp
from jax import lax
from jax.experimental import pallas as pl
from jax.experimental.pallas import tpu as pltpu

F32 = jnp.float32
BF16 = jnp.bfloat16
NEG_INF = -1e30
EPS = 1e-6
N_DEV = 8
BLK = 128
GROUP = 4 * BLK
VMEM_LIMIT_BYTES = 56 * 1024 * 1024

D_MODEL = 1024
N_QKV = 3840
N_GATES = 3072
N_MAIN = N_QKV + N_GATES
N_FORGET = 8
N_IN = N_MAIN + N_FORGET
F_COL = 2304
FFN_HIDDEN = 2816
N_REL = 257

ADAM_LR, ADAM_B1, ADAM_B2, ADAM_EPS, ADAM_WD, ADAM_STEP = 0.001, 0.9, 0.999, 1e-08, 0.01, 10

NN = (((1,), (0,)), ((), ()))
NT = (((1,), (1,)), ((), ()))
TN = (((0,), (0,)), ((), ()))
HIGHEST = lax.Precision.HIGHEST

ATTN_COLS = {"a": (0, 4, 5), "b": (6, 10, 14), "c": (18, 22, 26)}
ATTN_WINDOW = {"a": 2, "c": 5}
ATTN_BLOCKS_PER_STEP = {"a": 4, "b": GROUP // BLK, "c": 2}


def _params():
    return pltpu.CompilerParams(vmem_limit_bytes=VMEM_LIMIT_BYTES)


def _tile(n, target):
    best = None
    t = 128
    while t <= min(n, target):
        if n % t == 0:
            best = t
        t += 128
    return best if best is not None else n


def _row_tile(n, target):
    t = min(n, target)
    while n % t:
        t -= 8
    return t


TILES = {
    "proj_qkv": (1024, 1280, 1024), "proj_gates": (1024, 768, 1024), "proj_forget": (1024, 128, 1024),
    "ffn_out": (1024, 512, 2816), "ffn_fused": (512, 1408),
    "wgrad_ffn_out": (1408, 1024, 1024), "wgrad_ffn_in": (1408, 1024, 1024), "dgrad_ffn_in": (1024, 1024, 1408),
    "wgrad_out": (1024, 1024, 1024),
    "wgrad_in": (1024, 1408, 1024), "dgrad_in": (1024, 1024, 1408),
}


def _matmul(a, b, mode, out_dtype, name, tiles, *, n=None, a_off=0, b_off=0, m=None, after=None):
    tm, tn, tk = tiles
    if mode == "nn":
        M, K = a.shape if m is None else (m, a.shape[1])
        N = b.shape[1] if n is None else n
    elif mode == "nt":
        M, K = a.shape
        N = b.shape[0] if n is None else n
    else:
        K = a.shape[0]
        M = a.shape[1] if m is None else m
        N = b.shape[1] if n is None else n
    tm = _tile(M, tm) if M % 128 == 0 else M
    tn = _tile(N, tn)
    tk = _tile(K, tk)
    nk = K // tk
    dims = {"nn": NN, "nt": NT, "tn": TN}[mode]
    if mode == "nn":
        a_spec = pl.BlockSpec((tm, tk), lambda i, j, k: (i + a_off, k))
        b_spec = pl.BlockSpec((tk, tn), lambda i, j, k: (k, j + b_off))
    elif mode == "nt":
        a_spec = pl.BlockSpec((tm, tk), lambda i, j, k: (i + a_off, k))
        b_spec = pl.BlockSpec((tn, tk), lambda i, j, k: (j + b_off, k))
    else:
        a_spec = pl.BlockSpec((tk, tm), lambda i, j, k: (k, i + a_off))
        b_spec = pl.BlockSpec((tk, tn), lambda i, j, k: (k, j + b_off))

    def body(a_ref, b_ref, *rest):
        o_ref, acc_ref = rest[-2:]
        k = pl.program_id(2)
        part = lax.dot_general(a_ref[...], b_ref[...], dims, preferred_element_type=F32)
        if nk == 1:
            o_ref[...] = part.astype(o_ref.dtype)
        else:
            @pl.when(k == 0)
            def _():
                acc_ref[...] = part

            @pl.when(k > 0)
            def _():
                acc_ref[...] += part

            @pl.when(k == nk - 1)
            def _():
                o_ref[...] = acc_ref[...].astype(o_ref.dtype)

    return pl.pallas_call(
        body, name=name,
        out_shape=jax.ShapeDtypeStruct((M, N), out_dtype),
        grid=(M // tm, N // tn, nk),
        in_specs=[a_spec, b_spec] + ([ANY] if after is not None else []),
        out_specs=pl.BlockSpec((tm, tn), lambda i, j, k: (i, j)),
        scratch_shapes=[pltpu.VMEM((tm, tn) if nk > 1 else (8, 128), F32)],
        compiler_params=_params(),
    )(a, b, *([after] if after is not None else []))


def _matmul_resid(a, b, resid, gate, name, tiles, after=None):
    M, K = a.shape
    N = b.shape[1]
    tm, tn, tk = (_tile(d, t) for d, t in zip((M, N, K), tiles))
    nk = K // tk

    def body(a_ref, b_ref, r_ref, g_ref, *rest):
        o_ref, s_ref, acc_ref = rest[-3:]
        k = pl.program_id(2)
        part = jnp.dot(a_ref[...], b_ref[...], preferred_element_type=F32)

        def finish(acc):
            o_ref[...] = r_ref[...] + g_ref[...] * acc
            s_ref[...] = acc.astype(BF16)

        if nk == 1:
            finish(part)
        else:
            @pl.when(k == 0)
            def _():
                acc_ref[...] = part

            @pl.when(k > 0)
            def _():
                acc_ref[...] += part

            @pl.when(k == nk - 1)
            def _():
                finish(acc_ref[...])

    return pl.pallas_call(
        body, name=name,
        out_shape=(jax.ShapeDtypeStruct((M, N), F32), jax.ShapeDtypeStruct((M, N), BF16)),
        grid=(M // tm, N // tn, nk),
        in_specs=[pl.BlockSpec((tm, tk), lambda i, j, k: (i, k)),
                  pl.BlockSpec((tk, tn), lambda i, j, k: (k, j)),
                  pl.BlockSpec((tm, tn), lambda i, j, k: (i, j)),
                  pl.BlockSpec((1, tn), lambda i, j, k: (0, j))] + ([ANY] if after is not None else []),
        out_specs=(pl.BlockSpec((tm, tn), lambda i, j, k: (i, j)),
                   pl.BlockSpec((tm, tn), lambda i, j, k: (i, j))),
        scratch_shapes=[pltpu.VMEM((tm, tn) if nk > 1 else (8, 128), F32)],
        compiler_params=_params(),
    )(a, b, resid, gate, *([after] if after is not None else []))


def _norm_mod_fwd(x, g, shift, scale, name):
    S, D = x.shape
    ts = _row_tile(S, 256)

    def body(x_ref, g_ref, sh_ref, sc_ref, h_ref):
        xv = x_ref[...]
        rstd = lax.rsqrt(jnp.mean(xv * xv, axis=-1, keepdims=True) + EPS)
        y = xv * rstd * g_ref[...]
        h_ref[...] = (y * (1.0 + sc_ref[...]) + sh_ref[...]).astype(BF16)

    row = pl.BlockSpec((1, D), lambda i: (0, 0))
    return pl.pallas_call(
        body, name=name, out_shape=jax.ShapeDtypeStruct((S, D), BF16), grid=(S // ts,),
        in_specs=[pl.BlockSpec((ts, D), lambda i: (i, 0)), row, row, row],
        out_specs=pl.BlockSpec((ts, D), lambda i: (i, 0)),
        compiler_params=_params(),
    )(x, g, shift, scale)


def _norm_mod_bwd(x, dh, dres, g, scale, name):
    S, D = x.shape
    ts = _row_tile(S, 256)

    def body(x_ref, dh_ref, dr_ref, g_ref, sc_ref, dx_ref, dsh_ref, dsc_ref, dg_ref):
        i = pl.program_id(0)
        xv, dhv, gv = x_ref[...], dh_ref[...], g_ref[...]
        rstd = lax.rsqrt(jnp.mean(xv * xv, axis=-1, keepdims=True) + EPS)
        xhat = xv * rstd
        dn = dhv * (1.0 + sc_ref[...])
        dxhat = dn * gv
        proj = jnp.mean(dxhat * xhat, axis=-1, keepdims=True)
        dx_ref[...] = dr_ref[...] + rstd * (dxhat - xhat * proj)
        dsh = jnp.sum(dhv, axis=0, keepdims=True)
        dsc = jnp.sum(dhv * (xhat * gv), axis=0, keepdims=True)
        dg = jnp.sum(dn * xhat, axis=0, keepdims=True)

        @pl.when(i == 0)
        def _():
            dsh_ref[...] = dsh
            dsc_ref[...] = dsc
            dg_ref[...] = dg

        @pl.when(i > 0)
        def _():
            dsh_ref[...] += dsh
            dsc_ref[...] += dsc
            dg_ref[...] += dg

    tile = pl.BlockSpec((ts, D), lambda i: (i, 0))
    row = pl.BlockSpec((1, D), lambda i: (0, 0))
    vec = jax.ShapeDtypeStruct((1, D), F32)
    return pl.pallas_call(
        body, name=name, out_shape=(jax.ShapeDtypeStruct((S, D), F32), vec, vec, vec), grid=(S // ts,),
        in_specs=[tile, tile, tile, row, row], out_specs=(tile, row, row, row),
        compiler_params=_params(),
    )(x, dh, dres, g, scale)


def _gate_bwd(dx, f, gate, name):
    S, D = dx.shape
    ts = _row_tile(S, 256)

    def body(dx_ref, f_ref, g_ref, df_ref, dg_ref):
        i = pl.program_id(0)
        dxv = dx_ref[...]
        df_ref[...] = (dxv * g_ref[...]).astype(BF16)
        dg = jnp.sum(dxv * f_ref[...].astype(F32), axis=0, keepdims=True)

        @pl.when(i == 0)
        def _():
            dg_ref[...] = dg

        @pl.when(i > 0)
        def _():
            dg_ref[...] += dg

    tile = pl.BlockSpec((ts, D), lambda i: (i, 0))
    row = pl.BlockSpec((1, D), lambda i: (0, 0))
    return pl.pallas_call(
        body, name=name,
        out_shape=(jax.ShapeDtypeStruct((S, D), BF16), jax.ShapeDtypeStruct((1, D), F32)), grid=(S // ts,),
        in_specs=[tile, tile, row], out_specs=(tile, row),
        compiler_params=_params(),
    )(dx, f, gate)


def _ffn_in_fwd(h, w_t, name):
    S, D = h.shape
    F = w_t.shape[0] // 2
    tm, tn = _tile(S, TILES["ffn_fused"][0]), _tile(F, TILES["ffn_fused"][1])
    nj = F // tn

    def body(h_ref, wg_ref, wu_ref, o_ref):
        hv = h_ref[...]
        ug = lax.dot_general(hv, wg_ref[...], NT, preferred_element_type=F32)
        uu = lax.dot_general(hv, wu_ref[...], NT, preferred_element_type=F32)
        o_ref[...] = (ug * jax.nn.sigmoid(ug) * uu).astype(BF16)

    return pl.pallas_call(
        body, name=name, out_shape=jax.ShapeDtypeStruct((S, F), BF16), grid=(nj, S // tm),
        in_specs=[pl.BlockSpec((tm, D), lambda j, i: (i, 0)),
                  pl.BlockSpec((tn, D), lambda j, i: (j, 0)),
                  pl.BlockSpec((tn, D), lambda j, i: (j + nj, 0))],
        out_specs=pl.BlockSpec((tm, tn), lambda j, i: (i, j)),
        compiler_params=_params(),
    )(h, w_t, w_t)


def _ffn_mid_bwd(h, df, w_in_t, w_out, name):
    S, D = h.shape
    F = w_in_t.shape[0] // 2
    tm, tn = _tile(S, TILES["ffn_fused"][0]), _tile(F, TILES["ffn_fused"][1])
    nj = F // tn

    def body(h_ref, df_ref, wg_ref, wu_ref, wo_ref, dg_ref, du_ref):
        hv = h_ref[...]
        ug = lax.dot_general(hv, wg_ref[...], NT, preferred_element_type=F32)
        uu = lax.dot_general(hv, wu_ref[...], NT, preferred_element_type=F32)
        dact = lax.dot_general(df_ref[...], wo_ref[...], NT, preferred_element_type=F32)
        sig = jax.nn.sigmoid(ug)
        dg_ref[...] = (dact * uu * (sig * (1.0 + ug * (1.0 - sig)))).astype(BF16)
        du_ref[...] = (dact * (ug * sig)).astype(BF16)

    out = jax.ShapeDtypeStruct((S, F), BF16)
    return pl.pallas_call(
        body, name=name, out_shape=(out, out), grid=(nj, S // tm),
        in_specs=[pl.BlockSpec((tm, D), lambda j, i: (i, 0)),
                  pl.BlockSpec((tm, D), lambda j, i: (i, 0)),
                  pl.BlockSpec((tn, D), lambda j, i: (j, 0)),
                  pl.BlockSpec((tn, D), lambda j, i: (j + nj, 0)),
                  pl.BlockSpec((tn, D), lambda j, i: (j, 0))],
        out_specs=(pl.BlockSpec((tm, tn), lambda j, i: (i, j)), pl.BlockSpec((tm, tn), lambda j, i: (i, j))),
        compiler_params=_params(),
    )(h, df, w_in_t, w_in_t, w_out)


def _merge_fwd(o_a, o_b, o_c, gates, w_branch, w_out, resid, gate, name, *, tm=512):
    S, W = o_a.shape
    D = w_branch.shape[2]
    tm = _row_tile(S, tm)

    def body(oa_ref, ob_ref, oc_ref, g_ref, w_ref, wo_ref, r_ref, gm_ref, x_ref, m_ref, mix_ref):
        acc = None
        for k, o_ref in enumerate((oa_ref, ob_ref, oc_ref)):
            y = jnp.dot(o_ref[...], w_ref[k], preferred_element_type=F32)
            t = jax.nn.sigmoid(g_ref[:, k * D:(k + 1) * D]) * y
            acc = t if acc is None else acc + t
        merged = acc.astype(BF16)
        m_ref[...] = merged
        mix = jnp.dot(merged, wo_ref[...], preferred_element_type=F32)
        x_ref[...] = r_ref[...] + gm_ref[...] * mix
        mix_ref[...] = mix.astype(BF16)

    o_spec = pl.BlockSpec((tm, W), lambda i: (i, 0))
    tile = pl.BlockSpec((tm, D), lambda i: (i, 0))
    return pl.pallas_call(
        body, name=name,
        out_shape=(jax.ShapeDtypeStruct((S, D), F32), jax.ShapeDtypeStruct((S, D), BF16), jax.ShapeDtypeStruct((S, D), BF16)),
        grid=(S // tm,),
        in_specs=[o_spec, o_spec, o_spec, pl.BlockSpec((tm, 3 * D), lambda i: (i, 0)),
                  pl.BlockSpec((3, W, D), lambda i: (0, 0, 0)), pl.BlockSpec((D, D), lambda i: (0, 0)),
                  tile, pl.BlockSpec((1, D), lambda i: (0, 0))],
        out_specs=(tile, tile, tile),
        compiler_params=_params(),
    )(o_a, o_b, o_c, gates, w_branch, w_out, resid, gate)


def _merge_bwd(dmix, o_a, o_b, o_c, gates, w_branch, w_out, name, *, tm=256):
    S, W = o_a.shape
    D = w_branch.shape[2]
    tm = _row_tile(S, tm)
    n_heads = W // 64

    def body(dm_ref, oa_ref, ob_ref, oc_ref, g_ref, w_ref, wo_ref, dg_ref, dw_ref,
             doa_ref, dob_ref, doc_ref, dla_ref, dlb_ref, dlc_ref):
        first = pl.program_id(0) == 0
        dm = lax.dot_general(dm_ref[...], wo_ref[...], NT, preferred_element_type=F32)
        branches = ((oa_ref, doa_ref, dla_ref), (ob_ref, dob_ref, dlb_ref), (oc_ref, doc_ref, dlc_ref))
        for k, (o_ref, do_ref, dl_ref) in enumerate(branches):
            wk = w_ref[k]
            ov = o_ref[...]
            y = jnp.dot(ov, wk, preferred_element_type=F32)
            g = jax.nn.sigmoid(g_ref[:, k * D:(k + 1) * D])
            dy = (dm * g).astype(BF16)
            dwk = lax.dot_general(ov, dy, TN, preferred_element_type=F32)

            @pl.when(first)
            def _(k=k, dwk=dwk):
                dw_ref[k] = dwk

            @pl.when(jnp.logical_not(first))
            def _(k=k, dwk=dwk):
                dw_ref[k] += dwk
            dg_ref[:, k * D:(k + 1) * D] = (dm * y * (g * (1.0 - g))).astype(BF16)
            do16 = lax.dot_general(dy, wk, NT, preferred_element_type=F32).astype(BF16)
            do_ref[...] = do16
            prod = do16.astype(F32) * ov.astype(F32)
            for h in range(n_heads):
                dl_ref[:, h:h + 1] = jnp.sum(prod[:, 64 * h:64 * (h + 1)], axis=1, keepdims=True)

    o_spec = pl.BlockSpec((tm, W), lambda i: (i, 0))
    wide = pl.BlockSpec((tm, 3 * D), lambda i: (i, 0))
    dl_spec = pl.BlockSpec((tm, n_heads), lambda i: (i, 0))
    o_out = jax.ShapeDtypeStruct((S, W), BF16)
    wide_out = jax.ShapeDtypeStruct((S, 3 * D), BF16)
    dl_out = jax.ShapeDtypeStruct((S, n_heads), F32)
    whole = pl.BlockSpec((3, W, D), lambda i: (0, 0, 0))
    return pl.pallas_call(
        body, name=name,
        out_shape=(wide_out, jax.ShapeDtypeStruct((3, W, D), F32), o_out, o_out, o_out, dl_out, dl_out, dl_out),
        grid=(S // tm,),
        in_specs=[pl.BlockSpec((tm, D), lambda i: (i, 0)), o_spec, o_spec, o_spec, wide, whole,
                  pl.BlockSpec((D, D), lambda i: (0, 0))],
        out_specs=(wide, whole, o_spec, o_spec, o_spec, dl_spec, dl_spec, dl_spec),
        compiler_params=_params(),
    )(dmix, o_a, o_b, o_c, gates, w_branch, w_out)


def _band_mask(variant, t_abs, s_abs):
    if variant == "b":
        return s_abs <= t_abs
    qc, kc = t_abs >> 6, s_abs >> 6
    return (kc <= qc) & (kc >= qc - (2 if variant == "a" else 8))


def _attn_fwd(variant, qkv, name, *, sinks=None, slopes=None, cq_col=None, ck_row=None, bias=None, after=None):
    S = qkv.shape[0]
    nb = S // BLK
    qb, kb, vb = ATTN_COLS[variant]
    shared_kv = variant == "a"
    win = ATTN_WINDOW.get(variant)
    per_step = ATTN_BLOCKS_PER_STEP[variant]

    def body(*refs):
        if after is not None:
            refs = refs[:-3] + refs[-2:]
        if variant == "a":
            q_ref, k_ref, v_ref, sink_ref, slope_ref, o_ref, lse_ref = refs
        elif variant == "b":
            q_ref, k_ref, v_ref, cq_ref, ck_ref, o_ref, lse_ref = refs
        else:
            q_ref, k_ref, v_ref, bias_ref, o_ref, lse_ref = refs
        p = pl.program_id(0)
        lane = lax.broadcasted_iota(jnp.int32, (1, BLK), 1)

        def compute(i, rows, start, n_keys):
            n_rows = rows.stop - rows.start
            t_abs = i * BLK + lax.broadcasted_iota(jnp.int32, (n_rows, 1), 0)
            q2 = q_ref[rows, :].astype(F32) * 0.125
            k_w = k_ref[pl.ds(start, n_keys), :]
            v_w = v_ref[pl.ds(start, n_keys), :]
            s_abs = start + lax.broadcasted_iota(jnp.int32, (1, n_keys), 1)
            valid = _band_mask(variant, t_abs, s_abs)
            outs = []
            for half in (0, 1):
                hmask = (lane >= 64) if half else (lane < 64)
                qh = jnp.where(hmask, q2, 0.0)
                if shared_kv:
                    swap = (p // 2) != half
                    qh = jnp.where(swap, pltpu.roll(qh, 64, 1), qh)
                s = lax.dot_general(qh.astype(BF16), k_w, NT, preferred_element_type=F32)
                if variant == "a":
                    head = 2 * p + half
                    s = s + (-slope_ref[head]) * jnp.abs(t_abs - s_abs).astype(F32)
                elif variant == "b":
                    s = s + cq_ref[rows, half:half + 1] - ck_ref[half:half + 1, pl.ds(start, n_keys)]
                else:
                    j0 = start // BLK
                    s = s + jnp.concatenate(
                        [bias_ref[half, jnp.clip(i - j0 - b, 0, 4)] for b in range(win)], axis=1)
                s = jnp.where(valid, s, NEG_INF)
                m = jnp.max(s, axis=1, keepdims=True)
                if variant == "a":
                    m = jnp.maximum(m, sink_ref[head])
                pe = jnp.exp(s - m)
                l = jnp.sum(pe, axis=1, keepdims=True)
                if variant == "a":
                    l = l + jnp.exp(sink_ref[head] - m)
                out = jnp.dot(pe.astype(BF16), v_w, preferred_element_type=F32) / l
                if shared_kv:
                    out = jnp.where(swap, pltpu.roll(out, 64, 1), out)
                outs.append(out)
                lse_ref[rows, half:half + 1] = m + jnp.log(l)
            o_ref[rows, :] = jnp.where(lane < 64, outs[0], outs[1]).astype(BF16)

        step = pl.program_id(1)
        if variant == "b":
            for g in range(S // GROUP):
                pl.when(step == g)(functools.partial(compute, step * per_step, slice(0, GROUP), 0, (g + 1) * GROUP))
        else:
            for sub in range(per_step):
                i = step * per_step + sub
                start = jnp.clip(i - (win - 1), 0, nb - win) * BLK
                compute(i, slice(sub * BLK, (sub + 1) * BLK), pl.multiple_of(start, BLK), win * BLK)

    tq = per_step * BLK
    kv_col = (lambda p, i: (0, kb)) if shared_kv else (lambda p, i: (0, kb + p))
    vv_col = (lambda p, i: (0, vb)) if shared_kv else (lambda p, i: (0, vb + p))
    in_specs = [pl.BlockSpec((tq, BLK), lambda p, i: (i, qb + p)),
                pl.BlockSpec((S, BLK), kv_col), pl.BlockSpec((S, BLK), vv_col)]
    args = [qkv, qkv, qkv]
    if variant == "a":
        in_specs += [pl.BlockSpec(memory_space=pltpu.SMEM), pl.BlockSpec(memory_space=pltpu.SMEM)]
        args += [sinks, slopes]
    elif variant == "b":
        in_specs += [pl.BlockSpec((None, tq, 2), lambda p, i: (p, i, 0)),
                     pl.BlockSpec((None, 2, S), lambda p, i: (p, 0, 0))]
        args += [cq_col, ck_row]
    else:
        in_specs += [pl.BlockSpec((2, 5, BLK, BLK), lambda p, i: (p, 0, 0, 0))]
        args += [bias]
    if after is not None:
        in_specs.append(ANY)
        args.append(after)
    return pl.pallas_call(
        body, name=name,
        out_shape=(jax.ShapeDtypeStruct((S, 512), BF16), jax.ShapeDtypeStruct((4, S, 2), F32)),
        grid=(4, nb // per_step), in_specs=in_specs,
        out_specs=(pl.BlockSpec((tq, BLK), lambda p, i: (i, p)),
                   pl.BlockSpec((None, tq, 2), lambda p, i: (p, i, 0))),
        compiler_params=_params(),
    )(*args)


def _attn_bwd(variant, qkv, do, lse_row, delta_row, name, *, sinks=None, slopes=None, cq_row=None,
              ck_col=None, bias_t=None):
    S = qkv.shape[0]
    nb = S // BLK
    qb, kb, vb = ATTN_COLS[variant]
    shared_kv = variant == "a"
    win = ATTN_WINDOW.get(variant)
    per_step = ATTN_BLOCKS_PER_STEP[variant]

    def body(*refs):
        if variant == "a":
            (q_ref, k_ref, v_ref, do_ref, lse_ref, dl_ref, sink_ref, slope_ref,
             dq_ref, dk_ref, dv_ref, ex_ref) = refs
        elif variant == "b":
            (q_ref, k_ref, v_ref, do_ref, lse_ref, dl_ref, cq_ref, ck_ref,
             dq_ref, dk_ref, dv_ref, ex_ref, dcq_ref) = refs
        else:
            (q_ref, k_ref, v_ref, do_ref, lse_ref, dl_ref, bias_ref,
             dq_ref, dk_ref, dv_ref, ex_ref) = refs
        p = pl.program_id(0)
        lane = lax.broadcasted_iota(jnp.int32, (1, BLK), 1)
        hmasks = [(lane < 64), (lane >= 64)]
        swaps = [(p // 2) != half for half in (0, 1)] if shared_kv else None

        @pl.when(pl.program_id(1) == 0)
        def _():
            dq_ref[...] = jnp.zeros_like(dq_ref)
            if variant == "b":
                dcq_ref[...] = jnp.zeros_like(dcq_ref)
            else:
                ex_ref[...] = jnp.zeros_like(ex_ref)

        def to_kv_lanes(x, h):
            x = jnp.where(hmasks[h], x, 0.0)
            if shared_kv:
                x = jnp.where(swaps[h], pltpu.roll(x, 64, 1), x)
            return x

        def compute(j, rows, start, n_q):
            n_rows = rows.stop - rows.start
            s_abs = j * BLK + lax.broadcasted_iota(jnp.int32, (n_rows, 1), 0)
            off_k = pl.multiple_of(j * BLK, BLK)
            k2 = k_ref[rows, :].astype(F32)
            v2 = v_ref[rows, :].astype(F32)
            if shared_kv:
                kv_lane = (lane >> 6) == (p // 2)
                k_src, v_src = jnp.where(kv_lane, k2, 0.0), jnp.where(kv_lane, v2, 0.0)
                k_al = [jnp.where(swaps[h], pltpu.roll(k_src, 64, 1), k_src) for h in (0, 1)]
                v_al = [jnp.where(swaps[h], pltpu.roll(v_src, 64, 1), v_src) for h in (0, 1)]
            else:
                k_al = [jnp.where(hmasks[h], k2, 0.0) for h in (0, 1)]
                v_al = [jnp.where(hmasks[h], v2, 0.0) for h in (0, 1)]
            k_al = [(t * 0.125).astype(BF16) for t in k_al]
            v_al = [t.astype(BF16) for t in v_al]
            q_w = q_ref[pl.ds(start, n_q), :]
            do_w = do_ref[pl.ds(start, n_q), :]
            t_abs = start + lax.broadcasted_iota(jnp.int32, (1, n_q), 1)
            valid = _band_mask(variant, t_abs, s_abs)
            dk_acc = dv_acc = None
            ds_both = []
            for half in (0, 1):
                s = lax.dot_general(k_al[half], q_w, NT, preferred_element_type=F32)
                if variant == "a":
                    s = s + (-slope_ref[2 * p + half]) * jnp.abs(t_abs - s_abs).astype(F32)
                elif variant == "b":
                    s = s + cq_ref[half:half + 1, pl.ds(start, n_q)] - ck_ref[rows, half:half + 1]
                else:
                    i0 = start // BLK
                    s = s + jnp.concatenate(
                        [bias_ref[half, jnp.clip(i0 + b - j, 0, 4)] for b in range(win)], axis=1)
                pr = jnp.where(valid, jnp.exp(s - lse_ref[half:half + 1, pl.ds(start, n_q)]), 0.0)
                dp = lax.dot_general(v_al[half], do_w, NT, preferred_element_type=F32)
                ds = pr * (dp - dl_ref[half:half + 1, pl.ds(start, n_q)])
                ds16 = ds.astype(BF16)
                dv_h = to_kv_lanes(jnp.dot(pr.astype(BF16), do_w, preferred_element_type=F32), half)
                dk_h = to_kv_lanes(jnp.dot(ds16, q_w, preferred_element_type=F32) * 0.125, half)
                dv_acc = dv_h if dv_acc is None else dv_acc + dv_h
                dk_acc = dk_h if dk_acc is None else dk_acc + dk_h
                ds_both.append(ds16)
                if variant == "b":
                    ex_ref[rows, half:half + 1] = -jnp.sum(ds, axis=1, keepdims=True)
                    dcq_ref[half:half + 1, pl.ds(start, n_q)] += jnp.sum(ds, axis=0, keepdims=True)
                elif variant == "c":
                    for b in range(win):
                        ex_ref[half, jnp.clip(i0 + b - j, 0, 4)] += ds[:, b * BLK:(b + 1) * BLK]
            dq_t = lax.dot_general(jnp.concatenate(k_al, axis=0), jnp.concatenate(ds_both, axis=0), TN,
                                   preferred_element_type=F32)
            dq_ref[:, pl.ds(start, n_q)] += dq_t
            if shared_kv:
                @pl.when(p == 0)
                def _():
                    dk_ref[pl.ds(off_k, n_rows), :] = dk_acc
                    dv_ref[pl.ds(off_k, n_rows), :] = dv_acc

                @pl.when(p > 0)
                def _():
                    dk_ref[pl.ds(off_k, n_rows), :] += dk_acc
                    dv_ref[pl.ds(off_k, n_rows), :] += dv_acc
            else:
                dk_ref[pl.ds(off_k, n_rows), :] = dk_acc
                dv_ref[pl.ds(off_k, n_rows), :] = dv_acc
            if variant == "a":
                for half in (0, 1):
                    p_sink = jnp.exp(sink_ref[2 * p + half] - lse_ref[half:half + 1, pl.ds(off_k, n_rows)])
                    term = p_sink * dl_ref[half:half + 1, pl.ds(off_k, n_rows)]
                    ex_ref[half:half + 1, :] += -jnp.sum(term, axis=1, keepdims=True)

        step = pl.program_id(1)
        if variant == "b":
            for g in range(S // GROUP):
                pl.when(step == g)(functools.partial(compute, step * per_step, slice(0, GROUP), g * GROUP, S - g * GROUP))
        else:
            for sub in range(per_step):
                j = step * per_step + sub
                start = jnp.clip(j, 0, nb - win) * BLK
                compute(j, slice(sub * BLK, (sub + 1) * BLK), pl.multiple_of(start, BLK), win * BLK)

    tk = per_step * BLK
    col = lambda c0: (lambda p, j: (0, c0 + p))
    kv_blk = (lambda c0: (lambda p, j: (j, c0))) if shared_kv else (lambda c0: (lambda p, j: (j, c0 + p)))
    pair = lambda p, j: (0, p)
    row_stat = pl.BlockSpec((None, 2, S), lambda p, j: (p, 0, 0))
    in_specs = [pl.BlockSpec((S, BLK), col(qb)),
                pl.BlockSpec((tk, BLK), kv_blk(kb)), pl.BlockSpec((tk, BLK), kv_blk(vb)),
                pl.BlockSpec((S, BLK), pair), row_stat, row_stat]
    args = [qkv, qkv, qkv, do, lse_row, delta_row]
    kv_width = BLK if shared_kv else 512
    kv_out = pl.BlockSpec((S, BLK), (lambda p, j: (0, 0)) if shared_kv else pair)
    out_shape = [jax.ShapeDtypeStruct((512, S), F32), jax.ShapeDtypeStruct((S, kv_width), F32),
                 jax.ShapeDtypeStruct((S, kv_width), F32)]
    out_specs = [pl.BlockSpec((BLK, S), lambda p, j: (p, 0)), kv_out, kv_out]
    if variant == "a":
        in_specs += [pl.BlockSpec(memory_space=pltpu.SMEM), pl.BlockSpec(memory_space=pltpu.SMEM)]
        args += [sinks, slopes]
        out_shape.append(jax.ShapeDtypeStruct((4, 8, BLK), F32))
        out_specs.append(pl.BlockSpec((None, 8, BLK), lambda p, j: (p, 0, 0)))
    elif variant == "b":
        in_specs += [row_stat, pl.BlockSpec((None, tk, 2), lambda p, j: (p, j, 0))]
        args += [cq_row, ck_col]
        out_shape += [jax.ShapeDtypeStruct((4, S, 2), F32), jax.ShapeDtypeStruct((4, 2, S), F32)]
        out_specs += [pl.BlockSpec((None, tk, 2), lambda p, j: (p, j, 0)), row_stat]
    else:
        in_specs += [pl.BlockSpec((2, 5, BLK, BLK), lambda p, j: (p, 0, 0, 0))]
        args += [bias_t]
        out_shape.append(jax.ShapeDtypeStruct((8, 5, BLK, BLK), F32))
        out_specs.append(pl.BlockSpec((2, 5, BLK, BLK), lambda p, j: (p, 0, 0, 0)))
    return pl.pallas_call(
        body, name=name, out_shape=tuple(out_shape), grid=(4, nb // per_step),
        in_specs=in_specs, out_specs=tuple(out_specs),
        compiler_params=_params(),
    )(*args)


def _log_sigmoid(x):
    return jnp.minimum(x, 0.0) - jnp.log(1.0 + jnp.exp(-jnp.abs(x)))


def _forget_fwd(fb, b_forget, name):
    S = fb.shape[0]
    nb = S // BLK

    def body(fb_ref, b_ref, cum_ref, carry_ref):
        i = pl.program_id(0)
        logf = _log_sigmoid(fb_ref[...] + b_ref[...])
        r = lax.broadcasted_iota(jnp.int32, (BLK, BLK), 0)
        c = lax.broadcasted_iota(jnp.int32, (BLK, BLK), 1)
        tri = (c <= r).astype(F32)

        @pl.when(i == 0)
        def _():
            carry_ref[...] = jnp.zeros_like(carry_ref)

        cum = jnp.dot(tri, logf, preferred_element_type=F32, precision=HIGHEST) + carry_ref[0:1, :]
        cum_ref[...] = cum
        carry_ref[...] = jnp.broadcast_to(cum[BLK - 1:BLK, :], carry_ref.shape)

    return pl.pallas_call(
        body, name=name, out_shape=jax.ShapeDtypeStruct((S, BLK), F32), grid=(nb,),
        in_specs=[pl.BlockSpec((BLK, BLK), lambda i: (i, 0)), pl.BlockSpec((1, BLK), lambda i: (0, 0))],
        out_specs=pl.BlockSpec((BLK, BLK), lambda i: (i, 0)),
        scratch_shapes=[pltpu.VMEM((8, BLK), F32)],
        compiler_params=_params(),
    )(fb, b_forget)


def _forget_bwd(dcum_q, dcum_k, fb, b_forget, name):
    S = fb.shape[0]
    nb = S // BLK

    def body(dq_ref, dk_ref, fb_ref, b_ref, dfb_ref, db_ref, carry_ref):
        g = pl.program_id(0)
        r = lax.broadcasted_iota(jnp.int32, (BLK, BLK), 0)
        c = lax.broadcasted_iota(jnp.int32, (BLK, BLK), 1)
        tri = (c >= r).astype(F32)

        @pl.when(g == 0)
        def _():
            carry_ref[...] = jnp.zeros_like(carry_ref)

        dcum = dq_ref[...] + dk_ref[...]
        dlogf = jnp.dot(tri, dcum, preferred_element_type=F32, precision=HIGHEST) + carry_ref[0:1, :]
        carry_ref[...] = jnp.broadcast_to(dlogf[0:1, :], carry_ref.shape)
        x = fb_ref[...] + b_ref[...]
        dfb = jnp.where(c < N_FORGET, dlogf * jax.nn.sigmoid(-x), 0.0)
        dfb_ref[...] = dfb
        db = jnp.sum(dfb, axis=0, keepdims=True)

        @pl.when(g == 0)
        def _():
            db_ref[...] = db

        @pl.when(g > 0)
        def _():
            db_ref[...] += db

    rev = pl.BlockSpec((BLK, BLK), lambda g: (nb - 1 - g, 0))
    row = pl.BlockSpec((1, BLK), lambda g: (0, 0))
    return pl.pallas_call(
        body, name=name,
        out_shape=(jax.ShapeDtypeStruct((S, BLK), F32), jax.ShapeDtypeStruct((1, BLK), F32)), grid=(nb,),
        in_specs=[rev, rev, rev, row], out_specs=(rev, row),
        scratch_shapes=[pltpu.VMEM((8, BLK), F32)],
        compiler_params=_params(),
    )(dcum_q, dcum_k, fb, b_forget)


def _skew(x, sign):
    row = lax.broadcasted_iota(jnp.int32, x.shape, 0)
    for b in range(7):
        amount = (1 << b) if sign > 0 else 256 - (1 << b)
        x = jnp.where(((row >> b) & 1) == 1, pltpu.roll(x, amount, 1), x)
    return x


def _rel_bases(rel):
    far = rel[:, 256:257]
    far127 = jnp.broadcast_to(far, (rel.shape[0], 127))
    base0 = jnp.concatenate([rel[:, 128:0:-1], far, rel[:, 255:128:-1]], axis=1)
    base1 = jnp.concatenate([rel[:, 256:128:-1], far, far127], axis=1)
    base0_t = jnp.concatenate([rel[:, 128:256], far, rel[:, 1:128]], axis=1)
    base1_t = jnp.concatenate([jnp.broadcast_to(far, (rel.shape[0], 128)), far, rel[:, 129:256]], axis=1)
    return jnp.stack([base0, base1, base0_t, base1_t], axis=1)


def _rel_expand(bases, name):
    def body(b_ref, t_ref, tt_ref):
        far = jnp.broadcast_to(b_ref[1:2, 0:1], (BLK, BLK))
        for k, out_ref in ((0, t_ref), (2, tt_ref)):
            for d in (0, 1):
                x = jnp.broadcast_to(b_ref[k + d:k + d + 1, :], (BLK, 2 * BLK))
                out_ref[d] = _skew(x, 1)[:, :BLK]
            for d in (2, 3, 4):
                out_ref[d] = far

    out = jax.ShapeDtypeStruct((8, 5, BLK, BLK), F32)
    spec = pl.BlockSpec((None, 5, BLK, BLK), lambda h: (h, 0, 0, 0))
    return pl.pallas_call(
        body, name=name, out_shape=(out, out), grid=(8,),
        in_specs=[pl.BlockSpec((None, 4, 2 * BLK), lambda h: (h, 0, 0))], out_specs=(spec, spec),
        compiler_params=_params(),
    )(bases)


def _rel_reduce(dtiles_t, name):
    def body(dt_ref, o_ref):
        zeros = jnp.zeros((BLK, BLK), F32)
        sums = []
        for d in (0, 1):
            x = _skew(jnp.concatenate([dt_ref[d], zeros], axis=1), -1)
            sums.append(jnp.broadcast_to(jnp.sum(x, axis=0, keepdims=True), (8, 2 * BLK)))
        lane = lax.broadcasted_iota(jnp.int32, (8, 2 * BLK), 1)
        main = pltpu.roll(sums[0], BLK, 1) + jnp.where(lane > BLK, sums[1], 0.0)
        far = jnp.sum(jnp.where(lane < BLK, sums[1], 0.0)[0:1], axis=1, keepdims=True)
        far = far + jnp.sum(jnp.sum(dt_ref[2] + dt_ref[3] + dt_ref[4], axis=0, keepdims=True), axis=1, keepdims=True)
        o_ref[...] = jnp.concatenate([main[0:1], jnp.broadcast_to(far, (1, BLK))], axis=1)

    return pl.pallas_call(
        body, name=name, out_shape=jax.ShapeDtypeStruct((8, 1, 3 * BLK), F32), grid=(8,),
        in_specs=[pl.BlockSpec((None, 5, BLK, BLK), lambda h: (h, 0, 0, 0))],
        out_specs=pl.BlockSpec((None, 1, 3 * BLK), lambda h: (h, 0, 0)),
        compiler_params=_params(),
    )(dtiles_t)


def _final_loss(x, target, g, name):
    S, D = x.shape
    ts = _row_tile(S, 256)

    def body(x_ref, t_ref, g_ref, dx_ref, loss_ref, dg_ref):
        i = pl.program_id(0)
        xv, gv = x_ref[...], g_ref[...]
        rstd = lax.rsqrt(jnp.mean(xv * xv, axis=-1, keepdims=True) + EPS)
        xhat = xv * rstd
        err = xhat * gv - t_ref[...]
        part = 0.5 * jnp.sum(jnp.mean(err * err, axis=-1, keepdims=True), axis=0, keepdims=True)
        dy = err / D
        dg = jnp.sum(dy * xhat, axis=0, keepdims=True)
        dxhat = dy * gv
        proj = jnp.mean(dxhat * xhat, axis=-1, keepdims=True)
        dx_ref[...] = rstd * (dxhat - xhat * proj)

        @pl.when(i == 0)
        def _():
            loss_ref[...] = jnp.broadcast_to(part, loss_ref.shape)
            dg_ref[...] = dg

        @pl.when(i > 0)
        def _():
            loss_ref[...] += jnp.broadcast_to(part, loss_ref.shape)
            dg_ref[...] += dg

    tile = pl.BlockSpec((ts, D), lambda i: (i, 0))
    row = pl.BlockSpec((1, D), lambda i: (0, 0))
    return pl.pallas_call(
        body, name=name,
        out_shape=(jax.ShapeDtypeStruct((S, D), F32), jax.ShapeDtypeStruct((8, 128), F32),
                   jax.ShapeDtypeStruct((1, D), F32)),
        grid=(S // ts,), in_specs=[tile, tile, row],
        out_specs=(tile, pl.BlockSpec((8, 128), lambda i: (0, 0)), row),
        compiler_params=_params(),
    )(x, target, g)


def _ada_fwd(c_all, w_ada, name):
    L, D, E = w_ada.shape

    def body(c_ref, w_ref, o_ref):
        cv = c_ref[...]
        cond = cv * jax.nn.sigmoid(cv)
        o_ref[...] = jnp.dot(cond, w_ref[...], preferred_element_type=F32, precision=HIGHEST)

    return pl.pallas_call(
        body, name=name, out_shape=jax.ShapeDtypeStruct((L, N_DEV, E), F32), grid=(L,),
        in_specs=[pl.BlockSpec((N_DEV, D), lambda l: (0, 0)), pl.BlockSpec((None, D, E), lambda l: (l, 0, 0))],
        out_specs=pl.BlockSpec((None, N_DEV, E), lambda l: (l, 0, 0)),
        compiler_params=_params(),
    )(c_all, w_ada)


def _ada_bwd(c_all_t, dmod, name):
    D = c_all_t.shape[0]
    L, _, E = dmod.shape

    def body(c_ref, d_ref, o_ref):
        cv = c_ref[...]
        cond = cv * jax.nn.sigmoid(cv)
        acc = None
        for b in range(N_DEV):
            t = cond[:, b:b + 1] * d_ref[b:b + 1, :]
            acc = t if acc is None else acc + t
        o_ref[...] = acc

    return pl.pallas_call(
        body, name=name, out_shape=jax.ShapeDtypeStruct((L, D, E), F32), grid=(L,),
        in_specs=[pl.BlockSpec((D, N_DEV), lambda l: (0, 0)), pl.BlockSpec((None, N_DEV, E), lambda l: (l, 0, 0))],
        out_specs=pl.BlockSpec((None, D, E), lambda l: (l, 0, 0)),
        compiler_params=_params(),
    )(c_all_t, dmod)


def _adamw(w, m, v, g_parts, name, me, after=None):
    L, R, C = w.shape
    tr = _row_tile(R, max(8, (256 * 1024 // max(C, 128)) // 8 * 8))
    nr = R // tr
    c1 = 1.0 - ADAM_B1 ** ADAM_STEP
    c2 = 1.0 - ADAM_B2 ** ADAM_STEP
    direct = [isinstance(p, tuple) for p in g_parts]
    n_in = sum(2 if d else 1 for d in direct)

    def body(me_ref, w_ref, m_ref, v_ref, *rest):
        g_refs, (go_ref, d_ref, mo_ref, vo_ref) = list(rest[:n_in]), rest[-4:]
        layer = pl.program_id(0)
        g = None
        for l in range(L):
            land_ref = g_refs.pop(0)
            own = g_refs.pop(0)[...].astype(F32) if direct[l] else None
            gl = None
            for k in range(land_ref.shape[0]):
                part = land_ref[k].astype(F32)
                if direct[l]:
                    part = jnp.where(me_ref[l] == k, own, part)
                gl = part if gl is None else gl + part
            g = gl if g is None else jnp.where(layer == l, gl, g)
        mn = ADAM_B1 * m_ref[...] + (1.0 - ADAM_B1) * g
        vn = ADAM_B2 * v_ref[...] + (1.0 - ADAM_B2) * (g * g)
        m_hat = mn / c1
        v_hat = vn / c2
        go_ref[...] = g
        d_ref[...] = -ADAM_LR * (m_hat / (jnp.sqrt(v_hat) + ADAM_EPS) + ADAM_WD * w_ref[...])
        mo_ref[...] = mn
        vo_ref[...] = vn

    def rows(l, layer, i):
        return jnp.where(layer == l, i, 0 if l > 0 else nr - 1)

    in_specs, operands = [], []
    for l, p in enumerate(g_parts):
        land, sent = p if direct[l] else (p, None)
        in_specs.append(pl.BlockSpec((land.shape[0], tr, C), lambda layer, i, me_ref, l=l: (0, rows(l, layer, i), 0)))
        operands.append(land)
        if direct[l]:
            in_specs.append(pl.BlockSpec((None, tr, C), lambda layer, i, me_ref, l=l: (me_ref[l], rows(l, layer, i), 0)))
            operands.append(sent)
    if after is not None:
        in_specs.append(ANY)
        operands.append(after)
    tile = pl.BlockSpec((None, tr, C), lambda layer, i, me_ref: (layer, i, 0))
    out = jax.ShapeDtypeStruct((L, R, C), F32)
    return pl.pallas_call(
        body, name=name, out_shape=(out, out, out, out),
        grid_spec=pltpu.PrefetchScalarGridSpec(
            num_scalar_prefetch=1, grid=(L, nr), in_specs=[tile, tile, tile] + in_specs,
            out_specs=(tile, tile, tile, tile)),
        compiler_params=_params(),
    )(me, w, m, v, *operands)


def _pair_add(pieces, recv, core, name):
    _, _, R, C = pieces.shape
    tr = _row_tile(R, max(8, (512 * 1024 // max(C, 128)) // 8 * 8))

    def body(core_ref, a_ref, b_ref, o_ref):
        o_ref[...] = (a_ref[...].astype(F32) + b_ref[...].astype(F32)).astype(BF16)

    return pl.pallas_call(
        body, name=name, out_shape=jax.ShapeDtypeStruct((4, R, C), BF16),
        grid_spec=pltpu.PrefetchScalarGridSpec(
            num_scalar_prefetch=1, grid=(4, R // tr),
            in_specs=[pl.BlockSpec((None, None, tr, C), lambda k, i, core_ref: (core_ref[0], k, i, 0)),
                      pl.BlockSpec((None, tr, C), lambda k, i, core_ref: (k, i, 0))],
            out_specs=pl.BlockSpec((None, tr, C), lambda k, i, core_ref: (k, i, 0))),
        compiler_params=_params(),
    )(core, pieces, recv)


MESH = pl.DeviceIdType.MESH
ANY = pl.BlockSpec(memory_space=pl.ANY)


def _position():
    return lax.axis_index("x"), lax.axis_index("y"), lax.axis_index("c")


def _small_all_gather(v, name):
    m_per, n = v.shape

    def body(x_ref, out_ref, send_sems, recv_sems, local_sem):
        x, y, c = _position()
        me, sibling = (x, y, c), (x, y, 1 - c)
        chips = [(1 - x, y), (x, 1 - y), (1 - x, 1 - y)]

        def rows(px, py, pc):
            return out_ref.at[pl.ds((4 * px + 2 * py + pc) * m_per, m_per), :]

        def copy(k, block, to, src=None):
            return pltpu.make_async_remote_copy(
                src_ref=rows(*block) if src is None else src, dst_ref=rows(*block),
                send_sem=send_sems.at[k], recv_sem=recv_sems.at[k], device_id=to, device_id_type=MESH)

        mine = pltpu.make_async_copy(x_ref, rows(*me), local_sem)
        mine.start()
        first = [copy(0, me, sibling, src=x_ref)]
        first += [copy(1 + j, me, (*chip, c), src=x_ref) for j, chip in enumerate(chips)]
        for cp in first:
            cp.start()
        passed = [copy(4 + j, (*chip, c), sibling) for j, chip in enumerate(chips)]
        for j, chip in enumerate(chips):
            copy(1 + j, (*chip, c), me).wait_recv()
            passed[j].start()
        copy(0, sibling, me).wait_recv()
        for j, chip in enumerate(chips):
            copy(4 + j, (*chip, 1 - c), me).wait_recv()
        for cp in first + passed:
            cp.wait_send()
        mine.wait()

    return pl.pallas_call(
        body, name=name, out_shape=jax.ShapeDtypeStruct((N_DEV * m_per, n), v.dtype),
        in_specs=[pl.BlockSpec(memory_space=pltpu.VMEM)], out_specs=pl.BlockSpec(memory_space=pltpu.VMEM),
        scratch_shapes=[pltpu.SemaphoreType.DMA((7,)), pltpu.SemaphoreType.DMA((7,)), pltpu.SemaphoreType.DMA],
    )(v)


def _big_all_gather(shards, name):
    n_arr = len(shards)

    def body(*refs):
        x_refs, out_refs = refs[:n_arr], refs[n_arr:2 * n_arr]
        send_sems, recv_sems, local_sems = refs[2 * n_arr:]
        x, y, c = _position()
        me, sibling = (x, y, c), (x, y, 1 - c)
        chips = [(1 - x, y), (x, 1 - y), (1 - x, 1 - y)]

        def slot(a, px, py, pc):
            return out_refs[a].at[4 * px + 2 * py + pc]

        def copy(a, k, block, to, src=None):
            return pltpu.make_async_remote_copy(
                src_ref=slot(a, *block) if src is None else src, dst_ref=slot(a, *block),
                send_sem=send_sems.at[a, k], recv_sem=recv_sems.at[a, k], device_id=to, device_id_type=MESH)

        mine = [pltpu.make_async_copy(x_refs[a], slot(a, *me), local_sems.at[a]) for a in range(n_arr)]
        for cp in mine:
            cp.start()
        first = []
        for j, chip in enumerate(chips):
            first += [copy(a, 1 + j, me, (*chip, c), src=x_refs[a]) for a in range(n_arr)]
        first += [copy(a, 0, me, sibling, src=x_refs[a]) for a in range(n_arr)]
        for cp in first:
            cp.start()
        passed = []
        for j, chip in enumerate(chips):
            for a in range(n_arr):
                copy(a, 1 + j, (*chip, c), me).wait_recv()
                fwd = copy(a, 4 + j, (*chip, c), sibling)
                fwd.start()
                passed.append(fwd)
        for a in range(n_arr):
            copy(a, 0, sibling, me).wait_recv()
        for j, chip in enumerate(chips):
            for a in range(n_arr):
                copy(a, 4 + j, (*chip, 1 - c), me).wait_recv()
        for cp in first + passed:
            cp.wait_send()
        for cp in mine:
            cp.wait()

    return pl.pallas_call(
        body, name=name,
        out_shape=tuple(jax.ShapeDtypeStruct((N_DEV,) + s.shape, s.dtype) for s in shards),
        in_specs=[ANY] * n_arr, out_specs=tuple([ANY] * n_arr),
        scratch_shapes=[pltpu.SemaphoreType.DMA((n_arr, 7)), pltpu.SemaphoreType.DMA((n_arr, 7)),
                        pltpu.SemaphoreType.DMA((n_arr,))],
    )(*shards)


def _sibling_exchange(pieces, name):
    n_arr = len(pieces)

    def body(*refs):
        p_refs, out_refs = refs[:n_arr], refs[n_arr:2 * n_arr]
        send_sems, recv_sems = refs[2 * n_arr:]
        x, y, c = _position()
        copies = [pltpu.make_async_remote_copy(
            src_ref=p_refs[a].at[1 - c], dst_ref=out_refs[a], send_sem=send_sems.at[a], recv_sem=recv_sems.at[a],
            device_id=(x, y, 1 - c), device_id_type=MESH) for a in range(n_arr)]
        for cp in copies:
            cp.start()
        for cp in copies:
            cp.wait()

    return pl.pallas_call(
        body, name=name,
        out_shape=tuple(jax.ShapeDtypeStruct(p.shape[1:], p.dtype) for p in pieces),
        in_specs=[ANY] * n_arr, out_specs=tuple([ANY] * n_arr),
        scratch_shapes=[pltpu.SemaphoreType.DMA((n_arr,)), pltpu.SemaphoreType.DMA((n_arr,))],
    )(*pieces)


HBM = pl.BlockSpec(memory_space=pltpu.HBM)
SEM = pl.BlockSpec(memory_space=pltpu.SEMAPHORE)
EFFECT = pltpu.SideEffectType.DATAFLOW_SIDE_EFFECTING
RELATIONS = [(rx, ry, rc) for rx in (0, 1) for ry in (0, 1) for rc in (0, 1)][1:]


SAME_CORE = [r for r in RELATIONS if r == (0, 0, 1) or r[2] == 0]


CHIPS = [r for r in RELATIONS if r[2] == 0]


def _exchange_copies(src_refs, land_refs, send_sems, recv_sems, scatter, receive_side, relations):
    x, y, c = _position()
    index = (lambda px, py, pc: 2 * px + py) if relations == CHIPS else (lambda px, py, pc: 4 * px + 2 * py + pc)
    me = index(x, y, c)
    copies = []
    for k, (rx, ry, rc) in enumerate(relations):
        peer = ((1 - x) if rx else x, (1 - y) if ry else y, (1 - c) if rc else c)
        peer_index = index(*peer)
        for a, (src, land) in enumerate(zip(src_refs, land_refs)):
            copies.append(pltpu.make_async_remote_copy(
                src_ref=src.at[peer_index] if scatter else src,
                dst_ref=land.at[peer_index if receive_side else me],
                send_sem=send_sems.at[a * len(relations) + k], recv_sem=recv_sems.at[a * len(relations) + k],
                device_id=peer, device_id_type=MESH))
    return copies


def _exchange_start(srcs, scatter, after, name, relations=RELATIONS):
    n = len(srcs)
    land_shapes = [(s.shape if scatter else (N_DEV,) + s.shape) for s in srcs]

    def body(*refs):
        src_refs, land_refs = refs[:n], refs[n:2 * n]
        send_sems, recv_sems = refs[2 * n + 1], refs[2 * n + 2]
        token = refs[-1]
        for cp in _exchange_copies(src_refs, land_refs, send_sems, recv_sems, scatter, False, relations):
            cp.start()
        token[...] = jnp.zeros_like(token)

    sems = pltpu.SemaphoreType.DMA((n * len(relations),))
    outs = pl.pallas_call(
        body, name=name,
        out_shape=(sems, sems, *[pltpu.HBM(s.shape, s.dtype) for s in srcs],
                   *[pltpu.HBM(shape, s.dtype) for shape, s in zip(land_shapes, srcs)],
                   jax.ShapeDtypeStruct((8, 128), F32)),
        in_specs=[HBM] * (2 * n) + [ANY],
        out_specs=(SEM, SEM, *[HBM] * (2 * n), pl.BlockSpec(memory_space=pltpu.VMEM)),
        input_output_aliases={a: 2 + a for a in range(2 * n)},
        compiler_params=pltpu.CompilerParams(has_side_effects=EFFECT),
    )(*[pltpu.with_memory_space_constraint(s, pltpu.HBM) for s in srcs],
      *[pltpu.with_memory_space_constraint(lax.empty(shape, s.dtype), pltpu.HBM)
        for shape, s in zip(land_shapes, srcs)], after)
    return outs[0], outs[1], outs[2:2 + n], outs[2 + n:2 + 2 * n], outs[-1]


def _exchange_wait(started, scatter, after, name, relations=RELATIONS):
    send_sems, recv_sems, srcs, lands, _ = started
    n = len(srcs)

    def body(*refs):
        src_refs, land_refs = refs[:n], refs[n:2 * n]
        send_sems, recv_sems = refs[2 * n], refs[2 * n + 1]
        copies = _exchange_copies(src_refs, land_refs, send_sems, recv_sems, scatter, True, relations)
        for cp in copies:
            cp.wait_send()
        for cp in copies:
            cp.wait_recv()

    outs = pl.pallas_call(
        body, name=name,
        out_shape=(*[pltpu.HBM(s.shape, s.dtype) for s in srcs], *[pltpu.HBM(t.shape, t.dtype) for t in lands]),
        in_specs=[HBM] * (2 * n) + [SEM, SEM, ANY], out_specs=tuple([HBM] * (2 * n)),
        input_output_aliases={a: a for a in range(2 * n)},
        compiler_params=pltpu.CompilerParams(has_side_effects=EFFECT),
    )(*srcs, *lands, send_sems, recv_sems, after)
    return outs[:n], outs[n:]


def _forward_copies(land_refs, send_sems, recv_sems, receive_side):
    x, y, c = _position()
    copies = []
    for j, (px, py) in enumerate([(1 - x, y), (x, 1 - y), (1 - x, 1 - y)]):
        held, coming = 4 * px + 2 * py + c, 4 * px + 2 * py + (1 - c)
        for a, land in enumerate(land_refs):
            copies.append(pltpu.make_async_remote_copy(
                src_ref=land.at[held], dst_ref=land.at[coming if receive_side else held],
                send_sem=send_sems.at[3 * a + j], recv_sem=recv_sems.at[3 * a + j],
                device_id=(x, y, 1 - c), device_id_type=MESH))
    return copies


def _forward_start(lands, after, name):
    n = len(lands)

    def body(*refs):
        send_sems, recv_sems, token = refs[n + 1], refs[n + 2], refs[-1]
        for cp in _forward_copies(refs[:n], send_sems, recv_sems, False):
            cp.start()
        token[...] = jnp.zeros_like(token)

    sems = pltpu.SemaphoreType.DMA((3 * n,))
    outs = pl.pallas_call(
        body, name=name,
        out_shape=(sems, sems, *[pltpu.HBM(t.shape, t.dtype) for t in lands], jax.ShapeDtypeStruct((8, 128), F32)),
        in_specs=[HBM] * n + [ANY], out_specs=(SEM, SEM, *[HBM] * n, pl.BlockSpec(memory_space=pltpu.VMEM)),
        input_output_aliases={a: 2 + a for a in range(n)},
        compiler_params=pltpu.CompilerParams(has_side_effects=EFFECT),
    )(*lands, after)
    return outs[0], outs[1], outs[2:2 + n], outs[-1]


def _forward_wait(started, after, name):
    send_sems, recv_sems, lands, _ = started
    n = len(lands)

    def body(*refs):
        copies = _forward_copies(refs[:n], refs[n], refs[n + 1], True)
        for cp in copies:
            cp.wait_send()
        for cp in copies:
            cp.wait_recv()

    return pl.pallas_call(
        body, name=name, out_shape=tuple(pltpu.HBM(t.shape, t.dtype) for t in lands),
        in_specs=[HBM] * n + [SEM, SEM, ANY], out_specs=tuple([HBM] * n),
        input_output_aliases={a: a for a in range(n)},
        compiler_params=pltpu.CompilerParams(has_side_effects=EFFECT),
    )(*lands, send_sems, recv_sems, after)


def _place_own(lands, mine, me, name):
    n = len(lands)
    flat = [m.reshape(-1, m.shape[-1]) for m in mine]
    flat_lands = [t.reshape(N_DEV, -1, t.shape[-1]) for t in lands]

    def body(me_ref, *refs):
        for src, dst in zip(refs[:n], refs[2 * n:]):
            dst[...] = src[...]

    in_specs = [pl.BlockSpec((m.shape[0] // 2, m.shape[1]), lambda i, me_ref: (i, 0)) for m in flat]
    out_specs = [pl.BlockSpec((None, m.shape[0] // 2, m.shape[1]), lambda i, me_ref: (me_ref[0], i, 0)) for m in flat]
    outs = pl.pallas_call(
        body, name=name, out_shape=tuple(jax.ShapeDtypeStruct(t.shape, t.dtype) for t in flat_lands),
        grid_spec=pltpu.PrefetchScalarGridSpec(
            num_scalar_prefetch=1, grid=(2,), in_specs=in_specs + [ANY] * n, out_specs=tuple(out_specs)),
        input_output_aliases={1 + n + a: a for a in range(n)},
        compiler_params=_params(),
    )(me, *flat, *flat_lands)
    return [o.reshape(t.shape) for o, t in zip(outs, lands)]


W_IN_SHARD = N_IN // N_DEV
F_SHARD = F_COL // W_IN_SHARD
F_LO = F_COL - F_SHARD * W_IN_SHARD


def _w_ffn_in_view(w):
    return jnp.transpose(w, (0, 2, 1))


def _w_in_rearranged(g):
    parts = [g[d] for d in range(N_DEV)]
    with_f = parts[F_SHARD]
    parts[F_SHARD:F_SHARD + 1] = [with_f[:, :F_LO], with_f[:, F_LO + N_FORGET:]]
    parts += [with_f[:, F_LO:F_LO + N_FORGET], jnp.zeros((with_f.shape[0], BLK - N_FORGET), with_f.dtype)]
    return jnp.concatenate(parts, axis=1)


def _w_in_pieces(dw_r):
    def original(lo, hi):
        shift = 0 if hi <= F_COL else N_FORGET
        return dw_r[:, lo - shift:hi - shift]

    pieces = []
    for d in range(N_DEV):
        lo, hi = d * W_IN_SHARD, (d + 1) * W_IN_SHARD
        if d == F_SHARD:
            pieces.append(jnp.concatenate([original(lo, F_COL), dw_r[:, N_MAIN:N_MAIN + N_FORGET],
                                           original(F_COL + N_FORGET, hi)], axis=1))
        else:
            pieces.append(original(lo, hi))
    return jnp.stack(pieces)


def _row_pieces(dw):
    return dw.reshape(N_DEV, dw.shape[0] // N_DEV, dw.shape[1])


def _branch_pieces(dw):
    k, w, d = dw.shape
    return jnp.transpose(dw.reshape(k, w, N_DEV, d // N_DEV), (2, 0, 1, 3)).reshape(N_DEV, k * w, d // N_DEV)


def _pair_major(p8):
    return jnp.stack([p8[0::2], p8[1::2]])


def _pairs_col(a):
    return jnp.transpose(a.reshape(a.shape[0], 4, 2), (1, 0, 2))


def _pairs_row(a):
    return jnp.transpose(a.reshape(a.shape[0], 4, 2), (1, 2, 0))


def _heads_from_col(a):
    return jnp.transpose(a, (1, 0, 2)).reshape(a.shape[1], 8)


def _heads_from_row(a):
    return jnp.transpose(a, (2, 0, 1)).reshape(a.shape[2], 8)


def _pad_lanes(a, n):
    return jnp.pad(a, [(0, 0)] * (a.ndim - 1) + [(0, n - a.shape[-1])])


SMALL_SEGMENTS = (("dmod", 2 * 6 * D_MODEL), ("norm_mix_g", 2 * D_MODEL), ("norm_ffn_g", 2 * D_MODEL),
                  ("final_norm_g", D_MODEL), ("b_forget", 128), ("sinks", 128), ("rel_bias", 4224))
SMALL_ROWS = 176


def _pack_small(parts):
    flat = [_pad_lanes(parts[name].reshape(1, -1), size) for name, size in SMALL_SEGMENTS]
    total = sum(size for _, size in SMALL_SEGMENTS)
    flat.append(jnp.zeros((1, SMALL_ROWS * 128 - total), F32))
    return jnp.concatenate(flat, axis=1).reshape(SMALL_ROWS, 128)


def _unpack_small(packed, shapes):
    flat = packed.reshape(-1)
    out, pos = {}, 0
    for name, size in SMALL_SEGMENTS:
        shape = shapes[name]
        count = 1
        for d in shape:
            count *= d
        out[name] = flat[pos:pos + count].reshape(shape)
        pos += size
    return out


def kernel(x, c, norm_mix_g, norm_ffn_g, w_ada, b_ada, w_in, b_forget, sinks, rel_bias, w_branch, w_out, w_ffn_in, w_ffn_out, final_norm_g, loss_target, m_norm_mix_g, m_norm_ffn_g, m_w_ada, m_b_ada, m_w_in, m_b_forget, m_sinks, m_rel_bias, m_w_branch, m_w_out, m_w_ffn_in, m_w_ffn_out, m_final_norm_g, v_norm_mix_g, v_norm_ffn_g, v_w_ada, v_b_ada, v_w_in, v_b_forget, v_sinks, v_rel_bias, v_w_branch, v_w_out, v_w_ffn_in, v_w_ffn_out, v_final_norm_g):
    depth = w_in.shape[0]
    S, D = x.shape[1], x.shape[2]
    assert S % GROUP == 0 and S >= ATTN_WINDOW["c"] * BLK
    px, py, pc = _position()
    me = 4 * px + 2 * py + pc
    x0 = x[0]

    assert depth == 2
    big_weights = (w_in, w_branch, w_out, w_ffn_in, w_ffn_out)
    me_arr = jnp.stack([me, me]).astype(jnp.int32)
    me_in_arr = jnp.stack([2 * px + py, me]).astype(jnp.int32)

    def rest_matrices(g_branch, g_out, g_fin, g_fout):
        return (jnp.transpose(g_branch, (1, 2, 0, 3)).reshape(3, 512, D), g_out.reshape(D, D),
                g_fin.reshape(2 * FFN_HIDDEN, D), g_fout.reshape(FFN_HIDDEN, D))

    def arrive(started, after, name):
        mine, landed = _exchange_wait(started, False, after, f"{name}_wait", SAME_CORE)
        return mine, _forward_start(landed, mine[0], f"{name}_forward_start")

    def finish_gather(arrived, after, name):
        mine, forward = arrived
        landed = _forward_wait(forward, after, f"{name}_forward_wait")
        return _place_own(landed, mine, me.astype(jnp.int32).reshape(1), f"{name}_own")

    w_fin_t = _w_ffn_in_view(w_ffn_in)
    shards = [[t.astype(BF16) for t in (w_in[l], w_branch[l], w_out[l], w_fin_t[l], w_ffn_out[l])]
              for l in range(depth)]
    gathered_in0 = _big_all_gather(shards[0][:1], "comm_gather_w_in0")[0]
    gather_rest0 = _exchange_start(shards[0][1:], False, gathered_in0, "comm_gather_rest0_start", SAME_CORE)
    gather1 = _exchange_start(shards[1], False, gather_rest0[4], "comm_gather_weights1_start", SAME_CORE)
    W_in, W_branch, W_out, W_fin, W_fout = ([None, None] for _ in range(5))
    W_in[0] = _w_in_rearranged(gathered_in0)

    c_all = _small_all_gather(c.reshape(8, 128), "comm_gather_c").reshape(N_DEV, D)
    mod_cols = _ada_fwd(c_all, w_ada, "ada_fwd")
    mod_all = _small_all_gather(mod_cols.reshape(-1, 128), "comm_gather_mod")
    mod_all = mod_all.reshape(N_DEV, depth, N_DEV, w_ada.shape[2])
    mod_mine = lax.dynamic_index_in_dim(mod_all, me, axis=2, keepdims=False)
    mod = jnp.transpose(mod_mine, (1, 0, 2)).reshape(depth, 6 * D) + b_ada + gather1[4][0:1, 0:1]
    mods = [[mod[l:l + 1, k * D:(k + 1) * D] for k in range(6)] for l in range(depth)]

    slopes = jnp.exp2(-jnp.arange(1, 9, dtype=F32))
    saved = []
    xs = x0
    for l in range(depth):
        if l == 1:
            g_in1, *g_rest1 = finish_gather(arrived1, xs, "comm_gather_weights1")
            W_in[1] = _w_in_rearranged(g_in1)
            W_branch[1], W_out[1], W_fin[1], W_fout[1] = rest_matrices(*g_rest1)
        sh_m, sc_m, g_m, sh_f, sc_f, g_f = mods[l]
        gm, gf = norm_mix_g[l:l + 1], norm_ffn_g[l:l + 1]
        bfor = _pad_lanes(b_forget[l:l + 1], BLK)
        h = _norm_mod_fwd(xs, gm, sh_m, sc_m, f"norm_mix_fwd{l}")
        qkv = _matmul(h, W_in[l], "nn", BF16, f"proj_qkv{l}", TILES["proj_qkv"], n=N_QKV)
        gates = _matmul(h, W_in[l], "nn", F32, f"proj_gates{l}", TILES["proj_gates"], n=N_GATES,
                        b_off=N_QKV // TILES["proj_gates"][1])
        fb = _matmul(h, W_in[l], "nn", F32, f"proj_forget{l}", TILES["proj_forget"], n=BLK, b_off=N_MAIN // BLK)
        cum = _forget_fwd(fb, bfor, f"forget_fwd{l}")[:, :N_FORGET]
        cum_col, cum_row = _pairs_col(cum), _pairs_row(cum)
        tiles, tiles_t = _rel_expand(_rel_bases(rel_bias[l]), f"rel_expand{l}")
        o_a, lse_a = _attn_fwd("a", qkv, f"attn_a_fwd{l}", sinks=sinks[l], slopes=slopes)
        o_b, lse_b = _attn_fwd("b", qkv, f"attn_b_fwd{l}", cq_col=cum_col, ck_row=cum_row)
        arrived_rest0 = arrive(gather_rest0, o_b, "comm_gather_rest0") if l == 0 else None
        o_c, lse_c = _attn_fwd("c", qkv, f"attn_c_fwd{l}", bias=tiles, after=arrived_rest0[1][3] if l == 0 else None)
        if l == 0:
            W_branch[0], W_out[0], W_fin[0], W_fout[0] = rest_matrices(
                *finish_gather(arrived_rest0, o_c, "comm_gather_rest0"))
        x1, merged, mix = _merge_fwd(o_a, o_b, o_c, gates, W_branch[l], W_out[l], xs, g_m, f"merge_fwd{l}")
        h2 = _norm_mod_fwd(x1, gf, sh_f, sc_f, f"norm_ffn_fwd{l}")
        act = _ffn_in_fwd(h2, W_fin[l], f"ffn_in_fwd{l}")
        if l == 0:
            arrived1 = arrive(gather1, act, "comm_gather_weights1")
        x2, ffn = _matmul_resid(act, W_fout[l], x1, g_f, f"ffn_out{l}", TILES["ffn_out"],
                                after=arrived1[1][3] if l == 0 else None)
        saved.append(dict(x=xs, h=h, qkv=qkv, gates=gates, fb=fb, bfor=bfor, cum_col=cum_col, cum_row=cum_row,
                          tiles_t=tiles_t, o=(o_a, o_b, o_c), lse=(lse_a, lse_b, lse_c), merged=merged, mix=mix,
                          x1=x1, h2=h2, act=act, ffn=ffn))
        xs = x2

    dx, loss_tile, d_final_g = _final_loss(xs, loss_target[0], final_norm_g.reshape(1, D), "final_loss")
    loss = lax.psum(loss_tile[0, 0], ("x", "y", "c"))

    grads = {k: [None] * depth for k in ("w_in", "w_branch", "w_out", "w_ffn_in", "w_ffn_out", "norm_mix_g",
                                          "norm_ffn_g", "b_forget", "sinks", "rel_bias", "dmod")}
    def rest_pieces(l):
        return [_branch_pieces(grads["w_branch"][l]), _row_pieces(grads["w_out"][l]),
                _row_pieces(grads["w_ffn_in"][l]), _row_pieces(grads["w_ffn_out"][l])]

    reduce1 = reduce_rest0 = reduce_in0 = None
    for l in reversed(range(depth)):
        sv = saved[l]
        sh_m, sc_m, g_m, sh_f, sc_f, g_f = mods[l]
        if l == 0:
            g_f = g_f + reduce1[4][0:1, 0:1]
        gm, gf = norm_mix_g[l:l + 1], norm_ffn_g[l:l + 1]
        df, d_g_f = _gate_bwd(dx, sv["ffn"], g_f, f"ffn_gate_bwd{l}")
        du_g, du_u = _ffn_mid_bwd(sv["h2"], df, W_fin[l], W_fout[l], f"ffn_mid_bwd{l}")
        du = jnp.concatenate([du_g, du_u], axis=1)
        grads["w_ffn_out"][l] = _matmul(sv["act"], df, "tn", BF16, f"wgrad_ffn_out{l}", TILES["wgrad_ffn_out"])
        grads["w_ffn_in"][l] = _matmul(du, sv["h2"], "tn", BF16, f"wgrad_ffn_in{l}", TILES["wgrad_ffn_in"])
        dh2 = _matmul(du, W_fin[l], "nn", F32, f"dgrad_ffn_in{l}", TILES["dgrad_ffn_in"])
        dx1, d_sh_f, d_sc_f, d_gf = _norm_mod_bwd(sv["x1"], dh2, dx, gf, sc_f, f"norm_ffn_bwd{l}")
        dmix, d_g_m = _gate_bwd(dx1, sv["mix"], g_m, f"mix_gate_bwd{l}")
        grads["w_out"][l] = _matmul(sv["merged"], dmix, "tn", BF16, f"wgrad_out{l}", TILES["wgrad_out"])
        o_a, o_b, o_c = sv["o"]
        dgates, d_w_branch, do_a, do_b, do_c, dl_a, dl_b, dl_c = _merge_bwd(
            dmix, o_a, o_b, o_c, sv["gates"], W_branch[l], W_out[l], f"merge_bwd{l}")
        grads["w_branch"][l] = d_w_branch.astype(BF16)
        lse_rows = [_pairs_row(_heads_from_col(t)) for t in sv["lse"]]
        if l == 0:
            reduce_rest0 = _exchange_start(rest_pieces(0), True, dgates, "comm_reduce_rest0_start")
            lse_rows = [t + reduce_rest0[4][0:1, 0:1] for t in lse_rows]
        dqt_a, dk_a, dv_a, dsink = _attn_bwd("a", sv["qkv"], do_a, lse_rows[0], _pairs_row(dl_a), f"attn_a_bwd{l}",
                                             sinks=sinks[l], slopes=slopes)
        dqt_b, dk_b, dv_b, dck, dcq = _attn_bwd("b", sv["qkv"], do_b, lse_rows[1], _pairs_row(dl_b),
                                                f"attn_b_bwd{l}", cq_row=sv["cum_row"], ck_col=sv["cum_col"])
        dqt_c, dk_c, dv_c, dtiles_t = _attn_bwd("c", sv["qkv"], do_c, lse_rows[2], _pairs_row(dl_c),
                                                f"attn_c_bwd{l}", bias_t=sv["tiles_t"])
        grads["sinks"][l] = dsink[:, :2, 0].reshape(8)
        grads["rel_bias"][l] = _rel_reduce(dtiles_t, f"rel_reduce{l}")[:, 0, :N_REL]
        dcum_k = _pad_lanes(_heads_from_col(dck), BLK)
        dcum_q = _pad_lanes(_heads_from_row(dcq), BLK)
        dfb, d_bfor = _forget_bwd(dcum_q, dcum_k, sv["fb"], sv["bfor"], f"forget_bwd{l}")
        grads["b_forget"][l] = d_bfor[0, :N_FORGET]
        dproj = jnp.concatenate(
            [t.astype(BF16) for t in (dqt_a.T, dk_a, dv_a, dqt_b.T, dk_b, dv_b, dqt_c.T, dk_c, dv_c)]
            + [dgates, dfb.astype(BF16)], axis=1)
        grads["w_in"][l] = _matmul(sv["h"], dproj, "tn", BF16, f"wgrad_in{l}", TILES["wgrad_in"])
        dh = _matmul(dproj, W_in[l], "nt", F32, f"dgrad_in{l}", TILES["dgrad_in"])
        dx, d_sh_m, d_sc_m, d_gm = _norm_mod_bwd(sv["x"], dh, dx1, gm, sc_m, f"norm_mix_bwd{l}")
        grads["norm_mix_g"][l] = d_gm[0]
        grads["norm_ffn_g"][l] = d_gf[0]
        grads["dmod"][l] = jnp.concatenate([d_sh_m, d_sc_m, d_g_m, d_sh_f, d_sc_f, d_g_f], axis=1)[0]
        if l == 1:
            reduce1 = _exchange_start([_w_in_pieces(grads["w_in"][1])] + rest_pieces(1), True, dx, "comm_reduce1_start")

    grad_x = dx.reshape(x.shape)

    small_shapes = dict(dmod=b_ada.shape, norm_mix_g=norm_mix_g.shape, norm_ffn_g=norm_ffn_g.shape,
                        final_norm_g=final_norm_g.shape, b_forget=b_forget.shape, sinks=sinks.shape,
                        rel_bias=rel_bias.shape)
    mine_small = _pack_small(dict(
        dmod=jnp.stack(grads["dmod"]), norm_mix_g=jnp.stack(grads["norm_mix_g"]),
        norm_ffn_g=jnp.stack(grads["norm_ffn_g"]), final_norm_g=d_final_g[0],
        b_forget=_pad_lanes(jnp.stack(grads["b_forget"]).reshape(1, -1), 128),
        sinks=_pad_lanes(jnp.stack(grads["sinks"]).reshape(1, -1), 128),
        rel_bias=_pad_lanes(jnp.stack(grads["rel_bias"]).reshape(1, -1), 4224)))
    all_small = _small_all_gather(mine_small, "comm_gather_small").reshape(N_DEV, SMALL_ROWS, 128)
    pieces_in0 = _pair_major(_w_in_pieces(grads["w_in"][0]))
    from_sibling = _sibling_exchange([pieces_in0], "comm_reduce_in0_sibling")[0]
    pair_sum_in0 = _pair_add(pieces_in0, from_sibling, pc.astype(jnp.int32).reshape(1), "pair_add_in0")
    reduce_in0 = _exchange_start([pair_sum_in0], True, all_small, "comm_reduce_in0_start", CHIPS)
    in0_started = reduce_in0[4]

    def pack_params(b_ada_, nm, nf, fn, bf, sk, rb):
        return _pack_small(dict(dmod=b_ada_, norm_mix_g=nm, norm_ffn_g=nf, final_norm_g=fn,
                                b_forget=_pad_lanes(bf.reshape(1, -1), 128), sinks=_pad_lanes(sk.reshape(1, -1), 128),
                                rel_bias=_pad_lanes(rb.reshape(1, -1), 4224)))

    small_out = _adamw(
        pack_params(b_ada, norm_mix_g, norm_ffn_g, final_norm_g, b_forget, sinks, rel_bias)[None],
        pack_params(m_b_ada, m_norm_mix_g, m_norm_ffn_g, m_final_norm_g, m_b_forget, m_sinks, m_rel_bias)[None],
        pack_params(v_b_ada, v_norm_mix_g, v_norm_ffn_g, v_final_norm_g, v_b_forget, v_sinks, v_rel_bias)[None],
        [all_small], "adamw_small", me_arr, after=in0_started)
    small_out = [_unpack_small(t[0], small_shapes) for t in small_out]

    dmod_all = all_small[:, :96].reshape(N_DEV, depth, 6 * D)
    dmod_cols = lax.dynamic_slice_in_dim(dmod_all, me * w_ada.shape[2], w_ada.shape[2], axis=2)
    d_w_ada = _ada_bwd(jnp.transpose(c_all), jnp.transpose(dmod_cols, (1, 0, 2)), "ada_bwd")

    big = {"w_ada": _adamw(w_ada, m_w_ada, v_w_ada, [d_w_ada[l:l + 1] for l in range(depth)], "adamw_w_ada", me_arr,
                           after=in0_started)}
    sent1, landed1 = _exchange_wait(reduce1, True, big["w_ada"][0], "comm_reduce1_wait")
    sent_rest0, landed_rest0 = _exchange_wait(reduce_rest0, True, landed1[0], "comm_reduce_rest0_wait")
    parts = {"w_in": [None, (landed1[0], sent1[0])]}
    for a, name in enumerate(("w_branch", "w_out", "w_ffn_in", "w_ffn_out")):
        parts[name] = [(landed_rest0[a], sent_rest0[a]), (landed1[1 + a], sent1[1 + a])]

    def update(name, w, m, v, view=lambda t: t):
        per_layer = lambda t: t.reshape(depth, -1, t.shape[-1])
        outs = _adamw(*[per_layer(view(t)) for t in (w, m, v)], parts[name], f"adamw_{name}",
                      me_in_arr if name == "w_in" else me_arr)
        big[name] = [view(t).reshape(w.shape) for t in outs]

    update("w_ffn_in", w_ffn_in, m_w_ffn_in, v_w_ffn_in, _w_ffn_in_view)
    update("w_ffn_out", w_ffn_out, m_w_ffn_out, v_w_ffn_out)
    update("w_branch", w_branch, m_w_branch, v_w_branch)
    update("w_out", w_out, m_w_out, v_w_out)
    sent_in0, landed_in0 = _exchange_wait(reduce_in0, True, big["w_out"][0], "comm_reduce_in0_wait", CHIPS)
    parts["w_in"][0] = (landed_in0[0], sent_in0[0])
    update("w_in", w_in, m_w_in, v_w_in)

    def leaf(kind, name):
        if name in big:
            return big[name][kind]
        return small_out[kind]["dmod" if name == "b_ada" else name]

    order = ["norm_mix_g", "norm_ffn_g", "w_ada", "b_ada", "w_in", "b_forget", "sinks", "rel_bias", "w_branch",
             "w_out", "w_ffn_in", "w_ffn_out", "final_norm_g"]
    return (loss, grad_x, *[leaf(0, n) for n in order], *[leaf(1, n) for n in order],
            *[leaf(2, n) for n in order], *[leaf(3, n) for n in order])
```

```python
import functools

import jax
import jax.numpy as jnp
from jax import lax
from jax.experimental import pallas as pl
from jax.experimental.pallas import tpu as pltpu

F32 = jnp.float32
BF16 = jnp.bfloat16
NEG_INF = -1e30
EPS = 1e-6
N_DEV = 8
BLK = 128
GROUP = 4 * BLK
VMEM_LIMIT_BYTES = 56 * 1024 * 1024

D_MODEL = 1024
N_QKV = 3840
N_GATES = 3072
N_MAIN = N_QKV + N_GATES
N_FORGET = 8
N_IN = N_MAIN + N_FORGET
F_COL = 2304
FFN_HIDDEN = 2816
N_REL = 257

ADAM_LR, ADAM_B1, ADAM_B2, ADAM_EPS, ADAM_WD, ADAM_STEP = 0.001, 0.9, 0.999, 1e-08, 0.01, 10

NN = (((1,), (0,)), ((), ()))
NT = (((1,), (1,)), ((), ()))
TN = (((0,), (0,)), ((), ()))
HIGHEST = lax.Precision.HIGHEST

ATTN_COLS = {"a": (0, 4, 5), "b": (6, 10, 14), "c": (18, 22, 26)}
ATTN_WINDOW = {"a": 2, "c": 5}
ATTN_BLOCKS_PER_STEP = {"a": 4, "b": GROUP // BLK, "c": 2}


def _params():
    return pltpu.CompilerParams(vmem_limit_bytes=VMEM_LIMIT_BYTES)


def _tile(n, target):
    best = None
    t = 128
    while t <= min(n, target):
        if n % t == 0:
            best = t
        t += 128
    return best if best is not None else n


def _row_tile(n, target):
    t = min(n, target)
    while n % t:
        t -= 8
    return t


TILES = {
    "proj_qkv": (1024, 1280, 1024), "proj_gates": (1024, 768, 1024), "proj_forget": (1024, 128, 1024),
    "ffn_out": (1024, 512, 2816), "ffn_fused": (512, 1408),
    "wgrad_ffn_out": (1408, 1024, 1024), "wgrad_ffn_in": (1408, 1024, 1024), "dgrad_ffn_in": (1024, 1024, 1408),
    "wgrad_out": (1024, 1024, 1024),
    "wgrad_in": (1024, 1408, 1024), "dgrad_in": (1024, 1024, 1408),
}


def _matmul(a, b, mode, out_dtype, name, tiles, *, n=None, a_off=0, b_off=0, m=None, after=None):
    tm, tn, tk = tiles
    if mode == "nn":
        M, K = a.shape if m is None else (m, a.shape[1])
        N = b.shape[1] if n is None else n
    elif mode == "nt":
        M, K = a.shape
        N = b.shape[0] if n is None else n
    else:
        K = a.shape[0]
        M = a.shape[1] if m is None else m
        N = b.shape[1] if n is None else n
    tm = _tile(M, tm) if M % 128 == 0 else M
    tn = _tile(N, tn)
    tk = _tile(K, tk)
    nk = K // tk
    dims = {"nn": NN, "nt": NT, "tn": TN}[mode]
    if mode == "nn":
        a_spec = pl.BlockSpec((tm, tk), lambda i, j, k: (i + a_off, k))
        b_spec = pl.BlockSpec((tk, tn), lambda i, j, k: (k, j + b_off))
    elif mode == "nt":
        a_spec = pl.BlockSpec((tm, tk), lambda i, j, k: (i + a_off, k))
        b_spec = pl.BlockSpec((tn, tk), lambda i, j, k: (j + b_off, k))
    else:
        a_spec = pl.BlockSpec((tk, tm), lambda i, j, k: (k, i + a_off))
        b_spec = pl.BlockSpec((tk, tn), lambda i, j, k: (k, j + b_off))

    def body(a_ref, b_ref, *rest):
        o_ref, acc_ref = rest[-2:]
        k = pl.program_id(2)
        part = lax.dot_general(a_ref[...], b_ref[...], dims, preferred_element_type=F32)
        if nk == 1:
            o_ref[...] = part.astype(o_ref.dtype)
        else:
            @pl.when(k == 0)
            def _():
                acc_ref[...] = part

            @pl.when(k > 0)
            def _():
                acc_ref[...] += part

            @pl.when(k == nk - 1)
            def _():
                o_ref[...] = acc_ref[...].astype(o_ref.dtype)

    return pl.pallas_call(
        body, name=name,
        out_shape=jax.ShapeDtypeStruct((M, N), out_dtype),
        grid=(M // tm, N // tn, nk),
        in_specs=[a_spec, b_spec] + ([ANY] if after is not None else []),
        out_specs=pl.BlockSpec((tm, tn), lambda i, j, k: (i, j)),
        scratch_shapes=[pltpu.VMEM((tm, tn) if nk > 1 else (8, 128), F32)],
        compiler_params=_params(),
    )(a, b, *([after] if after is not None else []))


def _matmul_resid(a, b, resid, gate, name, tiles, after=None):
    M, K = a.shape
    N = b.shape[1]
    tm, tn, tk = (_tile(d, t) for d, t in zip((M, N, K), tiles))
    nk = K // tk

    def body(a_ref, b_ref, r_ref, g_ref, *rest):
        o_ref, s_ref, acc_ref = rest[-3:]
        k = pl.program_id(2)
        part = jnp.dot(a_ref[...], b_ref[...], preferred_element_type=F32)

        def finish(acc):
            o_ref[...] = r_ref[...] + g_ref[...] * acc
            s_ref[...] = acc.astype(BF16)

        if nk == 1:
            finish(part)
        else:
            @pl.when(k == 0)
            def _():
                acc_ref[...] = part

            @pl.when(k > 0)
            def _():
                acc_ref[...] += part

            @pl.when(k == nk - 1)
            def _():
                finish(acc_ref[...])

    return pl.pallas_call(
        body, name=name,
        out_shape=(jax.ShapeDtypeStruct((M, N), F32), jax.ShapeDtypeStruct((M, N), BF16)),
        grid=(M // tm, N // tn, nk),
        in_specs=[pl.BlockSpec((tm, tk), lambda i, j, k: (i, k)),
                  pl.BlockSpec((tk, tn), lambda i, j, k: (k, j)),
                  pl.BlockSpec((tm, tn), lambda i, j, k: (i, j)),
                  pl.BlockSpec((1, tn), lambda i, j, k: (0, j))] + ([ANY] if after is not None else []),
        out_specs=(pl.BlockSpec((tm, tn), lambda i, j, k: (i, j)),
                   pl.BlockSpec((tm, tn), lambda i, j, k: (i, j))),
        scratch_shapes=[pltpu.VMEM((tm, tn) if nk > 1 else (8, 128), F32)],
        compiler_params=_params(),
    )(a, b, resid, gate, *([after] if after is not None else []))


def _norm_mod_fwd(x, g, shift, scale, name):
    S, D = x.shape
    ts = _row_tile(S, 256)

    def body(x_ref, g_ref, sh_ref, sc_ref, h_ref):
        xv = x_ref[...]
        rstd = lax.rsqrt(jnp.mean(xv * xv, axis=-1, keepdims=True) + EPS)
        y = xv * rstd * g_ref[...]
        h_ref[...] = (y * (1.0 + sc_ref[...]) + sh_ref[...]).astype(BF16)

    row = pl.BlockSpec((1, D), lambda i: (0, 0))
    return pl.pallas_call(
        body, name=name, out_shape=jax.ShapeDtypeStruct((S, D), BF16), grid=(S // ts,),
        in_specs=[pl.BlockSpec((ts, D), lambda i: (i, 0)), row, row, row],
        out_specs=pl.BlockSpec((ts, D), lambda i: (i, 0)),
        compiler_params=_params(),
    )(x, g, shift, scale)


def _accumulate_rows(i, pairs):
    @pl.when(i == 0)
    def _():
        for ref, value in pairs:
            ref[...] = value

    @pl.when(i > 0)
    def _():
        for ref, value in pairs:
            ref[...] += value


def _gated_residual_bwd(dx, f_ref, gate_ref, df_ref):
    df_ref[...] = (dx * gate_ref[...]).astype(BF16)
    return jnp.sum(dx * f_ref[...].astype(F32), axis=0, keepdims=True)


def _norm_mod_bwd(x, dh, dres, g, scale, name, below=None):
    S, D = x.shape
    ts = _row_tile(S, 256)

    def body(x_ref, dh_ref, dr_ref, g_ref, sc_ref, *rest):
        i = pl.program_id(0)
        xv, dhv, gv = x_ref[...], dh_ref[...], g_ref[...]
        rstd = lax.rsqrt(jnp.mean(xv * xv, axis=-1, keepdims=True) + EPS)
        xhat = xv * rstd
        dn = dhv * (1.0 + sc_ref[...])
        dxhat = dn * gv
        proj = jnp.mean(dxhat * xhat, axis=-1, keepdims=True)
        dx = dr_ref[...] + rstd * (dxhat - xhat * proj)
        sums = [jnp.sum(dhv, axis=0, keepdims=True), jnp.sum(dhv * (xhat * gv), axis=0, keepdims=True),
                jnp.sum(dn * xhat, axis=0, keepdims=True)]
        if below is None:
            dx_ref, *sum_refs = rest
        else:
            f_ref, gate_ref, dx_ref, *sum_refs, df_ref = rest
            sums.append(_gated_residual_bwd(dx, f_ref, gate_ref, df_ref))
        dx_ref[...] = dx
        _accumulate_rows(i, list(zip(sum_refs, sums)))

    tile = pl.BlockSpec((ts, D), lambda i: (i, 0))
    row = pl.BlockSpec((1, D), lambda i: (0, 0))
    vec = jax.ShapeDtypeStruct((1, D), F32)
    fused = below is not None
    return pl.pallas_call(
        body, name=name,
        out_shape=(jax.ShapeDtypeStruct((S, D), F32), vec, vec, vec)
        + ((vec, jax.ShapeDtypeStruct((S, D), BF16)) if fused else ()),
        grid=(S // ts,),
        in_specs=[tile, tile, tile, row, row] + ([tile, row] if fused else []),
        out_specs=(tile, row, row, row) + ((row, tile) if fused else ()),
        compiler_params=_params(),
    )(x, dh, dres, g, scale, *(below if fused else ()))


def _ffn_in_fwd(h, w_t, name):
    S, D = h.shape
    F = w_t.shape[0] // 2
    tm, tn = _tile(S, TILES["ffn_fused"][0]), _tile(F, TILES["ffn_fused"][1])
    nj = F // tn

    def body(h_ref, wg_ref, wu_ref, o_ref):
        hv = h_ref[...]
        ug = lax.dot_general(hv, wg_ref[...], NT, preferred_element_type=F32)
        uu = lax.dot_general(hv, wu_ref[...], NT, preferred_element_type=F32)
        o_ref[...] = (ug * jax.nn.sigmoid(ug) * uu).astype(BF16)

    return pl.pallas_call(
        body, name=name, out_shape=jax.ShapeDtypeStruct((S, F), BF16), grid=(nj, S // tm),
        in_specs=[pl.BlockSpec((tm, D), lambda j, i: (i, 0)),
                  pl.BlockSpec((tn, D), lambda j, i: (j, 0)),
                  pl.BlockSpec((tn, D), lambda j, i: (j + nj, 0))],
        out_specs=pl.BlockSpec((tm, tn), lambda j, i: (i, j)),
        compiler_params=_params(),
    )(h, w_t, w_t)


def _ffn_mid_bwd(h, df, w_in_t, w_out, name):
    S, D = h.shape
    F = w_in_t.shape[0] // 2
    tm, tn = _tile(S, TILES["ffn_fused"][0]), _tile(F, TILES["ffn_fused"][1])
    nj = F // tn

    def body(h_ref, df_ref, wg_ref, wu_ref, wo_ref, dg_ref, du_ref):
        hv = h_ref[...]
        ug = lax.dot_general(hv, wg_ref[...], NT, preferred_element_type=F32)
        uu = lax.dot_general(hv, wu_ref[...], NT, preferred_element_type=F32)
        dact = lax.dot_general(df_ref[...], wo_ref[...], NT, preferred_element_type=F32)
        sig = jax.nn.sigmoid(ug)
        dg_ref[...] = (dact * uu * (sig * (1.0 + ug * (1.0 - sig)))).astype(BF16)
        du_ref[...] = (dact * (ug * sig)).astype(BF16)

    out = jax.ShapeDtypeStruct((S, F), BF16)
    return pl.pallas_call(
        body, name=name, out_shape=(out, out), grid=(nj, S // tm),
        in_specs=[pl.BlockSpec((tm, D), lambda j, i: (i, 0)),
                  pl.BlockSpec((tm, D), lambda j, i: (i, 0)),
                  pl.BlockSpec((tn, D), lambda j, i: (j, 0)),
                  pl.BlockSpec((tn, D), lambda j, i: (j + nj, 0)),
                  pl.BlockSpec((tn, D), lambda j, i: (j, 0))],
        out_specs=(pl.BlockSpec((tm, tn), lambda j, i: (i, j)), pl.BlockSpec((tm, tn), lambda j, i: (i, j))),
        compiler_params=_params(),
    )(h, df, w_in_t, w_in_t, w_out)


def _merge_fwd(o_a, o_b, o_c, gates, w_branch, w_out, resid, gate, name, *, tm=512):
    S, W = o_a.shape
    D = w_branch.shape[2]
    tm = _row_tile(S, tm)

    def body(oa_ref, ob_ref, oc_ref, g_ref, w_ref, wo_ref, r_ref, gm_ref, x_ref, m_ref, mix_ref):
        acc = None
        for k, o_ref in enumerate((oa_ref, ob_ref, oc_ref)):
            y = jnp.dot(o_ref[...], w_ref[k], preferred_element_type=F32)
            t = jax.nn.sigmoid(g_ref[:, k * D:(k + 1) * D]) * y
            acc = t if acc is None else acc + t
        merged = acc.astype(BF16)
        m_ref[...] = merged
        mix = jnp.dot(merged, wo_ref[...], preferred_element_type=F32)
        x_ref[...] = r_ref[...] + gm_ref[...] * mix
        mix_ref[...] = mix.astype(BF16)

    o_spec = pl.BlockSpec((tm, W), lambda i: (i, 0))
    tile = pl.BlockSpec((tm, D), lambda i: (i, 0))
    return pl.pallas_call(
        body, name=name,
        out_shape=(jax.ShapeDtypeStruct((S, D), F32), jax.ShapeDtypeStruct((S, D), BF16), jax.ShapeDtypeStruct((S, D), BF16)),
        grid=(S // tm,),
        in_specs=[o_spec, o_spec, o_spec, pl.BlockSpec((tm, 3 * D), lambda i: (i, 0)),
                  pl.BlockSpec((3, W, D), lambda i: (0, 0, 0)), pl.BlockSpec((D, D), lambda i: (0, 0)),
                  tile, pl.BlockSpec((1, D), lambda i: (0, 0))],
        out_specs=(tile, tile, tile),
        compiler_params=_params(),
    )(o_a, o_b, o_c, gates, w_branch, w_out, resid, gate)


def _merge_bwd(dmix, o_a, o_b, o_c, gates, w_branch, w_out, name, *, tm=256):
    S, W = o_a.shape
    D = w_branch.shape[2]
    tm = _row_tile(S, tm)
    n_heads = W // 64

    def body(dm_ref, oa_ref, ob_ref, oc_ref, g_ref, w_ref, wo_ref, dg_ref, dw_ref,
             doa_ref, dob_ref, doc_ref, dla_ref, dlb_ref, dlc_ref):
        first = pl.program_id(0) == 0
        dm = lax.dot_general(dm_ref[...], wo_ref[...], NT, preferred_element_type=F32)
        branches = ((oa_ref, doa_ref, dla_ref), (ob_ref, dob_ref, dlb_ref), (oc_ref, doc_ref, dlc_ref))
        for k, (o_ref, do_ref, dl_ref) in enumerate(branches):
            wk = w_ref[k]
            ov = o_ref[...]
            y = jnp.dot(ov, wk, preferred_element_type=F32)
            g = jax.nn.sigmoid(g_ref[:, k * D:(k + 1) * D])
            dy = (dm * g).astype(BF16)
            dwk = lax.dot_general(ov, dy, TN, preferred_element_type=F32)

            @pl.when(first)
            def _(k=k, dwk=dwk):
                dw_ref[k] = dwk

            @pl.when(jnp.logical_not(first))
            def _(k=k, dwk=dwk):
                dw_ref[k] += dwk
            dg_ref[:, k * D:(k + 1) * D] = (dm * y * (g * (1.0 - g))).astype(BF16)
            do16 = lax.dot_general(dy, wk, NT, preferred_element_type=F32).astype(BF16)
            do_ref[...] = do16
            prod = do16.astype(F32) * ov.astype(F32)
            for h in range(n_heads):
                dl_ref[:, h:h + 1] = jnp.sum(prod[:, 64 * h:64 * (h + 1)], axis=1, keepdims=True)

    o_spec = pl.BlockSpec((tm, W), lambda i: (i, 0))
    wide = pl.BlockSpec((tm, 3 * D), lambda i: (i, 0))
    dl_spec = pl.BlockSpec((tm, n_heads), lambda i: (i, 0))
    o_out = jax.ShapeDtypeStruct((S, W), BF16)
    wide_out = jax.ShapeDtypeStruct((S, 3 * D), BF16)
    dl_out = jax.ShapeDtypeStruct((S, n_heads), F32)
    whole = pl.BlockSpec((3, W, D), lambda i: (0, 0, 0))
    return pl.pallas_call(
        body, name=name,
        out_shape=(wide_out, jax.ShapeDtypeStruct((3, W, D), F32), o_out, o_out, o_out, dl_out, dl_out, dl_out),
        grid=(S // tm,),
        in_specs=[pl.BlockSpec((tm, D), lambda i: (i, 0)), o_spec, o_spec, o_spec, wide, whole,
                  pl.BlockSpec((D, D), lambda i: (0, 0))],
        out_specs=(wide, whole, o_spec, o_spec, o_spec, dl_spec, dl_spec, dl_spec),
        compiler_params=_params(),
    )(dmix, o_a, o_b, o_c, gates, w_branch, w_out)


def _band_mask(variant, t_abs, s_abs):
    if variant == "b":
        return s_abs <= t_abs
    qc, kc = t_abs >> 6, s_abs >> 6
    return (kc <= qc) & (kc >= qc - (2 if variant == "a" else 8))


def _attn_fwd(variant, qkv, name, *, sinks=None, slopes=None, cq_col=None, ck_row=None, bias=None, after=None):
    S = qkv.shape[0]
    nb = S // BLK
    qb, kb, vb = ATTN_COLS[variant]
    shared_kv = variant == "a"
    win = ATTN_WINDOW.get(variant)
    per_step = ATTN_BLOCKS_PER_STEP[variant]

    def body(*refs):
        if after is not None:
            refs = refs[:-3] + refs[-2:]
        if variant == "a":
            q_ref, k_ref, v_ref, sink_ref, slope_ref, o_ref, lse_ref = refs
        elif variant == "b":
            q_ref, k_ref, v_ref, cq_ref, ck_ref, o_ref, lse_ref = refs
        else:
            q_ref, k_ref, v_ref, bias_ref, o_ref, lse_ref = refs
        p = pl.program_id(0)
        lane = lax.broadcasted_iota(jnp.int32, (1, BLK), 1)

        def compute(i, rows, start, n_keys):
            n_rows = rows.stop - rows.start
            t_abs = i * BLK + lax.broadcasted_iota(jnp.int32, (n_rows, 1), 0)
            q2 = q_ref[rows, :].astype(F32) * 0.125
            k_w = k_ref[pl.ds(start, n_keys), :]
            v_w = v_ref[pl.ds(start, n_keys), :]
            s_abs = start + lax.broadcasted_iota(jnp.int32, (1, n_keys), 1)
            valid = _band_mask(variant, t_abs, s_abs)
            outs = []
            for half in (0, 1):
                hmask = (lane >= 64) if half else (lane < 64)
                qh = jnp.where(hmask, q2, 0.0)
                if shared_kv:
                    swap = (p // 2) != half
                    qh = jnp.where(swap, pltpu.roll(qh, 64, 1), qh)
                s = lax.dot_general(qh.astype(BF16), k_w, NT, preferred_element_type=F32)
                if variant == "a":
                    head = 2 * p + half
                    s = s + (-slope_ref[head]) * jnp.abs(t_abs - s_abs).astype(F32)
                elif variant == "b":
                    s = s + cq_ref[rows, half:half + 1] - ck_ref[half:half + 1, pl.ds(start, n_keys)]
                else:
                    j0 = start // BLK
                    s = s + jnp.concatenate(
                        [bias_ref[half, jnp.clip(i - j0 - b, 0, 4)] for b in range(win)], axis=1)
                s = jnp.where(valid, s, NEG_INF)
                m = jnp.max(s, axis=1, keepdims=True)
                if variant == "a":
                    m = jnp.maximum(m, sink_ref[head])
                pe = jnp.exp(s - m)
                l = jnp.sum(pe, axis=1, keepdims=True)
                if variant == "a":
                    l = l + jnp.exp(sink_ref[head] - m)
                out = jnp.dot(pe.astype(BF16), v_w, preferred_element_type=F32) / l
                if shared_kv:
                    out = jnp.where(swap, pltpu.roll(out, 64, 1), out)
                outs.append(out)
                lse_ref[rows, half:half + 1] = m + jnp.log(l)
            o_ref[rows, :] = jnp.where(lane < 64, outs[0], outs[1]).astype(BF16)

        step = pl.program_id(1)
        if variant == "b":
            for g in range(S // GROUP):
                pl.when(step == g)(functools.partial(compute, step * per_step, slice(0, GROUP), 0, (g + 1) * GROUP))
        else:
            for sub in range(per_step):
                i = step * per_step + sub
                start = jnp.clip(i - (win - 1), 0, nb - win) * BLK
                compute(i, slice(sub * BLK, (sub + 1) * BLK), pl.multiple_of(start, BLK), win * BLK)

    tq = per_step * BLK
    kv_col = (lambda p, i: (0, kb)) if shared_kv else (lambda p, i: (0, kb + p))
    vv_col = (lambda p, i: (0, vb)) if shared_kv else (lambda p, i: (0, vb + p))
    in_specs = [pl.BlockSpec((tq, BLK), lambda p, i: (i, qb + p)),
                pl.BlockSpec((S, BLK), kv_col), pl.BlockSpec((S, BLK), vv_col)]
    args = [qkv, qkv, qkv]
    if variant == "a":
        in_specs += [pl.BlockSpec(memory_space=pltpu.SMEM), pl.BlockSpec(memory_space=pltpu.SMEM)]
        args += [sinks, slopes]
    elif variant == "b":
        in_specs += [pl.BlockSpec((None, tq, 2), lambda p, i: (p, i, 0)),
                     pl.BlockSpec((None, 2, S), lambda p, i: (p, 0, 0))]
        args += [cq_col, ck_row]
    else:
        in_specs += [pl.BlockSpec((2, 5, BLK, BLK), lambda p, i: (p, 0, 0, 0))]
        args += [bias]
    if after is not None:
        in_specs.append(ANY)
        args.append(after)
    return pl.pallas_call(
        body, name=name,
        out_shape=(jax.ShapeDtypeStruct((S, 512), BF16), jax.ShapeDtypeStruct((4, S, 2), F32)),
        grid=(4, nb // per_step), in_specs=in_specs,
        out_specs=(pl.BlockSpec((tq, BLK), lambda p, i: (i, p)),
                   pl.BlockSpec((None, tq, 2), lambda p, i: (p, i, 0))),
        compiler_params=_params(),
    )(*args)


def _attn_bwd(variant, qkv, do, lse_row, delta_row, name, *, sinks=None, slopes=None, cq_row=None,
              ck_col=None, bias_t=None):
    S = qkv.shape[0]
    nb = S // BLK
    qb, kb, vb = ATTN_COLS[variant]
    shared_kv = variant == "a"
    win = ATTN_WINDOW.get(variant)
    per_step = ATTN_BLOCKS_PER_STEP[variant]

    def body(*refs):
        *refs, dqt_ref = refs
        if variant == "a":
            (q_ref, k_ref, v_ref, do_ref, lse_ref, dl_ref, sink_ref, slope_ref,
             dq_ref, dk_ref, dv_ref, ex_ref) = refs
        elif variant == "b":
            (q_ref, k_ref, v_ref, do_ref, lse_ref, dl_ref, cq_ref, ck_ref,
             dq_ref, dk_ref, dv_ref, ex_ref, dcq_ref) = refs
        else:
            (q_ref, k_ref, v_ref, do_ref, lse_ref, dl_ref, bias_ref,
             dq_ref, dk_ref, dv_ref, ex_ref) = refs
        p = pl.program_id(0)
        lane = lax.broadcasted_iota(jnp.int32, (1, BLK), 1)
        hmasks = [(lane < 64), (lane >= 64)]
        swaps = [(p // 2) != half for half in (0, 1)] if shared_kv else None

        @pl.when(pl.program_id(1) == 0)
        def _():
            dqt_ref[...] = jnp.zeros_like(dqt_ref)
            if variant == "b":
                dcq_ref[...] = jnp.zeros_like(dcq_ref)
            else:
                ex_ref[...] = jnp.zeros_like(ex_ref)

        def to_kv_lanes(x, h):
            x = jnp.where(hmasks[h], x, 0.0)
            if shared_kv:
                x = jnp.where(swaps[h], pltpu.roll(x, 64, 1), x)
            return x

        def compute(j, rows, start, n_q):
            n_rows = rows.stop - rows.start
            s_abs = j * BLK + lax.broadcasted_iota(jnp.int32, (n_rows, 1), 0)
            off_k = pl.multiple_of(j * BLK, BLK)
            k2 = k_ref[rows, :].astype(F32)
            v2 = v_ref[rows, :].astype(F32)
            if shared_kv:
                kv_lane = (lane >> 6) == (p // 2)
                k_src, v_src = jnp.where(kv_lane, k2, 0.0), jnp.where(kv_lane, v2, 0.0)
                k_al = [jnp.where(swaps[h], pltpu.roll(k_src, 64, 1), k_src) for h in (0, 1)]
                v_al = [jnp.where(swaps[h], pltpu.roll(v_src, 64, 1), v_src) for h in (0, 1)]
            else:
                k_al = [jnp.where(hmasks[h], k2, 0.0) for h in (0, 1)]
                v_al = [jnp.where(hmasks[h], v2, 0.0) for h in (0, 1)]
            k_al = [(t * 0.125).astype(BF16) for t in k_al]
            v_al = [t.astype(BF16) for t in v_al]
            q_w = q_ref[pl.ds(start, n_q), :]
            do_w = do_ref[pl.ds(start, n_q), :]
            t_abs = start + lax.broadcasted_iota(jnp.int32, (1, n_q), 1)
            valid = _band_mask(variant, t_abs, s_abs)
            dk_acc = dv_acc = None
            ds_both = []
            for half in (0, 1):
                s = lax.dot_general(k_al[half], q_w, NT, preferred_element_type=F32)
                if variant == "a":
                    s = s + (-slope_ref[2 * p + half]) * jnp.abs(t_abs - s_abs).astype(F32)
                elif variant == "b":
                    s = s + cq_ref[half:half + 1, pl.ds(start, n_q)] - ck_ref[rows, half:half + 1]
                else:
                    i0 = start // BLK
                    s = s + jnp.concatenate(
                        [bias_ref[half, jnp.clip(i0 + b - j, 0, 4)] for b in range(win)], axis=1)
                pr = jnp.where(valid, jnp.exp(s - lse_ref[half:half + 1, pl.ds(start, n_q)]), 0.0)
                dp = lax.dot_general(v_al[half], do_w, NT, preferred_element_type=F32)
                ds = pr * (dp - dl_ref[half:half + 1, pl.ds(start, n_q)])
                ds16 = ds.astype(BF16)
                dv_h = to_kv_lanes(jnp.dot(pr.astype(BF16), do_w, preferred_element_type=F32), half)
                dk_h = to_kv_lanes(jnp.dot(ds16, q_w, preferred_element_type=F32) * 0.125, half)
                dv_acc = dv_h if dv_acc is None else dv_acc + dv_h
                dk_acc = dk_h if dk_acc is None else dk_acc + dk_h
                ds_both.append(ds16)
                if variant == "b":
                    ex_ref[rows, half:half + 1] = -jnp.sum(ds, axis=1, keepdims=True)
                    dcq_ref[half:half + 1, pl.ds(start, n_q)] += jnp.sum(ds, axis=0, keepdims=True)
                elif variant == "c":
                    for b in range(win):
                        ex_ref[half, jnp.clip(i0 + b - j, 0, 4)] += ds[:, b * BLK:(b + 1) * BLK]
            dq_t = lax.dot_general(jnp.concatenate(k_al, axis=0), jnp.concatenate(ds_both, axis=0), TN,
                                   preferred_element_type=F32)
            dqt_ref[:, pl.ds(start, n_q)] += dq_t
            if shared_kv:
                @pl.when(p == 0)
                def _():
                    dk_ref[pl.ds(off_k, n_rows), :] = dk_acc
                    dv_ref[pl.ds(off_k, n_rows), :] = dv_acc

                @pl.when(p > 0)
                def _():
                    dk_ref[pl.ds(off_k, n_rows), :] += dk_acc
                    dv_ref[pl.ds(off_k, n_rows), :] += dv_acc
            else:
                dk_ref[pl.ds(off_k, n_rows), :] = dk_acc.astype(dk_ref.dtype)
                dv_ref[pl.ds(off_k, n_rows), :] = dv_acc.astype(dv_ref.dtype)
            if variant == "a":
                for half in (0, 1):
                    p_sink = jnp.exp(sink_ref[2 * p + half] - lse_ref[half:half + 1, pl.ds(off_k, n_rows)])
                    term = p_sink * dl_ref[half:half + 1, pl.ds(off_k, n_rows)]
                    ex_ref[half:half + 1, :] += -jnp.sum(term, axis=1, keepdims=True)

        step = pl.program_id(1)
        if variant == "b":
            for g in range(S // GROUP):
                pl.when(step == g)(functools.partial(compute, step * per_step, slice(0, GROUP), g * GROUP, S - g * GROUP))
        else:
            for sub in range(per_step):
                j = step * per_step + sub
                start = jnp.clip(j, 0, nb - win) * BLK
                compute(j, slice(sub * BLK, (sub + 1) * BLK), pl.multiple_of(start, BLK), win * BLK)

        @pl.when(step == nb // per_step - 1)
        def _():
            dq_ref[...] = jnp.transpose(dqt_ref[...]).astype(BF16)

    tk = per_step * BLK
    col = lambda c0: (lambda p, j: (0, c0 + p))
    kv_blk = (lambda c0: (lambda p, j: (j, c0))) if shared_kv else (lambda c0: (lambda p, j: (j, c0 + p)))
    pair = lambda p, j: (0, p)
    row_stat = pl.BlockSpec((None, 2, S), lambda p, j: (p, 0, 0))
    in_specs = [pl.BlockSpec((S, BLK), col(qb)),
                pl.BlockSpec((tk, BLK), kv_blk(kb)), pl.BlockSpec((tk, BLK), kv_blk(vb)),
                pl.BlockSpec((S, BLK), pair), row_stat, row_stat]
    args = [qkv, qkv, qkv, do, lse_row, delta_row]
    kv_width = BLK if shared_kv else 512
    kv_out = pl.BlockSpec((S, BLK), (lambda p, j: (0, 0)) if shared_kv else pair)
    kv_dtype = F32 if shared_kv else BF16
    out_shape = [jax.ShapeDtypeStruct((S, 512), BF16), jax.ShapeDtypeStruct((S, kv_width), kv_dtype),
                 jax.ShapeDtypeStruct((S, kv_width), kv_dtype)]
    out_specs = [pl.BlockSpec((S, BLK), pair), kv_out, kv_out]
    if variant == "a":
        in_specs += [pl.BlockSpec(memory_space=pltpu.SMEM), pl.BlockSpec(memory_space=pltpu.SMEM)]
        args += [sinks, slopes]
        out_shape.append(jax.ShapeDtypeStruct((4, 8, BLK), F32))
        out_specs.append(pl.BlockSpec((None, 8, BLK), lambda p, j: (p, 0, 0)))
    elif variant == "b":
        in_specs += [row_stat, pl.BlockSpec((None, tk, 2), lambda p, j: (p, j, 0))]
        args += [cq_row, ck_col]
        out_shape += [jax.ShapeDtypeStruct((4, S, 2), F32), jax.ShapeDtypeStruct((4, 2, S), F32)]
        out_specs += [pl.BlockSpec((None, tk, 2), lambda p, j: (p, j, 0)), row_stat]
    else:
        in_specs += [pl.BlockSpec((2, 5, BLK, BLK), lambda p, j: (p, 0, 0, 0))]
        args += [bias_t]
        out_shape.append(jax.ShapeDtypeStruct((8, 5, BLK, BLK), F32))
        out_specs.append(pl.BlockSpec((2, 5, BLK, BLK), lambda p, j: (p, 0, 0, 0)))
    return pl.pallas_call(
        body, name=name, out_shape=tuple(out_shape), grid=(4, nb // per_step),
        in_specs=in_specs, out_specs=tuple(out_specs), scratch_shapes=[pltpu.VMEM((BLK, S), F32)],
        compiler_params=_params(),
    )(*args)


def _log_sigmoid(x):
    return jnp.minimum(x, 0.0) - jnp.log(1.0 + jnp.exp(-jnp.abs(x)))


def _forget_fwd(fb, b_forget, name):
    S = fb.shape[0]
    nb = S // BLK

    def body(fb_ref, b_ref, cum_ref, carry_ref):
        i = pl.program_id(0)
        logf = _log_sigmoid(fb_ref[...] + b_ref[...])
        r = lax.broadcasted_iota(jnp.int32, (BLK, BLK), 0)
        c = lax.broadcasted_iota(jnp.int32, (BLK, BLK), 1)
        tri = (c <= r).astype(F32)

        @pl.when(i == 0)
        def _():
            carry_ref[...] = jnp.zeros_like(carry_ref)

        cum = jnp.dot(tri, logf, preferred_element_type=F32, precision=HIGHEST) + carry_ref[0:1, :]
        cum_ref[...] = cum
        carry_ref[...] = jnp.broadcast_to(cum[BLK - 1:BLK, :], carry_ref.shape)

    return pl.pallas_call(
        body, name=name, out_shape=jax.ShapeDtypeStruct((S, BLK), F32), grid=(nb,),
        in_specs=[pl.BlockSpec((BLK, BLK), lambda i: (i, 0)), pl.BlockSpec((1, BLK), lambda i: (0, 0))],
        out_specs=pl.BlockSpec((BLK, BLK), lambda i: (i, 0)),
        scratch_shapes=[pltpu.VMEM((8, BLK), F32)],
        compiler_params=_params(),
    )(fb, b_forget)


def _forget_bwd(dcum_q, dcum_k, fb, b_forget, name):
    S = fb.shape[0]
    nb = S // BLK

    def body(dq_ref, dk_ref, fb_ref, b_ref, dfb_ref, db_ref, carry_ref):
        g = pl.program_id(0)
        r = lax.broadcasted_iota(jnp.int32, (BLK, BLK), 0)
        c = lax.broadcasted_iota(jnp.int32, (BLK, BLK), 1)
        tri = (c >= r).astype(F32)

        @pl.when(g == 0)
        def _():
            carry_ref[...] = jnp.zeros_like(carry_ref)

        dcum = dq_ref[...] + dk_ref[...]
        dlogf = jnp.dot(tri, dcum, preferred_element_type=F32, precision=HIGHEST) + carry_ref[0:1, :]
        carry_ref[...] = jnp.broadcast_to(dlogf[0:1, :], carry_ref.shape)
        x = fb_ref[...] + b_ref[...]
        dfb = jnp.where(c < N_FORGET, dlogf * jax.nn.sigmoid(-x), 0.0)
        dfb_ref[...] = dfb
        db = jnp.sum(dfb, axis=0, keepdims=True)

        @pl.when(g == 0)
        def _():
            db_ref[...] = db

        @pl.when(g > 0)
        def _():
            db_ref[...] += db

    rev = pl.BlockSpec((BLK, BLK), lambda g: (nb - 1 - g, 0))
    row = pl.BlockSpec((1, BLK), lambda g: (0, 0))
    return pl.pallas_call(
        body, name=name,
        out_shape=(jax.ShapeDtypeStruct((S, BLK), F32), jax.ShapeDtypeStruct((1, BLK), F32)), grid=(nb,),
        in_specs=[rev, rev, rev, row], out_specs=(rev, row),
        scratch_shapes=[pltpu.VMEM((8, BLK), F32)],
        compiler_params=_params(),
    )(dcum_q, dcum_k, fb, b_forget)


def _skew(x, sign):
    row = lax.broadcasted_iota(jnp.int32, x.shape, 0)
    for b in range(7):
        amount = (1 << b) if sign > 0 else 256 - (1 << b)
        x = jnp.where(((row >> b) & 1) == 1, pltpu.roll(x, amount, 1), x)
    return x


def _rel_bases(rel):
    far = rel[:, 256:257]
    far127 = jnp.broadcast_to(far, (rel.shape[0], 127))
    base0 = jnp.concatenate([rel[:, 128:0:-1], far, rel[:, 255:128:-1]], axis=1)
    base1 = jnp.concatenate([rel[:, 256:128:-1], far, far127], axis=1)
    base0_t = jnp.concatenate([rel[:, 128:256], far, rel[:, 1:128]], axis=1)
    base1_t = jnp.concatenate([jnp.broadcast_to(far, (rel.shape[0], 128)), far, rel[:, 129:256]], axis=1)
    return jnp.stack([base0, base1, base0_t, base1_t], axis=1)


def _rel_expand(bases, name):
    def body(b_ref, t_ref, tt_ref):
        far = jnp.broadcast_to(b_ref[1:2, 0:1], (BLK, BLK))
        for k, out_ref in ((0, t_ref), (2, tt_ref)):
            for d in (0, 1):
                x = jnp.broadcast_to(b_ref[k + d:k + d + 1, :], (BLK, 2 * BLK))
                out_ref[d] = _skew(x, 1)[:, :BLK]
            for d in (2, 3, 4):
                out_ref[d] = far

    out = jax.ShapeDtypeStruct((8, 5, BLK, BLK), F32)
    spec = pl.BlockSpec((None, 5, BLK, BLK), lambda h: (h, 0, 0, 0))
    return pl.pallas_call(
        body, name=name, out_shape=(out, out), grid=(8,),
        in_specs=[pl.BlockSpec((None, 4, 2 * BLK), lambda h: (h, 0, 0))], out_specs=(spec, spec),
        compiler_params=_params(),
    )(bases)


def _rel_reduce(dtiles_t, name):
    def body(dt_ref, o_ref):
        zeros = jnp.zeros((BLK, BLK), F32)
        sums = []
        for d in (0, 1):
            x = _skew(jnp.concatenate([dt_ref[d], zeros], axis=1), -1)
            sums.append(jnp.broadcast_to(jnp.sum(x, axis=0, keepdims=True), (8, 2 * BLK)))
        lane = lax.broadcasted_iota(jnp.int32, (8, 2 * BLK), 1)
        main = pltpu.roll(sums[0], BLK, 1) + jnp.where(lane > BLK, sums[1], 0.0)
        far = jnp.sum(jnp.where(lane < BLK, sums[1], 0.0)[0:1], axis=1, keepdims=True)
        far = far + jnp.sum(jnp.sum(dt_ref[2] + dt_ref[3] + dt_ref[4], axis=0, keepdims=True), axis=1, keepdims=True)
        o_ref[...] = jnp.concatenate([main[0:1], jnp.broadcast_to(far, (1, BLK))], axis=1)

    return pl.pallas_call(
        body, name=name, out_shape=jax.ShapeDtypeStruct((8, 1, 3 * BLK), F32), grid=(8,),
        in_specs=[pl.BlockSpec((None, 5, BLK, BLK), lambda h: (h, 0, 0, 0))],
        out_specs=pl.BlockSpec((None, 1, 3 * BLK), lambda h: (h, 0, 0)),
        compiler_params=_params(),
    )(dtiles_t)


def _final_loss(x, target, g, below, name):
    S, D = x.shape
    ts = _row_tile(S, 256)

    def body(x_ref, t_ref, g_ref, f_ref, gate_ref, dx_ref, loss_ref, dg_ref, dgate_ref, df_ref):
        i = pl.program_id(0)
        xv, gv = x_ref[...], g_ref[...]
        rstd = lax.rsqrt(jnp.mean(xv * xv, axis=-1, keepdims=True) + EPS)
        xhat = xv * rstd
        err = xhat * gv - t_ref[...]
        part = 0.5 * jnp.sum(jnp.mean(err * err, axis=-1, keepdims=True), axis=0, keepdims=True)
        dy = err / D
        dg = jnp.sum(dy * xhat, axis=0, keepdims=True)
        dxhat = dy * gv
        proj = jnp.mean(dxhat * xhat, axis=-1, keepdims=True)
        dx = rstd * (dxhat - xhat * proj)
        dx_ref[...] = dx
        dgate = _gated_residual_bwd(dx, f_ref, gate_ref, df_ref)
        _accumulate_rows(i, [(loss_ref, jnp.broadcast_to(part, loss_ref.shape)), (dg_ref, dg), (dgate_ref, dgate)])

    tile = pl.BlockSpec((ts, D), lambda i: (i, 0))
    row = pl.BlockSpec((1, D), lambda i: (0, 0))
    vec = jax.ShapeDtypeStruct((1, D), F32)
    return pl.pallas_call(
        body, name=name,
        out_shape=(jax.ShapeDtypeStruct((S, D), F32), jax.ShapeDtypeStruct((8, 128), F32), vec, vec,
                   jax.ShapeDtypeStruct((S, D), BF16)),
        grid=(S // ts,), in_specs=[tile, tile, row, tile, row],
        out_specs=(tile, pl.BlockSpec((8, 128), lambda i: (0, 0)), row, row, tile),
        compiler_params=_params(),
    )(x, target, g, *below)


def _ada_fwd(c_all, w_ada, name):
    L, D, E = w_ada.shape

    def body(c_ref, w_ref, o_ref):
        cv = c_ref[...]
        cond = cv * jax.nn.sigmoid(cv)
        o_ref[...] = jnp.dot(cond, w_ref[...], preferred_element_type=F32, precision=HIGHEST)

    return pl.pallas_call(
        body, name=name, out_shape=jax.ShapeDtypeStruct((L, N_DEV, E), F32), grid=(L,),
        in_specs=[pl.BlockSpec((N_DEV, D), lambda l: (0, 0)), pl.BlockSpec((None, D, E), lambda l: (l, 0, 0))],
        out_specs=pl.BlockSpec((None, N_DEV, E), lambda l: (l, 0, 0)),
        compiler_params=_params(),
    )(c_all, w_ada)


def _ada_bwd(c_all_t, dmod, name):
    D = c_all_t.shape[0]
    L, _, E = dmod.shape

    def body(c_ref, d_ref, o_ref):
        cv = c_ref[...]
        cond = cv * jax.nn.sigmoid(cv)
        acc = None
        for b in range(N_DEV):
            t = cond[:, b:b + 1] * d_ref[b:b + 1, :]
            acc = t if acc is None else acc + t
        o_ref[...] = acc

    return pl.pallas_call(
        body, name=name, out_shape=jax.ShapeDtypeStruct((L, D, E), F32), grid=(L,),
        in_specs=[pl.BlockSpec((D, N_DEV), lambda l: (0, 0)), pl.BlockSpec((None, N_DEV, E), lambda l: (l, 0, 0))],
        out_specs=pl.BlockSpec((None, D, E), lambda l: (l, 0, 0)),
        compiler_params=_params(),
    )(c_all_t, dmod)


def _adamw(w, m, v, g_parts, name, me, after=None):
    L, R, C = w.shape
    tr = _row_tile(R, max(8, (256 * 1024 // max(C, 128)) // 8 * 8))
    nr = R // tr
    c1 = 1.0 - ADAM_B1 ** ADAM_STEP
    c2 = 1.0 - ADAM_B2 ** ADAM_STEP
    direct = [isinstance(p, tuple) for p in g_parts]
    n_in = sum(2 if d else 1 for d in direct)

    def body(me_ref, w_ref, m_ref, v_ref, *rest):
        g_refs, (go_ref, d_ref, mo_ref, vo_ref) = list(rest[:n_in]), rest[-4:]
        layer = pl.program_id(0)
        g = None
        for l in range(L):
            land_ref = g_refs.pop(0)
            own = g_refs.pop(0)[...].astype(F32) if direct[l] else None
            gl = None
            for k in range(land_ref.shape[0]):
                part = land_ref[k].astype(F32)
                if direct[l]:
                    part = jnp.where(me_ref[l] == k, own, part)
                gl = part if gl is None else gl + part
            g = gl if g is None else jnp.where(layer == l, gl, g)
        mn = ADAM_B1 * m_ref[...] + (1.0 - ADAM_B1) * g
        vn = ADAM_B2 * v_ref[...] + (1.0 - ADAM_B2) * (g * g)
        m_hat = mn / c1
        v_hat = vn / c2
        go_ref[...] = g
        d_ref[...] = -ADAM_LR * (m_hat / (jnp.sqrt(v_hat) + ADAM_EPS) + ADAM_WD * w_ref[...])
        mo_ref[...] = mn
        vo_ref[...] = vn

    def rows(l, layer, i):
        return jnp.where(layer == l, i, 0 if l > 0 else nr - 1)

    in_specs, operands = [], []
    for l, p in enumerate(g_parts):
        land, sent = p if direct[l] else (p, None)
        in_specs.append(pl.BlockSpec((land.shape[0], tr, C), lambda layer, i, me_ref, l=l: (0, rows(l, layer, i), 0)))
        operands.append(land)
        if direct[l]:
            in_specs.append(pl.BlockSpec((None, tr, C), lambda layer, i, me_ref, l=l: (me_ref[l], rows(l, layer, i), 0)))
            operands.append(sent)
    if after is not None:
        in_specs.append(ANY)
        operands.append(after)
    tile = pl.BlockSpec((None, tr, C), lambda layer, i, me_ref: (layer, i, 0))
    out = jax.ShapeDtypeStruct((L, R, C), F32)
    return pl.pallas_call(
        body, name=name, out_shape=(out, out, out, out),
        grid_spec=pltpu.PrefetchScalarGridSpec(
            num_scalar_prefetch=1, grid=(L, nr), in_specs=[tile, tile, tile] + in_specs,
            out_specs=(tile, tile, tile, tile)),
        compiler_params=_params(),
    )(me, w, m, v, *operands)


def _pair_add(pieces, recv, core, name):
    _, _, R, C = pieces.shape
    tr = _row_tile(R, max(8, (512 * 1024 // max(C, 128)) // 8 * 8))

    def body(core_ref, a_ref, b_ref, o_ref):
        o_ref[...] = (a_ref[...].astype(F32) + b_ref[...].astype(F32)).astype(BF16)

    return pl.pallas_call(
        body, name=name, out_shape=jax.ShapeDtypeStruct((4, R, C), BF16),
        grid_spec=pltpu.PrefetchScalarGridSpec(
            num_scalar_prefetch=1, grid=(4, R // tr),
            in_specs=[pl.BlockSpec((None, None, tr, C), lambda k, i, core_ref: (core_ref[0], k, i, 0)),
                      pl.BlockSpec((None, tr, C), lambda k, i, core_ref: (k, i, 0))],
            out_specs=pl.BlockSpec((None, tr, C), lambda k, i, core_ref: (k, i, 0))),
        compiler_params=_params(),
    )(core, pieces, recv)


MESH = pl.DeviceIdType.MESH
ANY = pl.BlockSpec(memory_space=pl.ANY)


def _position():
    return lax.axis_index("x"), lax.axis_index("y"), lax.axis_index("c")


def _small_all_gather(v, name):
    m_per, n = v.shape

    def body(x_ref, out_ref, send_sems, recv_sems, local_sem):
        x, y, c = _position()
        me, sibling = (x, y, c), (x, y, 1 - c)
        chips = [(1 - x, y), (x, 1 - y), (1 - x, 1 - y)]

        def rows(px, py, pc):
            return out_ref.at[pl.ds((4 * px + 2 * py + pc) * m_per, m_per), :]

        def copy(k, block, to, src=None):
            return pltpu.make_async_remote_copy(
                src_ref=rows(*block) if src is None else src, dst_ref=rows(*block),
                send_sem=send_sems.at[k], recv_sem=recv_sems.at[k], device_id=to, device_id_type=MESH)

        mine = pltpu.make_async_copy(x_ref, rows(*me), local_sem)
        mine.start()
        first = [copy(0, me, sibling, src=x_ref)]
        first += [copy(1 + j, me, (*chip, c), src=x_ref) for j, chip in enumerate(chips)]
        for cp in first:
            cp.start()
        passed = [copy(4 + j, (*chip, c), sibling) for j, chip in enumerate(chips)]
        for j, chip in enumerate(chips):
            copy(1 + j, (*chip, c), me).wait_recv()
            passed[j].start()
        copy(0, sibling, me).wait_recv()
        for j, chip in enumerate(chips):
            copy(4 + j, (*chip, 1 - c), me).wait_recv()
        for cp in first + passed:
            cp.wait_send()
        mine.wait()

    return pl.pallas_call(
        body, name=name, out_shape=jax.ShapeDtypeStruct((N_DEV * m_per, n), v.dtype),
        in_specs=[pl.BlockSpec(memory_space=pltpu.VMEM)], out_specs=pl.BlockSpec(memory_space=pltpu.VMEM),
        scratch_shapes=[pltpu.SemaphoreType.DMA((7,)), pltpu.SemaphoreType.DMA((7,)), pltpu.SemaphoreType.DMA],
    )(v)


def _big_all_gather(shards, name):
    n_arr = len(shards)

    def body(*refs):
        x_refs, out_refs = refs[:n_arr], refs[n_arr:2 * n_arr]
        send_sems, recv_sems, local_sems = refs[2 * n_arr:]
        x, y, c = _position()
        me, sibling = (x, y, c), (x, y, 1 - c)
        chips = [(1 - x, y), (x, 1 - y), (1 - x, 1 - y)]

        def slot(a, px, py, pc):
            return out_refs[a].at[4 * px + 2 * py + pc]

        def copy(a, k, block, to, src=None):
            return pltpu.make_async_remote_copy(
                src_ref=slot(a, *block) if src is None else src, dst_ref=slot(a, *block),
                send_sem=send_sems.at[a, k], recv_sem=recv_sems.at[a, k], device_id=to, device_id_type=MESH)

        mine = [pltpu.make_async_copy(x_refs[a], slot(a, *me), local_sems.at[a]) for a in range(n_arr)]
        for cp in mine:
            cp.start()
        first = []
        for j, chip in enumerate(chips):
            first += [copy(a, 1 + j, me, (*chip, c), src=x_refs[a]) for a in range(n_arr)]
        first += [copy(a, 0, me, sibling, src=x_refs[a]) for a in range(n_arr)]
        for cp in first:
            cp.start()
        passed = []
        for j, chip in enumerate(chips):
            for a in range(n_arr):
                copy(a, 1 + j, (*chip, c), me).wait_recv()
                fwd = copy(a, 4 + j, (*chip, c), sibling)
                fwd.start()
                passed.append(fwd)
        for a in range(n_arr):
            copy(a, 0, sibling, me).wait_recv()
        for j, chip in enumerate(chips):
            for a in range(n_arr):
                copy(a, 4 + j, (*chip, 1 - c), me).wait_recv()
        for cp in first + passed:
            cp.wait_send()
        for cp in mine:
            cp.wait()

    return pl.pallas_call(
        body, name=name,
        out_shape=tuple(jax.ShapeDtypeStruct((N_DEV,) + s.shape, s.dtype) for s in shards),
        in_specs=[ANY] * n_arr, out_specs=tuple([ANY] * n_arr),
        scratch_shapes=[pltpu.SemaphoreType.DMA((n_arr, 7)), pltpu.SemaphoreType.DMA((n_arr, 7)),
                        pltpu.SemaphoreType.DMA((n_arr,))],
    )(*shards)


def _sibling_exchange(pieces, name):
    n_arr = len(pieces)

    def body(*refs):
        p_refs, out_refs = refs[:n_arr], refs[n_arr:2 * n_arr]
        send_sems, recv_sems = refs[2 * n_arr:]
        x, y, c = _position()
        copies = [pltpu.make_async_remote_copy(
            src_ref=p_refs[a].at[1 - c], dst_ref=out_refs[a], send_sem=send_sems.at[a], recv_sem=recv_sems.at[a],
            device_id=(x, y, 1 - c), device_id_type=MESH) for a in range(n_arr)]
        for cp in copies:
            cp.start()
        for cp in copies:
            cp.wait()

    return pl.pallas_call(
        body, name=name,
        out_shape=tuple(jax.ShapeDtypeStruct(p.shape[1:], p.dtype) for p in pieces),
        in_specs=[ANY] * n_arr, out_specs=tuple([ANY] * n_arr),
        scratch_shapes=[pltpu.SemaphoreType.DMA((n_arr,)), pltpu.SemaphoreType.DMA((n_arr,))],
    )(*pieces)


HBM = pl.BlockSpec(memory_space=pltpu.HBM)
SEM = pl.BlockSpec(memory_space=pltpu.SEMAPHORE)
EFFECT = pltpu.SideEffectType.DATAFLOW_SIDE_EFFECTING
RELATIONS = [(rx, ry, rc) for rx in (0, 1) for ry in (0, 1) for rc in (0, 1)][1:]


SAME_CORE = [r for r in RELATIONS if r == (0, 0, 1) or r[2] == 0]


CHIPS = [r for r in RELATIONS if r[2] == 0]


def _exchange_copies(src_refs, land_refs, send_sems, recv_sems, scatter, receive_side, relations):
    x, y, c = _position()
    index = (lambda px, py, pc: 2 * px + py) if relations == CHIPS else (lambda px, py, pc: 4 * px + 2 * py + pc)
    me = index(x, y, c)
    copies = []
    for k, (rx, ry, rc) in enumerate(relations):
        peer = ((1 - x) if rx else x, (1 - y) if ry else y, (1 - c) if rc else c)
        peer_index = index(*peer)
        for a, (src, land) in enumerate(zip(src_refs, land_refs)):
            copies.append(pltpu.make_async_remote_copy(
                src_ref=src.at[peer_index] if scatter else src,
                dst_ref=land.at[peer_index if receive_side else me],
                send_sem=send_sems.at[a * len(relations) + k], recv_sem=recv_sems.at[a * len(relations) + k],
                device_id=peer, device_id_type=MESH))
    return copies


def _exchange_start(srcs, scatter, after, name, relations=RELATIONS):
    n = len(srcs)
    land_shapes = [(s.shape if scatter else (N_DEV,) + s.shape) for s in srcs]

    def body(*refs):
        src_refs, land_refs = refs[:n], refs[n:2 * n]
        send_sems, recv_sems = refs[2 * n + 1], refs[2 * n + 2]
        token = refs[-1]
        for cp in _exchange_copies(src_refs, land_refs, send_sems, recv_sems, scatter, False, relations):
            cp.start()
        token[...] = jnp.zeros_like(token)

    sems = pltpu.SemaphoreType.DMA((n * len(relations),))
    outs = pl.pallas_call(
        body, name=name,
        out_shape=(sems, sems, *[pltpu.HBM(s.shape, s.dtype) for s in srcs],
                   *[pltpu.HBM(shape, s.dtype) for shape, s in zip(land_shapes, srcs)],
                   jax.ShapeDtypeStruct((8, 128), F32)),
        in_specs=[HBM] * (2 * n) + [ANY],
        out_specs=(SEM, SEM, *[HBM] * (2 * n), pl.BlockSpec(memory_space=pltpu.VMEM)),
        input_output_aliases={a: 2 + a for a in range(2 * n)},
        compiler_params=pltpu.CompilerParams(has_side_effects=EFFECT),
    )(*[pltpu.with_memory_space_constraint(s, pltpu.HBM) for s in srcs],
      *[pltpu.with_memory_space_constraint(lax.empty(shape, s.dtype), pltpu.HBM)
        for shape, s in zip(land_shapes, srcs)], after)
    return outs[0], outs[1], outs[2:2 + n], outs[2 + n:2 + 2 * n], outs[-1]


def _exchange_wait(started, scatter, after, name, relations=RELATIONS):
    send_sems, recv_sems, srcs, lands, _ = started
    n = len(srcs)

    def body(*refs):
        src_refs, land_refs = refs[:n], refs[n:2 * n]
        send_sems, recv_sems = refs[2 * n], refs[2 * n + 1]
        copies = _exchange_copies(src_refs, land_refs, send_sems, recv_sems, scatter, True, relations)
        for cp in copies:
            cp.wait_send()
        for cp in copies:
            cp.wait_recv()

    outs = pl.pallas_call(
        body, name=name,
        out_shape=(*[pltpu.HBM(s.shape, s.dtype) for s in srcs], *[pltpu.HBM(t.shape, t.dtype) for t in lands]),
        in_specs=[HBM] * (2 * n) + [SEM, SEM, ANY], out_specs=tuple([HBM] * (2 * n)),
        input_output_aliases={a: a for a in range(2 * n)},
        compiler_params=pltpu.CompilerParams(has_side_effects=EFFECT),
    )(*srcs, *lands, send_sems, recv_sems, after)
    return outs[:n], outs[n:]


def _forward_copies(land_refs, send_sems, recv_sems, receive_side):
    x, y, c = _position()
    copies = []
    for j, (px, py) in enumerate([(1 - x, y), (x, 1 - y), (1 - x, 1 - y)]):
        held, coming = 4 * px + 2 * py + c, 4 * px + 2 * py + (1 - c)
        for a, land in enumerate(land_refs):
            copies.append(pltpu.make_async_remote_copy(
                src_ref=land.at[held], dst_ref=land.at[coming if receive_side else held],
                send_sem=send_sems.at[3 * a + j], recv_sem=recv_sems.at[3 * a + j],
                device_id=(x, y, 1 - c), device_id_type=MESH))
    return copies


def _forward_start(lands, after, name):
    n = len(lands)

    def body(*refs):
        send_sems, recv_sems, token = refs[n + 1], refs[n + 2], refs[-1]
        for cp in _forward_copies(refs[:n], send_sems, recv_sems, False):
            cp.start()
        token[...] = jnp.zeros_like(token)

    sems = pltpu.SemaphoreType.DMA((3 * n,))
    outs = pl.pallas_call(
        body, name=name,
        out_shape=(sems, sems, *[pltpu.HBM(t.shape, t.dtype) for t in lands], jax.ShapeDtypeStruct((8, 128), F32)),
        in_specs=[HBM] * n + [ANY], out_specs=(SEM, SEM, *[HBM] * n, pl.BlockSpec(memory_space=pltpu.VMEM)),
        input_output_aliases={a: 2 + a for a in range(n)},
        compiler_params=pltpu.CompilerParams(has_side_effects=EFFECT),
    )(*lands, after)
    return outs[0], outs[1], outs[2:2 + n], outs[-1]


def _forward_wait(started, after, name):
    send_sems, recv_sems, lands, _ = started
    n = len(lands)

    def body(*refs):
        copies = _forward_copies(refs[:n], refs[n], refs[n + 1], True)
        for cp in copies:
            cp.wait_send()
        for cp in copies:
            cp.wait_recv()

    return pl.pallas_call(
        body, name=name, out_shape=tuple(pltpu.HBM(t.shape, t.dtype) for t in lands),
        in_specs=[HBM] * n + [SEM, SEM, ANY], out_specs=tuple([HBM] * n),
        input_output_aliases={a: a for a in range(n)},
        compiler_params=pltpu.CompilerParams(has_side_effects=EFFECT),
    )(*lands, send_sems, recv_sems, after)


def _place_own(lands, mine, me, name):
    n = len(lands)
    flat = [m.reshape(-1, m.shape[-1]) for m in mine]
    flat_lands = [t.reshape(N_DEV, -1, t.shape[-1]) for t in lands]

    def body(me_ref, *refs):
        for src, dst in zip(refs[:n], refs[2 * n:]):
            dst[...] = src[...]

    in_specs = [pl.BlockSpec((m.shape[0] // 2, m.shape[1]), lambda i, me_ref: (i, 0)) for m in flat]
    out_specs = [pl.BlockSpec((None, m.shape[0] // 2, m.shape[1]), lambda i, me_ref: (me_ref[0], i, 0)) for m in flat]
    outs = pl.pallas_call(
        body, name=name, out_shape=tuple(jax.ShapeDtypeStruct(t.shape, t.dtype) for t in flat_lands),
        grid_spec=pltpu.PrefetchScalarGridSpec(
            num_scalar_prefetch=1, grid=(2,), in_specs=in_specs + [ANY] * n, out_specs=tuple(out_specs)),
        input_output_aliases={1 + n + a: a for a in range(n)},
        compiler_params=_params(),
    )(me, *flat, *flat_lands)
    return [o.reshape(t.shape) for o, t in zip(outs, lands)]


W_IN_SHARD = N_IN // N_DEV
F_SHARD = F_COL // W_IN_SHARD
F_LO = F_COL - F_SHARD * W_IN_SHARD


def _w_ffn_in_view(w):
    return jnp.transpose(w, (0, 2, 1))


def _w_in_rearranged(g):
    parts = [g[d] for d in range(N_DEV)]
    with_f = parts[F_SHARD]
    parts[F_SHARD:F_SHARD + 1] = [with_f[:, :F_LO], with_f[:, F_LO + N_FORGET:]]
    parts += [with_f[:, F_LO:F_LO + N_FORGET], jnp.zeros((with_f.shape[0], BLK - N_FORGET), with_f.dtype)]
    return jnp.concatenate(parts, axis=1)


def _w_in_pieces(dw_r):
    def original(lo, hi):
        shift = 0 if hi <= F_COL else N_FORGET
        return dw_r[:, lo - shift:hi - shift]

    pieces = []
    for d in range(N_DEV):
        lo, hi = d * W_IN_SHARD, (d + 1) * W_IN_SHARD
        if d == F_SHARD:
            pieces.append(jnp.concatenate([original(lo, F_COL), dw_r[:, N_MAIN:N_MAIN + N_FORGET],
                                           original(F_COL + N_FORGET, hi)], axis=1))
        else:
            pieces.append(original(lo, hi))
    return jnp.stack(pieces)


def _row_pieces(dw):
    return dw.reshape(N_DEV, dw.shape[0] // N_DEV, dw.shape[1])


def _branch_pieces(dw):
    k, w, d = dw.shape
    return jnp.transpose(dw.reshape(k, w, N_DEV, d // N_DEV), (2, 0, 1, 3)).reshape(N_DEV, k * w, d // N_DEV)


def _pair_major(p8):
    return jnp.stack([p8[0::2], p8[1::2]])


def _pairs_col(a):
    return jnp.transpose(a.reshape(a.shape[0], 4, 2), (1, 0, 2))


def _pairs_row(a):
    return jnp.transpose(a.reshape(a.shape[0], 4, 2), (1, 2, 0))


def _heads_from_col(a):
    return jnp.transpose(a, (1, 0, 2)).reshape(a.shape[1], 8)


def _heads_from_row(a):
    return jnp.transpose(a, (2, 0, 1)).reshape(a.shape[2], 8)


def _pad_lanes(a, n):
    return jnp.pad(a, [(0, 0)] * (a.ndim - 1) + [(0, n - a.shape[-1])])


SMALL_SEGMENTS = (("dmod", 2 * 6 * D_MODEL), ("norm_mix_g", 2 * D_MODEL), ("norm_ffn_g", 2 * D_MODEL),
                  ("final_norm_g", D_MODEL), ("b_forget", 128), ("sinks", 128), ("rel_bias", 4224), ("loss", 128))
SMALL_ROWS = 176


def _pack_small(parts):
    flat = [_pad_lanes(parts[name].reshape(1, -1), size) for name, size in SMALL_SEGMENTS]
    total = sum(size for _, size in SMALL_SEGMENTS)
    flat.append(jnp.zeros((1, SMALL_ROWS * 128 - total), F32))
    return jnp.concatenate(flat, axis=1).reshape(SMALL_ROWS, 128)


def _unpack_small(packed, shapes):
    flat = packed.reshape(-1)
    out, pos = {}, 0
    for name, size in SMALL_SEGMENTS:
        shape = shapes[name]
        count = 1
        for d in shape:
            count *= d
        out[name] = flat[pos:pos + count].reshape(shape)
        pos += size
    return out


def kernel(x, c, norm_mix_g, norm_ffn_g, w_ada, b_ada, w_in, b_forget, sinks, rel_bias, w_branch, w_out, w_ffn_in, w_ffn_out, final_norm_g, loss_target, m_norm_mix_g, m_norm_ffn_g, m_w_ada, m_b_ada, m_w_in, m_b_forget, m_sinks, m_rel_bias, m_w_branch, m_w_out, m_w_ffn_in, m_w_ffn_out, m_final_norm_g, v_norm_mix_g, v_norm_ffn_g, v_w_ada, v_b_ada, v_w_in, v_b_forget, v_sinks, v_rel_bias, v_w_branch, v_w_out, v_w_ffn_in, v_w_ffn_out, v_final_norm_g):
    depth = w_in.shape[0]
    S, D = x.shape[1], x.shape[2]
    assert S % GROUP == 0 and S >= ATTN_WINDOW["c"] * BLK
    px, py, pc = _position()
    me = 4 * px + 2 * py + pc
    x0 = x[0]

    assert depth == 2
    big_weights = (w_in, w_branch, w_out, w_ffn_in, w_ffn_out)
    me_arr = jnp.stack([me, me]).astype(jnp.int32)
    me_in_arr = jnp.stack([2 * px + py, me]).astype(jnp.int32)

    def rest_matrices(g_branch, g_out, g_fin, g_fout):
        return (jnp.transpose(g_branch, (1, 2, 0, 3)).reshape(3, 512, D), g_out.reshape(D, D),
                g_fin.reshape(2 * FFN_HIDDEN, D), g_fout.reshape(FFN_HIDDEN, D))

    def arrive(started, after, name):
        mine, landed = _exchange_wait(started, False, after, f"{name}_wait", SAME_CORE)
        return mine, _forward_start(landed, mine[0], f"{name}_forward_start")

    def finish_gather(arrived, after, name):
        mine, forward = arrived
        landed = _forward_wait(forward, after, f"{name}_forward_wait")
        return _place_own(landed, mine, me.astype(jnp.int32).reshape(1), f"{name}_own")

    w_fin_t = _w_ffn_in_view(w_ffn_in)
    shards = [[t.astype(BF16) for t in (w_in[l], w_branch[l], w_out[l], w_fin_t[l], w_ffn_out[l])]
              for l in range(depth)]
    gathered_in0 = _big_all_gather(shards[0][:1], "comm_gather_w_in0")[0]
    gather_rest0 = _exchange_start(shards[0][1:], False, gathered_in0, "comm_gather_rest0_start", SAME_CORE)
    gather1 = _exchange_start(shards[1], False, gather_rest0[4], "comm_gather_weights1_start", SAME_CORE)
    W_in, W_branch, W_out, W_fin, W_fout = ([None, None] for _ in range(5))
    W_in[0] = _w_in_rearranged(gathered_in0)

    c_all = _small_all_gather(c.reshape(8, 128), "comm_gather_c").reshape(N_DEV, D)
    mod_cols = _ada_fwd(c_all, w_ada, "ada_fwd")
    mod_all = _small_all_gather(mod_cols.reshape(-1, 128), "comm_gather_mod")
    mod_all = mod_all.reshape(N_DEV, depth, N_DEV, w_ada.shape[2])
    mod_mine = lax.dynamic_index_in_dim(mod_all, me, axis=2, keepdims=False)
    mod = jnp.transpose(mod_mine, (1, 0, 2)).reshape(depth, 6 * D) + b_ada + gather1[4][0:1, 0:1]
    mods = [[mod[l:l + 1, k * D:(k + 1) * D] for k in range(6)] for l in range(depth)]

    slopes = jnp.exp2(-jnp.arange(1, 9, dtype=F32))
    saved = []
    xs = x0
    for l in range(depth):
        if l == 1:
            g_in1, *g_rest1 = finish_gather(arrived1, xs, "comm_gather_weights1")
            W_in[1] = _w_in_rearranged(g_in1)
            W_branch[1], W_out[1], W_fin[1], W_fout[1] = rest_matrices(*g_rest1)
        sh_m, sc_m, g_m, sh_f, sc_f, g_f = mods[l]
        gm, gf = norm_mix_g[l:l + 1], norm_ffn_g[l:l + 1]
        bfor = _pad_lanes(b_forget[l:l + 1], BLK)
        h = _norm_mod_fwd(xs, gm, sh_m, sc_m, f"norm_mix_fwd{l}")
        qkv = _matmul(h, W_in[l], "nn", BF16, f"proj_qkv{l}", TILES["proj_qkv"], n=N_QKV)
        gates = _matmul(h, W_in[l], "nn", F32, f"proj_gates{l}", TILES["proj_gates"], n=N_GATES,
                        b_off=N_QKV // TILES["proj_gates"][1])
        fb = _matmul(h, W_in[l], "nn", F32, f"proj_forget{l}", TILES["proj_forget"], n=BLK, b_off=N_MAIN // BLK)
        cum = _forget_fwd(fb, bfor, f"forget_fwd{l}")[:, :N_FORGET]
        cum_col, cum_row = _pairs_col(cum), _pairs_row(cum)
        tiles, tiles_t = _rel_expand(_rel_bases(rel_bias[l]), f"rel_expand{l}")
        o_a, lse_a = _attn_fwd("a", qkv, f"attn_a_fwd{l}", sinks=sinks[l], slopes=slopes)
        o_b, lse_b = _attn_fwd("b", qkv, f"attn_b_fwd{l}", cq_col=cum_col, ck_row=cum_row)
        arrived_rest0 = arrive(gather_rest0, o_b, "comm_gather_rest0") if l == 0 else None
        o_c, lse_c = _attn_fwd("c", qkv, f"attn_c_fwd{l}", bias=tiles, after=arrived_rest0[1][3] if l == 0 else None)
        if l == 0:
            W_branch[0], W_out[0], W_fin[0], W_fout[0] = rest_matrices(
                *finish_gather(arrived_rest0, o_c, "comm_gather_rest0"))
        x1, merged, mix = _merge_fwd(o_a, o_b, o_c, gates, W_branch[l], W_out[l], xs, g_m, f"merge_fwd{l}")
        h2 = _norm_mod_fwd(x1, gf, sh_f, sc_f, f"norm_ffn_fwd{l}")
        act = _ffn_in_fwd(h2, W_fin[l], f"ffn_in_fwd{l}")
        if l == 0:
            arrived1 = arrive(gather1, act, "comm_gather_weights1")
        x2, ffn = _matmul_resid(act, W_fout[l], x1, g_f, f"ffn_out{l}", TILES["ffn_out"],
                                after=arrived1[1][3] if l == 0 else None)
        saved.append(dict(x=xs, h=h, qkv=qkv, gates=gates, fb=fb, bfor=bfor, cum_col=cum_col, cum_row=cum_row,
                          tiles_t=tiles_t, o=(o_a, o_b, o_c), lse=(lse_a, lse_b, lse_c), merged=merged, mix=mix,
                          x1=x1, h2=h2, act=act, ffn=ffn))
        xs = x2

    dx, loss_tile, d_final_g, d_g_f, df = _final_loss(
        xs, loss_target[0], final_norm_g.reshape(1, D), (saved[-1]["ffn"], mods[-1][5]), "final_loss")

    grads = {k: [None] * depth for k in ("w_in", "w_branch", "w_out", "w_ffn_in", "w_ffn_out", "norm_mix_g",
                                          "norm_ffn_g", "b_forget", "sinks", "rel_bias", "dmod")}
    def rest_pieces(l):
        return [_branch_pieces(grads["w_branch"][l]), _row_pieces(grads["w_out"][l]),
                _row_pieces(grads["w_ffn_in"][l]), _row_pieces(grads["w_ffn_out"][l])]

    reduce1 = reduce_rest0 = reduce_in0 = None
    for l in reversed(range(depth)):
        sv = saved[l]
        sh_m, sc_m, g_m, sh_f, sc_f, g_f = mods[l]
        gm, gf = norm_mix_g[l:l + 1], norm_ffn_g[l:l + 1]
        du_g, du_u = _ffn_mid_bwd(sv["h2"], df, W_fin[l], W_fout[l], f"ffn_mid_bwd{l}")
        du = jnp.concatenate([du_g, du_u], axis=1)
        grads["w_ffn_out"][l] = _matmul(sv["act"], df, "tn", BF16, f"wgrad_ffn_out{l}", TILES["wgrad_ffn_out"])
        grads["w_ffn_in"][l] = _matmul(du, sv["h2"], "tn", BF16, f"wgrad_ffn_in{l}", TILES["wgrad_ffn_in"])
        dh2 = _matmul(du, W_fin[l], "nn", F32, f"dgrad_ffn_in{l}", TILES["dgrad_ffn_in"])
        dx1, d_sh_f, d_sc_f, d_gf, d_g_m, dmix = _norm_mod_bwd(sv["x1"], dh2, dx, gf, sc_f, f"norm_ffn_bwd{l}",
                                                               below=(sv["mix"], g_m))
        grads["w_out"][l] = _matmul(sv["merged"], dmix, "tn", BF16, f"wgrad_out{l}", TILES["wgrad_out"])
        o_a, o_b, o_c = sv["o"]
        dgates, d_w_branch, do_a, do_b, do_c, dl_a, dl_b, dl_c = _merge_bwd(
            dmix, o_a, o_b, o_c, sv["gates"], W_branch[l], W_out[l], f"merge_bwd{l}")
        grads["w_branch"][l] = d_w_branch.astype(BF16)
        lse_rows = [_pairs_row(_heads_from_col(t)) for t in sv["lse"]]
        if l == 0:
            reduce_rest0 = _exchange_start(rest_pieces(0), True, dgates, "comm_reduce_rest0_start")
            lse_rows = [t + reduce_rest0[4][0:1, 0:1] for t in lse_rows]
        dq_a, dk_a, dv_a, dsink = _attn_bwd("a", sv["qkv"], do_a, lse_rows[0], _pairs_row(dl_a), f"attn_a_bwd{l}",
                                            sinks=sinks[l], slopes=slopes)
        dq_b, dk_b, dv_b, dck, dcq = _attn_bwd("b", sv["qkv"], do_b, lse_rows[1], _pairs_row(dl_b),
                                               f"attn_b_bwd{l}", cq_row=sv["cum_row"], ck_col=sv["cum_col"])
        dq_c, dk_c, dv_c, dtiles_t = _attn_bwd("c", sv["qkv"], do_c, lse_rows[2], _pairs_row(dl_c),
                                               f"attn_c_bwd{l}", bias_t=sv["tiles_t"])
        grads["sinks"][l] = dsink[:, :2, 0].reshape(8)
        grads["rel_bias"][l] = _rel_reduce(dtiles_t, f"rel_reduce{l}")[:, 0, :N_REL]
        dcum_k = _pad_lanes(_heads_from_col(dck), BLK)
        dcum_q = _pad_lanes(_heads_from_row(dcq), BLK)
        dfb, d_bfor = _forget_bwd(dcum_q, dcum_k, sv["fb"], sv["bfor"], f"forget_bwd{l}")
        grads["b_forget"][l] = d_bfor[0, :N_FORGET]
        dproj = jnp.concatenate(
            [t.astype(BF16) for t in (dq_a, dk_a, dv_a, dq_b, dk_b, dv_b, dq_c, dk_c, dv_c, dgates, dfb)],
            axis=1)
        grads["w_in"][l] = _matmul(sv["h"], dproj, "tn", BF16, f"wgrad_in{l}", TILES["wgrad_in"])
        if l == 1:
            reduce1 = _exchange_start([_w_in_pieces(grads["w_in"][1])] + rest_pieces(1), True, dproj, "comm_reduce1_start")
        dh = _matmul(dproj, W_in[l], "nt", F32, f"dgrad_in{l}", TILES["dgrad_in"], after=reduce1[4] if l == 1 else None)
        d_g_f_here = d_g_f
        if l > 0:
            dx, d_sh_m, d_sc_m, d_gm, d_g_f, df = _norm_mod_bwd(sv["x"], dh, dx1, gm, sc_m, f"norm_mix_bwd{l}",
                                                                below=(saved[l - 1]["ffn"], mods[l - 1][5]))
        else:
            dx, d_sh_m, d_sc_m, d_gm = _norm_mod_bwd(sv["x"], dh, dx1, gm, sc_m, f"norm_mix_bwd{l}")
        grads["norm_mix_g"][l] = d_gm[0]
        grads["norm_ffn_g"][l] = d_gf[0]
        grads["dmod"][l] = jnp.concatenate([d_sh_m, d_sc_m, d_g_m, d_sh_f, d_sc_f, d_g_f_here], axis=1)[0]

    grad_x = dx.reshape(x.shape)

    small_shapes = dict(dmod=b_ada.shape, norm_mix_g=norm_mix_g.shape, norm_ffn_g=norm_ffn_g.shape,
                        final_norm_g=final_norm_g.shape, b_forget=b_forget.shape, sinks=sinks.shape,
                        rel_bias=rel_bias.shape, loss=())
    mine_small = _pack_small(dict(
        loss=_pad_lanes(loss_tile[0:1, 0:1], 128),
        dmod=jnp.stack(grads["dmod"]), norm_mix_g=jnp.stack(grads["norm_mix_g"]),
        norm_ffn_g=jnp.stack(grads["norm_ffn_g"]), final_norm_g=d_final_g[0],
        b_forget=_pad_lanes(jnp.stack(grads["b_forget"]).reshape(1, -1), 128),
        sinks=_pad_lanes(jnp.stack(grads["sinks"]).reshape(1, -1), 128),
        rel_bias=_pad_lanes(jnp.stack(grads["rel_bias"]).reshape(1, -1), 4224)))
    all_small = _small_all_gather(mine_small, "comm_gather_small").reshape(N_DEV, SMALL_ROWS, 128)
    pieces_in0 = _pair_major(_w_in_pieces(grads["w_in"][0]))
    from_sibling = _sibling_exchange([pieces_in0], "comm_reduce_in0_sibling")[0]
    pair_sum_in0 = _pair_add(pieces_in0, from_sibling, pc.astype(jnp.int32).reshape(1), "pair_add_in0")
    reduce_in0 = _exchange_start([pair_sum_in0], True, all_small, "comm_reduce_in0_start", CHIPS)
    in0_started = reduce_in0[4]

    def pack_params(b_ada_, nm, nf, fn, bf, sk, rb):
        return _pack_small(dict(dmod=b_ada_, norm_mix_g=nm, norm_ffn_g=nf, final_norm_g=fn, loss=jnp.zeros((1, 128), F32),
                                b_forget=_pad_lanes(bf.reshape(1, -1), 128), sinks=_pad_lanes(sk.reshape(1, -1), 128),
                                rel_bias=_pad_lanes(rb.reshape(1, -1), 4224)))

    small_out = _adamw(
        pack_params(b_ada, norm_mix_g, norm_ffn_g, final_norm_g, b_forget, sinks, rel_bias)[None],
        pack_params(m_b_ada, m_norm_mix_g, m_norm_ffn_g, m_final_norm_g, m_b_forget, m_sinks, m_rel_bias)[None],
        pack_params(v_b_ada, v_norm_mix_g, v_norm_ffn_g, v_final_norm_g, v_b_forget, v_sinks, v_rel_bias)[None],
        [all_small], "adamw_small", me_arr, after=in0_started)
    small_out = [_unpack_small(t[0], small_shapes) for t in small_out]

    dmod_all = all_small[:, :96].reshape(N_DEV, depth, 6 * D)
    dmod_cols = lax.dynamic_slice_in_dim(dmod_all, me * w_ada.shape[2], w_ada.shape[2], axis=2)
    d_w_ada = _ada_bwd(jnp.transpose(c_all), jnp.transpose(dmod_cols, (1, 0, 2)), "ada_bwd")

    big = {"w_ada": _adamw(w_ada, m_w_ada, v_w_ada, [d_w_ada[l:l + 1] for l in range(depth)], "adamw_w_ada", me_arr,
                           after=in0_started)}
    sent1, landed1 = _exchange_wait(reduce1, True, big["w_ada"][0], "comm_reduce1_wait")
    sent_rest0, landed_rest0 = _exchange_wait(reduce_rest0, True, landed1[0], "comm_reduce_rest0_wait")
    parts = {"w_in": [None, (landed1[0], sent1[0])]}
    for a, name in enumerate(("w_branch", "w_out", "w_ffn_in", "w_ffn_out")):
        parts[name] = [(landed_rest0[a], sent_rest0[a]), (landed1[1 + a], sent1[1 + a])]

    def update(name, w, m, v, view=lambda t: t):
        per_layer = lambda t: t.reshape(depth, -1, t.shape[-1])
        outs = _adamw(*[per_layer(view(t)) for t in (w, m, v)], parts[name], f"adamw_{name}",
                      me_in_arr if name == "w_in" else me_arr)
        big[name] = [view(t).reshape(w.shape) for t in outs]

    update("w_ffn_in", w_ffn_in, m_w_ffn_in, v_w_ffn_in, _w_ffn_in_view)
    update("w_ffn_out", w_ffn_out, m_w_ffn_out, v_w_ffn_out)
    update("w_branch", w_branch, m_w_branch, v_w_branch)
    update("w_out", w_out, m_w_out, v_w_out)
    sent_in0, landed_in0 = _exchange_wait(reduce_in0, True, big["w_out"][0], "comm_reduce_in0_wait", CHIPS)
    parts["w_in"][0] = (landed_in0[0], sent_in0[0])
    update("w_in", w_in, m_w_in, v_w_in)

    def leaf(kind, name):
        if name in big:
            return big[name][kind]
        return small_out[kind]["dmod" if name == "b_ada" else name]

    order = ["norm_mix_g", "norm_ffn_g", "w_ada", "b_ada", "w_in", "b_forget", "sinks", "rel_bias", "w_branch",
             "w_out", "w_ffn_in", "w_ffn_out", "final_norm_g"]
    loss = small_out[0]["loss"]
    return (loss, grad_x, *[leaf(0, n) for n in order], *[leaf(1, n) for n in order],
            *[leaf(2, n) for n in order], *[leaf(3, n) for n in order])
```

```python
import functools

import jax
import jax.numpy as jnp
from jax import lax
from jax.experimental import pallas as pl
from jax.experimental.pallas import tpu as pltpu

F32 = jnp.float32
BF16 = jnp.bfloat16
NEG_INF = -1e30
EPS = 1e-6
N_DEV = 8
BLK = 128
GROUP = 4 * BLK
VMEM_LIMIT_BYTES = 56 * 1024 * 1024

D_MODEL = 1024
N_QKV = 3840
N_GATES = 3072
N_MAIN = N_QKV + N_GATES
N_FORGET = 8
N_IN = N_MAIN + N_FORGET
F_COL = 2304
FFN_HIDDEN = 2816
N_REL = 257

ADAM_LR, ADAM_B1, ADAM_B2, ADAM_EPS, ADAM_WD, ADAM_STEP = 0.001, 0.9, 0.999, 1e-08, 0.01, 10

NN = (((1,), (0,)), ((), ()))
NT = (((1,), (1,)), ((), ()))
TN = (((0,), (0,)), ((), ()))
HIGHEST = lax.Precision.HIGHEST

ATTN_COLS = {"a": (0, 4, 5), "b": (6, 10, 14), "c": (18, 22, 26)}
ATTN_WINDOW = {"a": 2, "c": 5}
ATTN_BLOCKS_PER_STEP = {"a": 4, "b": GROUP // BLK, "c": 2}


def _params():
    return pltpu.CompilerParams(vmem_limit_bytes=VMEM_LIMIT_BYTES)


def _tile(n, target):
    best = None
    t = 128
    while t <= min(n, target):
        if n % t == 0:
            best = t
        t += 128
    return best if best is not None else n


def _row_tile(n, target):
    t = min(n, target)
    while n % t:
        t -= 8
    return t


TILES = {
    "proj_qkv": (1024, 1280, 1024), "proj_gates": (1024, 768, 1024), "proj_forget": (1024, 128, 1024),
    "ffn_out": (1024, 512, 2816), "ffn_fused": (512, 1408),
    "wgrad_ffn_out": (1408, 1024, 1024), "wgrad_ffn_in": (1408, 1024, 1024), "dgrad_ffn_in": (1024, 1024, 1408),
    "wgrad_out": (1024, 1024, 1024),
    "wgrad_in": (1024, 1408, 1024), "dgrad_in": (1024, 1024, 1408),
}


def _matmul(a, b, mode, out_dtype, name, tiles, *, n=None, a_off=0, b_off=0, m=None, after=None):
    tm, tn, tk = tiles
    if mode == "nn":
        M, K = a.shape if m is None else (m, a.shape[1])
        N = b.shape[1] if n is None else n
    elif mode == "nt":
        M, K = a.shape
        N = b.shape[0] if n is None else n
    else:
        K = a.shape[0]
        M = a.shape[1] if m is None else m
        N = b.shape[1] if n is None else n
    tm = _tile(M, tm) if M % 128 == 0 else M
    tn = _tile(N, tn)
    tk = _tile(K, tk)
    nk = K // tk
    dims = {"nn": NN, "nt": NT, "tn": TN}[mode]
    if mode == "nn":
        a_spec = pl.BlockSpec((tm, tk), lambda i, j, k: (i + a_off, k))
        b_spec = pl.BlockSpec((tk, tn), lambda i, j, k: (k, j + b_off))
    elif mode == "nt":
        a_spec = pl.BlockSpec((tm, tk), lambda i, j, k: (i + a_off, k))
        b_spec = pl.BlockSpec((tn, tk), lambda i, j, k: (j + b_off, k))
    else:
        a_spec = pl.BlockSpec((tk, tm), lambda i, j, k: (k, i + a_off))
        b_spec = pl.BlockSpec((tk, tn), lambda i, j, k: (k, j + b_off))

    def body(a_ref, b_ref, *rest):
        o_ref, acc_ref = rest[-2:]
        k = pl.program_id(2)
        part = lax.dot_general(a_ref[...], b_ref[...], dims, preferred_element_type=F32)
        if nk == 1:
            o_ref[...] = part.astype(o_ref.dtype)
        else:
            @pl.when(k == 0)
            def _():
                acc_ref[...] = part

            @pl.when(k > 0)
            def _():
                acc_ref[...] += part

            @pl.when(k == nk - 1)
            def _():
                o_ref[...] = acc_ref[...].astype(o_ref.dtype)

    return pl.pallas_call(
        body, name=name,
        out_shape=jax.ShapeDtypeStruct((M, N), out_dtype),
        grid=(M // tm, N // tn, nk),
        in_specs=[a_spec, b_spec] + ([ANY] if after is not None else []),
        out_specs=pl.BlockSpec((tm, tn), lambda i, j, k: (i, j)),
        scratch_shapes=[pltpu.VMEM((tm, tn) if nk > 1 else (8, 128), F32)],
        compiler_params=_params(),
    )(a, b, *([after] if after is not None else []))


def _matmul_resid(a, b, resid, gate, name, tiles, after=None):
    M, K = a.shape
    N = b.shape[1]
    tm, tn, tk = (_tile(d, t) for d, t in zip((M, N, K), tiles))
    nk = K // tk

    def body(a_ref, b_ref, r_ref, g_ref, *rest):
        o_ref, s_ref, acc_ref = rest[-3:]
        k = pl.program_id(2)
        part = jnp.dot(a_ref[...], b_ref[...], preferred_element_type=F32)

        def finish(acc):
            o_ref[...] = r_ref[...] + g_ref[...] * acc
            s_ref[...] = acc.astype(BF16)

        if nk == 1:
            finish(part)
        else:
            @pl.when(k == 0)
            def _():
                acc_ref[...] = part

            @pl.when(k > 0)
            def _():
                acc_ref[...] += part

            @pl.when(k == nk - 1)
            def _():
                finish(acc_ref[...])

    return pl.pallas_call(
        body, name=name,
        out_shape=(jax.ShapeDtypeStruct((M, N), F32), jax.ShapeDtypeStruct((M, N), BF16)),
        grid=(M // tm, N // tn, nk),
        in_specs=[pl.BlockSpec((tm, tk), lambda i, j, k: (i, k)),
                  pl.BlockSpec((tk, tn), lambda i, j, k: (k, j)),
                  pl.BlockSpec((tm, tn), lambda i, j, k: (i, j)),
                  pl.BlockSpec((1, tn), lambda i, j, k: (0, j))] + ([ANY] if after is not None else []),
        out_specs=(pl.BlockSpec((tm, tn), lambda i, j, k: (i, j)),
                   pl.BlockSpec((tm, tn), lambda i, j, k: (i, j))),
        scratch_shapes=[pltpu.VMEM((tm, tn) if nk > 1 else (8, 128), F32)],
        compiler_params=_params(),
    )(a, b, resid, gate, *([after] if after is not None else []))


def _norm_mod_fwd(x, g, shift, scale, name):
    S, D = x.shape
    ts = _row_tile(S, 256)

    def body(x_ref, g_ref, sh_ref, sc_ref, h_ref):
        xv = x_ref[...]
        rstd = lax.rsqrt(jnp.mean(xv * xv, axis=-1, keepdims=True) + EPS)
        y = xv * rstd * g_ref[...]
        h_ref[...] = (y * (1.0 + sc_ref[...]) + sh_ref[...]).astype(BF16)

    row = pl.BlockSpec((1, D), lambda i: (0, 0))
    return pl.pallas_call(
        body, name=name, out_shape=jax.ShapeDtypeStruct((S, D), BF16), grid=(S // ts,),
        in_specs=[pl.BlockSpec((ts, D), lambda i: (i, 0)), row, row, row],
        out_specs=pl.BlockSpec((ts, D), lambda i: (i, 0)),
        compiler_params=_params(),
    )(x, g, shift, scale)


def _accumulate_rows(i, pairs):
    @pl.when(i == 0)
    def _():
        for ref, value in pairs:
            ref[...] = value

    @pl.when(i > 0)
    def _():
        for ref, value in pairs:
            ref[...] += value


def _gated_residual_bwd(dx, f_ref, gate_ref, df_ref):
    df_ref[...] = (dx * gate_ref[...]).astype(BF16)
    return jnp.sum(dx * f_ref[...].astype(F32), axis=0, keepdims=True)


def _norm_mod_bwd(x, dh, dres, g, scale, name, below=None):
    S, D = x.shape
    ts = _row_tile(S, 256)

    def body(x_ref, dh_ref, dr_ref, g_ref, sc_ref, *rest):
        i = pl.program_id(0)
        xv, dhv, gv = x_ref[...], dh_ref[...], g_ref[...]
        rstd = lax.rsqrt(jnp.mean(xv * xv, axis=-1, keepdims=True) + EPS)
        xhat = xv * rstd
        dn = dhv * (1.0 + sc_ref[...])
        dxhat = dn * gv
        proj = jnp.mean(dxhat * xhat, axis=-1, keepdims=True)
        dx = dr_ref[...] + rstd * (dxhat - xhat * proj)
        sums = [jnp.sum(dhv, axis=0, keepdims=True), jnp.sum(dhv * (xhat * gv), axis=0, keepdims=True),
                jnp.sum(dn * xhat, axis=0, keepdims=True)]
        if below is None:
            dx_ref, *sum_refs = rest
        else:
            f_ref, gate_ref, dx_ref, *sum_refs, df_ref = rest
            sums.append(_gated_residual_bwd(dx, f_ref, gate_ref, df_ref))
        dx_ref[...] = dx
        _accumulate_rows(i, list(zip(sum_refs, sums)))

    tile = pl.BlockSpec((ts, D), lambda i: (i, 0))
    row = pl.BlockSpec((1, D), lambda i: (0, 0))
    vec = jax.ShapeDtypeStruct((1, D), F32)
    fused = below is not None
    return pl.pallas_call(
        body, name=name,
        out_shape=(jax.ShapeDtypeStruct((S, D), F32), vec, vec, vec)
        + ((vec, jax.ShapeDtypeStruct((S, D), BF16)) if fused else ()),
        grid=(S // ts,),
        in_specs=[tile, tile, tile, row, row] + ([tile, row] if fused else []),
        out_specs=(tile, row, row, row) + ((row, tile) if fused else ()),
        compiler_params=_params(),
    )(x, dh, dres, g, scale, *(below if fused else ()))


def _ffn_in_fwd(h, w_t, name):
    S, D = h.shape
    F = w_t.shape[0] // 2
    tm, tn = _tile(S, TILES["ffn_fused"][0]), _tile(F, TILES["ffn_fused"][1])
    nj = F // tn

    def body(h_ref, wg_ref, wu_ref, o_ref):
        hv = h_ref[...]
        ug = lax.dot_general(hv, wg_ref[...], NT, preferred_element_type=F32)
        uu = lax.dot_general(hv, wu_ref[...], NT, preferred_element_type=F32)
        o_ref[...] = (ug * jax.nn.sigmoid(ug) * uu).astype(BF16)

    return pl.pallas_call(
        body, name=name, out_shape=jax.ShapeDtypeStruct((S, F), BF16), grid=(nj, S // tm),
        in_specs=[pl.BlockSpec((tm, D), lambda j, i: (i, 0)),
                  pl.BlockSpec((tn, D), lambda j, i: (j, 0)),
                  pl.BlockSpec((tn, D), lambda j, i: (j + nj, 0))],
        out_specs=pl.BlockSpec((tm, tn), lambda j, i: (i, j)),
        compiler_params=_params(),
    )(h, w_t, w_t)


def _ffn_mid_bwd(h, df, w_in_t, w_out, name):
    S, D = h.shape
    F = w_in_t.shape[0] // 2
    tm, tn = _tile(S, TILES["ffn_fused"][0]), _tile(F, TILES["ffn_fused"][1])
    nj = F // tn

    def body(h_ref, df_ref, wg_ref, wu_ref, wo_ref, dg_ref, du_ref):
        hv = h_ref[...]
        ug = lax.dot_general(hv, wg_ref[...], NT, preferred_element_type=F32)
        uu = lax.dot_general(hv, wu_ref[...], NT, preferred_element_type=F32)
        dact = lax.dot_general(df_ref[...], wo_ref[...], NT, preferred_element_type=F32)
        sig = jax.nn.sigmoid(ug)
        dg_ref[...] = (dact * uu * (sig * (1.0 + ug * (1.0 - sig)))).astype(BF16)
        du_ref[...] = (dact * (ug * sig)).astype(BF16)

    out = jax.ShapeDtypeStruct((S, F), BF16)
    return pl.pallas_call(
        body, name=name, out_shape=(out, out), grid=(nj, S // tm),
        in_specs=[pl.BlockSpec((tm, D), lambda j, i: (i, 0)),
                  pl.BlockSpec((tm, D), lambda j, i: (i, 0)),
                  pl.BlockSpec((tn, D), lambda j, i: (j, 0)),
                  pl.BlockSpec((tn, D), lambda j, i: (j + nj, 0)),
                  pl.BlockSpec((tn, D), lambda j, i: (j, 0))],
        out_specs=(pl.BlockSpec((tm, tn), lambda j, i: (i, j)), pl.BlockSpec((tm, tn), lambda j, i: (i, j))),
        compiler_params=_params(),
    )(h, df, w_in_t, w_in_t, w_out)


def _merge_fwd(o_a, o_b, o_c, gates, w_branch, w_out, resid, gate, name, *, tm=512):
    S, W = o_a.shape
    D = w_branch.shape[2]
    tm = _row_tile(S, tm)

    def body(oa_ref, ob_ref, oc_ref, g_ref, w_ref, wo_ref, r_ref, gm_ref, x_ref, m_ref, mix_ref):
        acc = None
        for k, o_ref in enumerate((oa_ref, ob_ref, oc_ref)):
            y = jnp.dot(o_ref[...], w_ref[k], preferred_element_type=F32)
            t = jax.nn.sigmoid(g_ref[:, k * D:(k + 1) * D]) * y
            acc = t if acc is None else acc + t
        merged = acc.astype(BF16)
        m_ref[...] = merged
        mix = jnp.dot(merged, wo_ref[...], preferred_element_type=F32)
        x_ref[...] = r_ref[...] + gm_ref[...] * mix
        mix_ref[...] = mix.astype(BF16)

    o_spec = pl.BlockSpec((tm, W), lambda i: (i, 0))
    tile = pl.BlockSpec((tm, D), lambda i: (i, 0))
    return pl.pallas_call(
        body, name=name,
        out_shape=(jax.ShapeDtypeStruct((S, D), F32), jax.ShapeDtypeStruct((S, D), BF16), jax.ShapeDtypeStruct((S, D), BF16)),
        grid=(S // tm,),
        in_specs=[o_spec, o_spec, o_spec, pl.BlockSpec((tm, 3 * D), lambda i: (i, 0)),
                  pl.BlockSpec((3, W, D), lambda i: (0, 0, 0)), pl.BlockSpec((D, D), lambda i: (0, 0)),
                  tile, pl.BlockSpec((1, D), lambda i: (0, 0))],
        out_specs=(tile, tile, tile),
        compiler_params=_params(),
    )(o_a, o_b, o_c, gates, w_branch, w_out, resid, gate)


def _merge_bwd(dmix, o_a, o_b, o_c, gates, w_branch, w_out, name, *, tm=256):
    S, W = o_a.shape
    D = w_branch.shape[2]
    tm = _row_tile(S, tm)
    n_heads = W // 64

    def body(dm_ref, oa_ref, ob_ref, oc_ref, g_ref, w_ref, wo_ref, dg_ref, dw_ref,
             doa_ref, dob_ref, doc_ref, dla_ref, dlb_ref, dlc_ref):
        first = pl.program_id(0) == 0
        dm = lax.dot_general(dm_ref[...], wo_ref[...], NT, preferred_element_type=F32)
        branches = ((oa_ref, doa_ref, dla_ref), (ob_ref, dob_ref, dlb_ref), (oc_ref, doc_ref, dlc_ref))
        for k, (o_ref, do_ref, dl_ref) in enumerate(branches):
            wk = w_ref[k]
            ov = o_ref[...]
            y = jnp.dot(ov, wk, preferred_element_type=F32)
            g = jax.nn.sigmoid(g_ref[:, k * D:(k + 1) * D])
            dy = (dm * g).astype(BF16)
            dwk = lax.dot_general(ov, dy, TN, preferred_element_type=F32)

            @pl.when(first)
            def _(k=k, dwk=dwk):
                dw_ref[k] = dwk

            @pl.when(jnp.logical_not(first))
            def _(k=k, dwk=dwk):
                dw_ref[k] += dwk
            dg_ref[:, k * D:(k + 1) * D] = (dm * y * (g * (1.0 - g))).astype(BF16)
            do16 = lax.dot_general(dy, wk, NT, preferred_element_type=F32).astype(BF16)
            do_ref[...] = do16
            prod = do16.astype(F32) * ov.astype(F32)
            for h in range(n_heads):
                dl_ref[:, h:h + 1] = jnp.sum(prod[:, 64 * h:64 * (h + 1)], axis=1, keepdims=True)

    o_spec = pl.BlockSpec((tm, W), lambda i: (i, 0))
    wide = pl.BlockSpec((tm, 3 * D), lambda i: (i, 0))
    dl_spec = pl.BlockSpec((tm, n_heads), lambda i: (i, 0))
    o_out = jax.ShapeDtypeStruct((S, W), BF16)
    wide_out = jax.ShapeDtypeStruct((S, 3 * D), BF16)
    dl_out = jax.ShapeDtypeStruct((S, n_heads), F32)
    whole = pl.BlockSpec((3, W, D), lambda i: (0, 0, 0))
    return pl.pallas_call(
        body, name=name,
        out_shape=(wide_out, jax.ShapeDtypeStruct((3, W, D), F32), o_out, o_out, o_out, dl_out, dl_out, dl_out),
        grid=(S // tm,),
        in_specs=[pl.BlockSpec((tm, D), lambda i: (i, 0)), o_spec, o_spec, o_spec, wide, whole,
                  pl.BlockSpec((D, D), lambda i: (0, 0))],
        out_specs=(wide, whole, o_spec, o_spec, o_spec, dl_spec, dl_spec, dl_spec),
        compiler_params=_params(),
    )(dmix, o_a, o_b, o_c, gates, w_branch, w_out)


def _band_mask(variant, t_abs, s_abs):
    if variant == "b":
        return s_abs <= t_abs
    qc, kc = t_abs >> 6, s_abs >> 6
    return (kc <= qc) & (kc >= qc - (2 if variant == "a" else 8))


def _attn_fwd(variant, qkv, name, *, sinks=None, slopes=None, cq_col=None, ck_row=None, bias=None, after=None):
    S = qkv.shape[0]
    nb = S // BLK
    qb, kb, vb = ATTN_COLS[variant]
    shared_kv = variant == "a"
    win = ATTN_WINDOW.get(variant)
    per_step = ATTN_BLOCKS_PER_STEP[variant]

    def body(*refs):
        if after is not None:
            refs = refs[:-3] + refs[-2:]
        if variant == "a":
            q_ref, k_ref, v_ref, sink_ref, slope_ref, o_ref, lse_ref = refs
        elif variant == "b":
            q_ref, k_ref, v_ref, cq_ref, ck_ref, o_ref, lse_ref = refs
        else:
            q_ref, k_ref, v_ref, bias_ref, o_ref, lse_ref = refs
        p = pl.program_id(0)
        lane = lax.broadcasted_iota(jnp.int32, (1, BLK), 1)

        def compute(i, rows, start, n_keys):
            n_rows = rows.stop - rows.start
            t_abs = i * BLK + lax.broadcasted_iota(jnp.int32, (n_rows, 1), 0)
            q2 = q_ref[rows, :].astype(F32) * 0.125
            k_w = k_ref[pl.ds(start, n_keys), :]
            v_w = v_ref[pl.ds(start, n_keys), :]
            s_abs = start + lax.broadcasted_iota(jnp.int32, (1, n_keys), 1)
            valid = _band_mask(variant, t_abs, s_abs)
            outs = []
            for half in (0, 1):
                hmask = (lane >= 64) if half else (lane < 64)
                qh = jnp.where(hmask, q2, 0.0)
                if shared_kv:
                    swap = (p // 2) != half
                    qh = jnp.where(swap, pltpu.roll(qh, 64, 1), qh)
                s = lax.dot_general(qh.astype(BF16), k_w, NT, preferred_element_type=F32)
                if variant == "a":
                    head = 2 * p + half
                    s = s + (-slope_ref[head]) * jnp.abs(t_abs - s_abs).astype(F32)
                elif variant == "b":
                    s = s + cq_ref[rows, half:half + 1] - ck_ref[half:half + 1, pl.ds(start, n_keys)]
                else:
                    j0 = start // BLK
                    s = s + jnp.concatenate([jnp.concatenate(
                        [bias_ref[half, jnp.clip(i + r - j0 - b, 0, 4)] for b in range(n_keys // BLK)], axis=1)
                        for r in range(n_rows // BLK)], axis=0)
                s = jnp.where(valid, s, NEG_INF)
                m = jnp.max(s, axis=1, keepdims=True)
                if variant == "a":
                    m = jnp.maximum(m, sink_ref[head])
                pe = jnp.exp(s - m)
                l = jnp.sum(pe, axis=1, keepdims=True)
                if variant == "a":
                    l = l + jnp.exp(sink_ref[head] - m)
                out = jnp.dot(pe.astype(BF16), v_w, preferred_element_type=F32) / l
                if shared_kv:
                    out = jnp.where(swap, pltpu.roll(out, 64, 1), out)
                outs.append(out)
                lse_ref[rows, half:half + 1] = m + jnp.log(l)
            o_ref[rows, :] = jnp.where(lane < 64, outs[0], outs[1]).astype(BF16)

        step = pl.program_id(1)
        if variant == "b":
            for g in range(S // GROUP):
                pl.when(step == g)(functools.partial(compute, step * per_step, slice(0, GROUP), 0, (g + 1) * GROUP))
        elif variant == "c":
            span = win + per_step - 1
            start = jnp.clip(step * per_step - (win - 1), 0, nb - span) * BLK
            compute(step * per_step, slice(0, per_step * BLK), pl.multiple_of(start, BLK), span * BLK)
        else:
            for sub in range(per_step):
                i = step * per_step + sub
                start = jnp.clip(i - (win - 1), 0, nb - win) * BLK
                compute(i, slice(sub * BLK, (sub + 1) * BLK), pl.multiple_of(start, BLK), win * BLK)

    tq = per_step * BLK
    kv_col = (lambda p, i: (0, kb)) if shared_kv else (lambda p, i: (0, kb + p))
    vv_col = (lambda p, i: (0, vb)) if shared_kv else (lambda p, i: (0, vb + p))
    in_specs = [pl.BlockSpec((tq, BLK), lambda p, i: (i, qb + p)),
                pl.BlockSpec((S, BLK), kv_col), pl.BlockSpec((S, BLK), vv_col)]
    args = [qkv, qkv, qkv]
    if variant == "a":
        in_specs += [pl.BlockSpec(memory_space=pltpu.SMEM), pl.BlockSpec(memory_space=pltpu.SMEM)]
        args += [sinks, slopes]
    elif variant == "b":
        in_specs += [pl.BlockSpec((None, tq, 2), lambda p, i: (p, i, 0)),
                     pl.BlockSpec((None, 2, S), lambda p, i: (p, 0, 0))]
        args += [cq_col, ck_row]
    else:
        in_specs += [pl.BlockSpec((2, 5, BLK, BLK), lambda p, i: (p, 0, 0, 0))]
        args += [bias]
    if after is not None:
        in_specs.append(ANY)
        args.append(after)
    return pl.pallas_call(
        body, name=name,
        out_shape=(jax.ShapeDtypeStruct((S, 512), BF16), jax.ShapeDtypeStruct((4, S, 2), F32)),
        grid=(4, nb // per_step), in_specs=in_specs,
        out_specs=(pl.BlockSpec((tq, BLK), lambda p, i: (i, p)),
                   pl.BlockSpec((None, tq, 2), lambda p, i: (p, i, 0))),
        compiler_params=_params(),
    )(*args)


def _attn_bwd(variant, qkv, do, lse_row, delta_row, name, *, sinks=None, slopes=None, cq_row=None,
              ck_col=None, bias_t=None):
    S = qkv.shape[0]
    nb = S // BLK
    qb, kb, vb = ATTN_COLS[variant]
    shared_kv = variant == "a"
    win = ATTN_WINDOW.get(variant)
    per_step = ATTN_BLOCKS_PER_STEP[variant]

    def body(*refs):
        *refs, dqt_ref = refs
        if variant == "a":
            (q_ref, k_ref, v_ref, do_ref, lse_ref, dl_ref, sink_ref, slope_ref,
             dq_ref, dk_ref, dv_ref, ex_ref) = refs
        elif variant == "b":
            (q_ref, k_ref, v_ref, do_ref, lse_ref, dl_ref, cq_ref, ck_ref,
             dq_ref, dk_ref, dv_ref, ex_ref, dcq_ref) = refs
        else:
            (q_ref, k_ref, v_ref, do_ref, lse_ref, dl_ref, bias_ref,
             dq_ref, dk_ref, dv_ref, ex_ref) = refs
        p = pl.program_id(0)
        lane = lax.broadcasted_iota(jnp.int32, (1, BLK), 1)
        hmasks = [(lane < 64), (lane >= 64)]
        swaps = [(p // 2) != half for half in (0, 1)] if shared_kv else None

        @pl.when(pl.program_id(1) == 0)
        def _():
            dqt_ref[...] = jnp.zeros_like(dqt_ref)
            if variant == "b":
                dcq_ref[...] = jnp.zeros_like(dcq_ref)
            else:
                ex_ref[...] = jnp.zeros_like(ex_ref)

        def to_kv_lanes(x, h):
            x = jnp.where(hmasks[h], x, 0.0)
            if shared_kv:
                x = jnp.where(swaps[h], pltpu.roll(x, 64, 1), x)
            return x

        def compute(j, rows, start, n_q):
            n_rows = rows.stop - rows.start
            s_abs = j * BLK + lax.broadcasted_iota(jnp.int32, (n_rows, 1), 0)
            off_k = pl.multiple_of(j * BLK, BLK)
            k2 = k_ref[rows, :].astype(F32)
            v2 = v_ref[rows, :].astype(F32)
            if shared_kv:
                kv_lane = (lane >> 6) == (p // 2)
                k_src, v_src = jnp.where(kv_lane, k2, 0.0), jnp.where(kv_lane, v2, 0.0)
                k_al = [jnp.where(swaps[h], pltpu.roll(k_src, 64, 1), k_src) for h in (0, 1)]
                v_al = [jnp.where(swaps[h], pltpu.roll(v_src, 64, 1), v_src) for h in (0, 1)]
            else:
                k_al = [jnp.where(hmasks[h], k2, 0.0) for h in (0, 1)]
                v_al = [jnp.where(hmasks[h], v2, 0.0) for h in (0, 1)]
            k_al = [(t * 0.125).astype(BF16) for t in k_al]
            v_al = [t.astype(BF16) for t in v_al]
            q_w = q_ref[pl.ds(start, n_q), :]
            do_w = do_ref[pl.ds(start, n_q), :]
            t_abs = start + lax.broadcasted_iota(jnp.int32, (1, n_q), 1)
            valid = _band_mask(variant, t_abs, s_abs)
            dk_acc = dv_acc = None
            ds_both = []
            for half in (0, 1):
                s = lax.dot_general(k_al[half], q_w, NT, preferred_element_type=F32)
                if variant == "a":
                    s = s + (-slope_ref[2 * p + half]) * jnp.abs(t_abs - s_abs).astype(F32)
                elif variant == "b":
                    s = s + cq_ref[half:half + 1, pl.ds(start, n_q)] - ck_ref[rows, half:half + 1]
                else:
                    i0 = start // BLK
                    s = s + jnp.concatenate([jnp.concatenate(
                        [bias_ref[half, jnp.clip(i0 + b - j - r, 0, 4)] for b in range(n_q // BLK)], axis=1)
                        for r in range(n_rows // BLK)], axis=0)
                pr = jnp.where(valid, jnp.exp(s - lse_ref[half:half + 1, pl.ds(start, n_q)]), 0.0)
                dp = lax.dot_general(v_al[half], do_w, NT, preferred_element_type=F32)
                ds = pr * (dp - dl_ref[half:half + 1, pl.ds(start, n_q)])
                ds16 = ds.astype(BF16)
                dv_h = to_kv_lanes(jnp.dot(pr.astype(BF16), do_w, preferred_element_type=F32), half)
                dk_h = to_kv_lanes(jnp.dot(ds16, q_w, preferred_element_type=F32) * 0.125, half)
                dv_acc = dv_h if dv_acc is None else dv_acc + dv_h
                dk_acc = dk_h if dk_acc is None else dk_acc + dk_h
                ds_both.append(ds16)
                if variant == "b":
                    ex_ref[rows, half:half + 1] = -jnp.sum(ds, axis=1, keepdims=True)
                    dcq_ref[half:half + 1, pl.ds(start, n_q)] += jnp.sum(ds, axis=0, keepdims=True)
                elif variant == "c":
                    for r in range(n_rows // BLK):
                        for b in range(n_q // BLK):
                            ex_ref[half, jnp.clip(i0 + b - j - r, 0, 4)] += ds[r * BLK:(r + 1) * BLK, b * BLK:(b + 1) * BLK]
            dq_t = lax.dot_general(jnp.concatenate(k_al, axis=0), jnp.concatenate(ds_both, axis=0), TN,
                                   preferred_element_type=F32)
            dqt_ref[:, pl.ds(start, n_q)] += dq_t
            if shared_kv:
                @pl.when(p == 0)
                def _():
                    dk_ref[pl.ds(off_k, n_rows), :] = dk_acc
                    dv_ref[pl.ds(off_k, n_rows), :] = dv_acc

                @pl.when(p > 0)
                def _():
                    dk_ref[pl.ds(off_k, n_rows), :] += dk_acc
                    dv_ref[pl.ds(off_k, n_rows), :] += dv_acc
            else:
                dk_ref[pl.ds(off_k, n_rows), :] = dk_acc.astype(dk_ref.dtype)
                dv_ref[pl.ds(off_k, n_rows), :] = dv_acc.astype(dv_ref.dtype)
            if variant == "a":
                for half in (0, 1):
                    p_sink = jnp.exp(sink_ref[2 * p + half] - lse_ref[half:half + 1, pl.ds(off_k, n_rows)])
                    term = p_sink * dl_ref[half:half + 1, pl.ds(off_k, n_rows)]
                    ex_ref[half:half + 1, :] += -jnp.sum(term, axis=1, keepdims=True)

        step = pl.program_id(1)
        if variant == "b":
            for g in range(S // GROUP):
                pl.when(step == g)(functools.partial(compute, step * per_step, slice(0, GROUP), g * GROUP, S - g * GROUP))
        elif variant == "c":
            span = win + per_step - 1
            start = jnp.clip(step * per_step, 0, nb - span) * BLK
            compute(step * per_step, slice(0, per_step * BLK), pl.multiple_of(start, BLK), span * BLK)
        else:
            for sub in range(per_step):
                j = step * per_step + sub
                start = jnp.clip(j, 0, nb - win) * BLK
                compute(j, slice(sub * BLK, (sub + 1) * BLK), pl.multiple_of(start, BLK), win * BLK)

        @pl.when(step == nb // per_step - 1)
        def _():
            dq_ref[...] = jnp.transpose(dqt_ref[...]).astype(BF16)

    tk = per_step * BLK
    col = lambda c0: (lambda p, j: (0, c0 + p))
    kv_blk = (lambda c0: (lambda p, j: (j, c0))) if shared_kv else (lambda c0: (lambda p, j: (j, c0 + p)))
    pair = lambda p, j: (0, p)
    row_stat = pl.BlockSpec((None, 2, S), lambda p, j: (p, 0, 0))
    in_specs = [pl.BlockSpec((S, BLK), col(qb)),
                pl.BlockSpec((tk, BLK), kv_blk(kb)), pl.BlockSpec((tk, BLK), kv_blk(vb)),
                pl.BlockSpec((S, BLK), pair), row_stat, row_stat]
    args = [qkv, qkv, qkv, do, lse_row, delta_row]
    kv_width = BLK if shared_kv else 512
    kv_out = pl.BlockSpec((S, BLK), (lambda p, j: (0, 0)) if shared_kv else pair)
    kv_dtype = F32 if shared_kv else BF16
    out_shape = [jax.ShapeDtypeStruct((S, 512), BF16), jax.ShapeDtypeStruct((S, kv_width), kv_dtype),
                 jax.ShapeDtypeStruct((S, kv_width), kv_dtype)]
    out_specs = [pl.BlockSpec((S, BLK), pair), kv_out, kv_out]
    if variant == "a":
        in_specs += [pl.BlockSpec(memory_space=pltpu.SMEM), pl.BlockSpec(memory_space=pltpu.SMEM)]
        args += [sinks, slopes]
        out_shape.append(jax.ShapeDtypeStruct((4, 8, BLK), F32))
        out_specs.append(pl.BlockSpec((None, 8, BLK), lambda p, j: (p, 0, 0)))
    elif variant == "b":
        in_specs += [row_stat, pl.BlockSpec((None, tk, 2), lambda p, j: (p, j, 0))]
        args += [cq_row, ck_col]
        out_shape += [jax.ShapeDtypeStruct((4, S, 2), F32), jax.ShapeDtypeStruct((4, 2, S), F32)]
        out_specs += [pl.BlockSpec((None, tk, 2), lambda p, j: (p, j, 0)), row_stat]
    else:
        in_specs += [pl.BlockSpec((2, 5, BLK, BLK), lambda p, j: (p, 0, 0, 0))]
        args += [bias_t]
        out_shape.append(jax.ShapeDtypeStruct((8, 5, BLK, BLK), F32))
        out_specs.append(pl.BlockSpec((2, 5, BLK, BLK), lambda p, j: (p, 0, 0, 0)))
    return pl.pallas_call(
        body, name=name, out_shape=tuple(out_shape), grid=(4, nb // per_step),
        in_specs=in_specs, out_specs=tuple(out_specs), scratch_shapes=[pltpu.VMEM((BLK, S), F32)],
        compiler_params=_params(),
    )(*args)


def _log_sigmoid(x):
    return jnp.minimum(x, 0.0) - jnp.log(1.0 + jnp.exp(-jnp.abs(x)))


def _forget_fwd(fb, b_forget, name):
    S = fb.shape[0]
    nb = S // GROUP

    def body(fb_ref, b_ref, cum_ref, carry_ref):
        i = pl.program_id(0)
        logf = _log_sigmoid(fb_ref[...] + b_ref[...])
        r = lax.broadcasted_iota(jnp.int32, (GROUP, GROUP), 0)
        c = lax.broadcasted_iota(jnp.int32, (GROUP, GROUP), 1)
        tri = (c <= r).astype(F32)

        @pl.when(i == 0)
        def _():
            carry_ref[...] = jnp.zeros_like(carry_ref)

        cum = jnp.dot(tri, logf, preferred_element_type=F32, precision=HIGHEST) + carry_ref[0:1, :]
        cum_ref[...] = cum
        carry_ref[...] = jnp.broadcast_to(cum[GROUP - 1:GROUP, :], carry_ref.shape)

    return pl.pallas_call(
        body, name=name, out_shape=jax.ShapeDtypeStruct((S, BLK), F32), grid=(nb,),
        in_specs=[pl.BlockSpec((GROUP, BLK), lambda i: (i, 0)), pl.BlockSpec((1, BLK), lambda i: (0, 0))],
        out_specs=pl.BlockSpec((GROUP, BLK), lambda i: (i, 0)),
        scratch_shapes=[pltpu.VMEM((8, BLK), F32)],
        compiler_params=_params(),
    )(fb, b_forget)


def _forget_bwd(dcum_q, dcum_k, fb, b_forget, name):
    S = fb.shape[0]
    nb = S // GROUP

    def body(dq_ref, dk_ref, fb_ref, b_ref, dfb_ref, db_ref, carry_ref):
        g = pl.program_id(0)
        r = lax.broadcasted_iota(jnp.int32, (GROUP, GROUP), 0)
        c = lax.broadcasted_iota(jnp.int32, (GROUP, GROUP), 1)
        tri = (c >= r).astype(F32)

        @pl.when(g == 0)
        def _():
            carry_ref[...] = jnp.zeros_like(carry_ref)

        dcum = dq_ref[...] + dk_ref[...]
        dlogf = jnp.dot(tri, dcum, preferred_element_type=F32, precision=HIGHEST) + carry_ref[0:1, :]
        carry_ref[...] = jnp.broadcast_to(dlogf[0:1, :], carry_ref.shape)
        x = fb_ref[...] + b_ref[...]
        lane = lax.broadcasted_iota(jnp.int32, (1, BLK), 1)
        dfb = jnp.where(lane < N_FORGET, dlogf * jax.nn.sigmoid(-x), 0.0)
        dfb_ref[...] = dfb
        db = jnp.sum(dfb, axis=0, keepdims=True)

        @pl.when(g == 0)
        def _():
            db_ref[...] = db

        @pl.when(g > 0)
        def _():
            db_ref[...] += db

    rev = pl.BlockSpec((GROUP, BLK), lambda g: (nb - 1 - g, 0))
    row = pl.BlockSpec((1, BLK), lambda g: (0, 0))
    return pl.pallas_call(
        body, name=name,
        out_shape=(jax.ShapeDtypeStruct((S, BLK), F32), jax.ShapeDtypeStruct((1, BLK), F32)), grid=(nb,),
        in_specs=[rev, rev, rev, row], out_specs=(rev, row),
        scratch_shapes=[pltpu.VMEM((8, BLK), F32)],
        compiler_params=_params(),
    )(dcum_q, dcum_k, fb, b_forget)


def _skew(x, sign):
    row = lax.broadcasted_iota(jnp.int32, x.shape, 0)
    for b in range(7):
        amount = (1 << b) if sign > 0 else 256 - (1 << b)
        x = jnp.where(((row >> b) & 1) == 1, pltpu.roll(x, amount, 1), x)
    return x


def _rel_bases(rel):
    far = rel[:, 256:257]
    far127 = jnp.broadcast_to(far, (rel.shape[0], 127))
    base0 = jnp.concatenate([rel[:, 128:0:-1], far, rel[:, 255:128:-1]], axis=1)
    base1 = jnp.concatenate([rel[:, 256:128:-1], far, far127], axis=1)
    base0_t = jnp.concatenate([rel[:, 128:256], far, rel[:, 1:128]], axis=1)
    base1_t = jnp.concatenate([jnp.broadcast_to(far, (rel.shape[0], 128)), far, rel[:, 129:256]], axis=1)
    return jnp.stack([base0, base1, base0_t, base1_t], axis=1)


def _rel_expand(bases, name):
    def body(b_ref, t_ref, tt_ref):
        far = jnp.broadcast_to(b_ref[1:2, 0:1], (BLK, BLK))
        for k, out_ref in ((0, t_ref), (2, tt_ref)):
            for d in (0, 1):
                x = jnp.broadcast_to(b_ref[k + d:k + d + 1, :], (BLK, 2 * BLK))
                out_ref[d] = _skew(x, 1)[:, :BLK]
            for d in (2, 3, 4):
                out_ref[d] = far

    out = jax.ShapeDtypeStruct((8, 5, BLK, BLK), F32)
    spec = pl.BlockSpec((None, 5, BLK, BLK), lambda h: (h, 0, 0, 0))
    return pl.pallas_call(
        body, name=name, out_shape=(out, out), grid=(8,),
        in_specs=[pl.BlockSpec((None, 4, 2 * BLK), lambda h: (h, 0, 0))], out_specs=(spec, spec),
        compiler_params=_params(),
    )(bases)


def _rel_reduce(dtiles_t, name):
    def body(dt_ref, o_ref):
        zeros = jnp.zeros((BLK, BLK), F32)
        sums = []
        for d in (0, 1):
            x = _skew(jnp.concatenate([dt_ref[d], zeros], axis=1), -1)
            sums.append(jnp.broadcast_to(jnp.sum(x, axis=0, keepdims=True), (8, 2 * BLK)))
        lane = lax.broadcasted_iota(jnp.int32, (8, 2 * BLK), 1)
        main = pltpu.roll(sums[0], BLK, 1) + jnp.where(lane > BLK, sums[1], 0.0)
        far = jnp.sum(jnp.where(lane < BLK, sums[1], 0.0)[0:1], axis=1, keepdims=True)
        far = far + jnp.sum(jnp.sum(dt_ref[2] + dt_ref[3] + dt_ref[4], axis=0, keepdims=True), axis=1, keepdims=True)
        o_ref[...] = jnp.concatenate([main[0:1], jnp.broadcast_to(far, (1, BLK))], axis=1)

    return pl.pallas_call(
        body, name=name, out_shape=jax.ShapeDtypeStruct((8, 1, 3 * BLK), F32), grid=(8,),
        in_specs=[pl.BlockSpec((None, 5, BLK, BLK), lambda h: (h, 0, 0, 0))],
        out_specs=pl.BlockSpec((None, 1, 3 * BLK), lambda h: (h, 0, 0)),
        compiler_params=_params(),
    )(dtiles_t)


def _final_loss(x, target, g, below, name):
    S, D = x.shape
    ts = _row_tile(S, 256)

    def body(x_ref, t_ref, g_ref, f_ref, gate_ref, dx_ref, loss_ref, dg_ref, dgate_ref, df_ref):
        i = pl.program_id(0)
        xv, gv = x_ref[...], g_ref[...]
        rstd = lax.rsqrt(jnp.mean(xv * xv, axis=-1, keepdims=True) + EPS)
        xhat = xv * rstd
        err = xhat * gv - t_ref[...]
        part = 0.5 * jnp.sum(jnp.mean(err * err, axis=-1, keepdims=True), axis=0, keepdims=True)
        dy = err / D
        dg = jnp.sum(dy * xhat, axis=0, keepdims=True)
        dxhat = dy * gv
        proj = jnp.mean(dxhat * xhat, axis=-1, keepdims=True)
        dx = rstd * (dxhat - xhat * proj)
        dx_ref[...] = dx
        dgate = _gated_residual_bwd(dx, f_ref, gate_ref, df_ref)
        _accumulate_rows(i, [(loss_ref, jnp.broadcast_to(part, loss_ref.shape)), (dg_ref, dg), (dgate_ref, dgate)])

    tile = pl.BlockSpec((ts, D), lambda i: (i, 0))
    row = pl.BlockSpec((1, D), lambda i: (0, 0))
    vec = jax.ShapeDtypeStruct((1, D), F32)
    return pl.pallas_call(
        body, name=name,
        out_shape=(jax.ShapeDtypeStruct((S, D), F32), jax.ShapeDtypeStruct((8, 128), F32), vec, vec,
                   jax.ShapeDtypeStruct((S, D), BF16)),
        grid=(S // ts,), in_specs=[tile, tile, row, tile, row],
        out_specs=(tile, pl.BlockSpec((8, 128), lambda i: (0, 0)), row, row, tile),
        compiler_params=_params(),
    )(x, target, g, *below)


def _ada_fwd(c_all, w_ada, name):
    L, D, E = w_ada.shape

    def body(c_ref, w_ref, o_ref):
        cv = c_ref[...]
        cond = cv * jax.nn.sigmoid(cv)
        o_ref[...] = jnp.dot(cond, w_ref[...], preferred_element_type=F32, precision=HIGHEST)

    return pl.pallas_call(
        body, name=name, out_shape=jax.ShapeDtypeStruct((L, N_DEV, E), F32), grid=(L,),
        in_specs=[pl.BlockSpec((N_DEV, D), lambda l: (0, 0)), pl.BlockSpec((None, D, E), lambda l: (l, 0, 0))],
        out_specs=pl.BlockSpec((None, N_DEV, E), lambda l: (l, 0, 0)),
        compiler_params=_params(),
    )(c_all, w_ada)


def _ada_bwd(c_all_t, dmod, name):
    D = c_all_t.shape[0]
    L, _, E = dmod.shape

    def body(c_ref, d_ref, o_ref):
        cv = c_ref[...]
        cond = cv * jax.nn.sigmoid(cv)
        acc = None
        for b in range(N_DEV):
            t = cond[:, b:b + 1] * d_ref[b:b + 1, :]
            acc = t if acc is None else acc + t
        o_ref[...] = acc

    return pl.pallas_call(
        body, name=name, out_shape=jax.ShapeDtypeStruct((L, D, E), F32), grid=(L,),
        in_specs=[pl.BlockSpec((D, N_DEV), lambda l: (0, 0)), pl.BlockSpec((None, N_DEV, E), lambda l: (l, 0, 0))],
        out_specs=pl.BlockSpec((None, D, E), lambda l: (l, 0, 0)),
        compiler_params=_params(),
    )(c_all_t, dmod)


def _adamw(w, m, v, g_parts, name, me, after=None):
    L, R, C = w.shape
    tr = _row_tile(R, max(8, (256 * 1024 // max(C, 128)) // 8 * 8))
    nr = R // tr
    c1 = 1.0 - ADAM_B1 ** ADAM_STEP
    c2 = 1.0 - ADAM_B2 ** ADAM_STEP
    direct = [isinstance(p, tuple) for p in g_parts]
    n_in = sum(2 if d else 1 for d in direct)

    def body(me_ref, w_ref, m_ref, v_ref, *rest):
        g_refs, (go_ref, d_ref, mo_ref, vo_ref) = list(rest[:n_in]), rest[-4:]
        layer = pl.program_id(0)
        g = None
        for l in range(L):
            land_ref = g_refs.pop(0)
            own = g_refs.pop(0)[...].astype(F32) if direct[l] else None
            gl = None
            for k in range(land_ref.shape[0]):
                part = land_ref[k].astype(F32)
                if direct[l]:
                    part = jnp.where(me_ref[l] == k, own, part)
                gl = part if gl is None else gl + part
            g = gl if g is None else jnp.where(layer == l, gl, g)
        mn = ADAM_B1 * m_ref[...] + (1.0 - ADAM_B1) * g
        vn = ADAM_B2 * v_ref[...] + (1.0 - ADAM_B2) * (g * g)
        m_hat = mn / c1
        v_hat = vn / c2
        go_ref[...] = g
        d_ref[...] = -ADAM_LR * (m_hat / (jnp.sqrt(v_hat) + ADAM_EPS) + ADAM_WD * w_ref[...])
        mo_ref[...] = mn
        vo_ref[...] = vn

    def rows(l, layer, i):
        return jnp.where(layer == l, i, 0 if l > 0 else nr - 1)

    in_specs, operands = [], []
    for l, p in enumerate(g_parts):
        land, sent = p if direct[l] else (p, None)
        in_specs.append(pl.BlockSpec((land.shape[0], tr, C), lambda layer, i, me_ref, l=l: (0, rows(l, layer, i), 0)))
        operands.append(land)
        if direct[l]:
            in_specs.append(pl.BlockSpec((None, tr, C), lambda layer, i, me_ref, l=l: (me_ref[l], rows(l, layer, i), 0)))
            operands.append(sent)
    if after is not None:
        in_specs.append(ANY)
        operands.append(after)
    tile = pl.BlockSpec((None, tr, C), lambda layer, i, me_ref: (layer, i, 0))
    out = jax.ShapeDtypeStruct((L, R, C), F32)
    return pl.pallas_call(
        body, name=name, out_shape=(out, out, out, out),
        grid_spec=pltpu.PrefetchScalarGridSpec(
            num_scalar_prefetch=1, grid=(L, nr), in_specs=[tile, tile, tile] + in_specs,
            out_specs=(tile, tile, tile, tile)),
        compiler_params=_params(),
    )(me, w, m, v, *operands)


def _pair_add(pieces, recv, core, name):
    _, _, R, C = pieces.shape
    tr = _row_tile(R, max(8, (512 * 1024 // max(C, 128)) // 8 * 8))

    def body(core_ref, a_ref, b_ref, o_ref):
        o_ref[...] = (a_ref[...].astype(F32) + b_ref[...].astype(F32)).astype(BF16)

    return pl.pallas_call(
        body, name=name, out_shape=jax.ShapeDtypeStruct((4, R, C), BF16),
        grid_spec=pltpu.PrefetchScalarGridSpec(
            num_scalar_prefetch=1, grid=(4, R // tr),
            in_specs=[pl.BlockSpec((None, None, tr, C), lambda k, i, core_ref: (core_ref[0], k, i, 0)),
                      pl.BlockSpec((None, tr, C), lambda k, i, core_ref: (k, i, 0))],
            out_specs=pl.BlockSpec((None, tr, C), lambda k, i, core_ref: (k, i, 0))),
        compiler_params=_params(),
    )(core, pieces, recv)


MESH = pl.DeviceIdType.MESH
ANY = pl.BlockSpec(memory_space=pl.ANY)


def _position():
    return lax.axis_index("x"), lax.axis_index("y"), lax.axis_index("c")


def _small_all_gather(v, name):
    m_per, n = v.shape

    def body(x_ref, out_ref, send_sems, recv_sems, local_sem):
        x, y, c = _position()
        me, sibling = (x, y, c), (x, y, 1 - c)
        chips = [(1 - x, y), (x, 1 - y), (1 - x, 1 - y)]

        def rows(px, py, pc):
            return out_ref.at[pl.ds((4 * px + 2 * py + pc) * m_per, m_per), :]

        def copy(k, block, to, src=None):
            return pltpu.make_async_remote_copy(
                src_ref=rows(*block) if src is None else src, dst_ref=rows(*block),
                send_sem=send_sems.at[k], recv_sem=recv_sems.at[k], device_id=to, device_id_type=MESH)

        mine = pltpu.make_async_copy(x_ref, rows(*me), local_sem)
        mine.start()
        first = [copy(0, me, sibling, src=x_ref)]
        first += [copy(1 + j, me, (*chip, c), src=x_ref) for j, chip in enumerate(chips)]
        for cp in first:
            cp.start()
        passed = [copy(4 + j, (*chip, c), sibling) for j, chip in enumerate(chips)]
        for j, chip in enumerate(chips):
            copy(1 + j, (*chip, c), me).wait_recv()
            passed[j].start()
        copy(0, sibling, me).wait_recv()
        for j, chip in enumerate(chips):
            copy(4 + j, (*chip, 1 - c), me).wait_recv()
        for cp in first + passed:
            cp.wait_send()
        mine.wait()

    return pl.pallas_call(
        body, name=name, out_shape=jax.ShapeDtypeStruct((N_DEV * m_per, n), v.dtype),
        in_specs=[pl.BlockSpec(memory_space=pltpu.VMEM)], out_specs=pl.BlockSpec(memory_space=pltpu.VMEM),
        scratch_shapes=[pltpu.SemaphoreType.DMA((7,)), pltpu.SemaphoreType.DMA((7,)), pltpu.SemaphoreType.DMA],
    )(v)


def _big_all_gather(shards, name):
    n_arr = len(shards)

    def body(*refs):
        x_refs, out_refs = refs[:n_arr], refs[n_arr:2 * n_arr]
        send_sems, recv_sems, local_sems = refs[2 * n_arr:]
        x, y, c = _position()
        me, sibling = (x, y, c), (x, y, 1 - c)
        chips = [(1 - x, y), (x, 1 - y), (1 - x, 1 - y)]

        def slot(a, px, py, pc):
            return out_refs[a].at[4 * px + 2 * py + pc]

        def copy(a, k, block, to, src=None):
            return pltpu.make_async_remote_copy(
                src_ref=slot(a, *block) if src is None else src, dst_ref=slot(a, *block),
                send_sem=send_sems.at[a, k], recv_sem=recv_sems.at[a, k], device_id=to, device_id_type=MESH)

        mine = [pltpu.make_async_copy(x_refs[a], slot(a, *me), local_sems.at[a]) for a in range(n_arr)]
        for cp in mine:
            cp.start()
        first = []
        for j, chip in enumerate(chips):
            first += [copy(a, 1 + j, me, (*chip, c), src=x_refs[a]) for a in range(n_arr)]
        first += [copy(a, 0, me, sibling, src=x_refs[a]) for a in range(n_arr)]
        for cp in first:
            cp.start()
        passed = []
        for j, chip in enumerate(chips):
            for a in range(n_arr):
                copy(a, 1 + j, (*chip, c), me).wait_recv()
                fwd = copy(a, 4 + j, (*chip, c), sibling)
                fwd.start()
                passed.append(fwd)
        for a in range(n_arr):
            copy(a, 0, sibling, me).wait_recv()
        for j, chip in enumerate(chips):
            for a in range(n_arr):
                copy(a, 4 + j, (*chip, 1 - c), me).wait_recv()
        for cp in first + passed:
            cp.wait_send()
        for cp in mine:
            cp.wait()

    return pl.pallas_call(
        body, name=name,
        out_shape=tuple(jax.ShapeDtypeStruct((N_DEV,) + s.shape, s.dtype) for s in shards),
        in_specs=[ANY] * n_arr, out_specs=tuple([ANY] * n_arr),
        scratch_shapes=[pltpu.SemaphoreType.DMA((n_arr, 7)), pltpu.SemaphoreType.DMA((n_arr, 7)),
                        pltpu.SemaphoreType.DMA((n_arr,))],
    )(*shards)


def _sibling_exchange(pieces, name):
    n_arr = len(pieces)

    def body(*refs):
        p_refs, out_refs = refs[:n_arr], refs[n_arr:2 * n_arr]
        send_sems, recv_sems = refs[2 * n_arr:]
        x, y, c = _position()
        copies = [pltpu.make_async_remote_copy(
            src_ref=p_refs[a].at[1 - c], dst_ref=out_refs[a], send_sem=send_sems.at[a], recv_sem=recv_sems.at[a],
            device_id=(x, y, 1 - c), device_id_type=MESH) for a in range(n_arr)]
        for cp in copies:
            cp.start()
        for cp in copies:
            cp.wait()

    return pl.pallas_call(
        body, name=name,
        out_shape=tuple(jax.ShapeDtypeStruct(p.shape[1:], p.dtype) for p in pieces),
        in_specs=[ANY] * n_arr, out_specs=tuple([ANY] * n_arr),
        scratch_shapes=[pltpu.SemaphoreType.DMA((n_arr,)), pltpu.SemaphoreType.DMA((n_arr,))],
    )(*pieces)


HBM = pl.BlockSpec(memory_space=pltpu.HBM)
SEM = pl.BlockSpec(memory_space=pltpu.SEMAPHORE)
EFFECT = pltpu.SideEffectType.DATAFLOW_SIDE_EFFECTING
RELATIONS = [(rx, ry, rc) for rx in (0, 1) for ry in (0, 1) for rc in (0, 1)][1:]


SAME_CORE = [r for r in RELATIONS if r == (0, 0, 1) or r[2] == 0]


CHIPS = [r for r in RELATIONS if r[2] == 0]


def _exchange_copies(src_refs, land_refs, send_sems, recv_sems, scatter, receive_side, relations):
    x, y, c = _position()
    index = (lambda px, py, pc: 2 * px + py) if relations == CHIPS else (lambda px, py, pc: 4 * px + 2 * py + pc)
    me = index(x, y, c)
    copies = []
    for k, (rx, ry, rc) in enumerate(relations):
        peer = ((1 - x) if rx else x, (1 - y) if ry else y, (1 - c) if rc else c)
        peer_index = index(*peer)
        for a, (src, land) in enumerate(zip(src_refs, land_refs)):
            copies.append(pltpu.make_async_remote_copy(
                src_ref=src.at[peer_index] if scatter else src,
                dst_ref=land.at[peer_index if receive_side else me],
                send_sem=send_sems.at[a * len(relations) + k], recv_sem=recv_sems.at[a * len(relations) + k],
                device_id=peer, device_id_type=MESH))
    return copies


def _exchange_start(srcs, scatter, after, name, relations=RELATIONS):
    n = len(srcs)
    land_shapes = [(s.shape if scatter else (N_DEV,) + s.shape) for s in srcs]

    def body(*refs):
        src_refs, land_refs = refs[:n], refs[n:2 * n]
        send_sems, recv_sems = refs[2 * n + 1], refs[2 * n + 2]
        token = refs[-1]
        for cp in _exchange_copies(src_refs, land_refs, send_sems, recv_sems, scatter, False, relations):
            cp.start()
        token[...] = jnp.zeros_like(token)

    sems = pltpu.SemaphoreType.DMA((n * len(relations),))
    outs = pl.pallas_call(
        body, name=name,
        out_shape=(sems, sems, *[pltpu.HBM(s.shape, s.dtype) for s in srcs],
                   *[pltpu.HBM(shape, s.dtype) for shape, s in zip(land_shapes, srcs)],
                   jax.ShapeDtypeStruct((8, 128), F32)),
        in_specs=[HBM] * (2 * n) + [ANY],
        out_specs=(SEM, SEM, *[HBM] * (2 * n), pl.BlockSpec(memory_space=pltpu.VMEM)),
        input_output_aliases={a: 2 + a for a in range(2 * n)},
        compiler_params=pltpu.CompilerParams(has_side_effects=EFFECT),
    )(*[pltpu.with_memory_space_constraint(s, pltpu.HBM) for s in srcs],
      *[pltpu.with_memory_space_constraint(lax.empty(shape, s.dtype), pltpu.HBM)
        for shape, s in zip(land_shapes, srcs)], after)
    return outs[0], outs[1], outs[2:2 + n], outs[2 + n:2 + 2 * n], outs[-1]


def _exchange_wait(started, scatter, after, name, relations=RELATIONS):
    send_sems, recv_sems, srcs, lands, _ = started
    n = len(srcs)

    def body(*refs):
        src_refs, land_refs = refs[:n], refs[n:2 * n]
        send_sems, recv_sems = refs[2 * n], refs[2 * n + 1]
        copies = _exchange_copies(src_refs, land_refs, send_sems, recv_sems, scatter, True, relations)
        for cp in copies:
            cp.wait_send()
        for cp in copies:
            cp.wait_recv()

    outs = pl.pallas_call(
        body, name=name,
        out_shape=(*[pltpu.HBM(s.shape, s.dtype) for s in srcs], *[pltpu.HBM(t.shape, t.dtype) for t in lands]),
        in_specs=[HBM] * (2 * n) + [SEM, SEM, ANY], out_specs=tuple([HBM] * (2 * n)),
        input_output_aliases={a: a for a in range(2 * n)},
        compiler_params=pltpu.CompilerParams(has_side_effects=EFFECT),
    )(*srcs, *lands, send_sems, recv_sems, after)
    return outs[:n], outs[n:]


def _forward_copies(land_refs, send_sems, recv_sems, receive_side):
    x, y, c = _position()
    copies = []
    for j, (px, py) in enumerate([(1 - x, y), (x, 1 - y), (1 - x, 1 - y)]):
        held, coming = 4 * px + 2 * py + c, 4 * px + 2 * py + (1 - c)
        for a, land in enumerate(land_refs):
            copies.append(pltpu.make_async_remote_copy(
                src_ref=land.at[held], dst_ref=land.at[coming if receive_side else held],
                send_sem=send_sems.at[3 * a + j], recv_sem=recv_sems.at[3 * a + j],
                device_id=(x, y, 1 - c), device_id_type=MESH))
    return copies


def _forward_start(lands, after, name):
    n = len(lands)

    def body(*refs):
        send_sems, recv_sems, token = refs[n + 1], refs[n + 2], refs[-1]
        for cp in _forward_copies(refs[:n], send_sems, recv_sems, False):
            cp.start()
        token[...] = jnp.zeros_like(token)

    sems = pltpu.SemaphoreType.DMA((3 * n,))
    outs = pl.pallas_call(
        body, name=name,
        out_shape=(sems, sems, *[pltpu.HBM(t.shape, t.dtype) for t in lands], jax.ShapeDtypeStruct((8, 128), F32)),
        in_specs=[HBM] * n + [ANY], out_specs=(SEM, SEM, *[HBM] * n, pl.BlockSpec(memory_space=pltpu.VMEM)),
        input_output_aliases={a: 2 + a for a in range(n)},
        compiler_params=pltpu.CompilerParams(has_side_effects=EFFECT),
    )(*lands, after)
    return outs[0], outs[1], outs[2:2 + n], outs[-1]


def _forward_wait(started, after, name):
    send_sems, recv_sems, lands, _ = started
    n = len(lands)

    def body(*refs):
        copies = _forward_copies(refs[:n], refs[n], refs[n + 1], True)
        for cp in copies:
            cp.wait_send()
        for cp in copies:
            cp.wait_recv()

    return pl.pallas_call(
        body, name=name, out_shape=tuple(pltpu.HBM(t.shape, t.dtype) for t in lands),
        in_specs=[HBM] * n + [SEM, SEM, ANY], out_specs=tuple([HBM] * n),
        input_output_aliases={a: a for a in range(n)},
        compiler_params=pltpu.CompilerParams(has_side_effects=EFFECT),
    )(*lands, send_sems, recv_sems, after)


def _place_own(lands, mine, me, name):
    n = len(lands)
    flat = [m.reshape(-1, m.shape[-1]) for m in mine]
    flat_lands = [t.reshape(N_DEV, -1, t.shape[-1]) for t in lands]

    def body(me_ref, *refs):
        for src, dst in zip(refs[:n], refs[2 * n:]):
            dst[...] = src[...]

    in_specs = [pl.BlockSpec((m.shape[0] // 2, m.shape[1]), lambda i, me_ref: (i, 0)) for m in flat]
    out_specs = [pl.BlockSpec((None, m.shape[0] // 2, m.shape[1]), lambda i, me_ref: (me_ref[0], i, 0)) for m in flat]
    outs = pl.pallas_call(
        body, name=name, out_shape=tuple(jax.ShapeDtypeStruct(t.shape, t.dtype) for t in flat_lands),
        grid_spec=pltpu.PrefetchScalarGridSpec(
            num_scalar_prefetch=1, grid=(2,), in_specs=in_specs + [ANY] * n, out_specs=tuple(out_specs)),
        input_output_aliases={1 + n + a: a for a in range(n)},
        compiler_params=_params(),
    )(me, *flat, *flat_lands)
    return [o.reshape(t.shape) for o, t in zip(outs, lands)]


W_IN_SHARD = N_IN // N_DEV
F_SHARD = F_COL // W_IN_SHARD
F_LO = F_COL - F_SHARD * W_IN_SHARD


def _w_ffn_in_view(w):
    return jnp.transpose(w, (0, 2, 1))


def _w_in_rearranged(g):
    parts = [g[d] for d in range(N_DEV)]
    with_f = parts[F_SHARD]
    parts[F_SHARD:F_SHARD + 1] = [with_f[:, :F_LO], with_f[:, F_LO + N_FORGET:]]
    parts += [with_f[:, F_LO:F_LO + N_FORGET], jnp.zeros((with_f.shape[0], BLK - N_FORGET), with_f.dtype)]
    return jnp.concatenate(parts, axis=1)


def _w_in_pieces(dw_r):
    def original(lo, hi):
        shift = 0 if hi <= F_COL else N_FORGET
        return dw_r[:, lo - shift:hi - shift]

    pieces = []
    for d in range(N_DEV):
        lo, hi = d * W_IN_SHARD, (d + 1) * W_IN_SHARD
        if d == F_SHARD:
            pieces.append(jnp.concatenate([original(lo, F_COL), dw_r[:, N_MAIN:N_MAIN + N_FORGET],
                                           original(F_COL + N_FORGET, hi)], axis=1))
        else:
            pieces.append(original(lo, hi))
    return jnp.stack(pieces)


def _row_pieces(dw):
    return dw.reshape(N_DEV, dw.shape[0] // N_DEV, dw.shape[1])


def _branch_pieces(dw):
    k, w, d = dw.shape
    return jnp.transpose(dw.reshape(k, w, N_DEV, d // N_DEV), (2, 0, 1, 3)).reshape(N_DEV, k * w, d // N_DEV)


def _pair_major(p8):
    return jnp.stack([p8[0::2], p8[1::2]])


def _pairs_col(a):
    return jnp.transpose(a.reshape(a.shape[0], 4, 2), (1, 0, 2))


def _pairs_row(a):
    return jnp.transpose(a.reshape(a.shape[0], 4, 2), (1, 2, 0))


def _heads_from_col(a):
    return jnp.transpose(a, (1, 0, 2)).reshape(a.shape[1], 8)


def _heads_from_row(a):
    return jnp.transpose(a, (2, 0, 1)).reshape(a.shape[2], 8)


def _pad_lanes(a, n):
    return jnp.pad(a, [(0, 0)] * (a.ndim - 1) + [(0, n - a.shape[-1])])


SMALL_SEGMENTS = (("dmod", 2 * 6 * D_MODEL), ("norm_mix_g", 2 * D_MODEL), ("norm_ffn_g", 2 * D_MODEL),
                  ("final_norm_g", D_MODEL), ("b_forget", 128), ("sinks", 128), ("rel_bias", 4224), ("loss", 128))
SMALL_ROWS = 176


def _pack_small(parts):
    flat = [_pad_lanes(parts[name].reshape(1, -1), size) for name, size in SMALL_SEGMENTS]
    total = sum(size for _, size in SMALL_SEGMENTS)
    flat.append(jnp.zeros((1, SMALL_ROWS * 128 - total), F32))
    return jnp.concatenate(flat, axis=1).reshape(SMALL_ROWS, 128)


def _unpack_small(packed, shapes):
    flat = packed.reshape(-1)
    out, pos = {}, 0
    for name, size in SMALL_SEGMENTS:
        shape = shapes[name]
        count = 1
        for d in shape:
            count *= d
        out[name] = flat[pos:pos + count].reshape(shape)
        pos += size
    return out


def kernel(x, c, norm_mix_g, norm_ffn_g, w_ada, b_ada, w_in, b_forget, sinks, rel_bias, w_branch, w_out, w_ffn_in, w_ffn_out, final_norm_g, loss_target, m_norm_mix_g, m_norm_ffn_g, m_w_ada, m_b_ada, m_w_in, m_b_forget, m_sinks, m_rel_bias, m_w_branch, m_w_out, m_w_ffn_in, m_w_ffn_out, m_final_norm_g, v_norm_mix_g, v_norm_ffn_g, v_w_ada, v_b_ada, v_w_in, v_b_forget, v_sinks, v_rel_bias, v_w_branch, v_w_out, v_w_ffn_in, v_w_ffn_out, v_final_norm_g):
    depth = w_in.shape[0]
    S, D = x.shape[1], x.shape[2]
    assert S % GROUP == 0 and S >= ATTN_WINDOW["c"] * BLK
    px, py, pc = _position()
    me = 4 * px + 2 * py + pc
    x0 = x[0]

    assert depth == 2
    big_weights = (w_in, w_branch, w_out, w_ffn_in, w_ffn_out)
    me_arr = jnp.stack([me, me]).astype(jnp.int32)
    me_in_arr = jnp.stack([2 * px + py, me]).astype(jnp.int32)

    def rest_matrices(g_branch, g_out, g_fin, g_fout):
        return (jnp.transpose(g_branch, (1, 2, 0, 3)).reshape(3, 512, D), g_out.reshape(D, D),
                g_fin.reshape(2 * FFN_HIDDEN, D), g_fout.reshape(FFN_HIDDEN, D))

    def arrive(started, after, name):
        mine, landed = _exchange_wait(started, False, after, f"{name}_wait", SAME_CORE)
        return mine, _forward_start(landed, mine[0], f"{name}_forward_start")

    def finish_gather(arrived, after, name):
        mine, forward = arrived
        landed = _forward_wait(forward, after, f"{name}_forward_wait")
        return _place_own(landed, mine, me.astype(jnp.int32).reshape(1), f"{name}_own")

    w_fin_t = _w_ffn_in_view(w_ffn_in)
    shards = [[t.astype(BF16) for t in (w_in[l], w_branch[l], w_out[l], w_fin_t[l], w_ffn_out[l])]
              for l in range(depth)]
    gathered_in0 = _big_all_gather(shards[0][:1], "comm_gather_w_in0")[0]
    gather_rest0 = _exchange_start(shards[0][1:], False, gathered_in0, "comm_gather_rest0_start", SAME_CORE)
    gather1 = _exchange_start(shards[1], False, gather_rest0[4], "comm_gather_weights1_start", SAME_CORE)
    W_in, W_branch, W_out, W_fin, W_fout = ([None, None] for _ in range(5))
    W_in[0] = _w_in_rearranged(gathered_in0)

    c_all = _small_all_gather(c.reshape(8, 128), "comm_gather_c").reshape(N_DEV, D)
    mod_cols = _ada_fwd(c_all, w_ada, "ada_fwd")
    mod_all = _small_all_gather(mod_cols.reshape(-1, 128), "comm_gather_mod")
    mod_all = mod_all.reshape(N_DEV, depth, N_DEV, w_ada.shape[2])
    mod_mine = lax.dynamic_index_in_dim(mod_all, me, axis=2, keepdims=False)
    mod = jnp.transpose(mod_mine, (1, 0, 2)).reshape(depth, 6 * D) + b_ada + gather1[4][0:1, 0:1]
    mods = [[mod[l:l + 1, k * D:(k + 1) * D] for k in range(6)] for l in range(depth)]

    slopes = jnp.exp2(-jnp.arange(1, 9, dtype=F32))
    saved = []
    xs = x0
    for l in range(depth):
        if l == 1:
            g_in1, *g_rest1 = finish_gather(arrived1, xs, "comm_gather_weights1")
            W_in[1] = _w_in_rearranged(g_in1)
            W_branch[1], W_out[1], W_fin[1], W_fout[1] = rest_matrices(*g_rest1)
        sh_m, sc_m, g_m, sh_f, sc_f, g_f = mods[l]
        gm, gf = norm_mix_g[l:l + 1], norm_ffn_g[l:l + 1]
        bfor = _pad_lanes(b_forget[l:l + 1], BLK)
        h = _norm_mod_fwd(xs, gm, sh_m, sc_m, f"norm_mix_fwd{l}")
        qkv = _matmul(h, W_in[l], "nn", BF16, f"proj_qkv{l}", TILES["proj_qkv"], n=N_QKV)
        gates = _matmul(h, W_in[l], "nn", F32, f"proj_gates{l}", TILES["proj_gates"], n=N_GATES,
                        b_off=N_QKV // TILES["proj_gates"][1])
        fb = _matmul(h, W_in[l], "nn", F32, f"proj_forget{l}", TILES["proj_forget"], n=BLK, b_off=N_MAIN // BLK)
        cum = _forget_fwd(fb, bfor, f"forget_fwd{l}")[:, :N_FORGET]
        cum_col, cum_row = _pairs_col(cum), _pairs_row(cum)
        tiles, tiles_t = _rel_expand(_rel_bases(rel_bias[l]), f"rel_expand{l}")
        o_a, lse_a = _attn_fwd("a", qkv, f"attn_a_fwd{l}", sinks=sinks[l], slopes=slopes)
        o_b, lse_b = _attn_fwd("b", qkv, f"attn_b_fwd{l}", cq_col=cum_col, ck_row=cum_row)
        arrived_rest0 = arrive(gather_rest0, o_b, "comm_gather_rest0") if l == 0 else None
        o_c, lse_c = _attn_fwd("c", qkv, f"attn_c_fwd{l}", bias=tiles, after=arrived_rest0[1][3] if l == 0 else None)
        if l == 0:
            W_branch[0], W_out[0], W_fin[0], W_fout[0] = rest_matrices(
                *finish_gather(arrived_rest0, o_c, "comm_gather_rest0"))
        x1, merged, mix = _merge_fwd(o_a, o_b, o_c, gates, W_branch[l], W_out[l], xs, g_m, f"merge_fwd{l}")
        h2 = _norm_mod_fwd(x1, gf, sh_f, sc_f, f"norm_ffn_fwd{l}")
        act = _ffn_in_fwd(h2, W_fin[l], f"ffn_in_fwd{l}")
        if l == 0:
            arrived1 = arrive(gather1, act, "comm_gather_weights1")
        x2, ffn = _matmul_resid(act, W_fout[l], x1, g_f, f"ffn_out{l}", TILES["ffn_out"],
                                after=arrived1[1][3] if l == 0 else None)
        saved.append(dict(x=xs, h=h, qkv=qkv, gates=gates, fb=fb, bfor=bfor, cum_col=cum_col, cum_row=cum_row,
                          tiles_t=tiles_t, o=(o_a, o_b, o_c), lse=(lse_a, lse_b, lse_c), merged=merged, mix=mix,
                          x1=x1, h2=h2, act=act, ffn=ffn))
        xs = x2

    dx, loss_tile, d_final_g, d_g_f, df = _final_loss(
        xs, loss_target[0], final_norm_g.reshape(1, D), (saved[-1]["ffn"], mods[-1][5]), "final_loss")

    grads = {k: [None] * depth for k in ("w_in", "w_branch", "w_out", "w_ffn_in", "w_ffn_out", "norm_mix_g",
                                          "norm_ffn_g", "b_forget", "sinks", "rel_bias", "dmod")}
    def rest_pieces(l):
        return [_branch_pieces(grads["w_branch"][l]), _row_pieces(grads["w_out"][l]),
                _row_pieces(grads["w_ffn_in"][l]), _row_pieces(grads["w_ffn_out"][l])]

    reduce1 = reduce_rest0 = reduce_in0 = None
    for l in reversed(range(depth)):
        sv = saved[l]
        sh_m, sc_m, g_m, sh_f, sc_f, g_f = mods[l]
        gm, gf = norm_mix_g[l:l + 1], norm_ffn_g[l:l + 1]
        du_g, du_u = _ffn_mid_bwd(sv["h2"], df, W_fin[l], W_fout[l], f"ffn_mid_bwd{l}")
        du = jnp.concatenate([du_g, du_u], axis=1)
        grads["w_ffn_out"][l] = _matmul(sv["act"], df, "tn", BF16, f"wgrad_ffn_out{l}", TILES["wgrad_ffn_out"])
        grads["w_ffn_in"][l] = _matmul(du, sv["h2"], "tn", BF16, f"wgrad_ffn_in{l}", TILES["wgrad_ffn_in"])
        dh2 = _matmul(du, W_fin[l], "nn", F32, f"dgrad_ffn_in{l}", TILES["dgrad_ffn_in"])
        dx1, d_sh_f, d_sc_f, d_gf, d_g_m, dmix = _norm_mod_bwd(sv["x1"], dh2, dx, gf, sc_f, f"norm_ffn_bwd{l}",
                                                               below=(sv["mix"], g_m))
        grads["w_out"][l] = _matmul(sv["merged"], dmix, "tn", BF16, f"wgrad_out{l}", TILES["wgrad_out"])
        o_a, o_b, o_c = sv["o"]
        dgates, d_w_branch, do_a, do_b, do_c, dl_a, dl_b, dl_c = _merge_bwd(
            dmix, o_a, o_b, o_c, sv["gates"], W_branch[l], W_out[l], f"merge_bwd{l}")
        grads["w_branch"][l] = d_w_branch.astype(BF16)
        lse_rows = [_pairs_row(_heads_from_col(t)) for t in sv["lse"]]
        if l == 0:
            reduce_rest0 = _exchange_start(rest_pieces(0), True, dgates, "comm_reduce_rest0_start")
            lse_rows = [t + reduce_rest0[4][0:1, 0:1] for t in lse_rows]
        dq_a, dk_a, dv_a, dsink = _attn_bwd("a", sv["qkv"], do_a, lse_rows[0], _pairs_row(dl_a), f"attn_a_bwd{l}",
                                            sinks=sinks[l], slopes=slopes)
        dq_b, dk_b, dv_b, dck, dcq = _attn_bwd("b", sv["qkv"], do_b, lse_rows[1], _pairs_row(dl_b),
                                               f"attn_b_bwd{l}", cq_row=sv["cum_row"], ck_col=sv["cum_col"])
        dq_c, dk_c, dv_c, dtiles_t = _attn_bwd("c", sv["qkv"], do_c, lse_rows[2], _pairs_row(dl_c),
                                               f"attn_c_bwd{l}", bias_t=sv["tiles_t"])
        grads["sinks"][l] = dsink[:, :2, 0].reshape(8)
        grads["rel_bias"][l] = _rel_reduce(dtiles_t, f"rel_reduce{l}")[:, 0, :N_REL]
        dcum_k = _pad_lanes(_heads_from_col(dck), BLK)
        dcum_q = _pad_lanes(_heads_from_row(dcq), BLK)
        dfb, d_bfor = _forget_bwd(dcum_q, dcum_k, sv["fb"], sv["bfor"], f"forget_bwd{l}")
        grads["b_forget"][l] = d_bfor[0, :N_FORGET]
        dproj = jnp.concatenate(
            [t.astype(BF16) for t in (dq_a, dk_a, dv_a, dq_b, dk_b, dv_b, dq_c, dk_c, dv_c, dgates, dfb)],
            axis=1)
        grads["w_in"][l] = _matmul(sv["h"], dproj, "tn", BF16, f"wgrad_in{l}", TILES["wgrad_in"])
        if l == 1:
            reduce1 = _exchange_start([_w_in_pieces(grads["w_in"][1])] + rest_pieces(1), True, dproj, "comm_reduce1_start")
        dh = _matmul(dproj, W_in[l], "nt", F32, f"dgrad_in{l}", TILES["dgrad_in"], after=reduce1[4] if l == 1 else None)
        d_g_f_here = d_g_f
        if l > 0:
            dx, d_sh_m, d_sc_m, d_gm, d_g_f, df = _norm_mod_bwd(sv["x"], dh, dx1, gm, sc_m, f"norm_mix_bwd{l}",
                                                                below=(saved[l - 1]["ffn"], mods[l - 1][5]))
        else:
            dx, d_sh_m, d_sc_m, d_gm = _norm_mod_bwd(sv["x"], dh, dx1, gm, sc_m, f"norm_mix_bwd{l}")
        grads["norm_mix_g"][l] = d_gm[0]
        grads["norm_ffn_g"][l] = d_gf[0]
        grads["dmod"][l] = jnp.concatenate([d_sh_m, d_sc_m, d_g_m, d_sh_f, d_sc_f, d_g_f_here], axis=1)[0]

    grad_x = dx.reshape(x.shape)

    small_shapes = dict(dmod=b_ada.shape, norm_mix_g=norm_mix_g.shape, norm_ffn_g=norm_ffn_g.shape,
                        final_norm_g=final_norm_g.shape, b_forget=b_forget.shape, sinks=sinks.shape,
                        rel_bias=rel_bias.shape, loss=())
    mine_small = _pack_small(dict(
        loss=_pad_lanes(loss_tile[0:1, 0:1], 128),
        dmod=jnp.stack(grads["dmod"]), norm_mix_g=jnp.stack(grads["norm_mix_g"]),
        norm_ffn_g=jnp.stack(grads["norm_ffn_g"]), final_norm_g=d_final_g[0],
        b_forget=_pad_lanes(jnp.stack(grads["b_forget"]).reshape(1, -1), 128),
        sinks=_pad_lanes(jnp.stack(grads["sinks"]).reshape(1, -1), 128),
        rel_bias=_pad_lanes(jnp.stack(grads["rel_bias"]).reshape(1, -1), 4224)))
    all_small = _small_all_gather(mine_small, "comm_gather_small").reshape(N_DEV, SMALL_ROWS, 128)
    pieces_in0 = _pair_major(_w_in_pieces(grads["w_in"][0]))
    from_sibling = _sibling_exchange([pieces_in0], "comm_reduce_in0_sibling")[0]
    pair_sum_in0 = _pair_add(pieces_in0, from_sibling, pc.astype(jnp.int32).reshape(1), "pair_add_in0")
    reduce_in0 = _exchange_start([pair_sum_in0], True, all_small, "comm_reduce_in0_start", CHIPS)
    in0_started = reduce_in0[4]

    def pack_params(b_ada_, nm, nf, fn, bf, sk, rb):
        return _pack_small(dict(dmod=b_ada_, norm_mix_g=nm, norm_ffn_g=nf, final_norm_g=fn, loss=jnp.zeros((1, 128), F32),
                                b_forget=_pad_lanes(bf.reshape(1, -1), 128), sinks=_pad_lanes(sk.reshape(1, -1), 128),
                                rel_bias=_pad_lanes(rb.reshape(1, -1), 4224)))

    small_out = _adamw(
        pack_params(b_ada, norm_mix_g, norm_ffn_g, final_norm_g, b_forget, sinks, rel_bias)[None],
        pack_params(m_b_ada, m_norm_mix_g, m_norm_ffn_g, m_final_norm_g, m_b_forget, m_sinks, m_rel_bias)[None],
        pack_params(v_b_ada, v_norm_mix_g, v_norm_ffn_g, v_final_norm_g, v_b_forget, v_sinks, v_rel_bias)[None],
        [all_small], "adamw_small", me_arr, after=in0_started)
    small_out = [_unpack_small(t[0], small_shapes) for t in small_out]

    dmod_all = all_small[:, :96].reshape(N_DEV, depth, 6 * D)
    dmod_cols = lax.dynamic_slice_in_dim(dmod_all, me * w_ada.shape[2], w_ada.shape[2], axis=2)
    d_w_ada = _ada_bwd(jnp.transpose(c_all), jnp.transpose(dmod_cols, (1, 0, 2)), "ada_bwd")

    big = {"w_ada": _adamw(w_ada, m_w_ada, v_w_ada, [d_w_ada[l:l + 1] for l in range(depth)], "adamw_w_ada", me_arr,
                           after=in0_started)}
    sent1, landed1 = _exchange_wait(reduce1, True, big["w_ada"][0], "comm_reduce1_wait")
    sent_rest0, landed_rest0 = _exchange_wait(reduce_rest0, True, landed1[0], "comm_reduce_rest0_wait")
    parts = {"w_in": [None, (landed1[0], sent1[0])]}
    for a, name in enumerate(("w_branch", "w_out", "w_ffn_in", "w_ffn_out")):
        parts[name] = [(landed_rest0[a], sent_rest0[a]), (landed1[1 + a], sent1[1 + a])]

    def update(name, w, m, v, view=lambda t: t):
        per_layer = lambda t: t.reshape(depth, -1, t.shape[-1])
        outs = _adamw(*[per_layer(view(t)) for t in (w, m, v)], parts[name], f"adamw_{name}",
                      me_in_arr if name == "w_in" else me_arr)
        big[name] = [view(t).reshape(w.shape) for t in outs]

    update("w_ffn_in", w_ffn_in, m_w_ffn_in, v_w_ffn_in, _w_ffn_in_view)
    update("w_ffn_out", w_ffn_out, m_w_ffn_out, v_w_ffn_out)
    update("w_branch", w_branch, m_w_branch, v_w_branch)
    update("w_out", w_out, m_w_out, v_w_out)
    sent_in0, landed_in0 = _exchange_wait(reduce_in0, True, big["w_out"][0], "comm_reduce_in0_wait", CHIPS)
    parts["w_in"][0] = (landed_in0[0], sent_in0[0])
    update("w_in", w_in, m_w_in, v_w_in)

    def leaf(kind, name):
        if name in big:
            return big[name][kind]
        return small_out[kind]["dmod" if name == "b_ada" else name]

    order = ["norm_mix_g", "norm_ffn_g", "w_ada", "b_ada", "w_in", "b_forget", "sinks", "rel_bias", "w_branch",
             "w_out", "w_ffn_in", "w_ffn_out", "final_norm_g"]
    loss = small_out[0]["loss"]
    return (loss, grad_x, *[leaf(0, n) for n in order], *[leaf(1, n) for n in order],
            *[leaf(2, n) for n in order], *[leaf(3, n) for n in order])
```

```python
import functools

import jax
import jax.numpy as jnp
from jax import lax
from jax.experimental import pallas as pl
from jax.experimental.pallas import tpu as pltpu

F32 = jnp.float32
BF16 = jnp.bfloat16
NEG_INF = -1e30
EPS = 1e-6
N_DEV = 8
BLK = 128
GROUP = 4 * BLK
VMEM_LIMIT_BYTES = 56 * 1024 * 1024

D_MODEL = 1024
N_QKV = 3840
N_GATES = 3072
N_MAIN = N_QKV + N_GATES
N_FORGET = 8
N_IN = N_MAIN + N_FORGET
F_COL = 2304
FFN_HIDDEN = 2816
N_REL = 257

ADAM_LR, ADAM_B1, ADAM_B2, ADAM_EPS, ADAM_WD, ADAM_STEP = 0.001, 0.9, 0.999, 1e-08, 0.01, 10

NN = (((1,), (0,)), ((), ()))
NT = (((1,), (1,)), ((), ()))
TN = (((0,), (0,)), ((), ()))
HIGHEST = lax.Precision.HIGHEST

ATTN_COLS = {"a": (0, 4, 5), "b": (6, 10, 14), "c": (18, 22, 26)}
ATTN_WINDOW = {"a": 2, "c": 5}
ATTN_BLOCKS_PER_STEP = {"a": 4, "b": GROUP // BLK, "c": 2}


def _params():
    return pltpu.CompilerParams(vmem_limit_bytes=VMEM_LIMIT_BYTES)


def _tile(n, target):
    best = None
    t = 128
    while t <= min(n, target):
        if n % t == 0:
            best = t
        t += 128
    return best if best is not None else n


def _row_tile(n, target):
    t = min(n, target)
    while n % t:
        t -= 8
    return t


TILES = {
    "proj_qkv": (1024, 1280, 1024), "proj_gates": (1024, 768, 1024), "proj_forget": (1024, 128, 1024),
    "ffn_out": (1024, 512, 2816), "ffn_fused": (512, 1408),
    "wgrad_ffn_out": (1408, 1024, 1024), "wgrad_ffn_in": (1408, 1024, 1024), "dgrad_ffn_in": (1024, 1024, 1408),
    "wgrad_out": (1024, 1024, 1024),
    "wgrad_in": (1024, 1408, 1024), "dgrad_in": (1024, 1024, 1408),
}


def _matmul(a, b, mode, out_dtype, name, tiles, *, n=None, a_off=0, b_off=0, m=None, after=None):
    tm, tn, tk = tiles
    if mode == "nn":
        M, K = a.shape if m is None else (m, a.shape[1])
        N = b.shape[1] if n is None else n
    elif mode == "nt":
        M, K = a.shape
        N = b.shape[0] if n is None else n
    else:
        K = a.shape[0]
        M = a.shape[1] if m is None else m
        N = b.shape[1] if n is None else n
    tm = _tile(M, tm) if M % 128 == 0 else M
    tn = _tile(N, tn)
    tk = _tile(K, tk)
    nk = K // tk
    dims = {"nn": NN, "nt": NT, "tn": TN}[mode]
    if mode == "nn":
        a_spec = pl.BlockSpec((tm, tk), lambda i, j, k: (i + a_off, k))
        b_spec = pl.BlockSpec((tk, tn), lambda i, j, k: (k, j + b_off))
    elif mode == "nt":
        a_spec = pl.BlockSpec((tm, tk), lambda i, j, k: (i + a_off, k))
        b_spec = pl.BlockSpec((tn, tk), lambda i, j, k: (j + b_off, k))
    else:
        a_spec = pl.BlockSpec((tk, tm), lambda i, j, k: (k, i + a_off))
        b_spec = pl.BlockSpec((tk, tn), lambda i, j, k: (k, j + b_off))

    def body(a_ref, b_ref, *rest):
        o_ref, acc_ref = rest[-2:]
        k = pl.program_id(2)
        part = lax.dot_general(a_ref[...], b_ref[...], dims, preferred_element_type=F32)
        if nk == 1:
            o_ref[...] = part.astype(o_ref.dtype)
        else:
            @pl.when(k == 0)
            def _():
                acc_ref[...] = part

            @pl.when(k > 0)
            def _():
                acc_ref[...] += part

            @pl.when(k == nk - 1)
            def _():
                o_ref[...] = acc_ref[...].astype(o_ref.dtype)

    return pl.pallas_call(
        body, name=name,
        out_shape=jax.ShapeDtypeStruct((M, N), out_dtype),
        grid=(M // tm, N // tn, nk),
        in_specs=[a_spec, b_spec] + ([ANY] if after is not None else []),
        out_specs=pl.BlockSpec((tm, tn), lambda i, j, k: (i, j)),
        scratch_shapes=[pltpu.VMEM((tm, tn) if nk > 1 else (8, 128), F32)],
        compiler_params=_params(),
    )(a, b, *([after] if after is not None else []))


def _matmul_resid(a, b, resid, gate, name, tiles, after=None):
    M, K = a.shape
    N = b.shape[1]
    tm, tn, tk = (_tile(d, t) for d, t in zip((M, N, K), tiles))
    nk = K // tk

    def body(a_ref, b_ref, r_ref, g_ref, *rest):
        o_ref, s_ref, acc_ref = rest[-3:]
        k = pl.program_id(2)
        part = jnp.dot(a_ref[...], b_ref[...], preferred_element_type=F32)

        def finish(acc):
            o_ref[...] = r_ref[...] + g_ref[...] * acc
            s_ref[...] = acc.astype(BF16)

        if nk == 1:
            finish(part)
        else:
            @pl.when(k == 0)
            def _():
                acc_ref[...] = part

            @pl.when(k > 0)
            def _():
                acc_ref[...] += part

            @pl.when(k == nk - 1)
            def _():
                finish(acc_ref[...])

    return pl.pallas_call(
        body, name=name,
        out_shape=(jax.ShapeDtypeStruct((M, N), F32), jax.ShapeDtypeStruct((M, N), BF16)),
        grid=(M // tm, N // tn, nk),
        in_specs=[pl.BlockSpec((tm, tk), lambda i, j, k: (i, k)),
                  pl.BlockSpec((tk, tn), lambda i, j, k: (k, j)),
                  pl.BlockSpec((tm, tn), lambda i, j, k: (i, j)),
                  pl.BlockSpec((1, tn), lambda i, j, k: (0, j))] + ([ANY] if after is not None else []),
        out_specs=(pl.BlockSpec((tm, tn), lambda i, j, k: (i, j)),
                   pl.BlockSpec((tm, tn), lambda i, j, k: (i, j))),
        scratch_shapes=[pltpu.VMEM((tm, tn) if nk > 1 else (8, 128), F32)],
        compiler_params=_params(),
    )(a, b, resid, gate, *([after] if after is not None else []))


def _norm_mod_fwd(x, g, shift, scale, name):
    S, D = x.shape
    ts = _row_tile(S, 256)

    def body(x_ref, g_ref, sh_ref, sc_ref, h_ref):
        xv = x_ref[...]
        rstd = lax.rsqrt(jnp.mean(xv * xv, axis=-1, keepdims=True) + EPS)
        y = xv * rstd * g_ref[...]
        h_ref[...] = (y * (1.0 + sc_ref[...]) + sh_ref[...]).astype(BF16)

    row = pl.BlockSpec((1, D), lambda i: (0, 0))
    return pl.pallas_call(
        body, name=name, out_shape=jax.ShapeDtypeStruct((S, D), BF16), grid=(S // ts,),
        in_specs=[pl.BlockSpec((ts, D), lambda i: (i, 0)), row, row, row],
        out_specs=pl.BlockSpec((ts, D), lambda i: (i, 0)),
        compiler_params=_params(),
    )(x, g, shift, scale)


def _accumulate_rows(i, pairs):
    @pl.when(i == 0)
    def _():
        for ref, value in pairs:
            ref[...] = value

    @pl.when(i > 0)
    def _():
        for ref, value in pairs:
            ref[...] += value


def _gated_residual_bwd(dx, f_ref, gate_ref, df_ref):
    df_ref[...] = (dx * gate_ref[...]).astype(BF16)
    return jnp.sum(dx * f_ref[...].astype(F32), axis=0, keepdims=True)


def _norm_mod_bwd(x, dh, dres, g, scale, name, below=None):
    S, D = x.shape
    ts = _row_tile(S, 256)

    def body(x_ref, dh_ref, dr_ref, g_ref, sc_ref, *rest):
        i = pl.program_id(0)
        xv, dhv, gv = x_ref[...], dh_ref[...], g_ref[...]
        rstd = lax.rsqrt(jnp.mean(xv * xv, axis=-1, keepdims=True) + EPS)
        xhat = xv * rstd
        dn = dhv * (1.0 + sc_ref[...])
        dxhat = dn * gv
        proj = jnp.mean(dxhat * xhat, axis=-1, keepdims=True)
        dx = dr_ref[...] + rstd * (dxhat - xhat * proj)
        sums = [jnp.sum(dhv, axis=0, keepdims=True), jnp.sum(dhv * (xhat * gv), axis=0, keepdims=True),
                jnp.sum(dn * xhat, axis=0, keepdims=True)]
        if below is None:
            dx_ref, *sum_refs = rest
        else:
            f_ref, gate_ref, dx_ref, *sum_refs, df_ref = rest
            sums.append(_gated_residual_bwd(dx, f_ref, gate_ref, df_ref))
        dx_ref[...] = dx
        _accumulate_rows(i, list(zip(sum_refs, sums)))

    tile = pl.BlockSpec((ts, D), lambda i: (i, 0))
    row = pl.BlockSpec((1, D), lambda i: (0, 0))
    vec = jax.ShapeDtypeStruct((1, D), F32)
    fused = below is not None
    return pl.pallas_call(
        body, name=name,
        out_shape=(jax.ShapeDtypeStruct((S, D), F32), vec, vec, vec)
        + ((vec, jax.ShapeDtypeStruct((S, D), BF16)) if fused else ()),
        grid=(S // ts,),
        in_specs=[tile, tile, tile, row, row] + ([tile, row] if fused else []),
        out_specs=(tile, row, row, row) + ((row, tile) if fused else ()),
        compiler_params=_params(),
    )(x, dh, dres, g, scale, *(below if fused else ()))


def _ffn_in_fwd(h, w_t, name):
    S, D = h.shape
    F = w_t.shape[0] // 2
    tm, tn = _tile(S, TILES["ffn_fused"][0]), _tile(F, TILES["ffn_fused"][1])
    nj = F // tn

    def body(h_ref, wg_ref, wu_ref, o_ref):
        hv = h_ref[...]
        ug = lax.dot_general(hv, wg_ref[...], NT, preferred_element_type=F32)
        uu = lax.dot_general(hv, wu_ref[...], NT, preferred_element_type=F32)
        o_ref[...] = (ug * jax.nn.sigmoid(ug) * uu).astype(BF16)

    return pl.pallas_call(
        body, name=name, out_shape=jax.ShapeDtypeStruct((S, F), BF16), grid=(nj, S // tm),
        in_specs=[pl.BlockSpec((tm, D), lambda j, i: (i, 0)),
                  pl.BlockSpec((tn, D), lambda j, i: (j, 0)),
                  pl.BlockSpec((tn, D), lambda j, i: (j + nj, 0))],
        out_specs=pl.BlockSpec((tm, tn), lambda j, i: (i, j)),
        compiler_params=_params(),
    )(h, w_t, w_t)


def _ffn_mid_bwd(h, df, w_in_t, w_out, name):
    S, D = h.shape
    F = w_in_t.shape[0] // 2
    tm, tn = _tile(S, TILES["ffn_fused"][0]), _tile(F, TILES["ffn_fused"][1])
    nj = F // tn

    def body(h_ref, df_ref, wg_ref, wu_ref, wo_ref, dg_ref, du_ref):
        hv = h_ref[...]
        ug = lax.dot_general(hv, wg_ref[...], NT, preferred_element_type=F32)
        uu = lax.dot_general(hv, wu_ref[...], NT, preferred_element_type=F32)
        dact = lax.dot_general(df_ref[...], wo_ref[...], NT, preferred_element_type=F32)
        sig = jax.nn.sigmoid(ug)
        dg_ref[...] = (dact * uu * (sig * (1.0 + ug * (1.0 - sig)))).astype(BF16)
        du_ref[...] = (dact * (ug * sig)).astype(BF16)

    out = jax.ShapeDtypeStruct((S, F), BF16)
    return pl.pallas_call(
        body, name=name, out_shape=(out, out), grid=(nj, S // tm),
        in_specs=[pl.BlockSpec((tm, D), lambda j, i: (i, 0)),
                  pl.BlockSpec((tm, D), lambda j, i: (i, 0)),
                  pl.BlockSpec((tn, D), lambda j, i: (j, 0)),
                  pl.BlockSpec((tn, D), lambda j, i: (j + nj, 0)),
                  pl.BlockSpec((tn, D), lambda j, i: (j, 0))],
        out_specs=(pl.BlockSpec((tm, tn), lambda j, i: (i, j)), pl.BlockSpec((tm, tn), lambda j, i: (i, j))),
        compiler_params=_params(),
    )(h, df, w_in_t, w_in_t, w_out)


def _merge_fwd(o_a, o_b, o_c, gates, w_branch, w_out, resid, gate, name, *, tm=512):
    S, W = o_a.shape
    D = w_branch.shape[2]
    tm = _row_tile(S, tm)

    def body(oa_ref, ob_ref, oc_ref, g_ref, w_ref, wo_ref, r_ref, gm_ref, x_ref, m_ref, mix_ref):
        acc = None
        for k, o_ref in enumerate((oa_ref, ob_ref, oc_ref)):
            y = jnp.dot(o_ref[...], w_ref[k], preferred_element_type=F32)
            t = jax.nn.sigmoid(g_ref[:, k * D:(k + 1) * D]) * y
            acc = t if acc is None else acc + t
        merged = acc.astype(BF16)
        m_ref[...] = merged
        mix = jnp.dot(merged, wo_ref[...], preferred_element_type=F32)
        x_ref[...] = r_ref[...] + gm_ref[...] * mix
        mix_ref[...] = mix.astype(BF16)

    o_spec = pl.BlockSpec((tm, W), lambda i: (i, 0))
    tile = pl.BlockSpec((tm, D), lambda i: (i, 0))
    return pl.pallas_call(
        body, name=name,
        out_shape=(jax.ShapeDtypeStruct((S, D), F32), jax.ShapeDtypeStruct((S, D), BF16), jax.ShapeDtypeStruct((S, D), BF16)),
        grid=(S // tm,),
        in_specs=[o_spec, o_spec, o_spec, pl.BlockSpec((tm, 3 * D), lambda i: (i, 0)),
                  pl.BlockSpec((3, W, D), lambda i: (0, 0, 0)), pl.BlockSpec((D, D), lambda i: (0, 0)),
                  tile, pl.BlockSpec((1, D), lambda i: (0, 0))],
        out_specs=(tile, tile, tile),
        compiler_params=_params(),
    )(o_a, o_b, o_c, gates, w_branch, w_out, resid, gate)


def _merge_bwd(dmix, o_a, o_b, o_c, gates, w_branch, w_out, name, *, tm=256):
    S, W = o_a.shape
    D = w_branch.shape[2]
    tm = _row_tile(S, tm)
    n_heads = W // 64

    def body(dm_ref, oa_ref, ob_ref, oc_ref, g_ref, w_ref, wo_ref, dg_ref, dw_ref,
             doa_ref, dob_ref, doc_ref, dla_ref, dlb_ref, dlc_ref):
        first = pl.program_id(0) == 0
        dm = lax.dot_general(dm_ref[...], wo_ref[...], NT, preferred_element_type=F32)
        branches = ((oa_ref, doa_ref, dla_ref), (ob_ref, dob_ref, dlb_ref), (oc_ref, doc_ref, dlc_ref))
        for k, (o_ref, do_ref, dl_ref) in enumerate(branches):
            wk = w_ref[k]
            ov = o_ref[...]
            y = jnp.dot(ov, wk, preferred_element_type=F32)
            g = jax.nn.sigmoid(g_ref[:, k * D:(k + 1) * D])
            dy = (dm * g).astype(BF16)
            dwk = lax.dot_general(ov, dy, TN, preferred_element_type=F32)

            @pl.when(first)
            def _(k=k, dwk=dwk):
                dw_ref[k] = dwk

            @pl.when(jnp.logical_not(first))
            def _(k=k, dwk=dwk):
                dw_ref[k] += dwk
            dg_ref[:, k * D:(k + 1) * D] = (dm * y * (g * (1.0 - g))).astype(BF16)
            do16 = lax.dot_general(dy, wk, NT, preferred_element_type=F32).astype(BF16)
            do_ref[...] = do16
            prod = do16.astype(F32) * ov.astype(F32)
            for h in range(n_heads):
                dl_ref[:, h:h + 1] = jnp.sum(prod[:, 64 * h:64 * (h + 1)], axis=1, keepdims=True)

    o_spec = pl.BlockSpec((tm, W), lambda i: (i, 0))
    wide = pl.BlockSpec((tm, 3 * D), lambda i: (i, 0))
    dl_spec = pl.BlockSpec((tm, n_heads), lambda i: (i, 0))
    o_out = jax.ShapeDtypeStruct((S, W), BF16)
    wide_out = jax.ShapeDtypeStruct((S, 3 * D), BF16)
    dl_out = jax.ShapeDtypeStruct((S, n_heads), F32)
    whole = pl.BlockSpec((3, W, D), lambda i: (0, 0, 0))
    return pl.pallas_call(
        body, name=name,
        out_shape=(wide_out, jax.ShapeDtypeStruct((3, W, D), F32), o_out, o_out, o_out, dl_out, dl_out, dl_out),
        grid=(S // tm,),
        in_specs=[pl.BlockSpec((tm, D), lambda i: (i, 0)), o_spec, o_spec, o_spec, wide, whole,
                  pl.BlockSpec((D, D), lambda i: (0, 0))],
        out_specs=(wide, whole, o_spec, o_spec, o_spec, dl_spec, dl_spec, dl_spec),
        compiler_params=_params(),
    )(dmix, o_a, o_b, o_c, gates, w_branch, w_out)


def _band_mask(variant, t_abs, s_abs):
    if variant == "b":
        return s_abs <= t_abs
    qc, kc = t_abs >> 6, s_abs >> 6
    return (kc <= qc) & (kc >= qc - (2 if variant == "a" else 8))


def _attn_fwd(variant, qkv, name, *, sinks=None, slopes=None, cq_col=None, ck_row=None, bias=None, after=None):
    S = qkv.shape[0]
    nb = S // BLK
    qb, kb, vb = ATTN_COLS[variant]
    shared_kv = variant == "a"
    win = ATTN_WINDOW.get(variant)
    per_step = ATTN_BLOCKS_PER_STEP[variant]

    def body(*refs):
        if after is not None:
            refs = refs[:-3] + refs[-2:]
        if variant == "a":
            q_ref, k_ref, v_ref, sink_ref, slope_ref, o_ref, lse_ref = refs
        elif variant == "b":
            q_ref, k_ref, v_ref, cq_ref, ck_ref, o_ref, lse_ref = refs
        else:
            q_ref, k_ref, v_ref, bias_ref, o_ref, lse_ref = refs
        p = pl.program_id(0)
        lane = lax.broadcasted_iota(jnp.int32, (1, BLK), 1)

        def compute(i, rows, start, n_keys):
            n_rows = rows.stop - rows.start
            t_abs = i * BLK + lax.broadcasted_iota(jnp.int32, (n_rows, 1), 0)
            q2 = q_ref[rows, :].astype(F32) * 0.125
            k_w = k_ref[pl.ds(start, n_keys), :]
            v_w = v_ref[pl.ds(start, n_keys), :]
            s_abs = start + lax.broadcasted_iota(jnp.int32, (1, n_keys), 1)
            valid = _band_mask(variant, t_abs, s_abs)
            outs = []
            for half in (0, 1):
                hmask = (lane >= 64) if half else (lane < 64)
                qh = jnp.where(hmask, q2, 0.0)
                if shared_kv:
                    swap = (p // 2) != half
                    qh = jnp.where(swap, pltpu.roll(qh, 64, 1), qh)
                s = lax.dot_general(qh.astype(BF16), k_w, NT, preferred_element_type=F32)
                if variant == "a":
                    head = 2 * p + half
                    s = s + (-slope_ref[head]) * jnp.abs(t_abs - s_abs).astype(F32)
                elif variant == "b":
                    s = s + cq_ref[rows, half:half + 1] - ck_ref[half:half + 1, pl.ds(start, n_keys)]
                else:
                    j0 = start // BLK
                    s = s + jnp.concatenate([jnp.concatenate(
                        [bias_ref[half, jnp.clip(i + r - j0 - b, 0, 4)] for b in range(n_keys // BLK)], axis=1)
                        for r in range(n_rows // BLK)], axis=0)
                s = jnp.where(valid, s, NEG_INF)
                m = jnp.max(s, axis=1, keepdims=True)
                if variant == "a":
                    m = jnp.maximum(m, sink_ref[head])
                pe = jnp.exp(s - m)
                l = jnp.sum(pe, axis=1, keepdims=True)
                if variant == "a":
                    l = l + jnp.exp(sink_ref[head] - m)
                out = jnp.dot(pe.astype(BF16), v_w, preferred_element_type=F32) / l
                if shared_kv:
                    out = jnp.where(swap, pltpu.roll(out, 64, 1), out)
                outs.append(out)
                lse_ref[rows, half:half + 1] = m + jnp.log(l)
            o_ref[rows, :] = jnp.where(lane < 64, outs[0], outs[1]).astype(BF16)

        step = pl.program_id(1)
        if variant == "b":
            for g in range(S // GROUP):
                pl.when(step == g)(functools.partial(compute, step * per_step, slice(0, GROUP), 0, (g + 1) * GROUP))
        elif variant == "c":
            span = win + per_step - 1
            start = jnp.clip(step * per_step - (win - 1), 0, nb - span) * BLK
            compute(step * per_step, slice(0, per_step * BLK), pl.multiple_of(start, BLK), span * BLK)
        else:
            for sub in range(per_step):
                i = step * per_step + sub
                start = jnp.clip(i - (win - 1), 0, nb - win) * BLK
                compute(i, slice(sub * BLK, (sub + 1) * BLK), pl.multiple_of(start, BLK), win * BLK)

    tq = per_step * BLK
    kv_col = (lambda p, i: (0, kb)) if shared_kv else (lambda p, i: (0, kb + p))
    vv_col = (lambda p, i: (0, vb)) if shared_kv else (lambda p, i: (0, vb + p))
    in_specs = [pl.BlockSpec((tq, BLK), lambda p, i: (i, qb + p)),
                pl.BlockSpec((S, BLK), kv_col), pl.BlockSpec((S, BLK), vv_col)]
    args = [qkv, qkv, qkv]
    if variant == "a":
        in_specs += [pl.BlockSpec(memory_space=pltpu.SMEM), pl.BlockSpec(memory_space=pltpu.SMEM)]
        args += [sinks, slopes]
    elif variant == "b":
        in_specs += [pl.BlockSpec((None, tq, 2), lambda p, i: (p, i, 0)),
                     pl.BlockSpec((None, 2, S), lambda p, i: (p, 0, 0))]
        args += [cq_col, ck_row]
    else:
        in_specs += [pl.BlockSpec((2, 5, BLK, BLK), lambda p, i: (p, 0, 0, 0))]
        args += [bias]
    if after is not None:
        in_specs.append(ANY)
        args.append(after)
    return pl.pallas_call(
        body, name=name,
        out_shape=(jax.ShapeDtypeStruct((S, 512), BF16), jax.ShapeDtypeStruct((4, S, 2), F32)),
        grid=(4, nb // per_step), in_specs=in_specs,
        out_specs=(pl.BlockSpec((tq, BLK), lambda p, i: (i, p)),
                   pl.BlockSpec((None, tq, 2), lambda p, i: (p, i, 0))),
        compiler_params=_params(),
    )(*args)


def _attn_bwd(variant, qkv, do, lse_row, delta_row, name, *, sinks=None, slopes=None, cq_row=None,
              ck_col=None, bias_t=None):
    S = qkv.shape[0]
    nb = S // BLK
    qb, kb, vb = ATTN_COLS[variant]
    shared_kv = variant == "a"
    win = ATTN_WINDOW.get(variant)
    per_step = ATTN_BLOCKS_PER_STEP[variant]

    def body(*refs):
        *refs, dqt_ref = refs
        if variant == "a":
            (q_ref, k_ref, v_ref, do_ref, lse_ref, dl_ref, sink_ref, slope_ref,
             dq_ref, dk_ref, dv_ref, ex_ref) = refs
        elif variant == "b":
            (q_ref, k_ref, v_ref, do_ref, lse_ref, dl_ref, cq_ref, ck_ref,
             dq_ref, dk_ref, dv_ref, ex_ref, dcq_ref) = refs
        else:
            (q_ref, k_ref, v_ref, do_ref, lse_ref, dl_ref, bias_ref,
             dq_ref, dk_ref, dv_ref, ex_ref) = refs
        p = pl.program_id(0)
        lane = lax.broadcasted_iota(jnp.int32, (1, BLK), 1)
        hmasks = [(lane < 64), (lane >= 64)]
        swaps = [(p // 2) != half for half in (0, 1)] if shared_kv else None

        @pl.when(pl.program_id(1) == 0)
        def _():
            dqt_ref[...] = jnp.zeros_like(dqt_ref)
            if variant == "b":
                dcq_ref[...] = jnp.zeros_like(dcq_ref)
            else:
                ex_ref[...] = jnp.zeros_like(ex_ref)

        def to_kv_lanes(x, h):
            x = jnp.where(hmasks[h], x, 0.0)
            if shared_kv:
                x = jnp.where(swaps[h], pltpu.roll(x, 64, 1), x)
            return x

        def compute(j, rows, start, n_q):
            n_rows = rows.stop - rows.start
            s_abs = j * BLK + lax.broadcasted_iota(jnp.int32, (n_rows, 1), 0)
            off_k = pl.multiple_of(j * BLK, BLK)
            k2 = k_ref[rows, :].astype(F32)
            v2 = v_ref[rows, :].astype(F32)
            if shared_kv:
                kv_lane = (lane >> 6) == (p // 2)
                k_src, v_src = jnp.where(kv_lane, k2, 0.0), jnp.where(kv_lane, v2, 0.0)
                k_al = [jnp.where(swaps[h], pltpu.roll(k_src, 64, 1), k_src) for h in (0, 1)]
                v_al = [jnp.where(swaps[h], pltpu.roll(v_src, 64, 1), v_src) for h in (0, 1)]
            else:
                k_al = [jnp.where(hmasks[h], k2, 0.0) for h in (0, 1)]
                v_al = [jnp.where(hmasks[h], v2, 0.0) for h in (0, 1)]
            k_al = [(t * 0.125).astype(BF16) for t in k_al]
            v_al = [t.astype(BF16) for t in v_al]
            q_w = q_ref[pl.ds(start, n_q), :]
            do_w = do_ref[pl.ds(start, n_q), :]
            t_abs = start + lax.broadcasted_iota(jnp.int32, (1, n_q), 1)
            valid = _band_mask(variant, t_abs, s_abs)
            dk_acc = dv_acc = None
            ds_both = []
            for half in (0, 1):
                s = lax.dot_general(k_al[half], q_w, NT, preferred_element_type=F32)
                if variant == "a":
                    s = s + (-slope_ref[2 * p + half]) * jnp.abs(t_abs - s_abs).astype(F32)
                elif variant == "b":
                    s = s + cq_ref[half:half + 1, pl.ds(start, n_q)] - ck_ref[rows, half:half + 1]
                else:
                    i0 = start // BLK
                    s = s + jnp.concatenate([jnp.concatenate(
                        [bias_ref[half, jnp.clip(i0 + b - j - r, 0, 4)] for b in range(n_q // BLK)], axis=1)
                        for r in range(n_rows // BLK)], axis=0)
                pr = jnp.where(valid, jnp.exp(s - lse_ref[half:half + 1, pl.ds(start, n_q)]), 0.0)
                dp = lax.dot_general(v_al[half], do_w, NT, preferred_element_type=F32)
                ds = pr * (dp - dl_ref[half:half + 1, pl.ds(start, n_q)])
                ds16 = ds.astype(BF16)
                dv_h = to_kv_lanes(jnp.dot(pr.astype(BF16), do_w, preferred_element_type=F32), half)
                dk_h = to_kv_lanes(jnp.dot(ds16, q_w, preferred_element_type=F32) * 0.125, half)
                dv_acc = dv_h if dv_acc is None else dv_acc + dv_h
                dk_acc = dk_h if dk_acc is None else dk_acc + dk_h
                ds_both.append(ds16)
                if variant == "b":
                    ex_ref[rows, half:half + 1] = -jnp.sum(ds, axis=1, keepdims=True)
                    dcq_ref[half:half + 1, pl.ds(start, n_q)] += jnp.sum(ds, axis=0, keepdims=True)
                elif variant == "c":
                    for r in range(n_rows // BLK):
                        for b in range(n_q // BLK):
                            ex_ref[half, jnp.clip(i0 + b - j - r, 0, 4)] += ds[r * BLK:(r + 1) * BLK, b * BLK:(b + 1) * BLK]
            dq_t = lax.dot_general(jnp.concatenate(k_al, axis=0), jnp.concatenate(ds_both, axis=0), TN,
                                   preferred_element_type=F32)
            dqt_ref[:, pl.ds(start, n_q)] += dq_t
            if shared_kv:
                @pl.when(p == 0)
                def _():
                    dk_ref[pl.ds(off_k, n_rows), :] = dk_acc
                    dv_ref[pl.ds(off_k, n_rows), :] = dv_acc

                @pl.when(p > 0)
                def _():
                    dk_ref[pl.ds(off_k, n_rows), :] += dk_acc
                    dv_ref[pl.ds(off_k, n_rows), :] += dv_acc
            else:
                dk_ref[pl.ds(off_k, n_rows), :] = dk_acc.astype(dk_ref.dtype)
                dv_ref[pl.ds(off_k, n_rows), :] = dv_acc.astype(dv_ref.dtype)
            if variant == "a":
                for half in (0, 1):
                    p_sink = jnp.exp(sink_ref[2 * p + half] - lse_ref[half:half + 1, pl.ds(off_k, n_rows)])
                    term = p_sink * dl_ref[half:half + 1, pl.ds(off_k, n_rows)]
                    ex_ref[half:half + 1, :] += -jnp.sum(term, axis=1, keepdims=True)

        step = pl.program_id(1)
        if variant == "b":
            for g in range(S // GROUP):
                pl.when(step == g)(functools.partial(compute, step * per_step, slice(0, GROUP), g * GROUP, S - g * GROUP))
        elif variant == "c":
            span = win + per_step - 1
            start = jnp.clip(step * per_step, 0, nb - span) * BLK
            compute(step * per_step, slice(0, per_step * BLK), pl.multiple_of(start, BLK), span * BLK)
        else:
            for sub in range(per_step):
                j = step * per_step + sub
                start = jnp.clip(j, 0, nb - win) * BLK
                compute(j, slice(sub * BLK, (sub + 1) * BLK), pl.multiple_of(start, BLK), win * BLK)

        @pl.when(step == nb // per_step - 1)
        def _():
            dq_ref[...] = jnp.transpose(dqt_ref[...]).astype(BF16)

    tk = per_step * BLK
    col = lambda c0: (lambda p, j: (0, c0 + p))
    kv_blk = (lambda c0: (lambda p, j: (j, c0))) if shared_kv else (lambda c0: (lambda p, j: (j, c0 + p)))
    pair = lambda p, j: (0, p)
    row_stat = pl.BlockSpec((None, 2, S), lambda p, j: (p, 0, 0))
    in_specs = [pl.BlockSpec((S, BLK), col(qb)),
                pl.BlockSpec((tk, BLK), kv_blk(kb)), pl.BlockSpec((tk, BLK), kv_blk(vb)),
                pl.BlockSpec((S, BLK), pair), row_stat, row_stat]
    args = [qkv, qkv, qkv, do, lse_row, delta_row]
    kv_width = BLK if shared_kv else 512
    kv_out = pl.BlockSpec((S, BLK), (lambda p, j: (0, 0)) if shared_kv else pair)
    kv_dtype = F32 if shared_kv else BF16
    out_shape = [jax.ShapeDtypeStruct((S, 512), BF16), jax.ShapeDtypeStruct((S, kv_width), kv_dtype),
                 jax.ShapeDtypeStruct((S, kv_width), kv_dtype)]
    out_specs = [pl.BlockSpec((S, BLK), pair), kv_out, kv_out]
    if variant == "a":
        in_specs += [pl.BlockSpec(memory_space=pltpu.SMEM), pl.BlockSpec(memory_space=pltpu.SMEM)]
        args += [sinks, slopes]
        out_shape.append(jax.ShapeDtypeStruct((4, 8, BLK), F32))
        out_specs.append(pl.BlockSpec((None, 8, BLK), lambda p, j: (p, 0, 0)))
    elif variant == "b":
        in_specs += [row_stat, pl.BlockSpec((None, tk, 2), lambda p, j: (p, j, 0))]
        args += [cq_row, ck_col]
        out_shape += [jax.ShapeDtypeStruct((4, S, 2), F32), jax.ShapeDtypeStruct((4, 2, S), F32)]
        out_specs += [pl.BlockSpec((None, tk, 2), lambda p, j: (p, j, 0)), row_stat]
    else:
        in_specs += [pl.BlockSpec((2, 5, BLK, BLK), lambda p, j: (p, 0, 0, 0))]
        args += [bias_t]
        out_shape.append(jax.ShapeDtypeStruct((8, 5, BLK, BLK), F32))
        out_specs.append(pl.BlockSpec((2, 5, BLK, BLK), lambda p, j: (p, 0, 0, 0)))
    return pl.pallas_call(
        body, name=name, out_shape=tuple(out_shape), grid=(4, nb // per_step),
        in_specs=in_specs, out_specs=tuple(out_specs), scratch_shapes=[pltpu.VMEM((BLK, S), F32)],
        compiler_params=_params(),
    )(*args)


def _log_sigmoid(x):
    return jnp.minimum(x, 0.0) - jnp.log(1.0 + jnp.exp(-jnp.abs(x)))


def _forget_fwd(fb, b_forget, name):
    S = fb.shape[0]
    nb = S // GROUP

    def body(fb_ref, b_ref, cum_ref, carry_ref):
        i = pl.program_id(0)
        logf = _log_sigmoid(fb_ref[...] + b_ref[...])
        r = lax.broadcasted_iota(jnp.int32, (GROUP, GROUP), 0)
        c = lax.broadcasted_iota(jnp.int32, (GROUP, GROUP), 1)
        tri = (c <= r).astype(F32)

        @pl.when(i == 0)
        def _():
            carry_ref[...] = jnp.zeros_like(carry_ref)

        cum = jnp.dot(tri, logf, preferred_element_type=F32, precision=HIGHEST) + carry_ref[0:1, :]
        cum_ref[...] = cum
        carry_ref[...] = jnp.broadcast_to(cum[GROUP - 1:GROUP, :], carry_ref.shape)

    return pl.pallas_call(
        body, name=name, out_shape=jax.ShapeDtypeStruct((S, BLK), F32), grid=(nb,),
        in_specs=[pl.BlockSpec((GROUP, BLK), lambda i: (i, 0)), pl.BlockSpec((1, BLK), lambda i: (0, 0))],
        out_specs=pl.BlockSpec((GROUP, BLK), lambda i: (i, 0)),
        scratch_shapes=[pltpu.VMEM((8, BLK), F32)],
        compiler_params=_params(),
    )(fb, b_forget)


def _forget_bwd(dcum_q, dcum_k, fb, b_forget, name):
    S = fb.shape[0]
    nb = S // GROUP

    def body(dq_ref, dk_ref, fb_ref, b_ref, dfb_ref, db_ref, carry_ref):
        g = pl.program_id(0)
        r = lax.broadcasted_iota(jnp.int32, (GROUP, GROUP), 0)
        c = lax.broadcasted_iota(jnp.int32, (GROUP, GROUP), 1)
        tri = (c >= r).astype(F32)

        @pl.when(g == 0)
        def _():
            carry_ref[...] = jnp.zeros_like(carry_ref)

        dcum = dq_ref[...] + dk_ref[...]
        dlogf = jnp.dot(tri, dcum, preferred_element_type=F32, precision=HIGHEST) + carry_ref[0:1, :]
        carry_ref[...] = jnp.broadcast_to(dlogf[0:1, :], carry_ref.shape)
        x = fb_ref[...] + b_ref[...]
        lane = lax.broadcasted_iota(jnp.int32, (1, BLK), 1)
        dfb = jnp.where(lane < N_FORGET, dlogf * jax.nn.sigmoid(-x), 0.0)
        dfb_ref[...] = dfb
        db = jnp.sum(dfb, axis=0, keepdims=True)

        @pl.when(g == 0)
        def _():
            db_ref[...] = db

        @pl.when(g > 0)
        def _():
            db_ref[...] += db

    rev = pl.BlockSpec((GROUP, BLK), lambda g: (nb - 1 - g, 0))
    row = pl.BlockSpec((1, BLK), lambda g: (0, 0))
    return pl.pallas_call(
        body, name=name,
        out_shape=(jax.ShapeDtypeStruct((S, BLK), F32), jax.ShapeDtypeStruct((1, BLK), F32)), grid=(nb,),
        in_specs=[rev, rev, rev, row], out_specs=(rev, row),
        scratch_shapes=[pltpu.VMEM((8, BLK), F32)],
        compiler_params=_params(),
    )(dcum_q, dcum_k, fb, b_forget)


def _skew(x, sign):
    row = lax.broadcasted_iota(jnp.int32, x.shape, 0)
    for b in range(7):
        amount = (1 << b) if sign > 0 else 256 - (1 << b)
        x = jnp.where(((row >> b) & 1) == 1, pltpu.roll(x, amount, 1), x)
    return x


def _rel_bases(rel):
    far = rel[:, 256:257]
    far127 = jnp.broadcast_to(far, (rel.shape[0], 127))
    base0 = jnp.concatenate([rel[:, 128:0:-1], far, rel[:, 255:128:-1]], axis=1)
    base1 = jnp.concatenate([rel[:, 256:128:-1], far, far127], axis=1)
    base0_t = jnp.concatenate([rel[:, 128:256], far, rel[:, 1:128]], axis=1)
    base1_t = jnp.concatenate([jnp.broadcast_to(far, (rel.shape[0], 128)), far, rel[:, 129:256]], axis=1)
    return jnp.stack([base0, base1, base0_t, base1_t], axis=1)


def _rel_expand(bases, name):
    def body(b_ref, t_ref, tt_ref):
        far = jnp.broadcast_to(b_ref[1:2, 0:1], (BLK, BLK))
        for k, out_ref in ((0, t_ref), (2, tt_ref)):
            for d in (0, 1):
                x = jnp.broadcast_to(b_ref[k + d:k + d + 1, :], (BLK, 2 * BLK))
                out_ref[d] = _skew(x, 1)[:, :BLK]
            for d in (2, 3, 4):
                out_ref[d] = far

    out = jax.ShapeDtypeStruct((8, 5, BLK, BLK), F32)
    spec = pl.BlockSpec((None, 5, BLK, BLK), lambda h: (h, 0, 0, 0))
    return pl.pallas_call(
        body, name=name, out_shape=(out, out), grid=(8,),
        in_specs=[pl.BlockSpec((None, 4, 2 * BLK), lambda h: (h, 0, 0))], out_specs=(spec, spec),
        compiler_params=_params(),
    )(bases)


def _rel_reduce(dtiles_t, name):
    def body(dt_ref, o_ref):
        zeros = jnp.zeros((BLK, BLK), F32)
        sums = []
        for d in (0, 1):
            x = _skew(jnp.concatenate([dt_ref[d], zeros], axis=1), -1)
            sums.append(jnp.broadcast_to(jnp.sum(x, axis=0, keepdims=True), (8, 2 * BLK)))
        lane = lax.broadcasted_iota(jnp.int32, (8, 2 * BLK), 1)
        main = pltpu.roll(sums[0], BLK, 1) + jnp.where(lane > BLK, sums[1], 0.0)
        far = jnp.sum(jnp.where(lane < BLK, sums[1], 0.0)[0:1], axis=1, keepdims=True)
        far = far + jnp.sum(jnp.sum(dt_ref[2] + dt_ref[3] + dt_ref[4], axis=0, keepdims=True), axis=1, keepdims=True)
        o_ref[...] = jnp.concatenate([main[0:1], jnp.broadcast_to(far, (1, BLK))], axis=1)

    return pl.pallas_call(
        body, name=name, out_shape=jax.ShapeDtypeStruct((8, 1, 3 * BLK), F32), grid=(8,),
        in_specs=[pl.BlockSpec((None, 5, BLK, BLK), lambda h: (h, 0, 0, 0))],
        out_specs=pl.BlockSpec((None, 1, 3 * BLK), lambda h: (h, 0, 0)),
        compiler_params=_params(),
    )(dtiles_t)


def _final_loss(x, target, g, below, name):
    S, D = x.shape
    ts = _row_tile(S, 256)

    def body(x_ref, t_ref, g_ref, f_ref, gate_ref, dx_ref, loss_ref, dg_ref, dgate_ref, df_ref):
        i = pl.program_id(0)
        xv, gv = x_ref[...], g_ref[...]
        rstd = lax.rsqrt(jnp.mean(xv * xv, axis=-1, keepdims=True) + EPS)
        xhat = xv * rstd
        err = xhat * gv - t_ref[...]
        part = 0.5 * jnp.sum(jnp.mean(err * err, axis=-1, keepdims=True), axis=0, keepdims=True)
        dy = err / D
        dg = jnp.sum(dy * xhat, axis=0, keepdims=True)
        dxhat = dy * gv
        proj = jnp.mean(dxhat * xhat, axis=-1, keepdims=True)
        dx = rstd * (dxhat - xhat * proj)
        dx_ref[...] = dx
        dgate = _gated_residual_bwd(dx, f_ref, gate_ref, df_ref)
        _accumulate_rows(i, [(loss_ref, jnp.broadcast_to(part, loss_ref.shape)), (dg_ref, dg), (dgate_ref, dgate)])

    tile = pl.BlockSpec((ts, D), lambda i: (i, 0))
    row = pl.BlockSpec((1, D), lambda i: (0, 0))
    vec = jax.ShapeDtypeStruct((1, D), F32)
    return pl.pallas_call(
        body, name=name,
        out_shape=(jax.ShapeDtypeStruct((S, D), F32), jax.ShapeDtypeStruct((8, 128), F32), vec, vec,
                   jax.ShapeDtypeStruct((S, D), BF16)),
        grid=(S // ts,), in_specs=[tile, tile, row, tile, row],
        out_specs=(tile, pl.BlockSpec((8, 128), lambda i: (0, 0)), row, row, tile),
        compiler_params=_params(),
    )(x, target, g, *below)


def _ada_fwd(c_all, w_ada, name):
    L, D, E = w_ada.shape

    def body(c_ref, w_ref, o_ref):
        cv = c_ref[...]
        cond = cv * jax.nn.sigmoid(cv)
        o_ref[...] = jnp.dot(cond, w_ref[...], preferred_element_type=F32, precision=HIGHEST)

    return pl.pallas_call(
        body, name=name, out_shape=jax.ShapeDtypeStruct((L, N_DEV, E), F32), grid=(L,),
        in_specs=[pl.BlockSpec((N_DEV, D), lambda l: (0, 0)), pl.BlockSpec((None, D, E), lambda l: (l, 0, 0))],
        out_specs=pl.BlockSpec((None, N_DEV, E), lambda l: (l, 0, 0)),
        compiler_params=_params(),
    )(c_all, w_ada)


def _ada_bwd(c_all_t, dmod, name):
    D = c_all_t.shape[0]
    L, _, E = dmod.shape

    def body(c_ref, d_ref, o_ref):
        cv = c_ref[...]
        cond = cv * jax.nn.sigmoid(cv)
        acc = None
        for b in range(N_DEV):
            t = cond[:, b:b + 1] * d_ref[b:b + 1, :]
            acc = t if acc is None else acc + t
        o_ref[...] = acc

    return pl.pallas_call(
        body, name=name, out_shape=jax.ShapeDtypeStruct((L, D, E), F32), grid=(L,),
        in_specs=[pl.BlockSpec((D, N_DEV), lambda l: (0, 0)), pl.BlockSpec((None, N_DEV, E), lambda l: (l, 0, 0))],
        out_specs=pl.BlockSpec((None, D, E), lambda l: (l, 0, 0)),
        compiler_params=_params(),
    )(c_all_t, dmod)


def _adamw(w, m, v, g_parts, name, me, after=None):
    L, R, C = w.shape
    tr = _row_tile(R, max(8, (256 * 1024 // max(C, 128)) // 8 * 8))
    nr = R // tr
    c1 = 1.0 - ADAM_B1 ** ADAM_STEP
    c2 = 1.0 - ADAM_B2 ** ADAM_STEP
    direct = [isinstance(p, tuple) for p in g_parts]
    n_in = sum(2 if d else 1 for d in direct)

    def body(me_ref, w_ref, m_ref, v_ref, *rest):
        g_refs, (go_ref, d_ref, mo_ref, vo_ref) = list(rest[:n_in]), rest[-4:]
        layer = pl.program_id(0)
        g = None
        for l in range(L):
            land_ref = g_refs.pop(0)
            own = g_refs.pop(0)[...].astype(F32) if direct[l] else None
            gl = None
            for k in range(land_ref.shape[0]):
                part = land_ref[k].astype(F32)
                if direct[l]:
                    part = jnp.where(me_ref[l] == k, own, part)
                gl = part if gl is None else gl + part
            g = gl if g is None else jnp.where(layer == l, gl, g)
        mn = ADAM_B1 * m_ref[...] + (1.0 - ADAM_B1) * g
        vn = ADAM_B2 * v_ref[...] + (1.0 - ADAM_B2) * (g * g)
        m_hat = mn / c1
        v_hat = vn / c2
        go_ref[...] = g
        d_ref[...] = -ADAM_LR * (m_hat / (jnp.sqrt(v_hat) + ADAM_EPS) + ADAM_WD * w_ref[...])
        mo_ref[...] = mn
        vo_ref[...] = vn

    def rows(l, layer, i):
        return jnp.where(layer == l, i, 0 if l > 0 else nr - 1)

    in_specs, operands = [], []
    for l, p in enumerate(g_parts):
        land, sent = p if direct[l] else (p, None)
        in_specs.append(pl.BlockSpec((land.shape[0], tr, C), lambda layer, i, me_ref, l=l: (0, rows(l, layer, i), 0)))
        operands.append(land)
        if direct[l]:
            in_specs.append(pl.BlockSpec((None, tr, C), lambda layer, i, me_ref, l=l: (me_ref[l], rows(l, layer, i), 0)))
            operands.append(sent)
    if after is not None:
        in_specs.append(ANY)
        operands.append(after)
    tile = pl.BlockSpec((None, tr, C), lambda layer, i, me_ref: (layer, i, 0))
    out = jax.ShapeDtypeStruct((L, R, C), F32)
    return pl.pallas_call(
        body, name=name, out_shape=(out, out, out, out),
        grid_spec=pltpu.PrefetchScalarGridSpec(
            num_scalar_prefetch=1, grid=(L, nr), in_specs=[tile, tile, tile] + in_specs,
            out_specs=(tile, tile, tile, tile)),
        compiler_params=_params(),
    )(me, w, m, v, *operands)


def _pair_add(pieces, recv, core, name):
    _, _, R, C = pieces.shape
    tr = _row_tile(R, max(8, (512 * 1024 // max(C, 128)) // 8 * 8))

    def body(core_ref, a_ref, b_ref, o_ref):
        o_ref[...] = (a_ref[...].astype(F32) + b_ref[...].astype(F32)).astype(BF16)

    return pl.pallas_call(
        body, name=name, out_shape=jax.ShapeDtypeStruct((4, R, C), BF16),
        grid_spec=pltpu.PrefetchScalarGridSpec(
            num_scalar_prefetch=1, grid=(4, R // tr),
            in_specs=[pl.BlockSpec((None, None, tr, C), lambda k, i, core_ref: (core_ref[0], k, i, 0)),
                      pl.BlockSpec((None, tr, C), lambda k, i, core_ref: (k, i, 0))],
            out_specs=pl.BlockSpec((None, tr, C), lambda k, i, core_ref: (k, i, 0))),
        compiler_params=_params(),
    )(core, pieces, recv)


MESH = pl.DeviceIdType.MESH
ANY = pl.BlockSpec(memory_space=pl.ANY)


def _position():
    return lax.axis_index("x"), lax.axis_index("y"), lax.axis_index("c")


def _small_all_gather(v, name):
    m_per, n = v.shape

    def body(x_ref, out_ref, send_sems, recv_sems, local_sem):
        x, y, c = _position()
        me, sibling = (x, y, c), (x, y, 1 - c)
        chips = [(1 - x, y), (x, 1 - y), (1 - x, 1 - y)]

        def rows(px, py, pc):
            return out_ref.at[pl.ds((4 * px + 2 * py + pc) * m_per, m_per), :]

        def copy(k, block, to, src=None):
            return pltpu.make_async_remote_copy(
                src_ref=rows(*block) if src is None else src, dst_ref=rows(*block),
                send_sem=send_sems.at[k], recv_sem=recv_sems.at[k], device_id=to, device_id_type=MESH)

        mine = pltpu.make_async_copy(x_ref, rows(*me), local_sem)
        mine.start()
        first = [copy(0, me, sibling, src=x_ref)]
        first += [copy(1 + j, me, (*chip, c), src=x_ref) for j, chip in enumerate(chips)]
        for cp in first:
            cp.start()
        passed = [copy(4 + j, (*chip, c), sibling) for j, chip in enumerate(chips)]
        for j, chip in enumerate(chips):
            copy(1 + j, (*chip, c), me).wait_recv()
            passed[j].start()
        copy(0, sibling, me).wait_recv()
        for j, chip in enumerate(chips):
            copy(4 + j, (*chip, 1 - c), me).wait_recv()
        for cp in first + passed:
            cp.wait_send()
        mine.wait()

    return pl.pallas_call(
        body, name=name, out_shape=jax.ShapeDtypeStruct((N_DEV * m_per, n), v.dtype),
        in_specs=[pl.BlockSpec(memory_space=pltpu.VMEM)], out_specs=pl.BlockSpec(memory_space=pltpu.VMEM),
        scratch_shapes=[pltpu.SemaphoreType.DMA((7,)), pltpu.SemaphoreType.DMA((7,)), pltpu.SemaphoreType.DMA],
    )(v)


def _big_all_gather(shards, name):
    n_arr = len(shards)

    def body(*refs):
        x_refs, out_refs = refs[:n_arr], refs[n_arr:2 * n_arr]
        send_sems, recv_sems, local_sems = refs[2 * n_arr:]
        x, y, c = _position()
        me, sibling = (x, y, c), (x, y, 1 - c)
        chips = [(1 - x, y), (x, 1 - y), (1 - x, 1 - y)]

        def slot(a, px, py, pc):
            return out_refs[a].at[4 * px + 2 * py + pc]

        def copy(a, k, block, to, src=None):
            return pltpu.make_async_remote_copy(
                src_ref=slot(a, *block) if src is None else src, dst_ref=slot(a, *block),
                send_sem=send_sems.at[a, k], recv_sem=recv_sems.at[a, k], device_id=to, device_id_type=MESH)

        mine = [pltpu.make_async_copy(x_refs[a], slot(a, *me), local_sems.at[a]) for a in range(n_arr)]
        for cp in mine:
            cp.start()
        first = []
        for j, chip in enumerate(chips):
            first += [copy(a, 1 + j, me, (*chip, c), src=x_refs[a]) for a in range(n_arr)]
        first += [copy(a, 0, me, sibling, src=x_refs[a]) for a in range(n_arr)]
        for cp in first:
            cp.start()
        passed = []
        for j, chip in enumerate(chips):
            for a in range(n_arr):
                copy(a, 1 + j, (*chip, c), me).wait_recv()
                fwd = copy(a, 4 + j, (*chip, c), sibling)
                fwd.start()
                passed.append(fwd)
        for a in range(n_arr):
            copy(a, 0, sibling, me).wait_recv()
        for j, chip in enumerate(chips):
            for a in range(n_arr):
                copy(a, 4 + j, (*chip, 1 - c), me).wait_recv()
        for cp in first + passed:
            cp.wait_send()
        for cp in mine:
            cp.wait()

    return pl.pallas_call(
        body, name=name,
        out_shape=tuple(jax.ShapeDtypeStruct((N_DEV,) + s.shape, s.dtype) for s in shards),
        in_specs=[ANY] * n_arr, out_specs=tuple([ANY] * n_arr),
        scratch_shapes=[pltpu.SemaphoreType.DMA((n_arr, 7)), pltpu.SemaphoreType.DMA((n_arr, 7)),
                        pltpu.SemaphoreType.DMA((n_arr,))],
    )(*shards)


def _sibling_exchange(pieces, name):
    n_arr = len(pieces)

    def body(*refs):
        p_refs, out_refs = refs[:n_arr], refs[n_arr:2 * n_arr]
        send_sems, recv_sems = refs[2 * n_arr:]
        x, y, c = _position()
        copies = [pltpu.make_async_remote_copy(
            src_ref=p_refs[a].at[1 - c], dst_ref=out_refs[a], send_sem=send_sems.at[a], recv_sem=recv_sems.at[a],
            device_id=(x, y, 1 - c), device_id_type=MESH) for a in range(n_arr)]
        for cp in copies:
            cp.start()
        for cp in copies:
            cp.wait()

    return pl.pallas_call(
        body, name=name,
        out_shape=tuple(jax.ShapeDtypeStruct(p.shape[1:], p.dtype) for p in pieces),
        in_specs=[ANY] * n_arr, out_specs=tuple([ANY] * n_arr),
        scratch_shapes=[pltpu.SemaphoreType.DMA((n_arr,)), pltpu.SemaphoreType.DMA((n_arr,))],
    )(*pieces)


HBM = pl.BlockSpec(memory_space=pltpu.HBM)
SEM = pl.BlockSpec(memory_space=pltpu.SEMAPHORE)
EFFECT = pltpu.SideEffectType.DATAFLOW_SIDE_EFFECTING
RELATIONS = [(rx, ry, rc) for rx in (0, 1) for ry in (0, 1) for rc in (0, 1)][1:]


SAME_CORE = [r for r in RELATIONS if r == (0, 0, 1) or r[2] == 0]


CHIPS = [r for r in RELATIONS if r[2] == 0]


def _exchange_copies(src_refs, land_refs, send_sems, recv_sems, scatter, receive_side, relations):
    x, y, c = _position()
    index = (lambda px, py, pc: 2 * px + py) if relations == CHIPS else (lambda px, py, pc: 4 * px + 2 * py + pc)
    me = index(x, y, c)
    copies = []
    for k, (rx, ry, rc) in enumerate(relations):
        peer = ((1 - x) if rx else x, (1 - y) if ry else y, (1 - c) if rc else c)
        peer_index = index(*peer)
        for a, (src, land) in enumerate(zip(src_refs, land_refs)):
            copies.append(pltpu.make_async_remote_copy(
                src_ref=src.at[peer_index] if scatter else src,
                dst_ref=land.at[peer_index if receive_side else me],
                send_sem=send_sems.at[a * len(relations) + k], recv_sem=recv_sems.at[a * len(relations) + k],
                device_id=peer, device_id_type=MESH))
    return copies


def _exchange_start(srcs, scatter, after, name, relations=RELATIONS):
    n = len(srcs)
    land_shapes = [(s.shape if scatter else (N_DEV,) + s.shape) for s in srcs]

    def body(*refs):
        src_refs, land_refs = refs[:n], refs[n:2 * n]
        send_sems, recv_sems = refs[2 * n + 1], refs[2 * n + 2]
        token = refs[-1]
        for cp in _exchange_copies(src_refs, land_refs, send_sems, recv_sems, scatter, False, relations):
            cp.start()
        token[...] = jnp.zeros_like(token)

    sems = pltpu.SemaphoreType.DMA((n * len(relations),))
    outs = pl.pallas_call(
        body, name=name,
        out_shape=(sems, sems, *[pltpu.HBM(s.shape, s.dtype) for s in srcs],
                   *[pltpu.HBM(shape, s.dtype) for shape, s in zip(land_shapes, srcs)],
                   jax.ShapeDtypeStruct((8, 128), F32)),
        in_specs=[HBM] * (2 * n) + [ANY],
        out_specs=(SEM, SEM, *[HBM] * (2 * n), pl.BlockSpec(memory_space=pltpu.VMEM)),
        input_output_aliases={a: 2 + a for a in range(2 * n)},
        compiler_params=pltpu.CompilerParams(has_side_effects=EFFECT),
    )(*[pltpu.with_memory_space_constraint(s, pltpu.HBM) for s in srcs],
      *[pltpu.with_memory_space_constraint(lax.empty(shape, s.dtype), pltpu.HBM)
        for shape, s in zip(land_shapes, srcs)], after)
    return outs[0], outs[1], outs[2:2 + n], outs[2 + n:2 + 2 * n], outs[-1]


def _exchange_wait(started, scatter, after, name, relations=RELATIONS):
    send_sems, recv_sems, srcs, lands, _ = started
    n = len(srcs)

    def body(*refs):
        src_refs, land_refs = refs[:n], refs[n:2 * n]
        send_sems, recv_sems = refs[2 * n], refs[2 * n + 1]
        copies = _exchange_copies(src_refs, land_refs, send_sems, recv_sems, scatter, True, relations)
        for cp in copies:
            cp.wait_send()
        for cp in copies:
            cp.wait_recv()

    outs = pl.pallas_call(
        body, name=name,
        out_shape=(*[pltpu.HBM(s.shape, s.dtype) for s in srcs], *[pltpu.HBM(t.shape, t.dtype) for t in lands]),
        in_specs=[HBM] * (2 * n) + [SEM, SEM, ANY], out_specs=tuple([HBM] * (2 * n)),
        input_output_aliases={a: a for a in range(2 * n)},
        compiler_params=pltpu.CompilerParams(has_side_effects=EFFECT),
    )(*srcs, *lands, send_sems, recv_sems, after)
    return outs[:n], outs[n:]


def _forward_copies(land_refs, send_sems, recv_sems, receive_side):
    x, y, c = _position()
    copies = []
    for j, (px, py) in enumerate([(1 - x, y), (x, 1 - y), (1 - x, 1 - y)]):
        held, coming = 4 * px + 2 * py + c, 4 * px + 2 * py + (1 - c)
        for a, land in enumerate(land_refs):
            copies.append(pltpu.make_async_remote_copy(
                src_ref=land.at[held], dst_ref=land.at[coming if receive_side else held],
                send_sem=send_sems.at[3 * a + j], recv_sem=recv_sems.at[3 * a + j],
                device_id=(x, y, 1 - c), device_id_type=MESH))
    return copies


def _forward_start(lands, after, name):
    n = len(lands)

    def body(*refs):
        send_sems, recv_sems, token = refs[n + 1], refs[n + 2], refs[-1]
        for cp in _forward_copies(refs[:n], send_sems, recv_sems, False):
            cp.start()
        token[...] = jnp.zeros_like(token)

    sems = pltpu.SemaphoreType.DMA((3 * n,))
    outs = pl.pallas_call(
        body, name=name,
        out_shape=(sems, sems, *[pltpu.HBM(t.shape, t.dtype) for t in lands], jax.ShapeDtypeStruct((8, 128), F32)),
        in_specs=[HBM] * n + [ANY], out_specs=(SEM, SEM, *[HBM] * n, pl.BlockSpec(memory_space=pltpu.VMEM)),
        input_output_aliases={a: 2 + a for a in range(n)},
        compiler_params=pltpu.CompilerParams(has_side_effects=EFFECT),
    )(*lands, after)
    return outs[0], outs[1], outs[2:2 + n], outs[-1]


def _forward_wait(started, after, name):
    send_sems, recv_sems, lands, _ = started
    n = len(lands)

    def body(*refs):
        copies = _forward_copies(refs[:n], refs[n], refs[n + 1], True)
        for cp in copies:
            cp.wait_send()
        for cp in copies:
            cp.wait_recv()

    return pl.pallas_call(
        body, name=name, out_shape=tuple(pltpu.HBM(t.shape, t.dtype) for t in lands),
        in_specs=[HBM] * n + [SEM, SEM, ANY], out_specs=tuple([HBM] * n),
        input_output_aliases={a: a for a in range(n)},
        compiler_params=pltpu.CompilerParams(has_side_effects=EFFECT),
    )(*lands, send_sems, recv_sems, after)


def _place_own(lands, mine, me, name):
    n = len(lands)
    flat = [m.reshape(-1, m.shape[-1]) for m in mine]
    flat_lands = [t.reshape(N_DEV, -1, t.shape[-1]) for t in lands]

    def body(me_ref, *refs):
        for src, dst in zip(refs[:n], refs[2 * n:]):
            dst[...] = src[...]

    in_specs = [pl.BlockSpec((m.shape[0] // 2, m.shape[1]), lambda i, me_ref: (i, 0)) for m in flat]
    out_specs = [pl.BlockSpec((None, m.shape[0] // 2, m.shape[1]), lambda i, me_ref: (me_ref[0], i, 0)) for m in flat]
    outs = pl.pallas_call(
        body, name=name, out_shape=tuple(jax.ShapeDtypeStruct(t.shape, t.dtype) for t in flat_lands),
        grid_spec=pltpu.PrefetchScalarGridSpec(
            num_scalar_prefetch=1, grid=(2,), in_specs=in_specs + [ANY] * n, out_specs=tuple(out_specs)),
        input_output_aliases={1 + n + a: a for a in range(n)},
        compiler_params=_params(),
    )(me, *flat, *flat_lands)
    return [o.reshape(t.shape) for o, t in zip(outs, lands)]


W_IN_SHARD = N_IN // N_DEV
F_SHARD = F_COL // W_IN_SHARD
F_LO = F_COL - F_SHARD * W_IN_SHARD


def _w_ffn_in_view(w):
    return jnp.transpose(w, (0, 2, 1))


def _w_in_segments():
    segments = []
    for d in range(N_DEV):
        if d == F_SHARD:
            segments += [(d, 0, d * W_IN_SHARD, F_LO), (d, F_LO, N_MAIN, N_FORGET),
                         (d, F_LO + N_FORGET, F_COL, W_IN_SHARD - F_LO - N_FORGET)]
        else:
            segments.append((d, 0, d * W_IN_SHARD - (N_FORGET if d > F_SHARD else 0), W_IN_SHARD))
    return segments


def _w_in_rearranged(g, name):
    D = g.shape[1]
    tr = _row_tile(D, 256)

    def body(g_ref, o_ref):
        for d, lo, at, width in _w_in_segments():
            o_ref[:, at:at + width] = g_ref[d, :, lo:lo + width]
        o_ref[:, N_IN:] = jnp.zeros((tr, BLK - N_FORGET), o_ref.dtype)

    return pl.pallas_call(
        body, name=name, out_shape=jax.ShapeDtypeStruct((D, N_MAIN + BLK), g.dtype), grid=(D // tr,),
        in_specs=[pl.BlockSpec((N_DEV, tr, W_IN_SHARD), lambda i: (0, i, 0))],
        out_specs=pl.BlockSpec((tr, N_MAIN + BLK), lambda i: (i, 0)),
        compiler_params=_params(),
    )(g)


def _w_in_pieces(dw_r, name, pair_major=False):
    D = dw_r.shape[0]
    tr = _row_tile(D, 256)
    lead = (2, 4) if pair_major else (N_DEV,)

    def body(x_ref, o_ref):
        for d, lo, at, width in _w_in_segments():
            slot = (d % 2, d // 2) if pair_major else (d,)
            o_ref[(*slot, slice(None), slice(lo, lo + width))] = x_ref[:, at:at + width]

    return pl.pallas_call(
        body, name=name, out_shape=jax.ShapeDtypeStruct((*lead, D, W_IN_SHARD), dw_r.dtype), grid=(D // tr,),
        in_specs=[pl.BlockSpec((tr, N_MAIN + BLK), lambda i: (i, 0))],
        out_specs=pl.BlockSpec((*lead, tr, W_IN_SHARD), lambda i: (*[0] * len(lead), i, 0)),
        compiler_params=_params(),
    )(dw_r)


def _row_pieces(dw):
    return dw.reshape(N_DEV, dw.shape[0] // N_DEV, dw.shape[1])


def _branch_pieces(dw):
    k, w, d = dw.shape
    return jnp.transpose(dw.reshape(k, w, N_DEV, d // N_DEV), (2, 0, 1, 3)).reshape(N_DEV, k * w, d // N_DEV)


def _pairs_col(a):
    return jnp.transpose(a.reshape(a.shape[0], 4, 2), (1, 0, 2))


def _pairs_row(a):
    return jnp.transpose(a.reshape(a.shape[0], 4, 2), (1, 2, 0))


def _heads_from_col(a):
    return jnp.transpose(a, (1, 0, 2)).reshape(a.shape[1], 8)


def _heads_from_row(a):
    return jnp.transpose(a, (2, 0, 1)).reshape(a.shape[2], 8)


def _pad_lanes(a, n):
    return jnp.pad(a, [(0, 0)] * (a.ndim - 1) + [(0, n - a.shape[-1])])


SMALL_SEGMENTS = (("dmod", 2 * 6 * D_MODEL), ("norm_mix_g", 2 * D_MODEL), ("norm_ffn_g", 2 * D_MODEL),
                  ("final_norm_g", D_MODEL), ("b_forget", 128), ("sinks", 128), ("rel_bias", 4224), ("loss", 128))
SMALL_ROWS = 176


def _pack_small(parts):
    flat = [_pad_lanes(parts[name].reshape(1, -1), size) for name, size in SMALL_SEGMENTS]
    total = sum(size for _, size in SMALL_SEGMENTS)
    flat.append(jnp.zeros((1, SMALL_ROWS * 128 - total), F32))
    return jnp.concatenate(flat, axis=1).reshape(SMALL_ROWS, 128)


def _unpack_small(packed, shapes):
    flat = packed.reshape(-1)
    out, pos = {}, 0
    for name, size in SMALL_SEGMENTS:
        shape = shapes[name]
        count = 1
        for d in shape:
            count *= d
        out[name] = flat[pos:pos + count].reshape(shape)
        pos += size
    return out


def kernel(x, c, norm_mix_g, norm_ffn_g, w_ada, b_ada, w_in, b_forget, sinks, rel_bias, w_branch, w_out, w_ffn_in, w_ffn_out, final_norm_g, loss_target, m_norm_mix_g, m_norm_ffn_g, m_w_ada, m_b_ada, m_w_in, m_b_forget, m_sinks, m_rel_bias, m_w_branch, m_w_out, m_w_ffn_in, m_w_ffn_out, m_final_norm_g, v_norm_mix_g, v_norm_ffn_g, v_w_ada, v_b_ada, v_w_in, v_b_forget, v_sinks, v_rel_bias, v_w_branch, v_w_out, v_w_ffn_in, v_w_ffn_out, v_final_norm_g):
    depth = w_in.shape[0]
    S, D = x.shape[1], x.shape[2]
    assert S % GROUP == 0 and S >= ATTN_WINDOW["c"] * BLK
    px, py, pc = _position()
    me = 4 * px + 2 * py + pc
    x0 = x[0]

    assert depth == 2
    big_weights = (w_in, w_branch, w_out, w_ffn_in, w_ffn_out)
    me_arr = jnp.stack([me, me]).astype(jnp.int32)
    me_in_arr = jnp.stack([2 * px + py, me]).astype(jnp.int32)

    def rest_matrices(g_branch, g_out, g_fin, g_fout):
        return (jnp.transpose(g_branch, (1, 2, 0, 3)).reshape(3, 512, D), g_out.reshape(D, D),
                g_fin.reshape(2 * FFN_HIDDEN, D), g_fout.reshape(FFN_HIDDEN, D))

    def arrive(started, after, name):
        mine, landed = _exchange_wait(started, False, after, f"{name}_wait", SAME_CORE)
        return mine, _forward_start(landed, mine[0], f"{name}_forward_start")

    def finish_gather(arrived, after, name):
        mine, forward = arrived
        landed = _forward_wait(forward, after, f"{name}_forward_wait")
        return _place_own(landed, mine, me.astype(jnp.int32).reshape(1), f"{name}_own")

    w_fin_t = _w_ffn_in_view(w_ffn_in)
    shards = [[t.astype(BF16) for t in (w_in[l], w_branch[l], w_out[l], w_fin_t[l], w_ffn_out[l])]
              for l in range(depth)]
    gathered_in0 = _big_all_gather(shards[0][:1], "comm_gather_w_in0")[0]
    gather_rest0 = _exchange_start(shards[0][1:], False, gathered_in0, "comm_gather_rest0_start", SAME_CORE)
    gather1 = _exchange_start(shards[1], False, gather_rest0[4], "comm_gather_weights1_start", SAME_CORE)
    W_in, W_branch, W_out, W_fin, W_fout = ([None, None] for _ in range(5))
    W_in[0] = _w_in_rearranged(gathered_in0, "w_in_rearrange0")

    c_all = _small_all_gather(c.reshape(8, 128), "comm_gather_c").reshape(N_DEV, D)
    mod_cols = _ada_fwd(c_all, w_ada, "ada_fwd")
    mod_all = _small_all_gather(mod_cols.reshape(-1, 128), "comm_gather_mod")
    mod_all = mod_all.reshape(N_DEV, depth, N_DEV, w_ada.shape[2])
    mod_mine = lax.dynamic_index_in_dim(mod_all, me, axis=2, keepdims=False)
    mod = jnp.transpose(mod_mine, (1, 0, 2)).reshape(depth, 6 * D) + b_ada + gather1[4][0:1, 0:1]
    mods = [[mod[l:l + 1, k * D:(k + 1) * D] for k in range(6)] for l in range(depth)]

    slopes = jnp.exp2(-jnp.arange(1, 9, dtype=F32))
    saved = []
    xs = x0
    for l in range(depth):
        if l == 1:
            g_in1, *g_rest1 = finish_gather(arrived1, xs, "comm_gather_weights1")
            W_in[1] = _w_in_rearranged(g_in1, "w_in_rearrange1")
            W_branch[1], W_out[1], W_fin[1], W_fout[1] = rest_matrices(*g_rest1)
        sh_m, sc_m, g_m, sh_f, sc_f, g_f = mods[l]
        gm, gf = norm_mix_g[l:l + 1], norm_ffn_g[l:l + 1]
        bfor = _pad_lanes(b_forget[l:l + 1], BLK)
        h = _norm_mod_fwd(xs, gm, sh_m, sc_m, f"norm_mix_fwd{l}")
        qkv = _matmul(h, W_in[l], "nn", BF16, f"proj_qkv{l}", TILES["proj_qkv"], n=N_QKV)
        gates = _matmul(h, W_in[l], "nn", F32, f"proj_gates{l}", TILES["proj_gates"], n=N_GATES,
                        b_off=N_QKV // TILES["proj_gates"][1])
        fb = _matmul(h, W_in[l], "nn", F32, f"proj_forget{l}", TILES["proj_forget"], n=BLK, b_off=N_MAIN // BLK)
        cum = _forget_fwd(fb, bfor, f"forget_fwd{l}")[:, :N_FORGET]
        cum_col, cum_row = _pairs_col(cum), _pairs_row(cum)
        tiles, tiles_t = _rel_expand(_rel_bases(rel_bias[l]), f"rel_expand{l}")
        o_a, lse_a = _attn_fwd("a", qkv, f"attn_a_fwd{l}", sinks=sinks[l], slopes=slopes)
        o_b, lse_b = _attn_fwd("b", qkv, f"attn_b_fwd{l}", cq_col=cum_col, ck_row=cum_row)
        arrived_rest0 = arrive(gather_rest0, o_b, "comm_gather_rest0") if l == 0 else None
        o_c, lse_c = _attn_fwd("c", qkv, f"attn_c_fwd{l}", bias=tiles, after=arrived_rest0[1][3] if l == 0 else None)
        if l == 0:
            W_branch[0], W_out[0], W_fin[0], W_fout[0] = rest_matrices(
                *finish_gather(arrived_rest0, o_c, "comm_gather_rest0"))
        x1, merged, mix = _merge_fwd(o_a, o_b, o_c, gates, W_branch[l], W_out[l], xs, g_m, f"merge_fwd{l}")
        h2 = _norm_mod_fwd(x1, gf, sh_f, sc_f, f"norm_ffn_fwd{l}")
        act = _ffn_in_fwd(h2, W_fin[l], f"ffn_in_fwd{l}")
        if l == 0:
            arrived1 = arrive(gather1, act, "comm_gather_weights1")
        x2, ffn = _matmul_resid(act, W_fout[l], x1, g_f, f"ffn_out{l}", TILES["ffn_out"],
                                after=arrived1[1][3] if l == 0 else None)
        saved.append(dict(x=xs, h=h, qkv=qkv, gates=gates, fb=fb, bfor=bfor, cum_col=cum_col, cum_row=cum_row,
                          tiles_t=tiles_t, o=(o_a, o_b, o_c), lse=(lse_a, lse_b, lse_c), merged=merged, mix=mix,
                          x1=x1, h2=h2, act=act, ffn=ffn))
        xs = x2

    dx, loss_tile, d_final_g, d_g_f, df = _final_loss(
        xs, loss_target[0], final_norm_g.reshape(1, D), (saved[-1]["ffn"], mods[-1][5]), "final_loss")

    grads = {k: [None] * depth for k in ("w_in", "w_branch", "w_out", "w_ffn_in", "w_ffn_out", "norm_mix_g",
                                          "norm_ffn_g", "b_forget", "sinks", "rel_bias", "dmod")}
    def rest_pieces(l):
        return [_branch_pieces(grads["w_branch"][l]), _row_pieces(grads["w_out"][l]),
                _row_pieces(grads["w_ffn_in"][l]), _row_pieces(grads["w_ffn_out"][l])]

    reduce1 = reduce_rest0 = reduce_in0 = None
    for l in reversed(range(depth)):
        sv = saved[l]
        sh_m, sc_m, g_m, sh_f, sc_f, g_f = mods[l]
        gm, gf = norm_mix_g[l:l + 1], norm_ffn_g[l:l + 1]
        du_g, du_u = _ffn_mid_bwd(sv["h2"], df, W_fin[l], W_fout[l], f"ffn_mid_bwd{l}")
        du = jnp.concatenate([du_g, du_u], axis=1)
        grads["w_ffn_out"][l] = _matmul(sv["act"], df, "tn", BF16, f"wgrad_ffn_out{l}", TILES["wgrad_ffn_out"])
        grads["w_ffn_in"][l] = _matmul(du, sv["h2"], "tn", BF16, f"wgrad_ffn_in{l}", TILES["wgrad_ffn_in"])
        dh2 = _matmul(du, W_fin[l], "nn", F32, f"dgrad_ffn_in{l}", TILES["dgrad_ffn_in"])
        dx1, d_sh_f, d_sc_f, d_gf, d_g_m, dmix = _norm_mod_bwd(sv["x1"], dh2, dx, gf, sc_f, f"norm_ffn_bwd{l}",
                                                               below=(sv["mix"], g_m))
        grads["w_out"][l] = _matmul(sv["merged"], dmix, "tn", BF16, f"wgrad_out{l}", TILES["wgrad_out"])
        o_a, o_b, o_c = sv["o"]
        dgates, d_w_branch, do_a, do_b, do_c, dl_a, dl_b, dl_c = _merge_bwd(
            dmix, o_a, o_b, o_c, sv["gates"], W_branch[l], W_out[l], f"merge_bwd{l}")
        grads["w_branch"][l] = d_w_branch.astype(BF16)
        lse_rows = [_pairs_row(_heads_from_col(t)) for t in sv["lse"]]
        if l == 0:
            reduce_rest0 = _exchange_start(rest_pieces(0), True, dgates, "comm_reduce_rest0_start")
            lse_rows = [t + reduce_rest0[4][0:1, 0:1] for t in lse_rows]
        dq_a, dk_a, dv_a, dsink = _attn_bwd("a", sv["qkv"], do_a, lse_rows[0], _pairs_row(dl_a), f"attn_a_bwd{l}",
                                            sinks=sinks[l], slopes=slopes)
        dq_b, dk_b, dv_b, dck, dcq = _attn_bwd("b", sv["qkv"], do_b, lse_rows[1], _pairs_row(dl_b),
                                               f"attn_b_bwd{l}", cq_row=sv["cum_row"], ck_col=sv["cum_col"])
        dq_c, dk_c, dv_c, dtiles_t = _attn_bwd("c", sv["qkv"], do_c, lse_rows[2], _pairs_row(dl_c),
                                               f"attn_c_bwd{l}", bias_t=sv["tiles_t"])
        grads["sinks"][l] = dsink[:, :2, 0].reshape(8)
        grads["rel_bias"][l] = _rel_reduce(dtiles_t, f"rel_reduce{l}")[:, 0, :N_REL]
        dcum_k = _pad_lanes(_heads_from_col(dck), BLK)
        dcum_q = _pad_lanes(_heads_from_row(dcq), BLK)
        dfb, d_bfor = _forget_bwd(dcum_q, dcum_k, sv["fb"], sv["bfor"], f"forget_bwd{l}")
        grads["b_forget"][l] = d_bfor[0, :N_FORGET]
        dproj = jnp.concatenate(
            [t.astype(BF16) for t in (dq_a, dk_a, dv_a, dq_b, dk_b, dv_b, dq_c, dk_c, dv_c, dgates, dfb)],
            axis=1)
        grads["w_in"][l] = _matmul(sv["h"], dproj, "tn", BF16, f"wgrad_in{l}", TILES["wgrad_in"])
        if l == 1:
            reduce1 = _exchange_start([_w_in_pieces(grads["w_in"][1], "w_in_pieces1")] + rest_pieces(1), True, dproj,
                                      "comm_reduce1_start")
        dh = _matmul(dproj, W_in[l], "nt", F32, f"dgrad_in{l}", TILES["dgrad_in"], after=reduce1[4] if l == 1 else None)
        d_g_f_here = d_g_f
        if l > 0:
            dx, d_sh_m, d_sc_m, d_gm, d_g_f, df = _norm_mod_bwd(sv["x"], dh, dx1, gm, sc_m, f"norm_mix_bwd{l}",
                                                                below=(saved[l - 1]["ffn"], mods[l - 1][5]))
        else:
            dx, d_sh_m, d_sc_m, d_gm = _norm_mod_bwd(sv["x"], dh, dx1, gm, sc_m, f"norm_mix_bwd{l}")
        grads["norm_mix_g"][l] = d_gm[0]
        grads["norm_ffn_g"][l] = d_gf[0]
        grads["dmod"][l] = jnp.concatenate([d_sh_m, d_sc_m, d_g_m, d_sh_f, d_sc_f, d_g_f_here], axis=1)[0]

    grad_x = dx.reshape(x.shape)

    small_shapes = dict(dmod=b_ada.shape, norm_mix_g=norm_mix_g.shape, norm_ffn_g=norm_ffn_g.shape,
                        final_norm_g=final_norm_g.shape, b_forget=b_forget.shape, sinks=sinks.shape,
                        rel_bias=rel_bias.shape, loss=())
    mine_small = _pack_small(dict(
        loss=_pad_lanes(loss_tile[0:1, 0:1], 128),
        dmod=jnp.stack(grads["dmod"]), norm_mix_g=jnp.stack(grads["norm_mix_g"]),
        norm_ffn_g=jnp.stack(grads["norm_ffn_g"]), final_norm_g=d_final_g[0],
        b_forget=_pad_lanes(jnp.stack(grads["b_forget"]).reshape(1, -1), 128),
        sinks=_pad_lanes(jnp.stack(grads["sinks"]).reshape(1, -1), 128),
        rel_bias=_pad_lanes(jnp.stack(grads["rel_bias"]).reshape(1, -1), 4224)))
    all_small = _small_all_gather(mine_small, "comm_gather_small").reshape(N_DEV, SMALL_ROWS, 128)
    pieces_in0 = _w_in_pieces(grads["w_in"][0], "w_in_pieces0", pair_major=True)
    from_sibling = _sibling_exchange([pieces_in0], "comm_reduce_in0_sibling")[0]
    pair_sum_in0 = _pair_add(pieces_in0, from_sibling, pc.astype(jnp.int32).reshape(1), "pair_add_in0")
    reduce_in0 = _exchange_start([pair_sum_in0], True, all_small, "comm_reduce_in0_start", CHIPS)
    in0_started = reduce_in0[4]

    def pack_params(b_ada_, nm, nf, fn, bf, sk, rb):
        return _pack_small(dict(dmod=b_ada_, norm_mix_g=nm, norm_ffn_g=nf, final_norm_g=fn, loss=jnp.zeros((1, 128), F32),
                                b_forget=_pad_lanes(bf.reshape(1, -1), 128), sinks=_pad_lanes(sk.reshape(1, -1), 128),
                                rel_bias=_pad_lanes(rb.reshape(1, -1), 4224)))

    small_out = _adamw(
        pack_params(b_ada, norm_mix_g, norm_ffn_g, final_norm_g, b_forget, sinks, rel_bias)[None],
        pack_params(m_b_ada, m_norm_mix_g, m_norm_ffn_g, m_final_norm_g, m_b_forget, m_sinks, m_rel_bias)[None],
        pack_params(v_b_ada, v_norm_mix_g, v_norm_ffn_g, v_final_norm_g, v_b_forget, v_sinks, v_rel_bias)[None],
        [all_small], "adamw_small", me_arr, after=in0_started)
    small_out = [_unpack_small(t[0], small_shapes) for t in small_out]

    dmod_all = all_small[:, :96].reshape(N_DEV, depth, 6 * D)
    dmod_cols = lax.dynamic_slice_in_dim(dmod_all, me * w_ada.shape[2], w_ada.shape[2], axis=2)
    d_w_ada = _ada_bwd(jnp.transpose(c_all), jnp.transpose(dmod_cols, (1, 0, 2)), "ada_bwd")

    big = {"w_ada": _adamw(w_ada, m_w_ada, v_w_ada, [d_w_ada[l:l + 1] for l in range(depth)], "adamw_w_ada", me_arr,
                           after=in0_started)}
    sent1, landed1 = _exchange_wait(reduce1, True, big["w_ada"][0], "comm_reduce1_wait")
    sent_rest0, landed_rest0 = _exchange_wait(reduce_rest0, True, landed1[0], "comm_reduce_rest0_wait")
    parts = {"w_in": [None, (landed1[0], sent1[0])]}
    for a, name in enumerate(("w_branch", "w_out", "w_ffn_in", "w_ffn_out")):
        parts[name] = [(landed_rest0[a], sent_rest0[a]), (landed1[1 + a], sent1[1 + a])]

    def update(name, w, m, v, view=lambda t: t):
        per_layer = lambda t: t.reshape(depth, -1, t.shape[-1])
        outs = _adamw(*[per_layer(view(t)) for t in (w, m, v)], parts[name], f"adamw_{name}",
                      me_in_arr if name == "w_in" else me_arr)
        big[name] = [view(t).reshape(w.shape) for t in outs]

    update("w_ffn_in", w_ffn_in, m_w_ffn_in, v_w_ffn_in, _w_ffn_in_view)
    update("w_ffn_out", w_ffn_out, m_w_ffn_out, v_w_ffn_out)
    update("w_branch", w_branch, m_w_branch, v_w_branch)
    update("w_out", w_out, m_w_out, v_w_out)
    sent_in0, landed_in0 = _exchange_wait(reduce_in0, True, big["w_out"][0], "comm_reduce_in0_wait", CHIPS)
    parts["w_in"][0] = (landed_in0[0], sent_in0[0])
    update("w_in", w_in, m_w_in, v_w_in)

    def leaf(kind, name):
        if name in big:
            return big[name][kind]
        return small_out[kind]["dmod" if name == "b_ada" else name]

    order = ["norm_mix_g", "norm_ffn_g", "w_ada", "b_ada", "w_in", "b_forget", "sinks", "rel_bias", "w_branch",
             "w_out", "w_ffn_in", "w_ffn_out", "final_norm_g"]
    loss = small_out[0]["loss"]
    return (loss, grad_x, *[leaf(0, n) for n in order], *[leaf(1, n) for n in order],
            *[leaf(2, n) for n in order], *[leaf(3, n) for n in order])
```

```python
import functools

import jax
import jax.numpy as jnp
from jax import lax
from jax.experimental import pallas as pl
from jax.experimental.pallas import tpu as pltpu

F32 = jnp.float32
BF16 = jnp.bfloat16
NEG_INF = -1e30
EPS = 1e-6
N_DEV = 8
BLK = 128
GROUP = 4 * BLK
VMEM_LIMIT_BYTES = 56 * 1024 * 1024

D_MODEL = 1024
N_QKV = 3840
N_GATES = 3072
N_MAIN = N_QKV + N_GATES
N_FORGET = 8
N_IN = N_MAIN + N_FORGET
F_COL = 2304
FFN_HIDDEN = 2816
N_REL = 257

ADAM_LR, ADAM_B1, ADAM_B2, ADAM_EPS, ADAM_WD, ADAM_STEP = 0.001, 0.9, 0.999, 1e-08, 0.01, 10

NN = (((1,), (0,)), ((), ()))
NT = (((1,), (1,)), ((), ()))
TN = (((0,), (0,)), ((), ()))
HIGHEST = lax.Precision.HIGHEST

ATTN_COLS = {"a": (0, 4, 5), "b": (6, 10, 14), "c": (18, 22, 26)}
ATTN_WINDOW = {"a": 2, "c": 5}
ATTN_BLOCKS_PER_STEP = {"a": 8, "b": GROUP // BLK, "c": 2}
ROW_TILE = 512


def _params():
    return pltpu.CompilerParams(vmem_limit_bytes=VMEM_LIMIT_BYTES)


def _tile(n, target):
    best = None
    t = 128
    while t <= min(n, target):
        if n % t == 0:
            best = t
        t += 128
    return best if best is not None else n


def _row_tile(n, target):
    t = min(n, target)
    while n % t:
        t -= 8
    return t


TILES = {
    "proj": (1024, 768, 1024), "proj_forget": (1024, 128, 1024),
    "ffn_out": (1024, 512, 2816), "ffn_fused": (512, 1408),
    "wgrad_ffn_out": (1408, 1024, 1024), "wgrad_ffn_in": (1408, 1024, 1024), "dgrad_ffn_in": (1024, 1024, 1408),
    "wgrad_out": (1024, 1024, 1024),
    "wgrad_in": (1024, 1408, 1024), "dgrad_in": (1024, 1024, 1408),
}


def _matmul(a, b, mode, out_dtype, name, tiles, *, n=None, a_off=0, b_off=0, m=None, after=None):
    tm, tn, tk = tiles
    if mode == "nn":
        M, K = a.shape if m is None else (m, a.shape[1])
        N = b.shape[1] if n is None else n
    elif mode == "nt":
        M, K = a.shape
        N = b.shape[0] if n is None else n
    else:
        K = a.shape[0]
        M = a.shape[1] if m is None else m
        N = b.shape[1] if n is None else n
    tm = _tile(M, tm) if M % 128 == 0 else M
    tn = _tile(N, tn)
    tk = _tile(K, tk)
    nk = K // tk
    dims = {"nn": NN, "nt": NT, "tn": TN}[mode]
    if mode == "nn":
        a_spec = pl.BlockSpec((tm, tk), lambda i, j, k: (i + a_off, k))
        b_spec = pl.BlockSpec((tk, tn), lambda i, j, k: (k, j + b_off))
    elif mode == "nt":
        a_spec = pl.BlockSpec((tm, tk), lambda i, j, k: (i + a_off, k))
        b_spec = pl.BlockSpec((tn, tk), lambda i, j, k: (j + b_off, k))
    else:
        a_spec = pl.BlockSpec((tk, tm), lambda i, j, k: (k, i + a_off))
        b_spec = pl.BlockSpec((tk, tn), lambda i, j, k: (k, j + b_off))

    def body(a_ref, b_ref, *rest):
        o_ref, acc_ref = rest[-2:]
        k = pl.program_id(2)
        part = lax.dot_general(a_ref[...], b_ref[...], dims, preferred_element_type=F32)
        if nk == 1:
            o_ref[...] = part.astype(o_ref.dtype)
        else:
            @pl.when(k == 0)
            def _():
                acc_ref[...] = part

            @pl.when(k > 0)
            def _():
                acc_ref[...] += part

            @pl.when(k == nk - 1)
            def _():
                o_ref[...] = acc_ref[...].astype(o_ref.dtype)

    return pl.pallas_call(
        body, name=name,
        out_shape=jax.ShapeDtypeStruct((M, N), out_dtype),
        grid=(M // tm, N // tn, nk),
        in_specs=[a_spec, b_spec] + ([ANY] if after is not None else []),
        out_specs=pl.BlockSpec((tm, tn), lambda i, j, k: (i, j)),
        scratch_shapes=[pltpu.VMEM((tm, tn) if nk > 1 else (8, 128), F32)],
        compiler_params=_params(),
    )(a, b, *([after] if after is not None else []))


def _project(h, w, name):
    S, D = h.shape
    tm, tn, _ = TILES["proj"]
    tm = _tile(S, tm)
    nq, ng = N_QKV // tn, N_GATES // tn

    def body(h_ref, w_ref, q_ref, g_ref):
        j = pl.program_id(1)
        acc = jnp.dot(h_ref[...], w_ref[...], preferred_element_type=F32)

        @pl.when(j < nq)
        def _():
            q_ref[...] = acc.astype(BF16)

        @pl.when(j >= nq)
        def _():
            g_ref[...] = acc

    return pl.pallas_call(
        body, name=name,
        out_shape=(jax.ShapeDtypeStruct((S, N_QKV), BF16), jax.ShapeDtypeStruct((S, N_GATES), F32)),
        grid=(S // tm, nq + ng),
        in_specs=[pl.BlockSpec((tm, D), lambda i, j: (i, 0)), pl.BlockSpec((D, tn), lambda i, j: (0, j))],
        out_specs=(pl.BlockSpec((tm, tn), lambda i, j: (i, jnp.minimum(j, nq - 1))),
                   pl.BlockSpec((tm, tn), lambda i, j: (i, jnp.maximum(j - nq, 0)))),
        compiler_params=_params(),
    )(h, w)


def _matmul_resid(a, b, resid, gate, name, tiles, after=None):
    M, K = a.shape
    N = b.shape[1]
    tm, tn, tk = (_tile(d, t) for d, t in zip((M, N, K), tiles))
    nk = K // tk

    def body(a_ref, b_ref, r_ref, g_ref, *rest):
        o_ref, s_ref, acc_ref = rest[-3:]
        k = pl.program_id(2)
        part = jnp.dot(a_ref[...], b_ref[...], preferred_element_type=F32)

        def finish(acc):
            o_ref[...] = r_ref[...] + g_ref[...] * acc
            s_ref[...] = acc.astype(BF16)

        if nk == 1:
            finish(part)
        else:
            @pl.when(k == 0)
            def _():
                acc_ref[...] = part

            @pl.when(k > 0)
            def _():
                acc_ref[...] += part

            @pl.when(k == nk - 1)
            def _():
                finish(acc_ref[...])

    return pl.pallas_call(
        body, name=name,
        out_shape=(jax.ShapeDtypeStruct((M, N), F32), jax.ShapeDtypeStruct((M, N), BF16)),
        grid=(M // tm, N // tn, nk),
        in_specs=[pl.BlockSpec((tm, tk), lambda i, j, k: (i, k)),
                  pl.BlockSpec((tk, tn), lambda i, j, k: (k, j)),
                  pl.BlockSpec((tm, tn), lambda i, j, k: (i, j)),
                  pl.BlockSpec((1, tn), lambda i, j, k: (0, j))] + ([ANY] if after is not None else []),
        out_specs=(pl.BlockSpec((tm, tn), lambda i, j, k: (i, j)),
                   pl.BlockSpec((tm, tn), lambda i, j, k: (i, j))),
        scratch_shapes=[pltpu.VMEM((tm, tn) if nk > 1 else (8, 128), F32)],
        compiler_params=_params(),
    )(a, b, resid, gate, *([after] if after is not None else []))


def _norm_mod_fwd(x, g, shift, scale, name):
    S, D = x.shape
    ts = _row_tile(S, ROW_TILE)

    def body(x_ref, g_ref, sh_ref, sc_ref, h_ref):
        xv = x_ref[...]
        rstd = lax.rsqrt(jnp.mean(xv * xv, axis=-1, keepdims=True) + EPS)
        y = xv * rstd * g_ref[...]
        h_ref[...] = (y * (1.0 + sc_ref[...]) + sh_ref[...]).astype(BF16)

    row = pl.BlockSpec((1, D), lambda i: (0, 0))
    return pl.pallas_call(
        body, name=name, out_shape=jax.ShapeDtypeStruct((S, D), BF16), grid=(S // ts,),
        in_specs=[pl.BlockSpec((ts, D), lambda i: (i, 0)), row, row, row],
        out_specs=pl.BlockSpec((ts, D), lambda i: (i, 0)),
        compiler_params=_params(),
    )(x, g, shift, scale)


def _accumulate_rows(i, pairs):
    @pl.when(i == 0)
    def _():
        for ref, value in pairs:
            ref[...] = value

    @pl.when(i > 0)
    def _():
        for ref, value in pairs:
            ref[...] += value


def _gated_residual_bwd(dx, f_ref, gate_ref, df_ref):
    df_ref[...] = (dx * gate_ref[...]).astype(BF16)
    return jnp.sum(dx * f_ref[...].astype(F32), axis=0, keepdims=True)


def _norm_mod_bwd(x, dh, dres, g, scale, name, below=None):
    S, D = x.shape
    ts = _row_tile(S, ROW_TILE)

    def body(x_ref, dh_ref, dr_ref, g_ref, sc_ref, *rest):
        i = pl.program_id(0)
        xv, dhv, gv = x_ref[...], dh_ref[...], g_ref[...]
        rstd = lax.rsqrt(jnp.mean(xv * xv, axis=-1, keepdims=True) + EPS)
        xhat = xv * rstd
        dn = dhv * (1.0 + sc_ref[...])
        dxhat = dn * gv
        proj = jnp.mean(dxhat * xhat, axis=-1, keepdims=True)
        dx = dr_ref[...] + rstd * (dxhat - xhat * proj)
        sums = [jnp.sum(dhv, axis=0, keepdims=True), jnp.sum(dhv * (xhat * gv), axis=0, keepdims=True),
                jnp.sum(dn * xhat, axis=0, keepdims=True)]
        if below is None:
            dx_ref, *sum_refs = rest
        else:
            f_ref, gate_ref, dx_ref, *sum_refs, df_ref = rest
            sums.append(_gated_residual_bwd(dx, f_ref, gate_ref, df_ref))
        dx_ref[...] = dx
        _accumulate_rows(i, list(zip(sum_refs, sums)))

    tile = pl.BlockSpec((ts, D), lambda i: (i, 0))
    row = pl.BlockSpec((1, D), lambda i: (0, 0))
    vec = jax.ShapeDtypeStruct((1, D), F32)
    fused = below is not None
    return pl.pallas_call(
        body, name=name,
        out_shape=(jax.ShapeDtypeStruct((S, D), F32), vec, vec, vec)
        + ((vec, jax.ShapeDtypeStruct((S, D), BF16)) if fused else ()),
        grid=(S // ts,),
        in_specs=[tile, tile, tile, row, row] + ([tile, row] if fused else []),
        out_specs=(tile, row, row, row) + ((row, tile) if fused else ()),
        compiler_params=_params(),
    )(x, dh, dres, g, scale, *(below if fused else ()))


def _ffn_in_fwd(h, w_t, name):
    S, D = h.shape
    F = w_t.shape[0] // 2
    tm, tn = _tile(S, TILES["ffn_fused"][0]), _tile(F, TILES["ffn_fused"][1])
    nj = F // tn

    def body(h_ref, wg_ref, wu_ref, o_ref):
        hv = h_ref[...]
        ug = lax.dot_general(hv, wg_ref[...], NT, preferred_element_type=F32)
        uu = lax.dot_general(hv, wu_ref[...], NT, preferred_element_type=F32)
        o_ref[...] = (ug * jax.nn.sigmoid(ug) * uu).astype(BF16)

    return pl.pallas_call(
        body, name=name, out_shape=jax.ShapeDtypeStruct((S, F), BF16), grid=(nj, S // tm),
        in_specs=[pl.BlockSpec((tm, D), lambda j, i: (i, 0)),
                  pl.BlockSpec((tn, D), lambda j, i: (j, 0)),
                  pl.BlockSpec((tn, D), lambda j, i: (j + nj, 0))],
        out_specs=pl.BlockSpec((tm, tn), lambda j, i: (i, j)),
        compiler_params=_params(),
    )(h, w_t, w_t)


def _ffn_mid_bwd(h, df, w_in_t, w_out, name):
    S, D = h.shape
    F = w_in_t.shape[0] // 2
    tm, tn = _tile(S, TILES["ffn_fused"][0]), _tile(F, TILES["ffn_fused"][1])
    nj = F // tn

    def body(h_ref, df_ref, wg_ref, wu_ref, wo_ref, dg_ref, du_ref):
        hv = h_ref[...]
        ug = lax.dot_general(hv, wg_ref[...], NT, preferred_element_type=F32)
        uu = lax.dot_general(hv, wu_ref[...], NT, preferred_element_type=F32)
        dact = lax.dot_general(df_ref[...], wo_ref[...], NT, preferred_element_type=F32)
        sig = jax.nn.sigmoid(ug)
        dg_ref[...] = (dact * uu * (sig * (1.0 + ug * (1.0 - sig)))).astype(BF16)
        du_ref[...] = (dact * (ug * sig)).astype(BF16)

    out = jax.ShapeDtypeStruct((S, F), BF16)
    return pl.pallas_call(
        body, name=name, out_shape=(out, out), grid=(nj, S // tm),
        in_specs=[pl.BlockSpec((tm, D), lambda j, i: (i, 0)),
                  pl.BlockSpec((tm, D), lambda j, i: (i, 0)),
                  pl.BlockSpec((tn, D), lambda j, i: (j, 0)),
                  pl.BlockSpec((tn, D), lambda j, i: (j + nj, 0)),
                  pl.BlockSpec((tn, D), lambda j, i: (j, 0))],
        out_specs=(pl.BlockSpec((tm, tn), lambda j, i: (i, j)), pl.BlockSpec((tm, tn), lambda j, i: (i, j))),
        compiler_params=_params(),
    )(h, df, w_in_t, w_in_t, w_out)


def _merge_fwd(o_a, o_b, o_c, gates, w_branch, w_out, resid, gate, name, *, tm=512):
    S, W = o_a.shape
    D = w_branch.shape[2]
    tm = _row_tile(S, tm)

    def body(oa_ref, ob_ref, oc_ref, g_ref, w_ref, wo_ref, r_ref, gm_ref, x_ref, m_ref, mix_ref):
        acc = None
        for k, o_ref in enumerate((oa_ref, ob_ref, oc_ref)):
            y = jnp.dot(o_ref[...], w_ref[k], preferred_element_type=F32)
            t = jax.nn.sigmoid(g_ref[:, k * D:(k + 1) * D]) * y
            acc = t if acc is None else acc + t
        merged = acc.astype(BF16)
        m_ref[...] = merged
        mix = jnp.dot(merged, wo_ref[...], preferred_element_type=F32)
        x_ref[...] = r_ref[...] + gm_ref[...] * mix
        mix_ref[...] = mix.astype(BF16)

    o_spec = pl.BlockSpec((tm, W), lambda i: (i, 0))
    tile = pl.BlockSpec((tm, D), lambda i: (i, 0))
    return pl.pallas_call(
        body, name=name,
        out_shape=(jax.ShapeDtypeStruct((S, D), F32), jax.ShapeDtypeStruct((S, D), BF16), jax.ShapeDtypeStruct((S, D), BF16)),
        grid=(S // tm,),
        in_specs=[o_spec, o_spec, o_spec, pl.BlockSpec((tm, 3 * D), lambda i: (i, 0)),
                  pl.BlockSpec((3, W, D), lambda i: (0, 0, 0)), pl.BlockSpec((D, D), lambda i: (0, 0)),
                  tile, pl.BlockSpec((1, D), lambda i: (0, 0))],
        out_specs=(tile, tile, tile),
        compiler_params=_params(),
    )(o_a, o_b, o_c, gates, w_branch, w_out, resid, gate)


def _merge_bwd(dmix, o_a, o_b, o_c, gates, w_branch, w_out, name, *, tm=256):
    S, W = o_a.shape
    D = w_branch.shape[2]
    tm = _row_tile(S, tm)
    n_heads = W // 64

    def body(dm_ref, oa_ref, ob_ref, oc_ref, g_ref, w_ref, wo_ref, dg_ref, dw_ref,
             doa_ref, dob_ref, doc_ref, dla_ref, dlb_ref, dlc_ref):
        first = pl.program_id(0) == 0
        dm = lax.dot_general(dm_ref[...], wo_ref[...], NT, preferred_element_type=F32)
        branches = ((oa_ref, doa_ref, dla_ref), (ob_ref, dob_ref, dlb_ref), (oc_ref, doc_ref, dlc_ref))
        for k, (o_ref, do_ref, dl_ref) in enumerate(branches):
            wk = w_ref[k]
            ov = o_ref[...]
            y = jnp.dot(ov, wk, preferred_element_type=F32)
            g = jax.nn.sigmoid(g_ref[:, k * D:(k + 1) * D])
            dy = (dm * g).astype(BF16)
            dwk = lax.dot_general(ov, dy, TN, preferred_element_type=F32)

            @pl.when(first)
            def _(k=k, dwk=dwk):
                dw_ref[k] = dwk

            @pl.when(jnp.logical_not(first))
            def _(k=k, dwk=dwk):
                dw_ref[k] += dwk
            dg_ref[:, k * D:(k + 1) * D] = (dm * y * (g * (1.0 - g))).astype(BF16)
            do16 = lax.dot_general(dy, wk, NT, preferred_element_type=F32).astype(BF16)
            do_ref[...] = do16
            prod = do16.astype(F32) * ov.astype(F32)
            for h in range(n_heads):
                dl_ref[:, h:h + 1] = jnp.sum(prod[:, 64 * h:64 * (h + 1)], axis=1, keepdims=True)

    o_spec = pl.BlockSpec((tm, W), lambda i: (i, 0))
    wide = pl.BlockSpec((tm, 3 * D), lambda i: (i, 0))
    dl_spec = pl.BlockSpec((tm, n_heads), lambda i: (i, 0))
    o_out = jax.ShapeDtypeStruct((S, W), BF16)
    wide_out = jax.ShapeDtypeStruct((S, 3 * D), BF16)
    dl_out = jax.ShapeDtypeStruct((S, n_heads), F32)
    whole = pl.BlockSpec((3, W, D), lambda i: (0, 0, 0))
    return pl.pallas_call(
        body, name=name,
        out_shape=(wide_out, jax.ShapeDtypeStruct((3, W, D), F32), o_out, o_out, o_out, dl_out, dl_out, dl_out),
        grid=(S // tm,),
        in_specs=[pl.BlockSpec((tm, D), lambda i: (i, 0)), o_spec, o_spec, o_spec, wide, whole,
                  pl.BlockSpec((D, D), lambda i: (0, 0))],
        out_specs=(wide, whole, o_spec, o_spec, o_spec, dl_spec, dl_spec, dl_spec),
        compiler_params=_params(),
    )(dmix, o_a, o_b, o_c, gates, w_branch, w_out)


def _band_mask(variant, t_abs, s_abs):
    if variant == "b":
        return s_abs <= t_abs
    qc, kc = t_abs >> 6, s_abs >> 6
    return (kc <= qc) & (kc >= qc - (2 if variant == "a" else 8))


def _attn_fwd(variant, qkv, name, *, sinks=None, slopes=None, cq_col=None, ck_row=None, bias=None, after=None):
    S = qkv.shape[0]
    nb = S // BLK
    qb, kb, vb = ATTN_COLS[variant]
    shared_kv = variant == "a"
    win = ATTN_WINDOW.get(variant)
    per_step = ATTN_BLOCKS_PER_STEP[variant]

    def body(*refs):
        if after is not None:
            refs = refs[:-3] + refs[-2:]
        if variant == "a":
            q_ref, k_ref, v_ref, sink_ref, slope_ref, o_ref, lse_ref = refs
        elif variant == "b":
            q_ref, k_ref, v_ref, cq_ref, ck_ref, o_ref, lse_ref = refs
        else:
            q_ref, k_ref, v_ref, bias_ref, o_ref, lse_ref = refs
        p = pl.program_id(0)
        lane = lax.broadcasted_iota(jnp.int32, (1, BLK), 1)

        def compute(i, rows, start, n_keys):
            n_rows = rows.stop - rows.start
            t_abs = i * BLK + lax.broadcasted_iota(jnp.int32, (n_rows, 1), 0)
            q2 = q_ref[rows, :].astype(F32) * 0.125
            k_w = k_ref[pl.ds(start, n_keys), :]
            v_w = v_ref[pl.ds(start, n_keys), :]
            s_abs = start + lax.broadcasted_iota(jnp.int32, (1, n_keys), 1)
            valid = _band_mask(variant, t_abs, s_abs)
            outs = []
            for half in (0, 1):
                hmask = (lane >= 64) if half else (lane < 64)
                qh = jnp.where(hmask, q2, 0.0)
                if shared_kv:
                    swap = (p // 2) != half
                    qh = jnp.where(swap, pltpu.roll(qh, 64, 1), qh)
                s = lax.dot_general(qh.astype(BF16), k_w, NT, preferred_element_type=F32)
                if variant == "a":
                    head = 2 * p + half
                    s = s + (-slope_ref[head]) * jnp.abs(t_abs - s_abs).astype(F32)
                elif variant == "b":
                    s = s + cq_ref[rows, half:half + 1] - ck_ref[half:half + 1, pl.ds(start, n_keys)]
                else:
                    j0 = start // BLK
                    s = s + jnp.concatenate([jnp.concatenate(
                        [bias_ref[half, jnp.clip(i + r - j0 - b, 0, 4)] for b in range(n_keys // BLK)], axis=1)
                        for r in range(n_rows // BLK)], axis=0)
                s = jnp.where(valid, s, NEG_INF)
                m = jnp.max(s, axis=1, keepdims=True)
                if variant == "a":
                    m = jnp.maximum(m, sink_ref[head])
                pe = jnp.exp(s - m)
                l = jnp.sum(pe, axis=1, keepdims=True)
                if variant == "a":
                    l = l + jnp.exp(sink_ref[head] - m)
                out = jnp.dot(pe.astype(BF16), v_w, preferred_element_type=F32) / l
                if shared_kv:
                    out = jnp.where(swap, pltpu.roll(out, 64, 1), out)
                outs.append(out)
                lse_ref[rows, half:half + 1] = m + jnp.log(l)
            o_ref[rows, :] = jnp.where(lane < 64, outs[0], outs[1]).astype(BF16)

        step = pl.program_id(1)
        if variant == "b":
            for g in range(S // GROUP):
                pl.when(step == g)(functools.partial(compute, step * per_step, slice(0, GROUP), 0, (g + 1) * GROUP))
        elif variant == "c":
            span = win + per_step - 1
            start = jnp.clip(step * per_step - (win - 1), 0, nb - span) * BLK
            compute(step * per_step, slice(0, per_step * BLK), pl.multiple_of(start, BLK), span * BLK)
        else:
            for sub in range(per_step):
                i = step * per_step + sub
                start = jnp.clip(i - (win - 1), 0, nb - win) * BLK
                compute(i, slice(sub * BLK, (sub + 1) * BLK), pl.multiple_of(start, BLK), win * BLK)

    tq = per_step * BLK
    kv_col = (lambda p, i: (0, kb)) if shared_kv else (lambda p, i: (0, kb + p))
    vv_col = (lambda p, i: (0, vb)) if shared_kv else (lambda p, i: (0, vb + p))
    in_specs = [pl.BlockSpec((tq, BLK), lambda p, i: (i, qb + p)),
                pl.BlockSpec((S, BLK), kv_col), pl.BlockSpec((S, BLK), vv_col)]
    args = [qkv, qkv, qkv]
    if variant == "a":
        in_specs += [pl.BlockSpec(memory_space=pltpu.SMEM), pl.BlockSpec(memory_space=pltpu.SMEM)]
        args += [sinks, slopes]
    elif variant == "b":
        in_specs += [pl.BlockSpec((None, tq, 2), lambda p, i: (p, i, 0)),
                     pl.BlockSpec((None, 2, S), lambda p, i: (p, 0, 0))]
        args += [cq_col, ck_row]
    else:
        in_specs += [pl.BlockSpec((2, 5, BLK, BLK), lambda p, i: (p, 0, 0, 0))]
        args += [bias]
    if after is not None:
        in_specs.append(ANY)
        args.append(after)
    return pl.pallas_call(
        body, name=name,
        out_shape=(jax.ShapeDtypeStruct((S, 512), BF16), jax.ShapeDtypeStruct((4, S, 2), F32)),
        grid=(4, nb // per_step), in_specs=in_specs,
        out_specs=(pl.BlockSpec((tq, BLK), lambda p, i: (i, p)),
                   pl.BlockSpec((None, tq, 2), lambda p, i: (p, i, 0))),
        compiler_params=_params(),
    )(*args)


def _attn_bwd(variant, qkv, do, lse_row, delta_row, name, *, sinks=None, slopes=None, cq_row=None,
              ck_col=None, bias_t=None):
    S = qkv.shape[0]
    nb = S // BLK
    qb, kb, vb = ATTN_COLS[variant]
    shared_kv = variant == "a"
    win = ATTN_WINDOW.get(variant)
    per_step = ATTN_BLOCKS_PER_STEP[variant]

    def body(*refs):
        *refs, dqt_ref = refs
        if variant == "a":
            (q_ref, k_ref, v_ref, do_ref, lse_ref, dl_ref, sink_ref, slope_ref,
             dq_ref, dk_ref, dv_ref, ex_ref) = refs
        elif variant == "b":
            (q_ref, k_ref, v_ref, do_ref, lse_ref, dl_ref, cq_ref, ck_ref,
             dq_ref, dk_ref, dv_ref, ex_ref, dcq_ref) = refs
        else:
            (q_ref, k_ref, v_ref, do_ref, lse_ref, dl_ref, bias_ref,
             dq_ref, dk_ref, dv_ref, ex_ref) = refs
        p = pl.program_id(0)
        lane = lax.broadcasted_iota(jnp.int32, (1, BLK), 1)
        hmasks = [(lane < 64), (lane >= 64)]
        swaps = [(p // 2) != half for half in (0, 1)] if shared_kv else None

        @pl.when(pl.program_id(1) == 0)
        def _():
            dqt_ref[...] = jnp.zeros_like(dqt_ref)
            if variant == "b":
                dcq_ref[...] = jnp.zeros_like(dcq_ref)
            else:
                ex_ref[...] = jnp.zeros_like(ex_ref)

        def to_kv_lanes(x, h):
            x = jnp.where(hmasks[h], x, 0.0)
            if shared_kv:
                x = jnp.where(swaps[h], pltpu.roll(x, 64, 1), x)
            return x

        def compute(j, rows, start, n_q):
            n_rows = rows.stop - rows.start
            s_abs = j * BLK + lax.broadcasted_iota(jnp.int32, (n_rows, 1), 0)
            off_k = pl.multiple_of(j * BLK, BLK)
            k2 = k_ref[rows, :].astype(F32)
            v2 = v_ref[rows, :].astype(F32)
            if shared_kv:
                kv_lane = (lane >> 6) == (p // 2)
                k_src, v_src = jnp.where(kv_lane, k2, 0.0), jnp.where(kv_lane, v2, 0.0)
                k_al = [jnp.where(swaps[h], pltpu.roll(k_src, 64, 1), k_src) for h in (0, 1)]
                v_al = [jnp.where(swaps[h], pltpu.roll(v_src, 64, 1), v_src) for h in (0, 1)]
            else:
                k_al = [jnp.where(hmasks[h], k2, 0.0) for h in (0, 1)]
                v_al = [jnp.where(hmasks[h], v2, 0.0) for h in (0, 1)]
            k_al = [(t * 0.125).astype(BF16) for t in k_al]
            v_al = [t.astype(BF16) for t in v_al]
            q_w = q_ref[pl.ds(start, n_q), :]
            do_w = do_ref[pl.ds(start, n_q), :]
            t_abs = start + lax.broadcasted_iota(jnp.int32, (1, n_q), 1)
            valid = _band_mask(variant, t_abs, s_abs)
            dk_acc = dv_acc = None
            ds_both = []
            for half in (0, 1):
                s = lax.dot_general(k_al[half], q_w, NT, preferred_element_type=F32)
                if variant == "a":
                    s = s + (-slope_ref[2 * p + half]) * jnp.abs(t_abs - s_abs).astype(F32)
                elif variant == "b":
                    s = s + cq_ref[half:half + 1, pl.ds(start, n_q)] - ck_ref[rows, half:half + 1]
                else:
                    i0 = start // BLK
                    s = s + jnp.concatenate([jnp.concatenate(
                        [bias_ref[half, jnp.clip(i0 + b - j - r, 0, 4)] for b in range(n_q // BLK)], axis=1)
                        for r in range(n_rows // BLK)], axis=0)
                pr = jnp.where(valid, jnp.exp(s - lse_ref[half:half + 1, pl.ds(start, n_q)]), 0.0)
                dp = lax.dot_general(v_al[half], do_w, NT, preferred_element_type=F32)
                ds = pr * (dp - dl_ref[half:half + 1, pl.ds(start, n_q)])
                ds16 = ds.astype(BF16)
                dv_h = to_kv_lanes(jnp.dot(pr.astype(BF16), do_w, preferred_element_type=F32), half)
                dk_h = to_kv_lanes(jnp.dot(ds16, q_w, preferred_element_type=F32) * 0.125, half)
                dv_acc = dv_h if dv_acc is None else dv_acc + dv_h
                dk_acc = dk_h if dk_acc is None else dk_acc + dk_h
                ds_both.append(ds16)
                if variant == "b":
                    ex_ref[rows, half:half + 1] = -jnp.sum(ds, axis=1, keepdims=True)
                    dcq_ref[half:half + 1, pl.ds(start, n_q)] += jnp.sum(ds, axis=0, keepdims=True)
                elif variant == "c":
                    for r in range(n_rows // BLK):
                        for b in range(n_q // BLK):
                            ex_ref[half, jnp.clip(i0 + b - j - r, 0, 4)] += ds[r * BLK:(r + 1) * BLK, b * BLK:(b + 1) * BLK]
            dq_t = lax.dot_general(jnp.concatenate(k_al, axis=0), jnp.concatenate(ds_both, axis=0), TN,
                                   preferred_element_type=F32)
            dqt_ref[:, pl.ds(start, n_q)] += dq_t
            if shared_kv:
                @pl.when(p == 0)
                def _():
                    dk_ref[pl.ds(off_k, n_rows), :] = dk_acc
                    dv_ref[pl.ds(off_k, n_rows), :] = dv_acc

                @pl.when(p > 0)
                def _():
                    dk_ref[pl.ds(off_k, n_rows), :] += dk_acc
                    dv_ref[pl.ds(off_k, n_rows), :] += dv_acc
            else:
                dk_ref[pl.ds(off_k, n_rows), :] = dk_acc.astype(dk_ref.dtype)
                dv_ref[pl.ds(off_k, n_rows), :] = dv_acc.astype(dv_ref.dtype)
            if variant == "a":
                for half in (0, 1):
                    p_sink = jnp.exp(sink_ref[2 * p + half] - lse_ref[half:half + 1, pl.ds(off_k, n_rows)])
                    term = p_sink * dl_ref[half:half + 1, pl.ds(off_k, n_rows)]
                    ex_ref[half:half + 1, :] += -jnp.sum(term, axis=1, keepdims=True)

        step = pl.program_id(1)
        if variant == "b":
            for g in range(S // GROUP):
                pl.when(step == g)(functools.partial(compute, step * per_step, slice(0, GROUP), g * GROUP, S - g * GROUP))
        elif variant == "c":
            span = win + per_step - 1
            start = jnp.clip(step * per_step, 0, nb - span) * BLK
            compute(step * per_step, slice(0, per_step * BLK), pl.multiple_of(start, BLK), span * BLK)
        else:
            for sub in range(per_step):
                j = step * per_step + sub
                start = jnp.clip(j, 0, nb - win) * BLK
                compute(j, slice(sub * BLK, (sub + 1) * BLK), pl.multiple_of(start, BLK), win * BLK)

        @pl.when(step == nb // per_step - 1)
        def _():
            dq_ref[...] = jnp.transpose(dqt_ref[...]).astype(BF16)

    tk = per_step * BLK
    col = lambda c0: (lambda p, j: (0, c0 + p))
    kv_blk = (lambda c0: (lambda p, j: (j, c0))) if shared_kv else (lambda c0: (lambda p, j: (j, c0 + p)))
    pair = lambda p, j: (0, p)
    row_stat = pl.BlockSpec((None, 2, S), lambda p, j: (p, 0, 0))
    in_specs = [pl.BlockSpec((S, BLK), col(qb)),
                pl.BlockSpec((tk, BLK), kv_blk(kb)), pl.BlockSpec((tk, BLK), kv_blk(vb)),
                pl.BlockSpec((S, BLK), pair), row_stat, row_stat]
    args = [qkv, qkv, qkv, do, lse_row, delta_row]
    kv_width = BLK if shared_kv else 512
    kv_out = pl.BlockSpec((S, BLK), (lambda p, j: (0, 0)) if shared_kv else pair)
    kv_dtype = F32 if shared_kv else BF16
    out_shape = [jax.ShapeDtypeStruct((S, 512), BF16), jax.ShapeDtypeStruct((S, kv_width), kv_dtype),
                 jax.ShapeDtypeStruct((S, kv_width), kv_dtype)]
    out_specs = [pl.BlockSpec((S, BLK), pair), kv_out, kv_out]
    if variant == "a":
        in_specs += [pl.BlockSpec(memory_space=pltpu.SMEM), pl.BlockSpec(memory_space=pltpu.SMEM)]
        args += [sinks, slopes]
        out_shape.append(jax.ShapeDtypeStruct((4, 8, BLK), F32))
        out_specs.append(pl.BlockSpec((None, 8, BLK), lambda p, j: (p, 0, 0)))
    elif variant == "b":
        in_specs += [row_stat, pl.BlockSpec((None, tk, 2), lambda p, j: (p, j, 0))]
        args += [cq_row, ck_col]
        out_shape += [jax.ShapeDtypeStruct((4, S, 2), F32), jax.ShapeDtypeStruct((4, 2, S), F32)]
        out_specs += [pl.BlockSpec((None, tk, 2), lambda p, j: (p, j, 0)), row_stat]
    else:
        in_specs += [pl.BlockSpec((2, 5, BLK, BLK), lambda p, j: (p, 0, 0, 0))]
        args += [bias_t]
        out_shape.append(jax.ShapeDtypeStruct((8, 5, BLK, BLK), F32))
        out_specs.append(pl.BlockSpec((2, 5, BLK, BLK), lambda p, j: (p, 0, 0, 0)))
    return pl.pallas_call(
        body, name=name, out_shape=tuple(out_shape), grid=(4, nb // per_step),
        in_specs=in_specs, out_specs=tuple(out_specs), scratch_shapes=[pltpu.VMEM((BLK, S), F32)],
        compiler_params=_params(),
    )(*args)


def _log_sigmoid(x):
    return jnp.minimum(x, 0.0) - jnp.log(1.0 + jnp.exp(-jnp.abs(x)))


def _forget_fwd(fb, b_forget, name):
    S = fb.shape[0]
    nb = S // GROUP

    def body(fb_ref, b_ref, cum_ref, carry_ref):
        i = pl.program_id(0)
        logf = _log_sigmoid(fb_ref[...] + b_ref[...])
        r = lax.broadcasted_iota(jnp.int32, (GROUP, GROUP), 0)
        c = lax.broadcasted_iota(jnp.int32, (GROUP, GROUP), 1)
        tri = (c <= r).astype(F32)

        @pl.when(i == 0)
        def _():
            carry_ref[...] = jnp.zeros_like(carry_ref)

        cum = jnp.dot(tri, logf, preferred_element_type=F32, precision=HIGHEST) + carry_ref[0:1, :]
        cum_ref[...] = cum
        carry_ref[...] = jnp.broadcast_to(cum[GROUP - 1:GROUP, :], carry_ref.shape)

    return pl.pallas_call(
        body, name=name, out_shape=jax.ShapeDtypeStruct((S, BLK), F32), grid=(nb,),
        in_specs=[pl.BlockSpec((GROUP, BLK), lambda i: (i, 0)), pl.BlockSpec((1, BLK), lambda i: (0, 0))],
        out_specs=pl.BlockSpec((GROUP, BLK), lambda i: (i, 0)),
        scratch_shapes=[pltpu.VMEM((8, BLK), F32)],
        compiler_params=_params(),
    )(fb, b_forget)


def _forget_bwd(dcum_q, dcum_k, fb, b_forget, name):
    S = fb.shape[0]
    nb = S // GROUP

    def body(dq_ref, dk_ref, fb_ref, b_ref, dfb_ref, db_ref, carry_ref):
        g = pl.program_id(0)
        r = lax.broadcasted_iota(jnp.int32, (GROUP, GROUP), 0)
        c = lax.broadcasted_iota(jnp.int32, (GROUP, GROUP), 1)
        tri = (c >= r).astype(F32)

        @pl.when(g == 0)
        def _():
            carry_ref[...] = jnp.zeros_like(carry_ref)

        dcum = dq_ref[...] + dk_ref[...]
        dlogf = jnp.dot(tri, dcum, preferred_element_type=F32, precision=HIGHEST) + carry_ref[0:1, :]
        carry_ref[...] = jnp.broadcast_to(dlogf[0:1, :], carry_ref.shape)
        x = fb_ref[...] + b_ref[...]
        lane = lax.broadcasted_iota(jnp.int32, (1, BLK), 1)
        dfb = jnp.where(lane < N_FORGET, dlogf * jax.nn.sigmoid(-x), 0.0)
        dfb_ref[...] = dfb
        db = jnp.sum(dfb, axis=0, keepdims=True)

        @pl.when(g == 0)
        def _():
            db_ref[...] = db

        @pl.when(g > 0)
        def _():
            db_ref[...] += db

    rev = pl.BlockSpec((GROUP, BLK), lambda g: (nb - 1 - g, 0))
    row = pl.BlockSpec((1, BLK), lambda g: (0, 0))
    return pl.pallas_call(
        body, name=name,
        out_shape=(jax.ShapeDtypeStruct((S, BLK), F32), jax.ShapeDtypeStruct((1, BLK), F32)), grid=(nb,),
        in_specs=[rev, rev, rev, row], out_specs=(rev, row),
        scratch_shapes=[pltpu.VMEM((8, BLK), F32)],
        compiler_params=_params(),
    )(dcum_q, dcum_k, fb, b_forget)


def _skew(x, sign):
    row = lax.broadcasted_iota(jnp.int32, x.shape, 0)
    for b in range(7):
        amount = (1 << b) if sign > 0 else 256 - (1 << b)
        x = jnp.where(((row >> b) & 1) == 1, pltpu.roll(x, amount, 1), x)
    return x


def _rel_bases(rel):
    far = rel[:, 256:257]
    far127 = jnp.broadcast_to(far, (rel.shape[0], 127))
    base0 = jnp.concatenate([rel[:, 128:0:-1], far, rel[:, 255:128:-1]], axis=1)
    base1 = jnp.concatenate([rel[:, 256:128:-1], far, far127], axis=1)
    base0_t = jnp.concatenate([rel[:, 128:256], far, rel[:, 1:128]], axis=1)
    base1_t = jnp.concatenate([jnp.broadcast_to(far, (rel.shape[0], 128)), far, rel[:, 129:256]], axis=1)
    return jnp.stack([base0, base1, base0_t, base1_t], axis=1)


def _rel_expand(bases, name):
    def body(b_ref, t_ref, tt_ref):
        far = jnp.broadcast_to(b_ref[1:2, 0:1], (BLK, BLK))
        for k, out_ref in ((0, t_ref), (2, tt_ref)):
            for d in (0, 1):
                x = jnp.broadcast_to(b_ref[k + d:k + d + 1, :], (BLK, 2 * BLK))
                out_ref[d] = _skew(x, 1)[:, :BLK]
            for d in (2, 3, 4):
                out_ref[d] = far

    out = jax.ShapeDtypeStruct((8, 5, BLK, BLK), F32)
    spec = pl.BlockSpec((None, 5, BLK, BLK), lambda h: (h, 0, 0, 0))
    return pl.pallas_call(
        body, name=name, out_shape=(out, out), grid=(8,),
        in_specs=[pl.BlockSpec((None, 4, 2 * BLK), lambda h: (h, 0, 0))], out_specs=(spec, spec),
        compiler_params=_params(),
    )(bases)


def _rel_reduce(dtiles_t, name):
    def body(dt_ref, o_ref):
        zeros = jnp.zeros((BLK, BLK), F32)
        sums = []
        for d in (0, 1):
            x = _skew(jnp.concatenate([dt_ref[d], zeros], axis=1), -1)
            sums.append(jnp.broadcast_to(jnp.sum(x, axis=0, keepdims=True), (8, 2 * BLK)))
        lane = lax.broadcasted_iota(jnp.int32, (8, 2 * BLK), 1)
        main = pltpu.roll(sums[0], BLK, 1) + jnp.where(lane > BLK, sums[1], 0.0)
        far = jnp.sum(jnp.where(lane < BLK, sums[1], 0.0)[0:1], axis=1, keepdims=True)
        far = far + jnp.sum(jnp.sum(dt_ref[2] + dt_ref[3] + dt_ref[4], axis=0, keepdims=True), axis=1, keepdims=True)
        o_ref[...] = jnp.concatenate([main[0:1], jnp.broadcast_to(far, (1, BLK))], axis=1)

    return pl.pallas_call(
        body, name=name, out_shape=jax.ShapeDtypeStruct((8, 1, 3 * BLK), F32), grid=(8,),
        in_specs=[pl.BlockSpec((None, 5, BLK, BLK), lambda h: (h, 0, 0, 0))],
        out_specs=pl.BlockSpec((None, 1, 3 * BLK), lambda h: (h, 0, 0)),
        compiler_params=_params(),
    )(dtiles_t)


def _final_loss(x, target, g, below, name):
    S, D = x.shape
    ts = _row_tile(S, ROW_TILE)

    def body(x_ref, t_ref, g_ref, f_ref, gate_ref, dx_ref, loss_ref, dg_ref, dgate_ref, df_ref):
        i = pl.program_id(0)
        xv, gv = x_ref[...], g_ref[...]
        rstd = lax.rsqrt(jnp.mean(xv * xv, axis=-1, keepdims=True) + EPS)
        xhat = xv * rstd
        err = xhat * gv - t_ref[...]
        part = 0.5 * jnp.sum(jnp.mean(err * err, axis=-1, keepdims=True), axis=0, keepdims=True)
        dy = err / D
        dg = jnp.sum(dy * xhat, axis=0, keepdims=True)
        dxhat = dy * gv
        proj = jnp.mean(dxhat * xhat, axis=-1, keepdims=True)
        dx = rstd * (dxhat - xhat * proj)
        dx_ref[...] = dx
        dgate = _gated_residual_bwd(dx, f_ref, gate_ref, df_ref)
        _accumulate_rows(i, [(loss_ref, jnp.broadcast_to(part, loss_ref.shape)), (dg_ref, dg), (dgate_ref, dgate)])

    tile = pl.BlockSpec((ts, D), lambda i: (i, 0))
    row = pl.BlockSpec((1, D), lambda i: (0, 0))
    vec = jax.ShapeDtypeStruct((1, D), F32)
    return pl.pallas_call(
        body, name=name,
        out_shape=(jax.ShapeDtypeStruct((S, D), F32), jax.ShapeDtypeStruct((8, 128), F32), vec, vec,
                   jax.ShapeDtypeStruct((S, D), BF16)),
        grid=(S // ts,), in_specs=[tile, tile, row, tile, row],
        out_specs=(tile, pl.BlockSpec((8, 128), lambda i: (0, 0)), row, row, tile),
        compiler_params=_params(),
    )(x, target, g, *below)


def _ada_fwd(c_all, w_ada, name):
    L, D, E = w_ada.shape

    def body(c_ref, w_ref, o_ref):
        cv = c_ref[...]
        cond = cv * jax.nn.sigmoid(cv)
        o_ref[...] = jnp.dot(cond, w_ref[...], preferred_element_type=F32, precision=HIGHEST)

    return pl.pallas_call(
        body, name=name, out_shape=jax.ShapeDtypeStruct((L, N_DEV, E), F32), grid=(L,),
        in_specs=[pl.BlockSpec((N_DEV, D), lambda l: (0, 0)), pl.BlockSpec((None, D, E), lambda l: (l, 0, 0))],
        out_specs=pl.BlockSpec((None, N_DEV, E), lambda l: (l, 0, 0)),
        compiler_params=_params(),
    )(c_all, w_ada)


def _ada_bwd(c_all_t, dmod, name):
    D = c_all_t.shape[0]
    L, _, E = dmod.shape

    def body(c_ref, d_ref, o_ref):
        cv = c_ref[...]
        cond = cv * jax.nn.sigmoid(cv)
        acc = None
        for b in range(N_DEV):
            t = cond[:, b:b + 1] * d_ref[b:b + 1, :]
            acc = t if acc is None else acc + t
        o_ref[...] = acc

    return pl.pallas_call(
        body, name=name, out_shape=jax.ShapeDtypeStruct((L, D, E), F32), grid=(L,),
        in_specs=[pl.BlockSpec((D, N_DEV), lambda l: (0, 0)), pl.BlockSpec((None, N_DEV, E), lambda l: (l, 0, 0))],
        out_specs=pl.BlockSpec((None, D, E), lambda l: (l, 0, 0)),
        compiler_params=_params(),
    )(c_all_t, dmod)


def _adamw(w, m, v, g_parts, name, me, after=None):
    L, R, C = w.shape
    tr = _row_tile(R, max(8, (256 * 1024 // max(C, 128)) // 8 * 8))
    nr = R // tr
    c1 = 1.0 - ADAM_B1 ** ADAM_STEP
    c2 = 1.0 - ADAM_B2 ** ADAM_STEP
    direct = [isinstance(p, tuple) for p in g_parts]
    n_in = sum(2 if d else 1 for d in direct)

    def body(me_ref, w_ref, m_ref, v_ref, *rest):
        g_refs, (go_ref, d_ref, mo_ref, vo_ref) = list(rest[:n_in]), rest[-4:]
        layer = pl.program_id(0)
        g = None
        for l in range(L):
            land_ref = g_refs.pop(0)
            own = g_refs.pop(0)[...].astype(F32) if direct[l] else None
            gl = None
            for k in range(land_ref.shape[0]):
                part = land_ref[k].astype(F32)
                if direct[l]:
                    part = jnp.where(me_ref[l] == k, own, part)
                gl = part if gl is None else gl + part
            g = gl if g is None else jnp.where(layer == l, gl, g)
        mn = ADAM_B1 * m_ref[...] + (1.0 - ADAM_B1) * g
        vn = ADAM_B2 * v_ref[...] + (1.0 - ADAM_B2) * (g * g)
        m_hat = mn / c1
        v_hat = vn / c2
        go_ref[...] = g
        d_ref[...] = -ADAM_LR * (m_hat / (jnp.sqrt(v_hat) + ADAM_EPS) + ADAM_WD * w_ref[...])
        mo_ref[...] = mn
        vo_ref[...] = vn

    def rows(l, layer, i):
        return jnp.where(layer == l, i, 0 if l > 0 else nr - 1)

    in_specs, operands = [], []
    for l, p in enumerate(g_parts):
        land, sent = p if direct[l] else (p, None)
        in_specs.append(pl.BlockSpec((land.shape[0], tr, C), lambda layer, i, me_ref, l=l: (0, rows(l, layer, i), 0)))
        operands.append(land)
        if direct[l]:
            in_specs.append(pl.BlockSpec((None, tr, C), lambda layer, i, me_ref, l=l: (me_ref[l], rows(l, layer, i), 0)))
            operands.append(sent)
    if after is not None:
        in_specs.append(ANY)
        operands.append(after)
    tile = pl.BlockSpec((None, tr, C), lambda layer, i, me_ref: (layer, i, 0))
    out = jax.ShapeDtypeStruct((L, R, C), F32)
    return pl.pallas_call(
        body, name=name, out_shape=(out, out, out, out),
        grid_spec=pltpu.PrefetchScalarGridSpec(
            num_scalar_prefetch=1, grid=(L, nr), in_specs=[tile, tile, tile] + in_specs,
            out_specs=(tile, tile, tile, tile)),
        compiler_params=_params(),
    )(me, w, m, v, *operands)


def _pair_add(pieces, recv, core, name):
    _, _, R, C = pieces.shape
    tr = _row_tile(R, max(8, (512 * 1024 // max(C, 128)) // 8 * 8))

    def body(core_ref, a_ref, b_ref, o_ref):
        o_ref[...] = (a_ref[...].astype(F32) + b_ref[...].astype(F32)).astype(BF16)

    return pl.pallas_call(
        body, name=name, out_shape=jax.ShapeDtypeStruct((4, R, C), BF16),
        grid_spec=pltpu.PrefetchScalarGridSpec(
            num_scalar_prefetch=1, grid=(4, R // tr),
            in_specs=[pl.BlockSpec((None, None, tr, C), lambda k, i, core_ref: (core_ref[0], k, i, 0)),
                      pl.BlockSpec((None, tr, C), lambda k, i, core_ref: (k, i, 0))],
            out_specs=pl.BlockSpec((None, tr, C), lambda k, i, core_ref: (k, i, 0))),
        compiler_params=_params(),
    )(core, pieces, recv)


MESH = pl.DeviceIdType.MESH
ANY = pl.BlockSpec(memory_space=pl.ANY)


def _position():
    return lax.axis_index("x"), lax.axis_index("y"), lax.axis_index("c")


def _small_all_gather(v, name):
    m_per, n = v.shape

    def body(x_ref, out_ref, send_sems, recv_sems, local_sem):
        x, y, c = _position()
        me, sibling = (x, y, c), (x, y, 1 - c)
        chips = [(1 - x, y), (x, 1 - y), (1 - x, 1 - y)]

        def rows(px, py, pc):
            return out_ref.at[pl.ds((4 * px + 2 * py + pc) * m_per, m_per), :]

        def copy(k, block, to, src=None):
            return pltpu.make_async_remote_copy(
                src_ref=rows(*block) if src is None else src, dst_ref=rows(*block),
                send_sem=send_sems.at[k], recv_sem=recv_sems.at[k], device_id=to, device_id_type=MESH)

        mine = pltpu.make_async_copy(x_ref, rows(*me), local_sem)
        mine.start()
        first = [copy(0, me, sibling, src=x_ref)]
        first += [copy(1 + j, me, (*chip, c), src=x_ref) for j, chip in enumerate(chips)]
        for cp in first:
            cp.start()
        passed = [copy(4 + j, (*chip, c), sibling) for j, chip in enumerate(chips)]
        for j, chip in enumerate(chips):
            copy(1 + j, (*chip, c), me).wait_recv()
            passed[j].start()
        copy(0, sibling, me).wait_recv()
        for j, chip in enumerate(chips):
            copy(4 + j, (*chip, 1 - c), me).wait_recv()
        for cp in first + passed:
            cp.wait_send()
        mine.wait()

    return pl.pallas_call(
        body, name=name, out_shape=jax.ShapeDtypeStruct((N_DEV * m_per, n), v.dtype),
        in_specs=[pl.BlockSpec(memory_space=pltpu.VMEM)], out_specs=pl.BlockSpec(memory_space=pltpu.VMEM),
        scratch_shapes=[pltpu.SemaphoreType.DMA((7,)), pltpu.SemaphoreType.DMA((7,)), pltpu.SemaphoreType.DMA],
    )(v)


def _big_all_gather(shards, name):
    n_arr = len(shards)

    def body(*refs):
        x_refs, out_refs = refs[:n_arr], refs[n_arr:2 * n_arr]
        send_sems, recv_sems, local_sems = refs[2 * n_arr:]
        x, y, c = _position()
        me, sibling = (x, y, c), (x, y, 1 - c)
        chips = [(1 - x, y), (x, 1 - y), (1 - x, 1 - y)]

        def slot(a, px, py, pc):
            return out_refs[a].at[4 * px + 2 * py + pc]

        def copy(a, k, block, to, src=None):
            return pltpu.make_async_remote_copy(
                src_ref=slot(a, *block) if src is None else src, dst_ref=slot(a, *block),
                send_sem=send_sems.at[a, k], recv_sem=recv_sems.at[a, k], device_id=to, device_id_type=MESH)

        mine = [pltpu.make_async_copy(x_refs[a], slot(a, *me), local_sems.at[a]) for a in range(n_arr)]
        for cp in mine:
            cp.start()
        first = []
        for j, chip in enumerate(chips):
            first += [copy(a, 1 + j, me, (*chip, c), src=x_refs[a]) for a in range(n_arr)]
        first += [copy(a, 0, me, sibling, src=x_refs[a]) for a in range(n_arr)]
        for cp in first:
            cp.start()
        passed = []
        for j, chip in enumerate(chips):
            for a in range(n_arr):
                copy(a, 1 + j, (*chip, c), me).wait_recv()
                fwd = copy(a, 4 + j, (*chip, c), sibling)
                fwd.start()
                passed.append(fwd)
        for a in range(n_arr):
            copy(a, 0, sibling, me).wait_recv()
        for j, chip in enumerate(chips):
            for a in range(n_arr):
                copy(a, 4 + j, (*chip, 1 - c), me).wait_recv()
        for cp in first + passed:
            cp.wait_send()
        for cp in mine:
            cp.wait()

    return pl.pallas_call(
        body, name=name,
        out_shape=tuple(jax.ShapeDtypeStruct((N_DEV,) + s.shape, s.dtype) for s in shards),
        in_specs=[ANY] * n_arr, out_specs=tuple([ANY] * n_arr),
        scratch_shapes=[pltpu.SemaphoreType.DMA((n_arr, 7)), pltpu.SemaphoreType.DMA((n_arr, 7)),
                        pltpu.SemaphoreType.DMA((n_arr,))],
    )(*shards)


def _sibling_exchange(pieces, name):
    n_arr = len(pieces)

    def body(*refs):
        p_refs, out_refs = refs[:n_arr], refs[n_arr:2 * n_arr]
        send_sems, recv_sems = refs[2 * n_arr:]
        x, y, c = _position()
        copies = [pltpu.make_async_remote_copy(
            src_ref=p_refs[a].at[1 - c], dst_ref=out_refs[a], send_sem=send_sems.at[a], recv_sem=recv_sems.at[a],
            device_id=(x, y, 1 - c), device_id_type=MESH) for a in range(n_arr)]
        for cp in copies:
            cp.start()
        for cp in copies:
            cp.wait()

    return pl.pallas_call(
        body, name=name,
        out_shape=tuple(jax.ShapeDtypeStruct(p.shape[1:], p.dtype) for p in pieces),
        in_specs=[ANY] * n_arr, out_specs=tuple([ANY] * n_arr),
        scratch_shapes=[pltpu.SemaphoreType.DMA((n_arr,)), pltpu.SemaphoreType.DMA((n_arr,))],
    )(*pieces)


HBM = pl.BlockSpec(memory_space=pltpu.HBM)
SEM = pl.BlockSpec(memory_space=pltpu.SEMAPHORE)
EFFECT = pltpu.SideEffectType.DATAFLOW_SIDE_EFFECTING
RELATIONS = [(rx, ry, rc) for rx in (0, 1) for ry in (0, 1) for rc in (0, 1)][1:]


SAME_CORE = [r for r in RELATIONS if r == (0, 0, 1) or r[2] == 0]


CHIPS = [r for r in RELATIONS if r[2] == 0]


def _exchange_copies(src_refs, land_refs, send_sems, recv_sems, scatter, receive_side, relations):
    x, y, c = _position()
    index = (lambda px, py, pc: 2 * px + py) if relations == CHIPS else (lambda px, py, pc: 4 * px + 2 * py + pc)
    me = index(x, y, c)
    copies = []
    for k, (rx, ry, rc) in enumerate(relations):
        peer = ((1 - x) if rx else x, (1 - y) if ry else y, (1 - c) if rc else c)
        peer_index = index(*peer)
        for a, (src, land) in enumerate(zip(src_refs, land_refs)):
            copies.append(pltpu.make_async_remote_copy(
                src_ref=src.at[peer_index] if scatter else src,
                dst_ref=land.at[peer_index if receive_side else me],
                send_sem=send_sems.at[a * len(relations) + k], recv_sem=recv_sems.at[a * len(relations) + k],
                device_id=peer, device_id_type=MESH))
    return copies


def _exchange_start(srcs, scatter, after, name, relations=RELATIONS):
    n = len(srcs)
    land_shapes = [(s.shape if scatter else (N_DEV,) + s.shape) for s in srcs]

    def body(*refs):
        src_refs, land_refs = refs[:n], refs[n:2 * n]
        send_sems, recv_sems = refs[2 * n + 1], refs[2 * n + 2]
        token = refs[-1]
        for cp in _exchange_copies(src_refs, land_refs, send_sems, recv_sems, scatter, False, relations):
            cp.start()
        token[...] = jnp.zeros_like(token)

    sems = pltpu.SemaphoreType.DMA((n * len(relations),))
    outs = pl.pallas_call(
        body, name=name,
        out_shape=(sems, sems, *[pltpu.HBM(s.shape, s.dtype) for s in srcs],
                   *[pltpu.HBM(shape, s.dtype) for shape, s in zip(land_shapes, srcs)],
                   jax.ShapeDtypeStruct((8, 128), F32)),
        in_specs=[HBM] * (2 * n) + [ANY],
        out_specs=(SEM, SEM, *[HBM] * (2 * n), pl.BlockSpec(memory_space=pltpu.VMEM)),
        input_output_aliases={a: 2 + a for a in range(2 * n)},
        compiler_params=pltpu.CompilerParams(has_side_effects=EFFECT),
    )(*[pltpu.with_memory_space_constraint(s, pltpu.HBM) for s in srcs],
      *[pltpu.with_memory_space_constraint(lax.empty(shape, s.dtype), pltpu.HBM)
        for shape, s in zip(land_shapes, srcs)], after)
    return outs[0], outs[1], outs[2:2 + n], outs[2 + n:2 + 2 * n], outs[-1]


def _exchange_wait(started, scatter, after, name, relations=RELATIONS):
    send_sems, recv_sems, srcs, lands, _ = started
    n = len(srcs)

    def body(*refs):
        src_refs, land_refs = refs[:n], refs[n:2 * n]
        send_sems, recv_sems = refs[2 * n], refs[2 * n + 1]
        copies = _exchange_copies(src_refs, land_refs, send_sems, recv_sems, scatter, True, relations)
        for cp in copies:
            cp.wait_send()
        for cp in copies:
            cp.wait_recv()

    outs = pl.pallas_call(
        body, name=name,
        out_shape=(*[pltpu.HBM(s.shape, s.dtype) for s in srcs], *[pltpu.HBM(t.shape, t.dtype) for t in lands]),
        in_specs=[HBM] * (2 * n) + [SEM, SEM, ANY], out_specs=tuple([HBM] * (2 * n)),
        input_output_aliases={a: a for a in range(2 * n)},
        compiler_params=pltpu.CompilerParams(has_side_effects=EFFECT),
    )(*srcs, *lands, send_sems, recv_sems, after)
    return outs[:n], outs[n:]


def _forward_copies(land_refs, send_sems, recv_sems, receive_side):
    x, y, c = _position()
    copies = []
    for j, (px, py) in enumerate([(1 - x, y), (x, 1 - y), (1 - x, 1 - y)]):
        held, coming = 4 * px + 2 * py + c, 4 * px + 2 * py + (1 - c)
        for a, land in enumerate(land_refs):
            copies.append(pltpu.make_async_remote_copy(
                src_ref=land.at[held], dst_ref=land.at[coming if receive_side else held],
                send_sem=send_sems.at[3 * a + j], recv_sem=recv_sems.at[3 * a + j],
                device_id=(x, y, 1 - c), device_id_type=MESH))
    return copies


def _forward_start(lands, after, name):
    n = len(lands)

    def body(*refs):
        send_sems, recv_sems, token = refs[n + 1], refs[n + 2], refs[-1]
        for cp in _forward_copies(refs[:n], send_sems, recv_sems, False):
            cp.start()
        token[...] = jnp.zeros_like(token)

    sems = pltpu.SemaphoreType.DMA((3 * n,))
    outs = pl.pallas_call(
        body, name=name,
        out_shape=(sems, sems, *[pltpu.HBM(t.shape, t.dtype) for t in lands], jax.ShapeDtypeStruct((8, 128), F32)),
        in_specs=[HBM] * n + [ANY], out_specs=(SEM, SEM, *[HBM] * n, pl.BlockSpec(memory_space=pltpu.VMEM)),
        input_output_aliases={a: 2 + a for a in range(n)},
        compiler_params=pltpu.CompilerParams(has_side_effects=EFFECT),
    )(*lands, after)
    return outs[0], outs[1], outs[2:2 + n], outs[-1]


def _forward_wait(started, after, name):
    send_sems, recv_sems, lands, _ = started
    n = len(lands)

    def body(*refs):
        copies = _forward_copies(refs[:n], refs[n], refs[n + 1], True)
        for cp in copies:
            cp.wait_send()
        for cp in copies:
            cp.wait_recv()

    return pl.pallas_call(
        body, name=name, out_shape=tuple(pltpu.HBM(t.shape, t.dtype) for t in lands),
        in_specs=[HBM] * n + [SEM, SEM, ANY], out_specs=tuple([HBM] * n),
        input_output_aliases={a: a for a in range(n)},
        compiler_params=pltpu.CompilerParams(has_side_effects=EFFECT),
    )(*lands, send_sems, recv_sems, after)


def _place_own(lands, mine, me, name):
    n = len(lands)
    flat = [m.reshape(-1, m.shape[-1]) for m in mine]
    flat_lands = [t.reshape(N_DEV, -1, t.shape[-1]) for t in lands]

    def body(me_ref, *refs):
        for src, dst in zip(refs[:n], refs[2 * n:]):
            dst[...] = src[...]

    in_specs = [pl.BlockSpec((m.shape[0] // 2, m.shape[1]), lambda i, me_ref: (i, 0)) for m in flat]
    out_specs = [pl.BlockSpec((None, m.shape[0] // 2, m.shape[1]), lambda i, me_ref: (me_ref[0], i, 0)) for m in flat]
    outs = pl.pallas_call(
        body, name=name, out_shape=tuple(jax.ShapeDtypeStruct(t.shape, t.dtype) for t in flat_lands),
        grid_spec=pltpu.PrefetchScalarGridSpec(
            num_scalar_prefetch=1, grid=(2,), in_specs=in_specs + [ANY] * n, out_specs=tuple(out_specs)),
        input_output_aliases={1 + n + a: a for a in range(n)},
        compiler_params=_params(),
    )(me, *flat, *flat_lands)
    return [o.reshape(t.shape) for o, t in zip(outs, lands)]


W_IN_SHARD = N_IN // N_DEV
F_SHARD = F_COL // W_IN_SHARD
F_LO = F_COL - F_SHARD * W_IN_SHARD


def _w_ffn_in_view(w):
    return jnp.transpose(w, (0, 2, 1))


def _w_in_segments():
    segments = []
    for d in range(N_DEV):
        if d == F_SHARD:
            segments += [(d, 0, d * W_IN_SHARD, F_LO), (d, F_LO, N_MAIN, N_FORGET),
                         (d, F_LO + N_FORGET, F_COL, W_IN_SHARD - F_LO - N_FORGET)]
        else:
            segments.append((d, 0, d * W_IN_SHARD - (N_FORGET if d > F_SHARD else 0), W_IN_SHARD))
    return segments


def _w_in_rearranged(g, name):
    D = g.shape[1]
    tr = _row_tile(D, 256)

    def body(g_ref, o_ref):
        for d, lo, at, width in _w_in_segments():
            o_ref[:, at:at + width] = g_ref[d, :, lo:lo + width]
        o_ref[:, N_IN:] = jnp.zeros((tr, BLK - N_FORGET), o_ref.dtype)

    return pl.pallas_call(
        body, name=name, out_shape=jax.ShapeDtypeStruct((D, N_MAIN + BLK), g.dtype), grid=(D // tr,),
        in_specs=[pl.BlockSpec((N_DEV, tr, W_IN_SHARD), lambda i: (0, i, 0))],
        out_specs=pl.BlockSpec((tr, N_MAIN + BLK), lambda i: (i, 0)),
        compiler_params=_params(),
    )(g)


def _w_in_pieces(dw_r, name, pair_major=False):
    D = dw_r.shape[0]
    tr = _row_tile(D, 256)
    lead = (2, 4) if pair_major else (N_DEV,)

    def body(x_ref, o_ref):
        for d, lo, at, width in _w_in_segments():
            slot = (d % 2, d // 2) if pair_major else (d,)
            o_ref[(*slot, slice(None), slice(lo, lo + width))] = x_ref[:, at:at + width]

    return pl.pallas_call(
        body, name=name, out_shape=jax.ShapeDtypeStruct((*lead, D, W_IN_SHARD), dw_r.dtype), grid=(D // tr,),
        in_specs=[pl.BlockSpec((tr, N_MAIN + BLK), lambda i: (i, 0))],
        out_specs=pl.BlockSpec((*lead, tr, W_IN_SHARD), lambda i: (*[0] * len(lead), i, 0)),
        compiler_params=_params(),
    )(dw_r)


def _row_pieces(dw):
    return dw.reshape(N_DEV, dw.shape[0] // N_DEV, dw.shape[1])


def _branch_pieces(dw):
    k, w, d = dw.shape
    return jnp.transpose(dw.reshape(k, w, N_DEV, d // N_DEV), (2, 0, 1, 3)).reshape(N_DEV, k * w, d // N_DEV)


def _pairs_col(a):
    return jnp.transpose(a.reshape(a.shape[0], 4, 2), (1, 0, 2))


def _pairs_row(a):
    return jnp.transpose(a.reshape(a.shape[0], 4, 2), (1, 2, 0))


def _heads_from_col(a):
    return jnp.transpose(a, (1, 0, 2)).reshape(a.shape[1], 8)


def _heads_from_row(a):
    return jnp.transpose(a, (2, 0, 1)).reshape(a.shape[2], 8)


def _pad_lanes(a, n):
    return jnp.pad(a, [(0, 0)] * (a.ndim - 1) + [(0, n - a.shape[-1])])


SMALL_SEGMENTS = (("dmod", 2 * 6 * D_MODEL), ("norm_mix_g", 2 * D_MODEL), ("norm_ffn_g", 2 * D_MODEL),
                  ("final_norm_g", D_MODEL), ("b_forget", 128), ("sinks", 128), ("rel_bias", 4224), ("loss", 128))
SMALL_ROWS = 176


def _pack_small(parts):
    flat = [_pad_lanes(parts[name].reshape(1, -1), size) for name, size in SMALL_SEGMENTS]
    total = sum(size for _, size in SMALL_SEGMENTS)
    flat.append(jnp.zeros((1, SMALL_ROWS * 128 - total), F32))
    return jnp.concatenate(flat, axis=1).reshape(SMALL_ROWS, 128)


def _unpack_small(packed, shapes):
    flat = packed.reshape(-1)
    out, pos = {}, 0
    for name, size in SMALL_SEGMENTS:
        shape = shapes[name]
        count = 1
        for d in shape:
            count *= d
        out[name] = flat[pos:pos + count].reshape(shape)
        pos += size
    return out


def kernel(x, c, norm_mix_g, norm_ffn_g, w_ada, b_ada, w_in, b_forget, sinks, rel_bias, w_branch, w_out, w_ffn_in, w_ffn_out, final_norm_g, loss_target, m_norm_mix_g, m_norm_ffn_g, m_w_ada, m_b_ada, m_w_in, m_b_forget, m_sinks, m_rel_bias, m_w_branch, m_w_out, m_w_ffn_in, m_w_ffn_out, m_final_norm_g, v_norm_mix_g, v_norm_ffn_g, v_w_ada, v_b_ada, v_w_in, v_b_forget, v_sinks, v_rel_bias, v_w_branch, v_w_out, v_w_ffn_in, v_w_ffn_out, v_final_norm_g):
    depth = w_in.shape[0]
    S, D = x.shape[1], x.shape[2]
    assert S % GROUP == 0 and S >= ATTN_WINDOW["c"] * BLK
    px, py, pc = _position()
    me = 4 * px + 2 * py + pc
    x0 = x[0]

    assert depth == 2
    big_weights = (w_in, w_branch, w_out, w_ffn_in, w_ffn_out)
    me_arr = jnp.stack([me, me]).astype(jnp.int32)
    me_in_arr = jnp.stack([2 * px + py, me]).astype(jnp.int32)

    def rest_matrices(g_branch, g_out, g_fin, g_fout):
        return (jnp.transpose(g_branch, (1, 2, 0, 3)).reshape(3, 512, D), g_out.reshape(D, D),
                g_fin.reshape(2 * FFN_HIDDEN, D), g_fout.reshape(FFN_HIDDEN, D))

    def arrive(started, after, name):
        mine, landed = _exchange_wait(started, False, after, f"{name}_wait", SAME_CORE)
        return mine, _forward_start(landed, mine[0], f"{name}_forward_start")

    def finish_gather(arrived, after, name):
        mine, forward = arrived
        landed = _forward_wait(forward, after, f"{name}_forward_wait")
        return _place_own(landed, mine, me.astype(jnp.int32).reshape(1), f"{name}_own")

    w_fin_t = _w_ffn_in_view(w_ffn_in)
    shards = [[t.astype(BF16) for t in (w_in[l], w_branch[l], w_out[l], w_fin_t[l], w_ffn_out[l])]
              for l in range(depth)]
    gathered_in0 = _big_all_gather(shards[0][:1], "comm_gather_w_in0")[0]
    gather_rest0 = _exchange_start(shards[0][1:], False, gathered_in0, "comm_gather_rest0_start", SAME_CORE)
    gather1 = _exchange_start(shards[1], False, gather_rest0[4], "comm_gather_weights1_start", SAME_CORE)
    W_in, W_branch, W_out, W_fin, W_fout = ([None, None] for _ in range(5))
    W_in[0] = _w_in_rearranged(gathered_in0, "w_in_rearrange0")

    c_all = _small_all_gather(c.reshape(8, 128), "comm_gather_c").reshape(N_DEV, D)
    mod_cols = _ada_fwd(c_all, w_ada, "ada_fwd")
    mod_all = _small_all_gather(mod_cols.reshape(-1, 128), "comm_gather_mod")
    mod_all = mod_all.reshape(N_DEV, depth, N_DEV, w_ada.shape[2])
    mod_mine = lax.dynamic_index_in_dim(mod_all, me, axis=2, keepdims=False)
    mod = jnp.transpose(mod_mine, (1, 0, 2)).reshape(depth, 6 * D) + b_ada + gather1[4][0:1, 0:1]
    mods = [[mod[l:l + 1, k * D:(k + 1) * D] for k in range(6)] for l in range(depth)]

    slopes = jnp.exp2(-jnp.arange(1, 9, dtype=F32))
    saved = []
    xs = x0
    for l in range(depth):
        if l == 1:
            g_in1, *g_rest1 = finish_gather(arrived1, xs, "comm_gather_weights1")
            W_in[1] = _w_in_rearranged(g_in1, "w_in_rearrange1")
            W_branch[1], W_out[1], W_fin[1], W_fout[1] = rest_matrices(*g_rest1)
        sh_m, sc_m, g_m, sh_f, sc_f, g_f = mods[l]
        gm, gf = norm_mix_g[l:l + 1], norm_ffn_g[l:l + 1]
        bfor = _pad_lanes(b_forget[l:l + 1], BLK)
        h = _norm_mod_fwd(xs, gm, sh_m, sc_m, f"norm_mix_fwd{l}")
        qkv, gates = _project(h, W_in[l], f"proj{l}")
        fb = _matmul(h, W_in[l], "nn", F32, f"proj_forget{l}", TILES["proj_forget"], n=BLK, b_off=N_MAIN // BLK)
        cum = _forget_fwd(fb, bfor, f"forget_fwd{l}")[:, :N_FORGET]
        cum_col, cum_row = _pairs_col(cum), _pairs_row(cum)
        tiles, tiles_t = _rel_expand(_rel_bases(rel_bias[l]), f"rel_expand{l}")
        o_a, lse_a = _attn_fwd("a", qkv, f"attn_a_fwd{l}", sinks=sinks[l], slopes=slopes)
        o_b, lse_b = _attn_fwd("b", qkv, f"attn_b_fwd{l}", cq_col=cum_col, ck_row=cum_row)
        arrived_rest0 = arrive(gather_rest0, o_b, "comm_gather_rest0") if l == 0 else None
        o_c, lse_c = _attn_fwd("c", qkv, f"attn_c_fwd{l}", bias=tiles, after=arrived_rest0[1][3] if l == 0 else None)
        if l == 0:
            W_branch[0], W_out[0], W_fin[0], W_fout[0] = rest_matrices(
                *finish_gather(arrived_rest0, o_c, "comm_gather_rest0"))
        x1, merged, mix = _merge_fwd(o_a, o_b, o_c, gates, W_branch[l], W_out[l], xs, g_m, f"merge_fwd{l}")
        h2 = _norm_mod_fwd(x1, gf, sh_f, sc_f, f"norm_ffn_fwd{l}")
        act = _ffn_in_fwd(h2, W_fin[l], f"ffn_in_fwd{l}")
        if l == 0:
            arrived1 = arrive(gather1, act, "comm_gather_weights1")
        x2, ffn = _matmul_resid(act, W_fout[l], x1, g_f, f"ffn_out{l}", TILES["ffn_out"],
                                after=arrived1[1][3] if l == 0 else None)
        saved.append(dict(x=xs, h=h, qkv=qkv, gates=gates, fb=fb, bfor=bfor, cum_col=cum_col, cum_row=cum_row,
                          tiles_t=tiles_t, o=(o_a, o_b, o_c), lse=(lse_a, lse_b, lse_c), merged=merged, mix=mix,
                          x1=x1, h2=h2, act=act, ffn=ffn))
        xs = x2

    dx, loss_tile, d_final_g, d_g_f, df = _final_loss(
        xs, loss_target[0], final_norm_g.reshape(1, D), (saved[-1]["ffn"], mods[-1][5]), "final_loss")

    grads = {k: [None] * depth for k in ("w_in", "w_branch", "w_out", "w_ffn_in", "w_ffn_out", "norm_mix_g",
                                          "norm_ffn_g", "b_forget", "sinks", "rel_bias", "dmod")}
    def rest_pieces(l):
        return [_branch_pieces(grads["w_branch"][l]), _row_pieces(grads["w_out"][l]),
                _row_pieces(grads["w_ffn_in"][l]), _row_pieces(grads["w_ffn_out"][l])]

    reduce1 = reduce_rest0 = reduce_in0 = None
    for l in reversed(range(depth)):
        sv = saved[l]
        sh_m, sc_m, g_m, sh_f, sc_f, g_f = mods[l]
        gm, gf = norm_mix_g[l:l + 1], norm_ffn_g[l:l + 1]
        du_g, du_u = _ffn_mid_bwd(sv["h2"], df, W_fin[l], W_fout[l], f"ffn_mid_bwd{l}")
        du = jnp.concatenate([du_g, du_u], axis=1)
        grads["w_ffn_out"][l] = _matmul(sv["act"], df, "tn", BF16, f"wgrad_ffn_out{l}", TILES["wgrad_ffn_out"])
        grads["w_ffn_in"][l] = _matmul(du, sv["h2"], "tn", BF16, f"wgrad_ffn_in{l}", TILES["wgrad_ffn_in"])
        dh2 = _matmul(du, W_fin[l], "nn", F32, f"dgrad_ffn_in{l}", TILES["dgrad_ffn_in"])
        dx1, d_sh_f, d_sc_f, d_gf, d_g_m, dmix = _norm_mod_bwd(sv["x1"], dh2, dx, gf, sc_f, f"norm_ffn_bwd{l}",
                                                               below=(sv["mix"], g_m))
        grads["w_out"][l] = _matmul(sv["merged"], dmix, "tn", BF16, f"wgrad_out{l}", TILES["wgrad_out"])
        o_a, o_b, o_c = sv["o"]
        dgates, d_w_branch, do_a, do_b, do_c, dl_a, dl_b, dl_c = _merge_bwd(
            dmix, o_a, o_b, o_c, sv["gates"], W_branch[l], W_out[l], f"merge_bwd{l}")
        grads["w_branch"][l] = d_w_branch.astype(BF16)
        lse_rows = [_pairs_row(_heads_from_col(t)) for t in sv["lse"]]
        if l == 0:
            reduce_rest0 = _exchange_start(rest_pieces(0), True, dgates, "comm_reduce_rest0_start")
            lse_rows = [t + reduce_rest0[4][0:1, 0:1] for t in lse_rows]
        dq_a, dk_a, dv_a, dsink = _attn_bwd("a", sv["qkv"], do_a, lse_rows[0], _pairs_row(dl_a), f"attn_a_bwd{l}",
                                            sinks=sinks[l], slopes=slopes)
        dq_b, dk_b, dv_b, dck, dcq = _attn_bwd("b", sv["qkv"], do_b, lse_rows[1], _pairs_row(dl_b),
                                               f"attn_b_bwd{l}", cq_row=sv["cum_row"], ck_col=sv["cum_col"])
        dq_c, dk_c, dv_c, dtiles_t = _attn_bwd("c", sv["qkv"], do_c, lse_rows[2], _pairs_row(dl_c),
                                               f"attn_c_bwd{l}", bias_t=sv["tiles_t"])
        grads["sinks"][l] = dsink[:, :2, 0].reshape(8)
        grads["rel_bias"][l] = _rel_reduce(dtiles_t, f"rel_reduce{l}")[:, 0, :N_REL]
        dcum_k = _pad_lanes(_heads_from_col(dck), BLK)
        dcum_q = _pad_lanes(_heads_from_row(dcq), BLK)
        dfb, d_bfor = _forget_bwd(dcum_q, dcum_k, sv["fb"], sv["bfor"], f"forget_bwd{l}")
        grads["b_forget"][l] = d_bfor[0, :N_FORGET]
        dproj = jnp.concatenate(
            [t.astype(BF16) for t in (dq_a, dk_a, dv_a, dq_b, dk_b, dv_b, dq_c, dk_c, dv_c, dgates, dfb)],
            axis=1)
        grads["w_in"][l] = _matmul(sv["h"], dproj, "tn", BF16, f"wgrad_in{l}", TILES["wgrad_in"])
        if l == 1:
            reduce1 = _exchange_start([_w_in_pieces(grads["w_in"][1], "w_in_pieces1")] + rest_pieces(1), True, dproj,
                                      "comm_reduce1_start")
        dh = _matmul(dproj, W_in[l], "nt", F32, f"dgrad_in{l}", TILES["dgrad_in"], after=reduce1[4] if l == 1 else None)
        d_g_f_here = d_g_f
        if l > 0:
            dx, d_sh_m, d_sc_m, d_gm, d_g_f, df = _norm_mod_bwd(sv["x"], dh, dx1, gm, sc_m, f"norm_mix_bwd{l}",
                                                                below=(saved[l - 1]["ffn"], mods[l - 1][5]))
        else:
            dx, d_sh_m, d_sc_m, d_gm = _norm_mod_bwd(sv["x"], dh, dx1, gm, sc_m, f"norm_mix_bwd{l}")
        grads["norm_mix_g"][l] = d_gm[0]
        grads["norm_ffn_g"][l] = d_gf[0]
        grads["dmod"][l] = jnp.concatenate([d_sh_m, d_sc_m, d_g_m, d_sh_f, d_sc_f, d_g_f_here], axis=1)[0]

    grad_x = dx.reshape(x.shape)

    small_shapes = dict(dmod=b_ada.shape, norm_mix_g=norm_mix_g.shape, norm_ffn_g=norm_ffn_g.shape,
                        final_norm_g=final_norm_g.shape, b_forget=b_forget.shape, sinks=sinks.shape,
                        rel_bias=rel_bias.shape, loss=())
    mine_small = _pack_small(dict(
        loss=_pad_lanes(loss_tile[0:1, 0:1], 128),
        dmod=jnp.stack(grads["dmod"]), norm_mix_g=jnp.stack(grads["norm_mix_g"]),
        norm_ffn_g=jnp.stack(grads["norm_ffn_g"]), final_norm_g=d_final_g[0],
        b_forget=_pad_lanes(jnp.stack(grads["b_forget"]).reshape(1, -1), 128),
        sinks=_pad_lanes(jnp.stack(grads["sinks"]).reshape(1, -1), 128),
        rel_bias=_pad_lanes(jnp.stack(grads["rel_bias"]).reshape(1, -1), 4224)))
    all_small = _small_all_gather(mine_small, "comm_gather_small").reshape(N_DEV, SMALL_ROWS, 128)
    pieces_in0 = _w_in_pieces(grads["w_in"][0], "w_in_pieces0", pair_major=True)
    from_sibling = _sibling_exchange([pieces_in0], "comm_reduce_in0_sibling")[0]
    pair_sum_in0 = _pair_add(pieces_in0, from_sibling, pc.astype(jnp.int32).reshape(1), "pair_add_in0")
    reduce_in0 = _exchange_start([pair_sum_in0], True, all_small, "comm_reduce_in0_start", CHIPS)
    in0_started = reduce_in0[4]

    def pack_params(b_ada_, nm, nf, fn, bf, sk, rb):
        return _pack_small(dict(dmod=b_ada_, norm_mix_g=nm, norm_ffn_g=nf, final_norm_g=fn, loss=jnp.zeros((1, 128), F32),
                                b_forget=_pad_lanes(bf.reshape(1, -1), 128), sinks=_pad_lanes(sk.reshape(1, -1), 128),
                                rel_bias=_pad_lanes(rb.reshape(1, -1), 4224)))

    small_out = _adamw(
        pack_params(b_ada, norm_mix_g, norm_ffn_g, final_norm_g, b_forget, sinks, rel_bias)[None],
        pack_params(m_b_ada, m_norm_mix_g, m_norm_ffn_g, m_final_norm_g, m_b_forget, m_sinks, m_rel_bias)[None],
        pack_params(v_b_ada, v_norm_mix_g, v_norm_ffn_g, v_final_norm_g, v_b_forget, v_sinks, v_rel_bias)[None],
        [all_small], "adamw_small", me_arr, after=in0_started)
    small_out = [_unpack_small(t[0], small_shapes) for t in small_out]

    dmod_all = all_small[:, :96].reshape(N_DEV, depth, 6 * D)
    dmod_cols = lax.dynamic_slice_in_dim(dmod_all, me * w_ada.shape[2], w_ada.shape[2], axis=2)
    d_w_ada = _ada_bwd(jnp.transpose(c_all), jnp.transpose(dmod_cols, (1, 0, 2)), "ada_bwd")

    big = {"w_ada": _adamw(w_ada, m_w_ada, v_w_ada, [d_w_ada[l:l + 1] for l in range(depth)], "adamw_w_ada", me_arr,
                           after=in0_started)}
    sent1, landed1 = _exchange_wait(reduce1, True, big["w_ada"][0], "comm_reduce1_wait")
    sent_rest0, landed_rest0 = _exchange_wait(reduce_rest0, True, landed1[0], "comm_reduce_rest0_wait")
    parts = {"w_in": [None, (landed1[0], sent1[0])]}
    for a, name in enumerate(("w_branch", "w_out", "w_ffn_in", "w_ffn_out")):
        parts[name] = [(landed_rest0[a], sent_rest0[a]), (landed1[1 + a], sent1[1 + a])]

    def update(name, w, m, v, view=lambda t: t):
        per_layer = lambda t: t.reshape(depth, -1, t.shape[-1])
        outs = _adamw(*[per_layer(view(t)) for t in (w, m, v)], parts[name], f"adamw_{name}",
                      me_in_arr if name == "w_in" else me_arr)
        big[name] = [view(t).reshape(w.shape) for t in outs]

    update("w_ffn_in", w_ffn_in, m_w_ffn_in, v_w_ffn_in, _w_ffn_in_view)
    update("w_ffn_out", w_ffn_out, m_w_ffn_out, v_w_ffn_out)
    update("w_branch", w_branch, m_w_branch, v_w_branch)
    update("w_out", w_out, m_w_out, v_w_out)
    sent_in0, landed_in0 = _exchange_wait(reduce_in0, True, big["w_out"][0], "comm_reduce_in0_wait", CHIPS)
    parts["w_in"][0] = (landed_in0[0], sent_in0[0])
    update("w_in", w_in, m_w_in, v_w_in)

    def leaf(kind, name):
        if name in big:
            return big[name][kind]
        return small_out[kind]["dmod" if name == "b_ada" else name]

    order = ["norm_mix_g", "norm_ffn_g", "w_ada", "b_ada", "w_in", "b_forget", "sinks", "rel_bias", "w_branch",
             "w_out", "w_ffn_in", "w_ffn_out", "final_norm_g"]
    loss = small_out[0]["loss"]
    return (loss, grad_x, *[leaf(0, n) for n in order], *[leaf(1, n) for n in order],
            *[leaf(2, n) for n in order], *[leaf(3, n) for n in order])
```

```python
import functools

import jax
import jax.numpy as jnp
from jax import lax
from jax.experimental import pallas as pl
from jax.experimental.pallas import tpu as pltpu

F32 = jnp.float32
BF16 = jnp.bfloat16
NEG_INF = -1e30
EPS = 1e-6
N_DEV = 8
BLK = 128
GROUP = 4 * BLK
VMEM_LIMIT_BYTES = 56 * 1024 * 1024

D_MODEL = 1024
N_QKV = 3840
N_GATES = 3072
N_MAIN = N_QKV + N_GATES
N_FORGET = 8
N_IN = N_MAIN + N_FORGET
F_COL = 2304
FFN_HIDDEN = 2816
N_REL = 257

ADAM_LR, ADAM_B1, ADAM_B2, ADAM_EPS, ADAM_WD, ADAM_STEP = 0.001, 0.9, 0.999, 1e-08, 0.01, 10

NN = (((1,), (0,)), ((), ()))
NT = (((1,), (1,)), ((), ()))
TN = (((0,), (0,)), ((), ()))
HIGHEST = lax.Precision.HIGHEST

ATTN_COLS = {"a": (0, 4, 5), "b": (6, 10, 14), "c": (18, 22, 26)}
ATTN_WINDOW = {"a": 2, "c": 5}
ATTN_BLOCKS_PER_STEP = {"a": 8, "b": GROUP // BLK, "c": 2}
ROW_TILE = 512


def _params():
    return pltpu.CompilerParams(vmem_limit_bytes=VMEM_LIMIT_BYTES)


def _tile(n, target):
    best = None
    t = 128
    while t <= min(n, target):
        if n % t == 0:
            best = t
        t += 128
    return best if best is not None else n


def _row_tile(n, target):
    t = min(n, target)
    while n % t:
        t -= 8
    return t


TILES = {
    "proj": (1024, 768, 1024), "proj_forget": (1024, 128, 1024),
    "ffn_out": (1024, 512, 2816), "ffn_fused": (512, 1408),
    "wgrad_ffn_out": (1408, 1024, 1024), "wgrad_ffn_in": (1408, 1024, 1024), "dgrad_ffn_in": (1024, 1024, 1408),
    "wgrad_out": (1024, 1024, 1024),
    "wgrad_in": (1024, 1408, 1024), "dgrad_in": (1024, 1024, 1408),
}


def _matmul(a, b, mode, out_dtype, name, tiles, *, n=None, a_off=0, b_off=0, m=None, after=None):
    tm, tn, tk = tiles
    if mode == "nn":
        M, K = a.shape if m is None else (m, a.shape[1])
        N = b.shape[1] if n is None else n
    elif mode == "nt":
        M, K = a.shape
        N = b.shape[0] if n is None else n
    else:
        K = a.shape[0]
        M = a.shape[1] if m is None else m
        N = b.shape[1] if n is None else n
    tm = _tile(M, tm) if M % 128 == 0 else M
    tn = _tile(N, tn)
    tk = _tile(K, tk)
    nk = K // tk
    dims = {"nn": NN, "nt": NT, "tn": TN}[mode]
    if mode == "nn":
        a_spec = pl.BlockSpec((tm, tk), lambda i, j, k: (i + a_off, k))
        b_spec = pl.BlockSpec((tk, tn), lambda i, j, k: (k, j + b_off))
    elif mode == "nt":
        a_spec = pl.BlockSpec((tm, tk), lambda i, j, k: (i + a_off, k))
        b_spec = pl.BlockSpec((tn, tk), lambda i, j, k: (j + b_off, k))
    else:
        a_spec = pl.BlockSpec((tk, tm), lambda i, j, k: (k, i + a_off))
        b_spec = pl.BlockSpec((tk, tn), lambda i, j, k: (k, j + b_off))

    def body(a_ref, b_ref, *rest):
        o_ref, acc_ref = rest[-2:]
        k = pl.program_id(2)
        part = lax.dot_general(a_ref[...], b_ref[...], dims, preferred_element_type=F32)
        if nk == 1:
            o_ref[...] = part.astype(o_ref.dtype)
        else:
            @pl.when(k == 0)
            def _():
                acc_ref[...] = part

            @pl.when(k > 0)
            def _():
                acc_ref[...] += part

            @pl.when(k == nk - 1)
            def _():
                o_ref[...] = acc_ref[...].astype(o_ref.dtype)

    return pl.pallas_call(
        body, name=name,
        out_shape=jax.ShapeDtypeStruct((M, N), out_dtype),
        grid=(M // tm, N // tn, nk),
        in_specs=[a_spec, b_spec] + ([ANY] if after is not None else []),
        out_specs=pl.BlockSpec((tm, tn), lambda i, j, k: (i, j)),
        scratch_shapes=[pltpu.VMEM((tm, tn) if nk > 1 else (8, 128), F32)],
        compiler_params=_params(),
    )(a, b, *([after] if after is not None else []))


def _project(h, w, name):
    S, D = h.shape
    tm, tn, _ = TILES["proj"]
    tm = _tile(S, tm)
    nq, ng = N_QKV // tn, N_GATES // tn

    def body(h_ref, w_ref, q_ref, g_ref):
        j = pl.program_id(1)
        acc = jnp.dot(h_ref[...], w_ref[...], preferred_element_type=F32)

        @pl.when(j < nq)
        def _():
            q_ref[...] = acc.astype(BF16)

        @pl.when(j >= nq)
        def _():
            g_ref[...] = acc

    return pl.pallas_call(
        body, name=name,
        out_shape=(jax.ShapeDtypeStruct((S, N_QKV), BF16), jax.ShapeDtypeStruct((S, N_GATES), F32)),
        grid=(S // tm, nq + ng),
        in_specs=[pl.BlockSpec((tm, D), lambda i, j: (i, 0)), pl.BlockSpec((D, tn), lambda i, j: (0, j))],
        out_specs=(pl.BlockSpec((tm, tn), lambda i, j: (i, jnp.minimum(j, nq - 1))),
                   pl.BlockSpec((tm, tn), lambda i, j: (i, jnp.maximum(j - nq, 0)))),
        compiler_params=_params(),
    )(h, w)


def _matmul_resid(a, b, resid, gate, name, tiles, after=None):
    M, K = a.shape
    N = b.shape[1]
    tm, tn, tk = (_tile(d, t) for d, t in zip((M, N, K), tiles))
    nk = K // tk

    def body(a_ref, b_ref, r_ref, g_ref, *rest):
        o_ref, s_ref, acc_ref = rest[-3:]
        k = pl.program_id(2)
        part = jnp.dot(a_ref[...], b_ref[...], preferred_element_type=F32)

        def finish(acc):
            o_ref[...] = r_ref[...] + g_ref[...] * acc
            s_ref[...] = acc.astype(BF16)

        if nk == 1:
            finish(part)
        else:
            @pl.when(k == 0)
            def _():
                acc_ref[...] = part

            @pl.when(k > 0)
            def _():
                acc_ref[...] += part

            @pl.when(k == nk - 1)
            def _():
                finish(acc_ref[...])

    return pl.pallas_call(
        body, name=name,
        out_shape=(jax.ShapeDtypeStruct((M, N), F32), jax.ShapeDtypeStruct((M, N), BF16)),
        grid=(M // tm, N // tn, nk),
        in_specs=[pl.BlockSpec((tm, tk), lambda i, j, k: (i, k)),
                  pl.BlockSpec((tk, tn), lambda i, j, k: (k, j)),
                  pl.BlockSpec((tm, tn), lambda i, j, k: (i, j)),
                  pl.BlockSpec((1, tn), lambda i, j, k: (0, j))] + ([ANY] if after is not None else []),
        out_specs=(pl.BlockSpec((tm, tn), lambda i, j, k: (i, j)),
                   pl.BlockSpec((tm, tn), lambda i, j, k: (i, j))),
        scratch_shapes=[pltpu.VMEM((tm, tn) if nk > 1 else (8, 128), F32)],
        compiler_params=_params(),
    )(a, b, resid, gate, *([after] if after is not None else []))


def _norm_mod_fwd(x, g, shift, scale, name):
    S, D = x.shape
    ts = _row_tile(S, ROW_TILE)

    def body(x_ref, g_ref, sh_ref, sc_ref, h_ref):
        xv = x_ref[...]
        rstd = lax.rsqrt(jnp.mean(xv * xv, axis=-1, keepdims=True) + EPS)
        y = xv * rstd * g_ref[...]
        h_ref[...] = (y * (1.0 + sc_ref[...]) + sh_ref[...]).astype(BF16)

    row = pl.BlockSpec((1, D), lambda i: (0, 0))
    return pl.pallas_call(
        body, name=name, out_shape=jax.ShapeDtypeStruct((S, D), BF16), grid=(S // ts,),
        in_specs=[pl.BlockSpec((ts, D), lambda i: (i, 0)), row, row, row],
        out_specs=pl.BlockSpec((ts, D), lambda i: (i, 0)),
        compiler_params=_params(),
    )(x, g, shift, scale)


def _accumulate_rows(i, pairs):
    @pl.when(i == 0)
    def _():
        for ref, value in pairs:
            ref[...] = value

    @pl.when(i > 0)
    def _():
        for ref, value in pairs:
            ref[...] += value


def _gated_residual_bwd(dx, f_ref, gate_ref, df_ref):
    df_ref[...] = (dx * gate_ref[...]).astype(BF16)
    return jnp.sum(dx * f_ref[...].astype(F32), axis=0, keepdims=True)


def _norm_mod_bwd(x, dh, dres, g, scale, name, below=None):
    S, D = x.shape
    ts = _row_tile(S, ROW_TILE)

    def body(x_ref, dh_ref, dr_ref, g_ref, sc_ref, *rest):
        i = pl.program_id(0)
        xv, dhv, gv = x_ref[...], dh_ref[...], g_ref[...]
        rstd = lax.rsqrt(jnp.mean(xv * xv, axis=-1, keepdims=True) + EPS)
        xhat = xv * rstd
        dn = dhv * (1.0 + sc_ref[...])
        dxhat = dn * gv
        proj = jnp.mean(dxhat * xhat, axis=-1, keepdims=True)
        dx = dr_ref[...] + rstd * (dxhat - xhat * proj)
        sums = [jnp.sum(dhv, axis=0, keepdims=True), jnp.sum(dhv * (xhat * gv), axis=0, keepdims=True),
                jnp.sum(dn * xhat, axis=0, keepdims=True)]
        if below is None:
            dx_ref, *sum_refs = rest
        else:
            f_ref, gate_ref, dx_ref, *sum_refs, df_ref = rest
            sums.append(_gated_residual_bwd(dx, f_ref, gate_ref, df_ref))
        dx_ref[...] = dx
        _accumulate_rows(i, list(zip(sum_refs, sums)))

    tile = pl.BlockSpec((ts, D), lambda i: (i, 0))
    row = pl.BlockSpec((1, D), lambda i: (0, 0))
    vec = jax.ShapeDtypeStruct((1, D), F32)
    fused = below is not None
    return pl.pallas_call(
        body, name=name,
        out_shape=(jax.ShapeDtypeStruct((S, D), F32), vec, vec, vec)
        + ((vec, jax.ShapeDtypeStruct((S, D), BF16)) if fused else ()),
        grid=(S // ts,),
        in_specs=[tile, tile, tile, row, row] + ([tile, row] if fused else []),
        out_specs=(tile, row, row, row) + ((row, tile) if fused else ()),
        compiler_params=_params(),
    )(x, dh, dres, g, scale, *(below if fused else ()))


def _ffn_in_fwd(h, w_t, name):
    S, D = h.shape
    F = w_t.shape[0] // 2
    tm, tn = _tile(S, TILES["ffn_fused"][0]), _tile(F, TILES["ffn_fused"][1])
    nj = F // tn

    def body(h_ref, wg_ref, wu_ref, o_ref):
        hv = h_ref[...]
        ug = lax.dot_general(hv, wg_ref[...], NT, preferred_element_type=F32)
        uu = lax.dot_general(hv, wu_ref[...], NT, preferred_element_type=F32)
        o_ref[...] = (ug * jax.nn.sigmoid(ug) * uu).astype(BF16)

    return pl.pallas_call(
        body, name=name, out_shape=jax.ShapeDtypeStruct((S, F), BF16), grid=(nj, S // tm),
        in_specs=[pl.BlockSpec((tm, D), lambda j, i: (i, 0)),
                  pl.BlockSpec((tn, D), lambda j, i: (j, 0)),
                  pl.BlockSpec((tn, D), lambda j, i: (j + nj, 0))],
        out_specs=pl.BlockSpec((tm, tn), lambda j, i: (i, j)),
        compiler_params=_params(),
    )(h, w_t, w_t)


def _ffn_mid_bwd(h, df, w_in_t, w_out, name):
    S, D = h.shape
    F = w_in_t.shape[0] // 2
    tm, tn = _tile(S, TILES["ffn_fused"][0]), _tile(F, TILES["ffn_fused"][1])
    nj = F // tn

    def body(h_ref, df_ref, wg_ref, wu_ref, wo_ref, dg_ref, du_ref):
        hv = h_ref[...]
        ug = lax.dot_general(hv, wg_ref[...], NT, preferred_element_type=F32)
        uu = lax.dot_general(hv, wu_ref[...], NT, preferred_element_type=F32)
        dact = lax.dot_general(df_ref[...], wo_ref[...], NT, preferred_element_type=F32)
        sig = jax.nn.sigmoid(ug)
        dg_ref[...] = (dact * uu * (sig * (1.0 + ug * (1.0 - sig)))).astype(BF16)
        du_ref[...] = (dact * (ug * sig)).astype(BF16)

    out = jax.ShapeDtypeStruct((S, F), BF16)
    return pl.pallas_call(
        body, name=name, out_shape=(out, out), grid=(nj, S // tm),
        in_specs=[pl.BlockSpec((tm, D), lambda j, i: (i, 0)),
                  pl.BlockSpec((tm, D), lambda j, i: (i, 0)),
                  pl.BlockSpec((tn, D), lambda j, i: (j, 0)),
                  pl.BlockSpec((tn, D), lambda j, i: (j + nj, 0)),
                  pl.BlockSpec((tn, D), lambda j, i: (j, 0))],
        out_specs=(pl.BlockSpec((tm, tn), lambda j, i: (i, j)), pl.BlockSpec((tm, tn), lambda j, i: (i, j))),
        compiler_params=_params(),
    )(h, df, w_in_t, w_in_t, w_out)


def _merge_fwd(o_a, o_b, o_c, gates, w_branch, w_out, resid, gate, name, *, tm=512):
    S, W = o_a.shape
    D = w_branch.shape[2]
    tm = _row_tile(S, tm)

    def body(oa_ref, ob_ref, oc_ref, g_ref, w_ref, wo_ref, r_ref, gm_ref, x_ref, m_ref, mix_ref):
        acc = None
        for k, o_ref in enumerate((oa_ref, ob_ref, oc_ref)):
            y = jnp.dot(o_ref[...], w_ref[k], preferred_element_type=F32)
            t = jax.nn.sigmoid(g_ref[:, k * D:(k + 1) * D]) * y
            acc = t if acc is None else acc + t
        merged = acc.astype(BF16)
        m_ref[...] = merged
        mix = jnp.dot(merged, wo_ref[...], preferred_element_type=F32)
        x_ref[...] = r_ref[...] + gm_ref[...] * mix
        mix_ref[...] = mix.astype(BF16)

    o_spec = pl.BlockSpec((tm, W), lambda i: (i, 0))
    tile = pl.BlockSpec((tm, D), lambda i: (i, 0))
    return pl.pallas_call(
        body, name=name,
        out_shape=(jax.ShapeDtypeStruct((S, D), F32), jax.ShapeDtypeStruct((S, D), BF16), jax.ShapeDtypeStruct((S, D), BF16)),
        grid=(S // tm,),
        in_specs=[o_spec, o_spec, o_spec, pl.BlockSpec((tm, 3 * D), lambda i: (i, 0)),
                  pl.BlockSpec((3, W, D), lambda i: (0, 0, 0)), pl.BlockSpec((D, D), lambda i: (0, 0)),
                  tile, pl.BlockSpec((1, D), lambda i: (0, 0))],
        out_specs=(tile, tile, tile),
        compiler_params=_params(),
    )(o_a, o_b, o_c, gates, w_branch, w_out, resid, gate)


def _merge_bwd(dmix, o_a, o_b, o_c, gates, w_branch, w_out, name, *, tm=256):
    S, W = o_a.shape
    D = w_branch.shape[2]
    tm = _row_tile(S, tm)
    n_heads = W // 64

    def body(dm_ref, oa_ref, ob_ref, oc_ref, g_ref, w_ref, wo_ref, dg_ref, dw_ref,
             doa_ref, dob_ref, doc_ref, dla_ref, dlb_ref, dlc_ref):
        first = pl.program_id(0) == 0
        dm = lax.dot_general(dm_ref[...], wo_ref[...], NT, preferred_element_type=F32)
        branches = ((oa_ref, doa_ref, dla_ref), (ob_ref, dob_ref, dlb_ref), (oc_ref, doc_ref, dlc_ref))
        for k, (o_ref, do_ref, dl_ref) in enumerate(branches):
            wk = w_ref[k]
            ov = o_ref[...]
            y = jnp.dot(ov, wk, preferred_element_type=F32)
            g = jax.nn.sigmoid(g_ref[:, k * D:(k + 1) * D])
            dy = (dm * g).astype(BF16)
            dwk = lax.dot_general(ov, dy, TN, preferred_element_type=F32)

            @pl.when(first)
            def _(k=k, dwk=dwk):
                dw_ref[k] = dwk

            @pl.when(jnp.logical_not(first))
            def _(k=k, dwk=dwk):
                dw_ref[k] += dwk
            dg_ref[:, k * D:(k + 1) * D] = (dm * y * (g * (1.0 - g))).astype(BF16)
            do16 = lax.dot_general(dy, wk, NT, preferred_element_type=F32).astype(BF16)
            do_ref[...] = do16
            prod = do16.astype(F32) * ov.astype(F32)
            for h in range(n_heads):
                dl_ref[:, h:h + 1] = jnp.sum(prod[:, 64 * h:64 * (h + 1)], axis=1, keepdims=True)

    o_spec = pl.BlockSpec((tm, W), lambda i: (i, 0))
    wide = pl.BlockSpec((tm, 3 * D), lambda i: (i, 0))
    dl_spec = pl.BlockSpec((tm, n_heads), lambda i: (i, 0))
    o_out = jax.ShapeDtypeStruct((S, W), BF16)
    wide_out = jax.ShapeDtypeStruct((S, 3 * D), BF16)
    dl_out = jax.ShapeDtypeStruct((S, n_heads), F32)
    whole = pl.BlockSpec((3, W, D), lambda i: (0, 0, 0))
    return pl.pallas_call(
        body, name=name,
        out_shape=(wide_out, jax.ShapeDtypeStruct((3, W, D), F32), o_out, o_out, o_out, dl_out, dl_out, dl_out),
        grid=(S // tm,),
        in_specs=[pl.BlockSpec((tm, D), lambda i: (i, 0)), o_spec, o_spec, o_spec, wide, whole,
                  pl.BlockSpec((D, D), lambda i: (0, 0))],
        out_specs=(wide, whole, o_spec, o_spec, o_spec, dl_spec, dl_spec, dl_spec),
        compiler_params=_params(),
    )(dmix, o_a, o_b, o_c, gates, w_branch, w_out)


def _band_mask(variant, t_abs, s_abs):
    if variant == "b":
        return s_abs <= t_abs
    qc, kc = t_abs >> 6, s_abs >> 6
    return (kc <= qc) & (kc >= qc - (2 if variant == "a" else 8))


def _attn_fwd(variant, qkv, name, *, sinks=None, slopes=None, cq_col=None, ck_row=None, bias=None, after=None):
    S = qkv.shape[0]
    nb = S // BLK
    qb, kb, vb = ATTN_COLS[variant]
    shared_kv = variant == "a"
    win = ATTN_WINDOW.get(variant)
    per_step = ATTN_BLOCKS_PER_STEP[variant]

    def body(*refs):
        if after is not None:
            refs = refs[:-3] + refs[-2:]
        if variant == "a":
            q_ref, k_ref, v_ref, sink_ref, slope_ref, o_ref, lse_ref = refs
        elif variant == "b":
            q_ref, k_ref, v_ref, cq_ref, ck_ref, o_ref, lse_ref = refs
        else:
            q_ref, k_ref, v_ref, bias_ref, o_ref, lse_ref = refs
        p = pl.program_id(0)
        lane = lax.broadcasted_iota(jnp.int32, (1, BLK), 1)

        def compute(i, rows, start, n_keys):
            n_rows = rows.stop - rows.start
            t_abs = i * BLK + lax.broadcasted_iota(jnp.int32, (n_rows, 1), 0)
            q2 = q_ref[rows, :].astype(F32) * 0.125
            k_w = k_ref[pl.ds(start, n_keys), :]
            v_w = v_ref[pl.ds(start, n_keys), :]
            s_abs = start + lax.broadcasted_iota(jnp.int32, (1, n_keys), 1)
            valid = _band_mask(variant, t_abs, s_abs)
            outs = []
            for half in (0, 1):
                hmask = (lane >= 64) if half else (lane < 64)
                qh = jnp.where(hmask, q2, 0.0)
                if shared_kv:
                    swap = (p // 2) != half
                    qh = jnp.where(swap, pltpu.roll(qh, 64, 1), qh)
                s = lax.dot_general(qh.astype(BF16), k_w, NT, preferred_element_type=F32)
                if variant == "a":
                    head = 2 * p + half
                    s = s + (-slope_ref[head]) * jnp.abs(t_abs - s_abs).astype(F32)
                elif variant == "b":
                    s = s + cq_ref[rows, half:half + 1] - ck_ref[half:half + 1, pl.ds(start, n_keys)]
                else:
                    j0 = start // BLK
                    s = s + jnp.concatenate([jnp.concatenate(
                        [bias_ref[half, jnp.clip(i + r - j0 - b, 0, 4)] for b in range(n_keys // BLK)], axis=1)
                        for r in range(n_rows // BLK)], axis=0)
                s = jnp.where(valid, s, NEG_INF)
                m = jnp.max(s, axis=1, keepdims=True)
                if variant == "a":
                    m = jnp.maximum(m, sink_ref[head])
                pe = jnp.exp(s - m)
                l = jnp.sum(pe, axis=1, keepdims=True)
                if variant == "a":
                    l = l + jnp.exp(sink_ref[head] - m)
                out = jnp.dot(pe.astype(BF16), v_w, preferred_element_type=F32) / l
                if shared_kv:
                    out = jnp.where(swap, pltpu.roll(out, 64, 1), out)
                outs.append(out)
                lse_ref[rows, half:half + 1] = m + jnp.log(l)
            o_ref[rows, :] = jnp.where(lane < 64, outs[0], outs[1]).astype(BF16)

        step = pl.program_id(1)
        if variant == "b":
            for g in range(S // GROUP):
                pl.when(step == g)(functools.partial(compute, step * per_step, slice(0, GROUP), 0, (g + 1) * GROUP))
        elif variant == "c":
            span = win + per_step - 1
            start = jnp.clip(step * per_step - (win - 1), 0, nb - span) * BLK
            compute(step * per_step, slice(0, per_step * BLK), pl.multiple_of(start, BLK), span * BLK)
        else:
            for sub in range(per_step):
                i = step * per_step + sub
                start = jnp.clip(i - (win - 1), 0, nb - win) * BLK
                compute(i, slice(sub * BLK, (sub + 1) * BLK), pl.multiple_of(start, BLK), win * BLK)

    tq = per_step * BLK
    kv_col = (lambda p, i: (0, kb)) if shared_kv else (lambda p, i: (0, kb + p))
    vv_col = (lambda p, i: (0, vb)) if shared_kv else (lambda p, i: (0, vb + p))
    in_specs = [pl.BlockSpec((tq, BLK), lambda p, i: (i, qb + p)),
                pl.BlockSpec((S, BLK), kv_col), pl.BlockSpec((S, BLK), vv_col)]
    args = [qkv, qkv, qkv]
    if variant == "a":
        in_specs += [pl.BlockSpec(memory_space=pltpu.SMEM), pl.BlockSpec(memory_space=pltpu.SMEM)]
        args += [sinks, slopes]
    elif variant == "b":
        in_specs += [pl.BlockSpec((None, tq, 2), lambda p, i: (p, i, 0)),
                     pl.BlockSpec((None, 2, S), lambda p, i: (p, 0, 0))]
        args += [cq_col, ck_row]
    else:
        in_specs += [pl.BlockSpec((2, 5, BLK, BLK), lambda p, i: (p, 0, 0, 0))]
        args += [bias]
    if after is not None:
        in_specs.append(ANY)
        args.append(after)
    return pl.pallas_call(
        body, name=name,
        out_shape=(jax.ShapeDtypeStruct((S, 512), BF16), jax.ShapeDtypeStruct((4, S, 2), F32)),
        grid=(4, nb // per_step), in_specs=in_specs,
        out_specs=(pl.BlockSpec((tq, BLK), lambda p, i: (i, p)),
                   pl.BlockSpec((None, tq, 2), lambda p, i: (p, i, 0))),
        compiler_params=_params(),
    )(*args)


def _attn_bwd(variant, qkv, do, lse_row, delta_row, name, *, sinks=None, slopes=None, cq_row=None,
              ck_col=None, bias_t=None):
    S = qkv.shape[0]
    nb = S // BLK
    qb, kb, vb = ATTN_COLS[variant]
    shared_kv = variant == "a"
    win = ATTN_WINDOW.get(variant)
    per_step = ATTN_BLOCKS_PER_STEP[variant]

    def body(*refs):
        *refs, dqt_ref = refs
        if variant == "a":
            (q_ref, k_ref, v_ref, do_ref, lse_ref, dl_ref, sink_ref, slope_ref,
             dq_ref, dk_ref, dv_ref, ex_ref) = refs
        elif variant == "b":
            (q_ref, k_ref, v_ref, do_ref, lse_ref, dl_ref, cq_ref, ck_ref,
             dq_ref, dk_ref, dv_ref, ex_ref, dcq_ref) = refs
        else:
            (q_ref, k_ref, v_ref, do_ref, lse_ref, dl_ref, bias_ref,
             dq_ref, dk_ref, dv_ref, ex_ref) = refs
        p = pl.program_id(0)
        lane = lax.broadcasted_iota(jnp.int32, (1, BLK), 1)
        hmasks = [(lane < 64), (lane >= 64)]
        swaps = [(p // 2) != half for half in (0, 1)] if shared_kv else None

        @pl.when(pl.program_id(1) == 0)
        def _():
            dqt_ref[...] = jnp.zeros_like(dqt_ref)
            if variant == "b":
                dcq_ref[...] = jnp.zeros_like(dcq_ref)
            else:
                ex_ref[...] = jnp.zeros_like(ex_ref)

        def to_kv_lanes(x, h):
            x = jnp.where(hmasks[h], x, 0.0)
            if shared_kv:
                x = jnp.where(swaps[h], pltpu.roll(x, 64, 1), x)
            return x

        def compute(j, rows, start, n_q):
            n_rows = rows.stop - rows.start
            s_abs = j * BLK + lax.broadcasted_iota(jnp.int32, (n_rows, 1), 0)
            off_k = pl.multiple_of(j * BLK, BLK)
            k2 = k_ref[rows, :].astype(F32)
            v2 = v_ref[rows, :].astype(F32)
            if shared_kv:
                kv_lane = (lane >> 6) == (p // 2)
                k_src, v_src = jnp.where(kv_lane, k2, 0.0), jnp.where(kv_lane, v2, 0.0)
                k_al = [jnp.where(swaps[h], pltpu.roll(k_src, 64, 1), k_src) for h in (0, 1)]
                v_al = [jnp.where(swaps[h], pltpu.roll(v_src, 64, 1), v_src) for h in (0, 1)]
            else:
                k_al = [jnp.where(hmasks[h], k2, 0.0) for h in (0, 1)]
                v_al = [jnp.where(hmasks[h], v2, 0.0) for h in (0, 1)]
            k_al = [(t * 0.125).astype(BF16) for t in k_al]
            v_al = [t.astype(BF16) for t in v_al]
            q_w = q_ref[pl.ds(start, n_q), :]
            do_w = do_ref[pl.ds(start, n_q), :]
            t_abs = start + lax.broadcasted_iota(jnp.int32, (1, n_q), 1)
            valid = _band_mask(variant, t_abs, s_abs)
            dk_acc = dv_acc = None
            ds_both = []
            for half in (0, 1):
                s = lax.dot_general(k_al[half], q_w, NT, preferred_element_type=F32)
                if variant == "a":
                    s = s + (-slope_ref[2 * p + half]) * jnp.abs(t_abs - s_abs).astype(F32)
                elif variant == "b":
                    s = s + cq_ref[half:half + 1, pl.ds(start, n_q)] - ck_ref[rows, half:half + 1]
                else:
                    i0 = start // BLK
                    s = s + jnp.concatenate([jnp.concatenate(
                        [bias_ref[half, jnp.clip(i0 + b - j - r, 0, 4)] for b in range(n_q // BLK)], axis=1)
                        for r in range(n_rows // BLK)], axis=0)
                pr = jnp.where(valid, jnp.exp(s - lse_ref[half:half + 1, pl.ds(start, n_q)]), 0.0)
                dp = lax.dot_general(v_al[half], do_w, NT, preferred_element_type=F32)
                ds = pr * (dp - dl_ref[half:half + 1, pl.ds(start, n_q)])
                ds16 = ds.astype(BF16)
                dv_h = to_kv_lanes(jnp.dot(pr.astype(BF16), do_w, preferred_element_type=F32), half)
                dk_h = to_kv_lanes(jnp.dot(ds16, q_w, preferred_element_type=F32) * 0.125, half)
                dv_acc = dv_h if dv_acc is None else dv_acc + dv_h
                dk_acc = dk_h if dk_acc is None else dk_acc + dk_h
                ds_both.append(ds16)
                if variant == "b":
                    ex_ref[rows, half:half + 1] = -jnp.sum(ds, axis=1, keepdims=True)
                    dcq_ref[half:half + 1, pl.ds(start, n_q)] += jnp.sum(ds, axis=0, keepdims=True)
                elif variant == "c":
                    for r in range(n_rows // BLK):
                        for b in range(n_q // BLK):
                            ex_ref[half, jnp.clip(i0 + b - j - r, 0, 4)] += ds[r * BLK:(r + 1) * BLK, b * BLK:(b + 1) * BLK]
            dq_t = lax.dot_general(jnp.concatenate(k_al, axis=0), jnp.concatenate(ds_both, axis=0), TN,
                                   preferred_element_type=F32)
            dqt_ref[:, pl.ds(start, n_q)] += dq_t
            if shared_kv:
                @pl.when(p == 0)
                def _():
                    dk_ref[pl.ds(off_k, n_rows), :] = dk_acc
                    dv_ref[pl.ds(off_k, n_rows), :] = dv_acc

                @pl.when(p > 0)
                def _():
                    dk_ref[pl.ds(off_k, n_rows), :] += dk_acc
                    dv_ref[pl.ds(off_k, n_rows), :] += dv_acc
            else:
                dk_ref[pl.ds(off_k, n_rows), :] = dk_acc.astype(dk_ref.dtype)
                dv_ref[pl.ds(off_k, n_rows), :] = dv_acc.astype(dv_ref.dtype)
            if variant == "a":
                for half in (0, 1):
                    p_sink = jnp.exp(sink_ref[2 * p + half] - lse_ref[half:half + 1, pl.ds(off_k, n_rows)])
                    term = p_sink * dl_ref[half:half + 1, pl.ds(off_k, n_rows)]
                    ex_ref[half:half + 1, :] += -jnp.sum(term, axis=1, keepdims=True)

        step = pl.program_id(1)
        if variant == "b":
            for g in range(S // GROUP):
                pl.when(step == g)(functools.partial(compute, step * per_step, slice(0, GROUP), g * GROUP, S - g * GROUP))
        elif variant == "c":
            span = win + per_step - 1
            start = jnp.clip(step * per_step, 0, nb - span) * BLK
            compute(step * per_step, slice(0, per_step * BLK), pl.multiple_of(start, BLK), span * BLK)
        else:
            for sub in range(per_step):
                j = step * per_step + sub
                start = jnp.clip(j, 0, nb - win) * BLK
                compute(j, slice(sub * BLK, (sub + 1) * BLK), pl.multiple_of(start, BLK), win * BLK)

        @pl.when(step == nb // per_step - 1)
        def _():
            dq_ref[...] = jnp.transpose(dqt_ref[...]).astype(BF16)

    tk = per_step * BLK
    col = lambda c0: (lambda p, j: (0, c0 + p))
    kv_blk = (lambda c0: (lambda p, j: (j, c0))) if shared_kv else (lambda c0: (lambda p, j: (j, c0 + p)))
    pair = lambda p, j: (0, p)
    row_stat = pl.BlockSpec((None, 2, S), lambda p, j: (p, 0, 0))
    in_specs = [pl.BlockSpec((S, BLK), col(qb)),
                pl.BlockSpec((tk, BLK), kv_blk(kb)), pl.BlockSpec((tk, BLK), kv_blk(vb)),
                pl.BlockSpec((S, BLK), pair), row_stat, row_stat]
    args = [qkv, qkv, qkv, do, lse_row, delta_row]
    kv_width = BLK if shared_kv else 512
    kv_out = pl.BlockSpec((S, BLK), (lambda p, j: (0, 0)) if shared_kv else pair)
    kv_dtype = F32 if shared_kv else BF16
    out_shape = [jax.ShapeDtypeStruct((S, 512), BF16), jax.ShapeDtypeStruct((S, kv_width), kv_dtype),
                 jax.ShapeDtypeStruct((S, kv_width), kv_dtype)]
    out_specs = [pl.BlockSpec((S, BLK), pair), kv_out, kv_out]
    if variant == "a":
        in_specs += [pl.BlockSpec(memory_space=pltpu.SMEM), pl.BlockSpec(memory_space=pltpu.SMEM)]
        args += [sinks, slopes]
        out_shape.append(jax.ShapeDtypeStruct((4, 8, BLK), F32))
        out_specs.append(pl.BlockSpec((None, 8, BLK), lambda p, j: (p, 0, 0)))
    elif variant == "b":
        in_specs += [row_stat, pl.BlockSpec((None, tk, 2), lambda p, j: (p, j, 0))]
        args += [cq_row, ck_col]
        out_shape += [jax.ShapeDtypeStruct((4, S, 2), F32), jax.ShapeDtypeStruct((4, 2, S), F32)]
        out_specs += [pl.BlockSpec((None, tk, 2), lambda p, j: (p, j, 0)), row_stat]
    else:
        in_specs += [pl.BlockSpec((2, 5, BLK, BLK), lambda p, j: (p, 0, 0, 0))]
        args += [bias_t]
        out_shape.append(jax.ShapeDtypeStruct((8, 5, BLK, BLK), F32))
        out_specs.append(pl.BlockSpec((2, 5, BLK, BLK), lambda p, j: (p, 0, 0, 0)))
    return pl.pallas_call(
        body, name=name, out_shape=tuple(out_shape), grid=(4, nb // per_step),
        in_specs=in_specs, out_specs=tuple(out_specs), scratch_shapes=[pltpu.VMEM((BLK, S), F32)],
        compiler_params=_params(),
    )(*args)


def _log_sigmoid(x):
    return jnp.minimum(x, 0.0) - jnp.log(1.0 + jnp.exp(-jnp.abs(x)))


def _forget_fwd(fb, b_forget, name):
    S = fb.shape[0]
    nb = S // GROUP

    def body(fb_ref, b_ref, cum_ref, carry_ref):
        i = pl.program_id(0)
        logf = _log_sigmoid(fb_ref[...] + b_ref[...])
        r = lax.broadcasted_iota(jnp.int32, (GROUP, GROUP), 0)
        c = lax.broadcasted_iota(jnp.int32, (GROUP, GROUP), 1)
        tri = (c <= r).astype(F32)

        @pl.when(i == 0)
        def _():
            carry_ref[...] = jnp.zeros_like(carry_ref)

        cum = jnp.dot(tri, logf, preferred_element_type=F32, precision=HIGHEST) + carry_ref[0:1, :]
        cum_ref[...] = cum
        carry_ref[...] = jnp.broadcast_to(cum[GROUP - 1:GROUP, :], carry_ref.shape)

    return pl.pallas_call(
        body, name=name, out_shape=jax.ShapeDtypeStruct((S, BLK), F32), grid=(nb,),
        in_specs=[pl.BlockSpec((GROUP, BLK), lambda i: (i, 0)), pl.BlockSpec((1, BLK), lambda i: (0, 0))],
        out_specs=pl.BlockSpec((GROUP, BLK), lambda i: (i, 0)),
        scratch_shapes=[pltpu.VMEM((8, BLK), F32)],
        compiler_params=_params(),
    )(fb, b_forget)


def _forget_bwd(dcum_q, dcum_k, fb, b_forget, name):
    S = fb.shape[0]
    nb = S // GROUP

    def body(dq_ref, dk_ref, fb_ref, b_ref, dfb_ref, db_ref, carry_ref):
        g = pl.program_id(0)
        r = lax.broadcasted_iota(jnp.int32, (GROUP, GROUP), 0)
        c = lax.broadcasted_iota(jnp.int32, (GROUP, GROUP), 1)
        tri = (c >= r).astype(F32)

        @pl.when(g == 0)
        def _():
            carry_ref[...] = jnp.zeros_like(carry_ref)

        dcum = dq_ref[...] + dk_ref[...]
        dlogf = jnp.dot(tri, dcum, preferred_element_type=F32, precision=HIGHEST) + carry_ref[0:1, :]
        carry_ref[...] = jnp.broadcast_to(dlogf[0:1, :], carry_ref.shape)
        x = fb_ref[...] + b_ref[...]
        lane = lax.broadcasted_iota(jnp.int32, (1, BLK), 1)
        dfb = jnp.where(lane < N_FORGET, dlogf * jax.nn.sigmoid(-x), 0.0)
        dfb_ref[...] = dfb
        db = jnp.sum(dfb, axis=0, keepdims=True)

        @pl.when(g == 0)
        def _():
            db_ref[...] = db

        @pl.when(g > 0)
        def _():
            db_ref[...] += db

    rev = pl.BlockSpec((GROUP, BLK), lambda g: (nb - 1 - g, 0))
    row = pl.BlockSpec((1, BLK), lambda g: (0, 0))
    return pl.pallas_call(
        body, name=name,
        out_shape=(jax.ShapeDtypeStruct((S, BLK), F32), jax.ShapeDtypeStruct((1, BLK), F32)), grid=(nb,),
        in_specs=[rev, rev, rev, row], out_specs=(rev, row),
        scratch_shapes=[pltpu.VMEM((8, BLK), F32)],
        compiler_params=_params(),
    )(dcum_q, dcum_k, fb, b_forget)


def _skew(x, sign):
    row = lax.broadcasted_iota(jnp.int32, x.shape, 0)
    for b in range(7):
        amount = (1 << b) if sign > 0 else 256 - (1 << b)
        x = jnp.where(((row >> b) & 1) == 1, pltpu.roll(x, amount, 1), x)
    return x


def _rel_bases(rel):
    far = rel[:, 256:257]
    far127 = jnp.broadcast_to(far, (rel.shape[0], 127))
    base0 = jnp.concatenate([rel[:, 128:0:-1], far, rel[:, 255:128:-1]], axis=1)
    base1 = jnp.concatenate([rel[:, 256:128:-1], far, far127], axis=1)
    base0_t = jnp.concatenate([rel[:, 128:256], far, rel[:, 1:128]], axis=1)
    base1_t = jnp.concatenate([jnp.broadcast_to(far, (rel.shape[0], 128)), far, rel[:, 129:256]], axis=1)
    return jnp.stack([base0, base1, base0_t, base1_t], axis=1)


def _rel_expand(bases, name):
    def body(b_ref, t_ref, tt_ref):
        far = jnp.broadcast_to(b_ref[1:2, 0:1], (BLK, BLK))
        for k, out_ref in ((0, t_ref), (2, tt_ref)):
            for d in (0, 1):
                x = jnp.broadcast_to(b_ref[k + d:k + d + 1, :], (BLK, 2 * BLK))
                out_ref[d] = _skew(x, 1)[:, :BLK]
            for d in (2, 3, 4):
                out_ref[d] = far

    out = jax.ShapeDtypeStruct((8, 5, BLK, BLK), F32)
    spec = pl.BlockSpec((None, 5, BLK, BLK), lambda h: (h, 0, 0, 0))
    return pl.pallas_call(
        body, name=name, out_shape=(out, out), grid=(8,),
        in_specs=[pl.BlockSpec((None, 4, 2 * BLK), lambda h: (h, 0, 0))], out_specs=(spec, spec),
        compiler_params=_params(),
    )(bases)


def _rel_reduce(dtiles_t, name):
    def body(dt_ref, o_ref):
        zeros = jnp.zeros((BLK, BLK), F32)
        sums = []
        for d in (0, 1):
            x = _skew(jnp.concatenate([dt_ref[d], zeros], axis=1), -1)
            sums.append(jnp.broadcast_to(jnp.sum(x, axis=0, keepdims=True), (8, 2 * BLK)))
        lane = lax.broadcasted_iota(jnp.int32, (8, 2 * BLK), 1)
        main = pltpu.roll(sums[0], BLK, 1) + jnp.where(lane > BLK, sums[1], 0.0)
        far = jnp.sum(jnp.where(lane < BLK, sums[1], 0.0)[0:1], axis=1, keepdims=True)
        far = far + jnp.sum(jnp.sum(dt_ref[2] + dt_ref[3] + dt_ref[4], axis=0, keepdims=True), axis=1, keepdims=True)
        o_ref[...] = jnp.concatenate([main[0:1], jnp.broadcast_to(far, (1, BLK))], axis=1)

    return pl.pallas_call(
        body, name=name, out_shape=jax.ShapeDtypeStruct((8, 1, 3 * BLK), F32), grid=(8,),
        in_specs=[pl.BlockSpec((None, 5, BLK, BLK), lambda h: (h, 0, 0, 0))],
        out_specs=pl.BlockSpec((None, 1, 3 * BLK), lambda h: (h, 0, 0)),
        compiler_params=_params(),
    )(dtiles_t)


def _final_loss(x, target, g, below, name):
    S, D = x.shape
    ts = _row_tile(S, ROW_TILE)

    def body(x_ref, t_ref, g_ref, f_ref, gate_ref, dx_ref, loss_ref, dg_ref, dgate_ref, df_ref):
        i = pl.program_id(0)
        xv, gv = x_ref[...], g_ref[...]
        rstd = lax.rsqrt(jnp.mean(xv * xv, axis=-1, keepdims=True) + EPS)
        xhat = xv * rstd
        err = xhat * gv - t_ref[...]
        part = 0.5 * jnp.sum(jnp.mean(err * err, axis=-1, keepdims=True), axis=0, keepdims=True)
        dy = err / D
        dg = jnp.sum(dy * xhat, axis=0, keepdims=True)
        dxhat = dy * gv
        proj = jnp.mean(dxhat * xhat, axis=-1, keepdims=True)
        dx = rstd * (dxhat - xhat * proj)
        dx_ref[...] = dx
        dgate = _gated_residual_bwd(dx, f_ref, gate_ref, df_ref)
        _accumulate_rows(i, [(loss_ref, jnp.broadcast_to(part, loss_ref.shape)), (dg_ref, dg), (dgate_ref, dgate)])

    tile = pl.BlockSpec((ts, D), lambda i: (i, 0))
    row = pl.BlockSpec((1, D), lambda i: (0, 0))
    vec = jax.ShapeDtypeStruct((1, D), F32)
    return pl.pallas_call(
        body, name=name,
        out_shape=(jax.ShapeDtypeStruct((S, D), F32), jax.ShapeDtypeStruct((8, 128), F32), vec, vec,
                   jax.ShapeDtypeStruct((S, D), BF16)),
        grid=(S // ts,), in_specs=[tile, tile, row, tile, row],
        out_specs=(tile, pl.BlockSpec((8, 128), lambda i: (0, 0)), row, row, tile),
        compiler_params=_params(),
    )(x, target, g, *below)


def _ada_fwd(c_all, w_ada, name):
    L, D, E = w_ada.shape

    def body(c_ref, w_ref, o_ref):
        cv = c_ref[...]
        cond = cv * jax.nn.sigmoid(cv)
        o_ref[...] = jnp.dot(cond, w_ref[...], preferred_element_type=F32, precision=HIGHEST)

    return pl.pallas_call(
        body, name=name, out_shape=jax.ShapeDtypeStruct((L, N_DEV, E), F32), grid=(L,),
        in_specs=[pl.BlockSpec((N_DEV, D), lambda l: (0, 0)), pl.BlockSpec((None, D, E), lambda l: (l, 0, 0))],
        out_specs=pl.BlockSpec((None, N_DEV, E), lambda l: (l, 0, 0)),
        compiler_params=_params(),
    )(c_all, w_ada)


def _ada_bwd(c_all_t, dmod, name):
    D = c_all_t.shape[0]
    L, _, E = dmod.shape

    def body(c_ref, d_ref, o_ref):
        cv = c_ref[...]
        cond = cv * jax.nn.sigmoid(cv)
        acc = None
        for b in range(N_DEV):
            t = cond[:, b:b + 1] * d_ref[b:b + 1, :]
            acc = t if acc is None else acc + t
        o_ref[...] = acc

    return pl.pallas_call(
        body, name=name, out_shape=jax.ShapeDtypeStruct((L, D, E), F32), grid=(L,),
        in_specs=[pl.BlockSpec((D, N_DEV), lambda l: (0, 0)), pl.BlockSpec((None, N_DEV, E), lambda l: (l, 0, 0))],
        out_specs=pl.BlockSpec((None, D, E), lambda l: (l, 0, 0)),
        compiler_params=_params(),
    )(c_all_t, dmod)


def _adamw(w, m, v, g_parts, name, me, after=None):
    L, R, C = w.shape
    tr = _row_tile(R, max(8, (256 * 1024 // max(C, 128)) // 8 * 8))
    nr = R // tr
    c1 = 1.0 - ADAM_B1 ** ADAM_STEP
    c2 = 1.0 - ADAM_B2 ** ADAM_STEP
    direct = [isinstance(p, tuple) for p in g_parts]
    n_in = sum(2 if d else 1 for d in direct)

    def body(me_ref, w_ref, m_ref, v_ref, *rest):
        g_refs, (go_ref, d_ref, mo_ref, vo_ref) = list(rest[:n_in]), rest[-4:]
        layer = pl.program_id(0)
        g = None
        for l in range(L):
            land_ref = g_refs.pop(0)
            own = g_refs.pop(0)[...].astype(F32) if direct[l] else None
            gl = None
            for k in range(land_ref.shape[0]):
                part = land_ref[k].astype(F32)
                if direct[l]:
                    part = jnp.where(me_ref[l] == k, own, part)
                gl = part if gl is None else gl + part
            g = gl if g is None else jnp.where(layer == l, gl, g)
        mn = ADAM_B1 * m_ref[...] + (1.0 - ADAM_B1) * g
        vn = ADAM_B2 * v_ref[...] + (1.0 - ADAM_B2) * (g * g)
        m_hat = mn / c1
        v_hat = vn / c2
        go_ref[...] = g
        d_ref[...] = -ADAM_LR * (m_hat / (jnp.sqrt(v_hat) + ADAM_EPS) + ADAM_WD * w_ref[...])
        mo_ref[...] = mn
        vo_ref[...] = vn

    def rows(l, layer, i):
        return jnp.where(layer == l, i, 0 if l > 0 else nr - 1)

    in_specs, operands = [], []
    for l, p in enumerate(g_parts):
        land, sent = p if direct[l] else (p, None)
        in_specs.append(pl.BlockSpec((land.shape[0], tr, C), lambda layer, i, me_ref, l=l: (0, rows(l, layer, i), 0)))
        operands.append(land)
        if direct[l]:
            in_specs.append(pl.BlockSpec((None, tr, C), lambda layer, i, me_ref, l=l: (me_ref[l], rows(l, layer, i), 0)))
            operands.append(sent)
    if after is not None:
        in_specs.append(ANY)
        operands.append(after)
    tile = pl.BlockSpec((None, tr, C), lambda layer, i, me_ref: (layer, i, 0))
    out = jax.ShapeDtypeStruct((L, R, C), F32)
    return pl.pallas_call(
        body, name=name, out_shape=(out, out, out, out),
        grid_spec=pltpu.PrefetchScalarGridSpec(
            num_scalar_prefetch=1, grid=(L, nr), in_specs=[tile, tile, tile] + in_specs,
            out_specs=(tile, tile, tile, tile)),
        compiler_params=_params(),
    )(me, w, m, v, *operands)


def _pair_add(pieces, recv, core, name):
    _, _, R, C = pieces.shape
    tr = _row_tile(R, max(8, (512 * 1024 // max(C, 128)) // 8 * 8))

    def body(core_ref, a_ref, b_ref, o_ref):
        o_ref[...] = (a_ref[...].astype(F32) + b_ref[...].astype(F32)).astype(BF16)

    return pl.pallas_call(
        body, name=name, out_shape=jax.ShapeDtypeStruct((4, R, C), BF16),
        grid_spec=pltpu.PrefetchScalarGridSpec(
            num_scalar_prefetch=1, grid=(4, R // tr),
            in_specs=[pl.BlockSpec((None, None, tr, C), lambda k, i, core_ref: (core_ref[0], k, i, 0)),
                      pl.BlockSpec((None, tr, C), lambda k, i, core_ref: (k, i, 0))],
            out_specs=pl.BlockSpec((None, tr, C), lambda k, i, core_ref: (k, i, 0))),
        compiler_params=_params(),
    )(core, pieces, recv)


MESH = pl.DeviceIdType.MESH
ANY = pl.BlockSpec(memory_space=pl.ANY)


def _position():
    return lax.axis_index("x"), lax.axis_index("y"), lax.axis_index("c")


def _small_all_gather(v, name):
    m_per, n = v.shape

    def body(x_ref, out_ref, send_sems, recv_sems, local_sem):
        x, y, c = _position()
        me, sibling = (x, y, c), (x, y, 1 - c)
        chips = [(1 - x, y), (x, 1 - y), (1 - x, 1 - y)]

        def rows(px, py, pc):
            return out_ref.at[pl.ds((4 * px + 2 * py + pc) * m_per, m_per), :]

        def copy(k, block, to, src=None):
            return pltpu.make_async_remote_copy(
                src_ref=rows(*block) if src is None else src, dst_ref=rows(*block),
                send_sem=send_sems.at[k], recv_sem=recv_sems.at[k], device_id=to, device_id_type=MESH)

        mine = pltpu.make_async_copy(x_ref, rows(*me), local_sem)
        mine.start()
        first = [copy(0, me, sibling, src=x_ref)]
        first += [copy(1 + j, me, (*chip, c), src=x_ref) for j, chip in enumerate(chips)]
        for cp in first:
            cp.start()
        passed = [copy(4 + j, (*chip, c), sibling) for j, chip in enumerate(chips)]
        for j, chip in enumerate(chips):
            copy(1 + j, (*chip, c), me).wait_recv()
            passed[j].start()
        copy(0, sibling, me).wait_recv()
        for j, chip in enumerate(chips):
            copy(4 + j, (*chip, 1 - c), me).wait_recv()
        for cp in first + passed:
            cp.wait_send()
        mine.wait()

    return pl.pallas_call(
        body, name=name, out_shape=jax.ShapeDtypeStruct((N_DEV * m_per, n), v.dtype),
        in_specs=[pl.BlockSpec(memory_space=pltpu.VMEM)], out_specs=pl.BlockSpec(memory_space=pltpu.VMEM),
        scratch_shapes=[pltpu.SemaphoreType.DMA((7,)), pltpu.SemaphoreType.DMA((7,)), pltpu.SemaphoreType.DMA],
    )(v)


def _sibling_exchange(pieces, name):
    n_arr = len(pieces)

    def body(*refs):
        p_refs, out_refs = refs[:n_arr], refs[n_arr:2 * n_arr]
        send_sems, recv_sems = refs[2 * n_arr:]
        x, y, c = _position()
        copies = [pltpu.make_async_remote_copy(
            src_ref=p_refs[a].at[1 - c], dst_ref=out_refs[a], send_sem=send_sems.at[a], recv_sem=recv_sems.at[a],
            device_id=(x, y, 1 - c), device_id_type=MESH) for a in range(n_arr)]
        for cp in copies:
            cp.start()
        for cp in copies:
            cp.wait()

    return pl.pallas_call(
        body, name=name,
        out_shape=tuple(jax.ShapeDtypeStruct(p.shape[1:], p.dtype) for p in pieces),
        in_specs=[ANY] * n_arr, out_specs=tuple([ANY] * n_arr),
        scratch_shapes=[pltpu.SemaphoreType.DMA((n_arr,)), pltpu.SemaphoreType.DMA((n_arr,))],
    )(*pieces)


HBM = pl.BlockSpec(memory_space=pltpu.HBM)
SEM = pl.BlockSpec(memory_space=pltpu.SEMAPHORE)
EFFECT = pltpu.SideEffectType.DATAFLOW_SIDE_EFFECTING
RELATIONS = [(rx, ry, rc) for rx in (0, 1) for ry in (0, 1) for rc in (0, 1)][1:]


SAME_CORE = [r for r in RELATIONS if r == (0, 0, 1) or r[2] == 0]


CHIPS = [r for r in RELATIONS if r[2] == 0]


def _exchange_copies(src_refs, land_refs, send_sems, recv_sems, scatter, receive_side, relations):
    x, y, c = _position()
    index = (lambda px, py, pc: 2 * px + py) if relations == CHIPS else (lambda px, py, pc: 4 * px + 2 * py + pc)
    me = index(x, y, c)
    copies = []
    for k, (rx, ry, rc) in enumerate(relations):
        peer = ((1 - x) if rx else x, (1 - y) if ry else y, (1 - c) if rc else c)
        peer_index = index(*peer)
        for a, (src, land) in enumerate(zip(src_refs, land_refs)):
            copies.append(pltpu.make_async_remote_copy(
                src_ref=src.at[peer_index] if scatter else src,
                dst_ref=land.at[peer_index if receive_side else me],
                send_sem=send_sems.at[a * len(relations) + k], recv_sem=recv_sems.at[a * len(relations) + k],
                device_id=peer, device_id_type=MESH))
    return copies


def _exchange_start(srcs, scatter, after, name, relations=RELATIONS):
    n = len(srcs)
    land_shapes = [(s.shape if scatter else (N_DEV,) + s.shape) for s in srcs]

    def body(*refs):
        src_refs, land_refs = refs[:n], refs[n:2 * n]
        send_sems, recv_sems = refs[2 * n + 1], refs[2 * n + 2]
        token = refs[-1]
        for cp in _exchange_copies(src_refs, land_refs, send_sems, recv_sems, scatter, False, relations):
            cp.start()
        token[...] = jnp.zeros_like(token)

    sems = pltpu.SemaphoreType.DMA((n * len(relations),))
    outs = pl.pallas_call(
        body, name=name,
        out_shape=(sems, sems, *[pltpu.HBM(s.shape, s.dtype) for s in srcs],
                   *[pltpu.HBM(shape, s.dtype) for shape, s in zip(land_shapes, srcs)],
                   jax.ShapeDtypeStruct((8, 128), F32)),
        in_specs=[HBM] * (2 * n) + [ANY],
        out_specs=(SEM, SEM, *[HBM] * (2 * n), pl.BlockSpec(memory_space=pltpu.VMEM)),
        input_output_aliases={a: 2 + a for a in range(2 * n)},
        compiler_params=pltpu.CompilerParams(has_side_effects=EFFECT),
    )(*[pltpu.with_memory_space_constraint(s, pltpu.HBM) for s in srcs],
      *[pltpu.with_memory_space_constraint(lax.empty(shape, s.dtype), pltpu.HBM)
        for shape, s in zip(land_shapes, srcs)], after)
    return outs[0], outs[1], outs[2:2 + n], outs[2 + n:2 + 2 * n], outs[-1]


def _exchange_wait(started, scatter, after, name, relations=RELATIONS):
    send_sems, recv_sems, srcs, lands, _ = started
    n = len(srcs)

    def body(*refs):
        src_refs, land_refs = refs[:n], refs[n:2 * n]
        send_sems, recv_sems = refs[2 * n], refs[2 * n + 1]
        copies = _exchange_copies(src_refs, land_refs, send_sems, recv_sems, scatter, True, relations)
        for cp in copies:
            cp.wait_send()
        for cp in copies:
            cp.wait_recv()

    outs = pl.pallas_call(
        body, name=name,
        out_shape=(*[pltpu.HBM(s.shape, s.dtype) for s in srcs], *[pltpu.HBM(t.shape, t.dtype) for t in lands]),
        in_specs=[HBM] * (2 * n) + [SEM, SEM, ANY], out_specs=tuple([HBM] * (2 * n)),
        input_output_aliases={a: a for a in range(2 * n)},
        compiler_params=pltpu.CompilerParams(has_side_effects=EFFECT),
    )(*srcs, *lands, send_sems, recv_sems, after)
    return outs[:n], outs[n:]


def _forward_copies(land_refs, send_sems, recv_sems, receive_side):
    x, y, c = _position()
    copies = []
    for j, (px, py) in enumerate([(1 - x, y), (x, 1 - y), (1 - x, 1 - y)]):
        held, coming = 4 * px + 2 * py + c, 4 * px + 2 * py + (1 - c)
        for a, land in enumerate(land_refs):
            copies.append(pltpu.make_async_remote_copy(
                src_ref=land.at[held], dst_ref=land.at[coming if receive_side else held],
                send_sem=send_sems.at[3 * a + j], recv_sem=recv_sems.at[3 * a + j],
                device_id=(x, y, 1 - c), device_id_type=MESH))
    return copies


def _forward_start(lands, after, name):
    n = len(lands)

    def body(*refs):
        send_sems, recv_sems, token = refs[n + 1], refs[n + 2], refs[-1]
        for cp in _forward_copies(refs[:n], send_sems, recv_sems, False):
            cp.start()
        token[...] = jnp.zeros_like(token)

    sems = pltpu.SemaphoreType.DMA((3 * n,))
    outs = pl.pallas_call(
        body, name=name,
        out_shape=(sems, sems, *[pltpu.HBM(t.shape, t.dtype) for t in lands], jax.ShapeDtypeStruct((8, 128), F32)),
        in_specs=[HBM] * n + [ANY], out_specs=(SEM, SEM, *[HBM] * n, pl.BlockSpec(memory_space=pltpu.VMEM)),
        input_output_aliases={a: 2 + a for a in range(n)},
        compiler_params=pltpu.CompilerParams(has_side_effects=EFFECT),
    )(*lands, after)
    return outs[0], outs[1], outs[2:2 + n], outs[-1]


def _forward_wait(started, after, name):
    send_sems, recv_sems, lands, _ = started
    n = len(lands)

    def body(*refs):
        copies = _forward_copies(refs[:n], refs[n], refs[n + 1], True)
        for cp in copies:
            cp.wait_send()
        for cp in copies:
            cp.wait_recv()

    return pl.pallas_call(
        body, name=name, out_shape=tuple(pltpu.HBM(t.shape, t.dtype) for t in lands),
        in_specs=[HBM] * n + [SEM, SEM, ANY], out_specs=tuple([HBM] * n),
        input_output_aliases={a: a for a in range(n)},
        compiler_params=pltpu.CompilerParams(has_side_effects=EFFECT),
    )(*lands, send_sems, recv_sems, after)


def _place_own(lands, mine, me, name):
    n = len(lands)
    flat = [m.reshape(-1, m.shape[-1]) for m in mine]
    flat_lands = [t.reshape(N_DEV, -1, t.shape[-1]) for t in lands]

    def body(me_ref, *refs):
        for src, dst in zip(refs[:n], refs[2 * n:]):
            dst[...] = src[...]

    in_specs = [pl.BlockSpec((m.shape[0] // 2, m.shape[1]), lambda i, me_ref: (i, 0)) for m in flat]
    out_specs = [pl.BlockSpec((None, m.shape[0] // 2, m.shape[1]), lambda i, me_ref: (me_ref[0], i, 0)) for m in flat]
    outs = pl.pallas_call(
        body, name=name, out_shape=tuple(jax.ShapeDtypeStruct(t.shape, t.dtype) for t in flat_lands),
        grid_spec=pltpu.PrefetchScalarGridSpec(
            num_scalar_prefetch=1, grid=(2,), in_specs=in_specs + [ANY] * n, out_specs=tuple(out_specs)),
        input_output_aliases={1 + n + a: a for a in range(n)},
        compiler_params=_params(),
    )(me, *flat, *flat_lands)
    return [o.reshape(t.shape) for o, t in zip(outs, lands)]


W_IN_SHARD = N_IN // N_DEV
F_SHARD = F_COL // W_IN_SHARD
F_LO = F_COL - F_SHARD * W_IN_SHARD


def _w_ffn_in_view(w):
    return jnp.transpose(w, (0, 2, 1))


def _w_in_segments():
    segments = []
    for d in range(N_DEV):
        if d == F_SHARD:
            segments += [(d, 0, d * W_IN_SHARD, F_LO), (d, F_LO, N_MAIN, N_FORGET),
                         (d, F_LO + N_FORGET, F_COL, W_IN_SHARD - F_LO - N_FORGET)]
        else:
            segments.append((d, 0, d * W_IN_SHARD - (N_FORGET if d > F_SHARD else 0), W_IN_SHARD))
    return segments


def _w_in_rearranged(g, name):
    D = g.shape[1]
    tr = _row_tile(D, 256)

    def body(g_ref, o_ref):
        for d, lo, at, width in _w_in_segments():
            o_ref[:, at:at + width] = g_ref[d, :, lo:lo + width]
        o_ref[:, N_IN:] = jnp.zeros((tr, BLK - N_FORGET), o_ref.dtype)

    return pl.pallas_call(
        body, name=name, out_shape=jax.ShapeDtypeStruct((D, N_MAIN + BLK), g.dtype), grid=(D // tr,),
        in_specs=[pl.BlockSpec((N_DEV, tr, W_IN_SHARD), lambda i: (0, i, 0))],
        out_specs=pl.BlockSpec((tr, N_MAIN + BLK), lambda i: (i, 0)),
        compiler_params=_params(),
    )(g)


def _w_in_pieces(dw_r, name, pair_major=False):
    D = dw_r.shape[0]
    tr = _row_tile(D, 256)
    lead = (2, 4) if pair_major else (N_DEV,)

    def body(x_ref, o_ref):
        for d, lo, at, width in _w_in_segments():
            slot = (d % 2, d // 2) if pair_major else (d,)
            o_ref[(*slot, slice(None), slice(lo, lo + width))] = x_ref[:, at:at + width]

    return pl.pallas_call(
        body, name=name, out_shape=jax.ShapeDtypeStruct((*lead, D, W_IN_SHARD), dw_r.dtype), grid=(D // tr,),
        in_specs=[pl.BlockSpec((tr, N_MAIN + BLK), lambda i: (i, 0))],
        out_specs=pl.BlockSpec((*lead, tr, W_IN_SHARD), lambda i: (*[0] * len(lead), i, 0)),
        compiler_params=_params(),
    )(dw_r)


def _row_pieces(dw):
    return dw.reshape(N_DEV, dw.shape[0] // N_DEV, dw.shape[1])


def _branch_pieces(dw):
    k, w, d = dw.shape
    return jnp.transpose(dw.reshape(k, w, N_DEV, d // N_DEV), (2, 0, 1, 3)).reshape(N_DEV, k * w, d // N_DEV)


def _pairs_col(a):
    return jnp.transpose(a.reshape(a.shape[0], 4, 2), (1, 0, 2))


def _pairs_row(a):
    return jnp.transpose(a.reshape(a.shape[0], 4, 2), (1, 2, 0))


def _heads_from_col(a):
    return jnp.transpose(a, (1, 0, 2)).reshape(a.shape[1], 8)


def _heads_from_row(a):
    return jnp.transpose(a, (2, 0, 1)).reshape(a.shape[2], 8)


def _pad_lanes(a, n):
    return jnp.pad(a, [(0, 0)] * (a.ndim - 1) + [(0, n - a.shape[-1])])


SMALL_SEGMENTS = (("dmod", 2 * 6 * D_MODEL), ("norm_mix_g", 2 * D_MODEL), ("norm_ffn_g", 2 * D_MODEL),
                  ("final_norm_g", D_MODEL), ("b_forget", 128), ("sinks", 128), ("rel_bias", 4224), ("loss", 128))
SMALL_ROWS = 176


def _pack_small(parts):
    flat = [_pad_lanes(parts[name].reshape(1, -1), size) for name, size in SMALL_SEGMENTS]
    total = sum(size for _, size in SMALL_SEGMENTS)
    flat.append(jnp.zeros((1, SMALL_ROWS * 128 - total), F32))
    return jnp.concatenate(flat, axis=1).reshape(SMALL_ROWS, 128)


def _unpack_small(packed, shapes):
    flat = packed.reshape(-1)
    out, pos = {}, 0
    for name, size in SMALL_SEGMENTS:
        shape = shapes[name]
        count = 1
        for d in shape:
            count *= d
        out[name] = flat[pos:pos + count].reshape(shape)
        pos += size
    return out


def kernel(x, c, norm_mix_g, norm_ffn_g, w_ada, b_ada, w_in, b_forget, sinks, rel_bias, w_branch, w_out, w_ffn_in, w_ffn_out, final_norm_g, loss_target, m_norm_mix_g, m_norm_ffn_g, m_w_ada, m_b_ada, m_w_in, m_b_forget, m_sinks, m_rel_bias, m_w_branch, m_w_out, m_w_ffn_in, m_w_ffn_out, m_final_norm_g, v_norm_mix_g, v_norm_ffn_g, v_w_ada, v_b_ada, v_w_in, v_b_forget, v_sinks, v_rel_bias, v_w_branch, v_w_out, v_w_ffn_in, v_w_ffn_out, v_final_norm_g):
    depth = w_in.shape[0]
    S, D = x.shape[1], x.shape[2]
    assert S % GROUP == 0 and S >= ATTN_WINDOW["c"] * BLK
    px, py, pc = _position()
    me = 4 * px + 2 * py + pc
    x0 = x[0]

    assert depth == 2
    big_weights = (w_in, w_branch, w_out, w_ffn_in, w_ffn_out)
    me_arr = jnp.stack([me, me]).astype(jnp.int32)
    me_in_arr = jnp.stack([2 * px + py, me]).astype(jnp.int32)

    def rest_matrices(g_branch, g_out, g_fin, g_fout):
        return (jnp.transpose(g_branch, (1, 2, 0, 3)).reshape(3, 512, D), g_out.reshape(D, D),
                g_fin.reshape(2 * FFN_HIDDEN, D), g_fout.reshape(FFN_HIDDEN, D))

    def arrive(started, after, name):
        mine, landed = _exchange_wait(started, False, after, f"{name}_wait", SAME_CORE)
        return mine, _forward_start(landed, mine[0], f"{name}_forward_start")

    def finish_gather(arrived, after, name):
        mine, forward = arrived
        landed = _forward_wait(forward, after, f"{name}_forward_wait")
        return _place_own(landed, mine, me.astype(jnp.int32).reshape(1), f"{name}_own")

    w_fin_t = _w_ffn_in_view(w_ffn_in)
    shards = [[t.astype(BF16) for t in (w_in[l], w_branch[l], w_out[l], w_fin_t[l], w_ffn_out[l])]
              for l in range(depth)]
    c_all = _small_all_gather(c.reshape(8, 128), "comm_gather_c").reshape(N_DEV, D)
    mod_cols = _ada_fwd(c_all, w_ada, "ada_fwd")
    mod_all = _small_all_gather(mod_cols.reshape(-1, 128), "comm_gather_mod")
    gather_in0 = _exchange_start(shards[0][:1], False, mod_all, "comm_gather_w_in0_start", SAME_CORE)
    gather_rest0 = _exchange_start(shards[0][1:], False, gather_in0[4], "comm_gather_rest0_start", SAME_CORE)
    gather1 = _exchange_start(shards[1], False, gather_rest0[4], "comm_gather_weights1_start", SAME_CORE)
    started = gather1[4][0:1, 0:1]
    W_in, W_branch, W_out, W_fin, W_fout = ([None, None] for _ in range(5))
    mod_all = mod_all.reshape(N_DEV, depth, N_DEV, w_ada.shape[2])
    mod_mine = lax.dynamic_index_in_dim(mod_all, me, axis=2, keepdims=False)
    mod = jnp.transpose(mod_mine, (1, 0, 2)).reshape(depth, 6 * D) + b_ada + started
    mods = [[mod[l:l + 1, k * D:(k + 1) * D] for k in range(6)] for l in range(depth)]
    rel_tiles = [_rel_expand(_rel_bases(rel_bias[l]) + started, f"rel_expand{l}") for l in range(depth)]

    slopes = jnp.exp2(-jnp.arange(1, 9, dtype=F32))
    saved = []
    xs = x0
    for l in range(depth):
        if l == 1:
            g_in1, *g_rest1 = finish_gather(arrived1, xs, "comm_gather_weights1")
            W_in[1] = _w_in_rearranged(g_in1, "w_in_rearrange1")
            W_branch[1], W_out[1], W_fin[1], W_fout[1] = rest_matrices(*g_rest1)
        sh_m, sc_m, g_m, sh_f, sc_f, g_f = mods[l]
        gm, gf = norm_mix_g[l:l + 1], norm_ffn_g[l:l + 1]
        bfor = _pad_lanes(b_forget[l:l + 1], BLK)
        h = _norm_mod_fwd(xs, gm, sh_m, sc_m, f"norm_mix_fwd{l}")
        tiles, tiles_t = rel_tiles[l]
        if l == 0:
            arrived_in0 = arrive(gather_in0, rel_tiles[-1][1], "comm_gather_w_in0")
            W_in[0] = _w_in_rearranged(finish_gather(arrived_in0, h, "comm_gather_w_in0")[0], "w_in_rearrange0")
        qkv, gates = _project(h, W_in[l], f"proj{l}")
        fb = _matmul(h, W_in[l], "nn", F32, f"proj_forget{l}", TILES["proj_forget"], n=BLK, b_off=N_MAIN // BLK)
        cum = _forget_fwd(fb, bfor, f"forget_fwd{l}")[:, :N_FORGET]
        cum_col, cum_row = _pairs_col(cum), _pairs_row(cum)
        o_a, lse_a = _attn_fwd("a", qkv, f"attn_a_fwd{l}", sinks=sinks[l], slopes=slopes)
        o_b, lse_b = _attn_fwd("b", qkv, f"attn_b_fwd{l}", cq_col=cum_col, ck_row=cum_row)
        arrived_rest0 = arrive(gather_rest0, o_b, "comm_gather_rest0") if l == 0 else None
        o_c, lse_c = _attn_fwd("c", qkv, f"attn_c_fwd{l}", bias=tiles, after=arrived_rest0[1][3] if l == 0 else None)
        if l == 0:
            W_branch[0], W_out[0], W_fin[0], W_fout[0] = rest_matrices(
                *finish_gather(arrived_rest0, o_c, "comm_gather_rest0"))
        x1, merged, mix = _merge_fwd(o_a, o_b, o_c, gates, W_branch[l], W_out[l], xs, g_m, f"merge_fwd{l}")
        h2 = _norm_mod_fwd(x1, gf, sh_f, sc_f, f"norm_ffn_fwd{l}")
        act = _ffn_in_fwd(h2, W_fin[l], f"ffn_in_fwd{l}")
        if l == 0:
            arrived1 = arrive(gather1, act, "comm_gather_weights1")
        x2, ffn = _matmul_resid(act, W_fout[l], x1, g_f, f"ffn_out{l}", TILES["ffn_out"],
                                after=arrived1[1][3] if l == 0 else None)
        saved.append(dict(x=xs, h=h, qkv=qkv, gates=gates, fb=fb, bfor=bfor, cum_col=cum_col, cum_row=cum_row,
                          tiles_t=tiles_t, o=(o_a, o_b, o_c), lse=(lse_a, lse_b, lse_c), merged=merged, mix=mix,
                          x1=x1, h2=h2, act=act, ffn=ffn))
        xs = x2

    dx, loss_tile, d_final_g, d_g_f, df = _final_loss(
        xs, loss_target[0], final_norm_g.reshape(1, D), (saved[-1]["ffn"], mods[-1][5]), "final_loss")

    grads = {k: [None] * depth for k in ("w_in", "w_branch", "w_out", "w_ffn_in", "w_ffn_out", "norm_mix_g",
                                          "norm_ffn_g", "b_forget", "sinks", "rel_bias", "dmod")}
    def rest_pieces(l):
        return [_branch_pieces(grads["w_branch"][l]), _row_pieces(grads["w_out"][l]),
                _row_pieces(grads["w_ffn_in"][l]), _row_pieces(grads["w_ffn_out"][l])]

    reduce1 = reduce_rest0 = reduce_in0 = None
    for l in reversed(range(depth)):
        sv = saved[l]
        sh_m, sc_m, g_m, sh_f, sc_f, g_f = mods[l]
        gm, gf = norm_mix_g[l:l + 1], norm_ffn_g[l:l + 1]
        du_g, du_u = _ffn_mid_bwd(sv["h2"], df, W_fin[l], W_fout[l], f"ffn_mid_bwd{l}")
        du = jnp.concatenate([du_g, du_u], axis=1)
        grads["w_ffn_out"][l] = _matmul(sv["act"], df, "tn", BF16, f"wgrad_ffn_out{l}", TILES["wgrad_ffn_out"])
        grads["w_ffn_in"][l] = _matmul(du, sv["h2"], "tn", BF16, f"wgrad_ffn_in{l}", TILES["wgrad_ffn_in"])
        dh2 = _matmul(du, W_fin[l], "nn", F32, f"dgrad_ffn_in{l}", TILES["dgrad_ffn_in"])
        dx1, d_sh_f, d_sc_f, d_gf, d_g_m, dmix = _norm_mod_bwd(sv["x1"], dh2, dx, gf, sc_f, f"norm_ffn_bwd{l}",
                                                               below=(sv["mix"], g_m))
        grads["w_out"][l] = _matmul(sv["merged"], dmix, "tn", BF16, f"wgrad_out{l}", TILES["wgrad_out"])
        o_a, o_b, o_c = sv["o"]
        dgates, d_w_branch, do_a, do_b, do_c, dl_a, dl_b, dl_c = _merge_bwd(
            dmix, o_a, o_b, o_c, sv["gates"], W_branch[l], W_out[l], f"merge_bwd{l}")
        grads["w_branch"][l] = d_w_branch.astype(BF16)
        lse_rows = [_pairs_row(_heads_from_col(t)) for t in sv["lse"]]
        if l == 0:
            reduce_rest0 = _exchange_start(rest_pieces(0), True, dgates, "comm_reduce_rest0_start")
            lse_rows = [t + reduce_rest0[4][0:1, 0:1] for t in lse_rows]
        dq_a, dk_a, dv_a, dsink = _attn_bwd("a", sv["qkv"], do_a, lse_rows[0], _pairs_row(dl_a), f"attn_a_bwd{l}",
                                            sinks=sinks[l], slopes=slopes)
        dq_b, dk_b, dv_b, dck, dcq = _attn_bwd("b", sv["qkv"], do_b, lse_rows[1], _pairs_row(dl_b),
                                               f"attn_b_bwd{l}", cq_row=sv["cum_row"], ck_col=sv["cum_col"])
        dq_c, dk_c, dv_c, dtiles_t = _attn_bwd("c", sv["qkv"], do_c, lse_rows[2], _pairs_row(dl_c),
                                               f"attn_c_bwd{l}", bias_t=sv["tiles_t"])
        grads["sinks"][l] = dsink[:, :2, 0].reshape(8)
        grads["rel_bias"][l] = _rel_reduce(dtiles_t, f"rel_reduce{l}")[:, 0, :N_REL]
        dcum_k = _pad_lanes(_heads_from_col(dck), BLK)
        dcum_q = _pad_lanes(_heads_from_row(dcq), BLK)
        dfb, d_bfor = _forget_bwd(dcum_q, dcum_k, sv["fb"], sv["bfor"], f"forget_bwd{l}")
        grads["b_forget"][l] = d_bfor[0, :N_FORGET]
        dproj = jnp.concatenate(
            [t.astype(BF16) for t in (dq_a, dk_a, dv_a, dq_b, dk_b, dv_b, dq_c, dk_c, dv_c, dgates, dfb)],
            axis=1)
        grads["w_in"][l] = _matmul(sv["h"], dproj, "tn", BF16, f"wgrad_in{l}", TILES["wgrad_in"])
        if l == 1:
            reduce1 = _exchange_start([_w_in_pieces(grads["w_in"][1], "w_in_pieces1")] + rest_pieces(1), True, dproj,
                                      "comm_reduce1_start")
        dh = _matmul(dproj, W_in[l], "nt", F32, f"dgrad_in{l}", TILES["dgrad_in"], after=reduce1[4] if l == 1 else None)
        d_g_f_here = d_g_f
        if l > 0:
            dx, d_sh_m, d_sc_m, d_gm, d_g_f, df = _norm_mod_bwd(sv["x"], dh, dx1, gm, sc_m, f"norm_mix_bwd{l}",
                                                                below=(saved[l - 1]["ffn"], mods[l - 1][5]))
        else:
            dx, d_sh_m, d_sc_m, d_gm = _norm_mod_bwd(sv["x"], dh, dx1, gm, sc_m, f"norm_mix_bwd{l}")
        grads["norm_mix_g"][l] = d_gm[0]
        grads["norm_ffn_g"][l] = d_gf[0]
        grads["dmod"][l] = jnp.concatenate([d_sh_m, d_sc_m, d_g_m, d_sh_f, d_sc_f, d_g_f_here], axis=1)[0]

    grad_x = dx.reshape(x.shape)

    small_shapes = dict(dmod=b_ada.shape, norm_mix_g=norm_mix_g.shape, norm_ffn_g=norm_ffn_g.shape,
                        final_norm_g=final_norm_g.shape, b_forget=b_forget.shape, sinks=sinks.shape,
                        rel_bias=rel_bias.shape, loss=())
    mine_small = _pack_small(dict(
        loss=_pad_lanes(loss_tile[0:1, 0:1], 128),
        dmod=jnp.stack(grads["dmod"]), norm_mix_g=jnp.stack(grads["norm_mix_g"]),
        norm_ffn_g=jnp.stack(grads["norm_ffn_g"]), final_norm_g=d_final_g[0],
        b_forget=_pad_lanes(jnp.stack(grads["b_forget"]).reshape(1, -1), 128),
        sinks=_pad_lanes(jnp.stack(grads["sinks"]).reshape(1, -1), 128),
        rel_bias=_pad_lanes(jnp.stack(grads["rel_bias"]).reshape(1, -1), 4224)))
    all_small = _small_all_gather(mine_small, "comm_gather_small").reshape(N_DEV, SMALL_ROWS, 128)
    pieces_in0 = _w_in_pieces(grads["w_in"][0], "w_in_pieces0", pair_major=True)
    from_sibling = _sibling_exchange([pieces_in0], "comm_reduce_in0_sibling")[0]
    pair_sum_in0 = _pair_add(pieces_in0, from_sibling, pc.astype(jnp.int32).reshape(1), "pair_add_in0")
    reduce_in0 = _exchange_start([pair_sum_in0], True, all_small, "comm_reduce_in0_start", CHIPS)
    in0_started = reduce_in0[4]

    def pack_params(b_ada_, nm, nf, fn, bf, sk, rb):
        return _pack_small(dict(dmod=b_ada_, norm_mix_g=nm, norm_ffn_g=nf, final_norm_g=fn, loss=jnp.zeros((1, 128), F32),
                                b_forget=_pad_lanes(bf.reshape(1, -1), 128), sinks=_pad_lanes(sk.reshape(1, -1), 128),
                                rel_bias=_pad_lanes(rb.reshape(1, -1), 4224)))

    small_out = _adamw(
        pack_params(b_ada, norm_mix_g, norm_ffn_g, final_norm_g, b_forget, sinks, rel_bias)[None],
        pack_params(m_b_ada, m_norm_mix_g, m_norm_ffn_g, m_final_norm_g, m_b_forget, m_sinks, m_rel_bias)[None],
        pack_params(v_b_ada, v_norm_mix_g, v_norm_ffn_g, v_final_norm_g, v_b_forget, v_sinks, v_rel_bias)[None],
        [all_small], "adamw_small", me_arr, after=in0_started)
    small_out = [_unpack_small(t[0], small_shapes) for t in small_out]

    dmod_all = all_small[:, :96].reshape(N_DEV, depth, 6 * D)
    dmod_cols = lax.dynamic_slice_in_dim(dmod_all, me * w_ada.shape[2], w_ada.shape[2], axis=2)
    d_w_ada = _ada_bwd(jnp.transpose(c_all), jnp.transpose(dmod_cols, (1, 0, 2)), "ada_bwd")

    big = {"w_ada": _adamw(w_ada, m_w_ada, v_w_ada, [d_w_ada[l:l + 1] for l in range(depth)], "adamw_w_ada", me_arr,
                           after=in0_started)}
    sent1, landed1 = _exchange_wait(reduce1, True, big["w_ada"][0], "comm_reduce1_wait")
    sent_rest0, landed_rest0 = _exchange_wait(reduce_rest0, True, landed1[0], "comm_reduce_rest0_wait")
    parts = {"w_in": [None, (landed1[0], sent1[0])]}
    for a, name in enumerate(("w_branch", "w_out", "w_ffn_in", "w_ffn_out")):
        parts[name] = [(landed_rest0[a], sent_rest0[a]), (landed1[1 + a], sent1[1 + a])]

    def update(name, w, m, v, view=lambda t: t):
        per_layer = lambda t: t.reshape(depth, -1, t.shape[-1])
        outs = _adamw(*[per_layer(view(t)) for t in (w, m, v)], parts[name], f"adamw_{name}",
                      me_in_arr if name == "w_in" else me_arr)
        big[name] = [view(t).reshape(w.shape) for t in outs]

    update("w_ffn_in", w_ffn_in, m_w_ffn_in, v_w_ffn_in, _w_ffn_in_view)
    update("w_ffn_out", w_ffn_out, m_w_ffn_out, v_w_ffn_out)
    update("w_branch", w_branch, m_w_branch, v_w_branch)
    update("w_out", w_out, m_w_out, v_w_out)
    sent_in0, landed_in0 = _exchange_wait(reduce_in0, True, big["w_out"][0], "comm_reduce_in0_wait", CHIPS)
    parts["w_in"][0] = (landed_in0[0], sent_in0[0])
    update("w_in", w_in, m_w_in, v_w_in)

    def leaf(kind, name):
        if name in big:
            return big[name][kind]
        return small_out[kind]["dmod" if name == "b_ada" else name]

    order = ["norm_mix_g", "norm_ffn_g", "w_ada", "b_ada", "w_in", "b_forget", "sinks", "rel_bias", "w_branch",
             "w_out", "w_ffn_in", "w_ffn_out", "final_norm_g"]
    loss = small_out[0]["loss"]
    return (loss, grad_x, *[leaf(0, n) for n in order], *[leaf(1, n) for n in order],
            *[leaf(2, n) for n in order], *[leaf(3, n) for n in order])
```

```python
import functools

import jax
import jax.numpy as jnp
from jax import lax
from jax.experimental import pallas as pl
from jax.experimental.pallas import tpu as pltpu

F32 = jnp.float32
BF16 = jnp.bfloat16
NEG_INF = -1e30
EPS = 1e-6
N_DEV = 8
BLK = 128
GROUP = 4 * BLK
VMEM_LIMIT_BYTES = 56 * 1024 * 1024

D_MODEL = 1024
N_QKV = 3840
N_GATES = 3072
N_MAIN = N_QKV + N_GATES
N_FORGET = 8
N_IN = N_MAIN + N_FORGET
F_COL = 2304
FFN_HIDDEN = 2816
N_REL = 257

ADAM_LR, ADAM_B1, ADAM_B2, ADAM_EPS, ADAM_WD, ADAM_STEP = 0.001, 0.9, 0.999, 1e-08, 0.01, 10

NN = (((1,), (0,)), ((), ()))
NT = (((1,), (1,)), ((), ()))
TN = (((0,), (0,)), ((), ()))
HIGHEST = lax.Precision.HIGHEST

ATTN_COLS = {"a": (0, 4, 5), "b": (6, 10, 14), "c": (18, 22, 26)}
ATTN_WINDOW = {"a": 2, "c": 5}
ATTN_BLOCKS_PER_STEP = {"a": 8, "b": GROUP // BLK, "c": 2}
ROW_TILE = 512


def _params():
    return pltpu.CompilerParams(vmem_limit_bytes=VMEM_LIMIT_BYTES)


def _tile(n, target):
    best = None
    t = 128
    while t <= min(n, target):
        if n % t == 0:
            best = t
        t += 128
    return best if best is not None else n


def _row_tile(n, target):
    t = min(n, target)
    while n % t:
        t -= 8
    return t


TILES = {
    "proj": (1024, 768, 1024), "proj_forget": (1024, 128, 1024),
    "ffn_out": (1024, 512, 2816), "ffn_fused": (512, 1408),
    "wgrad_ffn_out": (1408, 1024, 2048), "wgrad_ffn_in": (1408, 1024, 2048), "dgrad_ffn_in": (1024, 1024, 2816),
    "wgrad_out": (1024, 1024, 2048),
    "wgrad_in": (1024, 1408, 2048), "dgrad_in": (1024, 1024, 3520),
}


def _matmul(a, b, mode, out_dtype, name, tiles, *, n=None, a_off=0, b_off=0, m=None, after=None):
    tm, tn, tk = tiles
    if mode == "nn":
        M, K = a.shape if m is None else (m, a.shape[1])
        N = b.shape[1] if n is None else n
    elif mode == "nt":
        M, K = a.shape
        N = b.shape[0] if n is None else n
    else:
        K = a.shape[0]
        M = a.shape[1] if m is None else m
        N = b.shape[1] if n is None else n
    tm = _tile(M, tm) if M % 128 == 0 else M
    tn = _tile(N, tn)
    tk = _tile(K, tk)
    nk = K // tk
    dims = {"nn": NN, "nt": NT, "tn": TN}[mode]
    if mode == "nn":
        a_spec = pl.BlockSpec((tm, tk), lambda i, j, k: (i + a_off, k))
        b_spec = pl.BlockSpec((tk, tn), lambda i, j, k: (k, j + b_off))
    elif mode == "nt":
        a_spec = pl.BlockSpec((tm, tk), lambda i, j, k: (i + a_off, k))
        b_spec = pl.BlockSpec((tn, tk), lambda i, j, k: (j + b_off, k))
    else:
        a_spec = pl.BlockSpec((tk, tm), lambda i, j, k: (k, i + a_off))
        b_spec = pl.BlockSpec((tk, tn), lambda i, j, k: (k, j + b_off))

    def body(a_ref, b_ref, *rest):
        o_ref, acc_ref = rest[-2:]
        k = pl.program_id(2)
        part = lax.dot_general(a_ref[...], b_ref[...], dims, preferred_element_type=F32)
        if nk == 1:
            o_ref[...] = part.astype(o_ref.dtype)
        else:
            @pl.when(k == 0)
            def _():
                acc_ref[...] = part

            @pl.when(k > 0)
            def _():
                acc_ref[...] += part

            @pl.when(k == nk - 1)
            def _():
                o_ref[...] = acc_ref[...].astype(o_ref.dtype)

    return pl.pallas_call(
        body, name=name,
        out_shape=jax.ShapeDtypeStruct((M, N), out_dtype),
        grid=(M // tm, N // tn, nk),
        in_specs=[a_spec, b_spec] + ([ANY] if after is not None else []),
        out_specs=pl.BlockSpec((tm, tn), lambda i, j, k: (i, j)),
        scratch_shapes=[pltpu.VMEM((tm, tn) if nk > 1 else (8, 128), F32)],
        compiler_params=_params(),
    )(a, b, *([after] if after is not None else []))


def _project(h, w, name):
    S, D = h.shape
    tm, tn, _ = TILES["proj"]
    tm = _tile(S, tm)
    nq, ng = N_QKV // tn, N_GATES // tn

    def body(h_ref, w_ref, q_ref, g_ref):
        j = pl.program_id(1)
        acc = jnp.dot(h_ref[...], w_ref[...], preferred_element_type=F32)

        @pl.when(j < nq)
        def _():
            q_ref[...] = acc.astype(BF16)

        @pl.when(j >= nq)
        def _():
            g_ref[...] = acc

    return pl.pallas_call(
        body, name=name,
        out_shape=(jax.ShapeDtypeStruct((S, N_QKV), BF16), jax.ShapeDtypeStruct((S, N_GATES), F32)),
        grid=(S // tm, nq + ng),
        in_specs=[pl.BlockSpec((tm, D), lambda i, j: (i, 0)), pl.BlockSpec((D, tn), lambda i, j: (0, j))],
        out_specs=(pl.BlockSpec((tm, tn), lambda i, j: (i, jnp.minimum(j, nq - 1))),
                   pl.BlockSpec((tm, tn), lambda i, j: (i, jnp.maximum(j - nq, 0)))),
        compiler_params=_params(),
    )(h, w)


def _matmul_resid(a, b, resid, gate, name, tiles, after=None):
    M, K = a.shape
    N = b.shape[1]
    tm, tn, tk = (_tile(d, t) for d, t in zip((M, N, K), tiles))
    nk = K // tk

    def body(a_ref, b_ref, r_ref, g_ref, *rest):
        o_ref, s_ref, acc_ref = rest[-3:]
        k = pl.program_id(2)
        part = jnp.dot(a_ref[...], b_ref[...], preferred_element_type=F32)

        def finish(acc):
            o_ref[...] = r_ref[...] + g_ref[...] * acc
            s_ref[...] = acc.astype(BF16)

        if nk == 1:
            finish(part)
        else:
            @pl.when(k == 0)
            def _():
                acc_ref[...] = part

            @pl.when(k > 0)
            def _():
                acc_ref[...] += part

            @pl.when(k == nk - 1)
            def _():
                finish(acc_ref[...])

    return pl.pallas_call(
        body, name=name,
        out_shape=(jax.ShapeDtypeStruct((M, N), F32), jax.ShapeDtypeStruct((M, N), BF16)),
        grid=(M // tm, N // tn, nk),
        in_specs=[pl.BlockSpec((tm, tk), lambda i, j, k: (i, k)),
                  pl.BlockSpec((tk, tn), lambda i, j, k: (k, j)),
                  pl.BlockSpec((tm, tn), lambda i, j, k: (i, j)),
                  pl.BlockSpec((1, tn), lambda i, j, k: (0, j))] + ([ANY] if after is not None else []),
        out_specs=(pl.BlockSpec((tm, tn), lambda i, j, k: (i, j)),
                   pl.BlockSpec((tm, tn), lambda i, j, k: (i, j))),
        scratch_shapes=[pltpu.VMEM((tm, tn) if nk > 1 else (8, 128), F32)],
        compiler_params=_params(),
    )(a, b, resid, gate, *([after] if after is not None else []))


def _norm_mod_fwd(x, g, shift, scale, name):
    S, D = x.shape
    ts = _row_tile(S, ROW_TILE)

    def body(x_ref, g_ref, sh_ref, sc_ref, h_ref):
        xv = x_ref[...]
        rstd = lax.rsqrt(jnp.mean(xv * xv, axis=-1, keepdims=True) + EPS)
        y = xv * rstd * g_ref[...]
        h_ref[...] = (y * (1.0 + sc_ref[...]) + sh_ref[...]).astype(BF16)

    row = pl.BlockSpec((1, D), lambda i: (0, 0))
    return pl.pallas_call(
        body, name=name, out_shape=jax.ShapeDtypeStruct((S, D), BF16), grid=(S // ts,),
        in_specs=[pl.BlockSpec((ts, D), lambda i: (i, 0)), row, row, row],
        out_specs=pl.BlockSpec((ts, D), lambda i: (i, 0)),
        compiler_params=_params(),
    )(x, g, shift, scale)


def _accumulate_rows(i, pairs):
    @pl.when(i == 0)
    def _():
        for ref, value in pairs:
            ref[...] = value

    @pl.when(i > 0)
    def _():
        for ref, value in pairs:
            ref[...] += value


def _gated_residual_bwd(dx, f_ref, gate_ref, df_ref):
    df_ref[...] = (dx * gate_ref[...]).astype(BF16)
    return jnp.sum(dx * f_ref[...].astype(F32), axis=0, keepdims=True)


def _norm_mod_bwd(x, dh, dres, g, scale, name, below=None):
    S, D = x.shape
    ts = _row_tile(S, ROW_TILE)

    def body(x_ref, dh_ref, dr_ref, g_ref, sc_ref, *rest):
        i = pl.program_id(0)
        xv, dhv, gv = x_ref[...], dh_ref[...], g_ref[...]
        rstd = lax.rsqrt(jnp.mean(xv * xv, axis=-1, keepdims=True) + EPS)
        xhat = xv * rstd
        dn = dhv * (1.0 + sc_ref[...])
        dxhat = dn * gv
        proj = jnp.mean(dxhat * xhat, axis=-1, keepdims=True)
        dx = dr_ref[...] + rstd * (dxhat - xhat * proj)
        sums = [jnp.sum(dhv, axis=0, keepdims=True), jnp.sum(dhv * (xhat * gv), axis=0, keepdims=True),
                jnp.sum(dn * xhat, axis=0, keepdims=True)]
        if below is None:
            dx_ref, *sum_refs = rest
        else:
            f_ref, gate_ref, dx_ref, *sum_refs, df_ref = rest
            sums.append(_gated_residual_bwd(dx, f_ref, gate_ref, df_ref))
        dx_ref[...] = dx
        _accumulate_rows(i, list(zip(sum_refs, sums)))

    tile = pl.BlockSpec((ts, D), lambda i: (i, 0))
    row = pl.BlockSpec((1, D), lambda i: (0, 0))
    vec = jax.ShapeDtypeStruct((1, D), F32)
    fused = below is not None
    return pl.pallas_call(
        body, name=name,
        out_shape=(jax.ShapeDtypeStruct((S, D), F32), vec, vec, vec)
        + ((vec, jax.ShapeDtypeStruct((S, D), BF16)) if fused else ()),
        grid=(S // ts,),
        in_specs=[tile, tile, tile, row, row] + ([tile, row] if fused else []),
        out_specs=(tile, row, row, row) + ((row, tile) if fused else ()),
        compiler_params=_params(),
    )(x, dh, dres, g, scale, *(below if fused else ()))


def _ffn_in_fwd(h, w_t, name):
    S, D = h.shape
    F = w_t.shape[0] // 2
    tm, tn = _tile(S, TILES["ffn_fused"][0]), _tile(F, TILES["ffn_fused"][1])
    nj = F // tn

    def body(h_ref, wg_ref, wu_ref, o_ref):
        hv = h_ref[...]
        ug = lax.dot_general(hv, wg_ref[...], NT, preferred_element_type=F32)
        uu = lax.dot_general(hv, wu_ref[...], NT, preferred_element_type=F32)
        o_ref[...] = (ug * jax.nn.sigmoid(ug) * uu).astype(BF16)

    return pl.pallas_call(
        body, name=name, out_shape=jax.ShapeDtypeStruct((S, F), BF16), grid=(nj, S // tm),
        in_specs=[pl.BlockSpec((tm, D), lambda j, i: (i, 0)),
                  pl.BlockSpec((tn, D), lambda j, i: (j, 0)),
                  pl.BlockSpec((tn, D), lambda j, i: (j + nj, 0))],
        out_specs=pl.BlockSpec((tm, tn), lambda j, i: (i, j)),
        compiler_params=_params(),
    )(h, w_t, w_t)


def _ffn_mid_bwd(h, df, w_in_t, w_out, name):
    S, D = h.shape
    F = w_in_t.shape[0] // 2
    tm, tn = _tile(S, TILES["ffn_fused"][0]), _tile(F, TILES["ffn_fused"][1])
    nj = F // tn

    def body(h_ref, df_ref, wg_ref, wu_ref, wo_ref, dg_ref, du_ref):
        hv = h_ref[...]
        ug = lax.dot_general(hv, wg_ref[...], NT, preferred_element_type=F32)
        uu = lax.dot_general(hv, wu_ref[...], NT, preferred_element_type=F32)
        dact = lax.dot_general(df_ref[...], wo_ref[...], NT, preferred_element_type=F32)
        sig = jax.nn.sigmoid(ug)
        dg_ref[...] = (dact * uu * (sig * (1.0 + ug * (1.0 - sig)))).astype(BF16)
        du_ref[...] = (dact * (ug * sig)).astype(BF16)

    out = jax.ShapeDtypeStruct((S, F), BF16)
    return pl.pallas_call(
        body, name=name, out_shape=(out, out), grid=(nj, S // tm),
        in_specs=[pl.BlockSpec((tm, D), lambda j, i: (i, 0)),
                  pl.BlockSpec((tm, D), lambda j, i: (i, 0)),
                  pl.BlockSpec((tn, D), lambda j, i: (j, 0)),
                  pl.BlockSpec((tn, D), lambda j, i: (j + nj, 0)),
                  pl.BlockSpec((tn, D), lambda j, i: (j, 0))],
        out_specs=(pl.BlockSpec((tm, tn), lambda j, i: (i, j)), pl.BlockSpec((tm, tn), lambda j, i: (i, j))),
        compiler_params=_params(),
    )(h, df, w_in_t, w_in_t, w_out)


def _merge_fwd(o_a, o_b, o_c, gates, w_branch, w_out, resid, gate, name, *, tm=512):
    S, W = o_a.shape
    D = w_branch.shape[2]
    tm = _row_tile(S, tm)

    def body(oa_ref, ob_ref, oc_ref, g_ref, w_ref, wo_ref, r_ref, gm_ref, x_ref, m_ref, mix_ref):
        acc = None
        for k, o_ref in enumerate((oa_ref, ob_ref, oc_ref)):
            y = jnp.dot(o_ref[...], w_ref[k], preferred_element_type=F32)
            t = jax.nn.sigmoid(g_ref[:, k * D:(k + 1) * D]) * y
            acc = t if acc is None else acc + t
        merged = acc.astype(BF16)
        m_ref[...] = merged
        mix = jnp.dot(merged, wo_ref[...], preferred_element_type=F32)
        x_ref[...] = r_ref[...] + gm_ref[...] * mix
        mix_ref[...] = mix.astype(BF16)

    o_spec = pl.BlockSpec((tm, W), lambda i: (i, 0))
    tile = pl.BlockSpec((tm, D), lambda i: (i, 0))
    return pl.pallas_call(
        body, name=name,
        out_shape=(jax.ShapeDtypeStruct((S, D), F32), jax.ShapeDtypeStruct((S, D), BF16), jax.ShapeDtypeStruct((S, D), BF16)),
        grid=(S // tm,),
        in_specs=[o_spec, o_spec, o_spec, pl.BlockSpec((tm, 3 * D), lambda i: (i, 0)),
                  pl.BlockSpec((3, W, D), lambda i: (0, 0, 0)), pl.BlockSpec((D, D), lambda i: (0, 0)),
                  tile, pl.BlockSpec((1, D), lambda i: (0, 0))],
        out_specs=(tile, tile, tile),
        compiler_params=_params(),
    )(o_a, o_b, o_c, gates, w_branch, w_out, resid, gate)


def _merge_bwd(dmix, o_a, o_b, o_c, gates, w_branch, w_out, name, *, tm=256):
    S, W = o_a.shape
    D = w_branch.shape[2]
    tm = _row_tile(S, tm)
    n_heads = W // 64

    def body(dm_ref, oa_ref, ob_ref, oc_ref, g_ref, w_ref, wo_ref, dg_ref, dw_ref,
             doa_ref, dob_ref, doc_ref, dla_ref, dlb_ref, dlc_ref):
        first = pl.program_id(0) == 0
        head_of_column = (lax.broadcasted_iota(jnp.int32, (W, BLK), 0) // 64
                          == lax.broadcasted_iota(jnp.int32, (W, BLK), 1)).astype(F32)
        dm = lax.dot_general(dm_ref[...], wo_ref[...], NT, preferred_element_type=F32)
        branches = ((oa_ref, doa_ref, dla_ref), (ob_ref, dob_ref, dlb_ref), (oc_ref, doc_ref, dlc_ref))
        for k, (o_ref, do_ref, dl_ref) in enumerate(branches):
            wk = w_ref[k]
            ov = o_ref[...]
            y = jnp.dot(ov, wk, preferred_element_type=F32)
            g = jax.nn.sigmoid(g_ref[:, k * D:(k + 1) * D])
            dy = (dm * g).astype(BF16)
            dwk = lax.dot_general(ov, dy, TN, preferred_element_type=F32)

            @pl.when(first)
            def _(k=k, dwk=dwk):
                dw_ref[k] = dwk

            @pl.when(jnp.logical_not(first))
            def _(k=k, dwk=dwk):
                dw_ref[k] += dwk
            dg_ref[:, k * D:(k + 1) * D] = (dm * y * (g * (1.0 - g))).astype(BF16)
            do16 = lax.dot_general(dy, wk, NT, preferred_element_type=F32).astype(BF16)
            do_ref[...] = do16
            prod = do16.astype(F32) * ov.astype(F32)
            dl_ref[...] = jnp.dot(prod, head_of_column, preferred_element_type=F32, precision=HIGHEST)[:, :n_heads]

    o_spec = pl.BlockSpec((tm, W), lambda i: (i, 0))
    wide = pl.BlockSpec((tm, 3 * D), lambda i: (i, 0))
    dl_spec = pl.BlockSpec((tm, n_heads), lambda i: (i, 0))
    o_out = jax.ShapeDtypeStruct((S, W), BF16)
    wide_out = jax.ShapeDtypeStruct((S, 3 * D), BF16)
    dl_out = jax.ShapeDtypeStruct((S, n_heads), F32)
    whole = pl.BlockSpec((3, W, D), lambda i: (0, 0, 0))
    return pl.pallas_call(
        body, name=name,
        out_shape=(wide_out, jax.ShapeDtypeStruct((3, W, D), F32), o_out, o_out, o_out, dl_out, dl_out, dl_out),
        grid=(S // tm,),
        in_specs=[pl.BlockSpec((tm, D), lambda i: (i, 0)), o_spec, o_spec, o_spec, wide, whole,
                  pl.BlockSpec((D, D), lambda i: (0, 0))],
        out_specs=(wide, whole, o_spec, o_spec, o_spec, dl_spec, dl_spec, dl_spec),
        compiler_params=_params(),
    )(dmix, o_a, o_b, o_c, gates, w_branch, w_out)


def _band_mask(variant, t_abs, s_abs):
    if variant == "b":
        return s_abs <= t_abs
    qc, kc = t_abs >> 6, s_abs >> 6
    return (kc <= qc) & (kc >= qc - (2 if variant == "a" else 8))


def _attn_fwd(variant, qkv, name, *, sinks=None, slopes=None, cq_col=None, ck_row=None, bias=None, after=None):
    S = qkv.shape[0]
    nb = S // BLK
    qb, kb, vb = ATTN_COLS[variant]
    shared_kv = variant == "a"
    win = ATTN_WINDOW.get(variant)
    per_step = ATTN_BLOCKS_PER_STEP[variant]

    def body(*refs):
        if after is not None:
            refs = refs[:-3] + refs[-2:]
        if variant == "a":
            q_ref, k_ref, v_ref, sink_ref, slope_ref, o_ref, lse_ref = refs
        elif variant == "b":
            q_ref, k_ref, v_ref, cq_ref, ck_ref, o_ref, lse_ref = refs
        else:
            q_ref, k_ref, v_ref, bias_ref, o_ref, lse_ref = refs
        p = pl.program_id(0)
        lane = lax.broadcasted_iota(jnp.int32, (1, BLK), 1)

        def compute(i, rows, start, n_keys):
            n_rows = rows.stop - rows.start
            t_abs = i * BLK + lax.broadcasted_iota(jnp.int32, (n_rows, 1), 0)
            q2 = q_ref[rows, :].astype(F32) * 0.125
            k_w = k_ref[pl.ds(start, n_keys), :]
            v_w = v_ref[pl.ds(start, n_keys), :]
            s_abs = start + lax.broadcasted_iota(jnp.int32, (1, n_keys), 1)
            valid = _band_mask(variant, t_abs, s_abs)
            outs = []
            for half in (0, 1):
                hmask = (lane >= 64) if half else (lane < 64)
                qh = jnp.where(hmask, q2, 0.0)
                if shared_kv:
                    swap = (p // 2) != half
                    qh = jnp.where(swap, pltpu.roll(qh, 64, 1), qh)
                s = lax.dot_general(qh.astype(BF16), k_w, NT, preferred_element_type=F32)
                if variant == "a":
                    head = 2 * p + half
                    s = s + (-slope_ref[head]) * jnp.abs(t_abs - s_abs).astype(F32)
                elif variant == "b":
                    s = s + cq_ref[rows, half:half + 1] - ck_ref[half:half + 1, pl.ds(start, n_keys)]
                else:
                    j0 = start // BLK
                    s = s + jnp.concatenate([jnp.concatenate(
                        [bias_ref[half, jnp.clip(i + r - j0 - b, 0, 4)] for b in range(n_keys // BLK)], axis=1)
                        for r in range(n_rows // BLK)], axis=0)
                s = jnp.where(valid, s, NEG_INF)
                m = jnp.max(s, axis=1, keepdims=True)
                if variant == "a":
                    m = jnp.maximum(m, sink_ref[head])
                pe = jnp.exp(s - m)
                l = jnp.sum(pe, axis=1, keepdims=True)
                if variant == "a":
                    l = l + jnp.exp(sink_ref[head] - m)
                out = jnp.dot(pe.astype(BF16), v_w, preferred_element_type=F32) / l
                if shared_kv:
                    out = jnp.where(swap, pltpu.roll(out, 64, 1), out)
                outs.append(out)
                lse_ref[rows, half:half + 1] = m + jnp.log(l)
            o_ref[rows, :] = jnp.where(lane < 64, outs[0], outs[1]).astype(BF16)

        step = pl.program_id(1)
        if variant == "b":
            for g in range(S // GROUP):
                pl.when(step == g)(functools.partial(compute, step * per_step, slice(0, GROUP), 0, (g + 1) * GROUP))
        elif variant == "c":
            span = win + per_step - 1
            start = jnp.clip(step * per_step - (win - 1), 0, nb - span) * BLK
            compute(step * per_step, slice(0, per_step * BLK), pl.multiple_of(start, BLK), span * BLK)
        else:
            for sub in range(per_step):
                i = step * per_step + sub
                start = jnp.clip(i - (win - 1), 0, nb - win) * BLK
                compute(i, slice(sub * BLK, (sub + 1) * BLK), pl.multiple_of(start, BLK), win * BLK)

    tq = per_step * BLK
    kv_col = (lambda p, i: (0, kb)) if shared_kv else (lambda p, i: (0, kb + p))
    vv_col = (lambda p, i: (0, vb)) if shared_kv else (lambda p, i: (0, vb + p))
    in_specs = [pl.BlockSpec((tq, BLK), lambda p, i: (i, qb + p)),
                pl.BlockSpec((S, BLK), kv_col), pl.BlockSpec((S, BLK), vv_col)]
    args = [qkv, qkv, qkv]
    if variant == "a":
        in_specs += [pl.BlockSpec(memory_space=pltpu.SMEM), pl.BlockSpec(memory_space=pltpu.SMEM)]
        args += [sinks, slopes]
    elif variant == "b":
        in_specs += [pl.BlockSpec((None, tq, 2), lambda p, i: (p, i, 0)),
                     pl.BlockSpec((None, 2, S), lambda p, i: (p, 0, 0))]
        args += [cq_col, ck_row]
    else:
        in_specs += [pl.BlockSpec((2, 5, BLK, BLK), lambda p, i: (p, 0, 0, 0))]
        args += [bias]
    if after is not None:
        in_specs.append(ANY)
        args.append(after)
    return pl.pallas_call(
        body, name=name,
        out_shape=(jax.ShapeDtypeStruct((S, 512), BF16), jax.ShapeDtypeStruct((4, S, 2), F32)),
        grid=(4, nb // per_step), in_specs=in_specs,
        out_specs=(pl.BlockSpec((tq, BLK), lambda p, i: (i, p)),
                   pl.BlockSpec((None, tq, 2), lambda p, i: (p, i, 0))),
        compiler_params=_params(),
    )(*args)


def _attn_bwd(variant, qkv, do, lse_row, delta_row, name, *, sinks=None, slopes=None, cq_row=None,
              ck_col=None, bias_t=None):
    S = qkv.shape[0]
    nb = S // BLK
    qb, kb, vb = ATTN_COLS[variant]
    shared_kv = variant == "a"
    win = ATTN_WINDOW.get(variant)
    per_step = ATTN_BLOCKS_PER_STEP[variant]

    def body(*refs):
        *refs, dqt_ref = refs
        if variant == "a":
            (q_ref, k_ref, v_ref, do_ref, lse_ref, dl_ref, sink_ref, slope_ref,
             dq_ref, dk_ref, dv_ref, ex_ref) = refs
        elif variant == "b":
            (q_ref, k_ref, v_ref, do_ref, lse_ref, dl_ref, cq_ref, ck_ref,
             dq_ref, dk_ref, dv_ref, ex_ref, dcq_ref) = refs
        else:
            (q_ref, k_ref, v_ref, do_ref, lse_ref, dl_ref, bias_ref,
             dq_ref, dk_ref, dv_ref, ex_ref) = refs
        p = pl.program_id(0)
        lane = lax.broadcasted_iota(jnp.int32, (1, BLK), 1)
        hmasks = [(lane < 64), (lane >= 64)]
        swaps = [(p // 2) != half for half in (0, 1)] if shared_kv else None

        @pl.when(pl.program_id(1) == 0)
        def _():
            dqt_ref[...] = jnp.zeros_like(dqt_ref)
            if variant == "b":
                dcq_ref[...] = jnp.zeros_like(dcq_ref)
            else:
                ex_ref[...] = jnp.zeros_like(ex_ref)

        def to_kv_lanes(x, h):
            x = jnp.where(hmasks[h], x, 0.0)
            if shared_kv:
                x = jnp.where(swaps[h], pltpu.roll(x, 64, 1), x)
            return x

        def compute(j, rows, start, n_q):
            n_rows = rows.stop - rows.start
            s_abs = j * BLK + lax.broadcasted_iota(jnp.int32, (n_rows, 1), 0)
            off_k = pl.multiple_of(j * BLK, BLK)
            k2 = k_ref[rows, :].astype(F32)
            v2 = v_ref[rows, :].astype(F32)
            if shared_kv:
                kv_lane = (lane >> 6) == (p // 2)
                k_src, v_src = jnp.where(kv_lane, k2, 0.0), jnp.where(kv_lane, v2, 0.0)
                k_al = [jnp.where(swaps[h], pltpu.roll(k_src, 64, 1), k_src) for h in (0, 1)]
                v_al = [jnp.where(swaps[h], pltpu.roll(v_src, 64, 1), v_src) for h in (0, 1)]
            else:
                k_al = [jnp.where(hmasks[h], k2, 0.0) for h in (0, 1)]
                v_al = [jnp.where(hmasks[h], v2, 0.0) for h in (0, 1)]
            k_al = [(t * 0.125).astype(BF16) for t in k_al]
            v_al = [t.astype(BF16) for t in v_al]
            q_w = q_ref[pl.ds(start, n_q), :]
            do_w = do_ref[pl.ds(start, n_q), :]
            t_abs = start + lax.broadcasted_iota(jnp.int32, (1, n_q), 1)
            valid = _band_mask(variant, t_abs, s_abs)
            dk_acc = dv_acc = None
            ds_both = []
            for half in (0, 1):
                s = lax.dot_general(k_al[half], q_w, NT, preferred_element_type=F32)
                if variant == "a":
                    s = s + (-slope_ref[2 * p + half]) * jnp.abs(t_abs - s_abs).astype(F32)
                elif variant == "b":
                    s = s + cq_ref[half:half + 1, pl.ds(start, n_q)] - ck_ref[rows, half:half + 1]
                else:
                    i0 = start // BLK
                    s = s + jnp.concatenate([jnp.concatenate(
                        [bias_ref[half, jnp.clip(i0 + b - j - r, 0, 4)] for b in range(n_q // BLK)], axis=1)
                        for r in range(n_rows // BLK)], axis=0)
                pr = jnp.where(valid, jnp.exp(s - lse_ref[half:half + 1, pl.ds(start, n_q)]), 0.0)
                dp = lax.dot_general(v_al[half], do_w, NT, preferred_element_type=F32)
                ds = pr * (dp - dl_ref[half:half + 1, pl.ds(start, n_q)])
                ds16 = ds.astype(BF16)
                dv_h = to_kv_lanes(jnp.dot(pr.astype(BF16), do_w, preferred_element_type=F32), half)
                dk_h = to_kv_lanes(jnp.dot(ds16, q_w, preferred_element_type=F32) * 0.125, half)
                dv_acc = dv_h if dv_acc is None else dv_acc + dv_h
                dk_acc = dk_h if dk_acc is None else dk_acc + dk_h
                ds_both.append(ds16)
                if variant == "b":
                    ex_ref[rows, half:half + 1] = -jnp.sum(ds, axis=1, keepdims=True)
                    dcq_ref[half:half + 1, pl.ds(start, n_q)] += jnp.sum(ds, axis=0, keepdims=True)
                elif variant == "c":
                    for r in range(n_rows // BLK):
                        for b in range(n_q // BLK):
                            ex_ref[half, jnp.clip(i0 + b - j - r, 0, 4)] += ds[r * BLK:(r + 1) * BLK, b * BLK:(b + 1) * BLK]
            dq_t = lax.dot_general(jnp.concatenate(k_al, axis=0), jnp.concatenate(ds_both, axis=0), TN,
                                   preferred_element_type=F32)
            dqt_ref[:, pl.ds(start, n_q)] += dq_t
            if shared_kv:
                @pl.when(p == 0)
                def _():
                    dk_ref[pl.ds(off_k, n_rows), :] = dk_acc
                    dv_ref[pl.ds(off_k, n_rows), :] = dv_acc

                @pl.when(p > 0)
                def _():
                    dk_ref[pl.ds(off_k, n_rows), :] += dk_acc
                    dv_ref[pl.ds(off_k, n_rows), :] += dv_acc
            else:
                dk_ref[pl.ds(off_k, n_rows), :] = dk_acc.astype(dk_ref.dtype)
                dv_ref[pl.ds(off_k, n_rows), :] = dv_acc.astype(dv_ref.dtype)
            if variant == "a":
                for half in (0, 1):
                    p_sink = jnp.exp(sink_ref[2 * p + half] - lse_ref[half:half + 1, pl.ds(off_k, n_rows)])
                    term = p_sink * dl_ref[half:half + 1, pl.ds(off_k, n_rows)]
                    ex_ref[half:half + 1, :] += -jnp.sum(term, axis=1, keepdims=True)

        step = pl.program_id(1)
        if variant == "b":
            for g in range(S // GROUP):
                pl.when(step == g)(functools.partial(compute, step * per_step, slice(0, GROUP), g * GROUP, S - g * GROUP))
        elif variant == "c":
            span = win + per_step - 1
            start = jnp.clip(step * per_step, 0, nb - span) * BLK
            compute(step * per_step, slice(0, per_step * BLK), pl.multiple_of(start, BLK), span * BLK)
        else:
            for sub in range(per_step):
                j = step * per_step + sub
                start = jnp.clip(j, 0, nb - win) * BLK
                compute(j, slice(sub * BLK, (sub + 1) * BLK), pl.multiple_of(start, BLK), win * BLK)

        @pl.when(step == nb // per_step - 1)
        def _():
            dq_ref[...] = jnp.transpose(dqt_ref[...]).astype(BF16)

    tk = per_step * BLK
    col = lambda c0: (lambda p, j: (0, c0 + p))
    kv_blk = (lambda c0: (lambda p, j: (j, c0))) if shared_kv else (lambda c0: (lambda p, j: (j, c0 + p)))
    pair = lambda p, j: (0, p)
    row_stat = pl.BlockSpec((None, 2, S), lambda p, j: (p, 0, 0))
    in_specs = [pl.BlockSpec((S, BLK), col(qb)),
                pl.BlockSpec((tk, BLK), kv_blk(kb)), pl.BlockSpec((tk, BLK), kv_blk(vb)),
                pl.BlockSpec((S, BLK), pair), row_stat, row_stat]
    args = [qkv, qkv, qkv, do, lse_row, delta_row]
    kv_width = BLK if shared_kv else 512
    kv_out = pl.BlockSpec((S, BLK), (lambda p, j: (0, 0)) if shared_kv else pair)
    kv_dtype = F32 if shared_kv else BF16
    out_shape = [jax.ShapeDtypeStruct((S, 512), BF16), jax.ShapeDtypeStruct((S, kv_width), kv_dtype),
                 jax.ShapeDtypeStruct((S, kv_width), kv_dtype)]
    out_specs = [pl.BlockSpec((S, BLK), pair), kv_out, kv_out]
    if variant == "a":
        in_specs += [pl.BlockSpec(memory_space=pltpu.SMEM), pl.BlockSpec(memory_space=pltpu.SMEM)]
        args += [sinks, slopes]
        out_shape.append(jax.ShapeDtypeStruct((4, 8, BLK), F32))
        out_specs.append(pl.BlockSpec((None, 8, BLK), lambda p, j: (p, 0, 0)))
    elif variant == "b":
        in_specs += [row_stat, pl.BlockSpec((None, tk, 2), lambda p, j: (p, j, 0))]
        args += [cq_row, ck_col]
        out_shape += [jax.ShapeDtypeStruct((4, S, 2), F32), jax.ShapeDtypeStruct((4, 2, S), F32)]
        out_specs += [pl.BlockSpec((None, tk, 2), lambda p, j: (p, j, 0)), row_stat]
    else:
        in_specs += [pl.BlockSpec((2, 5, BLK, BLK), lambda p, j: (p, 0, 0, 0))]
        args += [bias_t]
        out_shape.append(jax.ShapeDtypeStruct((8, 5, BLK, BLK), F32))
        out_specs.append(pl.BlockSpec((2, 5, BLK, BLK), lambda p, j: (p, 0, 0, 0)))
    return pl.pallas_call(
        body, name=name, out_shape=tuple(out_shape), grid=(4, nb // per_step),
        in_specs=in_specs, out_specs=tuple(out_specs), scratch_shapes=[pltpu.VMEM((BLK, S), F32)],
        compiler_params=_params(),
    )(*args)


def _log_sigmoid(x):
    return jnp.minimum(x, 0.0) - jnp.log(1.0 + jnp.exp(-jnp.abs(x)))


def _forget_fwd(fb, b_forget, name):
    S = fb.shape[0]
    nb = S // GROUP

    def body(fb_ref, b_ref, cum_ref, carry_ref):
        i = pl.program_id(0)
        logf = _log_sigmoid(fb_ref[...] + b_ref[...])
        r = lax.broadcasted_iota(jnp.int32, (GROUP, GROUP), 0)
        c = lax.broadcasted_iota(jnp.int32, (GROUP, GROUP), 1)
        tri = (c <= r).astype(F32)

        @pl.when(i == 0)
        def _():
            carry_ref[...] = jnp.zeros_like(carry_ref)

        cum = jnp.dot(tri, logf, preferred_element_type=F32, precision=HIGHEST) + carry_ref[0:1, :]
        cum_ref[...] = cum
        carry_ref[...] = jnp.broadcast_to(cum[GROUP - 1:GROUP, :], carry_ref.shape)

    return pl.pallas_call(
        body, name=name, out_shape=jax.ShapeDtypeStruct((S, BLK), F32), grid=(nb,),
        in_specs=[pl.BlockSpec((GROUP, BLK), lambda i: (i, 0)), pl.BlockSpec((1, BLK), lambda i: (0, 0))],
        out_specs=pl.BlockSpec((GROUP, BLK), lambda i: (i, 0)),
        scratch_shapes=[pltpu.VMEM((8, BLK), F32)],
        compiler_params=_params(),
    )(fb, b_forget)


def _forget_bwd(dcum_q, dcum_k, fb, b_forget, name):
    S = fb.shape[0]
    nb = S // GROUP

    def body(dq_ref, dk_ref, fb_ref, b_ref, dfb_ref, db_ref, carry_ref):
        g = pl.program_id(0)
        r = lax.broadcasted_iota(jnp.int32, (GROUP, GROUP), 0)
        c = lax.broadcasted_iota(jnp.int32, (GROUP, GROUP), 1)
        tri = (c >= r).astype(F32)

        @pl.when(g == 0)
        def _():
            carry_ref[...] = jnp.zeros_like(carry_ref)

        dcum = dq_ref[...] + dk_ref[...]
        dlogf = jnp.dot(tri, dcum, preferred_element_type=F32, precision=HIGHEST) + carry_ref[0:1, :]
        carry_ref[...] = jnp.broadcast_to(dlogf[0:1, :], carry_ref.shape)
        x = fb_ref[...] + b_ref[...]
        lane = lax.broadcasted_iota(jnp.int32, (1, BLK), 1)
        dfb = jnp.where(lane < N_FORGET, dlogf * jax.nn.sigmoid(-x), 0.0)
        dfb_ref[...] = dfb
        db = jnp.sum(dfb, axis=0, keepdims=True)

        @pl.when(g == 0)
        def _():
            db_ref[...] = db

        @pl.when(g > 0)
        def _():
            db_ref[...] += db

    rev = pl.BlockSpec((GROUP, BLK), lambda g: (nb - 1 - g, 0))
    row = pl.BlockSpec((1, BLK), lambda g: (0, 0))
    return pl.pallas_call(
        body, name=name,
        out_shape=(jax.ShapeDtypeStruct((S, BLK), F32), jax.ShapeDtypeStruct((1, BLK), F32)), grid=(nb,),
        in_specs=[rev, rev, rev, row], out_specs=(rev, row),
        scratch_shapes=[pltpu.VMEM((8, BLK), F32)],
        compiler_params=_params(),
    )(dcum_q, dcum_k, fb, b_forget)


def _skew(x, sign):
    row = lax.broadcasted_iota(jnp.int32, x.shape, 0)
    for b in range(7):
        amount = (1 << b) if sign > 0 else 256 - (1 << b)
        x = jnp.where(((row >> b) & 1) == 1, pltpu.roll(x, amount, 1), x)
    return x


def _rel_bases(rel):
    far = rel[:, 256:257]
    far127 = jnp.broadcast_to(far, (rel.shape[0], 127))
    base0 = jnp.concatenate([rel[:, 128:0:-1], far, rel[:, 255:128:-1]], axis=1)
    base1 = jnp.concatenate([rel[:, 256:128:-1], far, far127], axis=1)
    base0_t = jnp.concatenate([rel[:, 128:256], far, rel[:, 1:128]], axis=1)
    base1_t = jnp.concatenate([jnp.broadcast_to(far, (rel.shape[0], 128)), far, rel[:, 129:256]], axis=1)
    return jnp.stack([base0, base1, base0_t, base1_t], axis=1)


def _rel_expand(bases, name):
    def body(b_ref, t_ref, tt_ref):
        far = jnp.broadcast_to(b_ref[1:2, 0:1], (BLK, BLK))
        for k, out_ref in ((0, t_ref), (2, tt_ref)):
            for d in (0, 1):
                x = jnp.broadcast_to(b_ref[k + d:k + d + 1, :], (BLK, 2 * BLK))
                out_ref[d] = _skew(x, 1)[:, :BLK]
            for d in (2, 3, 4):
                out_ref[d] = far

    out = jax.ShapeDtypeStruct((8, 5, BLK, BLK), F32)
    spec = pl.BlockSpec((None, 5, BLK, BLK), lambda h: (h, 0, 0, 0))
    return pl.pallas_call(
        body, name=name, out_shape=(out, out), grid=(8,),
        in_specs=[pl.BlockSpec((None, 4, 2 * BLK), lambda h: (h, 0, 0))], out_specs=(spec, spec),
        compiler_params=_params(),
    )(bases)


def _rel_reduce(dtiles_t, name):
    def body(dt_ref, o_ref):
        zeros = jnp.zeros((BLK, BLK), F32)
        sums = []
        for d in (0, 1):
            x = _skew(jnp.concatenate([dt_ref[d], zeros], axis=1), -1)
            sums.append(jnp.broadcast_to(jnp.sum(x, axis=0, keepdims=True), (8, 2 * BLK)))
        lane = lax.broadcasted_iota(jnp.int32, (8, 2 * BLK), 1)
        main = pltpu.roll(sums[0], BLK, 1) + jnp.where(lane > BLK, sums[1], 0.0)
        far = jnp.sum(jnp.where(lane < BLK, sums[1], 0.0)[0:1], axis=1, keepdims=True)
        far = far + jnp.sum(jnp.sum(dt_ref[2] + dt_ref[3] + dt_ref[4], axis=0, keepdims=True), axis=1, keepdims=True)
        o_ref[...] = jnp.concatenate([main[0:1], jnp.broadcast_to(far, (1, BLK))], axis=1)

    return pl.pallas_call(
        body, name=name, out_shape=jax.ShapeDtypeStruct((8, 1, 3 * BLK), F32), grid=(8,),
        in_specs=[pl.BlockSpec((None, 5, BLK, BLK), lambda h: (h, 0, 0, 0))],
        out_specs=pl.BlockSpec((None, 1, 3 * BLK), lambda h: (h, 0, 0)),
        compiler_params=_params(),
    )(dtiles_t)


def _final_loss(x, target, g, below, name):
    S, D = x.shape
    ts = _row_tile(S, ROW_TILE)

    def body(x_ref, t_ref, g_ref, f_ref, gate_ref, dx_ref, loss_ref, dg_ref, dgate_ref, df_ref):
        i = pl.program_id(0)
        xv, gv = x_ref[...], g_ref[...]
        rstd = lax.rsqrt(jnp.mean(xv * xv, axis=-1, keepdims=True) + EPS)
        xhat = xv * rstd
        err = xhat * gv - t_ref[...]
        part = 0.5 * jnp.sum(jnp.mean(err * err, axis=-1, keepdims=True), axis=0, keepdims=True)
        dy = err / D
        dg = jnp.sum(dy * xhat, axis=0, keepdims=True)
        dxhat = dy * gv
        proj = jnp.mean(dxhat * xhat, axis=-1, keepdims=True)
        dx = rstd * (dxhat - xhat * proj)
        dx_ref[...] = dx
        dgate = _gated_residual_bwd(dx, f_ref, gate_ref, df_ref)
        _accumulate_rows(i, [(loss_ref, jnp.broadcast_to(part, loss_ref.shape)), (dg_ref, dg), (dgate_ref, dgate)])

    tile = pl.BlockSpec((ts, D), lambda i: (i, 0))
    row = pl.BlockSpec((1, D), lambda i: (0, 0))
    vec = jax.ShapeDtypeStruct((1, D), F32)
    return pl.pallas_call(
        body, name=name,
        out_shape=(jax.ShapeDtypeStruct((S, D), F32), jax.ShapeDtypeStruct((8, 128), F32), vec, vec,
                   jax.ShapeDtypeStruct((S, D), BF16)),
        grid=(S // ts,), in_specs=[tile, tile, row, tile, row],
        out_specs=(tile, pl.BlockSpec((8, 128), lambda i: (0, 0)), row, row, tile),
        compiler_params=_params(),
    )(x, target, g, *below)


def _ada_fwd(c_all, w_ada, name):
    L, D, E = w_ada.shape

    def body(c_ref, w_ref, o_ref):
        cv = c_ref[...]
        cond = cv * jax.nn.sigmoid(cv)
        o_ref[...] = jnp.dot(cond, w_ref[...], preferred_element_type=F32, precision=HIGHEST)

    return pl.pallas_call(
        body, name=name, out_shape=jax.ShapeDtypeStruct((L, N_DEV, E), F32), grid=(L,),
        in_specs=[pl.BlockSpec((N_DEV, D), lambda l: (0, 0)), pl.BlockSpec((None, D, E), lambda l: (l, 0, 0))],
        out_specs=pl.BlockSpec((None, N_DEV, E), lambda l: (l, 0, 0)),
        compiler_params=_params(),
    )(c_all, w_ada)


def _ada_bwd(c_all_t, dmod, name):
    D = c_all_t.shape[0]
    L, _, E = dmod.shape

    def body(c_ref, d_ref, o_ref):
        cv = c_ref[...]
        cond = cv * jax.nn.sigmoid(cv)
        acc = None
        for b in range(N_DEV):
            t = cond[:, b:b + 1] * d_ref[b:b + 1, :]
            acc = t if acc is None else acc + t
        o_ref[...] = acc

    return pl.pallas_call(
        body, name=name, out_shape=jax.ShapeDtypeStruct((L, D, E), F32), grid=(L,),
        in_specs=[pl.BlockSpec((D, N_DEV), lambda l: (0, 0)), pl.BlockSpec((None, N_DEV, E), lambda l: (l, 0, 0))],
        out_specs=pl.BlockSpec((None, D, E), lambda l: (l, 0, 0)),
        compiler_params=_params(),
    )(c_all_t, dmod)


def _adamw(w, m, v, g_parts, name, me, after=None):
    L, R, C = w.shape
    tr = _row_tile(R, max(8, (256 * 1024 // max(C, 128)) // 8 * 8))
    nr = R // tr
    c1 = 1.0 - ADAM_B1 ** ADAM_STEP
    c2 = 1.0 - ADAM_B2 ** ADAM_STEP
    direct = [isinstance(p, tuple) for p in g_parts]
    n_in = sum(2 if d else 1 for d in direct)

    def body(me_ref, w_ref, m_ref, v_ref, *rest):
        g_refs, (go_ref, d_ref, mo_ref, vo_ref) = list(rest[:n_in]), rest[-4:]
        layer = pl.program_id(0)
        g = None
        for l in range(L):
            land_ref = g_refs.pop(0)
            own = g_refs.pop(0)[...].astype(F32) if direct[l] else None
            gl = None
            for k in range(land_ref.shape[0]):
                part = land_ref[k].astype(F32)
                if direct[l]:
                    part = jnp.where(me_ref[l] == k, own, part)
                gl = part if gl is None else gl + part
            g = gl if g is None else jnp.where(layer == l, gl, g)
        mn = ADAM_B1 * m_ref[...] + (1.0 - ADAM_B1) * g
        vn = ADAM_B2 * v_ref[...] + (1.0 - ADAM_B2) * (g * g)
        m_hat = mn / c1
        v_hat = vn / c2
        go_ref[...] = g
        d_ref[...] = -ADAM_LR * (m_hat / (jnp.sqrt(v_hat) + ADAM_EPS) + ADAM_WD * w_ref[...])
        mo_ref[...] = mn
        vo_ref[...] = vn

    def rows(l, layer, i):
        return jnp.where(layer == l, i, 0 if l > 0 else nr - 1)

    in_specs, operands = [], []
    for l, p in enumerate(g_parts):
        land, sent = p if direct[l] else (p, None)
        in_specs.append(pl.BlockSpec((land.shape[0], tr, C), lambda layer, i, me_ref, l=l: (0, rows(l, layer, i), 0)))
        operands.append(land)
        if direct[l]:
            in_specs.append(pl.BlockSpec((None, tr, C), lambda layer, i, me_ref, l=l: (me_ref[l], rows(l, layer, i), 0)))
            operands.append(sent)
    if after is not None:
        in_specs.append(ANY)
        operands.append(after)
    tile = pl.BlockSpec((None, tr, C), lambda layer, i, me_ref: (layer, i, 0))
    out = jax.ShapeDtypeStruct((L, R, C), F32)
    return pl.pallas_call(
        body, name=name, out_shape=(out, out, out, out),
        grid_spec=pltpu.PrefetchScalarGridSpec(
            num_scalar_prefetch=1, grid=(L, nr), in_specs=[tile, tile, tile] + in_specs,
            out_specs=(tile, tile, tile, tile)),
        compiler_params=_params(),
    )(me, w, m, v, *operands)


def _pair_add(pieces, recv, core, name):
    _, _, R, C = pieces.shape
    tr = _row_tile(R, max(8, (512 * 1024 // max(C, 128)) // 8 * 8))

    def body(core_ref, a_ref, b_ref, o_ref):
        o_ref[...] = (a_ref[...].astype(F32) + b_ref[...].astype(F32)).astype(BF16)

    return pl.pallas_call(
        body, name=name, out_shape=jax.ShapeDtypeStruct((4, R, C), BF16),
        grid_spec=pltpu.PrefetchScalarGridSpec(
            num_scalar_prefetch=1, grid=(4, R // tr),
            in_specs=[pl.BlockSpec((None, None, tr, C), lambda k, i, core_ref: (core_ref[0], k, i, 0)),
                      pl.BlockSpec((None, tr, C), lambda k, i, core_ref: (k, i, 0))],
            out_specs=pl.BlockSpec((None, tr, C), lambda k, i, core_ref: (k, i, 0))),
        compiler_params=_params(),
    )(core, pieces, recv)


MESH = pl.DeviceIdType.MESH
ANY = pl.BlockSpec(memory_space=pl.ANY)


def _position():
    return lax.axis_index("x"), lax.axis_index("y"), lax.axis_index("c")


def _small_all_gather(v, name):
    m_per, n = v.shape

    def body(x_ref, out_ref, send_sems, recv_sems, local_sem):
        x, y, c = _position()
        me, sibling = (x, y, c), (x, y, 1 - c)
        chips = [(1 - x, y), (x, 1 - y), (1 - x, 1 - y)]

        def rows(px, py, pc):
            return out_ref.at[pl.ds((4 * px + 2 * py + pc) * m_per, m_per), :]

        def copy(k, block, to, src=None):
            return pltpu.make_async_remote_copy(
                src_ref=rows(*block) if src is None else src, dst_ref=rows(*block),
                send_sem=send_sems.at[k], recv_sem=recv_sems.at[k], device_id=to, device_id_type=MESH)

        mine = pltpu.make_async_copy(x_ref, rows(*me), local_sem)
        mine.start()
        first = [copy(0, me, sibling, src=x_ref)]
        first += [copy(1 + j, me, (*chip, c), src=x_ref) for j, chip in enumerate(chips)]
        for cp in first:
            cp.start()
        passed = [copy(4 + j, (*chip, c), sibling) for j, chip in enumerate(chips)]
        for j, chip in enumerate(chips):
            copy(1 + j, (*chip, c), me).wait_recv()
            passed[j].start()
        copy(0, sibling, me).wait_recv()
        for j, chip in enumerate(chips):
            copy(4 + j, (*chip, 1 - c), me).wait_recv()
        for cp in first + passed:
            cp.wait_send()
        mine.wait()

    return pl.pallas_call(
        body, name=name, out_shape=jax.ShapeDtypeStruct((N_DEV * m_per, n), v.dtype),
        in_specs=[pl.BlockSpec(memory_space=pltpu.VMEM)], out_specs=pl.BlockSpec(memory_space=pltpu.VMEM),
        scratch_shapes=[pltpu.SemaphoreType.DMA((7,)), pltpu.SemaphoreType.DMA((7,)), pltpu.SemaphoreType.DMA],
    )(v)


def _sibling_exchange(pieces, name):
    n_arr = len(pieces)

    def body(*refs):
        p_refs, out_refs = refs[:n_arr], refs[n_arr:2 * n_arr]
        send_sems, recv_sems = refs[2 * n_arr:]
        x, y, c = _position()
        copies = [pltpu.make_async_remote_copy(
            src_ref=p_refs[a].at[1 - c], dst_ref=out_refs[a], send_sem=send_sems.at[a], recv_sem=recv_sems.at[a],
            device_id=(x, y, 1 - c), device_id_type=MESH) for a in range(n_arr)]
        for cp in copies:
            cp.start()
        for cp in copies:
            cp.wait()

    return pl.pallas_call(
        body, name=name,
        out_shape=tuple(jax.ShapeDtypeStruct(p.shape[1:], p.dtype) for p in pieces),
        in_specs=[ANY] * n_arr, out_specs=tuple([ANY] * n_arr),
        scratch_shapes=[pltpu.SemaphoreType.DMA((n_arr,)), pltpu.SemaphoreType.DMA((n_arr,))],
    )(*pieces)


HBM = pl.BlockSpec(memory_space=pltpu.HBM)
SEM = pl.BlockSpec(memory_space=pltpu.SEMAPHORE)
EFFECT = pltpu.SideEffectType.DATAFLOW_SIDE_EFFECTING
RELATIONS = [(rx, ry, rc) for rx in (0, 1) for ry in (0, 1) for rc in (0, 1)][1:]


SAME_CORE = [r for r in RELATIONS if r == (0, 0, 1) or r[2] == 0]


CHIPS = [r for r in RELATIONS if r[2] == 0]


def _exchange_copies(src_refs, land_refs, send_sems, recv_sems, scatter, receive_side, relations):
    x, y, c = _position()
    index = (lambda px, py, pc: 2 * px + py) if relations == CHIPS else (lambda px, py, pc: 4 * px + 2 * py + pc)
    me = index(x, y, c)
    copies = []
    for k, (rx, ry, rc) in enumerate(relations):
        peer = ((1 - x) if rx else x, (1 - y) if ry else y, (1 - c) if rc else c)
        peer_index = index(*peer)
        for a, (src, land) in enumerate(zip(src_refs, land_refs)):
            copies.append(pltpu.make_async_remote_copy(
                src_ref=src.at[peer_index] if scatter else src,
                dst_ref=land.at[peer_index if receive_side else me],
                send_sem=send_sems.at[a * len(relations) + k], recv_sem=recv_sems.at[a * len(relations) + k],
                device_id=peer, device_id_type=MESH))
    return copies


def _exchange_start(srcs, scatter, after, name, relations=RELATIONS):
    n = len(srcs)
    land_shapes = [(s.shape if scatter else (N_DEV,) + s.shape) for s in srcs]

    def body(*refs):
        src_refs, land_refs = refs[:n], refs[n:2 * n]
        send_sems, recv_sems = refs[2 * n + 1], refs[2 * n + 2]
        token = refs[-1]
        for cp in _exchange_copies(src_refs, land_refs, send_sems, recv_sems, scatter, False, relations):
            cp.start()
        token[...] = jnp.zeros_like(token)

    sems = pltpu.SemaphoreType.DMA((n * len(relations),))
    outs = pl.pallas_call(
        body, name=name,
        out_shape=(sems, sems, *[pltpu.HBM(s.shape, s.dtype) for s in srcs],
                   *[pltpu.HBM(shape, s.dtype) for shape, s in zip(land_shapes, srcs)],
                   jax.ShapeDtypeStruct((8, 128), F32)),
        in_specs=[HBM] * (2 * n) + [ANY],
        out_specs=(SEM, SEM, *[HBM] * (2 * n), pl.BlockSpec(memory_space=pltpu.VMEM)),
        input_output_aliases={a: 2 + a for a in range(2 * n)},
        compiler_params=pltpu.CompilerParams(has_side_effects=EFFECT),
    )(*[pltpu.with_memory_space_constraint(s, pltpu.HBM) for s in srcs],
      *[pltpu.with_memory_space_constraint(lax.empty(shape, s.dtype), pltpu.HBM)
        for shape, s in zip(land_shapes, srcs)], after)
    return outs[0], outs[1], outs[2:2 + n], outs[2 + n:2 + 2 * n], outs[-1]


def _exchange_wait(started, scatter, after, name, relations=RELATIONS):
    send_sems, recv_sems, srcs, lands, _ = started
    n = len(srcs)

    def body(*refs):
        src_refs, land_refs = refs[:n], refs[n:2 * n]
        send_sems, recv_sems = refs[2 * n], refs[2 * n + 1]
        copies = _exchange_copies(src_refs, land_refs, send_sems, recv_sems, scatter, True, relations)
        for cp in copies:
            cp.wait_send()
        for cp in copies:
            cp.wait_recv()

    outs = pl.pallas_call(
        body, name=name,
        out_shape=(*[pltpu.HBM(s.shape, s.dtype) for s in srcs], *[pltpu.HBM(t.shape, t.dtype) for t in lands]),
        in_specs=[HBM] * (2 * n) + [SEM, SEM, ANY], out_specs=tuple([HBM] * (2 * n)),
        input_output_aliases={a: a for a in range(2 * n)},
        compiler_params=pltpu.CompilerParams(has_side_effects=EFFECT),
    )(*srcs, *lands, send_sems, recv_sems, after)
    return outs[:n], outs[n:]


def _forward_copies(land_refs, send_sems, recv_sems, receive_side):
    x, y, c = _position()
    copies = []
    for j, (px, py) in enumerate([(1 - x, y), (x, 1 - y), (1 - x, 1 - y)]):
        held, coming = 4 * px + 2 * py + c, 4 * px + 2 * py + (1 - c)
        for a, land in enumerate(land_refs):
            copies.append(pltpu.make_async_remote_copy(
                src_ref=land.at[held], dst_ref=land.at[coming if receive_side else held],
                send_sem=send_sems.at[3 * a + j], recv_sem=recv_sems.at[3 * a + j],
                device_id=(x, y, 1 - c), device_id_type=MESH))
    return copies


def _forward_start(lands, after, name):
    n = len(lands)

    def body(*refs):
        send_sems, recv_sems, token = refs[n + 1], refs[n + 2], refs[-1]
        for cp in _forward_copies(refs[:n], send_sems, recv_sems, False):
            cp.start()
        token[...] = jnp.zeros_like(token)

    sems = pltpu.SemaphoreType.DMA((3 * n,))
    outs = pl.pallas_call(
        body, name=name,
        out_shape=(sems, sems, *[pltpu.HBM(t.shape, t.dtype) for t in lands], jax.ShapeDtypeStruct((8, 128), F32)),
        in_specs=[HBM] * n + [ANY], out_specs=(SEM, SEM, *[HBM] * n, pl.BlockSpec(memory_space=pltpu.VMEM)),
        input_output_aliases={a: 2 + a for a in range(n)},
        compiler_params=pltpu.CompilerParams(has_side_effects=EFFECT),
    )(*lands, after)
    return outs[0], outs[1], outs[2:2 + n], outs[-1]


def _forward_wait(started, after, name):
    send_sems, recv_sems, lands, _ = started
    n = len(lands)

    def body(*refs):
        copies = _forward_copies(refs[:n], refs[n], refs[n + 1], True)
        for cp in copies:
            cp.wait_send()
        for cp in copies:
            cp.wait_recv()

    return pl.pallas_call(
        body, name=name, out_shape=tuple(pltpu.HBM(t.shape, t.dtype) for t in lands),
        in_specs=[HBM] * n + [SEM, SEM, ANY], out_specs=tuple([HBM] * n),
        input_output_aliases={a: a for a in range(n)},
        compiler_params=pltpu.CompilerParams(has_side_effects=EFFECT),
    )(*lands, send_sems, recv_sems, after)


def _place_own(lands, mine, me, name):
    n = len(lands)
    flat = [m.reshape(-1, m.shape[-1]) for m in mine]
    flat_lands = [t.reshape(N_DEV, -1, t.shape[-1]) for t in lands]

    def body(me_ref, *refs):
        for src, dst in zip(refs[:n], refs[2 * n:]):
            dst[...] = src[...]

    in_specs = [pl.BlockSpec((m.shape[0] // 2, m.shape[1]), lambda i, me_ref: (i, 0)) for m in flat]
    out_specs = [pl.BlockSpec((None, m.shape[0] // 2, m.shape[1]), lambda i, me_ref: (me_ref[0], i, 0)) for m in flat]
    outs = pl.pallas_call(
        body, name=name, out_shape=tuple(jax.ShapeDtypeStruct(t.shape, t.dtype) for t in flat_lands),
        grid_spec=pltpu.PrefetchScalarGridSpec(
            num_scalar_prefetch=1, grid=(2,), in_specs=in_specs + [ANY] * n, out_specs=tuple(out_specs)),
        input_output_aliases={1 + n + a: a for a in range(n)},
        compiler_params=_params(),
    )(me, *flat, *flat_lands)
    return [o.reshape(t.shape) for o, t in zip(outs, lands)]


W_IN_SHARD = N_IN // N_DEV
F_SHARD = F_COL // W_IN_SHARD
F_LO = F_COL - F_SHARD * W_IN_SHARD


def _w_ffn_in_view(w):
    return jnp.transpose(w, (0, 2, 1))


def _w_in_segments():
    segments = []
    for d in range(N_DEV):
        if d == F_SHARD:
            segments += [(d, 0, d * W_IN_SHARD, F_LO), (d, F_LO, N_MAIN, N_FORGET),
                         (d, F_LO + N_FORGET, F_COL, W_IN_SHARD - F_LO - N_FORGET)]
        else:
            segments.append((d, 0, d * W_IN_SHARD - (N_FORGET if d > F_SHARD else 0), W_IN_SHARD))
    return segments


def _w_in_rearranged(g, name):
    D = g.shape[1]
    tr = _row_tile(D, 256)

    def body(g_ref, o_ref):
        for d, lo, at, width in _w_in_segments():
            o_ref[:, at:at + width] = g_ref[d, :, lo:lo + width]
        o_ref[:, N_IN:] = jnp.zeros((tr, BLK - N_FORGET), o_ref.dtype)

    return pl.pallas_call(
        body, name=name, out_shape=jax.ShapeDtypeStruct((D, N_MAIN + BLK), g.dtype), grid=(D // tr,),
        in_specs=[pl.BlockSpec((N_DEV, tr, W_IN_SHARD), lambda i: (0, i, 0))],
        out_specs=pl.BlockSpec((tr, N_MAIN + BLK), lambda i: (i, 0)),
        compiler_params=_params(),
    )(g)


def _w_in_pieces(dw_r, name, pair_major=False):
    D = dw_r.shape[0]
    tr = _row_tile(D, 256)
    lead = (2, 4) if pair_major else (N_DEV,)

    def body(x_ref, o_ref):
        for d, lo, at, width in _w_in_segments():
            slot = (d % 2, d // 2) if pair_major else (d,)
            o_ref[(*slot, slice(None), slice(lo, lo + width))] = x_ref[:, at:at + width]

    return pl.pallas_call(
        body, name=name, out_shape=jax.ShapeDtypeStruct((*lead, D, W_IN_SHARD), dw_r.dtype), grid=(D // tr,),
        in_specs=[pl.BlockSpec((tr, N_MAIN + BLK), lambda i: (i, 0))],
        out_specs=pl.BlockSpec((*lead, tr, W_IN_SHARD), lambda i: (*[0] * len(lead), i, 0)),
        compiler_params=_params(),
    )(dw_r)


def _row_pieces(dw):
    return dw.reshape(N_DEV, dw.shape[0] // N_DEV, dw.shape[1])


def _branch_pieces(dw):
    k, w, d = dw.shape
    return jnp.transpose(dw.reshape(k, w, N_DEV, d // N_DEV), (2, 0, 1, 3)).reshape(N_DEV, k * w, d // N_DEV)


def _pairs_col(a):
    return jnp.transpose(a.reshape(a.shape[0], 4, 2), (1, 0, 2))


def _pairs_row(a):
    return jnp.transpose(a.reshape(a.shape[0], 4, 2), (1, 2, 0))


def _heads_from_col(a):
    return jnp.transpose(a, (1, 0, 2)).reshape(a.shape[1], 8)


def _heads_from_row(a):
    return jnp.transpose(a, (2, 0, 1)).reshape(a.shape[2], 8)


def _pad_lanes(a, n):
    return jnp.pad(a, [(0, 0)] * (a.ndim - 1) + [(0, n - a.shape[-1])])


SMALL_SEGMENTS = (("dmod", 2 * 6 * D_MODEL), ("norm_mix_g", 2 * D_MODEL), ("norm_ffn_g", 2 * D_MODEL),
                  ("final_norm_g", D_MODEL), ("b_forget", 128), ("sinks", 128), ("rel_bias", 4224), ("loss", 128))
SMALL_ROWS = 176


def _pack_small(parts):
    flat = [_pad_lanes(parts[name].reshape(1, -1), size) for name, size in SMALL_SEGMENTS]
    total = sum(size for _, size in SMALL_SEGMENTS)
    flat.append(jnp.zeros((1, SMALL_ROWS * 128 - total), F32))
    return jnp.concatenate(flat, axis=1).reshape(SMALL_ROWS, 128)


def _unpack_small(packed, shapes):
    flat = packed.reshape(-1)
    out, pos = {}, 0
    for name, size in SMALL_SEGMENTS:
        shape = shapes[name]
        count = 1
        for d in shape:
            count *= d
        out[name] = flat[pos:pos + count].reshape(shape)
        pos += size
    return out


def kernel(x, c, norm_mix_g, norm_ffn_g, w_ada, b_ada, w_in, b_forget, sinks, rel_bias, w_branch, w_out, w_ffn_in, w_ffn_out, final_norm_g, loss_target, m_norm_mix_g, m_norm_ffn_g, m_w_ada, m_b_ada, m_w_in, m_b_forget, m_sinks, m_rel_bias, m_w_branch, m_w_out, m_w_ffn_in, m_w_ffn_out, m_final_norm_g, v_norm_mix_g, v_norm_ffn_g, v_w_ada, v_b_ada, v_w_in, v_b_forget, v_sinks, v_rel_bias, v_w_branch, v_w_out, v_w_ffn_in, v_w_ffn_out, v_final_norm_g):
    depth = w_in.shape[0]
    S, D = x.shape[1], x.shape[2]
    assert S % GROUP == 0 and S >= ATTN_WINDOW["c"] * BLK
    px, py, pc = _position()
    me = 4 * px + 2 * py + pc
    x0 = x[0]

    assert depth == 2
    big_weights = (w_in, w_branch, w_out, w_ffn_in, w_ffn_out)
    me_arr = jnp.stack([me, me]).astype(jnp.int32)
    me_in_arr = jnp.stack([2 * px + py, me]).astype(jnp.int32)

    def rest_matrices(g_branch, g_out, g_fin, g_fout):
        return (jnp.transpose(g_branch, (1, 2, 0, 3)).reshape(3, 512, D), g_out.reshape(D, D),
                g_fin.reshape(2 * FFN_HIDDEN, D), g_fout.reshape(FFN_HIDDEN, D))

    def arrive(started, after, name):
        mine, landed = _exchange_wait(started, False, after, f"{name}_wait", SAME_CORE)
        return mine, _forward_start(landed, mine[0], f"{name}_forward_start")

    def finish_gather(arrived, after, name):
        mine, forward = arrived
        landed = _forward_wait(forward, after, f"{name}_forward_wait")
        return _place_own(landed, mine, me.astype(jnp.int32).reshape(1), f"{name}_own")

    w_fin_t = _w_ffn_in_view(w_ffn_in)
    shards = [[t.astype(BF16) for t in (w_in[l], w_branch[l], w_out[l], w_fin_t[l], w_ffn_out[l])]
              for l in range(depth)]
    c_all = _small_all_gather(c.reshape(8, 128), "comm_gather_c").reshape(N_DEV, D)
    mod_cols = _ada_fwd(c_all, w_ada, "ada_fwd")
    mod_all = _small_all_gather(mod_cols.reshape(-1, 128), "comm_gather_mod")
    gather_in0 = _exchange_start(shards[0][:1], False, mod_all, "comm_gather_w_in0_start", SAME_CORE)
    gather_rest0 = _exchange_start(shards[0][1:], False, gather_in0[4], "comm_gather_rest0_start", SAME_CORE)
    gather1 = _exchange_start(shards[1], False, gather_rest0[4], "comm_gather_weights1_start", SAME_CORE)
    started = gather1[4][0:1, 0:1]
    W_in, W_branch, W_out, W_fin, W_fout = ([None, None] for _ in range(5))
    mod_all = mod_all.reshape(N_DEV, depth, N_DEV, w_ada.shape[2])
    mod_mine = lax.dynamic_index_in_dim(mod_all, me, axis=2, keepdims=False)
    mod = jnp.transpose(mod_mine, (1, 0, 2)).reshape(depth, 6 * D) + b_ada + started
    mods = [[mod[l:l + 1, k * D:(k + 1) * D] for k in range(6)] for l in range(depth)]
    rel_tiles = [_rel_expand(_rel_bases(rel_bias[l]) + started, f"rel_expand{l}") for l in range(depth)]

    slopes = jnp.exp2(-jnp.arange(1, 9, dtype=F32))
    saved = []
    xs = x0
    for l in range(depth):
        if l == 1:
            g_in1, *g_rest1 = finish_gather(arrived1, xs, "comm_gather_weights1")
            W_in[1] = _w_in_rearranged(g_in1, "w_in_rearrange1")
            W_branch[1], W_out[1], W_fin[1], W_fout[1] = rest_matrices(*g_rest1)
        sh_m, sc_m, g_m, sh_f, sc_f, g_f = mods[l]
        gm, gf = norm_mix_g[l:l + 1], norm_ffn_g[l:l + 1]
        bfor = _pad_lanes(b_forget[l:l + 1], BLK)
        h = _norm_mod_fwd(xs, gm, sh_m, sc_m, f"norm_mix_fwd{l}")
        tiles, tiles_t = rel_tiles[l]
        if l == 0:
            arrived_in0 = arrive(gather_in0, rel_tiles[-1][1], "comm_gather_w_in0")
            W_in[0] = _w_in_rearranged(finish_gather(arrived_in0, h, "comm_gather_w_in0")[0], "w_in_rearrange0")
        qkv, gates = _project(h, W_in[l], f"proj{l}")
        fb = _matmul(h, W_in[l], "nn", F32, f"proj_forget{l}", TILES["proj_forget"], n=BLK, b_off=N_MAIN // BLK)
        cum = _forget_fwd(fb, bfor, f"forget_fwd{l}")[:, :N_FORGET]
        cum_col, cum_row = _pairs_col(cum), _pairs_row(cum)
        o_a, lse_a = _attn_fwd("a", qkv, f"attn_a_fwd{l}", sinks=sinks[l], slopes=slopes)
        o_b, lse_b = _attn_fwd("b", qkv, f"attn_b_fwd{l}", cq_col=cum_col, ck_row=cum_row)
        arrived_rest0 = arrive(gather_rest0, o_b, "comm_gather_rest0") if l == 0 else None
        o_c, lse_c = _attn_fwd("c", qkv, f"attn_c_fwd{l}", bias=tiles, after=arrived_rest0[1][3] if l == 0 else None)
        if l == 0:
            W_branch[0], W_out[0], W_fin[0], W_fout[0] = rest_matrices(
                *finish_gather(arrived_rest0, o_c, "comm_gather_rest0"))
        x1, merged, mix = _merge_fwd(o_a, o_b, o_c, gates, W_branch[l], W_out[l], xs, g_m, f"merge_fwd{l}")
        h2 = _norm_mod_fwd(x1, gf, sh_f, sc_f, f"norm_ffn_fwd{l}")
        act = _ffn_in_fwd(h2, W_fin[l], f"ffn_in_fwd{l}")
        if l == 0:
            arrived1 = arrive(gather1, act, "comm_gather_weights1")
        x2, ffn = _matmul_resid(act, W_fout[l], x1, g_f, f"ffn_out{l}", TILES["ffn_out"],
                                after=arrived1[1][3] if l == 0 else None)
        saved.append(dict(x=xs, h=h, qkv=qkv, gates=gates, fb=fb, bfor=bfor, cum_col=cum_col, cum_row=cum_row,
                          tiles_t=tiles_t, o=(o_a, o_b, o_c), lse=(lse_a, lse_b, lse_c), merged=merged, mix=mix,
                          x1=x1, h2=h2, act=act, ffn=ffn))
        xs = x2

    dx, loss_tile, d_final_g, d_g_f, df = _final_loss(
        xs, loss_target[0], final_norm_g.reshape(1, D), (saved[-1]["ffn"], mods[-1][5]), "final_loss")

    grads = {k: [None] * depth for k in ("w_in", "w_branch", "w_out", "w_ffn_in", "w_ffn_out", "norm_mix_g",
                                          "norm_ffn_g", "b_forget", "sinks", "rel_bias", "dmod")}
    def rest_pieces(l):
        return [_branch_pieces(grads["w_branch"][l]), _row_pieces(grads["w_out"][l]),
                _row_pieces(grads["w_ffn_in"][l]), _row_pieces(grads["w_ffn_out"][l])]

    reduce1 = reduce_rest0 = reduce_in0 = None
    for l in reversed(range(depth)):
        sv = saved[l]
        sh_m, sc_m, g_m, sh_f, sc_f, g_f = mods[l]
        gm, gf = norm_mix_g[l:l + 1], norm_ffn_g[l:l + 1]
        du_g, du_u = _ffn_mid_bwd(sv["h2"], df, W_fin[l], W_fout[l], f"ffn_mid_bwd{l}")
        du = jnp.concatenate([du_g, du_u], axis=1)
        grads["w_ffn_out"][l] = _matmul(sv["act"], df, "tn", BF16, f"wgrad_ffn_out{l}", TILES["wgrad_ffn_out"])
        grads["w_ffn_in"][l] = _matmul(du, sv["h2"], "tn", BF16, f"wgrad_ffn_in{l}", TILES["wgrad_ffn_in"])
        dh2 = _matmul(du, W_fin[l], "nn", F32, f"dgrad_ffn_in{l}", TILES["dgrad_ffn_in"])
        dx1, d_sh_f, d_sc_f, d_gf, d_g_m, dmix = _norm_mod_bwd(sv["x1"], dh2, dx, gf, sc_f, f"norm_ffn_bwd{l}",
                                                               below=(sv["mix"], g_m))
        grads["w_out"][l] = _matmul(sv["merged"], dmix, "tn", BF16, f"wgrad_out{l}", TILES["wgrad_out"])
        o_a, o_b, o_c = sv["o"]
        dgates, d_w_branch, do_a, do_b, do_c, dl_a, dl_b, dl_c = _merge_bwd(
            dmix, o_a, o_b, o_c, sv["gates"], W_branch[l], W_out[l], f"merge_bwd{l}")
        grads["w_branch"][l] = d_w_branch.astype(BF16)
        lse_rows = [_pairs_row(_heads_from_col(t)) for t in sv["lse"]]
        if l == 0:
            reduce_rest0 = _exchange_start(rest_pieces(0), True, dgates, "comm_reduce_rest0_start")
            lse_rows = [t + reduce_rest0[4][0:1, 0:1] for t in lse_rows]
        dq_a, dk_a, dv_a, dsink = _attn_bwd("a", sv["qkv"], do_a, lse_rows[0], _pairs_row(dl_a), f"attn_a_bwd{l}",
                                            sinks=sinks[l], slopes=slopes)
        dq_b, dk_b, dv_b, dck, dcq = _attn_bwd("b", sv["qkv"], do_b, lse_rows[1], _pairs_row(dl_b),
                                               f"attn_b_bwd{l}", cq_row=sv["cum_row"], ck_col=sv["cum_col"])
        dq_c, dk_c, dv_c, dtiles_t = _attn_bwd("c", sv["qkv"], do_c, lse_rows[2], _pairs_row(dl_c),
                                               f"attn_c_bwd{l}", bias_t=sv["tiles_t"])
        grads["sinks"][l] = dsink[:, :2, 0].reshape(8)
        grads["rel_bias"][l] = _rel_reduce(dtiles_t, f"rel_reduce{l}")[:, 0, :N_REL]
        dcum_k = _pad_lanes(_heads_from_col(dck), BLK)
        dcum_q = _pad_lanes(_heads_from_row(dcq), BLK)
        dfb, d_bfor = _forget_bwd(dcum_q, dcum_k, sv["fb"], sv["bfor"], f"forget_bwd{l}")
        grads["b_forget"][l] = d_bfor[0, :N_FORGET]
        dproj = jnp.concatenate(
            [t.astype(BF16) for t in (dq_a, dk_a, dv_a, dq_b, dk_b, dv_b, dq_c, dk_c, dv_c, dgates, dfb)],
            axis=1)
        grads["w_in"][l] = _matmul(sv["h"], dproj, "tn", BF16, f"wgrad_in{l}", TILES["wgrad_in"])
        if l == 1:
            reduce1 = _exchange_start([_w_in_pieces(grads["w_in"][1], "w_in_pieces1")] + rest_pieces(1), True, dproj,
                                      "comm_reduce1_start")
        dh = _matmul(dproj, W_in[l], "nt", F32, f"dgrad_in{l}", TILES["dgrad_in"], after=reduce1[4] if l == 1 else None)
        d_g_f_here = d_g_f
        if l > 0:
            dx, d_sh_m, d_sc_m, d_gm, d_g_f, df = _norm_mod_bwd(sv["x"], dh, dx1, gm, sc_m, f"norm_mix_bwd{l}",
                                                                below=(saved[l - 1]["ffn"], mods[l - 1][5]))
        else:
            dx, d_sh_m, d_sc_m, d_gm = _norm_mod_bwd(sv["x"], dh, dx1, gm, sc_m, f"norm_mix_bwd{l}")
        grads["norm_mix_g"][l] = d_gm[0]
        grads["norm_ffn_g"][l] = d_gf[0]
        grads["dmod"][l] = jnp.concatenate([d_sh_m, d_sc_m, d_g_m, d_sh_f, d_sc_f, d_g_f_here], axis=1)[0]

    grad_x = dx.reshape(x.shape)

    small_shapes = dict(dmod=b_ada.shape, norm_mix_g=norm_mix_g.shape, norm_ffn_g=norm_ffn_g.shape,
                        final_norm_g=final_norm_g.shape, b_forget=b_forget.shape, sinks=sinks.shape,
                        rel_bias=rel_bias.shape, loss=())
    mine_small = _pack_small(dict(
        loss=_pad_lanes(loss_tile[0:1, 0:1], 128),
        dmod=jnp.stack(grads["dmod"]), norm_mix_g=jnp.stack(grads["norm_mix_g"]),
        norm_ffn_g=jnp.stack(grads["norm_ffn_g"]), final_norm_g=d_final_g[0],
        b_forget=_pad_lanes(jnp.stack(grads["b_forget"]).reshape(1, -1), 128),
        sinks=_pad_lanes(jnp.stack(grads["sinks"]).reshape(1, -1), 128),
        rel_bias=_pad_lanes(jnp.stack(grads["rel_bias"]).reshape(1, -1), 4224)))
    all_small = _small_all_gather(mine_small, "comm_gather_small").reshape(N_DEV, SMALL_ROWS, 128)
    pieces_in0 = _w_in_pieces(grads["w_in"][0], "w_in_pieces0", pair_major=True)
    from_sibling = _sibling_exchange([pieces_in0], "comm_reduce_in0_sibling")[0]
    pair_sum_in0 = _pair_add(pieces_in0, from_sibling, pc.astype(jnp.int32).reshape(1), "pair_add_in0")
    reduce_in0 = _exchange_start([pair_sum_in0], True, all_small, "comm_reduce_in0_start", CHIPS)
    in0_started = reduce_in0[4]

    def pack_params(b_ada_, nm, nf, fn, bf, sk, rb):
        return _pack_small(dict(dmod=b_ada_, norm_mix_g=nm, norm_ffn_g=nf, final_norm_g=fn, loss=jnp.zeros((1, 128), F32),
                                b_forget=_pad_lanes(bf.reshape(1, -1), 128), sinks=_pad_lanes(sk.reshape(1, -1), 128),
                                rel_bias=_pad_lanes(rb.reshape(1, -1), 4224)))

    small_out = _adamw(
        pack_params(b_ada, norm_mix_g, norm_ffn_g, final_norm_g, b_forget, sinks, rel_bias)[None],
        pack_params(m_b_ada, m_norm_mix_g, m_norm_ffn_g, m_final_norm_g, m_b_forget, m_sinks, m_rel_bias)[None],
        pack_params(v_b_ada, v_norm_mix_g, v_norm_ffn_g, v_final_norm_g, v_b_forget, v_sinks, v_rel_bias)[None],
        [all_small], "adamw_small", me_arr, after=in0_started)
    small_out = [_unpack_small(t[0], small_shapes) for t in small_out]

    dmod_all = all_small[:, :96].reshape(N_DEV, depth, 6 * D)
    dmod_cols = lax.dynamic_slice_in_dim(dmod_all, me * w_ada.shape[2], w_ada.shape[2], axis=2)
    d_w_ada = _ada_bwd(jnp.transpose(c_all), jnp.transpose(dmod_cols, (1, 0, 2)), "ada_bwd")

    big = {"w_ada": _adamw(w_ada, m_w_ada, v_w_ada, [d_w_ada[l:l + 1] for l in range(depth)], "adamw_w_ada", me_arr,
                           after=in0_started)}
    sent1, landed1 = _exchange_wait(reduce1, True, big["w_ada"][0], "comm_reduce1_wait")
    sent_rest0, landed_rest0 = _exchange_wait(reduce_rest0, True, landed1[0], "comm_reduce_rest0_wait")
    parts = {"w_in": [None, (landed1[0], sent1[0])]}
    for a, name in enumerate(("w_branch", "w_out", "w_ffn_in", "w_ffn_out")):
        parts[name] = [(landed_rest0[a], sent_rest0[a]), (landed1[1 + a], sent1[1 + a])]

    def update(name, w, m, v, view=lambda t: t):
        per_layer = lambda t: t.reshape(depth, -1, t.shape[-1])
        outs = _adamw(*[per_layer(view(t)) for t in (w, m, v)], parts[name], f"adamw_{name}",
                      me_in_arr if name == "w_in" else me_arr)
        big[name] = [view(t).reshape(w.shape) for t in outs]

    update("w_ffn_in", w_ffn_in, m_w_ffn_in, v_w_ffn_in, _w_ffn_in_view)
    update("w_ffn_out", w_ffn_out, m_w_ffn_out, v_w_ffn_out)
    update("w_branch", w_branch, m_w_branch, v_w_branch)
    update("w_out", w_out, m_w_out, v_w_out)
    sent_in0, landed_in0 = _exchange_wait(reduce_in0, True, big["w_out"][0], "comm_reduce_in0_wait", CHIPS)
    parts["w_in"][0] = (landed_in0[0], sent_in0[0])
    update("w_in", w_in, m_w_in, v_w_in)

    def leaf(kind, name):
        if name in big:
            return big[name][kind]
        return small_out[kind]["dmod" if name == "b_ada" else name]

    order = ["norm_mix_g", "norm_ffn_g", "w_ada", "b_ada", "w_in", "b_forget", "sinks", "rel_bias", "w_branch",
             "w_out", "w_ffn_in", "w_ffn_out", "final_norm_g"]
    loss = small_out[0]["loss"]
    return (loss, grad_x, *[leaf(0, n) for n in order], *[leaf(1, n) for n in order],
            *[leaf(2, n) for n in order], *[leaf(3, n) for n in order])
```

```python
import functools

import jax
import jax.numpy as jnp
from jax import lax
from jax.experimental import pallas as pl
from jax.experimental.pallas import tpu as pltpu

F32 = jnp.float32
BF16 = jnp.bfloat16
NEG_INF = -1e30
EPS = 1e-6
N_DEV = 8
BLK = 128
GROUP = 4 * BLK
VMEM_LIMIT_BYTES = 56 * 1024 * 1024

D_MODEL = 1024
N_QKV = 3840
N_GATES = 3072
N_MAIN = N_QKV + N_GATES
N_FORGET = 8
N_IN = N_MAIN + N_FORGET
F_COL = 2304
FFN_HIDDEN = 2816
N_REL = 257

ADAM_LR, ADAM_B1, ADAM_B2, ADAM_EPS, ADAM_WD, ADAM_STEP = 0.001, 0.9, 0.999, 1e-08, 0.01, 10

NN = (((1,), (0,)), ((), ()))
NT = (((1,), (1,)), ((), ()))
TN = (((0,), (0,)), ((), ()))
HIGHEST = lax.Precision.HIGHEST

ATTN_COLS = {"a": (0, 4, 5), "b": (6, 10, 14), "c": (18, 22, 26)}
ATTN_WINDOW = {"a": 2, "c": 5}
ATTN_BLOCKS_PER_STEP = {"a": 8, "b": GROUP // BLK, "c": 2}
ROW_TILE = 512


def _params():
    return pltpu.CompilerParams(vmem_limit_bytes=VMEM_LIMIT_BYTES)


def _tile(n, target):
    best = None
    t = 128
    while t <= min(n, target):
        if n % t == 0:
            best = t
        t += 128
    return best if best is not None else n


def _row_tile(n, target):
    t = min(n, target)
    while n % t:
        t -= 8
    return t


TILES = {
    "proj": (1024, 768, 1024), "proj_forget": (1024, 128, 1024),
    "ffn_out": (1024, 512, 2816), "ffn_fused": (512, 1408),
    "wgrad_ffn_out": (1408, 1024, 2048), "wgrad_ffn_in": (1408, 1024, 2048), "dgrad_ffn_in": (1024, 1024, 2816),
    "wgrad_out": (1024, 1024, 2048),
    "wgrad_in": (1024, 1408, 2048), "dgrad_in": (1024, 1024, 3520),
}


def _matmul(a, b, mode, out_dtype, name, tiles, *, n=None, a_off=0, b_off=0, m=None, after=None):
    tm, tn, tk = tiles
    if mode == "nn":
        M, K = a.shape if m is None else (m, a.shape[1])
        N = b.shape[1] if n is None else n
    elif mode == "nt":
        M, K = a.shape
        N = b.shape[0] if n is None else n
    else:
        K = a.shape[0]
        M = a.shape[1] if m is None else m
        N = b.shape[1] if n is None else n
    tm = _tile(M, tm) if M % 128 == 0 else M
    tn = _tile(N, tn)
    tk = _tile(K, tk)
    nk = K // tk
    dims = {"nn": NN, "nt": NT, "tn": TN}[mode]
    if mode == "nn":
        a_spec = pl.BlockSpec((tm, tk), lambda i, j, k: (i + a_off, k))
        b_spec = pl.BlockSpec((tk, tn), lambda i, j, k: (k, j + b_off))
    elif mode == "nt":
        a_spec = pl.BlockSpec((tm, tk), lambda i, j, k: (i + a_off, k))
        b_spec = pl.BlockSpec((tn, tk), lambda i, j, k: (j + b_off, k))
    else:
        a_spec = pl.BlockSpec((tk, tm), lambda i, j, k: (k, i + a_off))
        b_spec = pl.BlockSpec((tk, tn), lambda i, j, k: (k, j + b_off))

    def body(a_ref, b_ref, *rest):
        o_ref, acc_ref = rest[-2:]
        k = pl.program_id(2)
        part = lax.dot_general(a_ref[...], b_ref[...], dims, preferred_element_type=F32)
        if nk == 1:
            o_ref[...] = part.astype(o_ref.dtype)
        else:
            @pl.when(k == 0)
            def _():
                acc_ref[...] = part

            @pl.when(k > 0)
            def _():
                acc_ref[...] += part

            @pl.when(k == nk - 1)
            def _():
                o_ref[...] = acc_ref[...].astype(o_ref.dtype)

    return pl.pallas_call(
        body, name=name,
        out_shape=jax.ShapeDtypeStruct((M, N), out_dtype),
        grid=(M // tm, N // tn, nk),
        in_specs=[a_spec, b_spec] + ([ANY] if after is not None else []),
        out_specs=pl.BlockSpec((tm, tn), lambda i, j, k: (i, j)),
        scratch_shapes=[pltpu.VMEM((tm, tn) if nk > 1 else (8, 128), F32)],
        compiler_params=_params(),
    )(a, b, *([after] if after is not None else []))


def _project(h, w, name):
    S, D = h.shape
    tm, tn, _ = TILES["proj"]
    tm = _tile(S, tm)
    nq, ng = N_QKV // tn, N_GATES // tn

    def body(h_ref, w_ref, q_ref, g_ref):
        j = pl.program_id(1)
        acc = jnp.dot(h_ref[...], w_ref[...], preferred_element_type=F32)

        @pl.when(j < nq)
        def _():
            q_ref[...] = acc.astype(BF16)

        @pl.when(j >= nq)
        def _():
            g_ref[...] = acc

    return pl.pallas_call(
        body, name=name,
        out_shape=(jax.ShapeDtypeStruct((S, N_QKV), BF16), jax.ShapeDtypeStruct((S, N_GATES), F32)),
        grid=(S // tm, nq + ng),
        in_specs=[pl.BlockSpec((tm, D), lambda i, j: (i, 0)), pl.BlockSpec((D, tn), lambda i, j: (0, j))],
        out_specs=(pl.BlockSpec((tm, tn), lambda i, j: (i, jnp.minimum(j, nq - 1))),
                   pl.BlockSpec((tm, tn), lambda i, j: (i, jnp.maximum(j - nq, 0)))),
        compiler_params=_params(),
    )(h, w)


def _matmul_resid(a, b, resid, gate, name, tiles, after=None):
    M, K = a.shape
    N = b.shape[1]
    tm, tn, tk = (_tile(d, t) for d, t in zip((M, N, K), tiles))
    nk = K // tk

    def body(a_ref, b_ref, r_ref, g_ref, *rest):
        o_ref, s_ref, acc_ref = rest[-3:]
        k = pl.program_id(2)
        part = jnp.dot(a_ref[...], b_ref[...], preferred_element_type=F32)

        def finish(acc):
            o_ref[...] = r_ref[...] + g_ref[...] * acc
            s_ref[...] = acc.astype(BF16)

        if nk == 1:
            finish(part)
        else:
            @pl.when(k == 0)
            def _():
                acc_ref[...] = part

            @pl.when(k > 0)
            def _():
                acc_ref[...] += part

            @pl.when(k == nk - 1)
            def _():
                finish(acc_ref[...])

    return pl.pallas_call(
        body, name=name,
        out_shape=(jax.ShapeDtypeStruct((M, N), F32), jax.ShapeDtypeStruct((M, N), BF16)),
        grid=(M // tm, N // tn, nk),
        in_specs=[pl.BlockSpec((tm, tk), lambda i, j, k: (i, k)),
                  pl.BlockSpec((tk, tn), lambda i, j, k: (k, j)),
                  pl.BlockSpec((tm, tn), lambda i, j, k: (i, j)),
                  pl.BlockSpec((1, tn), lambda i, j, k: (0, j))] + ([ANY] if after is not None else []),
        out_specs=(pl.BlockSpec((tm, tn), lambda i, j, k: (i, j)),
                   pl.BlockSpec((tm, tn), lambda i, j, k: (i, j))),
        scratch_shapes=[pltpu.VMEM((tm, tn) if nk > 1 else (8, 128), F32)],
        compiler_params=_params(),
    )(a, b, resid, gate, *([after] if after is not None else []))


def _norm_mod_fwd(x, g, shift, scale, name):
    S, D = x.shape
    ts = _row_tile(S, ROW_TILE)

    def body(x_ref, g_ref, sh_ref, sc_ref, h_ref):
        xv = x_ref[...]
        rstd = lax.rsqrt(jnp.mean(xv * xv, axis=-1, keepdims=True) + EPS)
        y = xv * rstd * g_ref[...]
        h_ref[...] = (y * (1.0 + sc_ref[...]) + sh_ref[...]).astype(BF16)

    row = pl.BlockSpec((1, D), lambda i: (0, 0))
    return pl.pallas_call(
        body, name=name, out_shape=jax.ShapeDtypeStruct((S, D), BF16), grid=(S // ts,),
        in_specs=[pl.BlockSpec((ts, D), lambda i: (i, 0)), row, row, row],
        out_specs=pl.BlockSpec((ts, D), lambda i: (i, 0)),
        compiler_params=_params(),
    )(x, g, shift, scale)


def _accumulate_rows(i, pairs):
    @pl.when(i == 0)
    def _():
        for ref, value in pairs:
            ref[...] = value

    @pl.when(i > 0)
    def _():
        for ref, value in pairs:
            ref[...] += value


def _gated_residual_bwd(dx, f_ref, gate_ref, df_ref):
    df_ref[...] = (dx * gate_ref[...]).astype(BF16)
    return jnp.sum(dx * f_ref[...].astype(F32), axis=0, keepdims=True)


def _norm_mod_bwd(x, dh, dres, g, scale, name, below=None):
    S, D = x.shape
    ts = _row_tile(S, ROW_TILE)

    def body(x_ref, dh_ref, dr_ref, g_ref, sc_ref, *rest):
        i = pl.program_id(0)
        xv, dhv, gv = x_ref[...], dh_ref[...], g_ref[...]
        rstd = lax.rsqrt(jnp.mean(xv * xv, axis=-1, keepdims=True) + EPS)
        xhat = xv * rstd
        dn = dhv * (1.0 + sc_ref[...])
        dxhat = dn * gv
        proj = jnp.mean(dxhat * xhat, axis=-1, keepdims=True)
        dx = dr_ref[...] + rstd * (dxhat - xhat * proj)
        sums = [jnp.sum(dhv, axis=0, keepdims=True), jnp.sum(dhv * (xhat * gv), axis=0, keepdims=True),
                jnp.sum(dn * xhat, axis=0, keepdims=True)]
        if below is None:
            dx_ref, *sum_refs = rest
        else:
            f_ref, gate_ref, dx_ref, *sum_refs, df_ref = rest
            sums.append(_gated_residual_bwd(dx, f_ref, gate_ref, df_ref))
        dx_ref[...] = dx
        _accumulate_rows(i, list(zip(sum_refs, sums)))

    tile = pl.BlockSpec((ts, D), lambda i: (i, 0))
    row = pl.BlockSpec((1, D), lambda i: (0, 0))
    vec = jax.ShapeDtypeStruct((1, D), F32)
    fused = below is not None
    return pl.pallas_call(
        body, name=name,
        out_shape=(jax.ShapeDtypeStruct((S, D), F32), vec, vec, vec)
        + ((vec, jax.ShapeDtypeStruct((S, D), BF16)) if fused else ()),
        grid=(S // ts,),
        in_specs=[tile, tile, tile, row, row] + ([tile, row] if fused else []),
        out_specs=(tile, row, row, row) + ((row, tile) if fused else ()),
        compiler_params=_params(),
    )(x, dh, dres, g, scale, *(below if fused else ()))


def _ffn_in_fwd(h, w_t, name):
    S, D = h.shape
    F = w_t.shape[0] // 2
    tm, tn = _tile(S, TILES["ffn_fused"][0]), _tile(F, TILES["ffn_fused"][1])
    nj = F // tn

    def body(h_ref, wg_ref, wu_ref, o_ref):
        hv = h_ref[...]
        ug = lax.dot_general(hv, wg_ref[...], NT, preferred_element_type=F32)
        uu = lax.dot_general(hv, wu_ref[...], NT, preferred_element_type=F32)
        o_ref[...] = (ug * jax.nn.sigmoid(ug) * uu).astype(BF16)

    return pl.pallas_call(
        body, name=name, out_shape=jax.ShapeDtypeStruct((S, F), BF16), grid=(nj, S // tm),
        in_specs=[pl.BlockSpec((tm, D), lambda j, i: (i, 0)),
                  pl.BlockSpec((tn, D), lambda j, i: (j, 0)),
                  pl.BlockSpec((tn, D), lambda j, i: (j + nj, 0))],
        out_specs=pl.BlockSpec((tm, tn), lambda j, i: (i, j)),
        compiler_params=_params(),
    )(h, w_t, w_t)


def _ffn_mid_bwd(h, df, w_in_t, w_out, name):
    S, D = h.shape
    F = w_in_t.shape[0] // 2
    tm, tn = _tile(S, TILES["ffn_fused"][0]), _tile(F, TILES["ffn_fused"][1])
    nj = F // tn

    def body(h_ref, df_ref, wg_ref, wu_ref, wo_ref, dg_ref, du_ref):
        hv = h_ref[...]
        ug = lax.dot_general(hv, wg_ref[...], NT, preferred_element_type=F32)
        uu = lax.dot_general(hv, wu_ref[...], NT, preferred_element_type=F32)
        dact = lax.dot_general(df_ref[...], wo_ref[...], NT, preferred_element_type=F32)
        sig = jax.nn.sigmoid(ug)
        dg_ref[...] = (dact * uu * (sig * (1.0 + ug * (1.0 - sig)))).astype(BF16)
        du_ref[...] = (dact * (ug * sig)).astype(BF16)

    out = jax.ShapeDtypeStruct((S, F), BF16)
    return pl.pallas_call(
        body, name=name, out_shape=(out, out), grid=(nj, S // tm),
        in_specs=[pl.BlockSpec((tm, D), lambda j, i: (i, 0)),
                  pl.BlockSpec((tm, D), lambda j, i: (i, 0)),
                  pl.BlockSpec((tn, D), lambda j, i: (j, 0)),
                  pl.BlockSpec((tn, D), lambda j, i: (j + nj, 0)),
                  pl.BlockSpec((tn, D), lambda j, i: (j, 0))],
        out_specs=(pl.BlockSpec((tm, tn), lambda j, i: (i, j)), pl.BlockSpec((tm, tn), lambda j, i: (i, j))),
        compiler_params=_params(),
    )(h, df, w_in_t, w_in_t, w_out)


def _merge_fwd(o_a, o_b, o_c, gates, w_branch, w_out, resid, gate, name, *, tm=512):
    S, W = o_a.shape
    D = w_branch.shape[2]
    tm = _row_tile(S, tm)

    def body(oa_ref, ob_ref, oc_ref, g_ref, w_ref, wo_ref, r_ref, gm_ref, x_ref, m_ref, mix_ref):
        acc = None
        for k, o_ref in enumerate((oa_ref, ob_ref, oc_ref)):
            y = jnp.dot(o_ref[...], w_ref[k], preferred_element_type=F32)
            t = jax.nn.sigmoid(g_ref[:, k * D:(k + 1) * D]) * y
            acc = t if acc is None else acc + t
        merged = acc.astype(BF16)
        m_ref[...] = merged
        mix = jnp.dot(merged, wo_ref[...], preferred_element_type=F32)
        x_ref[...] = r_ref[...] + gm_ref[...] * mix
        mix_ref[...] = mix.astype(BF16)

    o_spec = pl.BlockSpec((tm, W), lambda i: (i, 0))
    tile = pl.BlockSpec((tm, D), lambda i: (i, 0))
    return pl.pallas_call(
        body, name=name,
        out_shape=(jax.ShapeDtypeStruct((S, D), F32), jax.ShapeDtypeStruct((S, D), BF16), jax.ShapeDtypeStruct((S, D), BF16)),
        grid=(S // tm,),
        in_specs=[o_spec, o_spec, o_spec, pl.BlockSpec((tm, 3 * D), lambda i: (i, 0)),
                  pl.BlockSpec((3, W, D), lambda i: (0, 0, 0)), pl.BlockSpec((D, D), lambda i: (0, 0)),
                  tile, pl.BlockSpec((1, D), lambda i: (0, 0))],
        out_specs=(tile, tile, tile),
        compiler_params=_params(),
    )(o_a, o_b, o_c, gates, w_branch, w_out, resid, gate)


def _merge_bwd(dmix, o_a, o_b, o_c, gates, w_branch, w_out, name, *, tm=256):
    S, W = o_a.shape
    D = w_branch.shape[2]
    tm = _row_tile(S, tm)
    n_heads = W // 64

    def body(dm_ref, oa_ref, ob_ref, oc_ref, g_ref, w_ref, wo_ref, dg_ref, dw_ref,
             doa_ref, dob_ref, doc_ref, dla_ref, dlb_ref, dlc_ref):
        first = pl.program_id(0) == 0
        head_of_column = (lax.broadcasted_iota(jnp.int32, (W, BLK), 0) // 64
                          == lax.broadcasted_iota(jnp.int32, (W, BLK), 1)).astype(F32)
        dm = lax.dot_general(dm_ref[...], wo_ref[...], NT, preferred_element_type=F32)
        branches = ((oa_ref, doa_ref, dla_ref), (ob_ref, dob_ref, dlb_ref), (oc_ref, doc_ref, dlc_ref))
        for k, (o_ref, do_ref, dl_ref) in enumerate(branches):
            wk = w_ref[k]
            ov = o_ref[...]
            y = jnp.dot(ov, wk, preferred_element_type=F32)
            g = jax.nn.sigmoid(g_ref[:, k * D:(k + 1) * D])
            dy = (dm * g).astype(BF16)
            dwk = lax.dot_general(ov, dy, TN, preferred_element_type=F32)

            @pl.when(first)
            def _(k=k, dwk=dwk):
                dw_ref[k] = dwk

            @pl.when(jnp.logical_not(first))
            def _(k=k, dwk=dwk):
                dw_ref[k] += dwk
            dg_ref[:, k * D:(k + 1) * D] = (dm * y * (g * (1.0 - g))).astype(BF16)
            do16 = lax.dot_general(dy, wk, NT, preferred_element_type=F32).astype(BF16)
            do_ref[...] = do16
            prod = do16.astype(F32) * ov.astype(F32)
            sums = jnp.dot(prod, head_of_column, preferred_element_type=F32, precision=HIGHEST)
            dl_ref[...] = jnp.transpose(sums)[:n_heads, :]

    o_spec = pl.BlockSpec((tm, W), lambda i: (i, 0))
    wide = pl.BlockSpec((tm, 3 * D), lambda i: (i, 0))
    dl_spec = pl.BlockSpec((n_heads, tm), lambda i: (0, i))
    o_out = jax.ShapeDtypeStruct((S, W), BF16)
    wide_out = jax.ShapeDtypeStruct((S, 3 * D), BF16)
    dl_out = jax.ShapeDtypeStruct((n_heads, S), F32)
    whole = pl.BlockSpec((3, W, D), lambda i: (0, 0, 0))
    return pl.pallas_call(
        body, name=name,
        out_shape=(wide_out, jax.ShapeDtypeStruct((3, W, D), F32), o_out, o_out, o_out, dl_out, dl_out, dl_out),
        grid=(S // tm,),
        in_specs=[pl.BlockSpec((tm, D), lambda i: (i, 0)), o_spec, o_spec, o_spec, wide, whole,
                  pl.BlockSpec((D, D), lambda i: (0, 0))],
        out_specs=(wide, whole, o_spec, o_spec, o_spec, dl_spec, dl_spec, dl_spec),
        compiler_params=_params(),
    )(dmix, o_a, o_b, o_c, gates, w_branch, w_out)


def _band_mask(variant, t_abs, s_abs):
    if variant == "b":
        return s_abs <= t_abs
    qc, kc = t_abs >> 6, s_abs >> 6
    return (kc <= qc) & (kc >= qc - (2 if variant == "a" else 8))


def _attn_fwd(variant, qkv, name, *, sinks=None, slopes=None, cq_col=None, ck_row=None, bias=None, after=None):
    S = qkv.shape[0]
    nb = S // BLK
    qb, kb, vb = ATTN_COLS[variant]
    shared_kv = variant == "a"
    win = ATTN_WINDOW.get(variant)
    per_step = ATTN_BLOCKS_PER_STEP[variant]

    def body(*refs):
        if after is not None:
            refs = refs[:-3] + refs[-2:]
        if variant == "a":
            q_ref, k_ref, v_ref, sink_ref, slope_ref, o_ref, lse_ref = refs
        elif variant == "b":
            q_ref, k_ref, v_ref, cq_ref, ck_ref, o_ref, lse_ref = refs
        else:
            q_ref, k_ref, v_ref, bias_ref, o_ref, lse_ref = refs
        p = pl.program_id(0)
        lane = lax.broadcasted_iota(jnp.int32, (1, BLK), 1)

        def compute(i, rows, start, n_keys):
            n_rows = rows.stop - rows.start
            t_abs = i * BLK + lax.broadcasted_iota(jnp.int32, (n_rows, 1), 0)
            q2 = q_ref[rows, :].astype(F32) * 0.125
            k_w = k_ref[pl.ds(start, n_keys), :]
            v_w = v_ref[pl.ds(start, n_keys), :]
            s_abs = start + lax.broadcasted_iota(jnp.int32, (1, n_keys), 1)
            valid = _band_mask(variant, t_abs, s_abs)
            outs = []
            for half in (0, 1):
                hmask = (lane >= 64) if half else (lane < 64)
                qh = jnp.where(hmask, q2, 0.0)
                if shared_kv:
                    swap = (p // 2) != half
                    qh = jnp.where(swap, pltpu.roll(qh, 64, 1), qh)
                s = lax.dot_general(qh.astype(BF16), k_w, NT, preferred_element_type=F32)
                if variant == "a":
                    head = 2 * p + half
                    s = s + (-slope_ref[head]) * jnp.abs(t_abs - s_abs).astype(F32)
                elif variant == "b":
                    s = s + cq_ref[rows, half:half + 1] - ck_ref[half:half + 1, pl.ds(start, n_keys)]
                else:
                    j0 = start // BLK
                    s = s + jnp.concatenate([jnp.concatenate(
                        [bias_ref[half, jnp.clip(i + r - j0 - b, 0, 4)] for b in range(n_keys // BLK)], axis=1)
                        for r in range(n_rows // BLK)], axis=0)
                s = jnp.where(valid, s, NEG_INF)
                m = jnp.max(s, axis=1, keepdims=True)
                if variant == "a":
                    m = jnp.maximum(m, sink_ref[head])
                pe = jnp.exp(s - m)
                l = jnp.sum(pe, axis=1, keepdims=True)
                if variant == "a":
                    l = l + jnp.exp(sink_ref[head] - m)
                out = jnp.dot(pe.astype(BF16), v_w, preferred_element_type=F32) / l
                if shared_kv:
                    out = jnp.where(swap, pltpu.roll(out, 64, 1), out)
                outs.append(out)
                lse = jnp.transpose(jnp.broadcast_to(m + jnp.log(l), (n_rows, BLK)))
                lse_ref[half:half + 1, rows] = lse[0:1, :]
            o_ref[rows, :] = jnp.where(lane < 64, outs[0], outs[1]).astype(BF16)

        step = pl.program_id(1)
        if variant == "b":
            for g in range(S // GROUP):
                pl.when(step == g)(functools.partial(compute, step * per_step, slice(0, GROUP), 0, (g + 1) * GROUP))
        elif variant == "c":
            span = win + per_step - 1
            start = jnp.clip(step * per_step - (win - 1), 0, nb - span) * BLK
            compute(step * per_step, slice(0, per_step * BLK), pl.multiple_of(start, BLK), span * BLK)
        else:
            for sub in range(per_step):
                i = step * per_step + sub
                start = jnp.clip(i - (win - 1), 0, nb - win) * BLK
                compute(i, slice(sub * BLK, (sub + 1) * BLK), pl.multiple_of(start, BLK), win * BLK)

    tq = per_step * BLK
    kv_col = (lambda p, i: (0, kb)) if shared_kv else (lambda p, i: (0, kb + p))
    vv_col = (lambda p, i: (0, vb)) if shared_kv else (lambda p, i: (0, vb + p))
    in_specs = [pl.BlockSpec((tq, BLK), lambda p, i: (i, qb + p)),
                pl.BlockSpec((S, BLK), kv_col), pl.BlockSpec((S, BLK), vv_col)]
    args = [qkv, qkv, qkv]
    if variant == "a":
        in_specs += [pl.BlockSpec(memory_space=pltpu.SMEM), pl.BlockSpec(memory_space=pltpu.SMEM)]
        args += [sinks, slopes]
    elif variant == "b":
        in_specs += [pl.BlockSpec((None, tq, 2), lambda p, i: (p, i, 0)),
                     pl.BlockSpec((None, 2, S), lambda p, i: (p, 0, 0))]
        args += [cq_col, ck_row]
    else:
        in_specs += [pl.BlockSpec((2, 5, BLK, BLK), lambda p, i: (p, 0, 0, 0))]
        args += [bias]
    if after is not None:
        in_specs.append(ANY)
        args.append(after)
    return pl.pallas_call(
        body, name=name,
        out_shape=(jax.ShapeDtypeStruct((S, 512), BF16), jax.ShapeDtypeStruct((4, 2, S), F32)),
        grid=(4, nb // per_step), in_specs=in_specs,
        out_specs=(pl.BlockSpec((tq, BLK), lambda p, i: (i, p)),
                   pl.BlockSpec((None, 2, tq), lambda p, i: (p, 0, i))),
        compiler_params=_params(),
    )(*args)


def _attn_bwd(variant, qkv, do, lse_row, delta_row, name, *, sinks=None, slopes=None, cq_row=None,
              ck_col=None, bias_t=None):
    S = qkv.shape[0]
    nb = S // BLK
    qb, kb, vb = ATTN_COLS[variant]
    shared_kv = variant == "a"
    win = ATTN_WINDOW.get(variant)
    per_step = ATTN_BLOCKS_PER_STEP[variant]

    def body(*refs):
        *refs, dqt_ref = refs
        if variant == "a":
            (q_ref, k_ref, v_ref, do_ref, lse_ref, dl_ref, sink_ref, slope_ref,
             dq_ref, dk_ref, dv_ref, ex_ref) = refs
        elif variant == "b":
            (q_ref, k_ref, v_ref, do_ref, lse_ref, dl_ref, cq_ref, ck_ref,
             dq_ref, dk_ref, dv_ref, ex_ref, dcq_ref) = refs
        else:
            (q_ref, k_ref, v_ref, do_ref, lse_ref, dl_ref, bias_ref,
             dq_ref, dk_ref, dv_ref, ex_ref) = refs
        p = pl.program_id(0)
        lane = lax.broadcasted_iota(jnp.int32, (1, BLK), 1)
        hmasks = [(lane < 64), (lane >= 64)]
        swaps = [(p // 2) != half for half in (0, 1)] if shared_kv else None

        @pl.when(pl.program_id(1) == 0)
        def _():
            dqt_ref[...] = jnp.zeros_like(dqt_ref)
            if variant == "b":
                dcq_ref[...] = jnp.zeros_like(dcq_ref)
            else:
                ex_ref[...] = jnp.zeros_like(ex_ref)

        def to_kv_lanes(x, h):
            x = jnp.where(hmasks[h], x, 0.0)
            if shared_kv:
                x = jnp.where(swaps[h], pltpu.roll(x, 64, 1), x)
            return x

        def compute(j, rows, start, n_q):
            n_rows = rows.stop - rows.start
            s_abs = j * BLK + lax.broadcasted_iota(jnp.int32, (n_rows, 1), 0)
            off_k = pl.multiple_of(j * BLK, BLK)
            k2 = k_ref[rows, :].astype(F32)
            v2 = v_ref[rows, :].astype(F32)
            if shared_kv:
                kv_lane = (lane >> 6) == (p // 2)
                k_src, v_src = jnp.where(kv_lane, k2, 0.0), jnp.where(kv_lane, v2, 0.0)
                k_al = [jnp.where(swaps[h], pltpu.roll(k_src, 64, 1), k_src) for h in (0, 1)]
                v_al = [jnp.where(swaps[h], pltpu.roll(v_src, 64, 1), v_src) for h in (0, 1)]
            else:
                k_al = [jnp.where(hmasks[h], k2, 0.0) for h in (0, 1)]
                v_al = [jnp.where(hmasks[h], v2, 0.0) for h in (0, 1)]
            k_al = [(t * 0.125).astype(BF16) for t in k_al]
            v_al = [t.astype(BF16) for t in v_al]
            q_w = q_ref[pl.ds(start, n_q), :]
            do_w = do_ref[pl.ds(start, n_q), :]
            t_abs = start + lax.broadcasted_iota(jnp.int32, (1, n_q), 1)
            valid = _band_mask(variant, t_abs, s_abs)
            dk_acc = dv_acc = None
            ds_both = []
            for half in (0, 1):
                s = lax.dot_general(k_al[half], q_w, NT, preferred_element_type=F32)
                if variant == "a":
                    s = s + (-slope_ref[2 * p + half]) * jnp.abs(t_abs - s_abs).astype(F32)
                elif variant == "b":
                    s = s + cq_ref[half:half + 1, pl.ds(start, n_q)] - ck_ref[rows, half:half + 1]
                else:
                    i0 = start // BLK
                    s = s + jnp.concatenate([jnp.concatenate(
                        [bias_ref[half, jnp.clip(i0 + b - j - r, 0, 4)] for b in range(n_q // BLK)], axis=1)
                        for r in range(n_rows // BLK)], axis=0)
                pr = jnp.where(valid, jnp.exp(s - lse_ref[half:half + 1, pl.ds(start, n_q)]), 0.0)
                dp = lax.dot_general(v_al[half], do_w, NT, preferred_element_type=F32)
                ds = pr * (dp - dl_ref[half:half + 1, pl.ds(start, n_q)])
                ds16 = ds.astype(BF16)
                dv_h = to_kv_lanes(jnp.dot(pr.astype(BF16), do_w, preferred_element_type=F32), half)
                dk_h = to_kv_lanes(jnp.dot(ds16, q_w, preferred_element_type=F32) * 0.125, half)
                dv_acc = dv_h if dv_acc is None else dv_acc + dv_h
                dk_acc = dk_h if dk_acc is None else dk_acc + dk_h
                ds_both.append(ds16)
                if variant == "b":
                    ex_ref[rows, half:half + 1] = -jnp.sum(ds, axis=1, keepdims=True)
                    dcq_ref[half:half + 1, pl.ds(start, n_q)] += jnp.sum(ds, axis=0, keepdims=True)
                elif variant == "c":
                    for r in range(n_rows // BLK):
                        for b in range(n_q // BLK):
                            ex_ref[half, jnp.clip(i0 + b - j - r, 0, 4)] += ds[r * BLK:(r + 1) * BLK, b * BLK:(b + 1) * BLK]
            dq_t = lax.dot_general(jnp.concatenate(k_al, axis=0), jnp.concatenate(ds_both, axis=0), TN,
                                   preferred_element_type=F32)
            dqt_ref[:, pl.ds(start, n_q)] += dq_t
            if shared_kv:
                @pl.when(p == 0)
                def _():
                    dk_ref[pl.ds(off_k, n_rows), :] = dk_acc
                    dv_ref[pl.ds(off_k, n_rows), :] = dv_acc

                @pl.when(p > 0)
                def _():
                    dk_ref[pl.ds(off_k, n_rows), :] += dk_acc
                    dv_ref[pl.ds(off_k, n_rows), :] += dv_acc
            else:
                dk_ref[pl.ds(off_k, n_rows), :] = dk_acc.astype(dk_ref.dtype)
                dv_ref[pl.ds(off_k, n_rows), :] = dv_acc.astype(dv_ref.dtype)
            if variant == "a":
                for half in (0, 1):
                    p_sink = jnp.exp(sink_ref[2 * p + half] - lse_ref[half:half + 1, pl.ds(off_k, n_rows)])
                    term = p_sink * dl_ref[half:half + 1, pl.ds(off_k, n_rows)]
                    ex_ref[half:half + 1, :] += -jnp.sum(term, axis=1, keepdims=True)

        step = pl.program_id(1)
        if variant == "b":
            for g in range(S // GROUP):
                pl.when(step == g)(functools.partial(compute, step * per_step, slice(0, GROUP), g * GROUP, S - g * GROUP))
        elif variant == "c":
            span = win + per_step - 1
            start = jnp.clip(step * per_step, 0, nb - span) * BLK
            compute(step * per_step, slice(0, per_step * BLK), pl.multiple_of(start, BLK), span * BLK)
        else:
            for sub in range(per_step):
                j = step * per_step + sub
                start = jnp.clip(j, 0, nb - win) * BLK
                compute(j, slice(sub * BLK, (sub + 1) * BLK), pl.multiple_of(start, BLK), win * BLK)

        @pl.when(step == nb // per_step - 1)
        def _():
            dq_ref[...] = jnp.transpose(dqt_ref[...]).astype(BF16)

    tk = per_step * BLK
    col = lambda c0: (lambda p, j: (0, c0 + p))
    kv_blk = (lambda c0: (lambda p, j: (j, c0))) if shared_kv else (lambda c0: (lambda p, j: (j, c0 + p)))
    pair = lambda p, j: (0, p)
    row_stat = pl.BlockSpec((None, 2, S), lambda p, j: (p, 0, 0))
    in_specs = [pl.BlockSpec((S, BLK), col(qb)),
                pl.BlockSpec((tk, BLK), kv_blk(kb)), pl.BlockSpec((tk, BLK), kv_blk(vb)),
                pl.BlockSpec((S, BLK), pair), row_stat, row_stat]
    args = [qkv, qkv, qkv, do, lse_row, delta_row]
    kv_width = BLK if shared_kv else 512
    kv_out = pl.BlockSpec((S, BLK), (lambda p, j: (0, 0)) if shared_kv else pair)
    kv_dtype = F32 if shared_kv else BF16
    out_shape = [jax.ShapeDtypeStruct((S, 512), BF16), jax.ShapeDtypeStruct((S, kv_width), kv_dtype),
                 jax.ShapeDtypeStruct((S, kv_width), kv_dtype)]
    out_specs = [pl.BlockSpec((S, BLK), pair), kv_out, kv_out]
    if variant == "a":
        in_specs += [pl.BlockSpec(memory_space=pltpu.SMEM), pl.BlockSpec(memory_space=pltpu.SMEM)]
        args += [sinks, slopes]
        out_shape.append(jax.ShapeDtypeStruct((4, 8, BLK), F32))
        out_specs.append(pl.BlockSpec((None, 8, BLK), lambda p, j: (p, 0, 0)))
    elif variant == "b":
        in_specs += [row_stat, pl.BlockSpec((None, tk, 2), lambda p, j: (p, j, 0))]
        args += [cq_row, ck_col]
        out_shape += [jax.ShapeDtypeStruct((4, S, 2), F32), jax.ShapeDtypeStruct((4, 2, S), F32)]
        out_specs += [pl.BlockSpec((None, tk, 2), lambda p, j: (p, j, 0)), row_stat]
    else:
        in_specs += [pl.BlockSpec((2, 5, BLK, BLK), lambda p, j: (p, 0, 0, 0))]
        args += [bias_t]
        out_shape.append(jax.ShapeDtypeStruct((8, 5, BLK, BLK), F32))
        out_specs.append(pl.BlockSpec((2, 5, BLK, BLK), lambda p, j: (p, 0, 0, 0)))
    return pl.pallas_call(
        body, name=name, out_shape=tuple(out_shape), grid=(4, nb // per_step),
        in_specs=in_specs, out_specs=tuple(out_specs), scratch_shapes=[pltpu.VMEM((BLK, S), F32)],
        compiler_params=_params(),
    )(*args)


def _log_sigmoid(x):
    return jnp.minimum(x, 0.0) - jnp.log(1.0 + jnp.exp(-jnp.abs(x)))


def _forget_fwd(fb, b_forget, name):
    S = fb.shape[0]
    nb = S // GROUP

    def body(fb_ref, b_ref, cum_ref, carry_ref):
        i = pl.program_id(0)
        logf = _log_sigmoid(fb_ref[...] + b_ref[...])
        r = lax.broadcasted_iota(jnp.int32, (GROUP, GROUP), 0)
        c = lax.broadcasted_iota(jnp.int32, (GROUP, GROUP), 1)
        tri = (c <= r).astype(F32)

        @pl.when(i == 0)
        def _():
            carry_ref[...] = jnp.zeros_like(carry_ref)

        cum = jnp.dot(tri, logf, preferred_element_type=F32, precision=HIGHEST) + carry_ref[0:1, :]
        cum_ref[...] = cum
        carry_ref[...] = jnp.broadcast_to(cum[GROUP - 1:GROUP, :], carry_ref.shape)

    return pl.pallas_call(
        body, name=name, out_shape=jax.ShapeDtypeStruct((S, BLK), F32), grid=(nb,),
        in_specs=[pl.BlockSpec((GROUP, BLK), lambda i: (i, 0)), pl.BlockSpec((1, BLK), lambda i: (0, 0))],
        out_specs=pl.BlockSpec((GROUP, BLK), lambda i: (i, 0)),
        scratch_shapes=[pltpu.VMEM((8, BLK), F32)],
        compiler_params=_params(),
    )(fb, b_forget)


def _forget_bwd(dcum_q, dcum_k, fb, b_forget, name):
    S = fb.shape[0]
    nb = S // GROUP

    def body(dq_ref, dk_ref, fb_ref, b_ref, dfb_ref, db_ref, carry_ref):
        g = pl.program_id(0)
        r = lax.broadcasted_iota(jnp.int32, (GROUP, GROUP), 0)
        c = lax.broadcasted_iota(jnp.int32, (GROUP, GROUP), 1)
        tri = (c >= r).astype(F32)

        @pl.when(g == 0)
        def _():
            carry_ref[...] = jnp.zeros_like(carry_ref)

        dcum = dq_ref[...] + dk_ref[...]
        dlogf = jnp.dot(tri, dcum, preferred_element_type=F32, precision=HIGHEST) + carry_ref[0:1, :]
        carry_ref[...] = jnp.broadcast_to(dlogf[0:1, :], carry_ref.shape)
        x = fb_ref[...] + b_ref[...]
        lane = lax.broadcasted_iota(jnp.int32, (1, BLK), 1)
        dfb = jnp.where(lane < N_FORGET, dlogf * jax.nn.sigmoid(-x), 0.0)
        dfb_ref[...] = dfb
        db = jnp.sum(dfb, axis=0, keepdims=True)

        @pl.when(g == 0)
        def _():
            db_ref[...] = db

        @pl.when(g > 0)
        def _():
            db_ref[...] += db

    rev = pl.BlockSpec((GROUP, BLK), lambda g: (nb - 1 - g, 0))
    row = pl.BlockSpec((1, BLK), lambda g: (0, 0))
    return pl.pallas_call(
        body, name=name,
        out_shape=(jax.ShapeDtypeStruct((S, BLK), F32), jax.ShapeDtypeStruct((1, BLK), F32)), grid=(nb,),
        in_specs=[rev, rev, rev, row], out_specs=(rev, row),
        scratch_shapes=[pltpu.VMEM((8, BLK), F32)],
        compiler_params=_params(),
    )(dcum_q, dcum_k, fb, b_forget)


def _skew(x, sign):
    row = lax.broadcasted_iota(jnp.int32, x.shape, 0)
    for b in range(7):
        amount = (1 << b) if sign > 0 else 256 - (1 << b)
        x = jnp.where(((row >> b) & 1) == 1, pltpu.roll(x, amount, 1), x)
    return x


def _rel_bases(rel):
    far = rel[:, 256:257]
    far127 = jnp.broadcast_to(far, (rel.shape[0], 127))
    base0 = jnp.concatenate([rel[:, 128:0:-1], far, rel[:, 255:128:-1]], axis=1)
    base1 = jnp.concatenate([rel[:, 256:128:-1], far, far127], axis=1)
    base0_t = jnp.concatenate([rel[:, 128:256], far, rel[:, 1:128]], axis=1)
    base1_t = jnp.concatenate([jnp.broadcast_to(far, (rel.shape[0], 128)), far, rel[:, 129:256]], axis=1)
    return jnp.stack([base0, base1, base0_t, base1_t], axis=1)


def _rel_expand(bases, name):
    def body(b_ref, t_ref, tt_ref):
        far = jnp.broadcast_to(b_ref[1:2, 0:1], (BLK, BLK))
        for k, out_ref in ((0, t_ref), (2, tt_ref)):
            for d in (0, 1):
                x = jnp.broadcast_to(b_ref[k + d:k + d + 1, :], (BLK, 2 * BLK))
                out_ref[d] = _skew(x, 1)[:, :BLK]
            for d in (2, 3, 4):
                out_ref[d] = far

    out = jax.ShapeDtypeStruct((8, 5, BLK, BLK), F32)
    spec = pl.BlockSpec((None, 5, BLK, BLK), lambda h: (h, 0, 0, 0))
    return pl.pallas_call(
        body, name=name, out_shape=(out, out), grid=(8,),
        in_specs=[pl.BlockSpec((None, 4, 2 * BLK), lambda h: (h, 0, 0))], out_specs=(spec, spec),
        compiler_params=_params(),
    )(bases)


def _rel_reduce(dtiles_t, name):
    def body(dt_ref, o_ref):
        zeros = jnp.zeros((BLK, BLK), F32)
        sums = []
        for d in (0, 1):
            x = _skew(jnp.concatenate([dt_ref[d], zeros], axis=1), -1)
            sums.append(jnp.broadcast_to(jnp.sum(x, axis=0, keepdims=True), (8, 2 * BLK)))
        lane = lax.broadcasted_iota(jnp.int32, (8, 2 * BLK), 1)
        main = pltpu.roll(sums[0], BLK, 1) + jnp.where(lane > BLK, sums[1], 0.0)
        far = jnp.sum(jnp.where(lane < BLK, sums[1], 0.0)[0:1], axis=1, keepdims=True)
        far = far + jnp.sum(jnp.sum(dt_ref[2] + dt_ref[3] + dt_ref[4], axis=0, keepdims=True), axis=1, keepdims=True)
        o_ref[...] = jnp.concatenate([main[0:1], jnp.broadcast_to(far, (1, BLK))], axis=1)

    return pl.pallas_call(
        body, name=name, out_shape=jax.ShapeDtypeStruct((8, 1, 3 * BLK), F32), grid=(8,),
        in_specs=[pl.BlockSpec((None, 5, BLK, BLK), lambda h: (h, 0, 0, 0))],
        out_specs=pl.BlockSpec((None, 1, 3 * BLK), lambda h: (h, 0, 0)),
        compiler_params=_params(),
    )(dtiles_t)


def _final_loss(x, target, g, below, name):
    S, D = x.shape
    ts = _row_tile(S, ROW_TILE)

    def body(x_ref, t_ref, g_ref, f_ref, gate_ref, dx_ref, loss_ref, dg_ref, dgate_ref, df_ref):
        i = pl.program_id(0)
        xv, gv = x_ref[...], g_ref[...]
        rstd = lax.rsqrt(jnp.mean(xv * xv, axis=-1, keepdims=True) + EPS)
        xhat = xv * rstd
        err = xhat * gv - t_ref[...]
        part = 0.5 * jnp.sum(jnp.mean(err * err, axis=-1, keepdims=True), axis=0, keepdims=True)
        dy = err / D
        dg = jnp.sum(dy * xhat, axis=0, keepdims=True)
        dxhat = dy * gv
        proj = jnp.mean(dxhat * xhat, axis=-1, keepdims=True)
        dx = rstd * (dxhat - xhat * proj)
        dx_ref[...] = dx
        dgate = _gated_residual_bwd(dx, f_ref, gate_ref, df_ref)
        _accumulate_rows(i, [(loss_ref, jnp.broadcast_to(part, loss_ref.shape)), (dg_ref, dg), (dgate_ref, dgate)])

    tile = pl.BlockSpec((ts, D), lambda i: (i, 0))
    row = pl.BlockSpec((1, D), lambda i: (0, 0))
    vec = jax.ShapeDtypeStruct((1, D), F32)
    return pl.pallas_call(
        body, name=name,
        out_shape=(jax.ShapeDtypeStruct((S, D), F32), jax.ShapeDtypeStruct((8, 128), F32), vec, vec,
                   jax.ShapeDtypeStruct((S, D), BF16)),
        grid=(S // ts,), in_specs=[tile, tile, row, tile, row],
        out_specs=(tile, pl.BlockSpec((8, 128), lambda i: (0, 0)), row, row, tile),
        compiler_params=_params(),
    )(x, target, g, *below)


def _ada_fwd(c_all, w_ada, name):
    L, D, E = w_ada.shape

    def body(c_ref, w_ref, o_ref):
        cv = c_ref[...]
        cond = cv * jax.nn.sigmoid(cv)
        o_ref[...] = jnp.dot(cond, w_ref[...], preferred_element_type=F32, precision=HIGHEST)

    return pl.pallas_call(
        body, name=name, out_shape=jax.ShapeDtypeStruct((L, N_DEV, E), F32), grid=(L,),
        in_specs=[pl.BlockSpec((N_DEV, D), lambda l: (0, 0)), pl.BlockSpec((None, D, E), lambda l: (l, 0, 0))],
        out_specs=pl.BlockSpec((None, N_DEV, E), lambda l: (l, 0, 0)),
        compiler_params=_params(),
    )(c_all, w_ada)


def _ada_bwd(c_all_t, dmod, name):
    D = c_all_t.shape[0]
    L, _, E = dmod.shape

    def body(c_ref, d_ref, o_ref):
        cv = c_ref[...]
        cond = cv * jax.nn.sigmoid(cv)
        acc = None
        for b in range(N_DEV):
            t = cond[:, b:b + 1] * d_ref[b:b + 1, :]
            acc = t if acc is None else acc + t
        o_ref[...] = acc

    return pl.pallas_call(
        body, name=name, out_shape=jax.ShapeDtypeStruct((L, D, E), F32), grid=(L,),
        in_specs=[pl.BlockSpec((D, N_DEV), lambda l: (0, 0)), pl.BlockSpec((None, N_DEV, E), lambda l: (l, 0, 0))],
        out_specs=pl.BlockSpec((None, D, E), lambda l: (l, 0, 0)),
        compiler_params=_params(),
    )(c_all_t, dmod)


def _adamw(w, m, v, g_parts, name, me, after=None):
    L, R, C = w.shape
    tr = _row_tile(R, max(8, (256 * 1024 // max(C, 128)) // 8 * 8))
    nr = R // tr
    c1 = 1.0 - ADAM_B1 ** ADAM_STEP
    c2 = 1.0 - ADAM_B2 ** ADAM_STEP
    direct = [isinstance(p, tuple) for p in g_parts]
    n_in = sum(2 if d else 1 for d in direct)

    def body(me_ref, w_ref, m_ref, v_ref, *rest):
        g_refs, (go_ref, d_ref, mo_ref, vo_ref) = list(rest[:n_in]), rest[-4:]
        layer = pl.program_id(0)
        g = None
        for l in range(L):
            land_ref = g_refs.pop(0)
            own = g_refs.pop(0)[...].astype(F32) if direct[l] else None
            gl = None
            for k in range(land_ref.shape[0]):
                part = land_ref[k].astype(F32)
                if direct[l]:
                    part = jnp.where(me_ref[l] == k, own, part)
                gl = part if gl is None else gl + part
            g = gl if g is None else jnp.where(layer == l, gl, g)
        mn = ADAM_B1 * m_ref[...] + (1.0 - ADAM_B1) * g
        vn = ADAM_B2 * v_ref[...] + (1.0 - ADAM_B2) * (g * g)
        m_hat = mn / c1
        v_hat = vn / c2
        go_ref[...] = g
        d_ref[...] = -ADAM_LR * (m_hat / (jnp.sqrt(v_hat) + ADAM_EPS) + ADAM_WD * w_ref[...])
        mo_ref[...] = mn
        vo_ref[...] = vn

    def rows(l, layer, i):
        return jnp.where(layer == l, i, 0 if l > 0 else nr - 1)

    in_specs, operands = [], []
    for l, p in enumerate(g_parts):
        land, sent = p if direct[l] else (p, None)
        in_specs.append(pl.BlockSpec((land.shape[0], tr, C), lambda layer, i, me_ref, l=l: (0, rows(l, layer, i), 0)))
        operands.append(land)
        if direct[l]:
            in_specs.append(pl.BlockSpec((None, tr, C), lambda layer, i, me_ref, l=l: (me_ref[l], rows(l, layer, i), 0)))
            operands.append(sent)
    if after is not None:
        in_specs.append(ANY)
        operands.append(after)
    tile = pl.BlockSpec((None, tr, C), lambda layer, i, me_ref: (layer, i, 0))
    out = jax.ShapeDtypeStruct((L, R, C), F32)
    return pl.pallas_call(
        body, name=name, out_shape=(out, out, out, out),
        grid_spec=pltpu.PrefetchScalarGridSpec(
            num_scalar_prefetch=1, grid=(L, nr), in_specs=[tile, tile, tile] + in_specs,
            out_specs=(tile, tile, tile, tile)),
        compiler_params=_params(),
    )(me, w, m, v, *operands)


def _pair_add(pieces, recv, core, name):
    _, _, R, C = pieces.shape
    tr = _row_tile(R, max(8, (512 * 1024 // max(C, 128)) // 8 * 8))

    def body(core_ref, a_ref, b_ref, o_ref):
        o_ref[...] = (a_ref[...].astype(F32) + b_ref[...].astype(F32)).astype(BF16)

    return pl.pallas_call(
        body, name=name, out_shape=jax.ShapeDtypeStruct((4, R, C), BF16),
        grid_spec=pltpu.PrefetchScalarGridSpec(
            num_scalar_prefetch=1, grid=(4, R // tr),
            in_specs=[pl.BlockSpec((None, None, tr, C), lambda k, i, core_ref: (core_ref[0], k, i, 0)),
                      pl.BlockSpec((None, tr, C), lambda k, i, core_ref: (k, i, 0))],
            out_specs=pl.BlockSpec((None, tr, C), lambda k, i, core_ref: (k, i, 0))),
        compiler_params=_params(),
    )(core, pieces, recv)


MESH = pl.DeviceIdType.MESH
ANY = pl.BlockSpec(memory_space=pl.ANY)


def _position():
    return lax.axis_index("x"), lax.axis_index("y"), lax.axis_index("c")


def _small_all_gather(v, name):
    m_per, n = v.shape

    def body(x_ref, out_ref, send_sems, recv_sems, local_sem):
        x, y, c = _position()
        me, sibling = (x, y, c), (x, y, 1 - c)
        chips = [(1 - x, y), (x, 1 - y), (1 - x, 1 - y)]

        def rows(px, py, pc):
            return out_ref.at[pl.ds((4 * px + 2 * py + pc) * m_per, m_per), :]

        def copy(k, block, to, src=None):
            return pltpu.make_async_remote_copy(
                src_ref=rows(*block) if src is None else src, dst_ref=rows(*block),
                send_sem=send_sems.at[k], recv_sem=recv_sems.at[k], device_id=to, device_id_type=MESH)

        mine = pltpu.make_async_copy(x_ref, rows(*me), local_sem)
        mine.start()
        first = [copy(0, me, sibling, src=x_ref)]
        first += [copy(1 + j, me, (*chip, c), src=x_ref) for j, chip in enumerate(chips)]
        for cp in first:
            cp.start()
        passed = [copy(4 + j, (*chip, c), sibling) for j, chip in enumerate(chips)]
        for j, chip in enumerate(chips):
            copy(1 + j, (*chip, c), me).wait_recv()
            passed[j].start()
        copy(0, sibling, me).wait_recv()
        for j, chip in enumerate(chips):
            copy(4 + j, (*chip, 1 - c), me).wait_recv()
        for cp in first + passed:
            cp.wait_send()
        mine.wait()

    return pl.pallas_call(
        body, name=name, out_shape=jax.ShapeDtypeStruct((N_DEV * m_per, n), v.dtype),
        in_specs=[pl.BlockSpec(memory_space=pltpu.VMEM)], out_specs=pl.BlockSpec(memory_space=pltpu.VMEM),
        scratch_shapes=[pltpu.SemaphoreType.DMA((7,)), pltpu.SemaphoreType.DMA((7,)), pltpu.SemaphoreType.DMA],
    )(v)


def _sibling_exchange(pieces, name):
    n_arr = len(pieces)

    def body(*refs):
        p_refs, out_refs = refs[:n_arr], refs[n_arr:2 * n_arr]
        send_sems, recv_sems = refs[2 * n_arr:]
        x, y, c = _position()
        copies = [pltpu.make_async_remote_copy(
            src_ref=p_refs[a].at[1 - c], dst_ref=out_refs[a], send_sem=send_sems.at[a], recv_sem=recv_sems.at[a],
            device_id=(x, y, 1 - c), device_id_type=MESH) for a in range(n_arr)]
        for cp in copies:
            cp.start()
        for cp in copies:
            cp.wait()

    return pl.pallas_call(
        body, name=name,
        out_shape=tuple(jax.ShapeDtypeStruct(p.shape[1:], p.dtype) for p in pieces),
        in_specs=[ANY] * n_arr, out_specs=tuple([ANY] * n_arr),
        scratch_shapes=[pltpu.SemaphoreType.DMA((n_arr,)), pltpu.SemaphoreType.DMA((n_arr,))],
    )(*pieces)


HBM = pl.BlockSpec(memory_space=pltpu.HBM)
SEM = pl.BlockSpec(memory_space=pltpu.SEMAPHORE)
EFFECT = pltpu.SideEffectType.DATAFLOW_SIDE_EFFECTING
RELATIONS = [(rx, ry, rc) for rx in (0, 1) for ry in (0, 1) for rc in (0, 1)][1:]


SAME_CORE = [r for r in RELATIONS if r == (0, 0, 1) or r[2] == 0]


CHIPS = [r for r in RELATIONS if r[2] == 0]


def _exchange_copies(src_refs, land_refs, send_sems, recv_sems, scatter, receive_side, relations):
    x, y, c = _position()
    index = (lambda px, py, pc: 2 * px + py) if relations == CHIPS else (lambda px, py, pc: 4 * px + 2 * py + pc)
    me = index(x, y, c)
    copies = []
    for k, (rx, ry, rc) in enumerate(relations):
        peer = ((1 - x) if rx else x, (1 - y) if ry else y, (1 - c) if rc else c)
        peer_index = index(*peer)
        for a, (src, land) in enumerate(zip(src_refs, land_refs)):
            copies.append(pltpu.make_async_remote_copy(
                src_ref=src.at[peer_index] if scatter else src,
                dst_ref=land.at[peer_index if receive_side else me],
                send_sem=send_sems.at[a * len(relations) + k], recv_sem=recv_sems.at[a * len(relations) + k],
                device_id=peer, device_id_type=MESH))
    return copies


def _exchange_start(srcs, scatter, after, name, relations=RELATIONS):
    n = len(srcs)
    land_shapes = [(s.shape if scatter else (N_DEV,) + s.shape) for s in srcs]

    def body(*refs):
        src_refs, land_refs = refs[:n], refs[n:2 * n]
        send_sems, recv_sems = refs[2 * n + 1], refs[2 * n + 2]
        token = refs[-1]
        for cp in _exchange_copies(src_refs, land_refs, send_sems, recv_sems, scatter, False, relations):
            cp.start()
        token[...] = jnp.zeros_like(token)

    sems = pltpu.SemaphoreType.DMA((n * len(relations),))
    outs = pl.pallas_call(
        body, name=name,
        out_shape=(sems, sems, *[pltpu.HBM(s.shape, s.dtype) for s in srcs],
                   *[pltpu.HBM(shape, s.dtype) for shape, s in zip(land_shapes, srcs)],
                   jax.ShapeDtypeStruct((8, 128), F32)),
        in_specs=[HBM] * (2 * n) + [ANY],
        out_specs=(SEM, SEM, *[HBM] * (2 * n), pl.BlockSpec(memory_space=pltpu.VMEM)),
        input_output_aliases={a: 2 + a for a in range(2 * n)},
        compiler_params=pltpu.CompilerParams(has_side_effects=EFFECT),
    )(*[pltpu.with_memory_space_constraint(s, pltpu.HBM) for s in srcs],
      *[pltpu.with_memory_space_constraint(lax.empty(shape, s.dtype), pltpu.HBM)
        for shape, s in zip(land_shapes, srcs)], after)
    return outs[0], outs[1], outs[2:2 + n], outs[2 + n:2 + 2 * n], outs[-1]


def _exchange_wait(started, scatter, after, name, relations=RELATIONS):
    send_sems, recv_sems, srcs, lands, _ = started
    n = len(srcs)

    def body(*refs):
        src_refs, land_refs = refs[:n], refs[n:2 * n]
        send_sems, recv_sems = refs[2 * n], refs[2 * n + 1]
        copies = _exchange_copies(src_refs, land_refs, send_sems, recv_sems, scatter, True, relations)
        for cp in copies:
            cp.wait_send()
        for cp in copies:
            cp.wait_recv()

    outs = pl.pallas_call(
        body, name=name,
        out_shape=(*[pltpu.HBM(s.shape, s.dtype) for s in srcs], *[pltpu.HBM(t.shape, t.dtype) for t in lands]),
        in_specs=[HBM] * (2 * n) + [SEM, SEM, ANY], out_specs=tuple([HBM] * (2 * n)),
        input_output_aliases={a: a for a in range(2 * n)},
        compiler_params=pltpu.CompilerParams(has_side_effects=EFFECT),
    )(*srcs, *lands, send_sems, recv_sems, after)
    return outs[:n], outs[n:]


def _forward_copies(land_refs, send_sems, recv_sems, receive_side):
    x, y, c = _position()
    copies = []
    for j, (px, py) in enumerate([(1 - x, y), (x, 1 - y), (1 - x, 1 - y)]):
        held, coming = 4 * px + 2 * py + c, 4 * px + 2 * py + (1 - c)
        for a, land in enumerate(land_refs):
            copies.append(pltpu.make_async_remote_copy(
                src_ref=land.at[held], dst_ref=land.at[coming if receive_side else held],
                send_sem=send_sems.at[3 * a + j], recv_sem=recv_sems.at[3 * a + j],
                device_id=(x, y, 1 - c), device_id_type=MESH))
    return copies


def _forward_start(lands, after, name):
    n = len(lands)

    def body(*refs):
        send_sems, recv_sems, token = refs[n + 1], refs[n + 2], refs[-1]
        for cp in _forward_copies(refs[:n], send_sems, recv_sems, False):
            cp.start()
        token[...] = jnp.zeros_like(token)

    sems = pltpu.SemaphoreType.DMA((3 * n,))
    outs = pl.pallas_call(
        body, name=name,
        out_shape=(sems, sems, *[pltpu.HBM(t.shape, t.dtype) for t in lands], jax.ShapeDtypeStruct((8, 128), F32)),
        in_specs=[HBM] * n + [ANY], out_specs=(SEM, SEM, *[HBM] * n, pl.BlockSpec(memory_space=pltpu.VMEM)),
        input_output_aliases={a: 2 + a for a in range(n)},
        compiler_params=pltpu.CompilerParams(has_side_effects=EFFECT),
    )(*lands, after)
    return outs[0], outs[1], outs[2:2 + n], outs[-1]


def _forward_wait(started, after, name):
    send_sems, recv_sems, lands, _ = started
    n = len(lands)

    def body(*refs):
        copies = _forward_copies(refs[:n], refs[n], refs[n + 1], True)
        for cp in copies:
            cp.wait_send()
        for cp in copies:
            cp.wait_recv()

    return pl.pallas_call(
        body, name=name, out_shape=tuple(pltpu.HBM(t.shape, t.dtype) for t in lands),
        in_specs=[HBM] * n + [SEM, SEM, ANY], out_specs=tuple([HBM] * n),
        input_output_aliases={a: a for a in range(n)},
        compiler_params=pltpu.CompilerParams(has_side_effects=EFFECT),
    )(*lands, send_sems, recv_sems, after)


def _place_own(lands, mine, me, name):
    n = len(lands)
    flat = [m.reshape(-1, m.shape[-1]) for m in mine]
    flat_lands = [t.reshape(N_DEV, -1, t.shape[-1]) for t in lands]

    def body(me_ref, *refs):
        for src, dst in zip(refs[:n], refs[2 * n:]):
            dst[...] = src[...]

    in_specs = [pl.BlockSpec((m.shape[0] // 2, m.shape[1]), lambda i, me_ref: (i, 0)) for m in flat]
    out_specs = [pl.BlockSpec((None, m.shape[0] // 2, m.shape[1]), lambda i, me_ref: (me_ref[0], i, 0)) for m in flat]
    outs = pl.pallas_call(
        body, name=name, out_shape=tuple(jax.ShapeDtypeStruct(t.shape, t.dtype) for t in flat_lands),
        grid_spec=pltpu.PrefetchScalarGridSpec(
            num_scalar_prefetch=1, grid=(2,), in_specs=in_specs + [ANY] * n, out_specs=tuple(out_specs)),
        input_output_aliases={1 + n + a: a for a in range(n)},
        compiler_params=_params(),
    )(me, *flat, *flat_lands)
    return [o.reshape(t.shape) for o, t in zip(outs, lands)]


W_IN_SHARD = N_IN // N_DEV
F_SHARD = F_COL // W_IN_SHARD
F_LO = F_COL - F_SHARD * W_IN_SHARD


def _w_ffn_in_view(w):
    return jnp.transpose(w, (0, 2, 1))


def _w_in_segments():
    segments = []
    for d in range(N_DEV):
        if d == F_SHARD:
            segments += [(d, 0, d * W_IN_SHARD, F_LO), (d, F_LO, N_MAIN, N_FORGET),
                         (d, F_LO + N_FORGET, F_COL, W_IN_SHARD - F_LO - N_FORGET)]
        else:
            segments.append((d, 0, d * W_IN_SHARD - (N_FORGET if d > F_SHARD else 0), W_IN_SHARD))
    return segments


def _w_in_rearranged(g, name):
    D = g.shape[1]
    tr = _row_tile(D, 256)

    def body(g_ref, o_ref):
        for d, lo, at, width in _w_in_segments():
            o_ref[:, at:at + width] = g_ref[d, :, lo:lo + width]
        o_ref[:, N_IN:] = jnp.zeros((tr, BLK - N_FORGET), o_ref.dtype)

    return pl.pallas_call(
        body, name=name, out_shape=jax.ShapeDtypeStruct((D, N_MAIN + BLK), g.dtype), grid=(D // tr,),
        in_specs=[pl.BlockSpec((N_DEV, tr, W_IN_SHARD), lambda i: (0, i, 0))],
        out_specs=pl.BlockSpec((tr, N_MAIN + BLK), lambda i: (i, 0)),
        compiler_params=_params(),
    )(g)


def _w_in_pieces(dw_r, name, pair_major=False):
    D = dw_r.shape[0]
    tr = _row_tile(D, 256)
    lead = (2, 4) if pair_major else (N_DEV,)

    def body(x_ref, o_ref):
        for d, lo, at, width in _w_in_segments():
            slot = (d % 2, d // 2) if pair_major else (d,)
            o_ref[(*slot, slice(None), slice(lo, lo + width))] = x_ref[:, at:at + width]

    return pl.pallas_call(
        body, name=name, out_shape=jax.ShapeDtypeStruct((*lead, D, W_IN_SHARD), dw_r.dtype), grid=(D // tr,),
        in_specs=[pl.BlockSpec((tr, N_MAIN + BLK), lambda i: (i, 0))],
        out_specs=pl.BlockSpec((*lead, tr, W_IN_SHARD), lambda i: (*[0] * len(lead), i, 0)),
        compiler_params=_params(),
    )(dw_r)


def _row_pieces(dw):
    return dw.reshape(N_DEV, dw.shape[0] // N_DEV, dw.shape[1])


def _branch_pieces(dw):
    k, w, d = dw.shape
    return jnp.transpose(dw.reshape(k, w, N_DEV, d // N_DEV), (2, 0, 1, 3)).reshape(N_DEV, k * w, d // N_DEV)


def _pairs_col(a):
    return jnp.transpose(a.reshape(a.shape[0], 4, 2), (1, 0, 2))


def _pairs_row(a):
    return jnp.transpose(a.reshape(a.shape[0], 4, 2), (1, 2, 0))


def _heads_from_col(a):
    return jnp.transpose(a, (1, 0, 2)).reshape(a.shape[1], 8)


def _heads_from_row(a):
    return jnp.transpose(a, (2, 0, 1)).reshape(a.shape[2], 8)


def _pad_lanes(a, n):
    return jnp.pad(a, [(0, 0)] * (a.ndim - 1) + [(0, n - a.shape[-1])])


SMALL_SEGMENTS = (("dmod", 2 * 6 * D_MODEL), ("norm_mix_g", 2 * D_MODEL), ("norm_ffn_g", 2 * D_MODEL),
                  ("final_norm_g", D_MODEL), ("b_forget", 128), ("sinks", 128), ("rel_bias", 4224), ("loss", 128))
SMALL_ROWS = 176


def _pack_small(parts):
    flat = [_pad_lanes(parts[name].reshape(1, -1), size) for name, size in SMALL_SEGMENTS]
    total = sum(size for _, size in SMALL_SEGMENTS)
    flat.append(jnp.zeros((1, SMALL_ROWS * 128 - total), F32))
    return jnp.concatenate(flat, axis=1).reshape(SMALL_ROWS, 128)


def _unpack_small(packed, shapes):
    flat = packed.reshape(-1)
    out, pos = {}, 0
    for name, size in SMALL_SEGMENTS:
        shape = shapes[name]
        count = 1
        for d in shape:
            count *= d
        out[name] = flat[pos:pos + count].reshape(shape)
        pos += size
    return out


def kernel(x, c, norm_mix_g, norm_ffn_g, w_ada, b_ada, w_in, b_forget, sinks, rel_bias, w_branch, w_out, w_ffn_in, w_ffn_out, final_norm_g, loss_target, m_norm_mix_g, m_norm_ffn_g, m_w_ada, m_b_ada, m_w_in, m_b_forget, m_sinks, m_rel_bias, m_w_branch, m_w_out, m_w_ffn_in, m_w_ffn_out, m_final_norm_g, v_norm_mix_g, v_norm_ffn_g, v_w_ada, v_b_ada, v_w_in, v_b_forget, v_sinks, v_rel_bias, v_w_branch, v_w_out, v_w_ffn_in, v_w_ffn_out, v_final_norm_g):
    depth = w_in.shape[0]
    S, D = x.shape[1], x.shape[2]
    assert S % GROUP == 0 and S >= ATTN_WINDOW["c"] * BLK
    px, py, pc = _position()
    me = 4 * px + 2 * py + pc
    x0 = x[0]

    assert depth == 2
    big_weights = (w_in, w_branch, w_out, w_ffn_in, w_ffn_out)
    me_arr = jnp.stack([me, me]).astype(jnp.int32)
    me_in_arr = jnp.stack([2 * px + py, me]).astype(jnp.int32)

    def rest_matrices(g_branch, g_out, g_fin, g_fout):
        return (jnp.transpose(g_branch, (1, 2, 0, 3)).reshape(3, 512, D), g_out.reshape(D, D),
                g_fin.reshape(2 * FFN_HIDDEN, D), g_fout.reshape(FFN_HIDDEN, D))

    def arrive(started, after, name):
        mine, landed = _exchange_wait(started, False, after, f"{name}_wait", SAME_CORE)
        return mine, _forward_start(landed, mine[0], f"{name}_forward_start")

    def finish_gather(arrived, after, name):
        mine, forward = arrived
        landed = _forward_wait(forward, after, f"{name}_forward_wait")
        return _place_own(landed, mine, me.astype(jnp.int32).reshape(1), f"{name}_own")

    w_fin_t = _w_ffn_in_view(w_ffn_in)
    shards = [[t.astype(BF16) for t in (w_in[l], w_branch[l], w_out[l], w_fin_t[l], w_ffn_out[l])]
              for l in range(depth)]
    c_all = _small_all_gather(c.reshape(8, 128), "comm_gather_c").reshape(N_DEV, D)
    mod_cols = _ada_fwd(c_all, w_ada, "ada_fwd")
    mod_all = _small_all_gather(mod_cols.reshape(-1, 128), "comm_gather_mod")
    gather_in0 = _exchange_start(shards[0][:1], False, mod_all, "comm_gather_w_in0_start", SAME_CORE)
    gather_rest0 = _exchange_start(shards[0][1:], False, gather_in0[4], "comm_gather_rest0_start", SAME_CORE)
    gather1 = _exchange_start(shards[1], False, gather_rest0[4], "comm_gather_weights1_start", SAME_CORE)
    started = gather1[4][0:1, 0:1]
    W_in, W_branch, W_out, W_fin, W_fout = ([None, None] for _ in range(5))
    mod_all = mod_all.reshape(N_DEV, depth, N_DEV, w_ada.shape[2])
    mod_mine = lax.dynamic_index_in_dim(mod_all, me, axis=2, keepdims=False)
    mod = jnp.transpose(mod_mine, (1, 0, 2)).reshape(depth, 6 * D) + b_ada + started
    mods = [[mod[l:l + 1, k * D:(k + 1) * D] for k in range(6)] for l in range(depth)]
    rel_tiles = [_rel_expand(_rel_bases(rel_bias[l]) + started, f"rel_expand{l}") for l in range(depth)]

    slopes = jnp.exp2(-jnp.arange(1, 9, dtype=F32))
    saved = []
    xs = x0
    for l in range(depth):
        if l == 1:
            g_in1, *g_rest1 = finish_gather(arrived1, xs, "comm_gather_weights1")
            W_in[1] = _w_in_rearranged(g_in1, "w_in_rearrange1")
            W_branch[1], W_out[1], W_fin[1], W_fout[1] = rest_matrices(*g_rest1)
        sh_m, sc_m, g_m, sh_f, sc_f, g_f = mods[l]
        gm, gf = norm_mix_g[l:l + 1], norm_ffn_g[l:l + 1]
        bfor = _pad_lanes(b_forget[l:l + 1], BLK)
        h = _norm_mod_fwd(xs, gm, sh_m, sc_m, f"norm_mix_fwd{l}")
        tiles, tiles_t = rel_tiles[l]
        if l == 0:
            arrived_in0 = arrive(gather_in0, rel_tiles[-1][1], "comm_gather_w_in0")
            W_in[0] = _w_in_rearranged(finish_gather(arrived_in0, h, "comm_gather_w_in0")[0], "w_in_rearrange0")
        qkv, gates = _project(h, W_in[l], f"proj{l}")
        fb = _matmul(h, W_in[l], "nn", F32, f"proj_forget{l}", TILES["proj_forget"], n=BLK, b_off=N_MAIN // BLK)
        cum = _forget_fwd(fb, bfor, f"forget_fwd{l}")[:, :N_FORGET]
        cum_col, cum_row = _pairs_col(cum), _pairs_row(cum)
        o_a, lse_a = _attn_fwd("a", qkv, f"attn_a_fwd{l}", sinks=sinks[l], slopes=slopes)
        o_b, lse_b = _attn_fwd("b", qkv, f"attn_b_fwd{l}", cq_col=cum_col, ck_row=cum_row)
        arrived_rest0 = arrive(gather_rest0, o_b, "comm_gather_rest0") if l == 0 else None
        o_c, lse_c = _attn_fwd("c", qkv, f"attn_c_fwd{l}", bias=tiles, after=arrived_rest0[1][3] if l == 0 else None)
        if l == 0:
            W_branch[0], W_out[0], W_fin[0], W_fout[0] = rest_matrices(
                *finish_gather(arrived_rest0, o_c, "comm_gather_rest0"))
        x1, merged, mix = _merge_fwd(o_a, o_b, o_c, gates, W_branch[l], W_out[l], xs, g_m, f"merge_fwd{l}")
        h2 = _norm_mod_fwd(x1, gf, sh_f, sc_f, f"norm_ffn_fwd{l}")
        act = _ffn_in_fwd(h2, W_fin[l], f"ffn_in_fwd{l}")
        if l == 0:
            arrived1 = arrive(gather1, act, "comm_gather_weights1")
        x2, ffn = _matmul_resid(act, W_fout[l], x1, g_f, f"ffn_out{l}", TILES["ffn_out"],
                                after=arrived1[1][3] if l == 0 else None)
        saved.append(dict(x=xs, h=h, qkv=qkv, gates=gates, fb=fb, bfor=bfor, cum_col=cum_col, cum_row=cum_row,
                          tiles_t=tiles_t, o=(o_a, o_b, o_c), lse=(lse_a, lse_b, lse_c), merged=merged, mix=mix,
                          x1=x1, h2=h2, act=act, ffn=ffn))
        xs = x2

    dx, loss_tile, d_final_g, d_g_f, df = _final_loss(
        xs, loss_target[0], final_norm_g.reshape(1, D), (saved[-1]["ffn"], mods[-1][5]), "final_loss")

    grads = {k: [None] * depth for k in ("w_in", "w_branch", "w_out", "w_ffn_in", "w_ffn_out", "norm_mix_g",
                                          "norm_ffn_g", "b_forget", "sinks", "rel_bias", "dmod")}
    def rest_pieces(l):
        return [_branch_pieces(grads["w_branch"][l]), _row_pieces(grads["w_out"][l]),
                _row_pieces(grads["w_ffn_in"][l]), _row_pieces(grads["w_ffn_out"][l])]

    reduce1 = reduce_rest0 = reduce_in0 = None
    for l in reversed(range(depth)):
        sv = saved[l]
        sh_m, sc_m, g_m, sh_f, sc_f, g_f = mods[l]
        gm, gf = norm_mix_g[l:l + 1], norm_ffn_g[l:l + 1]
        du_g, du_u = _ffn_mid_bwd(sv["h2"], df, W_fin[l], W_fout[l], f"ffn_mid_bwd{l}")
        du = jnp.concatenate([du_g, du_u], axis=1)
        grads["w_ffn_out"][l] = _matmul(sv["act"], df, "tn", BF16, f"wgrad_ffn_out{l}", TILES["wgrad_ffn_out"])
        grads["w_ffn_in"][l] = _matmul(du, sv["h2"], "tn", BF16, f"wgrad_ffn_in{l}", TILES["wgrad_ffn_in"])
        dh2 = _matmul(du, W_fin[l], "nn", F32, f"dgrad_ffn_in{l}", TILES["dgrad_ffn_in"])
        dx1, d_sh_f, d_sc_f, d_gf, d_g_m, dmix = _norm_mod_bwd(sv["x1"], dh2, dx, gf, sc_f, f"norm_ffn_bwd{l}",
                                                               below=(sv["mix"], g_m))
        grads["w_out"][l] = _matmul(sv["merged"], dmix, "tn", BF16, f"wgrad_out{l}", TILES["wgrad_out"])
        o_a, o_b, o_c = sv["o"]
        dgates, d_w_branch, do_a, do_b, do_c, dl_a, dl_b, dl_c = _merge_bwd(
            dmix, o_a, o_b, o_c, sv["gates"], W_branch[l], W_out[l], f"merge_bwd{l}")
        grads["w_branch"][l] = d_w_branch.astype(BF16)
        lse_rows = list(sv["lse"])
        if l == 0:
            reduce_rest0 = _exchange_start(rest_pieces(0), True, dgates, "comm_reduce_rest0_start")
            lse_rows = [t + reduce_rest0[4][0:1, 0:1] for t in lse_rows]
        dq_a, dk_a, dv_a, dsink = _attn_bwd("a", sv["qkv"], do_a, lse_rows[0], dl_a.reshape(4, 2, S), f"attn_a_bwd{l}",
                                            sinks=sinks[l], slopes=slopes)
        dq_b, dk_b, dv_b, dck, dcq = _attn_bwd("b", sv["qkv"], do_b, lse_rows[1], dl_b.reshape(4, 2, S),
                                               f"attn_b_bwd{l}", cq_row=sv["cum_row"], ck_col=sv["cum_col"])
        dq_c, dk_c, dv_c, dtiles_t = _attn_bwd("c", sv["qkv"], do_c, lse_rows[2], dl_c.reshape(4, 2, S),
                                               f"attn_c_bwd{l}", bias_t=sv["tiles_t"])
        grads["sinks"][l] = dsink[:, :2, 0].reshape(8)
        grads["rel_bias"][l] = _rel_reduce(dtiles_t, f"rel_reduce{l}")[:, 0, :N_REL]
        dcum_k = _pad_lanes(_heads_from_col(dck), BLK)
        dcum_q = _pad_lanes(_heads_from_row(dcq), BLK)
        dfb, d_bfor = _forget_bwd(dcum_q, dcum_k, sv["fb"], sv["bfor"], f"forget_bwd{l}")
        grads["b_forget"][l] = d_bfor[0, :N_FORGET]
        dproj = jnp.concatenate(
            [t.astype(BF16) for t in (dq_a, dk_a, dv_a, dq_b, dk_b, dv_b, dq_c, dk_c, dv_c, dgates, dfb)],
            axis=1)
        grads["w_in"][l] = _matmul(sv["h"], dproj, "tn", BF16, f"wgrad_in{l}", TILES["wgrad_in"])
        if l == 1:
            reduce1 = _exchange_start([_w_in_pieces(grads["w_in"][1], "w_in_pieces1")] + rest_pieces(1), True, dproj,
                                      "comm_reduce1_start")
        dh = _matmul(dproj, W_in[l], "nt", F32, f"dgrad_in{l}", TILES["dgrad_in"], after=reduce1[4] if l == 1 else None)
        d_g_f_here = d_g_f
        if l > 0:
            dx, d_sh_m, d_sc_m, d_gm, d_g_f, df = _norm_mod_bwd(sv["x"], dh, dx1, gm, sc_m, f"norm_mix_bwd{l}",
                                                                below=(saved[l - 1]["ffn"], mods[l - 1][5]))
        else:
            dx, d_sh_m, d_sc_m, d_gm = _norm_mod_bwd(sv["x"], dh, dx1, gm, sc_m, f"norm_mix_bwd{l}")
        grads["norm_mix_g"][l] = d_gm[0]
        grads["norm_ffn_g"][l] = d_gf[0]
        grads["dmod"][l] = jnp.concatenate([d_sh_m, d_sc_m, d_g_m, d_sh_f, d_sc_f, d_g_f_here], axis=1)[0]

    grad_x = dx.reshape(x.shape)

    small_shapes = dict(dmod=b_ada.shape, norm_mix_g=norm_mix_g.shape, norm_ffn_g=norm_ffn_g.shape,
                        final_norm_g=final_norm_g.shape, b_forget=b_forget.shape, sinks=sinks.shape,
                        rel_bias=rel_bias.shape, loss=())
    mine_small = _pack_small(dict(
        loss=_pad_lanes(loss_tile[0:1, 0:1], 128),
        dmod=jnp.stack(grads["dmod"]), norm_mix_g=jnp.stack(grads["norm_mix_g"]),
        norm_ffn_g=jnp.stack(grads["norm_ffn_g"]), final_norm_g=d_final_g[0],
        b_forget=_pad_lanes(jnp.stack(grads["b_forget"]).reshape(1, -1), 128),
        sinks=_pad_lanes(jnp.stack(grads["sinks"]).reshape(1, -1), 128),
        rel_bias=_pad_lanes(jnp.stack(grads["rel_bias"]).reshape(1, -1), 4224)))
    all_small = _small_all_gather(mine_small, "comm_gather_small").reshape(N_DEV, SMALL_ROWS, 128)
    pieces_in0 = _w_in_pieces(grads["w_in"][0], "w_in_pieces0", pair_major=True)
    from_sibling = _sibling_exchange([pieces_in0], "comm_reduce_in0_sibling")[0]
    pair_sum_in0 = _pair_add(pieces_in0, from_sibling, pc.astype(jnp.int32).reshape(1), "pair_add_in0")
    reduce_in0 = _exchange_start([pair_sum_in0], True, all_small, "comm_reduce_in0_start", CHIPS)
    in0_started = reduce_in0[4]

    def pack_params(b_ada_, nm, nf, fn, bf, sk, rb):
        return _pack_small(dict(dmod=b_ada_, norm_mix_g=nm, norm_ffn_g=nf, final_norm_g=fn, loss=jnp.zeros((1, 128), F32),
                                b_forget=_pad_lanes(bf.reshape(1, -1), 128), sinks=_pad_lanes(sk.reshape(1, -1), 128),
                                rel_bias=_pad_lanes(rb.reshape(1, -1), 4224)))

    small_out = _adamw(
        pack_params(b_ada, norm_mix_g, norm_ffn_g, final_norm_g, b_forget, sinks, rel_bias)[None],
        pack_params(m_b_ada, m_norm_mix_g, m_norm_ffn_g, m_final_norm_g, m_b_forget, m_sinks, m_rel_bias)[None],
        pack_params(v_b_ada, v_norm_mix_g, v_norm_ffn_g, v_final_norm_g, v_b_forget, v_sinks, v_rel_bias)[None],
        [all_small], "adamw_small", me_arr, after=in0_started)
    small_out = [_unpack_small(t[0], small_shapes) for t in small_out]

    dmod_all = all_small[:, :96].reshape(N_DEV, depth, 6 * D)
    dmod_cols = lax.dynamic_slice_in_dim(dmod_all, me * w_ada.shape[2], w_ada.shape[2], axis=2)
    d_w_ada = _ada_bwd(jnp.transpose(c_all), jnp.transpose(dmod_cols, (1, 0, 2)), "ada_bwd")

    big = {"w_ada": _adamw(w_ada, m_w_ada, v_w_ada, [d_w_ada[l:l + 1] for l in range(depth)], "adamw_w_ada", me_arr,
                           after=in0_started)}
    sent1, landed1 = _exchange_wait(reduce1, True, big["w_ada"][0], "comm_reduce1_wait")
    sent_rest0, landed_rest0 = _exchange_wait(reduce_rest0, True, landed1[0], "comm_reduce_rest0_wait")
    parts = {"w_in": [None, (landed1[0], sent1[0])]}
    for a, name in enumerate(("w_branch", "w_out", "w_ffn_in", "w_ffn_out")):
        parts[name] = [(landed_rest0[a], sent_rest0[a]), (landed1[1 + a], sent1[1 + a])]

    def update(name, w, m, v, view=lambda t: t):
        per_layer = lambda t: t.reshape(depth, -1, t.shape[-1])
        outs = _adamw(*[per_layer(view(t)) for t in (w, m, v)], parts[name], f"adamw_{name}",
                      me_in_arr if name == "w_in" else me_arr)
        big[name] = [view(t).reshape(w.shape) for t in outs]

    update("w_ffn_in", w_ffn_in, m_w_ffn_in, v_w_ffn_in, _w_ffn_in_view)
    update("w_ffn_out", w_ffn_out, m_w_ffn_out, v_w_ffn_out)
    update("w_branch", w_branch, m_w_branch, v_w_branch)
    update("w_out", w_out, m_w_out, v_w_out)
    sent_in0, landed_in0 = _exchange_wait(reduce_in0, True, big["w_out"][0], "comm_reduce_in0_wait", CHIPS)
    parts["w_in"][0] = (landed_in0[0], sent_in0[0])
    update("w_in", w_in, m_w_in, v_w_in)

    def leaf(kind, name):
        if name in big:
            return big[name][kind]
        return small_out[kind]["dmod" if name == "b_ada" else name]

    order = ["norm_mix_g", "norm_ffn_g", "w_ada", "b_ada", "w_in", "b_forget", "sinks", "rel_bias", "w_branch",
             "w_out", "w_ffn_in", "w_ffn_out", "final_norm_g"]
    loss = small_out[0]["loss"]
    return (loss, grad_x, *[leaf(0, n) for n in order], *[leaf(1, n) for n in order],
            *[leaf(2, n) for n in order], *[leaf(3, n) for n in order])
```

```python
import functools

import jax
import jax.numpy as jnp
from jax import lax
from jax.experimental import pallas as pl
from jax.experimental.pallas import tpu as pltpu

F32 = jnp.float32
BF16 = jnp.bfloat16
NEG_INF = -1e30
EPS = 1e-6
N_DEV = 8
BLK = 128
GROUP = 4 * BLK
VMEM_LIMIT_BYTES = 56 * 1024 * 1024

D_MODEL = 1024
N_QKV = 3840
N_GATES = 3072
N_MAIN = N_QKV + N_GATES
N_FORGET = 8
N_IN = N_MAIN + N_FORGET
F_COL = 2304
FFN_HIDDEN = 2816
N_REL = 257

ADAM_LR, ADAM_B1, ADAM_B2, ADAM_EPS, ADAM_WD, ADAM_STEP = 0.001, 0.9, 0.999, 1e-08, 0.01, 10

NN = (((1,), (0,)), ((), ()))
NT = (((1,), (1,)), ((), ()))
TN = (((0,), (0,)), ((), ()))
HIGHEST = lax.Precision.HIGHEST

ATTN_COLS = {"a": (0, 4, 5), "b": (6, 10, 14), "c": (18, 22, 26)}
ATTN_WINDOW = {"a": 2, "c": 5}
ATTN_BLOCKS_PER_STEP = {"a": 8, "b": GROUP // BLK, "c": 2}
ROW_TILE = 512


def _params():
    return pltpu.CompilerParams(vmem_limit_bytes=VMEM_LIMIT_BYTES)


def _tile(n, target):
    best = None
    t = 128
    while t <= min(n, target):
        if n % t == 0:
            best = t
        t += 128
    return best if best is not None else n


def _row_tile(n, target):
    t = min(n, target)
    while n % t:
        t -= 8
    return t


TILES = {
    "proj": (1024, 768, 1024), "proj_forget": (1024, 128, 1024),
    "ffn_out": (1024, 512, 2816), "ffn_fused": (512, 1408),
    "wgrad_ffn_out": (1408, 1024, 2048), "wgrad_ffn_in": (1408, 1024, 2048), "dgrad_ffn_in": (1024, 1024, 2816),
    "wgrad_out": (1024, 1024, 2048),
    "wgrad_in": (1024, 1408, 2048), "dgrad_in": (1024, 1024, 3520),
}


def _matmul(a, b, mode, out_dtype, name, tiles, *, n=None, a_off=0, b_off=0, m=None, after=None):
    tm, tn, tk = tiles
    if mode == "nn":
        M, K = a.shape if m is None else (m, a.shape[1])
        N = b.shape[1] if n is None else n
    elif mode == "nt":
        M, K = a.shape
        N = b.shape[0] if n is None else n
    else:
        K = a.shape[0]
        M = a.shape[1] if m is None else m
        N = b.shape[1] if n is None else n
    tm = _tile(M, tm) if M % 128 == 0 else M
    tn = _tile(N, tn)
    tk = _tile(K, tk)
    nk = K // tk
    dims = {"nn": NN, "nt": NT, "tn": TN}[mode]
    if mode == "nn":
        a_spec = pl.BlockSpec((tm, tk), lambda i, j, k: (i + a_off, k))
        b_spec = pl.BlockSpec((tk, tn), lambda i, j, k: (k, j + b_off))
    elif mode == "nt":
        a_spec = pl.BlockSpec((tm, tk), lambda i, j, k: (i + a_off, k))
        b_spec = pl.BlockSpec((tn, tk), lambda i, j, k: (j + b_off, k))
    else:
        a_spec = pl.BlockSpec((tk, tm), lambda i, j, k: (k, i + a_off))
        b_spec = pl.BlockSpec((tk, tn), lambda i, j, k: (k, j + b_off))

    def body(a_ref, b_ref, *rest):
        o_ref, acc_ref = rest[-2:]
        k = pl.program_id(2)
        part = lax.dot_general(a_ref[...], b_ref[...], dims, preferred_element_type=F32)
        if nk == 1:
            o_ref[...] = part.astype(o_ref.dtype)
        else:
            @pl.when(k == 0)
            def _():
                acc_ref[...] = part

            @pl.when(k > 0)
            def _():
                acc_ref[...] += part

            @pl.when(k == nk - 1)
            def _():
                o_ref[...] = acc_ref[...].astype(o_ref.dtype)

    return pl.pallas_call(
        body, name=name,
        out_shape=jax.ShapeDtypeStruct((M, N), out_dtype),
        grid=(M // tm, N // tn, nk),
        in_specs=[a_spec, b_spec] + ([ANY] if after is not None else []),
        out_specs=pl.BlockSpec((tm, tn), lambda i, j, k: (i, j)),
        scratch_shapes=[pltpu.VMEM((tm, tn) if nk > 1 else (8, 128), F32)],
        compiler_params=_params(),
    )(a, b, *([after] if after is not None else []))


def _project(h, w, name):
    S, D = h.shape
    tm, tn, _ = TILES["proj"]
    tm = _tile(S, tm)
    nq, ng = N_QKV // tn, N_GATES // tn

    def body(h_ref, w_ref, q_ref, g_ref):
        j = pl.program_id(1)
        acc = jnp.dot(h_ref[...], w_ref[...], preferred_element_type=F32)

        @pl.when(j < nq)
        def _():
            q_ref[...] = acc.astype(BF16)

        @pl.when(j >= nq)
        def _():
            g_ref[...] = acc

    return pl.pallas_call(
        body, name=name,
        out_shape=(jax.ShapeDtypeStruct((S, N_QKV), BF16), jax.ShapeDtypeStruct((S, N_GATES), F32)),
        grid=(S // tm, nq + ng),
        in_specs=[pl.BlockSpec((tm, D), lambda i, j: (i, 0)), pl.BlockSpec((D, tn), lambda i, j: (0, j))],
        out_specs=(pl.BlockSpec((tm, tn), lambda i, j: (i, jnp.minimum(j, nq - 1))),
                   pl.BlockSpec((tm, tn), lambda i, j: (i, jnp.maximum(j - nq, 0)))),
        compiler_params=_params(),
    )(h, w)


def _matmul_resid(a, b, resid, gate, name, tiles, after=None):
    M, K = a.shape
    N = b.shape[1]
    tm, tn, tk = (_tile(d, t) for d, t in zip((M, N, K), tiles))
    nk = K // tk

    def body(a_ref, b_ref, r_ref, g_ref, *rest):
        o_ref, s_ref, acc_ref = rest[-3:]
        k = pl.program_id(2)
        part = jnp.dot(a_ref[...], b_ref[...], preferred_element_type=F32)

        def finish(acc):
            o_ref[...] = r_ref[...] + g_ref[...] * acc
            s_ref[...] = acc.astype(BF16)

        if nk == 1:
            finish(part)
        else:
            @pl.when(k == 0)
            def _():
                acc_ref[...] = part

            @pl.when(k > 0)
            def _():
                acc_ref[...] += part

            @pl.when(k == nk - 1)
            def _():
                finish(acc_ref[...])

    return pl.pallas_call(
        body, name=name,
        out_shape=(jax.ShapeDtypeStruct((M, N), F32), jax.ShapeDtypeStruct((M, N), BF16)),
        grid=(M // tm, N // tn, nk),
        in_specs=[pl.BlockSpec((tm, tk), lambda i, j, k: (i, k)),
                  pl.BlockSpec((tk, tn), lambda i, j, k: (k, j)),
                  pl.BlockSpec((tm, tn), lambda i, j, k: (i, j)),
                  pl.BlockSpec((1, tn), lambda i, j, k: (0, j))] + ([ANY] if after is not None else []),
        out_specs=(pl.BlockSpec((tm, tn), lambda i, j, k: (i, j)),
                   pl.BlockSpec((tm, tn), lambda i, j, k: (i, j))),
        scratch_shapes=[pltpu.VMEM((tm, tn) if nk > 1 else (8, 128), F32)],
        compiler_params=_params(),
    )(a, b, resid, gate, *([after] if after is not None else []))


def _norm_mod_fwd(x, g, shift, scale, name):
    S, D = x.shape
    ts = _row_tile(S, ROW_TILE)

    def body(x_ref, g_ref, sh_ref, sc_ref, h_ref):
        xv = x_ref[...]
        rstd = lax.rsqrt(jnp.mean(xv * xv, axis=-1, keepdims=True) + EPS)
        y = xv * rstd * g_ref[...]
        h_ref[...] = (y * (1.0 + sc_ref[...]) + sh_ref[...]).astype(BF16)

    row = pl.BlockSpec((1, D), lambda i: (0, 0))
    return pl.pallas_call(
        body, name=name, out_shape=jax.ShapeDtypeStruct((S, D), BF16), grid=(S // ts,),
        in_specs=[pl.BlockSpec((ts, D), lambda i: (i, 0)), row, row, row],
        out_specs=pl.BlockSpec((ts, D), lambda i: (i, 0)),
        compiler_params=_params(),
    )(x, g, shift, scale)


def _accumulate_rows(i, pairs):
    @pl.when(i == 0)
    def _():
        for ref, value in pairs:
            ref[...] = value

    @pl.when(i > 0)
    def _():
        for ref, value in pairs:
            ref[...] += value


def _gated_residual_bwd(dx, f_ref, gate_ref, df_ref):
    df_ref[...] = (dx * gate_ref[...]).astype(BF16)
    return jnp.sum(dx * f_ref[...].astype(F32), axis=0, keepdims=True)


def _norm_mod_bwd(x, dh, dres, g, scale, name, below=None):
    S, D = x.shape
    ts = _row_tile(S, ROW_TILE)

    def body(x_ref, dh_ref, dr_ref, g_ref, sc_ref, *rest):
        i = pl.program_id(0)
        xv, dhv, gv = x_ref[...], dh_ref[...], g_ref[...]
        rstd = lax.rsqrt(jnp.mean(xv * xv, axis=-1, keepdims=True) + EPS)
        xhat = xv * rstd
        dn = dhv * (1.0 + sc_ref[...])
        dxhat = dn * gv
        proj = jnp.mean(dxhat * xhat, axis=-1, keepdims=True)
        dx = dr_ref[...] + rstd * (dxhat - xhat * proj)
        sums = [jnp.sum(dhv, axis=0, keepdims=True), jnp.sum(dhv * (xhat * gv), axis=0, keepdims=True),
                jnp.sum(dn * xhat, axis=0, keepdims=True)]
        if below is None:
            dx_ref, *sum_refs = rest
        else:
            f_ref, gate_ref, dx_ref, *sum_refs, df_ref = rest
            sums.append(_gated_residual_bwd(dx, f_ref, gate_ref, df_ref))
        dx_ref[...] = dx
        _accumulate_rows(i, list(zip(sum_refs, sums)))

    tile = pl.BlockSpec((ts, D), lambda i: (i, 0))
    row = pl.BlockSpec((1, D), lambda i: (0, 0))
    vec = jax.ShapeDtypeStruct((1, D), F32)
    fused = below is not None
    return pl.pallas_call(
        body, name=name,
        out_shape=(jax.ShapeDtypeStruct((S, D), F32), vec, vec, vec)
        + ((vec, jax.ShapeDtypeStruct((S, D), BF16)) if fused else ()),
        grid=(S // ts,),
        in_specs=[tile, tile, tile, row, row] + ([tile, row] if fused else []),
        out_specs=(tile, row, row, row) + ((row, tile) if fused else ()),
        compiler_params=_params(),
    )(x, dh, dres, g, scale, *(below if fused else ()))


def _ffn_in_fwd(h, w_t, name):
    S, D = h.shape
    F = w_t.shape[0] // 2
    tm, tn = _tile(S, TILES["ffn_fused"][0]), _tile(F, TILES["ffn_fused"][1])
    nj = F // tn

    def body(h_ref, wg_ref, wu_ref, o_ref):
        hv = h_ref[...]
        ug = lax.dot_general(hv, wg_ref[...], NT, preferred_element_type=F32)
        uu = lax.dot_general(hv, wu_ref[...], NT, preferred_element_type=F32)
        o_ref[...] = (ug * jax.nn.sigmoid(ug) * uu).astype(BF16)

    return pl.pallas_call(
        body, name=name, out_shape=jax.ShapeDtypeStruct((S, F), BF16), grid=(nj, S // tm),
        in_specs=[pl.BlockSpec((tm, D), lambda j, i: (i, 0)),
                  pl.BlockSpec((tn, D), lambda j, i: (j, 0)),
                  pl.BlockSpec((tn, D), lambda j, i: (j + nj, 0))],
        out_specs=pl.BlockSpec((tm, tn), lambda j, i: (i, j)),
        compiler_params=_params(),
    )(h, w_t, w_t)


def _ffn_mid_bwd(h, df, w_in_t, w_out, name):
    S, D = h.shape
    F = w_in_t.shape[0] // 2
    tm, tn = _tile(S, TILES["ffn_fused"][0]), _tile(F, TILES["ffn_fused"][1])
    nj = F // tn

    def body(h_ref, df_ref, wg_ref, wu_ref, wo_ref, dg_ref, du_ref):
        hv = h_ref[...]
        ug = lax.dot_general(hv, wg_ref[...], NT, preferred_element_type=F32)
        uu = lax.dot_general(hv, wu_ref[...], NT, preferred_element_type=F32)
        dact = lax.dot_general(df_ref[...], wo_ref[...], NT, preferred_element_type=F32)
        sig = jax.nn.sigmoid(ug)
        dg_ref[...] = (dact * uu * (sig * (1.0 + ug * (1.0 - sig)))).astype(BF16)
        du_ref[...] = (dact * (ug * sig)).astype(BF16)

    out = jax.ShapeDtypeStruct((S, F), BF16)
    return pl.pallas_call(
        body, name=name, out_shape=(out, out), grid=(nj, S // tm),
        in_specs=[pl.BlockSpec((tm, D), lambda j, i: (i, 0)),
                  pl.BlockSpec((tm, D), lambda j, i: (i, 0)),
                  pl.BlockSpec((tn, D), lambda j, i: (j, 0)),
                  pl.BlockSpec((tn, D), lambda j, i: (j + nj, 0)),
                  pl.BlockSpec((tn, D), lambda j, i: (j, 0))],
        out_specs=(pl.BlockSpec((tm, tn), lambda j, i: (i, j)), pl.BlockSpec((tm, tn), lambda j, i: (i, j))),
        compiler_params=_params(),
    )(h, df, w_in_t, w_in_t, w_out)


def _merge_fwd(o_a, o_b, o_c, gates, w_branch, w_out, resid, gate, name, *, tm=512):
    S, W = o_a.shape
    D = w_branch.shape[2]
    tm = _row_tile(S, tm)

    def body(oa_ref, ob_ref, oc_ref, g_ref, w_ref, wo_ref, r_ref, gm_ref, x_ref, m_ref, mix_ref):
        acc = None
        for k, o_ref in enumerate((oa_ref, ob_ref, oc_ref)):
            y = jnp.dot(o_ref[...], w_ref[k], preferred_element_type=F32)
            t = jax.nn.sigmoid(g_ref[:, k * D:(k + 1) * D]) * y
            acc = t if acc is None else acc + t
        merged = acc.astype(BF16)
        m_ref[...] = merged
        mix = jnp.dot(merged, wo_ref[...], preferred_element_type=F32)
        x_ref[...] = r_ref[...] + gm_ref[...] * mix
        mix_ref[...] = mix.astype(BF16)

    o_spec = pl.BlockSpec((tm, W), lambda i: (i, 0))
    tile = pl.BlockSpec((tm, D), lambda i: (i, 0))
    return pl.pallas_call(
        body, name=name,
        out_shape=(jax.ShapeDtypeStruct((S, D), F32), jax.ShapeDtypeStruct((S, D), BF16), jax.ShapeDtypeStruct((S, D), BF16)),
        grid=(S // tm,),
        in_specs=[o_spec, o_spec, o_spec, pl.BlockSpec((tm, 3 * D), lambda i: (i, 0)),
                  pl.BlockSpec((3, W, D), lambda i: (0, 0, 0)), pl.BlockSpec((D, D), lambda i: (0, 0)),
                  tile, pl.BlockSpec((1, D), lambda i: (0, 0))],
        out_specs=(tile, tile, tile),
        compiler_params=_params(),
    )(o_a, o_b, o_c, gates, w_branch, w_out, resid, gate)


def _merge_bwd(dmix, o_a, o_b, o_c, gates, w_branch, w_out, name, *, tm=256):
    S, W = o_a.shape
    D = w_branch.shape[2]
    tm = _row_tile(S, tm)
    n_heads = W // 64

    def body(dm_ref, oa_ref, ob_ref, oc_ref, g_ref, w_ref, wo_ref, dg_ref, dw_ref,
             doa_ref, dob_ref, doc_ref, dla_ref, dlb_ref, dlc_ref):
        first = pl.program_id(0) == 0
        head_of_column = (lax.broadcasted_iota(jnp.int32, (W, BLK), 0) // 64
                          == lax.broadcasted_iota(jnp.int32, (W, BLK), 1)).astype(F32)
        dm = lax.dot_general(dm_ref[...], wo_ref[...], NT, preferred_element_type=F32)
        branches = ((oa_ref, doa_ref, dla_ref), (ob_ref, dob_ref, dlb_ref), (oc_ref, doc_ref, dlc_ref))
        for k, (o_ref, do_ref, dl_ref) in enumerate(branches):
            wk = w_ref[k]
            ov = o_ref[...]
            y = jnp.dot(ov, wk, preferred_element_type=F32)
            g = jax.nn.sigmoid(g_ref[:, k * D:(k + 1) * D])
            dy = (dm * g).astype(BF16)
            dwk = lax.dot_general(ov, dy, TN, preferred_element_type=F32)

            @pl.when(first)
            def _(k=k, dwk=dwk):
                dw_ref[k] = dwk

            @pl.when(jnp.logical_not(first))
            def _(k=k, dwk=dwk):
                dw_ref[k] += dwk
            dg_ref[:, k * D:(k + 1) * D] = (dm * y * (g * (1.0 - g))).astype(BF16)
            do16 = lax.dot_general(dy, wk, NT, preferred_element_type=F32).astype(BF16)
            do_ref[...] = do16
            prod = do16.astype(F32) * ov.astype(F32)
            sums = jnp.dot(prod, head_of_column, preferred_element_type=F32, precision=HIGHEST)
            dl_ref[...] = jnp.transpose(sums)[:n_heads, :]

    o_spec = pl.BlockSpec((tm, W), lambda i: (i, 0))
    wide = pl.BlockSpec((tm, 3 * D), lambda i: (i, 0))
    dl_spec = pl.BlockSpec((n_heads, tm), lambda i: (0, i))
    o_out = jax.ShapeDtypeStruct((S, W), BF16)
    wide_out = jax.ShapeDtypeStruct((S, 3 * D), BF16)
    dl_out = jax.ShapeDtypeStruct((n_heads, S), F32)
    whole = pl.BlockSpec((3, W, D), lambda i: (0, 0, 0))
    return pl.pallas_call(
        body, name=name,
        out_shape=(wide_out, jax.ShapeDtypeStruct((3, W, D), F32), o_out, o_out, o_out, dl_out, dl_out, dl_out),
        grid=(S // tm,),
        in_specs=[pl.BlockSpec((tm, D), lambda i: (i, 0)), o_spec, o_spec, o_spec, wide, whole,
                  pl.BlockSpec((D, D), lambda i: (0, 0))],
        out_specs=(wide, whole, o_spec, o_spec, o_spec, dl_spec, dl_spec, dl_spec),
        compiler_params=_params(),
    )(dmix, o_a, o_b, o_c, gates, w_branch, w_out)


def _band_mask(variant, t_abs, s_abs):
    if variant == "b":
        return s_abs <= t_abs
    qc, kc = t_abs >> 6, s_abs >> 6
    return (kc <= qc) & (kc >= qc - (2 if variant == "a" else 8))


def _attn_fwd(variant, qkv, name, *, sinks=None, slopes=None, cq_col=None, ck_row=None, bias=None, after=None):
    S = qkv.shape[0]
    nb = S // BLK
    qb, kb, vb = ATTN_COLS[variant]
    shared_kv = variant == "a"
    win = ATTN_WINDOW.get(variant)
    per_step = ATTN_BLOCKS_PER_STEP[variant]

    def body(*refs):
        if after is not None:
            refs = refs[:-3] + refs[-2:]
        if variant == "a":
            q_ref, k_ref, v_ref, sink_ref, slope_ref, o_ref, lse_ref = refs
        elif variant == "b":
            q_ref, k_ref, v_ref, cq_ref, ck_ref, o_ref, lse_ref = refs
        else:
            q_ref, k_ref, v_ref, bias_ref, o_ref, lse_ref = refs
        p = pl.program_id(0)
        lane = lax.broadcasted_iota(jnp.int32, (1, BLK), 1)
        diagonal = lax.broadcasted_iota(jnp.int32, (BLK, BLK), 0) == lax.broadcasted_iota(jnp.int32, (BLK, BLK), 1)

        def compute(i, rows, start, n_keys):
            n_rows = rows.stop - rows.start
            t_abs = i * BLK + lax.broadcasted_iota(jnp.int32, (n_rows, 1), 0)
            q2 = q_ref[rows, :].astype(F32) * 0.125
            k_w = k_ref[pl.ds(start, n_keys), :]
            v_w = v_ref[pl.ds(start, n_keys), :]
            s_abs = start + lax.broadcasted_iota(jnp.int32, (1, n_keys), 1)
            valid = _band_mask(variant, t_abs, s_abs)
            outs = []
            for half in (0, 1):
                hmask = (lane >= 64) if half else (lane < 64)
                qh = jnp.where(hmask, q2, 0.0)
                if shared_kv:
                    swap = (p // 2) != half
                    qh = jnp.where(swap, pltpu.roll(qh, 64, 1), qh)
                s = lax.dot_general(qh.astype(BF16), k_w, NT, preferred_element_type=F32)
                if variant == "a":
                    head = 2 * p + half
                    s = s + (-slope_ref[head]) * jnp.abs(t_abs - s_abs).astype(F32)
                elif variant == "b":
                    s = s + cq_ref[rows, half:half + 1] - ck_ref[half:half + 1, pl.ds(start, n_keys)]
                else:
                    j0 = start // BLK
                    s = s + jnp.concatenate([jnp.concatenate(
                        [bias_ref[half, jnp.clip(i + r - j0 - b, 0, 4)] for b in range(n_keys // BLK)], axis=1)
                        for r in range(n_rows // BLK)], axis=0)
                s = jnp.where(valid, s, NEG_INF)
                m = jnp.max(s, axis=1, keepdims=True)
                if variant == "a":
                    m = jnp.maximum(m, sink_ref[head])
                pe = jnp.exp(s - m)
                l = jnp.sum(pe, axis=1, keepdims=True)
                if variant == "a":
                    l = l + jnp.exp(sink_ref[head] - m)
                out = jnp.dot(pe.astype(BF16), v_w, preferred_element_type=F32) / l
                if shared_kv:
                    out = jnp.where(swap, pltpu.roll(out, 64, 1), out)
                outs.append(out)
                lse = m + jnp.log(l)
                for b in range(n_rows // BLK):
                    part = jnp.where(diagonal, lse[b * BLK:(b + 1) * BLK, :], 0.0)
                    lse_ref[half:half + 1, rows.start + b * BLK:rows.start + (b + 1) * BLK] = jnp.sum(
                        part, axis=0, keepdims=True)
            o_ref[rows, :] = jnp.where(lane < 64, outs[0], outs[1]).astype(BF16)

        step = pl.program_id(1)
        if variant == "b":
            for g in range(S // GROUP):
                pl.when(step == g)(functools.partial(compute, step * per_step, slice(0, GROUP), 0, (g + 1) * GROUP))
        elif variant == "c":
            span = win + per_step - 1
            start = jnp.clip(step * per_step - (win - 1), 0, nb - span) * BLK
            compute(step * per_step, slice(0, per_step * BLK), pl.multiple_of(start, BLK), span * BLK)
        else:
            for sub in range(per_step):
                i = step * per_step + sub
                start = jnp.clip(i - (win - 1), 0, nb - win) * BLK
                compute(i, slice(sub * BLK, (sub + 1) * BLK), pl.multiple_of(start, BLK), win * BLK)

    tq = per_step * BLK
    kv_col = (lambda p, i: (0, kb)) if shared_kv else (lambda p, i: (0, kb + p))
    vv_col = (lambda p, i: (0, vb)) if shared_kv else (lambda p, i: (0, vb + p))
    in_specs = [pl.BlockSpec((tq, BLK), lambda p, i: (i, qb + p)),
                pl.BlockSpec((S, BLK), kv_col), pl.BlockSpec((S, BLK), vv_col)]
    args = [qkv, qkv, qkv]
    if variant == "a":
        in_specs += [pl.BlockSpec(memory_space=pltpu.SMEM), pl.BlockSpec(memory_space=pltpu.SMEM)]
        args += [sinks, slopes]
    elif variant == "b":
        in_specs += [pl.BlockSpec((None, tq, 2), lambda p, i: (p, i, 0)),
                     pl.BlockSpec((None, 2, S), lambda p, i: (p, 0, 0))]
        args += [cq_col, ck_row]
    else:
        in_specs += [pl.BlockSpec((2, 5, BLK, BLK), lambda p, i: (p, 0, 0, 0))]
        args += [bias]
    if after is not None:
        in_specs.append(ANY)
        args.append(after)
    return pl.pallas_call(
        body, name=name,
        out_shape=(jax.ShapeDtypeStruct((S, 512), BF16), jax.ShapeDtypeStruct((4, 2, S), F32)),
        grid=(4, nb // per_step), in_specs=in_specs,
        out_specs=(pl.BlockSpec((tq, BLK), lambda p, i: (i, p)),
                   pl.BlockSpec((None, 2, tq), lambda p, i: (p, 0, i))),
        compiler_params=_params(),
    )(*args)


def _attn_bwd(variant, qkv, do, lse_row, delta_row, name, *, sinks=None, slopes=None, cq_row=None,
              ck_col=None, bias_t=None):
    S = qkv.shape[0]
    nb = S // BLK
    qb, kb, vb = ATTN_COLS[variant]
    shared_kv = variant == "a"
    win = ATTN_WINDOW.get(variant)
    per_step = ATTN_BLOCKS_PER_STEP[variant]

    def body(*refs):
        *refs, dqt_ref = refs
        if variant == "a":
            (q_ref, k_ref, v_ref, do_ref, lse_ref, dl_ref, sink_ref, slope_ref,
             dq_ref, dk_ref, dv_ref, ex_ref) = refs
        elif variant == "b":
            (q_ref, k_ref, v_ref, do_ref, lse_ref, dl_ref, cq_ref, ck_ref,
             dq_ref, dk_ref, dv_ref, ex_ref, dcq_ref) = refs
        else:
            (q_ref, k_ref, v_ref, do_ref, lse_ref, dl_ref, bias_ref,
             dq_ref, dk_ref, dv_ref, ex_ref) = refs
        p = pl.program_id(0)
        lane = lax.broadcasted_iota(jnp.int32, (1, BLK), 1)
        hmasks = [(lane < 64), (lane >= 64)]
        swaps = [(p // 2) != half for half in (0, 1)] if shared_kv else None

        @pl.when(pl.program_id(1) == 0)
        def _():
            dqt_ref[...] = jnp.zeros_like(dqt_ref)
            if variant == "b":
                dcq_ref[...] = jnp.zeros_like(dcq_ref)
            else:
                ex_ref[...] = jnp.zeros_like(ex_ref)

        def to_kv_lanes(x, h):
            x = jnp.where(hmasks[h], x, 0.0)
            if shared_kv:
                x = jnp.where(swaps[h], pltpu.roll(x, 64, 1), x)
            return x

        def compute(j, rows, start, n_q):
            n_rows = rows.stop - rows.start
            s_abs = j * BLK + lax.broadcasted_iota(jnp.int32, (n_rows, 1), 0)
            off_k = pl.multiple_of(j * BLK, BLK)
            k2 = k_ref[rows, :].astype(F32)
            v2 = v_ref[rows, :].astype(F32)
            if shared_kv:
                kv_lane = (lane >> 6) == (p // 2)
                k_src, v_src = jnp.where(kv_lane, k2, 0.0), jnp.where(kv_lane, v2, 0.0)
                k_al = [jnp.where(swaps[h], pltpu.roll(k_src, 64, 1), k_src) for h in (0, 1)]
                v_al = [jnp.where(swaps[h], pltpu.roll(v_src, 64, 1), v_src) for h in (0, 1)]
            else:
                k_al = [jnp.where(hmasks[h], k2, 0.0) for h in (0, 1)]
                v_al = [jnp.where(hmasks[h], v2, 0.0) for h in (0, 1)]
            k_al = [(t * 0.125).astype(BF16) for t in k_al]
            v_al = [t.astype(BF16) for t in v_al]
            q_w = q_ref[pl.ds(start, n_q), :]
            do_w = do_ref[pl.ds(start, n_q), :]
            t_abs = start + lax.broadcasted_iota(jnp.int32, (1, n_q), 1)
            valid = _band_mask(variant, t_abs, s_abs)
            dk_acc = dv_acc = None
            ds_both = []
            for half in (0, 1):
                s = lax.dot_general(k_al[half], q_w, NT, preferred_element_type=F32)
                if variant == "a":
                    s = s + (-slope_ref[2 * p + half]) * jnp.abs(t_abs - s_abs).astype(F32)
                elif variant == "b":
                    s = s + cq_ref[half:half + 1, pl.ds(start, n_q)] - ck_ref[rows, half:half + 1]
                else:
                    i0 = start // BLK
                    s = s + jnp.concatenate([jnp.concatenate(
                        [bias_ref[half, jnp.clip(i0 + b - j - r, 0, 4)] for b in range(n_q // BLK)], axis=1)
                        for r in range(n_rows // BLK)], axis=0)
                pr = jnp.where(valid, jnp.exp(s - lse_ref[half:half + 1, pl.ds(start, n_q)]), 0.0)
                dp = lax.dot_general(v_al[half], do_w, NT, preferred_element_type=F32)
                ds = pr * (dp - dl_ref[half:half + 1, pl.ds(start, n_q)])
                ds16 = ds.astype(BF16)
                dv_h = to_kv_lanes(jnp.dot(pr.astype(BF16), do_w, preferred_element_type=F32), half)
                dk_h = to_kv_lanes(jnp.dot(ds16, q_w, preferred_element_type=F32) * 0.125, half)
                dv_acc = dv_h if dv_acc is None else dv_acc + dv_h
                dk_acc = dk_h if dk_acc is None else dk_acc + dk_h
                ds_both.append(ds16)
                if variant == "b":
                    ex_ref[rows, half:half + 1] = -jnp.sum(ds, axis=1, keepdims=True)
                    dcq_ref[half:half + 1, pl.ds(start, n_q)] += jnp.sum(ds, axis=0, keepdims=True)
                elif variant == "c":
                    for r in range(n_rows // BLK):
                        for b in range(n_q // BLK):
                            ex_ref[half, jnp.clip(i0 + b - j - r, 0, 4)] += ds[r * BLK:(r + 1) * BLK, b * BLK:(b + 1) * BLK]
            dq_t = lax.dot_general(jnp.concatenate(k_al, axis=0), jnp.concatenate(ds_both, axis=0), TN,
                                   preferred_element_type=F32)
            dqt_ref[:, pl.ds(start, n_q)] += dq_t
            if shared_kv:
                @pl.when(p == 0)
                def _():
                    dk_ref[pl.ds(off_k, n_rows), :] = dk_acc
                    dv_ref[pl.ds(off_k, n_rows), :] = dv_acc

                @pl.when(p > 0)
                def _():
                    dk_ref[pl.ds(off_k, n_rows), :] += dk_acc
                    dv_ref[pl.ds(off_k, n_rows), :] += dv_acc
            else:
                dk_ref[pl.ds(off_k, n_rows), :] = dk_acc.astype(dk_ref.dtype)
                dv_ref[pl.ds(off_k, n_rows), :] = dv_acc.astype(dv_ref.dtype)
            if variant == "a":
                for half in (0, 1):
                    p_sink = jnp.exp(sink_ref[2 * p + half] - lse_ref[half:half + 1, pl.ds(off_k, n_rows)])
                    term = p_sink * dl_ref[half:half + 1, pl.ds(off_k, n_rows)]
                    ex_ref[half:half + 1, :] += -jnp.sum(term, axis=1, keepdims=True)

        step = pl.program_id(1)
        if variant == "b":
            for g in range(S // GROUP):
                pl.when(step == g)(functools.partial(compute, step * per_step, slice(0, GROUP), g * GROUP, S - g * GROUP))
        elif variant == "c":
            span = win + per_step - 1
            start = jnp.clip(step * per_step, 0, nb - span) * BLK
            compute(step * per_step, slice(0, per_step * BLK), pl.multiple_of(start, BLK), span * BLK)
        else:
            for sub in range(per_step):
                j = step * per_step + sub
                start = jnp.clip(j, 0, nb - win) * BLK
                compute(j, slice(sub * BLK, (sub + 1) * BLK), pl.multiple_of(start, BLK), win * BLK)

        @pl.when(step == nb // per_step - 1)
        def _():
            dq_ref[...] = jnp.transpose(dqt_ref[...]).astype(BF16)

    tk = per_step * BLK
    col = lambda c0: (lambda p, j: (0, c0 + p))
    kv_blk = (lambda c0: (lambda p, j: (j, c0))) if shared_kv else (lambda c0: (lambda p, j: (j, c0 + p)))
    pair = lambda p, j: (0, p)
    row_stat = pl.BlockSpec((None, 2, S), lambda p, j: (p, 0, 0))
    in_specs = [pl.BlockSpec((S, BLK), col(qb)),
                pl.BlockSpec((tk, BLK), kv_blk(kb)), pl.BlockSpec((tk, BLK), kv_blk(vb)),
                pl.BlockSpec((S, BLK), pair), row_stat, row_stat]
    args = [qkv, qkv, qkv, do, lse_row, delta_row]
    kv_width = BLK if shared_kv else 512
    kv_out = pl.BlockSpec((S, BLK), (lambda p, j: (0, 0)) if shared_kv else pair)
    kv_dtype = F32 if shared_kv else BF16
    out_shape = [jax.ShapeDtypeStruct((S, 512), BF16), jax.ShapeDtypeStruct((S, kv_width), kv_dtype),
                 jax.ShapeDtypeStruct((S, kv_width), kv_dtype)]
    out_specs = [pl.BlockSpec((S, BLK), pair), kv_out, kv_out]
    if variant == "a":
        in_specs += [pl.BlockSpec(memory_space=pltpu.SMEM), pl.BlockSpec(memory_space=pltpu.SMEM)]
        args += [sinks, slopes]
        out_shape.append(jax.ShapeDtypeStruct((4, 8, BLK), F32))
        out_specs.append(pl.BlockSpec((None, 8, BLK), lambda p, j: (p, 0, 0)))
    elif variant == "b":
        in_specs += [row_stat, pl.BlockSpec((None, tk, 2), lambda p, j: (p, j, 0))]
        args += [cq_row, ck_col]
        out_shape += [jax.ShapeDtypeStruct((4, S, 2), F32), jax.ShapeDtypeStruct((4, 2, S), F32)]
        out_specs += [pl.BlockSpec((None, tk, 2), lambda p, j: (p, j, 0)), row_stat]
    else:
        in_specs += [pl.BlockSpec((2, 5, BLK, BLK), lambda p, j: (p, 0, 0, 0))]
        args += [bias_t]
        out_shape.append(jax.ShapeDtypeStruct((8, 5, BLK, BLK), F32))
        out_specs.append(pl.BlockSpec((2, 5, BLK, BLK), lambda p, j: (p, 0, 0, 0)))
    return pl.pallas_call(
        body, name=name, out_shape=tuple(out_shape), grid=(4, nb // per_step),
        in_specs=in_specs, out_specs=tuple(out_specs), scratch_shapes=[pltpu.VMEM((BLK, S), F32)],
        compiler_params=_params(),
    )(*args)


def _log_sigmoid(x):
    return jnp.minimum(x, 0.0) - jnp.log(1.0 + jnp.exp(-jnp.abs(x)))


def _forget_fwd(fb, b_forget, name):
    S = fb.shape[0]
    nb = S // GROUP

    def body(fb_ref, b_ref, cum_ref, carry_ref):
        i = pl.program_id(0)
        logf = _log_sigmoid(fb_ref[...] + b_ref[...])
        r = lax.broadcasted_iota(jnp.int32, (GROUP, GROUP), 0)
        c = lax.broadcasted_iota(jnp.int32, (GROUP, GROUP), 1)
        tri = (c <= r).astype(F32)

        @pl.when(i == 0)
        def _():
            carry_ref[...] = jnp.zeros_like(carry_ref)

        cum = jnp.dot(tri, logf, preferred_element_type=F32, precision=HIGHEST) + carry_ref[0:1, :]
        cum_ref[...] = cum
        carry_ref[...] = jnp.broadcast_to(cum[GROUP - 1:GROUP, :], carry_ref.shape)

    return pl.pallas_call(
        body, name=name, out_shape=jax.ShapeDtypeStruct((S, BLK), F32), grid=(nb,),
        in_specs=[pl.BlockSpec((GROUP, BLK), lambda i: (i, 0)), pl.BlockSpec((1, BLK), lambda i: (0, 0))],
        out_specs=pl.BlockSpec((GROUP, BLK), lambda i: (i, 0)),
        scratch_shapes=[pltpu.VMEM((8, BLK), F32)],
        compiler_params=_params(),
    )(fb, b_forget)


def _forget_bwd(dcum_q, dcum_k, fb, b_forget, name):
    S = fb.shape[0]
    nb = S // GROUP

    def body(dq_ref, dk_ref, fb_ref, b_ref, dfb_ref, db_ref, carry_ref):
        g = pl.program_id(0)
        r = lax.broadcasted_iota(jnp.int32, (GROUP, GROUP), 0)
        c = lax.broadcasted_iota(jnp.int32, (GROUP, GROUP), 1)
        tri = (c >= r).astype(F32)

        @pl.when(g == 0)
        def _():
            carry_ref[...] = jnp.zeros_like(carry_ref)

        dcum = dq_ref[...] + dk_ref[...]
        dlogf = jnp.dot(tri, dcum, preferred_element_type=F32, precision=HIGHEST) + carry_ref[0:1, :]
        carry_ref[...] = jnp.broadcast_to(dlogf[0:1, :], carry_ref.shape)
        x = fb_ref[...] + b_ref[...]
        lane = lax.broadcasted_iota(jnp.int32, (1, BLK), 1)
        dfb = jnp.where(lane < N_FORGET, dlogf * jax.nn.sigmoid(-x), 0.0)
        dfb_ref[...] = dfb
        db = jnp.sum(dfb, axis=0, keepdims=True)

        @pl.when(g == 0)
        def _():
            db_ref[...] = db

        @pl.when(g > 0)
        def _():
            db_ref[...] += db

    rev = pl.BlockSpec((GROUP, BLK), lambda g: (nb - 1 - g, 0))
    row = pl.BlockSpec((1, BLK), lambda g: (0, 0))
    return pl.pallas_call(
        body, name=name,
        out_shape=(jax.ShapeDtypeStruct((S, BLK), F32), jax.ShapeDtypeStruct((1, BLK), F32)), grid=(nb,),
        in_specs=[rev, rev, rev, row], out_specs=(rev, row),
        scratch_shapes=[pltpu.VMEM((8, BLK), F32)],
        compiler_params=_params(),
    )(dcum_q, dcum_k, fb, b_forget)


def _skew(x, sign):
    row = lax.broadcasted_iota(jnp.int32, x.shape, 0)
    for b in range(7):
        amount = (1 << b) if sign > 0 else 256 - (1 << b)
        x = jnp.where(((row >> b) & 1) == 1, pltpu.roll(x, amount, 1), x)
    return x


def _rel_bases(rel):
    far = rel[:, 256:257]
    far127 = jnp.broadcast_to(far, (rel.shape[0], 127))
    base0 = jnp.concatenate([rel[:, 128:0:-1], far, rel[:, 255:128:-1]], axis=1)
    base1 = jnp.concatenate([rel[:, 256:128:-1], far, far127], axis=1)
    base0_t = jnp.concatenate([rel[:, 128:256], far, rel[:, 1:128]], axis=1)
    base1_t = jnp.concatenate([jnp.broadcast_to(far, (rel.shape[0], 128)), far, rel[:, 129:256]], axis=1)
    return jnp.stack([base0, base1, base0_t, base1_t], axis=1)


def _rel_expand(bases, name):
    def body(b_ref, t_ref, tt_ref):
        far = jnp.broadcast_to(b_ref[1:2, 0:1], (BLK, BLK))
        for k, out_ref in ((0, t_ref), (2, tt_ref)):
            for d in (0, 1):
                x = jnp.broadcast_to(b_ref[k + d:k + d + 1, :], (BLK, 2 * BLK))
                out_ref[d] = _skew(x, 1)[:, :BLK]
            for d in (2, 3, 4):
                out_ref[d] = far

    out = jax.ShapeDtypeStruct((8, 5, BLK, BLK), F32)
    spec = pl.BlockSpec((None, 5, BLK, BLK), lambda h: (h, 0, 0, 0))
    return pl.pallas_call(
        body, name=name, out_shape=(out, out), grid=(8,),
        in_specs=[pl.BlockSpec((None, 4, 2 * BLK), lambda h: (h, 0, 0))], out_specs=(spec, spec),
        compiler_params=_params(),
    )(bases)


def _rel_reduce(dtiles_t, name):
    def body(dt_ref, o_ref):
        zeros = jnp.zeros((BLK, BLK), F32)
        sums = []
        for d in (0, 1):
            x = _skew(jnp.concatenate([dt_ref[d], zeros], axis=1), -1)
            sums.append(jnp.broadcast_to(jnp.sum(x, axis=0, keepdims=True), (8, 2 * BLK)))
        lane = lax.broadcasted_iota(jnp.int32, (8, 2 * BLK), 1)
        main = pltpu.roll(sums[0], BLK, 1) + jnp.where(lane > BLK, sums[1], 0.0)
        far = jnp.sum(jnp.where(lane < BLK, sums[1], 0.0)[0:1], axis=1, keepdims=True)
        far = far + jnp.sum(jnp.sum(dt_ref[2] + dt_ref[3] + dt_ref[4], axis=0, keepdims=True), axis=1, keepdims=True)
        o_ref[...] = jnp.concatenate([main[0:1], jnp.broadcast_to(far, (1, BLK))], axis=1)

    return pl.pallas_call(
        body, name=name, out_shape=jax.ShapeDtypeStruct((8, 1, 3 * BLK), F32), grid=(8,),
        in_specs=[pl.BlockSpec((None, 5, BLK, BLK), lambda h: (h, 0, 0, 0))],
        out_specs=pl.BlockSpec((None, 1, 3 * BLK), lambda h: (h, 0, 0)),
        compiler_params=_params(),
    )(dtiles_t)


def _final_loss(x, target, g, below, name):
    S, D = x.shape
    ts = _row_tile(S, ROW_TILE)

    def body(x_ref, t_ref, g_ref, f_ref, gate_ref, dx_ref, loss_ref, dg_ref, dgate_ref, df_ref):
        i = pl.program_id(0)
        xv, gv = x_ref[...], g_ref[...]
        rstd = lax.rsqrt(jnp.mean(xv * xv, axis=-1, keepdims=True) + EPS)
        xhat = xv * rstd
        err = xhat * gv - t_ref[...]
        part = 0.5 * jnp.sum(jnp.mean(err * err, axis=-1, keepdims=True), axis=0, keepdims=True)
        dy = err / D
        dg = jnp.sum(dy * xhat, axis=0, keepdims=True)
        dxhat = dy * gv
        proj = jnp.mean(dxhat * xhat, axis=-1, keepdims=True)
        dx = rstd * (dxhat - xhat * proj)
        dx_ref[...] = dx
        dgate = _gated_residual_bwd(dx, f_ref, gate_ref, df_ref)
        _accumulate_rows(i, [(loss_ref, jnp.broadcast_to(part, loss_ref.shape)), (dg_ref, dg), (dgate_ref, dgate)])

    tile = pl.BlockSpec((ts, D), lambda i: (i, 0))
    row = pl.BlockSpec((1, D), lambda i: (0, 0))
    vec = jax.ShapeDtypeStruct((1, D), F32)
    return pl.pallas_call(
        body, name=name,
        out_shape=(jax.ShapeDtypeStruct((S, D), F32), jax.ShapeDtypeStruct((8, 128), F32), vec, vec,
                   jax.ShapeDtypeStruct((S, D), BF16)),
        grid=(S // ts,), in_specs=[tile, tile, row, tile, row],
        out_specs=(tile, pl.BlockSpec((8, 128), lambda i: (0, 0)), row, row, tile),
        compiler_params=_params(),
    )(x, target, g, *below)


def _ada_fwd(c_all, w_ada, name):
    L, D, E = w_ada.shape

    def body(c_ref, w_ref, o_ref):
        cv = c_ref[...]
        cond = cv * jax.nn.sigmoid(cv)
        o_ref[...] = jnp.dot(cond, w_ref[...], preferred_element_type=F32, precision=HIGHEST)

    return pl.pallas_call(
        body, name=name, out_shape=jax.ShapeDtypeStruct((L, N_DEV, E), F32), grid=(L,),
        in_specs=[pl.BlockSpec((N_DEV, D), lambda l: (0, 0)), pl.BlockSpec((None, D, E), lambda l: (l, 0, 0))],
        out_specs=pl.BlockSpec((None, N_DEV, E), lambda l: (l, 0, 0)),
        compiler_params=_params(),
    )(c_all, w_ada)


def _ada_bwd(c_all_t, dmod, name):
    D = c_all_t.shape[0]
    L, _, E = dmod.shape

    def body(c_ref, d_ref, o_ref):
        cv = c_ref[...]
        cond = cv * jax.nn.sigmoid(cv)
        acc = None
        for b in range(N_DEV):
            t = cond[:, b:b + 1] * d_ref[b:b + 1, :]
            acc = t if acc is None else acc + t
        o_ref[...] = acc

    return pl.pallas_call(
        body, name=name, out_shape=jax.ShapeDtypeStruct((L, D, E), F32), grid=(L,),
        in_specs=[pl.BlockSpec((D, N_DEV), lambda l: (0, 0)), pl.BlockSpec((None, N_DEV, E), lambda l: (l, 0, 0))],
        out_specs=pl.BlockSpec((None, D, E), lambda l: (l, 0, 0)),
        compiler_params=_params(),
    )(c_all_t, dmod)


def _adamw(w, m, v, g_parts, name, me, after=None):
    L, R, C = w.shape
    tr = _row_tile(R, max(8, (256 * 1024 // max(C, 128)) // 8 * 8))
    nr = R // tr
    c1 = 1.0 - ADAM_B1 ** ADAM_STEP
    c2 = 1.0 - ADAM_B2 ** ADAM_STEP
    direct = [isinstance(p, tuple) for p in g_parts]
    n_in = sum(2 if d else 1 for d in direct)

    def body(me_ref, w_ref, m_ref, v_ref, *rest):
        g_refs, (go_ref, d_ref, mo_ref, vo_ref) = list(rest[:n_in]), rest[-4:]
        layer = pl.program_id(0)
        g = None
        for l in range(L):
            land_ref = g_refs.pop(0)
            own = g_refs.pop(0)[...].astype(F32) if direct[l] else None
            gl = None
            for k in range(land_ref.shape[0]):
                part = land_ref[k].astype(F32)
                if direct[l]:
                    part = jnp.where(me_ref[l] == k, own, part)
                gl = part if gl is None else gl + part
            g = gl if g is None else jnp.where(layer == l, gl, g)
        mn = ADAM_B1 * m_ref[...] + (1.0 - ADAM_B1) * g
        vn = ADAM_B2 * v_ref[...] + (1.0 - ADAM_B2) * (g * g)
        m_hat = mn / c1
        v_hat = vn / c2
        go_ref[...] = g
        d_ref[...] = -ADAM_LR * (m_hat / (jnp.sqrt(v_hat) + ADAM_EPS) + ADAM_WD * w_ref[...])
        mo_ref[...] = mn
        vo_ref[...] = vn

    def rows(l, layer, i):
        return jnp.where(layer == l, i, 0 if l > 0 else nr - 1)

    in_specs, operands = [], []
    for l, p in enumerate(g_parts):
        land, sent = p if direct[l] else (p, None)
        in_specs.append(pl.BlockSpec((land.shape[0], tr, C), lambda layer, i, me_ref, l=l: (0, rows(l, layer, i), 0)))
        operands.append(land)
        if direct[l]:
            in_specs.append(pl.BlockSpec((None, tr, C), lambda layer, i, me_ref, l=l: (me_ref[l], rows(l, layer, i), 0)))
            operands.append(sent)
    if after is not None:
        in_specs.append(ANY)
        operands.append(after)
    tile = pl.BlockSpec((None, tr, C), lambda layer, i, me_ref: (layer, i, 0))
    out = jax.ShapeDtypeStruct((L, R, C), F32)
    return pl.pallas_call(
        body, name=name, out_shape=(out, out, out, out),
        grid_spec=pltpu.PrefetchScalarGridSpec(
            num_scalar_prefetch=1, grid=(L, nr), in_specs=[tile, tile, tile] + in_specs,
            out_specs=(tile, tile, tile, tile)),
        compiler_params=_params(),
    )(me, w, m, v, *operands)


def _pair_add(pieces, recv, core, name):
    _, _, R, C = pieces.shape
    tr = _row_tile(R, max(8, (512 * 1024 // max(C, 128)) // 8 * 8))

    def body(core_ref, a_ref, b_ref, o_ref):
        o_ref[...] = (a_ref[...].astype(F32) + b_ref[...].astype(F32)).astype(BF16)

    return pl.pallas_call(
        body, name=name, out_shape=jax.ShapeDtypeStruct((4, R, C), BF16),
        grid_spec=pltpu.PrefetchScalarGridSpec(
            num_scalar_prefetch=1, grid=(4, R // tr),
            in_specs=[pl.BlockSpec((None, None, tr, C), lambda k, i, core_ref: (core_ref[0], k, i, 0)),
                      pl.BlockSpec((None, tr, C), lambda k, i, core_ref: (k, i, 0))],
            out_specs=pl.BlockSpec((None, tr, C), lambda k, i, core_ref: (k, i, 0))),
        compiler_params=_params(),
    )(core, pieces, recv)


MESH = pl.DeviceIdType.MESH
ANY = pl.BlockSpec(memory_space=pl.ANY)


def _position():
    return lax.axis_index("x"), lax.axis_index("y"), lax.axis_index("c")


def _small_all_gather(v, name):
    m_per, n = v.shape

    def body(x_ref, out_ref, send_sems, recv_sems, local_sem):
        x, y, c = _position()
        me, sibling = (x, y, c), (x, y, 1 - c)
        chips = [(1 - x, y), (x, 1 - y), (1 - x, 1 - y)]

        def rows(px, py, pc):
            return out_ref.at[pl.ds((4 * px + 2 * py + pc) * m_per, m_per), :]

        def copy(k, block, to, src=None):
            return pltpu.make_async_remote_copy(
                src_ref=rows(*block) if src is None else src, dst_ref=rows(*block),
                send_sem=send_sems.at[k], recv_sem=recv_sems.at[k], device_id=to, device_id_type=MESH)

        mine = pltpu.make_async_copy(x_ref, rows(*me), local_sem)
        mine.start()
        first = [copy(0, me, sibling, src=x_ref)]
        first += [copy(1 + j, me, (*chip, c), src=x_ref) for j, chip in enumerate(chips)]
        for cp in first:
            cp.start()
        passed = [copy(4 + j, (*chip, c), sibling) for j, chip in enumerate(chips)]
        for j, chip in enumerate(chips):
            copy(1 + j, (*chip, c), me).wait_recv()
            passed[j].start()
        copy(0, sibling, me).wait_recv()
        for j, chip in enumerate(chips):
            copy(4 + j, (*chip, 1 - c), me).wait_recv()
        for cp in first + passed:
            cp.wait_send()
        mine.wait()

    return pl.pallas_call(
        body, name=name, out_shape=jax.ShapeDtypeStruct((N_DEV * m_per, n), v.dtype),
        in_specs=[pl.BlockSpec(memory_space=pltpu.VMEM)], out_specs=pl.BlockSpec(memory_space=pltpu.VMEM),
        scratch_shapes=[pltpu.SemaphoreType.DMA((7,)), pltpu.SemaphoreType.DMA((7,)), pltpu.SemaphoreType.DMA],
    )(v)


def _sibling_exchange(pieces, name):
    n_arr = len(pieces)

    def body(*refs):
        p_refs, out_refs = refs[:n_arr], refs[n_arr:2 * n_arr]
        send_sems, recv_sems = refs[2 * n_arr:]
        x, y, c = _position()
        copies = [pltpu.make_async_remote_copy(
            src_ref=p_refs[a].at[1 - c], dst_ref=out_refs[a], send_sem=send_sems.at[a], recv_sem=recv_sems.at[a],
            device_id=(x, y, 1 - c), device_id_type=MESH) for a in range(n_arr)]
        for cp in copies:
            cp.start()
        for cp in copies:
            cp.wait()

    return pl.pallas_call(
        body, name=name,
        out_shape=tuple(jax.ShapeDtypeStruct(p.shape[1:], p.dtype) for p in pieces),
        in_specs=[ANY] * n_arr, out_specs=tuple([ANY] * n_arr),
        scratch_shapes=[pltpu.SemaphoreType.DMA((n_arr,)), pltpu.SemaphoreType.DMA((n_arr,))],
    )(*pieces)


HBM = pl.BlockSpec(memory_space=pltpu.HBM)
SEM = pl.BlockSpec(memory_space=pltpu.SEMAPHORE)
EFFECT = pltpu.SideEffectType.DATAFLOW_SIDE_EFFECTING
RELATIONS = [(rx, ry, rc) for rx in (0, 1) for ry in (0, 1) for rc in (0, 1)][1:]


SAME_CORE = [r for r in RELATIONS if r == (0, 0, 1) or r[2] == 0]


CHIPS = [r for r in RELATIONS if r[2] == 0]


def _exchange_copies(src_refs, land_refs, send_sems, recv_sems, scatter, receive_side, relations):
    x, y, c = _position()
    index = (lambda px, py, pc: 2 * px + py) if relations == CHIPS else (lambda px, py, pc: 4 * px + 2 * py + pc)
    me = index(x, y, c)
    copies = []
    for k, (rx, ry, rc) in enumerate(relations):
        peer = ((1 - x) if rx else x, (1 - y) if ry else y, (1 - c) if rc else c)
        peer_index = index(*peer)
        for a, (src, land) in enumerate(zip(src_refs, land_refs)):
            copies.append(pltpu.make_async_remote_copy(
                src_ref=src.at[peer_index] if scatter else src,
                dst_ref=land.at[peer_index if receive_side else me],
                send_sem=send_sems.at[a * len(relations) + k], recv_sem=recv_sems.at[a * len(relations) + k],
                device_id=peer, device_id_type=MESH))
    return copies


def _exchange_start(srcs, scatter, after, name, relations=RELATIONS):
    n = len(srcs)
    land_shapes = [(s.shape if scatter else (N_DEV,) + s.shape) for s in srcs]

    def body(*refs):
        src_refs, land_refs = refs[:n], refs[n:2 * n]
        send_sems, recv_sems = refs[2 * n + 1], refs[2 * n + 2]
        token = refs[-1]
        for cp in _exchange_copies(src_refs, land_refs, send_sems, recv_sems, scatter, False, relations):
            cp.start()
        token[...] = jnp.zeros_like(token)

    sems = pltpu.SemaphoreType.DMA((n * len(relations),))
    outs = pl.pallas_call(
        body, name=name,
        out_shape=(sems, sems, *[pltpu.HBM(s.shape, s.dtype) for s in srcs],
                   *[pltpu.HBM(shape, s.dtype) for shape, s in zip(land_shapes, srcs)],
                   jax.ShapeDtypeStruct((8, 128), F32)),
        in_specs=[HBM] * (2 * n) + [ANY],
        out_specs=(SEM, SEM, *[HBM] * (2 * n), pl.BlockSpec(memory_space=pltpu.VMEM)),
        input_output_aliases={a: 2 + a for a in range(2 * n)},
        compiler_params=pltpu.CompilerParams(has_side_effects=EFFECT),
    )(*[pltpu.with_memory_space_constraint(s, pltpu.HBM) for s in srcs],
      *[pltpu.with_memory_space_constraint(lax.empty(shape, s.dtype), pltpu.HBM)
        for shape, s in zip(land_shapes, srcs)], after)
    return outs[0], outs[1], outs[2:2 + n], outs[2 + n:2 + 2 * n], outs[-1]


def _exchange_wait(started, scatter, after, name, relations=RELATIONS):
    send_sems, recv_sems, srcs, lands, _ = started
    n = len(srcs)

    def body(*refs):
        src_refs, land_refs = refs[:n], refs[n:2 * n]
        send_sems, recv_sems = refs[2 * n], refs[2 * n + 1]
        copies = _exchange_copies(src_refs, land_refs, send_sems, recv_sems, scatter, True, relations)
        for cp in copies:
            cp.wait_send()
        for cp in copies:
            cp.wait_recv()

    outs = pl.pallas_call(
        body, name=name,
        out_shape=(*[pltpu.HBM(s.shape, s.dtype) for s in srcs], *[pltpu.HBM(t.shape, t.dtype) for t in lands]),
        in_specs=[HBM] * (2 * n) + [SEM, SEM, ANY], out_specs=tuple([HBM] * (2 * n)),
        input_output_aliases={a: a for a in range(2 * n)},
        compiler_params=pltpu.CompilerParams(has_side_effects=EFFECT),
    )(*srcs, *lands, send_sems, recv_sems, after)
    return outs[:n], outs[n:]


def _forward_copies(land_refs, send_sems, recv_sems, receive_side):
    x, y, c = _position()
    copies = []
    for j, (px, py) in enumerate([(1 - x, y), (x, 1 - y), (1 - x, 1 - y)]):
        held, coming = 4 * px + 2 * py + c, 4 * px + 2 * py + (1 - c)
        for a, land in enumerate(land_refs):
            copies.append(pltpu.make_async_remote_copy(
                src_ref=land.at[held], dst_ref=land.at[coming if receive_side else held],
                send_sem=send_sems.at[3 * a + j], recv_sem=recv_sems.at[3 * a + j],
                device_id=(x, y, 1 - c), device_id_type=MESH))
    return copies


def _forward_start(lands, after, name):
    n = len(lands)

    def body(*refs):
        send_sems, recv_sems, token = refs[n + 1], refs[n + 2], refs[-1]
        for cp in _forward_copies(refs[:n], send_sems, recv_sems, False):
            cp.start()
        token[...] = jnp.zeros_like(token)

    sems = pltpu.SemaphoreType.DMA((3 * n,))
    outs = pl.pallas_call(
        body, name=name,
        out_shape=(sems, sems, *[pltpu.HBM(t.shape, t.dtype) for t in lands], jax.ShapeDtypeStruct((8, 128), F32)),
        in_specs=[HBM] * n + [ANY], out_specs=(SEM, SEM, *[HBM] * n, pl.BlockSpec(memory_space=pltpu.VMEM)),
        input_output_aliases={a: 2 + a for a in range(n)},
        compiler_params=pltpu.CompilerParams(has_side_effects=EFFECT),
    )(*lands, after)
    return outs[0], outs[1], outs[2:2 + n], outs[-1]


def _forward_wait(started, after, name):
    send_sems, recv_sems, lands, _ = started
    n = len(lands)

    def body(*refs):
        copies = _forward_copies(refs[:n], refs[n], refs[n + 1], True)
        for cp in copies:
            cp.wait_send()
        for cp in copies:
            cp.wait_recv()

    return pl.pallas_call(
        body, name=name, out_shape=tuple(pltpu.HBM(t.shape, t.dtype) for t in lands),
        in_specs=[HBM] * n + [SEM, SEM, ANY], out_specs=tuple([HBM] * n),
        input_output_aliases={a: a for a in range(n)},
        compiler_params=pltpu.CompilerParams(has_side_effects=EFFECT),
    )(*lands, send_sems, recv_sems, after)


def _place_own(lands, mine, me, name):
    n = len(lands)
    flat = [m.reshape(-1, m.shape[-1]) for m in mine]
    flat_lands = [t.reshape(N_DEV, -1, t.shape[-1]) for t in lands]

    def body(me_ref, *refs):
        for src, dst in zip(refs[:n], refs[2 * n:]):
            dst[...] = src[...]

    in_specs = [pl.BlockSpec((m.shape[0] // 2, m.shape[1]), lambda i, me_ref: (i, 0)) for m in flat]
    out_specs = [pl.BlockSpec((None, m.shape[0] // 2, m.shape[1]), lambda i, me_ref: (me_ref[0], i, 0)) for m in flat]
    outs = pl.pallas_call(
        body, name=name, out_shape=tuple(jax.ShapeDtypeStruct(t.shape, t.dtype) for t in flat_lands),
        grid_spec=pltpu.PrefetchScalarGridSpec(
            num_scalar_prefetch=1, grid=(2,), in_specs=in_specs + [ANY] * n, out_specs=tuple(out_specs)),
        input_output_aliases={1 + n + a: a for a in range(n)},
        compiler_params=_params(),
    )(me, *flat, *flat_lands)
    return [o.reshape(t.shape) for o, t in zip(outs, lands)]


W_IN_SHARD = N_IN // N_DEV
F_SHARD = F_COL // W_IN_SHARD
F_LO = F_COL - F_SHARD * W_IN_SHARD


def _w_ffn_in_view(w):
    return jnp.transpose(w, (0, 2, 1))


def _w_in_segments():
    segments = []
    for d in range(N_DEV):
        if d == F_SHARD:
            segments += [(d, 0, d * W_IN_SHARD, F_LO), (d, F_LO, N_MAIN, N_FORGET),
                         (d, F_LO + N_FORGET, F_COL, W_IN_SHARD - F_LO - N_FORGET)]
        else:
            segments.append((d, 0, d * W_IN_SHARD - (N_FORGET if d > F_SHARD else 0), W_IN_SHARD))
    return segments


def _w_in_rearranged(g, name):
    D = g.shape[1]
    tr = _row_tile(D, 256)

    def body(g_ref, o_ref):
        for d, lo, at, width in _w_in_segments():
            o_ref[:, at:at + width] = g_ref[d, :, lo:lo + width]
        o_ref[:, N_IN:] = jnp.zeros((tr, BLK - N_FORGET), o_ref.dtype)

    return pl.pallas_call(
        body, name=name, out_shape=jax.ShapeDtypeStruct((D, N_MAIN + BLK), g.dtype), grid=(D // tr,),
        in_specs=[pl.BlockSpec((N_DEV, tr, W_IN_SHARD), lambda i: (0, i, 0))],
        out_specs=pl.BlockSpec((tr, N_MAIN + BLK), lambda i: (i, 0)),
        compiler_params=_params(),
    )(g)


def _w_in_pieces(dw_r, name, pair_major=False):
    D = dw_r.shape[0]
    tr = _row_tile(D, 256)
    lead = (2, 4) if pair_major else (N_DEV,)

    def body(x_ref, o_ref):
        for d, lo, at, width in _w_in_segments():
            slot = (d % 2, d // 2) if pair_major else (d,)
            o_ref[(*slot, slice(None), slice(lo, lo + width))] = x_ref[:, at:at + width]

    return pl.pallas_call(
        body, name=name, out_shape=jax.ShapeDtypeStruct((*lead, D, W_IN_SHARD), dw_r.dtype), grid=(D // tr,),
        in_specs=[pl.BlockSpec((tr, N_MAIN + BLK), lambda i: (i, 0))],
        out_specs=pl.BlockSpec((*lead, tr, W_IN_SHARD), lambda i: (*[0] * len(lead), i, 0)),
        compiler_params=_params(),
    )(dw_r)


def _row_pieces(dw):
    return dw.reshape(N_DEV, dw.shape[0] // N_DEV, dw.shape[1])


def _branch_pieces(dw):
    k, w, d = dw.shape
    return jnp.transpose(dw.reshape(k, w, N_DEV, d // N_DEV), (2, 0, 1, 3)).reshape(N_DEV, k * w, d // N_DEV)


def _pairs_col(a):
    return jnp.transpose(a.reshape(a.shape[0], 4, 2), (1, 0, 2))


def _pairs_row(a):
    return jnp.transpose(a.reshape(a.shape[0], 4, 2), (1, 2, 0))


def _heads_from_col(a):
    return jnp.transpose(a, (1, 0, 2)).reshape(a.shape[1], 8)


def _heads_from_row(a):
    return jnp.transpose(a, (2, 0, 1)).reshape(a.shape[2], 8)


def _pad_lanes(a, n):
    return jnp.pad(a, [(0, 0)] * (a.ndim - 1) + [(0, n - a.shape[-1])])


SMALL_SEGMENTS = (("dmod", 2 * 6 * D_MODEL), ("norm_mix_g", 2 * D_MODEL), ("norm_ffn_g", 2 * D_MODEL),
                  ("final_norm_g", D_MODEL), ("b_forget", 128), ("sinks", 128), ("rel_bias", 4224), ("loss", 128))
SMALL_ROWS = 176


def _pack_small(parts):
    flat = [_pad_lanes(parts[name].reshape(1, -1), size) for name, size in SMALL_SEGMENTS]
    total = sum(size for _, size in SMALL_SEGMENTS)
    flat.append(jnp.zeros((1, SMALL_ROWS * 128 - total), F32))
    return jnp.concatenate(flat, axis=1).reshape(SMALL_ROWS, 128)


def _unpack_small(packed, shapes):
    flat = packed.reshape(-1)
    out, pos = {}, 0
    for name, size in SMALL_SEGMENTS:
        shape = shapes[name]
        count = 1
        for d in shape:
            count *= d
        out[name] = flat[pos:pos + count].reshape(shape)
        pos += size
    return out


def kernel(x, c, norm_mix_g, norm_ffn_g, w_ada, b_ada, w_in, b_forget, sinks, rel_bias, w_branch, w_out, w_ffn_in, w_ffn_out, final_norm_g, loss_target, m_norm_mix_g, m_norm_ffn_g, m_w_ada, m_b_ada, m_w_in, m_b_forget, m_sinks, m_rel_bias, m_w_branch, m_w_out, m_w_ffn_in, m_w_ffn_out, m_final_norm_g, v_norm_mix_g, v_norm_ffn_g, v_w_ada, v_b_ada, v_w_in, v_b_forget, v_sinks, v_rel_bias, v_w_branch, v_w_out, v_w_ffn_in, v_w_ffn_out, v_final_norm_g):
    depth = w_in.shape[0]
    S, D = x.shape[1], x.shape[2]
    assert S % GROUP == 0 and S >= ATTN_WINDOW["c"] * BLK
    px, py, pc = _position()
    me = 4 * px + 2 * py + pc
    x0 = x[0]

    assert depth == 2
    big_weights = (w_in, w_branch, w_out, w_ffn_in, w_ffn_out)
    me_arr = jnp.stack([me, me]).astype(jnp.int32)
    me_in_arr = jnp.stack([2 * px + py, me]).astype(jnp.int32)

    def rest_matrices(g_branch, g_out, g_fin, g_fout):
        return (jnp.transpose(g_branch, (1, 2, 0, 3)).reshape(3, 512, D), g_out.reshape(D, D),
                g_fin.reshape(2 * FFN_HIDDEN, D), g_fout.reshape(FFN_HIDDEN, D))

    def arrive(started, after, name):
        mine, landed = _exchange_wait(started, False, after, f"{name}_wait", SAME_CORE)
        return mine, _forward_start(landed, mine[0], f"{name}_forward_start")

    def finish_gather(arrived, after, name):
        mine, forward = arrived
        landed = _forward_wait(forward, after, f"{name}_forward_wait")
        return _place_own(landed, mine, me.astype(jnp.int32).reshape(1), f"{name}_own")

    w_fin_t = _w_ffn_in_view(w_ffn_in)
    shards = [[t.astype(BF16) for t in (w_in[l], w_branch[l], w_out[l], w_fin_t[l], w_ffn_out[l])]
              for l in range(depth)]
    c_all = _small_all_gather(c.reshape(8, 128), "comm_gather_c").reshape(N_DEV, D)
    mod_cols = _ada_fwd(c_all, w_ada, "ada_fwd")
    mod_all = _small_all_gather(mod_cols.reshape(-1, 128), "comm_gather_mod")
    gather_in0 = _exchange_start(shards[0][:1], False, mod_all, "comm_gather_w_in0_start", SAME_CORE)
    gather_rest0 = _exchange_start(shards[0][1:], False, gather_in0[4], "comm_gather_rest0_start", SAME_CORE)
    gather1 = _exchange_start(shards[1], False, gather_rest0[4], "comm_gather_weights1_start", SAME_CORE)
    started = gather1[4][0:1, 0:1]
    W_in, W_branch, W_out, W_fin, W_fout = ([None, None] for _ in range(5))
    mod_all = mod_all.reshape(N_DEV, depth, N_DEV, w_ada.shape[2])
    mod_mine = lax.dynamic_index_in_dim(mod_all, me, axis=2, keepdims=False)
    mod = jnp.transpose(mod_mine, (1, 0, 2)).reshape(depth, 6 * D) + b_ada + started
    mods = [[mod[l:l + 1, k * D:(k + 1) * D] for k in range(6)] for l in range(depth)]
    rel_tiles = [_rel_expand(_rel_bases(rel_bias[l]) + started, f"rel_expand{l}") for l in range(depth)]

    slopes = jnp.exp2(-jnp.arange(1, 9, dtype=F32))
    saved = []
    xs = x0
    for l in range(depth):
        if l == 1:
            g_in1, *g_rest1 = finish_gather(arrived1, xs, "comm_gather_weights1")
            W_in[1] = _w_in_rearranged(g_in1, "w_in_rearrange1")
            W_branch[1], W_out[1], W_fin[1], W_fout[1] = rest_matrices(*g_rest1)
        sh_m, sc_m, g_m, sh_f, sc_f, g_f = mods[l]
        gm, gf = norm_mix_g[l:l + 1], norm_ffn_g[l:l + 1]
        bfor = _pad_lanes(b_forget[l:l + 1], BLK)
        h = _norm_mod_fwd(xs, gm, sh_m, sc_m, f"norm_mix_fwd{l}")
        tiles, tiles_t = rel_tiles[l]
        if l == 0:
            arrived_in0 = arrive(gather_in0, rel_tiles[-1][1], "comm_gather_w_in0")
            W_in[0] = _w_in_rearranged(finish_gather(arrived_in0, h, "comm_gather_w_in0")[0], "w_in_rearrange0")
        qkv, gates = _project(h, W_in[l], f"proj{l}")
        fb = _matmul(h, W_in[l], "nn", F32, f"proj_forget{l}", TILES["proj_forget"], n=BLK, b_off=N_MAIN // BLK)
        cum = _forget_fwd(fb, bfor, f"forget_fwd{l}")[:, :N_FORGET]
        cum_col, cum_row = _pairs_col(cum), _pairs_row(cum)
        o_a, lse_a = _attn_fwd("a", qkv, f"attn_a_fwd{l}", sinks=sinks[l], slopes=slopes)
        o_b, lse_b = _attn_fwd("b", qkv, f"attn_b_fwd{l}", cq_col=cum_col, ck_row=cum_row)
        arrived_rest0 = arrive(gather_rest0, o_b, "comm_gather_rest0") if l == 0 else None
        o_c, lse_c = _attn_fwd("c", qkv, f"attn_c_fwd{l}", bias=tiles, after=arrived_rest0[1][3] if l == 0 else None)
        if l == 0:
            W_branch[0], W_out[0], W_fin[0], W_fout[0] = rest_matrices(
                *finish_gather(arrived_rest0, o_c, "comm_gather_rest0"))
        x1, merged, mix = _merge_fwd(o_a, o_b, o_c, gates, W_branch[l], W_out[l], xs, g_m, f"merge_fwd{l}")
        h2 = _norm_mod_fwd(x1, gf, sh_f, sc_f, f"norm_ffn_fwd{l}")
        act = _ffn_in_fwd(h2, W_fin[l], f"ffn_in_fwd{l}")
        if l == 0:
            arrived1 = arrive(gather1, act, "comm_gather_weights1")
        x2, ffn = _matmul_resid(act, W_fout[l], x1, g_f, f"ffn_out{l}", TILES["ffn_out"],
                                after=arrived1[1][3] if l == 0 else None)
        saved.append(dict(x=xs, h=h, qkv=qkv, gates=gates, fb=fb, bfor=bfor, cum_col=cum_col, cum_row=cum_row,
                          tiles_t=tiles_t, o=(o_a, o_b, o_c), lse=(lse_a, lse_b, lse_c), merged=merged, mix=mix,
                          x1=x1, h2=h2, act=act, ffn=ffn))
        xs = x2

    dx, loss_tile, d_final_g, d_g_f, df = _final_loss(
        xs, loss_target[0], final_norm_g.reshape(1, D), (saved[-1]["ffn"], mods[-1][5]), "final_loss")

    grads = {k: [None] * depth for k in ("w_in", "w_branch", "w_out", "w_ffn_in", "w_ffn_out", "norm_mix_g",
                                          "norm_ffn_g", "b_forget", "sinks", "rel_bias", "dmod")}
    def rest_pieces(l):
        return [_branch_pieces(grads["w_branch"][l]), _row_pieces(grads["w_out"][l]),
                _row_pieces(grads["w_ffn_in"][l]), _row_pieces(grads["w_ffn_out"][l])]

    reduce1 = reduce_rest0 = reduce_in0 = None
    for l in reversed(range(depth)):
        sv = saved[l]
        sh_m, sc_m, g_m, sh_f, sc_f, g_f = mods[l]
        gm, gf = norm_mix_g[l:l + 1], norm_ffn_g[l:l + 1]
        du_g, du_u = _ffn_mid_bwd(sv["h2"], df, W_fin[l], W_fout[l], f"ffn_mid_bwd{l}")
        du = jnp.concatenate([du_g, du_u], axis=1)
        grads["w_ffn_out"][l] = _matmul(sv["act"], df, "tn", BF16, f"wgrad_ffn_out{l}", TILES["wgrad_ffn_out"])
        grads["w_ffn_in"][l] = _matmul(du, sv["h2"], "tn", BF16, f"wgrad_ffn_in{l}", TILES["wgrad_ffn_in"])
        dh2 = _matmul(du, W_fin[l], "nn", F32, f"dgrad_ffn_in{l}", TILES["dgrad_ffn_in"])
        dx1, d_sh_f, d_sc_f, d_gf, d_g_m, dmix = _norm_mod_bwd(sv["x1"], dh2, dx, gf, sc_f, f"norm_ffn_bwd{l}",
                                                               below=(sv["mix"], g_m))
        grads["w_out"][l] = _matmul(sv["merged"], dmix, "tn", BF16, f"wgrad_out{l}", TILES["wgrad_out"])
        o_a, o_b, o_c = sv["o"]
        dgates, d_w_branch, do_a, do_b, do_c, dl_a, dl_b, dl_c = _merge_bwd(
            dmix, o_a, o_b, o_c, sv["gates"], W_branch[l], W_out[l], f"merge_bwd{l}")
        grads["w_branch"][l] = d_w_branch.astype(BF16)
        lse_rows = list(sv["lse"])
        if l == 0:
            reduce_rest0 = _exchange_start(rest_pieces(0), True, dgates, "comm_reduce_rest0_start")
            lse_rows = [t + reduce_rest0[4][0:1, 0:1] for t in lse_rows]
        dq_a, dk_a, dv_a, dsink = _attn_bwd("a", sv["qkv"], do_a, lse_rows[0], dl_a.reshape(4, 2, S), f"attn_a_bwd{l}",
                                            sinks=sinks[l], slopes=slopes)
        dq_b, dk_b, dv_b, dck, dcq = _attn_bwd("b", sv["qkv"], do_b, lse_rows[1], dl_b.reshape(4, 2, S),
                                               f"attn_b_bwd{l}", cq_row=sv["cum_row"], ck_col=sv["cum_col"])
        dq_c, dk_c, dv_c, dtiles_t = _attn_bwd("c", sv["qkv"], do_c, lse_rows[2], dl_c.reshape(4, 2, S),
                                               f"attn_c_bwd{l}", bias_t=sv["tiles_t"])
        grads["sinks"][l] = dsink[:, :2, 0].reshape(8)
        grads["rel_bias"][l] = _rel_reduce(dtiles_t, f"rel_reduce{l}")[:, 0, :N_REL]
        dcum_k = _pad_lanes(_heads_from_col(dck), BLK)
        dcum_q = _pad_lanes(_heads_from_row(dcq), BLK)
        dfb, d_bfor = _forget_bwd(dcum_q, dcum_k, sv["fb"], sv["bfor"], f"forget_bwd{l}")
        grads["b_forget"][l] = d_bfor[0, :N_FORGET]
        dproj = jnp.concatenate(
            [t.astype(BF16) for t in (dq_a, dk_a, dv_a, dq_b, dk_b, dv_b, dq_c, dk_c, dv_c, dgates, dfb)],
            axis=1)
        grads["w_in"][l] = _matmul(sv["h"], dproj, "tn", BF16, f"wgrad_in{l}", TILES["wgrad_in"])
        if l == 1:
            reduce1 = _exchange_start([_w_in_pieces(grads["w_in"][1], "w_in_pieces1")] + rest_pieces(1), True, dproj,
                                      "comm_reduce1_start")
        dh = _matmul(dproj, W_in[l], "nt", F32, f"dgrad_in{l}", TILES["dgrad_in"], after=reduce1[4] if l == 1 else None)
        d_g_f_here = d_g_f
        if l > 0:
            dx, d_sh_m, d_sc_m, d_gm, d_g_f, df = _norm_mod_bwd(sv["x"], dh, dx1, gm, sc_m, f"norm_mix_bwd{l}",
                                                                below=(saved[l - 1]["ffn"], mods[l - 1][5]))
        else:
            dx, d_sh_m, d_sc_m, d_gm = _norm_mod_bwd(sv["x"], dh, dx1, gm, sc_m, f"norm_mix_bwd{l}")
        grads["norm_mix_g"][l] = d_gm[0]
        grads["norm_ffn_g"][l] = d_gf[0]
        grads["dmod"][l] = jnp.concatenate([d_sh_m, d_sc_m, d_g_m, d_sh_f, d_sc_f, d_g_f_here], axis=1)[0]

    grad_x = dx.reshape(x.shape)

    small_shapes = dict(dmod=b_ada.shape, norm_mix_g=norm_mix_g.shape, norm_ffn_g=norm_ffn_g.shape,
                        final_norm_g=final_norm_g.shape, b_forget=b_forget.shape, sinks=sinks.shape,
                        rel_bias=rel_bias.shape, loss=())
    mine_small = _pack_small(dict(
        loss=_pad_lanes(loss_tile[0:1, 0:1], 128),
        dmod=jnp.stack(grads["dmod"]), norm_mix_g=jnp.stack(grads["norm_mix_g"]),
        norm_ffn_g=jnp.stack(grads["norm_ffn_g"]), final_norm_g=d_final_g[0],
        b_forget=_pad_lanes(jnp.stack(grads["b_forget"]).reshape(1, -1), 128),
        sinks=_pad_lanes(jnp.stack(grads["sinks"]).reshape(1, -1), 128),
        rel_bias=_pad_lanes(jnp.stack(grads["rel_bias"]).reshape(1, -1), 4224)))
    all_small = _small_all_gather(mine_small, "comm_gather_small").reshape(N_DEV, SMALL_ROWS, 128)
    pieces_in0 = _w_in_pieces(grads["w_in"][0], "w_in_pieces0", pair_major=True)
    from_sibling = _sibling_exchange([pieces_in0], "comm_reduce_in0_sibling")[0]
    pair_sum_in0 = _pair_add(pieces_in0, from_sibling, pc.astype(jnp.int32).reshape(1), "pair_add_in0")
    reduce_in0 = _exchange_start([pair_sum_in0], True, all_small, "comm_reduce_in0_start", CHIPS)
    in0_started = reduce_in0[4]

    def pack_params(b_ada_, nm, nf, fn, bf, sk, rb):
        return _pack_small(dict(dmod=b_ada_, norm_mix_g=nm, norm_ffn_g=nf, final_norm_g=fn, loss=jnp.zeros((1, 128), F32),
                                b_forget=_pad_lanes(bf.reshape(1, -1), 128), sinks=_pad_lanes(sk.reshape(1, -1), 128),
                                rel_bias=_pad_lanes(rb.reshape(1, -1), 4224)))

    small_out = _adamw(
        pack_params(b_ada, norm_mix_g, norm_ffn_g, final_norm_g, b_forget, sinks, rel_bias)[None],
        pack_params(m_b_ada, m_norm_mix_g, m_norm_ffn_g, m_final_norm_g, m_b_forget, m_sinks, m_rel_bias)[None],
        pack_params(v_b_ada, v_norm_mix_g, v_norm_ffn_g, v_final_norm_g, v_b_forget, v_sinks, v_rel_bias)[None],
        [all_small], "adamw_small", me_arr, after=in0_started)
    small_out = [_unpack_small(t[0], small_shapes) for t in small_out]

    dmod_all = all_small[:, :96].reshape(N_DEV, depth, 6 * D)
    dmod_cols = lax.dynamic_slice_in_dim(dmod_all, me * w_ada.shape[2], w_ada.shape[2], axis=2)
    d_w_ada = _ada_bwd(jnp.transpose(c_all), jnp.transpose(dmod_cols, (1, 0, 2)), "ada_bwd")

    big = {"w_ada": _adamw(w_ada, m_w_ada, v_w_ada, [d_w_ada[l:l + 1] for l in range(depth)], "adamw_w_ada", me_arr,
                           after=in0_started)}
    sent1, landed1 = _exchange_wait(reduce1, True, big["w_ada"][0], "comm_reduce1_wait")
    sent_rest0, landed_rest0 = _exchange_wait(reduce_rest0, True, landed1[0], "comm_reduce_rest0_wait")
    parts = {"w_in": [None, (landed1[0], sent1[0])]}
    for a, name in enumerate(("w_branch", "w_out", "w_ffn_in", "w_ffn_out")):
        parts[name] = [(landed_rest0[a], sent_rest0[a]), (landed1[1 + a], sent1[1 + a])]

    def update(name, w, m, v, view=lambda t: t):
        per_layer = lambda t: t.reshape(depth, -1, t.shape[-1])
        outs = _adamw(*[per_layer(view(t)) for t in (w, m, v)], parts[name], f"adamw_{name}",
                      me_in_arr if name == "w_in" else me_arr)
        big[name] = [view(t).reshape(w.shape) for t in outs]

    update("w_ffn_in", w_ffn_in, m_w_ffn_in, v_w_ffn_in, _w_ffn_in_view)
    update("w_ffn_out", w_ffn_out, m_w_ffn_out, v_w_ffn_out)
    update("w_branch", w_branch, m_w_branch, v_w_branch)
    update("w_out", w_out, m_w_out, v_w_out)
    sent_in0, landed_in0 = _exchange_wait(reduce_in0, True, big["w_out"][0], "comm_reduce_in0_wait", CHIPS)
    parts["w_in"][0] = (landed_in0[0], sent_in0[0])
    update("w_in", w_in, m_w_in, v_w_in)

    def leaf(kind, name):
        if name in big:
            return big[name][kind]
        return small_out[kind]["dmod" if name == "b_ada" else name]

    order = ["norm_mix_g", "norm_ffn_g", "w_ada", "b_ada", "w_in", "b_forget", "sinks", "rel_bias", "w_branch",
             "w_out", "w_ffn_in", "w_ffn_out", "final_norm_g"]
    loss = small_out[0]["loss"]
    return (loss, grad_x, *[leaf(0, n) for n in order], *[leaf(1, n) for n in order],
            *[leaf(2, n) for n in order], *[leaf(3, n) for n in order])
```

```python
import functools

import jax
import jax.numpy as jnp
from jax import lax
from jax.experimental import pallas as pl
from jax.experimental.pallas import tpu as pltpu

F32 = jnp.float32
BF16 = jnp.bfloat16
NEG_INF = -1e30
EPS = 1e-6
N_DEV = 8
BLK = 128
GROUP = 4 * BLK
VMEM_LIMIT_BYTES = 56 * 1024 * 1024

D_MODEL = 1024
N_QKV = 3840
N_GATES = 3072
N_MAIN = N_QKV + N_GATES
N_FORGET = 8
N_IN = N_MAIN + N_FORGET
F_COL = 2304
FFN_HIDDEN = 2816
N_REL = 257

ADAM_LR, ADAM_B1, ADAM_B2, ADAM_EPS, ADAM_WD, ADAM_STEP = 0.001, 0.9, 0.999, 1e-08, 0.01, 10

NN = (((1,), (0,)), ((), ()))
NT = (((1,), (1,)), ((), ()))
TN = (((0,), (0,)), ((), ()))
HIGHEST = lax.Precision.HIGHEST

ATTN_COLS = {"a": (0, 4, 5), "b": (6, 10, 14), "c": (18, 22, 26)}
ATTN_WINDOW = {"a": 2, "c": 5}
ATTN_BLOCKS_PER_STEP = {"a": 8, "b": GROUP // BLK, "c": 2}
ROW_TILE = 512


def _params():
    return pltpu.CompilerParams(vmem_limit_bytes=VMEM_LIMIT_BYTES)


def _tile(n, target):
    best = None
    t = 128
    while t <= min(n, target):
        if n % t == 0:
            best = t
        t += 128
    return best if best is not None else n


def _row_tile(n, target):
    t = min(n, target)
    while n % t:
        t -= 8
    return t


TILES = {
    "proj": (2048, 768, 1024), "proj_forget": (1024, 128, 1024),
    "ffn_out": (1024, 512, 2816), "ffn_fused": (512, 1408),
    "wgrad_ffn_out": (1408, 1024, 2048), "wgrad_ffn_in": (1408, 1024, 2048), "dgrad_ffn_in": (2048, 1024, 1408),
    "wgrad_out": (1024, 1024, 2048),
    "wgrad_in": (1024, 1408, 2048), "dgrad_in": (2048, 1024, 1408),
}


def _matmul(a, b, mode, out_dtype, name, tiles, *, n=None, a_off=0, b_off=0, m=None, after=None):
    tm, tn, tk = tiles
    if mode == "nn":
        M, K = a.shape if m is None else (m, a.shape[1])
        N = b.shape[1] if n is None else n
    elif mode == "nt":
        M, K = a.shape
        N = b.shape[0] if n is None else n
    else:
        K = a.shape[0]
        M = a.shape[1] if m is None else m
        N = b.shape[1] if n is None else n
    tm = _tile(M, tm) if M % 128 == 0 else M
    tn = _tile(N, tn)
    tk = _tile(K, tk)
    nk = K // tk
    dims = {"nn": NN, "nt": NT, "tn": TN}[mode]
    if mode == "nn":
        a_spec = pl.BlockSpec((tm, tk), lambda i, j, k: (i + a_off, k))
        b_spec = pl.BlockSpec((tk, tn), lambda i, j, k: (k, j + b_off))
    elif mode == "nt":
        a_spec = pl.BlockSpec((tm, tk), lambda i, j, k: (i + a_off, k))
        b_spec = pl.BlockSpec((tn, tk), lambda i, j, k: (j + b_off, k))
    else:
        a_spec = pl.BlockSpec((tk, tm), lambda i, j, k: (k, i + a_off))
        b_spec = pl.BlockSpec((tk, tn), lambda i, j, k: (k, j + b_off))

    def body(a_ref, b_ref, *rest):
        o_ref, acc_ref = rest[-2:]
        k = pl.program_id(2)
        part = lax.dot_general(a_ref[...], b_ref[...], dims, preferred_element_type=F32)
        if nk == 1:
            o_ref[...] = part.astype(o_ref.dtype)
        else:
            @pl.when(k == 0)
            def _():
                acc_ref[...] = part

            @pl.when(k > 0)
            def _():
                acc_ref[...] += part

            @pl.when(k == nk - 1)
            def _():
                o_ref[...] = acc_ref[...].astype(o_ref.dtype)

    return pl.pallas_call(
        body, name=name,
        out_shape=jax.ShapeDtypeStruct((M, N), out_dtype),
        grid=(M // tm, N // tn, nk),
        in_specs=[a_spec, b_spec] + ([ANY] if after is not None else []),
        out_specs=pl.BlockSpec((tm, tn), lambda i, j, k: (i, j)),
        scratch_shapes=[pltpu.VMEM((tm, tn) if nk > 1 else (8, 128), F32)],
        compiler_params=_params(),
    )(a, b, *([after] if after is not None else []))


def _project(h, w, name):
    S, D = h.shape
    tm, tn, _ = TILES["proj"]
    tm = _tile(S, tm)
    nq, ng = N_QKV // tn, N_GATES // tn

    def body(h_ref, w_ref, q_ref, g_ref):
        j = pl.program_id(1)
        acc = jnp.dot(h_ref[...], w_ref[...], preferred_element_type=F32)

        @pl.when(j < nq)
        def _():
            q_ref[...] = acc.astype(BF16)

        @pl.when(j >= nq)
        def _():
            g_ref[...] = acc

    return pl.pallas_call(
        body, name=name,
        out_shape=(jax.ShapeDtypeStruct((S, N_QKV), BF16), jax.ShapeDtypeStruct((S, N_GATES), F32)),
        grid=(S // tm, nq + ng),
        in_specs=[pl.BlockSpec((tm, D), lambda i, j: (i, 0)), pl.BlockSpec((D, tn), lambda i, j: (0, j))],
        out_specs=(pl.BlockSpec((tm, tn), lambda i, j: (i, jnp.minimum(j, nq - 1))),
                   pl.BlockSpec((tm, tn), lambda i, j: (i, jnp.maximum(j - nq, 0)))),
        compiler_params=_params(),
    )(h, w)


def _matmul_resid(a, b, resid, gate, name, tiles, after=None):
    M, K = a.shape
    N = b.shape[1]
    tm, tn, tk = (_tile(d, t) for d, t in zip((M, N, K), tiles))
    nk = K // tk

    def body(a_ref, b_ref, r_ref, g_ref, *rest):
        o_ref, s_ref, acc_ref = rest[-3:]
        k = pl.program_id(2)
        part = jnp.dot(a_ref[...], b_ref[...], preferred_element_type=F32)

        def finish(acc):
            o_ref[...] = r_ref[...] + g_ref[...] * acc
            s_ref[...] = acc.astype(BF16)

        if nk == 1:
            finish(part)
        else:
            @pl.when(k == 0)
            def _():
                acc_ref[...] = part

            @pl.when(k > 0)
            def _():
                acc_ref[...] += part

            @pl.when(k == nk - 1)
            def _():
                finish(acc_ref[...])

    return pl.pallas_call(
        body, name=name,
        out_shape=(jax.ShapeDtypeStruct((M, N), F32), jax.ShapeDtypeStruct((M, N), BF16)),
        grid=(M // tm, N // tn, nk),
        in_specs=[pl.BlockSpec((tm, tk), lambda i, j, k: (i, k)),
                  pl.BlockSpec((tk, tn), lambda i, j, k: (k, j)),
                  pl.BlockSpec((tm, tn), lambda i, j, k: (i, j)),
                  pl.BlockSpec((1, tn), lambda i, j, k: (0, j))] + ([ANY] if after is not None else []),
        out_specs=(pl.BlockSpec((tm, tn), lambda i, j, k: (i, j)),
                   pl.BlockSpec((tm, tn), lambda i, j, k: (i, j))),
        scratch_shapes=[pltpu.VMEM((tm, tn) if nk > 1 else (8, 128), F32)],
        compiler_params=_params(),
    )(a, b, resid, gate, *([after] if after is not None else []))


def _norm_mod_fwd(x, g, shift, scale, name):
    S, D = x.shape
    ts = _row_tile(S, ROW_TILE)

    def body(x_ref, g_ref, sh_ref, sc_ref, h_ref):
        xv = x_ref[...]
        rstd = lax.rsqrt(jnp.mean(xv * xv, axis=-1, keepdims=True) + EPS)
        y = xv * rstd * g_ref[...]
        h_ref[...] = (y * (1.0 + sc_ref[...]) + sh_ref[...]).astype(BF16)

    row = pl.BlockSpec((1, D), lambda i: (0, 0))
    return pl.pallas_call(
        body, name=name, out_shape=jax.ShapeDtypeStruct((S, D), BF16), grid=(S // ts,),
        in_specs=[pl.BlockSpec((ts, D), lambda i: (i, 0)), row, row, row],
        out_specs=pl.BlockSpec((ts, D), lambda i: (i, 0)),
        compiler_params=_params(),
    )(x, g, shift, scale)


def _accumulate_rows(i, pairs):
    @pl.when(i == 0)
    def _():
        for ref, value in pairs:
            ref[...] = value

    @pl.when(i > 0)
    def _():
        for ref, value in pairs:
            ref[...] += value


def _gated_residual_bwd(dx, f_ref, gate_ref, df_ref):
    df_ref[...] = (dx * gate_ref[...]).astype(BF16)
    return jnp.sum(dx * f_ref[...].astype(F32), axis=0, keepdims=True)


def _norm_mod_bwd(x, dh, dres, g, scale, name, below=None):
    S, D = x.shape
    ts = _row_tile(S, ROW_TILE)

    def body(x_ref, dh_ref, dr_ref, g_ref, sc_ref, *rest):
        i = pl.program_id(0)
        xv, dhv, gv = x_ref[...], dh_ref[...], g_ref[...]
        rstd = lax.rsqrt(jnp.mean(xv * xv, axis=-1, keepdims=True) + EPS)
        xhat = xv * rstd
        dn = dhv * (1.0 + sc_ref[...])
        dxhat = dn * gv
        proj = jnp.mean(dxhat * xhat, axis=-1, keepdims=True)
        dx = dr_ref[...] + rstd * (dxhat - xhat * proj)
        sums = [jnp.sum(dhv, axis=0, keepdims=True), jnp.sum(dhv * (xhat * gv), axis=0, keepdims=True),
                jnp.sum(dn * xhat, axis=0, keepdims=True)]
        if below is None:
            dx_ref, *sum_refs = rest
        else:
            f_ref, gate_ref, dx_ref, *sum_refs, df_ref = rest
            sums.append(_gated_residual_bwd(dx, f_ref, gate_ref, df_ref))
        dx_ref[...] = dx
        _accumulate_rows(i, list(zip(sum_refs, sums)))

    tile = pl.BlockSpec((ts, D), lambda i: (i, 0))
    row = pl.BlockSpec((1, D), lambda i: (0, 0))
    vec = jax.ShapeDtypeStruct((1, D), F32)
    fused = below is not None
    return pl.pallas_call(
        body, name=name,
        out_shape=(jax.ShapeDtypeStruct((S, D), F32), vec, vec, vec)
        + ((vec, jax.ShapeDtypeStruct((S, D), BF16)) if fused else ()),
        grid=(S // ts,),
        in_specs=[tile, tile, tile, row, row] + ([tile, row] if fused else []),
        out_specs=(tile, row, row, row) + ((row, tile) if fused else ()),
        compiler_params=_params(),
    )(x, dh, dres, g, scale, *(below if fused else ()))


def _ffn_in_fwd(h, w_t, name):
    S, D = h.shape
    F = w_t.shape[0] // 2
    tm, tn = _tile(S, TILES["ffn_fused"][0]), _tile(F, TILES["ffn_fused"][1])
    nj = F // tn

    def body(h_ref, wg_ref, wu_ref, o_ref):
        hv = h_ref[...]
        ug = lax.dot_general(hv, wg_ref[...], NT, preferred_element_type=F32)
        uu = lax.dot_general(hv, wu_ref[...], NT, preferred_element_type=F32)
        o_ref[...] = (ug * jax.nn.sigmoid(ug) * uu).astype(BF16)

    return pl.pallas_call(
        body, name=name, out_shape=jax.ShapeDtypeStruct((S, F), BF16), grid=(nj, S // tm),
        in_specs=[pl.BlockSpec((tm, D), lambda j, i: (i, 0)),
                  pl.BlockSpec((tn, D), lambda j, i: (j, 0)),
                  pl.BlockSpec((tn, D), lambda j, i: (j + nj, 0))],
        out_specs=pl.BlockSpec((tm, tn), lambda j, i: (i, j)),
        compiler_params=_params(),
    )(h, w_t, w_t)


def _ffn_mid_bwd(h, df, w_in_t, w_out, name):
    S, D = h.shape
    F = w_in_t.shape[0] // 2
    tm, tn = _tile(S, TILES["ffn_fused"][0]), _tile(F, TILES["ffn_fused"][1])
    nj = F // tn

    def body(h_ref, df_ref, wg_ref, wu_ref, wo_ref, dg_ref, du_ref):
        hv = h_ref[...]
        ug = lax.dot_general(hv, wg_ref[...], NT, preferred_element_type=F32)
        uu = lax.dot_general(hv, wu_ref[...], NT, preferred_element_type=F32)
        dact = lax.dot_general(df_ref[...], wo_ref[...], NT, preferred_element_type=F32)
        sig = jax.nn.sigmoid(ug)
        dg_ref[...] = (dact * uu * (sig * (1.0 + ug * (1.0 - sig)))).astype(BF16)
        du_ref[...] = (dact * (ug * sig)).astype(BF16)

    out = jax.ShapeDtypeStruct((S, F), BF16)
    return pl.pallas_call(
        body, name=name, out_shape=(out, out), grid=(nj, S // tm),
        in_specs=[pl.BlockSpec((tm, D), lambda j, i: (i, 0)),
                  pl.BlockSpec((tm, D), lambda j, i: (i, 0)),
                  pl.BlockSpec((tn, D), lambda j, i: (j, 0)),
                  pl.BlockSpec((tn, D), lambda j, i: (j + nj, 0)),
                  pl.BlockSpec((tn, D), lambda j, i: (j, 0))],
        out_specs=(pl.BlockSpec((tm, tn), lambda j, i: (i, j)), pl.BlockSpec((tm, tn), lambda j, i: (i, j))),
        compiler_params=_params(),
    )(h, df, w_in_t, w_in_t, w_out)


def _merge_fwd(o_a, o_b, o_c, gates, w_branch, w_out, resid, gate, name, *, tm=512):
    S, W = o_a.shape
    D = w_branch.shape[2]
    tm = _row_tile(S, tm)

    def body(oa_ref, ob_ref, oc_ref, g_ref, w_ref, wo_ref, r_ref, gm_ref, x_ref, m_ref, mix_ref):
        acc = None
        for k, o_ref in enumerate((oa_ref, ob_ref, oc_ref)):
            y = jnp.dot(o_ref[...], w_ref[k], preferred_element_type=F32)
            t = jax.nn.sigmoid(g_ref[:, k * D:(k + 1) * D]) * y
            acc = t if acc is None else acc + t
        merged = acc.astype(BF16)
        m_ref[...] = merged
        mix = jnp.dot(merged, wo_ref[...], preferred_element_type=F32)
        x_ref[...] = r_ref[...] + gm_ref[...] * mix
        mix_ref[...] = mix.astype(BF16)

    o_spec = pl.BlockSpec((tm, W), lambda i: (i, 0))
    tile = pl.BlockSpec((tm, D), lambda i: (i, 0))
    return pl.pallas_call(
        body, name=name,
        out_shape=(jax.ShapeDtypeStruct((S, D), F32), jax.ShapeDtypeStruct((S, D), BF16), jax.ShapeDtypeStruct((S, D), BF16)),
        grid=(S // tm,),
        in_specs=[o_spec, o_spec, o_spec, pl.BlockSpec((tm, 3 * D), lambda i: (i, 0)),
                  pl.BlockSpec((3, W, D), lambda i: (0, 0, 0)), pl.BlockSpec((D, D), lambda i: (0, 0)),
                  tile, pl.BlockSpec((1, D), lambda i: (0, 0))],
        out_specs=(tile, tile, tile),
        compiler_params=_params(),
    )(o_a, o_b, o_c, gates, w_branch, w_out, resid, gate)


def _merge_bwd(dmix, o_a, o_b, o_c, gates, w_branch, w_out, name, *, tm=256):
    S, W = o_a.shape
    D = w_branch.shape[2]
    tm = _row_tile(S, tm)
    n_heads = W // 64

    def body(dm_ref, oa_ref, ob_ref, oc_ref, g_ref, w_ref, wo_ref, dg_ref, dw_ref,
             doa_ref, dob_ref, doc_ref, dla_ref, dlb_ref, dlc_ref):
        first = pl.program_id(0) == 0
        head_of_column = (lax.broadcasted_iota(jnp.int32, (W, BLK), 0) // 64
                          == lax.broadcasted_iota(jnp.int32, (W, BLK), 1)).astype(F32)
        dm = lax.dot_general(dm_ref[...], wo_ref[...], NT, preferred_element_type=F32)
        branches = ((oa_ref, doa_ref, dla_ref), (ob_ref, dob_ref, dlb_ref), (oc_ref, doc_ref, dlc_ref))
        for k, (o_ref, do_ref, dl_ref) in enumerate(branches):
            wk = w_ref[k]
            ov = o_ref[...]
            y = jnp.dot(ov, wk, preferred_element_type=F32)
            g = jax.nn.sigmoid(g_ref[:, k * D:(k + 1) * D])
            dy = (dm * g).astype(BF16)
            dwk = lax.dot_general(ov, dy, TN, preferred_element_type=F32)

            @pl.when(first)
            def _(k=k, dwk=dwk):
                dw_ref[k] = dwk

            @pl.when(jnp.logical_not(first))
            def _(k=k, dwk=dwk):
                dw_ref[k] += dwk
            dg_ref[:, k * D:(k + 1) * D] = (dm * y * (g * (1.0 - g))).astype(BF16)
            do16 = lax.dot_general(dy, wk, NT, preferred_element_type=F32).astype(BF16)
            do_ref[...] = do16
            prod = do16.astype(F32) * ov.astype(F32)
            sums = jnp.dot(prod, head_of_column, preferred_element_type=F32, precision=HIGHEST)
            dl_ref[...] = jnp.transpose(sums)[:n_heads, :]

    o_spec = pl.BlockSpec((tm, W), lambda i: (i, 0))
    wide = pl.BlockSpec((tm, 3 * D), lambda i: (i, 0))
    dl_spec = pl.BlockSpec((n_heads, tm), lambda i: (0, i))
    o_out = jax.ShapeDtypeStruct((S, W), BF16)
    wide_out = jax.ShapeDtypeStruct((S, 3 * D), BF16)
    dl_out = jax.ShapeDtypeStruct((n_heads, S), F32)
    whole = pl.BlockSpec((3, W, D), lambda i: (0, 0, 0))
    return pl.pallas_call(
        body, name=name,
        out_shape=(wide_out, jax.ShapeDtypeStruct((3, W, D), F32), o_out, o_out, o_out, dl_out, dl_out, dl_out),
        grid=(S // tm,),
        in_specs=[pl.BlockSpec((tm, D), lambda i: (i, 0)), o_spec, o_spec, o_spec, wide, whole,
                  pl.BlockSpec((D, D), lambda i: (0, 0))],
        out_specs=(wide, whole, o_spec, o_spec, o_spec, dl_spec, dl_spec, dl_spec),
        compiler_params=_params(),
    )(dmix, o_a, o_b, o_c, gates, w_branch, w_out)


def _band_mask(variant, t_abs, s_abs):
    if variant == "b":
        return s_abs <= t_abs
    qc, kc = t_abs >> 6, s_abs >> 6
    return (kc <= qc) & (kc >= qc - (2 if variant == "a" else 8))


def _attn_fwd(variant, qkv, name, *, sinks=None, slopes=None, cq_col=None, ck_row=None, bias=None, after=None):
    S = qkv.shape[0]
    nb = S // BLK
    qb, kb, vb = ATTN_COLS[variant]
    shared_kv = variant == "a"
    win = ATTN_WINDOW.get(variant)
    per_step = ATTN_BLOCKS_PER_STEP[variant]

    def body(*refs):
        if after is not None:
            refs = refs[:-3] + refs[-2:]
        if variant == "a":
            q_ref, k_ref, v_ref, sink_ref, slope_ref, o_ref, lse_ref = refs
        elif variant == "b":
            q_ref, k_ref, v_ref, cq_ref, ck_ref, o_ref, lse_ref = refs
        else:
            q_ref, k_ref, v_ref, bias_ref, o_ref, lse_ref = refs
        p = pl.program_id(0)
        lane = lax.broadcasted_iota(jnp.int32, (1, BLK), 1)
        diagonal = lax.broadcasted_iota(jnp.int32, (BLK, BLK), 0) == lax.broadcasted_iota(jnp.int32, (BLK, BLK), 1)

        def compute(i, rows, start, n_keys):
            n_rows = rows.stop - rows.start
            t_abs = i * BLK + lax.broadcasted_iota(jnp.int32, (n_rows, 1), 0)
            q2 = q_ref[rows, :].astype(F32) * 0.125
            k_w = k_ref[pl.ds(start, n_keys), :]
            v_w = v_ref[pl.ds(start, n_keys), :]
            s_abs = start + lax.broadcasted_iota(jnp.int32, (1, n_keys), 1)
            valid = _band_mask(variant, t_abs, s_abs)
            outs = []
            for half in (0, 1):
                hmask = (lane >= 64) if half else (lane < 64)
                qh = jnp.where(hmask, q2, 0.0)
                if shared_kv:
                    swap = (p // 2) != half
                    qh = jnp.where(swap, pltpu.roll(qh, 64, 1), qh)
                s = lax.dot_general(qh.astype(BF16), k_w, NT, preferred_element_type=F32)
                if variant == "a":
                    head = 2 * p + half
                    s = s + (-slope_ref[head]) * jnp.abs(t_abs - s_abs).astype(F32)
                elif variant == "b":
                    s = s + cq_ref[rows, half:half + 1] - ck_ref[half:half + 1, pl.ds(start, n_keys)]
                else:
                    j0 = start // BLK
                    s = s + jnp.concatenate([jnp.concatenate(
                        [bias_ref[half, jnp.clip(i + r - j0 - b, 0, 4)] for b in range(n_keys // BLK)], axis=1)
                        for r in range(n_rows // BLK)], axis=0)
                s = jnp.where(valid, s, NEG_INF)
                m = jnp.max(s, axis=1, keepdims=True)
                if variant == "a":
                    m = jnp.maximum(m, sink_ref[head])
                pe = jnp.exp(s - m)
                l = jnp.sum(pe, axis=1, keepdims=True)
                if variant == "a":
                    l = l + jnp.exp(sink_ref[head] - m)
                out = jnp.dot(pe.astype(BF16), v_w, preferred_element_type=F32) / l
                if shared_kv:
                    out = jnp.where(swap, pltpu.roll(out, 64, 1), out)
                outs.append(out)
                lse = m + jnp.log(l)
                for b in range(n_rows // BLK):
                    part = jnp.where(diagonal, lse[b * BLK:(b + 1) * BLK, :], 0.0)
                    lse_ref[half:half + 1, rows.start + b * BLK:rows.start + (b + 1) * BLK] = jnp.sum(
                        part, axis=0, keepdims=True)
            o_ref[rows, :] = jnp.where(lane < 64, outs[0], outs[1]).astype(BF16)

        step = pl.program_id(1)
        if variant == "b":
            for g in range(S // GROUP):
                pl.when(step == g)(functools.partial(compute, step * per_step, slice(0, GROUP), 0, (g + 1) * GROUP))
        elif variant == "c":
            span = win + per_step - 1
            start = jnp.clip(step * per_step - (win - 1), 0, nb - span) * BLK
            compute(step * per_step, slice(0, per_step * BLK), pl.multiple_of(start, BLK), span * BLK)
        else:
            for sub in range(per_step):
                i = step * per_step + sub
                start = jnp.clip(i - (win - 1), 0, nb - win) * BLK
                compute(i, slice(sub * BLK, (sub + 1) * BLK), pl.multiple_of(start, BLK), win * BLK)

    tq = per_step * BLK
    kv_col = (lambda p, i: (0, kb)) if shared_kv else (lambda p, i: (0, kb + p))
    vv_col = (lambda p, i: (0, vb)) if shared_kv else (lambda p, i: (0, vb + p))
    in_specs = [pl.BlockSpec((tq, BLK), lambda p, i: (i, qb + p)),
                pl.BlockSpec((S, BLK), kv_col), pl.BlockSpec((S, BLK), vv_col)]
    args = [qkv, qkv, qkv]
    if variant == "a":
        in_specs += [pl.BlockSpec(memory_space=pltpu.SMEM), pl.BlockSpec(memory_space=pltpu.SMEM)]
        args += [sinks, slopes]
    elif variant == "b":
        in_specs += [pl.BlockSpec((None, tq, 2), lambda p, i: (p, i, 0)),
                     pl.BlockSpec((None, 2, S), lambda p, i: (p, 0, 0))]
        args += [cq_col, ck_row]
    else:
        in_specs += [pl.BlockSpec((2, 5, BLK, BLK), lambda p, i: (p, 0, 0, 0))]
        args += [bias]
    if after is not None:
        in_specs.append(ANY)
        args.append(after)
    return pl.pallas_call(
        body, name=name,
        out_shape=(jax.ShapeDtypeStruct((S, 512), BF16), jax.ShapeDtypeStruct((4, 2, S), F32)),
        grid=(4, nb // per_step), in_specs=in_specs,
        out_specs=(pl.BlockSpec((tq, BLK), lambda p, i: (i, p)),
                   pl.BlockSpec((None, 2, tq), lambda p, i: (p, 0, i))),
        compiler_params=_params(),
    )(*args)


def _attn_bwd(variant, qkv, do, lse_row, delta_row, name, *, sinks=None, slopes=None, cq_row=None,
              ck_col=None, bias_t=None):
    S = qkv.shape[0]
    nb = S // BLK
    qb, kb, vb = ATTN_COLS[variant]
    shared_kv = variant == "a"
    win = ATTN_WINDOW.get(variant)
    per_step = ATTN_BLOCKS_PER_STEP[variant]

    def body(*refs):
        *refs, dqt_ref = refs
        if variant == "a":
            (q_ref, k_ref, v_ref, do_ref, lse_ref, dl_ref, sink_ref, slope_ref,
             dq_ref, dk_ref, dv_ref, ex_ref) = refs
        elif variant == "b":
            (q_ref, k_ref, v_ref, do_ref, lse_ref, dl_ref, cq_ref, ck_ref,
             dq_ref, dk_ref, dv_ref, ex_ref, dcq_ref) = refs
        else:
            (q_ref, k_ref, v_ref, do_ref, lse_ref, dl_ref, bias_ref,
             dq_ref, dk_ref, dv_ref, ex_ref) = refs
        p = pl.program_id(0)
        lane = lax.broadcasted_iota(jnp.int32, (1, BLK), 1)
        hmasks = [(lane < 64), (lane >= 64)]
        swaps = [(p // 2) != half for half in (0, 1)] if shared_kv else None

        @pl.when(pl.program_id(1) == 0)
        def _():
            dqt_ref[...] = jnp.zeros_like(dqt_ref)
            if variant == "b":
                dcq_ref[...] = jnp.zeros_like(dcq_ref)
            else:
                ex_ref[...] = jnp.zeros_like(ex_ref)

        def to_kv_lanes(x, h):
            x = jnp.where(hmasks[h], x, 0.0)
            if shared_kv:
                x = jnp.where(swaps[h], pltpu.roll(x, 64, 1), x)
            return x

        def compute(j, rows, start, n_q):
            n_rows = rows.stop - rows.start
            s_abs = j * BLK + lax.broadcasted_iota(jnp.int32, (n_rows, 1), 0)
            off_k = pl.multiple_of(j * BLK, BLK)
            k2 = k_ref[rows, :].astype(F32)
            v2 = v_ref[rows, :].astype(F32)
            if shared_kv:
                kv_lane = (lane >> 6) == (p // 2)
                k_src, v_src = jnp.where(kv_lane, k2, 0.0), jnp.where(kv_lane, v2, 0.0)
                k_al = [jnp.where(swaps[h], pltpu.roll(k_src, 64, 1), k_src) for h in (0, 1)]
                v_al = [jnp.where(swaps[h], pltpu.roll(v_src, 64, 1), v_src) for h in (0, 1)]
            else:
                k_al = [jnp.where(hmasks[h], k2, 0.0) for h in (0, 1)]
                v_al = [jnp.where(hmasks[h], v2, 0.0) for h in (0, 1)]
            k_al = [(t * 0.125).astype(BF16) for t in k_al]
            v_al = [t.astype(BF16) for t in v_al]
            q_w = q_ref[pl.ds(start, n_q), :]
            do_w = do_ref[pl.ds(start, n_q), :]
            t_abs = start + lax.broadcasted_iota(jnp.int32, (1, n_q), 1)
            valid = _band_mask(variant, t_abs, s_abs)
            dk_acc = dv_acc = None
            ds_both = []
            for half in (0, 1):
                s = lax.dot_general(k_al[half], q_w, NT, preferred_element_type=F32)
                if variant == "a":
                    s = s + (-slope_ref[2 * p + half]) * jnp.abs(t_abs - s_abs).astype(F32)
                elif variant == "b":
                    s = s + cq_ref[half:half + 1, pl.ds(start, n_q)] - ck_ref[rows, half:half + 1]
                else:
                    i0 = start // BLK
                    s = s + jnp.concatenate([jnp.concatenate(
                        [bias_ref[half, jnp.clip(i0 + b - j - r, 0, 4)] for b in range(n_q // BLK)], axis=1)
                        for r in range(n_rows // BLK)], axis=0)
                pr = jnp.where(valid, jnp.exp(s - lse_ref[half:half + 1, pl.ds(start, n_q)]), 0.0)
                dp = lax.dot_general(v_al[half], do_w, NT, preferred_element_type=F32)
                ds = pr * (dp - dl_ref[half:half + 1, pl.ds(start, n_q)])
                ds16 = ds.astype(BF16)
                dv_h = to_kv_lanes(jnp.dot(pr.astype(BF16), do_w, preferred_element_type=F32), half)
                dk_h = to_kv_lanes(jnp.dot(ds16, q_w, preferred_element_type=F32) * 0.125, half)
                dv_acc = dv_h if dv_acc is None else dv_acc + dv_h
                dk_acc = dk_h if dk_acc is None else dk_acc + dk_h
                ds_both.append(ds16)
                if variant == "b":
                    ex_ref[rows, half:half + 1] = -jnp.sum(ds, axis=1, keepdims=True)
                    dcq_ref[half:half + 1, pl.ds(start, n_q)] += jnp.sum(ds, axis=0, keepdims=True)
                elif variant == "c":
                    for r in range(n_rows // BLK):
                        for b in range(n_q // BLK):
                            ex_ref[half, jnp.clip(i0 + b - j - r, 0, 4)] += ds[r * BLK:(r + 1) * BLK, b * BLK:(b + 1) * BLK]
            dq_t = lax.dot_general(jnp.concatenate(k_al, axis=0), jnp.concatenate(ds_both, axis=0), TN,
                                   preferred_element_type=F32)
            dqt_ref[:, pl.ds(start, n_q)] += dq_t
            if shared_kv:
                @pl.when(p == 0)
                def _():
                    dk_ref[pl.ds(off_k, n_rows), :] = dk_acc
                    dv_ref[pl.ds(off_k, n_rows), :] = dv_acc

                @pl.when(p > 0)
                def _():
                    dk_ref[pl.ds(off_k, n_rows), :] += dk_acc
                    dv_ref[pl.ds(off_k, n_rows), :] += dv_acc
            else:
                dk_ref[pl.ds(off_k, n_rows), :] = dk_acc.astype(dk_ref.dtype)
                dv_ref[pl.ds(off_k, n_rows), :] = dv_acc.astype(dv_ref.dtype)
            if variant == "a":
                for half in (0, 1):
                    p_sink = jnp.exp(sink_ref[2 * p + half] - lse_ref[half:half + 1, pl.ds(off_k, n_rows)])
                    term = p_sink * dl_ref[half:half + 1, pl.ds(off_k, n_rows)]
                    ex_ref[half:half + 1, :] += -jnp.sum(term, axis=1, keepdims=True)

        step = pl.program_id(1)
        if variant == "b":
            for g in range(S // GROUP):
                pl.when(step == g)(functools.partial(compute, step * per_step, slice(0, GROUP), g * GROUP, S - g * GROUP))
        elif variant == "c":
            span = win + per_step - 1
            start = jnp.clip(step * per_step, 0, nb - span) * BLK
            compute(step * per_step, slice(0, per_step * BLK), pl.multiple_of(start, BLK), span * BLK)
        else:
            for sub in range(per_step):
                j = step * per_step + sub
                start = jnp.clip(j, 0, nb - win) * BLK
                compute(j, slice(sub * BLK, (sub + 1) * BLK), pl.multiple_of(start, BLK), win * BLK)

        @pl.when(step == nb // per_step - 1)
        def _():
            dq_ref[...] = jnp.transpose(dqt_ref[...]).astype(BF16)

    tk = per_step * BLK
    col = lambda c0: (lambda p, j: (0, c0 + p))
    kv_blk = (lambda c0: (lambda p, j: (j, c0))) if shared_kv else (lambda c0: (lambda p, j: (j, c0 + p)))
    pair = lambda p, j: (0, p)
    row_stat = pl.BlockSpec((None, 2, S), lambda p, j: (p, 0, 0))
    in_specs = [pl.BlockSpec((S, BLK), col(qb)),
                pl.BlockSpec((tk, BLK), kv_blk(kb)), pl.BlockSpec((tk, BLK), kv_blk(vb)),
                pl.BlockSpec((S, BLK), pair), row_stat, row_stat]
    args = [qkv, qkv, qkv, do, lse_row, delta_row]
    kv_width = BLK if shared_kv else 512
    kv_out = pl.BlockSpec((S, BLK), (lambda p, j: (0, 0)) if shared_kv else pair)
    kv_dtype = F32 if shared_kv else BF16
    out_shape = [jax.ShapeDtypeStruct((S, 512), BF16), jax.ShapeDtypeStruct((S, kv_width), kv_dtype),
                 jax.ShapeDtypeStruct((S, kv_width), kv_dtype)]
    out_specs = [pl.BlockSpec((S, BLK), pair), kv_out, kv_out]
    if variant == "a":
        in_specs += [pl.BlockSpec(memory_space=pltpu.SMEM), pl.BlockSpec(memory_space=pltpu.SMEM)]
        args += [sinks, slopes]
        out_shape.append(jax.ShapeDtypeStruct((4, 8, BLK), F32))
        out_specs.append(pl.BlockSpec((None, 8, BLK), lambda p, j: (p, 0, 0)))
    elif variant == "b":
        in_specs += [row_stat, pl.BlockSpec((None, tk, 2), lambda p, j: (p, j, 0))]
        args += [cq_row, ck_col]
        out_shape += [jax.ShapeDtypeStruct((4, S, 2), F32), jax.ShapeDtypeStruct((4, 2, S), F32)]
        out_specs += [pl.BlockSpec((None, tk, 2), lambda p, j: (p, j, 0)), row_stat]
    else:
        in_specs += [pl.BlockSpec((2, 5, BLK, BLK), lambda p, j: (p, 0, 0, 0))]
        args += [bias_t]
        out_shape.append(jax.ShapeDtypeStruct((8, 5, BLK, BLK), F32))
        out_specs.append(pl.BlockSpec((2, 5, BLK, BLK), lambda p, j: (p, 0, 0, 0)))
    return pl.pallas_call(
        body, name=name, out_shape=tuple(out_shape), grid=(4, nb // per_step),
        in_specs=in_specs, out_specs=tuple(out_specs), scratch_shapes=[pltpu.VMEM((BLK, S), F32)],
        compiler_params=_params(),
    )(*args)


def _log_sigmoid(x):
    return jnp.minimum(x, 0.0) - jnp.log(1.0 + jnp.exp(-jnp.abs(x)))


def _forget_fwd(fb, b_forget, name):
    S = fb.shape[0]
    nb = S // GROUP

    def body(fb_ref, b_ref, cum_ref, carry_ref):
        i = pl.program_id(0)
        logf = _log_sigmoid(fb_ref[...] + b_ref[...])
        r = lax.broadcasted_iota(jnp.int32, (GROUP, GROUP), 0)
        c = lax.broadcasted_iota(jnp.int32, (GROUP, GROUP), 1)
        tri = (c <= r).astype(F32)

        @pl.when(i == 0)
        def _():
            carry_ref[...] = jnp.zeros_like(carry_ref)

        cum = jnp.dot(tri, logf, preferred_element_type=F32, precision=HIGHEST) + carry_ref[0:1, :]
        cum_ref[...] = cum
        carry_ref[...] = jnp.broadcast_to(cum[GROUP - 1:GROUP, :], carry_ref.shape)

    return pl.pallas_call(
        body, name=name, out_shape=jax.ShapeDtypeStruct((S, BLK), F32), grid=(nb,),
        in_specs=[pl.BlockSpec((GROUP, BLK), lambda i: (i, 0)), pl.BlockSpec((1, BLK), lambda i: (0, 0))],
        out_specs=pl.BlockSpec((GROUP, BLK), lambda i: (i, 0)),
        scratch_shapes=[pltpu.VMEM((8, BLK), F32)],
        compiler_params=_params(),
    )(fb, b_forget)


def _forget_bwd(dcum_q, dcum_k, fb, b_forget, name):
    S = fb.shape[0]
    nb = S // GROUP

    def body(dq_ref, dk_ref, fb_ref, b_ref, dfb_ref, db_ref, carry_ref):
        g = pl.program_id(0)
        r = lax.broadcasted_iota(jnp.int32, (GROUP, GROUP), 0)
        c = lax.broadcasted_iota(jnp.int32, (GROUP, GROUP), 1)
        tri = (c >= r).astype(F32)

        @pl.when(g == 0)
        def _():
            carry_ref[...] = jnp.zeros_like(carry_ref)

        dcum = dq_ref[...] + dk_ref[...]
        dlogf = jnp.dot(tri, dcum, preferred_element_type=F32, precision=HIGHEST) + carry_ref[0:1, :]
        carry_ref[...] = jnp.broadcast_to(dlogf[0:1, :], carry_ref.shape)
        x = fb_ref[...] + b_ref[...]
        lane = lax.broadcasted_iota(jnp.int32, (1, BLK), 1)
        dfb = jnp.where(lane < N_FORGET, dlogf * jax.nn.sigmoid(-x), 0.0)
        dfb_ref[...] = dfb
        db = jnp.sum(dfb, axis=0, keepdims=True)

        @pl.when(g == 0)
        def _():
            db_ref[...] = db

        @pl.when(g > 0)
        def _():
            db_ref[...] += db

    rev = pl.BlockSpec((GROUP, BLK), lambda g: (nb - 1 - g, 0))
    row = pl.BlockSpec((1, BLK), lambda g: (0, 0))
    return pl.pallas_call(
        body, name=name,
        out_shape=(jax.ShapeDtypeStruct((S, BLK), F32), jax.ShapeDtypeStruct((1, BLK), F32)), grid=(nb,),
        in_specs=[rev, rev, rev, row], out_specs=(rev, row),
        scratch_shapes=[pltpu.VMEM((8, BLK), F32)],
        compiler_params=_params(),
    )(dcum_q, dcum_k, fb, b_forget)


def _skew(x, sign):
    row = lax.broadcasted_iota(jnp.int32, x.shape, 0)
    for b in range(7):
        amount = (1 << b) if sign > 0 else 256 - (1 << b)
        x = jnp.where(((row >> b) & 1) == 1, pltpu.roll(x, amount, 1), x)
    return x


def _rel_bases(rel):
    far = rel[:, 256:257]
    far127 = jnp.broadcast_to(far, (rel.shape[0], 127))
    base0 = jnp.concatenate([rel[:, 128:0:-1], far, rel[:, 255:128:-1]], axis=1)
    base1 = jnp.concatenate([rel[:, 256:128:-1], far, far127], axis=1)
    base0_t = jnp.concatenate([rel[:, 128:256], far, rel[:, 1:128]], axis=1)
    base1_t = jnp.concatenate([jnp.broadcast_to(far, (rel.shape[0], 128)), far, rel[:, 129:256]], axis=1)
    return jnp.stack([base0, base1, base0_t, base1_t], axis=1)


def _rel_expand(bases, name):
    def body(b_ref, t_ref, tt_ref):
        far = jnp.broadcast_to(b_ref[1:2, 0:1], (BLK, BLK))
        for k, out_ref in ((0, t_ref), (2, tt_ref)):
            for d in (0, 1):
                x = jnp.broadcast_to(b_ref[k + d:k + d + 1, :], (BLK, 2 * BLK))
                out_ref[d] = _skew(x, 1)[:, :BLK]
            for d in (2, 3, 4):
                out_ref[d] = far

    out = jax.ShapeDtypeStruct((8, 5, BLK, BLK), F32)
    spec = pl.BlockSpec((None, 5, BLK, BLK), lambda h: (h, 0, 0, 0))
    return pl.pallas_call(
        body, name=name, out_shape=(out, out), grid=(8,),
        in_specs=[pl.BlockSpec((None, 4, 2 * BLK), lambda h: (h, 0, 0))], out_specs=(spec, spec),
        compiler_params=_params(),
    )(bases)


def _rel_reduce(dtiles_t, name):
    def body(dt_ref, o_ref):
        zeros = jnp.zeros((BLK, BLK), F32)
        sums = []
        for d in (0, 1):
            x = _skew(jnp.concatenate([dt_ref[d], zeros], axis=1), -1)
            sums.append(jnp.broadcast_to(jnp.sum(x, axis=0, keepdims=True), (8, 2 * BLK)))
        lane = lax.broadcasted_iota(jnp.int32, (8, 2 * BLK), 1)
        main = pltpu.roll(sums[0], BLK, 1) + jnp.where(lane > BLK, sums[1], 0.0)
        far = jnp.sum(jnp.where(lane < BLK, sums[1], 0.0)[0:1], axis=1, keepdims=True)
        far = far + jnp.sum(jnp.sum(dt_ref[2] + dt_ref[3] + dt_ref[4], axis=0, keepdims=True), axis=1, keepdims=True)
        o_ref[...] = jnp.concatenate([main[0:1], jnp.broadcast_to(far, (1, BLK))], axis=1)

    return pl.pallas_call(
        body, name=name, out_shape=jax.ShapeDtypeStruct((8, 1, 3 * BLK), F32), grid=(8,),
        in_specs=[pl.BlockSpec((None, 5, BLK, BLK), lambda h: (h, 0, 0, 0))],
        out_specs=pl.BlockSpec((None, 1, 3 * BLK), lambda h: (h, 0, 0)),
        compiler_params=_params(),
    )(dtiles_t)


def _final_loss(x, target, g, below, name):
    S, D = x.shape
    ts = _row_tile(S, ROW_TILE)

    def body(x_ref, t_ref, g_ref, f_ref, gate_ref, dx_ref, loss_ref, dg_ref, dgate_ref, df_ref):
        i = pl.program_id(0)
        xv, gv = x_ref[...], g_ref[...]
        rstd = lax.rsqrt(jnp.mean(xv * xv, axis=-1, keepdims=True) + EPS)
        xhat = xv * rstd
        err = xhat * gv - t_ref[...]
        part = 0.5 * jnp.sum(jnp.mean(err * err, axis=-1, keepdims=True), axis=0, keepdims=True)
        dy = err / D
        dg = jnp.sum(dy * xhat, axis=0, keepdims=True)
        dxhat = dy * gv
        proj = jnp.mean(dxhat * xhat, axis=-1, keepdims=True)
        dx = rstd * (dxhat - xhat * proj)
        dx_ref[...] = dx
        dgate = _gated_residual_bwd(dx, f_ref, gate_ref, df_ref)
        _accumulate_rows(i, [(loss_ref, jnp.broadcast_to(part, loss_ref.shape)), (dg_ref, dg), (dgate_ref, dgate)])

    tile = pl.BlockSpec((ts, D), lambda i: (i, 0))
    row = pl.BlockSpec((1, D), lambda i: (0, 0))
    vec = jax.ShapeDtypeStruct((1, D), F32)
    return pl.pallas_call(
        body, name=name,
        out_shape=(jax.ShapeDtypeStruct((S, D), F32), jax.ShapeDtypeStruct((8, 128), F32), vec, vec,
                   jax.ShapeDtypeStruct((S, D), BF16)),
        grid=(S // ts,), in_specs=[tile, tile, row, tile, row],
        out_specs=(tile, pl.BlockSpec((8, 128), lambda i: (0, 0)), row, row, tile),
        compiler_params=_params(),
    )(x, target, g, *below)


def _ada_fwd(c_all, w_ada, name):
    L, D, E = w_ada.shape

    def body(c_ref, w_ref, o_ref):
        cv = c_ref[...]
        cond = cv * jax.nn.sigmoid(cv)
        o_ref[...] = jnp.dot(cond, w_ref[...], preferred_element_type=F32, precision=HIGHEST)

    return pl.pallas_call(
        body, name=name, out_shape=jax.ShapeDtypeStruct((L, N_DEV, E), F32), grid=(L,),
        in_specs=[pl.BlockSpec((N_DEV, D), lambda l: (0, 0)), pl.BlockSpec((None, D, E), lambda l: (l, 0, 0))],
        out_specs=pl.BlockSpec((None, N_DEV, E), lambda l: (l, 0, 0)),
        compiler_params=_params(),
    )(c_all, w_ada)


def _ada_bwd(c_all_t, dmod, name):
    D = c_all_t.shape[0]
    L, _, E = dmod.shape

    def body(c_ref, d_ref, o_ref):
        cv = c_ref[...]
        cond = cv * jax.nn.sigmoid(cv)
        acc = None
        for b in range(N_DEV):
            t = cond[:, b:b + 1] * d_ref[b:b + 1, :]
            acc = t if acc is None else acc + t
        o_ref[...] = acc

    return pl.pallas_call(
        body, name=name, out_shape=jax.ShapeDtypeStruct((L, D, E), F32), grid=(L,),
        in_specs=[pl.BlockSpec((D, N_DEV), lambda l: (0, 0)), pl.BlockSpec((None, N_DEV, E), lambda l: (l, 0, 0))],
        out_specs=pl.BlockSpec((None, D, E), lambda l: (l, 0, 0)),
        compiler_params=_params(),
    )(c_all_t, dmod)


def _adamw(w, m, v, g_parts, name, me, after=None):
    L, R, C = w.shape
    tr = _row_tile(R, max(8, (256 * 1024 // max(C, 128)) // 8 * 8))
    nr = R // tr
    c1 = 1.0 - ADAM_B1 ** ADAM_STEP
    c2 = 1.0 - ADAM_B2 ** ADAM_STEP
    direct = [isinstance(p, tuple) for p in g_parts]
    n_in = sum(2 if d else 1 for d in direct)

    def body(me_ref, w_ref, m_ref, v_ref, *rest):
        g_refs, (go_ref, d_ref, mo_ref, vo_ref) = list(rest[:n_in]), rest[-4:]
        layer = pl.program_id(0)
        g = None
        for l in range(L):
            land_ref = g_refs.pop(0)
            own = g_refs.pop(0)[...].astype(F32) if direct[l] else None
            gl = None
            for k in range(land_ref.shape[0]):
                part = land_ref[k].astype(F32)
                if direct[l]:
                    part = jnp.where(me_ref[l] == k, own, part)
                gl = part if gl is None else gl + part
            g = gl if g is None else jnp.where(layer == l, gl, g)
        mn = ADAM_B1 * m_ref[...] + (1.0 - ADAM_B1) * g
        vn = ADAM_B2 * v_ref[...] + (1.0 - ADAM_B2) * (g * g)
        m_hat = mn / c1
        v_hat = vn / c2
        go_ref[...] = g
        d_ref[...] = -ADAM_LR * (m_hat / (jnp.sqrt(v_hat) + ADAM_EPS) + ADAM_WD * w_ref[...])
        mo_ref[...] = mn
        vo_ref[...] = vn

    def rows(l, layer, i):
        return jnp.where(layer == l, i, 0 if l > 0 else nr - 1)

    in_specs, operands = [], []
    for l, p in enumerate(g_parts):
        land, sent = p if direct[l] else (p, None)
        in_specs.append(pl.BlockSpec((land.shape[0], tr, C), lambda layer, i, me_ref, l=l: (0, rows(l, layer, i), 0)))
        operands.append(land)
        if direct[l]:
            in_specs.append(pl.BlockSpec((None, tr, C), lambda layer, i, me_ref, l=l: (me_ref[l], rows(l, layer, i), 0)))
            operands.append(sent)
    if after is not None:
        in_specs.append(ANY)
        operands.append(after)
    tile = pl.BlockSpec((None, tr, C), lambda layer, i, me_ref: (layer, i, 0))
    out = jax.ShapeDtypeStruct((L, R, C), F32)
    return pl.pallas_call(
        body, name=name, out_shape=(out, out, out, out),
        grid_spec=pltpu.PrefetchScalarGridSpec(
            num_scalar_prefetch=1, grid=(L, nr), in_specs=[tile, tile, tile] + in_specs,
            out_specs=(tile, tile, tile, tile)),
        compiler_params=_params(),
    )(me, w, m, v, *operands)


def _pair_add(pieces, recv, core, name):
    _, _, R, C = pieces.shape
    tr = _row_tile(R, max(8, (512 * 1024 // max(C, 128)) // 8 * 8))

    def body(core_ref, a_ref, b_ref, o_ref):
        o_ref[...] = (a_ref[...].astype(F32) + b_ref[...].astype(F32)).astype(BF16)

    return pl.pallas_call(
        body, name=name, out_shape=jax.ShapeDtypeStruct((4, R, C), BF16),
        grid_spec=pltpu.PrefetchScalarGridSpec(
            num_scalar_prefetch=1, grid=(4, R // tr),
            in_specs=[pl.BlockSpec((None, None, tr, C), lambda k, i, core_ref: (core_ref[0], k, i, 0)),
                      pl.BlockSpec((None, tr, C), lambda k, i, core_ref: (k, i, 0))],
            out_specs=pl.BlockSpec((None, tr, C), lambda k, i, core_ref: (k, i, 0))),
        compiler_params=_params(),
    )(core, pieces, recv)


MESH = pl.DeviceIdType.MESH
ANY = pl.BlockSpec(memory_space=pl.ANY)


def _position():
    return lax.axis_index("x"), lax.axis_index("y"), lax.axis_index("c")


def _small_all_gather(v, name):
    m_per, n = v.shape

    def body(x_ref, out_ref, send_sems, recv_sems, local_sem):
        x, y, c = _position()
        me, sibling = (x, y, c), (x, y, 1 - c)
        chips = [(1 - x, y), (x, 1 - y), (1 - x, 1 - y)]

        def rows(px, py, pc):
            return out_ref.at[pl.ds((4 * px + 2 * py + pc) * m_per, m_per), :]

        def copy(k, block, to, src=None):
            return pltpu.make_async_remote_copy(
                src_ref=rows(*block) if src is None else src, dst_ref=rows(*block),
                send_sem=send_sems.at[k], recv_sem=recv_sems.at[k], device_id=to, device_id_type=MESH)

        mine = pltpu.make_async_copy(x_ref, rows(*me), local_sem)
        mine.start()
        first = [copy(0, me, sibling, src=x_ref)]
        first += [copy(1 + j, me, (*chip, c), src=x_ref) for j, chip in enumerate(chips)]
        for cp in first:
            cp.start()
        passed = [copy(4 + j, (*chip, c), sibling) for j, chip in enumerate(chips)]
        for j, chip in enumerate(chips):
            copy(1 + j, (*chip, c), me).wait_recv()
            passed[j].start()
        copy(0, sibling, me).wait_recv()
        for j, chip in enumerate(chips):
            copy(4 + j, (*chip, 1 - c), me).wait_recv()
        for cp in first + passed:
            cp.wait_send()
        mine.wait()

    return pl.pallas_call(
        body, name=name, out_shape=jax.ShapeDtypeStruct((N_DEV * m_per, n), v.dtype),
        in_specs=[pl.BlockSpec(memory_space=pltpu.VMEM)], out_specs=pl.BlockSpec(memory_space=pltpu.VMEM),
        scratch_shapes=[pltpu.SemaphoreType.DMA((7,)), pltpu.SemaphoreType.DMA((7,)), pltpu.SemaphoreType.DMA],
    )(v)


def _sibling_exchange(pieces, name):
    n_arr = len(pieces)

    def body(*refs):
        p_refs, out_refs = refs[:n_arr], refs[n_arr:2 * n_arr]
        send_sems, recv_sems = refs[2 * n_arr:]
        x, y, c = _position()
        copies = [pltpu.make_async_remote_copy(
            src_ref=p_refs[a].at[1 - c], dst_ref=out_refs[a], send_sem=send_sems.at[a], recv_sem=recv_sems.at[a],
            device_id=(x, y, 1 - c), device_id_type=MESH) for a in range(n_arr)]
        for cp in copies:
            cp.start()
        for cp in copies:
            cp.wait()

    return pl.pallas_call(
        body, name=name,
        out_shape=tuple(jax.ShapeDtypeStruct(p.shape[1:], p.dtype) for p in pieces),
        in_specs=[ANY] * n_arr, out_specs=tuple([ANY] * n_arr),
        scratch_shapes=[pltpu.SemaphoreType.DMA((n_arr,)), pltpu.SemaphoreType.DMA((n_arr,))],
    )(*pieces)


HBM = pl.BlockSpec(memory_space=pltpu.HBM)
SEM = pl.BlockSpec(memory_space=pltpu.SEMAPHORE)
EFFECT = pltpu.SideEffectType.DATAFLOW_SIDE_EFFECTING
RELATIONS = [(rx, ry, rc) for rx in (0, 1) for ry in (0, 1) for rc in (0, 1)][1:]


SAME_CORE = [r for r in RELATIONS if r == (0, 0, 1) or r[2] == 0]


CHIPS = [r for r in RELATIONS if r[2] == 0]


def _exchange_copies(src_refs, land_refs, send_sems, recv_sems, scatter, receive_side, relations):
    x, y, c = _position()
    index = (lambda px, py, pc: 2 * px + py) if relations == CHIPS else (lambda px, py, pc: 4 * px + 2 * py + pc)
    me = index(x, y, c)
    copies = []
    for k, (rx, ry, rc) in enumerate(relations):
        peer = ((1 - x) if rx else x, (1 - y) if ry else y, (1 - c) if rc else c)
        peer_index = index(*peer)
        for a, (src, land) in enumerate(zip(src_refs, land_refs)):
            copies.append(pltpu.make_async_remote_copy(
                src_ref=src.at[peer_index] if scatter else src,
                dst_ref=land.at[peer_index if receive_side else me],
                send_sem=send_sems.at[a * len(relations) + k], recv_sem=recv_sems.at[a * len(relations) + k],
                device_id=peer, device_id_type=MESH))
    return copies


def _exchange_start(srcs, scatter, after, name, relations=RELATIONS):
    n = len(srcs)
    land_shapes = [(s.shape if scatter else (N_DEV,) + s.shape) for s in srcs]

    def body(*refs):
        src_refs, land_refs = refs[:n], refs[n:2 * n]
        send_sems, recv_sems = refs[2 * n + 1], refs[2 * n + 2]
        token = refs[-1]
        for cp in _exchange_copies(src_refs, land_refs, send_sems, recv_sems, scatter, False, relations):
            cp.start()
        token[...] = jnp.zeros_like(token)

    sems = pltpu.SemaphoreType.DMA((n * len(relations),))
    outs = pl.pallas_call(
        body, name=name,
        out_shape=(sems, sems, *[pltpu.HBM(s.shape, s.dtype) for s in srcs],
                   *[pltpu.HBM(shape, s.dtype) for shape, s in zip(land_shapes, srcs)],
                   jax.ShapeDtypeStruct((8, 128), F32)),
        in_specs=[HBM] * (2 * n) + [ANY],
        out_specs=(SEM, SEM, *[HBM] * (2 * n), pl.BlockSpec(memory_space=pltpu.VMEM)),
        input_output_aliases={a: 2 + a for a in range(2 * n)},
        compiler_params=pltpu.CompilerParams(has_side_effects=EFFECT),
    )(*[pltpu.with_memory_space_constraint(s, pltpu.HBM) for s in srcs],
      *[pltpu.with_memory_space_constraint(lax.empty(shape, s.dtype), pltpu.HBM)
        for shape, s in zip(land_shapes, srcs)], after)
    return outs[0], outs[1], outs[2:2 + n], outs[2 + n:2 + 2 * n], outs[-1]


def _exchange_wait(started, scatter, after, name, relations=RELATIONS):
    send_sems, recv_sems, srcs, lands, _ = started
    n = len(srcs)

    def body(*refs):
        src_refs, land_refs = refs[:n], refs[n:2 * n]
        send_sems, recv_sems = refs[2 * n], refs[2 * n + 1]
        copies = _exchange_copies(src_refs, land_refs, send_sems, recv_sems, scatter, True, relations)
        for cp in copies:
            cp.wait_send()
        for cp in copies:
            cp.wait_recv()

    outs = pl.pallas_call(
        body, name=name,
        out_shape=(*[pltpu.HBM(s.shape, s.dtype) for s in srcs], *[pltpu.HBM(t.shape, t.dtype) for t in lands]),
        in_specs=[HBM] * (2 * n) + [SEM, SEM, ANY], out_specs=tuple([HBM] * (2 * n)),
        input_output_aliases={a: a for a in range(2 * n)},
        compiler_params=pltpu.CompilerParams(has_side_effects=EFFECT),
    )(*srcs, *lands, send_sems, recv_sems, after)
    return outs[:n], outs[n:]


def _forward_copies(land_refs, send_sems, recv_sems, receive_side):
    x, y, c = _position()
    copies = []
    for j, (px, py) in enumerate([(1 - x, y), (x, 1 - y), (1 - x, 1 - y)]):
        held, coming = 4 * px + 2 * py + c, 4 * px + 2 * py + (1 - c)
        for a, land in enumerate(land_refs):
            copies.append(pltpu.make_async_remote_copy(
                src_ref=land.at[held], dst_ref=land.at[coming if receive_side else held],
                send_sem=send_sems.at[3 * a + j], recv_sem=recv_sems.at[3 * a + j],
                device_id=(x, y, 1 - c), device_id_type=MESH))
    return copies


def _forward_start(lands, after, name):
    n = len(lands)

    def body(*refs):
        send_sems, recv_sems, token = refs[n + 1], refs[n + 2], refs[-1]
        for cp in _forward_copies(refs[:n], send_sems, recv_sems, False):
            cp.start()
        token[...] = jnp.zeros_like(token)

    sems = pltpu.SemaphoreType.DMA((3 * n,))
    outs = pl.pallas_call(
        body, name=name,
        out_shape=(sems, sems, *[pltpu.HBM(t.shape, t.dtype) for t in lands], jax.ShapeDtypeStruct((8, 128), F32)),
        in_specs=[HBM] * n + [ANY], out_specs=(SEM, SEM, *[HBM] * n, pl.BlockSpec(memory_space=pltpu.VMEM)),
        input_output_aliases={a: 2 + a for a in range(n)},
        compiler_params=pltpu.CompilerParams(has_side_effects=EFFECT),
    )(*lands, after)
    return outs[0], outs[1], outs[2:2 + n], outs[-1]


def _forward_wait(started, after, name):
    send_sems, recv_sems, lands, _ = started
    n = len(lands)

    def body(*refs):
        copies = _forward_copies(refs[:n], refs[n], refs[n + 1], True)
        for cp in copies:
            cp.wait_send()
        for cp in copies:
            cp.wait_recv()

    return pl.pallas_call(
        body, name=name, out_shape=tuple(pltpu.HBM(t.shape, t.dtype) for t in lands),
        in_specs=[HBM] * n + [SEM, SEM, ANY], out_specs=tuple([HBM] * n),
        input_output_aliases={a: a for a in range(n)},
        compiler_params=pltpu.CompilerParams(has_side_effects=EFFECT),
    )(*lands, send_sems, recv_sems, after)


def _place_own(lands, mine, me, name):
    n = len(lands)
    flat = [m.reshape(-1, m.shape[-1]) for m in mine]
    flat_lands = [t.reshape(N_DEV, -1, t.shape[-1]) for t in lands]

    def body(me_ref, *refs):
        for src, dst in zip(refs[:n], refs[2 * n:]):
            dst[...] = src[...]

    in_specs = [pl.BlockSpec((m.shape[0] // 2, m.shape[1]), lambda i, me_ref: (i, 0)) for m in flat]
    out_specs = [pl.BlockSpec((None, m.shape[0] // 2, m.shape[1]), lambda i, me_ref: (me_ref[0], i, 0)) for m in flat]
    outs = pl.pallas_call(
        body, name=name, out_shape=tuple(jax.ShapeDtypeStruct(t.shape, t.dtype) for t in flat_lands),
        grid_spec=pltpu.PrefetchScalarGridSpec(
            num_scalar_prefetch=1, grid=(2,), in_specs=in_specs + [ANY] * n, out_specs=tuple(out_specs)),
        input_output_aliases={1 + n + a: a for a in range(n)},
        compiler_params=_params(),
    )(me, *flat, *flat_lands)
    return [o.reshape(t.shape) for o, t in zip(outs, lands)]


W_IN_SHARD = N_IN // N_DEV
F_SHARD = F_COL // W_IN_SHARD
F_LO = F_COL - F_SHARD * W_IN_SHARD


def _w_ffn_in_view(w):
    return jnp.transpose(w, (0, 2, 1))


def _w_in_segments():
    segments = []
    for d in range(N_DEV):
        if d == F_SHARD:
            segments += [(d, 0, d * W_IN_SHARD, F_LO), (d, F_LO, N_MAIN, N_FORGET),
                         (d, F_LO + N_FORGET, F_COL, W_IN_SHARD - F_LO - N_FORGET)]
        else:
            segments.append((d, 0, d * W_IN_SHARD - (N_FORGET if d > F_SHARD else 0), W_IN_SHARD))
    return segments


def _w_in_rearranged(g, name):
    D = g.shape[1]
    tr = _row_tile(D, 256)

    def body(g_ref, o_ref):
        for d, lo, at, width in _w_in_segments():
            o_ref[:, at:at + width] = g_ref[d, :, lo:lo + width]
        o_ref[:, N_IN:] = jnp.zeros((tr, BLK - N_FORGET), o_ref.dtype)

    return pl.pallas_call(
        body, name=name, out_shape=jax.ShapeDtypeStruct((D, N_MAIN + BLK), g.dtype), grid=(D // tr,),
        in_specs=[pl.BlockSpec((N_DEV, tr, W_IN_SHARD), lambda i: (0, i, 0))],
        out_specs=pl.BlockSpec((tr, N_MAIN + BLK), lambda i: (i, 0)),
        compiler_params=_params(),
    )(g)


def _w_in_pieces(dw_r, name, pair_major=False):
    D = dw_r.shape[0]
    tr = _row_tile(D, 256)
    lead = (2, 4) if pair_major else (N_DEV,)

    def body(x_ref, o_ref):
        for d, lo, at, width in _w_in_segments():
            slot = (d % 2, d // 2) if pair_major else (d,)
            o_ref[(*slot, slice(None), slice(lo, lo + width))] = x_ref[:, at:at + width]

    return pl.pallas_call(
        body, name=name, out_shape=jax.ShapeDtypeStruct((*lead, D, W_IN_SHARD), dw_r.dtype), grid=(D // tr,),
        in_specs=[pl.BlockSpec((tr, N_MAIN + BLK), lambda i: (i, 0))],
        out_specs=pl.BlockSpec((*lead, tr, W_IN_SHARD), lambda i: (*[0] * len(lead), i, 0)),
        compiler_params=_params(),
    )(dw_r)


def _row_pieces(dw):
    return dw.reshape(N_DEV, dw.shape[0] // N_DEV, dw.shape[1])


def _branch_pieces(dw):
    k, w, d = dw.shape
    return jnp.transpose(dw.reshape(k, w, N_DEV, d // N_DEV), (2, 0, 1, 3)).reshape(N_DEV, k * w, d // N_DEV)


def _pairs_col(a):
    return jnp.transpose(a.reshape(a.shape[0], 4, 2), (1, 0, 2))


def _pairs_row(a):
    return jnp.transpose(a.reshape(a.shape[0], 4, 2), (1, 2, 0))


def _heads_from_col(a):
    return jnp.transpose(a, (1, 0, 2)).reshape(a.shape[1], 8)


def _heads_from_row(a):
    return jnp.transpose(a, (2, 0, 1)).reshape(a.shape[2], 8)


def _pad_lanes(a, n):
    return jnp.pad(a, [(0, 0)] * (a.ndim - 1) + [(0, n - a.shape[-1])])


SMALL_SEGMENTS = (("dmod", 2 * 6 * D_MODEL), ("norm_mix_g", 2 * D_MODEL), ("norm_ffn_g", 2 * D_MODEL),
                  ("final_norm_g", D_MODEL), ("b_forget", 128), ("sinks", 128), ("rel_bias", 4224), ("loss", 128))
SMALL_ROWS = 176


def _pack_small(parts):
    flat = [_pad_lanes(parts[name].reshape(1, -1), size) for name, size in SMALL_SEGMENTS]
    total = sum(size for _, size in SMALL_SEGMENTS)
    flat.append(jnp.zeros((1, SMALL_ROWS * 128 - total), F32))
    return jnp.concatenate(flat, axis=1).reshape(SMALL_ROWS, 128)


def _unpack_small(packed, shapes):
    flat = packed.reshape(-1)
    out, pos = {}, 0
    for name, size in SMALL_SEGMENTS:
        shape = shapes[name]
        count = 1
        for d in shape:
            count *= d
        out[name] = flat[pos:pos + count].reshape(shape)
        pos += size
    return out


def kernel(x, c, norm_mix_g, norm_ffn_g, w_ada, b_ada, w_in, b_forget, sinks, rel_bias, w_branch, w_out, w_ffn_in, w_ffn_out, final_norm_g, loss_target, m_norm_mix_g, m_norm_ffn_g, m_w_ada, m_b_ada, m_w_in, m_b_forget, m_sinks, m_rel_bias, m_w_branch, m_w_out, m_w_ffn_in, m_w_ffn_out, m_final_norm_g, v_norm_mix_g, v_norm_ffn_g, v_w_ada, v_b_ada, v_w_in, v_b_forget, v_sinks, v_rel_bias, v_w_branch, v_w_out, v_w_ffn_in, v_w_ffn_out, v_final_norm_g):
    depth = w_in.shape[0]
    S, D = x.shape[1], x.shape[2]
    assert S % GROUP == 0 and S >= ATTN_WINDOW["c"] * BLK
    px, py, pc = _position()
    me = 4 * px + 2 * py + pc
    x0 = x[0]

    assert depth == 2
    big_weights = (w_in, w_branch, w_out, w_ffn_in, w_ffn_out)
    me_arr = jnp.stack([me, me]).astype(jnp.int32)
    me_in_arr = jnp.stack([2 * px + py, me]).astype(jnp.int32)

    def rest_matrices(g_branch, g_out, g_fin, g_fout):
        return (jnp.transpose(g_branch, (1, 2, 0, 3)).reshape(3, 512, D), g_out.reshape(D, D),
                g_fin.reshape(2 * FFN_HIDDEN, D), g_fout.reshape(FFN_HIDDEN, D))

    def arrive(started, after, name):
        mine, landed = _exchange_wait(started, False, after, f"{name}_wait", SAME_CORE)
        return mine, _forward_start(landed, mine[0], f"{name}_forward_start")

    def finish_gather(arrived, after, name):
        mine, forward = arrived
        landed = _forward_wait(forward, after, f"{name}_forward_wait")
        return _place_own(landed, mine, me.astype(jnp.int32).reshape(1), f"{name}_own")

    w_fin_t = _w_ffn_in_view(w_ffn_in)
    shards = [[t.astype(BF16) for t in (w_in[l], w_branch[l], w_out[l], w_fin_t[l], w_ffn_out[l])]
              for l in range(depth)]
    c_all = _small_all_gather(c.reshape(8, 128), "comm_gather_c").reshape(N_DEV, D)
    mod_cols = _ada_fwd(c_all, w_ada, "ada_fwd")
    mod_all = _small_all_gather(mod_cols.reshape(-1, 128), "comm_gather_mod")
    gather_in0 = _exchange_start(shards[0][:1], False, mod_all, "comm_gather_w_in0_start", SAME_CORE)
    gather_rest0 = _exchange_start(shards[0][1:], False, gather_in0[4], "comm_gather_rest0_start", SAME_CORE)
    gather1 = _exchange_start(shards[1], False, gather_rest0[4], "comm_gather_weights1_start", SAME_CORE)
    started = gather1[4][0:1, 0:1]
    W_in, W_branch, W_out, W_fin, W_fout = ([None, None] for _ in range(5))
    mod_all = mod_all.reshape(N_DEV, depth, N_DEV, w_ada.shape[2])
    mod_mine = lax.dynamic_index_in_dim(mod_all, me, axis=2, keepdims=False)
    mod = jnp.transpose(mod_mine, (1, 0, 2)).reshape(depth, 6 * D) + b_ada + started
    mods = [[mod[l:l + 1, k * D:(k + 1) * D] for k in range(6)] for l in range(depth)]
    rel_tiles = [_rel_expand(_rel_bases(rel_bias[l]) + started, f"rel_expand{l}") for l in range(depth)]

    slopes = jnp.exp2(-jnp.arange(1, 9, dtype=F32))
    saved = []
    xs = x0
    for l in range(depth):
        if l == 1:
            g_in1, *g_rest1 = finish_gather(arrived1, xs, "comm_gather_weights1")
            W_in[1] = _w_in_rearranged(g_in1, "w_in_rearrange1")
            W_branch[1], W_out[1], W_fin[1], W_fout[1] = rest_matrices(*g_rest1)
        sh_m, sc_m, g_m, sh_f, sc_f, g_f = mods[l]
        gm, gf = norm_mix_g[l:l + 1], norm_ffn_g[l:l + 1]
        bfor = _pad_lanes(b_forget[l:l + 1], BLK)
        h = _norm_mod_fwd(xs, gm, sh_m, sc_m, f"norm_mix_fwd{l}")
        tiles, tiles_t = rel_tiles[l]
        if l == 0:
            arrived_in0 = arrive(gather_in0, rel_tiles[-1][1], "comm_gather_w_in0")
            W_in[0] = _w_in_rearranged(finish_gather(arrived_in0, h, "comm_gather_w_in0")[0], "w_in_rearrange0")
        qkv, gates = _project(h, W_in[l], f"proj{l}")
        fb = _matmul(h, W_in[l], "nn", F32, f"proj_forget{l}", TILES["proj_forget"], n=BLK, b_off=N_MAIN // BLK)
        cum = _forget_fwd(fb, bfor, f"forget_fwd{l}")[:, :N_FORGET]
        cum_col, cum_row = _pairs_col(cum), _pairs_row(cum)
        o_a, lse_a = _attn_fwd("a", qkv, f"attn_a_fwd{l}", sinks=sinks[l], slopes=slopes)
        o_b, lse_b = _attn_fwd("b", qkv, f"attn_b_fwd{l}", cq_col=cum_col, ck_row=cum_row)
        arrived_rest0 = arrive(gather_rest0, o_b, "comm_gather_rest0") if l == 0 else None
        o_c, lse_c = _attn_fwd("c", qkv, f"attn_c_fwd{l}", bias=tiles, after=arrived_rest0[1][3] if l == 0 else None)
        if l == 0:
            W_branch[0], W_out[0], W_fin[0], W_fout[0] = rest_matrices(
                *finish_gather(arrived_rest0, o_c, "comm_gather_rest0"))
        x1, merged, mix = _merge_fwd(o_a, o_b, o_c, gates, W_branch[l], W_out[l], xs, g_m, f"merge_fwd{l}")
        h2 = _norm_mod_fwd(x1, gf, sh_f, sc_f, f"norm_ffn_fwd{l}")
        act = _ffn_in_fwd(h2, W_fin[l], f"ffn_in_fwd{l}")
        if l == 0:
            arrived1 = arrive(gather1, act, "comm_gather_weights1")
        x2, ffn = _matmul_resid(act, W_fout[l], x1, g_f, f"ffn_out{l}", TILES["ffn_out"],
                                after=arrived1[1][3] if l == 0 else None)
        saved.append(dict(x=xs, h=h, qkv=qkv, gates=gates, fb=fb, bfor=bfor, cum_col=cum_col, cum_row=cum_row,
                          tiles_t=tiles_t, o=(o_a, o_b, o_c), lse=(lse_a, lse_b, lse_c), merged=merged, mix=mix,
                          x1=x1, h2=h2, act=act, ffn=ffn))
        xs = x2

    dx, loss_tile, d_final_g, d_g_f, df = _final_loss(
        xs, loss_target[0], final_norm_g.reshape(1, D), (saved[-1]["ffn"], mods[-1][5]), "final_loss")

    grads = {k: [None] * depth for k in ("w_in", "w_branch", "w_out", "w_ffn_in", "w_ffn_out", "norm_mix_g",
                                          "norm_ffn_g", "b_forget", "sinks", "rel_bias", "dmod")}
    def rest_pieces(l):
        return [_branch_pieces(grads["w_branch"][l]), _row_pieces(grads["w_out"][l]),
                _row_pieces(grads["w_ffn_in"][l]), _row_pieces(grads["w_ffn_out"][l])]

    reduce1 = reduce_rest0 = reduce_in0 = None
    for l in reversed(range(depth)):
        sv = saved[l]
        sh_m, sc_m, g_m, sh_f, sc_f, g_f = mods[l]
        gm, gf = norm_mix_g[l:l + 1], norm_ffn_g[l:l + 1]
        du_g, du_u = _ffn_mid_bwd(sv["h2"], df, W_fin[l], W_fout[l], f"ffn_mid_bwd{l}")
        du = jnp.concatenate([du_g, du_u], axis=1)
        grads["w_ffn_out"][l] = _matmul(sv["act"], df, "tn", BF16, f"wgrad_ffn_out{l}", TILES["wgrad_ffn_out"])
        grads["w_ffn_in"][l] = _matmul(du, sv["h2"], "tn", BF16, f"wgrad_ffn_in{l}", TILES["wgrad_ffn_in"])
        dh2 = _matmul(du, W_fin[l], "nn", F32, f"dgrad_ffn_in{l}", TILES["dgrad_ffn_in"])
        dx1, d_sh_f, d_sc_f, d_gf, d_g_m, dmix = _norm_mod_bwd(sv["x1"], dh2, dx, gf, sc_f, f"norm_ffn_bwd{l}",
                                                               below=(sv["mix"], g_m))
        grads["w_out"][l] = _matmul(sv["merged"], dmix, "tn", BF16, f"wgrad_out{l}", TILES["wgrad_out"])
        o_a, o_b, o_c = sv["o"]
        dgates, d_w_branch, do_a, do_b, do_c, dl_a, dl_b, dl_c = _merge_bwd(
            dmix, o_a, o_b, o_c, sv["gates"], W_branch[l], W_out[l], f"merge_bwd{l}")
        grads["w_branch"][l] = d_w_branch.astype(BF16)
        lse_rows = list(sv["lse"])
        if l == 0:
            reduce_rest0 = _exchange_start(rest_pieces(0), True, dgates, "comm_reduce_rest0_start")
            lse_rows = [t + reduce_rest0[4][0:1, 0:1] for t in lse_rows]
        dq_a, dk_a, dv_a, dsink = _attn_bwd("a", sv["qkv"], do_a, lse_rows[0], dl_a.reshape(4, 2, S), f"attn_a_bwd{l}",
                                            sinks=sinks[l], slopes=slopes)
        dq_b, dk_b, dv_b, dck, dcq = _attn_bwd("b", sv["qkv"], do_b, lse_rows[1], dl_b.reshape(4, 2, S),
                                               f"attn_b_bwd{l}", cq_row=sv["cum_row"], ck_col=sv["cum_col"])
        dq_c, dk_c, dv_c, dtiles_t = _attn_bwd("c", sv["qkv"], do_c, lse_rows[2], dl_c.reshape(4, 2, S),
                                               f"attn_c_bwd{l}", bias_t=sv["tiles_t"])
        grads["sinks"][l] = dsink[:, :2, 0].reshape(8)
        grads["rel_bias"][l] = _rel_reduce(dtiles_t, f"rel_reduce{l}")[:, 0, :N_REL]
        dcum_k = _pad_lanes(_heads_from_col(dck), BLK)
        dcum_q = _pad_lanes(_heads_from_row(dcq), BLK)
        dfb, d_bfor = _forget_bwd(dcum_q, dcum_k, sv["fb"], sv["bfor"], f"forget_bwd{l}")
        grads["b_forget"][l] = d_bfor[0, :N_FORGET]
        dproj = jnp.concatenate(
            [t.astype(BF16) for t in (dq_a, dk_a, dv_a, dq_b, dk_b, dv_b, dq_c, dk_c, dv_c, dgates, dfb)],
            axis=1)
        grads["w_in"][l] = _matmul(sv["h"], dproj, "tn", BF16, f"wgrad_in{l}", TILES["wgrad_in"])
        if l == 1:
            reduce1 = _exchange_start([_w_in_pieces(grads["w_in"][1], "w_in_pieces1")] + rest_pieces(1), True, dproj,
                                      "comm_reduce1_start")
        dh = _matmul(dproj, W_in[l], "nt", F32, f"dgrad_in{l}", TILES["dgrad_in"], after=reduce1[4] if l == 1 else None)
        d_g_f_here = d_g_f
        if l > 0:
            dx, d_sh_m, d_sc_m, d_gm, d_g_f, df = _norm_mod_bwd(sv["x"], dh, dx1, gm, sc_m, f"norm_mix_bwd{l}",
                                                                below=(saved[l - 1]["ffn"], mods[l - 1][5]))
        else:
            dx, d_sh_m, d_sc_m, d_gm = _norm_mod_bwd(sv["x"], dh, dx1, gm, sc_m, f"norm_mix_bwd{l}")
        grads["norm_mix_g"][l] = d_gm[0]
        grads["norm_ffn_g"][l] = d_gf[0]
        grads["dmod"][l] = jnp.concatenate([d_sh_m, d_sc_m, d_g_m, d_sh_f, d_sc_f, d_g_f_here], axis=1)[0]

    grad_x = dx.reshape(x.shape)

    small_shapes = dict(dmod=b_ada.shape, norm_mix_g=norm_mix_g.shape, norm_ffn_g=norm_ffn_g.shape,
                        final_norm_g=final_norm_g.shape, b_forget=b_forget.shape, sinks=sinks.shape,
                        rel_bias=rel_bias.shape, loss=())
    mine_small = _pack_small(dict(
        loss=_pad_lanes(loss_tile[0:1, 0:1], 128),
        dmod=jnp.stack(grads["dmod"]), norm_mix_g=jnp.stack(grads["norm_mix_g"]),
        norm_ffn_g=jnp.stack(grads["norm_ffn_g"]), final_norm_g=d_final_g[0],
        b_forget=_pad_lanes(jnp.stack(grads["b_forget"]).reshape(1, -1), 128),
        sinks=_pad_lanes(jnp.stack(grads["sinks"]).reshape(1, -1), 128),
        rel_bias=_pad_lanes(jnp.stack(grads["rel_bias"]).reshape(1, -1), 4224)))
    all_small = _small_all_gather(mine_small, "comm_gather_small").reshape(N_DEV, SMALL_ROWS, 128)
    pieces_in0 = _w_in_pieces(grads["w_in"][0], "w_in_pieces0", pair_major=True)
    from_sibling = _sibling_exchange([pieces_in0], "comm_reduce_in0_sibling")[0]
    pair_sum_in0 = _pair_add(pieces_in0, from_sibling, pc.astype(jnp.int32).reshape(1), "pair_add_in0")
    reduce_in0 = _exchange_start([pair_sum_in0], True, all_small, "comm_reduce_in0_start", CHIPS)
    in0_started = reduce_in0[4]

    def pack_params(b_ada_, nm, nf, fn, bf, sk, rb):
        return _pack_small(dict(dmod=b_ada_, norm_mix_g=nm, norm_ffn_g=nf, final_norm_g=fn, loss=jnp.zeros((1, 128), F32),
                                b_forget=_pad_lanes(bf.reshape(1, -1), 128), sinks=_pad_lanes(sk.reshape(1, -1), 128),
                                rel_bias=_pad_lanes(rb.reshape(1, -1), 4224)))

    small_out = _adamw(
        pack_params(b_ada, norm_mix_g, norm_ffn_g, final_norm_g, b_forget, sinks, rel_bias)[None],
        pack_params(m_b_ada, m_norm_mix_g, m_norm_ffn_g, m_final_norm_g, m_b_forget, m_sinks, m_rel_bias)[None],
        pack_params(v_b_ada, v_norm_mix_g, v_norm_ffn_g, v_final_norm_g, v_b_forget, v_sinks, v_rel_bias)[None],
        [all_small], "adamw_small", me_arr, after=in0_started)
    small_out = [_unpack_small(t[0], small_shapes) for t in small_out]

    dmod_all = all_small[:, :96].reshape(N_DEV, depth, 6 * D)
    dmod_cols = lax.dynamic_slice_in_dim(dmod_all, me * w_ada.shape[2], w_ada.shape[2], axis=2)
    d_w_ada = _ada_bwd(jnp.transpose(c_all), jnp.transpose(dmod_cols, (1, 0, 2)), "ada_bwd")

    big = {"w_ada": _adamw(w_ada, m_w_ada, v_w_ada, [d_w_ada[l:l + 1] for l in range(depth)], "adamw_w_ada", me_arr,
                           after=in0_started)}
    sent1, landed1 = _exchange_wait(reduce1, True, big["w_ada"][0], "comm_reduce1_wait")
    sent_rest0, landed_rest0 = _exchange_wait(reduce_rest0, True, landed1[0], "comm_reduce_rest0_wait")
    parts = {"w_in": [None, (landed1[0], sent1[0])]}
    for a, name in enumerate(("w_branch", "w_out", "w_ffn_in", "w_ffn_out")):
        parts[name] = [(landed_rest0[a], sent_rest0[a]), (landed1[1 + a], sent1[1 + a])]

    def update(name, w, m, v, view=lambda t: t):
        per_layer = lambda t: t.reshape(depth, -1, t.shape[-1])
        outs = _adamw(*[per_layer(view(t)) for t in (w, m, v)], parts[name], f"adamw_{name}",
                      me_in_arr if name == "w_in" else me_arr)
        big[name] = [view(t).reshape(w.shape) for t in outs]

    update("w_ffn_in", w_ffn_in, m_w_ffn_in, v_w_ffn_in, _w_ffn_in_view)
    update("w_ffn_out", w_ffn_out, m_w_ffn_out, v_w_ffn_out)
    update("w_branch", w_branch, m_w_branch, v_w_branch)
    update("w_out", w_out, m_w_out, v_w_out)
    sent_in0, landed_in0 = _exchange_wait(reduce_in0, True, big["w_out"][0], "comm_reduce_in0_wait", CHIPS)
    parts["w_in"][0] = (landed_in0[0], sent_in0[0])
    update("w_in", w_in, m_w_in, v_w_in)

    def leaf(kind, name):
        if name in big:
            return big[name][kind]
        return small_out[kind]["dmod" if name == "b_ada" else name]

    order = ["norm_mix_g", "norm_ffn_g", "w_ada", "b_ada", "w_in", "b_forget", "sinks", "rel_bias", "w_branch",
             "w_out", "w_ffn_in", "w_ffn_out", "final_norm_g"]
    loss = small_out[0]["loss"]
    return (loss, grad_x, *[leaf(0, n) for n in order], *[leaf(1, n) for n in order],
            *[leaf(2, n) for n in order], *[leaf(3, n) for n in order])
```

```python
import functools

import jax
import jax.numpy as jnp
from jax import lax
from jax.experimental import pallas as pl
from jax.experimental.pallas import tpu as pltpu

F32 = jnp.float32
BF16 = jnp.bfloat16
NEG_INF = -1e30
EPS = 1e-6
N_DEV = 8
BLK = 128
GROUP = 4 * BLK
VMEM_LIMIT_BYTES = 56 * 1024 * 1024

D_MODEL = 1024
N_QKV = 3840
N_GATES = 3072
N_MAIN = N_QKV + N_GATES
N_FORGET = 8
N_IN = N_MAIN + N_FORGET
F_COL = 2304
FFN_HIDDEN = 2816
N_REL = 257

ADAM_LR, ADAM_B1, ADAM_B2, ADAM_EPS, ADAM_WD, ADAM_STEP = 0.001, 0.9, 0.999, 1e-08, 0.01, 10

NN = (((1,), (0,)), ((), ()))
NT = (((1,), (1,)), ((), ()))
TN = (((0,), (0,)), ((), ()))
HIGHEST = lax.Precision.HIGHEST

ATTN_COLS = {"a": (0, 4, 5), "b": (6, 10, 14), "c": (18, 22, 26)}
ATTN_WINDOW = {"a": 2, "c": 5}
ATTN_BLOCKS_PER_STEP = {"a": 8, "b": GROUP // BLK, "c": 2}
ROW_TILE = 512


def _params():
    return pltpu.CompilerParams(vmem_limit_bytes=VMEM_LIMIT_BYTES)


def _tile(n, target):
    best = None
    t = 128
    while t <= min(n, target):
        if n % t == 0:
            best = t
        t += 128
    return best if best is not None else n


def _row_tile(n, target):
    t = min(n, target)
    while n % t:
        t -= 8
    return t


TILES = {
    "proj": (2048, 768, 1024), "proj_forget": (1024, 128, 1024),
    "ffn_out": (1024, 512, 2816), "ffn_fused": (512, 1408),
    "wgrad_ffn_out": (1408, 1024, 2048), "wgrad_ffn_in": (1408, 1024, 2048), "dgrad_ffn_in": (1024, 1024, 2816),
    "wgrad_out": (1024, 1024, 2048),
    "wgrad_in": (1024, 1408, 2048), "dgrad_in": (1024, 1024, 3520),
}


def _matmul(a, b, mode, out_dtype, name, tiles, *, n=None, a_off=0, b_off=0, m=None, after=None):
    tm, tn, tk = tiles
    if mode == "nn":
        M, K = a.shape if m is None else (m, a.shape[1])
        N = b.shape[1] if n is None else n
    elif mode == "nt":
        M, K = a.shape
        N = b.shape[0] if n is None else n
    else:
        K = a.shape[0]
        M = a.shape[1] if m is None else m
        N = b.shape[1] if n is None else n
    tm = _tile(M, tm) if M % 128 == 0 else M
    tn = _tile(N, tn)
    tk = _tile(K, tk)
    nk = K // tk
    dims = {"nn": NN, "nt": NT, "tn": TN}[mode]
    if mode == "nn":
        a_spec = pl.BlockSpec((tm, tk), lambda i, j, k: (i + a_off, k))
        b_spec = pl.BlockSpec((tk, tn), lambda i, j, k: (k, j + b_off))
    elif mode == "nt":
        a_spec = pl.BlockSpec((tm, tk), lambda i, j, k: (i + a_off, k))
        b_spec = pl.BlockSpec((tn, tk), lambda i, j, k: (j + b_off, k))
    else:
        a_spec = pl.BlockSpec((tk, tm), lambda i, j, k: (k, i + a_off))
        b_spec = pl.BlockSpec((tk, tn), lambda i, j, k: (k, j + b_off))

    def body(a_ref, b_ref, *rest):
        o_ref, acc_ref = rest[-2:]
        k = pl.program_id(2)
        part = lax.dot_general(a_ref[...], b_ref[...], dims, preferred_element_type=F32)
        if nk == 1:
            o_ref[...] = part.astype(o_ref.dtype)
        else:
            @pl.when(k == 0)
            def _():
                acc_ref[...] = part

            @pl.when(k > 0)
            def _():
                acc_ref[...] += part

            @pl.when(k == nk - 1)
            def _():
                o_ref[...] = acc_ref[...].astype(o_ref.dtype)

    return pl.pallas_call(
        body, name=name,
        out_shape=jax.ShapeDtypeStruct((M, N), out_dtype),
        grid=(M // tm, N // tn, nk),
        in_specs=[a_spec, b_spec] + ([ANY] if after is not None else []),
        out_specs=pl.BlockSpec((tm, tn), lambda i, j, k: (i, j)),
        scratch_shapes=[pltpu.VMEM((tm, tn) if nk > 1 else (8, 128), F32)],
        compiler_params=_params(),
    )(a, b, *([after] if after is not None else []))


def _project(h, w, name):
    S, D = h.shape
    tm, tn, _ = TILES["proj"]
    tm = _tile(S, tm)
    nq, ng = N_QKV // tn, N_GATES // tn

    def body(h_ref, w_ref, q_ref, g_ref):
        j = pl.program_id(1)
        acc = jnp.dot(h_ref[...], w_ref[...], preferred_element_type=F32)

        @pl.when(j < nq)
        def _():
            q_ref[...] = acc.astype(BF16)

        @pl.when(j >= nq)
        def _():
            g_ref[...] = acc

    return pl.pallas_call(
        body, name=name,
        out_shape=(jax.ShapeDtypeStruct((S, N_QKV), BF16), jax.ShapeDtypeStruct((S, N_GATES), F32)),
        grid=(S // tm, nq + ng),
        in_specs=[pl.BlockSpec((tm, D), lambda i, j: (i, 0)), pl.BlockSpec((D, tn), lambda i, j: (0, j))],
        out_specs=(pl.BlockSpec((tm, tn), lambda i, j: (i, jnp.minimum(j, nq - 1))),
                   pl.BlockSpec((tm, tn), lambda i, j: (i, jnp.maximum(j - nq, 0)))),
        compiler_params=_params(),
    )(h, w)


def _matmul_resid(a, b, resid, gate, name, tiles, after=None):
    M, K = a.shape
    N = b.shape[1]
    tm, tn, tk = (_tile(d, t) for d, t in zip((M, N, K), tiles))
    nk = K // tk

    def body(a_ref, b_ref, r_ref, g_ref, *rest):
        o_ref, s_ref, acc_ref = rest[-3:]
        k = pl.program_id(2)
        part = jnp.dot(a_ref[...], b_ref[...], preferred_element_type=F32)

        def finish(acc):
            o_ref[...] = r_ref[...] + g_ref[...] * acc
            s_ref[...] = acc.astype(BF16)

        if nk == 1:
            finish(part)
        else:
            @pl.when(k == 0)
            def _():
                acc_ref[...] = part

            @pl.when(k > 0)
            def _():
                acc_ref[...] += part

            @pl.when(k == nk - 1)
            def _():
                finish(acc_ref[...])

    return pl.pallas_call(
        body, name=name,
        out_shape=(jax.ShapeDtypeStruct((M, N), F32), jax.ShapeDtypeStruct((M, N), BF16)),
        grid=(M // tm, N // tn, nk),
        in_specs=[pl.BlockSpec((tm, tk), lambda i, j, k: (i, k)),
                  pl.BlockSpec((tk, tn), lambda i, j, k: (k, j)),
                  pl.BlockSpec((tm, tn), lambda i, j, k: (i, j)),
                  pl.BlockSpec((1, tn), lambda i, j, k: (0, j))] + ([ANY] if after is not None else []),
        out_specs=(pl.BlockSpec((tm, tn), lambda i, j, k: (i, j)),
                   pl.BlockSpec((tm, tn), lambda i, j, k: (i, j))),
        scratch_shapes=[pltpu.VMEM((tm, tn) if nk > 1 else (8, 128), F32)],
        compiler_params=_params(),
    )(a, b, resid, gate, *([after] if after is not None else []))


def _norm_mod_fwd(x, g, shift, scale, name):
    S, D = x.shape
    ts = _row_tile(S, ROW_TILE)

    def body(x_ref, g_ref, sh_ref, sc_ref, h_ref):
        xv = x_ref[...]
        rstd = lax.rsqrt(jnp.mean(xv * xv, axis=-1, keepdims=True) + EPS)
        y = xv * rstd * g_ref[...]
        h_ref[...] = (y * (1.0 + sc_ref[...]) + sh_ref[...]).astype(BF16)

    row = pl.BlockSpec((1, D), lambda i: (0, 0))
    return pl.pallas_call(
        body, name=name, out_shape=jax.ShapeDtypeStruct((S, D), BF16), grid=(S // ts,),
        in_specs=[pl.BlockSpec((ts, D), lambda i: (i, 0)), row, row, row],
        out_specs=pl.BlockSpec((ts, D), lambda i: (i, 0)),
        compiler_params=_params(),
    )(x, g, shift, scale)


def _accumulate_rows(i, pairs):
    @pl.when(i == 0)
    def _():
        for ref, value in pairs:
            ref[...] = value

    @pl.when(i > 0)
    def _():
        for ref, value in pairs:
            ref[...] += value


def _gated_residual_bwd(dx, f_ref, gate_ref, df_ref):
    df_ref[...] = (dx * gate_ref[...]).astype(BF16)
    return jnp.sum(dx * f_ref[...].astype(F32), axis=0, keepdims=True)


def _norm_mod_bwd(x, dh, dres, g, scale, name, below=None):
    S, D = x.shape
    ts = _row_tile(S, ROW_TILE)

    def body(x_ref, dh_ref, dr_ref, g_ref, sc_ref, *rest):
        i = pl.program_id(0)
        xv, dhv, gv = x_ref[...], dh_ref[...], g_ref[...]
        rstd = lax.rsqrt(jnp.mean(xv * xv, axis=-1, keepdims=True) + EPS)
        xhat = xv * rstd
        dn = dhv * (1.0 + sc_ref[...])
        dxhat = dn * gv
        proj = jnp.mean(dxhat * xhat, axis=-1, keepdims=True)
        dx = dr_ref[...] + rstd * (dxhat - xhat * proj)
        sums = [jnp.sum(dhv, axis=0, keepdims=True), jnp.sum(dhv * (xhat * gv), axis=0, keepdims=True),
                jnp.sum(dn * xhat, axis=0, keepdims=True)]
        if below is None:
            dx_ref, *sum_refs = rest
        else:
            f_ref, gate_ref, dx_ref, *sum_refs, df_ref = rest
            sums.append(_gated_residual_bwd(dx, f_ref, gate_ref, df_ref))
        dx_ref[...] = dx
        _accumulate_rows(i, list(zip(sum_refs, sums)))

    tile = pl.BlockSpec((ts, D), lambda i: (i, 0))
    row = pl.BlockSpec((1, D), lambda i: (0, 0))
    vec = jax.ShapeDtypeStruct((1, D), F32)
    fused = below is not None
    return pl.pallas_call(
        body, name=name,
        out_shape=(jax.ShapeDtypeStruct((S, D), F32), vec, vec, vec)
        + ((vec, jax.ShapeDtypeStruct((S, D), BF16)) if fused else ()),
        grid=(S // ts,),
        in_specs=[tile, tile, tile, row, row] + ([tile, row] if fused else []),
        out_specs=(tile, row, row, row) + ((row, tile) if fused else ()),
        compiler_params=_params(),
    )(x, dh, dres, g, scale, *(below if fused else ()))


def _ffn_in_fwd(h, w_t, name):
    S, D = h.shape
    F = w_t.shape[0] // 2
    tm, tn = _tile(S, TILES["ffn_fused"][0]), _tile(F, TILES["ffn_fused"][1])
    nj = F // tn

    def body(h_ref, wg_ref, wu_ref, o_ref):
        hv = h_ref[...]
        ug = lax.dot_general(hv, wg_ref[...], NT, preferred_element_type=F32)
        uu = lax.dot_general(hv, wu_ref[...], NT, preferred_element_type=F32)
        o_ref[...] = (ug * jax.nn.sigmoid(ug) * uu).astype(BF16)

    return pl.pallas_call(
        body, name=name, out_shape=jax.ShapeDtypeStruct((S, F), BF16), grid=(nj, S // tm),
        in_specs=[pl.BlockSpec((tm, D), lambda j, i: (i, 0)),
                  pl.BlockSpec((tn, D), lambda j, i: (j, 0)),
                  pl.BlockSpec((tn, D), lambda j, i: (j + nj, 0))],
        out_specs=pl.BlockSpec((tm, tn), lambda j, i: (i, j)),
        compiler_params=_params(),
    )(h, w_t, w_t)


def _ffn_mid_bwd(h, df, w_in_t, w_out, name):
    S, D = h.shape
    F = w_in_t.shape[0] // 2
    tm, tn = _tile(S, TILES["ffn_fused"][0]), _tile(F, TILES["ffn_fused"][1])
    nj = F // tn

    def body(h_ref, df_ref, wg_ref, wu_ref, wo_ref, dg_ref, du_ref):
        hv = h_ref[...]
        ug = lax.dot_general(hv, wg_ref[...], NT, preferred_element_type=F32)
        uu = lax.dot_general(hv, wu_ref[...], NT, preferred_element_type=F32)
        dact = lax.dot_general(df_ref[...], wo_ref[...], NT, preferred_element_type=F32)
        sig = jax.nn.sigmoid(ug)
        dg_ref[...] = (dact * uu * (sig * (1.0 + ug * (1.0 - sig)))).astype(BF16)
        du_ref[...] = (dact * (ug * sig)).astype(BF16)

    out = jax.ShapeDtypeStruct((S, F), BF16)
    return pl.pallas_call(
        body, name=name, out_shape=(out, out), grid=(nj, S // tm),
        in_specs=[pl.BlockSpec((tm, D), lambda j, i: (i, 0)),
                  pl.BlockSpec((tm, D), lambda j, i: (i, 0)),
                  pl.BlockSpec((tn, D), lambda j, i: (j, 0)),
                  pl.BlockSpec((tn, D), lambda j, i: (j + nj, 0)),
                  pl.BlockSpec((tn, D), lambda j, i: (j, 0))],
        out_specs=(pl.BlockSpec((tm, tn), lambda j, i: (i, j)), pl.BlockSpec((tm, tn), lambda j, i: (i, j))),
        compiler_params=_params(),
    )(h, df, w_in_t, w_in_t, w_out)


def _merge_fwd(o_a, o_b, o_c, gates, w_branch, w_out, resid, gate, name, *, tm=512):
    S, W = o_a.shape
    D = w_branch.shape[2]
    tm = _row_tile(S, tm)

    def body(oa_ref, ob_ref, oc_ref, g_ref, w_ref, wo_ref, r_ref, gm_ref, x_ref, m_ref, mix_ref):
        acc = None
        for k, o_ref in enumerate((oa_ref, ob_ref, oc_ref)):
            y = jnp.dot(o_ref[...], w_ref[k], preferred_element_type=F32)
            t = jax.nn.sigmoid(g_ref[:, k * D:(k + 1) * D]) * y
            acc = t if acc is None else acc + t
        merged = acc.astype(BF16)
        m_ref[...] = merged
        mix = jnp.dot(merged, wo_ref[...], preferred_element_type=F32)
        x_ref[...] = r_ref[...] + gm_ref[...] * mix
        mix_ref[...] = mix.astype(BF16)

    o_spec = pl.BlockSpec((tm, W), lambda i: (i, 0))
    tile = pl.BlockSpec((tm, D), lambda i: (i, 0))
    return pl.pallas_call(
        body, name=name,
        out_shape=(jax.ShapeDtypeStruct((S, D), F32), jax.ShapeDtypeStruct((S, D), BF16), jax.ShapeDtypeStruct((S, D), BF16)),
        grid=(S // tm,),
        in_specs=[o_spec, o_spec, o_spec, pl.BlockSpec((tm, 3 * D), lambda i: (i, 0)),
                  pl.BlockSpec((3, W, D), lambda i: (0, 0, 0)), pl.BlockSpec((D, D), lambda i: (0, 0)),
                  tile, pl.BlockSpec((1, D), lambda i: (0, 0))],
        out_specs=(tile, tile, tile),
        compiler_params=_params(),
    )(o_a, o_b, o_c, gates, w_branch, w_out, resid, gate)


def _merge_bwd(dmix, o_a, o_b, o_c, gates, w_branch, w_out, name, *, tm=256):
    S, W = o_a.shape
    D = w_branch.shape[2]
    tm = _row_tile(S, tm)
    n_heads = W // 64

    def body(dm_ref, oa_ref, ob_ref, oc_ref, g_ref, w_ref, wo_ref, dg_ref, dw_ref,
             doa_ref, dob_ref, doc_ref, dla_ref, dlb_ref, dlc_ref):
        first = pl.program_id(0) == 0
        head_of_column = (lax.broadcasted_iota(jnp.int32, (W, BLK), 0) // 64
                          == lax.broadcasted_iota(jnp.int32, (W, BLK), 1)).astype(F32)
        dm = lax.dot_general(dm_ref[...], wo_ref[...], NT, preferred_element_type=F32)
        branches = ((oa_ref, doa_ref, dla_ref), (ob_ref, dob_ref, dlb_ref), (oc_ref, doc_ref, dlc_ref))
        for k, (o_ref, do_ref, dl_ref) in enumerate(branches):
            wk = w_ref[k]
            ov = o_ref[...]
            y = jnp.dot(ov, wk, preferred_element_type=F32)
            g = jax.nn.sigmoid(g_ref[:, k * D:(k + 1) * D])
            dy = (dm * g).astype(BF16)
            dwk = lax.dot_general(ov, dy, TN, preferred_element_type=F32)

            @pl.when(first)
            def _(k=k, dwk=dwk):
                dw_ref[k] = dwk

            @pl.when(jnp.logical_not(first))
            def _(k=k, dwk=dwk):
                dw_ref[k] += dwk
            dg_ref[:, k * D:(k + 1) * D] = (dm * y * (g * (1.0 - g))).astype(BF16)
            do16 = lax.dot_general(dy, wk, NT, preferred_element_type=F32).astype(BF16)
            do_ref[...] = do16
            prod = do16.astype(F32) * ov.astype(F32)
            sums = jnp.dot(prod, head_of_column, preferred_element_type=F32, precision=HIGHEST)
            dl_ref[...] = jnp.transpose(sums)[:n_heads, :]

    o_spec = pl.BlockSpec((tm, W), lambda i: (i, 0))
    wide = pl.BlockSpec((tm, 3 * D), lambda i: (i, 0))
    dl_spec = pl.BlockSpec((n_heads, tm), lambda i: (0, i))
    o_out = jax.ShapeDtypeStruct((S, W), BF16)
    wide_out = jax.ShapeDtypeStruct((S, 3 * D), BF16)
    dl_out = jax.ShapeDtypeStruct((n_heads, S), F32)
    whole = pl.BlockSpec((3, W, D), lambda i: (0, 0, 0))
    return pl.pallas_call(
        body, name=name,
        out_shape=(wide_out, jax.ShapeDtypeStruct((3, W, D), F32), o_out, o_out, o_out, dl_out, dl_out, dl_out),
        grid=(S // tm,),
        in_specs=[pl.BlockSpec((tm, D), lambda i: (i, 0)), o_spec, o_spec, o_spec, wide, whole,
                  pl.BlockSpec((D, D), lambda i: (0, 0))],
        out_specs=(wide, whole, o_spec, o_spec, o_spec, dl_spec, dl_spec, dl_spec),
        compiler_params=_params(),
    )(dmix, o_a, o_b, o_c, gates, w_branch, w_out)


def _band_mask(variant, t_abs, s_abs):
    if variant == "b":
        return s_abs <= t_abs
    qc, kc = t_abs >> 6, s_abs >> 6
    return (kc <= qc) & (kc >= qc - (2 if variant == "a" else 8))


def _attn_fwd(variant, qkv, name, *, sinks=None, slopes=None, cq_col=None, ck_row=None, bias=None, after=None):
    S = qkv.shape[0]
    nb = S // BLK
    qb, kb, vb = ATTN_COLS[variant]
    shared_kv = variant == "a"
    win = ATTN_WINDOW.get(variant)
    per_step = ATTN_BLOCKS_PER_STEP[variant]

    def body(*refs):
        if after is not None:
            refs = refs[:-3] + refs[-2:]
        if variant == "a":
            q_ref, k_ref, v_ref, sink_ref, slope_ref, o_ref, lse_ref = refs
        elif variant == "b":
            q_ref, k_ref, v_ref, cq_ref, ck_ref, o_ref, lse_ref = refs
        else:
            q_ref, k_ref, v_ref, bias_ref, o_ref, lse_ref = refs
        p = pl.program_id(0)
        lane = lax.broadcasted_iota(jnp.int32, (1, BLK), 1)
        diagonal = lax.broadcasted_iota(jnp.int32, (BLK, BLK), 0) == lax.broadcasted_iota(jnp.int32, (BLK, BLK), 1)

        def compute(i, rows, start, n_keys):
            n_rows = rows.stop - rows.start
            t_abs = i * BLK + lax.broadcasted_iota(jnp.int32, (n_rows, 1), 0)
            q2 = q_ref[rows, :].astype(F32) * 0.125
            k_w = k_ref[pl.ds(start, n_keys), :]
            v_w = v_ref[pl.ds(start, n_keys), :]
            s_abs = start + lax.broadcasted_iota(jnp.int32, (1, n_keys), 1)
            valid = _band_mask(variant, t_abs, s_abs)
            outs = []
            for half in (0, 1):
                hmask = (lane >= 64) if half else (lane < 64)
                qh = jnp.where(hmask, q2, 0.0)
                if shared_kv:
                    swap = (p // 2) != half
                    qh = jnp.where(swap, pltpu.roll(qh, 64, 1), qh)
                s = lax.dot_general(qh.astype(BF16), k_w, NT, preferred_element_type=F32)
                if variant == "a":
                    head = 2 * p + half
                    s = s + (-slope_ref[head]) * jnp.abs(t_abs - s_abs).astype(F32)
                elif variant == "b":
                    s = s + cq_ref[rows, half:half + 1] - ck_ref[half:half + 1, pl.ds(start, n_keys)]
                else:
                    j0 = start // BLK
                    s = s + jnp.concatenate([jnp.concatenate(
                        [bias_ref[half, jnp.clip(i + r - j0 - b, 0, 4)] for b in range(n_keys // BLK)], axis=1)
                        for r in range(n_rows // BLK)], axis=0)
                s = jnp.where(valid, s, NEG_INF)
                m = jnp.max(s, axis=1, keepdims=True)
                if variant == "a":
                    m = jnp.maximum(m, sink_ref[head])
                pe = jnp.exp(s - m)
                l = jnp.sum(pe, axis=1, keepdims=True)
                if variant == "a":
                    l = l + jnp.exp(sink_ref[head] - m)
                out = jnp.dot(pe.astype(BF16), v_w, preferred_element_type=F32) / l
                if shared_kv:
                    out = jnp.where(swap, pltpu.roll(out, 64, 1), out)
                outs.append(out)
                lse = m + jnp.log(l)
                for b in range(n_rows // BLK):
                    part = jnp.where(diagonal, lse[b * BLK:(b + 1) * BLK, :], 0.0)
                    lse_ref[half:half + 1, rows.start + b * BLK:rows.start + (b + 1) * BLK] = jnp.sum(
                        part, axis=0, keepdims=True)
            o_ref[rows, :] = jnp.where(lane < 64, outs[0], outs[1]).astype(BF16)

        step = pl.program_id(1)
        if variant == "b":
            for g in range(S // GROUP):
                pl.when(step == g)(functools.partial(compute, step * per_step, slice(0, GROUP), 0, (g + 1) * GROUP))
        elif variant == "c":
            span = win + per_step - 1
            start = jnp.clip(step * per_step - (win - 1), 0, nb - span) * BLK
            compute(step * per_step, slice(0, per_step * BLK), pl.multiple_of(start, BLK), span * BLK)
        else:
            for sub in range(per_step):
                i = step * per_step + sub
                start = jnp.clip(i - (win - 1), 0, nb - win) * BLK
                compute(i, slice(sub * BLK, (sub + 1) * BLK), pl.multiple_of(start, BLK), win * BLK)

    tq = per_step * BLK
    kv_col = (lambda p, i: (0, kb)) if shared_kv else (lambda p, i: (0, kb + p))
    vv_col = (lambda p, i: (0, vb)) if shared_kv else (lambda p, i: (0, vb + p))
    in_specs = [pl.BlockSpec((tq, BLK), lambda p, i: (i, qb + p)),
                pl.BlockSpec((S, BLK), kv_col), pl.BlockSpec((S, BLK), vv_col)]
    args = [qkv, qkv, qkv]
    if variant == "a":
        in_specs += [pl.BlockSpec(memory_space=pltpu.SMEM), pl.BlockSpec(memory_space=pltpu.SMEM)]
        args += [sinks, slopes]
    elif variant == "b":
        in_specs += [pl.BlockSpec((None, tq, 2), lambda p, i: (p, i, 0)),
                     pl.BlockSpec((None, 2, S), lambda p, i: (p, 0, 0))]
        args += [cq_col, ck_row]
    else:
        in_specs += [pl.BlockSpec((2, 5, BLK, BLK), lambda p, i: (p, 0, 0, 0))]
        args += [bias]
    if after is not None:
        in_specs.append(ANY)
        args.append(after)
    return pl.pallas_call(
        body, name=name,
        out_shape=(jax.ShapeDtypeStruct((S, 512), BF16), jax.ShapeDtypeStruct((4, 2, S), F32)),
        grid=(4, nb // per_step), in_specs=in_specs,
        out_specs=(pl.BlockSpec((tq, BLK), lambda p, i: (i, p)),
                   pl.BlockSpec((None, 2, tq), lambda p, i: (p, 0, i))),
        compiler_params=_params(),
    )(*args)


def _attn_bwd(variant, qkv, do, lse_row, delta_row, name, *, sinks=None, slopes=None, cq_row=None,
              ck_col=None, bias_t=None):
    S = qkv.shape[0]
    nb = S // BLK
    qb, kb, vb = ATTN_COLS[variant]
    shared_kv = variant == "a"
    win = ATTN_WINDOW.get(variant)
    per_step = ATTN_BLOCKS_PER_STEP[variant]

    def body(*refs):
        *refs, dqt_ref = refs
        if variant == "a":
            (q_ref, k_ref, v_ref, do_ref, lse_ref, dl_ref, sink_ref, slope_ref,
             dq_ref, dk_ref, dv_ref, ex_ref) = refs
        elif variant == "b":
            (q_ref, k_ref, v_ref, do_ref, lse_ref, dl_ref, cq_ref, ck_ref,
             dq_ref, dk_ref, dv_ref, ex_ref, dcq_ref) = refs
        else:
            (q_ref, k_ref, v_ref, do_ref, lse_ref, dl_ref, bias_ref,
             dq_ref, dk_ref, dv_ref, ex_ref) = refs
        p = pl.program_id(0)
        lane = lax.broadcasted_iota(jnp.int32, (1, BLK), 1)
        hmasks = [(lane < 64), (lane >= 64)]
        swaps = [(p // 2) != half for half in (0, 1)] if shared_kv else None

        @pl.when(pl.program_id(1) == 0)
        def _():
            dqt_ref[...] = jnp.zeros_like(dqt_ref)
            if variant == "b":
                dcq_ref[...] = jnp.zeros_like(dcq_ref)
            else:
                ex_ref[...] = jnp.zeros_like(ex_ref)

        def to_kv_lanes(x, h):
            x = jnp.where(hmasks[h], x, 0.0)
            if shared_kv:
                x = jnp.where(swaps[h], pltpu.roll(x, 64, 1), x)
            return x

        def compute(j, rows, start, n_q):
            n_rows = rows.stop - rows.start
            s_abs = j * BLK + lax.broadcasted_iota(jnp.int32, (n_rows, 1), 0)
            off_k = pl.multiple_of(j * BLK, BLK)
            k2 = k_ref[rows, :].astype(F32)
            v2 = v_ref[rows, :].astype(F32)
            if shared_kv:
                kv_lane = (lane >> 6) == (p // 2)
                k_src, v_src = jnp.where(kv_lane, k2, 0.0), jnp.where(kv_lane, v2, 0.0)
                k_al = [jnp.where(swaps[h], pltpu.roll(k_src, 64, 1), k_src) for h in (0, 1)]
                v_al = [jnp.where(swaps[h], pltpu.roll(v_src, 64, 1), v_src) for h in (0, 1)]
            else:
                k_al = [jnp.where(hmasks[h], k2, 0.0) for h in (0, 1)]
                v_al = [jnp.where(hmasks[h], v2, 0.0) for h in (0, 1)]
            k_al = [(t * 0.125).astype(BF16) for t in k_al]
            v_al = [t.astype(BF16) for t in v_al]
            q_w = q_ref[pl.ds(start, n_q), :]
            do_w = do_ref[pl.ds(start, n_q), :]
            t_abs = start + lax.broadcasted_iota(jnp.int32, (1, n_q), 1)
            valid = _band_mask(variant, t_abs, s_abs)
            dk_acc = dv_acc = None
            ds_both = []
            for half in (0, 1):
                s = lax.dot_general(k_al[half], q_w, NT, preferred_element_type=F32)
                if variant == "a":
                    s = s + (-slope_ref[2 * p + half]) * jnp.abs(t_abs - s_abs).astype(F32)
                elif variant == "b":
                    s = s + cq_ref[half:half + 1, pl.ds(start, n_q)] - ck_ref[rows, half:half + 1]
                else:
                    i0 = start // BLK
                    s = s + jnp.concatenate([jnp.concatenate(
                        [bias_ref[half, jnp.clip(i0 + b - j - r, 0, 4)] for b in range(n_q // BLK)], axis=1)
                        for r in range(n_rows // BLK)], axis=0)
                pr = jnp.where(valid, jnp.exp(s - lse_ref[half:half + 1, pl.ds(start, n_q)]), 0.0)
                dp = lax.dot_general(v_al[half], do_w, NT, preferred_element_type=F32)
                ds = pr * (dp - dl_ref[half:half + 1, pl.ds(start, n_q)])
                ds16 = ds.astype(BF16)
                dv_h = to_kv_lanes(jnp.dot(pr.astype(BF16), do_w, preferred_element_type=F32), half)
                dk_h = to_kv_lanes(jnp.dot(ds16, q_w, preferred_element_type=F32) * 0.125, half)
                dv_acc = dv_h if dv_acc is None else dv_acc + dv_h
                dk_acc = dk_h if dk_acc is None else dk_acc + dk_h
                ds_both.append(ds16)
                if variant == "b":
                    ex_ref[rows, half:half + 1] = -jnp.sum(ds, axis=1, keepdims=True)
                    dcq_ref[half:half + 1, pl.ds(start, n_q)] += jnp.sum(ds, axis=0, keepdims=True)
                elif variant == "c":
                    for r in range(n_rows // BLK):
                        for b in range(n_q // BLK):
                            ex_ref[half, jnp.clip(i0 + b - j - r, 0, 4)] += ds[r * BLK:(r + 1) * BLK, b * BLK:(b + 1) * BLK]
            dq_t = lax.dot_general(jnp.concatenate(k_al, axis=0), jnp.concatenate(ds_both, axis=0), TN,
                                   preferred_element_type=F32)
            dqt_ref[:, pl.ds(start, n_q)] += dq_t
            if shared_kv:
                @pl.when(p == 0)
                def _():
                    dk_ref[pl.ds(off_k, n_rows), :] = dk_acc
                    dv_ref[pl.ds(off_k, n_rows), :] = dv_acc

                @pl.when(p > 0)
                def _():
                    dk_ref[pl.ds(off_k, n_rows), :] += dk_acc
                    dv_ref[pl.ds(off_k, n_rows), :] += dv_acc
            else:
                dk_ref[pl.ds(off_k, n_rows), :] = dk_acc.astype(dk_ref.dtype)
                dv_ref[pl.ds(off_k, n_rows), :] = dv_acc.astype(dv_ref.dtype)
            if variant == "a":
                for half in (0, 1):
                    p_sink = jnp.exp(sink_ref[2 * p + half] - lse_ref[half:half + 1, pl.ds(off_k, n_rows)])
                    term = p_sink * dl_ref[half:half + 1, pl.ds(off_k, n_rows)]
                    ex_ref[half:half + 1, :] += -jnp.sum(term, axis=1, keepdims=True)

        step = pl.program_id(1)
        if variant == "b":
            for g in range(S // GROUP):
                pl.when(step == g)(functools.partial(compute, step * per_step, slice(0, GROUP), g * GROUP, S - g * GROUP))
        elif variant == "c":
            span = win + per_step - 1
            start = jnp.clip(step * per_step, 0, nb - span) * BLK
            compute(step * per_step, slice(0, per_step * BLK), pl.multiple_of(start, BLK), span * BLK)
        else:
            for sub in range(per_step):
                j = step * per_step + sub
                start = jnp.clip(j, 0, nb - win) * BLK
                compute(j, slice(sub * BLK, (sub + 1) * BLK), pl.multiple_of(start, BLK), win * BLK)

        @pl.when(step == nb // per_step - 1)
        def _():
            dq_ref[...] = jnp.transpose(dqt_ref[...]).astype(BF16)

    tk = per_step * BLK
    col = lambda c0: (lambda p, j: (0, c0 + p))
    kv_blk = (lambda c0: (lambda p, j: (j, c0))) if shared_kv else (lambda c0: (lambda p, j: (j, c0 + p)))
    pair = lambda p, j: (0, p)
    row_stat = pl.BlockSpec((None, 2, S), lambda p, j: (p, 0, 0))
    in_specs = [pl.BlockSpec((S, BLK), col(qb)),
                pl.BlockSpec((tk, BLK), kv_blk(kb)), pl.BlockSpec((tk, BLK), kv_blk(vb)),
                pl.BlockSpec((S, BLK), pair), row_stat, row_stat]
    args = [qkv, qkv, qkv, do, lse_row, delta_row]
    kv_width = BLK if shared_kv else 512
    kv_out = pl.BlockSpec((S, BLK), (lambda p, j: (0, 0)) if shared_kv else pair)
    kv_dtype = F32 if shared_kv else BF16
    out_shape = [jax.ShapeDtypeStruct((S, 512), BF16), jax.ShapeDtypeStruct((S, kv_width), kv_dtype),
                 jax.ShapeDtypeStruct((S, kv_width), kv_dtype)]
    out_specs = [pl.BlockSpec((S, BLK), pair), kv_out, kv_out]
    if variant == "a":
        in_specs += [pl.BlockSpec(memory_space=pltpu.SMEM), pl.BlockSpec(memory_space=pltpu.SMEM)]
        args += [sinks, slopes]
        out_shape.append(jax.ShapeDtypeStruct((4, 8, BLK), F32))
        out_specs.append(pl.BlockSpec((None, 8, BLK), lambda p, j: (p, 0, 0)))
    elif variant == "b":
        in_specs += [row_stat, pl.BlockSpec((None, tk, 2), lambda p, j: (p, j, 0))]
        args += [cq_row, ck_col]
        out_shape += [jax.ShapeDtypeStruct((4, S, 2), F32), jax.ShapeDtypeStruct((4, 2, S), F32)]
        out_specs += [pl.BlockSpec((None, tk, 2), lambda p, j: (p, j, 0)), row_stat]
    else:
        in_specs += [pl.BlockSpec((2, 5, BLK, BLK), lambda p, j: (p, 0, 0, 0))]
        args += [bias_t]
        out_shape.append(jax.ShapeDtypeStruct((8, 5, BLK, BLK), F32))
        out_specs.append(pl.BlockSpec((2, 5, BLK, BLK), lambda p, j: (p, 0, 0, 0)))
    return pl.pallas_call(
        body, name=name, out_shape=tuple(out_shape), grid=(4, nb // per_step),
        in_specs=in_specs, out_specs=tuple(out_specs), scratch_shapes=[pltpu.VMEM((BLK, S), F32)],
        compiler_params=_params(),
    )(*args)


def _log_sigmoid(x):
    return jnp.minimum(x, 0.0) - jnp.log(1.0 + jnp.exp(-jnp.abs(x)))


def _forget_fwd(fb, b_forget, name):
    S = fb.shape[0]
    nb = S // GROUP

    def body(fb_ref, b_ref, cum_ref, carry_ref):
        i = pl.program_id(0)
        logf = _log_sigmoid(fb_ref[...] + b_ref[...])
        r = lax.broadcasted_iota(jnp.int32, (GROUP, GROUP), 0)
        c = lax.broadcasted_iota(jnp.int32, (GROUP, GROUP), 1)
        tri = (c <= r).astype(F32)

        @pl.when(i == 0)
        def _():
            carry_ref[...] = jnp.zeros_like(carry_ref)

        cum = jnp.dot(tri, logf, preferred_element_type=F32, precision=HIGHEST) + carry_ref[0:1, :]
        cum_ref[...] = cum
        carry_ref[...] = jnp.broadcast_to(cum[GROUP - 1:GROUP, :], carry_ref.shape)

    return pl.pallas_call(
        body, name=name, out_shape=jax.ShapeDtypeStruct((S, BLK), F32), grid=(nb,),
        in_specs=[pl.BlockSpec((GROUP, BLK), lambda i: (i, 0)), pl.BlockSpec((1, BLK), lambda i: (0, 0))],
        out_specs=pl.BlockSpec((GROUP, BLK), lambda i: (i, 0)),
        scratch_shapes=[pltpu.VMEM((8, BLK), F32)],
        compiler_params=_params(),
    )(fb, b_forget)


def _forget_bwd(dcum_q, dcum_k, fb, b_forget, name):
    S = fb.shape[0]
    nb = S // GROUP

    def body(dq_ref, dk_ref, fb_ref, b_ref, dfb_ref, db_ref, carry_ref):
        g = pl.program_id(0)
        r = lax.broadcasted_iota(jnp.int32, (GROUP, GROUP), 0)
        c = lax.broadcasted_iota(jnp.int32, (GROUP, GROUP), 1)
        tri = (c >= r).astype(F32)

        @pl.when(g == 0)
        def _():
            carry_ref[...] = jnp.zeros_like(carry_ref)

        dcum = dq_ref[...] + dk_ref[...]
        dlogf = jnp.dot(tri, dcum, preferred_element_type=F32, precision=HIGHEST) + carry_ref[0:1, :]
        carry_ref[...] = jnp.broadcast_to(dlogf[0:1, :], carry_ref.shape)
        x = fb_ref[...] + b_ref[...]
        lane = lax.broadcasted_iota(jnp.int32, (1, BLK), 1)
        dfb = jnp.where(lane < N_FORGET, dlogf * jax.nn.sigmoid(-x), 0.0)
        dfb_ref[...] = dfb
        db = jnp.sum(dfb, axis=0, keepdims=True)

        @pl.when(g == 0)
        def _():
            db_ref[...] = db

        @pl.when(g > 0)
        def _():
            db_ref[...] += db

    rev = pl.BlockSpec((GROUP, BLK), lambda g: (nb - 1 - g, 0))
    row = pl.BlockSpec((1, BLK), lambda g: (0, 0))
    return pl.pallas_call(
        body, name=name,
        out_shape=(jax.ShapeDtypeStruct((S, BLK), F32), jax.ShapeDtypeStruct((1, BLK), F32)), grid=(nb,),
        in_specs=[rev, rev, rev, row], out_specs=(rev, row),
        scratch_shapes=[pltpu.VMEM((8, BLK), F32)],
        compiler_params=_params(),
    )(dcum_q, dcum_k, fb, b_forget)


def _skew(x, sign):
    row = lax.broadcasted_iota(jnp.int32, x.shape, 0)
    for b in range(7):
        amount = (1 << b) if sign > 0 else 256 - (1 << b)
        x = jnp.where(((row >> b) & 1) == 1, pltpu.roll(x, amount, 1), x)
    return x


def _rel_bases(rel):
    far = rel[:, 256:257]
    far127 = jnp.broadcast_to(far, (rel.shape[0], 127))
    base0 = jnp.concatenate([rel[:, 128:0:-1], far, rel[:, 255:128:-1]], axis=1)
    base1 = jnp.concatenate([rel[:, 256:128:-1], far, far127], axis=1)
    base0_t = jnp.concatenate([rel[:, 128:256], far, rel[:, 1:128]], axis=1)
    base1_t = jnp.concatenate([jnp.broadcast_to(far, (rel.shape[0], 128)), far, rel[:, 129:256]], axis=1)
    return jnp.stack([base0, base1, base0_t, base1_t], axis=1)


def _rel_expand(bases, name):
    def body(b_ref, t_ref, tt_ref):
        far = jnp.broadcast_to(b_ref[1:2, 0:1], (BLK, BLK))
        for k, out_ref in ((0, t_ref), (2, tt_ref)):
            for d in (0, 1):
                x = jnp.broadcast_to(b_ref[k + d:k + d + 1, :], (BLK, 2 * BLK))
                out_ref[d] = _skew(x, 1)[:, :BLK]
            for d in (2, 3, 4):
                out_ref[d] = far

    out = jax.ShapeDtypeStruct((8, 5, BLK, BLK), F32)
    spec = pl.BlockSpec((None, 5, BLK, BLK), lambda h: (h, 0, 0, 0))
    return pl.pallas_call(
        body, name=name, out_shape=(out, out), grid=(8,),
        in_specs=[pl.BlockSpec((None, 4, 2 * BLK), lambda h: (h, 0, 0))], out_specs=(spec, spec),
        compiler_params=_params(),
    )(bases)


def _rel_reduce(dtiles_t, name):
    def body(dt_ref, o_ref):
        zeros = jnp.zeros((BLK, BLK), F32)
        sums = []
        for d in (0, 1):
            x = _skew(jnp.concatenate([dt_ref[d], zeros], axis=1), -1)
            sums.append(jnp.broadcast_to(jnp.sum(x, axis=0, keepdims=True), (8, 2 * BLK)))
        lane = lax.broadcasted_iota(jnp.int32, (8, 2 * BLK), 1)
        main = pltpu.roll(sums[0], BLK, 1) + jnp.where(lane > BLK, sums[1], 0.0)
        far = jnp.sum(jnp.where(lane < BLK, sums[1], 0.0)[0:1], axis=1, keepdims=True)
        far = far + jnp.sum(jnp.sum(dt_ref[2] + dt_ref[3] + dt_ref[4], axis=0, keepdims=True), axis=1, keepdims=True)
        o_ref[...] = jnp.concatenate([main[0:1], jnp.broadcast_to(far, (1, BLK))], axis=1)

    return pl.pallas_call(
        body, name=name, out_shape=jax.ShapeDtypeStruct((8, 1, 3 * BLK), F32), grid=(8,),
        in_specs=[pl.BlockSpec((None, 5, BLK, BLK), lambda h: (h, 0, 0, 0))],
        out_specs=pl.BlockSpec((None, 1, 3 * BLK), lambda h: (h, 0, 0)),
        compiler_params=_params(),
    )(dtiles_t)


def _final_loss(x, target, g, below, name):
    S, D = x.shape
    ts = _row_tile(S, ROW_TILE)

    def body(x_ref, t_ref, g_ref, f_ref, gate_ref, dx_ref, loss_ref, dg_ref, dgate_ref, df_ref):
        i = pl.program_id(0)
        xv, gv = x_ref[...], g_ref[...]
        rstd = lax.rsqrt(jnp.mean(xv * xv, axis=-1, keepdims=True) + EPS)
        xhat = xv * rstd
        err = xhat * gv - t_ref[...]
        part = 0.5 * jnp.sum(jnp.mean(err * err, axis=-1, keepdims=True), axis=0, keepdims=True)
        dy = err / D
        dg = jnp.sum(dy * xhat, axis=0, keepdims=True)
        dxhat = dy * gv
        proj = jnp.mean(dxhat * xhat, axis=-1, keepdims=True)
        dx = rstd * (dxhat - xhat * proj)
        dx_ref[...] = dx
        dgate = _gated_residual_bwd(dx, f_ref, gate_ref, df_ref)
        _accumulate_rows(i, [(loss_ref, jnp.broadcast_to(part, loss_ref.shape)), (dg_ref, dg), (dgate_ref, dgate)])

    tile = pl.BlockSpec((ts, D), lambda i: (i, 0))
    row = pl.BlockSpec((1, D), lambda i: (0, 0))
    vec = jax.ShapeDtypeStruct((1, D), F32)
    return pl.pallas_call(
        body, name=name,
        out_shape=(jax.ShapeDtypeStruct((S, D), F32), jax.ShapeDtypeStruct((8, 128), F32), vec, vec,
                   jax.ShapeDtypeStruct((S, D), BF16)),
        grid=(S // ts,), in_specs=[tile, tile, row, tile, row],
        out_specs=(tile, pl.BlockSpec((8, 128), lambda i: (0, 0)), row, row, tile),
        compiler_params=_params(),
    )(x, target, g, *below)


def _ada_fwd(c_all, w_ada, name):
    L, D, E = w_ada.shape

    def body(c_ref, w_ref, o_ref):
        cv = c_ref[...]
        cond = cv * jax.nn.sigmoid(cv)
        o_ref[...] = jnp.dot(cond, w_ref[...], preferred_element_type=F32, precision=HIGHEST)

    return pl.pallas_call(
        body, name=name, out_shape=jax.ShapeDtypeStruct((L, N_DEV, E), F32), grid=(L,),
        in_specs=[pl.BlockSpec((N_DEV, D), lambda l: (0, 0)), pl.BlockSpec((None, D, E), lambda l: (l, 0, 0))],
        out_specs=pl.BlockSpec((None, N_DEV, E), lambda l: (l, 0, 0)),
        compiler_params=_params(),
    )(c_all, w_ada)


def _ada_bwd(c_all_t, dmod, name):
    D = c_all_t.shape[0]
    L, _, E = dmod.shape

    def body(c_ref, d_ref, o_ref):
        cv = c_ref[...]
        cond = cv * jax.nn.sigmoid(cv)
        acc = None
        for b in range(N_DEV):
            t = cond[:, b:b + 1] * d_ref[b:b + 1, :]
            acc = t if acc is None else acc + t
        o_ref[...] = acc

    return pl.pallas_call(
        body, name=name, out_shape=jax.ShapeDtypeStruct((L, D, E), F32), grid=(L,),
        in_specs=[pl.BlockSpec((D, N_DEV), lambda l: (0, 0)), pl.BlockSpec((None, N_DEV, E), lambda l: (l, 0, 0))],
        out_specs=pl.BlockSpec((None, D, E), lambda l: (l, 0, 0)),
        compiler_params=_params(),
    )(c_all_t, dmod)


def _adamw(w, m, v, g_parts, name, me, after=None):
    L, R, C = w.shape
    tr = _row_tile(R, max(8, (256 * 1024 // max(C, 128)) // 8 * 8))
    nr = R // tr
    c1 = 1.0 - ADAM_B1 ** ADAM_STEP
    c2 = 1.0 - ADAM_B2 ** ADAM_STEP
    direct = [isinstance(p, tuple) for p in g_parts]
    n_in = sum(2 if d else 1 for d in direct)

    def body(me_ref, w_ref, m_ref, v_ref, *rest):
        g_refs, (go_ref, d_ref, mo_ref, vo_ref) = list(rest[:n_in]), rest[-4:]
        layer = pl.program_id(0)
        g = None
        for l in range(L):
            land_ref = g_refs.pop(0)
            own = g_refs.pop(0)[...].astype(F32) if direct[l] else None
            gl = None
            for k in range(land_ref.shape[0]):
                part = land_ref[k].astype(F32)
                if direct[l]:
                    part = jnp.where(me_ref[l] == k, own, part)
                gl = part if gl is None else gl + part
            g = gl if g is None else jnp.where(layer == l, gl, g)
        mn = ADAM_B1 * m_ref[...] + (1.0 - ADAM_B1) * g
        vn = ADAM_B2 * v_ref[...] + (1.0 - ADAM_B2) * (g * g)
        m_hat = mn / c1
        v_hat = vn / c2
        go_ref[...] = g
        d_ref[...] = -ADAM_LR * (m_hat / (jnp.sqrt(v_hat) + ADAM_EPS) + ADAM_WD * w_ref[...])
        mo_ref[...] = mn
        vo_ref[...] = vn

    def rows(l, layer, i):
        return jnp.where(layer == l, i, 0 if l > 0 else nr - 1)

    in_specs, operands = [], []
    for l, p in enumerate(g_parts):
        land, sent = p if direct[l] else (p, None)
        in_specs.append(pl.BlockSpec((land.shape[0], tr, C), lambda layer, i, me_ref, l=l: (0, rows(l, layer, i), 0)))
        operands.append(land)
        if direct[l]:
            in_specs.append(pl.BlockSpec((None, tr, C), lambda layer, i, me_ref, l=l: (me_ref[l], rows(l, layer, i), 0)))
            operands.append(sent)
    if after is not None:
        in_specs.append(ANY)
        operands.append(after)
    tile = pl.BlockSpec((None, tr, C), lambda layer, i, me_ref: (layer, i, 0))
    out = jax.ShapeDtypeStruct((L, R, C), F32)
    return pl.pallas_call(
        body, name=name, out_shape=(out, out, out, out),
        grid_spec=pltpu.PrefetchScalarGridSpec(
            num_scalar_prefetch=1, grid=(L, nr), in_specs=[tile, tile, tile] + in_specs,
            out_specs=(tile, tile, tile, tile)),
        compiler_params=_params(),
    )(me, w, m, v, *operands)


def _pair_add(pieces, recv, core, name):
    _, _, R, C = pieces.shape
    tr = _row_tile(R, max(8, (512 * 1024 // max(C, 128)) // 8 * 8))

    def body(core_ref, a_ref, b_ref, o_ref):
        o_ref[...] = (a_ref[...].astype(F32) + b_ref[...].astype(F32)).astype(BF16)

    return pl.pallas_call(
        body, name=name, out_shape=jax.ShapeDtypeStruct((4, R, C), BF16),
        grid_spec=pltpu.PrefetchScalarGridSpec(
            num_scalar_prefetch=1, grid=(4, R // tr),
            in_specs=[pl.BlockSpec((None, None, tr, C), lambda k, i, core_ref: (core_ref[0], k, i, 0)),
                      pl.BlockSpec((None, tr, C), lambda k, i, core_ref: (k, i, 0))],
            out_specs=pl.BlockSpec((None, tr, C), lambda k, i, core_ref: (k, i, 0))),
        compiler_params=_params(),
    )(core, pieces, recv)


MESH = pl.DeviceIdType.MESH
ANY = pl.BlockSpec(memory_space=pl.ANY)


def _position():
    return lax.axis_index("x"), lax.axis_index("y"), lax.axis_index("c")


def _small_all_gather(v, name):
    m_per, n = v.shape

    def body(x_ref, out_ref, send_sems, recv_sems, local_sem):
        x, y, c = _position()
        me, sibling = (x, y, c), (x, y, 1 - c)
        chips = [(1 - x, y), (x, 1 - y), (1 - x, 1 - y)]

        def rows(px, py, pc):
            return out_ref.at[pl.ds((4 * px + 2 * py + pc) * m_per, m_per), :]

        def copy(k, block, to, src=None):
            return pltpu.make_async_remote_copy(
                src_ref=rows(*block) if src is None else src, dst_ref=rows(*block),
                send_sem=send_sems.at[k], recv_sem=recv_sems.at[k], device_id=to, device_id_type=MESH)

        mine = pltpu.make_async_copy(x_ref, rows(*me), local_sem)
        mine.start()
        first = [copy(0, me, sibling, src=x_ref)]
        first += [copy(1 + j, me, (*chip, c), src=x_ref) for j, chip in enumerate(chips)]
        for cp in first:
            cp.start()
        passed = [copy(4 + j, (*chip, c), sibling) for j, chip in enumerate(chips)]
        for j, chip in enumerate(chips):
            copy(1 + j, (*chip, c), me).wait_recv()
            passed[j].start()
        copy(0, sibling, me).wait_recv()
        for j, chip in enumerate(chips):
            copy(4 + j, (*chip, 1 - c), me).wait_recv()
        for cp in first + passed:
            cp.wait_send()
        mine.wait()

    return pl.pallas_call(
        body, name=name, out_shape=jax.ShapeDtypeStruct((N_DEV * m_per, n), v.dtype),
        in_specs=[pl.BlockSpec(memory_space=pltpu.VMEM)], out_specs=pl.BlockSpec(memory_space=pltpu.VMEM),
        scratch_shapes=[pltpu.SemaphoreType.DMA((7,)), pltpu.SemaphoreType.DMA((7,)), pltpu.SemaphoreType.DMA],
    )(v)


def _sibling_exchange(pieces, name):
    n_arr = len(pieces)

    def body(*refs):
        p_refs, out_refs = refs[:n_arr], refs[n_arr:2 * n_arr]
        send_sems, recv_sems = refs[2 * n_arr:]
        x, y, c = _position()
        copies = [pltpu.make_async_remote_copy(
            src_ref=p_refs[a].at[1 - c], dst_ref=out_refs[a], send_sem=send_sems.at[a], recv_sem=recv_sems.at[a],
            device_id=(x, y, 1 - c), device_id_type=MESH) for a in range(n_arr)]
        for cp in copies:
            cp.start()
        for cp in copies:
            cp.wait()

    return pl.pallas_call(
        body, name=name,
        out_shape=tuple(jax.ShapeDtypeStruct(p.shape[1:], p.dtype) for p in pieces),
        in_specs=[ANY] * n_arr, out_specs=tuple([ANY] * n_arr),
        scratch_shapes=[pltpu.SemaphoreType.DMA((n_arr,)), pltpu.SemaphoreType.DMA((n_arr,))],
    )(*pieces)


HBM = pl.BlockSpec(memory_space=pltpu.HBM)
SEM = pl.BlockSpec(memory_space=pltpu.SEMAPHORE)
EFFECT = pltpu.SideEffectType.DATAFLOW_SIDE_EFFECTING
RELATIONS = [(rx, ry, rc) for rx in (0, 1) for ry in (0, 1) for rc in (0, 1)][1:]


SAME_CORE = [r for r in RELATIONS if r == (0, 0, 1) or r[2] == 0]


CHIPS = [r for r in RELATIONS if r[2] == 0]


def _exchange_copies(src_refs, land_refs, send_sems, recv_sems, scatter, receive_side, relations):
    x, y, c = _position()
    index = (lambda px, py, pc: 2 * px + py) if relations == CHIPS else (lambda px, py, pc: 4 * px + 2 * py + pc)
    me = index(x, y, c)
    copies = []
    for k, (rx, ry, rc) in enumerate(relations):
        peer = ((1 - x) if rx else x, (1 - y) if ry else y, (1 - c) if rc else c)
        peer_index = index(*peer)
        for a, (src, land) in enumerate(zip(src_refs, land_refs)):
            copies.append(pltpu.make_async_remote_copy(
                src_ref=src.at[peer_index] if scatter else src,
                dst_ref=land.at[peer_index if receive_side else me],
                send_sem=send_sems.at[a * len(relations) + k], recv_sem=recv_sems.at[a * len(relations) + k],
                device_id=peer, device_id_type=MESH))
    return copies


def _exchange_start(srcs, scatter, after, name, relations=RELATIONS):
    n = len(srcs)
    land_shapes = [(s.shape if scatter else (N_DEV,) + s.shape) for s in srcs]

    def body(*refs):
        src_refs, land_refs = refs[:n], refs[n:2 * n]
        send_sems, recv_sems = refs[2 * n + 1], refs[2 * n + 2]
        token = refs[-1]
        for cp in _exchange_copies(src_refs, land_refs, send_sems, recv_sems, scatter, False, relations):
            cp.start()
        token[...] = jnp.zeros_like(token)

    sems = pltpu.SemaphoreType.DMA((n * len(relations),))
    outs = pl.pallas_call(
        body, name=name,
        out_shape=(sems, sems, *[pltpu.HBM(s.shape, s.dtype) for s in srcs],
                   *[pltpu.HBM(shape, s.dtype) for shape, s in zip(land_shapes, srcs)],
                   jax.ShapeDtypeStruct((8, 128), F32)),
        in_specs=[HBM] * (2 * n) + [ANY],
        out_specs=(SEM, SEM, *[HBM] * (2 * n), pl.BlockSpec(memory_space=pltpu.VMEM)),
        input_output_aliases={a: 2 + a for a in range(2 * n)},
        compiler_params=pltpu.CompilerParams(has_side_effects=EFFECT),
    )(*[pltpu.with_memory_space_constraint(s, pltpu.HBM) for s in srcs],
      *[pltpu.with_memory_space_constraint(lax.empty(shape, s.dtype), pltpu.HBM)
        for shape, s in zip(land_shapes, srcs)], after)
    return outs[0], outs[1], outs[2:2 + n], outs[2 + n:2 + 2 * n], outs[-1]


def _exchange_wait(started, scatter, after, name, relations=RELATIONS):
    send_sems, recv_sems, srcs, lands, _ = started
    n = len(srcs)

    def body(*refs):
        src_refs, land_refs = refs[:n], refs[n:2 * n]
        send_sems, recv_sems = refs[2 * n], refs[2 * n + 1]
        copies = _exchange_copies(src_refs, land_refs, send_sems, recv_sems, scatter, True, relations)
        for cp in copies:
            cp.wait_send()
        for cp in copies:
            cp.wait_recv()

    outs = pl.pallas_call(
        body, name=name,
        out_shape=(*[pltpu.HBM(s.shape, s.dtype) for s in srcs], *[pltpu.HBM(t.shape, t.dtype) for t in lands]),
        in_specs=[HBM] * (2 * n) + [SEM, SEM, ANY], out_specs=tuple([HBM] * (2 * n)),
        input_output_aliases={a: a for a in range(2 * n)},
        compiler_params=pltpu.CompilerParams(has_side_effects=EFFECT),
    )(*srcs, *lands, send_sems, recv_sems, after)
    return outs[:n], outs[n:]


def _forward_copies(land_refs, send_sems, recv_sems, receive_side):
    x, y, c = _position()
    copies = []
    for j, (px, py) in enumerate([(1 - x, y), (x, 1 - y), (1 - x, 1 - y)]):
        held, coming = 4 * px + 2 * py + c, 4 * px + 2 * py + (1 - c)
        for a, land in enumerate(land_refs):
            copies.append(pltpu.make_async_remote_copy(
                src_ref=land.at[held], dst_ref=land.at[coming if receive_side else held],
                send_sem=send_sems.at[3 * a + j], recv_sem=recv_sems.at[3 * a + j],
                device_id=(x, y, 1 - c), device_id_type=MESH))
    return copies


def _forward_start(lands, after, name):
    n = len(lands)

    def body(*refs):
        send_sems, recv_sems, token = refs[n + 1], refs[n + 2], refs[-1]
        for cp in _forward_copies(refs[:n], send_sems, recv_sems, False):
            cp.start()
        token[...] = jnp.zeros_like(token)

    sems = pltpu.SemaphoreType.DMA((3 * n,))
    outs = pl.pallas_call(
        body, name=name,
        out_shape=(sems, sems, *[pltpu.HBM(t.shape, t.dtype) for t in lands], jax.ShapeDtypeStruct((8, 128), F32)),
        in_specs=[HBM] * n + [ANY], out_specs=(SEM, SEM, *[HBM] * n, pl.BlockSpec(memory_space=pltpu.VMEM)),
        input_output_aliases={a: 2 + a for a in range(n)},
        compiler_params=pltpu.CompilerParams(has_side_effects=EFFECT),
    )(*lands, after)
    return outs[0], outs[1], outs[2:2 + n], outs[-1]


def _forward_wait(started, after, name):
    send_sems, recv_sems, lands, _ = started
    n = len(lands)

    def body(*refs):
        copies = _forward_copies(refs[:n], refs[n], refs[n + 1], True)
        for cp in copies:
            cp.wait_send()
        for cp in copies:
            cp.wait_recv()

    return pl.pallas_call(
        body, name=name, out_shape=tuple(pltpu.HBM(t.shape, t.dtype) for t in lands),
        in_specs=[HBM] * n + [SEM, SEM, ANY], out_specs=tuple([HBM] * n),
        input_output_aliases={a: a for a in range(n)},
        compiler_params=pltpu.CompilerParams(has_side_effects=EFFECT),
    )(*lands, send_sems, recv_sems, after)


def _place_own(lands, mine, me, name):
    n = len(lands)
    flat = [m.reshape(-1, m.shape[-1]) for m in mine]
    flat_lands = [t.reshape(N_DEV, -1, t.shape[-1]) for t in lands]

    def body(me_ref, *refs):
        for src, dst in zip(refs[:n], refs[2 * n:]):
            dst[...] = src[...]

    in_specs = [pl.BlockSpec((m.shape[0] // 2, m.shape[1]), lambda i, me_ref: (i, 0)) for m in flat]
    out_specs = [pl.BlockSpec((None, m.shape[0] // 2, m.shape[1]), lambda i, me_ref: (me_ref[0], i, 0)) for m in flat]
    outs = pl.pallas_call(
        body, name=name, out_shape=tuple(jax.ShapeDtypeStruct(t.shape, t.dtype) for t in flat_lands),
        grid_spec=pltpu.PrefetchScalarGridSpec(
            num_scalar_prefetch=1, grid=(2,), in_specs=in_specs + [ANY] * n, out_specs=tuple(out_specs)),
        input_output_aliases={1 + n + a: a for a in range(n)},
        compiler_params=_params(),
    )(me, *flat, *flat_lands)
    return [o.reshape(t.shape) for o, t in zip(outs, lands)]


W_IN_SHARD = N_IN // N_DEV
F_SHARD = F_COL // W_IN_SHARD
F_LO = F_COL - F_SHARD * W_IN_SHARD


def _w_ffn_in_view(w):
    return jnp.transpose(w, (0, 2, 1))


def _w_in_segments():
    segments = []
    for d in range(N_DEV):
        if d == F_SHARD:
            segments += [(d, 0, d * W_IN_SHARD, F_LO), (d, F_LO, N_MAIN, N_FORGET),
                         (d, F_LO + N_FORGET, F_COL, W_IN_SHARD - F_LO - N_FORGET)]
        else:
            segments.append((d, 0, d * W_IN_SHARD - (N_FORGET if d > F_SHARD else 0), W_IN_SHARD))
    return segments


def _w_in_rearranged(g, name):
    D = g.shape[1]
    tr = _row_tile(D, 256)

    def body(g_ref, o_ref):
        for d, lo, at, width in _w_in_segments():
            o_ref[:, at:at + width] = g_ref[d, :, lo:lo + width]
        o_ref[:, N_IN:] = jnp.zeros((tr, BLK - N_FORGET), o_ref.dtype)

    return pl.pallas_call(
        body, name=name, out_shape=jax.ShapeDtypeStruct((D, N_MAIN + BLK), g.dtype), grid=(D // tr,),
        in_specs=[pl.BlockSpec((N_DEV, tr, W_IN_SHARD), lambda i: (0, i, 0))],
        out_specs=pl.BlockSpec((tr, N_MAIN + BLK), lambda i: (i, 0)),
        compiler_params=_params(),
    )(g)


def _w_in_pieces(dw_r, name, pair_major=False):
    D = dw_r.shape[0]
    tr = _row_tile(D, 256)
    lead = (2, 4) if pair_major else (N_DEV,)

    def body(x_ref, o_ref):
        for d, lo, at, width in _w_in_segments():
            slot = (d % 2, d // 2) if pair_major else (d,)
            o_ref[(*slot, slice(None), slice(lo, lo + width))] = x_ref[:, at:at + width]

    return pl.pallas_call(
        body, name=name, out_shape=jax.ShapeDtypeStruct((*lead, D, W_IN_SHARD), dw_r.dtype), grid=(D // tr,),
        in_specs=[pl.BlockSpec((tr, N_MAIN + BLK), lambda i: (i, 0))],
        out_specs=pl.BlockSpec((*lead, tr, W_IN_SHARD), lambda i: (*[0] * len(lead), i, 0)),
        compiler_params=_params(),
    )(dw_r)


def _row_pieces(dw):
    return dw.reshape(N_DEV, dw.shape[0] // N_DEV, dw.shape[1])


def _branch_pieces(dw):
    k, w, d = dw.shape
    return jnp.transpose(dw.reshape(k, w, N_DEV, d // N_DEV), (2, 0, 1, 3)).reshape(N_DEV, k * w, d // N_DEV)


def _pairs_col(a):
    return jnp.transpose(a.reshape(a.shape[0], 4, 2), (1, 0, 2))


def _pairs_row(a):
    return jnp.transpose(a.reshape(a.shape[0], 4, 2), (1, 2, 0))


def _heads_from_col(a):
    return jnp.transpose(a, (1, 0, 2)).reshape(a.shape[1], 8)


def _heads_from_row(a):
    return jnp.transpose(a, (2, 0, 1)).reshape(a.shape[2], 8)


def _pad_lanes(a, n):
    return jnp.pad(a, [(0, 0)] * (a.ndim - 1) + [(0, n - a.shape[-1])])


SMALL_SEGMENTS = (("dmod", 2 * 6 * D_MODEL), ("norm_mix_g", 2 * D_MODEL), ("norm_ffn_g", 2 * D_MODEL),
                  ("final_norm_g", D_MODEL), ("b_forget", 128), ("sinks", 128), ("rel_bias", 4224), ("loss", 128))
SMALL_ROWS = 176


def _pack_small(parts):
    flat = [_pad_lanes(parts[name].reshape(1, -1), size) for name, size in SMALL_SEGMENTS]
    total = sum(size for _, size in SMALL_SEGMENTS)
    flat.append(jnp.zeros((1, SMALL_ROWS * 128 - total), F32))
    return jnp.concatenate(flat, axis=1).reshape(SMALL_ROWS, 128)


def _unpack_small(packed, shapes):
    flat = packed.reshape(-1)
    out, pos = {}, 0
    for name, size in SMALL_SEGMENTS:
        shape = shapes[name]
        count = 1
        for d in shape:
            count *= d
        out[name] = flat[pos:pos + count].reshape(shape)
        pos += size
    return out


def kernel(x, c, norm_mix_g, norm_ffn_g, w_ada, b_ada, w_in, b_forget, sinks, rel_bias, w_branch, w_out, w_ffn_in, w_ffn_out, final_norm_g, loss_target, m_norm_mix_g, m_norm_ffn_g, m_w_ada, m_b_ada, m_w_in, m_b_forget, m_sinks, m_rel_bias, m_w_branch, m_w_out, m_w_ffn_in, m_w_ffn_out, m_final_norm_g, v_norm_mix_g, v_norm_ffn_g, v_w_ada, v_b_ada, v_w_in, v_b_forget, v_sinks, v_rel_bias, v_w_branch, v_w_out, v_w_ffn_in, v_w_ffn_out, v_final_norm_g):
    depth = w_in.shape[0]
    S, D = x.shape[1], x.shape[2]
    assert S % GROUP == 0 and S >= ATTN_WINDOW["c"] * BLK
    px, py, pc = _position()
    me = 4 * px + 2 * py + pc
    x0 = x[0]

    assert depth == 2
    big_weights = (w_in, w_branch, w_out, w_ffn_in, w_ffn_out)
    me_arr = jnp.stack([me, me]).astype(jnp.int32)
    me_in_arr = jnp.stack([2 * px + py, me]).astype(jnp.int32)

    def rest_matrices(g_branch, g_out, g_fin, g_fout):
        return (jnp.transpose(g_branch, (1, 2, 0, 3)).reshape(3, 512, D), g_out.reshape(D, D),
                g_fin.reshape(2 * FFN_HIDDEN, D), g_fout.reshape(FFN_HIDDEN, D))

    def arrive(started, after, name):
        mine, landed = _exchange_wait(started, False, after, f"{name}_wait", SAME_CORE)
        return mine, _forward_start(landed, mine[0], f"{name}_forward_start")

    def finish_gather(arrived, after, name):
        mine, forward = arrived
        landed = _forward_wait(forward, after, f"{name}_forward_wait")
        return _place_own(landed, mine, me.astype(jnp.int32).reshape(1), f"{name}_own")

    w_fin_t = _w_ffn_in_view(w_ffn_in)
    shards = [[t.astype(BF16) for t in (w_in[l], w_branch[l], w_out[l], w_fin_t[l], w_ffn_out[l])]
              for l in range(depth)]
    c_all = _small_all_gather(c.reshape(8, 128), "comm_gather_c").reshape(N_DEV, D)
    mod_cols = _ada_fwd(c_all, w_ada, "ada_fwd")
    mod_all = _small_all_gather(mod_cols.reshape(-1, 128), "comm_gather_mod")
    gather_in0 = _exchange_start(shards[0][:1], False, mod_all, "comm_gather_w_in0_start", SAME_CORE)
    gather_rest0 = _exchange_start(shards[0][1:], False, gather_in0[4], "comm_gather_rest0_start", SAME_CORE)
    gather1 = _exchange_start(shards[1], False, gather_rest0[4], "comm_gather_weights1_start", SAME_CORE)
    started = gather1[4][0:1, 0:1]
    W_in, W_branch, W_out, W_fin, W_fout = ([None, None] for _ in range(5))
    mod_all = mod_all.reshape(N_DEV, depth, N_DEV, w_ada.shape[2])
    mod_mine = lax.dynamic_index_in_dim(mod_all, me, axis=2, keepdims=False)
    mod = jnp.transpose(mod_mine, (1, 0, 2)).reshape(depth, 6 * D) + b_ada + started
    mods = [[mod[l:l + 1, k * D:(k + 1) * D] for k in range(6)] for l in range(depth)]
    rel_tiles = [_rel_expand(_rel_bases(rel_bias[l]) + started, f"rel_expand{l}") for l in range(depth)]

    slopes = jnp.exp2(-jnp.arange(1, 9, dtype=F32))
    saved = []
    xs = x0
    for l in range(depth):
        if l == 1:
            g_in1, *g_rest1 = finish_gather(arrived1, xs, "comm_gather_weights1")
            W_in[1] = _w_in_rearranged(g_in1, "w_in_rearrange1")
            W_branch[1], W_out[1], W_fin[1], W_fout[1] = rest_matrices(*g_rest1)
        sh_m, sc_m, g_m, sh_f, sc_f, g_f = mods[l]
        gm, gf = norm_mix_g[l:l + 1], norm_ffn_g[l:l + 1]
        bfor = _pad_lanes(b_forget[l:l + 1], BLK)
        h = _norm_mod_fwd(xs, gm, sh_m, sc_m, f"norm_mix_fwd{l}")
        tiles, tiles_t = rel_tiles[l]
        if l == 0:
            arrived_in0 = arrive(gather_in0, rel_tiles[-1][1], "comm_gather_w_in0")
            W_in[0] = _w_in_rearranged(finish_gather(arrived_in0, h, "comm_gather_w_in0")[0], "w_in_rearrange0")
        qkv, gates = _project(h, W_in[l], f"proj{l}")
        fb = _matmul(h, W_in[l], "nn", F32, f"proj_forget{l}", TILES["proj_forget"], n=BLK, b_off=N_MAIN // BLK)
        cum = _forget_fwd(fb, bfor, f"forget_fwd{l}")[:, :N_FORGET]
        cum_col, cum_row = _pairs_col(cum), _pairs_row(cum)
        o_a, lse_a = _attn_fwd("a", qkv, f"attn_a_fwd{l}", sinks=sinks[l], slopes=slopes)
        o_b, lse_b = _attn_fwd("b", qkv, f"attn_b_fwd{l}", cq_col=cum_col, ck_row=cum_row)
        arrived_rest0 = arrive(gather_rest0, o_b, "comm_gather_rest0") if l == 0 else None
        o_c, lse_c = _attn_fwd("c", qkv, f"attn_c_fwd{l}", bias=tiles, after=arrived_rest0[1][3] if l == 0 else None)
        if l == 0:
            W_branch[0], W_out[0], W_fin[0], W_fout[0] = rest_matrices(
                *finish_gather(arrived_rest0, o_c, "comm_gather_rest0"))
        x1, merged, mix = _merge_fwd(o_a, o_b, o_c, gates, W_branch[l], W_out[l], xs, g_m, f"merge_fwd{l}")
        h2 = _norm_mod_fwd(x1, gf, sh_f, sc_f, f"norm_ffn_fwd{l}")
        act = _ffn_in_fwd(h2, W_fin[l], f"ffn_in_fwd{l}")
        if l == 0:
            arrived1 = arrive(gather1, act, "comm_gather_weights1")
        x2, ffn = _matmul_resid(act, W_fout[l], x1, g_f, f"ffn_out{l}", TILES["ffn_out"],
                                after=arrived1[1][3] if l == 0 else None)
        saved.append(dict(x=xs, h=h, qkv=qkv, gates=gates, fb=fb, bfor=bfor, cum_col=cum_col, cum_row=cum_row,
                          tiles_t=tiles_t, o=(o_a, o_b, o_c), lse=(lse_a, lse_b, lse_c), merged=merged, mix=mix,
                          x1=x1, h2=h2, act=act, ffn=ffn))
        xs = x2

    dx, loss_tile, d_final_g, d_g_f, df = _final_loss(
        xs, loss_target[0], final_norm_g.reshape(1, D), (saved[-1]["ffn"], mods[-1][5]), "final_loss")

    grads = {k: [None] * depth for k in ("w_in", "w_branch", "w_out", "w_ffn_in", "w_ffn_out", "norm_mix_g",
                                          "norm_ffn_g", "b_forget", "sinks", "rel_bias", "dmod")}
    def rest_pieces(l):
        return [_branch_pieces(grads["w_branch"][l]), _row_pieces(grads["w_out"][l]),
                _row_pieces(grads["w_ffn_in"][l]), _row_pieces(grads["w_ffn_out"][l])]

    reduce1 = reduce_rest0 = reduce_in0 = None
    for l in reversed(range(depth)):
        sv = saved[l]
        sh_m, sc_m, g_m, sh_f, sc_f, g_f = mods[l]
        gm, gf = norm_mix_g[l:l + 1], norm_ffn_g[l:l + 1]
        du_g, du_u = _ffn_mid_bwd(sv["h2"], df, W_fin[l], W_fout[l], f"ffn_mid_bwd{l}")
        du = jnp.concatenate([du_g, du_u], axis=1)
        grads["w_ffn_out"][l] = _matmul(sv["act"], df, "tn", BF16, f"wgrad_ffn_out{l}", TILES["wgrad_ffn_out"])
        grads["w_ffn_in"][l] = _matmul(du, sv["h2"], "tn", BF16, f"wgrad_ffn_in{l}", TILES["wgrad_ffn_in"])
        dh2 = _matmul(du, W_fin[l], "nn", F32, f"dgrad_ffn_in{l}", TILES["dgrad_ffn_in"])
        dx1, d_sh_f, d_sc_f, d_gf, d_g_m, dmix = _norm_mod_bwd(sv["x1"], dh2, dx, gf, sc_f, f"norm_ffn_bwd{l}",
                                                               below=(sv["mix"], g_m))
        grads["w_out"][l] = _matmul(sv["merged"], dmix, "tn", BF16, f"wgrad_out{l}", TILES["wgrad_out"])
        o_a, o_b, o_c = sv["o"]
        dgates, d_w_branch, do_a, do_b, do_c, dl_a, dl_b, dl_c = _merge_bwd(
            dmix, o_a, o_b, o_c, sv["gates"], W_branch[l], W_out[l], f"merge_bwd{l}")
        grads["w_branch"][l] = d_w_branch.astype(BF16)
        lse_rows = list(sv["lse"])
        if l == 0:
            reduce_rest0 = _exchange_start(rest_pieces(0), True, dgates, "comm_reduce_rest0_start")
            lse_rows = [t + reduce_rest0[4][0:1, 0:1] for t in lse_rows]
        dq_a, dk_a, dv_a, dsink = _attn_bwd("a", sv["qkv"], do_a, lse_rows[0], dl_a.reshape(4, 2, S), f"attn_a_bwd{l}",
                                            sinks=sinks[l], slopes=slopes)
        dq_b, dk_b, dv_b, dck, dcq = _attn_bwd("b", sv["qkv"], do_b, lse_rows[1], dl_b.reshape(4, 2, S),
                                               f"attn_b_bwd{l}", cq_row=sv["cum_row"], ck_col=sv["cum_col"])
        dq_c, dk_c, dv_c, dtiles_t = _attn_bwd("c", sv["qkv"], do_c, lse_rows[2], dl_c.reshape(4, 2, S),
                                               f"attn_c_bwd{l}", bias_t=sv["tiles_t"])
        grads["sinks"][l] = dsink[:, :2, 0].reshape(8)
        grads["rel_bias"][l] = _rel_reduce(dtiles_t, f"rel_reduce{l}")[:, 0, :N_REL]
        dcum_k = _pad_lanes(_heads_from_col(dck), BLK)
        dcum_q = _pad_lanes(_heads_from_row(dcq), BLK)
        dfb, d_bfor = _forget_bwd(dcum_q, dcum_k, sv["fb"], sv["bfor"], f"forget_bwd{l}")
        grads["b_forget"][l] = d_bfor[0, :N_FORGET]
        dproj = jnp.concatenate(
            [t.astype(BF16) for t in (dq_a, dk_a, dv_a, dq_b, dk_b, dv_b, dq_c, dk_c, dv_c, dgates, dfb)],
            axis=1)
        grads["w_in"][l] = _matmul(sv["h"], dproj, "tn", BF16, f"wgrad_in{l}", TILES["wgrad_in"])
        if l == 1:
            reduce1 = _exchange_start([_w_in_pieces(grads["w_in"][1], "w_in_pieces1")] + rest_pieces(1), True, dproj,
                                      "comm_reduce1_start")
        dh = _matmul(dproj, W_in[l], "nt", F32, f"dgrad_in{l}", TILES["dgrad_in"], after=reduce1[4] if l == 1 else None)
        d_g_f_here = d_g_f
        if l > 0:
            dx, d_sh_m, d_sc_m, d_gm, d_g_f, df = _norm_mod_bwd(sv["x"], dh, dx1, gm, sc_m, f"norm_mix_bwd{l}",
                                                                below=(saved[l - 1]["ffn"], mods[l - 1][5]))
        else:
            dx, d_sh_m, d_sc_m, d_gm = _norm_mod_bwd(sv["x"], dh, dx1, gm, sc_m, f"norm_mix_bwd{l}")
        grads["norm_mix_g"][l] = d_gm[0]
        grads["norm_ffn_g"][l] = d_gf[0]
        grads["dmod"][l] = jnp.concatenate([d_sh_m, d_sc_m, d_g_m, d_sh_f, d_sc_f, d_g_f_here], axis=1)[0]

    grad_x = dx.reshape(x.shape)

    small_shapes = dict(dmod=b_ada.shape, norm_mix_g=norm_mix_g.shape, norm_ffn_g=norm_ffn_g.shape,
                        final_norm_g=final_norm_g.shape, b_forget=b_forget.shape, sinks=sinks.shape,
                        rel_bias=rel_bias.shape, loss=())
    mine_small = _pack_small(dict(
        loss=_pad_lanes(loss_tile[0:1, 0:1], 128),
        dmod=jnp.stack(grads["dmod"]), norm_mix_g=jnp.stack(grads["norm_mix_g"]),
        norm_ffn_g=jnp.stack(grads["norm_ffn_g"]), final_norm_g=d_final_g[0],
        b_forget=_pad_lanes(jnp.stack(grads["b_forget"]).reshape(1, -1), 128),
        sinks=_pad_lanes(jnp.stack(grads["sinks"]).reshape(1, -1), 128),
        rel_bias=_pad_lanes(jnp.stack(grads["rel_bias"]).reshape(1, -1), 4224)))
    all_small = _small_all_gather(mine_small, "comm_gather_small").reshape(N_DEV, SMALL_ROWS, 128)
    pieces_in0 = _w_in_pieces(grads["w_in"][0], "w_in_pieces0", pair_major=True)
    from_sibling = _sibling_exchange([pieces_in0], "comm_reduce_in0_sibling")[0]
    pair_sum_in0 = _pair_add(pieces_in0, from_sibling, pc.astype(jnp.int32).reshape(1), "pair_add_in0")
    reduce_in0 = _exchange_start([pair_sum_in0], True, all_small, "comm_reduce_in0_start", CHIPS)
    in0_started = reduce_in0[4]

    def pack_params(b_ada_, nm, nf, fn, bf, sk, rb):
        return _pack_small(dict(dmod=b_ada_, norm_mix_g=nm, norm_ffn_g=nf, final_norm_g=fn, loss=jnp.zeros((1, 128), F32),
                                b_forget=_pad_lanes(bf.reshape(1, -1), 128), sinks=_pad_lanes(sk.reshape(1, -1), 128),
                                rel_bias=_pad_lanes(rb.reshape(1, -1), 4224)))

    small_out = _adamw(
        pack_params(b_ada, norm_mix_g, norm_ffn_g, final_norm_g, b_forget, sinks, rel_bias)[None],
        pack_params(m_b_ada, m_norm_mix_g, m_norm_ffn_g, m_final_norm_g, m_b_forget, m_sinks, m_rel_bias)[None],
        pack_params(v_b_ada, v_norm_mix_g, v_norm_ffn_g, v_final_norm_g, v_b_forget, v_sinks, v_rel_bias)[None],
        [all_small], "adamw_small", me_arr, after=in0_started)
    small_out = [_unpack_small(t[0], small_shapes) for t in small_out]

    dmod_all = all_small[:, :96].reshape(N_DEV, depth, 6 * D)
    dmod_cols = lax.dynamic_slice_in_dim(dmod_all, me * w_ada.shape[2], w_ada.shape[2], axis=2)
    d_w_ada = _ada_bwd(jnp.transpose(c_all), jnp.transpose(dmod_cols, (1, 0, 2)), "ada_bwd")

    big = {"w_ada": _adamw(w_ada, m_w_ada, v_w_ada, [d_w_ada[l:l + 1] for l in range(depth)], "adamw_w_ada", me_arr,
                           after=in0_started)}
    sent1, landed1 = _exchange_wait(reduce1, True, big["w_ada"][0], "comm_reduce1_wait")
    sent_rest0, landed_rest0 = _exchange_wait(reduce_rest0, True, landed1[0], "comm_reduce_rest0_wait")
    parts = {"w_in": [None, (landed1[0], sent1[0])]}
    for a, name in enumerate(("w_branch", "w_out", "w_ffn_in", "w_ffn_out")):
        parts[name] = [(landed_rest0[a], sent_rest0[a]), (landed1[1 + a], sent1[1 + a])]

    def update(name, w, m, v, view=lambda t: t):
        per_layer = lambda t: t.reshape(depth, -1, t.shape[-1])
        outs = _adamw(*[per_layer(view(t)) for t in (w, m, v)], parts[name], f"adamw_{name}",
                      me_in_arr if name == "w_in" else me_arr)
        big[name] = [view(t).reshape(w.shape) for t in outs]

    update("w_ffn_in", w_ffn_in, m_w_ffn_in, v_w_ffn_in, _w_ffn_in_view)
    update("w_ffn_out", w_ffn_out, m_w_ffn_out, v_w_ffn_out)
    update("w_branch", w_branch, m_w_branch, v_w_branch)
    update("w_out", w_out, m_w_out, v_w_out)
    sent_in0, landed_in0 = _exchange_wait(reduce_in0, True, big["w_out"][0], "comm_reduce_in0_wait", CHIPS)
    parts["w_in"][0] = (landed_in0[0], sent_in0[0])
    update("w_in", w_in, m_w_in, v_w_in)

    def leaf(kind, name):
        if name in big:
            return big[name][kind]
        return small_out[kind]["dmod" if name == "b_ada" else name]

    order = ["norm_mix_g", "norm_ffn_g", "w_ada", "b_ada", "w_in", "b_forget", "sinks", "rel_bias", "w_branch",
             "w_out", "w_ffn_in", "w_ffn_out", "final_norm_g"]
    loss = small_out[0]["loss"]
    return (loss, grad_x, *[leaf(0, n) for n in order], *[leaf(1, n) for n in order],
            *[leaf(2, n) for n in order], *[leaf(3, n) for n in order])
```

```python
import functools

import jax
import jax.numpy as jnp
from jax import lax
from jax.experimental import pallas as pl
from jax.experimental.pallas import tpu as pltpu

F32 = jnp.float32
BF16 = jnp.bfloat16
NEG_INF = -1e30
EPS = 1e-6
N_DEV = 8
BLK = 128
GROUP = 4 * BLK
VMEM_LIMIT_BYTES = 56 * 1024 * 1024

D_MODEL = 1024
N_QKV = 3840
N_GATES = 3072
N_MAIN = N_QKV + N_GATES
N_FORGET = 8
N_IN = N_MAIN + N_FORGET
F_COL = 2304
FFN_HIDDEN = 2816
N_REL = 257

ADAM_LR, ADAM_B1, ADAM_B2, ADAM_EPS, ADAM_WD, ADAM_STEP = 0.001, 0.9, 0.999, 1e-08, 0.01, 10

NN = (((1,), (0,)), ((), ()))
NT = (((1,), (1,)), ((), ()))
TN = (((0,), (0,)), ((), ()))
HIGHEST = lax.Precision.HIGHEST

ATTN_COLS = {"a": (0, 4, 5), "b": (6, 10, 14), "c": (18, 22, 26)}
ATTN_WINDOW = {"a": 2, "c": 5}
ATTN_BLOCKS_PER_STEP = {"a": 8, "b": GROUP // BLK, "c": 2}
ROW_TILE = 1024


def _params():
    return pltpu.CompilerParams(vmem_limit_bytes=VMEM_LIMIT_BYTES)


def _tile(n, target):
    best = None
    t = 128
    while t <= min(n, target):
        if n % t == 0:
            best = t
        t += 128
    return best if best is not None else n


def _row_tile(n, target):
    t = min(n, target)
    while n % t:
        t -= 8
    return t


TILES = {
    "proj": (2048, 768, 1024), "proj_forget": (1024, 128, 1024),
    "ffn_out": (1024, 512, 2816), "ffn_fused": (512, 1408),
    "wgrad_ffn_out": (1408, 1024, 2048), "wgrad_ffn_in": (1408, 1024, 2048), "dgrad_ffn_in": (1024, 1024, 2816),
    "wgrad_out": (1024, 1024, 2048),
    "wgrad_in": (1024, 1408, 2048), "dgrad_in": (1024, 1024, 3520),
}


def _matmul(a, b, mode, out_dtype, name, tiles, *, n=None, a_off=0, b_off=0, m=None, after=None):
    tm, tn, tk = tiles
    if mode == "nn":
        M, K = a.shape if m is None else (m, a.shape[1])
        N = b.shape[1] if n is None else n
    elif mode == "nt":
        M, K = a.shape
        N = b.shape[0] if n is None else n
    else:
        K = a.shape[0]
        M = a.shape[1] if m is None else m
        N = b.shape[1] if n is None else n
    tm = _tile(M, tm) if M % 128 == 0 else M
    tn = _tile(N, tn)
    tk = _tile(K, tk)
    nk = K // tk
    dims = {"nn": NN, "nt": NT, "tn": TN}[mode]
    if mode == "nn":
        a_spec = pl.BlockSpec((tm, tk), lambda i, j, k: (i + a_off, k))
        b_spec = pl.BlockSpec((tk, tn), lambda i, j, k: (k, j + b_off))
    elif mode == "nt":
        a_spec = pl.BlockSpec((tm, tk), lambda i, j, k: (i + a_off, k))
        b_spec = pl.BlockSpec((tn, tk), lambda i, j, k: (j + b_off, k))
    else:
        a_spec = pl.BlockSpec((tk, tm), lambda i, j, k: (k, i + a_off))
        b_spec = pl.BlockSpec((tk, tn), lambda i, j, k: (k, j + b_off))

    def body(a_ref, b_ref, *rest):
        o_ref, acc_ref = rest[-2:]
        k = pl.program_id(2)
        part = lax.dot_general(a_ref[...], b_ref[...], dims, preferred_element_type=F32)
        if nk == 1:
            o_ref[...] = part.astype(o_ref.dtype)
        else:
            @pl.when(k == 0)
            def _():
                acc_ref[...] = part

            @pl.when(k > 0)
            def _():
                acc_ref[...] += part

            @pl.when(k == nk - 1)
            def _():
                o_ref[...] = acc_ref[...].astype(o_ref.dtype)

    return pl.pallas_call(
        body, name=name,
        out_shape=jax.ShapeDtypeStruct((M, N), out_dtype),
        grid=(M // tm, N // tn, nk),
        in_specs=[a_spec, b_spec] + ([ANY] if after is not None else []),
        out_specs=pl.BlockSpec((tm, tn), lambda i, j, k: (i, j)),
        scratch_shapes=[pltpu.VMEM((tm, tn) if nk > 1 else (8, 128), F32)],
        compiler_params=_params(),
    )(a, b, *([after] if after is not None else []))


def _project(h, w, name):
    S, D = h.shape
    tm, tn, _ = TILES["proj"]
    tm = _tile(S, tm)
    nq, ng = N_QKV // tn, N_GATES // tn

    def body(h_ref, w_ref, q_ref, g_ref):
        j = pl.program_id(1)
        acc = jnp.dot(h_ref[...], w_ref[...], preferred_element_type=F32)

        @pl.when(j < nq)
        def _():
            q_ref[...] = acc.astype(BF16)

        @pl.when(j >= nq)
        def _():
            g_ref[...] = acc

    return pl.pallas_call(
        body, name=name,
        out_shape=(jax.ShapeDtypeStruct((S, N_QKV), BF16), jax.ShapeDtypeStruct((S, N_GATES), F32)),
        grid=(S // tm, nq + ng),
        in_specs=[pl.BlockSpec((tm, D), lambda i, j: (i, 0)), pl.BlockSpec((D, tn), lambda i, j: (0, j))],
        out_specs=(pl.BlockSpec((tm, tn), lambda i, j: (i, jnp.minimum(j, nq - 1))),
                   pl.BlockSpec((tm, tn), lambda i, j: (i, jnp.maximum(j - nq, 0)))),
        compiler_params=_params(),
    )(h, w)


def _matmul_resid(a, b, resid, gate, name, tiles, after=None):
    M, K = a.shape
    N = b.shape[1]
    tm, tn, tk = (_tile(d, t) for d, t in zip((M, N, K), tiles))
    nk = K // tk

    def body(a_ref, b_ref, r_ref, g_ref, *rest):
        o_ref, s_ref, acc_ref = rest[-3:]
        k = pl.program_id(2)
        part = jnp.dot(a_ref[...], b_ref[...], preferred_element_type=F32)

        def finish(acc):
            o_ref[...] = r_ref[...] + g_ref[...] * acc
            s_ref[...] = acc.astype(BF16)

        if nk == 1:
            finish(part)
        else:
            @pl.when(k == 0)
            def _():
                acc_ref[...] = part

            @pl.when(k > 0)
            def _():
                acc_ref[...] += part

            @pl.when(k == nk - 1)
            def _():
                finish(acc_ref[...])

    return pl.pallas_call(
        body, name=name,
        out_shape=(jax.ShapeDtypeStruct((M, N), F32), jax.ShapeDtypeStruct((M, N), BF16)),
        grid=(M // tm, N // tn, nk),
        in_specs=[pl.BlockSpec((tm, tk), lambda i, j, k: (i, k)),
                  pl.BlockSpec((tk, tn), lambda i, j, k: (k, j)),
                  pl.BlockSpec((tm, tn), lambda i, j, k: (i, j)),
                  pl.BlockSpec((1, tn), lambda i, j, k: (0, j))] + ([ANY] if after is not None else []),
        out_specs=(pl.BlockSpec((tm, tn), lambda i, j, k: (i, j)),
                   pl.BlockSpec((tm, tn), lambda i, j, k: (i, j))),
        scratch_shapes=[pltpu.VMEM((tm, tn) if nk > 1 else (8, 128), F32)],
        compiler_params=_params(),
    )(a, b, resid, gate, *([after] if after is not None else []))


def _norm_mod_fwd(x, g, shift, scale, name):
    S, D = x.shape
    ts = _row_tile(S, ROW_TILE)

    def body(x_ref, g_ref, sh_ref, sc_ref, h_ref):
        xv = x_ref[...]
        rstd = lax.rsqrt(jnp.mean(xv * xv, axis=-1, keepdims=True) + EPS)
        y = xv * rstd * g_ref[...]
        h_ref[...] = (y * (1.0 + sc_ref[...]) + sh_ref[...]).astype(BF16)

    row = pl.BlockSpec((1, D), lambda i: (0, 0))
    return pl.pallas_call(
        body, name=name, out_shape=jax.ShapeDtypeStruct((S, D), BF16), grid=(S // ts,),
        in_specs=[pl.BlockSpec((ts, D), lambda i: (i, 0)), row, row, row],
        out_specs=pl.BlockSpec((ts, D), lambda i: (i, 0)),
        compiler_params=_params(),
    )(x, g, shift, scale)


def _accumulate_rows(i, pairs):
    @pl.when(i == 0)
    def _():
        for ref, value in pairs:
            ref[...] = value

    @pl.when(i > 0)
    def _():
        for ref, value in pairs:
            ref[...] += value


def _gated_residual_bwd(dx, f_ref, gate_ref, df_ref):
    df_ref[...] = (dx * gate_ref[...]).astype(BF16)
    return jnp.sum(dx * f_ref[...].astype(F32), axis=0, keepdims=True)


def _norm_mod_bwd(x, dh, dres, g, scale, name, below=None):
    S, D = x.shape
    ts = _row_tile(S, ROW_TILE)

    def body(x_ref, dh_ref, dr_ref, g_ref, sc_ref, *rest):
        i = pl.program_id(0)
        xv, dhv, gv = x_ref[...], dh_ref[...], g_ref[...]
        rstd = lax.rsqrt(jnp.mean(xv * xv, axis=-1, keepdims=True) + EPS)
        xhat = xv * rstd
        dn = dhv * (1.0 + sc_ref[...])
        dxhat = dn * gv
        proj = jnp.mean(dxhat * xhat, axis=-1, keepdims=True)
        dx = dr_ref[...] + rstd * (dxhat - xhat * proj)
        sums = [jnp.sum(dhv, axis=0, keepdims=True), jnp.sum(dhv * (xhat * gv), axis=0, keepdims=True),
                jnp.sum(dn * xhat, axis=0, keepdims=True)]
        if below is None:
            dx_ref, *sum_refs = rest
        else:
            f_ref, gate_ref, dx_ref, *sum_refs, df_ref = rest
            sums.append(_gated_residual_bwd(dx, f_ref, gate_ref, df_ref))
        dx_ref[...] = dx
        _accumulate_rows(i, list(zip(sum_refs, sums)))

    tile = pl.BlockSpec((ts, D), lambda i: (i, 0))
    row = pl.BlockSpec((1, D), lambda i: (0, 0))
    vec = jax.ShapeDtypeStruct((1, D), F32)
    fused = below is not None
    return pl.pallas_call(
        body, name=name,
        out_shape=(jax.ShapeDtypeStruct((S, D), F32), vec, vec, vec)
        + ((vec, jax.ShapeDtypeStruct((S, D), BF16)) if fused else ()),
        grid=(S // ts,),
        in_specs=[tile, tile, tile, row, row] + ([tile, row] if fused else []),
        out_specs=(tile, row, row, row) + ((row, tile) if fused else ()),
        compiler_params=_params(),
    )(x, dh, dres, g, scale, *(below if fused else ()))


def _ffn_in_fwd(h, w_t, name):
    S, D = h.shape
    F = w_t.shape[0] // 2
    tm, tn = _tile(S, TILES["ffn_fused"][0]), _tile(F, TILES["ffn_fused"][1])
    nj = F // tn

    def body(h_ref, wg_ref, wu_ref, o_ref):
        hv = h_ref[...]
        ug = lax.dot_general(hv, wg_ref[...], NT, preferred_element_type=F32)
        uu = lax.dot_general(hv, wu_ref[...], NT, preferred_element_type=F32)
        o_ref[...] = (ug * jax.nn.sigmoid(ug) * uu).astype(BF16)

    return pl.pallas_call(
        body, name=name, out_shape=jax.ShapeDtypeStruct((S, F), BF16), grid=(nj, S // tm),
        in_specs=[pl.BlockSpec((tm, D), lambda j, i: (i, 0)),
                  pl.BlockSpec((tn, D), lambda j, i: (j, 0)),
                  pl.BlockSpec((tn, D), lambda j, i: (j + nj, 0))],
        out_specs=pl.BlockSpec((tm, tn), lambda j, i: (i, j)),
        compiler_params=_params(),
    )(h, w_t, w_t)


def _ffn_mid_bwd(h, df, w_in_t, w_out, name):
    S, D = h.shape
    F = w_in_t.shape[0] // 2
    tm, tn = _tile(S, TILES["ffn_fused"][0]), _tile(F, TILES["ffn_fused"][1])
    nj = F // tn

    def body(h_ref, df_ref, wg_ref, wu_ref, wo_ref, dg_ref, du_ref):
        hv = h_ref[...]
        ug = lax.dot_general(hv, wg_ref[...], NT, preferred_element_type=F32)
        uu = lax.dot_general(hv, wu_ref[...], NT, preferred_element_type=F32)
        dact = lax.dot_general(df_ref[...], wo_ref[...], NT, preferred_element_type=F32)
        sig = jax.nn.sigmoid(ug)
        dg_ref[...] = (dact * uu * (sig * (1.0 + ug * (1.0 - sig)))).astype(BF16)
        du_ref[...] = (dact * (ug * sig)).astype(BF16)

    out = jax.ShapeDtypeStruct((S, F), BF16)
    return pl.pallas_call(
        body, name=name, out_shape=(out, out), grid=(nj, S // tm),
        in_specs=[pl.BlockSpec((tm, D), lambda j, i: (i, 0)),
                  pl.BlockSpec((tm, D), lambda j, i: (i, 0)),
                  pl.BlockSpec((tn, D), lambda j, i: (j, 0)),
                  pl.BlockSpec((tn, D), lambda j, i: (j + nj, 0)),
                  pl.BlockSpec((tn, D), lambda j, i: (j, 0))],
        out_specs=(pl.BlockSpec((tm, tn), lambda j, i: (i, j)), pl.BlockSpec((tm, tn), lambda j, i: (i, j))),
        compiler_params=_params(),
    )(h, df, w_in_t, w_in_t, w_out)


def _merge_fwd(o_a, o_b, o_c, gates, w_branch, w_out, resid, gate, name, *, tm=512):
    S, W = o_a.shape
    D = w_branch.shape[2]
    tm = _row_tile(S, tm)

    def body(oa_ref, ob_ref, oc_ref, g_ref, w_ref, wo_ref, r_ref, gm_ref, x_ref, m_ref, mix_ref):
        acc = None
        for k, o_ref in enumerate((oa_ref, ob_ref, oc_ref)):
            y = jnp.dot(o_ref[...], w_ref[k], preferred_element_type=F32)
            t = jax.nn.sigmoid(g_ref[:, k * D:(k + 1) * D]) * y
            acc = t if acc is None else acc + t
        merged = acc.astype(BF16)
        m_ref[...] = merged
        mix = jnp.dot(merged, wo_ref[...], preferred_element_type=F32)
        x_ref[...] = r_ref[...] + gm_ref[...] * mix
        mix_ref[...] = mix.astype(BF16)

    o_spec = pl.BlockSpec((tm, W), lambda i: (i, 0))
    tile = pl.BlockSpec((tm, D), lambda i: (i, 0))
    return pl.pallas_call(
        body, name=name,
        out_shape=(jax.ShapeDtypeStruct((S, D), F32), jax.ShapeDtypeStruct((S, D), BF16), jax.ShapeDtypeStruct((S, D), BF16)),
        grid=(S // tm,),
        in_specs=[o_spec, o_spec, o_spec, pl.BlockSpec((tm, 3 * D), lambda i: (i, 0)),
                  pl.BlockSpec((3, W, D), lambda i: (0, 0, 0)), pl.BlockSpec((D, D), lambda i: (0, 0)),
                  tile, pl.BlockSpec((1, D), lambda i: (0, 0))],
        out_specs=(tile, tile, tile),
        compiler_params=_params(),
    )(o_a, o_b, o_c, gates, w_branch, w_out, resid, gate)


def _merge_bwd(dmix, o_a, o_b, o_c, gates, w_branch, w_out, name, *, tm=256):
    S, W = o_a.shape
    D = w_branch.shape[2]
    tm = _row_tile(S, tm)
    n_heads = W // 64

    def body(dm_ref, oa_ref, ob_ref, oc_ref, g_ref, w_ref, wo_ref, dg_ref, dw_ref,
             doa_ref, dob_ref, doc_ref, dla_ref, dlb_ref, dlc_ref):
        first = pl.program_id(0) == 0
        head_of_column = (lax.broadcasted_iota(jnp.int32, (W, BLK), 0) // 64
                          == lax.broadcasted_iota(jnp.int32, (W, BLK), 1)).astype(F32)
        dm = lax.dot_general(dm_ref[...], wo_ref[...], NT, preferred_element_type=F32)
        branches = ((oa_ref, doa_ref, dla_ref), (ob_ref, dob_ref, dlb_ref), (oc_ref, doc_ref, dlc_ref))
        for k, (o_ref, do_ref, dl_ref) in enumerate(branches):
            wk = w_ref[k]
            ov = o_ref[...]
            y = jnp.dot(ov, wk, preferred_element_type=F32)
            g = jax.nn.sigmoid(g_ref[:, k * D:(k + 1) * D])
            dy = (dm * g).astype(BF16)
            dwk = lax.dot_general(ov, dy, TN, preferred_element_type=F32)

            @pl.when(first)
            def _(k=k, dwk=dwk):
                dw_ref[k] = dwk

            @pl.when(jnp.logical_not(first))
            def _(k=k, dwk=dwk):
                dw_ref[k] += dwk
            dg_ref[:, k * D:(k + 1) * D] = (dm * y * (g * (1.0 - g))).astype(BF16)
            do16 = lax.dot_general(dy, wk, NT, preferred_element_type=F32).astype(BF16)
            do_ref[...] = do16
            prod = do16.astype(F32) * ov.astype(F32)
            sums = jnp.dot(prod, head_of_column, preferred_element_type=F32, precision=HIGHEST)
            dl_ref[...] = jnp.transpose(sums)[:n_heads, :]

    o_spec = pl.BlockSpec((tm, W), lambda i: (i, 0))
    wide = pl.BlockSpec((tm, 3 * D), lambda i: (i, 0))
    dl_spec = pl.BlockSpec((n_heads, tm), lambda i: (0, i))
    o_out = jax.ShapeDtypeStruct((S, W), BF16)
    wide_out = jax.ShapeDtypeStruct((S, 3 * D), BF16)
    dl_out = jax.ShapeDtypeStruct((n_heads, S), F32)
    whole = pl.BlockSpec((3, W, D), lambda i: (0, 0, 0))
    return pl.pallas_call(
        body, name=name,
        out_shape=(wide_out, jax.ShapeDtypeStruct((3, W, D), F32), o_out, o_out, o_out, dl_out, dl_out, dl_out),
        grid=(S // tm,),
        in_specs=[pl.BlockSpec((tm, D), lambda i: (i, 0)), o_spec, o_spec, o_spec, wide, whole,
                  pl.BlockSpec((D, D), lambda i: (0, 0))],
        out_specs=(wide, whole, o_spec, o_spec, o_spec, dl_spec, dl_spec, dl_spec),
        compiler_params=_params(),
    )(dmix, o_a, o_b, o_c, gates, w_branch, w_out)


def _band_mask(variant, t_abs, s_abs):
    if variant == "b":
        return s_abs <= t_abs
    qc, kc = t_abs >> 6, s_abs >> 6
    return (kc <= qc) & (kc >= qc - (2 if variant == "a" else 8))


def _attn_fwd(variant, qkv, name, *, sinks=None, slopes=None, cq_col=None, ck_row=None, bias=None, after=None):
    S = qkv.shape[0]
    nb = S // BLK
    qb, kb, vb = ATTN_COLS[variant]
    shared_kv = variant == "a"
    win = ATTN_WINDOW.get(variant)
    per_step = ATTN_BLOCKS_PER_STEP[variant]

    def body(*refs):
        if after is not None:
            refs = refs[:-3] + refs[-2:]
        if variant == "a":
            q_ref, k_ref, v_ref, sink_ref, slope_ref, o_ref, lse_ref = refs
        elif variant == "b":
            q_ref, k_ref, v_ref, cq_ref, ck_ref, o_ref, lse_ref = refs
        else:
            q_ref, k_ref, v_ref, bias_ref, o_ref, lse_ref = refs
        p = pl.program_id(0)
        lane = lax.broadcasted_iota(jnp.int32, (1, BLK), 1)
        diagonal = lax.broadcasted_iota(jnp.int32, (BLK, BLK), 0) == lax.broadcasted_iota(jnp.int32, (BLK, BLK), 1)

        def compute(i, rows, start, n_keys):
            n_rows = rows.stop - rows.start
            t_abs = i * BLK + lax.broadcasted_iota(jnp.int32, (n_rows, 1), 0)
            q2 = q_ref[rows, :].astype(F32) * 0.125
            k_w = k_ref[pl.ds(start, n_keys), :]
            v_w = v_ref[pl.ds(start, n_keys), :]
            s_abs = start + lax.broadcasted_iota(jnp.int32, (1, n_keys), 1)
            valid = _band_mask(variant, t_abs, s_abs)
            outs = []
            for half in (0, 1):
                hmask = (lane >= 64) if half else (lane < 64)
                qh = jnp.where(hmask, q2, 0.0)
                if shared_kv:
                    swap = (p // 2) != half
                    qh = jnp.where(swap, pltpu.roll(qh, 64, 1), qh)
                s = lax.dot_general(qh.astype(BF16), k_w, NT, preferred_element_type=F32)
                if variant == "a":
                    head = 2 * p + half
                    s = s + (-slope_ref[head]) * jnp.abs(t_abs - s_abs).astype(F32)
                elif variant == "b":
                    s = s + cq_ref[rows, half:half + 1] - ck_ref[half:half + 1, pl.ds(start, n_keys)]
                else:
                    j0 = start // BLK
                    s = s + jnp.concatenate([jnp.concatenate(
                        [bias_ref[half, jnp.clip(i + r - j0 - b, 0, 4)] for b in range(n_keys // BLK)], axis=1)
                        for r in range(n_rows // BLK)], axis=0)
                s = jnp.where(valid, s, NEG_INF)
                m = jnp.max(s, axis=1, keepdims=True)
                if variant == "a":
                    m = jnp.maximum(m, sink_ref[head])
                pe = jnp.exp(s - m)
                l = jnp.sum(pe, axis=1, keepdims=True)
                if variant == "a":
                    l = l + jnp.exp(sink_ref[head] - m)
                out = jnp.dot(pe.astype(BF16), v_w, preferred_element_type=F32) / l
                if shared_kv:
                    out = jnp.where(swap, pltpu.roll(out, 64, 1), out)
                outs.append(out)
                lse = m + jnp.log(l)
                for b in range(n_rows // BLK):
                    part = jnp.where(diagonal, lse[b * BLK:(b + 1) * BLK, :], 0.0)
                    lse_ref[half:half + 1, rows.start + b * BLK:rows.start + (b + 1) * BLK] = jnp.sum(
                        part, axis=0, keepdims=True)
            o_ref[rows, :] = jnp.where(lane < 64, outs[0], outs[1]).astype(BF16)

        step = pl.program_id(1)
        if variant == "b":
            for g in range(S // GROUP):
                pl.when(step == g)(functools.partial(compute, step * per_step, slice(0, GROUP), 0, (g + 1) * GROUP))
        elif variant == "c":
            span = win + per_step - 1
            start = jnp.clip(step * per_step - (win - 1), 0, nb - span) * BLK
            compute(step * per_step, slice(0, per_step * BLK), pl.multiple_of(start, BLK), span * BLK)
        else:
            for sub in range(per_step):
                i = step * per_step + sub
                start = jnp.clip(i - (win - 1), 0, nb - win) * BLK
                compute(i, slice(sub * BLK, (sub + 1) * BLK), pl.multiple_of(start, BLK), win * BLK)

    tq = per_step * BLK
    kv_col = (lambda p, i: (0, kb)) if shared_kv else (lambda p, i: (0, kb + p))
    vv_col = (lambda p, i: (0, vb)) if shared_kv else (lambda p, i: (0, vb + p))
    in_specs = [pl.BlockSpec((tq, BLK), lambda p, i: (i, qb + p)),
                pl.BlockSpec((S, BLK), kv_col), pl.BlockSpec((S, BLK), vv_col)]
    args = [qkv, qkv, qkv]
    if variant == "a":
        in_specs += [pl.BlockSpec(memory_space=pltpu.SMEM), pl.BlockSpec(memory_space=pltpu.SMEM)]
        args += [sinks, slopes]
    elif variant == "b":
        in_specs += [pl.BlockSpec((None, tq, 2), lambda p, i: (p, i, 0)),
                     pl.BlockSpec((None, 2, S), lambda p, i: (p, 0, 0))]
        args += [cq_col, ck_row]
    else:
        in_specs += [pl.BlockSpec((2, 5, BLK, BLK), lambda p, i: (p, 0, 0, 0))]
        args += [bias]
    if after is not None:
        in_specs.append(ANY)
        args.append(after)
    return pl.pallas_call(
        body, name=name,
        out_shape=(jax.ShapeDtypeStruct((S, 512), BF16), jax.ShapeDtypeStruct((4, 2, S), F32)),
        grid=(4, nb // per_step), in_specs=in_specs,
        out_specs=(pl.BlockSpec((tq, BLK), lambda p, i: (i, p)),
                   pl.BlockSpec((None, 2, tq), lambda p, i: (p, 0, i))),
        compiler_params=_params(),
    )(*args)


def _attn_bwd(variant, qkv, do, lse_row, delta_row, name, *, sinks=None, slopes=None, cq_row=None,
              ck_col=None, bias_t=None):
    S = qkv.shape[0]
    nb = S // BLK
    qb, kb, vb = ATTN_COLS[variant]
    shared_kv = variant == "a"
    win = ATTN_WINDOW.get(variant)
    per_step = ATTN_BLOCKS_PER_STEP[variant]

    def body(*refs):
        *refs, dqt_ref = refs
        if variant == "a":
            (q_ref, k_ref, v_ref, do_ref, lse_ref, dl_ref, sink_ref, slope_ref,
             dq_ref, dk_ref, dv_ref, ex_ref) = refs
        elif variant == "b":
            (q_ref, k_ref, v_ref, do_ref, lse_ref, dl_ref, cq_ref, ck_ref,
             dq_ref, dk_ref, dv_ref, ex_ref, dcq_ref) = refs
        else:
            (q_ref, k_ref, v_ref, do_ref, lse_ref, dl_ref, bias_ref,
             dq_ref, dk_ref, dv_ref, ex_ref) = refs
        p = pl.program_id(0)
        lane = lax.broadcasted_iota(jnp.int32, (1, BLK), 1)
        hmasks = [(lane < 64), (lane >= 64)]
        swaps = [(p // 2) != half for half in (0, 1)] if shared_kv else None

        @pl.when(pl.program_id(1) == 0)
        def _():
            dqt_ref[...] = jnp.zeros_like(dqt_ref)
            if variant == "b":
                dcq_ref[...] = jnp.zeros_like(dcq_ref)
            else:
                ex_ref[...] = jnp.zeros_like(ex_ref)

        def to_kv_lanes(x, h):
            x = jnp.where(hmasks[h], x, 0.0)
            if shared_kv:
                x = jnp.where(swaps[h], pltpu.roll(x, 64, 1), x)
            return x

        def compute(j, rows, start, n_q):
            n_rows = rows.stop - rows.start
            s_abs = j * BLK + lax.broadcasted_iota(jnp.int32, (n_rows, 1), 0)
            off_k = pl.multiple_of(j * BLK, BLK)
            k2 = k_ref[rows, :].astype(F32)
            v2 = v_ref[rows, :].astype(F32)
            if shared_kv:
                kv_lane = (lane >> 6) == (p // 2)
                k_src, v_src = jnp.where(kv_lane, k2, 0.0), jnp.where(kv_lane, v2, 0.0)
                k_al = [jnp.where(swaps[h], pltpu.roll(k_src, 64, 1), k_src) for h in (0, 1)]
                v_al = [jnp.where(swaps[h], pltpu.roll(v_src, 64, 1), v_src) for h in (0, 1)]
            else:
                k_al = [jnp.where(hmasks[h], k2, 0.0) for h in (0, 1)]
                v_al = [jnp.where(hmasks[h], v2, 0.0) for h in (0, 1)]
            k_al = [(t * 0.125).astype(BF16) for t in k_al]
            v_al = [t.astype(BF16) for t in v_al]
            q_w = q_ref[pl.ds(start, n_q), :]
            do_w = do_ref[pl.ds(start, n_q), :]
            t_abs = start + lax.broadcasted_iota(jnp.int32, (1, n_q), 1)
            valid = _band_mask(variant, t_abs, s_abs)
            dk_acc = dv_acc = None
            ds_both = []
            for half in (0, 1):
                s = lax.dot_general(k_al[half], q_w, NT, preferred_element_type=F32)
                if variant == "a":
                    s = s + (-slope_ref[2 * p + half]) * jnp.abs(t_abs - s_abs).astype(F32)
                elif variant == "b":
                    s = s + cq_ref[half:half + 1, pl.ds(start, n_q)] - ck_ref[rows, half:half + 1]
                else:
                    i0 = start // BLK
                    s = s + jnp.concatenate([jnp.concatenate(
                        [bias_ref[half, jnp.clip(i0 + b - j - r, 0, 4)] for b in range(n_q // BLK)], axis=1)
                        for r in range(n_rows // BLK)], axis=0)
                pr = jnp.where(valid, jnp.exp(s - lse_ref[half:half + 1, pl.ds(start, n_q)]), 0.0)
                dp = lax.dot_general(v_al[half], do_w, NT, preferred_element_type=F32)
                ds = pr * (dp - dl_ref[half:half + 1, pl.ds(start, n_q)])
                ds16 = ds.astype(BF16)
                dv_h = to_kv_lanes(jnp.dot(pr.astype(BF16), do_w, preferred_element_type=F32), half)
                dk_h = to_kv_lanes(jnp.dot(ds16, q_w, preferred_element_type=F32) * 0.125, half)
                dv_acc = dv_h if dv_acc is None else dv_acc + dv_h
                dk_acc = dk_h if dk_acc is None else dk_acc + dk_h
                ds_both.append(ds16)
                if variant == "b":
                    ex_ref[rows, half:half + 1] = -jnp.sum(ds, axis=1, keepdims=True)
                    dcq_ref[half:half + 1, pl.ds(start, n_q)] += jnp.sum(ds, axis=0, keepdims=True)
                elif variant == "c":
                    for r in range(n_rows // BLK):
                        for b in range(n_q // BLK):
                            ex_ref[half, jnp.clip(i0 + b - j - r, 0, 4)] += ds[r * BLK:(r + 1) * BLK, b * BLK:(b + 1) * BLK]
            dq_t = lax.dot_general(jnp.concatenate(k_al, axis=0), jnp.concatenate(ds_both, axis=0), TN,
                                   preferred_element_type=F32)
            dqt_ref[:, pl.ds(start, n_q)] += dq_t
            if shared_kv:
                @pl.when(p == 0)
                def _():
                    dk_ref[pl.ds(off_k, n_rows), :] = dk_acc
                    dv_ref[pl.ds(off_k, n_rows), :] = dv_acc

                @pl.when(p > 0)
                def _():
                    dk_ref[pl.ds(off_k, n_rows), :] += dk_acc
                    dv_ref[pl.ds(off_k, n_rows), :] += dv_acc
            else:
                dk_ref[pl.ds(off_k, n_rows), :] = dk_acc.astype(dk_ref.dtype)
                dv_ref[pl.ds(off_k, n_rows), :] = dv_acc.astype(dv_ref.dtype)
            if variant == "a":
                for half in (0, 1):
                    p_sink = jnp.exp(sink_ref[2 * p + half] - lse_ref[half:half + 1, pl.ds(off_k, n_rows)])
                    term = p_sink * dl_ref[half:half + 1, pl.ds(off_k, n_rows)]
                    ex_ref[half:half + 1, :] += -jnp.sum(term, axis=1, keepdims=True)

        step = pl.program_id(1)
        if variant == "b":
            for g in range(S // GROUP):
                pl.when(step == g)(functools.partial(compute, step * per_step, slice(0, GROUP), g * GROUP, S - g * GROUP))
        elif variant == "c":
            span = win + per_step - 1
            start = jnp.clip(step * per_step, 0, nb - span) * BLK
            compute(step * per_step, slice(0, per_step * BLK), pl.multiple_of(start, BLK), span * BLK)
        else:
            for sub in range(per_step):
                j = step * per_step + sub
                start = jnp.clip(j, 0, nb - win) * BLK
                compute(j, slice(sub * BLK, (sub + 1) * BLK), pl.multiple_of(start, BLK), win * BLK)

        @pl.when(step == nb // per_step - 1)
        def _():
            dq_ref[...] = jnp.transpose(dqt_ref[...]).astype(BF16)

    tk = per_step * BLK
    col = lambda c0: (lambda p, j: (0, c0 + p))
    kv_blk = (lambda c0: (lambda p, j: (j, c0))) if shared_kv else (lambda c0: (lambda p, j: (j, c0 + p)))
    pair = lambda p, j: (0, p)
    row_stat = pl.BlockSpec((None, 2, S), lambda p, j: (p, 0, 0))
    in_specs = [pl.BlockSpec((S, BLK), col(qb)),
                pl.BlockSpec((tk, BLK), kv_blk(kb)), pl.BlockSpec((tk, BLK), kv_blk(vb)),
                pl.BlockSpec((S, BLK), pair), row_stat, row_stat]
    args = [qkv, qkv, qkv, do, lse_row, delta_row]
    kv_width = BLK if shared_kv else 512
    kv_out = pl.BlockSpec((S, BLK), (lambda p, j: (0, 0)) if shared_kv else pair)
    kv_dtype = F32 if shared_kv else BF16
    out_shape = [jax.ShapeDtypeStruct((S, 512), BF16), jax.ShapeDtypeStruct((S, kv_width), kv_dtype),
                 jax.ShapeDtypeStruct((S, kv_width), kv_dtype)]
    out_specs = [pl.BlockSpec((S, BLK), pair), kv_out, kv_out]
    if variant == "a":
        in_specs += [pl.BlockSpec(memory_space=pltpu.SMEM), pl.BlockSpec(memory_space=pltpu.SMEM)]
        args += [sinks, slopes]
        out_shape.append(jax.ShapeDtypeStruct((4, 8, BLK), F32))
        out_specs.append(pl.BlockSpec((None, 8, BLK), lambda p, j: (p, 0, 0)))
    elif variant == "b":
        in_specs += [row_stat, pl.BlockSpec((None, tk, 2), lambda p, j: (p, j, 0))]
        args += [cq_row, ck_col]
        out_shape += [jax.ShapeDtypeStruct((4, S, 2), F32), jax.ShapeDtypeStruct((4, 2, S), F32)]
        out_specs += [pl.BlockSpec((None, tk, 2), lambda p, j: (p, j, 0)), row_stat]
    else:
        in_specs += [pl.BlockSpec((2, 5, BLK, BLK), lambda p, j: (p, 0, 0, 0))]
        args += [bias_t]
        out_shape.append(jax.ShapeDtypeStruct((8, 5, BLK, BLK), F32))
        out_specs.append(pl.BlockSpec((2, 5, BLK, BLK), lambda p, j: (p, 0, 0, 0)))
    return pl.pallas_call(
        body, name=name, out_shape=tuple(out_shape), grid=(4, nb // per_step),
        in_specs=in_specs, out_specs=tuple(out_specs), scratch_shapes=[pltpu.VMEM((BLK, S), F32)],
        compiler_params=_params(),
    )(*args)


def _log_sigmoid(x):
    return jnp.minimum(x, 0.0) - jnp.log(1.0 + jnp.exp(-jnp.abs(x)))


def _forget_fwd(fb, b_forget, name):
    S = fb.shape[0]
    nb = S // GROUP

    def body(fb_ref, b_ref, cum_ref, carry_ref):
        i = pl.program_id(0)
        logf = _log_sigmoid(fb_ref[...] + b_ref[...])
        r = lax.broadcasted_iota(jnp.int32, (GROUP, GROUP), 0)
        c = lax.broadcasted_iota(jnp.int32, (GROUP, GROUP), 1)
        tri = (c <= r).astype(F32)

        @pl.when(i == 0)
        def _():
            carry_ref[...] = jnp.zeros_like(carry_ref)

        cum = jnp.dot(tri, logf, preferred_element_type=F32, precision=HIGHEST) + carry_ref[0:1, :]
        cum_ref[...] = cum
        carry_ref[...] = jnp.broadcast_to(cum[GROUP - 1:GROUP, :], carry_ref.shape)

    return pl.pallas_call(
        body, name=name, out_shape=jax.ShapeDtypeStruct((S, BLK), F32), grid=(nb,),
        in_specs=[pl.BlockSpec((GROUP, BLK), lambda i: (i, 0)), pl.BlockSpec((1, BLK), lambda i: (0, 0))],
        out_specs=pl.BlockSpec((GROUP, BLK), lambda i: (i, 0)),
        scratch_shapes=[pltpu.VMEM((8, BLK), F32)],
        compiler_params=_params(),
    )(fb, b_forget)


def _forget_bwd(dcum_q, dcum_k, fb, b_forget, name):
    S = fb.shape[0]
    nb = S // GROUP

    def body(dq_ref, dk_ref, fb_ref, b_ref, dfb_ref, db_ref, carry_ref):
        g = pl.program_id(0)
        r = lax.broadcasted_iota(jnp.int32, (GROUP, GROUP), 0)
        c = lax.broadcasted_iota(jnp.int32, (GROUP, GROUP), 1)
        tri = (c >= r).astype(F32)

        @pl.when(g == 0)
        def _():
            carry_ref[...] = jnp.zeros_like(carry_ref)

        dcum = dq_ref[...] + dk_ref[...]
        dlogf = jnp.dot(tri, dcum, preferred_element_type=F32, precision=HIGHEST) + carry_ref[0:1, :]
        carry_ref[...] = jnp.broadcast_to(dlogf[0:1, :], carry_ref.shape)
        x = fb_ref[...] + b_ref[...]
        lane = lax.broadcasted_iota(jnp.int32, (1, BLK), 1)
        dfb = jnp.where(lane < N_FORGET, dlogf * jax.nn.sigmoid(-x), 0.0)
        dfb_ref[...] = dfb
        db = jnp.sum(dfb, axis=0, keepdims=True)

        @pl.when(g == 0)
        def _():
            db_ref[...] = db

        @pl.when(g > 0)
        def _():
            db_ref[...] += db

    rev = pl.BlockSpec((GROUP, BLK), lambda g: (nb - 1 - g, 0))
    row = pl.BlockSpec((1, BLK), lambda g: (0, 0))
    return pl.pallas_call(
        body, name=name,
        out_shape=(jax.ShapeDtypeStruct((S, BLK), F32), jax.ShapeDtypeStruct((1, BLK), F32)), grid=(nb,),
        in_specs=[rev, rev, rev, row], out_specs=(rev, row),
        scratch_shapes=[pltpu.VMEM((8, BLK), F32)],
        compiler_params=_params(),
    )(dcum_q, dcum_k, fb, b_forget)


def _skew(x, sign):
    row = lax.broadcasted_iota(jnp.int32, x.shape, 0)
    for b in range(7):
        amount = (1 << b) if sign > 0 else 256 - (1 << b)
        x = jnp.where(((row >> b) & 1) == 1, pltpu.roll(x, amount, 1), x)
    return x


def _rel_bases(rel):
    far = rel[:, 256:257]
    far127 = jnp.broadcast_to(far, (rel.shape[0], 127))
    base0 = jnp.concatenate([rel[:, 128:0:-1], far, rel[:, 255:128:-1]], axis=1)
    base1 = jnp.concatenate([rel[:, 256:128:-1], far, far127], axis=1)
    base0_t = jnp.concatenate([rel[:, 128:256], far, rel[:, 1:128]], axis=1)
    base1_t = jnp.concatenate([jnp.broadcast_to(far, (rel.shape[0], 128)), far, rel[:, 129:256]], axis=1)
    return jnp.stack([base0, base1, base0_t, base1_t], axis=1)


def _rel_expand(bases, name):
    def body(b_ref, t_ref, tt_ref):
        far = jnp.broadcast_to(b_ref[1:2, 0:1], (BLK, BLK))
        for k, out_ref in ((0, t_ref), (2, tt_ref)):
            for d in (0, 1):
                x = jnp.broadcast_to(b_ref[k + d:k + d + 1, :], (BLK, 2 * BLK))
                out_ref[d] = _skew(x, 1)[:, :BLK]
            for d in (2, 3, 4):
                out_ref[d] = far

    out = jax.ShapeDtypeStruct((8, 5, BLK, BLK), F32)
    spec = pl.BlockSpec((None, 5, BLK, BLK), lambda h: (h, 0, 0, 0))
    return pl.pallas_call(
        body, name=name, out_shape=(out, out), grid=(8,),
        in_specs=[pl.BlockSpec((None, 4, 2 * BLK), lambda h: (h, 0, 0))], out_specs=(spec, spec),
        compiler_params=_params(),
    )(bases)


def _rel_reduce(dtiles_t, name):
    def body(dt_ref, o_ref):
        zeros = jnp.zeros((BLK, BLK), F32)
        sums = []
        for d in (0, 1):
            x = _skew(jnp.concatenate([dt_ref[d], zeros], axis=1), -1)
            sums.append(jnp.broadcast_to(jnp.sum(x, axis=0, keepdims=True), (8, 2 * BLK)))
        lane = lax.broadcasted_iota(jnp.int32, (8, 2 * BLK), 1)
        main = pltpu.roll(sums[0], BLK, 1) + jnp.where(lane > BLK, sums[1], 0.0)
        far = jnp.sum(jnp.where(lane < BLK, sums[1], 0.0)[0:1], axis=1, keepdims=True)
        far = far + jnp.sum(jnp.sum(dt_ref[2] + dt_ref[3] + dt_ref[4], axis=0, keepdims=True), axis=1, keepdims=True)
        o_ref[...] = jnp.concatenate([main[0:1], jnp.broadcast_to(far, (1, BLK))], axis=1)

    return pl.pallas_call(
        body, name=name, out_shape=jax.ShapeDtypeStruct((8, 1, 3 * BLK), F32), grid=(8,),
        in_specs=[pl.BlockSpec((None, 5, BLK, BLK), lambda h: (h, 0, 0, 0))],
        out_specs=pl.BlockSpec((None, 1, 3 * BLK), lambda h: (h, 0, 0)),
        compiler_params=_params(),
    )(dtiles_t)


def _final_loss(x, target, g, below, name):
    S, D = x.shape
    ts = _row_tile(S, ROW_TILE)

    def body(x_ref, t_ref, g_ref, f_ref, gate_ref, dx_ref, loss_ref, dg_ref, dgate_ref, df_ref):
        i = pl.program_id(0)
        xv, gv = x_ref[...], g_ref[...]
        rstd = lax.rsqrt(jnp.mean(xv * xv, axis=-1, keepdims=True) + EPS)
        xhat = xv * rstd
        err = xhat * gv - t_ref[...]
        part = 0.5 * jnp.sum(jnp.mean(err * err, axis=-1, keepdims=True), axis=0, keepdims=True)
        dy = err / D
        dg = jnp.sum(dy * xhat, axis=0, keepdims=True)
        dxhat = dy * gv
        proj = jnp.mean(dxhat * xhat, axis=-1, keepdims=True)
        dx = rstd * (dxhat - xhat * proj)
        dx_ref[...] = dx
        dgate = _gated_residual_bwd(dx, f_ref, gate_ref, df_ref)
        _accumulate_rows(i, [(loss_ref, jnp.broadcast_to(part, loss_ref.shape)), (dg_ref, dg), (dgate_ref, dgate)])

    tile = pl.BlockSpec((ts, D), lambda i: (i, 0))
    row = pl.BlockSpec((1, D), lambda i: (0, 0))
    vec = jax.ShapeDtypeStruct((1, D), F32)
    return pl.pallas_call(
        body, name=name,
        out_shape=(jax.ShapeDtypeStruct((S, D), F32), jax.ShapeDtypeStruct((8, 128), F32), vec, vec,
                   jax.ShapeDtypeStruct((S, D), BF16)),
        grid=(S // ts,), in_specs=[tile, tile, row, tile, row],
        out_specs=(tile, pl.BlockSpec((8, 128), lambda i: (0, 0)), row, row, tile),
        compiler_params=_params(),
    )(x, target, g, *below)


def _ada_fwd(c_all, w_ada, name):
    L, D, E = w_ada.shape

    def body(c_ref, w_ref, o_ref):
        cv = c_ref[...]
        cond = cv * jax.nn.sigmoid(cv)
        o_ref[...] = jnp.dot(cond, w_ref[...], preferred_element_type=F32, precision=HIGHEST)

    return pl.pallas_call(
        body, name=name, out_shape=jax.ShapeDtypeStruct((L, N_DEV, E), F32), grid=(L,),
        in_specs=[pl.BlockSpec((N_DEV, D), lambda l: (0, 0)), pl.BlockSpec((None, D, E), lambda l: (l, 0, 0))],
        out_specs=pl.BlockSpec((None, N_DEV, E), lambda l: (l, 0, 0)),
        compiler_params=_params(),
    )(c_all, w_ada)


def _ada_bwd(c_all_t, dmod, name):
    D = c_all_t.shape[0]
    L, _, E = dmod.shape

    def body(c_ref, d_ref, o_ref):
        cv = c_ref[...]
        cond = cv * jax.nn.sigmoid(cv)
        acc = None
        for b in range(N_DEV):
            t = cond[:, b:b + 1] * d_ref[b:b + 1, :]
            acc = t if acc is None else acc + t
        o_ref[...] = acc

    return pl.pallas_call(
        body, name=name, out_shape=jax.ShapeDtypeStruct((L, D, E), F32), grid=(L,),
        in_specs=[pl.BlockSpec((D, N_DEV), lambda l: (0, 0)), pl.BlockSpec((None, N_DEV, E), lambda l: (l, 0, 0))],
        out_specs=pl.BlockSpec((None, D, E), lambda l: (l, 0, 0)),
        compiler_params=_params(),
    )(c_all_t, dmod)


def _adamw(w, m, v, g_parts, name, me, after=None):
    L, R, C = w.shape
    tr = _row_tile(R, max(8, (256 * 1024 // max(C, 128)) // 8 * 8))
    nr = R // tr
    c1 = 1.0 - ADAM_B1 ** ADAM_STEP
    c2 = 1.0 - ADAM_B2 ** ADAM_STEP
    direct = [isinstance(p, tuple) for p in g_parts]
    n_in = sum(2 if d else 1 for d in direct)

    def body(me_ref, w_ref, m_ref, v_ref, *rest):
        g_refs, (go_ref, d_ref, mo_ref, vo_ref) = list(rest[:n_in]), rest[-4:]
        layer = pl.program_id(0)
        g = None
        for l in range(L):
            land_ref = g_refs.pop(0)
            own = g_refs.pop(0)[...].astype(F32) if direct[l] else None
            gl = None
            for k in range(land_ref.shape[0]):
                part = land_ref[k].astype(F32)
                if direct[l]:
                    part = jnp.where(me_ref[l] == k, own, part)
                gl = part if gl is None else gl + part
            g = gl if g is None else jnp.where(layer == l, gl, g)
        mn = ADAM_B1 * m_ref[...] + (1.0 - ADAM_B1) * g
        vn = ADAM_B2 * v_ref[...] + (1.0 - ADAM_B2) * (g * g)
        m_hat = mn / c1
        v_hat = vn / c2
        go_ref[...] = g
        d_ref[...] = -ADAM_LR * (m_hat / (jnp.sqrt(v_hat) + ADAM_EPS) + ADAM_WD * w_ref[...])
        mo_ref[...] = mn
        vo_ref[...] = vn

    def rows(l, layer, i):
        return jnp.where(layer == l, i, 0 if l > 0 else nr - 1)

    in_specs, operands = [], []
    for l, p in enumerate(g_parts):
        land, sent = p if direct[l] else (p, None)
        in_specs.append(pl.BlockSpec((land.shape[0], tr, C), lambda layer, i, me_ref, l=l: (0, rows(l, layer, i), 0)))
        operands.append(land)
        if direct[l]:
            in_specs.append(pl.BlockSpec((None, tr, C), lambda layer, i, me_ref, l=l: (me_ref[l], rows(l, layer, i), 0)))
            operands.append(sent)
    if after is not None:
        in_specs.append(ANY)
        operands.append(after)
    tile = pl.BlockSpec((None, tr, C), lambda layer, i, me_ref: (layer, i, 0))
    out = jax.ShapeDtypeStruct((L, R, C), F32)
    return pl.pallas_call(
        body, name=name, out_shape=(out, out, out, out),
        grid_spec=pltpu.PrefetchScalarGridSpec(
            num_scalar_prefetch=1, grid=(L, nr), in_specs=[tile, tile, tile] + in_specs,
            out_specs=(tile, tile, tile, tile)),
        compiler_params=_params(),
    )(me, w, m, v, *operands)


def _pair_add(pieces, recv, core, name):
    _, _, R, C = pieces.shape
    tr = _row_tile(R, max(8, (512 * 1024 // max(C, 128)) // 8 * 8))

    def body(core_ref, a_ref, b_ref, o_ref):
        o_ref[...] = (a_ref[...].astype(F32) + b_ref[...].astype(F32)).astype(BF16)

    return pl.pallas_call(
        body, name=name, out_shape=jax.ShapeDtypeStruct((4, R, C), BF16),
        grid_spec=pltpu.PrefetchScalarGridSpec(
            num_scalar_prefetch=1, grid=(4, R // tr),
            in_specs=[pl.BlockSpec((None, None, tr, C), lambda k, i, core_ref: (core_ref[0], k, i, 0)),
                      pl.BlockSpec((None, tr, C), lambda k, i, core_ref: (k, i, 0))],
            out_specs=pl.BlockSpec((None, tr, C), lambda k, i, core_ref: (k, i, 0))),
        compiler_params=_params(),
    )(core, pieces, recv)


MESH = pl.DeviceIdType.MESH
ANY = pl.BlockSpec(memory_space=pl.ANY)


def _position():
    return lax.axis_index("x"), lax.axis_index("y"), lax.axis_index("c")


def _small_all_gather(v, name):
    m_per, n = v.shape

    def body(x_ref, out_ref, send_sems, recv_sems, local_sem):
        x, y, c = _position()
        me, sibling = (x, y, c), (x, y, 1 - c)
        chips = [(1 - x, y), (x, 1 - y), (1 - x, 1 - y)]

        def rows(px, py, pc):
            return out_ref.at[pl.ds((4 * px + 2 * py + pc) * m_per, m_per), :]

        def copy(k, block, to, src=None):
            return pltpu.make_async_remote_copy(
                src_ref=rows(*block) if src is None else src, dst_ref=rows(*block),
                send_sem=send_sems.at[k], recv_sem=recv_sems.at[k], device_id=to, device_id_type=MESH)

        mine = pltpu.make_async_copy(x_ref, rows(*me), local_sem)
        mine.start()
        first = [copy(0, me, sibling, src=x_ref)]
        first += [copy(1 + j, me, (*chip, c), src=x_ref) for j, chip in enumerate(chips)]
        for cp in first:
            cp.start()
        passed = [copy(4 + j, (*chip, c), sibling) for j, chip in enumerate(chips)]
        for j, chip in enumerate(chips):
            copy(1 + j, (*chip, c), me).wait_recv()
            passed[j].start()
        copy(0, sibling, me).wait_recv()
        for j, chip in enumerate(chips):
            copy(4 + j, (*chip, 1 - c), me).wait_recv()
        for cp in first + passed:
            cp.wait_send()
        mine.wait()

    return pl.pallas_call(
        body, name=name, out_shape=jax.ShapeDtypeStruct((N_DEV * m_per, n), v.dtype),
        in_specs=[pl.BlockSpec(memory_space=pltpu.VMEM)], out_specs=pl.BlockSpec(memory_space=pltpu.VMEM),
        scratch_shapes=[pltpu.SemaphoreType.DMA((7,)), pltpu.SemaphoreType.DMA((7,)), pltpu.SemaphoreType.DMA],
    )(v)


def _sibling_exchange(pieces, name):
    n_arr = len(pieces)

    def body(*refs):
        p_refs, out_refs = refs[:n_arr], refs[n_arr:2 * n_arr]
        send_sems, recv_sems = refs[2 * n_arr:]
        x, y, c = _position()
        copies = [pltpu.make_async_remote_copy(
            src_ref=p_refs[a].at[1 - c], dst_ref=out_refs[a], send_sem=send_sems.at[a], recv_sem=recv_sems.at[a],
            device_id=(x, y, 1 - c), device_id_type=MESH) for a in range(n_arr)]
        for cp in copies:
            cp.start()
        for cp in copies:
            cp.wait()

    return pl.pallas_call(
        body, name=name,
        out_shape=tuple(jax.ShapeDtypeStruct(p.shape[1:], p.dtype) for p in pieces),
        in_specs=[ANY] * n_arr, out_specs=tuple([ANY] * n_arr),
        scratch_shapes=[pltpu.SemaphoreType.DMA((n_arr,)), pltpu.SemaphoreType.DMA((n_arr,))],
    )(*pieces)


HBM = pl.BlockSpec(memory_space=pltpu.HBM)
SEM = pl.BlockSpec(memory_space=pltpu.SEMAPHORE)
EFFECT = pltpu.SideEffectType.DATAFLOW_SIDE_EFFECTING
RELATIONS = [(rx, ry, rc) for rx in (0, 1) for ry in (0, 1) for rc in (0, 1)][1:]


SAME_CORE = [r for r in RELATIONS if r == (0, 0, 1) or r[2] == 0]


CHIPS = [r for r in RELATIONS if r[2] == 0]


def _exchange_copies(src_refs, land_refs, send_sems, recv_sems, scatter, receive_side, relations):
    x, y, c = _position()
    index = (lambda px, py, pc: 2 * px + py) if relations == CHIPS else (lambda px, py, pc: 4 * px + 2 * py + pc)
    me = index(x, y, c)
    copies = []
    for k, (rx, ry, rc) in enumerate(relations):
        peer = ((1 - x) if rx else x, (1 - y) if ry else y, (1 - c) if rc else c)
        peer_index = index(*peer)
        for a, (src, land) in enumerate(zip(src_refs, land_refs)):
            copies.append(pltpu.make_async_remote_copy(
                src_ref=src.at[peer_index] if scatter else src,
                dst_ref=land.at[peer_index if receive_side else me],
                send_sem=send_sems.at[a * len(relations) + k], recv_sem=recv_sems.at[a * len(relations) + k],
                device_id=peer, device_id_type=MESH))
    return copies


def _exchange_start(srcs, scatter, after, name, relations=RELATIONS):
    n = len(srcs)
    land_shapes = [(s.shape if scatter else (N_DEV,) + s.shape) for s in srcs]

    def body(*refs):
        src_refs, land_refs = refs[:n], refs[n:2 * n]
        send_sems, recv_sems = refs[2 * n + 1], refs[2 * n + 2]
        token = refs[-1]
        for cp in _exchange_copies(src_refs, land_refs, send_sems, recv_sems, scatter, False, relations):
            cp.start()
        token[...] = jnp.zeros_like(token)

    sems = pltpu.SemaphoreType.DMA((n * len(relations),))
    outs = pl.pallas_call(
        body, name=name,
        out_shape=(sems, sems, *[pltpu.HBM(s.shape, s.dtype) for s in srcs],
                   *[pltpu.HBM(shape, s.dtype) for shape, s in zip(land_shapes, srcs)],
                   jax.ShapeDtypeStruct((8, 128), F32)),
        in_specs=[HBM] * (2 * n) + [ANY],
        out_specs=(SEM, SEM, *[HBM] * (2 * n), pl.BlockSpec(memory_space=pltpu.VMEM)),
        input_output_aliases={a: 2 + a for a in range(2 * n)},
        compiler_params=pltpu.CompilerParams(has_side_effects=EFFECT),
    )(*[pltpu.with_memory_space_constraint(s, pltpu.HBM) for s in srcs],
      *[pltpu.with_memory_space_constraint(lax.empty(shape, s.dtype), pltpu.HBM)
        for shape, s in zip(land_shapes, srcs)], after)
    return outs[0], outs[1], outs[2:2 + n], outs[2 + n:2 + 2 * n], outs[-1]


def _exchange_wait(started, scatter, after, name, relations=RELATIONS):
    send_sems, recv_sems, srcs, lands, _ = started
    n = len(srcs)

    def body(*refs):
        src_refs, land_refs = refs[:n], refs[n:2 * n]
        send_sems, recv_sems = refs[2 * n], refs[2 * n + 1]
        copies = _exchange_copies(src_refs, land_refs, send_sems, recv_sems, scatter, True, relations)
        for cp in copies:
            cp.wait_send()
        for cp in copies:
            cp.wait_recv()

    outs = pl.pallas_call(
        body, name=name,
        out_shape=(*[pltpu.HBM(s.shape, s.dtype) for s in srcs], *[pltpu.HBM(t.shape, t.dtype) for t in lands]),
        in_specs=[HBM] * (2 * n) + [SEM, SEM, ANY], out_specs=tuple([HBM] * (2 * n)),
        input_output_aliases={a: a for a in range(2 * n)},
        compiler_params=pltpu.CompilerParams(has_side_effects=EFFECT),
    )(*srcs, *lands, send_sems, recv_sems, after)
    return outs[:n], outs[n:]


def _forward_copies(land_refs, send_sems, recv_sems, receive_side):
    x, y, c = _position()
    copies = []
    for j, (px, py) in enumerate([(1 - x, y), (x, 1 - y), (1 - x, 1 - y)]):
        held, coming = 4 * px + 2 * py + c, 4 * px + 2 * py + (1 - c)
        for a, land in enumerate(land_refs):
            copies.append(pltpu.make_async_remote_copy(
                src_ref=land.at[held], dst_ref=land.at[coming if receive_side else held],
                send_sem=send_sems.at[3 * a + j], recv_sem=recv_sems.at[3 * a + j],
                device_id=(x, y, 1 - c), device_id_type=MESH))
    return copies


def _forward_start(lands, after, name):
    n = len(lands)

    def body(*refs):
        send_sems, recv_sems, token = refs[n + 1], refs[n + 2], refs[-1]
        for cp in _forward_copies(refs[:n], send_sems, recv_sems, False):
            cp.start()
        token[...] = jnp.zeros_like(token)

    sems = pltpu.SemaphoreType.DMA((3 * n,))
    outs = pl.pallas_call(
        body, name=name,
        out_shape=(sems, sems, *[pltpu.HBM(t.shape, t.dtype) for t in lands], jax.ShapeDtypeStruct((8, 128), F32)),
        in_specs=[HBM] * n + [ANY], out_specs=(SEM, SEM, *[HBM] * n, pl.BlockSpec(memory_space=pltpu.VMEM)),
        input_output_aliases={a: 2 + a for a in range(n)},
        compiler_params=pltpu.CompilerParams(has_side_effects=EFFECT),
    )(*lands, after)
    return outs[0], outs[1], outs[2:2 + n], outs[-1]


def _forward_wait(started, after, name):
    send_sems, recv_sems, lands, _ = started
    n = len(lands)

    def body(*refs):
        copies = _forward_copies(refs[:n], refs[n], refs[n + 1], True)
        for cp in copies:
            cp.wait_send()
        for cp in copies:
            cp.wait_recv()

    return pl.pallas_call(
        body, name=name, out_shape=tuple(pltpu.HBM(t.shape, t.dtype) for t in lands),
        in_specs=[HBM] * n + [SEM, SEM, ANY], out_specs=tuple([HBM] * n),
        input_output_aliases={a: a for a in range(n)},
        compiler_params=pltpu.CompilerParams(has_side_effects=EFFECT),
    )(*lands, send_sems, recv_sems, after)


def _place_own(lands, mine, me, name):
    n = len(lands)
    flat = [m.reshape(-1, m.shape[-1]) for m in mine]
    flat_lands = [t.reshape(N_DEV, -1, t.shape[-1]) for t in lands]

    def body(me_ref, *refs):
        for src, dst in zip(refs[:n], refs[2 * n:]):
            dst[...] = src[...]

    in_specs = [pl.BlockSpec((m.shape[0] // 2, m.shape[1]), lambda i, me_ref: (i, 0)) for m in flat]
    out_specs = [pl.BlockSpec((None, m.shape[0] // 2, m.shape[1]), lambda i, me_ref: (me_ref[0], i, 0)) for m in flat]
    outs = pl.pallas_call(
        body, name=name, out_shape=tuple(jax.ShapeDtypeStruct(t.shape, t.dtype) for t in flat_lands),
        grid_spec=pltpu.PrefetchScalarGridSpec(
            num_scalar_prefetch=1, grid=(2,), in_specs=in_specs + [ANY] * n, out_specs=tuple(out_specs)),
        input_output_aliases={1 + n + a: a for a in range(n)},
        compiler_params=_params(),
    )(me, *flat, *flat_lands)
    return [o.reshape(t.shape) for o, t in zip(outs, lands)]


W_IN_SHARD = N_IN // N_DEV
F_SHARD = F_COL // W_IN_SHARD
F_LO = F_COL - F_SHARD * W_IN_SHARD


def _w_ffn_in_view(w):
    return jnp.transpose(w, (0, 2, 1))


def _w_in_segments():
    segments = []
    for d in range(N_DEV):
        if d == F_SHARD:
            segments += [(d, 0, d * W_IN_SHARD, F_LO), (d, F_LO, N_MAIN, N_FORGET),
                         (d, F_LO + N_FORGET, F_COL, W_IN_SHARD - F_LO - N_FORGET)]
        else:
            segments.append((d, 0, d * W_IN_SHARD - (N_FORGET if d > F_SHARD else 0), W_IN_SHARD))
    return segments


def _w_in_rearranged(g, name):
    D = g.shape[1]
    tr = _row_tile(D, 256)

    def body(g_ref, o_ref):
        for d, lo, at, width in _w_in_segments():
            o_ref[:, at:at + width] = g_ref[d, :, lo:lo + width]
        o_ref[:, N_IN:] = jnp.zeros((tr, BLK - N_FORGET), o_ref.dtype)

    return pl.pallas_call(
        body, name=name, out_shape=jax.ShapeDtypeStruct((D, N_MAIN + BLK), g.dtype), grid=(D // tr,),
        in_specs=[pl.BlockSpec((N_DEV, tr, W_IN_SHARD), lambda i: (0, i, 0))],
        out_specs=pl.BlockSpec((tr, N_MAIN + BLK), lambda i: (i, 0)),
        compiler_params=_params(),
    )(g)


def _w_in_pieces(dw_r, name, pair_major=False):
    D = dw_r.shape[0]
    tr = _row_tile(D, 256)
    lead = (2, 4) if pair_major else (N_DEV,)

    def body(x_ref, o_ref):
        for d, lo, at, width in _w_in_segments():
            slot = (d % 2, d // 2) if pair_major else (d,)
            o_ref[(*slot, slice(None), slice(lo, lo + width))] = x_ref[:, at:at + width]

    return pl.pallas_call(
        body, name=name, out_shape=jax.ShapeDtypeStruct((*lead, D, W_IN_SHARD), dw_r.dtype), grid=(D // tr,),
        in_specs=[pl.BlockSpec((tr, N_MAIN + BLK), lambda i: (i, 0))],
        out_specs=pl.BlockSpec((*lead, tr, W_IN_SHARD), lambda i: (*[0] * len(lead), i, 0)),
        compiler_params=_params(),
    )(dw_r)


def _row_pieces(dw):
    return dw.reshape(N_DEV, dw.shape[0] // N_DEV, dw.shape[1])


def _branch_pieces(dw):
    k, w, d = dw.shape
    return jnp.transpose(dw.reshape(k, w, N_DEV, d // N_DEV), (2, 0, 1, 3)).reshape(N_DEV, k * w, d // N_DEV)


def _pairs_col(a):
    return jnp.transpose(a.reshape(a.shape[0], 4, 2), (1, 0, 2))


def _pairs_row(a):
    return jnp.transpose(a.reshape(a.shape[0], 4, 2), (1, 2, 0))


def _heads_from_col(a):
    return jnp.transpose(a, (1, 0, 2)).reshape(a.shape[1], 8)


def _heads_from_row(a):
    return jnp.transpose(a, (2, 0, 1)).reshape(a.shape[2], 8)


def _pad_lanes(a, n):
    return jnp.pad(a, [(0, 0)] * (a.ndim - 1) + [(0, n - a.shape[-1])])


SMALL_SEGMENTS = (("dmod", 2 * 6 * D_MODEL), ("norm_mix_g", 2 * D_MODEL), ("norm_ffn_g", 2 * D_MODEL),
                  ("final_norm_g", D_MODEL), ("b_forget", 128), ("sinks", 128), ("rel_bias", 4224), ("loss", 128))
SMALL_ROWS = 176


def _pack_small(parts):
    flat = [_pad_lanes(parts[name].reshape(1, -1), size) for name, size in SMALL_SEGMENTS]
    total = sum(size for _, size in SMALL_SEGMENTS)
    flat.append(jnp.zeros((1, SMALL_ROWS * 128 - total), F32))
    return jnp.concatenate(flat, axis=1).reshape(SMALL_ROWS, 128)


def _unpack_small(packed, shapes):
    flat = packed.reshape(-1)
    out, pos = {}, 0
    for name, size in SMALL_SEGMENTS:
        shape = shapes[name]
        count = 1
        for d in shape:
            count *= d
        out[name] = flat[pos:pos + count].reshape(shape)
        pos += size
    return out


def kernel(x, c, norm_mix_g, norm_ffn_g, w_ada, b_ada, w_in, b_forget, sinks, rel_bias, w_branch, w_out, w_ffn_in, w_ffn_out, final_norm_g, loss_target, m_norm_mix_g, m_norm_ffn_g, m_w_ada, m_b_ada, m_w_in, m_b_forget, m_sinks, m_rel_bias, m_w_branch, m_w_out, m_w_ffn_in, m_w_ffn_out, m_final_norm_g, v_norm_mix_g, v_norm_ffn_g, v_w_ada, v_b_ada, v_w_in, v_b_forget, v_sinks, v_rel_bias, v_w_branch, v_w_out, v_w_ffn_in, v_w_ffn_out, v_final_norm_g):
    depth = w_in.shape[0]
    S, D = x.shape[1], x.shape[2]
    assert S % GROUP == 0 and S >= ATTN_WINDOW["c"] * BLK
    px, py, pc = _position()
    me = 4 * px + 2 * py + pc
    x0 = x[0]

    assert depth == 2
    big_weights = (w_in, w_branch, w_out, w_ffn_in, w_ffn_out)
    me_arr = jnp.stack([me, me]).astype(jnp.int32)
    me_in_arr = jnp.stack([2 * px + py, me]).astype(jnp.int32)

    def rest_matrices(g_branch, g_out, g_fin, g_fout):
        return (jnp.transpose(g_branch, (1, 2, 0, 3)).reshape(3, 512, D), g_out.reshape(D, D),
                g_fin.reshape(2 * FFN_HIDDEN, D), g_fout.reshape(FFN_HIDDEN, D))

    def arrive(started, after, name):
        mine, landed = _exchange_wait(started, False, after, f"{name}_wait", SAME_CORE)
        return mine, _forward_start(landed, mine[0], f"{name}_forward_start")

    def finish_gather(arrived, after, name):
        mine, forward = arrived
        landed = _forward_wait(forward, after, f"{name}_forward_wait")
        return _place_own(landed, mine, me.astype(jnp.int32).reshape(1), f"{name}_own")

    w_fin_t = _w_ffn_in_view(w_ffn_in)
    shards = [[t.astype(BF16) for t in (w_in[l], w_branch[l], w_out[l], w_fin_t[l], w_ffn_out[l])]
              for l in range(depth)]
    c_all = _small_all_gather(c.reshape(8, 128), "comm_gather_c").reshape(N_DEV, D)
    mod_cols = _ada_fwd(c_all, w_ada, "ada_fwd")
    mod_all = _small_all_gather(mod_cols.reshape(-1, 128), "comm_gather_mod")
    gather_in0 = _exchange_start(shards[0][:1], False, mod_all, "comm_gather_w_in0_start", SAME_CORE)
    gather_rest0 = _exchange_start(shards[0][1:], False, gather_in0[4], "comm_gather_rest0_start", SAME_CORE)
    gather1 = _exchange_start(shards[1], False, gather_rest0[4], "comm_gather_weights1_start", SAME_CORE)
    started = gather1[4][0:1, 0:1]
    W_in, W_branch, W_out, W_fin, W_fout = ([None, None] for _ in range(5))
    mod_all = mod_all.reshape(N_DEV, depth, N_DEV, w_ada.shape[2])
    mod_mine = lax.dynamic_index_in_dim(mod_all, me, axis=2, keepdims=False)
    mod = jnp.transpose(mod_mine, (1, 0, 2)).reshape(depth, 6 * D) + b_ada + started
    mods = [[mod[l:l + 1, k * D:(k + 1) * D] for k in range(6)] for l in range(depth)]
    rel_tiles = [_rel_expand(_rel_bases(rel_bias[l]) + started, f"rel_expand{l}") for l in range(depth)]

    slopes = jnp.exp2(-jnp.arange(1, 9, dtype=F32))
    saved = []
    xs = x0
    for l in range(depth):
        if l == 1:
            g_in1, *g_rest1 = finish_gather(arrived1, xs, "comm_gather_weights1")
            W_in[1] = _w_in_rearranged(g_in1, "w_in_rearrange1")
            W_branch[1], W_out[1], W_fin[1], W_fout[1] = rest_matrices(*g_rest1)
        sh_m, sc_m, g_m, sh_f, sc_f, g_f = mods[l]
        gm, gf = norm_mix_g[l:l + 1], norm_ffn_g[l:l + 1]
        bfor = _pad_lanes(b_forget[l:l + 1], BLK)
        h = _norm_mod_fwd(xs, gm, sh_m, sc_m, f"norm_mix_fwd{l}")
        tiles, tiles_t = rel_tiles[l]
        if l == 0:
            arrived_in0 = arrive(gather_in0, rel_tiles[-1][1], "comm_gather_w_in0")
            W_in[0] = _w_in_rearranged(finish_gather(arrived_in0, h, "comm_gather_w_in0")[0], "w_in_rearrange0")
        qkv, gates = _project(h, W_in[l], f"proj{l}")
        fb = _matmul(h, W_in[l], "nn", F32, f"proj_forget{l}", TILES["proj_forget"], n=BLK, b_off=N_MAIN // BLK)
        cum = _forget_fwd(fb, bfor, f"forget_fwd{l}")[:, :N_FORGET]
        cum_col, cum_row = _pairs_col(cum), _pairs_row(cum)
        o_a, lse_a = _attn_fwd("a", qkv, f"attn_a_fwd{l}", sinks=sinks[l], slopes=slopes)
        o_b, lse_b = _attn_fwd("b", qkv, f"attn_b_fwd{l}", cq_col=cum_col, ck_row=cum_row)
        arrived_rest0 = arrive(gather_rest0, o_b, "comm_gather_rest0") if l == 0 else None
        o_c, lse_c = _attn_fwd("c", qkv, f"attn_c_fwd{l}", bias=tiles, after=arrived_rest0[1][3] if l == 0 else None)
        if l == 0:
            W_branch[0], W_out[0], W_fin[0], W_fout[0] = rest_matrices(
                *finish_gather(arrived_rest0, o_c, "comm_gather_rest0"))
        x1, merged, mix = _merge_fwd(o_a, o_b, o_c, gates, W_branch[l], W_out[l], xs, g_m, f"merge_fwd{l}")
        h2 = _norm_mod_fwd(x1, gf, sh_f, sc_f, f"norm_ffn_fwd{l}")
        act = _ffn_in_fwd(h2, W_fin[l], f"ffn_in_fwd{l}")
        if l == 0:
            arrived1 = arrive(gather1, act, "comm_gather_weights1")
        x2, ffn = _matmul_resid(act, W_fout[l], x1, g_f, f"ffn_out{l}", TILES["ffn_out"],
                                after=arrived1[1][3] if l == 0 else None)
        saved.append(dict(x=xs, h=h, qkv=qkv, gates=gates, fb=fb, bfor=bfor, cum_col=cum_col, cum_row=cum_row,
                          tiles_t=tiles_t, o=(o_a, o_b, o_c), lse=(lse_a, lse_b, lse_c), merged=merged, mix=mix,
                          x1=x1, h2=h2, act=act, ffn=ffn))
        xs = x2

    dx, loss_tile, d_final_g, d_g_f, df = _final_loss(
        xs, loss_target[0], final_norm_g.reshape(1, D), (saved[-1]["ffn"], mods[-1][5]), "final_loss")

    grads = {k: [None] * depth for k in ("w_in", "w_branch", "w_out", "w_ffn_in", "w_ffn_out", "norm_mix_g",
                                          "norm_ffn_g", "b_forget", "sinks", "rel_bias", "dmod")}
    def rest_pieces(l):
        return [_branch_pieces(grads["w_branch"][l]), _row_pieces(grads["w_out"][l]),
                _row_pieces(grads["w_ffn_in"][l]), _row_pieces(grads["w_ffn_out"][l])]

    reduce1 = reduce_rest0 = reduce_in0 = None
    for l in reversed(range(depth)):
        sv = saved[l]
        sh_m, sc_m, g_m, sh_f, sc_f, g_f = mods[l]
        gm, gf = norm_mix_g[l:l + 1], norm_ffn_g[l:l + 1]
        du_g, du_u = _ffn_mid_bwd(sv["h2"], df, W_fin[l], W_fout[l], f"ffn_mid_bwd{l}")
        du = jnp.concatenate([du_g, du_u], axis=1)
        grads["w_ffn_out"][l] = _matmul(sv["act"], df, "tn", BF16, f"wgrad_ffn_out{l}", TILES["wgrad_ffn_out"])
        grads["w_ffn_in"][l] = _matmul(du, sv["h2"], "tn", BF16, f"wgrad_ffn_in{l}", TILES["wgrad_ffn_in"])
        dh2 = _matmul(du, W_fin[l], "nn", F32, f"dgrad_ffn_in{l}", TILES["dgrad_ffn_in"])
        dx1, d_sh_f, d_sc_f, d_gf, d_g_m, dmix = _norm_mod_bwd(sv["x1"], dh2, dx, gf, sc_f, f"norm_ffn_bwd{l}",
                                                               below=(sv["mix"], g_m))
        grads["w_out"][l] = _matmul(sv["merged"], dmix, "tn", BF16, f"wgrad_out{l}", TILES["wgrad_out"])
        o_a, o_b, o_c = sv["o"]
        dgates, d_w_branch, do_a, do_b, do_c, dl_a, dl_b, dl_c = _merge_bwd(
            dmix, o_a, o_b, o_c, sv["gates"], W_branch[l], W_out[l], f"merge_bwd{l}")
        grads["w_branch"][l] = d_w_branch.astype(BF16)
        lse_rows = list(sv["lse"])
        if l == 0:
            reduce_rest0 = _exchange_start(rest_pieces(0), True, dgates, "comm_reduce_rest0_start")
            lse_rows = [t + reduce_rest0[4][0:1, 0:1] for t in lse_rows]
        dq_a, dk_a, dv_a, dsink = _attn_bwd("a", sv["qkv"], do_a, lse_rows[0], dl_a.reshape(4, 2, S), f"attn_a_bwd{l}",
                                            sinks=sinks[l], slopes=slopes)
        dq_b, dk_b, dv_b, dck, dcq = _attn_bwd("b", sv["qkv"], do_b, lse_rows[1], dl_b.reshape(4, 2, S),
                                               f"attn_b_bwd{l}", cq_row=sv["cum_row"], ck_col=sv["cum_col"])
        dq_c, dk_c, dv_c, dtiles_t = _attn_bwd("c", sv["qkv"], do_c, lse_rows[2], dl_c.reshape(4, 2, S),
                                               f"attn_c_bwd{l}", bias_t=sv["tiles_t"])
        grads["sinks"][l] = dsink[:, :2, 0].reshape(8)
        grads["rel_bias"][l] = _rel_reduce(dtiles_t, f"rel_reduce{l}")[:, 0, :N_REL]
        dcum_k = _pad_lanes(_heads_from_col(dck), BLK)
        dcum_q = _pad_lanes(_heads_from_row(dcq), BLK)
        dfb, d_bfor = _forget_bwd(dcum_q, dcum_k, sv["fb"], sv["bfor"], f"forget_bwd{l}")
        grads["b_forget"][l] = d_bfor[0, :N_FORGET]
        dproj = jnp.concatenate(
            [t.astype(BF16) for t in (dq_a, dk_a, dv_a, dq_b, dk_b, dv_b, dq_c, dk_c, dv_c, dgates, dfb)],
            axis=1)
        grads["w_in"][l] = _matmul(sv["h"], dproj, "tn", BF16, f"wgrad_in{l}", TILES["wgrad_in"])
        if l == 1:
            reduce1 = _exchange_start([_w_in_pieces(grads["w_in"][1], "w_in_pieces1")] + rest_pieces(1), True, dproj,
                                      "comm_reduce1_start")
        dh = _matmul(dproj, W_in[l], "nt", F32, f"dgrad_in{l}", TILES["dgrad_in"], after=reduce1[4] if l == 1 else None)
        d_g_f_here = d_g_f
        if l > 0:
            dx, d_sh_m, d_sc_m, d_gm, d_g_f, df = _norm_mod_bwd(sv["x"], dh, dx1, gm, sc_m, f"norm_mix_bwd{l}",
                                                                below=(saved[l - 1]["ffn"], mods[l - 1][5]))
        else:
            dx, d_sh_m, d_sc_m, d_gm = _norm_mod_bwd(sv["x"], dh, dx1, gm, sc_m, f"norm_mix_bwd{l}")
        grads["norm_mix_g"][l] = d_gm[0]
        grads["norm_ffn_g"][l] = d_gf[0]
        grads["dmod"][l] = jnp.concatenate([d_sh_m, d_sc_m, d_g_m, d_sh_f, d_sc_f, d_g_f_here], axis=1)[0]

    grad_x = dx.reshape(x.shape)

    small_shapes = dict(dmod=b_ada.shape, norm_mix_g=norm_mix_g.shape, norm_ffn_g=norm_ffn_g.shape,
                        final_norm_g=final_norm_g.shape, b_forget=b_forget.shape, sinks=sinks.shape,
                        rel_bias=rel_bias.shape, loss=())
    mine_small = _pack_small(dict(
        loss=_pad_lanes(loss_tile[0:1, 0:1], 128),
        dmod=jnp.stack(grads["dmod"]), norm_mix_g=jnp.stack(grads["norm_mix_g"]),
        norm_ffn_g=jnp.stack(grads["norm_ffn_g"]), final_norm_g=d_final_g[0],
        b_forget=_pad_lanes(jnp.stack(grads["b_forget"]).reshape(1, -1), 128),
        sinks=_pad_lanes(jnp.stack(grads["sinks"]).reshape(1, -1), 128),
        rel_bias=_pad_lanes(jnp.stack(grads["rel_bias"]).reshape(1, -1), 4224)))
    all_small = _small_all_gather(mine_small, "comm_gather_small").reshape(N_DEV, SMALL_ROWS, 128)
    pieces_in0 = _w_in_pieces(grads["w_in"][0], "w_in_pieces0", pair_major=True)
    from_sibling = _sibling_exchange([pieces_in0], "comm_reduce_in0_sibling")[0]
    pair_sum_in0 = _pair_add(pieces_in0, from_sibling, pc.astype(jnp.int32).reshape(1), "pair_add_in0")
    reduce_in0 = _exchange_start([pair_sum_in0], True, all_small, "comm_reduce_in0_start", CHIPS)
    in0_started = reduce_in0[4]

    def pack_params(b_ada_, nm, nf, fn, bf, sk, rb):
        return _pack_small(dict(dmod=b_ada_, norm_mix_g=nm, norm_ffn_g=nf, final_norm_g=fn, loss=jnp.zeros((1, 128), F32),
                                b_forget=_pad_lanes(bf.reshape(1, -1), 128), sinks=_pad_lanes(sk.reshape(1, -1), 128),
                                rel_bias=_pad_lanes(rb.reshape(1, -1), 4224)))

    small_out = _adamw(
        pack_params(b_ada, norm_mix_g, norm_ffn_g, final_norm_g, b_forget, sinks, rel_bias)[None],
        pack_params(m_b_ada, m_norm_mix_g, m_norm_ffn_g, m_final_norm_g, m_b_forget, m_sinks, m_rel_bias)[None],
        pack_params(v_b_ada, v_norm_mix_g, v_norm_ffn_g, v_final_norm_g, v_b_forget, v_sinks, v_rel_bias)[None],
        [all_small], "adamw_small", me_arr, after=in0_started)
    small_out = [_unpack_small(t[0], small_shapes) for t in small_out]

    dmod_all = all_small[:, :96].reshape(N_DEV, depth, 6 * D)
    dmod_cols = lax.dynamic_slice_in_dim(dmod_all, me * w_ada.shape[2], w_ada.shape[2], axis=2)
    d_w_ada = _ada_bwd(jnp.transpose(c_all), jnp.transpose(dmod_cols, (1, 0, 2)), "ada_bwd")

    big = {"w_ada": _adamw(w_ada, m_w_ada, v_w_ada, [d_w_ada[l:l + 1] for l in range(depth)], "adamw_w_ada", me_arr,
                           after=in0_started)}
    sent1, landed1 = _exchange_wait(reduce1, True, big["w_ada"][0], "comm_reduce1_wait")
    sent_rest0, landed_rest0 = _exchange_wait(reduce_rest0, True, landed1[0], "comm_reduce_rest0_wait")
    parts = {"w_in": [None, (landed1[0], sent1[0])]}
    for a, name in enumerate(("w_branch", "w_out", "w_ffn_in", "w_ffn_out")):
        parts[name] = [(landed_rest0[a], sent_rest0[a]), (landed1[1 + a], sent1[1 + a])]

    def update(name, w, m, v, view=lambda t: t):
        per_layer = lambda t: t.reshape(depth, -1, t.shape[-1])
        outs = _adamw(*[per_layer(view(t)) for t in (w, m, v)], parts[name], f"adamw_{name}",
                      me_in_arr if name == "w_in" else me_arr)
        big[name] = [view(t).reshape(w.shape) for t in outs]

    update("w_ffn_in", w_ffn_in, m_w_ffn_in, v_w_ffn_in, _w_ffn_in_view)
    update("w_ffn_out", w_ffn_out, m_w_ffn_out, v_w_ffn_out)
    update("w_branch", w_branch, m_w_branch, v_w_branch)
    update("w_out", w_out, m_w_out, v_w_out)
    sent_in0, landed_in0 = _exchange_wait(reduce_in0, True, big["w_out"][0], "comm_reduce_in0_wait", CHIPS)
    parts["w_in"][0] = (landed_in0[0], sent_in0[0])
    update("w_in", w_in, m_w_in, v_w_in)

    def leaf(kind, name):
        if name in big:
            return big[name][kind]
        return small_out[kind]["dmod" if name == "b_ada" else name]

    order = ["norm_mix_g", "norm_ffn_g", "w_ada", "b_ada", "w_in", "b_forget", "sinks", "rel_bias", "w_branch",
             "w_out", "w_ffn_in", "w_ffn_out", "final_norm_g"]
    loss = small_out[0]["loss"]
    return (loss, grad_x, *[leaf(0, n) for n in order], *[leaf(1, n) for n in order],
            *[leaf(2, n) for n in order], *[leaf(3, n) for n in order])
```

```python
import functools

import jax
import jax.numpy as jnp
from jax import lax
from jax.experimental import pallas as pl
from jax.experimental.pallas import tpu as pltpu

F32 = jnp.float32
BF16 = jnp.bfloat16
NEG_INF = -1e30
EPS = 1e-6
N_DEV = 8
BLK = 128
GROUP = 4 * BLK
VMEM_LIMIT_BYTES = 56 * 1024 * 1024

D_MODEL = 1024
N_QKV = 3840
N_GATES = 3072
N_MAIN = N_QKV + N_GATES
N_FORGET = 8
N_IN = N_MAIN + N_FORGET
F_COL = 2304
FFN_HIDDEN = 2816
N_REL = 257

ADAM_LR, ADAM_B1, ADAM_B2, ADAM_EPS, ADAM_WD, ADAM_STEP = 0.001, 0.9, 0.999, 1e-08, 0.01, 10

NN = (((1,), (0,)), ((), ()))
NT = (((1,), (1,)), ((), ()))
TN = (((0,), (0,)), ((), ()))
HIGHEST = lax.Precision.HIGHEST

ATTN_COLS = {"a": (0, 4, 5), "b": (6, 10, 14), "c": (18, 22, 26)}
ATTN_WINDOW = {"a": 2, "c": 5}
ATTN_BLOCKS_PER_STEP = {"a": 8, "b": GROUP // BLK, "c": 2}
ROW_TILE = 1024


def _params():
    return pltpu.CompilerParams(vmem_limit_bytes=VMEM_LIMIT_BYTES)


def _tile(n, target):
    best = None
    t = 128
    while t <= min(n, target):
        if n % t == 0:
            best = t
        t += 128
    return best if best is not None else n


def _row_tile(n, target):
    t = min(n, target)
    while n % t:
        t -= 8
    return t


TILES = {
    "proj": (2048, 768, 1024), "proj_forget": (1024, 128, 1024),
    "ffn_out": (1024, 512, 2816), "ffn_fused": (512, 1408),
    "wgrad_ffn_out": (1408, 1024, 2048), "wgrad_ffn_in": (1408, 1024, 2048), "dgrad_ffn_in": (1024, 1024, 2816),
    "wgrad_out": (1024, 1024, 2048),
    "wgrad_in": (1024, 1408, 2048), "dgrad_in": (1024, 1024, 3520),
}


def _matmul(a, b, mode, out_dtype, name, tiles, *, n=None, a_off=0, b_off=0, m=None, after=None):
    tm, tn, tk = tiles
    if mode == "nn":
        M, K = a.shape if m is None else (m, a.shape[1])
        N = b.shape[1] if n is None else n
    elif mode == "nt":
        M, K = a.shape
        N = b.shape[0] if n is None else n
    else:
        K = a.shape[0]
        M = a.shape[1] if m is None else m
        N = b.shape[1] if n is None else n
    tm = _tile(M, tm) if M % 128 == 0 else M
    tn = _tile(N, tn)
    tk = _tile(K, tk)
    nk = K // tk
    dims = {"nn": NN, "nt": NT, "tn": TN}[mode]
    if mode == "nn":
        a_spec = pl.BlockSpec((tm, tk), lambda i, j, k: (i + a_off, k))
        b_spec = pl.BlockSpec((tk, tn), lambda i, j, k: (k, j + b_off))
    elif mode == "nt":
        a_spec = pl.BlockSpec((tm, tk), lambda i, j, k: (i + a_off, k))
        b_spec = pl.BlockSpec((tn, tk), lambda i, j, k: (j + b_off, k))
    else:
        a_spec = pl.BlockSpec((tk, tm), lambda i, j, k: (k, i + a_off))
        b_spec = pl.BlockSpec((tk, tn), lambda i, j, k: (k, j + b_off))

    def body(a_ref, b_ref, *rest):
        o_ref, acc_ref = rest[-2:]
        k = pl.program_id(2)
        part = lax.dot_general(a_ref[...], b_ref[...], dims, preferred_element_type=F32)
        if nk == 1:
            o_ref[...] = part.astype(o_ref.dtype)
        else:
            @pl.when(k == 0)
            def _():
                acc_ref[...] = part

            @pl.when(k > 0)
            def _():
                acc_ref[...] += part

            @pl.when(k == nk - 1)
            def _():
                o_ref[...] = acc_ref[...].astype(o_ref.dtype)

    return pl.pallas_call(
        body, name=name,
        out_shape=jax.ShapeDtypeStruct((M, N), out_dtype),
        grid=(M // tm, N // tn, nk),
        in_specs=[a_spec, b_spec] + ([ANY] if after is not None else []),
        out_specs=pl.BlockSpec((tm, tn), lambda i, j, k: (i, j)),
        scratch_shapes=[pltpu.VMEM((tm, tn) if nk > 1 else (8, 128), F32)],
        compiler_params=_params(),
    )(a, b, *([after] if after is not None else []))


def _project(h, w, name):
    S, D = h.shape
    tm, tn, _ = TILES["proj"]
    tm = _tile(S, tm)
    nq, ng = N_QKV // tn, N_GATES // tn

    def body(h_ref, w_ref, q_ref, g_ref):
        j = pl.program_id(1)
        acc = jnp.dot(h_ref[...], w_ref[...], preferred_element_type=F32)

        @pl.when(j < nq)
        def _():
            q_ref[...] = acc.astype(BF16)

        @pl.when(j >= nq)
        def _():
            g_ref[...] = acc

    return pl.pallas_call(
        body, name=name,
        out_shape=(jax.ShapeDtypeStruct((S, N_QKV), BF16), jax.ShapeDtypeStruct((S, N_GATES), F32)),
        grid=(S // tm, nq + ng),
        in_specs=[pl.BlockSpec((tm, D), lambda i, j: (i, 0)), pl.BlockSpec((D, tn), lambda i, j: (0, j))],
        out_specs=(pl.BlockSpec((tm, tn), lambda i, j: (i, jnp.minimum(j, nq - 1))),
                   pl.BlockSpec((tm, tn), lambda i, j: (i, jnp.maximum(j - nq, 0)))),
        compiler_params=_params(),
    )(h, w)


def _matmul_resid(a, b, resid, gate, name, tiles, after=None):
    M, K = a.shape
    N = b.shape[1]
    tm, tn, tk = (_tile(d, t) for d, t in zip((M, N, K), tiles))
    nk = K // tk

    def body(a_ref, b_ref, r_ref, g_ref, *rest):
        o_ref, s_ref, acc_ref = rest[-3:]
        k = pl.program_id(2)
        part = jnp.dot(a_ref[...], b_ref[...], preferred_element_type=F32)

        def finish(acc):
            o_ref[...] = r_ref[...] + g_ref[...] * acc
            s_ref[...] = acc.astype(BF16)

        if nk == 1:
            finish(part)
        else:
            @pl.when(k == 0)
            def _():
                acc_ref[...] = part

            @pl.when(k > 0)
            def _():
                acc_ref[...] += part

            @pl.when(k == nk - 1)
            def _():
                finish(acc_ref[...])

    return pl.pallas_call(
        body, name=name,
        out_shape=(jax.ShapeDtypeStruct((M, N), F32), jax.ShapeDtypeStruct((M, N), BF16)),
        grid=(M // tm, N // tn, nk),
        in_specs=[pl.BlockSpec((tm, tk), lambda i, j, k: (i, k)),
                  pl.BlockSpec((tk, tn), lambda i, j, k: (k, j)),
                  pl.BlockSpec((tm, tn), lambda i, j, k: (i, j)),
                  pl.BlockSpec((1, tn), lambda i, j, k: (0, j))] + ([ANY] if after is not None else []),
        out_specs=(pl.BlockSpec((tm, tn), lambda i, j, k: (i, j)),
                   pl.BlockSpec((tm, tn), lambda i, j, k: (i, j))),
        scratch_shapes=[pltpu.VMEM((tm, tn) if nk > 1 else (8, 128), F32)],
        compiler_params=_params(),
    )(a, b, resid, gate, *([after] if after is not None else []))


def _norm_mod_fwd(x, g, shift, scale, name):
    S, D = x.shape
    ts = _row_tile(S, ROW_TILE)

    def body(x_ref, g_ref, sh_ref, sc_ref, h_ref):
        xv = x_ref[...]
        rstd = lax.rsqrt(jnp.mean(xv * xv, axis=-1, keepdims=True) + EPS)
        y = xv * rstd * g_ref[...]
        h_ref[...] = (y * (1.0 + sc_ref[...]) + sh_ref[...]).astype(BF16)

    row = pl.BlockSpec((1, D), lambda i: (0, 0))
    return pl.pallas_call(
        body, name=name, out_shape=jax.ShapeDtypeStruct((S, D), BF16), grid=(S // ts,),
        in_specs=[pl.BlockSpec((ts, D), lambda i: (i, 0)), row, row, row],
        out_specs=pl.BlockSpec((ts, D), lambda i: (i, 0)),
        compiler_params=_params(),
    )(x, g, shift, scale)


def _accumulate_rows(i, pairs):
    @pl.when(i == 0)
    def _():
        for ref, value in pairs:
            ref[...] = value

    @pl.when(i > 0)
    def _():
        for ref, value in pairs:
            ref[...] += value


def _gated_residual_bwd(dx, f_ref, gate_ref, df_ref):
    df_ref[...] = (dx * gate_ref[...]).astype(BF16)
    return jnp.sum(dx * f_ref[...].astype(F32), axis=0, keepdims=True)


def _norm_mod_bwd(x, dh, dres, g, scale, name, below=None):
    S, D = x.shape
    ts = _row_tile(S, ROW_TILE)

    def body(x_ref, dh_ref, dr_ref, g_ref, sc_ref, *rest):
        i = pl.program_id(0)
        xv, dhv, gv = x_ref[...], dh_ref[...], g_ref[...]
        rstd = lax.rsqrt(jnp.mean(xv * xv, axis=-1, keepdims=True) + EPS)
        xhat = xv * rstd
        dn = dhv * (1.0 + sc_ref[...])
        dxhat = dn * gv
        proj = jnp.mean(dxhat * xhat, axis=-1, keepdims=True)
        dx = dr_ref[...] + rstd * (dxhat - xhat * proj)
        sums = [jnp.sum(dhv, axis=0, keepdims=True), jnp.sum(dhv * (xhat * gv), axis=0, keepdims=True),
                jnp.sum(dn * xhat, axis=0, keepdims=True)]
        if below is None:
            dx_ref, *sum_refs = rest
        else:
            f_ref, gate_ref, dx_ref, *sum_refs, df_ref = rest
            sums.append(_gated_residual_bwd(dx, f_ref, gate_ref, df_ref))
        dx_ref[...] = dx
        _accumulate_rows(i, list(zip(sum_refs, sums)))

    tile = pl.BlockSpec((ts, D), lambda i: (i, 0))
    row = pl.BlockSpec((1, D), lambda i: (0, 0))
    vec = jax.ShapeDtypeStruct((1, D), F32)
    fused = below is not None
    return pl.pallas_call(
        body, name=name,
        out_shape=(jax.ShapeDtypeStruct((S, D), F32), vec, vec, vec)
        + ((vec, jax.ShapeDtypeStruct((S, D), BF16)) if fused else ()),
        grid=(S // ts,),
        in_specs=[tile, tile, tile, row, row] + ([tile, row] if fused else []),
        out_specs=(tile, row, row, row) + ((row, tile) if fused else ()),
        compiler_params=_params(),
    )(x, dh, dres, g, scale, *(below if fused else ()))


def _ffn_in_fwd(h, w_t, name):
    S, D = h.shape
    F = w_t.shape[0] // 2
    tm, tn = _tile(S, TILES["ffn_fused"][0]), _tile(F, TILES["ffn_fused"][1])
    nj = F // tn

    def body(h_ref, wg_ref, wu_ref, o_ref):
        hv = h_ref[...]
        ug = lax.dot_general(hv, wg_ref[...], NT, preferred_element_type=F32)
        uu = lax.dot_general(hv, wu_ref[...], NT, preferred_element_type=F32)
        o_ref[...] = (ug * jax.nn.sigmoid(ug) * uu).astype(BF16)

    return pl.pallas_call(
        body, name=name, out_shape=jax.ShapeDtypeStruct((S, F), BF16), grid=(nj, S // tm),
        in_specs=[pl.BlockSpec((tm, D), lambda j, i: (i, 0)),
                  pl.BlockSpec((tn, D), lambda j, i: (j, 0)),
                  pl.BlockSpec((tn, D), lambda j, i: (j + nj, 0))],
        out_specs=pl.BlockSpec((tm, tn), lambda j, i: (i, j)),
        compiler_params=_params(),
    )(h, w_t, w_t)


def _ffn_mid_bwd(h, df, w_in_t, w_out, name):
    S, D = h.shape
    F = w_in_t.shape[0] // 2
    tm, tn = _tile(S, TILES["ffn_fused"][0]), _tile(F, TILES["ffn_fused"][1])
    nj = F // tn

    def body(h_ref, df_ref, wg_ref, wu_ref, wo_ref, dg_ref, du_ref):
        hv = h_ref[...]
        ug = lax.dot_general(hv, wg_ref[...], NT, preferred_element_type=F32)
        uu = lax.dot_general(hv, wu_ref[...], NT, preferred_element_type=F32)
        dact = lax.dot_general(df_ref[...], wo_ref[...], NT, preferred_element_type=F32)
        sig = jax.nn.sigmoid(ug)
        dg_ref[...] = (dact * uu * (sig * (1.0 + ug * (1.0 - sig)))).astype(BF16)
        du_ref[...] = (dact * (ug * sig)).astype(BF16)

    out = jax.ShapeDtypeStruct((S, F), BF16)
    return pl.pallas_call(
        body, name=name, out_shape=(out, out), grid=(nj, S // tm),
        in_specs=[pl.BlockSpec((tm, D), lambda j, i: (i, 0)),
                  pl.BlockSpec((tm, D), lambda j, i: (i, 0)),
                  pl.BlockSpec((tn, D), lambda j, i: (j, 0)),
                  pl.BlockSpec((tn, D), lambda j, i: (j + nj, 0)),
                  pl.BlockSpec((tn, D), lambda j, i: (j, 0))],
        out_specs=(pl.BlockSpec((tm, tn), lambda j, i: (i, j)), pl.BlockSpec((tm, tn), lambda j, i: (i, j))),
        compiler_params=_params(),
    )(h, df, w_in_t, w_in_t, w_out)


def _merge_fwd(o_a, o_b, o_c, gates, w_branch, w_out, resid, gate, norm_g, shift, scale, name, *, tm=512):
    S, W = o_a.shape
    D = w_branch.shape[2]
    tm = _row_tile(S, tm)

    def body(oa_ref, ob_ref, oc_ref, g_ref, w_ref, wo_ref, r_ref, gm_ref, ng_ref, sh_ref, sc_ref,
             x_ref, m_ref, mix_ref, h_ref):
        acc = None
        for k, o_ref in enumerate((oa_ref, ob_ref, oc_ref)):
            y = jnp.dot(o_ref[...], w_ref[k], preferred_element_type=F32)
            t = jax.nn.sigmoid(g_ref[:, k * D:(k + 1) * D]) * y
            acc = t if acc is None else acc + t
        merged = acc.astype(BF16)
        m_ref[...] = merged
        mix = jnp.dot(merged, wo_ref[...], preferred_element_type=F32)
        xv = r_ref[...] + gm_ref[...] * mix
        x_ref[...] = xv
        mix_ref[...] = mix.astype(BF16)
        rstd = lax.rsqrt(jnp.mean(xv * xv, axis=-1, keepdims=True) + EPS)
        h_ref[...] = (xv * rstd * ng_ref[...] * (1.0 + sc_ref[...]) + sh_ref[...]).astype(BF16)

    o_spec = pl.BlockSpec((tm, W), lambda i: (i, 0))
    tile = pl.BlockSpec((tm, D), lambda i: (i, 0))
    row = pl.BlockSpec((1, D), lambda i: (0, 0))
    half = jax.ShapeDtypeStruct((S, D), BF16)
    return pl.pallas_call(
        body, name=name,
        out_shape=(jax.ShapeDtypeStruct((S, D), F32), half, half, half),
        grid=(S // tm,),
        in_specs=[o_spec, o_spec, o_spec, pl.BlockSpec((tm, 3 * D), lambda i: (i, 0)),
                  pl.BlockSpec((3, W, D), lambda i: (0, 0, 0)), pl.BlockSpec((D, D), lambda i: (0, 0)),
                  tile, row, row, row, row],
        out_specs=(tile, tile, tile, tile),
        compiler_params=_params(),
    )(o_a, o_b, o_c, gates, w_branch, w_out, resid, gate, norm_g, shift, scale)


def _merge_bwd(dmix, o_a, o_b, o_c, gates, w_branch, w_out, name, *, tm=256):
    S, W = o_a.shape
    D = w_branch.shape[2]
    tm = _row_tile(S, tm)
    n_heads = W // 64

    def body(dm_ref, oa_ref, ob_ref, oc_ref, g_ref, w_ref, wo_ref, dg_ref, dw_ref,
             doa_ref, dob_ref, doc_ref, dla_ref, dlb_ref, dlc_ref):
        first = pl.program_id(0) == 0
        head_of_column = (lax.broadcasted_iota(jnp.int32, (W, BLK), 0) // 64
                          == lax.broadcasted_iota(jnp.int32, (W, BLK), 1)).astype(F32)
        dm = lax.dot_general(dm_ref[...], wo_ref[...], NT, preferred_element_type=F32)
        branches = ((oa_ref, doa_ref, dla_ref), (ob_ref, dob_ref, dlb_ref), (oc_ref, doc_ref, dlc_ref))
        for k, (o_ref, do_ref, dl_ref) in enumerate(branches):
            wk = w_ref[k]
            ov = o_ref[...]
            y = jnp.dot(ov, wk, preferred_element_type=F32)
            g = jax.nn.sigmoid(g_ref[:, k * D:(k + 1) * D])
            dy = (dm * g).astype(BF16)
            dwk = lax.dot_general(ov, dy, TN, preferred_element_type=F32)

            @pl.when(first)
            def _(k=k, dwk=dwk):
                dw_ref[k] = dwk

            @pl.when(jnp.logical_not(first))
            def _(k=k, dwk=dwk):
                dw_ref[k] += dwk
            dg_ref[:, k * D:(k + 1) * D] = (dm * y * (g * (1.0 - g))).astype(BF16)
            do16 = lax.dot_general(dy, wk, NT, preferred_element_type=F32).astype(BF16)
            do_ref[...] = do16
            prod = do16.astype(F32) * ov.astype(F32)
            sums = jnp.dot(prod, head_of_column, preferred_element_type=F32, precision=HIGHEST)
            dl_ref[...] = jnp.transpose(sums)[:n_heads, :]

    o_spec = pl.BlockSpec((tm, W), lambda i: (i, 0))
    wide = pl.BlockSpec((tm, 3 * D), lambda i: (i, 0))
    dl_spec = pl.BlockSpec((n_heads, tm), lambda i: (0, i))
    o_out = jax.ShapeDtypeStruct((S, W), BF16)
    wide_out = jax.ShapeDtypeStruct((S, 3 * D), BF16)
    dl_out = jax.ShapeDtypeStruct((n_heads, S), F32)
    whole = pl.BlockSpec((3, W, D), lambda i: (0, 0, 0))
    return pl.pallas_call(
        body, name=name,
        out_shape=(wide_out, jax.ShapeDtypeStruct((3, W, D), F32), o_out, o_out, o_out, dl_out, dl_out, dl_out),
        grid=(S // tm,),
        in_specs=[pl.BlockSpec((tm, D), lambda i: (i, 0)), o_spec, o_spec, o_spec, wide, whole,
                  pl.BlockSpec((D, D), lambda i: (0, 0))],
        out_specs=(wide, whole, o_spec, o_spec, o_spec, dl_spec, dl_spec, dl_spec),
        compiler_params=_params(),
    )(dmix, o_a, o_b, o_c, gates, w_branch, w_out)


def _band_mask(variant, t_abs, s_abs):
    if variant == "b":
        return s_abs <= t_abs
    qc, kc = t_abs >> 6, s_abs >> 6
    return (kc <= qc) & (kc >= qc - (2 if variant == "a" else 8))


def _attn_fwd(variant, qkv, name, *, sinks=None, slopes=None, cq_col=None, ck_row=None, bias=None, after=None):
    S = qkv.shape[0]
    nb = S // BLK
    qb, kb, vb = ATTN_COLS[variant]
    shared_kv = variant == "a"
    win = ATTN_WINDOW.get(variant)
    per_step = ATTN_BLOCKS_PER_STEP[variant]

    def body(*refs):
        if after is not None:
            refs = refs[:-3] + refs[-2:]
        if variant == "a":
            q_ref, k_ref, v_ref, sink_ref, slope_ref, o_ref, lse_ref = refs
        elif variant == "b":
            q_ref, k_ref, v_ref, cq_ref, ck_ref, o_ref, lse_ref = refs
        else:
            q_ref, k_ref, v_ref, bias_ref, o_ref, lse_ref = refs
        p = pl.program_id(0)
        lane = lax.broadcasted_iota(jnp.int32, (1, BLK), 1)
        diagonal = lax.broadcasted_iota(jnp.int32, (BLK, BLK), 0) == lax.broadcasted_iota(jnp.int32, (BLK, BLK), 1)

        def compute(i, rows, start, n_keys):
            n_rows = rows.stop - rows.start
            t_abs = i * BLK + lax.broadcasted_iota(jnp.int32, (n_rows, 1), 0)
            q2 = q_ref[rows, :].astype(F32) * 0.125
            k_w = k_ref[pl.ds(start, n_keys), :]
            v_w = v_ref[pl.ds(start, n_keys), :]
            s_abs = start + lax.broadcasted_iota(jnp.int32, (1, n_keys), 1)
            valid = _band_mask(variant, t_abs, s_abs)
            outs = []
            for half in (0, 1):
                hmask = (lane >= 64) if half else (lane < 64)
                qh = jnp.where(hmask, q2, 0.0)
                if shared_kv:
                    swap = (p // 2) != half
                    qh = jnp.where(swap, pltpu.roll(qh, 64, 1), qh)
                s = lax.dot_general(qh.astype(BF16), k_w, NT, preferred_element_type=F32)
                if variant == "a":
                    head = 2 * p + half
                    s = s + (-slope_ref[head]) * jnp.abs(t_abs - s_abs).astype(F32)
                elif variant == "b":
                    s = s + cq_ref[rows, half:half + 1] - ck_ref[half:half + 1, pl.ds(start, n_keys)]
                else:
                    j0 = start // BLK
                    s = s + jnp.concatenate([jnp.concatenate(
                        [bias_ref[half, jnp.clip(i + r - j0 - b, 0, 4)] for b in range(n_keys // BLK)], axis=1)
                        for r in range(n_rows // BLK)], axis=0)
                s = jnp.where(valid, s, NEG_INF)
                m = jnp.max(s, axis=1, keepdims=True)
                if variant == "a":
                    m = jnp.maximum(m, sink_ref[head])
                pe = jnp.exp(s - m)
                l = jnp.sum(pe, axis=1, keepdims=True)
                if variant == "a":
                    l = l + jnp.exp(sink_ref[head] - m)
                out = jnp.dot(pe.astype(BF16), v_w, preferred_element_type=F32) / l
                if shared_kv:
                    out = jnp.where(swap, pltpu.roll(out, 64, 1), out)
                outs.append(out)
                lse = m + jnp.log(l)
                for b in range(n_rows // BLK):
                    part = jnp.where(diagonal, lse[b * BLK:(b + 1) * BLK, :], 0.0)
                    lse_ref[half:half + 1, rows.start + b * BLK:rows.start + (b + 1) * BLK] = jnp.sum(
                        part, axis=0, keepdims=True)
            o_ref[rows, :] = jnp.where(lane < 64, outs[0], outs[1]).astype(BF16)

        step = pl.program_id(1)
        if variant == "b":
            for g in range(S // GROUP):
                pl.when(step == g)(functools.partial(compute, step * per_step, slice(0, GROUP), 0, (g + 1) * GROUP))
        elif variant == "c":
            span = win + per_step - 1
            start = jnp.clip(step * per_step - (win - 1), 0, nb - span) * BLK
            compute(step * per_step, slice(0, per_step * BLK), pl.multiple_of(start, BLK), span * BLK)
        else:
            for sub in range(per_step):
                i = step * per_step + sub
                start = jnp.clip(i - (win - 1), 0, nb - win) * BLK
                compute(i, slice(sub * BLK, (sub + 1) * BLK), pl.multiple_of(start, BLK), win * BLK)

    tq = per_step * BLK
    kv_col = (lambda p, i: (0, kb)) if shared_kv else (lambda p, i: (0, kb + p))
    vv_col = (lambda p, i: (0, vb)) if shared_kv else (lambda p, i: (0, vb + p))
    in_specs = [pl.BlockSpec((tq, BLK), lambda p, i: (i, qb + p)),
                pl.BlockSpec((S, BLK), kv_col), pl.BlockSpec((S, BLK), vv_col)]
    args = [qkv, qkv, qkv]
    if variant == "a":
        in_specs += [pl.BlockSpec(memory_space=pltpu.SMEM), pl.BlockSpec(memory_space=pltpu.SMEM)]
        args += [sinks, slopes]
    elif variant == "b":
        in_specs += [pl.BlockSpec((None, tq, 2), lambda p, i: (p, i, 0)),
                     pl.BlockSpec((None, 2, S), lambda p, i: (p, 0, 0))]
        args += [cq_col, ck_row]
    else:
        in_specs += [pl.BlockSpec((2, 5, BLK, BLK), lambda p, i: (p, 0, 0, 0))]
        args += [bias]
    if after is not None:
        in_specs.append(ANY)
        args.append(after)
    return pl.pallas_call(
        body, name=name,
        out_shape=(jax.ShapeDtypeStruct((S, 512), BF16), jax.ShapeDtypeStruct((4, 2, S), F32)),
        grid=(4, nb // per_step), in_specs=in_specs,
        out_specs=(pl.BlockSpec((tq, BLK), lambda p, i: (i, p)),
                   pl.BlockSpec((None, 2, tq), lambda p, i: (p, 0, i))),
        compiler_params=_params(),
    )(*args)


def _attn_bwd(variant, qkv, do, lse_row, delta_row, name, *, sinks=None, slopes=None, cq_row=None,
              ck_col=None, bias_t=None):
    S = qkv.shape[0]
    nb = S // BLK
    qb, kb, vb = ATTN_COLS[variant]
    shared_kv = variant == "a"
    win = ATTN_WINDOW.get(variant)
    per_step = ATTN_BLOCKS_PER_STEP[variant]

    def body(*refs):
        *refs, dqt_ref = refs
        if variant == "a":
            (q_ref, k_ref, v_ref, do_ref, lse_ref, dl_ref, sink_ref, slope_ref,
             dq_ref, dk_ref, dv_ref, ex_ref) = refs
        elif variant == "b":
            (q_ref, k_ref, v_ref, do_ref, lse_ref, dl_ref, cq_ref, ck_ref,
             dq_ref, dk_ref, dv_ref, ex_ref, dcq_ref) = refs
        else:
            (q_ref, k_ref, v_ref, do_ref, lse_ref, dl_ref, bias_ref,
             dq_ref, dk_ref, dv_ref, ex_ref) = refs
        p = pl.program_id(0)
        lane = lax.broadcasted_iota(jnp.int32, (1, BLK), 1)
        hmasks = [(lane < 64), (lane >= 64)]
        swaps = [(p // 2) != half for half in (0, 1)] if shared_kv else None

        @pl.when(pl.program_id(1) == 0)
        def _():
            dqt_ref[...] = jnp.zeros_like(dqt_ref)
            if variant == "b":
                dcq_ref[...] = jnp.zeros_like(dcq_ref)
            else:
                ex_ref[...] = jnp.zeros_like(ex_ref)

        def to_kv_lanes(x, h):
            x = jnp.where(hmasks[h], x, 0.0)
            if shared_kv:
                x = jnp.where(swaps[h], pltpu.roll(x, 64, 1), x)
            return x

        def compute(j, rows, start, n_q):
            n_rows = rows.stop - rows.start
            s_abs = j * BLK + lax.broadcasted_iota(jnp.int32, (n_rows, 1), 0)
            off_k = pl.multiple_of(j * BLK, BLK)
            k2 = k_ref[rows, :].astype(F32)
            v2 = v_ref[rows, :].astype(F32)
            if shared_kv:
                kv_lane = (lane >> 6) == (p // 2)
                k_src, v_src = jnp.where(kv_lane, k2, 0.0), jnp.where(kv_lane, v2, 0.0)
                k_al = [jnp.where(swaps[h], pltpu.roll(k_src, 64, 1), k_src) for h in (0, 1)]
                v_al = [jnp.where(swaps[h], pltpu.roll(v_src, 64, 1), v_src) for h in (0, 1)]
            else:
                k_al = [jnp.where(hmasks[h], k2, 0.0) for h in (0, 1)]
                v_al = [jnp.where(hmasks[h], v2, 0.0) for h in (0, 1)]
            k_al = [(t * 0.125).astype(BF16) for t in k_al]
            v_al = [t.astype(BF16) for t in v_al]
            q_w = q_ref[pl.ds(start, n_q), :]
            do_w = do_ref[pl.ds(start, n_q), :]
            t_abs = start + lax.broadcasted_iota(jnp.int32, (1, n_q), 1)
            valid = _band_mask(variant, t_abs, s_abs)
            dk_acc = dv_acc = None
            ds_both = []
            for half in (0, 1):
                s = lax.dot_general(k_al[half], q_w, NT, preferred_element_type=F32)
                if variant == "a":
                    s = s + (-slope_ref[2 * p + half]) * jnp.abs(t_abs - s_abs).astype(F32)
                elif variant == "b":
                    s = s + cq_ref[half:half + 1, pl.ds(start, n_q)] - ck_ref[rows, half:half + 1]
                else:
                    i0 = start // BLK
                    s = s + jnp.concatenate([jnp.concatenate(
                        [bias_ref[half, jnp.clip(i0 + b - j - r, 0, 4)] for b in range(n_q // BLK)], axis=1)
                        for r in range(n_rows // BLK)], axis=0)
                pr = jnp.where(valid, jnp.exp(s - lse_ref[half:half + 1, pl.ds(start, n_q)]), 0.0)
                dp = lax.dot_general(v_al[half], do_w, NT, preferred_element_type=F32)
                ds = pr * (dp - dl_ref[half:half + 1, pl.ds(start, n_q)])
                ds16 = ds.astype(BF16)
                dv_h = to_kv_lanes(jnp.dot(pr.astype(BF16), do_w, preferred_element_type=F32), half)
                dk_h = to_kv_lanes(jnp.dot(ds16, q_w, preferred_element_type=F32) * 0.125, half)
                dv_acc = dv_h if dv_acc is None else dv_acc + dv_h
                dk_acc = dk_h if dk_acc is None else dk_acc + dk_h
                ds_both.append(ds16)
                if variant == "b":
                    ex_ref[rows, half:half + 1] = -jnp.sum(ds, axis=1, keepdims=True)
                    dcq_ref[half:half + 1, pl.ds(start, n_q)] += jnp.sum(ds, axis=0, keepdims=True)
                elif variant == "c":
                    for r in range(n_rows // BLK):
                        for b in range(n_q // BLK):
                            ex_ref[half, jnp.clip(i0 + b - j - r, 0, 4)] += ds[r * BLK:(r + 1) * BLK, b * BLK:(b + 1) * BLK]
            dq_t = lax.dot_general(jnp.concatenate(k_al, axis=0), jnp.concatenate(ds_both, axis=0), TN,
                                   preferred_element_type=F32)
            dqt_ref[:, pl.ds(start, n_q)] += dq_t
            if shared_kv:
                @pl.when(p == 0)
                def _():
                    dk_ref[pl.ds(off_k, n_rows), :] = dk_acc
                    dv_ref[pl.ds(off_k, n_rows), :] = dv_acc

                @pl.when(p > 0)
                def _():
                    dk_ref[pl.ds(off_k, n_rows), :] += dk_acc
                    dv_ref[pl.ds(off_k, n_rows), :] += dv_acc
            else:
                dk_ref[pl.ds(off_k, n_rows), :] = dk_acc.astype(dk_ref.dtype)
                dv_ref[pl.ds(off_k, n_rows), :] = dv_acc.astype(dv_ref.dtype)
            if variant == "a":
                for half in (0, 1):
                    p_sink = jnp.exp(sink_ref[2 * p + half] - lse_ref[half:half + 1, pl.ds(off_k, n_rows)])
                    term = p_sink * dl_ref[half:half + 1, pl.ds(off_k, n_rows)]
                    ex_ref[half:half + 1, :] += -jnp.sum(term, axis=1, keepdims=True)

        step = pl.program_id(1)
        if variant == "b":
            for g in range(S // GROUP):
                pl.when(step == g)(functools.partial(compute, step * per_step, slice(0, GROUP), g * GROUP, S - g * GROUP))
        elif variant == "c":
            span = win + per_step - 1
            start = jnp.clip(step * per_step, 0, nb - span) * BLK
            compute(step * per_step, slice(0, per_step * BLK), pl.multiple_of(start, BLK), span * BLK)
        else:
            for sub in range(per_step):
                j = step * per_step + sub
                start = jnp.clip(j, 0, nb - win) * BLK
                compute(j, slice(sub * BLK, (sub + 1) * BLK), pl.multiple_of(start, BLK), win * BLK)

        @pl.when(step == nb // per_step - 1)
        def _():
            dq_ref[...] = jnp.transpose(dqt_ref[...]).astype(BF16)

    tk = per_step * BLK
    col = lambda c0: (lambda p, j: (0, c0 + p))
    kv_blk = (lambda c0: (lambda p, j: (j, c0))) if shared_kv else (lambda c0: (lambda p, j: (j, c0 + p)))
    pair = lambda p, j: (0, p)
    row_stat = pl.BlockSpec((None, 2, S), lambda p, j: (p, 0, 0))
    in_specs = [pl.BlockSpec((S, BLK), col(qb)),
                pl.BlockSpec((tk, BLK), kv_blk(kb)), pl.BlockSpec((tk, BLK), kv_blk(vb)),
                pl.BlockSpec((S, BLK), pair), row_stat, row_stat]
    args = [qkv, qkv, qkv, do, lse_row, delta_row]
    kv_width = BLK if shared_kv else 512
    kv_out = pl.BlockSpec((S, BLK), (lambda p, j: (0, 0)) if shared_kv else pair)
    kv_dtype = F32 if shared_kv else BF16
    out_shape = [jax.ShapeDtypeStruct((S, 512), BF16), jax.ShapeDtypeStruct((S, kv_width), kv_dtype),
                 jax.ShapeDtypeStruct((S, kv_width), kv_dtype)]
    out_specs = [pl.BlockSpec((S, BLK), pair), kv_out, kv_out]
    if variant == "a":
        in_specs += [pl.BlockSpec(memory_space=pltpu.SMEM), pl.BlockSpec(memory_space=pltpu.SMEM)]
        args += [sinks, slopes]
        out_shape.append(jax.ShapeDtypeStruct((4, 8, BLK), F32))
        out_specs.append(pl.BlockSpec((None, 8, BLK), lambda p, j: (p, 0, 0)))
    elif variant == "b":
        in_specs += [row_stat, pl.BlockSpec((None, tk, 2), lambda p, j: (p, j, 0))]
        args += [cq_row, ck_col]
        out_shape += [jax.ShapeDtypeStruct((4, S, 2), F32), jax.ShapeDtypeStruct((4, 2, S), F32)]
        out_specs += [pl.BlockSpec((None, tk, 2), lambda p, j: (p, j, 0)), row_stat]
    else:
        in_specs += [pl.BlockSpec((2, 5, BLK, BLK), lambda p, j: (p, 0, 0, 0))]
        args += [bias_t]
        out_shape.append(jax.ShapeDtypeStruct((8, 5, BLK, BLK), F32))
        out_specs.append(pl.BlockSpec((2, 5, BLK, BLK), lambda p, j: (p, 0, 0, 0)))
    return pl.pallas_call(
        body, name=name, out_shape=tuple(out_shape), grid=(4, nb // per_step),
        in_specs=in_specs, out_specs=tuple(out_specs), scratch_shapes=[pltpu.VMEM((BLK, S), F32)],
        compiler_params=_params(),
    )(*args)


def _log_sigmoid(x):
    return jnp.minimum(x, 0.0) - jnp.log(1.0 + jnp.exp(-jnp.abs(x)))


def _forget_fwd(fb, b_forget, name):
    S = fb.shape[0]
    nb = S // GROUP

    def body(fb_ref, b_ref, cum_ref, carry_ref):
        i = pl.program_id(0)
        logf = _log_sigmoid(fb_ref[...] + b_ref[...])
        r = lax.broadcasted_iota(jnp.int32, (GROUP, GROUP), 0)
        c = lax.broadcasted_iota(jnp.int32, (GROUP, GROUP), 1)
        tri = (c <= r).astype(F32)

        @pl.when(i == 0)
        def _():
            carry_ref[...] = jnp.zeros_like(carry_ref)

        cum = jnp.dot(tri, logf, preferred_element_type=F32, precision=HIGHEST) + carry_ref[0:1, :]
        cum_ref[...] = cum
        carry_ref[...] = jnp.broadcast_to(cum[GROUP - 1:GROUP, :], carry_ref.shape)

    return pl.pallas_call(
        body, name=name, out_shape=jax.ShapeDtypeStruct((S, BLK), F32), grid=(nb,),
        in_specs=[pl.BlockSpec((GROUP, BLK), lambda i: (i, 0)), pl.BlockSpec((1, BLK), lambda i: (0, 0))],
        out_specs=pl.BlockSpec((GROUP, BLK), lambda i: (i, 0)),
        scratch_shapes=[pltpu.VMEM((8, BLK), F32)],
        compiler_params=_params(),
    )(fb, b_forget)


def _forget_bwd(dcum_q, dcum_k, fb, b_forget, name):
    S = fb.shape[0]
    nb = S // GROUP

    def body(dq_ref, dk_ref, fb_ref, b_ref, dfb_ref, db_ref, carry_ref):
        g = pl.program_id(0)
        r = lax.broadcasted_iota(jnp.int32, (GROUP, GROUP), 0)
        c = lax.broadcasted_iota(jnp.int32, (GROUP, GROUP), 1)
        tri = (c >= r).astype(F32)

        @pl.when(g == 0)
        def _():
            carry_ref[...] = jnp.zeros_like(carry_ref)

        dcum = dq_ref[...] + dk_ref[...]
        dlogf = jnp.dot(tri, dcum, preferred_element_type=F32, precision=HIGHEST) + carry_ref[0:1, :]
        carry_ref[...] = jnp.broadcast_to(dlogf[0:1, :], carry_ref.shape)
        x = fb_ref[...] + b_ref[...]
        lane = lax.broadcasted_iota(jnp.int32, (1, BLK), 1)
        dfb = jnp.where(lane < N_FORGET, dlogf * jax.nn.sigmoid(-x), 0.0)
        dfb_ref[...] = dfb
        db = jnp.sum(dfb, axis=0, keepdims=True)

        @pl.when(g == 0)
        def _():
            db_ref[...] = db

        @pl.when(g > 0)
        def _():
            db_ref[...] += db

    rev = pl.BlockSpec((GROUP, BLK), lambda g: (nb - 1 - g, 0))
    row = pl.BlockSpec((1, BLK), lambda g: (0, 0))
    return pl.pallas_call(
        body, name=name,
        out_shape=(jax.ShapeDtypeStruct((S, BLK), F32), jax.ShapeDtypeStruct((1, BLK), F32)), grid=(nb,),
        in_specs=[rev, rev, rev, row], out_specs=(rev, row),
        scratch_shapes=[pltpu.VMEM((8, BLK), F32)],
        compiler_params=_params(),
    )(dcum_q, dcum_k, fb, b_forget)


def _skew(x, sign):
    row = lax.broadcasted_iota(jnp.int32, x.shape, 0)
    for b in range(7):
        amount = (1 << b) if sign > 0 else 256 - (1 << b)
        x = jnp.where(((row >> b) & 1) == 1, pltpu.roll(x, amount, 1), x)
    return x


def _rel_bases(rel):
    far = rel[:, 256:257]
    far127 = jnp.broadcast_to(far, (rel.shape[0], 127))
    base0 = jnp.concatenate([rel[:, 128:0:-1], far, rel[:, 255:128:-1]], axis=1)
    base1 = jnp.concatenate([rel[:, 256:128:-1], far, far127], axis=1)
    base0_t = jnp.concatenate([rel[:, 128:256], far, rel[:, 1:128]], axis=1)
    base1_t = jnp.concatenate([jnp.broadcast_to(far, (rel.shape[0], 128)), far, rel[:, 129:256]], axis=1)
    return jnp.stack([base0, base1, base0_t, base1_t], axis=1)


def _rel_expand(bases, name):
    def body(b_ref, t_ref, tt_ref):
        far = jnp.broadcast_to(b_ref[1:2, 0:1], (BLK, BLK))
        for k, out_ref in ((0, t_ref), (2, tt_ref)):
            for d in (0, 1):
                x = jnp.broadcast_to(b_ref[k + d:k + d + 1, :], (BLK, 2 * BLK))
                out_ref[d] = _skew(x, 1)[:, :BLK]
            for d in (2, 3, 4):
                out_ref[d] = far

    out = jax.ShapeDtypeStruct((8, 5, BLK, BLK), F32)
    spec = pl.BlockSpec((None, 5, BLK, BLK), lambda h: (h, 0, 0, 0))
    return pl.pallas_call(
        body, name=name, out_shape=(out, out), grid=(8,),
        in_specs=[pl.BlockSpec((None, 4, 2 * BLK), lambda h: (h, 0, 0))], out_specs=(spec, spec),
        compiler_params=_params(),
    )(bases)


def _rel_reduce(dtiles_t, name):
    def body(dt_ref, o_ref):
        zeros = jnp.zeros((BLK, BLK), F32)
        sums = []
        for d in (0, 1):
            x = _skew(jnp.concatenate([dt_ref[d], zeros], axis=1), -1)
            sums.append(jnp.broadcast_to(jnp.sum(x, axis=0, keepdims=True), (8, 2 * BLK)))
        lane = lax.broadcasted_iota(jnp.int32, (8, 2 * BLK), 1)
        main = pltpu.roll(sums[0], BLK, 1) + jnp.where(lane > BLK, sums[1], 0.0)
        far = jnp.sum(jnp.where(lane < BLK, sums[1], 0.0)[0:1], axis=1, keepdims=True)
        far = far + jnp.sum(jnp.sum(dt_ref[2] + dt_ref[3] + dt_ref[4], axis=0, keepdims=True), axis=1, keepdims=True)
        o_ref[...] = jnp.concatenate([main[0:1], jnp.broadcast_to(far, (1, BLK))], axis=1)

    return pl.pallas_call(
        body, name=name, out_shape=jax.ShapeDtypeStruct((8, 1, 3 * BLK), F32), grid=(8,),
        in_specs=[pl.BlockSpec((None, 5, BLK, BLK), lambda h: (h, 0, 0, 0))],
        out_specs=pl.BlockSpec((None, 1, 3 * BLK), lambda h: (h, 0, 0)),
        compiler_params=_params(),
    )(dtiles_t)


def _final_loss(x, target, g, below, name):
    S, D = x.shape
    ts = _row_tile(S, ROW_TILE)

    def body(x_ref, t_ref, g_ref, f_ref, gate_ref, dx_ref, loss_ref, dg_ref, dgate_ref, df_ref):
        i = pl.program_id(0)
        xv, gv = x_ref[...], g_ref[...]
        rstd = lax.rsqrt(jnp.mean(xv * xv, axis=-1, keepdims=True) + EPS)
        xhat = xv * rstd
        err = xhat * gv - t_ref[...]
        part = 0.5 * jnp.sum(jnp.mean(err * err, axis=-1, keepdims=True), axis=0, keepdims=True)
        dy = err / D
        dg = jnp.sum(dy * xhat, axis=0, keepdims=True)
        dxhat = dy * gv
        proj = jnp.mean(dxhat * xhat, axis=-1, keepdims=True)
        dx = rstd * (dxhat - xhat * proj)
        dx_ref[...] = dx
        dgate = _gated_residual_bwd(dx, f_ref, gate_ref, df_ref)
        _accumulate_rows(i, [(loss_ref, jnp.broadcast_to(part, loss_ref.shape)), (dg_ref, dg), (dgate_ref, dgate)])

    tile = pl.BlockSpec((ts, D), lambda i: (i, 0))
    row = pl.BlockSpec((1, D), lambda i: (0, 0))
    vec = jax.ShapeDtypeStruct((1, D), F32)
    return pl.pallas_call(
        body, name=name,
        out_shape=(jax.ShapeDtypeStruct((S, D), F32), jax.ShapeDtypeStruct((8, 128), F32), vec, vec,
                   jax.ShapeDtypeStruct((S, D), BF16)),
        grid=(S // ts,), in_specs=[tile, tile, row, tile, row],
        out_specs=(tile, pl.BlockSpec((8, 128), lambda i: (0, 0)), row, row, tile),
        compiler_params=_params(),
    )(x, target, g, *below)


def _ada_fwd(c_all, w_ada, name):
    L, D, E = w_ada.shape

    def body(c_ref, w_ref, o_ref):
        cv = c_ref[...]
        cond = cv * jax.nn.sigmoid(cv)
        o_ref[...] = jnp.dot(cond, w_ref[...], preferred_element_type=F32, precision=HIGHEST)

    return pl.pallas_call(
        body, name=name, out_shape=jax.ShapeDtypeStruct((L, N_DEV, E), F32), grid=(L,),
        in_specs=[pl.BlockSpec((N_DEV, D), lambda l: (0, 0)), pl.BlockSpec((None, D, E), lambda l: (l, 0, 0))],
        out_specs=pl.BlockSpec((None, N_DEV, E), lambda l: (l, 0, 0)),
        compiler_params=_params(),
    )(c_all, w_ada)


def _ada_bwd(c_all_t, dmod, name):
    D = c_all_t.shape[0]
    L, _, E = dmod.shape

    def body(c_ref, d_ref, o_ref):
        cv = c_ref[...]
        cond = cv * jax.nn.sigmoid(cv)
        acc = None
        for b in range(N_DEV):
            t = cond[:, b:b + 1] * d_ref[b:b + 1, :]
            acc = t if acc is None else acc + t
        o_ref[...] = acc

    return pl.pallas_call(
        body, name=name, out_shape=jax.ShapeDtypeStruct((L, D, E), F32), grid=(L,),
        in_specs=[pl.BlockSpec((D, N_DEV), lambda l: (0, 0)), pl.BlockSpec((None, N_DEV, E), lambda l: (l, 0, 0))],
        out_specs=pl.BlockSpec((None, D, E), lambda l: (l, 0, 0)),
        compiler_params=_params(),
    )(c_all_t, dmod)


def _adamw(w, m, v, g_parts, name, me, after=None):
    L, R, C = w.shape
    tr = _row_tile(R, max(8, (256 * 1024 // max(C, 128)) // 8 * 8))
    nr = R // tr
    c1 = 1.0 - ADAM_B1 ** ADAM_STEP
    c2 = 1.0 - ADAM_B2 ** ADAM_STEP
    direct = [isinstance(p, tuple) for p in g_parts]
    n_in = sum(2 if d else 1 for d in direct)

    def body(me_ref, w_ref, m_ref, v_ref, *rest):
        g_refs, (go_ref, d_ref, mo_ref, vo_ref) = list(rest[:n_in]), rest[-4:]
        layer = pl.program_id(0)
        g = None
        for l in range(L):
            land_ref = g_refs.pop(0)
            own = g_refs.pop(0)[...].astype(F32) if direct[l] else None
            gl = None
            for k in range(land_ref.shape[0]):
                part = land_ref[k].astype(F32)
                if direct[l]:
                    part = jnp.where(me_ref[l] == k, own, part)
                gl = part if gl is None else gl + part
            g = gl if g is None else jnp.where(layer == l, gl, g)
        mn = ADAM_B1 * m_ref[...] + (1.0 - ADAM_B1) * g
        vn = ADAM_B2 * v_ref[...] + (1.0 - ADAM_B2) * (g * g)
        m_hat = mn / c1
        v_hat = vn / c2
        go_ref[...] = g
        d_ref[...] = -ADAM_LR * (m_hat / (jnp.sqrt(v_hat) + ADAM_EPS) + ADAM_WD * w_ref[...])
        mo_ref[...] = mn
        vo_ref[...] = vn

    def rows(l, layer, i):
        return jnp.where(layer == l, i, 0 if l > 0 else nr - 1)

    in_specs, operands = [], []
    for l, p in enumerate(g_parts):
        land, sent = p if direct[l] else (p, None)
        in_specs.append(pl.BlockSpec((land.shape[0], tr, C), lambda layer, i, me_ref, l=l: (0, rows(l, layer, i), 0)))
        operands.append(land)
        if direct[l]:
            in_specs.append(pl.BlockSpec((None, tr, C), lambda layer, i, me_ref, l=l: (me_ref[l], rows(l, layer, i), 0)))
            operands.append(sent)
    if after is not None:
        in_specs.append(ANY)
        operands.append(after)
    tile = pl.BlockSpec((None, tr, C), lambda layer, i, me_ref: (layer, i, 0))
    out = jax.ShapeDtypeStruct((L, R, C), F32)
    return pl.pallas_call(
        body, name=name, out_shape=(out, out, out, out),
        grid_spec=pltpu.PrefetchScalarGridSpec(
            num_scalar_prefetch=1, grid=(L, nr), in_specs=[tile, tile, tile] + in_specs,
            out_specs=(tile, tile, tile, tile)),
        compiler_params=_params(),
    )(me, w, m, v, *operands)


def _pair_add(pieces, recv, core, name):
    _, _, R, C = pieces.shape
    tr = _row_tile(R, max(8, (512 * 1024 // max(C, 128)) // 8 * 8))

    def body(core_ref, a_ref, b_ref, o_ref):
        o_ref[...] = (a_ref[...].astype(F32) + b_ref[...].astype(F32)).astype(BF16)

    return pl.pallas_call(
        body, name=name, out_shape=jax.ShapeDtypeStruct((4, R, C), BF16),
        grid_spec=pltpu.PrefetchScalarGridSpec(
            num_scalar_prefetch=1, grid=(4, R // tr),
            in_specs=[pl.BlockSpec((None, None, tr, C), lambda k, i, core_ref: (core_ref[0], k, i, 0)),
                      pl.BlockSpec((None, tr, C), lambda k, i, core_ref: (k, i, 0))],
            out_specs=pl.BlockSpec((None, tr, C), lambda k, i, core_ref: (k, i, 0))),
        compiler_params=_params(),
    )(core, pieces, recv)


MESH = pl.DeviceIdType.MESH
ANY = pl.BlockSpec(memory_space=pl.ANY)


def _position():
    return lax.axis_index("x"), lax.axis_index("y"), lax.axis_index("c")


def _small_all_gather(v, name):
    m_per, n = v.shape

    def body(x_ref, out_ref, send_sems, recv_sems, local_sem):
        x, y, c = _position()
        me, sibling = (x, y, c), (x, y, 1 - c)
        chips = [(1 - x, y), (x, 1 - y), (1 - x, 1 - y)]

        def rows(px, py, pc):
            return out_ref.at[pl.ds((4 * px + 2 * py + pc) * m_per, m_per), :]

        def copy(k, block, to, src=None):
            return pltpu.make_async_remote_copy(
                src_ref=rows(*block) if src is None else src, dst_ref=rows(*block),
                send_sem=send_sems.at[k], recv_sem=recv_sems.at[k], device_id=to, device_id_type=MESH)

        mine = pltpu.make_async_copy(x_ref, rows(*me), local_sem)
        mine.start()
        first = [copy(0, me, sibling, src=x_ref)]
        first += [copy(1 + j, me, (*chip, c), src=x_ref) for j, chip in enumerate(chips)]
        for cp in first:
            cp.start()
        passed = [copy(4 + j, (*chip, c), sibling) for j, chip in enumerate(chips)]
        for j, chip in enumerate(chips):
            copy(1 + j, (*chip, c), me).wait_recv()
            passed[j].start()
        copy(0, sibling, me).wait_recv()
        for j, chip in enumerate(chips):
            copy(4 + j, (*chip, 1 - c), me).wait_recv()
        for cp in first + passed:
            cp.wait_send()
        mine.wait()

    return pl.pallas_call(
        body, name=name, out_shape=jax.ShapeDtypeStruct((N_DEV * m_per, n), v.dtype),
        in_specs=[pl.BlockSpec(memory_space=pltpu.VMEM)], out_specs=pl.BlockSpec(memory_space=pltpu.VMEM),
        scratch_shapes=[pltpu.SemaphoreType.DMA((7,)), pltpu.SemaphoreType.DMA((7,)), pltpu.SemaphoreType.DMA],
    )(v)


def _sibling_exchange(pieces, name):
    n_arr = len(pieces)

    def body(*refs):
        p_refs, out_refs = refs[:n_arr], refs[n_arr:2 * n_arr]
        send_sems, recv_sems = refs[2 * n_arr:]
        x, y, c = _position()
        copies = [pltpu.make_async_remote_copy(
            src_ref=p_refs[a].at[1 - c], dst_ref=out_refs[a], send_sem=send_sems.at[a], recv_sem=recv_sems.at[a],
            device_id=(x, y, 1 - c), device_id_type=MESH) for a in range(n_arr)]
        for cp in copies:
            cp.start()
        for cp in copies:
            cp.wait()

    return pl.pallas_call(
        body, name=name,
        out_shape=tuple(jax.ShapeDtypeStruct(p.shape[1:], p.dtype) for p in pieces),
        in_specs=[ANY] * n_arr, out_specs=tuple([ANY] * n_arr),
        scratch_shapes=[pltpu.SemaphoreType.DMA((n_arr,)), pltpu.SemaphoreType.DMA((n_arr,))],
    )(*pieces)


HBM = pl.BlockSpec(memory_space=pltpu.HBM)
SEM = pl.BlockSpec(memory_space=pltpu.SEMAPHORE)
EFFECT = pltpu.SideEffectType.DATAFLOW_SIDE_EFFECTING
RELATIONS = [(rx, ry, rc) for rx in (0, 1) for ry in (0, 1) for rc in (0, 1)][1:]


SAME_CORE = [r for r in RELATIONS if r == (0, 0, 1) or r[2] == 0]


CHIPS = [r for r in RELATIONS if r[2] == 0]


def _exchange_copies(src_refs, land_refs, send_sems, recv_sems, scatter, receive_side, relations):
    x, y, c = _position()
    index = (lambda px, py, pc: 2 * px + py) if relations == CHIPS else (lambda px, py, pc: 4 * px + 2 * py + pc)
    me = index(x, y, c)
    copies = []
    for k, (rx, ry, rc) in enumerate(relations):
        peer = ((1 - x) if rx else x, (1 - y) if ry else y, (1 - c) if rc else c)
        peer_index = index(*peer)
        for a, (src, land) in enumerate(zip(src_refs, land_refs)):
            copies.append(pltpu.make_async_remote_copy(
                src_ref=src.at[peer_index] if scatter else src,
                dst_ref=land.at[peer_index if receive_side else me],
                send_sem=send_sems.at[a * len(relations) + k], recv_sem=recv_sems.at[a * len(relations) + k],
                device_id=peer, device_id_type=MESH))
    return copies


def _exchange_start(srcs, scatter, after, name, relations=RELATIONS):
    n = len(srcs)
    land_shapes = [(s.shape if scatter else (N_DEV,) + s.shape) for s in srcs]

    def body(*refs):
        src_refs, land_refs = refs[:n], refs[n:2 * n]
        send_sems, recv_sems = refs[2 * n + 1], refs[2 * n + 2]
        token = refs[-1]
        for cp in _exchange_copies(src_refs, land_refs, send_sems, recv_sems, scatter, False, relations):
            cp.start()
        token[...] = jnp.zeros_like(token)

    sems = pltpu.SemaphoreType.DMA((n * len(relations),))
    outs = pl.pallas_call(
        body, name=name,
        out_shape=(sems, sems, *[pltpu.HBM(s.shape, s.dtype) for s in srcs],
                   *[pltpu.HBM(shape, s.dtype) for shape, s in zip(land_shapes, srcs)],
                   jax.ShapeDtypeStruct((8, 128), F32)),
        in_specs=[HBM] * (2 * n) + [ANY],
        out_specs=(SEM, SEM, *[HBM] * (2 * n), pl.BlockSpec(memory_space=pltpu.VMEM)),
        input_output_aliases={a: 2 + a for a in range(2 * n)},
        compiler_params=pltpu.CompilerParams(has_side_effects=EFFECT),
    )(*[pltpu.with_memory_space_constraint(s, pltpu.HBM) for s in srcs],
      *[pltpu.with_memory_space_constraint(lax.empty(shape, s.dtype), pltpu.HBM)
        for shape, s in zip(land_shapes, srcs)], after)
    return outs[0], outs[1], outs[2:2 + n], outs[2 + n:2 + 2 * n], outs[-1]


def _exchange_wait(started, scatter, after, name, relations=RELATIONS):
    send_sems, recv_sems, srcs, lands, _ = started
    n = len(srcs)

    def body(*refs):
        src_refs, land_refs = refs[:n], refs[n:2 * n]
        send_sems, recv_sems = refs[2 * n], refs[2 * n + 1]
        copies = _exchange_copies(src_refs, land_refs, send_sems, recv_sems, scatter, True, relations)
        for cp in copies:
            cp.wait_send()
        for cp in copies:
            cp.wait_recv()

    outs = pl.pallas_call(
        body, name=name,
        out_shape=(*[pltpu.HBM(s.shape, s.dtype) for s in srcs], *[pltpu.HBM(t.shape, t.dtype) for t in lands]),
        in_specs=[HBM] * (2 * n) + [SEM, SEM, ANY], out_specs=tuple([HBM] * (2 * n)),
        input_output_aliases={a: a for a in range(2 * n)},
        compiler_params=pltpu.CompilerParams(has_side_effects=EFFECT),
    )(*srcs, *lands, send_sems, recv_sems, after)
    return outs[:n], outs[n:]


def _forward_copies(land_refs, send_sems, recv_sems, receive_side):
    x, y, c = _position()
    copies = []
    for j, (px, py) in enumerate([(1 - x, y), (x, 1 - y), (1 - x, 1 - y)]):
        held, coming = 4 * px + 2 * py + c, 4 * px + 2 * py + (1 - c)
        for a, land in enumerate(land_refs):
            copies.append(pltpu.make_async_remote_copy(
                src_ref=land.at[held], dst_ref=land.at[coming if receive_side else held],
                send_sem=send_sems.at[3 * a + j], recv_sem=recv_sems.at[3 * a + j],
                device_id=(x, y, 1 - c), device_id_type=MESH))
    return copies


def _forward_start(lands, after, name):
    n = len(lands)

    def body(*refs):
        send_sems, recv_sems, token = refs[n + 1], refs[n + 2], refs[-1]
        for cp in _forward_copies(refs[:n], send_sems, recv_sems, False):
            cp.start()
        token[...] = jnp.zeros_like(token)

    sems = pltpu.SemaphoreType.DMA((3 * n,))
    outs = pl.pallas_call(
        body, name=name,
        out_shape=(sems, sems, *[pltpu.HBM(t.shape, t.dtype) for t in lands], jax.ShapeDtypeStruct((8, 128), F32)),
        in_specs=[HBM] * n + [ANY], out_specs=(SEM, SEM, *[HBM] * n, pl.BlockSpec(memory_space=pltpu.VMEM)),
        input_output_aliases={a: 2 + a for a in range(n)},
        compiler_params=pltpu.CompilerParams(has_side_effects=EFFECT),
    )(*lands, after)
    return outs[0], outs[1], outs[2:2 + n], outs[-1]


def _forward_wait(started, after, name):
    send_sems, recv_sems, lands, _ = started
    n = len(lands)

    def body(*refs):
        copies = _forward_copies(refs[:n], refs[n], refs[n + 1], True)
        for cp in copies:
            cp.wait_send()
        for cp in copies:
            cp.wait_recv()

    return pl.pallas_call(
        body, name=name, out_shape=tuple(pltpu.HBM(t.shape, t.dtype) for t in lands),
        in_specs=[HBM] * n + [SEM, SEM, ANY], out_specs=tuple([HBM] * n),
        input_output_aliases={a: a for a in range(n)},
        compiler_params=pltpu.CompilerParams(has_side_effects=EFFECT),
    )(*lands, send_sems, recv_sems, after)


def _place_own(lands, mine, me, name):
    n = len(lands)
    flat = [m.reshape(-1, m.shape[-1]) for m in mine]
    flat_lands = [t.reshape(N_DEV, -1, t.shape[-1]) for t in lands]

    def body(me_ref, *refs):
        for src, dst in zip(refs[:n], refs[2 * n:]):
            dst[...] = src[...]

    in_specs = [pl.BlockSpec((m.shape[0] // 2, m.shape[1]), lambda i, me_ref: (i, 0)) for m in flat]
    out_specs = [pl.BlockSpec((None, m.shape[0] // 2, m.shape[1]), lambda i, me_ref: (me_ref[0], i, 0)) for m in flat]
    outs = pl.pallas_call(
        body, name=name, out_shape=tuple(jax.ShapeDtypeStruct(t.shape, t.dtype) for t in flat_lands),
        grid_spec=pltpu.PrefetchScalarGridSpec(
            num_scalar_prefetch=1, grid=(2,), in_specs=in_specs + [ANY] * n, out_specs=tuple(out_specs)),
        input_output_aliases={1 + n + a: a for a in range(n)},
        compiler_params=_params(),
    )(me, *flat, *flat_lands)
    return [o.reshape(t.shape) for o, t in zip(outs, lands)]


W_IN_SHARD = N_IN // N_DEV
F_SHARD = F_COL // W_IN_SHARD
F_LO = F_COL - F_SHARD * W_IN_SHARD


def _w_ffn_in_view(w):
    return jnp.transpose(w, (0, 2, 1))


def _w_in_segments():
    segments = []
    for d in range(N_DEV):
        if d == F_SHARD:
            segments += [(d, 0, d * W_IN_SHARD, F_LO), (d, F_LO, N_MAIN, N_FORGET),
                         (d, F_LO + N_FORGET, F_COL, W_IN_SHARD - F_LO - N_FORGET)]
        else:
            segments.append((d, 0, d * W_IN_SHARD - (N_FORGET if d > F_SHARD else 0), W_IN_SHARD))
    return segments


def _w_in_rearranged(g, name):
    D = g.shape[1]
    tr = _row_tile(D, 256)

    def body(g_ref, o_ref):
        for d, lo, at, width in _w_in_segments():
            o_ref[:, at:at + width] = g_ref[d, :, lo:lo + width]
        o_ref[:, N_IN:] = jnp.zeros((tr, BLK - N_FORGET), o_ref.dtype)

    return pl.pallas_call(
        body, name=name, out_shape=jax.ShapeDtypeStruct((D, N_MAIN + BLK), g.dtype), grid=(D // tr,),
        in_specs=[pl.BlockSpec((N_DEV, tr, W_IN_SHARD), lambda i: (0, i, 0))],
        out_specs=pl.BlockSpec((tr, N_MAIN + BLK), lambda i: (i, 0)),
        compiler_params=_params(),
    )(g)


def _w_in_pieces(dw_r, name, pair_major=False):
    D = dw_r.shape[0]
    tr = _row_tile(D, 256)
    lead = (2, 4) if pair_major else (N_DEV,)

    def body(x_ref, o_ref):
        for d, lo, at, width in _w_in_segments():
            slot = (d % 2, d // 2) if pair_major else (d,)
            o_ref[(*slot, slice(None), slice(lo, lo + width))] = x_ref[:, at:at + width]

    return pl.pallas_call(
        body, name=name, out_shape=jax.ShapeDtypeStruct((*lead, D, W_IN_SHARD), dw_r.dtype), grid=(D // tr,),
        in_specs=[pl.BlockSpec((tr, N_MAIN + BLK), lambda i: (i, 0))],
        out_specs=pl.BlockSpec((*lead, tr, W_IN_SHARD), lambda i: (*[0] * len(lead), i, 0)),
        compiler_params=_params(),
    )(dw_r)


def _row_pieces(dw):
    return dw.reshape(N_DEV, dw.shape[0] // N_DEV, dw.shape[1])


def _branch_pieces(dw):
    k, w, d = dw.shape
    return jnp.transpose(dw.reshape(k, w, N_DEV, d // N_DEV), (2, 0, 1, 3)).reshape(N_DEV, k * w, d // N_DEV)


def _pairs_col(a):
    return jnp.transpose(a.reshape(a.shape[0], 4, 2), (1, 0, 2))


def _pairs_row(a):
    return jnp.transpose(a.reshape(a.shape[0], 4, 2), (1, 2, 0))


def _heads_from_col(a):
    return jnp.transpose(a, (1, 0, 2)).reshape(a.shape[1], 8)


def _heads_from_row(a):
    return jnp.transpose(a, (2, 0, 1)).reshape(a.shape[2], 8)


def _pad_lanes(a, n):
    return jnp.pad(a, [(0, 0)] * (a.ndim - 1) + [(0, n - a.shape[-1])])


SMALL_SEGMENTS = (("dmod", 2 * 6 * D_MODEL), ("norm_mix_g", 2 * D_MODEL), ("norm_ffn_g", 2 * D_MODEL),
                  ("final_norm_g", D_MODEL), ("b_forget", 128), ("sinks", 128), ("rel_bias", 4224), ("loss", 128))
SMALL_ROWS = 176


def _pack_small(parts):
    flat = [_pad_lanes(parts[name].reshape(1, -1), size) for name, size in SMALL_SEGMENTS]
    total = sum(size for _, size in SMALL_SEGMENTS)
    flat.append(jnp.zeros((1, SMALL_ROWS * 128 - total), F32))
    return jnp.concatenate(flat, axis=1).reshape(SMALL_ROWS, 128)


def _unpack_small(packed, shapes):
    flat = packed.reshape(-1)
    out, pos = {}, 0
    for name, size in SMALL_SEGMENTS:
        shape = shapes[name]
        count = 1
        for d in shape:
            count *= d
        out[name] = flat[pos:pos + count].reshape(shape)
        pos += size
    return out


def kernel(x, c, norm_mix_g, norm_ffn_g, w_ada, b_ada, w_in, b_forget, sinks, rel_bias, w_branch, w_out, w_ffn_in, w_ffn_out, final_norm_g, loss_target, m_norm_mix_g, m_norm_ffn_g, m_w_ada, m_b_ada, m_w_in, m_b_forget, m_sinks, m_rel_bias, m_w_branch, m_w_out, m_w_ffn_in, m_w_ffn_out, m_final_norm_g, v_norm_mix_g, v_norm_ffn_g, v_w_ada, v_b_ada, v_w_in, v_b_forget, v_sinks, v_rel_bias, v_w_branch, v_w_out, v_w_ffn_in, v_w_ffn_out, v_final_norm_g):
    depth = w_in.shape[0]
    S, D = x.shape[1], x.shape[2]
    assert S % GROUP == 0 and S >= ATTN_WINDOW["c"] * BLK
    px, py, pc = _position()
    me = 4 * px + 2 * py + pc
    x0 = x[0]

    assert depth == 2
    big_weights = (w_in, w_branch, w_out, w_ffn_in, w_ffn_out)
    me_arr = jnp.stack([me, me]).astype(jnp.int32)
    me_in_arr = jnp.stack([2 * px + py, me]).astype(jnp.int32)

    def rest_matrices(g_branch, g_out, g_fin, g_fout):
        return (jnp.transpose(g_branch, (1, 2, 0, 3)).reshape(3, 512, D), g_out.reshape(D, D),
                g_fin.reshape(2 * FFN_HIDDEN, D), g_fout.reshape(FFN_HIDDEN, D))

    def arrive(started, after, name):
        mine, landed = _exchange_wait(started, False, after, f"{name}_wait", SAME_CORE)
        return mine, _forward_start(landed, mine[0], f"{name}_forward_start")

    def finish_gather(arrived, after, name):
        mine, forward = arrived
        landed = _forward_wait(forward, after, f"{name}_forward_wait")
        return _place_own(landed, mine, me.astype(jnp.int32).reshape(1), f"{name}_own")

    w_fin_t = _w_ffn_in_view(w_ffn_in)
    shards = [[t.astype(BF16) for t in (w_in[l], w_branch[l], w_out[l], w_fin_t[l], w_ffn_out[l])]
              for l in range(depth)]
    c_all = _small_all_gather(c.reshape(8, 128), "comm_gather_c").reshape(N_DEV, D)
    mod_cols = _ada_fwd(c_all, w_ada, "ada_fwd")
    mod_all = _small_all_gather(mod_cols.reshape(-1, 128), "comm_gather_mod")
    gather_in0 = _exchange_start(shards[0][:1], False, mod_all, "comm_gather_w_in0_start", SAME_CORE)
    gather_rest0 = _exchange_start(shards[0][1:], False, gather_in0[4], "comm_gather_rest0_start", SAME_CORE)
    gather1 = _exchange_start(shards[1], False, gather_rest0[4], "comm_gather_weights1_start", SAME_CORE)
    started = gather1[4][0:1, 0:1]
    W_in, W_branch, W_out, W_fin, W_fout = ([None, None] for _ in range(5))
    mod_all = mod_all.reshape(N_DEV, depth, N_DEV, w_ada.shape[2])
    mod_mine = lax.dynamic_index_in_dim(mod_all, me, axis=2, keepdims=False)
    mod = jnp.transpose(mod_mine, (1, 0, 2)).reshape(depth, 6 * D) + b_ada + started
    mods = [[mod[l:l + 1, k * D:(k + 1) * D] for k in range(6)] for l in range(depth)]
    rel_tiles = [_rel_expand(_rel_bases(rel_bias[l]) + started, f"rel_expand{l}") for l in range(depth)]

    slopes = jnp.exp2(-jnp.arange(1, 9, dtype=F32))
    saved = []
    xs = x0
    for l in range(depth):
        if l == 1:
            g_in1, *g_rest1 = finish_gather(arrived1, xs, "comm_gather_weights1")
            W_in[1] = _w_in_rearranged(g_in1, "w_in_rearrange1")
            W_branch[1], W_out[1], W_fin[1], W_fout[1] = rest_matrices(*g_rest1)
        sh_m, sc_m, g_m, sh_f, sc_f, g_f = mods[l]
        gm, gf = norm_mix_g[l:l + 1], norm_ffn_g[l:l + 1]
        bfor = _pad_lanes(b_forget[l:l + 1], BLK)
        h = _norm_mod_fwd(xs, gm, sh_m, sc_m, f"norm_mix_fwd{l}")
        tiles, tiles_t = rel_tiles[l]
        if l == 0:
            arrived_in0 = arrive(gather_in0, rel_tiles[-1][1], "comm_gather_w_in0")
            W_in[0] = _w_in_rearranged(finish_gather(arrived_in0, h, "comm_gather_w_in0")[0], "w_in_rearrange0")
        qkv, gates = _project(h, W_in[l], f"proj{l}")
        fb = _matmul(h, W_in[l], "nn", F32, f"proj_forget{l}", TILES["proj_forget"], n=BLK, b_off=N_MAIN // BLK)
        cum = _forget_fwd(fb, bfor, f"forget_fwd{l}")[:, :N_FORGET]
        cum_col, cum_row = _pairs_col(cum), _pairs_row(cum)
        o_a, lse_a = _attn_fwd("a", qkv, f"attn_a_fwd{l}", sinks=sinks[l], slopes=slopes)
        o_b, lse_b = _attn_fwd("b", qkv, f"attn_b_fwd{l}", cq_col=cum_col, ck_row=cum_row)
        arrived_rest0 = arrive(gather_rest0, o_b, "comm_gather_rest0") if l == 0 else None
        o_c, lse_c = _attn_fwd("c", qkv, f"attn_c_fwd{l}", bias=tiles, after=arrived_rest0[1][3] if l == 0 else None)
        if l == 0:
            W_branch[0], W_out[0], W_fin[0], W_fout[0] = rest_matrices(
                *finish_gather(arrived_rest0, o_c, "comm_gather_rest0"))
        x1, merged, mix, h2 = _merge_fwd(o_a, o_b, o_c, gates, W_branch[l], W_out[l], xs, g_m, gf, sh_f, sc_f,
                                         f"merge_fwd{l}")
        act = _ffn_in_fwd(h2, W_fin[l], f"ffn_in_fwd{l}")
        if l == 0:
            arrived1 = arrive(gather1, act, "comm_gather_weights1")
        x2, ffn = _matmul_resid(act, W_fout[l], x1, g_f, f"ffn_out{l}", TILES["ffn_out"],
                                after=arrived1[1][3] if l == 0 else None)
        saved.append(dict(x=xs, h=h, qkv=qkv, gates=gates, fb=fb, bfor=bfor, cum_col=cum_col, cum_row=cum_row,
                          tiles_t=tiles_t, o=(o_a, o_b, o_c), lse=(lse_a, lse_b, lse_c), merged=merged, mix=mix,
                          x1=x1, h2=h2, act=act, ffn=ffn))
        xs = x2

    dx, loss_tile, d_final_g, d_g_f, df = _final_loss(
        xs, loss_target[0], final_norm_g.reshape(1, D), (saved[-1]["ffn"], mods[-1][5]), "final_loss")

    grads = {k: [None] * depth for k in ("w_in", "w_branch", "w_out", "w_ffn_in", "w_ffn_out", "norm_mix_g",
                                          "norm_ffn_g", "b_forget", "sinks", "rel_bias", "dmod")}
    def rest_pieces(l):
        return [_branch_pieces(grads["w_branch"][l]), _row_pieces(grads["w_out"][l]),
                _row_pieces(grads["w_ffn_in"][l]), _row_pieces(grads["w_ffn_out"][l])]

    reduce1 = reduce_rest0 = reduce_in0 = None
    for l in reversed(range(depth)):
        sv = saved[l]
        sh_m, sc_m, g_m, sh_f, sc_f, g_f = mods[l]
        gm, gf = norm_mix_g[l:l + 1], norm_ffn_g[l:l + 1]
        du_g, du_u = _ffn_mid_bwd(sv["h2"], df, W_fin[l], W_fout[l], f"ffn_mid_bwd{l}")
        du = jnp.concatenate([du_g, du_u], axis=1)
        grads["w_ffn_out"][l] = _matmul(sv["act"], df, "tn", BF16, f"wgrad_ffn_out{l}", TILES["wgrad_ffn_out"])
        grads["w_ffn_in"][l] = _matmul(du, sv["h2"], "tn", BF16, f"wgrad_ffn_in{l}", TILES["wgrad_ffn_in"])
        dh2 = _matmul(du, W_fin[l], "nn", F32, f"dgrad_ffn_in{l}", TILES["dgrad_ffn_in"])
        dx1, d_sh_f, d_sc_f, d_gf, d_g_m, dmix = _norm_mod_bwd(sv["x1"], dh2, dx, gf, sc_f, f"norm_ffn_bwd{l}",
                                                               below=(sv["mix"], g_m))
        grads["w_out"][l] = _matmul(sv["merged"], dmix, "tn", BF16, f"wgrad_out{l}", TILES["wgrad_out"])
        o_a, o_b, o_c = sv["o"]
        dgates, d_w_branch, do_a, do_b, do_c, dl_a, dl_b, dl_c = _merge_bwd(
            dmix, o_a, o_b, o_c, sv["gates"], W_branch[l], W_out[l], f"merge_bwd{l}")
        grads["w_branch"][l] = d_w_branch.astype(BF16)
        lse_rows = list(sv["lse"])
        if l == 0:
            reduce_rest0 = _exchange_start(rest_pieces(0), True, dgates, "comm_reduce_rest0_start")
            lse_rows = [t + reduce_rest0[4][0:1, 0:1] for t in lse_rows]
        dq_a, dk_a, dv_a, dsink = _attn_bwd("a", sv["qkv"], do_a, lse_rows[0], dl_a.reshape(4, 2, S), f"attn_a_bwd{l}",
                                            sinks=sinks[l], slopes=slopes)
        dq_b, dk_b, dv_b, dck, dcq = _attn_bwd("b", sv["qkv"], do_b, lse_rows[1], dl_b.reshape(4, 2, S),
                                               f"attn_b_bwd{l}", cq_row=sv["cum_row"], ck_col=sv["cum_col"])
        dq_c, dk_c, dv_c, dtiles_t = _attn_bwd("c", sv["qkv"], do_c, lse_rows[2], dl_c.reshape(4, 2, S),
                                               f"attn_c_bwd{l}", bias_t=sv["tiles_t"])
        grads["sinks"][l] = dsink[:, :2, 0].reshape(8)
        grads["rel_bias"][l] = _rel_reduce(dtiles_t, f"rel_reduce{l}")[:, 0, :N_REL]
        dcum_k = _pad_lanes(_heads_from_col(dck), BLK)
        dcum_q = _pad_lanes(_heads_from_row(dcq), BLK)
        dfb, d_bfor = _forget_bwd(dcum_q, dcum_k, sv["fb"], sv["bfor"], f"forget_bwd{l}")
        grads["b_forget"][l] = d_bfor[0, :N_FORGET]
        dproj = jnp.concatenate(
            [t.astype(BF16) for t in (dq_a, dk_a, dv_a, dq_b, dk_b, dv_b, dq_c, dk_c, dv_c, dgates, dfb)],
            axis=1)
        grads["w_in"][l] = _matmul(sv["h"], dproj, "tn", BF16, f"wgrad_in{l}", TILES["wgrad_in"])
        if l == 1:
            reduce1 = _exchange_start([_w_in_pieces(grads["w_in"][1], "w_in_pieces1")] + rest_pieces(1), True, dproj,
                                      "comm_reduce1_start")
        dh = _matmul(dproj, W_in[l], "nt", F32, f"dgrad_in{l}", TILES["dgrad_in"], after=reduce1[4] if l == 1 else None)
        d_g_f_here = d_g_f
        if l > 0:
            dx, d_sh_m, d_sc_m, d_gm, d_g_f, df = _norm_mod_bwd(sv["x"], dh, dx1, gm, sc_m, f"norm_mix_bwd{l}",
                                                                below=(saved[l - 1]["ffn"], mods[l - 1][5]))
        else:
            dx, d_sh_m, d_sc_m, d_gm = _norm_mod_bwd(sv["x"], dh, dx1, gm, sc_m, f"norm_mix_bwd{l}")
        grads["norm_mix_g"][l] = d_gm[0]
        grads["norm_ffn_g"][l] = d_gf[0]
        grads["dmod"][l] = jnp.concatenate([d_sh_m, d_sc_m, d_g_m, d_sh_f, d_sc_f, d_g_f_here], axis=1)[0]

    grad_x = dx.reshape(x.shape)

    small_shapes = dict(dmod=b_ada.shape, norm_mix_g=norm_mix_g.shape, norm_ffn_g=norm_ffn_g.shape,
                        final_norm_g=final_norm_g.shape, b_forget=b_forget.shape, sinks=sinks.shape,
                        rel_bias=rel_bias.shape, loss=())
    mine_small = _pack_small(dict(
        loss=_pad_lanes(loss_tile[0:1, 0:1], 128),
        dmod=jnp.stack(grads["dmod"]), norm_mix_g=jnp.stack(grads["norm_mix_g"]),
        norm_ffn_g=jnp.stack(grads["norm_ffn_g"]), final_norm_g=d_final_g[0],
        b_forget=_pad_lanes(jnp.stack(grads["b_forget"]).reshape(1, -1), 128),
        sinks=_pad_lanes(jnp.stack(grads["sinks"]).reshape(1, -1), 128),
        rel_bias=_pad_lanes(jnp.stack(grads["rel_bias"]).reshape(1, -1), 4224)))
    all_small = _small_all_gather(mine_small, "comm_gather_small").reshape(N_DEV, SMALL_ROWS, 128)
    pieces_in0 = _w_in_pieces(grads["w_in"][0], "w_in_pieces0", pair_major=True)
    from_sibling = _sibling_exchange([pieces_in0], "comm_reduce_in0_sibling")[0]
    pair_sum_in0 = _pair_add(pieces_in0, from_sibling, pc.astype(jnp.int32).reshape(1), "pair_add_in0")
    reduce_in0 = _exchange_start([pair_sum_in0], True, all_small, "comm_reduce_in0_start", CHIPS)
    in0_started = reduce_in0[4]

    def pack_params(b_ada_, nm, nf, fn, bf, sk, rb):
        return _pack_small(dict(dmod=b_ada_, norm_mix_g=nm, norm_ffn_g=nf, final_norm_g=fn, loss=jnp.zeros((1, 128), F32),
                                b_forget=_pad_lanes(bf.reshape(1, -1), 128), sinks=_pad_lanes(sk.reshape(1, -1), 128),
                                rel_bias=_pad_lanes(rb.reshape(1, -1), 4224)))

    small_out = _adamw(
        pack_params(b_ada, norm_mix_g, norm_ffn_g, final_norm_g, b_forget, sinks, rel_bias)[None],
        pack_params(m_b_ada, m_norm_mix_g, m_norm_ffn_g, m_final_norm_g, m_b_forget, m_sinks, m_rel_bias)[None],
        pack_params(v_b_ada, v_norm_mix_g, v_norm_ffn_g, v_final_norm_g, v_b_forget, v_sinks, v_rel_bias)[None],
        [all_small], "adamw_small", me_arr, after=in0_started)
    small_out = [_unpack_small(t[0], small_shapes) for t in small_out]

    dmod_all = all_small[:, :96].reshape(N_DEV, depth, 6 * D)
    dmod_cols = lax.dynamic_slice_in_dim(dmod_all, me * w_ada.shape[2], w_ada.shape[2], axis=2)
    d_w_ada = _ada_bwd(jnp.transpose(c_all), jnp.transpose(dmod_cols, (1, 0, 2)), "ada_bwd")

    big = {"w_ada": _adamw(w_ada, m_w_ada, v_w_ada, [d_w_ada[l:l + 1] for l in range(depth)], "adamw_w_ada", me_arr,
                           after=in0_started)}
    sent1, landed1 = _exchange_wait(reduce1, True, big["w_ada"][0], "comm_reduce1_wait")
    sent_rest0, landed_rest0 = _exchange_wait(reduce_rest0, True, landed1[0], "comm_reduce_rest0_wait")
    parts = {"w_in": [None, (landed1[0], sent1[0])]}
    for a, name in enumerate(("w_branch", "w_out", "w_ffn_in", "w_ffn_out")):
        parts[name] = [(landed_rest0[a], sent_rest0[a]), (landed1[1 + a], sent1[1 + a])]

    def update(name, w, m, v, view=lambda t: t):
        per_layer = lambda t: t.reshape(depth, -1, t.shape[-1])
        outs = _adamw(*[per_layer(view(t)) for t in (w, m, v)], parts[name], f"adamw_{name}",
                      me_in_arr if name == "w_in" else me_arr)
        big[name] = [view(t).reshape(w.shape) for t in outs]

    update("w_ffn_in", w_ffn_in, m_w_ffn_in, v_w_ffn_in, _w_ffn_in_view)
    update("w_ffn_out", w_ffn_out, m_w_ffn_out, v_w_ffn_out)
    update("w_branch", w_branch, m_w_branch, v_w_branch)
    update("w_out", w_out, m_w_out, v_w_out)
    sent_in0, landed_in0 = _exchange_wait(reduce_in0, True, big["w_out"][0], "comm_reduce_in0_wait", CHIPS)
    parts["w_in"][0] = (landed_in0[0], sent_in0[0])
    update("w_in", w_in, m_w_in, v_w_in)

    def leaf(kind, name):
        if name in big:
            return big[name][kind]
        return small_out[kind]["dmod" if name == "b_ada" else name]

    order = ["norm_mix_g", "norm_ffn_g", "w_ada", "b_ada", "w_in", "b_forget", "sinks", "rel_bias", "w_branch",
             "w_out", "w_ffn_in", "w_ffn_out", "final_norm_g"]
    loss = small_out[0]["loss"]
    return (loss, grad_x, *[leaf(0, n) for n in order], *[leaf(1, n) for n in order],
            *[leaf(2, n) for n in order], *[leaf(3, n) for n in order])
```
